```python
import math
import jax, jax.numpy as jnp
from jax import lax
import numpy as np

D_MODEL = 1024
BATCH = 8
SEQ = 8192
DEPTH = 1

GRID_W = 64
CTX_LEN = 256
S5_WIDTH = 512
S5_GROUP = 16
S5_GROUPS = S5_WIDTH // S5_GROUP
S5_STATE = 64
DT_MIN = 1e-3
DT_MAX = 1e-1
RET_WIDTH = D_MODEL - S5_WIDTH
RET_HEADS = 4
RET_HEAD_DIM = RET_WIDTH // RET_HEADS
RET_CHUNK = 128
ROPE_THETA = 10000.0
IN_COLS = S5_WIDTH + 4 * RET_WIDTH
D_FF = 2816
CONV_W = 3
NORM_EPS = 1e-6

kernel_name = "hybrid_s5_retention_convglu_dit_layer"


def rms_norm(t, w):
    tf = t.astype(jnp.float32)
    y = tf * lax.rsqrt(jnp.mean(tf * tf, axis=-1, keepdims=True) + NORM_EPS)
    return (y * w.astype(jnp.float32)).astype(t.dtype)


def adaln(cond, w_mod, b_mod):
    m = jax.nn.silu(cond) @ w_mod + b_mod
    return [t.reshape(-1, 1, D_MODEL) for t in jnp.split(m, 6, axis=-1)]


def modulate(h, shift, scale):
    return h * (1.0 + scale) + shift


def rope_2d(t):
    n_tok = t.shape[1]
    rows = n_tok // GRID_W
    row = jnp.repeat(jnp.arange(rows, dtype=jnp.float32), GRID_W)
    col = jnp.tile(jnp.arange(GRID_W, dtype=jnp.float32), rows)
    n_freq = RET_HEAD_DIM // 4
    inv_freq = ROPE_THETA ** (-jnp.arange(n_freq, dtype=jnp.float32) / n_freq)
    ang = jnp.concatenate([row[:, None] * inv_freq, col[:, None] * inv_freq], axis=-1)
    cos = jnp.cos(ang)[None, :, None, :]
    sin = jnp.sin(ang)[None, :, None, :]
    tf = t.astype(jnp.float32)
    t1, t2 = tf[..., 0::2], tf[..., 1::2]
    out = jnp.stack([t1 * cos - t2 * sin, t1 * sin + t2 * cos], axis=-1).reshape(t.shape)
    return out.astype(t.dtype)


def split_projection(p, rotate):
    b, n_tok, _ = p.shape
    u = p[..., :S5_WIDTH].reshape(b, n_tok, S5_GROUPS, S5_GROUP)
    q, k, v, g = jnp.split(p[..., S5_WIDTH:], 4, axis=-1)
    q = q.reshape(b, n_tok, RET_HEADS, RET_HEAD_DIM)
    k = k.reshape(b, n_tok, RET_HEADS, RET_HEAD_DIM) * (RET_HEAD_DIM ** -0.5)
    v = v.reshape(b, n_tok, RET_HEADS, RET_HEAD_DIM)
    if rotate:
        q = rope_2d(q)
        k = rope_2d(k)
    return (u, q.transpose(0, 2, 1, 3), k.transpose(0, 2, 1, 3), v.transpose(0, 2, 1, 3), g)


def s5_discretize(lam_re, lam_im, log_step, b_re, b_im):
    lam = lax.complex(lam_re.astype(jnp.float32), lam_im.astype(jnp.float32))
    step = jnp.exp(log_step.astype(jnp.float32))[:, None]
    lam_bar = jnp.exp(lam * step)
    b_mat = lax.complex(b_re.astype(jnp.float32), b_im.astype(jnp.float32))
    b_bar = ((lam_bar - 1.0) / lam)[..., None] * b_mat
    return lam_bar, b_bar


def _linear_combine(left, right):
    a_l, b_l = left
    a_r, b_r = right
    return a_r * a_l, a_r * b_l + b_r


def s5_scan(u, lam_bar, b_bar, h0, reverse):
    bu = jnp.einsum('gnp,blgp->blgn', b_bar, u.astype(jnp.float32).astype(jnp.complex64))
    if reverse:
        bu = jnp.flip(bu, axis=1)
    bu = bu.at[:, 0].add(lam_bar * h0)
    a = jnp.broadcast_to(lam_bar, bu.shape)
    _, h = lax.associative_scan(_linear_combine, (a, bu), axis=1)
    if reverse:
        h = jnp.flip(h, axis=1)
    return h


def s5_readout(u, h_f, h_b, c_re, c_im, d, w_glu, b_glu):
    b, n_tok = u.shape[0], u.shape[1]
    c_mat = lax.complex(c_re.astype(jnp.float32), c_im.astype(jnp.float32))
    y = jnp.real(jnp.einsum('gpn,blgn->blgp', c_mat, h_f + h_b))
    y = y + d.astype(jnp.float32).reshape(S5_GROUPS, S5_GROUP) * u.astype(jnp.float32)
    y = jax.nn.gelu(y.reshape(b, n_tok, S5_WIDTH))
    return y * jax.nn.sigmoid(y @ w_glu.astype(jnp.float32) + b_glu.astype(jnp.float32))


def retention_chunkwise(q, k, v, log_decay, r0, strict):
    b, h, n_tok, dk = q.shape
    dv = v.shape[-1]
    n_chunks = n_tok // RET_CHUNK
    ld = log_decay.astype(jnp.float32)
    qc = q.astype(jnp.float32).reshape(b, h, n_chunks, RET_CHUNK, dk)
    kc = k.astype(jnp.float32).reshape(b, h, n_chunks, RET_CHUNK, dk)
    vc = v.astype(jnp.float32).reshape(b, h, n_chunks, RET_CHUNK, dv)
    pos = jnp.arange(RET_CHUNK, dtype=jnp.float32)
    diff = pos[:, None] - pos[None, :]
    keep = diff > 0 if strict else diff >= 0
    intra_decay = jnp.where(keep, jnp.exp(ld[:, None, None] * jnp.maximum(diff, 0.0)), 0.0)
    scores = jnp.einsum('bhncd,bhnmd->bhncm', qc, kc) * intra_decay[None, :, None]
    intra = jnp.einsum('bhncm,bhnme->bhnce', scores, vc)
    zeta = jnp.exp(ld[:, None] * (RET_CHUNK - 1.0 - pos))
    chunk_kv = jnp.einsum('bhnmd,bhnme->nbhde', kc * zeta[None, :, None, :, None], vc)
    chunk_decay = jnp.exp(ld * RET_CHUNK)[None, :, None, None]

    def step(state, kv):
        return chunk_decay * state + kv, state

    _, r_prev = lax.scan(step, r0.astype(jnp.float32), chunk_kv)
    xi = jnp.exp(ld[:, None] * (pos + 1.0))
    cross = jnp.einsum('bhncd,nbhde->bhnce', qc * xi[None, :, None, :, None], r_prev)
    return (intra + cross).reshape(b, h, n_tok, dv)


def retention_final_state(k, v, log_decay):
    n_tok = k.shape[2]
    ld = log_decay.astype(jnp.float32)
    w = jnp.exp(ld[:, None] * (n_tok - 1.0 - jnp.arange(n_tok, dtype=jnp.float32)))
    return jnp.einsum('bhld,bhle->bhde', k.astype(jnp.float32) * w[None, :, :, None], v.astype(jnp.float32))


def retention_mixer(q, k, v, g, ld_f, ld_b, r0_f, r0_b):
    out_f = retention_chunkwise(q, k, v, ld_f, r0_f, strict=False)
    out_b = jnp.flip(retention_chunkwise(jnp.flip(q, 2), jnp.flip(k, 2), jnp.flip(v, 2), ld_b, r0_b, strict=True), 2)
    y = out_f + out_b
    mu = jnp.mean(y, axis=-1, keepdims=True)
    var = jnp.mean((y - mu) ** 2, axis=-1, keepdims=True)
    y = (y - mu) * lax.rsqrt(var + NORM_EPS)
    b, h, n_tok, dv = y.shape
    y = y.transpose(0, 2, 1, 3).reshape(b, n_tok, h * dv)
    return jax.nn.silu(g.astype(jnp.float32)) * y


def conv_ffn(h, w_up, conv_w, conv_b, w_down):
    a, g = jnp.split(h @ w_up, 2, axis=-1)
    n_tok = g.shape[1]
    half = CONV_W // 2
    gp = jnp.pad(g, ((0, 0), (half, half), (0, 0)))
    g_conv = conv_b + gp[:, 0:n_tok] * conv_w[0]
    for j in range(1, CONV_W):
        g_conv = g_conv + gp[:, j:j + n_tok] * conv_w[j]
    return (jax.nn.gelu(g_conv) * a) @ w_down


def _fwd_setup_inputs(seed: int = 0) -> dict:
    key = jax.random.key(seed)
    ks = jax.random.split(key, 32)
    f32 = jnp.float32

    def nrm(k, shape, s):
        return s * jax.random.normal(k, shape, f32)

    gshape = (DEPTH, S5_GROUPS, S5_STATE)
    lam_re = -0.5 * jnp.ones(gshape, f32)
    lam_im = math.pi * jnp.broadcast_to(jnp.arange(S5_STATE, dtype=f32), gshape)

    def log_dt(k):
        return math.log(DT_MIN) + jax.random.uniform(k, (DEPTH, S5_GROUPS), f32) * (math.log(DT_MAX) - math.log(DT_MIN))

    base_decay = jnp.log(1.0 - 2.0 ** (-5.0 - jnp.arange(RET_HEADS, dtype=f32)))
    return {
        "x": nrm(ks[0], (BATCH, SEQ, D_MODEL), 1.0),
        "c": nrm(ks[1], (BATCH, D_MODEL), 1.0),
        "ctx": nrm(ks[2], (BATCH, CTX_LEN, D_MODEL), 1.0),
        "c_ctx": nrm(ks[3], (D_MODEL,), 1.0),
        "w_mod": nrm(ks[4], (DEPTH, D_MODEL, 6 * D_MODEL), 0.5 * D_MODEL ** -0.5),
        "b_mod": nrm(ks[5], (DEPTH, 6 * D_MODEL), 0.01),
        "norm1_w": 1.0 + nrm(ks[6], (DEPTH, D_MODEL), 0.02),
        "w_in": nrm(ks[7], (DEPTH, D_MODEL, IN_COLS), D_MODEL ** -0.5),
        "s5_lambda_re_f": lam_re + nrm(ks[8], gshape, 0.01),
        "s5_lambda_im_f": lam_im + nrm(ks[9], gshape, 0.01),
        "s5_log_step_f": log_dt(ks[10]),
        "s5_lambda_re_b": lam_re + nrm(ks[11], gshape, 0.01),
        "s5_lambda_im_b": lam_im + nrm(ks[12], gshape, 0.01),
        "s5_log_step_b": log_dt(ks[13]),
        "s5_b_re": nrm(ks[14], (DEPTH, S5_GROUPS, S5_STATE, S5_GROUP), (2.0 * S5_GROUP) ** -0.5),
        "s5_b_im": nrm(ks[15], (DEPTH, S5_GROUPS, S5_STATE, S5_GROUP), (2.0 * S5_GROUP) ** -0.5),
        "s5_c_re": nrm(ks[16], (DEPTH, S5_GROUPS, S5_GROUP, S5_STATE), 0.5),
        "s5_c_im": nrm(ks[17], (DEPTH, S5_GROUPS, S5_GROUP, S5_STATE), 0.5),
        "s5_d": nrm(ks[18], (DEPTH, S5_WIDTH), 0.5),
        "s5_w_glu": nrm(ks[19], (DEPTH, S5_WIDTH, S5_WIDTH), S5_WIDTH ** -0.5),
        "s5_b_glu": nrm(ks[20], (DEPTH, S5_WIDTH), 0.01),
        "ret_log_decay_f": base_decay * jnp.exp(nrm(ks[21], (DEPTH, RET_HEADS), 0.05)),
        "ret_log_decay_b": base_decay * jnp.exp(nrm(ks[22], (DEPTH, RET_HEADS), 0.05)),
        "w_out": nrm(ks[23], (DEPTH, D_MODEL, D_MODEL), D_MODEL ** -0.5),
        "norm2_w": 1.0 + nrm(ks[24], (DEPTH, D_MODEL), 0.02),
        "w_up": nrm(ks[25], (DEPTH, D_MODEL, 2 * D_FF), D_MODEL ** -0.5),
        "conv_w": nrm(ks[26], (DEPTH, CONV_W, D_FF), CONV_W ** -0.5),
        "conv_b": nrm(ks[27], (DEPTH, D_FF), 0.01),
        "w_down": nrm(ks[28], (DEPTH, D_FF, D_MODEL), D_FF ** -0.5),
        "final_norm_w": 1.0 + nrm(ks[29], (D_MODEL,), 0.02),
    }


def _fwd_reference(x, c, ctx, c_ctx, w_mod, b_mod, norm1_w, w_in,
              s5_lambda_re_f, s5_lambda_im_f, s5_log_step_f,
              s5_lambda_re_b, s5_lambda_im_b, s5_log_step_b,
              s5_b_re, s5_b_im, s5_c_re, s5_c_im, s5_d, s5_w_glu, s5_b_glu,
              ret_log_decay_f, ret_log_decay_b, w_out,
              norm2_w, w_up, conv_w, conv_b, w_down, final_norm_w):
    batch = x.shape[0]
    zero_s5 = jnp.zeros((batch, S5_GROUPS, S5_STATE), jnp.complex64)
    zero_ret = jnp.zeros((batch, RET_HEADS, RET_HEAD_DIM, RET_HEAD_DIM), jnp.float32)
    for layer in range(DEPTH):
        mx = adaln(c, w_mod[layer], b_mod[layer])
        mc = adaln(c_ctx, w_mod[layer], b_mod[layer])
        hx = modulate(rms_norm(x, norm1_w[layer]), mx[0], mx[1])
        hc = modulate(rms_norm(ctx, norm1_w[layer]), mc[0], mc[1])
        ux, qx, kx, vx, gx = split_projection(hx @ w_in[layer], rotate=True)
        uc, qc, kc, vc, gc = split_projection(hc @ w_in[layer], rotate=False)
        lam_f, bbar_f = s5_discretize(s5_lambda_re_f[layer], s5_lambda_im_f[layer], s5_log_step_f[layer], s5_b_re[layer], s5_b_im[layer])
        lam_b, bbar_b = s5_discretize(s5_lambda_re_b[layer], s5_lambda_im_b[layer], s5_log_step_b[layer], s5_b_re[layer], s5_b_im[layer])
        hc_f = s5_scan(uc, lam_f, bbar_f, zero_s5, reverse=False)
        hc_b = s5_scan(uc, lam_b, bbar_b, zero_s5, reverse=True)
        rc_f = retention_final_state(kc, vc, ret_log_decay_f[layer])
        rc_b = retention_final_state(jnp.flip(kc, 2), jnp.flip(vc, 2), ret_log_decay_b[layer])
        hx_f = s5_scan(ux, lam_f, bbar_f, hc_f[:, -1], reverse=False)
        hx_b = s5_scan(ux, lam_b, bbar_b, hc_b[:, 0], reverse=True)
        s5_x = s5_readout(ux, hx_f, hx_b, s5_c_re[layer], s5_c_im[layer], s5_d[layer], s5_w_glu[layer], s5_b_glu[layer])
        ret_x = retention_mixer(qx, kx, vx, gx, ret_log_decay_f[layer], ret_log_decay_b[layer], rc_f, rc_b)
        mix_x = jnp.concatenate([s5_x, ret_x], axis=-1).astype(x.dtype) @ w_out[layer]
        x = x + mx[2] * mix_x
        hx2 = modulate(rms_norm(x, norm2_w[layer]), mx[3], mx[4])
        x = x + mx[5] * conv_ffn(hx2, w_up[layer], conv_w[layer], conv_b[layer], w_down[layer])
        if layer < DEPTH - 1:
            s5_c = s5_readout(uc, hc_f, hc_b, s5_c_re[layer], s5_c_im[layer], s5_d[layer], s5_w_glu[layer], s5_b_glu[layer])
            ret_c = retention_mixer(qc, kc, vc, gc, ret_log_decay_f[layer], ret_log_decay_b[layer], zero_ret, zero_ret)
            ctx = ctx + mc[2] * (jnp.concatenate([s5_c, ret_c], axis=-1).astype(ctx.dtype) @ w_out[layer])
            hc2 = modulate(rms_norm(ctx, norm2_w[layer]), mc[3], mc[4])
            ctx = ctx + mc[5] * conv_ffn(hc2, w_up[layer], conv_w[layer], conv_b[layer], w_down[layer])
    return rms_norm(x, final_norm_w)


import jax as _jax
import jax.numpy as _jnp

TWIN_FORMAT = 'train_step'
FWD_PARAMS = ['x', 'c', 'ctx', 'c_ctx', 'w_mod', 'b_mod', 'norm1_w', 'w_in', 's5_lambda_re_f', 's5_lambda_im_f', 's5_log_step_f', 's5_lambda_re_b', 's5_lambda_im_b', 's5_log_step_b', 's5_b_re', 's5_b_im', 's5_c_re', 's5_c_im', 's5_d', 's5_w_glu', 's5_b_glu', 'ret_log_decay_f', 'ret_log_decay_b', 'w_out', 'norm2_w', 'w_up', 'conv_w', 'conv_b', 'w_down', 'final_norm_w']
TWIN_WEIGHTS = ['c_ctx', 'w_mod', 'b_mod', 'norm1_w', 'w_in', 's5_lambda_re_f', 's5_lambda_im_f', 's5_log_step_f', 's5_lambda_re_b', 's5_lambda_im_b', 's5_log_step_b', 's5_b_re', 's5_b_im', 's5_c_re', 's5_c_im', 's5_d', 's5_w_glu', 's5_b_glu', 'ret_log_decay_f', 'ret_log_decay_b', 'w_out', 'norm2_w', 'w_up', 'conv_w', 'conv_b', 'w_down', 'final_norm_w']
TWIN_DIFF_INPUT = 'x'
TWIN_INPUTS = ['x', 'c', 'ctx', 'c_ctx', 'w_mod', 'b_mod', 'norm1_w', 'w_in', 's5_lambda_re_f', 's5_lambda_im_f', 's5_log_step_f', 's5_lambda_re_b', 's5_lambda_im_b', 's5_log_step_b', 's5_b_re', 's5_b_im', 's5_c_re', 's5_c_im', 's5_d', 's5_w_glu', 's5_b_glu', 'ret_log_decay_f', 'ret_log_decay_b', 'w_out', 'norm2_w', 'w_up', 'conv_w', 'conv_b', 'w_down', 'final_norm_w', 'loss_target', 'm_c_ctx', 'm_w_mod', 'm_b_mod', 'm_norm1_w', 'm_w_in', 'm_s5_lambda_re_f', 'm_s5_lambda_im_f', 'm_s5_log_step_f', 'm_s5_lambda_re_b', 'm_s5_lambda_im_b', 'm_s5_log_step_b', 'm_s5_b_re', 'm_s5_b_im', 'm_s5_c_re', 'm_s5_c_im', 'm_s5_d', 'm_s5_w_glu', 'm_s5_b_glu', 'm_ret_log_decay_f', 'm_ret_log_decay_b', 'm_w_out', 'm_norm2_w', 'm_w_up', 'm_conv_w', 'm_conv_b', 'm_w_down', 'm_final_norm_w', 'v_c_ctx', 'v_w_mod', 'v_b_mod', 'v_norm1_w', 'v_w_in', 'v_s5_lambda_re_f', 'v_s5_lambda_im_f', 'v_s5_log_step_f', 'v_s5_lambda_re_b', 'v_s5_lambda_im_b', 'v_s5_log_step_b', 'v_s5_b_re', 'v_s5_b_im', 'v_s5_c_re', 'v_s5_c_im', 'v_s5_d', 'v_s5_w_glu', 'v_s5_b_glu', 'v_ret_log_decay_f', 'v_ret_log_decay_b', 'v_w_out', 'v_norm2_w', 'v_w_up', 'v_conv_w', 'v_conv_b', 'v_w_down', 'v_final_norm_w']
TWIN_OUTPUTS = ['loss', 'grad_x', 'grad_c_ctx', 'grad_w_mod', 'grad_b_mod', 'grad_norm1_w', 'grad_w_in', 'grad_s5_lambda_re_f', 'grad_s5_lambda_im_f', 'grad_s5_log_step_f', 'grad_s5_lambda_re_b', 'grad_s5_lambda_im_b', 'grad_s5_log_step_b', 'grad_s5_b_re', 'grad_s5_b_im', 'grad_s5_c_re', 'grad_s5_c_im', 'grad_s5_d', 'grad_s5_w_glu', 'grad_s5_b_glu', 'grad_ret_log_decay_f', 'grad_ret_log_decay_b', 'grad_w_out', 'grad_norm2_w', 'grad_w_up', 'grad_conv_w', 'grad_conv_b', 'grad_w_down', 'grad_final_norm_w', 'delta_c_ctx', 'delta_w_mod', 'delta_b_mod', 'delta_norm1_w', 'delta_w_in', 'delta_s5_lambda_re_f', 'delta_s5_lambda_im_f', 'delta_s5_log_step_f', 'delta_s5_lambda_re_b', 'delta_s5_lambda_im_b', 'delta_s5_log_step_b', 'delta_s5_b_re', 'delta_s5_b_im', 'delta_s5_c_re', 'delta_s5_c_im', 'delta_s5_d', 'delta_s5_w_glu', 'delta_s5_b_glu', 'delta_ret_log_decay_f', 'delta_ret_log_decay_b', 'delta_w_out', 'delta_norm2_w', 'delta_w_up', 'delta_conv_w', 'delta_conv_b', 'delta_w_down', 'delta_final_norm_w', 'new_m_c_ctx', 'new_m_w_mod', 'new_m_b_mod', 'new_m_norm1_w', 'new_m_w_in', 'new_m_s5_lambda_re_f', 'new_m_s5_lambda_im_f', 'new_m_s5_log_step_f', 'new_m_s5_lambda_re_b', 'new_m_s5_lambda_im_b', 'new_m_s5_log_step_b', 'new_m_s5_b_re', 'new_m_s5_b_im', 'new_m_s5_c_re', 'new_m_s5_c_im', 'new_m_s5_d', 'new_m_s5_w_glu', 'new_m_s5_b_glu', 'new_m_ret_log_decay_f', 'new_m_ret_log_decay_b', 'new_m_w_out', 'new_m_norm2_w', 'new_m_w_up', 'new_m_conv_w', 'new_m_conv_b', 'new_m_w_down', 'new_m_final_norm_w', 'new_v_c_ctx', 'new_v_w_mod', 'new_v_b_mod', 'new_v_norm1_w', 'new_v_w_in', 'new_v_s5_lambda_re_f', 'new_v_s5_lambda_im_f', 'new_v_s5_log_step_f', 'new_v_s5_lambda_re_b', 'new_v_s5_lambda_im_b', 'new_v_s5_log_step_b', 'new_v_s5_b_re', 'new_v_s5_b_im', 'new_v_s5_c_re', 'new_v_s5_c_im', 'new_v_s5_d', 'new_v_s5_w_glu', 'new_v_s5_b_glu', 'new_v_ret_log_decay_f', 'new_v_ret_log_decay_b', 'new_v_w_out', 'new_v_norm2_w', 'new_v_w_up', 'new_v_conv_w', 'new_v_conv_b', 'new_v_w_down', 'new_v_final_norm_w']
TWIN_LEAF_KINDS = {'loss': 'loss', 'grad_x': 'grad_x', 'grad_c_ctx': 'grad_w', 'grad_w_mod': 'grad_w', 'grad_b_mod': 'grad_w', 'grad_norm1_w': 'grad_w', 'grad_w_in': 'grad_w', 'grad_s5_lambda_re_f': 'grad_w', 'grad_s5_lambda_im_f': 'grad_w', 'grad_s5_log_step_f': 'grad_w', 'grad_s5_lambda_re_b': 'grad_w', 'grad_s5_lambda_im_b': 'grad_w', 'grad_s5_log_step_b': 'grad_w', 'grad_s5_b_re': 'grad_w', 'grad_s5_b_im': 'grad_w', 'grad_s5_c_re': 'grad_w', 'grad_s5_c_im': 'grad_w', 'grad_s5_d': 'grad_w', 'grad_s5_w_glu': 'grad_w', 'grad_s5_b_glu': 'grad_w', 'grad_ret_log_decay_f': 'grad_w', 'grad_ret_log_decay_b': 'grad_w', 'grad_w_out': 'grad_w', 'grad_norm2_w': 'grad_w', 'grad_w_up': 'grad_w', 'grad_conv_w': 'grad_w', 'grad_conv_b': 'grad_w', 'grad_w_down': 'grad_w', 'grad_final_norm_w': 'grad_w', 'delta_c_ctx': 'delta_w', 'delta_w_mod': 'delta_w', 'delta_b_mod': 'delta_w', 'delta_norm1_w': 'delta_w', 'delta_w_in': 'delta_w', 'delta_s5_lambda_re_f': 'delta_w', 'delta_s5_lambda_im_f': 'delta_w', 'delta_s5_log_step_f': 'delta_w', 'delta_s5_lambda_re_b': 'delta_w', 'delta_s5_lambda_im_b': 'delta_w', 'delta_s5_log_step_b': 'delta_w', 'delta_s5_b_re': 'delta_w', 'delta_s5_b_im': 'delta_w', 'delta_s5_c_re': 'delta_w', 'delta_s5_c_im': 'delta_w', 'delta_s5_d': 'delta_w', 'delta_s5_w_glu': 'delta_w', 'delta_s5_b_glu': 'delta_w', 'delta_ret_log_decay_f': 'delta_w', 'delta_ret_log_decay_b': 'delta_w', 'delta_w_out': 'delta_w', 'delta_norm2_w': 'delta_w', 'delta_w_up': 'delta_w', 'delta_conv_w': 'delta_w', 'delta_conv_b': 'delta_w', 'delta_w_down': 'delta_w', 'delta_final_norm_w': 'delta_w', 'new_m_c_ctx': 'new_m', 'new_m_w_mod': 'new_m', 'new_m_b_mod': 'new_m', 'new_m_norm1_w': 'new_m', 'new_m_w_in': 'new_m', 'new_m_s5_lambda_re_f': 'new_m', 'new_m_s5_lambda_im_f': 'new_m', 'new_m_s5_log_step_f': 'new_m', 'new_m_s5_lambda_re_b': 'new_m', 'new_m_s5_lambda_im_b': 'new_m', 'new_m_s5_log_step_b': 'new_m', 'new_m_s5_b_re': 'new_m', 'new_m_s5_b_im': 'new_m', 'new_m_s5_c_re': 'new_m', 'new_m_s5_c_im': 'new_m', 'new_m_s5_d': 'new_m', 'new_m_s5_w_glu': 'new_m', 'new_m_s5_b_glu': 'new_m', 'new_m_ret_log_decay_f': 'new_m', 'new_m_ret_log_decay_b': 'new_m', 'new_m_w_out': 'new_m', 'new_m_norm2_w': 'new_m', 'new_m_w_up': 'new_m', 'new_m_conv_w': 'new_m', 'new_m_conv_b': 'new_m', 'new_m_w_down': 'new_m', 'new_m_final_norm_w': 'new_m', 'new_v_c_ctx': 'new_v', 'new_v_w_mod': 'new_v', 'new_v_b_mod': 'new_v', 'new_v_norm1_w': 'new_v', 'new_v_w_in': 'new_v', 'new_v_s5_lambda_re_f': 'new_v', 'new_v_s5_lambda_im_f': 'new_v', 'new_v_s5_log_step_f': 'new_v', 'new_v_s5_lambda_re_b': 'new_v', 'new_v_s5_lambda_im_b': 'new_v', 'new_v_s5_log_step_b': 'new_v', 'new_v_s5_b_re': 'new_v', 'new_v_s5_b_im': 'new_v', 'new_v_s5_c_re': 'new_v', 'new_v_s5_c_im': 'new_v', 'new_v_s5_d': 'new_v', 'new_v_s5_w_glu': 'new_v', 'new_v_s5_b_glu': 'new_v', 'new_v_ret_log_decay_f': 'new_v', 'new_v_ret_log_decay_b': 'new_v', 'new_v_w_out': 'new_v', 'new_v_norm2_w': 'new_v', 'new_v_w_up': 'new_v', 'new_v_conv_w': 'new_v', 'new_v_conv_b': 'new_v', 'new_v_w_down': 'new_v', 'new_v_final_norm_w': 'new_v'}


def _forward(args):
    return _fwd_reference(*[args[k] for k in FWD_PARAMS])


def _output_shape():
    def fwd():
        inp = _fwd_setup_inputs(0)
        return _fwd_reference(*[inp[k] for k in FWD_PARAMS])
    out = _jax.eval_shape(fwd)
    return out.shape, out.dtype

N_MICROBATCH = 1
ADAM_LR = 0.001
ADAM_B1 = 0.9
ADAM_B2 = 0.999
ADAM_EPS = 1e-08
ADAM_WD = 0.01
ADAM_STEP = 10
PER_EXAMPLE_BATCH_AXIS = {'x': 0, 'c': 0, 'ctx': 0, 'loss_target': 0}
SHARED_INPUTS = []
_WEIGHT_DTYPES = {'c_ctx': _jnp.float32, 'w_mod': _jnp.float32, 'b_mod': _jnp.float32, 'norm1_w': _jnp.float32, 'w_in': _jnp.float32, 's5_lambda_re_f': _jnp.float32, 's5_lambda_im_f': _jnp.float32, 's5_log_step_f': _jnp.float32, 's5_lambda_re_b': _jnp.float32, 's5_lambda_im_b': _jnp.float32, 's5_log_step_b': _jnp.float32, 's5_b_re': _jnp.float32, 's5_b_im': _jnp.float32, 's5_c_re': _jnp.float32, 's5_c_im': _jnp.float32, 's5_d': _jnp.float32, 's5_w_glu': _jnp.float32, 's5_b_glu': _jnp.float32, 'ret_log_decay_f': _jnp.float32, 'ret_log_decay_b': _jnp.float32, 'w_out': _jnp.float32, 'norm2_w': _jnp.float32, 'w_up': _jnp.float32, 'conv_w': _jnp.float32, 'conv_b': _jnp.float32, 'w_down': _jnp.float32, 'final_norm_w': _jnp.float32}
MOMENT_SCALE = {'c_ctx': 1.891646e-02, 'w_mod': 7.743244e-02, 'b_mod': 1.291798e-01, 'norm1_w': 7.301218e-02, 'w_in': 5.057355e-02, 's5_lambda_re_f': 1.080831e-02, 's5_lambda_im_f': 8.429068e-03, 's5_log_step_f': 4.314212e+00, 's5_lambda_re_b': 1.314249e-02, 's5_lambda_im_b': 1.212714e-02, 's5_log_step_b': 5.568230e+00, 's5_b_re': 1.279262e-02, 's5_b_im': 1.033140e-02, 's5_c_re': 4.080249e-03, 's5_c_im': 3.982257e-03, 's5_d': 3.280706e-02, 's5_w_glu': 7.136155e-03, 's5_b_glu': 1.195911e-02, 'ret_log_decay_f': 5.038318e+01, 'ret_log_decay_b': 7.362655e+01, 'w_out': 3.943640e-02, 'norm2_w': 7.946725e-02, 'w_up': 3.545084e-02, 'conv_w': 3.648263e-02, 'conv_b': 3.263875e-02, 'w_down': 5.782577e-02, 'final_norm_w': 6.399227e+01}


def _to_microbatches(a, axis):
    t = _jnp.moveaxis(a, axis, 0)
    t = t.reshape((N_MICROBATCH, t.shape[0] // N_MICROBATCH) + t.shape[1:])
    return _jnp.moveaxis(t, 1, axis + 1)


def setup_inputs(seed: int = 0) -> dict:
    inp = _fwd_setup_inputs(seed)
    key = _jax.random.fold_in(_jax.random.key(seed), 7919)
    shape, _ = _output_shape()
    out = dict(inp)
    out["loss_target"] = _jax.random.normal(_jax.random.fold_in(key, 0), shape, _jnp.float32)
    for i, name in enumerate(TWIN_WEIGHTS):
        w = inp[name].astype(_jnp.float32)
        if MOMENT_SCALE is None:
            s = _jnp.sqrt(_jnp.mean(_jnp.square(w)) + 1e-30)
        else:
            s = MOMENT_SCALE[name]
        km, kv = _jax.random.split(_jax.random.fold_in(key, i + 1))
        out[name] = w
        out["m_" + name] = s * _jax.random.normal(km, w.shape, _jnp.float32)
        out["v_" + name] = (s * s) * _jax.random.uniform(kv, w.shape, _jnp.float32, 0.5, 1.5)
    if N_MICROBATCH > 1:
        for name, axis in PER_EXAMPLE_BATCH_AXIS.items():
            out[name] = _to_microbatches(out[name], axis)
    return {'x': out['x'], 'c': out['c'], 'ctx': out['ctx'], 'c_ctx': out['c_ctx'], 'w_mod': out['w_mod'], 'b_mod': out['b_mod'], 'norm1_w': out['norm1_w'], 'w_in': out['w_in'], 's5_lambda_re_f': out['s5_lambda_re_f'], 's5_lambda_im_f': out['s5_lambda_im_f'], 's5_log_step_f': out['s5_log_step_f'], 's5_lambda_re_b': out['s5_lambda_re_b'], 's5_lambda_im_b': out['s5_lambda_im_b'], 's5_log_step_b': out['s5_log_step_b'], 's5_b_re': out['s5_b_re'], 's5_b_im': out['s5_b_im'], 's5_c_re': out['s5_c_re'], 's5_c_im': out['s5_c_im'], 's5_d': out['s5_d'], 's5_w_glu': out['s5_w_glu'], 's5_b_glu': out['s5_b_glu'], 'ret_log_decay_f': out['ret_log_decay_f'], 'ret_log_decay_b': out['ret_log_decay_b'], 'w_out': out['w_out'], 'norm2_w': out['norm2_w'], 'w_up': out['w_up'], 'conv_w': out['conv_w'], 'conv_b': out['conv_b'], 'w_down': out['w_down'], 'final_norm_w': out['final_norm_w'], 'loss_target': out['loss_target'], 'm_c_ctx': out['m_c_ctx'], 'm_w_mod': out['m_w_mod'], 'm_b_mod': out['m_b_mod'], 'm_norm1_w': out['m_norm1_w'], 'm_w_in': out['m_w_in'], 'm_s5_lambda_re_f': out['m_s5_lambda_re_f'], 'm_s5_lambda_im_f': out['m_s5_lambda_im_f'], 'm_s5_log_step_f': out['m_s5_log_step_f'], 'm_s5_lambda_re_b': out['m_s5_lambda_re_b'], 'm_s5_lambda_im_b': out['m_s5_lambda_im_b'], 'm_s5_log_step_b': out['m_s5_log_step_b'], 'm_s5_b_re': out['m_s5_b_re'], 'm_s5_b_im': out['m_s5_b_im'], 'm_s5_c_re': out['m_s5_c_re'], 'm_s5_c_im': out['m_s5_c_im'], 'm_s5_d': out['m_s5_d'], 'm_s5_w_glu': out['m_s5_w_glu'], 'm_s5_b_glu': out['m_s5_b_glu'], 'm_ret_log_decay_f': out['m_ret_log_decay_f'], 'm_ret_log_decay_b': out['m_ret_log_decay_b'], 'm_w_out': out['m_w_out'], 'm_norm2_w': out['m_norm2_w'], 'm_w_up': out['m_w_up'], 'm_conv_w': out['m_conv_w'], 'm_conv_b': out['m_conv_b'], 'm_w_down': out['m_w_down'], 'm_final_norm_w': out['m_final_norm_w'], 'v_c_ctx': out['v_c_ctx'], 'v_w_mod': out['v_w_mod'], 'v_b_mod': out['v_b_mod'], 'v_norm1_w': out['v_norm1_w'], 'v_w_in': out['v_w_in'], 'v_s5_lambda_re_f': out['v_s5_lambda_re_f'], 'v_s5_lambda_im_f': out['v_s5_lambda_im_f'], 'v_s5_log_step_f': out['v_s5_log_step_f'], 'v_s5_lambda_re_b': out['v_s5_lambda_re_b'], 'v_s5_lambda_im_b': out['v_s5_lambda_im_b'], 'v_s5_log_step_b': out['v_s5_log_step_b'], 'v_s5_b_re': out['v_s5_b_re'], 'v_s5_b_im': out['v_s5_b_im'], 'v_s5_c_re': out['v_s5_c_re'], 'v_s5_c_im': out['v_s5_c_im'], 'v_s5_d': out['v_s5_d'], 'v_s5_w_glu': out['v_s5_w_glu'], 'v_s5_b_glu': out['v_s5_b_glu'], 'v_ret_log_decay_f': out['v_ret_log_decay_f'], 'v_ret_log_decay_b': out['v_ret_log_decay_b'], 'v_w_out': out['v_w_out'], 'v_norm2_w': out['v_norm2_w'], 'v_w_up': out['v_w_up'], 'v_conv_w': out['v_conv_w'], 'v_conv_b': out['v_conv_b'], 'v_w_down': out['v_w_down'], 'v_final_norm_w': out['v_final_norm_w']}


def _loss(weights, diff, rest, loss_target):
    with _jax.named_scope("forward"):
        args = {**rest, TWIN_DIFF_INPUT: diff, **{k: w.astype(_WEIGHT_DTYPES[k]) for k, w in weights.items()}}
        y = _forward(args)
    with _jax.named_scope("loss_head"):
        err = _jnp.square(y.astype(_jnp.float32) - loss_target)
        return 0.5 * _jnp.sum(_jnp.mean(err, axis=-1)) if err.ndim else 0.5 * err


def _adamw(w, g, m, v):
    m = ADAM_B1 * m + (1.0 - ADAM_B1) * g
    v = ADAM_B2 * v + (1.0 - ADAM_B2) * _jnp.square(g)
    m_hat = m / (1.0 - ADAM_B1 ** ADAM_STEP)
    v_hat = v / (1.0 - ADAM_B2 ** ADAM_STEP)
    delta = -ADAM_LR * (m_hat / (_jnp.sqrt(v_hat) + ADAM_EPS) + ADAM_WD * w)
    return delta, m, v


def reference(x, c, ctx, c_ctx, w_mod, b_mod, norm1_w, w_in, s5_lambda_re_f, s5_lambda_im_f, s5_log_step_f, s5_lambda_re_b, s5_lambda_im_b, s5_log_step_b, s5_b_re, s5_b_im, s5_c_re, s5_c_im, s5_d, s5_w_glu, s5_b_glu, ret_log_decay_f, ret_log_decay_b, w_out, norm2_w, w_up, conv_w, conv_b, w_down, final_norm_w, loss_target, m_c_ctx, m_w_mod, m_b_mod, m_norm1_w, m_w_in, m_s5_lambda_re_f, m_s5_lambda_im_f, m_s5_log_step_f, m_s5_lambda_re_b, m_s5_lambda_im_b, m_s5_log_step_b, m_s5_b_re, m_s5_b_im, m_s5_c_re, m_s5_c_im, m_s5_d, m_s5_w_glu, m_s5_b_glu, m_ret_log_decay_f, m_ret_log_decay_b, m_w_out, m_norm2_w, m_w_up, m_conv_w, m_conv_b, m_w_down, m_final_norm_w, v_c_ctx, v_w_mod, v_b_mod, v_norm1_w, v_w_in, v_s5_lambda_re_f, v_s5_lambda_im_f, v_s5_log_step_f, v_s5_lambda_re_b, v_s5_lambda_im_b, v_s5_log_step_b, v_s5_b_re, v_s5_b_im, v_s5_c_re, v_s5_c_im, v_s5_d, v_s5_w_glu, v_s5_b_glu, v_ret_log_decay_f, v_ret_log_decay_b, v_w_out, v_norm2_w, v_w_up, v_conv_w, v_conv_b, v_w_down, v_final_norm_w):
    given = dict(x=x, c=c, ctx=ctx, c_ctx=c_ctx, w_mod=w_mod, b_mod=b_mod, norm1_w=norm1_w, w_in=w_in, s5_lambda_re_f=s5_lambda_re_f, s5_lambda_im_f=s5_lambda_im_f, s5_log_step_f=s5_log_step_f, s5_lambda_re_b=s5_lambda_re_b, s5_lambda_im_b=s5_lambda_im_b, s5_log_step_b=s5_log_step_b, s5_b_re=s5_b_re, s5_b_im=s5_b_im, s5_c_re=s5_c_re, s5_c_im=s5_c_im, s5_d=s5_d, s5_w_glu=s5_w_glu, s5_b_glu=s5_b_glu, ret_log_decay_f=ret_log_decay_f, ret_log_decay_b=ret_log_decay_b, w_out=w_out, norm2_w=norm2_w, w_up=w_up, conv_w=conv_w, conv_b=conv_b, w_down=w_down, final_norm_w=final_norm_w, loss_target=loss_target, m_c_ctx=m_c_ctx, m_w_mod=m_w_mod, m_b_mod=m_b_mod, m_norm1_w=m_norm1_w, m_w_in=m_w_in, m_s5_lambda_re_f=m_s5_lambda_re_f, m_s5_lambda_im_f=m_s5_lambda_im_f, m_s5_log_step_f=m_s5_log_step_f, m_s5_lambda_re_b=m_s5_lambda_re_b, m_s5_lambda_im_b=m_s5_lambda_im_b, m_s5_log_step_b=m_s5_log_step_b, m_s5_b_re=m_s5_b_re, m_s5_b_im=m_s5_b_im, m_s5_c_re=m_s5_c_re, m_s5_c_im=m_s5_c_im, m_s5_d=m_s5_d, m_s5_w_glu=m_s5_w_glu, m_s5_b_glu=m_s5_b_glu, m_ret_log_decay_f=m_ret_log_decay_f, m_ret_log_decay_b=m_ret_log_decay_b, m_w_out=m_w_out, m_norm2_w=m_norm2_w, m_w_up=m_w_up, m_conv_w=m_conv_w, m_conv_b=m_conv_b, m_w_down=m_w_down, m_final_norm_w=m_final_norm_w, v_c_ctx=v_c_ctx, v_w_mod=v_w_mod, v_b_mod=v_b_mod, v_norm1_w=v_norm1_w, v_w_in=v_w_in, v_s5_lambda_re_f=v_s5_lambda_re_f, v_s5_lambda_im_f=v_s5_lambda_im_f, v_s5_log_step_f=v_s5_log_step_f, v_s5_lambda_re_b=v_s5_lambda_re_b, v_s5_lambda_im_b=v_s5_lambda_im_b, v_s5_log_step_b=v_s5_log_step_b, v_s5_b_re=v_s5_b_re, v_s5_b_im=v_s5_b_im, v_s5_c_re=v_s5_c_re, v_s5_c_im=v_s5_c_im, v_s5_d=v_s5_d, v_s5_w_glu=v_s5_w_glu, v_s5_b_glu=v_s5_b_glu, v_ret_log_decay_f=v_ret_log_decay_f, v_ret_log_decay_b=v_ret_log_decay_b, v_w_out=v_w_out, v_norm2_w=v_norm2_w, v_w_up=v_w_up, v_conv_w=v_conv_w, v_conv_b=v_conv_b, v_w_down=v_w_down, v_final_norm_w=v_final_norm_w)
    weights = {n: given[n] for n in TWIN_WEIGHTS}
    shared = {n: given[n] for n in SHARED_INPUTS}
    per_example = {n: given[n] for n in ['x', 'c', 'ctx']}
    grad_fn = _jax.value_and_grad(_loss, argnums=(0, 1))

    def one_microbatch(ex, loss_target):
        ex = dict(ex)
        diff = ex.pop(TWIN_DIFF_INPUT)
        return grad_fn(weights, diff, {**shared, **ex}, loss_target)

    if N_MICROBATCH == 1:
        loss, (grad_w, grad_x) = one_microbatch(per_example, given["loss_target"])
    else:
        def body(carry, xs):
            loss_sum, grad_sum = carry
            l_k, (gw_k, gx_k) = one_microbatch(xs[0], xs[1])
            with _jax.named_scope("update"):
                return (loss_sum + l_k, _jax.tree.map(_jnp.add, grad_sum, gw_k)), gx_k

        init = (_jnp.zeros((), _jnp.float32), _jax.tree.map(_jnp.zeros_like, weights))
        (loss, grad_w), grad_x = _jax.lax.scan(body, init, (per_example, given["loss_target"]))
    with _jax.named_scope("update"):
        delta_w, new_m, new_v = {}, {}, {}
        for n in TWIN_WEIGHTS:
            delta_w[n], new_m[n], new_v[n] = _adamw(weights[n], grad_w[n], given["m_" + n], given["v_" + n])
    return (loss, grad_x, *[grad_w[n] for n in TWIN_WEIGHTS], *[delta_w[n] for n in TWIN_WEIGHTS],
            *[new_m[n] for n in TWIN_WEIGHTS], *[new_v[n] for n in TWIN_WEIGHTS])
```

```python
import functools
import math

import numpy as np
import jax
import jax.numpy as jnp
from jax import lax
from jax.experimental import pallas as pl
from jax.experimental.pallas import tpu as pltpu

f32, bf16 = jnp.float32, jnp.bfloat16

D = 1024
S5W, S5G, S5P, S5N = 512, 32, 16, 64
GN = S5G * S5N
NB = 2
KB, NS = S5W // NB, GN // NB
GB = S5G // NB
RH, DH = 4, 128
RW = RH * DH
INC = S5W + 4 * RW
DFF = 2816
T = 128
R = 256
RF = 128
HALO = 8
EPS = 1e-6
ROPE_THETA = 10000.0
GRID_W = 64
NDEV = 8
LR, B1, B2, AEPS, WD, STEP = 0.001, 0.9, 0.999, 1e-08, 0.01, 10
VMEM_LIMIT = 60 * 1024 * 1024
MESH = pl.DeviceIdType.MESH

_CP = functools.partial(pltpu.CompilerParams, vmem_limit_bytes=VMEM_LIMIT)
_ARB = ("arbitrary",)


def _dg(a, b, dims):
    return lax.dot_general(a.astype(bf16), b.astype(bf16), (dims, ((), ())), preferred_element_type=f32)


@jax.custom_vjp
def dnn(a, b):
    return _dg(a, b, ((1,), (0,)))


@jax.custom_vjp
def dnt(a, b):
    return _dg(a, b, ((1,), (1,)))


@jax.custom_vjp
def dtn(a, b):
    return _dg(a, b, ((0,), (0,)))


dnn.defvjp(lambda a, b: (dnn(a, b), (a, b)), lambda r, g: (dnt(g, r[1]).astype(r[0].dtype), dtn(r[0], g).astype(r[1].dtype)))
dnt.defvjp(lambda a, b: (dnt(a, b), (a, b)), lambda r, g: (dnn(g, r[1]).astype(r[0].dtype), dtn(g, r[0]).astype(r[1].dtype)))
dtn.defvjp(lambda a, b: (dtn(a, b), (a, b)), lambda r, g: (dnt(r[1], g).astype(r[0].dtype), dnn(r[0], g).astype(r[1].dtype)))


def _rms(t, w):
    return t * lax.rsqrt(jnp.mean(t * t, axis=-1, keepdims=True) + EPS) * w


def _mod(h, shift, scale):
    return h * (1.0 + scale) + shift


def _const_spec(shape):
    n = len(shape)
    return pl.BlockSpec(shape, lambda i, _n=n: (0,) * _n, pipeline_mode=pl.Buffered(1))


def _acc_spec(shape):
    n = len(shape)
    return pl.BlockSpec(shape, lambda i, _n=n: (0,) * _n)


def _me():
    return 4 * lax.axis_index("x") + 2 * lax.axis_index("y") + lax.axis_index("c")


def _peer(r):
    x, y, c = lax.axis_index("x"), lax.axis_index("y"), lax.axis_index("c")
    px = 1 - x if (r >> 2) & 1 else x
    py = 1 - y if (r >> 1) & 1 else y
    pc = 1 - c if r & 1 else c
    return (px, py, pc), 4 * px + 2 * py + pc


def _all_gather_small(v, name):
    r, c = v.shape

    def body(v_ref, out_ref, send_sems, recv_sems):
        me = _me()
        out_ref[me] = v_ref[...]
        sends = []
        for k in range(1, NDEV):
            peer, _ = _peer(k)
            cp = pltpu.make_async_remote_copy(src_ref=v_ref, dst_ref=out_ref.at[me], send_sem=send_sems.at[k - 1],
                                              recv_sem=recv_sems.at[k - 1], device_id=peer, device_id_type=MESH)
            cp.start()
            sends.append(cp)
        for k in range(1, NDEV):
            peer, pidx = _peer(k)
            pltpu.make_async_remote_copy(src_ref=v_ref, dst_ref=out_ref.at[pidx], send_sem=send_sems.at[k - 1],
                                         recv_sem=recv_sems.at[k - 1], device_id=peer, device_id_type=MESH).wait_recv()
        for cp in sends:
            cp.wait_send()

    return pl.pallas_call(
        body, name=name, out_shape=jax.ShapeDtypeStruct((NDEV, r, c), v.dtype),
        in_specs=[pl.BlockSpec(memory_space=pltpu.VMEM)], out_specs=pl.BlockSpec(memory_space=pltpu.VMEM),
        scratch_shapes=[pltpu.SemaphoreType.DMA((NDEV - 1,)), pltpu.SemaphoreType.DMA((NDEV - 1,))],
        compiler_params=_CP(),
    )(v)


def _all_reduce_small(v, name):
    r, c = v.shape

    def body(v_ref, out_ref, land, send_sems, recv_sems):
        me = _me()
        land[me] = v_ref[...]
        sends = []
        for k in range(1, NDEV):
            peer, _ = _peer(k)
            cp = pltpu.make_async_remote_copy(src_ref=v_ref, dst_ref=land.at[me], send_sem=send_sems.at[k - 1],
                                              recv_sem=recv_sems.at[k - 1], device_id=peer, device_id_type=MESH)
            cp.start()
            sends.append(cp)
        for k in range(1, NDEV):
            peer, pidx = _peer(k)
            pltpu.make_async_remote_copy(src_ref=v_ref, dst_ref=land.at[pidx], send_sem=send_sems.at[k - 1],
                                         recv_sem=recv_sems.at[k - 1], device_id=peer, device_id_type=MESH).wait_recv()
        for cp in sends:
            cp.wait_send()
        acc = land[0]
        for j in range(1, NDEV):
            acc = acc + land[j]
        out_ref[...] = acc

    return pl.pallas_call(
        body, name=name, out_shape=jax.ShapeDtypeStruct((r, c), v.dtype),
        in_specs=[pl.BlockSpec(memory_space=pltpu.VMEM)], out_specs=pl.BlockSpec(memory_space=pltpu.VMEM),
        scratch_shapes=[pltpu.VMEM((NDEV, r, c), v.dtype), pltpu.SemaphoreType.DMA((NDEV - 1,)),
                        pltpu.SemaphoreType.DMA((NDEV - 1,))],
        compiler_params=_CP(),
    )(v)


def _all_gather_hbm(v, name):
    r, c = v.shape

    def body(v_ref, out_ref, send_sems, recv_sems, local_sem):
        me = _me()
        mine = pltpu.make_async_copy(v_ref, out_ref.at[me], local_sem)
        mine.start()
        sends = []
        for k in range(1, NDEV):
            peer, _ = _peer(k)
            cp = pltpu.make_async_remote_copy(src_ref=v_ref, dst_ref=out_ref.at[me], send_sem=send_sems.at[k - 1],
                                              recv_sem=recv_sems.at[k - 1], device_id=peer, device_id_type=MESH)
            cp.start()
            sends.append(cp)
        for k in range(1, NDEV):
            peer, pidx = _peer(k)
            pltpu.make_async_remote_copy(src_ref=v_ref, dst_ref=out_ref.at[pidx], send_sem=send_sems.at[k - 1],
                                         recv_sem=recv_sems.at[k - 1], device_id=peer, device_id_type=MESH).wait_recv()
        for cp in sends:
            cp.wait_send()
        mine.wait()

    return pl.pallas_call(
        body, name=name, out_shape=jax.ShapeDtypeStruct((NDEV, r, c), v.dtype),
        in_specs=[pl.BlockSpec(memory_space=pl.ANY)], out_specs=pl.BlockSpec(memory_space=pl.ANY),
        scratch_shapes=[pltpu.SemaphoreType.DMA((NDEV - 1,)), pltpu.SemaphoreType.DMA((NDEV - 1,)), pltpu.SemaphoreType.DMA],
        compiler_params=_CP(),
    )(v)


def _scatter_hbm(g, name):
    _, r, c = g.shape

    def body(g_ref, out_ref, send_sems, recv_sems, local_sem):
        me = _me()
        mine = pltpu.make_async_copy(g_ref.at[me], out_ref.at[me], local_sem)
        mine.start()
        sends = []
        for k in range(1, NDEV):
            peer, pidx = _peer(k)
            cp = pltpu.make_async_remote_copy(src_ref=g_ref.at[pidx], dst_ref=out_ref.at[me], send_sem=send_sems.at[k - 1],
                                              recv_sem=recv_sems.at[k - 1], device_id=peer, device_id_type=MESH)
            cp.start()
            sends.append(cp)
        for k in range(1, NDEV):
            peer, pidx = _peer(k)
            pltpu.make_async_remote_copy(src_ref=g_ref.at[pidx], dst_ref=out_ref.at[pidx], send_sem=send_sems.at[k - 1],
                                         recv_sem=recv_sems.at[k - 1], device_id=peer, device_id_type=MESH).wait_recv()
        for cp in sends:
            cp.wait_send()
        mine.wait()

    return pl.pallas_call(
        body, name=name, out_shape=jax.ShapeDtypeStruct(g.shape, g.dtype),
        in_specs=[pl.BlockSpec(memory_space=pl.ANY)], out_specs=pl.BlockSpec(memory_space=pl.ANY),
        scratch_shapes=[pltpu.SemaphoreType.DMA((NDEV - 1,)), pltpu.SemaphoreType.DMA((NDEV - 1,)), pltpu.SemaphoreType.DMA],
        compiler_params=_CP(),
    )(g)


def _sum8(land, name):
    _, r, c = land.shape
    rb = 256 if r % 256 == 0 else r

    def body(l_ref, o_ref):
        acc = l_ref[0].astype(f32)
        for j in range(1, NDEV):
            acc = acc + l_ref[j].astype(f32)
        o_ref[...] = acc

    return pl.pallas_call(
        body, name=name, grid=(r // rb,), out_shape=jax.ShapeDtypeStruct((r, c), f32),
        in_specs=[pl.BlockSpec((NDEV, rb, c), lambda i: (0, i, 0))], out_specs=pl.BlockSpec((rb, c), lambda i: (i, 0)),
        compiler_params=_CP(dimension_semantics=("parallel",)),
    )(land)


def _ada_fwd(c9, w_mod_l, name):
    def body(c_ref, w_ref, o_ref):
        o_ref[...] = dnn(jax.nn.silu(c_ref[...]), w_ref[...])

    return pl.pallas_call(body, name=name, out_shape=jax.ShapeDtypeStruct((16, w_mod_l.shape[1]), f32),
                          compiler_params=_CP())(c9, w_mod_l)


def _mod_select(m_all, b_mod6, name):
    def body(m_ref, b_ref, mx_ref, mc_ref):
        me = _me()
        mx_ref[...] = m_ref[me] + b_ref[...]
        mc_ref[...] = m_ref[8] + b_ref[...]

    return pl.pallas_call(body, name=name, out_shape=[jax.ShapeDtypeStruct((6, D), f32)] * 2, compiler_params=_CP())(m_all, b_mod6)


def _ada_bwd(c9, dmx_all, dmc_all, dmx_l, dmc_l, w_mod_l, name):
    ncol = w_mod_l.shape[1]

    def rowsum(r):
        acc = r[0:1]
        for j in range(1, NDEV):
            acc = acc + r[j:j + 1]
        return acc

    def body(c_ref, xa_ref, ca_ref, xl_ref, cl_ref, w_ref, gw_ref, gb_ref, dc_ref):
        s9, vjp = jax.vjp(jax.nn.silu, c_ref[...])
        dm9 = jnp.concatenate([xl_ref[...], rowsum(cl_ref[...]), jnp.zeros((7, ncol), f32)], axis=0)
        gw_ref[...] = dtn(s9, dm9)
        gb_ref[...] = rowsum(xa_ref[...]) + rowsum(ca_ref[...])
        dc_ref[...] = vjp(dnt(dm9, w_ref[...]))[0]

    return pl.pallas_call(
        body, name=name,
        out_shape=[jax.ShapeDtypeStruct((D, ncol), f32), jax.ShapeDtypeStruct((1, 6 * D), f32), jax.ShapeDtypeStruct((16, D), f32)],
        compiler_params=_CP())(c9, dmx_all, dmc_all, dmx_l, dmc_l, w_mod_l)


def _s5_disc(lre, lim, lstep, bre, bim):
    s = jnp.exp(lstep)
    ar, ai = lre * s, lim * s
    e = jnp.exp(ar)
    lbr, lbi = e * jnp.cos(ai), e * jnp.sin(ai)
    nr, ni = lbr - 1.0, lbi
    den = lre * lre + lim * lim
    cr, ci = (nr * lre + ni * lim) / den, (ni * lre - nr * lim) / den
    return lbr, lbi, cr[None] * bre - ci[None] * bim, cr[None] * bim + ci[None] * bre


def _s5_prep(lre, lim, lstep, bre, bim, name):
    def body(lre_ref, lim_ref, ls_ref, bre_ref, bim_ref, lam_ref, bbr_ref, bbi_ref, tab_ref):
        lbr, lbi, bbr, bbi = _s5_disc(lre_ref[...], lim_ref[...], ls_ref[...], bre_ref[...], bim_ref[...])
        lam_ref[0], lam_ref[1] = lbr, lbi
        bbr_ref[...], bbi_ref[...] = bbr, bbi
        s = jnp.exp(ls_ref[...])
        ar, ai = (lre_ref[...] * s)[None], (lim_ref[...] * s)[None]
        t = lax.broadcasted_iota(jnp.int32, (T, S5G, S5N), 0).astype(f32)
        ea, ang = jnp.exp(t * ar), t * ai
        cs, sn = jnp.cos(ang), jnp.sin(ang)
        tab_ref[0], tab_ref[1] = cs / ea, -sn / ea
        tab_ref[2], tab_ref[3] = ea * cs, ea * sn

    gshape = (S5G, S5N)
    return pl.pallas_call(
        body, name=name,
        out_shape=[jax.ShapeDtypeStruct((2,) + gshape, f32), jax.ShapeDtypeStruct((S5P,) + gshape, f32),
                   jax.ShapeDtypeStruct((S5P,) + gshape, f32), jax.ShapeDtypeStruct((4, T) + gshape, f32)],
        compiler_params=_CP())(lre, lim, lstep, bre, bim)


def _s5_prep_bwd(lre, lim, lstep, bre, bim, dlacc, dbbr, dbbi, name):
    def body(lre_ref, lim_ref, ls_ref, bre_ref, bim_ref, dl_ref, dbr_ref, dbi_ref, glre, glim, gls, gbre, gbim):
        prim = (lre_ref[...], lim_ref[...], ls_ref[...], bre_ref[...], bim_ref[...])
        (lbr, lbi, _, _), vjp = jax.vjp(_s5_disc, *prim)
        x, y = dl_ref[0], dl_ref[1]
        n2 = lbr * lbr + lbi * lbi
        dlr, dli = (x * lbr - y * lbi) / n2, (x * lbi + y * lbr) / n2
        g = vjp((dlr, dli, dbr_ref[...], dbi_ref[...]))
        glre[...], glim[...], gls[...], gbre[...], gbim[...] = g

    gshape = (S5G, S5N)
    return pl.pallas_call(
        body, name=name,
        out_shape=[jax.ShapeDtypeStruct(gshape, f32), jax.ShapeDtypeStruct(gshape, f32), jax.ShapeDtypeStruct((S5G, 1), f32),
                   jax.ShapeDtypeStruct((S5P,) + gshape, f32), jax.ShapeDtypeStruct((S5P,) + gshape, f32)],
        compiler_params=_CP())(lre, lim, lstep, bre, bim, dlacc, dbbr, dbbi)


def _blockdiag(m_gpn):
    m = m_gpn.reshape(NB, GB, S5P, S5N)
    eye = jnp.eye(GB, dtype=m.dtype)
    return (m[:, :, :, None, :] * eye[None, :, None, :, None]).reshape(NB, KB, NS).astype(bf16)


def _blockdiag_extract(m):
    m = m.reshape(NB, GB, S5P, GB, S5N)
    return jnp.einsum("bgpgn->bgpn", m).reshape(S5G, S5P, S5N)


def _cmul(ar, ai, br, bi):
    return ar * br - ai * bi, ar * bi + ai * br


def _tri_dot(tri, z):
    zh = z.astype(bf16)
    zl = (z - zh.astype(f32)).astype(bf16)
    return jnp.dot(tri, zh, preferred_element_type=f32) + jnp.dot(tri, zl, preferred_element_type=f32)


def _scan_chunk(xr, xi, wir, wii, wfr, wfi, tri, gr, gi):
    zr, zi = _cmul(wir, wii, xr, xi)
    return _cmul(wfr, wfi, _tri_dot(tri, zr) + gr, _tri_dot(tri, zi) + gi)


def _proj_in(u, m_ref):
    return jnp.concatenate([jnp.dot(u[:, b * KB:(b + 1) * KB], m_ref[b], preferred_element_type=f32) for b in range(NB)], axis=1)


def _proj_out(h, m_ref):
    return jnp.concatenate([dnt(h[:, b * NS:(b + 1) * NS], m_ref[b]) for b in range(NB)], axis=1)


def _outer_acc(a, h):
    return jnp.stack([dtn(a[:, b * KB:(b + 1) * KB], h[:, b * NS:(b + 1) * NS]) for b in range(NB)], axis=0)


def _idx_fwd(nctx, nch):
    return lambda i: i


def _idx_rev(nctx, nch):
    return lambda i: jnp.where(i < nctx, nctx - 1 - i, nch + nctx - 1 - i)


def _s5_fwd(p_ext, mats, tab, lam, tri, rev, nctx, name):
    n = p_ext.shape[0]
    nch = n // T
    idx = (_idx_rev if rev else _idx_fwd)(nctx, nch)
    bre, bim, cre, cim = mats

    def body(u_ref, bre_ref, bim_ref, cre_ref, cim_ref, tab_ref, lam_ref, tri_ref, y_ref, hb_ref, h_s):
        @pl.when(pl.program_id(0) == 0)
        def _():
            h_s[...] = jnp.zeros_like(h_s)

        u = u_ref[...]
        xr, xi = _proj_in(u, bre_ref), _proj_in(u, bim_ref)
        hp = h_s[...]
        hb_ref[0] = hp
        gr, gi = _cmul(lam_ref[0:1], lam_ref[1:2], hp[0:1], hp[1:2])
        hr, hi = _scan_chunk(xr, xi, tab_ref[0], tab_ref[1], tab_ref[2], tab_ref[3], tri_ref[...], gr, gi)
        last = 0 if rev else T - 1
        h_s[0:1] = hr[last:last + 1]
        h_s[1:2] = hi[last:last + 1]
        y_ref[...] = _proj_out(hr, cre_ref) - _proj_out(hi, cim_ref)

    mspec = _const_spec((NB, KB, NS))
    return pl.pallas_call(
        body, name=name, grid=(nch,),
        in_specs=[pl.BlockSpec((T, S5W), lambda i: (idx(i), 0)), mspec, mspec, mspec, mspec,
                  _const_spec((4, T, GN)), _const_spec((2, GN)), _const_spec((T, T))],
        out_specs=[pl.BlockSpec((T, S5W), lambda i: (idx(i), 0)), pl.BlockSpec((1, 2, GN), lambda i: (i, 0, 0))],
        out_shape=[jax.ShapeDtypeStruct((n, S5W), f32), jax.ShapeDtypeStruct((nch, 2, GN), f32)],
        scratch_shapes=[pltpu.VMEM((2, GN), f32)],
        compiler_params=_CP(dimension_semantics=_ARB),
    )(p_ext, bre, bim, cre, cim, tab, lam, tri)


def _s5_bwd(p_ext, dy_ext, hb, mats, tab_f, tab_a, lam, tri_f, tri_a, rev, nctx, name):
    n = p_ext.shape[0]
    nch = n // T
    idx0 = (_idx_rev if rev else _idx_fwd)(nctx, nch)
    idx = lambda j: idx0(nch - 1 - j)
    bre, bim, cre, cim = mats

    def body(u_ref, dy_ref, hb_ref, bre_ref, bim_ref, cre_ref, cim_ref, tf_ref, ta_ref, lam_ref, trf_ref, tra_ref,
             du_ref, dbre_ref, dbim_ref, dcre_ref, dcim_ref, dlam_ref, a_s):
        @pl.when(pl.program_id(0) == 0)
        def _():
            a_s[...] = jnp.zeros_like(a_s)
            dbre_ref[...] = jnp.zeros_like(dbre_ref)
            dbim_ref[...] = jnp.zeros_like(dbim_ref)
            dcre_ref[...] = jnp.zeros_like(dcre_ref)
            dcim_ref[...] = jnp.zeros_like(dcim_ref)
            dlam_ref[...] = jnp.zeros_like(dlam_ref)

        u = u_ref[...]
        dy = dy_ref[...].astype(bf16)
        lr, li = lam_ref[0:1], lam_ref[1:2]
        xr, xi = _proj_in(u, bre_ref), _proj_in(u, bim_ref)
        hp = hb_ref[0]
        gr, gi = _cmul(lr, li, hp[0:1], hp[1:2])
        hr, hi = _scan_chunk(xr, xi, tf_ref[0], tf_ref[1], tf_ref[2], tf_ref[3], trf_ref[...], gr, gi)
        cgr, cgi = _proj_in(dy, cre_ref), -_proj_in(dy, cim_ref)
        ac = a_s[...]
        agr, agi = _cmul(lr, -li, ac[0:1], ac[1:2])
        ar, ai = _scan_chunk(cgr, cgi, ta_ref[0], -ta_ref[1], ta_ref[2], -ta_ref[3], tra_ref[...], agr, agi)
        first = T - 1 if rev else 0
        a_s[0:1] = ar[first:first + 1]
        a_s[1:2] = ai[first:first + 1]
        du_ref[...] = _proj_out(ar, bre_ref) + _proj_out(ai, bim_ref)
        dbre_ref[...] += _outer_acc(u, ar)
        dbim_ref[...] += _outer_acc(u, ai)
        dcre_ref[...] += _outer_acc(dy, hr)
        dcim_ref[...] -= _outer_acc(dy, hi)
        mr, mi = hr - xr, hi - xi
        dlam_ref[0:1] += jnp.sum(ar * mr + ai * mi, axis=0, keepdims=True)
        dlam_ref[1:2] += jnp.sum(ai * mr - ar * mi, axis=0, keepdims=True)

    mspec = _const_spec((NB, KB, NS))
    aspec = _acc_spec((NB, KB, NS))
    ashape = jax.ShapeDtypeStruct((NB, KB, NS), f32)
    return pl.pallas_call(
        body, name=name, grid=(nch,),
        in_specs=[pl.BlockSpec((T, S5W), lambda j: (idx(j), 0)), pl.BlockSpec((T, S5W), lambda j: (idx(j), 0)),
                  pl.BlockSpec((1, 2, GN), lambda j: (nch - 1 - j, 0, 0)), mspec, mspec, mspec, mspec,
                  _const_spec((4, T, GN)), _const_spec((4, T, GN)), _const_spec((2, GN)), _const_spec((T, T)), _const_spec((T, T))],
        out_specs=[pl.BlockSpec((T, S5W), lambda j: (idx(j), 0)), aspec, aspec, aspec, aspec, _acc_spec((2, GN))],
        out_shape=[jax.ShapeDtypeStruct((n, S5W), f32), ashape, ashape, ashape, ashape, jax.ShapeDtypeStruct((2, GN), f32)],
        scratch_shapes=[pltpu.VMEM((2, GN), f32)],
        compiler_params=_CP(dimension_semantics=_ARB),
    )(p_ext, dy_ext, hb, bre, bim, cre, cim, tab_f, tab_a, lam, tri_f, tri_a)


def _rot(t, cosf, sins):
    return t * cosf + pltpu.roll(t, DH // 2, axis=1) * sins


def _rot_t(d, cosf, sins):
    return d * cosf - pltpu.roll(d, DH // 2, axis=1) * sins


def _ret_chunk(qr, kr, v, rp, ld, rev):
    pos = lax.broadcasted_iota(jnp.int32, (T, 1), 0).astype(f32)
    diff = pos - lax.broadcasted_iota(jnp.int32, (1, T), 1).astype(f32)
    if rev:
        keep, dist = diff < 0, jnp.maximum(-diff, 0.0)
        xi, zeta = jnp.exp(ld * (T - pos)), jnp.exp(ld * pos)
    else:
        keep, dist = diff >= 0, jnp.maximum(diff, 0.0)
        xi, zeta = jnp.exp(ld * (pos + 1.0)), jnp.exp(ld * (T - 1.0 - pos))
    dm = jnp.where(keep, jnp.exp(ld * dist), 0.0)
    out = dnn(dnt(qr, kr) * dm, v) + dnn(qr * xi, rp)
    rn = jnp.exp(ld * float(T)) * rp + dtn(kr * zeta, v)
    return out, rn


def _ret_fwd(p_ext, cosf, sins, ld8, rev, nctx, name):
    n = p_ext.shape[0]
    nch = n // T
    idx = (_idx_rev if rev else _idx_fwd)(nctx, nch)
    scale = DH ** -0.5

    def body(q_ref, k_ref, v_ref, cos_ref, sin_ref, ld_ref, o_ref, rp_ref, r_s):
        @pl.when(pl.program_id(0) == 0)
        def _():
            r_s[...] = jnp.zeros_like(r_s)

        cf, ss = cos_ref[...], sin_ref[...]
        for h in range(RH):
            sl = slice(h * DH, (h + 1) * DH)
            qr = _rot(q_ref[:, sl].astype(f32), cf, ss)
            kr = _rot(k_ref[:, sl].astype(f32), cf, ss) * scale
            rp = r_s[h]
            rp_ref[0, h] = rp
            out, rn = _ret_chunk(qr, kr, v_ref[:, sl].astype(f32), rp, ld_ref[h:h + 1, 0:1], rev)
            r_s[h] = rn
            o_ref[:, sl] = out

    def colspec(cb):
        return pl.BlockSpec((T, RW), lambda i, _c=cb: (idx(i), _c))

    tspec = pl.BlockSpec((T, DH), lambda i: (idx(i), 0))
    return pl.pallas_call(
        body, name=name, grid=(nch,),
        in_specs=[colspec(1), colspec(2), colspec(3), tspec, tspec, _const_spec((8, 128))],
        out_specs=[pl.BlockSpec((T, RW), lambda i: (idx(i), 0)), pl.BlockSpec((1, RH, DH, DH), lambda i: (i, 0, 0, 0))],
        out_shape=[jax.ShapeDtypeStruct((n, RW), f32), jax.ShapeDtypeStruct((nch, RH, DH, DH), f32)],
        scratch_shapes=[pltpu.VMEM((RH, DH, DH), f32)],
        compiler_params=_CP(dimension_semantics=_ARB),
    )(p_ext, p_ext, p_ext, cosf, sins, ld8)


def _ret_bwd(p_ext, cosf, sins, ld8, rprev, do_ext, rev, nctx, name):
    n = p_ext.shape[0]
    nch = n // T
    idx0 = (_idx_rev if rev else _idx_fwd)(nctx, nch)
    idx = lambda j: idx0(nch - 1 - j)
    scale = DH ** -0.5

    def body(q_ref, k_ref, v_ref, cos_ref, sin_ref, ld_ref, rp_ref, do_ref, dq_ref, dk_ref, dv_ref, dld_ref, dr_s):
        @pl.when(pl.program_id(0) == 0)
        def _():
            dr_s[...] = jnp.zeros_like(dr_s)
            dld_ref[...] = jnp.zeros_like(dld_ref)

        cf, ss = cos_ref[...], sin_ref[...]
        for h in range(RH):
            sl = slice(h * DH, (h + 1) * DH)
            qr = _rot(q_ref[:, sl].astype(f32), cf, ss)
            kr = _rot(k_ref[:, sl].astype(f32), cf, ss) * scale
            _, vjp = jax.vjp(functools.partial(_ret_chunk, rev=rev), qr, kr, v_ref[:, sl].astype(f32), rp_ref[0, h],
                             ld_ref[h:h + 1, 0:1])
            dqr, dkr, dv, drp, dld = vjp((do_ref[:, sl], dr_s[h]))
            dr_s[h] = drp
            dq_ref[:, sl] = _rot_t(dqr, cf, ss)
            dk_ref[:, sl] = _rot_t(dkr, cf, ss) * scale
            dv_ref[:, sl] = dv
            dld_ref[h:h + 1, :] += jnp.broadcast_to(dld, (1, 128))

    def colspec(cb):
        return pl.BlockSpec((T, RW), lambda j, _c=cb: (idx(j), _c))

    tspec = pl.BlockSpec((T, DH), lambda j: (idx(j), 0))
    ospec = pl.BlockSpec((T, RW), lambda j: (idx(j), 0))
    oshape = jax.ShapeDtypeStruct((n, RW), f32)
    return pl.pallas_call(
        body, name=name, grid=(nch,),
        in_specs=[colspec(1), colspec(2), colspec(3), tspec, tspec, _const_spec((8, 128)),
                  pl.BlockSpec((1, RH, DH, DH), lambda j: (nch - 1 - j, 0, 0, 0)), ospec],
        out_specs=[ospec, ospec, ospec, _acc_spec((8, 128))],
        out_shape=[oshape, oshape, oshape, jax.ShapeDtypeStruct((8, 128), f32)],
        scratch_shapes=[pltpu.VMEM((RH, DH, DH), f32)],
        compiler_params=_CP(dimension_semantics=_ARB),
    )(p_ext, p_ext, p_ext, cosf, sins, ld8, rprev, do_ext)


def _f1_fn(xin, nw, sh, sc, w):
    h = _mod(_rms(xin, nw), sh, sc)
    return dnn(h, w), h


def _f1_fwd(x, ctx, modx, modc, nw1, w_in, name):
    L = x.shape[0]
    nb = L // R + 1

    def body(x_ref, c_ref, mx_ref, mc_ref, nw_ref, w_ref, p_ref):
        is_ctx = pl.program_id(0) == 0
        xin = jnp.where(is_ctx, c_ref[...], x_ref[...])
        sh = jnp.where(is_ctx, mc_ref[0:1], mx_ref[0:1])
        sc = jnp.where(is_ctx, mc_ref[1:2], mx_ref[1:2])
        p_ref[...] = _f1_fn(xin, nw_ref[...], sh, sc, w_ref[...])[0].astype(bf16)

    return pl.pallas_call(
        body, name=name, grid=(nb,),
        in_specs=[pl.BlockSpec((R, D), lambda i: (jnp.maximum(i - 1, 0), 0)), _const_spec((R, D)), _const_spec((6, D)),
                  _const_spec((6, D)), _const_spec((1, D)), _const_spec((D, INC))],
        out_specs=pl.BlockSpec((R, INC), lambda i: (i, 0)),
        out_shape=jax.ShapeDtypeStruct((L + R, INC), bf16),
        compiler_params=_CP(dimension_semantics=("parallel",)),
    )(x, ctx, modx, modc, nw1, w_in)


def _f1_bwd(x, ctx, modx, modc, nw1, w_in, dx1, parts, name):
    L = x.shape[0]
    nb = L // R + 1

    def body(x_ref, c_ref, mx_ref, mc_ref, nw_ref, w_ref, dx1_ref, du0, du1, du2, dq0, dq1, dk0, dk1, dv0, dv1, dg0,
             gx_ref, dp_ref, h1_ref, dnw_ref, dmx_ref, dmc_ref):
        i = pl.program_id(0)
        is_ctx = i == 0

        @pl.when(is_ctx)
        def _():
            dnw_ref[...] = jnp.zeros_like(dnw_ref)
            dmx_ref[...] = jnp.zeros_like(dmx_ref)
            dmc_ref[...] = jnp.zeros_like(dmc_ref)

        dp = jnp.concatenate([du0[...] + du1[...] + du2[...], dq0[...] + dq1[...], dk0[...] + dk1[...], dv0[...] + dv1[...],
                              dg0[...]], axis=1).astype(bf16)
        dp_ref[...] = dp
        xin = jnp.where(is_ctx, c_ref[...], x_ref[...])
        sh = jnp.where(is_ctx, mc_ref[0:1], mx_ref[0:1])
        sc = jnp.where(is_ctx, mc_ref[1:2], mx_ref[1:2])
        dh = dnt(dp, w_ref[...])
        h, vjp = jax.vjp(lambda a, b, c, d: _mod(_rms(a, b), c, d), xin, nw_ref[...], sh, sc)
        dxin, dnw, dsh, dsc = vjp(dh)
        h1_ref[...] = h.astype(bf16)
        gx_ref[...] = dx1_ref[...] + dxin
        dnw_ref[...] += dnw
        wx = jnp.where(is_ctx, 0.0, 1.0)
        dmx_ref[0:1] += dsh * wx
        dmx_ref[1:2] += dsc * wx
        dmc_ref[0:1] += dsh * (1.0 - wx)
        dmc_ref[1:2] += dsc * (1.0 - wx)

    lat = pl.BlockSpec((R, D), lambda i: (jnp.maximum(i - 1, 0), 0))
    ext = pl.BlockSpec((R, S5W), lambda i: (i, 0))
    return pl.pallas_call(
        body, name=name, grid=(nb,),
        in_specs=[lat, _const_spec((R, D)), _const_spec((6, D)), _const_spec((6, D)), _const_spec((1, D)), _const_spec((D, INC)),
                  lat] + [ext] * 10,
        out_specs=[lat, pl.BlockSpec((R, INC), lambda i: (i, 0)), pl.BlockSpec((R, D), lambda i: (i, 0)),
                   _acc_spec((1, D)), _acc_spec((6, D)), _acc_spec((6, D))],
        out_shape=[jax.ShapeDtypeStruct((L, D), f32), jax.ShapeDtypeStruct((L + R, INC), bf16),
                   jax.ShapeDtypeStruct((L + R, D), bf16), jax.ShapeDtypeStruct((1, D), f32),
                   jax.ShapeDtypeStruct((6, D), f32), jax.ShapeDtypeStruct((6, D), f32)],
        compiler_params=_CP(dimension_semantics=_ARB),
    )(x, ctx, modx, modc, nw1, w_in, dx1, *parts)


def _ret_post(yr, g):
    outs = []
    for h in range(RH):
        yh = yr[:, h * DH:(h + 1) * DH]
        mu = jnp.mean(yh, axis=-1, keepdims=True)
        var = jnp.mean((yh - mu) ** 2, axis=-1, keepdims=True)
        outs.append((yh - mu) * lax.rsqrt(var + EPS))
    return jax.nn.silu(g) * jnp.concatenate(outs, axis=1)


def _mix_fn(yf, yb, u, of, ob, g, x, dvec, bglu, gate1, pz, pm, wglu, wout):
    y = yf + yb + dvec * u
    s = jax.nn.gelu(y)
    z = dnn(s, wglu) + bglu + pz
    cat = jnp.concatenate([s * jax.nn.sigmoid(z), _ret_post(of + ob, g)], axis=1)
    mix = dnn(cat, wout) + pm
    return x + gate1 * mix, (s, cat)


def _mix_fwd(x, yf, yb, of, ob, p_ext, dvec, bglu, modx, wglu, wout, name):
    L = x.shape[0]

    def body(x_ref, yf_ref, yb_ref, of_ref, ob_ref, u_ref, g_ref, d_ref, b_ref, mx_ref, wg_ref, wo_ref, x1_ref):
        x1_ref[...] = _mix_fn(yf_ref[...], yb_ref[...], u_ref[...].astype(f32), of_ref[...], ob_ref[...], g_ref[...].astype(f32),
                              x_ref[...], d_ref[...], b_ref[...], mx_ref[2:3], 0.0, 0.0, wg_ref[...], wo_ref[...])[0]

    ext = pl.BlockSpec((R, S5W), lambda i: (i + 1, 0))
    return pl.pallas_call(
        body, name=name, grid=(L // R,),
        in_specs=[pl.BlockSpec((R, D), lambda i: (i, 0)), ext, ext, ext, ext, ext, pl.BlockSpec((R, RW), lambda i: (i + 1, 4)),
                  _const_spec((1, S5W)), _const_spec((1, S5W)), _const_spec((6, D)), _const_spec((S5W, S5W)), _const_spec((D, D))],
        out_specs=pl.BlockSpec((R, D), lambda i: (i, 0)),
        out_shape=jax.ShapeDtypeStruct((L, D), f32),
        compiler_params=_CP(dimension_semantics=("parallel",)),
    )(x, yf, yb, of, ob, p_ext, p_ext, dvec, bglu, modx, wglu, wout)


def _mix_bwd(x, yf, yb, of, ob, p_ext, dvec, bglu, modx, wglu, wout, dx1, name):
    L = x.shape[0]
    nb = L // R + 1

    def body(x_ref, yf_ref, yb_ref, of_ref, ob_ref, u_ref, g_ref, d_ref, b_ref, mx_ref, wg_ref, wo_ref, dx1_ref,
             dy_ref, dud_ref, do_ref, dg_ref, cat_ref, dmix_ref, s_ref, dz_ref, dd_ref, db_ref, dg1_ref):
        i = pl.program_id(0)

        @pl.when(i == 0)
        def _():
            for r in (dy_ref, dud_ref, do_ref, dg_ref, cat_ref, dmix_ref, s_ref, dz_ref, dd_ref, db_ref, dg1_ref):
                r[...] = jnp.zeros_like(r)

        @pl.when(i > 0)
        def _():
            fn = lambda yf_, u_, of_, g_, d_, b_, g1_, pz_, pm_: _mix_fn(
                yf_, yb_ref[...], u_, of_, ob_ref[...], g_, x_ref[...], d_, b_, g1_, pz_, pm_, wg_ref[...], wo_ref[...])
            _, vjp, (s, cat) = jax.vjp(fn, yf_ref[...], u_ref[...].astype(f32), of_ref[...], g_ref[...].astype(f32), d_ref[...],
                                       b_ref[...], mx_ref[2:3], jnp.zeros((R, S5W), f32), jnp.zeros((R, D), f32), has_aux=True)
            dy, dud, do, dg, dd, db, dg1, dz, dmix = vjp(dx1_ref[...])
            dy_ref[...], dud_ref[...], do_ref[...], dg_ref[...] = dy, dud, do, dg
            cat_ref[...], dmix_ref[...] = cat.astype(bf16), dmix.astype(bf16)
            s_ref[...], dz_ref[...] = s.astype(bf16), dz.astype(bf16)
            dd_ref[...] += dd
            db_ref[...] += db
            dg1_ref[...] += dg1

    lat = pl.BlockSpec((R, D), lambda i: (jnp.maximum(i - 1, 0), 0))
    lat5 = pl.BlockSpec((R, S5W), lambda i: (jnp.maximum(i - 1, 0), 0))
    ext = pl.BlockSpec((R, S5W), lambda i: (i, 0))
    eshape = jax.ShapeDtypeStruct((L + R, S5W), f32)
    return pl.pallas_call(
        body, name=name, grid=(nb,),
        in_specs=[lat, ext, ext, ext, ext, ext, pl.BlockSpec((R, RW), lambda i: (i, 4)),
                  _const_spec((1, S5W)), _const_spec((1, S5W)), _const_spec((6, D)), _const_spec((S5W, S5W)), _const_spec((D, D)), lat],
        out_specs=[ext, ext, ext, ext, lat, lat, lat5, lat5, _acc_spec((1, S5W)), _acc_spec((1, S5W)), _acc_spec((1, D))],
        out_shape=[eshape, eshape, eshape, eshape, jax.ShapeDtypeStruct((L, D), bf16), jax.ShapeDtypeStruct((L, D), bf16),
                   jax.ShapeDtypeStruct((L, S5W), bf16), jax.ShapeDtypeStruct((L, S5W), bf16),
                   jax.ShapeDtypeStruct((1, S5W), f32), jax.ShapeDtypeStruct((1, S5W), f32), jax.ShapeDtypeStruct((1, D), f32)],
        compiler_params=_CP(dimension_semantics=_ARB),
    )(x, yf, yb, of, ob, p_ext, p_ext, dvec, bglu, modx, wglu, wout, dx1)


def _ffn_tail(gc, a, x1, gate2, fnw, pf, wdown, tgt):
    f = jax.nn.gelu(gc) * a
    ffn = dnn(f, wdown) + pf
    y = _rms(x1 + gate2 * ffn, fnw)
    err = y - tgt
    loss = 0.5 * jnp.sum(jnp.mean(err * err, axis=-1, keepdims=True), axis=0, keepdims=True)
    return loss, f


def _ffn_fwd(x1, tgt, nw2, modx, wup, cw, cb, wdown, fnw, name):
    L = x1.shape[0]
    nb = L // RF
    per = RF // HALO

    def body(x_ref, xp_ref, xn_ref, t_ref, nw_ref, mx_ref, wu_ref, cw_ref, cb_ref, wd_ref, fn_ref,
             dx2_ref, da_ref, dgc_ref, f_ref, dffn_ref, loss_ref, dfn_ref, dg2_ref, dcb_ref, dcw_ref):
        i = pl.program_id(0)

        @pl.when(i == 0)
        def _():
            for r in (loss_ref, dfn_ref, dg2_ref, dcb_ref, dcw_ref):
                r[...] = jnp.zeros_like(r)

        nw, sh, sc, gate2 = nw_ref[...], mx_ref[3:4], mx_ref[4:5], mx_ref[5:6]
        x1b = x_ref[...]
        ag = dnn(_mod(_rms(x1b, nw), sh, sc), wu_ref[...])
        a, g = ag[:, :DFF], ag[:, DFF:]
        wug = wu_ref[:, DFF:]
        gp = dnn(_mod(_rms(xp_ref[...], nw), sh, sc), wug)[HALO - 1:HALO] * jnp.where(i > 0, 1.0, 0.0)
        gn = dnn(_mod(_rms(xn_ref[...], nw), sh, sc), wug)[0:1] * jnp.where(i < nb - 1, 1.0, 0.0)
        row = lax.broadcasted_iota(jnp.int32, (RF, 1), 0)
        g_prev = jnp.where(row == 0, gp, pltpu.roll(g, 1, axis=0))
        g_next = jnp.where(row == RF - 1, gn, pltpu.roll(g, RF - 1, axis=0))
        gc = cb_ref[...] + g_prev * cw_ref[0:1] + g * cw_ref[1:2] + g_next * cw_ref[2:3]
        fn = lambda gc_, a_, x_, g2_, fw_, pf_: _ffn_tail(gc_, a_, x_, g2_, fw_, pf_, wd_ref[...], t_ref[...])
        loss, vjp, f = jax.vjp(fn, gc, a, x1b, gate2, fn_ref[...], jnp.zeros((RF, D), f32), has_aux=True)
        dgc, da, dx2, dg2, dfw, dffn = vjp(jnp.ones((1, 1), f32))
        dx2_ref[...] = dx2
        da_ref[...], dgc_ref[...] = da.astype(bf16), dgc
        f_ref[...], dffn_ref[...] = f.astype(bf16), dffn.astype(bf16)
        loss_ref[...] += jnp.broadcast_to(loss, (1, 128))
        dfn_ref[...] += dfw
        dg2_ref[...] += dg2
        dcb_ref[...] += jnp.sum(dgc, axis=0, keepdims=True)
        dcw_ref[0:1] += jnp.sum(dgc * g_prev, axis=0, keepdims=True)
        dcw_ref[1:2] += jnp.sum(dgc * g, axis=0, keepdims=True)
        dcw_ref[2:3] += jnp.sum(dgc * g_next, axis=0, keepdims=True)

    blk = lambda w: pl.BlockSpec((RF, w), lambda i: (i, 0))
    return pl.pallas_call(
        body, name=name, grid=(nb,),
        in_specs=[blk(D), pl.BlockSpec((HALO, D), lambda i: (jnp.maximum(i * per - 1, 0), 0)),
                  pl.BlockSpec((HALO, D), lambda i: (jnp.minimum((i + 1) * per, L // HALO - 1), 0)), blk(D),
                  _const_spec((1, D)), _const_spec((6, D)), _const_spec((D, 2 * DFF)), _const_spec((3, DFF)), _const_spec((1, DFF)),
                  _const_spec((DFF, D)), _const_spec((1, D))],
        out_specs=[blk(D), blk(DFF), blk(DFF), blk(DFF), blk(D), _acc_spec((1, 128)), _acc_spec((1, D)), _acc_spec((1, D)),
                   _acc_spec((1, DFF)), _acc_spec((3, DFF))],
        out_shape=[jax.ShapeDtypeStruct((L, D), f32), jax.ShapeDtypeStruct((L, DFF), bf16), jax.ShapeDtypeStruct((L, DFF), f32),
                   jax.ShapeDtypeStruct((L, DFF), bf16), jax.ShapeDtypeStruct((L, D), bf16), jax.ShapeDtypeStruct((1, 128), f32),
                   jax.ShapeDtypeStruct((1, D), f32), jax.ShapeDtypeStruct((1, D), f32), jax.ShapeDtypeStruct((1, DFF), f32),
                   jax.ShapeDtypeStruct((3, DFF), f32)],
        compiler_params=_CP(dimension_semantics=_ARB),
    )(x1, x1, x1, tgt, nw2, modx, wup, cw, cb, wdown, fnw)


def _ffn_bwd(x1, dx2, da, dgc, nw2, modx, wup, cw, name):
    L = x1.shape[0]
    nb = L // RF
    per = RF // HALO

    def body(x_ref, dx2_ref, da_ref, dgc_ref, dgp_ref, dgn_ref, nw_ref, mx_ref, wu_ref, cw_ref,
             dx1_ref, dag_ref, h2_ref, dnw_ref, dmx_ref):
        i = pl.program_id(0)

        @pl.when(i == 0)
        def _():
            dnw_ref[...] = jnp.zeros_like(dnw_ref)
            dmx_ref[...] = jnp.zeros_like(dmx_ref)

        dgc_b = dgc_ref[...]
        before = dgp_ref[HALO - 1:HALO] * jnp.where(i > 0, 1.0, 0.0)
        after = dgn_ref[0:1] * jnp.where(i < nb - 1, 1.0, 0.0)
        row = lax.broadcasted_iota(jnp.int32, (RF, 1), 0)
        d_prev = jnp.where(row == 0, before, pltpu.roll(dgc_b, 1, axis=0))
        d_next = jnp.where(row == RF - 1, after, pltpu.roll(dgc_b, RF - 1, axis=0))
        dg = cw_ref[0:1] * d_next + cw_ref[1:2] * dgc_b + cw_ref[2:3] * d_prev
        dag = jnp.concatenate([da_ref[...], dg.astype(bf16)], axis=1)
        dag_ref[...] = dag
        dh2 = dnt(dag, wu_ref[...])
        h2, vjp = jax.vjp(lambda a, b, c, d: _mod(_rms(a, b), c, d), x_ref[...], nw_ref[...], mx_ref[3:4], mx_ref[4:5])
        dxa, dnw, dsh, dsc = vjp(dh2)
        h2_ref[...] = h2.astype(bf16)
        dx1_ref[...] = dx2_ref[...] + dxa
        dnw_ref[...] += dnw
        dmx_ref[3:4] += dsh
        dmx_ref[4:5] += dsc

    blk = lambda w: pl.BlockSpec((RF, w), lambda i: (i, 0))
    return pl.pallas_call(
        body, name=name, grid=(nb,),
        in_specs=[blk(D), blk(D), blk(DFF), blk(DFF), pl.BlockSpec((HALO, DFF), lambda i: (jnp.maximum(i * per - 1, 0), 0)),
                  pl.BlockSpec((HALO, DFF), lambda i: (jnp.minimum((i + 1) * per, L // HALO - 1), 0)),
                  _const_spec((1, D)), _const_spec((6, D)), _const_spec((D, 2 * DFF)), _const_spec((3, DFF))],
        out_specs=[blk(D), blk(2 * DFF), blk(D), _acc_spec((1, D)), _acc_spec((6, D))],
        out_shape=[jax.ShapeDtypeStruct((L, D), f32), jax.ShapeDtypeStruct((L, 2 * DFF), bf16), jax.ShapeDtypeStruct((L, D), bf16),
                   jax.ShapeDtypeStruct((1, D), f32), jax.ShapeDtypeStruct((6, D), f32)],
        compiler_params=_CP(dimension_semantics=_ARB),
    )(x1, dx2, da, dgc, dgc, dgc, nw2, modx, wup, cw)


def _tile(n, cands):
    for c in cands:
        if n % c == 0:
            return c
    return n


def _matmul_tn(a, b, name):
    k, m = a.shape
    n = b.shape[1]
    tm, tn, tk = _tile(m, (512, 256, 128)), _tile(n, (512, 256, 128)), _tile(k, (1024, 512, 256))

    def body(a_ref, b_ref, o_ref):
        @pl.when(pl.program_id(2) == 0)
        def _():
            o_ref[...] = jnp.zeros_like(o_ref)

        o_ref[...] += dtn(a_ref[...], b_ref[...])

    return pl.pallas_call(
        body, name=name, grid=(m // tm, n // tn, k // tk),
        in_specs=[pl.BlockSpec((tk, tm), lambda i, j, q: (q, i)), pl.BlockSpec((tk, tn), lambda i, j, q: (q, j))],
        out_specs=pl.BlockSpec((tm, tn), lambda i, j, q: (i, j)),
        out_shape=jax.ShapeDtypeStruct((m, n), f32),
        compiler_params=_CP(dimension_semantics=("parallel", "parallel", "arbitrary")),
    )(a, b)


def _adamw(w, g, m, v, name):
    c1, c2 = 1.0 - B1 ** STEP, 1.0 - B2 ** STEP

    def body(w_ref, g_ref, m_ref, v_ref, d_ref, nm_ref, nv_ref):
        gg = g_ref[...]
        nm = B1 * m_ref[...] + (1.0 - B1) * gg
        nv = B2 * v_ref[...] + (1.0 - B2) * jnp.square(gg)
        d_ref[...] = -LR * ((nm / c1) / (jnp.sqrt(nv / c2) + AEPS) + WD * w_ref[...])
        nm_ref[...], nv_ref[...] = nm, nv

    return pl.pallas_call(body, name=name, out_shape=[jax.ShapeDtypeStruct(w.shape, f32)] * 3, compiler_params=_CP())(w, g, m, v)


SMALL = ["conv_w", "c_ctx", "norm1_w", "s5_lambda_re_f", "s5_lambda_im_f", "s5_log_step_f", "s5_lambda_re_b", "s5_lambda_im_b",
         "s5_log_step_b", "s5_b_re", "s5_b_im", "s5_c_re", "s5_c_im", "s5_d", "s5_b_glu", "ret_log_decay_f", "ret_log_decay_b",
         "norm2_w", "conv_b", "final_norm_w"]
BIG = ["w_in", "w_out", "w_up", "w_down", "s5_w_glu"]
WEIGHTS = ["c_ctx", "w_mod", "b_mod", "norm1_w", "w_in", "s5_lambda_re_f", "s5_lambda_im_f", "s5_log_step_f", "s5_lambda_re_b",
           "s5_lambda_im_b", "s5_log_step_b", "s5_b_re", "s5_b_im", "s5_c_re", "s5_c_im", "s5_d", "s5_w_glu", "s5_b_glu",
           "ret_log_decay_f", "ret_log_decay_b", "w_out", "norm2_w", "w_up", "conv_w", "conv_b", "w_down", "final_norm_w"]


def _pack_small(vals):
    flat, offs, o = [], [], 0
    for a in vals:
        n = a.size
        npad = -n % 128
        flat.append(jnp.pad(a.reshape(-1), (0, npad)))
        offs.append((o, n))
        o += n + npad
    tail = -o % 1024
    if tail:
        flat.append(jnp.zeros((tail,), f32))
    return jnp.concatenate(flat).reshape(-1, 128), offs


def _unpack_small(packed, offs, shapes):
    flat = packed.reshape(-1)
    return [flat[o:o + n].reshape(s) for (o, n), s in zip(offs, shapes)]


def _cols_to_blocks(w, per):
    rows = w.shape[0]
    return w.reshape(rows, NDEV, per).transpose(1, 0, 2).reshape(NDEV, rows * per // D, D)


def _blocks_to_cols(b, rows, per):
    return b.reshape(NDEV, rows, per).transpose(1, 0, 2).reshape(rows, NDEV * per)


def _rope_tables(L, nctx_rows):
    t = np.arange(L)
    inv = ROPE_THETA ** (-np.arange(DH // 4, dtype=np.float64) / (DH // 4))
    ang = np.concatenate([(t // GRID_W).astype(np.float32)[:, None] * inv.astype(np.float32),
                          (t % GRID_W).astype(np.float32)[:, None] * inv.astype(np.float32)], axis=-1)
    cos, sin = np.cos(ang).astype(np.float32), np.sin(ang).astype(np.float32)
    cosf = np.concatenate([np.ones((nctx_rows, DH), np.float32), np.concatenate([cos, cos], axis=1)], axis=0)
    sins = np.concatenate([np.zeros((nctx_rows, DH), np.float32), np.concatenate([-sin, sin], axis=1)], axis=0)
    return jnp.asarray(cosf), jnp.asarray(sins)


def _qk_perm():
    head = np.concatenate([np.arange(0, DH, 2), np.arange(1, DH, 2)])
    perm = np.arange(INC)
    for base in (S5W, S5W + RW):
        for h in range(RH):
            perm[base + h * DH: base + (h + 1) * DH] = base + h * DH + head
    return perm, np.argsort(perm)


def kernel(x, c, ctx, c_ctx, w_mod, b_mod, norm1_w, w_in, s5_lambda_re_f, s5_lambda_im_f, s5_log_step_f, s5_lambda_re_b, s5_lambda_im_b, s5_log_step_b, s5_b_re, s5_b_im, s5_c_re, s5_c_im, s5_d, s5_w_glu, s5_b_glu, ret_log_decay_f, ret_log_decay_b, w_out, norm2_w, w_up, conv_w, conv_b, w_down, final_norm_w, loss_target, m_c_ctx, m_w_mod, m_b_mod, m_norm1_w, m_w_in, m_s5_lambda_re_f, m_s5_lambda_im_f, m_s5_log_step_f, m_s5_lambda_re_b, m_s5_lambda_im_b, m_s5_log_step_b, m_s5_b_re, m_s5_b_im, m_s5_c_re, m_s5_c_im, m_s5_d, m_s5_w_glu, m_s5_b_glu, m_ret_log_decay_f, m_ret_log_decay_b, m_w_out, m_norm2_w, m_w_up, m_conv_w, m_conv_b, m_w_down, m_final_norm_w, v_c_ctx, v_w_mod, v_b_mod, v_norm1_w, v_w_in, v_s5_lambda_re_f, v_s5_lambda_im_f, v_s5_log_step_f, v_s5_lambda_re_b, v_s5_lambda_im_b, v_s5_log_step_b, v_s5_b_re, v_s5_b_im, v_s5_c_re, v_s5_c_im, v_s5_d, v_s5_w_glu, v_s5_b_glu, v_ret_log_decay_f, v_ret_log_decay_b, v_w_out, v_norm2_w, v_w_up, v_conv_w, v_conv_b, v_w_down, v_final_norm_w):
    args = dict(locals())
    W = {n: args[n] for n in WEIGHTS}
    M = {n: args["m_" + n] for n in WEIGHTS}
    V = {n: args["v_" + n] for n in WEIGHTS}
    me = _me()
    x2, ctx2, tgt = x[0], ctx[0], loss_target[0]
    L, Lc = x2.shape[0], ctx2.shape[0]
    assert Lc == R and L % R == 0 and L % GRID_W == 0
    nctx = Lc // T

    c_all = _all_gather_small(jnp.pad(c, ((0, 7), (0, 0))), "gather_c")[:, 0, :]
    c9 = jnp.concatenate([c_all, c_ctx[None], jnp.zeros((7, D), f32)], axis=0)
    w_mod_l = w_mod[0]
    ncol = w_mod_l.shape[1]
    m_part = _ada_fwd(c9, w_mod_l, "ada_fwd")
    m_all = _all_gather_small(m_part, "gather_mod").transpose(1, 0, 2).reshape(16, 6, D)
    modx, modc = _mod_select(m_all, b_mod.reshape(6, D), "mod_select")

    per_in, per_up = w_in.shape[2], w_up.shape[2]
    pieces = [w_in[0].astype(bf16).reshape(-1, D), w_out[0].astype(bf16), w_up[0].astype(bf16).reshape(-1, D),
              w_down[0].astype(bf16), s5_w_glu[0].astype(bf16).reshape(-1, D)]
    rows = [p.shape[0] for p in pieces]
    offs = np.concatenate([[0], np.cumsum(rows)])
    wg = _all_gather_hbm(jnp.concatenate(pieces, axis=0), "gather_weights")
    seg = lambda k: wg[:, offs[k]:offs[k + 1], :]
    perm, inv_perm = _qk_perm()
    w_in_f = _blocks_to_cols(seg(0), D, per_in)[:, perm]
    w_out_f = seg(1).reshape(D, D)
    w_up_f = _blocks_to_cols(seg(2), D, per_up)
    w_down_f = seg(3).reshape(DFF, D)
    w_glu_f = seg(4).reshape(S5W, S5W)
    conv_pad = jnp.pad(conv_w[0], ((0, 5), (0, 128 * 3 - conv_w.shape[2])))
    conv_f = _all_gather_small(conv_pad, "gather_conv")[:, :3, :conv_w.shape[2]].transpose(1, 0, 2).reshape(3, DFF)

    gn = lambda a: a[0]
    bre_t, bim_t = gn(s5_b_re).transpose(2, 0, 1), gn(s5_b_im).transpose(2, 0, 1)
    cre_m, cim_m = _blockdiag(gn(s5_c_re)), _blockdiag(gn(s5_c_im))
    tril = jnp.tril(jnp.ones((T, T), bf16))
    s5 = {}
    for tag, lre, lim, ls in (("f", s5_lambda_re_f, s5_lambda_im_f, s5_log_step_f), ("b", s5_lambda_re_b, s5_lambda_im_b, s5_log_step_b)):
        lam, bbr, bbi, tab = _s5_prep(gn(lre), gn(lim), gn(ls).reshape(S5G, 1), bre_t, bim_t, "s5_prep_" + tag)
        tab = tab.reshape(4, T, GN)
        s5[tag] = dict(lam=lam.reshape(2, GN), tab=tab, tab_flip=tab[:, ::-1, :],
                       mats=(_blockdiag(bbr.transpose(1, 0, 2)), _blockdiag(bbi.transpose(1, 0, 2)), cre_m, cim_m))

    nw1, nw2, fnw = norm1_w, norm2_w, final_norm_w[None]
    p_ext = _f1_fwd(x2, ctx2, modx, modc, nw1, w_in_f, "f1_fwd")
    yf, hb_f = _s5_fwd(p_ext, s5["f"]["mats"], s5["f"]["tab"], s5["f"]["lam"], tril, False, nctx, "s5_fwd_f")
    yb, hb_b = _s5_fwd(p_ext, s5["b"]["mats"], s5["b"]["tab_flip"], s5["b"]["lam"], tril.T, True, nctx, "s5_fwd_b")
    cosf, sins = _rope_tables(L, Lc)
    ld8 = lambda ld: jnp.pad(jnp.broadcast_to(ld[0][:, None], (RH, 128)), ((0, 8 - RH), (0, 0)))
    ldf8, ldb8 = ld8(ret_log_decay_f), ld8(ret_log_decay_b)
    of, rp_f = _ret_fwd(p_ext, cosf, sins, ldf8, False, nctx, "ret_fwd_f")
    ob, rp_b = _ret_fwd(p_ext, cosf, sins, ldb8, True, nctx, "ret_fwd_b")
    x1 = _mix_fwd(x2, yf, yb, of, ob, p_ext, s5_d, s5_b_glu, modx, w_glu_f, w_out_f, "mix_fwd")

    (dx2, da, dgc, f_act, dffn, loss_acc, g_fnw, g_gate2, g_cb, g_cw) = _ffn_fwd(
        x1, tgt, nw2, modx, w_up_f, conv_f, conv_b, w_down_f, fnw, "ffn_fwd")
    dx1, dag, h2, g_nw2, dmx2 = _ffn_bwd(x1, dx2, da, dgc, nw2, modx, w_up_f, conv_f, "ffn_bwd")
    gw_down = _matmul_tn(f_act, dffn, "dw_down")
    gw_up = _matmul_tn(h2, dag, "dw_up")
    (dy_e, dud_e, do_e, dg_e, cat, dmix, s_act, dz, g_d, g_bglu, g_gate1) = _mix_bwd(
        x2, yf, yb, of, ob, p_ext, s5_d, s5_b_glu, modx, w_glu_f, w_out_f, dx1, "mix_bwd")
    gw_out = _matmul_tn(cat, dmix, "dw_out")
    gw_glu = _matmul_tn(s_act, dz, "dw_glu")
    dq_f, dk_f, dv_f, gld_f = _ret_bwd(p_ext, cosf, sins, ldf8, rp_f, do_e, False, nctx, "ret_bwd_f")
    dq_b, dk_b, dv_b, gld_b = _ret_bwd(p_ext, cosf, sins, ldb8, rp_b, do_e, True, nctx, "ret_bwd_b")
    du_f, dbre_f, dbim_f, dcre_f, dcim_f, dl_f = _s5_bwd(
        p_ext, dy_e, hb_f, s5["f"]["mats"], s5["f"]["tab"], s5["f"]["tab_flip"], s5["f"]["lam"], tril, tril.T, False, nctx, "s5_bwd_f")
    du_b, dbre_b, dbim_b, dcre_b, dcim_b, dl_b = _s5_bwd(
        p_ext, dy_e, hb_b, s5["b"]["mats"], s5["b"]["tab_flip"], s5["b"]["tab"], s5["b"]["lam"], tril.T, tril, True, nctx, "s5_bwd_b")
    grad_x, dp_ext, h1, g_nw1, dmx1, dmc1 = _f1_bwd(
        x2, ctx2, modx, modc, nw1, w_in_f, dx1, (du_f, du_b, dud_e, dq_f, dq_b, dk_f, dk_b, dv_f, dv_b, dg_e), "f1_bwd")
    gw_in = _matmul_tn(h1, dp_ext, "dw_in")[:, inv_perm]

    tg = lambda m: _blockdiag_extract(m).transpose(1, 0, 2)
    gs5 = {}
    for tag, lre, lim, ls, dl, dbr, dbi in (("f", s5_lambda_re_f, s5_lambda_im_f, s5_log_step_f, dl_f, dbre_f, dbim_f),
                                             ("b", s5_lambda_re_b, s5_lambda_im_b, s5_log_step_b, dl_b, dbre_b, dbim_b)):
        gs5[tag] = _s5_prep_bwd(gn(lre), gn(lim), gn(ls).reshape(S5G, 1), bre_t, bim_t, dl.reshape(2, S5G, S5N), tg(dbr), tg(dbi),
                                "s5_prep_bwd_" + tag)
    g_bre = (gs5["f"][3] + gs5["b"][3]).transpose(1, 2, 0)
    g_bim = (gs5["f"][4] + gs5["b"][4]).transpose(1, 2, 0)
    g_cre = _blockdiag_extract(dcre_f + dcre_b)
    g_cim = _blockdiag_extract(dcim_f + dcim_b)

    dmx = dmx1 + dmx2
    dmx = dmx.at[2].set(g_gate1[0]).at[5].set(g_gate2[0])
    dm_me = jnp.stack([dmx.reshape(-1), dmc1.reshape(-1)], axis=0)
    dm_all = _all_gather_small(jnp.pad(dm_me, ((0, 6), (0, 0))), "gather_dmod")
    dmx_all, dmc_all = dm_all[:, 0, :], dm_all[:, 1, :]
    my_cols = lambda a: lax.dynamic_slice(a, (0, me * ncol), (NDEV, ncol))
    gw_mod, g_bmod, dc9 = _ada_bwd(c9, dmx_all, dmc_all, my_cols(dmx_all), my_cols(dmc_all), w_mod_l, "ada_bwd")

    small = {
        "conv_w": g_cw, "c_ctx": dc9[8], "norm1_w": g_nw1, "s5_lambda_re_f": gs5["f"][0], "s5_lambda_im_f": gs5["f"][1], "s5_log_step_f": gs5["f"][2],
        "s5_lambda_re_b": gs5["b"][0], "s5_lambda_im_b": gs5["b"][1], "s5_log_step_b": gs5["b"][2], "s5_b_re": g_bre, "s5_b_im": g_bim,
        "s5_c_re": g_cre, "s5_c_im": g_cim, "s5_d": g_d, "s5_b_glu": g_bglu, "ret_log_decay_f": gld_f[:RH, 0],
        "ret_log_decay_b": gld_b[:RH, 0], "norm2_w": g_nw2, "conv_b": g_cb, "final_norm_w": g_fnw,
    }
    packed, soffs = _pack_small([small[n].astype(f32) for n in SMALL])
    red = _all_reduce_small(packed, "reduce_small")
    sshapes = [(3, DFF) if n == "conv_w" else W[n].shape for n in SMALL]
    G = dict(zip(SMALL, _unpack_small(red, soffs, sshapes)))
    per_cv = conv_w.shape[2]
    G["conv_w"] = lax.dynamic_slice(G["conv_w"], (0, me * per_cv), (3, per_cv))[None]
    G["b_mod"] = g_bmod.reshape(b_mod.shape)
    G["w_mod"] = gw_mod[None]

    gpieces = [_cols_to_blocks(gw_in, per_in), gw_out.reshape(NDEV, -1, D), _cols_to_blocks(gw_up, per_up),
               gw_down.reshape(NDEV, -1, D), gw_glu.reshape(NDEV, -1, D)]
    grows = [p.shape[1] for p in gpieces]
    goffs = np.concatenate([[0], np.cumsum(grows)])
    land = _scatter_hbm(jnp.concatenate(gpieces, axis=1).astype(bf16), "scatter_grads")
    gsum = _sum8(land, "sum_grads")
    gseg = lambda k: gsum[goffs[k]:goffs[k + 1]]
    G["w_in"] = gseg(0).reshape(1, D, per_in)
    G["w_out"] = gseg(1)[None]
    G["w_up"] = gseg(2).reshape(1, D, per_up)
    G["w_down"] = gseg(3)[None]
    G["s5_w_glu"] = gseg(4).reshape(s5_w_glu.shape)

    delta, new_m, new_v = {}, {}, {}
    sm_names = SMALL[1:] + ["b_mod"]
    pk = lambda d: _pack_small([d[n].astype(f32) for n in sm_names])
    (pw, aoffs), (pg, _), (pm, _), (pv, _) = pk(W), pk(G), pk(M), pk(V)
    pd, pnm, pnv = _adamw(pw, pg, pm, pv, "adamw_small")
    shapes = [W[n].shape for n in sm_names]
    for dst, src in ((delta, pd), (new_m, pnm), (new_v, pnv)):
        dst.update(zip(sm_names, _unpack_small(src, aoffs, shapes)))
    for n in ["w_mod", "w_in", "w_out", "w_up", "w_down", "s5_w_glu", "conv_w"]:
        d, nm, nv = _adamw(W[n][0], G[n][0], M[n][0], V[n][0], "adamw_" + n)
        delta[n], new_m[n], new_v[n] = d[None], nm[None], nv[None]

    loss = lax.psum(loss_acc[0, 0], ("x", "y", "c"))
    return (loss, grad_x[None], *[G[n] for n in WEIGHTS], *[delta[n] for n in WEIGHTS], *[new_m[n] for n in WEIGHTS],
            *[new_v[n] for n in WEIGHTS])
```

```python
import functools

import numpy as np
import jax
import jax.numpy as jnp
from jax import lax
from jax.experimental import pallas as pl
from jax.experimental.pallas import tpu as pltpu

f32, bf16 = jnp.float32, jnp.bfloat16

D = 1024
S5W, S5G, S5P, S5N = 512, 32, 16, 64
GN = S5G * S5N
NB = 2
KB, NS = S5W // NB, GN // NB
GB = S5G // NB
RH, DH = 4, 128
RW = RH * DH
INC = S5W + 4 * RW
DFF = 2816
T = 128
R = 256
RF = 128
HALO = 8
EPS = 1e-6
ROPE_THETA = 10000.0
GRID_W = 64
NDEV = 8
LR, B1, B2, AEPS, WD, STEP = 0.001, 0.9, 0.999, 1e-08, 0.01, 10
VMEM_LIMIT = 60 * 1024 * 1024
ACC_TILE_BYTES = 6 * 1024 * 1024
MESH = pl.DeviceIdType.MESH

_CP = functools.partial(pltpu.CompilerParams, vmem_limit_bytes=VMEM_LIMIT)
_ARB = ("arbitrary",)
_ANY = pl.BlockSpec(memory_space=pl.ANY)


def _dg(a, b, dims):
    return lax.dot_general(a.astype(bf16), b.astype(bf16), (dims, ((), ())), preferred_element_type=f32)


@jax.custom_vjp
def dnn(a, b):
    return _dg(a, b, ((1,), (0,)))


@jax.custom_vjp
def dnt(a, b):
    return _dg(a, b, ((1,), (1,)))


@jax.custom_vjp
def dtn(a, b):
    return _dg(a, b, ((0,), (0,)))


dnn.defvjp(lambda a, b: (dnn(a, b), (a, b)), lambda r, g: (dnt(g, r[1]).astype(r[0].dtype), dtn(r[0], g).astype(r[1].dtype)))
dnt.defvjp(lambda a, b: (dnt(a, b), (a, b)), lambda r, g: (dnn(g, r[1]).astype(r[0].dtype), dtn(g, r[0]).astype(r[1].dtype)))
dtn.defvjp(lambda a, b: (dtn(a, b), (a, b)), lambda r, g: (dnt(r[1], g).astype(r[0].dtype), dnn(r[0], g).astype(r[1].dtype)))


def _rms(t, w):
    return t * lax.rsqrt(jnp.mean(t * t, axis=-1, keepdims=True) + EPS) * w


def _mod(h, shift, scale):
    return h * (1.0 + scale) + shift


def _const_spec(shape):
    n = len(shape)
    return pl.BlockSpec(shape, lambda i, _n=n: (0,) * _n, pipeline_mode=pl.Buffered(1))


def _acc_spec(shape):
    n = len(shape)
    return pl.BlockSpec(shape, lambda i, _n=n: (0,) * _n)


def _me():
    return 4 * lax.axis_index("x") + 2 * lax.axis_index("y") + lax.axis_index("c")


def _peer(r):
    x, y, c = lax.axis_index("x"), lax.axis_index("y"), lax.axis_index("c")
    px = 1 - x if (r >> 2) & 1 else x
    py = 1 - y if (r >> 1) & 1 else y
    pc = 1 - c if r & 1 else c
    return (px, py, pc), 4 * px + 2 * py + pc


def _all_gather_small(v, name):
    r, c = v.shape

    def body(v_ref, out_ref, send_sems, recv_sems):
        me = _me()
        out_ref[me] = v_ref[...]
        sends = []
        for k in range(1, NDEV):
            peer, _ = _peer(k)
            cp = pltpu.make_async_remote_copy(src_ref=v_ref, dst_ref=out_ref.at[me], send_sem=send_sems.at[k - 1],
                                              recv_sem=recv_sems.at[k - 1], device_id=peer, device_id_type=MESH)
            cp.start()
            sends.append(cp)
        for k in range(1, NDEV):
            peer, pidx = _peer(k)
            pltpu.make_async_remote_copy(src_ref=v_ref, dst_ref=out_ref.at[pidx], send_sem=send_sems.at[k - 1],
                                         recv_sem=recv_sems.at[k - 1], device_id=peer, device_id_type=MESH).wait_recv()
        for cp in sends:
            cp.wait_send()

    return pl.pallas_call(
        body, name=name, out_shape=jax.ShapeDtypeStruct((NDEV, r, c), v.dtype),
        in_specs=[pl.BlockSpec(memory_space=pltpu.VMEM)], out_specs=pl.BlockSpec(memory_space=pltpu.VMEM),
        scratch_shapes=[pltpu.SemaphoreType.DMA((NDEV - 1,)), pltpu.SemaphoreType.DMA((NDEV - 1,))],
        compiler_params=_CP(),
    )(v)


def _all_reduce_small(v, name):
    r, c = v.shape

    def body(v_ref, out_ref, land, send_sems, recv_sems):
        me = _me()
        land[me] = v_ref[...]
        sends = []
        for k in range(1, NDEV):
            peer, _ = _peer(k)
            cp = pltpu.make_async_remote_copy(src_ref=v_ref, dst_ref=land.at[me], send_sem=send_sems.at[k - 1],
                                              recv_sem=recv_sems.at[k - 1], device_id=peer, device_id_type=MESH)
            cp.start()
            sends.append(cp)
        for k in range(1, NDEV):
            peer, pidx = _peer(k)
            pltpu.make_async_remote_copy(src_ref=v_ref, dst_ref=land.at[pidx], send_sem=send_sems.at[k - 1],
                                         recv_sem=recv_sems.at[k - 1], device_id=peer, device_id_type=MESH).wait_recv()
        for cp in sends:
            cp.wait_send()
        acc = land[0]
        for j in range(1, NDEV):
            acc = acc + land[j]
        out_ref[...] = acc

    return pl.pallas_call(
        body, name=name, out_shape=jax.ShapeDtypeStruct((r, c), v.dtype),
        in_specs=[pl.BlockSpec(memory_space=pltpu.VMEM)], out_specs=pl.BlockSpec(memory_space=pltpu.VMEM),
        scratch_shapes=[pltpu.VMEM((NDEV, r, c), v.dtype), pltpu.SemaphoreType.DMA((NDEV - 1,)),
                        pltpu.SemaphoreType.DMA((NDEV - 1,))],
        compiler_params=_CP(),
    )(v)


class _Exchange:
    def __init__(self, srcs, dsts, send_sems, recv_sems, local_sems, scatter):
        me = _me()
        n = len(srcs)
        self.sends, self.recvs, self.locals = [], [], []
        for a, (s, d) in enumerate(zip(srcs, dsts)):
            self.locals.append(pltpu.make_async_copy(s.at[me] if scatter else s, d.at[me], local_sems.at[a]))
        for k in range(1, NDEV):
            peer, pidx = _peer(k)
            for a, (s, d) in enumerate(zip(srcs, dsts)):
                src = s.at[pidx] if scatter else s
                sem = (k - 1) * n + a
                for dst, out in ((d.at[me], self.sends), (d.at[pidx], self.recvs)):
                    out.append(pltpu.make_async_remote_copy(src_ref=src, dst_ref=dst, send_sem=send_sems.at[sem],
                                                            recv_sem=recv_sems.at[sem], device_id=peer, device_id_type=MESH))

    def start(self):
        for cp in self.locals + self.sends:
            cp.start()

    def wait(self):
        for cp in self.recvs:
            cp.wait_recv()
        for cp in self.sends:
            cp.wait_send()
        for cp in self.locals:
            cp.wait()


def _exchange_shapes(arrays, scatter):
    return [jax.ShapeDtypeStruct(a.shape if scatter else (NDEV,) + a.shape, a.dtype) for a in arrays]


def _exchange_sems(n):
    return [pltpu.SemaphoreType.DMA(((NDEV - 1) * n,)), pltpu.SemaphoreType.DMA(((NDEV - 1) * n,)), pltpu.SemaphoreType.DMA((n,))]


def _exchange(arrays, scatter, name):
    n = len(arrays)

    def body(*refs):
        ex = _Exchange(refs[:n], refs[n:2 * n], *refs[2 * n:], scatter)
        ex.start()
        ex.wait()

    return pl.pallas_call(body, name=name, out_shape=_exchange_shapes(arrays, scatter), in_specs=[_ANY] * n,
                          out_specs=[_ANY] * n, scratch_shapes=_exchange_sems(n), compiler_params=_CP())(*arrays)


def _ride(refs, n_in, n_out, cargo, nsteps):
    if cargo is None:
        return
    arrays, scatter = cargo
    n = len(arrays)
    srcs = refs[n_in:n_in + n]
    dsts = refs[n_in + n + n_out:n_in + 2 * n + n_out]
    ex = _Exchange(srcs, dsts, *refs[-3:], scatter)

    @pl.when(pl.program_id(0) == 0)
    def _():
        ex.start()

    @pl.when(pl.program_id(0) == nsteps - 1)
    def _():
        ex.wait()


def _sum8(land, name):
    _, r, c = land.shape
    rb = next((b for b in (256, 64, 32) if r % b == 0), r)

    def body(l_ref, o_ref):
        acc = l_ref[0].astype(f32)
        for j in range(1, NDEV):
            acc = acc + l_ref[j].astype(f32)
        o_ref[...] = acc

    return pl.pallas_call(
        body, name=name, grid=(r // rb,), out_shape=jax.ShapeDtypeStruct((r, c), f32),
        in_specs=[pl.BlockSpec((NDEV, rb, c), lambda i: (0, i, 0))], out_specs=pl.BlockSpec((rb, c), lambda i: (i, 0)),
        compiler_params=_CP(dimension_semantics=("parallel",)),
    )(land)


def _ada_fwd(c9, w_mod_l, name):
    def body(c_ref, w_ref, o_ref):
        o_ref[...] = dnn(jax.nn.silu(c_ref[...]), w_ref[...])

    return pl.pallas_call(body, name=name, out_shape=jax.ShapeDtypeStruct((16, w_mod_l.shape[1]), f32),
                          compiler_params=_CP())(c9, w_mod_l)


def _mod_select(m_all, b_mod6, name):
    def body(m_ref, b_ref, mx_ref, mc_ref):
        me = _me()
        mx_ref[...] = m_ref[me] + b_ref[...]
        mc_ref[...] = m_ref[8] + b_ref[...]

    return pl.pallas_call(body, name=name, out_shape=[jax.ShapeDtypeStruct((6, D), f32)] * 2, compiler_params=_CP())(m_all, b_mod6)


def _ada_bwd(c9, dmx_all, dmc_all, dmx_l, dmc_l, w_mod_l, name):
    ncol = w_mod_l.shape[1]

    def rowsum(r):
        acc = r[0:1]
        for j in range(1, NDEV):
            acc = acc + r[j:j + 1]
        return acc

    def body(c_ref, xa_ref, ca_ref, xl_ref, cl_ref, w_ref, gw_ref, gb_ref, dc_ref):
        s9, vjp = jax.vjp(jax.nn.silu, c_ref[...])
        dm9 = jnp.concatenate([xl_ref[...], rowsum(cl_ref[...]), jnp.zeros((7, ncol), f32)], axis=0)
        gw_ref[...] = dtn(s9, dm9)
        gb_ref[...] = rowsum(xa_ref[...]) + rowsum(ca_ref[...])
        dc_ref[...] = vjp(dnt(dm9, w_ref[...]))[0]

    return pl.pallas_call(
        body, name=name,
        out_shape=[jax.ShapeDtypeStruct((D, ncol), f32), jax.ShapeDtypeStruct((1, 6 * D), f32), jax.ShapeDtypeStruct((16, D), f32)],
        compiler_params=_CP())(c9, dmx_all, dmc_all, dmx_l, dmc_l, w_mod_l)


def _s5_disc(lre, lim, lstep, bre, bim):
    s = jnp.exp(lstep)
    ar, ai = lre * s, lim * s
    e = jnp.exp(ar)
    lbr, lbi = e * jnp.cos(ai), e * jnp.sin(ai)
    nr, ni = lbr - 1.0, lbi
    den = lre * lre + lim * lim
    cr, ci = (nr * lre + ni * lim) / den, (ni * lre - nr * lim) / den
    return lbr, lbi, cr[None] * bre - ci[None] * bim, cr[None] * bim + ci[None] * bre


def _s5_prep(lre, lim, lstep, bre, bim, name):
    def body(lre_ref, lim_ref, ls_ref, bre_ref, bim_ref, lam_ref, bbr_ref, bbi_ref, tab_ref):
        lbr, lbi, bbr, bbi = _s5_disc(lre_ref[...], lim_ref[...], ls_ref[...], bre_ref[...], bim_ref[...])
        lam_ref[0], lam_ref[1] = lbr, lbi
        bbr_ref[...], bbi_ref[...] = bbr, bbi
        s = jnp.exp(ls_ref[...])
        ar, ai = (lre_ref[...] * s)[None], (lim_ref[...] * s)[None]
        up = lax.broadcasted_iota(jnp.int32, (T, S5G, S5N), 0).astype(f32)
        for k, t in ((0, up), (4, T - 1.0 - up)):
            ea, ang = jnp.exp(t * ar), t * ai
            cs, sn = jnp.cos(ang), jnp.sin(ang)
            tab_ref[k + 0], tab_ref[k + 1] = cs / ea, -sn / ea
            tab_ref[k + 2], tab_ref[k + 3] = ea * cs, ea * sn

    gshape = (S5G, S5N)
    return pl.pallas_call(
        body, name=name,
        out_shape=[jax.ShapeDtypeStruct((2,) + gshape, f32), jax.ShapeDtypeStruct((S5P,) + gshape, f32),
                   jax.ShapeDtypeStruct((S5P,) + gshape, f32), jax.ShapeDtypeStruct((8, T) + gshape, f32)],
        compiler_params=_CP())(lre, lim, lstep, bre, bim)


def _s5_prep_bwd(lre, lim, lstep, bre, bim, dlacc, dbbr, dbbi, name):
    def body(lre_ref, lim_ref, ls_ref, bre_ref, bim_ref, dl_ref, dbr_ref, dbi_ref, glre, glim, gls, gbre, gbim):
        prim = (lre_ref[...], lim_ref[...], ls_ref[...], bre_ref[...], bim_ref[...])
        (lbr, lbi, _, _), vjp = jax.vjp(_s5_disc, *prim)
        x, y = dl_ref[0], dl_ref[1]
        n2 = lbr * lbr + lbi * lbi
        dlr, dli = (x * lbr - y * lbi) / n2, (x * lbi + y * lbr) / n2
        g = vjp((dlr, dli, dbr_ref[...], dbi_ref[...]))
        glre[...], glim[...], gls[...], gbre[...], gbim[...] = g

    gshape = (S5G, S5N)
    return pl.pallas_call(
        body, name=name,
        out_shape=[jax.ShapeDtypeStruct(gshape, f32), jax.ShapeDtypeStruct(gshape, f32), jax.ShapeDtypeStruct((S5G, 1), f32),
                   jax.ShapeDtypeStruct((S5P,) + gshape, f32), jax.ShapeDtypeStruct((S5P,) + gshape, f32)],
        compiler_params=_CP())(lre, lim, lstep, bre, bim, dlacc, dbbr, dbbi)


def _blockdiag(m_gpn):
    m = m_gpn.reshape(NB, GB, S5P, S5N)
    eye = jnp.eye(GB, dtype=m.dtype)
    return (m[:, :, :, None, :] * eye[None, :, None, :, None]).reshape(NB, KB, NS).astype(bf16)


def _blockdiag_extract(m):
    m = m.reshape(NB, GB, S5P, GB, S5N)
    return jnp.einsum("bgpgn->bgpn", m).reshape(S5G, S5P, S5N)


def _cmul(ar, ai, br, bi):
    return ar * br - ai * bi, ar * bi + ai * br


def _tri_dot(tri, z):
    zh = z.astype(bf16)
    zl = (z - zh.astype(f32)).astype(bf16)
    return jnp.dot(tri, zh, preferred_element_type=f32) + jnp.dot(tri, zl, preferred_element_type=f32)


def _scan_chunk(xr, xi, wir, wii, wfr, wfi, tri, gr, gi):
    zr, zi = _cmul(wir, wii, xr, xi)
    return _cmul(wfr, wfi, _tri_dot(tri, zr) + gr, _tri_dot(tri, zi) + gi)


def _proj_in(u, m_ref):
    return jnp.concatenate([jnp.dot(u[:, b * KB:(b + 1) * KB], m_ref[b], preferred_element_type=f32) for b in range(NB)], axis=1)


def _proj_out(h, m_ref):
    return jnp.concatenate([dnt(h[:, b * NS:(b + 1) * NS], m_ref[b]) for b in range(NB)], axis=1)


def _outer_acc(a, h):
    return jnp.stack([dtn(a[:, b * KB:(b + 1) * KB], h[:, b * NS:(b + 1) * NS]) for b in range(NB)], axis=0)


def _idx_fwd(nctx, nch):
    return lambda i: i


def _idx_rev(nctx, nch):
    return lambda i: jnp.where(i < nctx, nctx - 1 - i, nch + nctx - 1 - i)


def _s5_fwd(p_ext, mats, tab, lam, tri, rev, nctx, name, cargo=None):
    n = p_ext.shape[0]
    nch = n // T
    idx = (_idx_rev if rev else _idx_fwd)(nctx, nch)
    bre, bim, cre, cim = mats
    tb = 4 if rev else 0
    carr, cscat = cargo if cargo else ([], False)
    nc = len(carr)

    def body(*refs):
        u_ref, bre_ref, bim_ref, cre_ref, cim_ref, tab_ref, lam_ref, tri_ref = refs[:8]
        y_ref, hb_ref = refs[8 + nc:10 + nc]
        h_s = refs[10 + 2 * nc]
        _ride(refs, 8, 2, cargo, nch)

        @pl.when(pl.program_id(0) == 0)
        def _():
            h_s[...] = jnp.zeros_like(h_s)

        u = u_ref[...]
        xr, xi = _proj_in(u, bre_ref), _proj_in(u, bim_ref)
        hp = h_s[...]
        hb_ref[0] = hp
        gr, gi = _cmul(lam_ref[0:1], lam_ref[1:2], hp[0:1], hp[1:2])
        hr, hi = _scan_chunk(xr, xi, tab_ref[tb], tab_ref[tb + 1], tab_ref[tb + 2], tab_ref[tb + 3], tri_ref[...], gr, gi)
        last = 0 if rev else T - 1
        h_s[0:1] = hr[last:last + 1]
        h_s[1:2] = hi[last:last + 1]
        y_ref[...] = _proj_out(hr, cre_ref) - _proj_out(hi, cim_ref)

    mspec = _const_spec((NB, KB, NS))
    return pl.pallas_call(
        body, name=name, grid=(nch,),
        in_specs=[pl.BlockSpec((T, S5W), lambda i: (idx(i), 0)), mspec, mspec, mspec, mspec,
                  _const_spec((8, T, GN)), _const_spec((2, GN)), _const_spec((T, T))] + [_ANY] * nc,
        out_specs=[pl.BlockSpec((T, S5W), lambda i: (idx(i), 0)), pl.BlockSpec((1, 2, GN), lambda i: (i, 0, 0))] + [_ANY] * nc,
        out_shape=[jax.ShapeDtypeStruct((n, S5W), f32), jax.ShapeDtypeStruct((nch, 2, GN), f32)] + _exchange_shapes(carr, cscat),
        scratch_shapes=[pltpu.VMEM((2, GN), f32)] + (_exchange_sems(nc) if nc else []),
        compiler_params=_CP(dimension_semantics=_ARB),
    )(p_ext, bre, bim, cre, cim, tab, lam, tri, *carr)


def _s5_bwd(p_ext, dy_ext, hb, mats, tab, lam, tri_f, tri_a, rev, nctx, name, cargo=None):
    n = p_ext.shape[0]
    nch = n // T
    idx0 = (_idx_rev if rev else _idx_fwd)(nctx, nch)
    idx = lambda j: idx0(nch - 1 - j)
    bre, bim, cre, cim = mats
    tf, ta = (4, 0) if rev else (0, 4)
    carr, cscat = cargo if cargo else ([], False)
    nc = len(carr)

    def body(*refs):
        u_ref, dy_ref, hb_ref, bre_ref, bim_ref, cre_ref, cim_ref, tab_ref, lam_ref, trf_ref, tra_ref = refs[:11]
        du_ref, dbre_ref, dbim_ref, dcre_ref, dcim_ref, dlam_ref = refs[11 + nc:17 + nc]
        a_s = refs[17 + 2 * nc]
        _ride(refs, 11, 6, cargo, nch)

        @pl.when(pl.program_id(0) == 0)
        def _():
            a_s[...] = jnp.zeros_like(a_s)
            dbre_ref[...] = jnp.zeros_like(dbre_ref)
            dbim_ref[...] = jnp.zeros_like(dbim_ref)
            dcre_ref[...] = jnp.zeros_like(dcre_ref)
            dcim_ref[...] = jnp.zeros_like(dcim_ref)
            dlam_ref[...] = jnp.zeros_like(dlam_ref)

        u = u_ref[...]
        dy = dy_ref[...].astype(bf16)
        lr, li = lam_ref[0:1], lam_ref[1:2]
        xr, xi = _proj_in(u, bre_ref), _proj_in(u, bim_ref)
        hp = hb_ref[0]
        gr, gi = _cmul(lr, li, hp[0:1], hp[1:2])
        hr, hi = _scan_chunk(xr, xi, tab_ref[tf], tab_ref[tf + 1], tab_ref[tf + 2], tab_ref[tf + 3], trf_ref[...], gr, gi)
        cgr, cgi = _proj_in(dy, cre_ref), -_proj_in(dy, cim_ref)
        ac = a_s[...]
        agr, agi = _cmul(lr, -li, ac[0:1], ac[1:2])
        ar, ai = _scan_chunk(cgr, cgi, tab_ref[ta], -tab_ref[ta + 1], tab_ref[ta + 2], -tab_ref[ta + 3], tra_ref[...], agr, agi)
        first = T - 1 if rev else 0
        a_s[0:1] = ar[first:first + 1]
        a_s[1:2] = ai[first:first + 1]
        du_ref[...] = _proj_out(ar, bre_ref) + _proj_out(ai, bim_ref)
        dbre_ref[...] += _outer_acc(u, ar)
        dbim_ref[...] += _outer_acc(u, ai)
        dcre_ref[...] += _outer_acc(dy, hr)
        dcim_ref[...] -= _outer_acc(dy, hi)
        mr, mi = hr - xr, hi - xi
        dlam_ref[0:1] += jnp.sum(ar * mr + ai * mi, axis=0, keepdims=True)
        dlam_ref[1:2] += jnp.sum(ai * mr - ar * mi, axis=0, keepdims=True)

    mspec = _const_spec((NB, KB, NS))
    aspec = _acc_spec((NB, KB, NS))
    ashape = jax.ShapeDtypeStruct((NB, KB, NS), f32)
    return pl.pallas_call(
        body, name=name, grid=(nch,),
        in_specs=[pl.BlockSpec((T, S5W), lambda j: (idx(j), 0)), pl.BlockSpec((T, S5W), lambda j: (idx(j), 0)),
                  pl.BlockSpec((1, 2, GN), lambda j: (nch - 1 - j, 0, 0)), mspec, mspec, mspec, mspec,
                  _const_spec((8, T, GN)), _const_spec((2, GN)), _const_spec((T, T)), _const_spec((T, T))] + [_ANY] * nc,
        out_specs=[pl.BlockSpec((T, S5W), lambda j: (idx(j), 0)), aspec, aspec, aspec, aspec, _acc_spec((2, GN))] + [_ANY] * nc,
        out_shape=[jax.ShapeDtypeStruct((n, S5W), f32), ashape, ashape, ashape, ashape, jax.ShapeDtypeStruct((2, GN), f32)]
        + _exchange_shapes(carr, cscat),
        scratch_shapes=[pltpu.VMEM((2, GN), f32)] + (_exchange_sems(nc) if nc else []),
        compiler_params=_CP(dimension_semantics=_ARB),
    )(p_ext, dy_ext, hb, bre, bim, cre, cim, tab, lam, tri_f, tri_a, *carr)


def _swap_pairs(t):
    lane = lax.broadcasted_iota(jnp.int32, t.shape, 1)
    return jnp.where(lane % 2 == 0, pltpu.roll(t, DH - 1, axis=1), pltpu.roll(t, 1, axis=1))


def _rot(t, cosf, sins):
    return t * cosf + _swap_pairs(t) * sins


def _rot_t(d, cosf, sins):
    return d * cosf - _swap_pairs(d) * sins


def _ret_chunk(qr, kr, v, rp, ld, rev):
    pos = lax.broadcasted_iota(jnp.int32, (T, 1), 0).astype(f32)
    diff = pos - lax.broadcasted_iota(jnp.int32, (1, T), 1).astype(f32)
    if rev:
        keep, dist = diff < 0, jnp.maximum(-diff, 0.0)
        xi, zeta = jnp.exp(ld * (T - pos)), jnp.exp(ld * pos)
    else:
        keep, dist = diff >= 0, jnp.maximum(diff, 0.0)
        xi, zeta = jnp.exp(ld * (pos + 1.0)), jnp.exp(ld * (T - 1.0 - pos))
    dm = jnp.where(keep, jnp.exp(ld * dist), 0.0)
    out = dnn(dnt(qr, kr) * dm, v) + dnn(qr * xi, rp)
    rn = jnp.exp(ld * float(T)) * rp + dtn(kr * zeta, v)
    return out, rn


def _ret_fwd(p_ext, cosf, sins, ld8, rev, nctx, name):
    n = p_ext.shape[0]
    nch = n // T
    idx = (_idx_rev if rev else _idx_fwd)(nctx, nch)
    scale = DH ** -0.5

    def body(q_ref, k_ref, v_ref, cos_ref, sin_ref, ld_ref, o_ref, rp_ref, r_s):
        @pl.when(pl.program_id(0) == 0)
        def _():
            r_s[...] = jnp.zeros_like(r_s)

        cf, ss = cos_ref[...], sin_ref[...]
        for h in range(RH):
            sl = slice(h * DH, (h + 1) * DH)
            qr = _rot(q_ref[:, sl].astype(f32), cf, ss)
            kr = _rot(k_ref[:, sl].astype(f32), cf, ss) * scale
            rp = r_s[h]
            rp_ref[0, h] = rp
            out, rn = _ret_chunk(qr, kr, v_ref[:, sl].astype(f32), rp, ld_ref[h:h + 1, 0:1], rev)
            r_s[h] = rn
            o_ref[:, sl] = out

    def colspec(cb):
        return pl.BlockSpec((T, RW), lambda i, _c=cb: (idx(i), _c))

    tspec = pl.BlockSpec((T, DH), lambda i: (idx(i), 0))
    return pl.pallas_call(
        body, name=name, grid=(nch,),
        in_specs=[colspec(1), colspec(2), colspec(3), tspec, tspec, _const_spec((8, 128))],
        out_specs=[pl.BlockSpec((T, RW), lambda i: (idx(i), 0)), pl.BlockSpec((1, RH, DH, DH), lambda i: (i, 0, 0, 0))],
        out_shape=[jax.ShapeDtypeStruct((n, RW), f32), jax.ShapeDtypeStruct((nch, RH, DH, DH), f32)],
        scratch_shapes=[pltpu.VMEM((RH, DH, DH), f32)],
        compiler_params=_CP(dimension_semantics=_ARB),
    )(p_ext, p_ext, p_ext, cosf, sins, ld8)


def _ret_bwd(p_ext, cosf, sins, ld8, rprev, do_ext, rev, nctx, name):
    n = p_ext.shape[0]
    nch = n // T
    idx0 = (_idx_rev if rev else _idx_fwd)(nctx, nch)
    idx = lambda j: idx0(nch - 1 - j)
    scale = DH ** -0.5

    def body(q_ref, k_ref, v_ref, cos_ref, sin_ref, ld_ref, rp_ref, do_ref, dq_ref, dk_ref, dv_ref, dld_ref, dr_s):
        @pl.when(pl.program_id(0) == 0)
        def _():
            dr_s[...] = jnp.zeros_like(dr_s)
            dld_ref[...] = jnp.zeros_like(dld_ref)

        cf, ss = cos_ref[...], sin_ref[...]
        for h in range(RH):
            sl = slice(h * DH, (h + 1) * DH)
            qr = _rot(q_ref[:, sl].astype(f32), cf, ss)
            kr = _rot(k_ref[:, sl].astype(f32), cf, ss) * scale
            _, vjp = jax.vjp(functools.partial(_ret_chunk, rev=rev), qr, kr, v_ref[:, sl].astype(f32), rp_ref[0, h],
                             ld_ref[h:h + 1, 0:1])
            dqr, dkr, dv, drp, dld = vjp((do_ref[:, sl], dr_s[h]))
            dr_s[h] = drp
            dq_ref[:, sl] = _rot_t(dqr, cf, ss)
            dk_ref[:, sl] = _rot_t(dkr, cf, ss) * scale
            dv_ref[:, sl] = dv
            dld_ref[h:h + 1, :] += jnp.broadcast_to(dld, (1, 128))

    def colspec(cb):
        return pl.BlockSpec((T, RW), lambda j, _c=cb: (idx(j), _c))

    tspec = pl.BlockSpec((T, DH), lambda j: (idx(j), 0))
    ospec = pl.BlockSpec((T, RW), lambda j: (idx(j), 0))
    oshape = jax.ShapeDtypeStruct((n, RW), f32)
    return pl.pallas_call(
        body, name=name, grid=(nch,),
        in_specs=[colspec(1), colspec(2), colspec(3), tspec, tspec, _const_spec((8, 128)),
                  pl.BlockSpec((1, RH, DH, DH), lambda j: (nch - 1 - j, 0, 0, 0)), ospec],
        out_specs=[ospec, ospec, ospec, _acc_spec((8, 128))],
        out_shape=[oshape, oshape, oshape, jax.ShapeDtypeStruct((8, 128), f32)],
        scratch_shapes=[pltpu.VMEM((RH, DH, DH), f32)],
        compiler_params=_CP(dimension_semantics=_ARB),
    )(p_ext, p_ext, p_ext, cosf, sins, ld8, rprev, do_ext)


def _f1_fwd(x, ctx, modx, modc, nw1, w_in_t, name):
    L = x.shape[0]
    nb = L // R + 1

    def body(x_ref, c_ref, mx_ref, mc_ref, nw_ref, w_ref, p_ref):
        is_ctx = pl.program_id(0) == 0
        xin = jnp.where(is_ctx, c_ref[...], x_ref[...])
        sh = jnp.where(is_ctx, mc_ref[0:1], mx_ref[0:1])
        sc = jnp.where(is_ctx, mc_ref[1:2], mx_ref[1:2])
        p_ref[...] = dnt(_mod(_rms(xin, nw_ref[...]), sh, sc), w_ref[...]).astype(bf16)

    return pl.pallas_call(
        body, name=name, grid=(nb,),
        in_specs=[pl.BlockSpec((R, D), lambda i: (jnp.maximum(i - 1, 0), 0)), _const_spec((R, D)), _const_spec((6, D)),
                  _const_spec((6, D)), _const_spec((1, D)), _const_spec((INC, D))],
        out_specs=pl.BlockSpec((R, INC), lambda i: (i, 0)),
        out_shape=jax.ShapeDtypeStruct((L + R, INC), bf16),
        compiler_params=_CP(dimension_semantics=("parallel",)),
    )(x, ctx, modx, modc, nw1, w_in_t)


def _f1_bwd(x, ctx, modx, modc, nw1, w_in_t, dx1, parts, name):
    L = x.shape[0]
    nb = L // R + 1

    def body(x_ref, c_ref, mx_ref, mc_ref, nw_ref, w_ref, dx1_ref, du0, du1, du2, dq0, dq1, dk0, dk1, dv0, dv1, dg0,
             gx_ref, dp_ref, h1_ref, dnw_ref, dmx_ref, dmc_ref):
        i = pl.program_id(0)
        is_ctx = i == 0

        @pl.when(is_ctx)
        def _():
            dnw_ref[...] = jnp.zeros_like(dnw_ref)
            dmx_ref[...] = jnp.zeros_like(dmx_ref)
            dmc_ref[...] = jnp.zeros_like(dmc_ref)

        dp = jnp.concatenate([du0[...] + du1[...] + du2[...], dq0[...] + dq1[...], dk0[...] + dk1[...], dv0[...] + dv1[...],
                              dg0[...]], axis=1).astype(bf16)
        dp_ref[...] = dp
        xin = jnp.where(is_ctx, c_ref[...], x_ref[...])
        sh = jnp.where(is_ctx, mc_ref[0:1], mx_ref[0:1])
        sc = jnp.where(is_ctx, mc_ref[1:2], mx_ref[1:2])
        dh = dnn(dp, w_ref[...])
        h, vjp = jax.vjp(lambda a, b, c, d: _mod(_rms(a, b), c, d), xin, nw_ref[...], sh, sc)
        dxin, dnw, dsh, dsc = vjp(dh)
        h1_ref[...] = h.astype(bf16)
        gx_ref[...] = dx1_ref[...] + dxin
        dnw_ref[...] += dnw
        wx = jnp.where(is_ctx, 0.0, 1.0)
        dmx_ref[0:1] += dsh * wx
        dmx_ref[1:2] += dsc * wx
        dmc_ref[0:1] += dsh * (1.0 - wx)
        dmc_ref[1:2] += dsc * (1.0 - wx)

    lat = pl.BlockSpec((R, D), lambda i: (jnp.maximum(i - 1, 0), 0))
    ext = pl.BlockSpec((R, S5W), lambda i: (i, 0))
    return pl.pallas_call(
        body, name=name, grid=(nb,),
        in_specs=[lat, _const_spec((R, D)), _const_spec((6, D)), _const_spec((6, D)), _const_spec((1, D)), _const_spec((INC, D)),
                  lat] + [ext] * 10,
        out_specs=[lat, pl.BlockSpec((R, INC), lambda i: (i, 0)), pl.BlockSpec((R, D), lambda i: (i, 0)),
                   _acc_spec((1, D)), _acc_spec((6, D)), _acc_spec((6, D))],
        out_shape=[jax.ShapeDtypeStruct((L, D), f32), jax.ShapeDtypeStruct((L + R, INC), bf16),
                   jax.ShapeDtypeStruct((L + R, D), bf16), jax.ShapeDtypeStruct((1, D), f32),
                   jax.ShapeDtypeStruct((6, D), f32), jax.ShapeDtypeStruct((6, D), f32)],
        compiler_params=_CP(dimension_semantics=_ARB),
    )(x, ctx, modx, modc, nw1, w_in_t, dx1, *parts)


def _ret_post(yr, g):
    outs = []
    for h in range(RH):
        yh = yr[:, h * DH:(h + 1) * DH]
        mu = jnp.mean(yh, axis=-1, keepdims=True)
        var = jnp.mean((yh - mu) ** 2, axis=-1, keepdims=True)
        outs.append((yh - mu) * lax.rsqrt(var + EPS))
    return jax.nn.silu(g) * jnp.concatenate(outs, axis=1)


def _mix_fn(yf, yb, u, of, ob, g, x, dvec, bglu, gate1, pz, pm, wglu, wout):
    y = yf + yb + dvec * u
    s = jax.nn.gelu(y)
    z = dnn(s, wglu) + bglu + pz
    cat = jnp.concatenate([s * jax.nn.sigmoid(z), _ret_post(of + ob, g)], axis=1)
    mix = dnn(cat, wout) + pm
    return x + gate1 * mix, (s, cat)


def _mix_fwd(x, yf, yb, of, ob, p_ext, dvec, bglu, modx, wglu, wout, name):
    L = x.shape[0]

    def body(x_ref, yf_ref, yb_ref, of_ref, ob_ref, u_ref, g_ref, d_ref, b_ref, mx_ref, wg_ref, wo_ref, x1_ref):
        x1_ref[...] = _mix_fn(yf_ref[...], yb_ref[...], u_ref[...].astype(f32), of_ref[...], ob_ref[...], g_ref[...].astype(f32),
                              x_ref[...], d_ref[...], b_ref[...], mx_ref[2:3], 0.0, 0.0, wg_ref[...], wo_ref[...])[0]

    ext = pl.BlockSpec((R, S5W), lambda i: (i + 1, 0))
    return pl.pallas_call(
        body, name=name, grid=(L // R,),
        in_specs=[pl.BlockSpec((R, D), lambda i: (i, 0)), ext, ext, ext, ext, ext, pl.BlockSpec((R, RW), lambda i: (i + 1, 4)),
                  _const_spec((1, S5W)), _const_spec((1, S5W)), _const_spec((6, D)), _const_spec((S5W, S5W)), _const_spec((D, D))],
        out_specs=pl.BlockSpec((R, D), lambda i: (i, 0)),
        out_shape=jax.ShapeDtypeStruct((L, D), f32),
        compiler_params=_CP(dimension_semantics=("parallel",)),
    )(x, yf, yb, of, ob, p_ext, p_ext, dvec, bglu, modx, wglu, wout)


def _mix_bwd(x, yf, yb, of, ob, p_ext, dvec, bglu, modx, wglu, wout, dx1, name):
    L = x.shape[0]
    nb = L // R + 1

    def body(x_ref, yf_ref, yb_ref, of_ref, ob_ref, u_ref, g_ref, d_ref, b_ref, mx_ref, wg_ref, wo_ref, dx1_ref,
             dy_ref, dud_ref, do_ref, dg_ref, cat_ref, dmix_ref, s_ref, dz_ref, dd_ref, db_ref, dg1_ref):
        i = pl.program_id(0)

        @pl.when(i == 0)
        def _():
            for r in (dy_ref, dud_ref, do_ref, dg_ref, cat_ref, dmix_ref, s_ref, dz_ref, dd_ref, db_ref, dg1_ref):
                r[...] = jnp.zeros_like(r)

        @pl.when(i > 0)
        def _():
            fn = lambda yf_, u_, of_, g_, d_, b_, g1_, pz_, pm_: _mix_fn(
                yf_, yb_ref[...], u_, of_, ob_ref[...], g_, x_ref[...], d_, b_, g1_, pz_, pm_, wg_ref[...], wo_ref[...])
            _, vjp, (s, cat) = jax.vjp(fn, yf_ref[...], u_ref[...].astype(f32), of_ref[...], g_ref[...].astype(f32), d_ref[...],
                                       b_ref[...], mx_ref[2:3], jnp.zeros((R, S5W), f32), jnp.zeros((R, D), f32), has_aux=True)
            dy, dud, do, dg, dd, db, dg1, dz, dmix = vjp(dx1_ref[...])
            dy_ref[...], dud_ref[...], do_ref[...], dg_ref[...] = dy, dud, do, dg
            cat_ref[...], dmix_ref[...] = cat.astype(bf16), dmix.astype(bf16)
            s_ref[...], dz_ref[...] = s.astype(bf16), dz.astype(bf16)
            dd_ref[...] += dd
            db_ref[...] += db
            dg1_ref[...] += dg1

    lat = pl.BlockSpec((R, D), lambda i: (jnp.maximum(i - 1, 0), 0))
    lat5 = pl.BlockSpec((R, S5W), lambda i: (jnp.maximum(i - 1, 0), 0))
    ext = pl.BlockSpec((R, S5W), lambda i: (i, 0))
    eshape = jax.ShapeDtypeStruct((L + R, S5W), f32)
    return pl.pallas_call(
        body, name=name, grid=(nb,),
        in_specs=[lat, ext, ext, ext, ext, ext, pl.BlockSpec((R, RW), lambda i: (i, 4)),
                  _const_spec((1, S5W)), _const_spec((1, S5W)), _const_spec((6, D)), _const_spec((S5W, S5W)), _const_spec((D, D)), lat],
        out_specs=[ext, ext, ext, ext, lat, lat, lat5, lat5, _acc_spec((1, S5W)), _acc_spec((1, S5W)), _acc_spec((1, D))],
        out_shape=[eshape, eshape, eshape, eshape, jax.ShapeDtypeStruct((L, D), bf16), jax.ShapeDtypeStruct((L, D), bf16),
                   jax.ShapeDtypeStruct((L, S5W), bf16), jax.ShapeDtypeStruct((L, S5W), bf16),
                   jax.ShapeDtypeStruct((1, S5W), f32), jax.ShapeDtypeStruct((1, S5W), f32), jax.ShapeDtypeStruct((1, D), f32)],
        compiler_params=_CP(dimension_semantics=_ARB),
    )(x, yf, yb, of, ob, p_ext, p_ext, dvec, bglu, modx, wglu, wout, dx1)


def _ffn_tail(gc, a, x1, gate2, fnw, pf, wdown, tgt):
    f = jax.nn.gelu(gc) * a
    ffn = dnn(f, wdown) + pf
    y = _rms(x1 + gate2 * ffn, fnw)
    err = y - tgt
    loss = 0.5 * jnp.sum(jnp.mean(err * err, axis=-1, keepdims=True), axis=0, keepdims=True)
    return loss, f


def _ffn_fwd(x1, tgt, nw2, modx, wup_t, cw, cb, wdown, fnw, name):
    L = x1.shape[0]
    nb = L // RF
    per = RF // HALO

    def body(x_ref, xp_ref, xn_ref, t_ref, nw_ref, mx_ref, wu_ref, cw_ref, cb_ref, wd_ref, fn_ref,
             dx2_ref, da_ref, dgc_ref, f_ref, dffn_ref, loss_ref, dfn_ref, dg2_ref, dcb_ref, dcw_ref):
        i = pl.program_id(0)

        @pl.when(i == 0)
        def _():
            for r in (loss_ref, dfn_ref, dg2_ref, dcb_ref, dcw_ref):
                r[...] = jnp.zeros_like(r)

        nw, sh, sc, gate2 = nw_ref[...], mx_ref[3:4], mx_ref[4:5], mx_ref[5:6]
        x1b = x_ref[...]
        h2 = _mod(_rms(x1b, nw), sh, sc)
        a, g = dnt(h2, wu_ref[0:DFF]), dnt(h2, wu_ref[DFF:2 * DFF])
        gp = dnt(_mod(_rms(xp_ref[...], nw), sh, sc), wu_ref[DFF:2 * DFF])[HALO - 1:HALO] * jnp.where(i > 0, 1.0, 0.0)
        gn = dnt(_mod(_rms(xn_ref[...], nw), sh, sc), wu_ref[DFF:2 * DFF])[0:1] * jnp.where(i < nb - 1, 1.0, 0.0)
        row = lax.broadcasted_iota(jnp.int32, (RF, 1), 0)
        g_prev = jnp.where(row == 0, gp, pltpu.roll(g, 1, axis=0))
        g_next = jnp.where(row == RF - 1, gn, pltpu.roll(g, RF - 1, axis=0))
        gc = cb_ref[...] + g_prev * cw_ref[0:1] + g * cw_ref[1:2] + g_next * cw_ref[2:3]
        fn = lambda gc_, a_, x_, g2_, fw_, pf_: _ffn_tail(gc_, a_, x_, g2_, fw_, pf_, wd_ref[...], t_ref[...])
        loss, vjp, f = jax.vjp(fn, gc, a, x1b, gate2, fn_ref[...], jnp.zeros((RF, D), f32), has_aux=True)
        dgc, da, dx2, dg2, dfw, dffn = vjp(jnp.ones((1, 1), f32))
        dx2_ref[...] = dx2
        da_ref[...], dgc_ref[...] = da.astype(bf16), dgc
        f_ref[...], dffn_ref[...] = f.astype(bf16), dffn.astype(bf16)
        loss_ref[...] += jnp.broadcast_to(loss, (1, 128))
        dfn_ref[...] += dfw
        dg2_ref[...] += dg2
        dcb_ref[...] += jnp.sum(dgc, axis=0, keepdims=True)
        dcw_ref[0:1] += jnp.sum(dgc * g_prev, axis=0, keepdims=True)
        dcw_ref[1:2] += jnp.sum(dgc * g, axis=0, keepdims=True)
        dcw_ref[2:3] += jnp.sum(dgc * g_next, axis=0, keepdims=True)

    blk = lambda w: pl.BlockSpec((RF, w), lambda i: (i, 0))
    return pl.pallas_call(
        body, name=name, grid=(nb,),
        in_specs=[blk(D), pl.BlockSpec((HALO, D), lambda i: (jnp.maximum(i * per - 1, 0), 0)),
                  pl.BlockSpec((HALO, D), lambda i: (jnp.minimum((i + 1) * per, L // HALO - 1), 0)), blk(D),
                  _const_spec((1, D)), _const_spec((6, D)), _const_spec((2 * DFF, D)), _const_spec((3, DFF)), _const_spec((1, DFF)),
                  _const_spec((DFF, D)), _const_spec((1, D))],
        out_specs=[blk(D), blk(DFF), blk(DFF), blk(DFF), blk(D), _acc_spec((1, 128)), _acc_spec((1, D)), _acc_spec((1, D)),
                   _acc_spec((1, DFF)), _acc_spec((3, DFF))],
        out_shape=[jax.ShapeDtypeStruct((L, D), f32), jax.ShapeDtypeStruct((L, DFF), bf16), jax.ShapeDtypeStruct((L, DFF), f32),
                   jax.ShapeDtypeStruct((L, DFF), bf16), jax.ShapeDtypeStruct((L, D), bf16), jax.ShapeDtypeStruct((1, 128), f32),
                   jax.ShapeDtypeStruct((1, D), f32), jax.ShapeDtypeStruct((1, D), f32), jax.ShapeDtypeStruct((1, DFF), f32),
                   jax.ShapeDtypeStruct((3, DFF), f32)],
        compiler_params=_CP(dimension_semantics=_ARB),
    )(x1, x1, x1, tgt, nw2, modx, wup_t, cw, cb, wdown, fnw)


def _ffn_bwd(x1, dx2, da, dgc, nw2, modx, wup_t, cw, name):
    L = x1.shape[0]
    nb = L // RF
    per = RF // HALO

    def body(x_ref, dx2_ref, da_ref, dgc_ref, dgp_ref, dgn_ref, nw_ref, mx_ref, wu_ref, cw_ref,
             dx1_ref, dag_ref, h2_ref, dnw_ref, dmx_ref):
        i = pl.program_id(0)

        @pl.when(i == 0)
        def _():
            dnw_ref[...] = jnp.zeros_like(dnw_ref)
            dmx_ref[...] = jnp.zeros_like(dmx_ref)

        dgc_b = dgc_ref[...]
        before = dgp_ref[HALO - 1:HALO] * jnp.where(i > 0, 1.0, 0.0)
        after = dgn_ref[0:1] * jnp.where(i < nb - 1, 1.0, 0.0)
        row = lax.broadcasted_iota(jnp.int32, (RF, 1), 0)
        d_prev = jnp.where(row == 0, before, pltpu.roll(dgc_b, 1, axis=0))
        d_next = jnp.where(row == RF - 1, after, pltpu.roll(dgc_b, RF - 1, axis=0))
        dg = cw_ref[0:1] * d_next + cw_ref[1:2] * dgc_b + cw_ref[2:3] * d_prev
        dag = jnp.concatenate([da_ref[...], dg.astype(bf16)], axis=1)
        dag_ref[...] = dag
        dh2 = dnn(dag, wu_ref[...])
        h2, vjp = jax.vjp(lambda a, b, c, d: _mod(_rms(a, b), c, d), x_ref[...], nw_ref[...], mx_ref[3:4], mx_ref[4:5])
        dxa, dnw, dsh, dsc = vjp(dh2)
        h2_ref[...] = h2.astype(bf16)
        dx1_ref[...] = dx2_ref[...] + dxa
        dnw_ref[...] += dnw
        dmx_ref[3:4] += dsh
        dmx_ref[4:5] += dsc

    blk = lambda w: pl.BlockSpec((RF, w), lambda i: (i, 0))
    return pl.pallas_call(
        body, name=name, grid=(nb,),
        in_specs=[blk(D), blk(D), blk(DFF), blk(DFF), pl.BlockSpec((HALO, DFF), lambda i: (jnp.maximum(i * per - 1, 0), 0)),
                  pl.BlockSpec((HALO, DFF), lambda i: (jnp.minimum((i + 1) * per, L // HALO - 1), 0)),
                  _const_spec((1, D)), _const_spec((6, D)), _const_spec((2 * DFF, D)), _const_spec((3, DFF))],
        out_specs=[blk(D), blk(2 * DFF), blk(D), _acc_spec((1, D)), _acc_spec((6, D))],
        out_shape=[jax.ShapeDtypeStruct((L, D), f32), jax.ShapeDtypeStruct((L, 2 * DFF), bf16), jax.ShapeDtypeStruct((L, D), bf16),
                   jax.ShapeDtypeStruct((1, D), f32), jax.ShapeDtypeStruct((6, D), f32)],
        compiler_params=_CP(dimension_semantics=_ARB),
    )(x1, dx2, da, dgc, dgc, dgc, nw2, modx, wup_t, cw)


def _matmul_tn(a, b, name):
    k, m = a.shape
    n = b.shape[1]
    divs = lambda d: [c for c in range(d, 0, -128) if d % c == 0]
    _, tm, tn = min((m * (n // cn) + n * (m // cm), cm, cn) for cm in divs(m) for cn in divs(n) if cm * cn * 4 <= ACC_TILE_BYTES)
    tk = next(c for c in (512, 768, 256, 128) if k % c == 0)
    nk = k // tk

    def body(a_ref, b_ref, o_ref, acc):
        q = pl.program_id(2)

        @pl.when(q == 0)
        def _():
            acc[...] = jnp.zeros_like(acc)

        acc[...] += dtn(a_ref[...], b_ref[...])

        @pl.when(q == nk - 1)
        def _():
            o_ref[...] = acc[...].astype(bf16)

    return pl.pallas_call(
        body, name=name, grid=(m // tm, n // tn, nk),
        in_specs=[pl.BlockSpec((tk, tm), lambda i, j, q: (q, i)), pl.BlockSpec((tk, tn), lambda i, j, q: (q, j))],
        out_specs=pl.BlockSpec((tm, tn), lambda i, j, q: (i, j)),
        out_shape=jax.ShapeDtypeStruct((m, n), bf16),
        scratch_shapes=[pltpu.VMEM((tm, tn), f32)],
        compiler_params=_CP(dimension_semantics=("parallel", "parallel", "arbitrary")),
    )(a, b)


def _adamw(w, g, m, v, name):
    c1, c2 = 1.0 - B1 ** STEP, 1.0 - B2 ** STEP

    def body(w_ref, g_ref, m_ref, v_ref, d_ref, nm_ref, nv_ref):
        gg = g_ref[...]
        nm = B1 * m_ref[...] + (1.0 - B1) * gg
        nv = B2 * v_ref[...] + (1.0 - B2) * jnp.square(gg)
        d_ref[...] = -LR * ((nm / c1) / (jnp.sqrt(nv / c2) + AEPS) + WD * w_ref[...])
        nm_ref[...], nv_ref[...] = nm, nv

    return pl.pallas_call(body, name=name, out_shape=[jax.ShapeDtypeStruct(w.shape, f32)] * 3, compiler_params=_CP())(w, g, m, v)


SMALL = ["conv_w", "c_ctx", "norm1_w", "s5_lambda_re_f", "s5_lambda_im_f", "s5_log_step_f", "s5_lambda_re_b", "s5_lambda_im_b",
         "s5_log_step_b", "s5_b_re", "s5_b_im", "s5_c_re", "s5_c_im", "s5_d", "s5_b_glu", "ret_log_decay_f", "ret_log_decay_b",
         "norm2_w", "conv_b", "final_norm_w"]
WEIGHTS = ["c_ctx", "w_mod", "b_mod", "norm1_w", "w_in", "s5_lambda_re_f", "s5_lambda_im_f", "s5_log_step_f", "s5_lambda_re_b",
           "s5_lambda_im_b", "s5_log_step_b", "s5_b_re", "s5_b_im", "s5_c_re", "s5_c_im", "s5_d", "s5_w_glu", "s5_b_glu",
           "ret_log_decay_f", "ret_log_decay_b", "w_out", "norm2_w", "w_up", "conv_w", "conv_b", "w_down", "final_norm_w"]


def _pack_small(vals):
    flat, offs, o = [], [], 0
    for a in vals:
        n = a.size
        npad = -n % 128
        flat.append(jnp.pad(a.reshape(-1), (0, npad)))
        offs.append((o, n))
        o += n + npad
    tail = -o % 1024
    if tail:
        flat.append(jnp.zeros((tail,), f32))
    return jnp.concatenate(flat).reshape(-1, 128), offs


def _unpack_small(packed, offs, shapes):
    flat = packed.reshape(-1)
    return [flat[o:o + n].reshape(s) for (o, n), s in zip(offs, shapes)]


def _rope_tables(L, nctx_rows):
    t = np.arange(L)
    inv = (ROPE_THETA ** (-np.arange(DH // 4, dtype=np.float64) / (DH // 4))).astype(np.float32)
    ang = np.concatenate([(t // GRID_W).astype(np.float32)[:, None] * inv, (t % GRID_W).astype(np.float32)[:, None] * inv], axis=-1)
    cos = np.repeat(np.cos(ang).astype(np.float32), 2, axis=1)
    sin = np.repeat(np.sin(ang).astype(np.float32), 2, axis=1) * np.tile(np.array([-1.0, 1.0], np.float32), DH // 2)
    cosf = np.concatenate([np.ones((nctx_rows, DH), np.float32), cos], axis=0)
    sins = np.concatenate([np.zeros((nctx_rows, DH), np.float32), sin], axis=0)
    return jnp.asarray(cosf), jnp.asarray(sins)


def kernel(x, c, ctx, c_ctx, w_mod, b_mod, norm1_w, w_in, s5_lambda_re_f, s5_lambda_im_f, s5_log_step_f, s5_lambda_re_b, s5_lambda_im_b, s5_log_step_b, s5_b_re, s5_b_im, s5_c_re, s5_c_im, s5_d, s5_w_glu, s5_b_glu, ret_log_decay_f, ret_log_decay_b, w_out, norm2_w, w_up, conv_w, conv_b, w_down, final_norm_w, loss_target, m_c_ctx, m_w_mod, m_b_mod, m_norm1_w, m_w_in, m_s5_lambda_re_f, m_s5_lambda_im_f, m_s5_log_step_f, m_s5_lambda_re_b, m_s5_lambda_im_b, m_s5_log_step_b, m_s5_b_re, m_s5_b_im, m_s5_c_re, m_s5_c_im, m_s5_d, m_s5_w_glu, m_s5_b_glu, m_ret_log_decay_f, m_ret_log_decay_b, m_w_out, m_norm2_w, m_w_up, m_conv_w, m_conv_b, m_w_down, m_final_norm_w, v_c_ctx, v_w_mod, v_b_mod, v_norm1_w, v_w_in, v_s5_lambda_re_f, v_s5_lambda_im_f, v_s5_log_step_f, v_s5_lambda_re_b, v_s5_lambda_im_b, v_s5_log_step_b, v_s5_b_re, v_s5_b_im, v_s5_c_re, v_s5_c_im, v_s5_d, v_s5_w_glu, v_s5_b_glu, v_ret_log_decay_f, v_ret_log_decay_b, v_w_out, v_norm2_w, v_w_up, v_conv_w, v_conv_b, v_w_down, v_final_norm_w):
    args = dict(locals())
    W = {n: args[n] for n in WEIGHTS}
    M = {n: args["m_" + n] for n in WEIGHTS}
    V = {n: args["v_" + n] for n in WEIGHTS}
    me = _me()
    x2, ctx2, tgt = x[0], ctx[0], loss_target[0]
    L, Lc = x2.shape[0], ctx2.shape[0]
    assert Lc == R and L % R == 0 and L % GRID_W == 0
    nctx = Lc // T

    c_all = _all_gather_small(jnp.pad(c, ((0, 7), (0, 0))), "gather_c")[:, 0, :]
    c9 = jnp.concatenate([c_all, c_ctx[None], jnp.zeros((7, D), f32)], axis=0)
    w_mod_l = w_mod[0]
    ncol = w_mod_l.shape[1]
    m_part = _ada_fwd(c9, w_mod_l, "ada_fwd")
    m_all = _all_gather_small(m_part, "gather_mod").transpose(1, 0, 2).reshape(16, 6, D)
    modx, modc = _mod_select(m_all, b_mod.reshape(6, D), "mod_select")

    w_in_tl, w_up_tl = w_in[0].T.astype(bf16), w_up[0].T.astype(bf16)
    w_out_l, w_down_l, w_glu_l = w_out[0].astype(bf16), w_down[0].astype(bf16), s5_w_glu[0].astype(bf16)
    (w_in_g,) = _exchange([w_in_tl], False, "gather_w_in")
    w_in_t = w_in_g.reshape(INC, D)
    per_cv = conv_w.shape[2]
    conv_pad = jnp.pad(conv_w[0], ((0, 5), (0, 128 * 3 - per_cv)))
    conv_f = _all_gather_small(conv_pad, "gather_conv")[:, :3, :per_cv].transpose(1, 0, 2).reshape(3, DFF)

    gn = lambda a: a[0]
    bre_t, bim_t = gn(s5_b_re).transpose(2, 0, 1), gn(s5_b_im).transpose(2, 0, 1)
    cre_m, cim_m = _blockdiag(gn(s5_c_re)), _blockdiag(gn(s5_c_im))
    tril = jnp.tril(jnp.ones((T, T), bf16))
    triu = tril.T
    s5 = {}
    for tag, lre, lim, ls in (("f", s5_lambda_re_f, s5_lambda_im_f, s5_log_step_f), ("b", s5_lambda_re_b, s5_lambda_im_b, s5_log_step_b)):
        lam, bbr, bbi, tab = _s5_prep(gn(lre), gn(lim), gn(ls).reshape(S5G, 1), bre_t, bim_t, "s5_prep_" + tag)
        s5[tag] = dict(lam=lam.reshape(2, GN), tab=tab.reshape(8, T, GN),
                       mats=(_blockdiag(bbr.transpose(1, 0, 2)), _blockdiag(bbi.transpose(1, 0, 2)), cre_m, cim_m))

    nw1, nw2, fnw = norm1_w, norm2_w, final_norm_w[None]
    p_ext = _f1_fwd(x2, ctx2, modx, modc, nw1, w_in_t, "f1_fwd")
    yf, hb_f, w_out_g, w_down_g, w_glu_g = _s5_fwd(p_ext, s5["f"]["mats"], s5["f"]["tab"], s5["f"]["lam"], tril, False, nctx,
                                                   "s5_fwd_f", cargo=([w_out_l, w_down_l, w_glu_l], False))
    yb, hb_b, w_up_g = _s5_fwd(p_ext, s5["b"]["mats"], s5["b"]["tab"], s5["b"]["lam"], triu, True, nctx, "s5_fwd_b",
                               cargo=([w_up_tl], False))
    w_out_f, w_down_f, w_glu_f = w_out_g.reshape(D, D), w_down_g.reshape(DFF, D), w_glu_g.reshape(S5W, S5W)
    w_up_t = w_up_g.reshape(2 * DFF, D)
    cosf, sins = _rope_tables(L, Lc)
    ld8 = lambda ld: jnp.pad(jnp.broadcast_to(ld[0][:, None], (RH, 128)), ((0, 8 - RH), (0, 0)))
    ldf8, ldb8 = ld8(ret_log_decay_f), ld8(ret_log_decay_b)
    of, rp_f = _ret_fwd(p_ext, cosf, sins, ldf8, False, nctx, "ret_fwd_f")
    ob, rp_b = _ret_fwd(p_ext, cosf, sins, ldb8, True, nctx, "ret_fwd_b")
    x1 = _mix_fwd(x2, yf, yb, of, ob, p_ext, s5_d, s5_b_glu, modx, w_glu_f, w_out_f, "mix_fwd")

    (dx2, da, dgc, f_act, dffn, loss_acc, g_fnw, g_gate2, g_cb, g_cw) = _ffn_fwd(
        x1, tgt, nw2, modx, w_up_t, conv_f, conv_b, w_down_f, fnw, "ffn_fwd")
    dx1, dag, h2, g_nw2, dmx2 = _ffn_bwd(x1, dx2, da, dgc, nw2, modx, w_up_t, conv_f, "ffn_bwd")
    gw_down = _matmul_tn(f_act, dffn, "dw_down").reshape(NDEV, -1, D)
    gw_up_t = _matmul_tn(dag, h2, "dw_up").reshape(NDEV, -1, D)
    (dy_e, dud_e, do_e, dg_e, cat, dmix, s_act, dz, g_d, g_bglu, g_gate1) = _mix_bwd(
        x2, yf, yb, of, ob, p_ext, s5_d, s5_b_glu, modx, w_glu_f, w_out_f, dx1, "mix_bwd")
    gw_out = _matmul_tn(cat, dmix, "dw_out").reshape(NDEV, -1, D)
    gw_glu = _matmul_tn(s_act, dz, "dw_glu").reshape(NDEV, -1, S5W)
    dq_f, dk_f, dv_f, gld_f = _ret_bwd(p_ext, cosf, sins, ldf8, rp_f, do_e, False, nctx, "ret_bwd_f")
    dq_b, dk_b, dv_b, gld_b = _ret_bwd(p_ext, cosf, sins, ldb8, rp_b, do_e, True, nctx, "ret_bwd_b")
    du_f, dbre_f, dbim_f, dcre_f, dcim_f, dl_f, l_up, l_down = _s5_bwd(
        p_ext, dy_e, hb_f, s5["f"]["mats"], s5["f"]["tab"], s5["f"]["lam"], tril, triu, False, nctx, "s5_bwd_f",
        cargo=([gw_up_t, gw_down], True))
    du_b, dbre_b, dbim_b, dcre_b, dcim_b, dl_b, l_out, l_glu = _s5_bwd(
        p_ext, dy_e, hb_b, s5["b"]["mats"], s5["b"]["tab"], s5["b"]["lam"], triu, tril, True, nctx, "s5_bwd_b",
        cargo=([gw_out, gw_glu], True))
    grad_x, dp_ext, h1, g_nw1, dmx1, dmc1 = _f1_bwd(
        x2, ctx2, modx, modc, nw1, w_in_t, dx1, (du_f, du_b, dud_e, dq_f, dq_b, dk_f, dk_b, dv_f, dv_b, dg_e), "f1_bwd")
    gw_in_t = _matmul_tn(dp_ext, h1, "dw_in").reshape(NDEV, -1, D)
    (l_in,) = _exchange([gw_in_t], True, "scatter_dw_in")

    tg = lambda m: _blockdiag_extract(m).transpose(1, 0, 2)
    gs5 = {}
    for tag, lre, lim, ls, dl, dbr, dbi in (("f", s5_lambda_re_f, s5_lambda_im_f, s5_log_step_f, dl_f, dbre_f, dbim_f),
                                             ("b", s5_lambda_re_b, s5_lambda_im_b, s5_log_step_b, dl_b, dbre_b, dbim_b)):
        gs5[tag] = _s5_prep_bwd(gn(lre), gn(lim), gn(ls).reshape(S5G, 1), bre_t, bim_t, dl.reshape(2, S5G, S5N), tg(dbr), tg(dbi),
                                "s5_prep_bwd_" + tag)
    g_bre = (gs5["f"][3] + gs5["b"][3]).transpose(1, 2, 0)
    g_bim = (gs5["f"][4] + gs5["b"][4]).transpose(1, 2, 0)
    g_cre = _blockdiag_extract(dcre_f + dcre_b)
    g_cim = _blockdiag_extract(dcim_f + dcim_b)

    dmx = dmx1 + dmx2
    dmx = dmx.at[2].set(g_gate1[0]).at[5].set(g_gate2[0])
    dm_me = jnp.stack([dmx.reshape(-1), dmc1.reshape(-1)], axis=0)
    dm_all = _all_gather_small(jnp.pad(dm_me, ((0, 6), (0, 0))), "gather_dmod")
    dmx_all, dmc_all = dm_all[:, 0, :], dm_all[:, 1, :]
    my_cols = lambda a: lax.dynamic_slice(a, (0, me * ncol), (NDEV, ncol))
    gw_mod, g_bmod, dc9 = _ada_bwd(c9, dmx_all, dmc_all, my_cols(dmx_all), my_cols(dmc_all), w_mod_l, "ada_bwd")

    small = {
        "conv_w": g_cw, "c_ctx": dc9[8], "norm1_w": g_nw1, "s5_lambda_re_f": gs5["f"][0], "s5_lambda_im_f": gs5["f"][1],
        "s5_log_step_f": gs5["f"][2], "s5_lambda_re_b": gs5["b"][0], "s5_lambda_im_b": gs5["b"][1], "s5_log_step_b": gs5["b"][2],
        "s5_b_re": g_bre, "s5_b_im": g_bim, "s5_c_re": g_cre, "s5_c_im": g_cim, "s5_d": g_d, "s5_b_glu": g_bglu,
        "ret_log_decay_f": gld_f[:RH, 0], "ret_log_decay_b": gld_b[:RH, 0], "norm2_w": g_nw2, "conv_b": g_cb, "final_norm_w": g_fnw,
    }
    packed, soffs = _pack_small([small[n].astype(f32) for n in SMALL])
    red = _all_reduce_small(packed, "reduce_small")
    sshapes = [(3, DFF) if n == "conv_w" else W[n].shape for n in SMALL]
    G = dict(zip(SMALL, _unpack_small(red, soffs, sshapes)))
    G["conv_w"] = lax.dynamic_slice(G["conv_w"], (0, me * per_cv), (3, per_cv))[None]
    G["b_mod"] = g_bmod.reshape(b_mod.shape)
    G["w_mod"] = gw_mod[None]
    G["w_in"] = _sum8(l_in, "sum_dw_in").T[None]
    G["w_up"] = _sum8(l_up, "sum_dw_up").T[None]
    G["w_out"] = _sum8(l_out, "sum_dw_out")[None]
    G["w_down"] = _sum8(l_down, "sum_dw_down")[None]
    G["s5_w_glu"] = _sum8(l_glu, "sum_dw_glu")[None]

    delta, new_m, new_v = {}, {}, {}
    sm_names = SMALL[1:] + ["b_mod"]
    pk = lambda d: _pack_small([d[n].astype(f32) for n in sm_names])
    (pw, aoffs), (pg, _), (pm, _), (pv, _) = pk(W), pk(G), pk(M), pk(V)
    pd, pnm, pnv = _adamw(pw, pg, pm, pv, "adamw_small")
    shapes = [W[n].shape for n in sm_names]
    for dst, src in ((delta, pd), (new_m, pnm), (new_v, pnv)):
        dst.update(zip(sm_names, _unpack_small(src, aoffs, shapes)))
    for n in ["w_mod", "w_in", "w_out", "w_up", "w_down", "s5_w_glu", "conv_w"]:
        d, nm, nv = _adamw(W[n][0], G[n][0], M[n][0], V[n][0], "adamw_" + n)
        delta[n], new_m[n], new_v[n] = d[None], nm[None], nv[None]

    loss = lax.psum(loss_acc[0, 0], ("x", "y", "c"))
    return (loss, grad_x[None], *[G[n] for n in WEIGHTS], *[delta[n] for n in WEIGHTS], *[new_m[n] for n in WEIGHTS],
            *[new_v[n] for n in WEIGHTS])
```

```python
import functools

import numpy as np
import jax
import jax.numpy as jnp
from jax import lax
from jax.experimental import pallas as pl
from jax.experimental.pallas import tpu as pltpu

f32, bf16 = jnp.float32, jnp.bfloat16

D = 1024
S5W, S5G, S5P, S5N = 512, 32, 16, 64
GN = S5G * S5N
NB = 2
KB, NS = S5W // NB, GN // NB
GB = S5G // NB
RH, DH = 4, 128
RW = RH * DH
INC = S5W + 4 * RW
DFF = 2816
T = 128
R = 256
RF = 128
HALO = 8
EPS = 1e-6
ROPE_THETA = 10000.0
GRID_W = 64
NDEV = 8
LR, B1, B2, AEPS, WD, STEP = 0.001, 0.9, 0.999, 1e-08, 0.01, 10
VMEM_LIMIT = 60 * 1024 * 1024
ACC_TILE_BYTES = 6 * 1024 * 1024
MESH = pl.DeviceIdType.MESH

_CP = functools.partial(pltpu.CompilerParams, vmem_limit_bytes=VMEM_LIMIT)
_ARB = ("arbitrary",)
_ANY = pl.BlockSpec(memory_space=pl.ANY)


def _dg(a, b, dims):
    return lax.dot_general(a.astype(bf16), b.astype(bf16), (dims, ((), ())), preferred_element_type=f32)


@jax.custom_vjp
def dnn(a, b):
    return _dg(a, b, ((1,), (0,)))


@jax.custom_vjp
def dnt(a, b):
    return _dg(a, b, ((1,), (1,)))


@jax.custom_vjp
def dtn(a, b):
    return _dg(a, b, ((0,), (0,)))


dnn.defvjp(lambda a, b: (dnn(a, b), (a, b)), lambda r, g: (dnt(g, r[1]).astype(r[0].dtype), dtn(r[0], g).astype(r[1].dtype)))
dnt.defvjp(lambda a, b: (dnt(a, b), (a, b)), lambda r, g: (dnn(g, r[1]).astype(r[0].dtype), dtn(g, r[0]).astype(r[1].dtype)))
dtn.defvjp(lambda a, b: (dtn(a, b), (a, b)), lambda r, g: (dnt(r[1], g).astype(r[0].dtype), dnn(r[0], g).astype(r[1].dtype)))


def _rms(t, w):
    return t * lax.rsqrt(jnp.mean(t * t, axis=-1, keepdims=True) + EPS) * w


def _mod(h, shift, scale):
    return h * (1.0 + scale) + shift


def _const_spec(shape):
    n = len(shape)
    return pl.BlockSpec(shape, lambda i, _n=n: (0,) * _n, pipeline_mode=pl.Buffered(1))


def _acc_spec(shape):
    n = len(shape)
    return pl.BlockSpec(shape, lambda i, _n=n: (0,) * _n)


def _me():
    return 4 * lax.axis_index("x") + 2 * lax.axis_index("y") + lax.axis_index("c")


def _peer(r):
    x, y, c = lax.axis_index("x"), lax.axis_index("y"), lax.axis_index("c")
    px = 1 - x if (r >> 2) & 1 else x
    py = 1 - y if (r >> 1) & 1 else y
    pc = 1 - c if r & 1 else c
    return (px, py, pc), 4 * px + 2 * py + pc


def _all_gather_small(v, name):
    r, c = v.shape

    def body(v_ref, out_ref, send_sems, recv_sems):
        me = _me()
        out_ref[me] = v_ref[...]
        sends = []
        for k in range(1, NDEV):
            peer, _ = _peer(k)
            cp = pltpu.make_async_remote_copy(src_ref=v_ref, dst_ref=out_ref.at[me], send_sem=send_sems.at[k - 1],
                                              recv_sem=recv_sems.at[k - 1], device_id=peer, device_id_type=MESH)
            cp.start()
            sends.append(cp)
        for k in range(1, NDEV):
            peer, pidx = _peer(k)
            pltpu.make_async_remote_copy(src_ref=v_ref, dst_ref=out_ref.at[pidx], send_sem=send_sems.at[k - 1],
                                         recv_sem=recv_sems.at[k - 1], device_id=peer, device_id_type=MESH).wait_recv()
        for cp in sends:
            cp.wait_send()

    return pl.pallas_call(
        body, name=name, out_shape=jax.ShapeDtypeStruct((NDEV, r, c), v.dtype),
        in_specs=[pl.BlockSpec(memory_space=pltpu.VMEM)], out_specs=pl.BlockSpec(memory_space=pltpu.VMEM),
        scratch_shapes=[pltpu.SemaphoreType.DMA((NDEV - 1,)), pltpu.SemaphoreType.DMA((NDEV - 1,))],
        compiler_params=_CP(),
    )(v)


def _all_reduce_small(v, name):
    r, c = v.shape

    def body(v_ref, out_ref, land, send_sems, recv_sems):
        me = _me()
        land[me] = v_ref[...]
        sends = []
        for k in range(1, NDEV):
            peer, _ = _peer(k)
            cp = pltpu.make_async_remote_copy(src_ref=v_ref, dst_ref=land.at[me], send_sem=send_sems.at[k - 1],
                                              recv_sem=recv_sems.at[k - 1], device_id=peer, device_id_type=MESH)
            cp.start()
            sends.append(cp)
        for k in range(1, NDEV):
            peer, pidx = _peer(k)
            pltpu.make_async_remote_copy(src_ref=v_ref, dst_ref=land.at[pidx], send_sem=send_sems.at[k - 1],
                                         recv_sem=recv_sems.at[k - 1], device_id=peer, device_id_type=MESH).wait_recv()
        for cp in sends:
            cp.wait_send()
        acc = land[0]
        for j in range(1, NDEV):
            acc = acc + land[j]
        out_ref[...] = acc

    return pl.pallas_call(
        body, name=name, out_shape=jax.ShapeDtypeStruct((r, c), v.dtype),
        in_specs=[pl.BlockSpec(memory_space=pltpu.VMEM)], out_specs=pl.BlockSpec(memory_space=pltpu.VMEM),
        scratch_shapes=[pltpu.VMEM((NDEV, r, c), v.dtype), pltpu.SemaphoreType.DMA((NDEV - 1,)),
                        pltpu.SemaphoreType.DMA((NDEV - 1,))],
        compiler_params=_CP(),
    )(v)


class _Exchange:
    def __init__(self, srcs, dsts, send_sems, recv_sems, local_sems, scatter):
        me = _me()
        n = len(srcs)
        self.sends, self.recvs, self.locals = [], [], []
        for a, (s, d) in enumerate(zip(srcs, dsts)):
            self.locals.append(pltpu.make_async_copy(s.at[me] if scatter else s, d.at[me], local_sems.at[a]))
        for k in range(1, NDEV):
            peer, pidx = _peer(k)
            for a, (s, d) in enumerate(zip(srcs, dsts)):
                src = s.at[pidx] if scatter else s
                sem = (k - 1) * n + a
                for dst, out in ((d.at[me], self.sends), (d.at[pidx], self.recvs)):
                    out.append(pltpu.make_async_remote_copy(src_ref=src, dst_ref=dst, send_sem=send_sems.at[sem],
                                                            recv_sem=recv_sems.at[sem], device_id=peer, device_id_type=MESH))

    def start(self):
        for cp in self.locals + self.sends:
            cp.start()

    def wait(self):
        for cp in self.recvs:
            cp.wait_recv()
        for cp in self.sends:
            cp.wait_send()
        for cp in self.locals:
            cp.wait()


def _exchange_shapes(arrays, scatter):
    return [jax.ShapeDtypeStruct(a.shape if scatter else (NDEV,) + a.shape, a.dtype) for a in arrays]


def _exchange_sems(n):
    return [pltpu.SemaphoreType.DMA(((NDEV - 1) * n,)), pltpu.SemaphoreType.DMA(((NDEV - 1) * n,)), pltpu.SemaphoreType.DMA((n,))]


def _exchange(arrays, scatter, name):
    n = len(arrays)

    def body(*refs):
        ex = _Exchange(refs[:n], refs[n:2 * n], *refs[2 * n:], scatter)
        ex.start()
        ex.wait()

    return pl.pallas_call(body, name=name, out_shape=_exchange_shapes(arrays, scatter), in_specs=[_ANY] * n,
                          out_specs=[_ANY] * n, scratch_shapes=_exchange_sems(n), compiler_params=_CP())(*arrays)


def _ride(refs, n_in, n_out, cargo, nsteps):
    if cargo is None:
        return
    arrays, scatter = cargo
    n = len(arrays)
    srcs = refs[n_in:n_in + n]
    dsts = refs[n_in + n + n_out:n_in + 2 * n + n_out]
    ex = _Exchange(srcs, dsts, *refs[-3:], scatter)

    @pl.when(pl.program_id(0) == 0)
    def _():
        ex.start()

    @pl.when(pl.program_id(0) == nsteps - 1)
    def _():
        ex.wait()


def _sum8(land, name):
    _, r, c = land.shape
    rb = next((b for b in (256, 64, 32) if r % b == 0), r)

    def body(l_ref, o_ref):
        acc = l_ref[0].astype(f32)
        for j in range(1, NDEV):
            acc = acc + l_ref[j].astype(f32)
        o_ref[...] = acc

    return pl.pallas_call(
        body, name=name, grid=(r // rb,), out_shape=jax.ShapeDtypeStruct((r, c), f32),
        in_specs=[pl.BlockSpec((NDEV, rb, c), lambda i: (0, i, 0))], out_specs=pl.BlockSpec((rb, c), lambda i: (i, 0)),
        compiler_params=_CP(dimension_semantics=("parallel",)),
    )(land)


def _ada_fwd(c9, w_mod_l, name):
    def body(c_ref, w_ref, o_ref):
        o_ref[...] = dnn(jax.nn.silu(c_ref[...]), w_ref[...])

    return pl.pallas_call(body, name=name, out_shape=jax.ShapeDtypeStruct((16, w_mod_l.shape[1]), f32),
                          compiler_params=_CP())(c9, w_mod_l)


def _mod_select(m_all, b_mod6, name):
    def body(m_ref, b_ref, mx_ref, mc_ref):
        me = _me()
        mx_ref[...] = m_ref[me] + b_ref[...]
        mc_ref[...] = m_ref[8] + b_ref[...]

    return pl.pallas_call(body, name=name, out_shape=[jax.ShapeDtypeStruct((6, D), f32)] * 2, compiler_params=_CP())(m_all, b_mod6)


def _ada_bwd(c9, dmx_all, dmc_all, dmx_l, dmc_l, w_mod_l, name):
    ncol = w_mod_l.shape[1]

    def rowsum(r):
        acc = r[0:1]
        for j in range(1, NDEV):
            acc = acc + r[j:j + 1]
        return acc

    def body(c_ref, xa_ref, ca_ref, xl_ref, cl_ref, w_ref, gw_ref, gb_ref, dc_ref):
        s9, vjp = jax.vjp(jax.nn.silu, c_ref[...])
        dm9 = jnp.concatenate([xl_ref[...], rowsum(cl_ref[...]), jnp.zeros((7, ncol), f32)], axis=0)
        gw_ref[...] = dtn(s9, dm9)
        gb_ref[...] = rowsum(xa_ref[...]) + rowsum(ca_ref[...])
        dc_ref[...] = vjp(dnt(dm9, w_ref[...]))[0]

    return pl.pallas_call(
        body, name=name,
        out_shape=[jax.ShapeDtypeStruct((D, ncol), f32), jax.ShapeDtypeStruct((1, 6 * D), f32), jax.ShapeDtypeStruct((16, D), f32)],
        compiler_params=_CP())(c9, dmx_all, dmc_all, dmx_l, dmc_l, w_mod_l)


def _s5_disc(lre, lim, lstep, bre, bim):
    s = jnp.exp(lstep)
    ar, ai = lre * s, lim * s
    e = jnp.exp(ar)
    lbr, lbi = e * jnp.cos(ai), e * jnp.sin(ai)
    nr, ni = lbr - 1.0, lbi
    den = lre * lre + lim * lim
    cr, ci = (nr * lre + ni * lim) / den, (ni * lre - nr * lim) / den
    return lbr, lbi, cr[None] * bre - ci[None] * bim, cr[None] * bim + ci[None] * bre


def _s5_prep(lre, lim, lstep, bre, bim, name):
    def body(lre_ref, lim_ref, ls_ref, bre_ref, bim_ref, lam_ref, bbr_ref, bbi_ref, tab_ref):
        lbr, lbi, bbr, bbi = _s5_disc(lre_ref[...], lim_ref[...], ls_ref[...], bre_ref[...], bim_ref[...])
        lam_ref[0], lam_ref[1] = lbr, lbi
        bbr_ref[...], bbi_ref[...] = bbr, bbi
        s = jnp.exp(ls_ref[...])
        ar, ai = (lre_ref[...] * s)[None], (lim_ref[...] * s)[None]
        up = lax.broadcasted_iota(jnp.int32, (T, S5G, S5N), 0).astype(f32)
        for k, t in ((0, up), (4, T - 1.0 - up)):
            ea, ang = jnp.exp(t * ar), t * ai
            cs, sn = jnp.cos(ang), jnp.sin(ang)
            tab_ref[k + 0], tab_ref[k + 1] = cs / ea, -sn / ea
            tab_ref[k + 2], tab_ref[k + 3] = ea * cs, ea * sn

    gshape = (S5G, S5N)
    return pl.pallas_call(
        body, name=name,
        out_shape=[jax.ShapeDtypeStruct((2,) + gshape, f32), jax.ShapeDtypeStruct((S5P,) + gshape, f32),
                   jax.ShapeDtypeStruct((S5P,) + gshape, f32), jax.ShapeDtypeStruct((8, T) + gshape, f32)],
        compiler_params=_CP())(lre, lim, lstep, bre, bim)


def _s5_prep_bwd(lre, lim, lstep, bre, bim, dlacc, dbbr, dbbi, name):
    def body(lre_ref, lim_ref, ls_ref, bre_ref, bim_ref, dl_ref, dbr_ref, dbi_ref, glre, glim, gls, gbre, gbim):
        prim = (lre_ref[...], lim_ref[...], ls_ref[...], bre_ref[...], bim_ref[...])
        (lbr, lbi, _, _), vjp = jax.vjp(_s5_disc, *prim)
        x, y = dl_ref[0], dl_ref[1]
        n2 = lbr * lbr + lbi * lbi
        dlr, dli = (x * lbr - y * lbi) / n2, (x * lbi + y * lbr) / n2
        g = vjp((dlr, dli, dbr_ref[...], dbi_ref[...]))
        glre[...], glim[...], gls[...], gbre[...], gbim[...] = g

    gshape = (S5G, S5N)
    return pl.pallas_call(
        body, name=name,
        out_shape=[jax.ShapeDtypeStruct(gshape, f32), jax.ShapeDtypeStruct(gshape, f32), jax.ShapeDtypeStruct((S5G, 1), f32),
                   jax.ShapeDtypeStruct((S5P,) + gshape, f32), jax.ShapeDtypeStruct((S5P,) + gshape, f32)],
        compiler_params=_CP())(lre, lim, lstep, bre, bim, dlacc, dbbr, dbbi)


def _blockdiag(m_gpn):
    m = m_gpn.reshape(NB, GB, S5P, S5N)
    eye = jnp.eye(GB, dtype=m.dtype)
    return (m[:, :, :, None, :] * eye[None, :, None, :, None]).reshape(NB, KB, NS).astype(bf16)


def _blockdiag_extract(m):
    m = m.reshape(NB, GB, S5P, GB, S5N)
    return jnp.einsum("bgpgn->bgpn", m).reshape(S5G, S5P, S5N)


def _cmul(ar, ai, br, bi):
    return ar * br - ai * bi, ar * bi + ai * br


def _tri_dot(tri, z):
    zh = z.astype(bf16)
    zl = (z - zh.astype(f32)).astype(bf16)
    return jnp.dot(tri, zh, preferred_element_type=f32) + jnp.dot(tri, zl, preferred_element_type=f32)


def _scan_chunk(xr, xi, wir, wii, wfr, wfi, tri, gr, gi):
    zr, zi = _cmul(wir, wii, xr, xi)
    return _cmul(wfr, wfi, _tri_dot(tri, zr) + gr, _tri_dot(tri, zi) + gi)


def _proj_in(u, m_ref):
    return jnp.concatenate([jnp.dot(u[:, b * KB:(b + 1) * KB], m_ref[b], preferred_element_type=f32) for b in range(NB)], axis=1)


def _proj_out(h, m_ref):
    return jnp.concatenate([dnt(h[:, b * NS:(b + 1) * NS], m_ref[b]) for b in range(NB)], axis=1)


def _outer_acc(a, h):
    return jnp.stack([dtn(a[:, b * KB:(b + 1) * KB], h[:, b * NS:(b + 1) * NS]) for b in range(NB)], axis=0)


def _idx_fwd(nctx, nch):
    return lambda i: i


def _idx_rev(nctx, nch):
    return lambda i: jnp.where(i < nctx, nctx - 1 - i, nch + nctx - 1 - i)


def _s5_fwd(p_ext, mats, tab, lam, tri, rev, nctx, name, cargo=None):
    n = p_ext.shape[0]
    nch = n // T
    idx = (_idx_rev if rev else _idx_fwd)(nctx, nch)
    bre, bim, cre, cim = mats
    tb = 4 if rev else 0
    carr, cscat = cargo if cargo else ([], False)
    nc = len(carr)

    def body(*refs):
        u_ref, bre_ref, bim_ref, cre_ref, cim_ref, tab_ref, lam_ref, tri_ref = refs[:8]
        y_ref, hb_ref = refs[8 + nc:10 + nc]
        h_s = refs[10 + 2 * nc]
        _ride(refs, 8, 2, cargo, nch)

        @pl.when(pl.program_id(0) == 0)
        def _():
            h_s[...] = jnp.zeros_like(h_s)

        u = u_ref[...]
        xr, xi = _proj_in(u, bre_ref), _proj_in(u, bim_ref)
        hp = h_s[...]
        hb_ref[0] = hp
        gr, gi = _cmul(lam_ref[0:1], lam_ref[1:2], hp[0:1], hp[1:2])
        hr, hi = _scan_chunk(xr, xi, tab_ref[tb], tab_ref[tb + 1], tab_ref[tb + 2], tab_ref[tb + 3], tri_ref[...], gr, gi)
        last = 0 if rev else T - 1
        h_s[0:1] = hr[last:last + 1]
        h_s[1:2] = hi[last:last + 1]
        y_ref[...] = _proj_out(hr, cre_ref) - _proj_out(hi, cim_ref)

    mspec = _const_spec((NB, KB, NS))
    return pl.pallas_call(
        body, name=name, grid=(nch,),
        in_specs=[pl.BlockSpec((T, S5W), lambda i: (idx(i), 0)), mspec, mspec, mspec, mspec,
                  _const_spec((8, T, GN)), _const_spec((2, GN)), _const_spec((T, T))] + [_ANY] * nc,
        out_specs=[pl.BlockSpec((T, S5W), lambda i: (idx(i), 0)), pl.BlockSpec((1, 2, GN), lambda i: (i, 0, 0))] + [_ANY] * nc,
        out_shape=[jax.ShapeDtypeStruct((n, S5W), f32), jax.ShapeDtypeStruct((nch, 2, GN), f32)] + _exchange_shapes(carr, cscat),
        scratch_shapes=[pltpu.VMEM((2, GN), f32)] + (_exchange_sems(nc) if nc else []),
        compiler_params=_CP(dimension_semantics=_ARB),
    )(p_ext, bre, bim, cre, cim, tab, lam, tri, *carr)


def _s5_bwd(p_ext, dy_ext, hb, mats, tab, lam, tri_f, tri_a, rev, nctx, name, cargo=None):
    n = p_ext.shape[0]
    nch = n // T
    idx0 = (_idx_rev if rev else _idx_fwd)(nctx, nch)
    idx = lambda j: idx0(nch - 1 - j)
    bre, bim, cre, cim = mats
    tf, ta = (4, 0) if rev else (0, 4)
    carr, cscat = cargo if cargo else ([], False)
    nc = len(carr)

    def body(*refs):
        u_ref, dy_ref, hb_ref, bre_ref, bim_ref, cre_ref, cim_ref, tab_ref, lam_ref, trf_ref, tra_ref = refs[:11]
        du_ref, dbre_ref, dbim_ref, dcre_ref, dcim_ref, dlam_ref = refs[11 + nc:17 + nc]
        a_s = refs[17 + 2 * nc]
        _ride(refs, 11, 6, cargo, nch)

        @pl.when(pl.program_id(0) == 0)
        def _():
            a_s[...] = jnp.zeros_like(a_s)
            dbre_ref[...] = jnp.zeros_like(dbre_ref)
            dbim_ref[...] = jnp.zeros_like(dbim_ref)
            dcre_ref[...] = jnp.zeros_like(dcre_ref)
            dcim_ref[...] = jnp.zeros_like(dcim_ref)
            dlam_ref[...] = jnp.zeros_like(dlam_ref)

        u = u_ref[...]
        dy = dy_ref[...].astype(bf16)
        lr, li = lam_ref[0:1], lam_ref[1:2]
        xr, xi = _proj_in(u, bre_ref), _proj_in(u, bim_ref)
        hp = hb_ref[0]
        gr, gi = _cmul(lr, li, hp[0:1], hp[1:2])
        hr, hi = _scan_chunk(xr, xi, tab_ref[tf], tab_ref[tf + 1], tab_ref[tf + 2], tab_ref[tf + 3], trf_ref[...], gr, gi)
        cgr, cgi = _proj_in(dy, cre_ref), -_proj_in(dy, cim_ref)
        ac = a_s[...]
        agr, agi = _cmul(lr, -li, ac[0:1], ac[1:2])
        ar, ai = _scan_chunk(cgr, cgi, tab_ref[ta], -tab_ref[ta + 1], tab_ref[ta + 2], -tab_ref[ta + 3], tra_ref[...], agr, agi)
        first = T - 1 if rev else 0
        a_s[0:1] = ar[first:first + 1]
        a_s[1:2] = ai[first:first + 1]
        du_ref[...] = _proj_out(ar, bre_ref) + _proj_out(ai, bim_ref)
        dbre_ref[...] += _outer_acc(u, ar)
        dbim_ref[...] += _outer_acc(u, ai)
        dcre_ref[...] += _outer_acc(dy, hr)
        dcim_ref[...] -= _outer_acc(dy, hi)
        mr, mi = hr - xr, hi - xi
        dlam_ref[0:1] += jnp.sum(ar * mr + ai * mi, axis=0, keepdims=True)
        dlam_ref[1:2] += jnp.sum(ai * mr - ar * mi, axis=0, keepdims=True)

    mspec = _const_spec((NB, KB, NS))
    aspec = _acc_spec((NB, KB, NS))
    ashape = jax.ShapeDtypeStruct((NB, KB, NS), f32)
    return pl.pallas_call(
        body, name=name, grid=(nch,),
        in_specs=[pl.BlockSpec((T, S5W), lambda j: (idx(j), 0)), pl.BlockSpec((T, S5W), lambda j: (idx(j), 0)),
                  pl.BlockSpec((1, 2, GN), lambda j: (nch - 1 - j, 0, 0)), mspec, mspec, mspec, mspec,
                  _const_spec((8, T, GN)), _const_spec((2, GN)), _const_spec((T, T)), _const_spec((T, T))] + [_ANY] * nc,
        out_specs=[pl.BlockSpec((T, S5W), lambda j: (idx(j), 0)), aspec, aspec, aspec, aspec, _acc_spec((2, GN))] + [_ANY] * nc,
        out_shape=[jax.ShapeDtypeStruct((n, S5W), f32), ashape, ashape, ashape, ashape, jax.ShapeDtypeStruct((2, GN), f32)]
        + _exchange_shapes(carr, cscat),
        scratch_shapes=[pltpu.VMEM((2, GN), f32)] + (_exchange_sems(nc) if nc else []),
        compiler_params=_CP(dimension_semantics=_ARB),
    )(p_ext, dy_ext, hb, bre, bim, cre, cim, tab, lam, tri_f, tri_a, *carr)


def _swap_pairs(t):
    lane = lax.broadcasted_iota(jnp.int32, t.shape, 1)
    return jnp.where(lane % 2 == 0, pltpu.roll(t, DH - 1, axis=1), pltpu.roll(t, 1, axis=1))


def _rot(t, cosf, sins):
    return t * cosf + _swap_pairs(t) * sins


def _rot_t(d, cosf, sins):
    return d * cosf - _swap_pairs(d) * sins


def _ret_chunk(qr, kr, v, rp, ld, rev):
    pos = lax.broadcasted_iota(jnp.int32, (T, 1), 0).astype(f32)
    diff = pos - lax.broadcasted_iota(jnp.int32, (1, T), 1).astype(f32)
    if rev:
        keep, dist = diff < 0, jnp.maximum(-diff, 0.0)
        xi, zeta = jnp.exp(ld * (T - pos)), jnp.exp(ld * pos)
    else:
        keep, dist = diff >= 0, jnp.maximum(diff, 0.0)
        xi, zeta = jnp.exp(ld * (pos + 1.0)), jnp.exp(ld * (T - 1.0 - pos))
    dm = jnp.where(keep, jnp.exp(ld * dist), 0.0)
    out = dnn(dnt(qr, kr) * dm, v) + dnn(qr * xi, rp)
    rn = jnp.exp(ld * float(T)) * rp + dtn(kr * zeta, v)
    return out, rn


def _ret_fwd(p_ext, cosf, sins, ld8, rev, nctx, name):
    n = p_ext.shape[0]
    nch = n // T
    idx = (_idx_rev if rev else _idx_fwd)(nctx, nch)
    scale = DH ** -0.5

    def body(q_ref, k_ref, v_ref, cos_ref, sin_ref, ld_ref, o_ref, rp_ref, r_s):
        @pl.when(pl.program_id(0) == 0)
        def _():
            r_s[...] = jnp.zeros_like(r_s)

        cf, ss = cos_ref[...], sin_ref[...]
        for h in range(RH):
            sl = slice(h * DH, (h + 1) * DH)
            qr = _rot(q_ref[:, sl].astype(f32), cf, ss)
            kr = _rot(k_ref[:, sl].astype(f32), cf, ss) * scale
            rp = r_s[h]
            rp_ref[0, h] = rp
            out, rn = _ret_chunk(qr, kr, v_ref[:, sl].astype(f32), rp, ld_ref[h:h + 1, 0:1], rev)
            r_s[h] = rn
            o_ref[:, sl] = out

    def colspec(cb):
        return pl.BlockSpec((T, RW), lambda i, _c=cb: (idx(i), _c))

    tspec = pl.BlockSpec((T, DH), lambda i: (idx(i), 0))
    return pl.pallas_call(
        body, name=name, grid=(nch,),
        in_specs=[colspec(1), colspec(2), colspec(3), tspec, tspec, _const_spec((8, 128))],
        out_specs=[pl.BlockSpec((T, RW), lambda i: (idx(i), 0)), pl.BlockSpec((1, RH, DH, DH), lambda i: (i, 0, 0, 0))],
        out_shape=[jax.ShapeDtypeStruct((n, RW), f32), jax.ShapeDtypeStruct((nch, RH, DH, DH), f32)],
        scratch_shapes=[pltpu.VMEM((RH, DH, DH), f32)],
        compiler_params=_CP(dimension_semantics=_ARB),
    )(p_ext, p_ext, p_ext, cosf, sins, ld8)


def _ret_bwd(p_ext, cosf, sins, ld8, rprev, do_ext, rev, nctx, name):
    n = p_ext.shape[0]
    nch = n // T
    idx0 = (_idx_rev if rev else _idx_fwd)(nctx, nch)
    idx = lambda j: idx0(nch - 1 - j)
    scale = DH ** -0.5

    def body(q_ref, k_ref, v_ref, cos_ref, sin_ref, ld_ref, rp_ref, do_ref, dq_ref, dk_ref, dv_ref, dld_ref, dr_s):
        @pl.when(pl.program_id(0) == 0)
        def _():
            dr_s[...] = jnp.zeros_like(dr_s)
            dld_ref[...] = jnp.zeros_like(dld_ref)

        cf, ss = cos_ref[...], sin_ref[...]
        for h in range(RH):
            sl = slice(h * DH, (h + 1) * DH)
            qr = _rot(q_ref[:, sl].astype(f32), cf, ss)
            kr = _rot(k_ref[:, sl].astype(f32), cf, ss) * scale
            _, vjp = jax.vjp(functools.partial(_ret_chunk, rev=rev), qr, kr, v_ref[:, sl].astype(f32), rp_ref[0, h],
                             ld_ref[h:h + 1, 0:1])
            dqr, dkr, dv, drp, dld = vjp((do_ref[:, sl], dr_s[h]))
            dr_s[h] = drp
            dq_ref[:, sl] = _rot_t(dqr, cf, ss)
            dk_ref[:, sl] = _rot_t(dkr, cf, ss) * scale
            dv_ref[:, sl] = dv
            dld_ref[h:h + 1, :] += jnp.broadcast_to(dld, (1, 128))

    def colspec(cb):
        return pl.BlockSpec((T, RW), lambda j, _c=cb: (idx(j), _c))

    tspec = pl.BlockSpec((T, DH), lambda j: (idx(j), 0))
    ospec = pl.BlockSpec((T, RW), lambda j: (idx(j), 0))
    oshape = jax.ShapeDtypeStruct((n, RW), f32)
    return pl.pallas_call(
        body, name=name, grid=(nch,),
        in_specs=[colspec(1), colspec(2), colspec(3), tspec, tspec, _const_spec((8, 128)),
                  pl.BlockSpec((1, RH, DH, DH), lambda j: (nch - 1 - j, 0, 0, 0)), ospec],
        out_specs=[ospec, ospec, ospec, _acc_spec((8, 128))],
        out_shape=[oshape, oshape, oshape, jax.ShapeDtypeStruct((8, 128), f32)],
        scratch_shapes=[pltpu.VMEM((RH, DH, DH), f32)],
        compiler_params=_CP(dimension_semantics=_ARB),
    )(p_ext, p_ext, p_ext, cosf, sins, ld8, rprev, do_ext)


def _f1_fwd(x, ctx, modx, modc, nw1, w_in_n, name):
    L = x.shape[0]
    nb = L // R + 1

    def body(x_ref, c_ref, mx_ref, mc_ref, nw_ref, w_ref, p_ref):
        is_ctx = pl.program_id(0) == 0
        xin = jnp.where(is_ctx, c_ref[...], x_ref[...])
        sh = jnp.where(is_ctx, mc_ref[0:1], mx_ref[0:1])
        sc = jnp.where(is_ctx, mc_ref[1:2], mx_ref[1:2])
        p_ref[...] = dnn(_mod(_rms(xin, nw_ref[...]), sh, sc), w_ref[...]).astype(bf16)

    return pl.pallas_call(
        body, name=name, grid=(nb,),
        in_specs=[pl.BlockSpec((R, D), lambda i: (jnp.maximum(i - 1, 0), 0)), _const_spec((R, D)), _const_spec((6, D)),
                  _const_spec((6, D)), _const_spec((1, D)), _const_spec((D, INC))],
        out_specs=pl.BlockSpec((R, INC), lambda i: (i, 0)),
        out_shape=jax.ShapeDtypeStruct((L + R, INC), bf16),
        compiler_params=_CP(dimension_semantics=("parallel",)),
    )(x, ctx, modx, modc, nw1, w_in_n)


def _f1_bwd(x, ctx, modx, modc, nw1, w_in_t, dx1, parts, name):
    L = x.shape[0]
    nb = L // R + 1

    def body(x_ref, c_ref, mx_ref, mc_ref, nw_ref, w_ref, dx1_ref, du0, du1, du2, dq0, dq1, dk0, dk1, dv0, dv1, dg0,
             gx_ref, dp_ref, h1_ref, dnw_ref, dmx_ref, dmc_ref):
        i = pl.program_id(0)
        is_ctx = i == 0

        @pl.when(is_ctx)
        def _():
            dnw_ref[...] = jnp.zeros_like(dnw_ref)
            dmx_ref[...] = jnp.zeros_like(dmx_ref)
            dmc_ref[...] = jnp.zeros_like(dmc_ref)

        dp = jnp.concatenate([du0[...] + du1[...] + du2[...], dq0[...] + dq1[...], dk0[...] + dk1[...], dv0[...] + dv1[...],
                              dg0[...]], axis=1).astype(bf16)
        dp_ref[...] = dp
        xin = jnp.where(is_ctx, c_ref[...], x_ref[...])
        sh = jnp.where(is_ctx, mc_ref[0:1], mx_ref[0:1])
        sc = jnp.where(is_ctx, mc_ref[1:2], mx_ref[1:2])
        dh = dnn(dp, w_ref[...])
        h, vjp = jax.vjp(lambda a, b, c, d: _mod(_rms(a, b), c, d), xin, nw_ref[...], sh, sc)
        dxin, dnw, dsh, dsc = vjp(dh)
        h1_ref[...] = h.astype(bf16)
        gx_ref[...] = dx1_ref[...] + dxin
        dnw_ref[...] += dnw
        wx = jnp.where(is_ctx, 0.0, 1.0)
        dmx_ref[0:1] += dsh * wx
        dmx_ref[1:2] += dsc * wx
        dmc_ref[0:1] += dsh * (1.0 - wx)
        dmc_ref[1:2] += dsc * (1.0 - wx)

    lat = pl.BlockSpec((R, D), lambda i: (jnp.maximum(i - 1, 0), 0))
    ext = pl.BlockSpec((R, S5W), lambda i: (i, 0))
    return pl.pallas_call(
        body, name=name, grid=(nb,),
        in_specs=[lat, _const_spec((R, D)), _const_spec((6, D)), _const_spec((6, D)), _const_spec((1, D)), _const_spec((INC, D)),
                  lat] + [ext] * 10,
        out_specs=[lat, pl.BlockSpec((R, INC), lambda i: (i, 0)), pl.BlockSpec((R, D), lambda i: (i, 0)),
                   _acc_spec((1, D)), _acc_spec((6, D)), _acc_spec((6, D))],
        out_shape=[jax.ShapeDtypeStruct((L, D), f32), jax.ShapeDtypeStruct((L + R, INC), bf16),
                   jax.ShapeDtypeStruct((L + R, D), bf16), jax.ShapeDtypeStruct((1, D), f32),
                   jax.ShapeDtypeStruct((6, D), f32), jax.ShapeDtypeStruct((6, D), f32)],
        compiler_params=_CP(dimension_semantics=_ARB),
    )(x, ctx, modx, modc, nw1, w_in_t, dx1, *parts)


def _ret_post(yr, g):
    outs = []
    for h in range(RH):
        yh = yr[:, h * DH:(h + 1) * DH]
        mu = jnp.mean(yh, axis=-1, keepdims=True)
        var = jnp.mean((yh - mu) ** 2, axis=-1, keepdims=True)
        outs.append((yh - mu) * lax.rsqrt(var + EPS))
    return jax.nn.silu(g) * jnp.concatenate(outs, axis=1)


def _mix_fn(yf, yb, u, of, ob, g, x, dvec, bglu, gate1, pz, pm, wglu, wout):
    y = yf + yb + dvec * u
    s = jax.nn.gelu(y)
    z = dnn(s, wglu) + bglu + pz
    cat = jnp.concatenate([s * jax.nn.sigmoid(z), _ret_post(of + ob, g)], axis=1)
    mix = dnn(cat, wout) + pm
    return x + gate1 * mix, (s, cat)


def _mix_fwd(x, yf, yb, of, ob, p_ext, dvec, bglu, modx, wglu, wout, name):
    L = x.shape[0]

    def body(x_ref, yf_ref, yb_ref, of_ref, ob_ref, u_ref, g_ref, d_ref, b_ref, mx_ref, wg_ref, wo_ref, x1_ref):
        x1_ref[...] = _mix_fn(yf_ref[...], yb_ref[...], u_ref[...].astype(f32), of_ref[...], ob_ref[...], g_ref[...].astype(f32),
                              x_ref[...], d_ref[...], b_ref[...], mx_ref[2:3], 0.0, 0.0, wg_ref[...], wo_ref[...])[0]

    ext = pl.BlockSpec((R, S5W), lambda i: (i + 1, 0))
    return pl.pallas_call(
        body, name=name, grid=(L // R,),
        in_specs=[pl.BlockSpec((R, D), lambda i: (i, 0)), ext, ext, ext, ext, ext, pl.BlockSpec((R, RW), lambda i: (i + 1, 4)),
                  _const_spec((1, S5W)), _const_spec((1, S5W)), _const_spec((6, D)), _const_spec((S5W, S5W)), _const_spec((D, D))],
        out_specs=pl.BlockSpec((R, D), lambda i: (i, 0)),
        out_shape=jax.ShapeDtypeStruct((L, D), f32),
        compiler_params=_CP(dimension_semantics=("parallel",)),
    )(x, yf, yb, of, ob, p_ext, p_ext, dvec, bglu, modx, wglu, wout)


def _mix_bwd(x, yf, yb, of, ob, p_ext, dvec, bglu, modx, wglu, wout, dx1, name):
    L = x.shape[0]
    nb = L // R + 1

    def body(x_ref, yf_ref, yb_ref, of_ref, ob_ref, u_ref, g_ref, d_ref, b_ref, mx_ref, wg_ref, wo_ref, dx1_ref,
             dy_ref, dud_ref, do_ref, dg_ref, cat_ref, dmix_ref, s_ref, dz_ref, dd_ref, db_ref, dg1_ref):
        i = pl.program_id(0)

        @pl.when(i == 0)
        def _():
            for r in (dy_ref, dud_ref, do_ref, dg_ref, cat_ref, dmix_ref, s_ref, dz_ref, dd_ref, db_ref, dg1_ref):
                r[...] = jnp.zeros_like(r)

        @pl.when(i > 0)
        def _():
            fn = lambda yf_, u_, of_, g_, d_, b_, g1_, pz_, pm_: _mix_fn(
                yf_, yb_ref[...], u_, of_, ob_ref[...], g_, x_ref[...], d_, b_, g1_, pz_, pm_, wg_ref[...], wo_ref[...])
            _, vjp, (s, cat) = jax.vjp(fn, yf_ref[...], u_ref[...].astype(f32), of_ref[...], g_ref[...].astype(f32), d_ref[...],
                                       b_ref[...], mx_ref[2:3], jnp.zeros((R, S5W), f32), jnp.zeros((R, D), f32), has_aux=True)
            dy, dud, do, dg, dd, db, dg1, dz, dmix = vjp(dx1_ref[...])
            dy_ref[...], dud_ref[...], do_ref[...], dg_ref[...] = dy, dud, do, dg
            cat_ref[...], dmix_ref[...] = cat.astype(bf16), dmix.astype(bf16)
            s_ref[...], dz_ref[...] = s.astype(bf16), dz.astype(bf16)
            dd_ref[...] += dd
            db_ref[...] += db
            dg1_ref[...] += dg1

    lat = pl.BlockSpec((R, D), lambda i: (jnp.maximum(i - 1, 0), 0))
    lat5 = pl.BlockSpec((R, S5W), lambda i: (jnp.maximum(i - 1, 0), 0))
    ext = pl.BlockSpec((R, S5W), lambda i: (i, 0))
    eshape = jax.ShapeDtypeStruct((L + R, S5W), f32)
    return pl.pallas_call(
        body, name=name, grid=(nb,),
        in_specs=[lat, ext, ext, ext, ext, ext, pl.BlockSpec((R, RW), lambda i: (i, 4)),
                  _const_spec((1, S5W)), _const_spec((1, S5W)), _const_spec((6, D)), _const_spec((S5W, S5W)), _const_spec((D, D)), lat],
        out_specs=[ext, ext, ext, ext, lat, lat, lat5, lat5, _acc_spec((1, S5W)), _acc_spec((1, S5W)), _acc_spec((1, D))],
        out_shape=[eshape, eshape, eshape, eshape, jax.ShapeDtypeStruct((L, D), bf16), jax.ShapeDtypeStruct((L, D), bf16),
                   jax.ShapeDtypeStruct((L, S5W), bf16), jax.ShapeDtypeStruct((L, S5W), bf16),
                   jax.ShapeDtypeStruct((1, S5W), f32), jax.ShapeDtypeStruct((1, S5W), f32), jax.ShapeDtypeStruct((1, D), f32)],
        compiler_params=_CP(dimension_semantics=_ARB),
    )(x, yf, yb, of, ob, p_ext, p_ext, dvec, bglu, modx, wglu, wout, dx1)


@jax.custom_vjp
def _dnn_const(a, w, wt):
    return dnn(a, w)


_dnn_const.defvjp(lambda a, w, wt: (dnn(a, w), wt), lambda wt, g: (dnn(g, wt), None, None))


def _ffn_tail(gc, a, x1, gate2, fnw, pf, wdown, wdown_t, tgt):
    f = jax.nn.gelu(gc) * a
    ffn = _dnn_const(f, wdown, wdown_t) + pf
    y = _rms(x1 + gate2 * ffn, fnw)
    err = y - tgt
    loss = 0.5 * jnp.sum(jnp.mean(err * err, axis=-1, keepdims=True), axis=0, keepdims=True)
    return loss, f


def _ffn_fwd(x1, tgt, nw2, modx, w_a, w_g, cw, cb, wdown, wdown_t, fnw, name):
    L = x1.shape[0]
    nb = L // RF
    per = RF // HALO

    def body(x_ref, xp_ref, xn_ref, t_ref, nw_ref, mx_ref, wa_ref, wg_ref, cw_ref, cb_ref, wd_ref, wdt_ref, fn_ref,
             dx2_ref, da_ref, dgc_ref, f_ref, dffn_ref, loss_ref, dfn_ref, dg2_ref, dcb_ref, dcw_ref):
        i = pl.program_id(0)

        @pl.when(i == 0)
        def _():
            for r in (loss_ref, dfn_ref, dg2_ref, dcb_ref, dcw_ref):
                r[...] = jnp.zeros_like(r)

        nw, sh, sc, gate2 = nw_ref[...], mx_ref[3:4], mx_ref[4:5], mx_ref[5:6]
        x1b = x_ref[...]
        h2 = _mod(_rms(x1b, nw), sh, sc)
        h2e = jnp.concatenate([_mod(_rms(xp_ref[...], nw), sh, sc), h2, _mod(_rms(xn_ref[...], nw), sh, sc)], axis=0)
        a = dnn(h2, wa_ref[...])
        ge = dnn(h2e, wg_ref[...])
        g = ge[HALO:HALO + RF]
        gp = ge[HALO - 1:HALO] * jnp.where(i > 0, 1.0, 0.0)
        gn = ge[HALO + RF:HALO + RF + 1] * jnp.where(i < nb - 1, 1.0, 0.0)
        row = lax.broadcasted_iota(jnp.int32, (RF, 1), 0)
        g_prev = jnp.where(row == 0, gp, pltpu.roll(g, 1, axis=0))
        g_next = jnp.where(row == RF - 1, gn, pltpu.roll(g, RF - 1, axis=0))
        gc = cb_ref[...] + g_prev * cw_ref[0:1] + g * cw_ref[1:2] + g_next * cw_ref[2:3]
        fn = lambda gc_, a_, x_, g2_, fw_, pf_: _ffn_tail(gc_, a_, x_, g2_, fw_, pf_, wd_ref[...], wdt_ref[...], t_ref[...])
        loss, vjp, f = jax.vjp(fn, gc, a, x1b, gate2, fn_ref[...], jnp.zeros((RF, D), f32), has_aux=True)
        dgc, da, dx2, dg2, dfw, dffn = vjp(jnp.ones((1, 1), f32))
        dx2_ref[...] = dx2
        da_ref[...], dgc_ref[...] = da.astype(bf16), dgc
        f_ref[...], dffn_ref[...] = f.astype(bf16), dffn.astype(bf16)
        loss_ref[...] += jnp.broadcast_to(loss, (1, 128))
        dfn_ref[...] += dfw
        dg2_ref[...] += dg2
        dcb_ref[...] += jnp.sum(dgc, axis=0, keepdims=True)
        dcw_ref[0:1] += jnp.sum(dgc * g_prev, axis=0, keepdims=True)
        dcw_ref[1:2] += jnp.sum(dgc * g, axis=0, keepdims=True)
        dcw_ref[2:3] += jnp.sum(dgc * g_next, axis=0, keepdims=True)

    blk = lambda w: pl.BlockSpec((RF, w), lambda i: (i, 0))
    return pl.pallas_call(
        body, name=name, grid=(nb,),
        in_specs=[blk(D), pl.BlockSpec((HALO, D), lambda i: (jnp.maximum(i * per - 1, 0), 0)),
                  pl.BlockSpec((HALO, D), lambda i: (jnp.minimum((i + 1) * per, L // HALO - 1), 0)), blk(D),
                  _const_spec((1, D)), _const_spec((6, D)), _const_spec((D, DFF)), _const_spec((D, DFF)), _const_spec((3, DFF)),
                  _const_spec((1, DFF)), _const_spec((DFF, D)), _const_spec((D, DFF)), _const_spec((1, D))],
        out_specs=[blk(D), blk(DFF), blk(DFF), blk(DFF), blk(D), _acc_spec((1, 128)), _acc_spec((1, D)), _acc_spec((1, D)),
                   _acc_spec((1, DFF)), _acc_spec((3, DFF))],
        out_shape=[jax.ShapeDtypeStruct((L, D), f32), jax.ShapeDtypeStruct((L, DFF), bf16), jax.ShapeDtypeStruct((L, DFF), f32),
                   jax.ShapeDtypeStruct((L, DFF), bf16), jax.ShapeDtypeStruct((L, D), bf16), jax.ShapeDtypeStruct((1, 128), f32),
                   jax.ShapeDtypeStruct((1, D), f32), jax.ShapeDtypeStruct((1, D), f32), jax.ShapeDtypeStruct((1, DFF), f32),
                   jax.ShapeDtypeStruct((3, DFF), f32)],
        compiler_params=_CP(dimension_semantics=_ARB),
    )(x1, x1, x1, tgt, nw2, modx, w_a, w_g, cw, cb, wdown, wdown_t, fnw)


def _ffn_bwd(x1, dx2, da, dgc, nw2, modx, wup_t, cw, name):
    L = x1.shape[0]
    nb = L // RF
    per = RF // HALO

    def body(x_ref, dx2_ref, da_ref, dgc_ref, dgp_ref, dgn_ref, nw_ref, mx_ref, wu_ref, cw_ref,
             dx1_ref, dag_ref, h2_ref, dnw_ref, dmx_ref):
        i = pl.program_id(0)

        @pl.when(i == 0)
        def _():
            dnw_ref[...] = jnp.zeros_like(dnw_ref)
            dmx_ref[...] = jnp.zeros_like(dmx_ref)

        dgc_b = dgc_ref[...]
        before = dgp_ref[HALO - 1:HALO] * jnp.where(i > 0, 1.0, 0.0)
        after = dgn_ref[0:1] * jnp.where(i < nb - 1, 1.0, 0.0)
        row = lax.broadcasted_iota(jnp.int32, (RF, 1), 0)
        d_prev = jnp.where(row == 0, before, pltpu.roll(dgc_b, 1, axis=0))
        d_next = jnp.where(row == RF - 1, after, pltpu.roll(dgc_b, RF - 1, axis=0))
        dg = cw_ref[0:1] * d_next + cw_ref[1:2] * dgc_b + cw_ref[2:3] * d_prev
        dag = jnp.concatenate([da_ref[...], dg.astype(bf16)], axis=1)
        dag_ref[...] = dag
        dh2 = dnn(dag, wu_ref[...])
        h2, vjp = jax.vjp(lambda a, b, c, d: _mod(_rms(a, b), c, d), x_ref[...], nw_ref[...], mx_ref[3:4], mx_ref[4:5])
        dxa, dnw, dsh, dsc = vjp(dh2)
        h2_ref[...] = h2.astype(bf16)
        dx1_ref[...] = dx2_ref[...] + dxa
        dnw_ref[...] += dnw
        dmx_ref[3:4] += dsh
        dmx_ref[4:5] += dsc

    blk = lambda w: pl.BlockSpec((RF, w), lambda i: (i, 0))
    return pl.pallas_call(
        body, name=name, grid=(nb,),
        in_specs=[blk(D), blk(D), blk(DFF), blk(DFF), pl.BlockSpec((HALO, DFF), lambda i: (jnp.maximum(i * per - 1, 0), 0)),
                  pl.BlockSpec((HALO, DFF), lambda i: (jnp.minimum((i + 1) * per, L // HALO - 1), 0)),
                  _const_spec((1, D)), _const_spec((6, D)), _const_spec((2 * DFF, D)), _const_spec((3, DFF))],
        out_specs=[blk(D), blk(2 * DFF), blk(D), _acc_spec((1, D)), _acc_spec((6, D))],
        out_shape=[jax.ShapeDtypeStruct((L, D), f32), jax.ShapeDtypeStruct((L, 2 * DFF), bf16), jax.ShapeDtypeStruct((L, D), bf16),
                   jax.ShapeDtypeStruct((1, D), f32), jax.ShapeDtypeStruct((6, D), f32)],
        compiler_params=_CP(dimension_semantics=_ARB),
    )(x1, dx2, da, dgc, dgc, dgc, nw2, modx, wup_t, cw)


def _matmul_tn(a, b, name):
    k, m = a.shape
    n = b.shape[1]
    divs = lambda d: [c for c in range(d, 0, -128) if d % c == 0]
    _, tm, tn = min((m * (n // cn) + n * (m // cm), cm, cn) for cm in divs(m) for cn in divs(n) if cm * cn * 4 <= ACC_TILE_BYTES)
    tk = next(c for c in (512, 768, 256, 128) if k % c == 0)
    nk = k // tk

    def body(a_ref, b_ref, o_ref, acc):
        q = pl.program_id(2)

        @pl.when(q == 0)
        def _():
            acc[...] = jnp.zeros_like(acc)

        acc[...] += dtn(a_ref[...], b_ref[...])

        @pl.when(q == nk - 1)
        def _():
            o_ref[...] = acc[...].astype(bf16)

    return pl.pallas_call(
        body, name=name, grid=(m // tm, n // tn, nk),
        in_specs=[pl.BlockSpec((tk, tm), lambda i, j, q: (q, i)), pl.BlockSpec((tk, tn), lambda i, j, q: (q, j))],
        out_specs=pl.BlockSpec((tm, tn), lambda i, j, q: (i, j)),
        out_shape=jax.ShapeDtypeStruct((m, n), bf16),
        scratch_shapes=[pltpu.VMEM((tm, tn), f32)],
        compiler_params=_CP(dimension_semantics=("parallel", "parallel", "arbitrary")),
    )(a, b)


def _adamw(w, g, m, v, name):
    c1, c2 = 1.0 - B1 ** STEP, 1.0 - B2 ** STEP

    def body(w_ref, g_ref, m_ref, v_ref, d_ref, nm_ref, nv_ref):
        gg = g_ref[...]
        nm = B1 * m_ref[...] + (1.0 - B1) * gg
        nv = B2 * v_ref[...] + (1.0 - B2) * jnp.square(gg)
        d_ref[...] = -LR * ((nm / c1) / (jnp.sqrt(nv / c2) + AEPS) + WD * w_ref[...])
        nm_ref[...], nv_ref[...] = nm, nv

    return pl.pallas_call(body, name=name, out_shape=[jax.ShapeDtypeStruct(w.shape, f32)] * 3, compiler_params=_CP())(w, g, m, v)


SMALL = ["conv_w", "c_ctx", "norm1_w", "s5_lambda_re_f", "s5_lambda_im_f", "s5_log_step_f", "s5_lambda_re_b", "s5_lambda_im_b",
         "s5_log_step_b", "s5_b_re", "s5_b_im", "s5_c_re", "s5_c_im", "s5_d", "s5_b_glu", "ret_log_decay_f", "ret_log_decay_b",
         "norm2_w", "conv_b", "final_norm_w"]
WEIGHTS = ["c_ctx", "w_mod", "b_mod", "norm1_w", "w_in", "s5_lambda_re_f", "s5_lambda_im_f", "s5_log_step_f", "s5_lambda_re_b",
           "s5_lambda_im_b", "s5_log_step_b", "s5_b_re", "s5_b_im", "s5_c_re", "s5_c_im", "s5_d", "s5_w_glu", "s5_b_glu",
           "ret_log_decay_f", "ret_log_decay_b", "w_out", "norm2_w", "w_up", "conv_w", "conv_b", "w_down", "final_norm_w"]


def _pack_small(vals):
    flat, offs, o = [], [], 0
    for a in vals:
        n = a.size
        npad = -n % 128
        flat.append(a.reshape(-1))
        if npad:
            flat.append(jnp.zeros((npad,), f32))
        offs.append((o, n))
        o += n + npad
    tail = -o % 1024
    if tail:
        flat.append(jnp.zeros((tail,), f32))
    return jnp.concatenate(flat).reshape(-1, 128), offs


def _unpack_small(packed, offs, shapes):
    flat = packed.reshape(-1)
    return [flat[o:o + n].reshape(s) for (o, n), s in zip(offs, shapes)]


def _rope_tables(L, nctx_rows):
    t = np.arange(L)
    inv = (ROPE_THETA ** (-np.arange(DH // 4, dtype=np.float64) / (DH // 4))).astype(np.float32)
    ang = np.concatenate([(t // GRID_W).astype(np.float32)[:, None] * inv, (t % GRID_W).astype(np.float32)[:, None] * inv], axis=-1)
    cos = np.repeat(np.cos(ang).astype(np.float32), 2, axis=1)
    sin = np.repeat(np.sin(ang).astype(np.float32), 2, axis=1) * np.tile(np.array([-1.0, 1.0], np.float32), DH // 2)
    cosf = np.concatenate([np.ones((nctx_rows, DH), np.float32), cos], axis=0)
    sins = np.concatenate([np.zeros((nctx_rows, DH), np.float32), sin], axis=0)
    return jnp.asarray(cosf), jnp.asarray(sins)


def kernel(x, c, ctx, c_ctx, w_mod, b_mod, norm1_w, w_in, s5_lambda_re_f, s5_lambda_im_f, s5_log_step_f, s5_lambda_re_b, s5_lambda_im_b, s5_log_step_b, s5_b_re, s5_b_im, s5_c_re, s5_c_im, s5_d, s5_w_glu, s5_b_glu, ret_log_decay_f, ret_log_decay_b, w_out, norm2_w, w_up, conv_w, conv_b, w_down, final_norm_w, loss_target, m_c_ctx, m_w_mod, m_b_mod, m_norm1_w, m_w_in, m_s5_lambda_re_f, m_s5_lambda_im_f, m_s5_log_step_f, m_s5_lambda_re_b, m_s5_lambda_im_b, m_s5_log_step_b, m_s5_b_re, m_s5_b_im, m_s5_c_re, m_s5_c_im, m_s5_d, m_s5_w_glu, m_s5_b_glu, m_ret_log_decay_f, m_ret_log_decay_b, m_w_out, m_norm2_w, m_w_up, m_conv_w, m_conv_b, m_w_down, m_final_norm_w, v_c_ctx, v_w_mod, v_b_mod, v_norm1_w, v_w_in, v_s5_lambda_re_f, v_s5_lambda_im_f, v_s5_log_step_f, v_s5_lambda_re_b, v_s5_lambda_im_b, v_s5_log_step_b, v_s5_b_re, v_s5_b_im, v_s5_c_re, v_s5_c_im, v_s5_d, v_s5_w_glu, v_s5_b_glu, v_ret_log_decay_f, v_ret_log_decay_b, v_w_out, v_norm2_w, v_w_up, v_conv_w, v_conv_b, v_w_down, v_final_norm_w):
    args = dict(locals())
    W = {n: args[n] for n in WEIGHTS}
    M = {n: args["m_" + n] for n in WEIGHTS}
    V = {n: args["v_" + n] for n in WEIGHTS}
    me = _me()
    x2, ctx2, tgt = x[0], ctx[0], loss_target[0]
    L, Lc = x2.shape[0], ctx2.shape[0]
    assert Lc == R and L % R == 0 and L % GRID_W == 0
    nctx = Lc // T

    c_all = _all_gather_small(jnp.pad(c, ((0, 7), (0, 0))), "gather_c")[:, 0, :]
    c9 = jnp.concatenate([c_all, c_ctx[None], jnp.zeros((7, D), f32)], axis=0)
    w_mod_l = w_mod[0]
    ncol = w_mod_l.shape[1]
    m_part = _ada_fwd(c9, w_mod_l, "ada_fwd")
    m_all = _all_gather_small(m_part, "gather_mod").transpose(1, 0, 2).reshape(16, 6, D)
    modx, modc = _mod_select(m_all, b_mod.reshape(6, D), "mod_select")

    w_in_tl, w_up_tl = w_in[0].T.astype(bf16), w_up[0].T.astype(bf16)
    w_out_l, w_down_l, w_glu_l = w_out[0].astype(bf16), w_down[0].astype(bf16), s5_w_glu[0].astype(bf16)
    (w_in_g,) = _exchange([w_in_tl], False, "gather_w_in")
    w_in_t = w_in_g.reshape(INC, D)
    per_cv = conv_w.shape[2]
    conv_pad = jnp.pad(conv_w[0], ((0, 5), (0, 128 * 3 - per_cv)))
    conv_f = _all_gather_small(conv_pad, "gather_conv")[:, :3, :per_cv].transpose(1, 0, 2).reshape(3, DFF)

    gn = lambda a: a[0]
    bre_t, bim_t = gn(s5_b_re).transpose(2, 0, 1), gn(s5_b_im).transpose(2, 0, 1)
    cre_m, cim_m = _blockdiag(gn(s5_c_re)), _blockdiag(gn(s5_c_im))
    tril = jnp.tril(jnp.ones((T, T), bf16))
    triu = tril.T
    s5 = {}
    for tag, lre, lim, ls in (("f", s5_lambda_re_f, s5_lambda_im_f, s5_log_step_f), ("b", s5_lambda_re_b, s5_lambda_im_b, s5_log_step_b)):
        lam, bbr, bbi, tab = _s5_prep(gn(lre), gn(lim), gn(ls).reshape(S5G, 1), bre_t, bim_t, "s5_prep_" + tag)
        s5[tag] = dict(lam=lam.reshape(2, GN), tab=tab.reshape(8, T, GN),
                       mats=(_blockdiag(bbr.transpose(1, 0, 2)), _blockdiag(bbi.transpose(1, 0, 2)), cre_m, cim_m))

    nw1, nw2, fnw = norm1_w, norm2_w, final_norm_w[None]
    p_ext = _f1_fwd(x2, ctx2, modx, modc, nw1, w_in_t.T, "f1_fwd")
    yf, hb_f, w_out_g, w_down_g, w_glu_g = _s5_fwd(p_ext, s5["f"]["mats"], s5["f"]["tab"], s5["f"]["lam"], tril, False, nctx,
                                                   "s5_fwd_f", cargo=([w_out_l, w_down_l, w_glu_l], False))
    yb, hb_b, w_up_g = _s5_fwd(p_ext, s5["b"]["mats"], s5["b"]["tab"], s5["b"]["lam"], triu, True, nctx, "s5_fwd_b",
                               cargo=([w_up_tl], False))
    w_out_f, w_down_f, w_glu_f = w_out_g.reshape(D, D), w_down_g.reshape(DFF, D), w_glu_g.reshape(S5W, S5W)
    w_up_t = w_up_g.reshape(2 * DFF, D)
    cosf, sins = _rope_tables(L, Lc)
    ld8 = lambda ld: jnp.pad(jnp.broadcast_to(ld[0][:, None], (RH, 128)), ((0, 8 - RH), (0, 0)))
    ldf8, ldb8 = ld8(ret_log_decay_f), ld8(ret_log_decay_b)
    of, rp_f = _ret_fwd(p_ext, cosf, sins, ldf8, False, nctx, "ret_fwd_f")
    ob, rp_b = _ret_fwd(p_ext, cosf, sins, ldb8, True, nctx, "ret_fwd_b")
    x1 = _mix_fwd(x2, yf, yb, of, ob, p_ext, s5_d, s5_b_glu, modx, w_glu_f, w_out_f, "mix_fwd")

    (dx2, da, dgc, f_act, dffn, loss_acc, g_fnw, g_gate2, g_cb, g_cw) = _ffn_fwd(
        x1, tgt, nw2, modx, w_up_t[:DFF].T, w_up_t[DFF:].T, conv_f, conv_b, w_down_f, w_down_f.T, fnw, "ffn_fwd")
    dx1, dag, h2, g_nw2, dmx2 = _ffn_bwd(x1, dx2, da, dgc, nw2, modx, w_up_t, conv_f, "ffn_bwd")
    gw_down = _matmul_tn(f_act, dffn, "dw_down").reshape(NDEV, -1, D)
    gw_up_t = _matmul_tn(dag, h2, "dw_up").reshape(NDEV, -1, D)
    (dy_e, dud_e, do_e, dg_e, cat, dmix, s_act, dz, g_d, g_bglu, g_gate1) = _mix_bwd(
        x2, yf, yb, of, ob, p_ext, s5_d, s5_b_glu, modx, w_glu_f, w_out_f, dx1, "mix_bwd")
    gw_out = _matmul_tn(cat, dmix, "dw_out").reshape(NDEV, -1, D)
    gw_glu = _matmul_tn(s_act, dz, "dw_glu").reshape(NDEV, -1, S5W)
    dq_f, dk_f, dv_f, gld_f = _ret_bwd(p_ext, cosf, sins, ldf8, rp_f, do_e, False, nctx, "ret_bwd_f")
    dq_b, dk_b, dv_b, gld_b = _ret_bwd(p_ext, cosf, sins, ldb8, rp_b, do_e, True, nctx, "ret_bwd_b")
    du_f, dbre_f, dbim_f, dcre_f, dcim_f, dl_f, l_up, l_down = _s5_bwd(
        p_ext, dy_e, hb_f, s5["f"]["mats"], s5["f"]["tab"], s5["f"]["lam"], tril, triu, False, nctx, "s5_bwd_f",
        cargo=([gw_up_t, gw_down], True))
    du_b, dbre_b, dbim_b, dcre_b, dcim_b, dl_b, l_out, l_glu = _s5_bwd(
        p_ext, dy_e, hb_b, s5["b"]["mats"], s5["b"]["tab"], s5["b"]["lam"], triu, tril, True, nctx, "s5_bwd_b",
        cargo=([gw_out, gw_glu], True))
    grad_x, dp_ext, h1, g_nw1, dmx1, dmc1 = _f1_bwd(
        x2, ctx2, modx, modc, nw1, w_in_t, dx1, (du_f, du_b, dud_e, dq_f, dq_b, dk_f, dk_b, dv_f, dv_b, dg_e), "f1_bwd")
    gw_in_t = _matmul_tn(dp_ext, h1, "dw_in").reshape(NDEV, -1, D)
    (l_in,) = _exchange([gw_in_t], True, "scatter_dw_in")

    tg = lambda m: _blockdiag_extract(m).transpose(1, 0, 2)
    gs5 = {}
    for tag, lre, lim, ls, dl, dbr, dbi in (("f", s5_lambda_re_f, s5_lambda_im_f, s5_log_step_f, dl_f, dbre_f, dbim_f),
                                             ("b", s5_lambda_re_b, s5_lambda_im_b, s5_log_step_b, dl_b, dbre_b, dbim_b)):
        gs5[tag] = _s5_prep_bwd(gn(lre), gn(lim), gn(ls).reshape(S5G, 1), bre_t, bim_t, dl.reshape(2, S5G, S5N), tg(dbr), tg(dbi),
                                "s5_prep_bwd_" + tag)
    g_bre = (gs5["f"][3] + gs5["b"][3]).transpose(1, 2, 0)
    g_bim = (gs5["f"][4] + gs5["b"][4]).transpose(1, 2, 0)
    g_cre = _blockdiag_extract(dcre_f + dcre_b)
    g_cim = _blockdiag_extract(dcim_f + dcim_b)

    dmx = dmx1 + dmx2
    dmx = dmx.at[2].set(g_gate1[0]).at[5].set(g_gate2[0])
    dm_me = jnp.stack([dmx.reshape(-1), dmc1.reshape(-1)], axis=0)
    dm_all = _all_gather_small(jnp.pad(dm_me, ((0, 6), (0, 0))), "gather_dmod")
    dmx_all, dmc_all = dm_all[:, 0, :], dm_all[:, 1, :]
    my_cols = lambda a: lax.dynamic_slice(a, (0, me * ncol), (NDEV, ncol))
    gw_mod, g_bmod, dc9 = _ada_bwd(c9, dmx_all, dmc_all, my_cols(dmx_all), my_cols(dmc_all), w_mod_l, "ada_bwd")

    small = {
        "conv_w": g_cw, "c_ctx": dc9[8], "norm1_w": g_nw1, "s5_lambda_re_f": gs5["f"][0], "s5_lambda_im_f": gs5["f"][1],
        "s5_log_step_f": gs5["f"][2], "s5_lambda_re_b": gs5["b"][0], "s5_lambda_im_b": gs5["b"][1], "s5_log_step_b": gs5["b"][2],
        "s5_b_re": g_bre, "s5_b_im": g_bim, "s5_c_re": g_cre, "s5_c_im": g_cim, "s5_d": g_d, "s5_b_glu": g_bglu,
        "ret_log_decay_f": gld_f[:RH, 0], "ret_log_decay_b": gld_b[:RH, 0], "norm2_w": g_nw2, "conv_b": g_cb, "final_norm_w": g_fnw,
    }
    packed, soffs = _pack_small([small[n].astype(f32) for n in SMALL])
    red = _all_reduce_small(packed, "reduce_small")
    sshapes = [(3, DFF) if n == "conv_w" else W[n].shape for n in SMALL]
    G = dict(zip(SMALL, _unpack_small(red, soffs, sshapes)))
    G["conv_w"] = lax.dynamic_slice(G["conv_w"], (0, me * per_cv), (3, per_cv))[None]
    G["b_mod"] = g_bmod.reshape(b_mod.shape)
    G["w_mod"] = gw_mod[None]
    G["w_in"] = _sum8(l_in, "sum_dw_in").T[None]
    G["w_up"] = _sum8(l_up, "sum_dw_up").T[None]
    G["w_out"] = _sum8(l_out, "sum_dw_out")[None]
    G["w_down"] = _sum8(l_down, "sum_dw_down")[None]
    G["s5_w_glu"] = _sum8(l_glu, "sum_dw_glu")[None]

    delta, new_m, new_v = {}, {}, {}
    sm_names = SMALL[1:] + ["b_mod"]
    pk = lambda d: _pack_small([d[n].astype(f32) for n in sm_names])
    (pw, aoffs), (pg, _), (pm, _), (pv, _) = pk(W), pk(G), pk(M), pk(V)
    pd, pnm, pnv = _adamw(pw, pg, pm, pv, "adamw_small")
    shapes = [W[n].shape for n in sm_names]
    for dst, src in ((delta, pd), (new_m, pnm), (new_v, pnv)):
        dst.update(zip(sm_names, _unpack_small(src, aoffs, shapes)))
    for n in ["w_mod", "w_in", "w_out", "w_up", "w_down", "s5_w_glu", "conv_w"]:
        d, nm, nv = _adamw(W[n][0], G[n][0], M[n][0], V[n][0], "adamw_" + n)
        delta[n], new_m[n], new_v[n] = d[None], nm[None], nv[None]

    loss = lax.psum(loss_acc[0, 0], ("x", "y", "c"))
    return (loss, grad_x[None], *[G[n] for n in WEIGHTS], *[delta[n] for n in WEIGHTS], *[new_m[n] for n in WEIGHTS],
            *[new_v[n] for n in WEIGHTS])
```

```python
import functools

import numpy as np
import jax
import jax.numpy as jnp
from jax import lax
from jax.experimental import pallas as pl
from jax.experimental.pallas import tpu as pltpu

f32, bf16 = jnp.float32, jnp.bfloat16

D = 1024
S5W, S5G, S5P, S5N = 512, 32, 16, 64
TC = 16
TCP = TC * S5P
SB = 2 * S5N
GBK = 8
RH, DH = 4, 128
RW = RH * DH
INC = S5W + 4 * RW
DFF = 2816
T = 128
R = 256
RF = 128
HALO = 8
EPS = 1e-6
ROPE_THETA = 10000.0
GRID_W = 64
NDEV = 8
LR, B1, B2, AEPS, WD, STEP = 0.001, 0.9, 0.999, 1e-08, 0.01, 10
VMEM_LIMIT = 60 * 1024 * 1024
ACC_TILE_BYTES = 6 * 1024 * 1024
MESH = pl.DeviceIdType.MESH

_CP = functools.partial(pltpu.CompilerParams, vmem_limit_bytes=VMEM_LIMIT)
_ARB = ("arbitrary",)
_ANY = pl.BlockSpec(memory_space=pl.ANY)


def _dg(a, b, dims):
    return lax.dot_general(a.astype(bf16), b.astype(bf16), (dims, ((), ())), preferred_element_type=f32)


@jax.custom_vjp
def dnn(a, b):
    return _dg(a, b, ((1,), (0,)))


@jax.custom_vjp
def dnt(a, b):
    return _dg(a, b, ((1,), (1,)))


@jax.custom_vjp
def dtn(a, b):
    return _dg(a, b, ((0,), (0,)))


dnn.defvjp(lambda a, b: (dnn(a, b), (a, b)), lambda r, g: (dnt(g, r[1]).astype(r[0].dtype), dtn(r[0], g).astype(r[1].dtype)))
dnt.defvjp(lambda a, b: (dnt(a, b), (a, b)), lambda r, g: (dnn(g, r[1]).astype(r[0].dtype), dtn(g, r[0]).astype(r[1].dtype)))
dtn.defvjp(lambda a, b: (dtn(a, b), (a, b)), lambda r, g: (dnt(r[1], g).astype(r[0].dtype), dnn(r[0], g).astype(r[1].dtype)))


@jax.custom_vjp
def _dnn_const(a, w, wt):
    return dnn(a, w)


_dnn_const.defvjp(lambda a, w, wt: (dnn(a, w), wt), lambda wt, g: (dnn(g, wt), None, None))


def _rms(t, w):
    return t * lax.rsqrt(jnp.mean(t * t, axis=-1, keepdims=True) + EPS) * w


def _mod(h, shift, scale):
    return h * (1.0 + scale) + shift


def _const_spec(shape):
    n = len(shape)
    return pl.BlockSpec(shape, lambda i, _n=n: (0,) * _n, pipeline_mode=pl.Buffered(1))


def _acc_spec(shape):
    n = len(shape)
    return pl.BlockSpec(shape, lambda i, _n=n: (0,) * _n)


def _me():
    return 4 * lax.axis_index("x") + 2 * lax.axis_index("y") + lax.axis_index("c")


def _peer(r):
    x, y, c = lax.axis_index("x"), lax.axis_index("y"), lax.axis_index("c")
    px = 1 - x if (r >> 2) & 1 else x
    py = 1 - y if (r >> 1) & 1 else y
    pc = 1 - c if r & 1 else c
    return (px, py, pc), 4 * px + 2 * py + pc


def _all_gather_small(v, name):
    r, c = v.shape

    def body(v_ref, out_ref, send_sems, recv_sems):
        me = _me()
        out_ref[me] = v_ref[...]
        sends = []
        for k in range(1, NDEV):
            peer, _ = _peer(k)
            cp = pltpu.make_async_remote_copy(src_ref=v_ref, dst_ref=out_ref.at[me], send_sem=send_sems.at[k - 1],
                                              recv_sem=recv_sems.at[k - 1], device_id=peer, device_id_type=MESH)
            cp.start()
            sends.append(cp)
        for k in range(1, NDEV):
            peer, pidx = _peer(k)
            pltpu.make_async_remote_copy(src_ref=v_ref, dst_ref=out_ref.at[pidx], send_sem=send_sems.at[k - 1],
                                         recv_sem=recv_sems.at[k - 1], device_id=peer, device_id_type=MESH).wait_recv()
        for cp in sends:
            cp.wait_send()

    return pl.pallas_call(
        body, name=name, out_shape=jax.ShapeDtypeStruct((NDEV, r, c), v.dtype),
        in_specs=[pl.BlockSpec(memory_space=pltpu.VMEM)], out_specs=pl.BlockSpec(memory_space=pltpu.VMEM),
        scratch_shapes=[pltpu.SemaphoreType.DMA((NDEV - 1,)), pltpu.SemaphoreType.DMA((NDEV - 1,))],
        compiler_params=_CP(),
    )(v)


def _all_reduce_small(v, name):
    r, c = v.shape

    def body(v_ref, out_ref, land, send_sems, recv_sems):
        me = _me()
        land[me] = v_ref[...]
        sends = []
        for k in range(1, NDEV):
            peer, _ = _peer(k)
            cp = pltpu.make_async_remote_copy(src_ref=v_ref, dst_ref=land.at[me], send_sem=send_sems.at[k - 1],
                                              recv_sem=recv_sems.at[k - 1], device_id=peer, device_id_type=MESH)
            cp.start()
            sends.append(cp)
        for k in range(1, NDEV):
            peer, pidx = _peer(k)
            pltpu.make_async_remote_copy(src_ref=v_ref, dst_ref=land.at[pidx], send_sem=send_sems.at[k - 1],
                                         recv_sem=recv_sems.at[k - 1], device_id=peer, device_id_type=MESH).wait_recv()
        for cp in sends:
            cp.wait_send()
        acc = land[0]
        for j in range(1, NDEV):
            acc = acc + land[j]
        out_ref[...] = acc

    return pl.pallas_call(
        body, name=name, out_shape=jax.ShapeDtypeStruct((r, c), v.dtype),
        in_specs=[pl.BlockSpec(memory_space=pltpu.VMEM)], out_specs=pl.BlockSpec(memory_space=pltpu.VMEM),
        scratch_shapes=[pltpu.VMEM((NDEV, r, c), v.dtype), pltpu.SemaphoreType.DMA((NDEV - 1,)),
                        pltpu.SemaphoreType.DMA((NDEV - 1,))],
        compiler_params=_CP(),
    )(v)


class _Exchange:
    def __init__(self, srcs, dsts, send_sems, recv_sems, local_sems, scatter):
        me = _me()
        n = len(srcs)
        self.sends, self.recvs, self.locals = [], [], []
        for a, (s, d) in enumerate(zip(srcs, dsts)):
            self.locals.append(pltpu.make_async_copy(s.at[me] if scatter else s, d.at[me], local_sems.at[a]))
        for k in range(1, NDEV):
            peer, pidx = _peer(k)
            for a, (s, d) in enumerate(zip(srcs, dsts)):
                src = s.at[pidx] if scatter else s
                sem = (k - 1) * n + a
                for dst, out in ((d.at[me], self.sends), (d.at[pidx], self.recvs)):
                    out.append(pltpu.make_async_remote_copy(src_ref=src, dst_ref=dst, send_sem=send_sems.at[sem],
                                                            recv_sem=recv_sems.at[sem], device_id=peer, device_id_type=MESH))

    def start(self):
        for cp in self.locals + self.sends:
            cp.start()

    def wait(self):
        for cp in self.recvs:
            cp.wait_recv()
        for cp in self.sends:
            cp.wait_send()
        for cp in self.locals:
            cp.wait()


def _exchange_shapes(arrays, scatter):
    return [jax.ShapeDtypeStruct(a.shape if scatter else (NDEV,) + a.shape, a.dtype) for a in arrays]


def _exchange_sems(n):
    return [pltpu.SemaphoreType.DMA(((NDEV - 1) * n,)), pltpu.SemaphoreType.DMA(((NDEV - 1) * n,)), pltpu.SemaphoreType.DMA((n,))]


def _exchange(arrays, scatter, name):
    n = len(arrays)

    def body(*refs):
        ex = _Exchange(refs[:n], refs[n:2 * n], *refs[2 * n:], scatter)
        ex.start()
        ex.wait()

    return pl.pallas_call(body, name=name, out_shape=_exchange_shapes(arrays, scatter), in_specs=[_ANY] * n,
                          out_specs=[_ANY] * n, scratch_shapes=_exchange_sems(n), compiler_params=_CP())(*arrays)


class _Cargo:
    def __init__(self, cargo):
        self.arrays, self.scatter = cargo if cargo else ([], False)
        self.n = len(self.arrays)

    def in_specs(self):
        return [_ANY] * self.n

    def out_shapes(self):
        return _exchange_shapes(self.arrays, self.scatter)

    def sems(self):
        return _exchange_sems(self.n) if self.n else []

    def split(self, refs, n_in, n_out, n_scratch):
        n = self.n
        return refs[:n_in], refs[n_in + n:n_in + n + n_out], refs[n_in + 2 * n + n_out:n_in + 2 * n + n_out + n_scratch]

    def ride(self, refs, n_in, n_out, nsteps):
        if not self.n:
            return
        n = self.n
        ex = _Exchange(refs[n_in:n_in + n], refs[n_in + n + n_out:n_in + 2 * n + n_out], *refs[-3:], self.scatter)

        @pl.when(pl.program_id(0) == 0)
        def _():
            ex.start()

        @pl.when(pl.program_id(0) == nsteps - 1)
        def _():
            ex.wait()


def _sum8(land, name):
    _, r, c = land.shape
    rb = next((b for b in (256, 64, 32) if r % b == 0), r)

    def body(l_ref, o_ref):
        acc = l_ref[0].astype(f32)
        for j in range(1, NDEV):
            acc = acc + l_ref[j].astype(f32)
        o_ref[...] = acc

    return pl.pallas_call(
        body, name=name, grid=(r // rb,), out_shape=jax.ShapeDtypeStruct((r, c), f32),
        in_specs=[pl.BlockSpec((NDEV, rb, c), lambda i: (0, i, 0))], out_specs=pl.BlockSpec((rb, c), lambda i: (i, 0)),
        compiler_params=_CP(dimension_semantics=("parallel",)),
    )(land)


def _ada_fwd(c9, w_mod_l, name):
    def body(c_ref, w_ref, o_ref):
        o_ref[...] = dnn(jax.nn.silu(c_ref[...]), w_ref[...])

    return pl.pallas_call(body, name=name, out_shape=jax.ShapeDtypeStruct((16, w_mod_l.shape[1]), f32),
                          compiler_params=_CP())(c9, w_mod_l)


def _mod_select(m_all, b_mod6, name):
    def body(m_ref, b_ref, mx_ref, mc_ref):
        me = _me()
        mx_ref[...] = m_ref[me] + b_ref[...]
        mc_ref[...] = m_ref[8] + b_ref[...]

    return pl.pallas_call(body, name=name, out_shape=[jax.ShapeDtypeStruct((6, D), f32)] * 2, compiler_params=_CP())(m_all, b_mod6)


def _ada_bwd(c9, dmx_all, dmc_all, dmx_l, dmc_l, w_mod_l, name):
    ncol = w_mod_l.shape[1]

    def rowsum(r):
        acc = r[0:1]
        for j in range(1, NDEV):
            acc = acc + r[j:j + 1]
        return acc

    def body(c_ref, xa_ref, ca_ref, xl_ref, cl_ref, w_ref, gw_ref, gb_ref, dc_ref):
        s9, vjp = jax.vjp(jax.nn.silu, c_ref[...])
        dm9 = jnp.concatenate([xl_ref[...], rowsum(cl_ref[...]), jnp.zeros((7, ncol), f32)], axis=0)
        gw_ref[...] = dtn(s9, dm9)
        gb_ref[...] = rowsum(xa_ref[...]) + rowsum(ca_ref[...])
        dc_ref[...] = vjp(dnt(dm9, w_ref[...]))[0]

    return pl.pallas_call(
        body, name=name,
        out_shape=[jax.ShapeDtypeStruct((D, ncol), f32), jax.ShapeDtypeStruct((1, 6 * D), f32), jax.ShapeDtypeStruct((16, D), f32)],
        compiler_params=_CP())(c9, dmx_all, dmc_all, dmx_l, dmc_l, w_mod_l)


def _lane_sign(rank):
    shape = (1,) * (rank - 1) + (SB,)
    return jnp.where(lax.broadcasted_iota(jnp.int32, shape, rank - 1) < S5N, -1.0, 1.0)


def _s5_build_fn(lre2, lim2, ls, bn, bs, cn, cs, rev):
    sg = _lane_sign(3)
    s = jnp.exp(ls)
    ar, ai = lre2 * s, lim2 * s
    e = jnp.exp(ar)
    nr, ni = e * jnp.cos(ai) - 1.0, e * jnp.sin(ai)
    den = lre2 * lre2 + lim2 * lim2
    cr, ci = (nr * lre2 + ni * lim2) / den, (ni * lre2 - nr * lim2) / den
    bbn = cr * bn + (ci * sg) * bs
    bbs = cr * bs - (ci * sg) * bn

    def powers(ex):
        m, ang = jnp.exp(ex * ar), ex * ai
        return m * jnp.cos(ang), m * jnp.sin(ang) * sg

    def times(tabs, xn, xs):
        f1, f2 = tabs
        return f1[:, :, None, :] * xn[:, None, :, :] + f2[:, :, None, :] * xs[:, None, :, :]

    t = lax.broadcasted_iota(jnp.int32, (1, TC, 1), 1).astype(f32)
    if rev:
        e_src, e_dst, e_out, e_in = t - (TC - 1.0), (TC - 1.0) - t, t, TC - t
    else:
        e_src, e_dst, e_out, e_in = -t, t, (TC - 1.0) - t, t + 1.0
    g = lre2.shape[0]
    flat = lambda a: a.reshape(g, TCP, SB)
    conj = -_lane_sign(4)
    ll = flat(times(powers(e_src), bbn, bbs))
    rr = flat(times(powers(e_dst), cn, cs) * conj)
    mb = flat(times(powers(e_out), bbn, bbs))
    mct = flat(times(powers(e_in), cn, cs) * conj)
    a1, a2 = powers(float(TC))
    row = lax.broadcasted_iota(jnp.int32, (TCP, TCP), 0) // S5P
    col = lax.broadcasted_iota(jnp.int32, (TCP, TCP), 1) // S5P
    mask = jnp.where((col <= row) if rev else (col >= row), 1.0, 0.0)
    m = jnp.concatenate([dnt(ll[j], rr[j])[None] for j in range(g)], axis=0) * mask
    return m, mb, mct, a1, a2


def _gspec(*tail):
    nt = len(tail)
    return pl.BlockSpec((GBK,) + tail, lambda i, _n=nt: (i,) + (0,) * _n)


def _s5_build(params, rev, name):
    def body(l1, l2, ls, bn, bs, cn, cs, m_ref, mb_ref, mc_ref, a1_ref, a2_ref):
        m, mb, mct, a1, a2 = _s5_build_fn(l1[...], l2[...], ls[...], bn[...], bs[...], cn[...], cs[...], rev)
        m_ref[...], mb_ref[...], mc_ref[...] = m.astype(bf16), mb.astype(bf16), mct.astype(bf16)
        a1_ref[...], a2_ref[...] = a1, a2

    vec, pm = _gspec(1, SB), _gspec(S5P, SB)
    return pl.pallas_call(
        body, name=name, grid=(S5G // GBK,),
        in_specs=[vec, vec, _gspec(1, 1), pm, pm, pm, pm],
        out_specs=[_gspec(TCP, TCP), _gspec(TCP, SB), _gspec(TCP, SB), vec, vec],
        out_shape=[jax.ShapeDtypeStruct((S5G, TCP, TCP), bf16), jax.ShapeDtypeStruct((S5G, TCP, SB), bf16),
                   jax.ShapeDtypeStruct((S5G, TCP, SB), bf16), jax.ShapeDtypeStruct((S5G, 1, SB), f32),
                   jax.ShapeDtypeStruct((S5G, 1, SB), f32)],
        compiler_params=_CP(dimension_semantics=("parallel",)),
    )(*params)


def _s5_build_bwd(params, cots, prev, rev, name):
    def body(l1, l2, ls, bn, bs, cn, cs, dm, dmb, dmc, da1, da2, pb, pc, gl1, gl2, gls, gb, gc):
        prim = (l1[...], l2[...], ls[...], bn[...], bs[...], cn[...], cs[...])
        _, vjp = jax.vjp(functools.partial(_s5_build_fn, rev=rev), *prim)
        d1, d2, dls, dbn, dbs, dcn, dcs = vjp((dm[...], dmb[...], dmc[...], da1[...], da2[...]))
        gl1[...] = d1 + pltpu.roll(d1, S5N, axis=2)
        gl2[...] = d2 + pltpu.roll(d2, S5N, axis=2)
        gls[...] = dls
        gb[...] = dbn + pltpu.roll(dbs, S5N, axis=2) + pb[...]
        gc[...] = dcn + pltpu.roll(dcs, S5N, axis=2) + pc[...]

    vec, pm, big = _gspec(1, SB), _gspec(S5P, SB), _gspec(TCP, SB)
    return pl.pallas_call(
        body, name=name, grid=(S5G // GBK,),
        in_specs=[vec, vec, _gspec(1, 1), pm, pm, pm, pm, _gspec(TCP, TCP), big, big, vec, vec, pm, pm],
        out_specs=[vec, vec, _gspec(1, 1), pm, pm],
        out_shape=[jax.ShapeDtypeStruct((S5G, 1, SB), f32), jax.ShapeDtypeStruct((S5G, 1, SB), f32),
                   jax.ShapeDtypeStruct((S5G, 1, 1), f32), jax.ShapeDtypeStruct((S5G, S5P, SB), f32),
                   jax.ShapeDtypeStruct((S5G, S5P, SB), f32)],
        compiler_params=_CP(dimension_semantics=("parallel",)),
    )(*params, *cots, *prev)


def _s5_inc(u, mb_f, mb_b, name):
    nc = u.shape[1]

    def body(u_ref, mf_ref, mb_ref, sf_ref, sb_ref):
        for j in range(GBK):
            sf_ref[:, j, :] = jnp.dot(u_ref[j], mf_ref[j], preferred_element_type=f32)
            sb_ref[:, j, :] = jnp.dot(u_ref[j], mb_ref[j], preferred_element_type=f32)

    sspec = pl.BlockSpec((nc, GBK, SB), lambda i: (0, i, 0))
    return pl.pallas_call(
        body, name=name, grid=(S5G // GBK,), in_specs=[_gspec(nc, TCP), _gspec(TCP, SB), _gspec(TCP, SB)],
        out_specs=[sspec, sspec], out_shape=[jax.ShapeDtypeStruct((nc, S5G, SB), f32)] * 2,
        compiler_params=_CP(dimension_semantics=("parallel",)),
    )(u, mb_f, mb_b)


def _idx_fwd(nctx, nch):
    return lambda i: i


def _idx_rev(nctx, nch):
    return lambda i: jnp.where(i < nctx, nctx - 1 - i, nch + nctx - 1 - i)


def _s5_carry(s3, a1, a2, rev, nctx, name):
    nc = s3.shape[0]
    idx = (_idx_rev if rev else _idx_fwd)(nctx, nc)

    def body(s_ref, a1_ref, a2_ref, hp_ref):
        f1, f2 = a1_ref[...], a2_ref[...]

        def step(i, h):
            r = idx(i)
            hp_ref[r] = h
            return f1 * h + f2 * pltpu.roll(h, S5N, axis=1) + s_ref[r]

        lax.fori_loop(0, nc, step, jnp.zeros((S5G, SB), f32))

    return pl.pallas_call(body, name=name, out_shape=jax.ShapeDtypeStruct(s3.shape, f32), compiler_params=_CP())(s3, a1, a2)


def _s5_carry_bwd(dhp, hp, a1, a2, rev, nctx, name):
    nc = hp.shape[0]
    idx = (_idx_rev if rev else _idx_fwd)(nctx, nc)

    def body(dhp_ref, hp_ref, a1_ref, a2_ref, ds_ref, d1_ref, d2_ref):
        f1, f2 = a1_ref[...], a2_ref[...]

        def step(k, carry):
            ab, d1, d2 = carry
            r = idx(nc - 1 - k)
            ds_ref[r] = ab
            h = hp_ref[r]
            return (dhp_ref[r] + f1 * ab + pltpu.roll(f2 * ab, S5N, axis=1), d1 + ab * h, d2 + ab * pltpu.roll(h, S5N, axis=1))

        z = jnp.zeros((S5G, SB), f32)
        _, d1, d2 = lax.fori_loop(0, nc, step, (z, z, z))
        d1_ref[...], d2_ref[...] = d1, d2

    return pl.pallas_call(
        body, name=name,
        out_shape=[jax.ShapeDtypeStruct(hp.shape, f32), jax.ShapeDtypeStruct((S5G, SB), f32), jax.ShapeDtypeStruct((S5G, SB), f32)],
        compiler_params=_CP())(dhp, hp, a1, a2)


def _s5_out(u, m_f, m_b, hp_f, hp_b, mc_f, mc_b, name):
    nc = u.shape[1]

    def body(u_ref, mf_ref, mb_ref, hf_ref, hb_ref, cf_ref, cb_ref, y_ref):
        for j in range(GBK):
            uj = u_ref[j]
            y_ref[j] = (jnp.dot(uj, mf_ref[j], preferred_element_type=f32) + jnp.dot(uj, mb_ref[j], preferred_element_type=f32)
                        + dnt(hf_ref[:, j, :], cf_ref[j]) + dnt(hb_ref[:, j, :], cb_ref[j]))

    sspec = pl.BlockSpec((nc, GBK, SB), lambda i: (0, i, 0))
    return pl.pallas_call(
        body, name=name, grid=(S5G // GBK,),
        in_specs=[_gspec(nc, TCP), _gspec(TCP, TCP), _gspec(TCP, TCP), sspec, sspec, _gspec(TCP, SB), _gspec(TCP, SB)],
        out_specs=_gspec(nc, TCP), out_shape=jax.ShapeDtypeStruct((S5G, nc, TCP), f32),
        compiler_params=_CP(dimension_semantics=("parallel",)),
    )(u, m_f, m_b, hp_f, hp_b, mc_f, mc_b)


def _s5_out_bwd(dy, u, m_f, m_b, hp_f, hp_b, mc_f, mc_b, name):
    nc = u.shape[1]

    def body(dy_ref, u_ref, mf_ref, mb_ref, hf_ref, hb_ref, cf_ref, cb_ref, du_ref, g_ref, dhf_ref, dhb_ref, dcf_ref, dcb_ref):
        for j in range(GBK):
            dyj = dy_ref[j].astype(bf16)
            du_ref[j] = dnt(dyj, mf_ref[j]) + dnt(dyj, mb_ref[j])
            g_ref[j] = dtn(u_ref[j], dyj)
            dhf_ref[:, j, :] = dnn(dyj, cf_ref[j])
            dhb_ref[:, j, :] = dnn(dyj, cb_ref[j])
            dcf_ref[j] = dtn(dyj, hf_ref[:, j, :])
            dcb_ref[j] = dtn(dyj, hb_ref[:, j, :])

    sspec = pl.BlockSpec((nc, GBK, SB), lambda i: (0, i, 0))
    sshape = jax.ShapeDtypeStruct((nc, S5G, SB), f32)
    cshape = jax.ShapeDtypeStruct((S5G, TCP, SB), f32)
    return pl.pallas_call(
        body, name=name, grid=(S5G // GBK,),
        in_specs=[_gspec(nc, TCP), _gspec(nc, TCP), _gspec(TCP, TCP), _gspec(TCP, TCP), sspec, sspec, _gspec(TCP, SB), _gspec(TCP, SB)],
        out_specs=[_gspec(nc, TCP), _gspec(TCP, TCP), sspec, sspec, _gspec(TCP, SB), _gspec(TCP, SB)],
        out_shape=[jax.ShapeDtypeStruct((S5G, nc, TCP), f32), jax.ShapeDtypeStruct((S5G, TCP, TCP), f32), sshape, sshape, cshape, cshape],
        compiler_params=_CP(dimension_semantics=("parallel",)),
    )(dy, u, m_f, m_b, hp_f, hp_b, mc_f, mc_b)


def _s5_inc_bwd(du1, u, ds_f, ds_b, mb_f, mb_b, name):
    nc = u.shape[1]

    def body(du1_ref, u_ref, dsf_ref, dsb_ref, mf_ref, mb_ref, du_ref, dmf_ref, dmb_ref):
        for j in range(GBK):
            dsf, dsb = dsf_ref[:, j, :], dsb_ref[:, j, :]
            du_ref[j] = du1_ref[j] + dnt(dsf, mf_ref[j]) + dnt(dsb, mb_ref[j])
            dmf_ref[j] = dtn(u_ref[j], dsf)
            dmb_ref[j] = dtn(u_ref[j], dsb)

    sspec = pl.BlockSpec((nc, GBK, SB), lambda i: (0, i, 0))
    cshape = jax.ShapeDtypeStruct((S5G, TCP, SB), f32)
    return pl.pallas_call(
        body, name=name, grid=(S5G // GBK,),
        in_specs=[_gspec(nc, TCP), _gspec(nc, TCP), sspec, sspec, _gspec(TCP, SB), _gspec(TCP, SB)],
        out_specs=[_gspec(nc, TCP), _gspec(TCP, SB), _gspec(TCP, SB)],
        out_shape=[jax.ShapeDtypeStruct((S5G, nc, TCP), f32), cshape, cshape],
        compiler_params=_CP(dimension_semantics=("parallel",)),
    )(du1, u, ds_f, ds_b, mb_f, mb_b)


def _to_groups(a):
    n = a.shape[0]
    return a.reshape(n // TC, TC, S5G, S5P).transpose(2, 0, 1, 3).reshape(S5G, n // TC, TCP)


def _from_groups(a):
    nc = a.shape[1]
    return a.reshape(S5G, nc, TC, S5P).transpose(1, 2, 0, 3).reshape(nc * TC, S5W)


def _swap_pairs(t):
    lane = lax.broadcasted_iota(jnp.int32, t.shape, 1)
    return jnp.where(lane % 2 == 0, pltpu.roll(t, DH - 1, axis=1), pltpu.roll(t, 1, axis=1))


def _rot(t, cosf, sins):
    return t * cosf + _swap_pairs(t) * sins


def _rot_t(d, cosf, sins):
    return d * cosf - _swap_pairs(d) * sins


def _ret_chunk(qr, kr, v, rp, ld, rev):
    pos = lax.broadcasted_iota(jnp.int32, (T, 1), 0).astype(f32)
    diff = pos - lax.broadcasted_iota(jnp.int32, (1, T), 1).astype(f32)
    if rev:
        keep, dist = diff < 0, jnp.maximum(-diff, 0.0)
        xi, zeta = jnp.exp(ld * (T - pos)), jnp.exp(ld * pos)
    else:
        keep, dist = diff >= 0, jnp.maximum(diff, 0.0)
        xi, zeta = jnp.exp(ld * (pos + 1.0)), jnp.exp(ld * (T - 1.0 - pos))
    dm = jnp.where(keep, jnp.exp(ld * dist), 0.0)
    out = dnn(dnt(qr, kr) * dm, v) + dnn(qr * xi, rp)
    rn = jnp.exp(ld * float(T)) * rp + dtn(kr * zeta, v)
    return out, rn


def _ret_fwd(p_ext, cosf, sins, ld8, rev, nctx, name, cargo=None):
    n = p_ext.shape[0]
    nch = n // T
    idx = (_idx_rev if rev else _idx_fwd)(nctx, nch)
    scale = DH ** -0.5
    cg = _Cargo(cargo)

    def body(*refs):
        (q_ref, k_ref, v_ref, cos_ref, sin_ref, ld_ref), (o_ref, rp_ref), (r_s,) = cg.split(refs, 6, 2, 1)
        cg.ride(refs, 6, 2, nch)

        @pl.when(pl.program_id(0) == 0)
        def _():
            r_s[...] = jnp.zeros_like(r_s)

        cf, ss = cos_ref[...], sin_ref[...]
        for h in range(RH):
            sl = slice(h * DH, (h + 1) * DH)
            qr = _rot(q_ref[:, sl].astype(f32), cf, ss)
            kr = _rot(k_ref[:, sl].astype(f32), cf, ss) * scale
            rp = r_s[h]
            rp_ref[0, h] = rp
            out, rn = _ret_chunk(qr, kr, v_ref[:, sl].astype(f32), rp, ld_ref[h:h + 1, 0:1], rev)
            r_s[h] = rn
            o_ref[:, sl] = out

    def colspec(cb):
        return pl.BlockSpec((T, RW), lambda i, _c=cb: (idx(i), _c))

    tspec = pl.BlockSpec((T, DH), lambda i: (idx(i), 0))
    return pl.pallas_call(
        body, name=name, grid=(nch,),
        in_specs=[colspec(1), colspec(2), colspec(3), tspec, tspec, _const_spec((8, 128))] + cg.in_specs(),
        out_specs=[pl.BlockSpec((T, RW), lambda i: (idx(i), 0)), pl.BlockSpec((1, RH, DH, DH), lambda i: (i, 0, 0, 0))] + cg.in_specs(),
        out_shape=[jax.ShapeDtypeStruct((n, RW), f32), jax.ShapeDtypeStruct((nch, RH, DH, DH), f32)] + cg.out_shapes(),
        scratch_shapes=[pltpu.VMEM((RH, DH, DH), f32)] + cg.sems(),
        compiler_params=_CP(dimension_semantics=_ARB),
    )(p_ext, p_ext, p_ext, cosf, sins, ld8, *cg.arrays)


def _ret_bwd(p_ext, cosf, sins, ld8, rprev, do_ext, rev, nctx, name, cargo=None):
    n = p_ext.shape[0]
    nch = n // T
    idx0 = (_idx_rev if rev else _idx_fwd)(nctx, nch)
    idx = lambda j: idx0(nch - 1 - j)
    scale = DH ** -0.5
    cg = _Cargo(cargo)

    def body(*refs):
        ins, (dq_ref, dk_ref, dv_ref, dld_ref), (dr_s,) = cg.split(refs, 8, 4, 1)
        q_ref, k_ref, v_ref, cos_ref, sin_ref, ld_ref, rp_ref, do_ref = ins
        cg.ride(refs, 8, 4, nch)

        @pl.when(pl.program_id(0) == 0)
        def _():
            dr_s[...] = jnp.zeros_like(dr_s)
            dld_ref[...] = jnp.zeros_like(dld_ref)

        cf, ss = cos_ref[...], sin_ref[...]
        for h in range(RH):
            sl = slice(h * DH, (h + 1) * DH)
            qr = _rot(q_ref[:, sl].astype(f32), cf, ss)
            kr = _rot(k_ref[:, sl].astype(f32), cf, ss) * scale
            _, vjp = jax.vjp(functools.partial(_ret_chunk, rev=rev), qr, kr, v_ref[:, sl].astype(f32), rp_ref[0, h],
                             ld_ref[h:h + 1, 0:1])
            dqr, dkr, dv, drp, dld = vjp((do_ref[:, sl], dr_s[h]))
            dr_s[h] = drp
            dq_ref[:, sl] = _rot_t(dqr, cf, ss)
            dk_ref[:, sl] = _rot_t(dkr, cf, ss) * scale
            dv_ref[:, sl] = dv
            dld_ref[h:h + 1, :] += jnp.broadcast_to(dld, (1, 128))

    def colspec(cb):
        return pl.BlockSpec((T, RW), lambda j, _c=cb: (idx(j), _c))

    tspec = pl.BlockSpec((T, DH), lambda j: (idx(j), 0))
    ospec = pl.BlockSpec((T, RW), lambda j: (idx(j), 0))
    oshape = jax.ShapeDtypeStruct((n, RW), f32)
    return pl.pallas_call(
        body, name=name, grid=(nch,),
        in_specs=[colspec(1), colspec(2), colspec(3), tspec, tspec, _const_spec((8, 128)),
                  pl.BlockSpec((1, RH, DH, DH), lambda j: (nch - 1 - j, 0, 0, 0)), ospec] + cg.in_specs(),
        out_specs=[ospec, ospec, ospec, _acc_spec((8, 128))] + cg.in_specs(),
        out_shape=[oshape, oshape, oshape, jax.ShapeDtypeStruct((8, 128), f32)] + cg.out_shapes(),
        scratch_shapes=[pltpu.VMEM((RH, DH, DH), f32)] + cg.sems(),
        compiler_params=_CP(dimension_semantics=_ARB),
    )(p_ext, p_ext, p_ext, cosf, sins, ld8, rprev, do_ext, *cg.arrays)


def _f1_fwd(x, ctx, modx, modc, nw1, w_in_n, name):
    L = x.shape[0]
    nb = L // R + 1

    def body(x_ref, c_ref, mx_ref, mc_ref, nw_ref, w_ref, p_ref):
        is_ctx = pl.program_id(0) == 0
        xin = jnp.where(is_ctx, c_ref[...], x_ref[...])
        sh = jnp.where(is_ctx, mc_ref[0:1], mx_ref[0:1])
        sc = jnp.where(is_ctx, mc_ref[1:2], mx_ref[1:2])
        p_ref[...] = dnn(_mod(_rms(xin, nw_ref[...]), sh, sc), w_ref[...]).astype(bf16)

    return pl.pallas_call(
        body, name=name, grid=(nb,),
        in_specs=[pl.BlockSpec((R, D), lambda i: (jnp.maximum(i - 1, 0), 0)), _const_spec((R, D)), _const_spec((6, D)),
                  _const_spec((6, D)), _const_spec((1, D)), _const_spec((D, INC))],
        out_specs=pl.BlockSpec((R, INC), lambda i: (i, 0)),
        out_shape=jax.ShapeDtypeStruct((L + R, INC), bf16),
        compiler_params=_CP(dimension_semantics=("parallel",)),
    )(x, ctx, modx, modc, nw1, w_in_n)


def _f1_bwd(x, ctx, modx, modc, nw1, w_in_t, dx1, parts, name):
    L = x.shape[0]
    nb = L // R + 1

    def body(x_ref, c_ref, mx_ref, mc_ref, nw_ref, w_ref, dx1_ref, du0, du1, dq0, dq1, dk0, dk1, dv0, dv1, dg0,
             gx_ref, dp_ref, h1_ref, dnw_ref, dmx_ref, dmc_ref):
        i = pl.program_id(0)
        is_ctx = i == 0

        @pl.when(is_ctx)
        def _():
            dnw_ref[...] = jnp.zeros_like(dnw_ref)
            dmx_ref[...] = jnp.zeros_like(dmx_ref)
            dmc_ref[...] = jnp.zeros_like(dmc_ref)

        dp = jnp.concatenate([du0[...] + du1[...], dq0[...] + dq1[...], dk0[...] + dk1[...], dv0[...] + dv1[...],
                              dg0[...]], axis=1).astype(bf16)
        dp_ref[...] = dp
        xin = jnp.where(is_ctx, c_ref[...], x_ref[...])
        sh = jnp.where(is_ctx, mc_ref[0:1], mx_ref[0:1])
        sc = jnp.where(is_ctx, mc_ref[1:2], mx_ref[1:2])
        dh = dnn(dp, w_ref[...])
        h, vjp = jax.vjp(lambda a, b, c, d: _mod(_rms(a, b), c, d), xin, nw_ref[...], sh, sc)
        dxin, dnw, dsh, dsc = vjp(dh)
        h1_ref[...] = h.astype(bf16)
        gx_ref[...] = dx1_ref[...] + dxin
        dnw_ref[...] += dnw
        wx = jnp.where(is_ctx, 0.0, 1.0)
        dmx_ref[0:1] += dsh * wx
        dmx_ref[1:2] += dsc * wx
        dmc_ref[0:1] += dsh * (1.0 - wx)
        dmc_ref[1:2] += dsc * (1.0 - wx)

    lat = pl.BlockSpec((R, D), lambda i: (jnp.maximum(i - 1, 0), 0))
    ext = pl.BlockSpec((R, S5W), lambda i: (i, 0))
    return pl.pallas_call(
        body, name=name, grid=(nb,),
        in_specs=[lat, _const_spec((R, D)), _const_spec((6, D)), _const_spec((6, D)), _const_spec((1, D)), _const_spec((INC, D)),
                  lat] + [ext] * 9,
        out_specs=[lat, pl.BlockSpec((R, INC), lambda i: (i, 0)), pl.BlockSpec((R, D), lambda i: (i, 0)),
                   _acc_spec((1, D)), _acc_spec((6, D)), _acc_spec((6, D))],
        out_shape=[jax.ShapeDtypeStruct((L, D), f32), jax.ShapeDtypeStruct((L + R, INC), bf16),
                   jax.ShapeDtypeStruct((L + R, D), bf16), jax.ShapeDtypeStruct((1, D), f32),
                   jax.ShapeDtypeStruct((6, D), f32), jax.ShapeDtypeStruct((6, D), f32)],
        compiler_params=_CP(dimension_semantics=_ARB),
    )(x, ctx, modx, modc, nw1, w_in_t, dx1, *parts)


def _ret_post(yr, g):
    outs = []
    for h in range(RH):
        yh = yr[:, h * DH:(h + 1) * DH]
        mu = jnp.mean(yh, axis=-1, keepdims=True)
        var = jnp.mean((yh - mu) ** 2, axis=-1, keepdims=True)
        outs.append((yh - mu) * lax.rsqrt(var + EPS))
    return jax.nn.silu(g) * jnp.concatenate(outs, axis=1)


def _mix_fn(ys, u, of, ob, g, x, dvec, bglu, gate1, pz, pm, wglu, wout):
    s = jax.nn.gelu(ys + dvec * u)
    z = dnn(s, wglu) + bglu + pz
    cat = jnp.concatenate([s * jax.nn.sigmoid(z), _ret_post(of + ob, g)], axis=1)
    mix = dnn(cat, wout) + pm
    return x + gate1 * mix, (s, cat)


def _mix_fwd(x, ys, of, ob, p_ext, dvec, bglu, modx, wglu, wout, name, cargo=None):
    L = x.shape[0]
    nb = L // R
    cg = _Cargo(cargo)

    def body(*refs):
        ins, (x1_ref,), _ = cg.split(refs, 11, 1, 0)
        x_ref, ys_ref, of_ref, ob_ref, u_ref, g_ref, d_ref, b_ref, mx_ref, wg_ref, wo_ref = ins
        cg.ride(refs, 11, 1, nb)
        x1_ref[...] = _mix_fn(ys_ref[...], u_ref[...].astype(f32), of_ref[...], ob_ref[...], g_ref[...].astype(f32),
                              x_ref[...], d_ref[...], b_ref[...], mx_ref[2:3], 0.0, 0.0, wg_ref[...], wo_ref[...])[0]

    ext = pl.BlockSpec((R, S5W), lambda i: (i + 1, 0))
    return pl.pallas_call(
        body, name=name, grid=(nb,),
        in_specs=[pl.BlockSpec((R, D), lambda i: (i, 0)), ext, ext, ext, ext, pl.BlockSpec((R, RW), lambda i: (i + 1, 4)),
                  _const_spec((1, S5W)), _const_spec((1, S5W)), _const_spec((6, D)), _const_spec((S5W, S5W)), _const_spec((D, D))]
        + cg.in_specs(),
        out_specs=[pl.BlockSpec((R, D), lambda i: (i, 0))] + cg.in_specs(),
        out_shape=[jax.ShapeDtypeStruct((L, D), f32)] + cg.out_shapes(),
        scratch_shapes=cg.sems(),
        compiler_params=_CP(dimension_semantics=_ARB),
    )(x, ys, of, ob, p_ext, p_ext, dvec, bglu, modx, wglu, wout, *cg.arrays)


def _mix_bwd(x, ys, of, ob, p_ext, dvec, bglu, modx, wglu, wout, dx1, name, cargo=None):
    L = x.shape[0]
    nb = L // R + 1
    cg = _Cargo(cargo)

    def body(*refs):
        ins, outs, _ = cg.split(refs, 12, 11, 0)
        x_ref, ys_ref, of_ref, ob_ref, u_ref, g_ref, d_ref, b_ref, mx_ref, wg_ref, wo_ref, dx1_ref = ins
        dy_ref, dud_ref, do_ref, dg_ref, cat_ref, dmix_ref, s_ref, dz_ref, dd_ref, db_ref, dg1_ref = outs
        cg.ride(refs, 12, 11, nb)
        i = pl.program_id(0)

        @pl.when(i == 0)
        def _():
            for r in outs:
                r[...] = jnp.zeros_like(r)

        @pl.when(i > 0)
        def _():
            fn = lambda ys_, u_, of_, g_, d_, b_, g1_, pz_, pm_: _mix_fn(
                ys_, u_, of_, ob_ref[...], g_, x_ref[...], d_, b_, g1_, pz_, pm_, wg_ref[...], wo_ref[...])
            _, vjp, (s, cat) = jax.vjp(fn, ys_ref[...], u_ref[...].astype(f32), of_ref[...], g_ref[...].astype(f32), d_ref[...],
                                       b_ref[...], mx_ref[2:3], jnp.zeros((R, S5W), f32), jnp.zeros((R, D), f32), has_aux=True)
            dy, dud, do, dg, dd, db, dg1, dz, dmix = vjp(dx1_ref[...])
            dy_ref[...], dud_ref[...], do_ref[...], dg_ref[...] = dy, dud, do, dg
            cat_ref[...], dmix_ref[...] = cat.astype(bf16), dmix.astype(bf16)
            s_ref[...], dz_ref[...] = s.astype(bf16), dz.astype(bf16)
            dd_ref[...] += dd
            db_ref[...] += db
            dg1_ref[...] += dg1

    lat = pl.BlockSpec((R, D), lambda i: (jnp.maximum(i - 1, 0), 0))
    lat5 = pl.BlockSpec((R, S5W), lambda i: (jnp.maximum(i - 1, 0), 0))
    ext = pl.BlockSpec((R, S5W), lambda i: (i, 0))
    eshape = jax.ShapeDtypeStruct((L + R, S5W), f32)
    return pl.pallas_call(
        body, name=name, grid=(nb,),
        in_specs=[lat, ext, ext, ext, ext, pl.BlockSpec((R, RW), lambda i: (i, 4)),
                  _const_spec((1, S5W)), _const_spec((1, S5W)), _const_spec((6, D)), _const_spec((S5W, S5W)), _const_spec((D, D)), lat]
        + cg.in_specs(),
        out_specs=[ext, ext, ext, ext, lat, lat, lat5, lat5, _acc_spec((1, S5W)), _acc_spec((1, S5W)), _acc_spec((1, D))]
        + cg.in_specs(),
        out_shape=[eshape, eshape, eshape, eshape, jax.ShapeDtypeStruct((L, D), bf16), jax.ShapeDtypeStruct((L, D), bf16),
                   jax.ShapeDtypeStruct((L, S5W), bf16), jax.ShapeDtypeStruct((L, S5W), bf16),
                   jax.ShapeDtypeStruct((1, S5W), f32), jax.ShapeDtypeStruct((1, S5W), f32), jax.ShapeDtypeStruct((1, D), f32)]
        + cg.out_shapes(),
        scratch_shapes=cg.sems(),
        compiler_params=_CP(dimension_semantics=_ARB),
    )(x, ys, of, ob, p_ext, p_ext, dvec, bglu, modx, wglu, wout, dx1, *cg.arrays)


def _ffn_tail(gc, a, x1, gate2, fnw, pf, wdown, wdown_t, tgt):
    f = jax.nn.gelu(gc) * a
    ffn = _dnn_const(f, wdown, wdown_t) + pf
    y = _rms(x1 + gate2 * ffn, fnw)
    err = y - tgt
    loss = 0.5 * jnp.sum(jnp.mean(err * err, axis=-1, keepdims=True), axis=0, keepdims=True)
    return loss, f


def _ffn_fwd(x1, tgt, nw2, modx, w_a, w_g, cw, cb, wdown, wdown_t, fnw, name):
    L = x1.shape[0]
    nb = L // RF
    per = RF // HALO

    def body(x_ref, xp_ref, xn_ref, t_ref, nw_ref, mx_ref, wa_ref, wg_ref, cw_ref, cb_ref, wd_ref, wdt_ref, fn_ref,
             dx2_ref, da_ref, dgc_ref, f_ref, dffn_ref, loss_ref, dfn_ref, dg2_ref, dcb_ref, dcw_ref):
        i = pl.program_id(0)

        @pl.when(i == 0)
        def _():
            for r in (loss_ref, dfn_ref, dg2_ref, dcb_ref, dcw_ref):
                r[...] = jnp.zeros_like(r)

        nw, sh, sc, gate2 = nw_ref[...], mx_ref[3:4], mx_ref[4:5], mx_ref[5:6]
        x1b = x_ref[...]
        h2 = _mod(_rms(x1b, nw), sh, sc)
        h2e = jnp.concatenate([_mod(_rms(xp_ref[...], nw), sh, sc), h2, _mod(_rms(xn_ref[...], nw), sh, sc)], axis=0)
        a = dnn(h2, wa_ref[...])
        ge = dnn(h2e, wg_ref[...])
        g = ge[HALO:HALO + RF]
        gp = ge[HALO - 1:HALO] * jnp.where(i > 0, 1.0, 0.0)
        gn = ge[HALO + RF:HALO + RF + 1] * jnp.where(i < nb - 1, 1.0, 0.0)
        row = lax.broadcasted_iota(jnp.int32, (RF, 1), 0)
        g_prev = jnp.where(row == 0, gp, pltpu.roll(g, 1, axis=0))
        g_next = jnp.where(row == RF - 1, gn, pltpu.roll(g, RF - 1, axis=0))
        gc = cb_ref[...] + g_prev * cw_ref[0:1] + g * cw_ref[1:2] + g_next * cw_ref[2:3]
        fn = lambda gc_, a_, x_, g2_, fw_, pf_: _ffn_tail(gc_, a_, x_, g2_, fw_, pf_, wd_ref[...], wdt_ref[...], t_ref[...])
        loss, vjp, f = jax.vjp(fn, gc, a, x1b, gate2, fn_ref[...], jnp.zeros((RF, D), f32), has_aux=True)
        dgc, da, dx2, dg2, dfw, dffn = vjp(jnp.ones((1, 1), f32))
        dx2_ref[...] = dx2
        da_ref[...], dgc_ref[...] = da.astype(bf16), dgc
        f_ref[...], dffn_ref[...] = f.astype(bf16), dffn.astype(bf16)
        loss_ref[...] += jnp.broadcast_to(loss, (1, 128))
        dfn_ref[...] += dfw
        dg2_ref[...] += dg2
        dcb_ref[...] += jnp.sum(dgc, axis=0, keepdims=True)
        dcw_ref[0:1] += jnp.sum(dgc * g_prev, axis=0, keepdims=True)
        dcw_ref[1:2] += jnp.sum(dgc * g, axis=0, keepdims=True)
        dcw_ref[2:3] += jnp.sum(dgc * g_next, axis=0, keepdims=True)

    blk = lambda w: pl.BlockSpec((RF, w), lambda i: (i, 0))
    return pl.pallas_call(
        body, name=name, grid=(nb,),
        in_specs=[blk(D), pl.BlockSpec((HALO, D), lambda i: (jnp.maximum(i * per - 1, 0), 0)),
                  pl.BlockSpec((HALO, D), lambda i: (jnp.minimum((i + 1) * per, L // HALO - 1), 0)), blk(D),
                  _const_spec((1, D)), _const_spec((6, D)), _const_spec((D, DFF)), _const_spec((D, DFF)), _const_spec((3, DFF)),
                  _const_spec((1, DFF)), _const_spec((DFF, D)), _const_spec((D, DFF)), _const_spec((1, D))],
        out_specs=[blk(D), blk(DFF), blk(DFF), blk(DFF), blk(D), _acc_spec((1, 128)), _acc_spec((1, D)), _acc_spec((1, D)),
                   _acc_spec((1, DFF)), _acc_spec((3, DFF))],
        out_shape=[jax.ShapeDtypeStruct((L, D), f32), jax.ShapeDtypeStruct((L, DFF), bf16), jax.ShapeDtypeStruct((L, DFF), f32),
                   jax.ShapeDtypeStruct((L, DFF), bf16), jax.ShapeDtypeStruct((L, D), bf16), jax.ShapeDtypeStruct((1, 128), f32),
                   jax.ShapeDtypeStruct((1, D), f32), jax.ShapeDtypeStruct((1, D), f32), jax.ShapeDtypeStruct((1, DFF), f32),
                   jax.ShapeDtypeStruct((3, DFF), f32)],
        compiler_params=_CP(dimension_semantics=_ARB),
    )(x1, x1, x1, tgt, nw2, modx, w_a, w_g, cw, cb, wdown, wdown_t, fnw)


def _ffn_bwd(x1, dx2, da, dgc, nw2, modx, wup_t, cw, name):
    L = x1.shape[0]
    nb = L // RF
    per = RF // HALO

    def body(x_ref, dx2_ref, da_ref, dgc_ref, dgp_ref, dgn_ref, nw_ref, mx_ref, wu_ref, cw_ref,
             dx1_ref, dag_ref, h2_ref, dnw_ref, dmx_ref):
        i = pl.program_id(0)

        @pl.when(i == 0)
        def _():
            dnw_ref[...] = jnp.zeros_like(dnw_ref)
            dmx_ref[...] = jnp.zeros_like(dmx_ref)

        dgc_b = dgc_ref[...]
        before = dgp_ref[HALO - 1:HALO] * jnp.where(i > 0, 1.0, 0.0)
        after = dgn_ref[0:1] * jnp.where(i < nb - 1, 1.0, 0.0)
        row = lax.broadcasted_iota(jnp.int32, (RF, 1), 0)
        d_prev = jnp.where(row == 0, before, pltpu.roll(dgc_b, 1, axis=0))
        d_next = jnp.where(row == RF - 1, after, pltpu.roll(dgc_b, RF - 1, axis=0))
        dg = cw_ref[0:1] * d_next + cw_ref[1:2] * dgc_b + cw_ref[2:3] * d_prev
        dag = jnp.concatenate([da_ref[...], dg.astype(bf16)], axis=1)
        dag_ref[...] = dag
        dh2 = dnn(dag, wu_ref[...])
        h2, vjp = jax.vjp(lambda a, b, c, d: _mod(_rms(a, b), c, d), x_ref[...], nw_ref[...], mx_ref[3:4], mx_ref[4:5])
        dxa, dnw, dsh, dsc = vjp(dh2)
        h2_ref[...] = h2.astype(bf16)
        dx1_ref[...] = dx2_ref[...] + dxa
        dnw_ref[...] += dnw
        dmx_ref[3:4] += dsh
        dmx_ref[4:5] += dsc

    blk = lambda w: pl.BlockSpec((RF, w), lambda i: (i, 0))
    return pl.pallas_call(
        body, name=name, grid=(nb,),
        in_specs=[blk(D), blk(D), blk(DFF), blk(DFF), pl.BlockSpec((HALO, DFF), lambda i: (jnp.maximum(i * per - 1, 0), 0)),
                  pl.BlockSpec((HALO, DFF), lambda i: (jnp.minimum((i + 1) * per, L // HALO - 1), 0)),
                  _const_spec((1, D)), _const_spec((6, D)), _const_spec((2 * DFF, D)), _const_spec((3, DFF))],
        out_specs=[blk(D), blk(2 * DFF), blk(D), _acc_spec((1, D)), _acc_spec((6, D))],
        out_shape=[jax.ShapeDtypeStruct((L, D), f32), jax.ShapeDtypeStruct((L, 2 * DFF), bf16), jax.ShapeDtypeStruct((L, D), bf16),
                   jax.ShapeDtypeStruct((1, D), f32), jax.ShapeDtypeStruct((6, D), f32)],
        compiler_params=_CP(dimension_semantics=_ARB),
    )(x1, dx2, da, dgc, dgc, dgc, nw2, modx, wup_t, cw)


def _matmul_tn(a, b, name):
    k, m = a.shape
    n = b.shape[1]
    divs = lambda d: [c for c in range(d, 0, -128) if d % c == 0]
    _, tm, tn = min((m * (n // cn) + n * (m // cm), cm, cn) for cm in divs(m) for cn in divs(n) if cm * cn * 4 <= ACC_TILE_BYTES)
    tk = next(c for c in (512, 768, 256, 128) if k % c == 0)
    nk = k // tk

    def body(a_ref, b_ref, o_ref, acc):
        q = pl.program_id(2)

        @pl.when(q == 0)
        def _():
            acc[...] = jnp.zeros_like(acc)

        acc[...] += dtn(a_ref[...], b_ref[...])

        @pl.when(q == nk - 1)
        def _():
            o_ref[...] = acc[...].astype(bf16)

    return pl.pallas_call(
        body, name=name, grid=(m // tm, n // tn, nk),
        in_specs=[pl.BlockSpec((tk, tm), lambda i, j, q: (q, i)), pl.BlockSpec((tk, tn), lambda i, j, q: (q, j))],
        out_specs=pl.BlockSpec((tm, tn), lambda i, j, q: (i, j)),
        out_shape=jax.ShapeDtypeStruct((m, n), bf16),
        scratch_shapes=[pltpu.VMEM((tm, tn), f32)],
        compiler_params=_CP(dimension_semantics=("parallel", "parallel", "arbitrary")),
    )(a, b)


def _adamw(w, g, m, v, name):
    c1, c2 = 1.0 - B1 ** STEP, 1.0 - B2 ** STEP

    def body(w_ref, g_ref, m_ref, v_ref, d_ref, nm_ref, nv_ref):
        gg = g_ref[...]
        nm = B1 * m_ref[...] + (1.0 - B1) * gg
        nv = B2 * v_ref[...] + (1.0 - B2) * jnp.square(gg)
        d_ref[...] = -LR * ((nm / c1) / (jnp.sqrt(nv / c2) + AEPS) + WD * w_ref[...])
        nm_ref[...], nv_ref[...] = nm, nv

    return pl.pallas_call(body, name=name, out_shape=[jax.ShapeDtypeStruct(w.shape, f32)] * 3, compiler_params=_CP())(w, g, m, v)


SMALL = ["conv_w", "c_ctx", "norm1_w", "s5_lambda_re_f", "s5_lambda_im_f", "s5_log_step_f", "s5_lambda_re_b", "s5_lambda_im_b",
         "s5_log_step_b", "s5_b_re", "s5_b_im", "s5_c_re", "s5_c_im", "s5_d", "s5_b_glu", "ret_log_decay_f", "ret_log_decay_b",
         "norm2_w", "conv_b", "final_norm_w"]
WEIGHTS = ["c_ctx", "w_mod", "b_mod", "norm1_w", "w_in", "s5_lambda_re_f", "s5_lambda_im_f", "s5_log_step_f", "s5_lambda_re_b",
           "s5_lambda_im_b", "s5_log_step_b", "s5_b_re", "s5_b_im", "s5_c_re", "s5_c_im", "s5_d", "s5_w_glu", "s5_b_glu",
           "ret_log_decay_f", "ret_log_decay_b", "w_out", "norm2_w", "w_up", "conv_w", "conv_b", "w_down", "final_norm_w"]


def _pack_small(vals):
    flat, offs, o = [], [], 0
    for a in vals:
        n = a.size
        npad = -n % 128
        flat.append(jnp.pad(a.reshape(-1), (0, npad)))
        offs.append((o, n))
        o += n + npad
    tail = -o % 1024
    if tail:
        flat.append(jnp.zeros((tail,), f32))
    return jnp.concatenate(flat).reshape(-1, 128), offs


def _unpack_small(packed, offs, shapes):
    flat = packed.reshape(-1)
    return [flat[o:o + n].reshape(s) for (o, n), s in zip(offs, shapes)]


def _rope_tables(L, nctx_rows):
    t = np.arange(L)
    inv = (ROPE_THETA ** (-np.arange(DH // 4, dtype=np.float64) / (DH // 4))).astype(np.float32)
    ang = np.concatenate([(t // GRID_W).astype(np.float32)[:, None] * inv, (t % GRID_W).astype(np.float32)[:, None] * inv], axis=-1)
    cos = np.repeat(np.cos(ang).astype(np.float32), 2, axis=1)
    sin = np.repeat(np.sin(ang).astype(np.float32), 2, axis=1) * np.tile(np.array([-1.0, 1.0], np.float32), DH // 2)
    cosf = np.concatenate([np.ones((nctx_rows, DH), np.float32), cos], axis=0)
    sins = np.concatenate([np.zeros((nctx_rows, DH), np.float32), sin], axis=0)
    return jnp.asarray(cosf), jnp.asarray(sins)


def kernel(x, c, ctx, c_ctx, w_mod, b_mod, norm1_w, w_in, s5_lambda_re_f, s5_lambda_im_f, s5_log_step_f, s5_lambda_re_b, s5_lambda_im_b, s5_log_step_b, s5_b_re, s5_b_im, s5_c_re, s5_c_im, s5_d, s5_w_glu, s5_b_glu, ret_log_decay_f, ret_log_decay_b, w_out, norm2_w, w_up, conv_w, conv_b, w_down, final_norm_w, loss_target, m_c_ctx, m_w_mod, m_b_mod, m_norm1_w, m_w_in, m_s5_lambda_re_f, m_s5_lambda_im_f, m_s5_log_step_f, m_s5_lambda_re_b, m_s5_lambda_im_b, m_s5_log_step_b, m_s5_b_re, m_s5_b_im, m_s5_c_re, m_s5_c_im, m_s5_d, m_s5_w_glu, m_s5_b_glu, m_ret_log_decay_f, m_ret_log_decay_b, m_w_out, m_norm2_w, m_w_up, m_conv_w, m_conv_b, m_w_down, m_final_norm_w, v_c_ctx, v_w_mod, v_b_mod, v_norm1_w, v_w_in, v_s5_lambda_re_f, v_s5_lambda_im_f, v_s5_log_step_f, v_s5_lambda_re_b, v_s5_lambda_im_b, v_s5_log_step_b, v_s5_b_re, v_s5_b_im, v_s5_c_re, v_s5_c_im, v_s5_d, v_s5_w_glu, v_s5_b_glu, v_ret_log_decay_f, v_ret_log_decay_b, v_w_out, v_norm2_w, v_w_up, v_conv_w, v_conv_b, v_w_down, v_final_norm_w):
    args = dict(locals())
    W = {n: args[n] for n in WEIGHTS}
    M = {n: args["m_" + n] for n in WEIGHTS}
    V = {n: args["v_" + n] for n in WEIGHTS}
    me = _me()
    x2, ctx2, tgt = x[0], ctx[0], loss_target[0]
    L, Lc = x2.shape[0], ctx2.shape[0]
    assert Lc == R and L % R == 0 and L % GRID_W == 0
    nctx = Lc // T

    c_all = _all_gather_small(jnp.pad(c, ((0, 7), (0, 0))), "gather_c")[:, 0, :]
    c9 = jnp.concatenate([c_all, c_ctx[None], jnp.zeros((7, D), f32)], axis=0)
    w_mod_l = w_mod[0]
    ncol = w_mod_l.shape[1]
    m_part = _ada_fwd(c9, w_mod_l, "ada_fwd")
    m_all = _all_gather_small(m_part, "gather_mod").transpose(1, 0, 2).reshape(16, 6, D)
    modx, modc = _mod_select(m_all, b_mod.reshape(6, D), "mod_select")

    w_in_tl, w_up_tl = w_in[0].T.astype(bf16), w_up[0].T.astype(bf16)
    w_out_l, w_down_l, w_glu_l = w_out[0].astype(bf16), w_down[0].astype(bf16), s5_w_glu[0].astype(bf16)
    half_up = w_up_tl.shape[0] // 2
    (w_in_g,) = _exchange([w_in_tl], False, "gather_w_in")
    w_in_t = w_in_g.reshape(INC, D)
    per_cv = conv_w.shape[2]
    conv_pad = jnp.pad(conv_w[0], ((0, 5), (0, 128 * 3 - per_cv)))
    conv_f = _all_gather_small(conv_pad, "gather_conv")[:, :3, :per_cv].transpose(1, 0, 2).reshape(3, DFF)

    pair = lambda a, b: jnp.concatenate([a, b], axis=-1)
    bre_g, bim_g = s5_b_re[0].transpose(0, 2, 1), s5_b_im[0].transpose(0, 2, 1)
    cre_g, cim_g = s5_c_re[0], s5_c_im[0]
    shared = (pair(bre_g, bim_g), pair(bim_g, bre_g), pair(cre_g, cim_g), pair(cim_g, cre_g))
    s5p = {}
    for tag, lre, lim, ls in (("f", s5_lambda_re_f, s5_lambda_im_f, s5_log_step_f), ("b", s5_lambda_re_b, s5_lambda_im_b, s5_log_step_b)):
        s5p[tag] = (pair(lre[0], lre[0])[:, None, :], pair(lim[0], lim[0])[:, None, :], ls[0].reshape(S5G, 1, 1)) + shared
    m_f, mb_f, mc_f, a1_f, a2_f = _s5_build(s5p["f"], False, "s5_build_f")
    m_b, mb_b, mc_b, a1_b, a2_b = _s5_build(s5p["b"], True, "s5_build_b")
    a1_f, a2_f, a1_b, a2_b = (a.reshape(S5G, SB) for a in (a1_f, a2_f, a1_b, a2_b))

    nw1, nw2, fnw = norm1_w, norm2_w, final_norm_w[None]
    p_ext = _f1_fwd(x2, ctx2, modx, modc, nw1, w_in_t.T, "f1_fwd")
    nctx5 = Lc // TC
    u_g = _to_groups(p_ext[:, :S5W])
    s_f, s_b = _s5_inc(u_g, mb_f, mb_b, "s5_inc")
    hp_f = _s5_carry(s_f, a1_f, a2_f, False, nctx5, "s5_carry_f")
    hp_b = _s5_carry(s_b, a1_b, a2_b, True, nctx5, "s5_carry_b")
    ys = _from_groups(_s5_out(u_g, m_f, m_b, hp_f, hp_b, mc_f, mc_b, "s5_out"))
    cosf, sins = _rope_tables(L, Lc)
    ld8 = lambda ld: jnp.pad(jnp.broadcast_to(ld[0][:, None], (RH, 128)), ((0, 8 - RH), (0, 0)))
    ldf8, ldb8 = ld8(ret_log_decay_f), ld8(ret_log_decay_b)
    of, rp_f, w_out_g, w_glu_g, w_up_g1 = _ret_fwd(p_ext, cosf, sins, ldf8, False, nctx, "ret_fwd_f",
                                                   cargo=([w_out_l, w_glu_l, w_up_tl[:half_up]], False))
    ob, rp_b, w_up_g2 = _ret_fwd(p_ext, cosf, sins, ldb8, True, nctx, "ret_fwd_b", cargo=([w_up_tl[half_up:]], False))
    w_out_f, w_glu_f = w_out_g.reshape(D, D), w_glu_g.reshape(S5W, S5W)
    x1, w_down_g = _mix_fwd(x2, ys, of, ob, p_ext, s5_d, s5_b_glu, modx, w_glu_f, w_out_f, "mix_fwd", cargo=([w_down_l], False))
    w_down_f = w_down_g.reshape(DFF, D)
    w_up_t = jnp.concatenate([w_up_g1, w_up_g2], axis=1).reshape(2 * DFF, D)

    (dx2, da, dgc, f_act, dffn, loss_acc, g_fnw, g_gate2, g_cb, g_cw) = _ffn_fwd(
        x1, tgt, nw2, modx, w_up_t[:DFF].T, w_up_t[DFF:].T, conv_f, conv_b, w_down_f, w_down_f.T, fnw, "ffn_fwd")
    dx1, dag, h2, g_nw2, dmx2 = _ffn_bwd(x1, dx2, da, dgc, nw2, modx, w_up_t, conv_f, "ffn_bwd")
    gw_down = _matmul_tn(f_act, dffn, "dw_down").reshape(NDEV, -1, D)
    gw_up_t = _matmul_tn(dag, h2, "dw_up").reshape(NDEV, -1, D)
    (dy_e, dud_e, do_e, dg_e, cat, dmix, s_act, dz, g_d, g_bglu, g_gate1, l_down) = _mix_bwd(
        x2, ys, of, ob, p_ext, s5_d, s5_b_glu, modx, w_glu_f, w_out_f, dx1, "mix_bwd", cargo=([gw_down], True))
    gw_out = _matmul_tn(cat, dmix, "dw_out").reshape(NDEV, -1, D)
    gw_glu = _matmul_tn(s_act, dz, "dw_glu").reshape(NDEV, -1, S5W)
    dq_f, dk_f, dv_f, gld_f, l_up = _ret_bwd(p_ext, cosf, sins, ldf8, rp_f, do_e, False, nctx, "ret_bwd_f", cargo=([gw_up_t], True))
    dq_b, dk_b, dv_b, gld_b, l_out, l_glu = _ret_bwd(p_ext, cosf, sins, ldb8, rp_b, do_e, True, nctx, "ret_bwd_b",
                                                     cargo=([gw_out, gw_glu], True))

    du1, g_m, dhp_f, dhp_b, dmc_f, dmc_b = _s5_out_bwd(_to_groups(dy_e), u_g, m_f, m_b, hp_f, hp_b, mc_f, mc_b, "s5_out_bwd")
    ds_f, da1_f, da2_f = _s5_carry_bwd(dhp_f, hp_f, a1_f, a2_f, False, nctx5, "s5_carry_bwd_f")
    ds_b, da1_b, da2_b = _s5_carry_bwd(dhp_b, hp_b, a1_b, a2_b, True, nctx5, "s5_carry_bwd_b")
    du_g, dmb_f, dmb_b = _s5_inc_bwd(du1, u_g, ds_f, ds_b, mb_f, mb_b, "s5_inc_bwd")
    zero_p = jnp.zeros((S5G, S5P, SB), f32)
    gf = _s5_build_bwd(s5p["f"], (g_m, dmb_f, dmc_f, da1_f[:, None, :], da2_f[:, None, :]), (zero_p, zero_p), False, "s5_build_bwd_f")
    gb = _s5_build_bwd(s5p["b"], (g_m, dmb_b, dmc_b, da1_b[:, None, :], da2_b[:, None, :]), (gf[3], gf[4]), True, "s5_build_bwd_b")
    g_bre, g_bim = gb[3][:, :, :S5N].transpose(0, 2, 1), gb[3][:, :, S5N:].transpose(0, 2, 1)
    g_cre, g_cim = gb[4][:, :, :S5N], gb[4][:, :, S5N:]

    grad_x, dp_ext, h1, g_nw1, dmx1, dmc1 = _f1_bwd(
        x2, ctx2, modx, modc, nw1, w_in_t, dx1, (_from_groups(du_g), dud_e, dq_f, dq_b, dk_f, dk_b, dv_f, dv_b, dg_e), "f1_bwd")
    gw_in_t = _matmul_tn(dp_ext, h1, "dw_in").reshape(NDEV, -1, D)
    (l_in,) = _exchange([gw_in_t], True, "scatter_dw_in")

    dmx = dmx1 + dmx2
    dmx = dmx.at[2].set(g_gate1[0]).at[5].set(g_gate2[0])
    dm_me = jnp.stack([dmx.reshape(-1), dmc1.reshape(-1)], axis=0)
    dm_all = _all_gather_small(jnp.pad(dm_me, ((0, 6), (0, 0))), "gather_dmod")
    dmx_all, dmc_all = dm_all[:, 0, :], dm_all[:, 1, :]
    my_cols = lambda a: lax.dynamic_slice(a, (0, me * ncol), (NDEV, ncol))
    gw_mod, g_bmod, dc9 = _ada_bwd(c9, dmx_all, dmc_all, my_cols(dmx_all), my_cols(dmc_all), w_mod_l, "ada_bwd")

    small = {
        "conv_w": g_cw, "c_ctx": dc9[8], "norm1_w": g_nw1, "s5_lambda_re_f": gf[0][:, 0, :S5N], "s5_lambda_im_f": gf[1][:, 0, :S5N],
        "s5_log_step_f": gf[2], "s5_lambda_re_b": gb[0][:, 0, :S5N], "s5_lambda_im_b": gb[1][:, 0, :S5N], "s5_log_step_b": gb[2],
        "s5_b_re": g_bre, "s5_b_im": g_bim, "s5_c_re": g_cre, "s5_c_im": g_cim, "s5_d": g_d, "s5_b_glu": g_bglu,
        "ret_log_decay_f": gld_f[:RH, 0], "ret_log_decay_b": gld_b[:RH, 0], "norm2_w": g_nw2, "conv_b": g_cb, "final_norm_w": g_fnw,
    }
    packed, soffs = _pack_small([small[n].astype(f32) for n in SMALL])
    red = _all_reduce_small(packed, "reduce_small")
    sshapes = [(3, DFF) if n == "conv_w" else W[n].shape for n in SMALL]
    G = dict(zip(SMALL, _unpack_small(red, soffs, sshapes)))
    G["conv_w"] = lax.dynamic_slice(G["conv_w"], (0, me * per_cv), (3, per_cv))[None]
    G["b_mod"] = g_bmod.reshape(b_mod.shape)
    G["w_mod"] = gw_mod[None]
    G["w_in"] = _sum8(l_in, "sum_dw_in").T[None]
    G["w_up"] = _sum8(l_up, "sum_dw_up").T[None]
    G["w_out"] = _sum8(l_out, "sum_dw_out")[None]
    G["w_down"] = _sum8(l_down, "sum_dw_down")[None]
    G["s5_w_glu"] = _sum8(l_glu, "sum_dw_glu")[None]

    delta, new_m, new_v = {}, {}, {}
    sm_names = SMALL[1:] + ["b_mod"]
    pk = lambda d: _pack_small([d[n].astype(f32) for n in sm_names])
    (pw, aoffs), (pg, _), (pm, _), (pv, _) = pk(W), pk(G), pk(M), pk(V)
    pd, pnm, pnv = _adamw(pw, pg, pm, pv, "adamw_small")
    shapes = [W[n].shape for n in sm_names]
    for dst, src in ((delta, pd), (new_m, pnm), (new_v, pnv)):
        dst.update(zip(sm_names, _unpack_small(src, aoffs, shapes)))
    for n in ["w_mod", "w_in", "w_out", "w_up", "w_down", "s5_w_glu", "conv_w"]:
        d, nm, nv = _adamw(W[n][0], G[n][0], M[n][0], V[n][0], "adamw_" + n)
        delta[n], new_m[n], new_v[n] = d[None], nm[None], nv[None]

    loss = lax.psum(loss_acc[0, 0], ("x", "y", "c"))
    return (loss, grad_x[None], *[G[n] for n in WEIGHTS], *[delta[n] for n in WEIGHTS], *[new_m[n] for n in WEIGHTS],
            *[new_v[n] for n in WEIGHTS])
```

```python
import functools

import numpy as np
import jax
import jax.numpy as jnp
from jax import lax
from jax.experimental import pallas as pl
from jax.experimental.pallas import tpu as pltpu

f32, bf16 = jnp.float32, jnp.bfloat16

D = 1024
S5W, S5G, S5P, S5N = 512, 32, 16, 64
TC = 16
TCP = TC * S5P
SB = 2 * S5N
GBK = 8
RH, DH = 4, 128
RW = RH * DH
INC = S5W + 4 * RW
DFF = 2816
T = 128
R = 256
RF = 128
HALO = 8
EPS = 1e-6
ROPE_THETA = 10000.0
GRID_W = 64
NDEV = 8
LR, B1, B2, AEPS, WD, STEP = 0.001, 0.9, 0.999, 1e-08, 0.01, 10
VMEM_LIMIT = 60 * 1024 * 1024
ACC_TILE_BYTES = 6 * 1024 * 1024
MESH = pl.DeviceIdType.MESH

_CP = functools.partial(pltpu.CompilerParams, vmem_limit_bytes=VMEM_LIMIT)
_ARB = ("arbitrary",)
_ANY = pl.BlockSpec(memory_space=pl.ANY)


def _dg(a, b, dims):
    return lax.dot_general(a.astype(bf16), b.astype(bf16), (dims, ((), ())), preferred_element_type=f32)


@jax.custom_vjp
def dnn(a, b):
    return _dg(a, b, ((1,), (0,)))


@jax.custom_vjp
def dnt(a, b):
    return _dg(a, b, ((1,), (1,)))


@jax.custom_vjp
def dtn(a, b):
    return _dg(a, b, ((0,), (0,)))


dnn.defvjp(lambda a, b: (dnn(a, b), (a, b)), lambda r, g: (dnt(g, r[1]).astype(r[0].dtype), dtn(r[0], g).astype(r[1].dtype)))
dnt.defvjp(lambda a, b: (dnt(a, b), (a, b)), lambda r, g: (dnn(g, r[1]).astype(r[0].dtype), dtn(g, r[0]).astype(r[1].dtype)))
dtn.defvjp(lambda a, b: (dtn(a, b), (a, b)), lambda r, g: (dnt(r[1], g).astype(r[0].dtype), dnn(r[0], g).astype(r[1].dtype)))


@jax.custom_vjp
def _dnn_const(a, w, wt):
    return dnn(a, w)


_dnn_const.defvjp(lambda a, w, wt: (dnn(a, w), wt), lambda wt, g: (dnn(g, wt), None, None))


def _rms(t, w):
    return t * lax.rsqrt(jnp.mean(t * t, axis=-1, keepdims=True) + EPS) * w


def _mod(h, shift, scale):
    return h * (1.0 + scale) + shift


def _const_spec(shape):
    n = len(shape)
    return pl.BlockSpec(shape, lambda i, _n=n: (0,) * _n, pipeline_mode=pl.Buffered(1))


def _acc_spec(shape):
    n = len(shape)
    return pl.BlockSpec(shape, lambda i, _n=n: (0,) * _n)


def _me():
    return 4 * lax.axis_index("x") + 2 * lax.axis_index("y") + lax.axis_index("c")


def _peer(r):
    x, y, c = lax.axis_index("x"), lax.axis_index("y"), lax.axis_index("c")
    px = 1 - x if (r >> 2) & 1 else x
    py = 1 - y if (r >> 1) & 1 else y
    pc = 1 - c if r & 1 else c
    return (px, py, pc), 4 * px + 2 * py + pc


def _all_gather_small(v, name):
    r, c = v.shape

    def body(v_ref, out_ref, send_sems, recv_sems):
        me = _me()
        out_ref[me] = v_ref[...]
        sends = []
        for k in range(1, NDEV):
            peer, _ = _peer(k)
            cp = pltpu.make_async_remote_copy(src_ref=v_ref, dst_ref=out_ref.at[me], send_sem=send_sems.at[k - 1],
                                              recv_sem=recv_sems.at[k - 1], device_id=peer, device_id_type=MESH)
            cp.start()
            sends.append(cp)
        for k in range(1, NDEV):
            peer, pidx = _peer(k)
            pltpu.make_async_remote_copy(src_ref=v_ref, dst_ref=out_ref.at[pidx], send_sem=send_sems.at[k - 1],
                                         recv_sem=recv_sems.at[k - 1], device_id=peer, device_id_type=MESH).wait_recv()
        for cp in sends:
            cp.wait_send()

    return pl.pallas_call(
        body, name=name, out_shape=jax.ShapeDtypeStruct((NDEV, r, c), v.dtype),
        in_specs=[pl.BlockSpec(memory_space=pltpu.VMEM)], out_specs=pl.BlockSpec(memory_space=pltpu.VMEM),
        scratch_shapes=[pltpu.SemaphoreType.DMA((NDEV - 1,)), pltpu.SemaphoreType.DMA((NDEV - 1,))],
        compiler_params=_CP(),
    )(v)


def _all_reduce_small(v, name):
    r, c = v.shape

    def body(v_ref, out_ref, land, send_sems, recv_sems):
        me = _me()
        land[me] = v_ref[...]
        sends = []
        for k in range(1, NDEV):
            peer, _ = _peer(k)
            cp = pltpu.make_async_remote_copy(src_ref=v_ref, dst_ref=land.at[me], send_sem=send_sems.at[k - 1],
                                              recv_sem=recv_sems.at[k - 1], device_id=peer, device_id_type=MESH)
            cp.start()
            sends.append(cp)
        for k in range(1, NDEV):
            peer, pidx = _peer(k)
            pltpu.make_async_remote_copy(src_ref=v_ref, dst_ref=land.at[pidx], send_sem=send_sems.at[k - 1],
                                         recv_sem=recv_sems.at[k - 1], device_id=peer, device_id_type=MESH).wait_recv()
        for cp in sends:
            cp.wait_send()
        acc = land[0]
        for j in range(1, NDEV):
            acc = acc + land[j]
        out_ref[...] = acc

    return pl.pallas_call(
        body, name=name, out_shape=jax.ShapeDtypeStruct((r, c), v.dtype),
        in_specs=[pl.BlockSpec(memory_space=pltpu.VMEM)], out_specs=pl.BlockSpec(memory_space=pltpu.VMEM),
        scratch_shapes=[pltpu.VMEM((NDEV, r, c), v.dtype), pltpu.SemaphoreType.DMA((NDEV - 1,)),
                        pltpu.SemaphoreType.DMA((NDEV - 1,))],
        compiler_params=_CP(),
    )(v)


class _Exchange:
    def __init__(self, srcs, dsts, send_sems, recv_sems, local_sems, scatter):
        me = _me()
        n = len(srcs)
        self.sends, self.recvs, self.locals = [], [], []
        for a, (s, d) in enumerate(zip(srcs, dsts)):
            self.locals.append(pltpu.make_async_copy(s.at[me] if scatter else s, d.at[me], local_sems.at[a]))
        for k in range(1, NDEV):
            peer, pidx = _peer(k)
            for a, (s, d) in enumerate(zip(srcs, dsts)):
                src = s.at[pidx] if scatter else s
                sem = (k - 1) * n + a
                for dst, out in ((d.at[me], self.sends), (d.at[pidx], self.recvs)):
                    out.append(pltpu.make_async_remote_copy(src_ref=src, dst_ref=dst, send_sem=send_sems.at[sem],
                                                            recv_sem=recv_sems.at[sem], device_id=peer, device_id_type=MESH))

    def start(self):
        for cp in self.locals + self.sends:
            cp.start()

    def wait(self):
        for cp in self.recvs:
            cp.wait_recv()
        for cp in self.sends:
            cp.wait_send()
        for cp in self.locals:
            cp.wait()


def _exchange_shapes(arrays, scatter):
    return [jax.ShapeDtypeStruct(a.shape if scatter else (NDEV,) + a.shape, a.dtype) for a in arrays]


def _exchange_sems(n):
    return [pltpu.SemaphoreType.DMA(((NDEV - 1) * n,)), pltpu.SemaphoreType.DMA(((NDEV - 1) * n,)), pltpu.SemaphoreType.DMA((n,))]


def _exchange(arrays, scatter, name):
    n = len(arrays)

    def body(*refs):
        ex = _Exchange(refs[:n], refs[n:2 * n], *refs[2 * n:], scatter)
        ex.start()
        ex.wait()

    return pl.pallas_call(body, name=name, out_shape=_exchange_shapes(arrays, scatter), in_specs=[_ANY] * n,
                          out_specs=[_ANY] * n, scratch_shapes=_exchange_sems(n), compiler_params=_CP())(*arrays)


class _Cargo:
    def __init__(self, cargo):
        self.arrays, self.scatter = cargo if cargo else ([], False)
        self.n = len(self.arrays)

    def in_specs(self):
        return [_ANY] * self.n

    def out_shapes(self):
        return _exchange_shapes(self.arrays, self.scatter)

    def sems(self):
        return _exchange_sems(self.n) if self.n else []

    def split(self, refs, n_in, n_out, n_scratch):
        n = self.n
        return refs[:n_in], refs[n_in + n:n_in + n + n_out], refs[n_in + 2 * n + n_out:n_in + 2 * n + n_out + n_scratch]

    def ride(self, refs, n_in, n_out, nsteps):
        if not self.n:
            return
        n = self.n
        ex = _Exchange(refs[n_in:n_in + n], refs[n_in + n + n_out:n_in + 2 * n + n_out], *refs[-3:], self.scatter)

        @pl.when(pl.program_id(0) == 0)
        def _():
            ex.start()

        @pl.when(pl.program_id(0) == nsteps - 1)
        def _():
            ex.wait()


def _sum8(land, name):
    _, r, c = land.shape
    rb = next((b for b in (256, 64, 32) if r % b == 0), r)

    def body(l_ref, o_ref):
        acc = l_ref[0].astype(f32)
        for j in range(1, NDEV):
            acc = acc + l_ref[j].astype(f32)
        o_ref[...] = acc

    return pl.pallas_call(
        body, name=name, grid=(r // rb,), out_shape=jax.ShapeDtypeStruct((r, c), f32),
        in_specs=[pl.BlockSpec((NDEV, rb, c), lambda i: (0, i, 0))], out_specs=pl.BlockSpec((rb, c), lambda i: (i, 0)),
        compiler_params=_CP(dimension_semantics=("parallel",)),
    )(land)


def _ada_fwd(c9, w_mod_l, name):
    def body(c_ref, w_ref, o_ref):
        o_ref[...] = dnn(jax.nn.silu(c_ref[...]), w_ref[...])

    return pl.pallas_call(body, name=name, out_shape=jax.ShapeDtypeStruct((16, w_mod_l.shape[1]), f32),
                          compiler_params=_CP())(c9, w_mod_l)


def _mod_select(m_all, b_mod6, name):
    def body(m_ref, b_ref, mx_ref, mc_ref):
        me = _me()
        mx_ref[...] = m_ref[me] + b_ref[...]
        mc_ref[...] = m_ref[8] + b_ref[...]

    return pl.pallas_call(body, name=name, out_shape=[jax.ShapeDtypeStruct((6, D), f32)] * 2, compiler_params=_CP())(m_all, b_mod6)


def _ada_bwd(c9, dmx_all, dmc_all, dmx_l, dmc_l, w_mod_l, name):
    ncol = w_mod_l.shape[1]

    def rowsum(r):
        acc = r[0:1]
        for j in range(1, NDEV):
            acc = acc + r[j:j + 1]
        return acc

    def body(c_ref, xa_ref, ca_ref, xl_ref, cl_ref, w_ref, gw_ref, gb_ref, dc_ref):
        s9, vjp = jax.vjp(jax.nn.silu, c_ref[...])
        dm9 = jnp.concatenate([xl_ref[...], rowsum(cl_ref[...]), jnp.zeros((7, ncol), f32)], axis=0)
        gw_ref[...] = dtn(s9, dm9)
        gb_ref[...] = rowsum(xa_ref[...]) + rowsum(ca_ref[...])
        dc_ref[...] = vjp(dnt(dm9, w_ref[...]))[0]

    return pl.pallas_call(
        body, name=name,
        out_shape=[jax.ShapeDtypeStruct((D, ncol), f32), jax.ShapeDtypeStruct((1, 6 * D), f32), jax.ShapeDtypeStruct((16, D), f32)],
        compiler_params=_CP())(c9, dmx_all, dmc_all, dmx_l, dmc_l, w_mod_l)


def _lane_sign(rank):
    shape = (1,) * (rank - 1) + (SB,)
    return jnp.where(lax.broadcasted_iota(jnp.int32, shape, rank - 1) < S5N, -1.0, 1.0)


def _s5_build_fn(lre2, lim2, ls, bn, bs, cn, cs, rev):
    sg = _lane_sign(3)
    s = jnp.exp(ls)
    ar, ai = lre2 * s, lim2 * s
    e = jnp.exp(ar)
    nr, ni = e * jnp.cos(ai) - 1.0, e * jnp.sin(ai)
    den = lre2 * lre2 + lim2 * lim2
    cr, ci = (nr * lre2 + ni * lim2) / den, (ni * lre2 - nr * lim2) / den
    bbn = cr * bn + (ci * sg) * bs
    bbs = cr * bs - (ci * sg) * bn

    def powers(ex):
        m, ang = jnp.exp(ex * ar), ex * ai
        return m * jnp.cos(ang), m * jnp.sin(ang) * sg

    def times(tabs, xn, xs):
        f1, f2 = tabs
        return f1[:, :, None, :] * xn[:, None, :, :] + f2[:, :, None, :] * xs[:, None, :, :]

    t = lax.broadcasted_iota(jnp.int32, (1, TC, 1), 1).astype(f32)
    if rev:
        e_src, e_dst, e_out, e_in = t - (TC - 1.0), (TC - 1.0) - t, t, TC - t
    else:
        e_src, e_dst, e_out, e_in = -t, t, (TC - 1.0) - t, t + 1.0
    g = lre2.shape[0]
    flat = lambda a: a.reshape(g, TCP, SB)
    conj = -_lane_sign(4)
    ll = flat(times(powers(e_src), bbn, bbs))
    rr = flat(times(powers(e_dst), cn, cs) * conj)
    mb = flat(times(powers(e_out), bbn, bbs))
    mct = flat(times(powers(e_in), cn, cs) * conj)
    a1, a2 = powers(float(TC))
    row = lax.broadcasted_iota(jnp.int32, (TCP, TCP), 0) // S5P
    col = lax.broadcasted_iota(jnp.int32, (TCP, TCP), 1) // S5P
    mask = jnp.where((col <= row) if rev else (col >= row), 1.0, 0.0)
    m = jnp.concatenate([dnt(ll[j], rr[j])[None] for j in range(g)], axis=0) * mask
    return m, mb, mct, a1, a2


def _gspec(*tail):
    nt = len(tail)
    return pl.BlockSpec((GBK,) + tail, lambda i, _n=nt: (i,) + (0,) * _n)


def _s5_build(params, m_other, rev, name):
    def body(l1, l2, ls, bn, bs, cn, cs, mo_ref, m_ref, mb_ref, mc_ref, a1_ref, a2_ref):
        m, mb, mct, a1, a2 = _s5_build_fn(l1[...], l2[...], ls[...], bn[...], bs[...], cn[...], cs[...], rev)
        m_ref[...] = (m + mo_ref[...].astype(f32)).astype(bf16)
        mb_ref[...], mc_ref[...] = mb.astype(bf16), mct.astype(bf16)
        a1_ref[...], a2_ref[...] = a1, a2

    vec, pm = _gspec(1, SB), _gspec(S5P, SB)
    return pl.pallas_call(
        body, name=name, grid=(S5G // GBK,),
        in_specs=[vec, vec, _gspec(1, 1), pm, pm, pm, pm, _gspec(TCP, TCP)],
        out_specs=[_gspec(TCP, TCP), _gspec(TCP, SB), _gspec(TCP, SB), vec, vec],
        out_shape=[jax.ShapeDtypeStruct((S5G, TCP, TCP), bf16), jax.ShapeDtypeStruct((S5G, TCP, SB), bf16),
                   jax.ShapeDtypeStruct((S5G, TCP, SB), bf16), jax.ShapeDtypeStruct((S5G, 1, SB), f32),
                   jax.ShapeDtypeStruct((S5G, 1, SB), f32)],
        compiler_params=_CP(dimension_semantics=("parallel",)),
    )(*params, m_other)


def _s5_build_bwd(params, cots, prev, rev, name):
    def body(l1, l2, ls, bn, bs, cn, cs, dm, dmb, dmc, da1, da2, pb, pc, gl1, gl2, gls, gb, gc):
        prim = (l1[...], l2[...], ls[...], bn[...], bs[...], cn[...], cs[...])
        _, vjp = jax.vjp(functools.partial(_s5_build_fn, rev=rev), *prim)
        d1, d2, dls, dbn, dbs, dcn, dcs = vjp((dm[...], dmb[...], dmc[...], da1[...], da2[...]))
        gl1[...] = d1 + pltpu.roll(d1, S5N, axis=2)
        gl2[...] = d2 + pltpu.roll(d2, S5N, axis=2)
        gls[...] = dls
        gb[...] = dbn + pltpu.roll(dbs, S5N, axis=2) + pb[...]
        gc[...] = dcn + pltpu.roll(dcs, S5N, axis=2) + pc[...]

    vec, pm, big = _gspec(1, SB), _gspec(S5P, SB), _gspec(TCP, SB)
    return pl.pallas_call(
        body, name=name, grid=(S5G // GBK,),
        in_specs=[vec, vec, _gspec(1, 1), pm, pm, pm, pm, _gspec(TCP, TCP), big, big, vec, vec, pm, pm],
        out_specs=[vec, vec, _gspec(1, 1), pm, pm],
        out_shape=[jax.ShapeDtypeStruct((S5G, 1, SB), f32), jax.ShapeDtypeStruct((S5G, 1, SB), f32),
                   jax.ShapeDtypeStruct((S5G, 1, 1), f32), jax.ShapeDtypeStruct((S5G, S5P, SB), f32),
                   jax.ShapeDtypeStruct((S5G, S5P, SB), f32)],
        compiler_params=_CP(dimension_semantics=("parallel",)),
    )(*params, *cots, *prev)


LW = GBK * S5P
SW = GBK * SB
TW = TC * LW
NSG = S5G // GBK


def _spread_m(m):
    eye = jnp.eye(GBK, dtype=m.dtype)
    m = m.reshape(NSG, GBK, TC, S5P, TC, S5P)
    out = m[:, :, :, :, :, None, :] * eye[None, :, None, None, None, :, None]
    return out.transpose(0, 2, 1, 3, 4, 5, 6).reshape(NSG, TC, LW, TW)


def _spread_b(mb):
    eye = jnp.eye(GBK, dtype=mb.dtype)
    m = mb.reshape(NSG, GBK, TC, S5P, SB)
    out = m[:, :, :, :, None, :] * eye[None, :, None, None, :, None]
    return out.transpose(0, 2, 1, 3, 4, 5).reshape(NSG, TC, LW, SW)


def _spread_t(mct):
    eye = jnp.eye(GBK, dtype=mct.dtype)
    m = mct.reshape(NSG, GBK, TC, S5P, SB)
    out = m[:, :, :, :, None, :] * eye[None, :, None, None, :, None]
    return out.transpose(0, 2, 1, 3, 4, 5).reshape(NSG, TW, SW)


NRB = 2
_TOK_GRID = dict(dimension_semantics=("parallel", "arbitrary"))


def _tok_spec(nr):
    return pl.BlockSpec((nr, TC, LW), lambda i, r: (r, 0, i))


def _state_spec(nr):
    return pl.BlockSpec((nr, GBK, SB), lambda i, r: (r, i, 0))


def _sg_spec(*tail):
    nt = len(tail)
    return pl.BlockSpec((1,) + tail, lambda i, r, _n=nt: (i,) + (0,) * _n, pipeline_mode=pl.Buffered(1))


def _sg_acc_spec(lead, *tail):
    nt = len(tail)
    return pl.BlockSpec((lead,) + tail, lambda i, r, _n=nt: (i,) + (0,) * _n)


def _lanes(ref, n):
    return jnp.concatenate([ref[:, j, :] for j in range(n)], axis=1)


def _s5_inc(p3, mbx_f, mbx_b, name):
    nc = p3.shape[0]
    nr = nc // NRB

    def body(x_ref, mf_ref, mb_ref, sf_ref, sb_ref):
        accf = accb = jnp.zeros((nr, SW), f32)
        for s in range(TC):
            xs = x_ref[:, s, :]
            accf += jnp.dot(xs, mf_ref[0, s], preferred_element_type=f32)
            accb += jnp.dot(xs, mb_ref[0, s], preferred_element_type=f32)
        for j in range(GBK):
            sf_ref[:, j, :] = accf[:, j * SB:(j + 1) * SB]
            sb_ref[:, j, :] = accb[:, j * SB:(j + 1) * SB]

    return pl.pallas_call(
        body, name=name, grid=(NSG, NRB), in_specs=[_tok_spec(nr), _sg_spec(TC, LW, SW), _sg_spec(TC, LW, SW)],
        out_specs=[_state_spec(nr), _state_spec(nr)], out_shape=[jax.ShapeDtypeStruct((nc, S5G, SB), f32)] * 2,
        compiler_params=_CP(**_TOK_GRID),
    )(p3, mbx_f, mbx_b)


def _idx_fwd(nctx, nch):
    return lambda i: i


def _idx_rev(nctx, nch):
    return lambda i: jnp.where(i < nctx, nctx - 1 - i, nch + nctx - 1 - i)


def _s5_carry(s_f, s_b, a_f, a_b, nctx, name):
    nc = s_f.shape[0]
    idx_b = _idx_rev(nctx, nc)

    def body(sf_ref, sb_ref, f1_ref, f2_ref, b1_ref, b2_ref, hf_ref, hb_ref):
        f1, f2, b1, b2 = f1_ref[...], f2_ref[...], b1_ref[...], b2_ref[...]

        def step(i, c):
            hf, hfs, hb, hbs = c
            rb = idx_b(i)
            hf_ref[i] = hf
            hb_ref[rb] = hb
            sf, sb = sf_ref[i], sb_ref[rb]
            return (f1 * hf + f2 * hfs + sf, f1 * hfs - f2 * hf + pltpu.roll(sf, S5N, axis=1),
                    b1 * hb + b2 * hbs + sb, b1 * hbs - b2 * hb + pltpu.roll(sb, S5N, axis=1))

        z = jnp.zeros((S5G, SB), f32)
        lax.fori_loop(0, nc, step, (z, z, z, z))

    return pl.pallas_call(body, name=name, out_shape=[jax.ShapeDtypeStruct(s_f.shape, f32)] * 2,
                          compiler_params=_CP())(s_f, s_b, *a_f, *a_b)


def _s5_carry_bwd(dhp, hp, a1, a2, rev, nctx, name):
    nc = hp.shape[0]
    idx = (_idx_rev if rev else _idx_fwd)(nctx, nc)

    def body(dhp_ref, hp_ref, a1_ref, a2_ref, ds_ref, d1_ref, d2_ref):
        f1, f2 = a1_ref[...], a2_ref[...]

        def step(k, carry):
            ab, abs_, d1, d2 = carry
            r = idx(nc - 1 - k)
            ds_ref[r] = ab
            h, dh = hp_ref[r], dhp_ref[r]
            return (dh + f1 * ab - f2 * abs_, pltpu.roll(dh, S5N, axis=1) + f1 * abs_ + f2 * ab,
                    d1 + ab * h, d2 + ab * pltpu.roll(h, S5N, axis=1))

        z = jnp.zeros((S5G, SB), f32)
        _, _, d1, d2 = lax.fori_loop(0, nc, step, (z, z, z, z))
        d1_ref[...], d2_ref[...] = d1, d2

    return pl.pallas_call(
        body, name=name,
        out_shape=[jax.ShapeDtypeStruct(hp.shape, f32), jax.ShapeDtypeStruct((S5G, SB), f32), jax.ShapeDtypeStruct((S5G, SB), f32)],
        compiler_params=_CP())(dhp, hp, a1, a2)


def _s5_out(p3, mx, hp_f, hp_b, mcx_f, mcx_b, name):
    nc = p3.shape[0]
    nr = nc // NRB

    def body(x_ref, m_ref, hf_ref, hb_ref, cf_ref, cb_ref, y_ref):
        acc = dnt(_lanes(hf_ref, GBK), cf_ref[0]) + dnt(_lanes(hb_ref, GBK), cb_ref[0])
        for s in range(TC):
            acc += jnp.dot(x_ref[:, s, :], m_ref[0, s], preferred_element_type=f32)
        for t in range(TC):
            y_ref[:, t, :] = acc[:, t * LW:(t + 1) * LW]

    return pl.pallas_call(
        body, name=name, grid=(NSG, NRB),
        in_specs=[_tok_spec(nr), _sg_spec(TC, LW, TW), _state_spec(nr), _state_spec(nr), _sg_spec(TW, SW), _sg_spec(TW, SW)],
        out_specs=_tok_spec(nr), out_shape=jax.ShapeDtypeStruct((nc, TC, S5W), f32),
        compiler_params=_CP(**_TOK_GRID),
    )(p3, mx, hp_f, hp_b, mcx_f, mcx_b)


def _s5_out_bwd(dy3, p3, mx, hp_f, hp_b, mcx_f, mcx_b, fold, name):
    nc = p3.shape[0]
    nr = nc // NRB

    def body(dy_ref, x_ref, m_ref, hf_ref, hb_ref, cf_ref, cb_ref, fold_ref, du_ref, g_ref, dhf_ref, dhb_ref, dcf_ref, dcb_ref):
        @pl.when(pl.program_id(1) == 0)
        def _():
            for r in (g_ref, dcf_ref, dcb_ref):
                r[...] = jnp.zeros_like(r)

        dyf = _lanes(dy_ref, TC).astype(bf16)
        own = (lax.broadcasted_iota(jnp.int32, (LW, TW), 0) // S5P) == ((lax.broadcasted_iota(jnp.int32, (LW, TW), 1) % LW) // S5P)
        for s in range(TC):
            du_ref[:, s, :] = dnt(dyf, m_ref[0, s])
            gs = jnp.where(own, dtn(x_ref[:, s, :], dyf), 0.0)
            gh = gs.astype(bf16)
            gl = (gs - gh.astype(f32)).astype(bf16)
            g_ref[0, s] += (jnp.dot(gh, fold_ref[...], preferred_element_type=f32)
                            + jnp.dot(gl, fold_ref[...], preferred_element_type=f32))
        for h_ref, c_ref, dh_ref, dc_ref in ((hf_ref, cf_ref, dhf_ref, dcf_ref), (hb_ref, cb_ref, dhb_ref, dcb_ref)):
            dh = dnn(dyf, c_ref[0])
            for j in range(GBK):
                dh_ref[:, j, :] = dh[:, j * SB:(j + 1) * SB]
                dc = dtn(dyf, h_ref[:, j, :])
                for t in range(TC):
                    dc_ref[j, t * S5P:(t + 1) * S5P, :] += dc[t * LW + j * S5P:t * LW + (j + 1) * S5P, :]

    sshape = jax.ShapeDtypeStruct((nc, S5G, SB), f32)
    cshape = jax.ShapeDtypeStruct((S5G, TCP, SB), f32)
    wide = pl.BlockSpec((TW, TCP), lambda i, r: (0, 0), pipeline_mode=pl.Buffered(1))
    return pl.pallas_call(
        body, name=name, grid=(NSG, NRB),
        in_specs=[_tok_spec(nr), _tok_spec(nr), _sg_spec(TC, LW, TW), _state_spec(nr), _state_spec(nr), _sg_spec(TW, SW),
                  _sg_spec(TW, SW), wide],
        out_specs=[_tok_spec(nr), _sg_acc_spec(1, TC, LW, TCP), _state_spec(nr), _state_spec(nr), _sg_acc_spec(GBK, TCP, SB),
                   _sg_acc_spec(GBK, TCP, SB)],
        out_shape=[jax.ShapeDtypeStruct((nc, TC, S5W), f32), jax.ShapeDtypeStruct((NSG, TC, LW, TCP), f32), sshape, sshape, cshape, cshape],
        compiler_params=_CP(**_TOK_GRID),
    )(dy3, p3, mx, hp_f, hp_b, mcx_f, mcx_b, fold)


def _s5_inc_bwd(du1, p3, ds_f, ds_b, mbx_f, mbx_b, name):
    nc = p3.shape[0]
    nr = nc // NRB

    def body(du1_ref, x_ref, dsf_ref, dsb_ref, mf_ref, mb_ref, du_ref, dmf_ref, dmb_ref):
        @pl.when(pl.program_id(1) == 0)
        def _():
            dmf_ref[...] = jnp.zeros_like(dmf_ref)
            dmb_ref[...] = jnp.zeros_like(dmb_ref)

        dsf, dsb = _lanes(dsf_ref, GBK), _lanes(dsb_ref, GBK)
        for s in range(TC):
            xs = x_ref[:, s, :]
            du_ref[:, s, :] = du1_ref[:, s, :] + dnt(dsf, mf_ref[0, s]) + dnt(dsb, mb_ref[0, s])
            for ds, dm_ref in ((dsf, dmf_ref), (dsb, dmb_ref)):
                d = dtn(xs, ds)
                for j in range(GBK):
                    dm_ref[j, s * S5P:(s + 1) * S5P, :] += d[j * S5P:(j + 1) * S5P, j * SB:(j + 1) * SB]

    cshape = jax.ShapeDtypeStruct((S5G, TCP, SB), f32)
    return pl.pallas_call(
        body, name=name, grid=(NSG, NRB),
        in_specs=[_tok_spec(nr), _tok_spec(nr), _state_spec(nr), _state_spec(nr), _sg_spec(TC, LW, SW), _sg_spec(TC, LW, SW)],
        out_specs=[_tok_spec(nr), _sg_acc_spec(GBK, TCP, SB), _sg_acc_spec(GBK, TCP, SB)],
        out_shape=[jax.ShapeDtypeStruct((nc, TC, S5W), f32), cshape, cshape],
        compiler_params=_CP(**_TOK_GRID),
    )(du1, p3, ds_f, ds_b, mbx_f, mbx_b)


def _fold_matrix():
    r = np.arange(TW)
    f = np.zeros((TW, TCP), np.float32)
    f[r, (r // LW) * S5P + r % S5P] = 1.0
    return jnp.asarray(f, bf16)


def _swap_pairs(t):
    lane = lax.broadcasted_iota(jnp.int32, t.shape, 1)
    return jnp.where(lane % 2 == 0, pltpu.roll(t, DH - 1, axis=1), pltpu.roll(t, 1, axis=1))


def _rot(t, cosf, sins):
    return t * cosf + _swap_pairs(t) * sins


def _rot_t(d, cosf, sins):
    return d * cosf - _swap_pairs(d) * sins


def _ret_chunk(qr, kr, v, rp, ld, rev):
    pos = lax.broadcasted_iota(jnp.int32, (T, 1), 0).astype(f32)
    diff = pos - lax.broadcasted_iota(jnp.int32, (1, T), 1).astype(f32)
    if rev:
        keep, dist = diff < 0, jnp.maximum(-diff, 0.0)
        xi, zeta = jnp.exp(ld * (T - pos)), jnp.exp(ld * pos)
    else:
        keep, dist = diff >= 0, jnp.maximum(diff, 0.0)
        xi, zeta = jnp.exp(ld * (pos + 1.0)), jnp.exp(ld * (T - 1.0 - pos))
    dm = jnp.where(keep, jnp.exp(ld * dist), 0.0)
    out = dnn(dnt(qr, kr) * dm, v) + dnn(qr * xi, rp)
    rn = jnp.exp(ld * float(T)) * rp + dtn(kr * zeta, v)
    return out, rn


def _ret_fwd(p_ext, cosf, sins, ld8, rev, nctx, name, cargo=None):
    n = p_ext.shape[0]
    nch = n // T
    idx = (_idx_rev if rev else _idx_fwd)(nctx, nch)
    scale = DH ** -0.5
    cg = _Cargo(cargo)

    def body(*refs):
        (q_ref, k_ref, v_ref, cos_ref, sin_ref, ld_ref), (o_ref, rp_ref), (r_s,) = cg.split(refs, 6, 2, 1)
        cg.ride(refs, 6, 2, nch)

        @pl.when(pl.program_id(0) == 0)
        def _():
            r_s[...] = jnp.zeros_like(r_s)

        cf, ss = cos_ref[...], sin_ref[...]
        for h in range(RH):
            sl = slice(h * DH, (h + 1) * DH)
            qr = _rot(q_ref[:, sl].astype(f32), cf, ss)
            kr = _rot(k_ref[:, sl].astype(f32), cf, ss) * scale
            rp = r_s[h]
            rp_ref[0, h] = rp
            out, rn = _ret_chunk(qr, kr, v_ref[:, sl].astype(f32), rp, ld_ref[h:h + 1, 0:1], rev)
            r_s[h] = rn
            o_ref[:, sl] = out

    def colspec(cb):
        return pl.BlockSpec((T, RW), lambda i, _c=cb: (idx(i), _c))

    tspec = pl.BlockSpec((T, DH), lambda i: (idx(i), 0))
    return pl.pallas_call(
        body, name=name, grid=(nch,),
        in_specs=[colspec(1), colspec(2), colspec(3), tspec, tspec, _const_spec((8, 128))] + cg.in_specs(),
        out_specs=[pl.BlockSpec((T, RW), lambda i: (idx(i), 0)), pl.BlockSpec((1, RH, DH, DH), lambda i: (i, 0, 0, 0))] + cg.in_specs(),
        out_shape=[jax.ShapeDtypeStruct((n, RW), f32), jax.ShapeDtypeStruct((nch, RH, DH, DH), f32)] + cg.out_shapes(),
        scratch_shapes=[pltpu.VMEM((RH, DH, DH), f32)] + cg.sems(),
        compiler_params=_CP(dimension_semantics=_ARB),
    )(p_ext, p_ext, p_ext, cosf, sins, ld8, *cg.arrays)


def _ret_bwd(p_ext, cosf, sins, ld8, rprev, do_ext, rev, nctx, name, cargo=None):
    n = p_ext.shape[0]
    nch = n // T
    idx0 = (_idx_rev if rev else _idx_fwd)(nctx, nch)
    idx = lambda j: idx0(nch - 1 - j)
    scale = DH ** -0.5
    cg = _Cargo(cargo)

    def body(*refs):
        ins, (dq_ref, dk_ref, dv_ref, dld_ref), (dr_s,) = cg.split(refs, 8, 4, 1)
        q_ref, k_ref, v_ref, cos_ref, sin_ref, ld_ref, rp_ref, do_ref = ins
        cg.ride(refs, 8, 4, nch)

        @pl.when(pl.program_id(0) == 0)
        def _():
            dr_s[...] = jnp.zeros_like(dr_s)
            dld_ref[...] = jnp.zeros_like(dld_ref)

        cf, ss = cos_ref[...], sin_ref[...]
        for h in range(RH):
            sl = slice(h * DH, (h + 1) * DH)
            qr = _rot(q_ref[:, sl].astype(f32), cf, ss)
            kr = _rot(k_ref[:, sl].astype(f32), cf, ss) * scale
            _, vjp = jax.vjp(functools.partial(_ret_chunk, rev=rev), qr, kr, v_ref[:, sl].astype(f32), rp_ref[0, h],
                             ld_ref[h:h + 1, 0:1])
            dqr, dkr, dv, drp, dld = vjp((do_ref[:, sl], dr_s[h]))
            dr_s[h] = drp
            dq_ref[:, sl] = _rot_t(dqr, cf, ss)
            dk_ref[:, sl] = _rot_t(dkr, cf, ss) * scale
            dv_ref[:, sl] = dv
            dld_ref[h:h + 1, :] += jnp.broadcast_to(dld, (1, 128))

    def colspec(cb):
        return pl.BlockSpec((T, RW), lambda j, _c=cb: (idx(j), _c))

    tspec = pl.BlockSpec((T, DH), lambda j: (idx(j), 0))
    ospec = pl.BlockSpec((T, RW), lambda j: (idx(j), 0))
    oshape = jax.ShapeDtypeStruct((n, RW), f32)
    return pl.pallas_call(
        body, name=name, grid=(nch,),
        in_specs=[colspec(1), colspec(2), colspec(3), tspec, tspec, _const_spec((8, 128)),
                  pl.BlockSpec((1, RH, DH, DH), lambda j: (nch - 1 - j, 0, 0, 0)), ospec] + cg.in_specs(),
        out_specs=[ospec, ospec, ospec, _acc_spec((8, 128))] + cg.in_specs(),
        out_shape=[oshape, oshape, oshape, jax.ShapeDtypeStruct((8, 128), f32)] + cg.out_shapes(),
        scratch_shapes=[pltpu.VMEM((RH, DH, DH), f32)] + cg.sems(),
        compiler_params=_CP(dimension_semantics=_ARB),
    )(p_ext, p_ext, p_ext, cosf, sins, ld8, rprev, do_ext, *cg.arrays)


def _f1_fwd(x, ctx, modx, modc, nw1, w_in_n, name):
    L = x.shape[0]
    nb = L // R + 1

    def body(x_ref, c_ref, mx_ref, mc_ref, nw_ref, w_ref, p_ref):
        is_ctx = pl.program_id(0) == 0
        xin = jnp.where(is_ctx, c_ref[...], x_ref[...])
        sh = jnp.where(is_ctx, mc_ref[0:1], mx_ref[0:1])
        sc = jnp.where(is_ctx, mc_ref[1:2], mx_ref[1:2])
        p_ref[...] = dnn(_mod(_rms(xin, nw_ref[...]), sh, sc), w_ref[...]).astype(bf16)

    return pl.pallas_call(
        body, name=name, grid=(nb,),
        in_specs=[pl.BlockSpec((R, D), lambda i: (jnp.maximum(i - 1, 0), 0)), _const_spec((R, D)), _const_spec((6, D)),
                  _const_spec((6, D)), _const_spec((1, D)), _const_spec((D, INC))],
        out_specs=pl.BlockSpec((R, INC), lambda i: (i, 0)),
        out_shape=jax.ShapeDtypeStruct((L + R, INC), bf16),
        compiler_params=_CP(dimension_semantics=("parallel",)),
    )(x, ctx, modx, modc, nw1, w_in_n)


def _f1_bwd(x, ctx, modx, modc, nw1, w_in_t, dx1, parts, name):
    L = x.shape[0]
    nb = L // R + 1

    def body(x_ref, c_ref, mx_ref, mc_ref, nw_ref, w_ref, dx1_ref, du0, du1, dq0, dq1, dk0, dk1, dv0, dv1, dg0,
             gx_ref, dp_ref, h1_ref, dnw_ref, dmx_ref, dmc_ref):
        i = pl.program_id(0)
        is_ctx = i == 0

        @pl.when(is_ctx)
        def _():
            dnw_ref[...] = jnp.zeros_like(dnw_ref)
            dmx_ref[...] = jnp.zeros_like(dmx_ref)
            dmc_ref[...] = jnp.zeros_like(dmc_ref)

        dp = jnp.concatenate([du0[...] + du1[...], dq0[...] + dq1[...], dk0[...] + dk1[...], dv0[...] + dv1[...],
                              dg0[...]], axis=1).astype(bf16)
        dp_ref[...] = dp
        xin = jnp.where(is_ctx, c_ref[...], x_ref[...])
        sh = jnp.where(is_ctx, mc_ref[0:1], mx_ref[0:1])
        sc = jnp.where(is_ctx, mc_ref[1:2], mx_ref[1:2])
        dh = dnn(dp, w_ref[...])
        h, vjp = jax.vjp(lambda a, b, c, d: _mod(_rms(a, b), c, d), xin, nw_ref[...], sh, sc)
        dxin, dnw, dsh, dsc = vjp(dh)
        h1_ref[...] = h.astype(bf16)
        gx_ref[...] = dx1_ref[...] + dxin
        dnw_ref[...] += dnw
        wx = jnp.where(is_ctx, 0.0, 1.0)
        dmx_ref[0:1] += dsh * wx
        dmx_ref[1:2] += dsc * wx
        dmc_ref[0:1] += dsh * (1.0 - wx)
        dmc_ref[1:2] += dsc * (1.0 - wx)

    lat = pl.BlockSpec((R, D), lambda i: (jnp.maximum(i - 1, 0), 0))
    ext = pl.BlockSpec((R, S5W), lambda i: (i, 0))
    return pl.pallas_call(
        body, name=name, grid=(nb,),
        in_specs=[lat, _const_spec((R, D)), _const_spec((6, D)), _const_spec((6, D)), _const_spec((1, D)), _const_spec((INC, D)),
                  lat] + [ext] * 9,
        out_specs=[lat, pl.BlockSpec((R, INC), lambda i: (i, 0)), pl.BlockSpec((R, D), lambda i: (i, 0)),
                   _acc_spec((1, D)), _acc_spec((6, D)), _acc_spec((6, D))],
        out_shape=[jax.ShapeDtypeStruct((L, D), f32), jax.ShapeDtypeStruct((L + R, INC), bf16),
                   jax.ShapeDtypeStruct((L + R, D), bf16), jax.ShapeDtypeStruct((1, D), f32),
                   jax.ShapeDtypeStruct((6, D), f32), jax.ShapeDtypeStruct((6, D), f32)],
        compiler_params=_CP(dimension_semantics=_ARB),
    )(x, ctx, modx, modc, nw1, w_in_t, dx1, *parts)


def _ret_post(yr, g):
    outs = []
    for h in range(RH):
        yh = yr[:, h * DH:(h + 1) * DH]
        mu = jnp.mean(yh, axis=-1, keepdims=True)
        var = jnp.mean((yh - mu) ** 2, axis=-1, keepdims=True)
        outs.append((yh - mu) * lax.rsqrt(var + EPS))
    return jax.nn.silu(g) * jnp.concatenate(outs, axis=1)


def _mix_fn(ys, u, of, ob, g, x, dvec, bglu, gate1, pz, pm, wglu, wout):
    s = jax.nn.gelu(ys + dvec * u)
    z = dnn(s, wglu) + bglu + pz
    cat = jnp.concatenate([s * jax.nn.sigmoid(z), _ret_post(of + ob, g)], axis=1)
    mix = dnn(cat, wout) + pm
    return x + gate1 * mix, (s, cat)


def _mix_fwd(x, ys, of, ob, p_ext, dvec, bglu, modx, wglu, wout, name, cargo=None):
    L = x.shape[0]
    nb = L // R
    cg = _Cargo(cargo)

    def body(*refs):
        ins, (x1_ref,), _ = cg.split(refs, 11, 1, 0)
        x_ref, ys_ref, of_ref, ob_ref, u_ref, g_ref, d_ref, b_ref, mx_ref, wg_ref, wo_ref = ins
        cg.ride(refs, 11, 1, nb)
        x1_ref[...] = _mix_fn(ys_ref[...], u_ref[...].astype(f32), of_ref[...], ob_ref[...], g_ref[...].astype(f32),
                              x_ref[...], d_ref[...], b_ref[...], mx_ref[2:3], 0.0, 0.0, wg_ref[...], wo_ref[...])[0]

    ext = pl.BlockSpec((R, S5W), lambda i: (i + 1, 0))
    return pl.pallas_call(
        body, name=name, grid=(nb,),
        in_specs=[pl.BlockSpec((R, D), lambda i: (i, 0)), ext, ext, ext, ext, pl.BlockSpec((R, RW), lambda i: (i + 1, 4)),
                  _const_spec((1, S5W)), _const_spec((1, S5W)), _const_spec((6, D)), _const_spec((S5W, S5W)), _const_spec((D, D))]
        + cg.in_specs(),
        out_specs=[pl.BlockSpec((R, D), lambda i: (i, 0))] + cg.in_specs(),
        out_shape=[jax.ShapeDtypeStruct((L, D), f32)] + cg.out_shapes(),
        scratch_shapes=cg.sems(),
        compiler_params=_CP(dimension_semantics=_ARB),
    )(x, ys, of, ob, p_ext, p_ext, dvec, bglu, modx, wglu, wout, *cg.arrays)


def _mix_bwd(x, ys, of, ob, p_ext, dvec, bglu, modx, wglu, wout, dx1, name, cargo=None):
    L = x.shape[0]
    nb = L // R + 1
    cg = _Cargo(cargo)

    def body(*refs):
        ins, outs, _ = cg.split(refs, 12, 11, 0)
        x_ref, ys_ref, of_ref, ob_ref, u_ref, g_ref, d_ref, b_ref, mx_ref, wg_ref, wo_ref, dx1_ref = ins
        dy_ref, dud_ref, do_ref, dg_ref, cat_ref, dmix_ref, s_ref, dz_ref, dd_ref, db_ref, dg1_ref = outs
        cg.ride(refs, 12, 11, nb)
        i = pl.program_id(0)

        @pl.when(i == 0)
        def _():
            for r in outs:
                r[...] = jnp.zeros_like(r)

        @pl.when(i > 0)
        def _():
            fn = lambda ys_, u_, of_, g_, d_, b_, g1_, pz_, pm_: _mix_fn(
                ys_, u_, of_, ob_ref[...], g_, x_ref[...], d_, b_, g1_, pz_, pm_, wg_ref[...], wo_ref[...])
            _, vjp, (s, cat) = jax.vjp(fn, ys_ref[...], u_ref[...].astype(f32), of_ref[...], g_ref[...].astype(f32), d_ref[...],
                                       b_ref[...], mx_ref[2:3], jnp.zeros((R, S5W), f32), jnp.zeros((R, D), f32), has_aux=True)
            dy, dud, do, dg, dd, db, dg1, dz, dmix = vjp(dx1_ref[...])
            dy_ref[...], dud_ref[...], do_ref[...], dg_ref[...] = dy, dud, do, dg
            cat_ref[...], dmix_ref[...] = cat.astype(bf16), dmix.astype(bf16)
            s_ref[...], dz_ref[...] = s.astype(bf16), dz.astype(bf16)
            dd_ref[...] += dd
            db_ref[...] += db
            dg1_ref[...] += dg1

    lat = pl.BlockSpec((R, D), lambda i: (jnp.maximum(i - 1, 0), 0))
    lat5 = pl.BlockSpec((R, S5W), lambda i: (jnp.maximum(i - 1, 0), 0))
    ext = pl.BlockSpec((R, S5W), lambda i: (i, 0))
    eshape = jax.ShapeDtypeStruct((L + R, S5W), f32)
    return pl.pallas_call(
        body, name=name, grid=(nb,),
        in_specs=[lat, ext, ext, ext, ext, pl.BlockSpec((R, RW), lambda i: (i, 4)),
                  _const_spec((1, S5W)), _const_spec((1, S5W)), _const_spec((6, D)), _const_spec((S5W, S5W)), _const_spec((D, D)), lat]
        + cg.in_specs(),
        out_specs=[ext, ext, ext, ext, lat, lat, lat5, lat5, _acc_spec((1, S5W)), _acc_spec((1, S5W)), _acc_spec((1, D))]
        + cg.in_specs(),
        out_shape=[eshape, eshape, eshape, eshape, jax.ShapeDtypeStruct((L, D), bf16), jax.ShapeDtypeStruct((L, D), bf16),
                   jax.ShapeDtypeStruct((L, S5W), bf16), jax.ShapeDtypeStruct((L, S5W), bf16),
                   jax.ShapeDtypeStruct((1, S5W), f32), jax.ShapeDtypeStruct((1, S5W), f32), jax.ShapeDtypeStruct((1, D), f32)]
        + cg.out_shapes(),
        scratch_shapes=cg.sems(),
        compiler_params=_CP(dimension_semantics=_ARB),
    )(x, ys, of, ob, p_ext, p_ext, dvec, bglu, modx, wglu, wout, dx1, *cg.arrays)


def _ffn_tail(gc, a, x1, gate2, fnw, pf, wdown, wdown_t, tgt):
    f = jax.nn.gelu(gc) * a
    ffn = _dnn_const(f, wdown, wdown_t) + pf
    y = _rms(x1 + gate2 * ffn, fnw)
    err = y - tgt
    loss = 0.5 * jnp.sum(jnp.mean(err * err, axis=-1, keepdims=True), axis=0, keepdims=True)
    return loss, f


def _ffn_fwd(x1, tgt, nw2, modx, w_a, w_g, cw, cb, wdown, wdown_t, fnw, name):
    L = x1.shape[0]
    nb = L // RF
    per = RF // HALO

    def body(x_ref, xp_ref, xn_ref, t_ref, nw_ref, mx_ref, wa_ref, wg_ref, cw_ref, cb_ref, wd_ref, wdt_ref, fn_ref,
             dx2_ref, da_ref, dgc_ref, f_ref, dffn_ref, loss_ref, dfn_ref, dg2_ref, dcb_ref, dcw_ref):
        i = pl.program_id(0)

        @pl.when(i == 0)
        def _():
            for r in (loss_ref, dfn_ref, dg2_ref, dcb_ref, dcw_ref):
                r[...] = jnp.zeros_like(r)

        nw, sh, sc, gate2 = nw_ref[...], mx_ref[3:4], mx_ref[4:5], mx_ref[5:6]
        x1b = x_ref[...]
        h2 = _mod(_rms(x1b, nw), sh, sc)
        h2e = jnp.concatenate([_mod(_rms(xp_ref[...], nw), sh, sc), h2, _mod(_rms(xn_ref[...], nw), sh, sc)], axis=0)
        a = dnn(h2, wa_ref[...])
        ge = dnn(h2e, wg_ref[...])
        g = ge[HALO:HALO + RF]
        gp = ge[HALO - 1:HALO] * jnp.where(i > 0, 1.0, 0.0)
        gn = ge[HALO + RF:HALO + RF + 1] * jnp.where(i < nb - 1, 1.0, 0.0)
        row = lax.broadcasted_iota(jnp.int32, (RF, 1), 0)
        g_prev = jnp.where(row == 0, gp, pltpu.roll(g, 1, axis=0))
        g_next = jnp.where(row == RF - 1, gn, pltpu.roll(g, RF - 1, axis=0))
        gc = cb_ref[...] + g_prev * cw_ref[0:1] + g * cw_ref[1:2] + g_next * cw_ref[2:3]
        fn = lambda gc_, a_, x_, g2_, fw_, pf_: _ffn_tail(gc_, a_, x_, g2_, fw_, pf_, wd_ref[...], wdt_ref[...], t_ref[...])
        loss, vjp, f = jax.vjp(fn, gc, a, x1b, gate2, fn_ref[...], jnp.zeros((RF, D), f32), has_aux=True)
        dgc, da, dx2, dg2, dfw, dffn = vjp(jnp.ones((1, 1), f32))
        dx2_ref[...] = dx2
        da_ref[...], dgc_ref[...] = da.astype(bf16), dgc
        f_ref[...], dffn_ref[...] = f.astype(bf16), dffn.astype(bf16)
        loss_ref[...] += jnp.broadcast_to(loss, (1, 128))
        dfn_ref[...] += dfw
        dg2_ref[...] += dg2
        dcb_ref[...] += jnp.sum(dgc, axis=0, keepdims=True)
        dcw_ref[0:1] += jnp.sum(dgc * g_prev, axis=0, keepdims=True)
        dcw_ref[1:2] += jnp.sum(dgc * g, axis=0, keepdims=True)
        dcw_ref[2:3] += jnp.sum(dgc * g_next, axis=0, keepdims=True)

    blk = lambda w: pl.BlockSpec((RF, w), lambda i: (i, 0))
    return pl.pallas_call(
        body, name=name, grid=(nb,),
        in_specs=[blk(D), pl.BlockSpec((HALO, D), lambda i: (jnp.maximum(i * per - 1, 0), 0)),
                  pl.BlockSpec((HALO, D), lambda i: (jnp.minimum((i + 1) * per, L // HALO - 1), 0)), blk(D),
                  _const_spec((1, D)), _const_spec((6, D)), _const_spec((D, DFF)), _const_spec((D, DFF)), _const_spec((3, DFF)),
                  _const_spec((1, DFF)), _const_spec((DFF, D)), _const_spec((D, DFF)), _const_spec((1, D))],
        out_specs=[blk(D), blk(DFF), blk(DFF), blk(DFF), blk(D), _acc_spec((1, 128)), _acc_spec((1, D)), _acc_spec((1, D)),
                   _acc_spec((1, DFF)), _acc_spec((3, DFF))],
        out_shape=[jax.ShapeDtypeStruct((L, D), f32), jax.ShapeDtypeStruct((L, DFF), bf16), jax.ShapeDtypeStruct((L, DFF), f32),
                   jax.ShapeDtypeStruct((L, DFF), bf16), jax.ShapeDtypeStruct((L, D), bf16), jax.ShapeDtypeStruct((1, 128), f32),
                   jax.ShapeDtypeStruct((1, D), f32), jax.ShapeDtypeStruct((1, D), f32), jax.ShapeDtypeStruct((1, DFF), f32),
                   jax.ShapeDtypeStruct((3, DFF), f32)],
        compiler_params=_CP(dimension_semantics=_ARB),
    )(x1, x1, x1, tgt, nw2, modx, w_a, w_g, cw, cb, wdown, wdown_t, fnw)


def _ffn_bwd(x1, dx2, da, dgc, nw2, modx, wup_t, cw, name):
    L = x1.shape[0]
    nb = L // RF
    per = RF // HALO

    def body(x_ref, dx2_ref, da_ref, dgc_ref, dgp_ref, dgn_ref, nw_ref, mx_ref, wu_ref, cw_ref,
             dx1_ref, dag_ref, h2_ref, dnw_ref, dmx_ref):
        i = pl.program_id(0)

        @pl.when(i == 0)
        def _():
            dnw_ref[...] = jnp.zeros_like(dnw_ref)
            dmx_ref[...] = jnp.zeros_like(dmx_ref)

        dgc_b = dgc_ref[...]
        before = dgp_ref[HALO - 1:HALO] * jnp.where(i > 0, 1.0, 0.0)
        after = dgn_ref[0:1] * jnp.where(i < nb - 1, 1.0, 0.0)
        row = lax.broadcasted_iota(jnp.int32, (RF, 1), 0)
        d_prev = jnp.where(row == 0, before, pltpu.roll(dgc_b, 1, axis=0))
        d_next = jnp.where(row == RF - 1, after, pltpu.roll(dgc_b, RF - 1, axis=0))
        dg = cw_ref[0:1] * d_next + cw_ref[1:2] * dgc_b + cw_ref[2:3] * d_prev
        dag = jnp.concatenate([da_ref[...], dg.astype(bf16)], axis=1)
        dag_ref[...] = dag
        dh2 = dnn(dag, wu_ref[...])
        h2, vjp = jax.vjp(lambda a, b, c, d: _mod(_rms(a, b), c, d), x_ref[...], nw_ref[...], mx_ref[3:4], mx_ref[4:5])
        dxa, dnw, dsh, dsc = vjp(dh2)
        h2_ref[...] = h2.astype(bf16)
        dx1_ref[...] = dx2_ref[...] + dxa
        dnw_ref[...] += dnw
        dmx_ref[3:4] += dsh
        dmx_ref[4:5] += dsc

    blk = lambda w: pl.BlockSpec((RF, w), lambda i: (i, 0))
    return pl.pallas_call(
        body, name=name, grid=(nb,),
        in_specs=[blk(D), blk(D), blk(DFF), blk(DFF), pl.BlockSpec((HALO, DFF), lambda i: (jnp.maximum(i * per - 1, 0), 0)),
                  pl.BlockSpec((HALO, DFF), lambda i: (jnp.minimum((i + 1) * per, L // HALO - 1), 0)),
                  _const_spec((1, D)), _const_spec((6, D)), _const_spec((2 * DFF, D)), _const_spec((3, DFF))],
        out_specs=[blk(D), blk(2 * DFF), blk(D), _acc_spec((1, D)), _acc_spec((6, D))],
        out_shape=[jax.ShapeDtypeStruct((L, D), f32), jax.ShapeDtypeStruct((L, 2 * DFF), bf16), jax.ShapeDtypeStruct((L, D), bf16),
                   jax.ShapeDtypeStruct((1, D), f32), jax.ShapeDtypeStruct((6, D), f32)],
        compiler_params=_CP(dimension_semantics=_ARB),
    )(x1, dx2, da, dgc, dgc, dgc, nw2, modx, wup_t, cw)


def _matmul_tn(a, b, name):
    k, m = a.shape
    n = b.shape[1]
    divs = lambda d: [c for c in range(d, 0, -128) if d % c == 0]
    _, tm, tn = min((m * (n // cn) + n * (m // cm), cm, cn) for cm in divs(m) for cn in divs(n) if cm * cn * 4 <= ACC_TILE_BYTES)
    tk = next(c for c in (512, 768, 256, 128) if k % c == 0)
    nk = k // tk

    def body(a_ref, b_ref, o_ref, acc):
        q = pl.program_id(2)

        @pl.when(q == 0)
        def _():
            acc[...] = jnp.zeros_like(acc)

        acc[...] += dtn(a_ref[...], b_ref[...])

        @pl.when(q == nk - 1)
        def _():
            o_ref[...] = acc[...].astype(bf16)

    return pl.pallas_call(
        body, name=name, grid=(m // tm, n // tn, nk),
        in_specs=[pl.BlockSpec((tk, tm), lambda i, j, q: (q, i)), pl.BlockSpec((tk, tn), lambda i, j, q: (q, j))],
        out_specs=pl.BlockSpec((tm, tn), lambda i, j, q: (i, j)),
        out_shape=jax.ShapeDtypeStruct((m, n), bf16),
        scratch_shapes=[pltpu.VMEM((tm, tn), f32)],
        compiler_params=_CP(dimension_semantics=("parallel", "parallel", "arbitrary")),
    )(a, b)


def _adamw(w, g, m, v, name):
    c1, c2 = 1.0 - B1 ** STEP, 1.0 - B2 ** STEP

    def body(w_ref, g_ref, m_ref, v_ref, d_ref, nm_ref, nv_ref):
        gg = g_ref[...]
        nm = B1 * m_ref[...] + (1.0 - B1) * gg
        nv = B2 * v_ref[...] + (1.0 - B2) * jnp.square(gg)
        d_ref[...] = -LR * ((nm / c1) / (jnp.sqrt(nv / c2) + AEPS) + WD * w_ref[...])
        nm_ref[...], nv_ref[...] = nm, nv

    return pl.pallas_call(body, name=name, out_shape=[jax.ShapeDtypeStruct(w.shape, f32)] * 3, compiler_params=_CP())(w, g, m, v)


SMALL = ["conv_w", "c_ctx", "norm1_w", "s5_lambda_re_f", "s5_lambda_im_f", "s5_log_step_f", "s5_lambda_re_b", "s5_lambda_im_b",
         "s5_log_step_b", "s5_b_re", "s5_b_im", "s5_c_re", "s5_c_im", "s5_d", "s5_b_glu", "ret_log_decay_f", "ret_log_decay_b",
         "norm2_w", "conv_b", "final_norm_w"]
WEIGHTS = ["c_ctx", "w_mod", "b_mod", "norm1_w", "w_in", "s5_lambda_re_f", "s5_lambda_im_f", "s5_log_step_f", "s5_lambda_re_b",
           "s5_lambda_im_b", "s5_log_step_b", "s5_b_re", "s5_b_im", "s5_c_re", "s5_c_im", "s5_d", "s5_w_glu", "s5_b_glu",
           "ret_log_decay_f", "ret_log_decay_b", "w_out", "norm2_w", "w_up", "conv_w", "conv_b", "w_down", "final_norm_w"]


def _pack_small(vals):
    flat, offs, o = [], [], 0
    for a in vals:
        n = a.size
        npad = -n % 128
        flat.append(jnp.pad(a.reshape(-1), (0, npad)))
        offs.append((o, n))
        o += n + npad
    tail = -o % 1024
    if tail:
        flat.append(jnp.zeros((tail,), f32))
    return jnp.concatenate(flat).reshape(-1, 128), offs


def _unpack_small(packed, offs, shapes):
    flat = packed.reshape(-1)
    return [flat[o:o + n].reshape(s) for (o, n), s in zip(offs, shapes)]


def _rope_tables(L, nctx_rows):
    t = np.arange(L)
    inv = (ROPE_THETA ** (-np.arange(DH // 4, dtype=np.float64) / (DH // 4))).astype(np.float32)
    ang = np.concatenate([(t // GRID_W).astype(np.float32)[:, None] * inv, (t % GRID_W).astype(np.float32)[:, None] * inv], axis=-1)
    cos = np.repeat(np.cos(ang).astype(np.float32), 2, axis=1)
    sin = np.repeat(np.sin(ang).astype(np.float32), 2, axis=1) * np.tile(np.array([-1.0, 1.0], np.float32), DH // 2)
    cosf = np.concatenate([np.ones((nctx_rows, DH), np.float32), cos], axis=0)
    sins = np.concatenate([np.zeros((nctx_rows, DH), np.float32), sin], axis=0)
    return jnp.asarray(cosf), jnp.asarray(sins)


def kernel(x, c, ctx, c_ctx, w_mod, b_mod, norm1_w, w_in, s5_lambda_re_f, s5_lambda_im_f, s5_log_step_f, s5_lambda_re_b, s5_lambda_im_b, s5_log_step_b, s5_b_re, s5_b_im, s5_c_re, s5_c_im, s5_d, s5_w_glu, s5_b_glu, ret_log_decay_f, ret_log_decay_b, w_out, norm2_w, w_up, conv_w, conv_b, w_down, final_norm_w, loss_target, m_c_ctx, m_w_mod, m_b_mod, m_norm1_w, m_w_in, m_s5_lambda_re_f, m_s5_lambda_im_f, m_s5_log_step_f, m_s5_lambda_re_b, m_s5_lambda_im_b, m_s5_log_step_b, m_s5_b_re, m_s5_b_im, m_s5_c_re, m_s5_c_im, m_s5_d, m_s5_w_glu, m_s5_b_glu, m_ret_log_decay_f, m_ret_log_decay_b, m_w_out, m_norm2_w, m_w_up, m_conv_w, m_conv_b, m_w_down, m_final_norm_w, v_c_ctx, v_w_mod, v_b_mod, v_norm1_w, v_w_in, v_s5_lambda_re_f, v_s5_lambda_im_f, v_s5_log_step_f, v_s5_lambda_re_b, v_s5_lambda_im_b, v_s5_log_step_b, v_s5_b_re, v_s5_b_im, v_s5_c_re, v_s5_c_im, v_s5_d, v_s5_w_glu, v_s5_b_glu, v_ret_log_decay_f, v_ret_log_decay_b, v_w_out, v_norm2_w, v_w_up, v_conv_w, v_conv_b, v_w_down, v_final_norm_w):
    args = dict(locals())
    W = {n: args[n] for n in WEIGHTS}
    M = {n: args["m_" + n] for n in WEIGHTS}
    V = {n: args["v_" + n] for n in WEIGHTS}
    me = _me()
    x2, ctx2, tgt = x[0], ctx[0], loss_target[0]
    L, Lc = x2.shape[0], ctx2.shape[0]
    assert Lc == R and L % R == 0 and L % GRID_W == 0
    nctx = Lc // T

    c_all = _all_gather_small(jnp.pad(c, ((0, 7), (0, 0))), "gather_c")[:, 0, :]
    c9 = jnp.concatenate([c_all, c_ctx[None], jnp.zeros((7, D), f32)], axis=0)
    w_mod_l = w_mod[0]
    ncol = w_mod_l.shape[1]
    m_part = _ada_fwd(c9, w_mod_l, "ada_fwd")
    m_all = _all_gather_small(m_part, "gather_mod").transpose(1, 0, 2).reshape(16, 6, D)
    modx, modc = _mod_select(m_all, b_mod.reshape(6, D), "mod_select")

    w_in_tl, w_up_tl = w_in[0].T.astype(bf16), w_up[0].T.astype(bf16)
    w_out_l, w_down_l, w_glu_l = w_out[0].astype(bf16), w_down[0].astype(bf16), s5_w_glu[0].astype(bf16)
    half_up = w_up_tl.shape[0] // 2
    (w_in_g,) = _exchange([w_in_tl], False, "gather_w_in")
    w_in_t = w_in_g.reshape(INC, D)
    per_cv = conv_w.shape[2]
    conv_pad = jnp.pad(conv_w[0], ((0, 5), (0, 128 * 3 - per_cv)))
    conv_f = _all_gather_small(conv_pad, "gather_conv")[:, :3, :per_cv].transpose(1, 0, 2).reshape(3, DFF)

    pair = lambda a, b: jnp.concatenate([a, b], axis=-1)
    bre_g, bim_g = s5_b_re[0].transpose(0, 2, 1), s5_b_im[0].transpose(0, 2, 1)
    cre_g, cim_g = s5_c_re[0], s5_c_im[0]
    shared = (pair(bre_g, bim_g), pair(bim_g, bre_g), pair(cre_g, cim_g), pair(cim_g, cre_g))
    s5p = {}
    for tag, lre, lim, ls in (("f", s5_lambda_re_f, s5_lambda_im_f, s5_log_step_f), ("b", s5_lambda_re_b, s5_lambda_im_b, s5_log_step_b)):
        s5p[tag] = (pair(lre[0], lre[0])[:, None, :], pair(lim[0], lim[0])[:, None, :], ls[0].reshape(S5G, 1, 1)) + shared
    m_f, mb_f, mc_f, a1_f, a2_f = _s5_build(s5p["f"], jnp.zeros((S5G, TCP, TCP), bf16), False, "s5_build_f")
    m_fb, mb_b, mc_b, a1_b, a2_b = _s5_build(s5p["b"], m_f, True, "s5_build_b")
    mx, mbx_f, mbx_b, mcx_f, mcx_b = _spread_m(m_fb), _spread_b(mb_f), _spread_b(mb_b), _spread_t(mc_f), _spread_t(mc_b)
    a1_f, a2_f, a1_b, a2_b = (a.reshape(S5G, SB) for a in (a1_f, a2_f, a1_b, a2_b))

    nw1, nw2, fnw = norm1_w, norm2_w, final_norm_w[None]
    p_ext = _f1_fwd(x2, ctx2, modx, modc, nw1, w_in_t.T, "f1_fwd")
    nctx5 = Lc // TC
    p3 = p_ext.reshape(-1, TC, INC)
    s_f, s_b = _s5_inc(p3, mbx_f, mbx_b, "s5_inc")
    hp_f, hp_b = _s5_carry(s_f, s_b, (a1_f, a2_f), (a1_b, a2_b), nctx5, "s5_carry")
    ys = _s5_out(p3, mx, hp_f, hp_b, mcx_f, mcx_b, "s5_out").reshape(-1, S5W)
    cosf, sins = _rope_tables(L, Lc)
    ld8 = lambda ld: jnp.pad(jnp.broadcast_to(ld[0][:, None], (RH, 128)), ((0, 8 - RH), (0, 0)))
    ldf8, ldb8 = ld8(ret_log_decay_f), ld8(ret_log_decay_b)
    of, rp_f, w_out_g, w_glu_g, w_up_g1 = _ret_fwd(p_ext, cosf, sins, ldf8, False, nctx, "ret_fwd_f",
                                                   cargo=([w_out_l, w_glu_l, w_up_tl[:half_up]], False))
    ob, rp_b, w_up_g2 = _ret_fwd(p_ext, cosf, sins, ldb8, True, nctx, "ret_fwd_b", cargo=([w_up_tl[half_up:]], False))
    w_out_f, w_glu_f = w_out_g.reshape(D, D), w_glu_g.reshape(S5W, S5W)
    x1, w_down_g = _mix_fwd(x2, ys, of, ob, p_ext, s5_d, s5_b_glu, modx, w_glu_f, w_out_f, "mix_fwd", cargo=([w_down_l], False))
    w_down_f = w_down_g.reshape(DFF, D)
    w_up_t = jnp.concatenate([w_up_g1, w_up_g2], axis=1).reshape(2 * DFF, D)

    (dx2, da, dgc, f_act, dffn, loss_acc, g_fnw, g_gate2, g_cb, g_cw) = _ffn_fwd(
        x1, tgt, nw2, modx, w_up_t[:DFF].T, w_up_t[DFF:].T, conv_f, conv_b, w_down_f, w_down_f.T, fnw, "ffn_fwd")
    dx1, dag, h2, g_nw2, dmx2 = _ffn_bwd(x1, dx2, da, dgc, nw2, modx, w_up_t, conv_f, "ffn_bwd")
    gw_down = _matmul_tn(f_act, dffn, "dw_down").reshape(NDEV, -1, D)
    gw_up_t = _matmul_tn(dag, h2, "dw_up").reshape(NDEV, -1, D)
    (dy_e, dud_e, do_e, dg_e, cat, dmix, s_act, dz, g_d, g_bglu, g_gate1, l_down) = _mix_bwd(
        x2, ys, of, ob, p_ext, s5_d, s5_b_glu, modx, w_glu_f, w_out_f, dx1, "mix_bwd", cargo=([gw_down], True))
    gw_out = _matmul_tn(cat, dmix, "dw_out").reshape(NDEV, -1, D)
    gw_glu = _matmul_tn(s_act, dz, "dw_glu").reshape(NDEV, -1, S5W)
    dq_f, dk_f, dv_f, gld_f, l_up = _ret_bwd(p_ext, cosf, sins, ldf8, rp_f, do_e, False, nctx, "ret_bwd_f", cargo=([gw_up_t], True))
    dq_b, dk_b, dv_b, gld_b, l_out, l_glu = _ret_bwd(p_ext, cosf, sins, ldb8, rp_b, do_e, True, nctx, "ret_bwd_b",
                                                     cargo=([gw_out, gw_glu], True))

    du1, g_x, dhp_f, dhp_b, dmc_f, dmc_b = _s5_out_bwd(dy_e.reshape(-1, TC, S5W), p3, mx, hp_f, hp_b, mcx_f, mcx_b, _fold_matrix(),
                                                       "s5_out_bwd")
    g_m = g_x.reshape(NSG, TC, GBK, S5P, TCP).transpose(0, 2, 1, 3, 4).reshape(S5G, TCP, TCP)
    ds_f, da1_f, da2_f = _s5_carry_bwd(dhp_f, hp_f, a1_f, a2_f, False, nctx5, "s5_carry_bwd_f")
    ds_b, da1_b, da2_b = _s5_carry_bwd(dhp_b, hp_b, a1_b, a2_b, True, nctx5, "s5_carry_bwd_b")
    du3, dmb_f, dmb_b = _s5_inc_bwd(du1, p3, ds_f, ds_b, mbx_f, mbx_b, "s5_inc_bwd")
    zero_p = jnp.zeros((S5G, S5P, SB), f32)
    gf = _s5_build_bwd(s5p["f"], (g_m, dmb_f, dmc_f, da1_f[:, None, :], da2_f[:, None, :]), (zero_p, zero_p), False, "s5_build_bwd_f")
    gb = _s5_build_bwd(s5p["b"], (g_m, dmb_b, dmc_b, da1_b[:, None, :], da2_b[:, None, :]), (gf[3], gf[4]), True, "s5_build_bwd_b")
    g_bre, g_bim = gb[3][:, :, :S5N].transpose(0, 2, 1), gb[3][:, :, S5N:].transpose(0, 2, 1)
    g_cre, g_cim = gb[4][:, :, :S5N], gb[4][:, :, S5N:]

    grad_x, dp_ext, h1, g_nw1, dmx1, dmc1 = _f1_bwd(
        x2, ctx2, modx, modc, nw1, w_in_t, dx1, (du3.reshape(-1, S5W), dud_e, dq_f, dq_b, dk_f, dk_b, dv_f, dv_b, dg_e), "f1_bwd")
    gw_in_t = _matmul_tn(dp_ext, h1, "dw_in").reshape(NDEV, -1, D)
    (l_in,) = _exchange([gw_in_t], True, "scatter_dw_in")

    dmx = dmx1 + dmx2
    dmx = dmx.at[2].set(g_gate1[0]).at[5].set(g_gate2[0])
    dm_me = jnp.stack([dmx.reshape(-1), dmc1.reshape(-1)], axis=0)
    dm_all = _all_gather_small(jnp.pad(dm_me, ((0, 6), (0, 0))), "gather_dmod")
    dmx_all, dmc_all = dm_all[:, 0, :], dm_all[:, 1, :]
    my_cols = lambda a: lax.dynamic_slice(a, (0, me * ncol), (NDEV, ncol))
    gw_mod, g_bmod, dc9 = _ada_bwd(c9, dmx_all, dmc_all, my_cols(dmx_all), my_cols(dmc_all), w_mod_l, "ada_bwd")

    small = {
        "conv_w": g_cw, "c_ctx": dc9[8], "norm1_w": g_nw1, "s5_lambda_re_f": gf[0][:, 0, :S5N], "s5_lambda_im_f": gf[1][:, 0, :S5N],
        "s5_log_step_f": gf[2], "s5_lambda_re_b": gb[0][:, 0, :S5N], "s5_lambda_im_b": gb[1][:, 0, :S5N], "s5_log_step_b": gb[2],
        "s5_b_re": g_bre, "s5_b_im": g_bim, "s5_c_re": g_cre, "s5_c_im": g_cim, "s5_d": g_d, "s5_b_glu": g_bglu,
        "ret_log_decay_f": gld_f[:RH, 0], "ret_log_decay_b": gld_b[:RH, 0], "norm2_w": g_nw2, "conv_b": g_cb, "final_norm_w": g_fnw,
    }
    packed, soffs = _pack_small([small[n].astype(f32) for n in SMALL])
    red = _all_reduce_small(packed, "reduce_small")
    sshapes = [(3, DFF) if n == "conv_w" else W[n].shape for n in SMALL]
    G = dict(zip(SMALL, _unpack_small(red, soffs, sshapes)))
    G["conv_w"] = lax.dynamic_slice(G["conv_w"], (0, me * per_cv), (3, per_cv))[None]
    G["b_mod"] = g_bmod.reshape(b_mod.shape)
    G["w_mod"] = gw_mod[None]
    G["w_in"] = _sum8(l_in, "sum_dw_in").T[None]
    G["w_up"] = _sum8(l_up, "sum_dw_up").T[None]
    G["w_out"] = _sum8(l_out, "sum_dw_out")[None]
    G["w_down"] = _sum8(l_down, "sum_dw_down")[None]
    G["s5_w_glu"] = _sum8(l_glu, "sum_dw_glu")[None]

    delta, new_m, new_v = {}, {}, {}
    sm_names = SMALL[1:] + ["b_mod"]
    pk = lambda d: _pack_small([d[n].astype(f32) for n in sm_names])
    (pw, aoffs), (pg, _), (pm, _), (pv, _) = pk(W), pk(G), pk(M), pk(V)
    pd, pnm, pnv = _adamw(pw, pg, pm, pv, "adamw_small")
    shapes = [W[n].shape for n in sm_names]
    for dst, src in ((delta, pd), (new_m, pnm), (new_v, pnv)):
        dst.update(zip(sm_names, _unpack_small(src, aoffs, shapes)))
    for n in ["w_mod", "w_in", "w_out", "w_up", "w_down", "s5_w_glu", "conv_w"]:
        d, nm, nv = _adamw(W[n][0], G[n][0], M[n][0], V[n][0], "adamw_" + n)
        delta[n], new_m[n], new_v[n] = d[None], nm[None], nv[None]

    loss = lax.psum(loss_acc[0, 0], ("x", "y", "c"))
    return (loss, grad_x[None], *[G[n] for n in WEIGHTS], *[delta[n] for n in WEIGHTS], *[new_m[n] for n in WEIGHTS],
            *[new_v[n] for n in WEIGHTS])
```

```python
import functools

import numpy as np
import jax
import jax.numpy as jnp
from jax import lax
from jax.experimental import pallas as pl
from jax.experimental.pallas import tpu as pltpu

f32, bf16 = jnp.float32, jnp.bfloat16

D = 1024
S5W, S5G, S5P, S5N = 512, 32, 16, 64
TC = 16
TCP = TC * S5P
SB = 2 * S5N
GBK = 8
RH, DH = 4, 128
RW = RH * DH
INC = S5W + 4 * RW
DFF = 2816
T = 128
R = 256
RF = 128
HALO = 8
EPS = 1e-6
ROPE_THETA = 10000.0
GRID_W = 64
NDEV = 8
LR, B1, B2, AEPS, WD, STEP = 0.001, 0.9, 0.999, 1e-08, 0.01, 10
VMEM_LIMIT = 60 * 1024 * 1024
ACC_TILE_BYTES = 6 * 1024 * 1024
MESH = pl.DeviceIdType.MESH

_CP = functools.partial(pltpu.CompilerParams, vmem_limit_bytes=VMEM_LIMIT)
_ARB = ("arbitrary",)
_ANY = pl.BlockSpec(memory_space=pl.ANY)


def _dg(a, b, dims):
    return lax.dot_general(a.astype(bf16), b.astype(bf16), (dims, ((), ())), preferred_element_type=f32)


@jax.custom_vjp
def dnn(a, b):
    return _dg(a, b, ((1,), (0,)))


@jax.custom_vjp
def dnt(a, b):
    return _dg(a, b, ((1,), (1,)))


@jax.custom_vjp
def dtn(a, b):
    return _dg(a, b, ((0,), (0,)))


dnn.defvjp(lambda a, b: (dnn(a, b), (a, b)), lambda r, g: (dnt(g, r[1]).astype(r[0].dtype), dtn(r[0], g).astype(r[1].dtype)))
dnt.defvjp(lambda a, b: (dnt(a, b), (a, b)), lambda r, g: (dnn(g, r[1]).astype(r[0].dtype), dtn(g, r[0]).astype(r[1].dtype)))
dtn.defvjp(lambda a, b: (dtn(a, b), (a, b)), lambda r, g: (dnt(r[1], g).astype(r[0].dtype), dnn(r[0], g).astype(r[1].dtype)))


@jax.custom_vjp
def _dnn_const(a, w, wt):
    return dnn(a, w)


_dnn_const.defvjp(lambda a, w, wt: (dnn(a, w), wt), lambda wt, g: (dnn(g, wt), None, None))


def _rms(t, w):
    return t * lax.rsqrt(jnp.mean(t * t, axis=-1, keepdims=True) + EPS) * w


def _mod(h, shift, scale):
    return h * (1.0 + scale) + shift


def _const_spec(shape):
    n = len(shape)
    return pl.BlockSpec(shape, lambda i, _n=n: (0,) * _n, pipeline_mode=pl.Buffered(1))


def _acc_spec(shape):
    n = len(shape)
    return pl.BlockSpec(shape, lambda i, _n=n: (0,) * _n)


def _me():
    return 4 * lax.axis_index("x") + 2 * lax.axis_index("y") + lax.axis_index("c")


def _peer(r):
    x, y, c = lax.axis_index("x"), lax.axis_index("y"), lax.axis_index("c")
    px = 1 - x if (r >> 2) & 1 else x
    py = 1 - y if (r >> 1) & 1 else y
    pc = 1 - c if r & 1 else c
    return (px, py, pc), 4 * px + 2 * py + pc


def _all_gather_small(v, name):
    r, c = v.shape

    def body(v_ref, out_ref, send_sems, recv_sems):
        me = _me()
        out_ref[me] = v_ref[...]
        sends = []
        for k in range(1, NDEV):
            peer, _ = _peer(k)
            cp = pltpu.make_async_remote_copy(src_ref=v_ref, dst_ref=out_ref.at[me], send_sem=send_sems.at[k - 1],
                                              recv_sem=recv_sems.at[k - 1], device_id=peer, device_id_type=MESH)
            cp.start()
            sends.append(cp)
        for k in range(1, NDEV):
            peer, pidx = _peer(k)
            pltpu.make_async_remote_copy(src_ref=v_ref, dst_ref=out_ref.at[pidx], send_sem=send_sems.at[k - 1],
                                         recv_sem=recv_sems.at[k - 1], device_id=peer, device_id_type=MESH).wait_recv()
        for cp in sends:
            cp.wait_send()

    return pl.pallas_call(
        body, name=name, out_shape=jax.ShapeDtypeStruct((NDEV, r, c), v.dtype),
        in_specs=[pl.BlockSpec(memory_space=pltpu.VMEM)], out_specs=pl.BlockSpec(memory_space=pltpu.VMEM),
        scratch_shapes=[pltpu.SemaphoreType.DMA((NDEV - 1,)), pltpu.SemaphoreType.DMA((NDEV - 1,))],
        compiler_params=_CP(),
    )(v)


def _all_reduce_small(v, name):
    r, c = v.shape

    def body(v_ref, out_ref, land, send_sems, recv_sems):
        me = _me()
        land[me] = v_ref[...]
        sends = []
        for k in range(1, NDEV):
            peer, _ = _peer(k)
            cp = pltpu.make_async_remote_copy(src_ref=v_ref, dst_ref=land.at[me], send_sem=send_sems.at[k - 1],
                                              recv_sem=recv_sems.at[k - 1], device_id=peer, device_id_type=MESH)
            cp.start()
            sends.append(cp)
        for k in range(1, NDEV):
            peer, pidx = _peer(k)
            pltpu.make_async_remote_copy(src_ref=v_ref, dst_ref=land.at[pidx], send_sem=send_sems.at[k - 1],
                                         recv_sem=recv_sems.at[k - 1], device_id=peer, device_id_type=MESH).wait_recv()
        for cp in sends:
            cp.wait_send()
        acc = land[0]
        for j in range(1, NDEV):
            acc = acc + land[j]
        out_ref[...] = acc

    return pl.pallas_call(
        body, name=name, out_shape=jax.ShapeDtypeStruct((r, c), v.dtype),
        in_specs=[pl.BlockSpec(memory_space=pltpu.VMEM)], out_specs=pl.BlockSpec(memory_space=pltpu.VMEM),
        scratch_shapes=[pltpu.VMEM((NDEV, r, c), v.dtype), pltpu.SemaphoreType.DMA((NDEV - 1,)),
                        pltpu.SemaphoreType.DMA((NDEV - 1,))],
        compiler_params=_CP(),
    )(v)


class _Exchange:
    def __init__(self, srcs, dsts, send_sems, recv_sems, local_sems, scatter):
        me = _me()
        n = len(srcs)
        self.sends, self.recvs, self.locals = [], [], []
        for a, (s, d) in enumerate(zip(srcs, dsts)):
            self.locals.append(pltpu.make_async_copy(s.at[me] if scatter else s, d.at[me], local_sems.at[a]))
        for k in range(1, NDEV):
            peer, pidx = _peer(k)
            for a, (s, d) in enumerate(zip(srcs, dsts)):
                src = s.at[pidx] if scatter else s
                sem = (k - 1) * n + a
                for dst, out in ((d.at[me], self.sends), (d.at[pidx], self.recvs)):
                    out.append(pltpu.make_async_remote_copy(src_ref=src, dst_ref=dst, send_sem=send_sems.at[sem],
                                                            recv_sem=recv_sems.at[sem], device_id=peer, device_id_type=MESH))

    def start(self):
        for cp in self.locals + self.sends:
            cp.start()

    def wait(self):
        for cp in self.recvs:
            cp.wait_recv()
        for cp in self.sends:
            cp.wait_send()
        for cp in self.locals:
            cp.wait()


def _exchange_shapes(arrays, scatter):
    return [jax.ShapeDtypeStruct(a.shape if scatter else (NDEV,) + a.shape, a.dtype) for a in arrays]


def _exchange_sems(n):
    return [pltpu.SemaphoreType.DMA(((NDEV - 1) * n,)), pltpu.SemaphoreType.DMA(((NDEV - 1) * n,)), pltpu.SemaphoreType.DMA((n,))]


def _exchange(arrays, scatter, name):
    n = len(arrays)

    def body(*refs):
        ex = _Exchange(refs[:n], refs[n:2 * n], *refs[2 * n:], scatter)
        ex.start()
        ex.wait()

    return pl.pallas_call(body, name=name, out_shape=_exchange_shapes(arrays, scatter), in_specs=[_ANY] * n,
                          out_specs=[_ANY] * n, scratch_shapes=_exchange_sems(n), compiler_params=_CP())(*arrays)


class _Cargo:
    def __init__(self, cargo):
        self.arrays, self.scatter = cargo if cargo else ([], False)
        self.n = len(self.arrays)

    def in_specs(self):
        return [_ANY] * self.n

    def out_shapes(self):
        return _exchange_shapes(self.arrays, self.scatter)

    def sems(self):
        return _exchange_sems(self.n) if self.n else []

    def split(self, refs, n_in, n_out, n_scratch):
        n = self.n
        return refs[:n_in], refs[n_in + n:n_in + n + n_out], refs[n_in + 2 * n + n_out:n_in + 2 * n + n_out + n_scratch]

    def ride(self, refs, n_in, n_out, nsteps):
        if not self.n:
            return
        n = self.n
        ex = _Exchange(refs[n_in:n_in + n], refs[n_in + n + n_out:n_in + 2 * n + n_out], *refs[-3:], self.scatter)

        @pl.when(pl.program_id(0) == 0)
        def _():
            ex.start()

        @pl.when(pl.program_id(0) == nsteps - 1)
        def _():
            ex.wait()


def _sum8(land, name):
    _, r, c = land.shape
    rb = next((b for b in (256, 64, 32) if r % b == 0), r)

    def body(l_ref, o_ref):
        acc = l_ref[0].astype(f32)
        for j in range(1, NDEV):
            acc = acc + l_ref[j].astype(f32)
        o_ref[...] = acc

    return pl.pallas_call(
        body, name=name, grid=(r // rb,), out_shape=jax.ShapeDtypeStruct((r, c), f32),
        in_specs=[pl.BlockSpec((NDEV, rb, c), lambda i: (0, i, 0))], out_specs=pl.BlockSpec((rb, c), lambda i: (i, 0)),
        compiler_params=_CP(dimension_semantics=("parallel",)),
    )(land)


def _ada_fwd(c9, w_mod_l, name):
    def body(c_ref, w_ref, o_ref):
        o_ref[...] = dnn(jax.nn.silu(c_ref[...]), w_ref[...])

    return pl.pallas_call(body, name=name, out_shape=jax.ShapeDtypeStruct((16, w_mod_l.shape[1]), f32),
                          compiler_params=_CP())(c9, w_mod_l)


def _mod_select(m_all, b_mod6, name):
    def body(m_ref, b_ref, mx_ref, mc_ref):
        me = _me()
        mx_ref[...] = m_ref[me] + b_ref[...]
        mc_ref[...] = m_ref[8] + b_ref[...]

    return pl.pallas_call(body, name=name, out_shape=[jax.ShapeDtypeStruct((6, D), f32)] * 2, compiler_params=_CP())(m_all, b_mod6)


def _ada_bwd(c9, dmx_all, dmc_all, dmx_l, dmc_l, w_mod_l, name):
    ncol = w_mod_l.shape[1]

    def rowsum(r):
        acc = r[0:1]
        for j in range(1, NDEV):
            acc = acc + r[j:j + 1]
        return acc

    def body(c_ref, xa_ref, ca_ref, xl_ref, cl_ref, w_ref, gw_ref, gb_ref, dc_ref):
        s9, vjp = jax.vjp(jax.nn.silu, c_ref[...])
        dm9 = jnp.concatenate([xl_ref[...], rowsum(cl_ref[...]), jnp.zeros((7, ncol), f32)], axis=0)
        gw_ref[...] = dtn(s9, dm9)
        gb_ref[...] = rowsum(xa_ref[...]) + rowsum(ca_ref[...])
        dc_ref[...] = vjp(dnt(dm9, w_ref[...]))[0]

    return pl.pallas_call(
        body, name=name,
        out_shape=[jax.ShapeDtypeStruct((D, ncol), f32), jax.ShapeDtypeStruct((1, 6 * D), f32), jax.ShapeDtypeStruct((16, D), f32)],
        compiler_params=_CP())(c9, dmx_all, dmc_all, dmx_l, dmc_l, w_mod_l)


def _lane_sign(rank):
    shape = (1,) * (rank - 1) + (SB,)
    return jnp.where(lax.broadcasted_iota(jnp.int32, shape, rank - 1) < S5N, -1.0, 1.0)


def _s5_build_fn(lre2, lim2, ls, bn, bs, cn, cs, rev):
    sg = _lane_sign(3)
    s = jnp.exp(ls)
    ar, ai = lre2 * s, lim2 * s
    e = jnp.exp(ar)
    nr, ni = e * jnp.cos(ai) - 1.0, e * jnp.sin(ai)
    den = lre2 * lre2 + lim2 * lim2
    cr, ci = (nr * lre2 + ni * lim2) / den, (ni * lre2 - nr * lim2) / den
    bbn = cr * bn + (ci * sg) * bs
    bbs = cr * bs - (ci * sg) * bn

    def powers(ex):
        m, ang = jnp.exp(ex * ar), ex * ai
        return m * jnp.cos(ang), m * jnp.sin(ang) * sg

    def times(tabs, xn, xs):
        f1, f2 = tabs
        return f1[:, :, None, :] * xn[:, None, :, :] + f2[:, :, None, :] * xs[:, None, :, :]

    t = lax.broadcasted_iota(jnp.int32, (1, TC, 1), 1).astype(f32)
    if rev:
        e_src, e_dst, e_out, e_in = t - (TC - 1.0), (TC - 1.0) - t, t, TC - t
    else:
        e_src, e_dst, e_out, e_in = -t, t, (TC - 1.0) - t, t + 1.0
    g = lre2.shape[0]
    flat = lambda a: a.reshape(g, TCP, SB)
    conj = -_lane_sign(4)
    ll = flat(times(powers(e_src), bbn, bbs))
    rr = flat(times(powers(e_dst), cn, cs) * conj)
    mb = flat(times(powers(e_out), bbn, bbs))
    mct = flat(times(powers(e_in), cn, cs) * conj)
    a1, a2 = powers(float(TC))
    row = lax.broadcasted_iota(jnp.int32, (TCP, TCP), 0) // S5P
    col = lax.broadcasted_iota(jnp.int32, (TCP, TCP), 1) // S5P
    mask = jnp.where((col <= row) if rev else (col >= row), 1.0, 0.0)
    m = jnp.concatenate([dnt(ll[j], rr[j])[None] for j in range(g)], axis=0) * mask
    return m, mb, mct, a1, a2


def _gspec(*tail):
    nt = len(tail)
    return pl.BlockSpec((GBK,) + tail, lambda i, _n=nt: (i,) + (0,) * _n)


def _s5_build(params, rev, name):
    def body(l1, l2, ls, bn, bs, cn, cs, m_ref, mb_ref, mc_ref, a1_ref, a2_ref):
        m, mb, mct, a1, a2 = _s5_build_fn(l1[...], l2[...], ls[...], bn[...], bs[...], cn[...], cs[...], rev)
        m_ref[...], mb_ref[...], mc_ref[...] = m.astype(bf16), mb.astype(bf16), mct.astype(bf16)
        a1_ref[...], a2_ref[...] = a1, a2

    vec, pm = _gspec(1, SB), _gspec(S5P, SB)
    return pl.pallas_call(
        body, name=name, grid=(S5G // GBK,),
        in_specs=[vec, vec, _gspec(1, 1), pm, pm, pm, pm],
        out_specs=[_gspec(TCP, TCP), _gspec(TCP, SB), _gspec(TCP, SB), vec, vec],
        out_shape=[jax.ShapeDtypeStruct((S5G, TCP, TCP), bf16), jax.ShapeDtypeStruct((S5G, TCP, SB), bf16),
                   jax.ShapeDtypeStruct((S5G, TCP, SB), bf16), jax.ShapeDtypeStruct((S5G, 1, SB), f32),
                   jax.ShapeDtypeStruct((S5G, 1, SB), f32)],
        compiler_params=_CP(dimension_semantics=("parallel",)),
    )(*params)


def _s5_build_bwd(params, cots, prev, rev, name):
    def body(l1, l2, ls, bn, bs, cn, cs, dm, dmb, dmc, da1, da2, pb, pc, gl1, gl2, gls, gb, gc):
        prim = (l1[...], l2[...], ls[...], bn[...], bs[...], cn[...], cs[...])
        _, vjp = jax.vjp(functools.partial(_s5_build_fn, rev=rev), *prim)
        d1, d2, dls, dbn, dbs, dcn, dcs = vjp((dm[...], dmb[...], dmc[...], da1[...], da2[...]))
        gl1[...] = d1 + pltpu.roll(d1, S5N, axis=2)
        gl2[...] = d2 + pltpu.roll(d2, S5N, axis=2)
        gls[...] = dls
        gb[...] = dbn + pltpu.roll(dbs, S5N, axis=2) + pb[...]
        gc[...] = dcn + pltpu.roll(dcs, S5N, axis=2) + pc[...]

    vec, pm, big = _gspec(1, SB), _gspec(S5P, SB), _gspec(TCP, SB)
    return pl.pallas_call(
        body, name=name, grid=(S5G // GBK,),
        in_specs=[vec, vec, _gspec(1, 1), pm, pm, pm, pm, _gspec(TCP, TCP), big, big, vec, vec, pm, pm],
        out_specs=[vec, vec, _gspec(1, 1), pm, pm],
        out_shape=[jax.ShapeDtypeStruct((S5G, 1, SB), f32), jax.ShapeDtypeStruct((S5G, 1, SB), f32),
                   jax.ShapeDtypeStruct((S5G, 1, 1), f32), jax.ShapeDtypeStruct((S5G, S5P, SB), f32),
                   jax.ShapeDtypeStruct((S5G, S5P, SB), f32)],
        compiler_params=_CP(dimension_semantics=("parallel",)),
    )(*params, *cots, *prev)


def _s5_inc(u, mb_f, mb_b, name):
    nc = u.shape[1]

    def body(u_ref, mf_ref, mb_ref, sf_ref, sb_ref):
        for j in range(GBK):
            sf_ref[:, j, :] = jnp.dot(u_ref[j], mf_ref[j], preferred_element_type=f32)
            sb_ref[:, j, :] = jnp.dot(u_ref[j], mb_ref[j], preferred_element_type=f32)

    sspec = pl.BlockSpec((nc, GBK, SB), lambda i: (0, i, 0))
    return pl.pallas_call(
        body, name=name, grid=(S5G // GBK,), in_specs=[_gspec(nc, TCP), _gspec(TCP, SB), _gspec(TCP, SB)],
        out_specs=[sspec, sspec], out_shape=[jax.ShapeDtypeStruct((nc, S5G, SB), f32)] * 2,
        compiler_params=_CP(dimension_semantics=("parallel",)),
    )(u, mb_f, mb_b)


def _idx_fwd(nctx, nch):
    return lambda i: i


def _idx_rev(nctx, nch):
    return lambda i: jnp.where(i < nctx, nctx - 1 - i, nch + nctx - 1 - i)


def _s5_carry(s_f, s_b, a_f, a_b, nctx, name):
    nc = s_f.shape[0]
    idx_b = _idx_rev(nctx, nc)

    def body(sf_ref, sb_ref, f1_ref, f2_ref, b1_ref, b2_ref, hf_ref, hb_ref):
        f1, f2, b1, b2 = f1_ref[...], f2_ref[...], b1_ref[...], b2_ref[...]

        def step(i, c):
            hf, hfs, hb, hbs = c
            rb = idx_b(i)
            hf_ref[i] = hf
            hb_ref[rb] = hb
            sf, sb = sf_ref[i], sb_ref[rb]
            return (f1 * hf + f2 * hfs + sf, f1 * hfs - f2 * hf + pltpu.roll(sf, S5N, axis=1),
                    b1 * hb + b2 * hbs + sb, b1 * hbs - b2 * hb + pltpu.roll(sb, S5N, axis=1))

        z = jnp.zeros((S5G, SB), f32)
        lax.fori_loop(0, nc, step, (z, z, z, z))

    return pl.pallas_call(body, name=name, out_shape=[jax.ShapeDtypeStruct(s_f.shape, f32)] * 2,
                          compiler_params=_CP())(s_f, s_b, *a_f, *a_b)


def _s5_carry_bwd(dhp, hp, a1, a2, rev, nctx, name):
    nc = hp.shape[0]
    idx = (_idx_rev if rev else _idx_fwd)(nctx, nc)

    def body(dhp_ref, hp_ref, a1_ref, a2_ref, ds_ref, d1_ref, d2_ref):
        f1, f2 = a1_ref[...], a2_ref[...]

        def step(k, carry):
            ab, abs_, d1, d2 = carry
            r = idx(nc - 1 - k)
            ds_ref[r] = ab
            h, dh = hp_ref[r], dhp_ref[r]
            return (dh + f1 * ab - f2 * abs_, pltpu.roll(dh, S5N, axis=1) + f1 * abs_ + f2 * ab,
                    d1 + ab * h, d2 + ab * pltpu.roll(h, S5N, axis=1))

        z = jnp.zeros((S5G, SB), f32)
        _, _, d1, d2 = lax.fori_loop(0, nc, step, (z, z, z, z))
        d1_ref[...], d2_ref[...] = d1, d2

    return pl.pallas_call(
        body, name=name,
        out_shape=[jax.ShapeDtypeStruct(hp.shape, f32), jax.ShapeDtypeStruct((S5G, SB), f32), jax.ShapeDtypeStruct((S5G, SB), f32)],
        compiler_params=_CP())(dhp, hp, a1, a2)


def _s5_out(u, m_f, m_b, hp_f, hp_b, mc_f, mc_b, name):
    nc = u.shape[1]

    def body(u_ref, mf_ref, mb_ref, hf_ref, hb_ref, cf_ref, cb_ref, y_ref):
        for j in range(GBK):
            uj = u_ref[j]
            y_ref[j] = (jnp.dot(uj, mf_ref[j], preferred_element_type=f32) + jnp.dot(uj, mb_ref[j], preferred_element_type=f32)
                        + dnt(hf_ref[:, j, :], cf_ref[j]) + dnt(hb_ref[:, j, :], cb_ref[j])).astype(bf16)

    sspec = pl.BlockSpec((nc, GBK, SB), lambda i: (0, i, 0))
    return pl.pallas_call(
        body, name=name, grid=(S5G // GBK,),
        in_specs=[_gspec(nc, TCP), _gspec(TCP, TCP), _gspec(TCP, TCP), sspec, sspec, _gspec(TCP, SB), _gspec(TCP, SB)],
        out_specs=_gspec(nc, TCP), out_shape=jax.ShapeDtypeStruct((S5G, nc, TCP), bf16),
        compiler_params=_CP(dimension_semantics=("parallel",)),
    )(u, m_f, m_b, hp_f, hp_b, mc_f, mc_b)


def _s5_out_bwd(dy, u, m_f, m_b, hp_f, hp_b, mc_f, mc_b, name):
    nc = u.shape[1]

    def body(dy_ref, u_ref, mf_ref, mb_ref, hf_ref, hb_ref, cf_ref, cb_ref, du_ref, g_ref, dhf_ref, dhb_ref, dcf_ref, dcb_ref):
        for j in range(GBK):
            dyj = dy_ref[j]
            du_ref[j] = dnt(dyj, mf_ref[j]) + dnt(dyj, mb_ref[j])
            g_ref[j] = dtn(u_ref[j], dyj)
            dhf_ref[:, j, :] = dnn(dyj, cf_ref[j])
            dhb_ref[:, j, :] = dnn(dyj, cb_ref[j])
            dcf_ref[j] = dtn(dyj, hf_ref[:, j, :])
            dcb_ref[j] = dtn(dyj, hb_ref[:, j, :])

    sspec = pl.BlockSpec((nc, GBK, SB), lambda i: (0, i, 0))
    sshape = jax.ShapeDtypeStruct((nc, S5G, SB), f32)
    cshape = jax.ShapeDtypeStruct((S5G, TCP, SB), f32)
    return pl.pallas_call(
        body, name=name, grid=(S5G // GBK,),
        in_specs=[_gspec(nc, TCP), _gspec(nc, TCP), _gspec(TCP, TCP), _gspec(TCP, TCP), sspec, sspec, _gspec(TCP, SB), _gspec(TCP, SB)],
        out_specs=[_gspec(nc, TCP), _gspec(TCP, TCP), sspec, sspec, _gspec(TCP, SB), _gspec(TCP, SB)],
        out_shape=[jax.ShapeDtypeStruct((S5G, nc, TCP), f32), jax.ShapeDtypeStruct((S5G, TCP, TCP), f32), sshape, sshape, cshape, cshape],
        compiler_params=_CP(dimension_semantics=("parallel",)),
    )(dy, u, m_f, m_b, hp_f, hp_b, mc_f, mc_b)


def _s5_inc_bwd(du1, u, ds_f, ds_b, mb_f, mb_b, name):
    nc = u.shape[1]

    def body(du1_ref, u_ref, dsf_ref, dsb_ref, mf_ref, mb_ref, du_ref, dmf_ref, dmb_ref):
        for j in range(GBK):
            dsf, dsb = dsf_ref[:, j, :], dsb_ref[:, j, :]
            du_ref[j] = (du1_ref[j] + dnt(dsf, mf_ref[j]) + dnt(dsb, mb_ref[j])).astype(bf16)
            dmf_ref[j] = dtn(u_ref[j], dsf)
            dmb_ref[j] = dtn(u_ref[j], dsb)

    sspec = pl.BlockSpec((nc, GBK, SB), lambda i: (0, i, 0))
    cshape = jax.ShapeDtypeStruct((S5G, TCP, SB), f32)
    return pl.pallas_call(
        body, name=name, grid=(S5G // GBK,),
        in_specs=[_gspec(nc, TCP), _gspec(nc, TCP), sspec, sspec, _gspec(TCP, SB), _gspec(TCP, SB)],
        out_specs=[_gspec(nc, TCP), _gspec(TCP, SB), _gspec(TCP, SB)],
        out_shape=[jax.ShapeDtypeStruct((S5G, nc, TCP), bf16), cshape, cshape],
        compiler_params=_CP(dimension_semantics=("parallel",)),
    )(du1, u, ds_f, ds_b, mb_f, mb_b)


def _to_groups(a):
    n = a.shape[0]
    return a.reshape(n // TC, TC, S5G, S5P).transpose(2, 0, 1, 3).reshape(S5G, n // TC, TCP)


def _from_groups(a):
    nc = a.shape[1]
    return a.reshape(S5G, nc, TC, S5P).transpose(1, 2, 0, 3).reshape(nc * TC, S5W)


def _swap_pairs(t):
    lane = lax.broadcasted_iota(jnp.int32, t.shape, 1)
    return jnp.where(lane % 2 == 0, pltpu.roll(t, DH - 1, axis=1), pltpu.roll(t, 1, axis=1))


def _rot(t, cosf, sins):
    return t * cosf + _swap_pairs(t) * sins


def _rot_t(d, cosf, sins):
    return d * cosf - _swap_pairs(d) * sins


def _ret_chunk(qr, kr, v, rp, ld, rev):
    pos = lax.broadcasted_iota(jnp.int32, (T, 1), 0).astype(f32)
    diff = pos - lax.broadcasted_iota(jnp.int32, (1, T), 1).astype(f32)
    if rev:
        keep, dist = diff < 0, jnp.maximum(-diff, 0.0)
        xi, zeta = jnp.exp(ld * (T - pos)), jnp.exp(ld * pos)
    else:
        keep, dist = diff >= 0, jnp.maximum(diff, 0.0)
        xi, zeta = jnp.exp(ld * (pos + 1.0)), jnp.exp(ld * (T - 1.0 - pos))
    dm = jnp.where(keep, jnp.exp(ld * dist), 0.0)
    out = dnn(dnt(qr, kr) * dm, v) + dnn(qr * xi, rp)
    rn = jnp.exp(ld * float(T)) * rp + dtn(kr * zeta, v)
    return out, rn


def _ret_fwd(p_ext, ld8, rev, nctx, name, cargo=None):
    n = p_ext.shape[0]
    nch = n // T
    idx = (_idx_rev if rev else _idx_fwd)(nctx, nch)
    cg = _Cargo(cargo)

    def body(*refs):
        (q_ref, k_ref, v_ref, ld_ref), (o_ref, rp_ref), (r_s,) = cg.split(refs, 4, 2, 1)
        cg.ride(refs, 4, 2, nch)

        @pl.when(pl.program_id(0) == 0)
        def _():
            r_s[...] = jnp.zeros_like(r_s)

        for h in range(RH):
            sl = slice(h * DH, (h + 1) * DH)
            rp = r_s[h]
            rp_ref[0, h] = rp
            out, rn = _ret_chunk(q_ref[:, sl].astype(f32), k_ref[:, sl].astype(f32), v_ref[:, sl].astype(f32), rp,
                                 ld_ref[h:h + 1, 0:1], rev)
            r_s[h] = rn
            o_ref[:, sl] = out

    def colspec(cb):
        return pl.BlockSpec((T, RW), lambda i, _c=cb: (idx(i), _c))

    return pl.pallas_call(
        body, name=name, grid=(nch,),
        in_specs=[colspec(1), colspec(2), colspec(3), _const_spec((8, 128))] + cg.in_specs(),
        out_specs=[pl.BlockSpec((T, RW), lambda i: (idx(i), 0)), pl.BlockSpec((1, RH, DH, DH), lambda i: (i, 0, 0, 0))] + cg.in_specs(),
        out_shape=[jax.ShapeDtypeStruct((n, RW), f32), jax.ShapeDtypeStruct((nch, RH, DH, DH), f32)] + cg.out_shapes(),
        scratch_shapes=[pltpu.VMEM((RH, DH, DH), f32)] + cg.sems(),
        compiler_params=_CP(dimension_semantics=_ARB),
    )(p_ext, p_ext, p_ext, ld8, *cg.arrays)


def _ret_bwd(p_ext, ld8, rprev, do_ext, rev, nctx, name, cargo=None):
    n = p_ext.shape[0]
    nch = n // T
    idx0 = (_idx_rev if rev else _idx_fwd)(nctx, nch)
    idx = lambda j: idx0(nch - 1 - j)
    cg = _Cargo(cargo)

    def body(*refs):
        ins, (dq_ref, dk_ref, dv_ref, dld_ref), (dr_s,) = cg.split(refs, 6, 4, 1)
        q_ref, k_ref, v_ref, ld_ref, rp_ref, do_ref = ins
        cg.ride(refs, 6, 4, nch)

        @pl.when(pl.program_id(0) == 0)
        def _():
            dr_s[...] = jnp.zeros_like(dr_s)
            dld_ref[...] = jnp.zeros_like(dld_ref)

        for h in range(RH):
            sl = slice(h * DH, (h + 1) * DH)
            _, vjp = jax.vjp(functools.partial(_ret_chunk, rev=rev), q_ref[:, sl].astype(f32), k_ref[:, sl].astype(f32),
                             v_ref[:, sl].astype(f32), rp_ref[0, h], ld_ref[h:h + 1, 0:1])
            dqr, dkr, dv, drp, dld = vjp((do_ref[:, sl], dr_s[h]))
            dr_s[h] = drp
            dq_ref[:, sl], dk_ref[:, sl], dv_ref[:, sl] = dqr, dkr, dv
            dld_ref[h:h + 1, :] += jnp.broadcast_to(dld, (1, 128))

    def colspec(cb):
        return pl.BlockSpec((T, RW), lambda j, _c=cb: (idx(j), _c))

    ospec = pl.BlockSpec((T, RW), lambda j: (idx(j), 0))
    oshape = jax.ShapeDtypeStruct((n, RW), f32)
    return pl.pallas_call(
        body, name=name, grid=(nch,),
        in_specs=[colspec(1), colspec(2), colspec(3), _const_spec((8, 128)),
                  pl.BlockSpec((1, RH, DH, DH), lambda j: (nch - 1 - j, 0, 0, 0)), ospec] + cg.in_specs(),
        out_specs=[ospec, ospec, ospec, _acc_spec((8, 128))] + cg.in_specs(),
        out_shape=[oshape, oshape, oshape, jax.ShapeDtypeStruct((8, 128), f32)] + cg.out_shapes(),
        scratch_shapes=[pltpu.VMEM((RH, DH, DH), f32)] + cg.sems(),
        compiler_params=_CP(dimension_semantics=_ARB),
    )(p_ext, p_ext, p_ext, ld8, rprev, do_ext, *cg.arrays)


def _qk_heads(p, fn_q, fn_k):
    heads = lambda base, fn: [fn(p[:, base + h * DH:base + (h + 1) * DH]) for h in range(RH)]
    return jnp.concatenate([p[:, :S5W]] + heads(S5W, fn_q) + heads(S5W + RW, fn_k) + [p[:, S5W + 2 * RW:]], axis=1)


def _f1_fwd(x, ctx, modx, modc, nw1, w_in_n, cosf, sins, name):
    L = x.shape[0]
    nb = L // R + 1
    scale = DH ** -0.5

    def body(x_ref, c_ref, mx_ref, mc_ref, nw_ref, w_ref, cos_ref, sin_ref, p_ref):
        is_ctx = pl.program_id(0) == 0
        xin = jnp.where(is_ctx, c_ref[...], x_ref[...])
        sh = jnp.where(is_ctx, mc_ref[0:1], mx_ref[0:1])
        sc = jnp.where(is_ctx, mc_ref[1:2], mx_ref[1:2])
        cf, ss = cos_ref[...], sin_ref[...]
        p = dnn(_mod(_rms(xin, nw_ref[...]), sh, sc), w_ref[...])
        p_ref[...] = _qk_heads(p, lambda t: _rot(t, cf, ss), lambda t: _rot(t * scale, cf, ss)).astype(bf16)

    return pl.pallas_call(
        body, name=name, grid=(nb,),
        in_specs=[pl.BlockSpec((R, D), lambda i: (jnp.maximum(i - 1, 0), 0)), _const_spec((R, D)), _const_spec((6, D)),
                  _const_spec((6, D)), _const_spec((1, D)), _const_spec((D, INC)), pl.BlockSpec((R, DH), lambda i: (i, 0)),
                  pl.BlockSpec((R, DH), lambda i: (i, 0))],
        out_specs=pl.BlockSpec((R, INC), lambda i: (i, 0)),
        out_shape=jax.ShapeDtypeStruct((L + R, INC), bf16),
        compiler_params=_CP(dimension_semantics=("parallel",)),
    )(x, ctx, modx, modc, nw1, w_in_n, cosf, sins)


def _f1_bwd(x, ctx, modx, modc, nw1, w_in_t, cosf, sins, dx1, parts, name):
    L = x.shape[0]
    nb = L // R + 1
    scale = DH ** -0.5

    def body(x_ref, c_ref, mx_ref, mc_ref, nw_ref, w_ref, cos_ref, sin_ref, dx1_ref, du0, du1, dq0, dq1, dk0, dk1, dv0, dv1, dg0,
             gx_ref, dp_ref, h1_ref, dnw_ref, dmx_ref, dmc_ref):
        i = pl.program_id(0)
        is_ctx = i == 0

        @pl.when(is_ctx)
        def _():
            dnw_ref[...] = jnp.zeros_like(dnw_ref)
            dmx_ref[...] = jnp.zeros_like(dmx_ref)
            dmc_ref[...] = jnp.zeros_like(dmc_ref)

        cf, ss = cos_ref[...], sin_ref[...]
        dp = jnp.concatenate([du0[...].astype(f32) + du1[...], dq0[...] + dq1[...], dk0[...] + dk1[...], dv0[...] + dv1[...],
                              dg0[...]], axis=1)
        dp = _qk_heads(dp, lambda t: _rot_t(t, cf, ss), lambda t: _rot_t(t, cf, ss) * scale).astype(bf16)
        dp_ref[...] = dp
        xin = jnp.where(is_ctx, c_ref[...], x_ref[...])
        sh = jnp.where(is_ctx, mc_ref[0:1], mx_ref[0:1])
        sc = jnp.where(is_ctx, mc_ref[1:2], mx_ref[1:2])
        dh = dnn(dp, w_ref[...])
        h, vjp = jax.vjp(lambda a, b, c, d: _mod(_rms(a, b), c, d), xin, nw_ref[...], sh, sc)
        dxin, dnw, dsh, dsc = vjp(dh)
        h1_ref[...] = h.astype(bf16)
        gx_ref[...] = dx1_ref[...] + dxin
        dnw_ref[...] += dnw
        wx = jnp.where(is_ctx, 0.0, 1.0)
        dmx_ref[0:1] += dsh * wx
        dmx_ref[1:2] += dsc * wx
        dmc_ref[0:1] += dsh * (1.0 - wx)
        dmc_ref[1:2] += dsc * (1.0 - wx)

    lat = pl.BlockSpec((R, D), lambda i: (jnp.maximum(i - 1, 0), 0))
    ext = pl.BlockSpec((R, S5W), lambda i: (i, 0))
    return pl.pallas_call(
        body, name=name, grid=(nb,),
        in_specs=[lat, _const_spec((R, D)), _const_spec((6, D)), _const_spec((6, D)), _const_spec((1, D)), _const_spec((INC, D)),
                  pl.BlockSpec((R, DH), lambda i: (i, 0)), pl.BlockSpec((R, DH), lambda i: (i, 0)), lat] + [ext] * 9,
        out_specs=[lat, pl.BlockSpec((R, INC), lambda i: (i, 0)), pl.BlockSpec((R, D), lambda i: (i, 0)),
                   _acc_spec((1, D)), _acc_spec((6, D)), _acc_spec((6, D))],
        out_shape=[jax.ShapeDtypeStruct((L, D), f32), jax.ShapeDtypeStruct((L + R, INC), bf16),
                   jax.ShapeDtypeStruct((L + R, D), bf16), jax.ShapeDtypeStruct((1, D), f32),
                   jax.ShapeDtypeStruct((6, D), f32), jax.ShapeDtypeStruct((6, D), f32)],
        compiler_params=_CP(dimension_semantics=_ARB),
    )(x, ctx, modx, modc, nw1, w_in_t, cosf, sins, dx1, *parts)


def _ret_post(yr, g):
    outs = []
    for h in range(RH):
        yh = yr[:, h * DH:(h + 1) * DH]
        mu = jnp.mean(yh, axis=-1, keepdims=True)
        var = jnp.mean((yh - mu) ** 2, axis=-1, keepdims=True)
        outs.append((yh - mu) * lax.rsqrt(var + EPS))
    return jax.nn.silu(g) * jnp.concatenate(outs, axis=1)


def _mix_fn(ys, u, of, ob, g, x, dvec, bglu, gate1, pz, pm, wglu, wout):
    s = jax.nn.gelu(ys + dvec * u)
    z = dnn(s, wglu) + bglu + pz
    cat = jnp.concatenate([s * jax.nn.sigmoid(z), _ret_post(of + ob, g)], axis=1)
    mix = dnn(cat, wout) + pm
    return x + gate1 * mix, (s, cat)


def _mix_fwd(x, ys, of, ob, p_ext, dvec, bglu, modx, wglu, wout, name, cargo=None):
    L = x.shape[0]
    nb = L // R
    cg = _Cargo(cargo)

    def body(*refs):
        ins, (x1_ref,), _ = cg.split(refs, 11, 1, 0)
        x_ref, ys_ref, of_ref, ob_ref, u_ref, g_ref, d_ref, b_ref, mx_ref, wg_ref, wo_ref = ins
        cg.ride(refs, 11, 1, nb)
        x1_ref[...] = _mix_fn(ys_ref[...].astype(f32), u_ref[...].astype(f32), of_ref[...], ob_ref[...], g_ref[...].astype(f32),
                              x_ref[...], d_ref[...], b_ref[...], mx_ref[2:3], 0.0, 0.0, wg_ref[...], wo_ref[...])[0]

    ext = pl.BlockSpec((R, S5W), lambda i: (i + 1, 0))
    return pl.pallas_call(
        body, name=name, grid=(nb,),
        in_specs=[pl.BlockSpec((R, D), lambda i: (i, 0)), ext, ext, ext, ext, pl.BlockSpec((R, RW), lambda i: (i + 1, 4)),
                  _const_spec((1, S5W)), _const_spec((1, S5W)), _const_spec((6, D)), _const_spec((S5W, S5W)), _const_spec((D, D))]
        + cg.in_specs(),
        out_specs=[pl.BlockSpec((R, D), lambda i: (i, 0))] + cg.in_specs(),
        out_shape=[jax.ShapeDtypeStruct((L, D), f32)] + cg.out_shapes(),
        scratch_shapes=cg.sems(),
        compiler_params=_CP(dimension_semantics=_ARB),
    )(x, ys, of, ob, p_ext, p_ext, dvec, bglu, modx, wglu, wout, *cg.arrays)


def _mix_bwd(x, ys, of, ob, p_ext, dvec, bglu, modx, wglu, wout, dx1, name, cargo=None):
    L = x.shape[0]
    nb = L // R + 1
    cg = _Cargo(cargo)

    def body(*refs):
        ins, outs, _ = cg.split(refs, 12, 11, 0)
        x_ref, ys_ref, of_ref, ob_ref, u_ref, g_ref, d_ref, b_ref, mx_ref, wg_ref, wo_ref, dx1_ref = ins
        dy_ref, dud_ref, do_ref, dg_ref, cat_ref, dmix_ref, s_ref, dz_ref, dd_ref, db_ref, dg1_ref = outs
        cg.ride(refs, 12, 11, nb)
        i = pl.program_id(0)

        @pl.when(i == 0)
        def _():
            for r in outs:
                r[...] = jnp.zeros_like(r)

        @pl.when(i > 0)
        def _():
            fn = lambda ys_, u_, of_, g_, d_, b_, g1_, pz_, pm_: _mix_fn(
                ys_, u_, of_, ob_ref[...], g_, x_ref[...], d_, b_, g1_, pz_, pm_, wg_ref[...], wo_ref[...])
            _, vjp, (s, cat) = jax.vjp(fn, ys_ref[...].astype(f32), u_ref[...].astype(f32), of_ref[...], g_ref[...].astype(f32), d_ref[...],
                                       b_ref[...], mx_ref[2:3], jnp.zeros((R, S5W), f32), jnp.zeros((R, D), f32), has_aux=True)
            dy, dud, do, dg, dd, db, dg1, dz, dmix = vjp(dx1_ref[...])
            dy_ref[...], dud_ref[...], do_ref[...], dg_ref[...] = dy.astype(bf16), dud, do, dg
            cat_ref[...], dmix_ref[...] = cat.astype(bf16), dmix.astype(bf16)
            s_ref[...], dz_ref[...] = s.astype(bf16), dz.astype(bf16)
            dd_ref[...] += dd
            db_ref[...] += db
            dg1_ref[...] += dg1

    lat = pl.BlockSpec((R, D), lambda i: (jnp.maximum(i - 1, 0), 0))
    lat5 = pl.BlockSpec((R, S5W), lambda i: (jnp.maximum(i - 1, 0), 0))
    ext = pl.BlockSpec((R, S5W), lambda i: (i, 0))
    eshape = jax.ShapeDtypeStruct((L + R, S5W), f32)
    return pl.pallas_call(
        body, name=name, grid=(nb,),
        in_specs=[lat, ext, ext, ext, ext, pl.BlockSpec((R, RW), lambda i: (i, 4)),
                  _const_spec((1, S5W)), _const_spec((1, S5W)), _const_spec((6, D)), _const_spec((S5W, S5W)), _const_spec((D, D)), lat]
        + cg.in_specs(),
        out_specs=[ext, ext, ext, ext, lat, lat, lat5, lat5, _acc_spec((1, S5W)), _acc_spec((1, S5W)), _acc_spec((1, D))]
        + cg.in_specs(),
        out_shape=[jax.ShapeDtypeStruct((L + R, S5W), bf16), eshape, eshape, eshape, jax.ShapeDtypeStruct((L, D), bf16),
                   jax.ShapeDtypeStruct((L, D), bf16), jax.ShapeDtypeStruct((L, S5W), bf16), jax.ShapeDtypeStruct((L, S5W), bf16),
                   jax.ShapeDtypeStruct((1, S5W), f32), jax.ShapeDtypeStruct((1, S5W), f32), jax.ShapeDtypeStruct((1, D), f32)]
        + cg.out_shapes(),
        scratch_shapes=cg.sems(),
        compiler_params=_CP(dimension_semantics=_ARB),
    )(x, ys, of, ob, p_ext, p_ext, dvec, bglu, modx, wglu, wout, dx1, *cg.arrays)


def _ffn_tail(gc, a, x1, gate2, fnw, pf, wdown, wdown_t, tgt):
    f = jax.nn.gelu(gc) * a
    ffn = _dnn_const(f, wdown, wdown_t) + pf
    y = _rms(x1 + gate2 * ffn, fnw)
    err = y - tgt
    loss = 0.5 * jnp.sum(jnp.mean(err * err, axis=-1, keepdims=True), axis=0, keepdims=True)
    return loss, f


def _ffn_fwd(x1, tgt, nw2, modx, w_a, w_g, cw, cb, wdown, wdown_t, fnw, name):
    L = x1.shape[0]
    nb = L // RF
    per = RF // HALO

    def body(x_ref, xp_ref, xn_ref, t_ref, nw_ref, mx_ref, wa_ref, wg_ref, cw_ref, cb_ref, wd_ref, wdt_ref, fn_ref,
             dx2_ref, da_ref, dgc_ref, f_ref, dffn_ref, loss_ref, dfn_ref, dg2_ref, dcb_ref, dcw_ref):
        i = pl.program_id(0)

        @pl.when(i == 0)
        def _():
            for r in (loss_ref, dfn_ref, dg2_ref, dcb_ref, dcw_ref):
                r[...] = jnp.zeros_like(r)

        nw, sh, sc, gate2 = nw_ref[...], mx_ref[3:4], mx_ref[4:5], mx_ref[5:6]
        x1b = x_ref[...]
        h2 = _mod(_rms(x1b, nw), sh, sc)
        h2e = jnp.concatenate([_mod(_rms(xp_ref[...], nw), sh, sc), h2, _mod(_rms(xn_ref[...], nw), sh, sc)], axis=0)
        a = dnn(h2, wa_ref[...])
        ge = dnn(h2e, wg_ref[...])
        g = ge[HALO:HALO + RF]
        gp = ge[HALO - 1:HALO] * jnp.where(i > 0, 1.0, 0.0)
        gn = ge[HALO + RF:HALO + RF + 1] * jnp.where(i < nb - 1, 1.0, 0.0)
        row = lax.broadcasted_iota(jnp.int32, (RF, 1), 0)
        g_prev = jnp.where(row == 0, gp, pltpu.roll(g, 1, axis=0))
        g_next = jnp.where(row == RF - 1, gn, pltpu.roll(g, RF - 1, axis=0))
        gc = cb_ref[...] + g_prev * cw_ref[0:1] + g * cw_ref[1:2] + g_next * cw_ref[2:3]
        fn = lambda gc_, a_, x_, g2_, fw_, pf_: _ffn_tail(gc_, a_, x_, g2_, fw_, pf_, wd_ref[...], wdt_ref[...], t_ref[...])
        loss, vjp, f = jax.vjp(fn, gc, a, x1b, gate2, fn_ref[...], jnp.zeros((RF, D), f32), has_aux=True)
        dgc, da, dx2, dg2, dfw, dffn = vjp(jnp.ones((1, 1), f32))
        dx2_ref[...] = dx2
        da_ref[...], dgc_ref[...] = da.astype(bf16), dgc
        f_ref[...], dffn_ref[...] = f.astype(bf16), dffn.astype(bf16)
        loss_ref[...] += jnp.broadcast_to(loss, (1, 128))
        dfn_ref[...] += dfw
        dg2_ref[...] += dg2
        dcb_ref[...] += jnp.sum(dgc, axis=0, keepdims=True)
        dcw_ref[0:1] += jnp.sum(dgc * g_prev, axis=0, keepdims=True)
        dcw_ref[1:2] += jnp.sum(dgc * g, axis=0, keepdims=True)
        dcw_ref[2:3] += jnp.sum(dgc * g_next, axis=0, keepdims=True)

    blk = lambda w: pl.BlockSpec((RF, w), lambda i: (i, 0))
    return pl.pallas_call(
        body, name=name, grid=(nb,),
        in_specs=[blk(D), pl.BlockSpec((HALO, D), lambda i: (jnp.maximum(i * per - 1, 0), 0)),
                  pl.BlockSpec((HALO, D), lambda i: (jnp.minimum((i + 1) * per, L // HALO - 1), 0)), blk(D),
                  _const_spec((1, D)), _const_spec((6, D)), _const_spec((D, DFF)), _const_spec((D, DFF)), _const_spec((3, DFF)),
                  _const_spec((1, DFF)), _const_spec((DFF, D)), _const_spec((D, DFF)), _const_spec((1, D))],
        out_specs=[blk(D), blk(DFF), blk(DFF), blk(DFF), blk(D), _acc_spec((1, 128)), _acc_spec((1, D)), _acc_spec((1, D)),
                   _acc_spec((1, DFF)), _acc_spec((3, DFF))],
        out_shape=[jax.ShapeDtypeStruct((L, D), f32), jax.ShapeDtypeStruct((L, DFF), bf16), jax.ShapeDtypeStruct((L, DFF), f32),
                   jax.ShapeDtypeStruct((L, DFF), bf16), jax.ShapeDtypeStruct((L, D), bf16), jax.ShapeDtypeStruct((1, 128), f32),
                   jax.ShapeDtypeStruct((1, D), f32), jax.ShapeDtypeStruct((1, D), f32), jax.ShapeDtypeStruct((1, DFF), f32),
                   jax.ShapeDtypeStruct((3, DFF), f32)],
        compiler_params=_CP(dimension_semantics=_ARB),
    )(x1, x1, x1, tgt, nw2, modx, w_a, w_g, cw, cb, wdown, wdown_t, fnw)


def _ffn_bwd(x1, dx2, da, dgc, nw2, modx, wup_t, cw, name):
    L = x1.shape[0]
    nb = L // RF
    per = RF // HALO

    def body(x_ref, dx2_ref, da_ref, dgc_ref, dgp_ref, dgn_ref, nw_ref, mx_ref, wu_ref, cw_ref,
             dx1_ref, dag_ref, h2_ref, dnw_ref, dmx_ref):
        i = pl.program_id(0)

        @pl.when(i == 0)
        def _():
            dnw_ref[...] = jnp.zeros_like(dnw_ref)
            dmx_ref[...] = jnp.zeros_like(dmx_ref)

        dgc_b = dgc_ref[...]
        before = dgp_ref[HALO - 1:HALO] * jnp.where(i > 0, 1.0, 0.0)
        after = dgn_ref[0:1] * jnp.where(i < nb - 1, 1.0, 0.0)
        row = lax.broadcasted_iota(jnp.int32, (RF, 1), 0)
        d_prev = jnp.where(row == 0, before, pltpu.roll(dgc_b, 1, axis=0))
        d_next = jnp.where(row == RF - 1, after, pltpu.roll(dgc_b, RF - 1, axis=0))
        dg = cw_ref[0:1] * d_next + cw_ref[1:2] * dgc_b + cw_ref[2:3] * d_prev
        dag = jnp.concatenate([da_ref[...], dg.astype(bf16)], axis=1)
        dag_ref[...] = dag
        dh2 = dnn(dag, wu_ref[...])
        h2, vjp = jax.vjp(lambda a, b, c, d: _mod(_rms(a, b), c, d), x_ref[...], nw_ref[...], mx_ref[3:4], mx_ref[4:5])
        dxa, dnw, dsh, dsc = vjp(dh2)
        h2_ref[...] = h2.astype(bf16)
        dx1_ref[...] = dx2_ref[...] + dxa
        dnw_ref[...] += dnw
        dmx_ref[3:4] += dsh
        dmx_ref[4:5] += dsc

    blk = lambda w: pl.BlockSpec((RF, w), lambda i: (i, 0))
    return pl.pallas_call(
        body, name=name, grid=(nb,),
        in_specs=[blk(D), blk(D), blk(DFF), blk(DFF), pl.BlockSpec((HALO, DFF), lambda i: (jnp.maximum(i * per - 1, 0), 0)),
                  pl.BlockSpec((HALO, DFF), lambda i: (jnp.minimum((i + 1) * per, L // HALO - 1), 0)),
                  _const_spec((1, D)), _const_spec((6, D)), _const_spec((2 * DFF, D)), _const_spec((3, DFF))],
        out_specs=[blk(D), blk(2 * DFF), blk(D), _acc_spec((1, D)), _acc_spec((6, D))],
        out_shape=[jax.ShapeDtypeStruct((L, D), f32), jax.ShapeDtypeStruct((L, 2 * DFF), bf16), jax.ShapeDtypeStruct((L, D), bf16),
                   jax.ShapeDtypeStruct((1, D), f32), jax.ShapeDtypeStruct((6, D), f32)],
        compiler_params=_CP(dimension_semantics=_ARB),
    )(x1, dx2, da, dgc, dgc, dgc, nw2, modx, wup_t, cw)


def _matmul_tn(a, b, name):
    k, m = a.shape
    n = b.shape[1]
    divs = lambda d: [c for c in range(d, 0, -128) if d % c == 0]
    _, tm, tn = min((m * (n // cn) + n * (m // cm), cm, cn) for cm in divs(m) for cn in divs(n) if cm * cn * 4 <= ACC_TILE_BYTES)
    tk = next(c for c in (512, 768, 256, 128) if k % c == 0)
    nk = k // tk

    def body(a_ref, b_ref, o_ref, acc):
        q = pl.program_id(2)

        @pl.when(q == 0)
        def _():
            acc[...] = jnp.zeros_like(acc)

        acc[...] += dtn(a_ref[...], b_ref[...])

        @pl.when(q == nk - 1)
        def _():
            o_ref[...] = acc[...].astype(bf16)

    return pl.pallas_call(
        body, name=name, grid=(m // tm, n // tn, nk),
        in_specs=[pl.BlockSpec((tk, tm), lambda i, j, q: (q, i)), pl.BlockSpec((tk, tn), lambda i, j, q: (q, j))],
        out_specs=pl.BlockSpec((tm, tn), lambda i, j, q: (i, j)),
        out_shape=jax.ShapeDtypeStruct((m, n), bf16),
        scratch_shapes=[pltpu.VMEM((tm, tn), f32)],
        compiler_params=_CP(dimension_semantics=("parallel", "parallel", "arbitrary")),
    )(a, b)


def _adamw(w, g, m, v, name):
    c1, c2 = 1.0 - B1 ** STEP, 1.0 - B2 ** STEP

    def body(w_ref, g_ref, m_ref, v_ref, d_ref, nm_ref, nv_ref):
        gg = g_ref[...]
        nm = B1 * m_ref[...] + (1.0 - B1) * gg
        nv = B2 * v_ref[...] + (1.0 - B2) * jnp.square(gg)
        d_ref[...] = -LR * ((nm / c1) / (jnp.sqrt(nv / c2) + AEPS) + WD * w_ref[...])
        nm_ref[...], nv_ref[...] = nm, nv

    return pl.pallas_call(body, name=name, out_shape=[jax.ShapeDtypeStruct(w.shape, f32)] * 3, compiler_params=_CP())(w, g, m, v)


SMALL = ["conv_w", "c_ctx", "norm1_w", "s5_lambda_re_f", "s5_lambda_im_f", "s5_log_step_f", "s5_lambda_re_b", "s5_lambda_im_b",
         "s5_log_step_b", "s5_b_re", "s5_b_im", "s5_c_re", "s5_c_im", "s5_d", "s5_b_glu", "ret_log_decay_f", "ret_log_decay_b",
         "norm2_w", "conv_b", "final_norm_w"]
WEIGHTS = ["c_ctx", "w_mod", "b_mod", "norm1_w", "w_in", "s5_lambda_re_f", "s5_lambda_im_f", "s5_log_step_f", "s5_lambda_re_b",
           "s5_lambda_im_b", "s5_log_step_b", "s5_b_re", "s5_b_im", "s5_c_re", "s5_c_im", "s5_d", "s5_w_glu", "s5_b_glu",
           "ret_log_decay_f", "ret_log_decay_b", "w_out", "norm2_w", "w_up", "conv_w", "conv_b", "w_down", "final_norm_w"]


def _pack_small(vals):
    flat, offs, o = [], [], 0
    for a in vals:
        n = a.size
        npad = -n % 128
        flat.append(jnp.pad(a.reshape(-1), (0, npad)))
        offs.append((o, n))
        o += n + npad
    tail = -o % 1024
    if tail:
        flat.append(jnp.zeros((tail,), f32))
    return jnp.concatenate(flat).reshape(-1, 128), offs


def _unpack_small(packed, offs, shapes):
    flat = packed.reshape(-1)
    return [flat[o:o + n].reshape(s) for (o, n), s in zip(offs, shapes)]


def _rope_tables(L, nctx_rows):
    t = np.arange(L)
    inv = (ROPE_THETA ** (-np.arange(DH // 4, dtype=np.float64) / (DH // 4))).astype(np.float32)
    ang = np.concatenate([(t // GRID_W).astype(np.float32)[:, None] * inv, (t % GRID_W).astype(np.float32)[:, None] * inv], axis=-1)
    cos = np.repeat(np.cos(ang).astype(np.float32), 2, axis=1)
    sin = np.repeat(np.sin(ang).astype(np.float32), 2, axis=1) * np.tile(np.array([-1.0, 1.0], np.float32), DH // 2)
    cosf = np.concatenate([np.ones((nctx_rows, DH), np.float32), cos], axis=0)
    sins = np.concatenate([np.zeros((nctx_rows, DH), np.float32), sin], axis=0)
    return jnp.asarray(cosf), jnp.asarray(sins)


def kernel(x, c, ctx, c_ctx, w_mod, b_mod, norm1_w, w_in, s5_lambda_re_f, s5_lambda_im_f, s5_log_step_f, s5_lambda_re_b, s5_lambda_im_b, s5_log_step_b, s5_b_re, s5_b_im, s5_c_re, s5_c_im, s5_d, s5_w_glu, s5_b_glu, ret_log_decay_f, ret_log_decay_b, w_out, norm2_w, w_up, conv_w, conv_b, w_down, final_norm_w, loss_target, m_c_ctx, m_w_mod, m_b_mod, m_norm1_w, m_w_in, m_s5_lambda_re_f, m_s5_lambda_im_f, m_s5_log_step_f, m_s5_lambda_re_b, m_s5_lambda_im_b, m_s5_log_step_b, m_s5_b_re, m_s5_b_im, m_s5_c_re, m_s5_c_im, m_s5_d, m_s5_w_glu, m_s5_b_glu, m_ret_log_decay_f, m_ret_log_decay_b, m_w_out, m_norm2_w, m_w_up, m_conv_w, m_conv_b, m_w_down, m_final_norm_w, v_c_ctx, v_w_mod, v_b_mod, v_norm1_w, v_w_in, v_s5_lambda_re_f, v_s5_lambda_im_f, v_s5_log_step_f, v_s5_lambda_re_b, v_s5_lambda_im_b, v_s5_log_step_b, v_s5_b_re, v_s5_b_im, v_s5_c_re, v_s5_c_im, v_s5_d, v_s5_w_glu, v_s5_b_glu, v_ret_log_decay_f, v_ret_log_decay_b, v_w_out, v_norm2_w, v_w_up, v_conv_w, v_conv_b, v_w_down, v_final_norm_w):
    args = dict(locals())
    W = {n: args[n] for n in WEIGHTS}
    M = {n: args["m_" + n] for n in WEIGHTS}
    V = {n: args["v_" + n] for n in WEIGHTS}
    me = _me()
    x2, ctx2, tgt = x[0], ctx[0], loss_target[0]
    L, Lc = x2.shape[0], ctx2.shape[0]
    assert Lc == R and L % R == 0 and L % GRID_W == 0
    nctx = Lc // T

    c_all = _all_gather_small(jnp.pad(c, ((0, 7), (0, 0))), "gather_c")[:, 0, :]
    c9 = jnp.concatenate([c_all, c_ctx[None], jnp.zeros((7, D), f32)], axis=0)
    w_mod_l = w_mod[0]
    ncol = w_mod_l.shape[1]
    m_part = _ada_fwd(c9, w_mod_l, "ada_fwd")
    m_all = _all_gather_small(m_part, "gather_mod").transpose(1, 0, 2).reshape(16, 6, D)
    modx, modc = _mod_select(m_all, b_mod.reshape(6, D), "mod_select")

    w_in_tl, w_up_tl = w_in[0].T.astype(bf16), w_up[0].T.astype(bf16)
    w_out_l, w_down_l, w_glu_l = w_out[0].astype(bf16), w_down[0].astype(bf16), s5_w_glu[0].astype(bf16)
    half_up = w_up_tl.shape[0] // 2
    (w_in_g,) = _exchange([w_in_tl], False, "gather_w_in")
    w_in_t = w_in_g.reshape(INC, D)
    per_cv = conv_w.shape[2]
    conv_pad = jnp.pad(conv_w[0], ((0, 5), (0, 128 * 3 - per_cv)))
    conv_f = _all_gather_small(conv_pad, "gather_conv")[:, :3, :per_cv].transpose(1, 0, 2).reshape(3, DFF)

    pair = lambda a, b: jnp.concatenate([a, b], axis=-1)
    bre_g, bim_g = s5_b_re[0].transpose(0, 2, 1), s5_b_im[0].transpose(0, 2, 1)
    cre_g, cim_g = s5_c_re[0], s5_c_im[0]
    shared = (pair(bre_g, bim_g), pair(bim_g, bre_g), pair(cre_g, cim_g), pair(cim_g, cre_g))
    s5p = {}
    for tag, lre, lim, ls in (("f", s5_lambda_re_f, s5_lambda_im_f, s5_log_step_f), ("b", s5_lambda_re_b, s5_lambda_im_b, s5_log_step_b)):
        s5p[tag] = (pair(lre[0], lre[0])[:, None, :], pair(lim[0], lim[0])[:, None, :], ls[0].reshape(S5G, 1, 1)) + shared
    m_f, mb_f, mc_f, a1_f, a2_f = _s5_build(s5p["f"], False, "s5_build_f")
    m_b, mb_b, mc_b, a1_b, a2_b = _s5_build(s5p["b"], True, "s5_build_b")
    a1_f, a2_f, a1_b, a2_b = (a.reshape(S5G, SB) for a in (a1_f, a2_f, a1_b, a2_b))

    nw1, nw2, fnw = norm1_w, norm2_w, final_norm_w[None]
    cosf, sins = _rope_tables(L, Lc)
    p_ext = _f1_fwd(x2, ctx2, modx, modc, nw1, w_in_t.T, cosf, sins, "f1_fwd")
    nctx5 = Lc // TC
    u_g = _to_groups(p_ext[:, :S5W])
    s_f, s_b = _s5_inc(u_g, mb_f, mb_b, "s5_inc")
    hp_f, hp_b = _s5_carry(s_f, s_b, (a1_f, a2_f), (a1_b, a2_b), nctx5, "s5_carry")
    ys = _from_groups(_s5_out(u_g, m_f, m_b, hp_f, hp_b, mc_f, mc_b, "s5_out"))
    ld8 = lambda ld: jnp.pad(jnp.broadcast_to(ld[0][:, None], (RH, 128)), ((0, 8 - RH), (0, 0)))
    ldf8, ldb8 = ld8(ret_log_decay_f), ld8(ret_log_decay_b)
    of, rp_f, w_out_g, w_glu_g, w_up_g1 = _ret_fwd(p_ext, ldf8, False, nctx, "ret_fwd_f",
                                                   cargo=([w_out_l, w_glu_l, w_up_tl[:half_up]], False))
    ob, rp_b, w_up_g2 = _ret_fwd(p_ext, ldb8, True, nctx, "ret_fwd_b", cargo=([w_up_tl[half_up:]], False))
    w_out_f, w_glu_f = w_out_g.reshape(D, D), w_glu_g.reshape(S5W, S5W)
    x1, w_down_g = _mix_fwd(x2, ys, of, ob, p_ext, s5_d, s5_b_glu, modx, w_glu_f, w_out_f, "mix_fwd", cargo=([w_down_l], False))
    w_down_f = w_down_g.reshape(DFF, D)
    w_up_t = jnp.concatenate([w_up_g1, w_up_g2], axis=1).reshape(2 * DFF, D)

    (dx2, da, dgc, f_act, dffn, loss_acc, g_fnw, g_gate2, g_cb, g_cw) = _ffn_fwd(
        x1, tgt, nw2, modx, w_up_t[:DFF].T, w_up_t[DFF:].T, conv_f, conv_b, w_down_f, w_down_f.T, fnw, "ffn_fwd")
    dx1, dag, h2, g_nw2, dmx2 = _ffn_bwd(x1, dx2, da, dgc, nw2, modx, w_up_t, conv_f, "ffn_bwd")
    gw_down = _matmul_tn(f_act, dffn, "dw_down").reshape(NDEV, -1, D)
    gw_up_t = _matmul_tn(dag, h2, "dw_up").reshape(NDEV, -1, D)
    (dy_e, dud_e, do_e, dg_e, cat, dmix, s_act, dz, g_d, g_bglu, g_gate1, l_down) = _mix_bwd(
        x2, ys, of, ob, p_ext, s5_d, s5_b_glu, modx, w_glu_f, w_out_f, dx1, "mix_bwd", cargo=([gw_down], True))
    gw_out = _matmul_tn(cat, dmix, "dw_out").reshape(NDEV, -1, D)
    gw_glu = _matmul_tn(s_act, dz, "dw_glu").reshape(NDEV, -1, S5W)
    dq_f, dk_f, dv_f, gld_f, l_up = _ret_bwd(p_ext, ldf8, rp_f, do_e, False, nctx, "ret_bwd_f", cargo=([gw_up_t], True))
    dq_b, dk_b, dv_b, gld_b, l_out, l_glu = _ret_bwd(p_ext, ldb8, rp_b, do_e, True, nctx, "ret_bwd_b",
                                                     cargo=([gw_out, gw_glu], True))

    du1, g_m, dhp_f, dhp_b, dmc_f, dmc_b = _s5_out_bwd(_to_groups(dy_e), u_g, m_f, m_b, hp_f, hp_b, mc_f, mc_b, "s5_out_bwd")
    ds_f, da1_f, da2_f = _s5_carry_bwd(dhp_f, hp_f, a1_f, a2_f, False, nctx5, "s5_carry_bwd_f")
    ds_b, da1_b, da2_b = _s5_carry_bwd(dhp_b, hp_b, a1_b, a2_b, True, nctx5, "s5_carry_bwd_b")
    du_g, dmb_f, dmb_b = _s5_inc_bwd(du1, u_g, ds_f, ds_b, mb_f, mb_b, "s5_inc_bwd")
    zero_p = jnp.zeros((S5G, S5P, SB), f32)
    gf = _s5_build_bwd(s5p["f"], (g_m, dmb_f, dmc_f, da1_f[:, None, :], da2_f[:, None, :]), (zero_p, zero_p), False, "s5_build_bwd_f")
    gb = _s5_build_bwd(s5p["b"], (g_m, dmb_b, dmc_b, da1_b[:, None, :], da2_b[:, None, :]), (gf[3], gf[4]), True, "s5_build_bwd_b")
    g_bre, g_bim = gb[3][:, :, :S5N].transpose(0, 2, 1), gb[3][:, :, S5N:].transpose(0, 2, 1)
    g_cre, g_cim = gb[4][:, :, :S5N], gb[4][:, :, S5N:]

    grad_x, dp_ext, h1, g_nw1, dmx1, dmc1 = _f1_bwd(
        x2, ctx2, modx, modc, nw1, w_in_t, cosf, sins, dx1, (_from_groups(du_g), dud_e, dq_f, dq_b, dk_f, dk_b, dv_f, dv_b, dg_e), "f1_bwd")
    gw_in_t = _matmul_tn(dp_ext, h1, "dw_in").reshape(NDEV, -1, D)
    (l_in,) = _exchange([gw_in_t], True, "scatter_dw_in")

    dmx = dmx1 + dmx2
    dmx = dmx.at[2].set(g_gate1[0]).at[5].set(g_gate2[0])
    dm_me = jnp.stack([dmx.reshape(-1), dmc1.reshape(-1)], axis=0)
    dm_all = _all_gather_small(jnp.pad(dm_me, ((0, 6), (0, 0))), "gather_dmod")
    dmx_all, dmc_all = dm_all[:, 0, :], dm_all[:, 1, :]
    my_cols = lambda a: lax.dynamic_slice(a, (0, me * ncol), (NDEV, ncol))
    gw_mod, g_bmod, dc9 = _ada_bwd(c9, dmx_all, dmc_all, my_cols(dmx_all), my_cols(dmc_all), w_mod_l, "ada_bwd")

    small = {
        "conv_w": g_cw, "c_ctx": dc9[8], "norm1_w": g_nw1, "s5_lambda_re_f": gf[0][:, 0, :S5N], "s5_lambda_im_f": gf[1][:, 0, :S5N],
        "s5_log_step_f": gf[2], "s5_lambda_re_b": gb[0][:, 0, :S5N], "s5_lambda_im_b": gb[1][:, 0, :S5N], "s5_log_step_b": gb[2],
        "s5_b_re": g_bre, "s5_b_im": g_bim, "s5_c_re": g_cre, "s5_c_im": g_cim, "s5_d": g_d, "s5_b_glu": g_bglu,
        "ret_log_decay_f": gld_f[:RH, 0], "ret_log_decay_b": gld_b[:RH, 0], "norm2_w": g_nw2, "conv_b": g_cb, "final_norm_w": g_fnw,
    }
    packed, soffs = _pack_small([small[n].astype(f32) for n in SMALL])
    red = _all_reduce_small(packed, "reduce_small")
    sshapes = [(3, DFF) if n == "conv_w" else W[n].shape for n in SMALL]
    G = dict(zip(SMALL, _unpack_small(red, soffs, sshapes)))
    G["conv_w"] = lax.dynamic_slice(G["conv_w"], (0, me * per_cv), (3, per_cv))[None]
    G["b_mod"] = g_bmod.reshape(b_mod.shape)
    G["w_mod"] = gw_mod[None]
    G["w_in"] = _sum8(l_in, "sum_dw_in").T[None]
    G["w_up"] = _sum8(l_up, "sum_dw_up").T[None]
    G["w_out"] = _sum8(l_out, "sum_dw_out")[None]
    G["w_down"] = _sum8(l_down, "sum_dw_down")[None]
    G["s5_w_glu"] = _sum8(l_glu, "sum_dw_glu")[None]

    delta, new_m, new_v = {}, {}, {}
    sm_names = SMALL[1:] + ["b_mod"]
    pk = lambda d: _pack_small([d[n].astype(f32) for n in sm_names])
    (pw, aoffs), (pg, _), (pm, _), (pv, _) = pk(W), pk(G), pk(M), pk(V)
    pd, pnm, pnv = _adamw(pw, pg, pm, pv, "adamw_small")
    shapes = [W[n].shape for n in sm_names]
    for dst, src in ((delta, pd), (new_m, pnm), (new_v, pnv)):
        dst.update(zip(sm_names, _unpack_small(src, aoffs, shapes)))
    for n in ["w_mod", "w_in", "w_out", "w_up", "w_down", "s5_w_glu", "conv_w"]:
        d, nm, nv = _adamw(W[n][0], G[n][0], M[n][0], V[n][0], "adamw_" + n)
        delta[n], new_m[n], new_v[n] = d[None], nm[None], nv[None]

    loss = lax.psum(loss_acc[0, 0], ("x", "y", "c"))
    return (loss, grad_x[None], *[G[n] for n in WEIGHTS], *[delta[n] for n in WEIGHTS], *[new_m[n] for n in WEIGHTS],
            *[new_v[n] for n in WEIGHTS])
```

```python
import functools

import numpy as np
import jax
import jax.numpy as jnp
from jax import lax
from jax.experimental import pallas as pl
from jax.experimental.pallas import tpu as pltpu

f32, bf16 = jnp.float32, jnp.bfloat16

D = 1024
S5W, S5G, S5P, S5N = 512, 32, 16, 64
TC = 16
TCP = TC * S5P
SB = 2 * S5N
GBK = 8
CARRY_UNROLL = 8
RH, DH = 4, 128
RW = RH * DH
INC = S5W + 4 * RW
DFF = 2816
T = 128
R = 256
RF = 128
HALO = 8
EPS = 1e-6
ROPE_THETA = 10000.0
GRID_W = 64
NDEV = 8
LR, B1, B2, AEPS, WD, STEP = 0.001, 0.9, 0.999, 1e-08, 0.01, 10
VMEM_LIMIT = 60 * 1024 * 1024
ACC_TILE_BYTES = 6 * 1024 * 1024
MESH = pl.DeviceIdType.MESH

_CP = functools.partial(pltpu.CompilerParams, vmem_limit_bytes=VMEM_LIMIT)
_ARB = ("arbitrary",)
_ANY = pl.BlockSpec(memory_space=pl.ANY)


def _dg(a, b, dims):
    return lax.dot_general(a.astype(bf16), b.astype(bf16), (dims, ((), ())), preferred_element_type=f32)


@jax.custom_vjp
def dnn(a, b):
    return _dg(a, b, ((1,), (0,)))


@jax.custom_vjp
def dnt(a, b):
    return _dg(a, b, ((1,), (1,)))


@jax.custom_vjp
def dtn(a, b):
    return _dg(a, b, ((0,), (0,)))


dnn.defvjp(lambda a, b: (dnn(a, b), (a, b)), lambda r, g: (dnt(g, r[1]).astype(r[0].dtype), dtn(r[0], g).astype(r[1].dtype)))
dnt.defvjp(lambda a, b: (dnt(a, b), (a, b)), lambda r, g: (dnn(g, r[1]).astype(r[0].dtype), dtn(g, r[0]).astype(r[1].dtype)))
dtn.defvjp(lambda a, b: (dtn(a, b), (a, b)), lambda r, g: (dnt(r[1], g).astype(r[0].dtype), dnn(r[0], g).astype(r[1].dtype)))


@jax.custom_vjp
def _dnn_const(a, w, wt):
    return dnn(a, w)


_dnn_const.defvjp(lambda a, w, wt: (dnn(a, w), wt), lambda wt, g: (dnn(g, wt), None, None))


def _rms(t, w):
    return t * lax.rsqrt(jnp.mean(t * t, axis=-1, keepdims=True) + EPS) * w


def _mod(h, shift, scale):
    return h * (1.0 + scale) + shift


def _const_spec(shape):
    n = len(shape)
    return pl.BlockSpec(shape, lambda i, _n=n: (0,) * _n, pipeline_mode=pl.Buffered(1))


def _acc_spec(shape):
    n = len(shape)
    return pl.BlockSpec(shape, lambda i, _n=n: (0,) * _n)


def _me():
    return 4 * lax.axis_index("x") + 2 * lax.axis_index("y") + lax.axis_index("c")


def _peer(r):
    x, y, c = lax.axis_index("x"), lax.axis_index("y"), lax.axis_index("c")
    px = 1 - x if (r >> 2) & 1 else x
    py = 1 - y if (r >> 1) & 1 else y
    pc = 1 - c if r & 1 else c
    return (px, py, pc), 4 * px + 2 * py + pc


def _all_gather_small(v, name):
    r, c = v.shape

    def body(v_ref, out_ref, send_sems, recv_sems):
        me = _me()
        out_ref[me] = v_ref[...]
        sends = []
        for k in range(1, NDEV):
            peer, _ = _peer(k)
            cp = pltpu.make_async_remote_copy(src_ref=v_ref, dst_ref=out_ref.at[me], send_sem=send_sems.at[k - 1],
                                              recv_sem=recv_sems.at[k - 1], device_id=peer, device_id_type=MESH)
            cp.start()
            sends.append(cp)
        for k in range(1, NDEV):
            peer, pidx = _peer(k)
            pltpu.make_async_remote_copy(src_ref=v_ref, dst_ref=out_ref.at[pidx], send_sem=send_sems.at[k - 1],
                                         recv_sem=recv_sems.at[k - 1], device_id=peer, device_id_type=MESH).wait_recv()
        for cp in sends:
            cp.wait_send()

    return pl.pallas_call(
        body, name=name, out_shape=jax.ShapeDtypeStruct((NDEV, r, c), v.dtype),
        in_specs=[pl.BlockSpec(memory_space=pltpu.VMEM)], out_specs=pl.BlockSpec(memory_space=pltpu.VMEM),
        scratch_shapes=[pltpu.SemaphoreType.DMA((NDEV - 1,)), pltpu.SemaphoreType.DMA((NDEV - 1,))],
        compiler_params=_CP(),
    )(v)


def _all_reduce_small(v, name):
    r, c = v.shape

    def body(v_ref, out_ref, land, send_sems, recv_sems):
        me = _me()
        land[me] = v_ref[...]
        sends = []
        for k in range(1, NDEV):
            peer, _ = _peer(k)
            cp = pltpu.make_async_remote_copy(src_ref=v_ref, dst_ref=land.at[me], send_sem=send_sems.at[k - 1],
                                              recv_sem=recv_sems.at[k - 1], device_id=peer, device_id_type=MESH)
            cp.start()
            sends.append(cp)
        for k in range(1, NDEV):
            peer, pidx = _peer(k)
            pltpu.make_async_remote_copy(src_ref=v_ref, dst_ref=land.at[pidx], send_sem=send_sems.at[k - 1],
                                         recv_sem=recv_sems.at[k - 1], device_id=peer, device_id_type=MESH).wait_recv()
        for cp in sends:
            cp.wait_send()
        acc = land[0]
        for j in range(1, NDEV):
            acc = acc + land[j]
        out_ref[...] = acc

    return pl.pallas_call(
        body, name=name, out_shape=jax.ShapeDtypeStruct((r, c), v.dtype),
        in_specs=[pl.BlockSpec(memory_space=pltpu.VMEM)], out_specs=pl.BlockSpec(memory_space=pltpu.VMEM),
        scratch_shapes=[pltpu.VMEM((NDEV, r, c), v.dtype), pltpu.SemaphoreType.DMA((NDEV - 1,)),
                        pltpu.SemaphoreType.DMA((NDEV - 1,))],
        compiler_params=_CP(),
    )(v)


class _Exchange:
    def __init__(self, srcs, dsts, send_sems, recv_sems, local_sems, scatter):
        me = _me()
        n = len(srcs)
        self.sends, self.recvs, self.locals = [], [], []
        for a, (s, d) in enumerate(zip(srcs, dsts)):
            self.locals.append(pltpu.make_async_copy(s.at[me] if scatter else s, d.at[me], local_sems.at[a]))
        for k in range(1, NDEV):
            peer, pidx = _peer(k)
            for a, (s, d) in enumerate(zip(srcs, dsts)):
                src = s.at[pidx] if scatter else s
                sem = (k - 1) * n + a
                for dst, out in ((d.at[me], self.sends), (d.at[pidx], self.recvs)):
                    out.append(pltpu.make_async_remote_copy(src_ref=src, dst_ref=dst, send_sem=send_sems.at[sem],
                                                            recv_sem=recv_sems.at[sem], device_id=peer, device_id_type=MESH))

    def start(self):
        for cp in self.locals + self.sends:
            cp.start()

    def wait(self):
        for cp in self.recvs:
            cp.wait_recv()
        for cp in self.sends:
            cp.wait_send()
        for cp in self.locals:
            cp.wait()


def _exchange_shapes(arrays, scatter):
    return [jax.ShapeDtypeStruct(a.shape if scatter else (NDEV,) + a.shape, a.dtype) for a in arrays]


def _exchange_sems(n):
    return [pltpu.SemaphoreType.DMA(((NDEV - 1) * n,)), pltpu.SemaphoreType.DMA(((NDEV - 1) * n,)), pltpu.SemaphoreType.DMA((n,))]


def _exchange(arrays, scatter, name):
    n = len(arrays)

    def body(*refs):
        ex = _Exchange(refs[:n], refs[n:2 * n], *refs[2 * n:], scatter)
        ex.start()
        ex.wait()

    return pl.pallas_call(body, name=name, out_shape=_exchange_shapes(arrays, scatter), in_specs=[_ANY] * n,
                          out_specs=[_ANY] * n, scratch_shapes=_exchange_sems(n), compiler_params=_CP())(*arrays)


class _Cargo:
    def __init__(self, cargo):
        self.arrays, self.scatter = cargo if cargo else ([], False)
        self.n = len(self.arrays)

    def in_specs(self):
        return [_ANY] * self.n

    def out_shapes(self):
        return _exchange_shapes(self.arrays, self.scatter)

    def sems(self):
        return _exchange_sems(self.n) if self.n else []

    def split(self, refs, n_in, n_out, n_scratch):
        n = self.n
        return refs[:n_in], refs[n_in + n:n_in + n + n_out], refs[n_in + 2 * n + n_out:n_in + 2 * n + n_out + n_scratch]

    def ride(self, refs, n_in, n_out, nsteps):
        if not self.n:
            return
        n = self.n
        ex = _Exchange(refs[n_in:n_in + n], refs[n_in + n + n_out:n_in + 2 * n + n_out], *refs[-3:], self.scatter)

        @pl.when(pl.program_id(0) == 0)
        def _():
            ex.start()

        @pl.when(pl.program_id(0) == nsteps - 1)
        def _():
            ex.wait()


def _sum8(land, name):
    _, r, c = land.shape
    rb = next((b for b in (256, 64, 32) if r % b == 0), r)

    def body(l_ref, o_ref):
        acc = l_ref[0].astype(f32)
        for j in range(1, NDEV):
            acc = acc + l_ref[j].astype(f32)
        o_ref[...] = acc

    return pl.pallas_call(
        body, name=name, grid=(r // rb,), out_shape=jax.ShapeDtypeStruct((r, c), f32),
        in_specs=[pl.BlockSpec((NDEV, rb, c), lambda i: (0, i, 0))], out_specs=pl.BlockSpec((rb, c), lambda i: (i, 0)),
        compiler_params=_CP(dimension_semantics=("parallel",)),
    )(land)


def _ada_fwd(c9, w_mod_l, name):
    def body(c_ref, w_ref, o_ref):
        o_ref[...] = dnn(jax.nn.silu(c_ref[...]), w_ref[...])

    return pl.pallas_call(body, name=name, out_shape=jax.ShapeDtypeStruct((16, w_mod_l.shape[1]), f32),
                          compiler_params=_CP())(c9, w_mod_l)


def _mod_select(m_all, b_mod6, name):
    def body(m_ref, b_ref, mx_ref, mc_ref):
        me = _me()
        mx_ref[...] = m_ref[me] + b_ref[...]
        mc_ref[...] = m_ref[8] + b_ref[...]

    return pl.pallas_call(body, name=name, out_shape=[jax.ShapeDtypeStruct((6, D), f32)] * 2, compiler_params=_CP())(m_all, b_mod6)


def _ada_bwd(c9, dmx_all, dmc_all, dmx_l, dmc_l, w_mod_l, name):
    ncol = w_mod_l.shape[1]

    def rowsum(r):
        acc = r[0:1]
        for j in range(1, NDEV):
            acc = acc + r[j:j + 1]
        return acc

    def body(c_ref, xa_ref, ca_ref, xl_ref, cl_ref, w_ref, gw_ref, gb_ref, dc_ref):
        s9, vjp = jax.vjp(jax.nn.silu, c_ref[...])
        dm9 = jnp.concatenate([xl_ref[...], rowsum(cl_ref[...]), jnp.zeros((7, ncol), f32)], axis=0)
        gw_ref[...] = dtn(s9, dm9)
        gb_ref[...] = rowsum(xa_ref[...]) + rowsum(ca_ref[...])
        dc_ref[...] = vjp(dnt(dm9, w_ref[...]))[0]

    return pl.pallas_call(
        body, name=name,
        out_shape=[jax.ShapeDtypeStruct((D, ncol), f32), jax.ShapeDtypeStruct((1, 6 * D), f32), jax.ShapeDtypeStruct((16, D), f32)],
        compiler_params=_CP())(c9, dmx_all, dmc_all, dmx_l, dmc_l, w_mod_l)


def _lane_sign(rank):
    shape = (1,) * (rank - 1) + (SB,)
    return jnp.where(lax.broadcasted_iota(jnp.int32, shape, rank - 1) < S5N, -1.0, 1.0)


def _s5_build_fn(lre2, lim2, ls, bn, bs, cn, cs, rev):
    sg = _lane_sign(3)
    s = jnp.exp(ls)
    ar, ai = lre2 * s, lim2 * s
    e = jnp.exp(ar)
    nr, ni = e * jnp.cos(ai) - 1.0, e * jnp.sin(ai)
    den = lre2 * lre2 + lim2 * lim2
    cr, ci = (nr * lre2 + ni * lim2) / den, (ni * lre2 - nr * lim2) / den
    bbn = cr * bn + (ci * sg) * bs
    bbs = cr * bs - (ci * sg) * bn

    def powers(ex):
        m, ang = jnp.exp(ex * ar), ex * ai
        return m * jnp.cos(ang), m * jnp.sin(ang) * sg

    def times(tabs, xn, xs):
        f1, f2 = tabs
        return f1[:, :, None, :] * xn[:, None, :, :] + f2[:, :, None, :] * xs[:, None, :, :]

    t = lax.broadcasted_iota(jnp.int32, (1, TC, 1), 1).astype(f32)
    if rev:
        e_src, e_dst, e_out, e_in = t - (TC - 1.0), (TC - 1.0) - t, t, TC - t
    else:
        e_src, e_dst, e_out, e_in = -t, t, (TC - 1.0) - t, t + 1.0
    g = lre2.shape[0]
    flat = lambda a: a.reshape(g, TCP, SB)
    conj = -_lane_sign(4)
    ll = flat(times(powers(e_src), bbn, bbs))
    rr = flat(times(powers(e_dst), cn, cs) * conj)
    mb = flat(times(powers(e_out), bbn, bbs))
    mct = flat(times(powers(e_in), cn, cs) * conj)
    a1, a2 = powers(float(TC))
    row = lax.broadcasted_iota(jnp.int32, (TCP, TCP), 0) // S5P
    col = lax.broadcasted_iota(jnp.int32, (TCP, TCP), 1) // S5P
    mask = jnp.where((col <= row) if rev else (col >= row), 1.0, 0.0)
    m = jnp.concatenate([dnt(ll[j], rr[j])[None] for j in range(g)], axis=0) * mask
    return m, mb, mct, a1, a2


def _gspec(*tail):
    nt = len(tail)
    return pl.BlockSpec((GBK,) + tail, lambda i, _n=nt: (i,) + (0,) * _n)


def _s5_build(params, rev, name):
    def body(l1, l2, ls, bn, bs, cn, cs, m_ref, mb_ref, mc_ref, a1_ref, a2_ref):
        m, mb, mct, a1, a2 = _s5_build_fn(l1[...], l2[...], ls[...], bn[...], bs[...], cn[...], cs[...], rev)
        m_ref[...], mb_ref[...], mc_ref[...] = m.astype(bf16), mb.astype(bf16), mct.astype(bf16)
        a1_ref[...], a2_ref[...] = a1, a2

    vec, pm = _gspec(1, SB), _gspec(S5P, SB)
    return pl.pallas_call(
        body, name=name, grid=(S5G // GBK,),
        in_specs=[vec, vec, _gspec(1, 1), pm, pm, pm, pm],
        out_specs=[_gspec(TCP, TCP), _gspec(TCP, SB), _gspec(TCP, SB), vec, vec],
        out_shape=[jax.ShapeDtypeStruct((S5G, TCP, TCP), bf16), jax.ShapeDtypeStruct((S5G, TCP, SB), bf16),
                   jax.ShapeDtypeStruct((S5G, TCP, SB), bf16), jax.ShapeDtypeStruct((S5G, 1, SB), f32),
                   jax.ShapeDtypeStruct((S5G, 1, SB), f32)],
        compiler_params=_CP(dimension_semantics=("parallel",)),
    )(*params)


def _s5_build_bwd(params, cots, prev, rev, name):
    def body(l1, l2, ls, bn, bs, cn, cs, dm, dmb, dmc, da1, da2, pb, pc, gl1, gl2, gls, gb, gc):
        prim = (l1[...], l2[...], ls[...], bn[...], bs[...], cn[...], cs[...])
        _, vjp = jax.vjp(functools.partial(_s5_build_fn, rev=rev), *prim)
        d1, d2, dls, dbn, dbs, dcn, dcs = vjp((dm[...], dmb[...], dmc[...], da1[...], da2[...]))
        gl1[...] = d1 + pltpu.roll(d1, S5N, axis=2)
        gl2[...] = d2 + pltpu.roll(d2, S5N, axis=2)
        gls[...] = dls
        gb[...] = dbn + pltpu.roll(dbs, S5N, axis=2) + pb[...]
        gc[...] = dcn + pltpu.roll(dcs, S5N, axis=2) + pc[...]

    vec, pm, big = _gspec(1, SB), _gspec(S5P, SB), _gspec(TCP, SB)
    return pl.pallas_call(
        body, name=name, grid=(S5G // GBK,),
        in_specs=[vec, vec, _gspec(1, 1), pm, pm, pm, pm, _gspec(TCP, TCP), big, big, vec, vec, pm, pm],
        out_specs=[vec, vec, _gspec(1, 1), pm, pm],
        out_shape=[jax.ShapeDtypeStruct((S5G, 1, SB), f32), jax.ShapeDtypeStruct((S5G, 1, SB), f32),
                   jax.ShapeDtypeStruct((S5G, 1, 1), f32), jax.ShapeDtypeStruct((S5G, S5P, SB), f32),
                   jax.ShapeDtypeStruct((S5G, S5P, SB), f32)],
        compiler_params=_CP(dimension_semantics=("parallel",)),
    )(*params, *cots, *prev)


def _s5_inc(u, mb_f, mb_b, name):
    nc = u.shape[1]

    def body(u_ref, mf_ref, mb_ref, sf_ref, sb_ref):
        for j in range(GBK):
            sf_ref[:, j, :] = jnp.dot(u_ref[j], mf_ref[j], preferred_element_type=f32)
            sb_ref[:, j, :] = jnp.dot(u_ref[j], mb_ref[j], preferred_element_type=f32)

    sspec = pl.BlockSpec((nc, GBK, SB), lambda i: (0, i, 0))
    return pl.pallas_call(
        body, name=name, grid=(S5G // GBK,), in_specs=[_gspec(nc, TCP), _gspec(TCP, SB), _gspec(TCP, SB)],
        out_specs=[sspec, sspec], out_shape=[jax.ShapeDtypeStruct((nc, S5G, SB), f32)] * 2,
        compiler_params=_CP(dimension_semantics=("parallel",)),
    )(u, mb_f, mb_b)


def _idx_fwd(nctx, nch):
    return lambda i: i


def _idx_rev(nctx, nch):
    return lambda i: jnp.where(i < nctx, nctx - 1 - i, nch + nctx - 1 - i)


def _s5_carry(s_f, s_b, a_f, a_b, nctx, name):
    nc = s_f.shape[0]
    idx_b = _idx_rev(nctx, nc)

    def body(sf_ref, sb_ref, f1_ref, f2_ref, b1_ref, b2_ref, hf_ref, hb_ref):
        f1, f2, b1, b2 = f1_ref[...], f2_ref[...], b1_ref[...], b2_ref[...]

        def step(i, c):
            hf, hfs, hb, hbs = c
            rb = idx_b(i)
            hf_ref[i] = hf
            hb_ref[rb] = hb
            sf, sb = sf_ref[i], sb_ref[rb]
            return (f1 * hf + f2 * hfs + sf, f1 * hfs - f2 * hf + pltpu.roll(sf, S5N, axis=1),
                    b1 * hb + b2 * hbs + sb, b1 * hbs - b2 * hb + pltpu.roll(sb, S5N, axis=1))

        z = jnp.zeros((S5G, SB), f32)
        lax.fori_loop(0, nc, step, (z, z, z, z), unroll=CARRY_UNROLL)

    return pl.pallas_call(body, name=name, out_shape=[jax.ShapeDtypeStruct(s_f.shape, f32)] * 2,
                          compiler_params=_CP())(s_f, s_b, *a_f, *a_b)


def _s5_carry_bwd(dhp, hp, a1, a2, rev, nctx, name):
    nc = hp.shape[0]
    idx = (_idx_rev if rev else _idx_fwd)(nctx, nc)

    def body(dhp_ref, hp_ref, a1_ref, a2_ref, ds_ref, d1_ref, d2_ref):
        f1, f2 = a1_ref[...], a2_ref[...]

        def step(k, carry):
            ab, abs_, d1, d2 = carry
            r = idx(nc - 1 - k)
            ds_ref[r] = ab
            h, dh = hp_ref[r], dhp_ref[r]
            return (dh + f1 * ab - f2 * abs_, pltpu.roll(dh, S5N, axis=1) + f1 * abs_ + f2 * ab,
                    d1 + ab * h, d2 + ab * pltpu.roll(h, S5N, axis=1))

        z = jnp.zeros((S5G, SB), f32)
        _, _, d1, d2 = lax.fori_loop(0, nc, step, (z, z, z, z), unroll=CARRY_UNROLL)
        d1_ref[...], d2_ref[...] = d1, d2

    return pl.pallas_call(
        body, name=name,
        out_shape=[jax.ShapeDtypeStruct(hp.shape, f32), jax.ShapeDtypeStruct((S5G, SB), f32), jax.ShapeDtypeStruct((S5G, SB), f32)],
        compiler_params=_CP())(dhp, hp, a1, a2)


def _s5_out(u, m_f, m_b, hp_f, hp_b, mc_f, mc_b, name):
    nc = u.shape[1]

    def body(u_ref, mf_ref, mb_ref, hf_ref, hb_ref, cf_ref, cb_ref, y_ref):
        for j in range(GBK):
            uj = u_ref[j]
            y_ref[j] = (jnp.dot(uj, mf_ref[j], preferred_element_type=f32) + jnp.dot(uj, mb_ref[j], preferred_element_type=f32)
                        + dnt(hf_ref[:, j, :], cf_ref[j]) + dnt(hb_ref[:, j, :], cb_ref[j])).astype(bf16)

    sspec = pl.BlockSpec((nc, GBK, SB), lambda i: (0, i, 0))
    return pl.pallas_call(
        body, name=name, grid=(S5G // GBK,),
        in_specs=[_gspec(nc, TCP), _gspec(TCP, TCP), _gspec(TCP, TCP), sspec, sspec, _gspec(TCP, SB), _gspec(TCP, SB)],
        out_specs=_gspec(nc, TCP), out_shape=jax.ShapeDtypeStruct((S5G, nc, TCP), bf16),
        compiler_params=_CP(dimension_semantics=("parallel",)),
    )(u, m_f, m_b, hp_f, hp_b, mc_f, mc_b)


def _s5_out_bwd(dy, u, m_f, m_b, hp_f, hp_b, mc_f, mc_b, name):
    nc = u.shape[1]

    def body(dy_ref, u_ref, mf_ref, mb_ref, hf_ref, hb_ref, cf_ref, cb_ref, du_ref, g_ref, dhf_ref, dhb_ref, dcf_ref, dcb_ref):
        for j in range(GBK):
            dyj = dy_ref[j]
            du_ref[j] = dnt(dyj, mf_ref[j]) + dnt(dyj, mb_ref[j])
            g_ref[j] = dtn(u_ref[j], dyj)
            dhf_ref[:, j, :] = dnn(dyj, cf_ref[j])
            dhb_ref[:, j, :] = dnn(dyj, cb_ref[j])
            dcf_ref[j] = dtn(dyj, hf_ref[:, j, :])
            dcb_ref[j] = dtn(dyj, hb_ref[:, j, :])

    sspec = pl.BlockSpec((nc, GBK, SB), lambda i: (0, i, 0))
    sshape = jax.ShapeDtypeStruct((nc, S5G, SB), f32)
    cshape = jax.ShapeDtypeStruct((S5G, TCP, SB), f32)
    return pl.pallas_call(
        body, name=name, grid=(S5G // GBK,),
        in_specs=[_gspec(nc, TCP), _gspec(nc, TCP), _gspec(TCP, TCP), _gspec(TCP, TCP), sspec, sspec, _gspec(TCP, SB), _gspec(TCP, SB)],
        out_specs=[_gspec(nc, TCP), _gspec(TCP, TCP), sspec, sspec, _gspec(TCP, SB), _gspec(TCP, SB)],
        out_shape=[jax.ShapeDtypeStruct((S5G, nc, TCP), f32), jax.ShapeDtypeStruct((S5G, TCP, TCP), f32), sshape, sshape, cshape, cshape],
        compiler_params=_CP(dimension_semantics=("parallel",)),
    )(dy, u, m_f, m_b, hp_f, hp_b, mc_f, mc_b)


def _s5_inc_bwd(du1, u, ds_f, ds_b, mb_f, mb_b, name):
    nc = u.shape[1]

    def body(du1_ref, u_ref, dsf_ref, dsb_ref, mf_ref, mb_ref, du_ref, dmf_ref, dmb_ref):
        for j in range(GBK):
            dsf, dsb = dsf_ref[:, j, :], dsb_ref[:, j, :]
            du_ref[j] = (du1_ref[j] + dnt(dsf, mf_ref[j]) + dnt(dsb, mb_ref[j])).astype(bf16)
            dmf_ref[j] = dtn(u_ref[j], dsf)
            dmb_ref[j] = dtn(u_ref[j], dsb)

    sspec = pl.BlockSpec((nc, GBK, SB), lambda i: (0, i, 0))
    cshape = jax.ShapeDtypeStruct((S5G, TCP, SB), f32)
    return pl.pallas_call(
        body, name=name, grid=(S5G // GBK,),
        in_specs=[_gspec(nc, TCP), _gspec(nc, TCP), sspec, sspec, _gspec(TCP, SB), _gspec(TCP, SB)],
        out_specs=[_gspec(nc, TCP), _gspec(TCP, SB), _gspec(TCP, SB)],
        out_shape=[jax.ShapeDtypeStruct((S5G, nc, TCP), bf16), cshape, cshape],
        compiler_params=_CP(dimension_semantics=("parallel",)),
    )(du1, u, ds_f, ds_b, mb_f, mb_b)


def _to_groups(a):
    n = a.shape[0]
    return a.reshape(n // TC, TC, S5G, S5P).transpose(2, 0, 1, 3).reshape(S5G, n // TC, TCP)


def _from_groups(a):
    nc = a.shape[1]
    return a.reshape(S5G, nc, TC, S5P).transpose(1, 2, 0, 3).reshape(nc * TC, S5W)


def _swap_pairs(t):
    lane = lax.broadcasted_iota(jnp.int32, t.shape, 1)
    return jnp.where(lane % 2 == 0, pltpu.roll(t, DH - 1, axis=1), pltpu.roll(t, 1, axis=1))


def _rot(t, cosf, sins):
    return t * cosf + _swap_pairs(t) * sins


def _rot_t(d, cosf, sins):
    return d * cosf - _swap_pairs(d) * sins


def _ret_chunk(qr, kr, v, rp, ld, rev):
    pos = lax.broadcasted_iota(jnp.int32, (T, 1), 0).astype(f32)
    diff = pos - lax.broadcasted_iota(jnp.int32, (1, T), 1).astype(f32)
    if rev:
        keep, dist = diff < 0, jnp.maximum(-diff, 0.0)
        xi, zeta = jnp.exp(ld * (T - pos)), jnp.exp(ld * pos)
    else:
        keep, dist = diff >= 0, jnp.maximum(diff, 0.0)
        xi, zeta = jnp.exp(ld * (pos + 1.0)), jnp.exp(ld * (T - 1.0 - pos))
    dm = jnp.where(keep, jnp.exp(ld * dist), 0.0)
    out = dnn(dnt(qr, kr) * dm, v) + dnn(qr * xi, rp)
    rn = jnp.exp(ld * float(T)) * rp + dtn(kr * zeta, v)
    return out, rn


def _ret_fwd(p_ext, ld8, rev, nctx, name, cargo=None):
    n = p_ext.shape[0]
    nch = n // T
    idx = (_idx_rev if rev else _idx_fwd)(nctx, nch)
    cg = _Cargo(cargo)

    def body(*refs):
        (q_ref, k_ref, v_ref, ld_ref), (o_ref, rp_ref), (r_s,) = cg.split(refs, 4, 2, 1)
        cg.ride(refs, 4, 2, nch)

        @pl.when(pl.program_id(0) == 0)
        def _():
            r_s[...] = jnp.zeros_like(r_s)

        for h in range(RH):
            sl = slice(h * DH, (h + 1) * DH)
            rp = r_s[h]
            rp_ref[0, h] = rp
            out, rn = _ret_chunk(q_ref[:, sl].astype(f32), k_ref[:, sl].astype(f32), v_ref[:, sl].astype(f32), rp,
                                 ld_ref[h:h + 1, 0:1], rev)
            r_s[h] = rn
            o_ref[:, sl] = out

    def colspec(cb):
        return pl.BlockSpec((T, RW), lambda i, _c=cb: (idx(i), _c))

    return pl.pallas_call(
        body, name=name, grid=(nch,),
        in_specs=[colspec(1), colspec(2), colspec(3), _const_spec((8, 128))] + cg.in_specs(),
        out_specs=[pl.BlockSpec((T, RW), lambda i: (idx(i), 0)), pl.BlockSpec((1, RH, DH, DH), lambda i: (i, 0, 0, 0))] + cg.in_specs(),
        out_shape=[jax.ShapeDtypeStruct((n, RW), f32), jax.ShapeDtypeStruct((nch, RH, DH, DH), f32)] + cg.out_shapes(),
        scratch_shapes=[pltpu.VMEM((RH, DH, DH), f32)] + cg.sems(),
        compiler_params=_CP(dimension_semantics=_ARB),
    )(p_ext, p_ext, p_ext, ld8, *cg.arrays)


def _ret_bwd(p_ext, ld8, rprev, do_ext, rev, nctx, name, cargo=None):
    n = p_ext.shape[0]
    nch = n // T
    idx0 = (_idx_rev if rev else _idx_fwd)(nctx, nch)
    idx = lambda j: idx0(nch - 1 - j)
    cg = _Cargo(cargo)

    def body(*refs):
        ins, (dq_ref, dk_ref, dv_ref, dld_ref), (dr_s,) = cg.split(refs, 6, 4, 1)
        q_ref, k_ref, v_ref, ld_ref, rp_ref, do_ref = ins
        cg.ride(refs, 6, 4, nch)

        @pl.when(pl.program_id(0) == 0)
        def _():
            dr_s[...] = jnp.zeros_like(dr_s)
            dld_ref[...] = jnp.zeros_like(dld_ref)

        for h in range(RH):
            sl = slice(h * DH, (h + 1) * DH)
            _, vjp = jax.vjp(functools.partial(_ret_chunk, rev=rev), q_ref[:, sl].astype(f32), k_ref[:, sl].astype(f32),
                             v_ref[:, sl].astype(f32), rp_ref[0, h], ld_ref[h:h + 1, 0:1])
            dqr, dkr, dv, drp, dld = vjp((do_ref[:, sl], dr_s[h]))
            dr_s[h] = drp
            dq_ref[:, sl], dk_ref[:, sl], dv_ref[:, sl] = dqr, dkr, dv
            dld_ref[h:h + 1, :] += jnp.broadcast_to(dld, (1, 128))

    def colspec(cb):
        return pl.BlockSpec((T, RW), lambda j, _c=cb: (idx(j), _c))

    ospec = pl.BlockSpec((T, RW), lambda j: (idx(j), 0))
    oshape = jax.ShapeDtypeStruct((n, RW), f32)
    return pl.pallas_call(
        body, name=name, grid=(nch,),
        in_specs=[colspec(1), colspec(2), colspec(3), _const_spec((8, 128)),
                  pl.BlockSpec((1, RH, DH, DH), lambda j: (nch - 1 - j, 0, 0, 0)), ospec] + cg.in_specs(),
        out_specs=[ospec, ospec, ospec, _acc_spec((8, 128))] + cg.in_specs(),
        out_shape=[oshape, oshape, oshape, jax.ShapeDtypeStruct((8, 128), f32)] + cg.out_shapes(),
        scratch_shapes=[pltpu.VMEM((RH, DH, DH), f32)] + cg.sems(),
        compiler_params=_CP(dimension_semantics=_ARB),
    )(p_ext, p_ext, p_ext, ld8, rprev, do_ext, *cg.arrays)


def _qk_heads(p, fn_q, fn_k):
    heads = lambda base, fn: [fn(p[:, base + h * DH:base + (h + 1) * DH]) for h in range(RH)]
    return jnp.concatenate([p[:, :S5W]] + heads(S5W, fn_q) + heads(S5W + RW, fn_k) + [p[:, S5W + 2 * RW:]], axis=1)


def _f1_fwd(x, ctx, modx, modc, nw1, w_in_n, cosf, sins, name, cargo=None):
    L = x.shape[0]
    nb = L // R + 1
    scale = DH ** -0.5
    cg = _Cargo(cargo)

    def body(*refs):
        (x_ref, c_ref, mx_ref, mc_ref, nw_ref, w_ref, cos_ref, sin_ref), (p_ref,), _ = cg.split(refs, 8, 1, 0)
        cg.ride(refs, 8, 1, nb)
        is_ctx = pl.program_id(0) == 0
        xin = jnp.where(is_ctx, c_ref[...], x_ref[...])
        sh = jnp.where(is_ctx, mc_ref[0:1], mx_ref[0:1])
        sc = jnp.where(is_ctx, mc_ref[1:2], mx_ref[1:2])
        cf, ss = cos_ref[...], sin_ref[...]
        p = dnn(_mod(_rms(xin, nw_ref[...]), sh, sc), w_ref[...])
        p_ref[...] = _qk_heads(p, lambda t: _rot(t, cf, ss), lambda t: _rot(t * scale, cf, ss)).astype(bf16)

    return pl.pallas_call(
        body, name=name, grid=(nb,),
        in_specs=[pl.BlockSpec((R, D), lambda i: (jnp.maximum(i - 1, 0), 0)), _const_spec((R, D)), _const_spec((6, D)),
                  _const_spec((6, D)), _const_spec((1, D)), _const_spec((D, INC)), pl.BlockSpec((R, DH), lambda i: (i, 0)),
                  pl.BlockSpec((R, DH), lambda i: (i, 0))] + cg.in_specs(),
        out_specs=[pl.BlockSpec((R, INC), lambda i: (i, 0))] + cg.in_specs(),
        out_shape=[jax.ShapeDtypeStruct((L + R, INC), bf16)] + cg.out_shapes(),
        scratch_shapes=cg.sems(),
        compiler_params=_CP(dimension_semantics=_ARB),
    )(x, ctx, modx, modc, nw1, w_in_n, cosf, sins, *cg.arrays)


def _f1_bwd(x, ctx, modx, modc, nw1, w_in_t, cosf, sins, dx1, parts, name):
    L = x.shape[0]
    nb = L // R + 1
    scale = DH ** -0.5

    def body(x_ref, c_ref, mx_ref, mc_ref, nw_ref, w_ref, cos_ref, sin_ref, dx1_ref, du0, du1, dq0, dq1, dk0, dk1, dv0, dv1, dg0,
             gx_ref, dp_ref, h1_ref, dnw_ref, dmx_ref, dmc_ref):
        i = pl.program_id(0)
        is_ctx = i == 0

        @pl.when(is_ctx)
        def _():
            dnw_ref[...] = jnp.zeros_like(dnw_ref)
            dmx_ref[...] = jnp.zeros_like(dmx_ref)
            dmc_ref[...] = jnp.zeros_like(dmc_ref)

        cf, ss = cos_ref[...], sin_ref[...]
        dp = jnp.concatenate([du0[...].astype(f32) + du1[...], dq0[...] + dq1[...], dk0[...] + dk1[...], dv0[...] + dv1[...],
                              dg0[...]], axis=1)
        dp = _qk_heads(dp, lambda t: _rot_t(t, cf, ss), lambda t: _rot_t(t, cf, ss) * scale).astype(bf16)
        dp_ref[...] = dp
        xin = jnp.where(is_ctx, c_ref[...], x_ref[...])
        sh = jnp.where(is_ctx, mc_ref[0:1], mx_ref[0:1])
        sc = jnp.where(is_ctx, mc_ref[1:2], mx_ref[1:2])
        dh = dnn(dp, w_ref[...])
        h, vjp = jax.vjp(lambda a, b, c, d: _mod(_rms(a, b), c, d), xin, nw_ref[...], sh, sc)
        dxin, dnw, dsh, dsc = vjp(dh)
        h1_ref[...] = h.astype(bf16)
        gx_ref[...] = dx1_ref[...] + dxin
        dnw_ref[...] += dnw
        wx = jnp.where(is_ctx, 0.0, 1.0)
        dmx_ref[0:1] += dsh * wx
        dmx_ref[1:2] += dsc * wx
        dmc_ref[0:1] += dsh * (1.0 - wx)
        dmc_ref[1:2] += dsc * (1.0 - wx)

    lat = pl.BlockSpec((R, D), lambda i: (jnp.maximum(i - 1, 0), 0))
    ext = pl.BlockSpec((R, S5W), lambda i: (i, 0))
    return pl.pallas_call(
        body, name=name, grid=(nb,),
        in_specs=[lat, _const_spec((R, D)), _const_spec((6, D)), _const_spec((6, D)), _const_spec((1, D)), _const_spec((INC, D)),
                  pl.BlockSpec((R, DH), lambda i: (i, 0)), pl.BlockSpec((R, DH), lambda i: (i, 0)), lat] + [ext] * 9,
        out_specs=[lat, pl.BlockSpec((R, INC), lambda i: (i, 0)), pl.BlockSpec((R, D), lambda i: (i, 0)),
                   _acc_spec((1, D)), _acc_spec((6, D)), _acc_spec((6, D))],
        out_shape=[jax.ShapeDtypeStruct((L, D), f32), jax.ShapeDtypeStruct((L + R, INC), bf16),
                   jax.ShapeDtypeStruct((L + R, D), bf16), jax.ShapeDtypeStruct((1, D), f32),
                   jax.ShapeDtypeStruct((6, D), f32), jax.ShapeDtypeStruct((6, D), f32)],
        compiler_params=_CP(dimension_semantics=_ARB),
    )(x, ctx, modx, modc, nw1, w_in_t, cosf, sins, dx1, *parts)


def _ret_post(yr, g):
    outs = []
    for h in range(RH):
        yh = yr[:, h * DH:(h + 1) * DH]
        mu = jnp.mean(yh, axis=-1, keepdims=True)
        var = jnp.mean((yh - mu) ** 2, axis=-1, keepdims=True)
        outs.append((yh - mu) * lax.rsqrt(var + EPS))
    return jax.nn.silu(g) * jnp.concatenate(outs, axis=1)


def _mix_fn(ys, u, of, ob, g, x, dvec, bglu, gate1, pz, pm, wglu, wout):
    s = jax.nn.gelu(ys + dvec * u)
    z = dnn(s, wglu) + bglu + pz
    cat = jnp.concatenate([s * jax.nn.sigmoid(z), _ret_post(of + ob, g)], axis=1)
    mix = dnn(cat, wout) + pm
    return x + gate1 * mix, (s, cat)


def _mix_fwd(x, ys, of, ob, p_ext, dvec, bglu, modx, wglu, wout, name, cargo=None):
    L = x.shape[0]
    nb = L // R
    cg = _Cargo(cargo)

    def body(*refs):
        ins, (x1_ref,), _ = cg.split(refs, 11, 1, 0)
        x_ref, ys_ref, of_ref, ob_ref, u_ref, g_ref, d_ref, b_ref, mx_ref, wg_ref, wo_ref = ins
        cg.ride(refs, 11, 1, nb)
        x1_ref[...] = _mix_fn(ys_ref[...].astype(f32), u_ref[...].astype(f32), of_ref[...], ob_ref[...], g_ref[...].astype(f32),
                              x_ref[...], d_ref[...], b_ref[...], mx_ref[2:3], 0.0, 0.0, wg_ref[...], wo_ref[...])[0]

    ext = pl.BlockSpec((R, S5W), lambda i: (i + 1, 0))
    return pl.pallas_call(
        body, name=name, grid=(nb,),
        in_specs=[pl.BlockSpec((R, D), lambda i: (i, 0)), ext, ext, ext, ext, pl.BlockSpec((R, RW), lambda i: (i + 1, 4)),
                  _const_spec((1, S5W)), _const_spec((1, S5W)), _const_spec((6, D)), _const_spec((S5W, S5W)), _const_spec((D, D))]
        + cg.in_specs(),
        out_specs=[pl.BlockSpec((R, D), lambda i: (i, 0))] + cg.in_specs(),
        out_shape=[jax.ShapeDtypeStruct((L, D), f32)] + cg.out_shapes(),
        scratch_shapes=cg.sems(),
        compiler_params=_CP(dimension_semantics=_ARB),
    )(x, ys, of, ob, p_ext, p_ext, dvec, bglu, modx, wglu, wout, *cg.arrays)


def _mix_bwd(x, ys, of, ob, p_ext, dvec, bglu, modx, wglu, wout, dx1, name, cargo=None):
    L = x.shape[0]
    nb = L // R + 1
    cg = _Cargo(cargo)

    def body(*refs):
        ins, outs, _ = cg.split(refs, 12, 11, 0)
        x_ref, ys_ref, of_ref, ob_ref, u_ref, g_ref, d_ref, b_ref, mx_ref, wg_ref, wo_ref, dx1_ref = ins
        dy_ref, dud_ref, do_ref, dg_ref, cat_ref, dmix_ref, s_ref, dz_ref, dd_ref, db_ref, dg1_ref = outs
        cg.ride(refs, 12, 11, nb)
        i = pl.program_id(0)

        @pl.when(i == 0)
        def _():
            for r in outs:
                r[...] = jnp.zeros_like(r)

        @pl.when(i > 0)
        def _():
            fn = lambda ys_, u_, of_, g_, d_, b_, g1_, pz_, pm_: _mix_fn(
                ys_, u_, of_, ob_ref[...], g_, x_ref[...], d_, b_, g1_, pz_, pm_, wg_ref[...], wo_ref[...])
            _, vjp, (s, cat) = jax.vjp(fn, ys_ref[...].astype(f32), u_ref[...].astype(f32), of_ref[...], g_ref[...].astype(f32), d_ref[...],
                                       b_ref[...], mx_ref[2:3], jnp.zeros((R, S5W), f32), jnp.zeros((R, D), f32), has_aux=True)
            dy, dud, do, dg, dd, db, dg1, dz, dmix = vjp(dx1_ref[...])
            dy_ref[...], dud_ref[...], do_ref[...], dg_ref[...] = dy.astype(bf16), dud, do, dg
            cat_ref[...], dmix_ref[...] = cat.astype(bf16), dmix.astype(bf16)
            s_ref[...], dz_ref[...] = s.astype(bf16), dz.astype(bf16)
            dd_ref[...] += dd
            db_ref[...] += db
            dg1_ref[...] += dg1

    lat = pl.BlockSpec((R, D), lambda i: (jnp.maximum(i - 1, 0), 0))
    lat5 = pl.BlockSpec((R, S5W), lambda i: (jnp.maximum(i - 1, 0), 0))
    ext = pl.BlockSpec((R, S5W), lambda i: (i, 0))
    eshape = jax.ShapeDtypeStruct((L + R, S5W), f32)
    return pl.pallas_call(
        body, name=name, grid=(nb,),
        in_specs=[lat, ext, ext, ext, ext, pl.BlockSpec((R, RW), lambda i: (i, 4)),
                  _const_spec((1, S5W)), _const_spec((1, S5W)), _const_spec((6, D)), _const_spec((S5W, S5W)), _const_spec((D, D)), lat]
        + cg.in_specs(),
        out_specs=[ext, ext, ext, ext, lat, lat, lat5, lat5, _acc_spec((1, S5W)), _acc_spec((1, S5W)), _acc_spec((1, D))]
        + cg.in_specs(),
        out_shape=[jax.ShapeDtypeStruct((L + R, S5W), bf16), eshape, eshape, eshape, jax.ShapeDtypeStruct((L, D), bf16),
                   jax.ShapeDtypeStruct((L, D), bf16), jax.ShapeDtypeStruct((L, S5W), bf16), jax.ShapeDtypeStruct((L, S5W), bf16),
                   jax.ShapeDtypeStruct((1, S5W), f32), jax.ShapeDtypeStruct((1, S5W), f32), jax.ShapeDtypeStruct((1, D), f32)]
        + cg.out_shapes(),
        scratch_shapes=cg.sems(),
        compiler_params=_CP(dimension_semantics=_ARB),
    )(x, ys, of, ob, p_ext, p_ext, dvec, bglu, modx, wglu, wout, dx1, *cg.arrays)


def _ffn_tail(gc, a, x1, gate2, fnw, pf, wdown, wdown_t, tgt):
    f = jax.nn.gelu(gc) * a
    ffn = _dnn_const(f, wdown, wdown_t) + pf
    y = _rms(x1 + gate2 * ffn, fnw)
    err = y - tgt
    loss = 0.5 * jnp.sum(jnp.mean(err * err, axis=-1, keepdims=True), axis=0, keepdims=True)
    return loss, f


def _ffn_fwd(x1, tgt, nw2, modx, w_a, w_g, cw, cb, wdown, wdown_t, fnw, name):
    L = x1.shape[0]
    nb = L // RF
    per = RF // HALO

    def body(x_ref, xp_ref, xn_ref, t_ref, nw_ref, mx_ref, wa_ref, wg_ref, cw_ref, cb_ref, wd_ref, wdt_ref, fn_ref,
             dx2_ref, da_ref, dgc_ref, f_ref, dffn_ref, loss_ref, dfn_ref, dg2_ref, dcb_ref, dcw_ref):
        i = pl.program_id(0)

        @pl.when(i == 0)
        def _():
            for r in (loss_ref, dfn_ref, dg2_ref, dcb_ref, dcw_ref):
                r[...] = jnp.zeros_like(r)

        nw, sh, sc, gate2 = nw_ref[...], mx_ref[3:4], mx_ref[4:5], mx_ref[5:6]
        x1b = x_ref[...]
        h2 = _mod(_rms(x1b, nw), sh, sc)
        h2e = jnp.concatenate([_mod(_rms(xp_ref[...], nw), sh, sc), h2, _mod(_rms(xn_ref[...], nw), sh, sc)], axis=0)
        a = dnn(h2, wa_ref[...])
        ge = dnn(h2e, wg_ref[...])
        g = ge[HALO:HALO + RF]
        gp = ge[HALO - 1:HALO] * jnp.where(i > 0, 1.0, 0.0)
        gn = ge[HALO + RF:HALO + RF + 1] * jnp.where(i < nb - 1, 1.0, 0.0)
        row = lax.broadcasted_iota(jnp.int32, (RF, 1), 0)
        g_prev = jnp.where(row == 0, gp, pltpu.roll(g, 1, axis=0))
        g_next = jnp.where(row == RF - 1, gn, pltpu.roll(g, RF - 1, axis=0))
        gc = cb_ref[...] + g_prev * cw_ref[0:1] + g * cw_ref[1:2] + g_next * cw_ref[2:3]
        fn = lambda gc_, a_, x_, g2_, fw_, pf_: _ffn_tail(gc_, a_, x_, g2_, fw_, pf_, wd_ref[...], wdt_ref[...], t_ref[...])
        loss, vjp, f = jax.vjp(fn, gc, a, x1b, gate2, fn_ref[...], jnp.zeros((RF, D), f32), has_aux=True)
        dgc, da, dx2, dg2, dfw, dffn = vjp(jnp.ones((1, 1), f32))
        dx2_ref[...] = dx2
        da_ref[...], dgc_ref[...] = da.astype(bf16), dgc
        f_ref[...], dffn_ref[...] = f.astype(bf16), dffn.astype(bf16)
        loss_ref[...] += jnp.broadcast_to(loss, (1, 128))
        dfn_ref[...] += dfw
        dg2_ref[...] += dg2
        dcb_ref[...] += jnp.sum(dgc, axis=0, keepdims=True)
        dcw_ref[0:1] += jnp.sum(dgc * g_prev, axis=0, keepdims=True)
        dcw_ref[1:2] += jnp.sum(dgc * g, axis=0, keepdims=True)
        dcw_ref[2:3] += jnp.sum(dgc * g_next, axis=0, keepdims=True)

    blk = lambda w: pl.BlockSpec((RF, w), lambda i: (i, 0))
    return pl.pallas_call(
        body, name=name, grid=(nb,),
        in_specs=[blk(D), pl.BlockSpec((HALO, D), lambda i: (jnp.maximum(i * per - 1, 0), 0)),
                  pl.BlockSpec((HALO, D), lambda i: (jnp.minimum((i + 1) * per, L // HALO - 1), 0)), blk(D),
                  _const_spec((1, D)), _const_spec((6, D)), _const_spec((D, DFF)), _const_spec((D, DFF)), _const_spec((3, DFF)),
                  _const_spec((1, DFF)), _const_spec((DFF, D)), _const_spec((D, DFF)), _const_spec((1, D))],
        out_specs=[blk(D), blk(DFF), blk(DFF), blk(DFF), blk(D), _acc_spec((1, 128)), _acc_spec((1, D)), _acc_spec((1, D)),
                   _acc_spec((1, DFF)), _acc_spec((3, DFF))],
        out_shape=[jax.ShapeDtypeStruct((L, D), f32), jax.ShapeDtypeStruct((L, DFF), bf16), jax.ShapeDtypeStruct((L, DFF), f32),
                   jax.ShapeDtypeStruct((L, DFF), bf16), jax.ShapeDtypeStruct((L, D), bf16), jax.ShapeDtypeStruct((1, 128), f32),
                   jax.ShapeDtypeStruct((1, D), f32), jax.ShapeDtypeStruct((1, D), f32), jax.ShapeDtypeStruct((1, DFF), f32),
                   jax.ShapeDtypeStruct((3, DFF), f32)],
        compiler_params=_CP(dimension_semantics=_ARB),
    )(x1, x1, x1, tgt, nw2, modx, w_a, w_g, cw, cb, wdown, wdown_t, fnw)


def _ffn_bwd(x1, dx2, da, dgc, nw2, modx, wup_t, cw, name):
    L = x1.shape[0]
    nb = L // RF
    per = RF // HALO

    def body(x_ref, dx2_ref, da_ref, dgc_ref, dgp_ref, dgn_ref, nw_ref, mx_ref, wu_ref, cw_ref,
             dx1_ref, dag_ref, h2_ref, dnw_ref, dmx_ref):
        i = pl.program_id(0)

        @pl.when(i == 0)
        def _():
            dnw_ref[...] = jnp.zeros_like(dnw_ref)
            dmx_ref[...] = jnp.zeros_like(dmx_ref)

        dgc_b = dgc_ref[...]
        before = dgp_ref[HALO - 1:HALO] * jnp.where(i > 0, 1.0, 0.0)
        after = dgn_ref[0:1] * jnp.where(i < nb - 1, 1.0, 0.0)
        row = lax.broadcasted_iota(jnp.int32, (RF, 1), 0)
        d_prev = jnp.where(row == 0, before, pltpu.roll(dgc_b, 1, axis=0))
        d_next = jnp.where(row == RF - 1, after, pltpu.roll(dgc_b, RF - 1, axis=0))
        dg = cw_ref[0:1] * d_next + cw_ref[1:2] * dgc_b + cw_ref[2:3] * d_prev
        dag = jnp.concatenate([da_ref[...], dg.astype(bf16)], axis=1)
        dag_ref[...] = dag
        dh2 = dnn(dag, wu_ref[...])
        h2, vjp = jax.vjp(lambda a, b, c, d: _mod(_rms(a, b), c, d), x_ref[...], nw_ref[...], mx_ref[3:4], mx_ref[4:5])
        dxa, dnw, dsh, dsc = vjp(dh2)
        h2_ref[...] = h2.astype(bf16)
        dx1_ref[...] = dx2_ref[...] + dxa
        dnw_ref[...] += dnw
        dmx_ref[3:4] += dsh
        dmx_ref[4:5] += dsc

    blk = lambda w: pl.BlockSpec((RF, w), lambda i: (i, 0))
    return pl.pallas_call(
        body, name=name, grid=(nb,),
        in_specs=[blk(D), blk(D), blk(DFF), blk(DFF), pl.BlockSpec((HALO, DFF), lambda i: (jnp.maximum(i * per - 1, 0), 0)),
                  pl.BlockSpec((HALO, DFF), lambda i: (jnp.minimum((i + 1) * per, L // HALO - 1), 0)),
                  _const_spec((1, D)), _const_spec((6, D)), _const_spec((2 * DFF, D)), _const_spec((3, DFF))],
        out_specs=[blk(D), blk(2 * DFF), blk(D), _acc_spec((1, D)), _acc_spec((6, D))],
        out_shape=[jax.ShapeDtypeStruct((L, D), f32), jax.ShapeDtypeStruct((L, 2 * DFF), bf16), jax.ShapeDtypeStruct((L, D), bf16),
                   jax.ShapeDtypeStruct((1, D), f32), jax.ShapeDtypeStruct((6, D), f32)],
        compiler_params=_CP(dimension_semantics=_ARB),
    )(x1, dx2, da, dgc, dgc, dgc, nw2, modx, wup_t, cw)


def _matmul_tn(a, b, name):
    k, m = a.shape
    n = b.shape[1]
    divs = lambda d: [c for c in range(d, 0, -128) if d % c == 0]
    _, tm, tn = min((m * (n // cn) + n * (m // cm), cm, cn) for cm in divs(m) for cn in divs(n) if cm * cn * 4 <= ACC_TILE_BYTES)
    tk = next(c for c in (512, 768, 256, 128) if k % c == 0)
    nk = k // tk

    def body(a_ref, b_ref, o_ref, acc):
        q = pl.program_id(2)

        @pl.when(q == 0)
        def _():
            acc[...] = jnp.zeros_like(acc)

        acc[...] += dtn(a_ref[...], b_ref[...])

        @pl.when(q == nk - 1)
        def _():
            o_ref[...] = acc[...].astype(bf16)

    return pl.pallas_call(
        body, name=name, grid=(m // tm, n // tn, nk),
        in_specs=[pl.BlockSpec((tk, tm), lambda i, j, q: (q, i)), pl.BlockSpec((tk, tn), lambda i, j, q: (q, j))],
        out_specs=pl.BlockSpec((tm, tn), lambda i, j, q: (i, j)),
        out_shape=jax.ShapeDtypeStruct((m, n), bf16),
        scratch_shapes=[pltpu.VMEM((tm, tn), f32)],
        compiler_params=_CP(dimension_semantics=("parallel", "parallel", "arbitrary")),
    )(a, b)


def _adamw(w, g, m, v, name):
    c1, c2 = 1.0 - B1 ** STEP, 1.0 - B2 ** STEP

    def body(w_ref, g_ref, m_ref, v_ref, d_ref, nm_ref, nv_ref):
        gg = g_ref[...]
        nm = B1 * m_ref[...] + (1.0 - B1) * gg
        nv = B2 * v_ref[...] + (1.0 - B2) * jnp.square(gg)
        d_ref[...] = -LR * ((nm / c1) / (jnp.sqrt(nv / c2) + AEPS) + WD * w_ref[...])
        nm_ref[...], nv_ref[...] = nm, nv

    return pl.pallas_call(body, name=name, out_shape=[jax.ShapeDtypeStruct(w.shape, f32)] * 3, compiler_params=_CP())(w, g, m, v)


SMALL = ["conv_w", "c_ctx", "norm1_w", "s5_lambda_re_f", "s5_lambda_im_f", "s5_log_step_f", "s5_lambda_re_b", "s5_lambda_im_b",
         "s5_log_step_b", "s5_b_re", "s5_b_im", "s5_c_re", "s5_c_im", "s5_d", "s5_b_glu", "ret_log_decay_f", "ret_log_decay_b",
         "norm2_w", "conv_b", "final_norm_w"]
WEIGHTS = ["c_ctx", "w_mod", "b_mod", "norm1_w", "w_in", "s5_lambda_re_f", "s5_lambda_im_f", "s5_log_step_f", "s5_lambda_re_b",
           "s5_lambda_im_b", "s5_log_step_b", "s5_b_re", "s5_b_im", "s5_c_re", "s5_c_im", "s5_d", "s5_w_glu", "s5_b_glu",
           "ret_log_decay_f", "ret_log_decay_b", "w_out", "norm2_w", "w_up", "conv_w", "conv_b", "w_down", "final_norm_w"]


def _pack_small(vals):
    flat, offs, o = [], [], 0
    for a in vals:
        n = a.size
        npad = -n % 128
        flat.append(jnp.pad(a.reshape(-1), (0, npad)))
        offs.append((o, n))
        o += n + npad
    tail = -o % 1024
    if tail:
        flat.append(jnp.zeros((tail,), f32))
    return jnp.concatenate(flat).reshape(-1, 128), offs


def _unpack_small(packed, offs, shapes):
    flat = packed.reshape(-1)
    return [flat[o:o + n].reshape(s) for (o, n), s in zip(offs, shapes)]


def _rope_tables(L, nctx_rows):
    t = np.arange(L)
    inv = (ROPE_THETA ** (-np.arange(DH // 4, dtype=np.float64) / (DH // 4))).astype(np.float32)
    ang = np.concatenate([(t // GRID_W).astype(np.float32)[:, None] * inv, (t % GRID_W).astype(np.float32)[:, None] * inv], axis=-1)
    cos = np.repeat(np.cos(ang).astype(np.float32), 2, axis=1)
    sin = np.repeat(np.sin(ang).astype(np.float32), 2, axis=1) * np.tile(np.array([-1.0, 1.0], np.float32), DH // 2)
    cosf = np.concatenate([np.ones((nctx_rows, DH), np.float32), cos], axis=0)
    sins = np.concatenate([np.zeros((nctx_rows, DH), np.float32), sin], axis=0)
    return jnp.asarray(cosf), jnp.asarray(sins)


def kernel(x, c, ctx, c_ctx, w_mod, b_mod, norm1_w, w_in, s5_lambda_re_f, s5_lambda_im_f, s5_log_step_f, s5_lambda_re_b, s5_lambda_im_b, s5_log_step_b, s5_b_re, s5_b_im, s5_c_re, s5_c_im, s5_d, s5_w_glu, s5_b_glu, ret_log_decay_f, ret_log_decay_b, w_out, norm2_w, w_up, conv_w, conv_b, w_down, final_norm_w, loss_target, m_c_ctx, m_w_mod, m_b_mod, m_norm1_w, m_w_in, m_s5_lambda_re_f, m_s5_lambda_im_f, m_s5_log_step_f, m_s5_lambda_re_b, m_s5_lambda_im_b, m_s5_log_step_b, m_s5_b_re, m_s5_b_im, m_s5_c_re, m_s5_c_im, m_s5_d, m_s5_w_glu, m_s5_b_glu, m_ret_log_decay_f, m_ret_log_decay_b, m_w_out, m_norm2_w, m_w_up, m_conv_w, m_conv_b, m_w_down, m_final_norm_w, v_c_ctx, v_w_mod, v_b_mod, v_norm1_w, v_w_in, v_s5_lambda_re_f, v_s5_lambda_im_f, v_s5_log_step_f, v_s5_lambda_re_b, v_s5_lambda_im_b, v_s5_log_step_b, v_s5_b_re, v_s5_b_im, v_s5_c_re, v_s5_c_im, v_s5_d, v_s5_w_glu, v_s5_b_glu, v_ret_log_decay_f, v_ret_log_decay_b, v_w_out, v_norm2_w, v_w_up, v_conv_w, v_conv_b, v_w_down, v_final_norm_w):
    args = dict(locals())
    W = {n: args[n] for n in WEIGHTS}
    M = {n: args["m_" + n] for n in WEIGHTS}
    V = {n: args["v_" + n] for n in WEIGHTS}
    me = _me()
    x2, ctx2, tgt = x[0], ctx[0], loss_target[0]
    L, Lc = x2.shape[0], ctx2.shape[0]
    assert Lc == R and L % R == 0 and L % GRID_W == 0
    nctx = Lc // T

    c_all = _all_gather_small(jnp.pad(c, ((0, 7), (0, 0))), "gather_c")[:, 0, :]
    c9 = jnp.concatenate([c_all, c_ctx[None], jnp.zeros((7, D), f32)], axis=0)
    w_mod_l = w_mod[0]
    ncol = w_mod_l.shape[1]
    m_part = _ada_fwd(c9, w_mod_l, "ada_fwd")
    m_all = _all_gather_small(m_part, "gather_mod").transpose(1, 0, 2).reshape(16, 6, D)
    modx, modc = _mod_select(m_all, b_mod.reshape(6, D), "mod_select")

    w_in_tl, w_up_tl = w_in[0].T.astype(bf16), w_up[0].T.astype(bf16)
    w_out_l, w_down_l, w_glu_l = w_out[0].astype(bf16), w_down[0].astype(bf16), s5_w_glu[0].astype(bf16)
    half_up = w_up_tl.shape[0] // 2
    (w_in_g,) = _exchange([w_in_tl], False, "gather_w_in")
    w_in_t = w_in_g.reshape(INC, D)
    per_cv = conv_w.shape[2]
    conv_pad = jnp.pad(conv_w[0], ((0, 5), (0, 128 * 3 - per_cv)))
    conv_f = _all_gather_small(conv_pad, "gather_conv")[:, :3, :per_cv].transpose(1, 0, 2).reshape(3, DFF)

    pair = lambda a, b: jnp.concatenate([a, b], axis=-1)
    bre_g, bim_g = s5_b_re[0].transpose(0, 2, 1), s5_b_im[0].transpose(0, 2, 1)
    cre_g, cim_g = s5_c_re[0], s5_c_im[0]
    shared = (pair(bre_g, bim_g), pair(bim_g, bre_g), pair(cre_g, cim_g), pair(cim_g, cre_g))
    s5p = {}
    for tag, lre, lim, ls in (("f", s5_lambda_re_f, s5_lambda_im_f, s5_log_step_f), ("b", s5_lambda_re_b, s5_lambda_im_b, s5_log_step_b)):
        s5p[tag] = (pair(lre[0], lre[0])[:, None, :], pair(lim[0], lim[0])[:, None, :], ls[0].reshape(S5G, 1, 1)) + shared
    m_f, mb_f, mc_f, a1_f, a2_f = _s5_build(s5p["f"], False, "s5_build_f")
    m_b, mb_b, mc_b, a1_b, a2_b = _s5_build(s5p["b"], True, "s5_build_b")
    a1_f, a2_f, a1_b, a2_b = (a.reshape(S5G, SB) for a in (a1_f, a2_f, a1_b, a2_b))

    nw1, nw2, fnw = norm1_w, norm2_w, final_norm_w[None]
    cosf, sins = _rope_tables(L, Lc)
    p_ext, w_out_g, w_glu_g = _f1_fwd(x2, ctx2, modx, modc, nw1, w_in_t.T, cosf, sins, "f1_fwd", cargo=([w_out_l, w_glu_l], False))
    nctx5 = Lc // TC
    u_g = _to_groups(p_ext[:, :S5W])
    s_f, s_b = _s5_inc(u_g, mb_f, mb_b, "s5_inc")
    hp_f, hp_b = _s5_carry(s_f, s_b, (a1_f, a2_f), (a1_b, a2_b), nctx5, "s5_carry")
    ys = _from_groups(_s5_out(u_g, m_f, m_b, hp_f, hp_b, mc_f, mc_b, "s5_out"))
    ld8 = lambda ld: jnp.pad(jnp.broadcast_to(ld[0][:, None], (RH, 128)), ((0, 8 - RH), (0, 0)))
    ldf8, ldb8 = ld8(ret_log_decay_f), ld8(ret_log_decay_b)
    of, rp_f, w_up_g1 = _ret_fwd(p_ext, ldf8, False, nctx, "ret_fwd_f", cargo=([w_up_tl[:half_up]], False))
    ob, rp_b, w_up_g2 = _ret_fwd(p_ext, ldb8, True, nctx, "ret_fwd_b", cargo=([w_up_tl[half_up:]], False))
    w_out_f, w_glu_f = w_out_g.reshape(D, D), w_glu_g.reshape(S5W, S5W)
    x1, w_down_g = _mix_fwd(x2, ys, of, ob, p_ext, s5_d, s5_b_glu, modx, w_glu_f, w_out_f, "mix_fwd", cargo=([w_down_l], False))
    w_down_f = w_down_g.reshape(DFF, D)
    w_up_t = jnp.concatenate([w_up_g1, w_up_g2], axis=1).reshape(2 * DFF, D)

    (dx2, da, dgc, f_act, dffn, loss_acc, g_fnw, g_gate2, g_cb, g_cw) = _ffn_fwd(
        x1, tgt, nw2, modx, w_up_t[:DFF].T, w_up_t[DFF:].T, conv_f, conv_b, w_down_f, w_down_f.T, fnw, "ffn_fwd")
    dx1, dag, h2, g_nw2, dmx2 = _ffn_bwd(x1, dx2, da, dgc, nw2, modx, w_up_t, conv_f, "ffn_bwd")
    gw_down = _matmul_tn(f_act, dffn, "dw_down").reshape(NDEV, -1, D)
    gw_up_t = _matmul_tn(dag, h2, "dw_up").reshape(NDEV, -1, D)
    (dy_e, dud_e, do_e, dg_e, cat, dmix, s_act, dz, g_d, g_bglu, g_gate1, l_down) = _mix_bwd(
        x2, ys, of, ob, p_ext, s5_d, s5_b_glu, modx, w_glu_f, w_out_f, dx1, "mix_bwd", cargo=([gw_down], True))
    gw_out = _matmul_tn(cat, dmix, "dw_out").reshape(NDEV, -1, D)
    gw_glu = _matmul_tn(s_act, dz, "dw_glu").reshape(NDEV, -1, S5W)
    dq_f, dk_f, dv_f, gld_f, l_up1 = _ret_bwd(p_ext, ldf8, rp_f, do_e, False, nctx, "ret_bwd_f",
                                              cargo=([gw_up_t[:, :half_up]], True))
    dq_b, dk_b, dv_b, gld_b, l_out, l_glu, l_up2 = _ret_bwd(p_ext, ldb8, rp_b, do_e, True, nctx, "ret_bwd_b",
                                                            cargo=([gw_out, gw_glu, gw_up_t[:, half_up:]], True))

    du1, g_m, dhp_f, dhp_b, dmc_f, dmc_b = _s5_out_bwd(_to_groups(dy_e), u_g, m_f, m_b, hp_f, hp_b, mc_f, mc_b, "s5_out_bwd")
    ds_f, da1_f, da2_f = _s5_carry_bwd(dhp_f, hp_f, a1_f, a2_f, False, nctx5, "s5_carry_bwd_f")
    ds_b, da1_b, da2_b = _s5_carry_bwd(dhp_b, hp_b, a1_b, a2_b, True, nctx5, "s5_carry_bwd_b")
    du_g, dmb_f, dmb_b = _s5_inc_bwd(du1, u_g, ds_f, ds_b, mb_f, mb_b, "s5_inc_bwd")
    zero_p = jnp.zeros((S5G, S5P, SB), f32)
    gf = _s5_build_bwd(s5p["f"], (g_m, dmb_f, dmc_f, da1_f[:, None, :], da2_f[:, None, :]), (zero_p, zero_p), False, "s5_build_bwd_f")
    gb = _s5_build_bwd(s5p["b"], (g_m, dmb_b, dmc_b, da1_b[:, None, :], da2_b[:, None, :]), (gf[3], gf[4]), True, "s5_build_bwd_b")
    g_bre, g_bim = gb[3][:, :, :S5N].transpose(0, 2, 1), gb[3][:, :, S5N:].transpose(0, 2, 1)
    g_cre, g_cim = gb[4][:, :, :S5N], gb[4][:, :, S5N:]

    grad_x, dp_ext, h1, g_nw1, dmx1, dmc1 = _f1_bwd(
        x2, ctx2, modx, modc, nw1, w_in_t, cosf, sins, dx1, (_from_groups(du_g), dud_e, dq_f, dq_b, dk_f, dk_b, dv_f, dv_b, dg_e), "f1_bwd")
    gw_in_t = _matmul_tn(dp_ext, h1, "dw_in").reshape(NDEV, -1, D)
    (l_in,) = _exchange([gw_in_t], True, "scatter_dw_in")

    dmx = dmx1 + dmx2
    dmx = dmx.at[2].set(g_gate1[0]).at[5].set(g_gate2[0])
    dm_me = jnp.stack([dmx.reshape(-1), dmc1.reshape(-1)], axis=0)
    dm_all = _all_gather_small(jnp.pad(dm_me, ((0, 6), (0, 0))), "gather_dmod")
    dmx_all, dmc_all = dm_all[:, 0, :], dm_all[:, 1, :]
    my_cols = lambda a: lax.dynamic_slice(a, (0, me * ncol), (NDEV, ncol))
    gw_mod, g_bmod, dc9 = _ada_bwd(c9, dmx_all, dmc_all, my_cols(dmx_all), my_cols(dmc_all), w_mod_l, "ada_bwd")

    small = {
        "conv_w": g_cw, "c_ctx": dc9[8], "norm1_w": g_nw1, "s5_lambda_re_f": gf[0][:, 0, :S5N], "s5_lambda_im_f": gf[1][:, 0, :S5N],
        "s5_log_step_f": gf[2], "s5_lambda_re_b": gb[0][:, 0, :S5N], "s5_lambda_im_b": gb[1][:, 0, :S5N], "s5_log_step_b": gb[2],
        "s5_b_re": g_bre, "s5_b_im": g_bim, "s5_c_re": g_cre, "s5_c_im": g_cim, "s5_d": g_d, "s5_b_glu": g_bglu,
        "ret_log_decay_f": gld_f[:RH, 0], "ret_log_decay_b": gld_b[:RH, 0], "norm2_w": g_nw2, "conv_b": g_cb, "final_norm_w": g_fnw,
    }
    packed, soffs = _pack_small([small[n].astype(f32) for n in SMALL])
    red = _all_reduce_small(packed, "reduce_small")
    sshapes = [(3, DFF) if n == "conv_w" else W[n].shape for n in SMALL]
    G = dict(zip(SMALL, _unpack_small(red, soffs, sshapes)))
    G["conv_w"] = lax.dynamic_slice(G["conv_w"], (0, me * per_cv), (3, per_cv))[None]
    G["b_mod"] = g_bmod.reshape(b_mod.shape)
    G["w_mod"] = gw_mod[None]
    G["w_in"] = _sum8(l_in, "sum_dw_in").T[None]
    G["w_up"] = jnp.concatenate([_sum8(l_up1, "sum_dw_up1"), _sum8(l_up2, "sum_dw_up2")], axis=0).T[None]
    G["w_out"] = _sum8(l_out, "sum_dw_out")[None]
    G["w_down"] = _sum8(l_down, "sum_dw_down")[None]
    G["s5_w_glu"] = _sum8(l_glu, "sum_dw_glu")[None]

    delta, new_m, new_v = {}, {}, {}
    sm_names = SMALL[1:] + ["b_mod"]
    pk = lambda d: _pack_small([d[n].astype(f32) for n in sm_names])
    (pw, aoffs), (pg, _), (pm, _), (pv, _) = pk(W), pk(G), pk(M), pk(V)
    pd, pnm, pnv = _adamw(pw, pg, pm, pv, "adamw_small")
    shapes = [W[n].shape for n in sm_names]
    for dst, src in ((delta, pd), (new_m, pnm), (new_v, pnv)):
        dst.update(zip(sm_names, _unpack_small(src, aoffs, shapes)))
    for n in ["w_mod", "w_in", "w_out", "w_up", "w_down", "s5_w_glu", "conv_w"]:
        d, nm, nv = _adamw(W[n][0], G[n][0], M[n][0], V[n][0], "adamw_" + n)
        delta[n], new_m[n], new_v[n] = d[None], nm[None], nv[None]

    loss = lax.psum(loss_acc[0, 0], ("x", "y", "c"))
    return (loss, grad_x[None], *[G[n] for n in WEIGHTS], *[delta[n] for n in WEIGHTS], *[new_m[n] for n in WEIGHTS],
            *[new_v[n] for n in WEIGHTS])
```

```python
import functools

import numpy as np
import jax
import jax.numpy as jnp
from jax import lax
from jax.experimental import pallas as pl
from jax.experimental.pallas import tpu as pltpu

f32, bf16 = jnp.float32, jnp.bfloat16

D = 1024
S5W, S5G, S5P, S5N = 512, 32, 16, 64
TC = 16
TCP = TC * S5P
SB = 2 * S5N
GBK = 8
CARRY_UNROLL = 8
RH, DH = 4, 128
RW = RH * DH
INC = S5W + 4 * RW
DFF = 2816
T = 128
R = 256
RF = 128
HALO = 8
EPS = 1e-6
ROPE_THETA = 10000.0
GRID_W = 64
NDEV = 8
LR, B1, B2, AEPS, WD, STEP = 0.001, 0.9, 0.999, 1e-08, 0.01, 10
VMEM_LIMIT = 60 * 1024 * 1024
ACC_TILE_BYTES = 6 * 1024 * 1024
MESH = pl.DeviceIdType.MESH

_CP = functools.partial(pltpu.CompilerParams, vmem_limit_bytes=VMEM_LIMIT)
_ARB = ("arbitrary",)
_ANY = pl.BlockSpec(memory_space=pl.ANY)


def _dg(a, b, dims):
    return lax.dot_general(a.astype(bf16), b.astype(bf16), (dims, ((), ())), preferred_element_type=f32)


@jax.custom_vjp
def dnn(a, b):
    return _dg(a, b, ((1,), (0,)))


@jax.custom_vjp
def dnt(a, b):
    return _dg(a, b, ((1,), (1,)))


@jax.custom_vjp
def dtn(a, b):
    return _dg(a, b, ((0,), (0,)))


dnn.defvjp(lambda a, b: (dnn(a, b), (a, b)), lambda r, g: (dnt(g, r[1]).astype(r[0].dtype), dtn(r[0], g).astype(r[1].dtype)))
dnt.defvjp(lambda a, b: (dnt(a, b), (a, b)), lambda r, g: (dnn(g, r[1]).astype(r[0].dtype), dtn(g, r[0]).astype(r[1].dtype)))
dtn.defvjp(lambda a, b: (dtn(a, b), (a, b)), lambda r, g: (dnt(r[1], g).astype(r[0].dtype), dnn(r[0], g).astype(r[1].dtype)))


@jax.custom_vjp
def _dnn_const(a, w, wt):
    return dnn(a, w)


_dnn_const.defvjp(lambda a, w, wt: (dnn(a, w), wt), lambda wt, g: (dnn(g, wt), None, None))


def _rms(t, w):
    return t * lax.rsqrt(jnp.mean(t * t, axis=-1, keepdims=True) + EPS) * w


def _mod(h, shift, scale):
    return h * (1.0 + scale) + shift


def _const_spec(shape):
    n = len(shape)
    return pl.BlockSpec(shape, lambda i, _n=n: (0,) * _n, pipeline_mode=pl.Buffered(1))


def _acc_spec(shape):
    n = len(shape)
    return pl.BlockSpec(shape, lambda i, _n=n: (0,) * _n)


def _me():
    return 4 * lax.axis_index("x") + 2 * lax.axis_index("y") + lax.axis_index("c")


def _peer(r):
    x, y, c = lax.axis_index("x"), lax.axis_index("y"), lax.axis_index("c")
    px = 1 - x if (r >> 2) & 1 else x
    py = 1 - y if (r >> 1) & 1 else y
    pc = 1 - c if r & 1 else c
    return (px, py, pc), 4 * px + 2 * py + pc


def _all_gather_small(v, name):
    r, c = v.shape

    def body(v_ref, out_ref, send_sems, recv_sems):
        me = _me()
        out_ref[me] = v_ref[...]
        sends = []
        for k in range(1, NDEV):
            peer, _ = _peer(k)
            cp = pltpu.make_async_remote_copy(src_ref=v_ref, dst_ref=out_ref.at[me], send_sem=send_sems.at[k - 1],
                                              recv_sem=recv_sems.at[k - 1], device_id=peer, device_id_type=MESH)
            cp.start()
            sends.append(cp)
        for k in range(1, NDEV):
            peer, pidx = _peer(k)
            pltpu.make_async_remote_copy(src_ref=v_ref, dst_ref=out_ref.at[pidx], send_sem=send_sems.at[k - 1],
                                         recv_sem=recv_sems.at[k - 1], device_id=peer, device_id_type=MESH).wait_recv()
        for cp in sends:
            cp.wait_send()

    return pl.pallas_call(
        body, name=name, out_shape=jax.ShapeDtypeStruct((NDEV, r, c), v.dtype),
        in_specs=[pl.BlockSpec(memory_space=pltpu.VMEM)], out_specs=pl.BlockSpec(memory_space=pltpu.VMEM),
        scratch_shapes=[pltpu.SemaphoreType.DMA((NDEV - 1,)), pltpu.SemaphoreType.DMA((NDEV - 1,))],
        compiler_params=_CP(),
    )(v)


def _all_reduce_small(v, name):
    r, c = v.shape

    def body(v_ref, out_ref, land, send_sems, recv_sems):
        me = _me()
        land[me] = v_ref[...]
        sends = []
        for k in range(1, NDEV):
            peer, _ = _peer(k)
            cp = pltpu.make_async_remote_copy(src_ref=v_ref, dst_ref=land.at[me], send_sem=send_sems.at[k - 1],
                                              recv_sem=recv_sems.at[k - 1], device_id=peer, device_id_type=MESH)
            cp.start()
            sends.append(cp)
        for k in range(1, NDEV):
            peer, pidx = _peer(k)
            pltpu.make_async_remote_copy(src_ref=v_ref, dst_ref=land.at[pidx], send_sem=send_sems.at[k - 1],
                                         recv_sem=recv_sems.at[k - 1], device_id=peer, device_id_type=MESH).wait_recv()
        for cp in sends:
            cp.wait_send()
        acc = land[0]
        for j in range(1, NDEV):
            acc = acc + land[j]
        out_ref[...] = acc

    return pl.pallas_call(
        body, name=name, out_shape=jax.ShapeDtypeStruct((r, c), v.dtype),
        in_specs=[pl.BlockSpec(memory_space=pltpu.VMEM)], out_specs=pl.BlockSpec(memory_space=pltpu.VMEM),
        scratch_shapes=[pltpu.VMEM((NDEV, r, c), v.dtype), pltpu.SemaphoreType.DMA((NDEV - 1,)),
                        pltpu.SemaphoreType.DMA((NDEV - 1,))],
        compiler_params=_CP(),
    )(v)


class _Exchange:
    def __init__(self, srcs, dsts, send_sems, recv_sems, local_sems, scatter):
        me = _me()
        n = len(srcs)
        self.sends, self.recvs, self.locals = [], [], []
        for a, (s, d) in enumerate(zip(srcs, dsts)):
            self.locals.append(pltpu.make_async_copy(s.at[me] if scatter else s, d.at[me], local_sems.at[a]))
        for k in range(1, NDEV):
            peer, pidx = _peer(k)
            for a, (s, d) in enumerate(zip(srcs, dsts)):
                src = s.at[pidx] if scatter else s
                sem = (k - 1) * n + a
                for dst, out in ((d.at[me], self.sends), (d.at[pidx], self.recvs)):
                    out.append(pltpu.make_async_remote_copy(src_ref=src, dst_ref=dst, send_sem=send_sems.at[sem],
                                                            recv_sem=recv_sems.at[sem], device_id=peer, device_id_type=MESH))

    def start(self):
        for cp in self.locals + self.sends:
            cp.start()

    def wait(self):
        for cp in self.recvs:
            cp.wait_recv()
        for cp in self.sends:
            cp.wait_send()
        for cp in self.locals:
            cp.wait()


def _exchange_shapes(arrays, scatter):
    return [jax.ShapeDtypeStruct(a.shape if scatter else (NDEV,) + a.shape, a.dtype) for a in arrays]


def _exchange_sems(n):
    return [pltpu.SemaphoreType.DMA(((NDEV - 1) * n,)), pltpu.SemaphoreType.DMA(((NDEV - 1) * n,)), pltpu.SemaphoreType.DMA((n,))]


def _exchange(arrays, scatter, name):
    n = len(arrays)

    def body(*refs):
        ex = _Exchange(refs[:n], refs[n:2 * n], *refs[2 * n:], scatter)
        ex.start()
        ex.wait()

    return pl.pallas_call(body, name=name, out_shape=_exchange_shapes(arrays, scatter), in_specs=[_ANY] * n,
                          out_specs=[_ANY] * n, scratch_shapes=_exchange_sems(n), compiler_params=_CP())(*arrays)


class _Cargo:
    def __init__(self, cargo):
        self.arrays, self.scatter = cargo if cargo else ([], False)
        self.n = len(self.arrays)

    def in_specs(self):
        return [_ANY] * self.n

    def out_shapes(self):
        return _exchange_shapes(self.arrays, self.scatter)

    def sems(self):
        return _exchange_sems(self.n) if self.n else []

    def split(self, refs, n_in, n_out, n_scratch):
        n = self.n
        return refs[:n_in], refs[n_in + n:n_in + n + n_out], refs[n_in + 2 * n + n_out:n_in + 2 * n + n_out + n_scratch]

    def ride(self, refs, n_in, n_out, nsteps):
        if not self.n:
            return
        n = self.n
        ex = _Exchange(refs[n_in:n_in + n], refs[n_in + n + n_out:n_in + 2 * n + n_out], *refs[-3:], self.scatter)

        @pl.when(pl.program_id(0) == 0)
        def _():
            ex.start()

        @pl.when(pl.program_id(0) == nsteps - 1)
        def _():
            ex.wait()


def _sum8(land, name):
    _, r, c = land.shape
    rb = next((b for b in (256, 64, 32) if r % b == 0), r)

    def body(l_ref, o_ref):
        acc = l_ref[0].astype(f32)
        for j in range(1, NDEV):
            acc = acc + l_ref[j].astype(f32)
        o_ref[...] = acc

    return pl.pallas_call(
        body, name=name, grid=(r // rb,), out_shape=jax.ShapeDtypeStruct((r, c), f32),
        in_specs=[pl.BlockSpec((NDEV, rb, c), lambda i: (0, i, 0))], out_specs=pl.BlockSpec((rb, c), lambda i: (i, 0)),
        compiler_params=_CP(dimension_semantics=("parallel",)),
    )(land)


def _ada_fwd(c9, w_mod_l, name):
    def body(c_ref, w_ref, o_ref):
        o_ref[...] = dnn(jax.nn.silu(c_ref[...]), w_ref[...])

    return pl.pallas_call(body, name=name, out_shape=jax.ShapeDtypeStruct((16, w_mod_l.shape[1]), f32),
                          compiler_params=_CP())(c9, w_mod_l)


def _mod_select(m_all, b_mod6, name):
    def body(m_ref, b_ref, mx_ref, mc_ref):
        me = _me()
        mx_ref[...] = m_ref[me] + b_ref[...]
        mc_ref[...] = m_ref[8] + b_ref[...]

    return pl.pallas_call(body, name=name, out_shape=[jax.ShapeDtypeStruct((6, D), f32)] * 2, compiler_params=_CP())(m_all, b_mod6)


def _ada_bwd(c9, dmx_all, dmc_all, dmx_l, dmc_l, w_mod_l, name):
    ncol = w_mod_l.shape[1]

    def rowsum(r):
        acc = r[0:1]
        for j in range(1, NDEV):
            acc = acc + r[j:j + 1]
        return acc

    def body(c_ref, xa_ref, ca_ref, xl_ref, cl_ref, w_ref, gw_ref, gb_ref, dc_ref):
        s9, vjp = jax.vjp(jax.nn.silu, c_ref[...])
        dm9 = jnp.concatenate([xl_ref[...], rowsum(cl_ref[...]), jnp.zeros((7, ncol), f32)], axis=0)
        gw_ref[...] = dtn(s9, dm9)
        gb_ref[...] = rowsum(xa_ref[...]) + rowsum(ca_ref[...])
        dc_ref[...] = vjp(dnt(dm9, w_ref[...]))[0]

    return pl.pallas_call(
        body, name=name,
        out_shape=[jax.ShapeDtypeStruct((D, ncol), f32), jax.ShapeDtypeStruct((1, 6 * D), f32), jax.ShapeDtypeStruct((16, D), f32)],
        compiler_params=_CP())(c9, dmx_all, dmc_all, dmx_l, dmc_l, w_mod_l)


def _lane_sign(rank):
    shape = (1,) * (rank - 1) + (SB,)
    return jnp.where(lax.broadcasted_iota(jnp.int32, shape, rank - 1) < S5N, -1.0, 1.0)


def _s5_build_fn(lre2, lim2, ls, bn, bs, cn, cs, rev):
    sg = _lane_sign(3)
    s = jnp.exp(ls)
    ar, ai = lre2 * s, lim2 * s
    e = jnp.exp(ar)
    nr, ni = e * jnp.cos(ai) - 1.0, e * jnp.sin(ai)
    den = lre2 * lre2 + lim2 * lim2
    cr, ci = (nr * lre2 + ni * lim2) / den, (ni * lre2 - nr * lim2) / den
    bbn = cr * bn + (ci * sg) * bs
    bbs = cr * bs - (ci * sg) * bn

    def powers(ex):
        m, ang = jnp.exp(ex * ar), ex * ai
        return m * jnp.cos(ang), m * jnp.sin(ang) * sg

    def times(tabs, xn, xs):
        f1, f2 = tabs
        return f1[:, :, None, :] * xn[:, None, :, :] + f2[:, :, None, :] * xs[:, None, :, :]

    t = lax.broadcasted_iota(jnp.int32, (1, TC, 1), 1).astype(f32)
    if rev:
        e_src, e_dst, e_out, e_in = t - (TC - 1.0), (TC - 1.0) - t, t, TC - t
    else:
        e_src, e_dst, e_out, e_in = -t, t, (TC - 1.0) - t, t + 1.0
    g = lre2.shape[0]
    flat = lambda a: a.reshape(g, TCP, SB)
    conj = -_lane_sign(4)
    ll = flat(times(powers(e_src), bbn, bbs))
    rr = flat(times(powers(e_dst), cn, cs) * conj)
    mb = flat(times(powers(e_out), bbn, bbs))
    mct = flat(times(powers(e_in), cn, cs) * conj)
    a1, a2 = powers(float(TC))
    row = lax.broadcasted_iota(jnp.int32, (TCP, TCP), 0) // S5P
    col = lax.broadcasted_iota(jnp.int32, (TCP, TCP), 1) // S5P
    mask = jnp.where((col <= row) if rev else (col >= row), 1.0, 0.0)
    m = jnp.concatenate([dnt(ll[j], rr[j])[None] for j in range(g)], axis=0) * mask
    return m, mb, mct, a1, a2


def _gspec(*tail):
    nt = len(tail)
    return pl.BlockSpec((GBK,) + tail, lambda i, _n=nt: (i,) + (0,) * _n)


def _s5_build(params, rev, name):
    def body(l1, l2, ls, bn, bs, cn, cs, m_ref, mb_ref, mc_ref, a1_ref, a2_ref):
        m, mb, mct, a1, a2 = _s5_build_fn(l1[...], l2[...], ls[...], bn[...], bs[...], cn[...], cs[...], rev)
        m_ref[...], mb_ref[...], mc_ref[...] = m.astype(bf16), mb.astype(bf16), mct.astype(bf16)
        a1_ref[...], a2_ref[...] = a1, a2

    vec, pm = _gspec(1, SB), _gspec(S5P, SB)
    return pl.pallas_call(
        body, name=name, grid=(S5G // GBK,),
        in_specs=[vec, vec, _gspec(1, 1), pm, pm, pm, pm],
        out_specs=[_gspec(TCP, TCP), _gspec(TCP, SB), _gspec(TCP, SB), vec, vec],
        out_shape=[jax.ShapeDtypeStruct((S5G, TCP, TCP), bf16), jax.ShapeDtypeStruct((S5G, TCP, SB), bf16),
                   jax.ShapeDtypeStruct((S5G, TCP, SB), bf16), jax.ShapeDtypeStruct((S5G, 1, SB), f32),
                   jax.ShapeDtypeStruct((S5G, 1, SB), f32)],
        compiler_params=_CP(dimension_semantics=("parallel",)),
    )(*params)


def _s5_build_bwd(params, cots, prev, rev, name):
    def body(l1, l2, ls, bn, bs, cn, cs, dm, dmb, dmc, da1, da2, pb, pc, gl1, gl2, gls, gb, gc):
        prim = (l1[...], l2[...], ls[...], bn[...], bs[...], cn[...], cs[...])
        _, vjp = jax.vjp(functools.partial(_s5_build_fn, rev=rev), *prim)
        d1, d2, dls, dbn, dbs, dcn, dcs = vjp((dm[...], dmb[...], dmc[...], da1[...], da2[...]))
        gl1[...] = d1 + pltpu.roll(d1, S5N, axis=2)
        gl2[...] = d2 + pltpu.roll(d2, S5N, axis=2)
        gls[...] = dls
        gb[...] = dbn + pltpu.roll(dbs, S5N, axis=2) + pb[...]
        gc[...] = dcn + pltpu.roll(dcs, S5N, axis=2) + pc[...]

    vec, pm, big = _gspec(1, SB), _gspec(S5P, SB), _gspec(TCP, SB)
    return pl.pallas_call(
        body, name=name, grid=(S5G // GBK,),
        in_specs=[vec, vec, _gspec(1, 1), pm, pm, pm, pm, _gspec(TCP, TCP), big, big, vec, vec, pm, pm],
        out_specs=[vec, vec, _gspec(1, 1), pm, pm],
        out_shape=[jax.ShapeDtypeStruct((S5G, 1, SB), f32), jax.ShapeDtypeStruct((S5G, 1, SB), f32),
                   jax.ShapeDtypeStruct((S5G, 1, 1), f32), jax.ShapeDtypeStruct((S5G, S5P, SB), f32),
                   jax.ShapeDtypeStruct((S5G, S5P, SB), f32)],
        compiler_params=_CP(dimension_semantics=("parallel",)),
    )(*params, *cots, *prev)


def _s5_inc(u, mb_f, mb_b, name):
    nc = u.shape[1]

    def body(u_ref, mf_ref, mb_ref, sf_ref, sb_ref):
        for j in range(GBK):
            sf_ref[:, j, :] = jnp.dot(u_ref[j], mf_ref[j], preferred_element_type=f32)
            sb_ref[:, j, :] = jnp.dot(u_ref[j], mb_ref[j], preferred_element_type=f32)

    sspec = pl.BlockSpec((nc, GBK, SB), lambda i: (0, i, 0))
    return pl.pallas_call(
        body, name=name, grid=(S5G // GBK,), in_specs=[_gspec(nc, TCP), _gspec(TCP, SB), _gspec(TCP, SB)],
        out_specs=[sspec, sspec], out_shape=[jax.ShapeDtypeStruct((nc, S5G, SB), f32)] * 2,
        compiler_params=_CP(dimension_semantics=("parallel",)),
    )(u, mb_f, mb_b)


def _idx_fwd(nctx, nch):
    return lambda i: i


def _idx_rev(nctx, nch):
    return lambda i: jnp.where(i < nctx, nctx - 1 - i, nch + nctx - 1 - i)


def _carry_loop(nc, step, init):
    def trip(i, c):
        for k in range(CARRY_UNROLL):
            c = step(i * CARRY_UNROLL + k, c)
        return c

    return lax.fori_loop(0, nc // CARRY_UNROLL, trip, init)


def _s5_carry(s_f, s_b, a_f, a_b, nctx, name):
    nc = s_f.shape[0]
    idx_b = _idx_rev(nctx, nc)

    def body(sf_ref, sb_ref, f1_ref, f2_ref, b1_ref, b2_ref, hf_ref, hb_ref):
        f1, f2, b1, b2 = f1_ref[...], f2_ref[...], b1_ref[...], b2_ref[...]

        def step(i, c):
            hf, hfs, hb, hbs = c
            rb = idx_b(i)
            hf_ref[i] = hf
            hb_ref[rb] = hb
            sf, sb = sf_ref[i], sb_ref[rb]
            return (f1 * hf + f2 * hfs + sf, f1 * hfs - f2 * hf + pltpu.roll(sf, S5N, axis=1),
                    b1 * hb + b2 * hbs + sb, b1 * hbs - b2 * hb + pltpu.roll(sb, S5N, axis=1))

        z = jnp.zeros((S5G, SB), f32)
        _carry_loop(nc, step, (z, z, z, z))

    return pl.pallas_call(body, name=name, out_shape=[jax.ShapeDtypeStruct(s_f.shape, f32)] * 2,
                          compiler_params=_CP())(s_f, s_b, *a_f, *a_b)


def _s5_carry_bwd(dhp, hp, a1, a2, rev, nctx, name):
    nc = hp.shape[0]
    idx = (_idx_rev if rev else _idx_fwd)(nctx, nc)

    def body(dhp_ref, hp_ref, a1_ref, a2_ref, ds_ref, d1_ref, d2_ref):
        f1, f2 = a1_ref[...], a2_ref[...]

        def step(k, carry):
            ab, abs_, d1, d2 = carry
            r = idx(nc - 1 - k)
            ds_ref[r] = ab
            h, dh = hp_ref[r], dhp_ref[r]
            return (dh + f1 * ab - f2 * abs_, pltpu.roll(dh, S5N, axis=1) + f1 * abs_ + f2 * ab,
                    d1 + ab * h, d2 + ab * pltpu.roll(h, S5N, axis=1))

        z = jnp.zeros((S5G, SB), f32)
        _, _, d1, d2 = _carry_loop(nc, step, (z, z, z, z))
        d1_ref[...], d2_ref[...] = d1, d2

    return pl.pallas_call(
        body, name=name,
        out_shape=[jax.ShapeDtypeStruct(hp.shape, f32), jax.ShapeDtypeStruct((S5G, SB), f32), jax.ShapeDtypeStruct((S5G, SB), f32)],
        compiler_params=_CP())(dhp, hp, a1, a2)


def _s5_out(u, m_f, m_b, hp_f, hp_b, mc_f, mc_b, name):
    nc = u.shape[1]

    def body(u_ref, mf_ref, mb_ref, hf_ref, hb_ref, cf_ref, cb_ref, y_ref):
        for j in range(GBK):
            uj = u_ref[j]
            y_ref[j] = (jnp.dot(uj, mf_ref[j], preferred_element_type=f32) + jnp.dot(uj, mb_ref[j], preferred_element_type=f32)
                        + dnt(hf_ref[:, j, :], cf_ref[j]) + dnt(hb_ref[:, j, :], cb_ref[j])).astype(bf16)

    sspec = pl.BlockSpec((nc, GBK, SB), lambda i: (0, i, 0))
    return pl.pallas_call(
        body, name=name, grid=(S5G // GBK,),
        in_specs=[_gspec(nc, TCP), _gspec(TCP, TCP), _gspec(TCP, TCP), sspec, sspec, _gspec(TCP, SB), _gspec(TCP, SB)],
        out_specs=_gspec(nc, TCP), out_shape=jax.ShapeDtypeStruct((S5G, nc, TCP), bf16),
        compiler_params=_CP(dimension_semantics=("parallel",)),
    )(u, m_f, m_b, hp_f, hp_b, mc_f, mc_b)


def _s5_out_bwd(dy, u, m_f, m_b, hp_f, hp_b, mc_f, mc_b, name):
    nc = u.shape[1]

    def body(dy_ref, u_ref, mf_ref, mb_ref, hf_ref, hb_ref, cf_ref, cb_ref, du_ref, g_ref, dhf_ref, dhb_ref, dcf_ref, dcb_ref):
        for j in range(GBK):
            dyj = dy_ref[j]
            du_ref[j] = dnt(dyj, mf_ref[j]) + dnt(dyj, mb_ref[j])
            g_ref[j] = dtn(u_ref[j], dyj)
            dhf_ref[:, j, :] = dnn(dyj, cf_ref[j])
            dhb_ref[:, j, :] = dnn(dyj, cb_ref[j])
            dcf_ref[j] = dtn(dyj, hf_ref[:, j, :])
            dcb_ref[j] = dtn(dyj, hb_ref[:, j, :])

    sspec = pl.BlockSpec((nc, GBK, SB), lambda i: (0, i, 0))
    sshape = jax.ShapeDtypeStruct((nc, S5G, SB), f32)
    cshape = jax.ShapeDtypeStruct((S5G, TCP, SB), f32)
    return pl.pallas_call(
        body, name=name, grid=(S5G // GBK,),
        in_specs=[_gspec(nc, TCP), _gspec(nc, TCP), _gspec(TCP, TCP), _gspec(TCP, TCP), sspec, sspec, _gspec(TCP, SB), _gspec(TCP, SB)],
        out_specs=[_gspec(nc, TCP), _gspec(TCP, TCP), sspec, sspec, _gspec(TCP, SB), _gspec(TCP, SB)],
        out_shape=[jax.ShapeDtypeStruct((S5G, nc, TCP), f32), jax.ShapeDtypeStruct((S5G, TCP, TCP), f32), sshape, sshape, cshape, cshape],
        compiler_params=_CP(dimension_semantics=("parallel",)),
    )(dy, u, m_f, m_b, hp_f, hp_b, mc_f, mc_b)


def _s5_inc_bwd(du1, u, ds_f, ds_b, mb_f, mb_b, name):
    nc = u.shape[1]

    def body(du1_ref, u_ref, dsf_ref, dsb_ref, mf_ref, mb_ref, du_ref, dmf_ref, dmb_ref):
        for j in range(GBK):
            dsf, dsb = dsf_ref[:, j, :], dsb_ref[:, j, :]
            du_ref[j] = (du1_ref[j] + dnt(dsf, mf_ref[j]) + dnt(dsb, mb_ref[j])).astype(bf16)
            dmf_ref[j] = dtn(u_ref[j], dsf)
            dmb_ref[j] = dtn(u_ref[j], dsb)

    sspec = pl.BlockSpec((nc, GBK, SB), lambda i: (0, i, 0))
    cshape = jax.ShapeDtypeStruct((S5G, TCP, SB), f32)
    return pl.pallas_call(
        body, name=name, grid=(S5G // GBK,),
        in_specs=[_gspec(nc, TCP), _gspec(nc, TCP), sspec, sspec, _gspec(TCP, SB), _gspec(TCP, SB)],
        out_specs=[_gspec(nc, TCP), _gspec(TCP, SB), _gspec(TCP, SB)],
        out_shape=[jax.ShapeDtypeStruct((S5G, nc, TCP), bf16), cshape, cshape],
        compiler_params=_CP(dimension_semantics=("parallel",)),
    )(du1, u, ds_f, ds_b, mb_f, mb_b)


def _to_groups(a):
    n = a.shape[0]
    return a.reshape(n // TC, TC, S5G, S5P).transpose(2, 0, 1, 3).reshape(S5G, n // TC, TCP)


def _from_groups(a):
    nc = a.shape[1]
    return a.reshape(S5G, nc, TC, S5P).transpose(1, 2, 0, 3).reshape(nc * TC, S5W)


def _swap_pairs(t):
    lane = lax.broadcasted_iota(jnp.int32, t.shape, 1)
    return jnp.where(lane % 2 == 0, pltpu.roll(t, DH - 1, axis=1), pltpu.roll(t, 1, axis=1))


def _rot(t, cosf, sins):
    return t * cosf + _swap_pairs(t) * sins


def _rot_t(d, cosf, sins):
    return d * cosf - _swap_pairs(d) * sins


def _ret_chunk(qr, kr, v, rp, ld, rev):
    pos = lax.broadcasted_iota(jnp.int32, (T, 1), 0).astype(f32)
    diff = pos - lax.broadcasted_iota(jnp.int32, (1, T), 1).astype(f32)
    if rev:
        keep, dist = diff < 0, jnp.maximum(-diff, 0.0)
        xi, zeta = jnp.exp(ld * (T - pos)), jnp.exp(ld * pos)
    else:
        keep, dist = diff >= 0, jnp.maximum(diff, 0.0)
        xi, zeta = jnp.exp(ld * (pos + 1.0)), jnp.exp(ld * (T - 1.0 - pos))
    dm = jnp.where(keep, jnp.exp(ld * dist), 0.0)
    out = dnn(dnt(qr, kr) * dm, v) + dnn(qr * xi, rp)
    rn = jnp.exp(ld * float(T)) * rp + dtn(kr * zeta, v)
    return out, rn


def _ret_fwd(p_ext, ld8, rev, nctx, name, cargo=None):
    n = p_ext.shape[0]
    nch = n // T
    idx = (_idx_rev if rev else _idx_fwd)(nctx, nch)
    cg = _Cargo(cargo)

    def body(*refs):
        (q_ref, k_ref, v_ref, ld_ref), (o_ref, rp_ref), (r_s,) = cg.split(refs, 4, 2, 1)
        cg.ride(refs, 4, 2, nch)

        @pl.when(pl.program_id(0) == 0)
        def _():
            r_s[...] = jnp.zeros_like(r_s)

        for h in range(RH):
            sl = slice(h * DH, (h + 1) * DH)
            rp = r_s[h]
            rp_ref[0, h] = rp
            out, rn = _ret_chunk(q_ref[:, sl].astype(f32), k_ref[:, sl].astype(f32), v_ref[:, sl].astype(f32), rp,
                                 ld_ref[h:h + 1, 0:1], rev)
            r_s[h] = rn
            o_ref[:, sl] = out

    def colspec(cb):
        return pl.BlockSpec((T, RW), lambda i, _c=cb: (idx(i), _c))

    return pl.pallas_call(
        body, name=name, grid=(nch,),
        in_specs=[colspec(1), colspec(2), colspec(3), _const_spec((8, 128))] + cg.in_specs(),
        out_specs=[pl.BlockSpec((T, RW), lambda i: (idx(i), 0)), pl.BlockSpec((1, RH, DH, DH), lambda i: (i, 0, 0, 0))] + cg.in_specs(),
        out_shape=[jax.ShapeDtypeStruct((n, RW), f32), jax.ShapeDtypeStruct((nch, RH, DH, DH), f32)] + cg.out_shapes(),
        scratch_shapes=[pltpu.VMEM((RH, DH, DH), f32)] + cg.sems(),
        compiler_params=_CP(dimension_semantics=_ARB),
    )(p_ext, p_ext, p_ext, ld8, *cg.arrays)


def _ret_bwd(p_ext, ld8, rprev, do_ext, rev, nctx, name, cargo=None):
    n = p_ext.shape[0]
    nch = n // T
    idx0 = (_idx_rev if rev else _idx_fwd)(nctx, nch)
    idx = lambda j: idx0(nch - 1 - j)
    cg = _Cargo(cargo)

    def body(*refs):
        ins, (dq_ref, dk_ref, dv_ref, dld_ref), (dr_s,) = cg.split(refs, 6, 4, 1)
        q_ref, k_ref, v_ref, ld_ref, rp_ref, do_ref = ins
        cg.ride(refs, 6, 4, nch)

        @pl.when(pl.program_id(0) == 0)
        def _():
            dr_s[...] = jnp.zeros_like(dr_s)
            dld_ref[...] = jnp.zeros_like(dld_ref)

        for h in range(RH):
            sl = slice(h * DH, (h + 1) * DH)
            _, vjp = jax.vjp(functools.partial(_ret_chunk, rev=rev), q_ref[:, sl].astype(f32), k_ref[:, sl].astype(f32),
                             v_ref[:, sl].astype(f32), rp_ref[0, h], ld_ref[h:h + 1, 0:1])
            dqr, dkr, dv, drp, dld = vjp((do_ref[:, sl], dr_s[h]))
            dr_s[h] = drp
            dq_ref[:, sl], dk_ref[:, sl], dv_ref[:, sl] = dqr, dkr, dv
            dld_ref[h:h + 1, :] += jnp.broadcast_to(dld, (1, 128))

    def colspec(cb):
        return pl.BlockSpec((T, RW), lambda j, _c=cb: (idx(j), _c))

    ospec = pl.BlockSpec((T, RW), lambda j: (idx(j), 0))
    oshape = jax.ShapeDtypeStruct((n, RW), f32)
    return pl.pallas_call(
        body, name=name, grid=(nch,),
        in_specs=[colspec(1), colspec(2), colspec(3), _const_spec((8, 128)),
                  pl.BlockSpec((1, RH, DH, DH), lambda j: (nch - 1 - j, 0, 0, 0)), ospec] + cg.in_specs(),
        out_specs=[ospec, ospec, ospec, _acc_spec((8, 128))] + cg.in_specs(),
        out_shape=[oshape, oshape, oshape, jax.ShapeDtypeStruct((8, 128), f32)] + cg.out_shapes(),
        scratch_shapes=[pltpu.VMEM((RH, DH, DH), f32)] + cg.sems(),
        compiler_params=_CP(dimension_semantics=_ARB),
    )(p_ext, p_ext, p_ext, ld8, rprev, do_ext, *cg.arrays)


def _qk_heads(p, fn_q, fn_k):
    heads = lambda base, fn: [fn(p[:, base + h * DH:base + (h + 1) * DH]) for h in range(RH)]
    return jnp.concatenate([p[:, :S5W]] + heads(S5W, fn_q) + heads(S5W + RW, fn_k) + [p[:, S5W + 2 * RW:]], axis=1)


def _f1_fwd(x, ctx, modx, modc, nw1, w_in_n, cosf, sins, name, cargo=None):
    L = x.shape[0]
    nb = L // R + 1
    scale = DH ** -0.5
    cg = _Cargo(cargo)

    def body(*refs):
        (x_ref, c_ref, mx_ref, mc_ref, nw_ref, w_ref, cos_ref, sin_ref), (p_ref,), _ = cg.split(refs, 8, 1, 0)
        cg.ride(refs, 8, 1, nb)
        is_ctx = pl.program_id(0) == 0
        xin = jnp.where(is_ctx, c_ref[...], x_ref[...])
        sh = jnp.where(is_ctx, mc_ref[0:1], mx_ref[0:1])
        sc = jnp.where(is_ctx, mc_ref[1:2], mx_ref[1:2])
        cf, ss = cos_ref[...], sin_ref[...]
        p = dnn(_mod(_rms(xin, nw_ref[...]), sh, sc), w_ref[...])
        p_ref[...] = _qk_heads(p, lambda t: _rot(t, cf, ss), lambda t: _rot(t * scale, cf, ss)).astype(bf16)

    return pl.pallas_call(
        body, name=name, grid=(nb,),
        in_specs=[pl.BlockSpec((R, D), lambda i: (jnp.maximum(i - 1, 0), 0)), _const_spec((R, D)), _const_spec((6, D)),
                  _const_spec((6, D)), _const_spec((1, D)), _const_spec((D, INC)), pl.BlockSpec((R, DH), lambda i: (i, 0)),
                  pl.BlockSpec((R, DH), lambda i: (i, 0))] + cg.in_specs(),
        out_specs=[pl.BlockSpec((R, INC), lambda i: (i, 0))] + cg.in_specs(),
        out_shape=[jax.ShapeDtypeStruct((L + R, INC), bf16)] + cg.out_shapes(),
        scratch_shapes=cg.sems(),
        compiler_params=_CP(dimension_semantics=_ARB),
    )(x, ctx, modx, modc, nw1, w_in_n, cosf, sins, *cg.arrays)


def _f1_bwd(x, ctx, modx, modc, nw1, w_in_t, cosf, sins, dx1, parts, name):
    L = x.shape[0]
    nb = L // R + 1
    scale = DH ** -0.5

    def body(x_ref, c_ref, mx_ref, mc_ref, nw_ref, w_ref, cos_ref, sin_ref, dx1_ref, du0, du1, dq0, dq1, dk0, dk1, dv0, dv1, dg0,
             gx_ref, dp_ref, h1_ref, dnw_ref, dmx_ref, dmc_ref):
        i = pl.program_id(0)
        is_ctx = i == 0

        @pl.when(is_ctx)
        def _():
            dnw_ref[...] = jnp.zeros_like(dnw_ref)
            dmx_ref[...] = jnp.zeros_like(dmx_ref)
            dmc_ref[...] = jnp.zeros_like(dmc_ref)

        cf, ss = cos_ref[...], sin_ref[...]
        dp = jnp.concatenate([du0[...].astype(f32) + du1[...], dq0[...] + dq1[...], dk0[...] + dk1[...], dv0[...] + dv1[...],
                              dg0[...]], axis=1)
        dp = _qk_heads(dp, lambda t: _rot_t(t, cf, ss), lambda t: _rot_t(t, cf, ss) * scale).astype(bf16)
        dp_ref[...] = dp
        xin = jnp.where(is_ctx, c_ref[...], x_ref[...])
        sh = jnp.where(is_ctx, mc_ref[0:1], mx_ref[0:1])
        sc = jnp.where(is_ctx, mc_ref[1:2], mx_ref[1:2])
        dh = dnn(dp, w_ref[...])
        h, vjp = jax.vjp(lambda a, b, c, d: _mod(_rms(a, b), c, d), xin, nw_ref[...], sh, sc)
        dxin, dnw, dsh, dsc = vjp(dh)
        h1_ref[...] = h.astype(bf16)
        gx_ref[...] = dx1_ref[...] + dxin
        dnw_ref[...] += dnw
        wx = jnp.where(is_ctx, 0.0, 1.0)
        dmx_ref[0:1] += dsh * wx
        dmx_ref[1:2] += dsc * wx
        dmc_ref[0:1] += dsh * (1.0 - wx)
        dmc_ref[1:2] += dsc * (1.0 - wx)

    lat = pl.BlockSpec((R, D), lambda i: (jnp.maximum(i - 1, 0), 0))
    ext = pl.BlockSpec((R, S5W), lambda i: (i, 0))
    return pl.pallas_call(
        body, name=name, grid=(nb,),
        in_specs=[lat, _const_spec((R, D)), _const_spec((6, D)), _const_spec((6, D)), _const_spec((1, D)), _const_spec((INC, D)),
                  pl.BlockSpec((R, DH), lambda i: (i, 0)), pl.BlockSpec((R, DH), lambda i: (i, 0)), lat] + [ext] * 9,
        out_specs=[lat, pl.BlockSpec((R, INC), lambda i: (i, 0)), pl.BlockSpec((R, D), lambda i: (i, 0)),
                   _acc_spec((1, D)), _acc_spec((6, D)), _acc_spec((6, D))],
        out_shape=[jax.ShapeDtypeStruct((L, D), f32), jax.ShapeDtypeStruct((L + R, INC), bf16),
                   jax.ShapeDtypeStruct((L + R, D), bf16), jax.ShapeDtypeStruct((1, D), f32),
                   jax.ShapeDtypeStruct((6, D), f32), jax.ShapeDtypeStruct((6, D), f32)],
        compiler_params=_CP(dimension_semantics=_ARB),
    )(x, ctx, modx, modc, nw1, w_in_t, cosf, sins, dx1, *parts)


def _ret_post(yr, g):
    outs = []
    for h in range(RH):
        yh = yr[:, h * DH:(h + 1) * DH]
        mu = jnp.mean(yh, axis=-1, keepdims=True)
        var = jnp.mean((yh - mu) ** 2, axis=-1, keepdims=True)
        outs.append((yh - mu) * lax.rsqrt(var + EPS))
    return jax.nn.silu(g) * jnp.concatenate(outs, axis=1)


def _mix_fn(ys, u, of, ob, g, x, dvec, bglu, gate1, pz, pm, wglu, wout):
    s = jax.nn.gelu(ys + dvec * u)
    z = dnn(s, wglu) + bglu + pz
    cat = jnp.concatenate([s * jax.nn.sigmoid(z), _ret_post(of + ob, g)], axis=1)
    mix = dnn(cat, wout) + pm
    return x + gate1 * mix, (s, cat)


def _mix_fwd(x, ys, of, ob, p_ext, dvec, bglu, modx, wglu, wout, name, cargo=None):
    L = x.shape[0]
    nb = L // R
    cg = _Cargo(cargo)

    def body(*refs):
        ins, (x1_ref,), _ = cg.split(refs, 11, 1, 0)
        x_ref, ys_ref, of_ref, ob_ref, u_ref, g_ref, d_ref, b_ref, mx_ref, wg_ref, wo_ref = ins
        cg.ride(refs, 11, 1, nb)
        x1_ref[...] = _mix_fn(ys_ref[...].astype(f32), u_ref[...].astype(f32), of_ref[...], ob_ref[...], g_ref[...].astype(f32),
                              x_ref[...], d_ref[...], b_ref[...], mx_ref[2:3], 0.0, 0.0, wg_ref[...], wo_ref[...])[0]

    ext = pl.BlockSpec((R, S5W), lambda i: (i + 1, 0))
    return pl.pallas_call(
        body, name=name, grid=(nb,),
        in_specs=[pl.BlockSpec((R, D), lambda i: (i, 0)), ext, ext, ext, ext, pl.BlockSpec((R, RW), lambda i: (i + 1, 4)),
                  _const_spec((1, S5W)), _const_spec((1, S5W)), _const_spec((6, D)), _const_spec((S5W, S5W)), _const_spec((D, D))]
        + cg.in_specs(),
        out_specs=[pl.BlockSpec((R, D), lambda i: (i, 0))] + cg.in_specs(),
        out_shape=[jax.ShapeDtypeStruct((L, D), f32)] + cg.out_shapes(),
        scratch_shapes=cg.sems(),
        compiler_params=_CP(dimension_semantics=_ARB),
    )(x, ys, of, ob, p_ext, p_ext, dvec, bglu, modx, wglu, wout, *cg.arrays)


def _mix_bwd(x, ys, of, ob, p_ext, dvec, bglu, modx, wglu, wout, dx1, name, cargo=None):
    L = x.shape[0]
    nb = L // R + 1
    cg = _Cargo(cargo)

    def body(*refs):
        ins, outs, _ = cg.split(refs, 12, 11, 0)
        x_ref, ys_ref, of_ref, ob_ref, u_ref, g_ref, d_ref, b_ref, mx_ref, wg_ref, wo_ref, dx1_ref = ins
        dy_ref, dud_ref, do_ref, dg_ref, cat_ref, dmix_ref, s_ref, dz_ref, dd_ref, db_ref, dg1_ref = outs
        cg.ride(refs, 12, 11, nb)
        i = pl.program_id(0)

        @pl.when(i == 0)
        def _():
            for r in outs:
                r[...] = jnp.zeros_like(r)

        @pl.when(i > 0)
        def _():
            fn = lambda ys_, u_, of_, g_, d_, b_, g1_, pz_, pm_: _mix_fn(
                ys_, u_, of_, ob_ref[...], g_, x_ref[...], d_, b_, g1_, pz_, pm_, wg_ref[...], wo_ref[...])
            _, vjp, (s, cat) = jax.vjp(fn, ys_ref[...].astype(f32), u_ref[...].astype(f32), of_ref[...], g_ref[...].astype(f32), d_ref[...],
                                       b_ref[...], mx_ref[2:3], jnp.zeros((R, S5W), f32), jnp.zeros((R, D), f32), has_aux=True)
            dy, dud, do, dg, dd, db, dg1, dz, dmix = vjp(dx1_ref[...])
            dy_ref[...], dud_ref[...], do_ref[...], dg_ref[...] = dy.astype(bf16), dud, do, dg
            cat_ref[...], dmix_ref[...] = cat.astype(bf16), dmix.astype(bf16)
            s_ref[...], dz_ref[...] = s.astype(bf16), dz.astype(bf16)
            dd_ref[...] += dd
            db_ref[...] += db
            dg1_ref[...] += dg1

    lat = pl.BlockSpec((R, D), lambda i: (jnp.maximum(i - 1, 0), 0))
    lat5 = pl.BlockSpec((R, S5W), lambda i: (jnp.maximum(i - 1, 0), 0))
    ext = pl.BlockSpec((R, S5W), lambda i: (i, 0))
    eshape = jax.ShapeDtypeStruct((L + R, S5W), f32)
    return pl.pallas_call(
        body, name=name, grid=(nb,),
        in_specs=[lat, ext, ext, ext, ext, pl.BlockSpec((R, RW), lambda i: (i, 4)),
                  _const_spec((1, S5W)), _const_spec((1, S5W)), _const_spec((6, D)), _const_spec((S5W, S5W)), _const_spec((D, D)), lat]
        + cg.in_specs(),
        out_specs=[ext, ext, ext, ext, lat, lat, lat5, lat5, _acc_spec((1, S5W)), _acc_spec((1, S5W)), _acc_spec((1, D))]
        + cg.in_specs(),
        out_shape=[jax.ShapeDtypeStruct((L + R, S5W), bf16), eshape, eshape, eshape, jax.ShapeDtypeStruct((L, D), bf16),
                   jax.ShapeDtypeStruct((L, D), bf16), jax.ShapeDtypeStruct((L, S5W), bf16), jax.ShapeDtypeStruct((L, S5W), bf16),
                   jax.ShapeDtypeStruct((1, S5W), f32), jax.ShapeDtypeStruct((1, S5W), f32), jax.ShapeDtypeStruct((1, D), f32)]
        + cg.out_shapes(),
        scratch_shapes=cg.sems(),
        compiler_params=_CP(dimension_semantics=_ARB),
    )(x, ys, of, ob, p_ext, p_ext, dvec, bglu, modx, wglu, wout, dx1, *cg.arrays)


def _ffn_tail(gc, a, x1, gate2, fnw, pf, wdown, wdown_t, tgt):
    f = jax.nn.gelu(gc) * a
    ffn = _dnn_const(f, wdown, wdown_t) + pf
    y = _rms(x1 + gate2 * ffn, fnw)
    err = y - tgt
    loss = 0.5 * jnp.sum(jnp.mean(err * err, axis=-1, keepdims=True), axis=0, keepdims=True)
    return loss, f


def _ffn_fwd(x1, tgt, nw2, modx, w_a, w_g, cw, cb, wdown, wdown_t, fnw, name):
    L = x1.shape[0]
    nb = L // RF
    per = RF // HALO

    def body(x_ref, xp_ref, xn_ref, t_ref, nw_ref, mx_ref, wa_ref, wg_ref, cw_ref, cb_ref, wd_ref, wdt_ref, fn_ref,
             dx2_ref, da_ref, dgc_ref, f_ref, dffn_ref, loss_ref, dfn_ref, dg2_ref, dcb_ref, dcw_ref):
        i = pl.program_id(0)

        @pl.when(i == 0)
        def _():
            for r in (loss_ref, dfn_ref, dg2_ref, dcb_ref, dcw_ref):
                r[...] = jnp.zeros_like(r)

        nw, sh, sc, gate2 = nw_ref[...], mx_ref[3:4], mx_ref[4:5], mx_ref[5:6]
        x1b = x_ref[...]
        h2 = _mod(_rms(x1b, nw), sh, sc)
        h2e = jnp.concatenate([_mod(_rms(xp_ref[...], nw), sh, sc), h2, _mod(_rms(xn_ref[...], nw), sh, sc)], axis=0)
        a = dnn(h2, wa_ref[...])
        ge = dnn(h2e, wg_ref[...])
        g = ge[HALO:HALO + RF]
        gp = ge[HALO - 1:HALO] * jnp.where(i > 0, 1.0, 0.0)
        gn = ge[HALO + RF:HALO + RF + 1] * jnp.where(i < nb - 1, 1.0, 0.0)
        row = lax.broadcasted_iota(jnp.int32, (RF, 1), 0)
        g_prev = jnp.where(row == 0, gp, pltpu.roll(g, 1, axis=0))
        g_next = jnp.where(row == RF - 1, gn, pltpu.roll(g, RF - 1, axis=0))
        gc = cb_ref[...] + g_prev * cw_ref[0:1] + g * cw_ref[1:2] + g_next * cw_ref[2:3]
        fn = lambda gc_, a_, x_, g2_, fw_, pf_: _ffn_tail(gc_, a_, x_, g2_, fw_, pf_, wd_ref[...], wdt_ref[...], t_ref[...])
        loss, vjp, f = jax.vjp(fn, gc, a, x1b, gate2, fn_ref[...], jnp.zeros((RF, D), f32), has_aux=True)
        dgc, da, dx2, dg2, dfw, dffn = vjp(jnp.ones((1, 1), f32))
        dx2_ref[...] = dx2
        da_ref[...], dgc_ref[...] = da.astype(bf16), dgc
        f_ref[...], dffn_ref[...] = f.astype(bf16), dffn.astype(bf16)
        loss_ref[...] += jnp.broadcast_to(loss, (1, 128))
        dfn_ref[...] += dfw
        dg2_ref[...] += dg2
        dcb_ref[...] += jnp.sum(dgc, axis=0, keepdims=True)
        dcw_ref[0:1] += jnp.sum(dgc * g_prev, axis=0, keepdims=True)
        dcw_ref[1:2] += jnp.sum(dgc * g, axis=0, keepdims=True)
        dcw_ref[2:3] += jnp.sum(dgc * g_next, axis=0, keepdims=True)

    blk = lambda w: pl.BlockSpec((RF, w), lambda i: (i, 0))
    return pl.pallas_call(
        body, name=name, grid=(nb,),
        in_specs=[blk(D), pl.BlockSpec((HALO, D), lambda i: (jnp.maximum(i * per - 1, 0), 0)),
                  pl.BlockSpec((HALO, D), lambda i: (jnp.minimum((i + 1) * per, L // HALO - 1), 0)), blk(D),
                  _const_spec((1, D)), _const_spec((6, D)), _const_spec((D, DFF)), _const_spec((D, DFF)), _const_spec((3, DFF)),
                  _const_spec((1, DFF)), _const_spec((DFF, D)), _const_spec((D, DFF)), _const_spec((1, D))],
        out_specs=[blk(D), blk(DFF), blk(DFF), blk(DFF), blk(D), _acc_spec((1, 128)), _acc_spec((1, D)), _acc_spec((1, D)),
                   _acc_spec((1, DFF)), _acc_spec((3, DFF))],
        out_shape=[jax.ShapeDtypeStruct((L, D), f32), jax.ShapeDtypeStruct((L, DFF), bf16), jax.ShapeDtypeStruct((L, DFF), f32),
                   jax.ShapeDtypeStruct((L, DFF), bf16), jax.ShapeDtypeStruct((L, D), bf16), jax.ShapeDtypeStruct((1, 128), f32),
                   jax.ShapeDtypeStruct((1, D), f32), jax.ShapeDtypeStruct((1, D), f32), jax.ShapeDtypeStruct((1, DFF), f32),
                   jax.ShapeDtypeStruct((3, DFF), f32)],
        compiler_params=_CP(dimension_semantics=_ARB),
    )(x1, x1, x1, tgt, nw2, modx, w_a, w_g, cw, cb, wdown, wdown_t, fnw)


def _ffn_bwd(x1, dx2, da, dgc, nw2, modx, wup_t, cw, name):
    L = x1.shape[0]
    nb = L // RF
    per = RF // HALO

    def body(x_ref, dx2_ref, da_ref, dgc_ref, dgp_ref, dgn_ref, nw_ref, mx_ref, wu_ref, cw_ref,
             dx1_ref, dag_ref, h2_ref, dnw_ref, dmx_ref):
        i = pl.program_id(0)

        @pl.when(i == 0)
        def _():
            dnw_ref[...] = jnp.zeros_like(dnw_ref)
            dmx_ref[...] = jnp.zeros_like(dmx_ref)

        dgc_b = dgc_ref[...]
        before = dgp_ref[HALO - 1:HALO] * jnp.where(i > 0, 1.0, 0.0)
        after = dgn_ref[0:1] * jnp.where(i < nb - 1, 1.0, 0.0)
        row = lax.broadcasted_iota(jnp.int32, (RF, 1), 0)
        d_prev = jnp.where(row == 0, before, pltpu.roll(dgc_b, 1, axis=0))
        d_next = jnp.where(row == RF - 1, after, pltpu.roll(dgc_b, RF - 1, axis=0))
        dg = cw_ref[0:1] * d_next + cw_ref[1:2] * dgc_b + cw_ref[2:3] * d_prev
        dag = jnp.concatenate([da_ref[...], dg.astype(bf16)], axis=1)
        dag_ref[...] = dag
        dh2 = dnn(dag, wu_ref[...])
        h2, vjp = jax.vjp(lambda a, b, c, d: _mod(_rms(a, b), c, d), x_ref[...], nw_ref[...], mx_ref[3:4], mx_ref[4:5])
        dxa, dnw, dsh, dsc = vjp(dh2)
        h2_ref[...] = h2.astype(bf16)
        dx1_ref[...] = dx2_ref[...] + dxa
        dnw_ref[...] += dnw
        dmx_ref[3:4] += dsh
        dmx_ref[4:5] += dsc

    blk = lambda w: pl.BlockSpec((RF, w), lambda i: (i, 0))
    return pl.pallas_call(
        body, name=name, grid=(nb,),
        in_specs=[blk(D), blk(D), blk(DFF), blk(DFF), pl.BlockSpec((HALO, DFF), lambda i: (jnp.maximum(i * per - 1, 0), 0)),
                  pl.BlockSpec((HALO, DFF), lambda i: (jnp.minimum((i + 1) * per, L // HALO - 1), 0)),
                  _const_spec((1, D)), _const_spec((6, D)), _const_spec((2 * DFF, D)), _const_spec((3, DFF))],
        out_specs=[blk(D), blk(2 * DFF), blk(D), _acc_spec((1, D)), _acc_spec((6, D))],
        out_shape=[jax.ShapeDtypeStruct((L, D), f32), jax.ShapeDtypeStruct((L, 2 * DFF), bf16), jax.ShapeDtypeStruct((L, D), bf16),
                   jax.ShapeDtypeStruct((1, D), f32), jax.ShapeDtypeStruct((6, D), f32)],
        compiler_params=_CP(dimension_semantics=_ARB),
    )(x1, dx2, da, dgc, dgc, dgc, nw2, modx, wup_t, cw)


def _matmul_tn(a, b, name):
    k, m = a.shape
    n = b.shape[1]
    divs = lambda d: [c for c in range(d, 0, -128) if d % c == 0]
    _, tm, tn = min((m * (n // cn) + n * (m // cm), cm, cn) for cm in divs(m) for cn in divs(n) if cm * cn * 4 <= ACC_TILE_BYTES)
    tk = next(c for c in (512, 768, 256, 128) if k % c == 0)
    nk = k // tk

    def body(a_ref, b_ref, o_ref, acc):
        q = pl.program_id(2)

        @pl.when(q == 0)
        def _():
            acc[...] = jnp.zeros_like(acc)

        acc[...] += dtn(a_ref[...], b_ref[...])

        @pl.when(q == nk - 1)
        def _():
            o_ref[...] = acc[...].astype(bf16)

    return pl.pallas_call(
        body, name=name, grid=(m // tm, n // tn, nk),
        in_specs=[pl.BlockSpec((tk, tm), lambda i, j, q: (q, i)), pl.BlockSpec((tk, tn), lambda i, j, q: (q, j))],
        out_specs=pl.BlockSpec((tm, tn), lambda i, j, q: (i, j)),
        out_shape=jax.ShapeDtypeStruct((m, n), bf16),
        scratch_shapes=[pltpu.VMEM((tm, tn), f32)],
        compiler_params=_CP(dimension_semantics=("parallel", "parallel", "arbitrary")),
    )(a, b)


def _adamw_refs(w_ref, g_ref, m_ref, v_ref, d_ref, nm_ref, nv_ref):
    c1, c2 = 1.0 - B1 ** STEP, 1.0 - B2 ** STEP
    gg = g_ref[...]
    nm = B1 * m_ref[...] + (1.0 - B1) * gg
    nv = B2 * v_ref[...] + (1.0 - B2) * jnp.square(gg)
    d_ref[...] = -LR * ((nm / c1) / (jnp.sqrt(nv / c2) + AEPS) + WD * w_ref[...])
    nm_ref[...], nv_ref[...] = nm, nv


def _adamw(w, g, m, v, name):
    def body(*refs):
        _adamw_refs(*refs)

    return pl.pallas_call(body, name=name, out_shape=[jax.ShapeDtypeStruct(w.shape, f32)] * 3, compiler_params=_CP())(w, g, m, v)


def _adamw_many(ws, gs, ms, vs, name):
    n = len(ws)

    def body(*refs):
        for k in range(n):
            _adamw_refs(*[refs[j * n + k] for j in range(7)])

    outs = pl.pallas_call(body, name=name, out_shape=[jax.ShapeDtypeStruct(w.shape, f32) for w in ws] * 3,
                          compiler_params=_CP())(*ws, *gs, *ms, *vs)
    return outs[:n], outs[n:2 * n], outs[2 * n:]


SMALL = ["conv_w", "c_ctx", "norm1_w", "s5_lambda_re_f", "s5_lambda_im_f", "s5_log_step_f", "s5_lambda_re_b", "s5_lambda_im_b",
         "s5_log_step_b", "s5_b_re", "s5_b_im", "s5_c_re", "s5_c_im", "s5_d", "s5_b_glu", "ret_log_decay_f", "ret_log_decay_b",
         "norm2_w", "conv_b", "final_norm_w"]
WEIGHTS = ["c_ctx", "w_mod", "b_mod", "norm1_w", "w_in", "s5_lambda_re_f", "s5_lambda_im_f", "s5_log_step_f", "s5_lambda_re_b",
           "s5_lambda_im_b", "s5_log_step_b", "s5_b_re", "s5_b_im", "s5_c_re", "s5_c_im", "s5_d", "s5_w_glu", "s5_b_glu",
           "ret_log_decay_f", "ret_log_decay_b", "w_out", "norm2_w", "w_up", "conv_w", "conv_b", "w_down", "final_norm_w"]


def _pack_small(vals):
    flat, offs, o = [], [], 0
    for a in vals:
        n = a.size
        npad = -n % 128
        flat.append(jnp.pad(a.reshape(-1), (0, npad)))
        offs.append((o, n))
        o += n + npad
    tail = -o % 1024
    if tail:
        flat.append(jnp.zeros((tail,), f32))
    return jnp.concatenate(flat).reshape(-1, 128), offs


def _unpack_small(packed, offs, shapes):
    flat = packed.reshape(-1)
    return [flat[o:o + n].reshape(s) for (o, n), s in zip(offs, shapes)]


def _rope_tables(L, nctx_rows):
    t = np.arange(L)
    inv = (ROPE_THETA ** (-np.arange(DH // 4, dtype=np.float64) / (DH // 4))).astype(np.float32)
    ang = np.concatenate([(t // GRID_W).astype(np.float32)[:, None] * inv, (t % GRID_W).astype(np.float32)[:, None] * inv], axis=-1)
    cos = np.repeat(np.cos(ang).astype(np.float32), 2, axis=1)
    sin = np.repeat(np.sin(ang).astype(np.float32), 2, axis=1) * np.tile(np.array([-1.0, 1.0], np.float32), DH // 2)
    cosf = np.concatenate([np.ones((nctx_rows, DH), np.float32), cos], axis=0)
    sins = np.concatenate([np.zeros((nctx_rows, DH), np.float32), sin], axis=0)
    return jnp.asarray(cosf), jnp.asarray(sins)


def kernel(x, c, ctx, c_ctx, w_mod, b_mod, norm1_w, w_in, s5_lambda_re_f, s5_lambda_im_f, s5_log_step_f, s5_lambda_re_b, s5_lambda_im_b, s5_log_step_b, s5_b_re, s5_b_im, s5_c_re, s5_c_im, s5_d, s5_w_glu, s5_b_glu, ret_log_decay_f, ret_log_decay_b, w_out, norm2_w, w_up, conv_w, conv_b, w_down, final_norm_w, loss_target, m_c_ctx, m_w_mod, m_b_mod, m_norm1_w, m_w_in, m_s5_lambda_re_f, m_s5_lambda_im_f, m_s5_log_step_f, m_s5_lambda_re_b, m_s5_lambda_im_b, m_s5_log_step_b, m_s5_b_re, m_s5_b_im, m_s5_c_re, m_s5_c_im, m_s5_d, m_s5_w_glu, m_s5_b_glu, m_ret_log_decay_f, m_ret_log_decay_b, m_w_out, m_norm2_w, m_w_up, m_conv_w, m_conv_b, m_w_down, m_final_norm_w, v_c_ctx, v_w_mod, v_b_mod, v_norm1_w, v_w_in, v_s5_lambda_re_f, v_s5_lambda_im_f, v_s5_log_step_f, v_s5_lambda_re_b, v_s5_lambda_im_b, v_s5_log_step_b, v_s5_b_re, v_s5_b_im, v_s5_c_re, v_s5_c_im, v_s5_d, v_s5_w_glu, v_s5_b_glu, v_ret_log_decay_f, v_ret_log_decay_b, v_w_out, v_norm2_w, v_w_up, v_conv_w, v_conv_b, v_w_down, v_final_norm_w):
    args = dict(locals())
    W = {n: args[n] for n in WEIGHTS}
    M = {n: args["m_" + n] for n in WEIGHTS}
    V = {n: args["v_" + n] for n in WEIGHTS}
    me = _me()
    x2, ctx2, tgt = x[0], ctx[0], loss_target[0]
    L, Lc = x2.shape[0], ctx2.shape[0]
    assert Lc == R and L % R == 0 and L % GRID_W == 0
    nctx = Lc // T

    c_all = _all_gather_small(jnp.pad(c, ((0, 7), (0, 0))), "gather_c")[:, 0, :]
    c9 = jnp.concatenate([c_all, c_ctx[None], jnp.zeros((7, D), f32)], axis=0)
    w_mod_l = w_mod[0]
    ncol = w_mod_l.shape[1]
    m_part = _ada_fwd(c9, w_mod_l, "ada_fwd")
    m_all = _all_gather_small(m_part, "gather_mod").transpose(1, 0, 2).reshape(16, 6, D)
    modx, modc = _mod_select(m_all, b_mod.reshape(6, D), "mod_select")

    w_in_tl, w_up_tl = w_in[0].T.astype(bf16), w_up[0].T.astype(bf16)
    w_out_l, w_down_l, w_glu_l = w_out[0].astype(bf16), w_down[0].astype(bf16), s5_w_glu[0].astype(bf16)
    half_up = w_up_tl.shape[0] // 2
    (w_in_g,) = _exchange([w_in_tl], False, "gather_w_in")
    w_in_t = w_in_g.reshape(INC, D)
    per_cv = conv_w.shape[2]
    conv_pad = jnp.pad(conv_w[0], ((0, 5), (0, 128 * 3 - per_cv)))
    conv_f = _all_gather_small(conv_pad, "gather_conv")[:, :3, :per_cv].transpose(1, 0, 2).reshape(3, DFF)

    pair = lambda a, b: jnp.concatenate([a, b], axis=-1)
    bre_g, bim_g = s5_b_re[0].transpose(0, 2, 1), s5_b_im[0].transpose(0, 2, 1)
    cre_g, cim_g = s5_c_re[0], s5_c_im[0]
    shared = (pair(bre_g, bim_g), pair(bim_g, bre_g), pair(cre_g, cim_g), pair(cim_g, cre_g))
    s5p = {}
    for tag, lre, lim, ls in (("f", s5_lambda_re_f, s5_lambda_im_f, s5_log_step_f), ("b", s5_lambda_re_b, s5_lambda_im_b, s5_log_step_b)):
        s5p[tag] = (pair(lre[0], lre[0])[:, None, :], pair(lim[0], lim[0])[:, None, :], ls[0].reshape(S5G, 1, 1)) + shared
    m_f, mb_f, mc_f, a1_f, a2_f = _s5_build(s5p["f"], False, "s5_build_f")
    m_b, mb_b, mc_b, a1_b, a2_b = _s5_build(s5p["b"], True, "s5_build_b")
    a1_f, a2_f, a1_b, a2_b = (a.reshape(S5G, SB) for a in (a1_f, a2_f, a1_b, a2_b))

    nw1, nw2, fnw = norm1_w, norm2_w, final_norm_w[None]
    cosf, sins = _rope_tables(L, Lc)
    p_ext, w_out_g, w_glu_g = _f1_fwd(x2, ctx2, modx, modc, nw1, w_in_t.T, cosf, sins, "f1_fwd", cargo=([w_out_l, w_glu_l], False))
    nctx5 = Lc // TC
    u_g = _to_groups(p_ext[:, :S5W])
    s_f, s_b = _s5_inc(u_g, mb_f, mb_b, "s5_inc")
    hp_f, hp_b = _s5_carry(s_f, s_b, (a1_f, a2_f), (a1_b, a2_b), nctx5, "s5_carry")
    ys = _from_groups(_s5_out(u_g, m_f, m_b, hp_f, hp_b, mc_f, mc_b, "s5_out"))
    ld8 = lambda ld: jnp.pad(jnp.broadcast_to(ld[0][:, None], (RH, 128)), ((0, 8 - RH), (0, 0)))
    ldf8, ldb8 = ld8(ret_log_decay_f), ld8(ret_log_decay_b)
    of, rp_f, w_up_g1 = _ret_fwd(p_ext, ldf8, False, nctx, "ret_fwd_f", cargo=([w_up_tl[:half_up]], False))
    ob, rp_b, w_up_g2 = _ret_fwd(p_ext, ldb8, True, nctx, "ret_fwd_b", cargo=([w_up_tl[half_up:]], False))
    w_out_f, w_glu_f = w_out_g.reshape(D, D), w_glu_g.reshape(S5W, S5W)
    x1, w_down_g = _mix_fwd(x2, ys, of, ob, p_ext, s5_d, s5_b_glu, modx, w_glu_f, w_out_f, "mix_fwd", cargo=([w_down_l], False))
    w_down_f = w_down_g.reshape(DFF, D)
    w_up_t = jnp.concatenate([w_up_g1, w_up_g2], axis=1).reshape(2 * DFF, D)

    (dx2, da, dgc, f_act, dffn, loss_acc, g_fnw, g_gate2, g_cb, g_cw) = _ffn_fwd(
        x1, tgt, nw2, modx, w_up_t[:DFF].T, w_up_t[DFF:].T, conv_f, conv_b, w_down_f, w_down_f.T, fnw, "ffn_fwd")
    dx1, dag, h2, g_nw2, dmx2 = _ffn_bwd(x1, dx2, da, dgc, nw2, modx, w_up_t, conv_f, "ffn_bwd")
    gw_down = _matmul_tn(f_act, dffn, "dw_down").reshape(NDEV, -1, D)
    gw_up_t = _matmul_tn(dag, h2, "dw_up").reshape(NDEV, -1, D)
    (dy_e, dud_e, do_e, dg_e, cat, dmix, s_act, dz, g_d, g_bglu, g_gate1, l_down) = _mix_bwd(
        x2, ys, of, ob, p_ext, s5_d, s5_b_glu, modx, w_glu_f, w_out_f, dx1, "mix_bwd", cargo=([gw_down], True))
    gw_out = _matmul_tn(cat, dmix, "dw_out").reshape(NDEV, -1, D)
    gw_glu = _matmul_tn(s_act, dz, "dw_glu").reshape(NDEV, -1, S5W)
    dq_f, dk_f, dv_f, gld_f, l_up1 = _ret_bwd(p_ext, ldf8, rp_f, do_e, False, nctx, "ret_bwd_f",
                                              cargo=([gw_up_t[:, :half_up]], True))
    dq_b, dk_b, dv_b, gld_b, l_out, l_glu, l_up2 = _ret_bwd(p_ext, ldb8, rp_b, do_e, True, nctx, "ret_bwd_b",
                                                            cargo=([gw_out, gw_glu, gw_up_t[:, half_up:]], True))

    du1, g_m, dhp_f, dhp_b, dmc_f, dmc_b = _s5_out_bwd(_to_groups(dy_e), u_g, m_f, m_b, hp_f, hp_b, mc_f, mc_b, "s5_out_bwd")
    ds_f, da1_f, da2_f = _s5_carry_bwd(dhp_f, hp_f, a1_f, a2_f, False, nctx5, "s5_carry_bwd_f")
    ds_b, da1_b, da2_b = _s5_carry_bwd(dhp_b, hp_b, a1_b, a2_b, True, nctx5, "s5_carry_bwd_b")
    du_g, dmb_f, dmb_b = _s5_inc_bwd(du1, u_g, ds_f, ds_b, mb_f, mb_b, "s5_inc_bwd")
    zero_p = jnp.zeros((S5G, S5P, SB), f32)
    gf = _s5_build_bwd(s5p["f"], (g_m, dmb_f, dmc_f, da1_f[:, None, :], da2_f[:, None, :]), (zero_p, zero_p), False, "s5_build_bwd_f")
    gb = _s5_build_bwd(s5p["b"], (g_m, dmb_b, dmc_b, da1_b[:, None, :], da2_b[:, None, :]), (gf[3], gf[4]), True, "s5_build_bwd_b")
    g_bre, g_bim = gb[3][:, :, :S5N].transpose(0, 2, 1), gb[3][:, :, S5N:].transpose(0, 2, 1)
    g_cre, g_cim = gb[4][:, :, :S5N], gb[4][:, :, S5N:]

    grad_x, dp_ext, h1, g_nw1, dmx1, dmc1 = _f1_bwd(
        x2, ctx2, modx, modc, nw1, w_in_t, cosf, sins, dx1, (_from_groups(du_g), dud_e, dq_f, dq_b, dk_f, dk_b, dv_f, dv_b, dg_e), "f1_bwd")
    gw_in_t = _matmul_tn(dp_ext, h1, "dw_in").reshape(NDEV, -1, D)
    (l_in,) = _exchange([gw_in_t], True, "scatter_dw_in")

    dmx = dmx1 + dmx2
    dmx = dmx.at[2].set(g_gate1[0]).at[5].set(g_gate2[0])
    dm_me = jnp.stack([dmx.reshape(-1), dmc1.reshape(-1)], axis=0)
    dm_all = _all_gather_small(jnp.pad(dm_me, ((0, 6), (0, 0))), "gather_dmod")
    dmx_all, dmc_all = dm_all[:, 0, :], dm_all[:, 1, :]
    my_cols = lambda a: lax.dynamic_slice(a, (0, me * ncol), (NDEV, ncol))
    gw_mod, g_bmod, dc9 = _ada_bwd(c9, dmx_all, dmc_all, my_cols(dmx_all), my_cols(dmc_all), w_mod_l, "ada_bwd")

    small = {
        "conv_w": g_cw, "c_ctx": dc9[8], "norm1_w": g_nw1, "s5_lambda_re_f": gf[0][:, 0, :S5N], "s5_lambda_im_f": gf[1][:, 0, :S5N],
        "s5_log_step_f": gf[2], "s5_lambda_re_b": gb[0][:, 0, :S5N], "s5_lambda_im_b": gb[1][:, 0, :S5N], "s5_log_step_b": gb[2],
        "s5_b_re": g_bre, "s5_b_im": g_bim, "s5_c_re": g_cre, "s5_c_im": g_cim, "s5_d": g_d, "s5_b_glu": g_bglu,
        "ret_log_decay_f": gld_f[:RH, 0], "ret_log_decay_b": gld_b[:RH, 0], "norm2_w": g_nw2, "conv_b": g_cb, "final_norm_w": g_fnw,
    }
    packed, soffs = _pack_small([small[n].astype(f32) for n in SMALL])
    red = _all_reduce_small(packed, "reduce_small")
    sshapes = [(3, DFF) if n == "conv_w" else W[n].shape for n in SMALL]
    G = dict(zip(SMALL, _unpack_small(red, soffs, sshapes)))
    G["conv_w"] = lax.dynamic_slice(G["conv_w"], (0, me * per_cv), (3, per_cv))[None]
    G["b_mod"] = g_bmod.reshape(b_mod.shape)
    G["w_mod"] = gw_mod[None]
    G["w_in"] = _sum8(l_in, "sum_dw_in").T[None]
    G["w_up"] = jnp.concatenate([_sum8(l_up1, "sum_dw_up1"), _sum8(l_up2, "sum_dw_up2")], axis=0).T[None]
    G["w_out"] = _sum8(l_out, "sum_dw_out")[None]
    G["w_down"] = _sum8(l_down, "sum_dw_down")[None]
    G["s5_w_glu"] = _sum8(l_glu, "sum_dw_glu")[None]

    delta, new_m, new_v = {}, {}, {}
    sm_names = SMALL[1:] + ["b_mod"]
    rows = lambda a: a.reshape(-1, a.shape[-1])
    outs = _adamw_many(*[[rows(d[n]) for n in sm_names] for d in (W, G, M, V)], "adamw_small")
    for dst, src in zip((delta, new_m, new_v), outs):
        dst.update({n: a.reshape(W[n].shape) for n, a in zip(sm_names, src)})
    for n in ["w_mod", "w_in", "w_out", "w_up", "w_down", "s5_w_glu", "conv_w"]:
        d, nm, nv = _adamw(W[n][0], G[n][0], M[n][0], V[n][0], "adamw_" + n)
        delta[n], new_m[n], new_v[n] = d[None], nm[None], nv[None]

    loss = lax.psum(loss_acc[0, 0], ("x", "y", "c"))
    return (loss, grad_x[None], *[G[n] for n in WEIGHTS], *[delta[n] for n in WEIGHTS], *[new_m[n] for n in WEIGHTS],
            *[new_v[n] for n in WEIGHTS])
```

```python
import functools

import numpy as np
import jax
import jax.numpy as jnp
from jax import lax
from jax.experimental import pallas as pl
from jax.experimental.pallas import tpu as pltpu

f32, bf16 = jnp.float32, jnp.bfloat16

D = 1024
S5W, S5G, S5P, S5N = 512, 32, 16, 64
TC = 16
TCP = TC * S5P
SB = 2 * S5N
GBK = 16
CARRY_UNROLL = 8
RH, DH = 4, 128
RW = RH * DH
INC = S5W + 4 * RW
DFF = 2816
T = 128
R = 256
RF = 128
HALO = 8
EPS = 1e-6
ROPE_THETA = 10000.0
GRID_W = 64
NDEV = 8
LR, B1, B2, AEPS, WD, STEP = 0.001, 0.9, 0.999, 1e-08, 0.01, 10
VMEM_LIMIT = 60 * 1024 * 1024
ACC_TILE_BYTES = 6 * 1024 * 1024
MESH = pl.DeviceIdType.MESH

_CP = functools.partial(pltpu.CompilerParams, vmem_limit_bytes=VMEM_LIMIT)
_ARB = ("arbitrary",)
_ANY = pl.BlockSpec(memory_space=pl.ANY)


def _dg(a, b, dims):
    return lax.dot_general(a.astype(bf16), b.astype(bf16), (dims, ((), ())), preferred_element_type=f32)


@jax.custom_vjp
def dnn(a, b):
    return _dg(a, b, ((1,), (0,)))


@jax.custom_vjp
def dnt(a, b):
    return _dg(a, b, ((1,), (1,)))


@jax.custom_vjp
def dtn(a, b):
    return _dg(a, b, ((0,), (0,)))


dnn.defvjp(lambda a, b: (dnn(a, b), (a, b)), lambda r, g: (dnt(g, r[1]).astype(r[0].dtype), dtn(r[0], g).astype(r[1].dtype)))
dnt.defvjp(lambda a, b: (dnt(a, b), (a, b)), lambda r, g: (dnn(g, r[1]).astype(r[0].dtype), dtn(g, r[0]).astype(r[1].dtype)))
dtn.defvjp(lambda a, b: (dtn(a, b), (a, b)), lambda r, g: (dnt(r[1], g).astype(r[0].dtype), dnn(r[0], g).astype(r[1].dtype)))


@jax.custom_vjp
def _dnn_const(a, w, wt):
    return dnn(a, w)


_dnn_const.defvjp(lambda a, w, wt: (dnn(a, w), wt), lambda wt, g: (dnn(g, wt), None, None))


def _rms(t, w):
    return t * lax.rsqrt(jnp.mean(t * t, axis=-1, keepdims=True) + EPS) * w


def _mod(h, shift, scale):
    return h * (1.0 + scale) + shift


def _const_spec(shape):
    n = len(shape)
    return pl.BlockSpec(shape, lambda i, _n=n: (0,) * _n, pipeline_mode=pl.Buffered(1))


def _acc_spec(shape):
    n = len(shape)
    return pl.BlockSpec(shape, lambda i, _n=n: (0,) * _n)


def _me():
    return 4 * lax.axis_index("x") + 2 * lax.axis_index("y") + lax.axis_index("c")


def _peer(r):
    x, y, c = lax.axis_index("x"), lax.axis_index("y"), lax.axis_index("c")
    px = 1 - x if (r >> 2) & 1 else x
    py = 1 - y if (r >> 1) & 1 else y
    pc = 1 - c if r & 1 else c
    return (px, py, pc), 4 * px + 2 * py + pc


def _all_gather_small(v, name):
    r, c = v.shape

    def body(v_ref, out_ref, send_sems, recv_sems):
        me = _me()
        out_ref[me] = v_ref[...]
        sends = []
        for k in range(1, NDEV):
            peer, _ = _peer(k)
            cp = pltpu.make_async_remote_copy(src_ref=v_ref, dst_ref=out_ref.at[me], send_sem=send_sems.at[k - 1],
                                              recv_sem=recv_sems.at[k - 1], device_id=peer, device_id_type=MESH)
            cp.start()
            sends.append(cp)
        for k in range(1, NDEV):
            peer, pidx = _peer(k)
            pltpu.make_async_remote_copy(src_ref=v_ref, dst_ref=out_ref.at[pidx], send_sem=send_sems.at[k - 1],
                                         recv_sem=recv_sems.at[k - 1], device_id=peer, device_id_type=MESH).wait_recv()
        for cp in sends:
            cp.wait_send()

    return pl.pallas_call(
        body, name=name, out_shape=jax.ShapeDtypeStruct((NDEV, r, c), v.dtype),
        in_specs=[pl.BlockSpec(memory_space=pltpu.VMEM)], out_specs=pl.BlockSpec(memory_space=pltpu.VMEM),
        scratch_shapes=[pltpu.SemaphoreType.DMA((NDEV - 1,)), pltpu.SemaphoreType.DMA((NDEV - 1,))],
        compiler_params=_CP(),
    )(v)


def _all_reduce_small(v, name):
    r, c = v.shape

    def body(v_ref, out_ref, land, send_sems, recv_sems):
        me = _me()
        land[me] = v_ref[...]
        sends = []
        for k in range(1, NDEV):
            peer, _ = _peer(k)
            cp = pltpu.make_async_remote_copy(src_ref=v_ref, dst_ref=land.at[me], send_sem=send_sems.at[k - 1],
                                              recv_sem=recv_sems.at[k - 1], device_id=peer, device_id_type=MESH)
            cp.start()
            sends.append(cp)
        for k in range(1, NDEV):
            peer, pidx = _peer(k)
            pltpu.make_async_remote_copy(src_ref=v_ref, dst_ref=land.at[pidx], send_sem=send_sems.at[k - 1],
                                         recv_sem=recv_sems.at[k - 1], device_id=peer, device_id_type=MESH).wait_recv()
        for cp in sends:
            cp.wait_send()
        acc = land[0]
        for j in range(1, NDEV):
            acc = acc + land[j]
        out_ref[...] = acc

    return pl.pallas_call(
        body, name=name, out_shape=jax.ShapeDtypeStruct((r, c), v.dtype),
        in_specs=[pl.BlockSpec(memory_space=pltpu.VMEM)], out_specs=pl.BlockSpec(memory_space=pltpu.VMEM),
        scratch_shapes=[pltpu.VMEM((NDEV, r, c), v.dtype), pltpu.SemaphoreType.DMA((NDEV - 1,)),
                        pltpu.SemaphoreType.DMA((NDEV - 1,))],
        compiler_params=_CP(),
    )(v)


class _Exchange:
    def __init__(self, srcs, dsts, send_sems, recv_sems, local_sems, scatter):
        me = _me()
        n = len(srcs)
        self.sends, self.recvs, self.locals = [], [], []
        for a, (s, d) in enumerate(zip(srcs, dsts)):
            self.locals.append(pltpu.make_async_copy(s.at[me] if scatter else s, d.at[me], local_sems.at[a]))
        for k in range(1, NDEV):
            peer, pidx = _peer(k)
            for a, (s, d) in enumerate(zip(srcs, dsts)):
                src = s.at[pidx] if scatter else s
                sem = (k - 1) * n + a
                for dst, out in ((d.at[me], self.sends), (d.at[pidx], self.recvs)):
                    out.append(pltpu.make_async_remote_copy(src_ref=src, dst_ref=dst, send_sem=send_sems.at[sem],
                                                            recv_sem=recv_sems.at[sem], device_id=peer, device_id_type=MESH))

    def start(self):
        for cp in self.locals + self.sends:
            cp.start()

    def wait(self):
        for cp in self.recvs:
            cp.wait_recv()
        for cp in self.sends:
            cp.wait_send()
        for cp in self.locals:
            cp.wait()


def _exchange_shapes(arrays, scatter):
    return [jax.ShapeDtypeStruct(a.shape if scatter else (NDEV,) + a.shape, a.dtype) for a in arrays]


def _exchange_sems(n):
    return [pltpu.SemaphoreType.DMA(((NDEV - 1) * n,)), pltpu.SemaphoreType.DMA(((NDEV - 1) * n,)), pltpu.SemaphoreType.DMA((n,))]


def _exchange(arrays, scatter, name):
    n = len(arrays)

    def body(*refs):
        ex = _Exchange(refs[:n], refs[n:2 * n], *refs[2 * n:], scatter)
        ex.start()
        ex.wait()

    return pl.pallas_call(body, name=name, out_shape=_exchange_shapes(arrays, scatter), in_specs=[_ANY] * n,
                          out_specs=[_ANY] * n, scratch_shapes=_exchange_sems(n), compiler_params=_CP())(*arrays)


class _Cargo:
    def __init__(self, cargo):
        self.arrays, self.scatter = cargo if cargo else ([], False)
        self.n = len(self.arrays)

    def in_specs(self):
        return [_ANY] * self.n

    def out_shapes(self):
        return _exchange_shapes(self.arrays, self.scatter)

    def sems(self):
        return _exchange_sems(self.n) if self.n else []

    def split(self, refs, n_in, n_out, n_scratch):
        n = self.n
        return refs[:n_in], refs[n_in + n:n_in + n + n_out], refs[n_in + 2 * n + n_out:n_in + 2 * n + n_out + n_scratch]

    def ride(self, refs, n_in, n_out, nsteps):
        if not self.n:
            return
        n = self.n
        ex = _Exchange(refs[n_in:n_in + n], refs[n_in + n + n_out:n_in + 2 * n + n_out], *refs[-3:], self.scatter)

        @pl.when(pl.program_id(0) == 0)
        def _():
            ex.start()

        @pl.when(pl.program_id(0) == nsteps - 1)
        def _():
            ex.wait()


def _sum8(land, name):
    _, r, c = land.shape
    rb = next((b for b in (256, 64, 32) if r % b == 0), r)

    def body(l_ref, o_ref):
        acc = l_ref[0].astype(f32)
        for j in range(1, NDEV):
            acc = acc + l_ref[j].astype(f32)
        o_ref[...] = acc

    return pl.pallas_call(
        body, name=name, grid=(r // rb,), out_shape=jax.ShapeDtypeStruct((r, c), f32),
        in_specs=[pl.BlockSpec((NDEV, rb, c), lambda i: (0, i, 0))], out_specs=pl.BlockSpec((rb, c), lambda i: (i, 0)),
        compiler_params=_CP(dimension_semantics=("parallel",)),
    )(land)


def _ada_fwd(c9, w_mod_l, name):
    def body(c_ref, w_ref, o_ref):
        o_ref[...] = dnn(jax.nn.silu(c_ref[...]), w_ref[...])

    return pl.pallas_call(body, name=name, out_shape=jax.ShapeDtypeStruct((16, w_mod_l.shape[1]), f32),
                          compiler_params=_CP())(c9, w_mod_l)


def _mod_select(m_all, b_mod6, name):
    def body(m_ref, b_ref, mx_ref, mc_ref):
        me = _me()
        mx_ref[...] = m_ref[me] + b_ref[...]
        mc_ref[...] = m_ref[8] + b_ref[...]

    return pl.pallas_call(body, name=name, out_shape=[jax.ShapeDtypeStruct((6, D), f32)] * 2, compiler_params=_CP())(m_all, b_mod6)


def _ada_bwd(c9, dmx_all, dmc_all, dmx_l, dmc_l, w_mod_l, name):
    ncol = w_mod_l.shape[1]

    def rowsum(r):
        acc = r[0:1]
        for j in range(1, NDEV):
            acc = acc + r[j:j + 1]
        return acc

    def body(c_ref, xa_ref, ca_ref, xl_ref, cl_ref, w_ref, gw_ref, gb_ref, dc_ref):
        s9, vjp = jax.vjp(jax.nn.silu, c_ref[...])
        dm9 = jnp.concatenate([xl_ref[...], rowsum(cl_ref[...]), jnp.zeros((7, ncol), f32)], axis=0)
        gw_ref[...] = dtn(s9, dm9)
        gb_ref[...] = rowsum(xa_ref[...]) + rowsum(ca_ref[...])
        dc_ref[...] = vjp(dnt(dm9, w_ref[...]))[0]

    return pl.pallas_call(
        body, name=name,
        out_shape=[jax.ShapeDtypeStruct((D, ncol), f32), jax.ShapeDtypeStruct((1, 6 * D), f32), jax.ShapeDtypeStruct((16, D), f32)],
        compiler_params=_CP())(c9, dmx_all, dmc_all, dmx_l, dmc_l, w_mod_l)


def _lane_sign(rank):
    shape = (1,) * (rank - 1) + (SB,)
    return jnp.where(lax.broadcasted_iota(jnp.int32, shape, rank - 1) < S5N, -1.0, 1.0)


def _s5_build_fn(lre2, lim2, ls, bn, bs, cn, cs, rev):
    sg = _lane_sign(3)
    s = jnp.exp(ls)
    ar, ai = lre2 * s, lim2 * s
    e = jnp.exp(ar)
    nr, ni = e * jnp.cos(ai) - 1.0, e * jnp.sin(ai)
    den = lre2 * lre2 + lim2 * lim2
    cr, ci = (nr * lre2 + ni * lim2) / den, (ni * lre2 - nr * lim2) / den
    bbn = cr * bn + (ci * sg) * bs
    bbs = cr * bs - (ci * sg) * bn

    def powers(ex):
        m, ang = jnp.exp(ex * ar), ex * ai
        return m * jnp.cos(ang), m * jnp.sin(ang) * sg

    def times(tabs, xn, xs):
        f1, f2 = tabs
        return f1[:, :, None, :] * xn[:, None, :, :] + f2[:, :, None, :] * xs[:, None, :, :]

    t = lax.broadcasted_iota(jnp.int32, (1, TC, 1), 1).astype(f32)
    if rev:
        e_src, e_dst, e_out, e_in = t - (TC - 1.0), (TC - 1.0) - t, t, TC - t
    else:
        e_src, e_dst, e_out, e_in = -t, t, (TC - 1.0) - t, t + 1.0
    g = lre2.shape[0]
    flat = lambda a: a.reshape(g, TCP, SB)
    conj = -_lane_sign(4)
    ll = flat(times(powers(e_src), bbn, bbs))
    rr = flat(times(powers(e_dst), cn, cs) * conj)
    mb = flat(times(powers(e_out), bbn, bbs))
    mct = flat(times(powers(e_in), cn, cs) * conj)
    a1, a2 = powers(float(TC))
    row = lax.broadcasted_iota(jnp.int32, (TCP, TCP), 0) // S5P
    col = lax.broadcasted_iota(jnp.int32, (TCP, TCP), 1) // S5P
    mask = jnp.where((col <= row) if rev else (col >= row), 1.0, 0.0)
    m = jnp.concatenate([dnt(ll[j], rr[j])[None] for j in range(g)], axis=0) * mask
    return m, mb, mct, a1, a2


def _gspec(*tail):
    nt = len(tail)
    return pl.BlockSpec((GBK,) + tail, lambda i, _n=nt: (i,) + (0,) * _n)


NRB = 4
_S5_GRID = dict(dimension_semantics=("parallel", "arbitrary"))


def _act_spec(nc):
    return pl.BlockSpec((nc // NRB, GBK, TCP), lambda i, r: (r, i, 0))


def _st_spec(nc):
    return pl.BlockSpec((nc // NRB, GBK, SB), lambda i, r: (r, i, 0))


def _gspec2(*tail):
    nt = len(tail)
    return pl.BlockSpec((GBK,) + tail, lambda i, r, _n=nt: (i,) + (0,) * _n)


def _s5_build(params, rev, name):
    def body(l1, l2, ls, bn, bs, cn, cs, m_ref, mb_ref, mc_ref, a1_ref, a2_ref):
        m, mb, mct, a1, a2 = _s5_build_fn(l1[...], l2[...], ls[...], bn[...], bs[...], cn[...], cs[...], rev)
        m_ref[...], mb_ref[...], mc_ref[...] = m.astype(bf16), mb.astype(bf16), mct.astype(bf16)
        a1_ref[...], a2_ref[...] = a1, a2

    vec, pm = _gspec(1, SB), _gspec(S5P, SB)
    return pl.pallas_call(
        body, name=name, grid=(S5G // GBK,),
        in_specs=[vec, vec, _gspec(1, 1), pm, pm, pm, pm],
        out_specs=[_gspec(TCP, TCP), _gspec(TCP, SB), _gspec(TCP, SB), vec, vec],
        out_shape=[jax.ShapeDtypeStruct((S5G, TCP, TCP), bf16), jax.ShapeDtypeStruct((S5G, TCP, SB), bf16),
                   jax.ShapeDtypeStruct((S5G, TCP, SB), bf16), jax.ShapeDtypeStruct((S5G, 1, SB), f32),
                   jax.ShapeDtypeStruct((S5G, 1, SB), f32)],
        compiler_params=_CP(dimension_semantics=("parallel",)),
    )(*params)


def _s5_build_bwd(params, cots, prev, rev, name):
    def body(l1, l2, ls, bn, bs, cn, cs, dm, dmb, dmc, da1, da2, pb, pc, gl1, gl2, gls, gb, gc):
        prim = (l1[...], l2[...], ls[...], bn[...], bs[...], cn[...], cs[...])
        _, vjp = jax.vjp(functools.partial(_s5_build_fn, rev=rev), *prim)
        d1, d2, dls, dbn, dbs, dcn, dcs = vjp((dm[...], dmb[...], dmc[...], da1[...], da2[...]))
        gl1[...] = d1 + pltpu.roll(d1, S5N, axis=2)
        gl2[...] = d2 + pltpu.roll(d2, S5N, axis=2)
        gls[...] = dls
        gb[...] = dbn + pltpu.roll(dbs, S5N, axis=2) + pb[...]
        gc[...] = dcn + pltpu.roll(dcs, S5N, axis=2) + pc[...]

    vec, pm, big = _gspec(1, SB), _gspec(S5P, SB), _gspec(TCP, SB)
    return pl.pallas_call(
        body, name=name, grid=(S5G // GBK,),
        in_specs=[vec, vec, _gspec(1, 1), pm, pm, pm, pm, _gspec(TCP, TCP), big, big, vec, vec, pm, pm],
        out_specs=[vec, vec, _gspec(1, 1), pm, pm],
        out_shape=[jax.ShapeDtypeStruct((S5G, 1, SB), f32), jax.ShapeDtypeStruct((S5G, 1, SB), f32),
                   jax.ShapeDtypeStruct((S5G, 1, 1), f32), jax.ShapeDtypeStruct((S5G, S5P, SB), f32),
                   jax.ShapeDtypeStruct((S5G, S5P, SB), f32)],
        compiler_params=_CP(dimension_semantics=("parallel",)),
    )(*params, *cots, *prev)


def _s5_inc(u, mb_f, mb_b, name):
    nc = u.shape[0]

    def body(u_ref, mf_ref, mb_ref, sf_ref, sb_ref):
        for j in range(GBK):
            sf_ref[:, j, :] = jnp.dot(u_ref[:, j, :], mf_ref[j], preferred_element_type=f32)
            sb_ref[:, j, :] = jnp.dot(u_ref[:, j, :], mb_ref[j], preferred_element_type=f32)

    return pl.pallas_call(
        body, name=name, grid=(S5G // GBK, NRB), in_specs=[_act_spec(nc), _gspec2(TCP, SB), _gspec2(TCP, SB)],
        out_specs=[_st_spec(nc), _st_spec(nc)], out_shape=[jax.ShapeDtypeStruct((nc, S5G, SB), f32)] * 2,
        compiler_params=_CP(**_S5_GRID),
    )(u, mb_f, mb_b)


def _idx_fwd(nctx, nch):
    return lambda i: i


def _idx_rev(nctx, nch):
    return lambda i: jnp.where(i < nctx, nctx - 1 - i, nch + nctx - 1 - i)


def _carry_loop(nc, step, init):
    def trip(i, c):
        for k in range(CARRY_UNROLL):
            c = step(i * CARRY_UNROLL + k, c)
        return c

    return lax.fori_loop(0, nc // CARRY_UNROLL, trip, init)


def _s5_carry(s_f, s_b, a_f, a_b, nctx, name):
    nc = s_f.shape[0]
    idx_b = _idx_rev(nctx, nc)

    def body(sf_ref, sb_ref, f1_ref, f2_ref, b1_ref, b2_ref, hf_ref, hb_ref):
        f1, f2, b1, b2 = f1_ref[...], f2_ref[...], b1_ref[...], b2_ref[...]

        def step(i, c):
            hf, hfs, hb, hbs = c
            rb = idx_b(i)
            hf_ref[i] = hf
            hb_ref[rb] = hb
            sf, sb = sf_ref[i], sb_ref[rb]
            return (f1 * hf + f2 * hfs + sf, f1 * hfs - f2 * hf + pltpu.roll(sf, S5N, axis=1),
                    b1 * hb + b2 * hbs + sb, b1 * hbs - b2 * hb + pltpu.roll(sb, S5N, axis=1))

        z = jnp.zeros((S5G, SB), f32)
        _carry_loop(nc, step, (z, z, z, z))

    return pl.pallas_call(body, name=name, out_shape=[jax.ShapeDtypeStruct(s_f.shape, f32)] * 2,
                          compiler_params=_CP())(s_f, s_b, *a_f, *a_b)


def _s5_carry_bwd(dhp, hp, a1, a2, rev, nctx, name):
    nc = hp.shape[0]
    idx = (_idx_rev if rev else _idx_fwd)(nctx, nc)

    def body(dhp_ref, hp_ref, a1_ref, a2_ref, ds_ref, d1_ref, d2_ref):
        f1, f2 = a1_ref[...], a2_ref[...]

        def step(k, carry):
            ab, abs_, d1, d2 = carry
            r = idx(nc - 1 - k)
            ds_ref[r] = ab
            h, dh = hp_ref[r], dhp_ref[r]
            return (dh + f1 * ab - f2 * abs_, pltpu.roll(dh, S5N, axis=1) + f1 * abs_ + f2 * ab,
                    d1 + ab * h, d2 + ab * pltpu.roll(h, S5N, axis=1))

        z = jnp.zeros((S5G, SB), f32)
        _, _, d1, d2 = _carry_loop(nc, step, (z, z, z, z))
        d1_ref[...], d2_ref[...] = d1, d2

    return pl.pallas_call(
        body, name=name,
        out_shape=[jax.ShapeDtypeStruct(hp.shape, f32), jax.ShapeDtypeStruct((S5G, SB), f32), jax.ShapeDtypeStruct((S5G, SB), f32)],
        compiler_params=_CP())(dhp, hp, a1, a2)


def _s5_out(u, m_f, m_b, hp_f, hp_b, mc_f, mc_b, name):
    nc = u.shape[0]

    def body(u_ref, mf_ref, mb_ref, hf_ref, hb_ref, cf_ref, cb_ref, y_ref):
        for j in range(GBK):
            uj = u_ref[:, j, :]
            y_ref[:, j, :] = (jnp.dot(uj, mf_ref[j], preferred_element_type=f32) + jnp.dot(uj, mb_ref[j], preferred_element_type=f32)
                        + dnt(hf_ref[:, j, :], cf_ref[j]) + dnt(hb_ref[:, j, :], cb_ref[j])).astype(bf16)

    return pl.pallas_call(
        body, name=name, grid=(S5G // GBK, NRB),
        in_specs=[_act_spec(nc), _gspec2(TCP, TCP), _gspec2(TCP, TCP), _st_spec(nc), _st_spec(nc), _gspec2(TCP, SB), _gspec2(TCP, SB)],
        out_specs=_act_spec(nc), out_shape=jax.ShapeDtypeStruct((nc, S5G, TCP), bf16),
        compiler_params=_CP(**_S5_GRID),
    )(u, m_f, m_b, hp_f, hp_b, mc_f, mc_b)


def _s5_out_bwd(dy, u, m_f, m_b, hp_f, hp_b, mc_f, mc_b, name):
    nc = u.shape[0]

    def body(dy_ref, u_ref, mf_ref, mb_ref, hf_ref, hb_ref, cf_ref, cb_ref, du_ref, g_ref, dhf_ref, dhb_ref, dcf_ref, dcb_ref):
        @pl.when(pl.program_id(1) == 0)
        def _():
            for r in (g_ref, dcf_ref, dcb_ref):
                r[...] = jnp.zeros_like(r)

        for j in range(GBK):
            dyj = dy_ref[:, j, :]
            du_ref[:, j, :] = dnt(dyj, mf_ref[j]) + dnt(dyj, mb_ref[j])
            g_ref[j] += dtn(u_ref[:, j, :], dyj)
            dhf_ref[:, j, :] = dnn(dyj, cf_ref[j])
            dhb_ref[:, j, :] = dnn(dyj, cb_ref[j])
            dcf_ref[j] += dtn(dyj, hf_ref[:, j, :])
            dcb_ref[j] += dtn(dyj, hb_ref[:, j, :])

    sshape = jax.ShapeDtypeStruct((nc, S5G, SB), f32)
    cshape = jax.ShapeDtypeStruct((S5G, TCP, SB), f32)
    return pl.pallas_call(
        body, name=name, grid=(S5G // GBK, NRB),
        in_specs=[_act_spec(nc), _act_spec(nc), _gspec2(TCP, TCP), _gspec2(TCP, TCP), _st_spec(nc), _st_spec(nc), _gspec2(TCP, SB),
                  _gspec2(TCP, SB)],
        out_specs=[_act_spec(nc), _gspec2(TCP, TCP), _st_spec(nc), _st_spec(nc), _gspec2(TCP, SB), _gspec2(TCP, SB)],
        out_shape=[jax.ShapeDtypeStruct((nc, S5G, TCP), f32), jax.ShapeDtypeStruct((S5G, TCP, TCP), f32), sshape, sshape, cshape, cshape],
        compiler_params=_CP(**_S5_GRID),
    )(dy, u, m_f, m_b, hp_f, hp_b, mc_f, mc_b)


def _s5_inc_bwd(du1, u, ds_f, ds_b, mb_f, mb_b, name):
    nc = u.shape[0]

    def body(du1_ref, u_ref, dsf_ref, dsb_ref, mf_ref, mb_ref, du_ref, dmf_ref, dmb_ref):
        @pl.when(pl.program_id(1) == 0)
        def _():
            dmf_ref[...] = jnp.zeros_like(dmf_ref)
            dmb_ref[...] = jnp.zeros_like(dmb_ref)

        for j in range(GBK):
            dsf, dsb = dsf_ref[:, j, :], dsb_ref[:, j, :]
            du_ref[:, j, :] = (du1_ref[:, j, :] + dnt(dsf, mf_ref[j]) + dnt(dsb, mb_ref[j])).astype(bf16)
            dmf_ref[j] += dtn(u_ref[:, j, :], dsf)
            dmb_ref[j] += dtn(u_ref[:, j, :], dsb)

    cshape = jax.ShapeDtypeStruct((S5G, TCP, SB), f32)
    return pl.pallas_call(
        body, name=name, grid=(S5G // GBK, NRB),
        in_specs=[_act_spec(nc), _act_spec(nc), _st_spec(nc), _st_spec(nc), _gspec2(TCP, SB), _gspec2(TCP, SB)],
        out_specs=[_act_spec(nc), _gspec2(TCP, SB), _gspec2(TCP, SB)],
        out_shape=[jax.ShapeDtypeStruct((nc, S5G, TCP), bf16), cshape, cshape],
        compiler_params=_CP(**_S5_GRID),
    )(du1, u, ds_f, ds_b, mb_f, mb_b)


def _to_groups(a):
    n = a.shape[0]
    return a.reshape(n // TC, TC, S5G, S5P).transpose(0, 2, 1, 3).reshape(n // TC, S5G, TCP)


def _from_groups(a):
    nc = a.shape[0]
    return a.reshape(nc, S5G, TC, S5P).transpose(0, 2, 1, 3).reshape(nc * TC, S5W)


def _swap_pairs(t):
    lane = lax.broadcasted_iota(jnp.int32, t.shape, 1)
    return jnp.where(lane % 2 == 0, pltpu.roll(t, DH - 1, axis=1), pltpu.roll(t, 1, axis=1))


def _rot(t, cosf, sins):
    return t * cosf + _swap_pairs(t) * sins


def _rot_t(d, cosf, sins):
    return d * cosf - _swap_pairs(d) * sins


def _ret_chunk(qr, kr, v, rp, ld, rev):
    pos = lax.broadcasted_iota(jnp.int32, (T, 1), 0).astype(f32)
    diff = pos - lax.broadcasted_iota(jnp.int32, (1, T), 1).astype(f32)
    if rev:
        keep, dist = diff < 0, jnp.maximum(-diff, 0.0)
        xi, zeta = jnp.exp(ld * (T - pos)), jnp.exp(ld * pos)
    else:
        keep, dist = diff >= 0, jnp.maximum(diff, 0.0)
        xi, zeta = jnp.exp(ld * (pos + 1.0)), jnp.exp(ld * (T - 1.0 - pos))
    dm = jnp.where(keep, jnp.exp(ld * dist), 0.0)
    out = dnn(dnt(qr, kr) * dm, v) + dnn(qr * xi, rp)
    rn = jnp.exp(ld * float(T)) * rp + dtn(kr * zeta, v)
    return out, rn


def _ret_fwd(p_ext, ld8, rev, nctx, name, cargo=None):
    n = p_ext.shape[0]
    nch = n // T
    idx = (_idx_rev if rev else _idx_fwd)(nctx, nch)
    cg = _Cargo(cargo)

    def body(*refs):
        (q_ref, k_ref, v_ref, ld_ref), (o_ref, rp_ref), (r_s,) = cg.split(refs, 4, 2, 1)
        cg.ride(refs, 4, 2, nch)

        @pl.when(pl.program_id(0) == 0)
        def _():
            r_s[...] = jnp.zeros_like(r_s)

        for h in range(RH):
            sl = slice(h * DH, (h + 1) * DH)
            rp = r_s[h]
            rp_ref[0, h] = rp
            out, rn = _ret_chunk(q_ref[:, sl].astype(f32), k_ref[:, sl].astype(f32), v_ref[:, sl].astype(f32), rp,
                                 ld_ref[h:h + 1, 0:1], rev)
            r_s[h] = rn
            o_ref[:, sl] = out

    def colspec(cb):
        return pl.BlockSpec((T, RW), lambda i, _c=cb: (idx(i), _c))

    return pl.pallas_call(
        body, name=name, grid=(nch,),
        in_specs=[colspec(1), colspec(2), colspec(3), _const_spec((8, 128))] + cg.in_specs(),
        out_specs=[pl.BlockSpec((T, RW), lambda i: (idx(i), 0)), pl.BlockSpec((1, RH, DH, DH), lambda i: (i, 0, 0, 0))] + cg.in_specs(),
        out_shape=[jax.ShapeDtypeStruct((n, RW), f32), jax.ShapeDtypeStruct((nch, RH, DH, DH), f32)] + cg.out_shapes(),
        scratch_shapes=[pltpu.VMEM((RH, DH, DH), f32)] + cg.sems(),
        compiler_params=_CP(dimension_semantics=_ARB),
    )(p_ext, p_ext, p_ext, ld8, *cg.arrays)


def _ret_bwd(p_ext, ld8, rprev, do_ext, rev, nctx, name, cargo=None):
    n = p_ext.shape[0]
    nch = n // T
    idx0 = (_idx_rev if rev else _idx_fwd)(nctx, nch)
    idx = lambda j: idx0(nch - 1 - j)
    cg = _Cargo(cargo)

    def body(*refs):
        ins, (dq_ref, dk_ref, dv_ref, dld_ref), (dr_s,) = cg.split(refs, 6, 4, 1)
        q_ref, k_ref, v_ref, ld_ref, rp_ref, do_ref = ins
        cg.ride(refs, 6, 4, nch)

        @pl.when(pl.program_id(0) == 0)
        def _():
            dr_s[...] = jnp.zeros_like(dr_s)
            dld_ref[...] = jnp.zeros_like(dld_ref)

        for h in range(RH):
            sl = slice(h * DH, (h + 1) * DH)
            _, vjp = jax.vjp(functools.partial(_ret_chunk, rev=rev), q_ref[:, sl].astype(f32), k_ref[:, sl].astype(f32),
                             v_ref[:, sl].astype(f32), rp_ref[0, h], ld_ref[h:h + 1, 0:1])
            dqr, dkr, dv, drp, dld = vjp((do_ref[:, sl], dr_s[h]))
            dr_s[h] = drp
            dq_ref[:, sl], dk_ref[:, sl], dv_ref[:, sl] = dqr, dkr, dv
            dld_ref[h:h + 1, :] += jnp.broadcast_to(dld, (1, 128))

    def colspec(cb):
        return pl.BlockSpec((T, RW), lambda j, _c=cb: (idx(j), _c))

    ospec = pl.BlockSpec((T, RW), lambda j: (idx(j), 0))
    oshape = jax.ShapeDtypeStruct((n, RW), f32)
    return pl.pallas_call(
        body, name=name, grid=(nch,),
        in_specs=[colspec(1), colspec(2), colspec(3), _const_spec((8, 128)),
                  pl.BlockSpec((1, RH, DH, DH), lambda j: (nch - 1 - j, 0, 0, 0)), ospec] + cg.in_specs(),
        out_specs=[ospec, ospec, ospec, _acc_spec((8, 128))] + cg.in_specs(),
        out_shape=[oshape, oshape, oshape, jax.ShapeDtypeStruct((8, 128), f32)] + cg.out_shapes(),
        scratch_shapes=[pltpu.VMEM((RH, DH, DH), f32)] + cg.sems(),
        compiler_params=_CP(dimension_semantics=_ARB),
    )(p_ext, p_ext, p_ext, ld8, rprev, do_ext, *cg.arrays)


def _qk_heads(p, fn_q, fn_k):
    heads = lambda base, fn: [fn(p[:, base + h * DH:base + (h + 1) * DH]) for h in range(RH)]
    return jnp.concatenate([p[:, :S5W]] + heads(S5W, fn_q) + heads(S5W + RW, fn_k) + [p[:, S5W + 2 * RW:]], axis=1)


def _f1_fwd(x, ctx, modx, modc, nw1, w_in_n, cosf, sins, name, cargo=None):
    L = x.shape[0]
    nb = L // R + 1
    scale = DH ** -0.5
    cg = _Cargo(cargo)

    def body(*refs):
        (x_ref, c_ref, mx_ref, mc_ref, nw_ref, w_ref, cos_ref, sin_ref), (p_ref,), _ = cg.split(refs, 8, 1, 0)
        cg.ride(refs, 8, 1, nb)
        is_ctx = pl.program_id(0) == 0
        xin = jnp.where(is_ctx, c_ref[...], x_ref[...])
        sh = jnp.where(is_ctx, mc_ref[0:1], mx_ref[0:1])
        sc = jnp.where(is_ctx, mc_ref[1:2], mx_ref[1:2])
        cf, ss = cos_ref[...], sin_ref[...]
        p = dnn(_mod(_rms(xin, nw_ref[...]), sh, sc), w_ref[...])
        p_ref[...] = _qk_heads(p, lambda t: _rot(t, cf, ss), lambda t: _rot(t * scale, cf, ss)).astype(bf16)

    return pl.pallas_call(
        body, name=name, grid=(nb,),
        in_specs=[pl.BlockSpec((R, D), lambda i: (jnp.maximum(i - 1, 0), 0)), _const_spec((R, D)), _const_spec((6, D)),
                  _const_spec((6, D)), _const_spec((1, D)), _const_spec((D, INC)), pl.BlockSpec((R, DH), lambda i: (i, 0)),
                  pl.BlockSpec((R, DH), lambda i: (i, 0))] + cg.in_specs(),
        out_specs=[pl.BlockSpec((R, INC), lambda i: (i, 0))] + cg.in_specs(),
        out_shape=[jax.ShapeDtypeStruct((L + R, INC), bf16)] + cg.out_shapes(),
        scratch_shapes=cg.sems(),
        compiler_params=_CP(dimension_semantics=_ARB),
    )(x, ctx, modx, modc, nw1, w_in_n, cosf, sins, *cg.arrays)


def _f1_bwd(x, ctx, modx, modc, nw1, w_in_t, cosf, sins, dx1, parts, name):
    L = x.shape[0]
    nb = L // R + 1
    scale = DH ** -0.5

    def body(x_ref, c_ref, mx_ref, mc_ref, nw_ref, w_ref, cos_ref, sin_ref, dx1_ref, du0, du1, dq0, dq1, dk0, dk1, dv0, dv1, dg0,
             gx_ref, dp_ref, h1_ref, dnw_ref, dmx_ref, dmc_ref):
        i = pl.program_id(0)
        is_ctx = i == 0

        @pl.when(is_ctx)
        def _():
            dnw_ref[...] = jnp.zeros_like(dnw_ref)
            dmx_ref[...] = jnp.zeros_like(dmx_ref)
            dmc_ref[...] = jnp.zeros_like(dmc_ref)

        cf, ss = cos_ref[...], sin_ref[...]
        dp = jnp.concatenate([du0[...].astype(f32) + du1[...], dq0[...] + dq1[...], dk0[...] + dk1[...], dv0[...] + dv1[...],
                              dg0[...]], axis=1)
        dp = _qk_heads(dp, lambda t: _rot_t(t, cf, ss), lambda t: _rot_t(t, cf, ss) * scale).astype(bf16)
        dp_ref[...] = dp
        xin = jnp.where(is_ctx, c_ref[...], x_ref[...])
        sh = jnp.where(is_ctx, mc_ref[0:1], mx_ref[0:1])
        sc = jnp.where(is_ctx, mc_ref[1:2], mx_ref[1:2])
        dh = dnn(dp, w_ref[...])
        h, vjp = jax.vjp(lambda a, b, c, d: _mod(_rms(a, b), c, d), xin, nw_ref[...], sh, sc)
        dxin, dnw, dsh, dsc = vjp(dh)
        h1_ref[...] = h.astype(bf16)
        gx_ref[...] = dx1_ref[...] + dxin
        dnw_ref[...] += dnw
        wx = jnp.where(is_ctx, 0.0, 1.0)
        dmx_ref[0:1] += dsh * wx
        dmx_ref[1:2] += dsc * wx
        dmc_ref[0:1] += dsh * (1.0 - wx)
        dmc_ref[1:2] += dsc * (1.0 - wx)

    lat = pl.BlockSpec((R, D), lambda i: (jnp.maximum(i - 1, 0), 0))
    ext = pl.BlockSpec((R, S5W), lambda i: (i, 0))
    return pl.pallas_call(
        body, name=name, grid=(nb,),
        in_specs=[lat, _const_spec((R, D)), _const_spec((6, D)), _const_spec((6, D)), _const_spec((1, D)), _const_spec((INC, D)),
                  pl.BlockSpec((R, DH), lambda i: (i, 0)), pl.BlockSpec((R, DH), lambda i: (i, 0)), lat] + [ext] * 9,
        out_specs=[lat, pl.BlockSpec((R, INC), lambda i: (i, 0)), pl.BlockSpec((R, D), lambda i: (i, 0)),
                   _acc_spec((1, D)), _acc_spec((6, D)), _acc_spec((6, D))],
        out_shape=[jax.ShapeDtypeStruct((L, D), f32), jax.ShapeDtypeStruct((L + R, INC), bf16),
                   jax.ShapeDtypeStruct((L + R, D), bf16), jax.ShapeDtypeStruct((1, D), f32),
                   jax.ShapeDtypeStruct((6, D), f32), jax.ShapeDtypeStruct((6, D), f32)],
        compiler_params=_CP(dimension_semantics=_ARB),
    )(x, ctx, modx, modc, nw1, w_in_t, cosf, sins, dx1, *parts)


def _ret_post(yr, g):
    outs = []
    for h in range(RH):
        yh = yr[:, h * DH:(h + 1) * DH]
        mu = jnp.mean(yh, axis=-1, keepdims=True)
        var = jnp.mean((yh - mu) ** 2, axis=-1, keepdims=True)
        outs.append((yh - mu) * lax.rsqrt(var + EPS))
    return jax.nn.silu(g) * jnp.concatenate(outs, axis=1)


def _mix_fn(ys, u, of, ob, g, x, dvec, bglu, gate1, pz, pm, wglu, wout):
    s = jax.nn.gelu(ys + dvec * u)
    z = dnn(s, wglu) + bglu + pz
    cat = jnp.concatenate([s * jax.nn.sigmoid(z), _ret_post(of + ob, g)], axis=1)
    mix = dnn(cat, wout) + pm
    return x + gate1 * mix, (s, cat)


def _mix_fwd(x, ys, of, ob, p_ext, dvec, bglu, modx, wglu, wout, name, cargo=None):
    L = x.shape[0]
    nb = L // R
    cg = _Cargo(cargo)

    def body(*refs):
        ins, (x1_ref,), _ = cg.split(refs, 11, 1, 0)
        x_ref, ys_ref, of_ref, ob_ref, u_ref, g_ref, d_ref, b_ref, mx_ref, wg_ref, wo_ref = ins
        cg.ride(refs, 11, 1, nb)
        x1_ref[...] = _mix_fn(ys_ref[...].astype(f32), u_ref[...].astype(f32), of_ref[...], ob_ref[...], g_ref[...].astype(f32),
                              x_ref[...], d_ref[...], b_ref[...], mx_ref[2:3], 0.0, 0.0, wg_ref[...], wo_ref[...])[0]

    ext = pl.BlockSpec((R, S5W), lambda i: (i + 1, 0))
    return pl.pallas_call(
        body, name=name, grid=(nb,),
        in_specs=[pl.BlockSpec((R, D), lambda i: (i, 0)), ext, ext, ext, ext, pl.BlockSpec((R, RW), lambda i: (i + 1, 4)),
                  _const_spec((1, S5W)), _const_spec((1, S5W)), _const_spec((6, D)), _const_spec((S5W, S5W)), _const_spec((D, D))]
        + cg.in_specs(),
        out_specs=[pl.BlockSpec((R, D), lambda i: (i, 0))] + cg.in_specs(),
        out_shape=[jax.ShapeDtypeStruct((L, D), f32)] + cg.out_shapes(),
        scratch_shapes=cg.sems(),
        compiler_params=_CP(dimension_semantics=_ARB),
    )(x, ys, of, ob, p_ext, p_ext, dvec, bglu, modx, wglu, wout, *cg.arrays)


def _mix_bwd(x, ys, of, ob, p_ext, dvec, bglu, modx, wglu, wout, dx1, name, cargo=None):
    L = x.shape[0]
    nb = L // R + 1
    cg = _Cargo(cargo)

    def body(*refs):
        ins, outs, _ = cg.split(refs, 12, 11, 0)
        x_ref, ys_ref, of_ref, ob_ref, u_ref, g_ref, d_ref, b_ref, mx_ref, wg_ref, wo_ref, dx1_ref = ins
        dy_ref, dud_ref, do_ref, dg_ref, cat_ref, dmix_ref, s_ref, dz_ref, dd_ref, db_ref, dg1_ref = outs
        cg.ride(refs, 12, 11, nb)
        i = pl.program_id(0)

        @pl.when(i == 0)
        def _():
            for r in outs:
                r[...] = jnp.zeros_like(r)

        @pl.when(i > 0)
        def _():
            fn = lambda ys_, u_, of_, g_, d_, b_, g1_, pz_, pm_: _mix_fn(
                ys_, u_, of_, ob_ref[...], g_, x_ref[...], d_, b_, g1_, pz_, pm_, wg_ref[...], wo_ref[...])
            _, vjp, (s, cat) = jax.vjp(fn, ys_ref[...].astype(f32), u_ref[...].astype(f32), of_ref[...], g_ref[...].astype(f32), d_ref[...],
                                       b_ref[...], mx_ref[2:3], jnp.zeros((R, S5W), f32), jnp.zeros((R, D), f32), has_aux=True)
            dy, dud, do, dg, dd, db, dg1, dz, dmix = vjp(dx1_ref[...])
            dy_ref[...], dud_ref[...], do_ref[...], dg_ref[...] = dy.astype(bf16), dud, do, dg
            cat_ref[...], dmix_ref[...] = cat.astype(bf16), dmix.astype(bf16)
            s_ref[...], dz_ref[...] = s.astype(bf16), dz.astype(bf16)
            dd_ref[...] += dd
            db_ref[...] += db
            dg1_ref[...] += dg1

    lat = pl.BlockSpec((R, D), lambda i: (jnp.maximum(i - 1, 0), 0))
    lat5 = pl.BlockSpec((R, S5W), lambda i: (jnp.maximum(i - 1, 0), 0))
    ext = pl.BlockSpec((R, S5W), lambda i: (i, 0))
    eshape = jax.ShapeDtypeStruct((L + R, S5W), f32)
    return pl.pallas_call(
        body, name=name, grid=(nb,),
        in_specs=[lat, ext, ext, ext, ext, pl.BlockSpec((R, RW), lambda i: (i, 4)),
                  _const_spec((1, S5W)), _const_spec((1, S5W)), _const_spec((6, D)), _const_spec((S5W, S5W)), _const_spec((D, D)), lat]
        + cg.in_specs(),
        out_specs=[ext, ext, ext, ext, lat, lat, lat5, lat5, _acc_spec((1, S5W)), _acc_spec((1, S5W)), _acc_spec((1, D))]
        + cg.in_specs(),
        out_shape=[jax.ShapeDtypeStruct((L + R, S5W), bf16), eshape, eshape, eshape, jax.ShapeDtypeStruct((L, D), bf16),
                   jax.ShapeDtypeStruct((L, D), bf16), jax.ShapeDtypeStruct((L, S5W), bf16), jax.ShapeDtypeStruct((L, S5W), bf16),
                   jax.ShapeDtypeStruct((1, S5W), f32), jax.ShapeDtypeStruct((1, S5W), f32), jax.ShapeDtypeStruct((1, D), f32)]
        + cg.out_shapes(),
        scratch_shapes=cg.sems(),
        compiler_params=_CP(dimension_semantics=_ARB),
    )(x, ys, of, ob, p_ext, p_ext, dvec, bglu, modx, wglu, wout, dx1, *cg.arrays)


def _ffn_tail(gc, a, x1, gate2, fnw, pf, wdown, wdown_t, tgt):
    f = jax.nn.gelu(gc) * a
    ffn = _dnn_const(f, wdown, wdown_t) + pf
    y = _rms(x1 + gate2 * ffn, fnw)
    err = y - tgt
    loss = 0.5 * jnp.sum(jnp.mean(err * err, axis=-1, keepdims=True), axis=0, keepdims=True)
    return loss, f


def _ffn_fwd(x1, tgt, nw2, modx, w_a, w_g, cw, cb, wdown, wdown_t, fnw, name):
    L = x1.shape[0]
    nb = L // RF
    per = RF // HALO

    def body(x_ref, xp_ref, xn_ref, t_ref, nw_ref, mx_ref, wa_ref, wg_ref, cw_ref, cb_ref, wd_ref, wdt_ref, fn_ref,
             dx2_ref, da_ref, dgc_ref, f_ref, dffn_ref, loss_ref, dfn_ref, dg2_ref, dcb_ref, dcw_ref):
        i = pl.program_id(0)

        @pl.when(i == 0)
        def _():
            for r in (loss_ref, dfn_ref, dg2_ref, dcb_ref, dcw_ref):
                r[...] = jnp.zeros_like(r)

        nw, sh, sc, gate2 = nw_ref[...], mx_ref[3:4], mx_ref[4:5], mx_ref[5:6]
        x1b = x_ref[...]
        h2 = _mod(_rms(x1b, nw), sh, sc)
        h2e = jnp.concatenate([_mod(_rms(xp_ref[...], nw), sh, sc), h2, _mod(_rms(xn_ref[...], nw), sh, sc)], axis=0)
        a = dnn(h2, wa_ref[...])
        ge = dnn(h2e, wg_ref[...])
        g = ge[HALO:HALO + RF]
        gp = ge[HALO - 1:HALO] * jnp.where(i > 0, 1.0, 0.0)
        gn = ge[HALO + RF:HALO + RF + 1] * jnp.where(i < nb - 1, 1.0, 0.0)
        row = lax.broadcasted_iota(jnp.int32, (RF, 1), 0)
        g_prev = jnp.where(row == 0, gp, pltpu.roll(g, 1, axis=0))
        g_next = jnp.where(row == RF - 1, gn, pltpu.roll(g, RF - 1, axis=0))
        gc = cb_ref[...] + g_prev * cw_ref[0:1] + g * cw_ref[1:2] + g_next * cw_ref[2:3]
        fn = lambda gc_, a_, x_, g2_, fw_, pf_: _ffn_tail(gc_, a_, x_, g2_, fw_, pf_, wd_ref[...], wdt_ref[...], t_ref[...])
        loss, vjp, f = jax.vjp(fn, gc, a, x1b, gate2, fn_ref[...], jnp.zeros((RF, D), f32), has_aux=True)
        dgc, da, dx2, dg2, dfw, dffn = vjp(jnp.ones((1, 1), f32))
        dx2_ref[...] = dx2
        da_ref[...], dgc_ref[...] = da.astype(bf16), dgc
        f_ref[...], dffn_ref[...] = f.astype(bf16), dffn.astype(bf16)
        loss_ref[...] += jnp.broadcast_to(loss, (1, 128))
        dfn_ref[...] += dfw
        dg2_ref[...] += dg2
        dcb_ref[...] += jnp.sum(dgc, axis=0, keepdims=True)
        dcw_ref[0:1] += jnp.sum(dgc * g_prev, axis=0, keepdims=True)
        dcw_ref[1:2] += jnp.sum(dgc * g, axis=0, keepdims=True)
        dcw_ref[2:3] += jnp.sum(dgc * g_next, axis=0, keepdims=True)

    blk = lambda w: pl.BlockSpec((RF, w), lambda i: (i, 0))
    return pl.pallas_call(
        body, name=name, grid=(nb,),
        in_specs=[blk(D), pl.BlockSpec((HALO, D), lambda i: (jnp.maximum(i * per - 1, 0), 0)),
                  pl.BlockSpec((HALO, D), lambda i: (jnp.minimum((i + 1) * per, L // HALO - 1), 0)), blk(D),
                  _const_spec((1, D)), _const_spec((6, D)), _const_spec((D, DFF)), _const_spec((D, DFF)), _const_spec((3, DFF)),
                  _const_spec((1, DFF)), _const_spec((DFF, D)), _const_spec((D, DFF)), _const_spec((1, D))],
        out_specs=[blk(D), blk(DFF), blk(DFF), blk(DFF), blk(D), _acc_spec((1, 128)), _acc_spec((1, D)), _acc_spec((1, D)),
                   _acc_spec((1, DFF)), _acc_spec((3, DFF))],
        out_shape=[jax.ShapeDtypeStruct((L, D), f32), jax.ShapeDtypeStruct((L, DFF), bf16), jax.ShapeDtypeStruct((L, DFF), f32),
                   jax.ShapeDtypeStruct((L, DFF), bf16), jax.ShapeDtypeStruct((L, D), bf16), jax.ShapeDtypeStruct((1, 128), f32),
                   jax.ShapeDtypeStruct((1, D), f32), jax.ShapeDtypeStruct((1, D), f32), jax.ShapeDtypeStruct((1, DFF), f32),
                   jax.ShapeDtypeStruct((3, DFF), f32)],
        compiler_params=_CP(dimension_semantics=_ARB),
    )(x1, x1, x1, tgt, nw2, modx, w_a, w_g, cw, cb, wdown, wdown_t, fnw)


def _ffn_bwd(x1, dx2, da, dgc, nw2, modx, wup_t, cw, name):
    L = x1.shape[0]
    nb = L // RF
    per = RF // HALO

    def body(x_ref, dx2_ref, da_ref, dgc_ref, dgp_ref, dgn_ref, nw_ref, mx_ref, wu_ref, cw_ref,
             dx1_ref, dag_ref, h2_ref, dnw_ref, dmx_ref):
        i = pl.program_id(0)

        @pl.when(i == 0)
        def _():
            dnw_ref[...] = jnp.zeros_like(dnw_ref)
            dmx_ref[...] = jnp.zeros_like(dmx_ref)

        dgc_b = dgc_ref[...]
        before = dgp_ref[HALO - 1:HALO] * jnp.where(i > 0, 1.0, 0.0)
        after = dgn_ref[0:1] * jnp.where(i < nb - 1, 1.0, 0.0)
        row = lax.broadcasted_iota(jnp.int32, (RF, 1), 0)
        d_prev = jnp.where(row == 0, before, pltpu.roll(dgc_b, 1, axis=0))
        d_next = jnp.where(row == RF - 1, after, pltpu.roll(dgc_b, RF - 1, axis=0))
        dg = cw_ref[0:1] * d_next + cw_ref[1:2] * dgc_b + cw_ref[2:3] * d_prev
        dag = jnp.concatenate([da_ref[...], dg.astype(bf16)], axis=1)
        dag_ref[...] = dag
        dh2 = dnn(dag, wu_ref[...])
        h2, vjp = jax.vjp(lambda a, b, c, d: _mod(_rms(a, b), c, d), x_ref[...], nw_ref[...], mx_ref[3:4], mx_ref[4:5])
        dxa, dnw, dsh, dsc = vjp(dh2)
        h2_ref[...] = h2.astype(bf16)
        dx1_ref[...] = dx2_ref[...] + dxa
        dnw_ref[...] += dnw
        dmx_ref[3:4] += dsh
        dmx_ref[4:5] += dsc

    blk = lambda w: pl.BlockSpec((RF, w), lambda i: (i, 0))
    return pl.pallas_call(
        body, name=name, grid=(nb,),
        in_specs=[blk(D), blk(D), blk(DFF), blk(DFF), pl.BlockSpec((HALO, DFF), lambda i: (jnp.maximum(i * per - 1, 0), 0)),
                  pl.BlockSpec((HALO, DFF), lambda i: (jnp.minimum((i + 1) * per, L // HALO - 1), 0)),
                  _const_spec((1, D)), _const_spec((6, D)), _const_spec((2 * DFF, D)), _const_spec((3, DFF))],
        out_specs=[blk(D), blk(2 * DFF), blk(D), _acc_spec((1, D)), _acc_spec((6, D))],
        out_shape=[jax.ShapeDtypeStruct((L, D), f32), jax.ShapeDtypeStruct((L, 2 * DFF), bf16), jax.ShapeDtypeStruct((L, D), bf16),
                   jax.ShapeDtypeStruct((1, D), f32), jax.ShapeDtypeStruct((6, D), f32)],
        compiler_params=_CP(dimension_semantics=_ARB),
    )(x1, dx2, da, dgc, dgc, dgc, nw2, modx, wup_t, cw)


def _matmul_tn(a, b, name):
    k, m = a.shape
    n = b.shape[1]
    divs = lambda d: [c for c in range(d, 0, -128) if d % c == 0]
    _, tm, tn = min((m * (n // cn) + n * (m // cm), cm, cn) for cm in divs(m) for cn in divs(n) if cm * cn * 4 <= ACC_TILE_BYTES)
    tk = next(c for c in (512, 768, 256, 128) if k % c == 0)
    nk = k // tk

    def body(a_ref, b_ref, o_ref, acc):
        q = pl.program_id(2)

        @pl.when(q == 0)
        def _():
            acc[...] = jnp.zeros_like(acc)

        acc[...] += dtn(a_ref[...], b_ref[...])

        @pl.when(q == nk - 1)
        def _():
            o_ref[...] = acc[...].astype(bf16)

    return pl.pallas_call(
        body, name=name, grid=(m // tm, n // tn, nk),
        in_specs=[pl.BlockSpec((tk, tm), lambda i, j, q: (q, i)), pl.BlockSpec((tk, tn), lambda i, j, q: (q, j))],
        out_specs=pl.BlockSpec((tm, tn), lambda i, j, q: (i, j)),
        out_shape=jax.ShapeDtypeStruct((m, n), bf16),
        scratch_shapes=[pltpu.VMEM((tm, tn), f32)],
        compiler_params=_CP(dimension_semantics=("parallel", "parallel", "arbitrary")),
    )(a, b)


def _adamw_refs(w_ref, g_ref, m_ref, v_ref, d_ref, nm_ref, nv_ref):
    c1, c2 = 1.0 - B1 ** STEP, 1.0 - B2 ** STEP
    gg = g_ref[...]
    nm = B1 * m_ref[...] + (1.0 - B1) * gg
    nv = B2 * v_ref[...] + (1.0 - B2) * jnp.square(gg)
    d_ref[...] = -LR * ((nm / c1) / (jnp.sqrt(nv / c2) + AEPS) + WD * w_ref[...])
    nm_ref[...], nv_ref[...] = nm, nv


def _adamw(w, g, m, v, name):
    def body(*refs):
        _adamw_refs(*refs)

    return pl.pallas_call(body, name=name, out_shape=[jax.ShapeDtypeStruct(w.shape, f32)] * 3, compiler_params=_CP())(w, g, m, v)


def _adamw_many(ws, gs, ms, vs, name):
    n = len(ws)

    def body(*refs):
        for k in range(n):
            _adamw_refs(*[refs[j * n + k] for j in range(7)])

    outs = pl.pallas_call(body, name=name, out_shape=[jax.ShapeDtypeStruct(w.shape, f32) for w in ws] * 3,
                          compiler_params=_CP())(*ws, *gs, *ms, *vs)
    return outs[:n], outs[n:2 * n], outs[2 * n:]


SMALL = ["conv_w", "c_ctx", "norm1_w", "s5_lambda_re_f", "s5_lambda_im_f", "s5_log_step_f", "s5_lambda_re_b", "s5_lambda_im_b",
         "s5_log_step_b", "s5_b_re", "s5_b_im", "s5_c_re", "s5_c_im", "s5_d", "s5_b_glu", "ret_log_decay_f", "ret_log_decay_b",
         "norm2_w", "conv_b", "final_norm_w"]
WEIGHTS = ["c_ctx", "w_mod", "b_mod", "norm1_w", "w_in", "s5_lambda_re_f", "s5_lambda_im_f", "s5_log_step_f", "s5_lambda_re_b",
           "s5_lambda_im_b", "s5_log_step_b", "s5_b_re", "s5_b_im", "s5_c_re", "s5_c_im", "s5_d", "s5_w_glu", "s5_b_glu",
           "ret_log_decay_f", "ret_log_decay_b", "w_out", "norm2_w", "w_up", "conv_w", "conv_b", "w_down", "final_norm_w"]


def _pack_small(vals):
    flat, offs, o = [], [], 0
    for a in vals:
        n = a.size
        npad = -n % 128
        flat.append(jnp.pad(a.reshape(-1), (0, npad)))
        offs.append((o, n))
        o += n + npad
    tail = -o % 1024
    if tail:
        flat.append(jnp.zeros((tail,), f32))
    return jnp.concatenate(flat).reshape(-1, 128), offs


def _unpack_small(packed, offs, shapes):
    flat = packed.reshape(-1)
    return [flat[o:o + n].reshape(s) for (o, n), s in zip(offs, shapes)]


def _rope_tables(L, nctx_rows):
    t = np.arange(L)
    inv = (ROPE_THETA ** (-np.arange(DH // 4, dtype=np.float64) / (DH // 4))).astype(np.float32)
    ang = np.concatenate([(t // GRID_W).astype(np.float32)[:, None] * inv, (t % GRID_W).astype(np.float32)[:, None] * inv], axis=-1)
    cos = np.repeat(np.cos(ang).astype(np.float32), 2, axis=1)
    sin = np.repeat(np.sin(ang).astype(np.float32), 2, axis=1) * np.tile(np.array([-1.0, 1.0], np.float32), DH // 2)
    cosf = np.concatenate([np.ones((nctx_rows, DH), np.float32), cos], axis=0)
    sins = np.concatenate([np.zeros((nctx_rows, DH), np.float32), sin], axis=0)
    return jnp.asarray(cosf), jnp.asarray(sins)


def kernel(x, c, ctx, c_ctx, w_mod, b_mod, norm1_w, w_in, s5_lambda_re_f, s5_lambda_im_f, s5_log_step_f, s5_lambda_re_b, s5_lambda_im_b, s5_log_step_b, s5_b_re, s5_b_im, s5_c_re, s5_c_im, s5_d, s5_w_glu, s5_b_glu, ret_log_decay_f, ret_log_decay_b, w_out, norm2_w, w_up, conv_w, conv_b, w_down, final_norm_w, loss_target, m_c_ctx, m_w_mod, m_b_mod, m_norm1_w, m_w_in, m_s5_lambda_re_f, m_s5_lambda_im_f, m_s5_log_step_f, m_s5_lambda_re_b, m_s5_lambda_im_b, m_s5_log_step_b, m_s5_b_re, m_s5_b_im, m_s5_c_re, m_s5_c_im, m_s5_d, m_s5_w_glu, m_s5_b_glu, m_ret_log_decay_f, m_ret_log_decay_b, m_w_out, m_norm2_w, m_w_up, m_conv_w, m_conv_b, m_w_down, m_final_norm_w, v_c_ctx, v_w_mod, v_b_mod, v_norm1_w, v_w_in, v_s5_lambda_re_f, v_s5_lambda_im_f, v_s5_log_step_f, v_s5_lambda_re_b, v_s5_lambda_im_b, v_s5_log_step_b, v_s5_b_re, v_s5_b_im, v_s5_c_re, v_s5_c_im, v_s5_d, v_s5_w_glu, v_s5_b_glu, v_ret_log_decay_f, v_ret_log_decay_b, v_w_out, v_norm2_w, v_w_up, v_conv_w, v_conv_b, v_w_down, v_final_norm_w):
    args = dict(locals())
    W = {n: args[n] for n in WEIGHTS}
    M = {n: args["m_" + n] for n in WEIGHTS}
    V = {n: args["v_" + n] for n in WEIGHTS}
    me = _me()
    x2, ctx2, tgt = x[0], ctx[0], loss_target[0]
    L, Lc = x2.shape[0], ctx2.shape[0]
    assert Lc == R and L % R == 0 and L % GRID_W == 0
    nctx = Lc // T

    c_all = _all_gather_small(jnp.pad(c, ((0, 7), (0, 0))), "gather_c")[:, 0, :]
    c9 = jnp.concatenate([c_all, c_ctx[None], jnp.zeros((7, D), f32)], axis=0)
    w_mod_l = w_mod[0]
    ncol = w_mod_l.shape[1]
    m_part = _ada_fwd(c9, w_mod_l, "ada_fwd")
    m_all = _all_gather_small(m_part, "gather_mod").transpose(1, 0, 2).reshape(16, 6, D)
    modx, modc = _mod_select(m_all, b_mod.reshape(6, D), "mod_select")

    w_in_tl, w_up_tl = w_in[0].T.astype(bf16), w_up[0].T.astype(bf16)
    w_out_l, w_down_l, w_glu_l = w_out[0].astype(bf16), w_down[0].astype(bf16), s5_w_glu[0].astype(bf16)
    half_up = w_up_tl.shape[0] // 2
    (w_in_g,) = _exchange([w_in_tl], False, "gather_w_in")
    w_in_t = w_in_g.reshape(INC, D)
    per_cv = conv_w.shape[2]
    conv_pad = jnp.pad(conv_w[0], ((0, 5), (0, 128 * 3 - per_cv)))
    conv_f = _all_gather_small(conv_pad, "gather_conv")[:, :3, :per_cv].transpose(1, 0, 2).reshape(3, DFF)

    pair = lambda a, b: jnp.concatenate([a, b], axis=-1)
    bre_g, bim_g = s5_b_re[0].transpose(0, 2, 1), s5_b_im[0].transpose(0, 2, 1)
    cre_g, cim_g = s5_c_re[0], s5_c_im[0]
    shared = (pair(bre_g, bim_g), pair(bim_g, bre_g), pair(cre_g, cim_g), pair(cim_g, cre_g))
    s5p = {}
    for tag, lre, lim, ls in (("f", s5_lambda_re_f, s5_lambda_im_f, s5_log_step_f), ("b", s5_lambda_re_b, s5_lambda_im_b, s5_log_step_b)):
        s5p[tag] = (pair(lre[0], lre[0])[:, None, :], pair(lim[0], lim[0])[:, None, :], ls[0].reshape(S5G, 1, 1)) + shared
    m_f, mb_f, mc_f, a1_f, a2_f = _s5_build(s5p["f"], False, "s5_build_f")
    m_b, mb_b, mc_b, a1_b, a2_b = _s5_build(s5p["b"], True, "s5_build_b")
    a1_f, a2_f, a1_b, a2_b = (a.reshape(S5G, SB) for a in (a1_f, a2_f, a1_b, a2_b))

    nw1, nw2, fnw = norm1_w, norm2_w, final_norm_w[None]
    cosf, sins = _rope_tables(L, Lc)
    p_ext, w_out_g, w_glu_g = _f1_fwd(x2, ctx2, modx, modc, nw1, w_in_t.T, cosf, sins, "f1_fwd", cargo=([w_out_l, w_glu_l], False))
    nctx5 = Lc // TC
    u_g = _to_groups(p_ext[:, :S5W])
    s_f, s_b = _s5_inc(u_g, mb_f, mb_b, "s5_inc")
    hp_f, hp_b = _s5_carry(s_f, s_b, (a1_f, a2_f), (a1_b, a2_b), nctx5, "s5_carry")
    ys = _from_groups(_s5_out(u_g, m_f, m_b, hp_f, hp_b, mc_f, mc_b, "s5_out"))
    ld8 = lambda ld: jnp.pad(jnp.broadcast_to(ld[0][:, None], (RH, 128)), ((0, 8 - RH), (0, 0)))
    ldf8, ldb8 = ld8(ret_log_decay_f), ld8(ret_log_decay_b)
    of, rp_f, w_up_g1 = _ret_fwd(p_ext, ldf8, False, nctx, "ret_fwd_f", cargo=([w_up_tl[:half_up]], False))
    ob, rp_b, w_up_g2 = _ret_fwd(p_ext, ldb8, True, nctx, "ret_fwd_b", cargo=([w_up_tl[half_up:]], False))
    w_out_f, w_glu_f = w_out_g.reshape(D, D), w_glu_g.reshape(S5W, S5W)
    x1, w_down_g = _mix_fwd(x2, ys, of, ob, p_ext, s5_d, s5_b_glu, modx, w_glu_f, w_out_f, "mix_fwd", cargo=([w_down_l], False))
    w_down_f = w_down_g.reshape(DFF, D)
    w_up_t = jnp.concatenate([w_up_g1, w_up_g2], axis=1).reshape(2 * DFF, D)

    (dx2, da, dgc, f_act, dffn, loss_acc, g_fnw, g_gate2, g_cb, g_cw) = _ffn_fwd(
        x1, tgt, nw2, modx, w_up_t[:DFF].T, w_up_t[DFF:].T, conv_f, conv_b, w_down_f, w_down_f.T, fnw, "ffn_fwd")
    dx1, dag, h2, g_nw2, dmx2 = _ffn_bwd(x1, dx2, da, dgc, nw2, modx, w_up_t, conv_f, "ffn_bwd")
    gw_down = _matmul_tn(f_act, dffn, "dw_down").reshape(NDEV, -1, D)
    gw_up_t = _matmul_tn(dag, h2, "dw_up").reshape(NDEV, -1, D)
    (dy_e, dud_e, do_e, dg_e, cat, dmix, s_act, dz, g_d, g_bglu, g_gate1, l_down) = _mix_bwd(
        x2, ys, of, ob, p_ext, s5_d, s5_b_glu, modx, w_glu_f, w_out_f, dx1, "mix_bwd", cargo=([gw_down], True))
    gw_out = _matmul_tn(cat, dmix, "dw_out").reshape(NDEV, -1, D)
    gw_glu = _matmul_tn(s_act, dz, "dw_glu").reshape(NDEV, -1, S5W)
    dq_f, dk_f, dv_f, gld_f, l_up1 = _ret_bwd(p_ext, ldf8, rp_f, do_e, False, nctx, "ret_bwd_f",
                                              cargo=([gw_up_t[:, :half_up]], True))
    dq_b, dk_b, dv_b, gld_b, l_out, l_glu, l_up2 = _ret_bwd(p_ext, ldb8, rp_b, do_e, True, nctx, "ret_bwd_b",
                                                            cargo=([gw_out, gw_glu, gw_up_t[:, half_up:]], True))

    du1, g_m, dhp_f, dhp_b, dmc_f, dmc_b = _s5_out_bwd(_to_groups(dy_e), u_g, m_f, m_b, hp_f, hp_b, mc_f, mc_b, "s5_out_bwd")
    ds_f, da1_f, da2_f = _s5_carry_bwd(dhp_f, hp_f, a1_f, a2_f, False, nctx5, "s5_carry_bwd_f")
    ds_b, da1_b, da2_b = _s5_carry_bwd(dhp_b, hp_b, a1_b, a2_b, True, nctx5, "s5_carry_bwd_b")
    du_g, dmb_f, dmb_b = _s5_inc_bwd(du1, u_g, ds_f, ds_b, mb_f, mb_b, "s5_inc_bwd")
    zero_p = jnp.zeros((S5G, S5P, SB), f32)
    gf = _s5_build_bwd(s5p["f"], (g_m, dmb_f, dmc_f, da1_f[:, None, :], da2_f[:, None, :]), (zero_p, zero_p), False, "s5_build_bwd_f")
    gb = _s5_build_bwd(s5p["b"], (g_m, dmb_b, dmc_b, da1_b[:, None, :], da2_b[:, None, :]), (gf[3], gf[4]), True, "s5_build_bwd_b")
    g_bre, g_bim = gb[3][:, :, :S5N].transpose(0, 2, 1), gb[3][:, :, S5N:].transpose(0, 2, 1)
    g_cre, g_cim = gb[4][:, :, :S5N], gb[4][:, :, S5N:]

    grad_x, dp_ext, h1, g_nw1, dmx1, dmc1 = _f1_bwd(
        x2, ctx2, modx, modc, nw1, w_in_t, cosf, sins, dx1, (_from_groups(du_g), dud_e, dq_f, dq_b, dk_f, dk_b, dv_f, dv_b, dg_e), "f1_bwd")
    gw_in_t = _matmul_tn(dp_ext, h1, "dw_in").reshape(NDEV, -1, D)
    (l_in,) = _exchange([gw_in_t], True, "scatter_dw_in")

    dmx = dmx1 + dmx2
    dmx = dmx.at[2].set(g_gate1[0]).at[5].set(g_gate2[0])
    dm_me = jnp.stack([dmx.reshape(-1), dmc1.reshape(-1)], axis=0)
    dm_all = _all_gather_small(jnp.pad(dm_me, ((0, 6), (0, 0))), "gather_dmod")
    dmx_all, dmc_all = dm_all[:, 0, :], dm_all[:, 1, :]
    my_cols = lambda a: lax.dynamic_slice(a, (0, me * ncol), (NDEV, ncol))
    gw_mod, g_bmod, dc9 = _ada_bwd(c9, dmx_all, dmc_all, my_cols(dmx_all), my_cols(dmc_all), w_mod_l, "ada_bwd")

    small = {
        "conv_w": g_cw, "c_ctx": dc9[8], "norm1_w": g_nw1, "s5_lambda_re_f": gf[0][:, 0, :S5N], "s5_lambda_im_f": gf[1][:, 0, :S5N],
        "s5_log_step_f": gf[2], "s5_lambda_re_b": gb[0][:, 0, :S5N], "s5_lambda_im_b": gb[1][:, 0, :S5N], "s5_log_step_b": gb[2],
        "s5_b_re": g_bre, "s5_b_im": g_bim, "s5_c_re": g_cre, "s5_c_im": g_cim, "s5_d": g_d, "s5_b_glu": g_bglu,
        "ret_log_decay_f": gld_f[:RH, 0], "ret_log_decay_b": gld_b[:RH, 0], "norm2_w": g_nw2, "conv_b": g_cb, "final_norm_w": g_fnw,
    }
    packed, soffs = _pack_small([small[n].astype(f32) for n in SMALL])
    red = _all_reduce_small(packed, "reduce_small")
    sshapes = [(3, DFF) if n == "conv_w" else W[n].shape for n in SMALL]
    G = dict(zip(SMALL, _unpack_small(red, soffs, sshapes)))
    G["conv_w"] = lax.dynamic_slice(G["conv_w"], (0, me * per_cv), (3, per_cv))[None]
    G["b_mod"] = g_bmod.reshape(b_mod.shape)
    G["w_mod"] = gw_mod[None]
    G["w_in"] = _sum8(l_in, "sum_dw_in").T[None]
    G["w_up"] = jnp.concatenate([_sum8(l_up1, "sum_dw_up1"), _sum8(l_up2, "sum_dw_up2")], axis=0).T[None]
    G["w_out"] = _sum8(l_out, "sum_dw_out")[None]
    G["w_down"] = _sum8(l_down, "sum_dw_down")[None]
    G["s5_w_glu"] = _sum8(l_glu, "sum_dw_glu")[None]

    delta, new_m, new_v = {}, {}, {}
    sm_names = SMALL[1:] + ["b_mod"]
    rows = lambda a: a.reshape(-1, a.shape[-1])
    outs = _adamw_many(*[[rows(d[n]) for n in sm_names] for d in (W, G, M, V)], "adamw_small")
    for dst, src in zip((delta, new_m, new_v), outs):
        dst.update({n: a.reshape(W[n].shape) for n, a in zip(sm_names, src)})
    for n in ["w_mod", "w_in", "w_out", "w_up", "w_down", "s5_w_glu", "conv_w"]:
        d, nm, nv = _adamw(W[n][0], G[n][0], M[n][0], V[n][0], "adamw_" + n)
        delta[n], new_m[n], new_v[n] = d[None], nm[None], nv[None]

    loss = lax.psum(loss_acc[0, 0], ("x", "y", "c"))
    return (loss, grad_x[None], *[G[n] for n in WEIGHTS], *[delta[n] for n in WEIGHTS], *[new_m[n] for n in WEIGHTS],
            *[new_v[n] for n in WEIGHTS])
```

```python
import functools

import numpy as np
import jax
import jax.numpy as jnp
from jax import lax
from jax.experimental import pallas as pl
from jax.experimental.pallas import tpu as pltpu

f32, bf16 = jnp.float32, jnp.bfloat16

D = 1024
S5W, S5G, S5P, S5N = 512, 32, 16, 64
TC = 16
TCP = TC * S5P
SB = 2 * S5N
GBK = 8
CARRY_UNROLL = 8
RH, DH = 4, 128
RW = RH * DH
INC = S5W + 4 * RW
DFF = 2816
T = 128
R = 256
RF = 128
HALO = 8
EPS = 1e-6
ROPE_THETA = 10000.0
GRID_W = 64
NDEV = 8
LR, B1, B2, AEPS, WD, STEP = 0.001, 0.9, 0.999, 1e-08, 0.01, 10
VMEM_LIMIT = 60 * 1024 * 1024
ACC_TILE_BYTES = 6 * 1024 * 1024
MESH = pl.DeviceIdType.MESH

_CP = functools.partial(pltpu.CompilerParams, vmem_limit_bytes=VMEM_LIMIT)
_ARB = ("arbitrary",)
_ANY = pl.BlockSpec(memory_space=pl.ANY)


def _dg(a, b, dims):
    return lax.dot_general(a.astype(bf16), b.astype(bf16), (dims, ((), ())), preferred_element_type=f32)


@jax.custom_vjp
def dnn(a, b):
    return _dg(a, b, ((1,), (0,)))


@jax.custom_vjp
def dnt(a, b):
    return _dg(a, b, ((1,), (1,)))


@jax.custom_vjp
def dtn(a, b):
    return _dg(a, b, ((0,), (0,)))


dnn.defvjp(lambda a, b: (dnn(a, b), (a, b)), lambda r, g: (dnt(g, r[1]).astype(r[0].dtype), dtn(r[0], g).astype(r[1].dtype)))
dnt.defvjp(lambda a, b: (dnt(a, b), (a, b)), lambda r, g: (dnn(g, r[1]).astype(r[0].dtype), dtn(g, r[0]).astype(r[1].dtype)))
dtn.defvjp(lambda a, b: (dtn(a, b), (a, b)), lambda r, g: (dnt(r[1], g).astype(r[0].dtype), dnn(r[0], g).astype(r[1].dtype)))


@jax.custom_vjp
def _dnn_const(a, w, wt):
    return dnn(a, w)


_dnn_const.defvjp(lambda a, w, wt: (dnn(a, w), wt), lambda wt, g: (dnn(g, wt), None, None))


def _rms(t, w):
    return t * lax.rsqrt(jnp.mean(t * t, axis=-1, keepdims=True) + EPS) * w


def _mod(h, shift, scale):
    return h * (1.0 + scale) + shift


def _const_spec(shape):
    n = len(shape)
    return pl.BlockSpec(shape, lambda i, _n=n: (0,) * _n, pipeline_mode=pl.Buffered(1))


def _acc_spec(shape):
    n = len(shape)
    return pl.BlockSpec(shape, lambda i, _n=n: (0,) * _n)


def _me():
    return 4 * lax.axis_index("x") + 2 * lax.axis_index("y") + lax.axis_index("c")


def _peer(r):
    x, y, c = lax.axis_index("x"), lax.axis_index("y"), lax.axis_index("c")
    px = 1 - x if (r >> 2) & 1 else x
    py = 1 - y if (r >> 1) & 1 else y
    pc = 1 - c if r & 1 else c
    return (px, py, pc), 4 * px + 2 * py + pc


def _all_gather_small(v, name):
    r, c = v.shape

    def body(v_ref, out_ref, send_sems, recv_sems):
        me = _me()
        out_ref[me] = v_ref[...]
        sends = []
        for k in range(1, NDEV):
            peer, _ = _peer(k)
            cp = pltpu.make_async_remote_copy(src_ref=v_ref, dst_ref=out_ref.at[me], send_sem=send_sems.at[k - 1],
                                              recv_sem=recv_sems.at[k - 1], device_id=peer, device_id_type=MESH)
            cp.start()
            sends.append(cp)
        for k in range(1, NDEV):
            peer, pidx = _peer(k)
            pltpu.make_async_remote_copy(src_ref=v_ref, dst_ref=out_ref.at[pidx], send_sem=send_sems.at[k - 1],
                                         recv_sem=recv_sems.at[k - 1], device_id=peer, device_id_type=MESH).wait_recv()
        for cp in sends:
            cp.wait_send()

    return pl.pallas_call(
        body, name=name, out_shape=jax.ShapeDtypeStruct((NDEV, r, c), v.dtype),
        in_specs=[pl.BlockSpec(memory_space=pltpu.VMEM)], out_specs=pl.BlockSpec(memory_space=pltpu.VMEM),
        scratch_shapes=[pltpu.SemaphoreType.DMA((NDEV - 1,)), pltpu.SemaphoreType.DMA((NDEV - 1,))],
        compiler_params=_CP(),
    )(v)


def _all_reduce_small(v, name):
    r, c = v.shape

    def body(v_ref, out_ref, land, send_sems, recv_sems):
        me = _me()
        land[me] = v_ref[...]
        sends = []
        for k in range(1, NDEV):
            peer, _ = _peer(k)
            cp = pltpu.make_async_remote_copy(src_ref=v_ref, dst_ref=land.at[me], send_sem=send_sems.at[k - 1],
                                              recv_sem=recv_sems.at[k - 1], device_id=peer, device_id_type=MESH)
            cp.start()
            sends.append(cp)
        for k in range(1, NDEV):
            peer, pidx = _peer(k)
            pltpu.make_async_remote_copy(src_ref=v_ref, dst_ref=land.at[pidx], send_sem=send_sems.at[k - 1],
                                         recv_sem=recv_sems.at[k - 1], device_id=peer, device_id_type=MESH).wait_recv()
        for cp in sends:
            cp.wait_send()
        acc = land[0]
        for j in range(1, NDEV):
            acc = acc + land[j]
        out_ref[...] = acc

    return pl.pallas_call(
        body, name=name, out_shape=jax.ShapeDtypeStruct((r, c), v.dtype),
        in_specs=[pl.BlockSpec(memory_space=pltpu.VMEM)], out_specs=pl.BlockSpec(memory_space=pltpu.VMEM),
        scratch_shapes=[pltpu.VMEM((NDEV, r, c), v.dtype), pltpu.SemaphoreType.DMA((NDEV - 1,)),
                        pltpu.SemaphoreType.DMA((NDEV - 1,))],
        compiler_params=_CP(),
    )(v)


class _Exchange:
    def __init__(self, srcs, dsts, send_sems, recv_sems, local_sems, scatter):
        me = _me()
        n = len(srcs)
        self.sends, self.recvs, self.locals = [], [], []
        for a, (s, d) in enumerate(zip(srcs, dsts)):
            self.locals.append(pltpu.make_async_copy(s.at[me] if scatter else s, d.at[me], local_sems.at[a]))
        for k in range(1, NDEV):
            peer, pidx = _peer(k)
            for a, (s, d) in enumerate(zip(srcs, dsts)):
                src = s.at[pidx] if scatter else s
                sem = (k - 1) * n + a
                for dst, out in ((d.at[me], self.sends), (d.at[pidx], self.recvs)):
                    out.append(pltpu.make_async_remote_copy(src_ref=src, dst_ref=dst, send_sem=send_sems.at[sem],
                                                            recv_sem=recv_sems.at[sem], device_id=peer, device_id_type=MESH))

    def start(self):
        for cp in self.locals + self.sends:
            cp.start()

    def wait(self):
        for cp in self.recvs:
            cp.wait_recv()
        for cp in self.sends:
            cp.wait_send()
        for cp in self.locals:
            cp.wait()


def _exchange_shapes(arrays, scatter):
    return [jax.ShapeDtypeStruct(a.shape if scatter else (NDEV,) + a.shape, a.dtype) for a in arrays]


def _exchange_sems(n):
    return [pltpu.SemaphoreType.DMA(((NDEV - 1) * n,)), pltpu.SemaphoreType.DMA(((NDEV - 1) * n,)), pltpu.SemaphoreType.DMA((n,))]


def _exchange(arrays, scatter, name):
    n = len(arrays)

    def body(*refs):
        ex = _Exchange(refs[:n], refs[n:2 * n], *refs[2 * n:], scatter)
        ex.start()
        ex.wait()

    return pl.pallas_call(body, name=name, out_shape=_exchange_shapes(arrays, scatter), in_specs=[_ANY] * n,
                          out_specs=[_ANY] * n, scratch_shapes=_exchange_sems(n), compiler_params=_CP())(*arrays)


class _Cargo:
    def __init__(self, cargo):
        self.arrays, self.scatter = cargo if cargo else ([], False)
        self.n = len(self.arrays)

    def in_specs(self):
        return [_ANY] * self.n

    def out_shapes(self):
        return _exchange_shapes(self.arrays, self.scatter)

    def sems(self):
        return _exchange_sems(self.n) if self.n else []

    def split(self, refs, n_in, n_out, n_scratch):
        n = self.n
        return refs[:n_in], refs[n_in + n:n_in + n + n_out], refs[n_in + 2 * n + n_out:n_in + 2 * n + n_out + n_scratch]

    def ride(self, refs, n_in, n_out, nsteps):
        if not self.n:
            return
        n = self.n
        ex = _Exchange(refs[n_in:n_in + n], refs[n_in + n + n_out:n_in + 2 * n + n_out], *refs[-3:], self.scatter)

        @pl.when(pl.program_id(0) == 0)
        def _():
            ex.start()

        @pl.when(pl.program_id(0) == nsteps - 1)
        def _():
            ex.wait()


def _sum8(land, name):
    _, r, c = land.shape
    rb = next((b for b in (256, 64, 32) if r % b == 0), r)

    def body(l_ref, o_ref):
        acc = l_ref[0].astype(f32)
        for j in range(1, NDEV):
            acc = acc + l_ref[j].astype(f32)
        o_ref[...] = acc

    return pl.pallas_call(
        body, name=name, grid=(r // rb,), out_shape=jax.ShapeDtypeStruct((r, c), f32),
        in_specs=[pl.BlockSpec((NDEV, rb, c), lambda i: (0, i, 0))], out_specs=pl.BlockSpec((rb, c), lambda i: (i, 0)),
        compiler_params=_CP(dimension_semantics=("parallel",)),
    )(land)


def _ada_fwd(c9, w_mod_l, name):
    def body(c_ref, w_ref, o_ref):
        o_ref[...] = dnn(jax.nn.silu(c_ref[...]), w_ref[...])

    return pl.pallas_call(body, name=name, out_shape=jax.ShapeDtypeStruct((16, w_mod_l.shape[1]), f32),
                          compiler_params=_CP())(c9, w_mod_l)


def _mod_select(m_all, b_mod6, name):
    def body(m_ref, b_ref, mx_ref, mc_ref):
        me = _me()
        mx_ref[...] = m_ref[me] + b_ref[...]
        mc_ref[...] = m_ref[8] + b_ref[...]

    return pl.pallas_call(body, name=name, out_shape=[jax.ShapeDtypeStruct((6, D), f32)] * 2, compiler_params=_CP())(m_all, b_mod6)


def _ada_bwd(c9, dmx_all, dmc_all, dmx_l, dmc_l, w_mod_l, name):
    ncol = w_mod_l.shape[1]

    def rowsum(r):
        acc = r[0:1]
        for j in range(1, NDEV):
            acc = acc + r[j:j + 1]
        return acc

    def body(c_ref, xa_ref, ca_ref, xl_ref, cl_ref, w_ref, gw_ref, gb_ref, dc_ref):
        s9, vjp = jax.vjp(jax.nn.silu, c_ref[...])
        dm9 = jnp.concatenate([xl_ref[...], rowsum(cl_ref[...]), jnp.zeros((7, ncol), f32)], axis=0)
        gw_ref[...] = dtn(s9, dm9)
        gb_ref[...] = rowsum(xa_ref[...]) + rowsum(ca_ref[...])
        dc_ref[...] = vjp(dnt(dm9, w_ref[...]))[0]

    return pl.pallas_call(
        body, name=name,
        out_shape=[jax.ShapeDtypeStruct((D, ncol), f32), jax.ShapeDtypeStruct((1, 6 * D), f32), jax.ShapeDtypeStruct((16, D), f32)],
        compiler_params=_CP())(c9, dmx_all, dmc_all, dmx_l, dmc_l, w_mod_l)


def _lane_sign(rank):
    shape = (1,) * (rank - 1) + (SB,)
    return jnp.where(lax.broadcasted_iota(jnp.int32, shape, rank - 1) < S5N, -1.0, 1.0)


def _s5_build_fn(lre2, lim2, ls, bn, bs, cn, cs, rev):
    sg = _lane_sign(3)
    s = jnp.exp(ls)
    ar, ai = lre2 * s, lim2 * s
    e = jnp.exp(ar)
    nr, ni = e * jnp.cos(ai) - 1.0, e * jnp.sin(ai)
    den = lre2 * lre2 + lim2 * lim2
    cr, ci = (nr * lre2 + ni * lim2) / den, (ni * lre2 - nr * lim2) / den
    bbn = cr * bn + (ci * sg) * bs
    bbs = cr * bs - (ci * sg) * bn

    def powers(ex):
        m, ang = jnp.exp(ex * ar), ex * ai
        return m * jnp.cos(ang), m * jnp.sin(ang) * sg

    def times(tabs, xn, xs):
        f1, f2 = tabs
        return f1[:, :, None, :] * xn[:, None, :, :] + f2[:, :, None, :] * xs[:, None, :, :]

    t = lax.broadcasted_iota(jnp.int32, (1, TC, 1), 1).astype(f32)
    if rev:
        e_src, e_dst, e_out, e_in = t - (TC - 1.0), (TC - 1.0) - t, t, TC - t
    else:
        e_src, e_dst, e_out, e_in = -t, t, (TC - 1.0) - t, t + 1.0
    g = lre2.shape[0]
    flat = lambda a: a.reshape(g, TCP, SB)
    conj = -_lane_sign(4)
    ll = flat(times(powers(e_src), bbn, bbs))
    rr = flat(times(powers(e_dst), cn, cs) * conj)
    mb = flat(times(powers(e_out), bbn, bbs))
    mct = flat(times(powers(e_in), cn, cs) * conj)
    a1, a2 = powers(float(TC))
    row = lax.broadcasted_iota(jnp.int32, (TCP, TCP), 0) // S5P
    col = lax.broadcasted_iota(jnp.int32, (TCP, TCP), 1) // S5P
    mask = jnp.where((col <= row) if rev else (col >= row), 1.0, 0.0)
    m = jnp.concatenate([dnt(ll[j], rr[j])[None] for j in range(g)], axis=0) * mask
    return m, mb, mct, a1, a2


def _gspec(*tail):
    nt = len(tail)
    return pl.BlockSpec((GBK,) + tail, lambda i, _n=nt: (i,) + (0,) * _n)


def _s5_build(params, rev, name):
    def body(l1, l2, ls, bn, bs, cn, cs, m_ref, mb_ref, mc_ref, a1_ref, a2_ref):
        m, mb, mct, a1, a2 = _s5_build_fn(l1[...], l2[...], ls[...], bn[...], bs[...], cn[...], cs[...], rev)
        m_ref[...], mb_ref[...], mc_ref[...] = m.astype(bf16), mb.astype(bf16), mct.astype(bf16)
        a1_ref[...], a2_ref[...] = a1, a2

    vec, pm = _gspec(1, SB), _gspec(S5P, SB)
    return pl.pallas_call(
        body, name=name, grid=(S5G // GBK,),
        in_specs=[vec, vec, _gspec(1, 1), pm, pm, pm, pm],
        out_specs=[_gspec(TCP, TCP), _gspec(TCP, SB), _gspec(TCP, SB), vec, vec],
        out_shape=[jax.ShapeDtypeStruct((S5G, TCP, TCP), bf16), jax.ShapeDtypeStruct((S5G, TCP, SB), bf16),
                   jax.ShapeDtypeStruct((S5G, TCP, SB), bf16), jax.ShapeDtypeStruct((S5G, 1, SB), f32),
                   jax.ShapeDtypeStruct((S5G, 1, SB), f32)],
        compiler_params=_CP(dimension_semantics=("parallel",)),
    )(*params)


def _s5_build_bwd(params, cots, prev, rev, name):
    def body(l1, l2, ls, bn, bs, cn, cs, dm, dmb, dmc, da1, da2, pb, pc, gl1, gl2, gls, gb, gc):
        prim = (l1[...], l2[...], ls[...], bn[...], bs[...], cn[...], cs[...])
        _, vjp = jax.vjp(functools.partial(_s5_build_fn, rev=rev), *prim)
        d1, d2, dls, dbn, dbs, dcn, dcs = vjp((dm[...], dmb[...], dmc[...], da1[...], da2[...]))
        gl1[...] = d1 + pltpu.roll(d1, S5N, axis=2)
        gl2[...] = d2 + pltpu.roll(d2, S5N, axis=2)
        gls[...] = dls
        gb[...] = dbn + pltpu.roll(dbs, S5N, axis=2) + pb[...]
        gc[...] = dcn + pltpu.roll(dcs, S5N, axis=2) + pc[...]

    vec, pm, big = _gspec(1, SB), _gspec(S5P, SB), _gspec(TCP, SB)
    return pl.pallas_call(
        body, name=name, grid=(S5G // GBK,),
        in_specs=[vec, vec, _gspec(1, 1), pm, pm, pm, pm, _gspec(TCP, TCP), big, big, vec, vec, pm, pm],
        out_specs=[vec, vec, _gspec(1, 1), pm, pm],
        out_shape=[jax.ShapeDtypeStruct((S5G, 1, SB), f32), jax.ShapeDtypeStruct((S5G, 1, SB), f32),
                   jax.ShapeDtypeStruct((S5G, 1, 1), f32), jax.ShapeDtypeStruct((S5G, S5P, SB), f32),
                   jax.ShapeDtypeStruct((S5G, S5P, SB), f32)],
        compiler_params=_CP(dimension_semantics=("parallel",)),
    )(*params, *cots, *prev)


def _s5_inc(u, mb_f, mb_b, name):
    nc = u.shape[1]

    def body(u_ref, mf_ref, mb_ref, sf_ref, sb_ref):
        for j in range(GBK):
            sf_ref[:, j, :] = jnp.dot(u_ref[j], mf_ref[j], preferred_element_type=f32)
            sb_ref[:, j, :] = jnp.dot(u_ref[j], mb_ref[j], preferred_element_type=f32)

    sspec = pl.BlockSpec((nc, GBK, SB), lambda i: (0, i, 0))
    return pl.pallas_call(
        body, name=name, grid=(S5G // GBK,), in_specs=[_gspec(nc, TCP), _gspec(TCP, SB), _gspec(TCP, SB)],
        out_specs=[sspec, sspec], out_shape=[jax.ShapeDtypeStruct((nc, S5G, SB), f32)] * 2,
        compiler_params=_CP(dimension_semantics=("parallel",)),
    )(u, mb_f, mb_b)


def _idx_fwd(nctx, nch):
    return lambda i: i


def _idx_rev(nctx, nch):
    return lambda i: jnp.where(i < nctx, nctx - 1 - i, nch + nctx - 1 - i)


def _carry_loop(nc, step, init):
    def trip(i, c):
        for k in range(CARRY_UNROLL):
            c = step(i * CARRY_UNROLL + k, c)
        return c

    return lax.fori_loop(0, nc // CARRY_UNROLL, trip, init)


def _s5_carry(s_f, s_b, a_f, a_b, nctx, name):
    nc = s_f.shape[0]
    idx_b = _idx_rev(nctx, nc)

    def body(sf_ref, sb_ref, f1_ref, f2_ref, b1_ref, b2_ref, hf_ref, hb_ref):
        f1, f2, b1, b2 = f1_ref[...], f2_ref[...], b1_ref[...], b2_ref[...]

        def step(i, c):
            hf, hfs, hb, hbs = c
            rb = idx_b(i)
            hf_ref[i] = hf
            hb_ref[rb] = hb
            sf, sb = sf_ref[i], sb_ref[rb]
            return (f1 * hf + f2 * hfs + sf, f1 * hfs - f2 * hf + pltpu.roll(sf, S5N, axis=1),
                    b1 * hb + b2 * hbs + sb, b1 * hbs - b2 * hb + pltpu.roll(sb, S5N, axis=1))

        z = jnp.zeros((S5G, SB), f32)
        _carry_loop(nc, step, (z, z, z, z))

    return pl.pallas_call(body, name=name, out_shape=[jax.ShapeDtypeStruct(s_f.shape, f32)] * 2,
                          compiler_params=_CP())(s_f, s_b, *a_f, *a_b)


def _s5_carry_bwd(dhp, hp, a1, a2, rev, nctx, name):
    nc = hp.shape[0]
    idx = (_idx_rev if rev else _idx_fwd)(nctx, nc)

    def body(dhp_ref, hp_ref, a1_ref, a2_ref, ds_ref, d1_ref, d2_ref):
        f1, f2 = a1_ref[...], a2_ref[...]

        def step(k, carry):
            ab, abs_, d1, d2 = carry
            r = idx(nc - 1 - k)
            ds_ref[r] = ab
            h, dh = hp_ref[r], dhp_ref[r]
            return (dh + f1 * ab - f2 * abs_, pltpu.roll(dh, S5N, axis=1) + f1 * abs_ + f2 * ab,
                    d1 + ab * h, d2 + ab * pltpu.roll(h, S5N, axis=1))

        z = jnp.zeros((S5G, SB), f32)
        _, _, d1, d2 = _carry_loop(nc, step, (z, z, z, z))
        d1_ref[...], d2_ref[...] = d1, d2

    return pl.pallas_call(
        body, name=name,
        out_shape=[jax.ShapeDtypeStruct(hp.shape, f32), jax.ShapeDtypeStruct((S5G, SB), f32), jax.ShapeDtypeStruct((S5G, SB), f32)],
        compiler_params=_CP())(dhp, hp, a1, a2)


def _s5_out(u, m_f, m_b, hp_f, hp_b, mc_f, mc_b, name):
    nc = u.shape[1]

    def body(u_ref, mf_ref, mb_ref, hf_ref, hb_ref, cf_ref, cb_ref, y_ref):
        for j in range(GBK):
            uj = u_ref[j]
            y_ref[j] = (jnp.dot(uj, mf_ref[j], preferred_element_type=f32) + jnp.dot(uj, mb_ref[j], preferred_element_type=f32)
                        + dnt(hf_ref[:, j, :], cf_ref[j]) + dnt(hb_ref[:, j, :], cb_ref[j])).astype(bf16)

    sspec = pl.BlockSpec((nc, GBK, SB), lambda i: (0, i, 0))
    return pl.pallas_call(
        body, name=name, grid=(S5G // GBK,),
        in_specs=[_gspec(nc, TCP), _gspec(TCP, TCP), _gspec(TCP, TCP), sspec, sspec, _gspec(TCP, SB), _gspec(TCP, SB)],
        out_specs=_gspec(nc, TCP), out_shape=jax.ShapeDtypeStruct((S5G, nc, TCP), bf16),
        compiler_params=_CP(dimension_semantics=("parallel",)),
    )(u, m_f, m_b, hp_f, hp_b, mc_f, mc_b)


def _s5_out_bwd(dy, u, m_f, m_b, hp_f, hp_b, mc_f, mc_b, name):
    nc = u.shape[1]

    def body(dy_ref, u_ref, mf_ref, mb_ref, hf_ref, hb_ref, cf_ref, cb_ref, du_ref, g_ref, dhf_ref, dhb_ref, dcf_ref, dcb_ref):
        for j in range(GBK):
            dyj = dy_ref[j]
            du_ref[j] = dnt(dyj, mf_ref[j]) + dnt(dyj, mb_ref[j])
            g_ref[j] = dtn(u_ref[j], dyj)
            dhf_ref[:, j, :] = dnn(dyj, cf_ref[j])
            dhb_ref[:, j, :] = dnn(dyj, cb_ref[j])
            dcf_ref[j] = dtn(dyj, hf_ref[:, j, :])
            dcb_ref[j] = dtn(dyj, hb_ref[:, j, :])

    sspec = pl.BlockSpec((nc, GBK, SB), lambda i: (0, i, 0))
    sshape = jax.ShapeDtypeStruct((nc, S5G, SB), f32)
    cshape = jax.ShapeDtypeStruct((S5G, TCP, SB), f32)
    return pl.pallas_call(
        body, name=name, grid=(S5G // GBK,),
        in_specs=[_gspec(nc, TCP), _gspec(nc, TCP), _gspec(TCP, TCP), _gspec(TCP, TCP), sspec, sspec, _gspec(TCP, SB), _gspec(TCP, SB)],
        out_specs=[_gspec(nc, TCP), _gspec(TCP, TCP), sspec, sspec, _gspec(TCP, SB), _gspec(TCP, SB)],
        out_shape=[jax.ShapeDtypeStruct((S5G, nc, TCP), f32), jax.ShapeDtypeStruct((S5G, TCP, TCP), f32), sshape, sshape, cshape, cshape],
        compiler_params=_CP(dimension_semantics=("parallel",)),
    )(dy, u, m_f, m_b, hp_f, hp_b, mc_f, mc_b)


def _s5_inc_bwd(du1, u, ds_f, ds_b, mb_f, mb_b, name):
    nc = u.shape[1]

    def body(du1_ref, u_ref, dsf_ref, dsb_ref, mf_ref, mb_ref, du_ref, dmf_ref, dmb_ref):
        for j in range(GBK):
            dsf, dsb = dsf_ref[:, j, :], dsb_ref[:, j, :]
            du_ref[j] = (du1_ref[j] + dnt(dsf, mf_ref[j]) + dnt(dsb, mb_ref[j])).astype(bf16)
            dmf_ref[j] = dtn(u_ref[j], dsf)
            dmb_ref[j] = dtn(u_ref[j], dsb)

    sspec = pl.BlockSpec((nc, GBK, SB), lambda i: (0, i, 0))
    cshape = jax.ShapeDtypeStruct((S5G, TCP, SB), f32)
    return pl.pallas_call(
        body, name=name, grid=(S5G // GBK,),
        in_specs=[_gspec(nc, TCP), _gspec(nc, TCP), sspec, sspec, _gspec(TCP, SB), _gspec(TCP, SB)],
        out_specs=[_gspec(nc, TCP), _gspec(TCP, SB), _gspec(TCP, SB)],
        out_shape=[jax.ShapeDtypeStruct((S5G, nc, TCP), bf16), cshape, cshape],
        compiler_params=_CP(dimension_semantics=("parallel",)),
    )(du1, u, ds_f, ds_b, mb_f, mb_b)


def _to_groups(a):
    n = a.shape[0]
    return a.reshape(n // TC, TC, S5G, S5P).transpose(2, 0, 1, 3).reshape(S5G, n // TC, TCP)


def _from_groups(a):
    nc = a.shape[1]
    return a.reshape(S5G, nc, TC, S5P).transpose(1, 2, 0, 3).reshape(nc * TC, S5W)


def _swap_pairs(t):
    lane = lax.broadcasted_iota(jnp.int32, t.shape, 1)
    return jnp.where(lane % 2 == 0, pltpu.roll(t, DH - 1, axis=1), pltpu.roll(t, 1, axis=1))


def _rot(t, cosf, sins):
    return t * cosf + _swap_pairs(t) * sins


def _rot_t(d, cosf, sins):
    return d * cosf - _swap_pairs(d) * sins


def _ret_chunk(qr, kr, v, rp, ld, rev):
    pos = lax.broadcasted_iota(jnp.int32, (T, 1), 0).astype(f32)
    diff = pos - lax.broadcasted_iota(jnp.int32, (1, T), 1).astype(f32)
    if rev:
        keep, dist = diff < 0, jnp.maximum(-diff, 0.0)
        xi, zeta = jnp.exp(ld * (T - pos)), jnp.exp(ld * pos)
    else:
        keep, dist = diff >= 0, jnp.maximum(diff, 0.0)
        xi, zeta = jnp.exp(ld * (pos + 1.0)), jnp.exp(ld * (T - 1.0 - pos))
    dm = jnp.where(keep, jnp.exp(ld * dist), 0.0)
    out = dnn(dnt(qr, kr) * dm, v) + dnn(qr * xi, rp)
    rn = jnp.exp(ld * float(T)) * rp + dtn(kr * zeta, v)
    return out, rn


def _ret_fwd(p_ext, ld8, rev, nctx, name, cargo=None):
    n = p_ext.shape[0]
    nch = n // T
    idx = (_idx_rev if rev else _idx_fwd)(nctx, nch)
    cg = _Cargo(cargo)

    def body(*refs):
        (q_ref, k_ref, v_ref, ld_ref), (o_ref, rp_ref), (r_s,) = cg.split(refs, 4, 2, 1)
        cg.ride(refs, 4, 2, nch)

        @pl.when(pl.program_id(0) == 0)
        def _():
            r_s[...] = jnp.zeros_like(r_s)

        for h in range(RH):
            sl = slice(h * DH, (h + 1) * DH)
            rp = r_s[h]
            rp_ref[0, h] = rp
            out, rn = _ret_chunk(q_ref[:, sl].astype(f32), k_ref[:, sl].astype(f32), v_ref[:, sl].astype(f32), rp,
                                 ld_ref[h:h + 1, 0:1], rev)
            r_s[h] = rn
            o_ref[:, sl] = out

    def colspec(cb):
        return pl.BlockSpec((T, RW), lambda i, _c=cb: (idx(i), _c))

    return pl.pallas_call(
        body, name=name, grid=(nch,),
        in_specs=[colspec(1), colspec(2), colspec(3), _const_spec((8, 128))] + cg.in_specs(),
        out_specs=[pl.BlockSpec((T, RW), lambda i: (idx(i), 0)), pl.BlockSpec((1, RH, DH, DH), lambda i: (i, 0, 0, 0))] + cg.in_specs(),
        out_shape=[jax.ShapeDtypeStruct((n, RW), f32), jax.ShapeDtypeStruct((nch, RH, DH, DH), f32)] + cg.out_shapes(),
        scratch_shapes=[pltpu.VMEM((RH, DH, DH), f32)] + cg.sems(),
        compiler_params=_CP(dimension_semantics=_ARB),
    )(p_ext, p_ext, p_ext, ld8, *cg.arrays)


def _ret_bwd(p_ext, ld8, rprev, do_ext, rev, nctx, name, cargo=None):
    n = p_ext.shape[0]
    nch = n // T
    idx0 = (_idx_rev if rev else _idx_fwd)(nctx, nch)
    idx = lambda j: idx0(nch - 1 - j)
    cg = _Cargo(cargo)

    def body(*refs):
        ins, (dq_ref, dk_ref, dv_ref, dld_ref), (dr_s,) = cg.split(refs, 6, 4, 1)
        q_ref, k_ref, v_ref, ld_ref, rp_ref, do_ref = ins
        cg.ride(refs, 6, 4, nch)

        @pl.when(pl.program_id(0) == 0)
        def _():
            dr_s[...] = jnp.zeros_like(dr_s)
            dld_ref[...] = jnp.zeros_like(dld_ref)

        for h in range(RH):
            sl = slice(h * DH, (h + 1) * DH)
            _, vjp = jax.vjp(functools.partial(_ret_chunk, rev=rev), q_ref[:, sl].astype(f32), k_ref[:, sl].astype(f32),
                             v_ref[:, sl].astype(f32), rp_ref[0, h], ld_ref[h:h + 1, 0:1])
            dqr, dkr, dv, drp, dld = vjp((do_ref[:, sl], dr_s[h]))
            dr_s[h] = drp
            dq_ref[:, sl], dk_ref[:, sl], dv_ref[:, sl] = dqr, dkr, dv
            dld_ref[h:h + 1, :] += jnp.broadcast_to(dld, (1, 128))

    def colspec(cb):
        return pl.BlockSpec((T, RW), lambda j, _c=cb: (idx(j), _c))

    ospec = pl.BlockSpec((T, RW), lambda j: (idx(j), 0))
    oshape = jax.ShapeDtypeStruct((n, RW), f32)
    return pl.pallas_call(
        body, name=name, grid=(nch,),
        in_specs=[colspec(1), colspec(2), colspec(3), _const_spec((8, 128)),
                  pl.BlockSpec((1, RH, DH, DH), lambda j: (nch - 1 - j, 0, 0, 0)), ospec] + cg.in_specs(),
        out_specs=[ospec, ospec, ospec, _acc_spec((8, 128))] + cg.in_specs(),
        out_shape=[oshape, oshape, oshape, jax.ShapeDtypeStruct((8, 128), f32)] + cg.out_shapes(),
        scratch_shapes=[pltpu.VMEM((RH, DH, DH), f32)] + cg.sems(),
        compiler_params=_CP(dimension_semantics=_ARB),
    )(p_ext, p_ext, p_ext, ld8, rprev, do_ext, *cg.arrays)


def _qk_heads(p, fn_q, fn_k):
    heads = lambda base, fn: [fn(p[:, base + h * DH:base + (h + 1) * DH]) for h in range(RH)]
    return jnp.concatenate([p[:, :S5W]] + heads(S5W, fn_q) + heads(S5W + RW, fn_k) + [p[:, S5W + 2 * RW:]], axis=1)


def _f1_fwd(x, ctx, modx, modc, nw1, w_in_n, cosf, sins, name, cargo=None):
    L = x.shape[0]
    nb = L // R + 1
    scale = DH ** -0.5
    cg = _Cargo(cargo)

    def body(*refs):
        (x_ref, c_ref, mx_ref, mc_ref, nw_ref, w_ref, cos_ref, sin_ref), (p_ref,), _ = cg.split(refs, 8, 1, 0)
        cg.ride(refs, 8, 1, nb)
        is_ctx = pl.program_id(0) == 0
        xin = jnp.where(is_ctx, c_ref[...], x_ref[...])
        sh = jnp.where(is_ctx, mc_ref[0:1], mx_ref[0:1])
        sc = jnp.where(is_ctx, mc_ref[1:2], mx_ref[1:2])
        cf, ss = cos_ref[...], sin_ref[...]
        p = dnn(_mod(_rms(xin, nw_ref[...]), sh, sc), w_ref[...])
        p_ref[...] = _qk_heads(p, lambda t: _rot(t, cf, ss), lambda t: _rot(t * scale, cf, ss)).astype(bf16)

    return pl.pallas_call(
        body, name=name, grid=(nb,),
        in_specs=[pl.BlockSpec((R, D), lambda i: (jnp.maximum(i - 1, 0), 0)), _const_spec((R, D)), _const_spec((6, D)),
                  _const_spec((6, D)), _const_spec((1, D)), _const_spec((D, INC)), pl.BlockSpec((R, DH), lambda i: (i, 0)),
                  pl.BlockSpec((R, DH), lambda i: (i, 0))] + cg.in_specs(),
        out_specs=[pl.BlockSpec((R, INC), lambda i: (i, 0))] + cg.in_specs(),
        out_shape=[jax.ShapeDtypeStruct((L + R, INC), bf16)] + cg.out_shapes(),
        scratch_shapes=cg.sems(),
        compiler_params=_CP(dimension_semantics=_ARB),
    )(x, ctx, modx, modc, nw1, w_in_n, cosf, sins, *cg.arrays)


def _f1_bwd(x, ctx, modx, modc, nw1, w_in_t, cosf, sins, dx1, parts, name, cargo=None):
    L = x.shape[0]
    nb = L // R + 1
    scale = DH ** -0.5
    cg = _Cargo(cargo)

    def body(*refs):
        ins, (gx_ref, dp_ref, h1_ref, dnw_ref, dmx_ref, dmc_ref), _ = cg.split(refs, 18, 6, 0)
        x_ref, c_ref, mx_ref, mc_ref, nw_ref, w_ref, cos_ref, sin_ref, dx1_ref, du0, du1, dq0, dq1, dk0, dk1, dv0, dv1, dg0 = ins
        cg.ride(refs, 18, 6, nb)
        i = pl.program_id(0)
        is_ctx = i == 0

        @pl.when(is_ctx)
        def _():
            dnw_ref[...] = jnp.zeros_like(dnw_ref)
            dmx_ref[...] = jnp.zeros_like(dmx_ref)
            dmc_ref[...] = jnp.zeros_like(dmc_ref)

        cf, ss = cos_ref[...], sin_ref[...]
        dp = jnp.concatenate([du0[...].astype(f32) + du1[...], dq0[...] + dq1[...], dk0[...] + dk1[...], dv0[...] + dv1[...],
                              dg0[...]], axis=1)
        dp = _qk_heads(dp, lambda t: _rot_t(t, cf, ss), lambda t: _rot_t(t, cf, ss) * scale).astype(bf16)
        dp_ref[...] = dp
        xin = jnp.where(is_ctx, c_ref[...], x_ref[...])
        sh = jnp.where(is_ctx, mc_ref[0:1], mx_ref[0:1])
        sc = jnp.where(is_ctx, mc_ref[1:2], mx_ref[1:2])
        dh = dnn(dp, w_ref[...])
        h, vjp = jax.vjp(lambda a, b, c, d: _mod(_rms(a, b), c, d), xin, nw_ref[...], sh, sc)
        dxin, dnw, dsh, dsc = vjp(dh)
        h1_ref[...] = h.astype(bf16)
        gx_ref[...] = dx1_ref[...] + dxin
        dnw_ref[...] += dnw
        wx = jnp.where(is_ctx, 0.0, 1.0)
        dmx_ref[0:1] += dsh * wx
        dmx_ref[1:2] += dsc * wx
        dmc_ref[0:1] += dsh * (1.0 - wx)
        dmc_ref[1:2] += dsc * (1.0 - wx)

    lat = pl.BlockSpec((R, D), lambda i: (jnp.maximum(i - 1, 0), 0))
    ext = pl.BlockSpec((R, S5W), lambda i: (i, 0))
    return pl.pallas_call(
        body, name=name, grid=(nb,),
        in_specs=[lat, _const_spec((R, D)), _const_spec((6, D)), _const_spec((6, D)), _const_spec((1, D)), _const_spec((INC, D)),
                  pl.BlockSpec((R, DH), lambda i: (i, 0)), pl.BlockSpec((R, DH), lambda i: (i, 0)), lat] + [ext] * 9 + cg.in_specs(),
        out_specs=[lat, pl.BlockSpec((R, INC), lambda i: (i, 0)), pl.BlockSpec((R, D), lambda i: (i, 0)),
                   _acc_spec((1, D)), _acc_spec((6, D)), _acc_spec((6, D))] + cg.in_specs(),
        out_shape=[jax.ShapeDtypeStruct((L, D), f32), jax.ShapeDtypeStruct((L + R, INC), bf16),
                   jax.ShapeDtypeStruct((L + R, D), bf16), jax.ShapeDtypeStruct((1, D), f32),
                   jax.ShapeDtypeStruct((6, D), f32), jax.ShapeDtypeStruct((6, D), f32)] + cg.out_shapes(),
        scratch_shapes=cg.sems(),
        compiler_params=_CP(dimension_semantics=_ARB),
    )(x, ctx, modx, modc, nw1, w_in_t, cosf, sins, dx1, *parts, *cg.arrays)


def _ret_post(yr, g):
    outs = []
    for h in range(RH):
        yh = yr[:, h * DH:(h + 1) * DH]
        mu = jnp.mean(yh, axis=-1, keepdims=True)
        var = jnp.mean((yh - mu) ** 2, axis=-1, keepdims=True)
        outs.append((yh - mu) * lax.rsqrt(var + EPS))
    return jax.nn.silu(g) * jnp.concatenate(outs, axis=1)


def _mix_fn(ys, u, of, ob, g, x, dvec, bglu, gate1, pz, pm, wglu, wout):
    s = jax.nn.gelu(ys + dvec * u)
    z = dnn(s, wglu) + bglu + pz
    cat = jnp.concatenate([s * jax.nn.sigmoid(z), _ret_post(of + ob, g)], axis=1)
    mix = dnn(cat, wout) + pm
    return x + gate1 * mix, (s, cat)


def _mix_fwd(x, ys, of, ob, p_ext, dvec, bglu, modx, wglu, wout, name, cargo=None):
    L = x.shape[0]
    nb = L // R
    cg = _Cargo(cargo)

    def body(*refs):
        ins, (x1_ref,), _ = cg.split(refs, 11, 1, 0)
        x_ref, ys_ref, of_ref, ob_ref, u_ref, g_ref, d_ref, b_ref, mx_ref, wg_ref, wo_ref = ins
        cg.ride(refs, 11, 1, nb)
        x1_ref[...] = _mix_fn(ys_ref[...].astype(f32), u_ref[...].astype(f32), of_ref[...], ob_ref[...], g_ref[...].astype(f32),
                              x_ref[...], d_ref[...], b_ref[...], mx_ref[2:3], 0.0, 0.0, wg_ref[...], wo_ref[...])[0]

    ext = pl.BlockSpec((R, S5W), lambda i: (i + 1, 0))
    return pl.pallas_call(
        body, name=name, grid=(nb,),
        in_specs=[pl.BlockSpec((R, D), lambda i: (i, 0)), ext, ext, ext, ext, pl.BlockSpec((R, RW), lambda i: (i + 1, 4)),
                  _const_spec((1, S5W)), _const_spec((1, S5W)), _const_spec((6, D)), _const_spec((S5W, S5W)), _const_spec((D, D))]
        + cg.in_specs(),
        out_specs=[pl.BlockSpec((R, D), lambda i: (i, 0))] + cg.in_specs(),
        out_shape=[jax.ShapeDtypeStruct((L, D), f32)] + cg.out_shapes(),
        scratch_shapes=cg.sems(),
        compiler_params=_CP(dimension_semantics=_ARB),
    )(x, ys, of, ob, p_ext, p_ext, dvec, bglu, modx, wglu, wout, *cg.arrays)


def _mix_bwd(x, ys, of, ob, p_ext, dvec, bglu, modx, wglu, wout, dx1, name, cargo=None):
    L = x.shape[0]
    nb = L // R + 1
    cg = _Cargo(cargo)

    def body(*refs):
        ins, outs, _ = cg.split(refs, 12, 11, 0)
        x_ref, ys_ref, of_ref, ob_ref, u_ref, g_ref, d_ref, b_ref, mx_ref, wg_ref, wo_ref, dx1_ref = ins
        dy_ref, dud_ref, do_ref, dg_ref, cat_ref, dmix_ref, s_ref, dz_ref, dd_ref, db_ref, dg1_ref = outs
        cg.ride(refs, 12, 11, nb)
        i = pl.program_id(0)

        @pl.when(i == 0)
        def _():
            for r in outs:
                r[...] = jnp.zeros_like(r)

        @pl.when(i > 0)
        def _():
            fn = lambda ys_, u_, of_, g_, d_, b_, g1_, pz_, pm_: _mix_fn(
                ys_, u_, of_, ob_ref[...], g_, x_ref[...], d_, b_, g1_, pz_, pm_, wg_ref[...], wo_ref[...])
            _, vjp, (s, cat) = jax.vjp(fn, ys_ref[...].astype(f32), u_ref[...].astype(f32), of_ref[...], g_ref[...].astype(f32), d_ref[...],
                                       b_ref[...], mx_ref[2:3], jnp.zeros((R, S5W), f32), jnp.zeros((R, D), f32), has_aux=True)
            dy, dud, do, dg, dd, db, dg1, dz, dmix = vjp(dx1_ref[...])
            dy_ref[...], dud_ref[...], do_ref[...], dg_ref[...] = dy.astype(bf16), dud, do, dg
            cat_ref[...], dmix_ref[...] = cat.astype(bf16), dmix.astype(bf16)
            s_ref[...], dz_ref[...] = s.astype(bf16), dz.astype(bf16)
            dd_ref[...] += dd
            db_ref[...] += db
            dg1_ref[...] += dg1

    lat = pl.BlockSpec((R, D), lambda i: (jnp.maximum(i - 1, 0), 0))
    lat5 = pl.BlockSpec((R, S5W), lambda i: (jnp.maximum(i - 1, 0), 0))
    ext = pl.BlockSpec((R, S5W), lambda i: (i, 0))
    eshape = jax.ShapeDtypeStruct((L + R, S5W), f32)
    return pl.pallas_call(
        body, name=name, grid=(nb,),
        in_specs=[lat, ext, ext, ext, ext, pl.BlockSpec((R, RW), lambda i: (i, 4)),
                  _const_spec((1, S5W)), _const_spec((1, S5W)), _const_spec((6, D)), _const_spec((S5W, S5W)), _const_spec((D, D)), lat]
        + cg.in_specs(),
        out_specs=[ext, ext, ext, ext, lat, lat, lat5, lat5, _acc_spec((1, S5W)), _acc_spec((1, S5W)), _acc_spec((1, D))]
        + cg.in_specs(),
        out_shape=[jax.ShapeDtypeStruct((L + R, S5W), bf16), eshape, eshape, eshape, jax.ShapeDtypeStruct((L, D), bf16),
                   jax.ShapeDtypeStruct((L, D), bf16), jax.ShapeDtypeStruct((L, S5W), bf16), jax.ShapeDtypeStruct((L, S5W), bf16),
                   jax.ShapeDtypeStruct((1, S5W), f32), jax.ShapeDtypeStruct((1, S5W), f32), jax.ShapeDtypeStruct((1, D), f32)]
        + cg.out_shapes(),
        scratch_shapes=cg.sems(),
        compiler_params=_CP(dimension_semantics=_ARB),
    )(x, ys, of, ob, p_ext, p_ext, dvec, bglu, modx, wglu, wout, dx1, *cg.arrays)


def _ffn_tail(gc, a, x1, gate2, fnw, pf, wdown, wdown_t, tgt):
    f = jax.nn.gelu(gc) * a
    ffn = _dnn_const(f, wdown, wdown_t) + pf
    y = _rms(x1 + gate2 * ffn, fnw)
    err = y - tgt
    loss = 0.5 * jnp.sum(jnp.mean(err * err, axis=-1, keepdims=True), axis=0, keepdims=True)
    return loss, f


def _ffn_fwd(x1, tgt, nw2, modx, w_a, w_g, cw, cb, wdown, wdown_t, fnw, name):
    L = x1.shape[0]
    nb = L // RF
    per = RF // HALO

    def body(x_ref, xp_ref, xn_ref, t_ref, nw_ref, mx_ref, wa_ref, wg_ref, cw_ref, cb_ref, wd_ref, wdt_ref, fn_ref,
             dx2_ref, da_ref, dgc_ref, f_ref, dffn_ref, loss_ref, dfn_ref, dg2_ref, dcb_ref, dcw_ref):
        i = pl.program_id(0)

        @pl.when(i == 0)
        def _():
            for r in (loss_ref, dfn_ref, dg2_ref, dcb_ref, dcw_ref):
                r[...] = jnp.zeros_like(r)

        nw, sh, sc, gate2 = nw_ref[...], mx_ref[3:4], mx_ref[4:5], mx_ref[5:6]
        x1b = x_ref[...]
        h2 = _mod(_rms(x1b, nw), sh, sc)
        h2e = jnp.concatenate([_mod(_rms(xp_ref[...], nw), sh, sc), h2, _mod(_rms(xn_ref[...], nw), sh, sc)], axis=0)
        a = dnn(h2, wa_ref[...])
        ge = dnn(h2e, wg_ref[...])
        g = ge[HALO:HALO + RF]
        gp = ge[HALO - 1:HALO] * jnp.where(i > 0, 1.0, 0.0)
        gn = ge[HALO + RF:HALO + RF + 1] * jnp.where(i < nb - 1, 1.0, 0.0)
        row = lax.broadcasted_iota(jnp.int32, (RF, 1), 0)
        g_prev = jnp.where(row == 0, gp, pltpu.roll(g, 1, axis=0))
        g_next = jnp.where(row == RF - 1, gn, pltpu.roll(g, RF - 1, axis=0))
        gc = cb_ref[...] + g_prev * cw_ref[0:1] + g * cw_ref[1:2] + g_next * cw_ref[2:3]
        fn = lambda gc_, a_, x_, g2_, fw_, pf_: _ffn_tail(gc_, a_, x_, g2_, fw_, pf_, wd_ref[...], wdt_ref[...], t_ref[...])
        loss, vjp, f = jax.vjp(fn, gc, a, x1b, gate2, fn_ref[...], jnp.zeros((RF, D), f32), has_aux=True)
        dgc, da, dx2, dg2, dfw, dffn = vjp(jnp.ones((1, 1), f32))
        dx2_ref[...] = dx2
        da_ref[...], dgc_ref[...] = da.astype(bf16), dgc
        f_ref[...], dffn_ref[...] = f.astype(bf16), dffn.astype(bf16)
        loss_ref[...] += jnp.broadcast_to(loss, (1, 128))
        dfn_ref[...] += dfw
        dg2_ref[...] += dg2
        dcb_ref[...] += jnp.sum(dgc, axis=0, keepdims=True)
        dcw_ref[0:1] += jnp.sum(dgc * g_prev, axis=0, keepdims=True)
        dcw_ref[1:2] += jnp.sum(dgc * g, axis=0, keepdims=True)
        dcw_ref[2:3] += jnp.sum(dgc * g_next, axis=0, keepdims=True)

    blk = lambda w: pl.BlockSpec((RF, w), lambda i: (i, 0))
    return pl.pallas_call(
        body, name=name, grid=(nb,),
        in_specs=[blk(D), pl.BlockSpec((HALO, D), lambda i: (jnp.maximum(i * per - 1, 0), 0)),
                  pl.BlockSpec((HALO, D), lambda i: (jnp.minimum((i + 1) * per, L // HALO - 1), 0)), blk(D),
                  _const_spec((1, D)), _const_spec((6, D)), _const_spec((D, DFF)), _const_spec((D, DFF)), _const_spec((3, DFF)),
                  _const_spec((1, DFF)), _const_spec((DFF, D)), _const_spec((D, DFF)), _const_spec((1, D))],
        out_specs=[blk(D), blk(DFF), blk(DFF), blk(DFF), blk(D), _acc_spec((1, 128)), _acc_spec((1, D)), _acc_spec((1, D)),
                   _acc_spec((1, DFF)), _acc_spec((3, DFF))],
        out_shape=[jax.ShapeDtypeStruct((L, D), f32), jax.ShapeDtypeStruct((L, DFF), bf16), jax.ShapeDtypeStruct((L, DFF), f32),
                   jax.ShapeDtypeStruct((L, DFF), bf16), jax.ShapeDtypeStruct((L, D), bf16), jax.ShapeDtypeStruct((1, 128), f32),
                   jax.ShapeDtypeStruct((1, D), f32), jax.ShapeDtypeStruct((1, D), f32), jax.ShapeDtypeStruct((1, DFF), f32),
                   jax.ShapeDtypeStruct((3, DFF), f32)],
        compiler_params=_CP(dimension_semantics=_ARB),
    )(x1, x1, x1, tgt, nw2, modx, w_a, w_g, cw, cb, wdown, wdown_t, fnw)


def _ffn_bwd(x1, dx2, da, dgc, nw2, modx, wup_t, cw, name):
    L = x1.shape[0]
    nb = L // RF
    per = RF // HALO

    def body(x_ref, dx2_ref, da_ref, dgc_ref, dgp_ref, dgn_ref, nw_ref, mx_ref, wu_ref, cw_ref,
             dx1_ref, dag_ref, h2_ref, dnw_ref, dmx_ref):
        i = pl.program_id(0)

        @pl.when(i == 0)
        def _():
            dnw_ref[...] = jnp.zeros_like(dnw_ref)
            dmx_ref[...] = jnp.zeros_like(dmx_ref)

        dgc_b = dgc_ref[...]
        before = dgp_ref[HALO - 1:HALO] * jnp.where(i > 0, 1.0, 0.0)
        after = dgn_ref[0:1] * jnp.where(i < nb - 1, 1.0, 0.0)
        row = lax.broadcasted_iota(jnp.int32, (RF, 1), 0)
        d_prev = jnp.where(row == 0, before, pltpu.roll(dgc_b, 1, axis=0))
        d_next = jnp.where(row == RF - 1, after, pltpu.roll(dgc_b, RF - 1, axis=0))
        dg = cw_ref[0:1] * d_next + cw_ref[1:2] * dgc_b + cw_ref[2:3] * d_prev
        dag = jnp.concatenate([da_ref[...], dg.astype(bf16)], axis=1)
        dag_ref[...] = dag
        dh2 = dnn(dag, wu_ref[...])
        h2, vjp = jax.vjp(lambda a, b, c, d: _mod(_rms(a, b), c, d), x_ref[...], nw_ref[...], mx_ref[3:4], mx_ref[4:5])
        dxa, dnw, dsh, dsc = vjp(dh2)
        h2_ref[...] = h2.astype(bf16)
        dx1_ref[...] = dx2_ref[...] + dxa
        dnw_ref[...] += dnw
        dmx_ref[3:4] += dsh
        dmx_ref[4:5] += dsc

    blk = lambda w: pl.BlockSpec((RF, w), lambda i: (i, 0))
    return pl.pallas_call(
        body, name=name, grid=(nb,),
        in_specs=[blk(D), blk(D), blk(DFF), blk(DFF), pl.BlockSpec((HALO, DFF), lambda i: (jnp.maximum(i * per - 1, 0), 0)),
                  pl.BlockSpec((HALO, DFF), lambda i: (jnp.minimum((i + 1) * per, L // HALO - 1), 0)),
                  _const_spec((1, D)), _const_spec((6, D)), _const_spec((2 * DFF, D)), _const_spec((3, DFF))],
        out_specs=[blk(D), blk(2 * DFF), blk(D), _acc_spec((1, D)), _acc_spec((6, D))],
        out_shape=[jax.ShapeDtypeStruct((L, D), f32), jax.ShapeDtypeStruct((L, 2 * DFF), bf16), jax.ShapeDtypeStruct((L, D), bf16),
                   jax.ShapeDtypeStruct((1, D), f32), jax.ShapeDtypeStruct((6, D), f32)],
        compiler_params=_CP(dimension_semantics=_ARB),
    )(x1, dx2, da, dgc, dgc, dgc, nw2, modx, wup_t, cw)


def _matmul_tn(a, b, name):
    k, m = a.shape
    n = b.shape[1]
    divs = lambda d: [c for c in range(d, 0, -128) if d % c == 0]
    _, tm, tn = min((m * (n // cn) + n * (m // cm), cm, cn) for cm in divs(m) for cn in divs(n) if cm * cn * 4 <= ACC_TILE_BYTES)
    tk = next(c for c in (512, 768, 256, 128) if k % c == 0)
    nk = k // tk

    def body(a_ref, b_ref, o_ref, acc):
        q = pl.program_id(2)

        @pl.when(q == 0)
        def _():
            acc[...] = jnp.zeros_like(acc)

        acc[...] += dtn(a_ref[...], b_ref[...])

        @pl.when(q == nk - 1)
        def _():
            o_ref[...] = acc[...].astype(bf16)

    return pl.pallas_call(
        body, name=name, grid=(m // tm, n // tn, nk),
        in_specs=[pl.BlockSpec((tk, tm), lambda i, j, q: (q, i)), pl.BlockSpec((tk, tn), lambda i, j, q: (q, j))],
        out_specs=pl.BlockSpec((tm, tn), lambda i, j, q: (i, j)),
        out_shape=jax.ShapeDtypeStruct((m, n), bf16),
        scratch_shapes=[pltpu.VMEM((tm, tn), f32)],
        compiler_params=_CP(dimension_semantics=("parallel", "parallel", "arbitrary")),
    )(a, b)


def _adamw_refs(w_ref, g_ref, m_ref, v_ref, d_ref, nm_ref, nv_ref):
    c1, c2 = 1.0 - B1 ** STEP, 1.0 - B2 ** STEP
    gg = g_ref[...]
    nm = B1 * m_ref[...] + (1.0 - B1) * gg
    nv = B2 * v_ref[...] + (1.0 - B2) * jnp.square(gg)
    d_ref[...] = -LR * ((nm / c1) / (jnp.sqrt(nv / c2) + AEPS) + WD * w_ref[...])
    nm_ref[...], nv_ref[...] = nm, nv


def _adamw(w, g, m, v, name):
    def body(*refs):
        _adamw_refs(*refs)

    return pl.pallas_call(body, name=name, out_shape=[jax.ShapeDtypeStruct(w.shape, f32)] * 3, compiler_params=_CP())(w, g, m, v)


def _adamw_many(ws, gs, ms, vs, name):
    n = len(ws)

    def body(*refs):
        for k in range(n):
            _adamw_refs(*[refs[j * n + k] for j in range(7)])

    outs = pl.pallas_call(body, name=name, out_shape=[jax.ShapeDtypeStruct(w.shape, f32) for w in ws] * 3,
                          compiler_params=_CP())(*ws, *gs, *ms, *vs)
    return outs[:n], outs[n:2 * n], outs[2 * n:]


SMALL = ["conv_w", "c_ctx", "norm1_w", "s5_lambda_re_f", "s5_lambda_im_f", "s5_log_step_f", "s5_lambda_re_b", "s5_lambda_im_b",
         "s5_log_step_b", "s5_b_re", "s5_b_im", "s5_c_re", "s5_c_im", "s5_d", "s5_b_glu", "ret_log_decay_f", "ret_log_decay_b",
         "norm2_w", "conv_b", "final_norm_w"]
WEIGHTS = ["c_ctx", "w_mod", "b_mod", "norm1_w", "w_in", "s5_lambda_re_f", "s5_lambda_im_f", "s5_log_step_f", "s5_lambda_re_b",
           "s5_lambda_im_b", "s5_log_step_b", "s5_b_re", "s5_b_im", "s5_c_re", "s5_c_im", "s5_d", "s5_w_glu", "s5_b_glu",
           "ret_log_decay_f", "ret_log_decay_b", "w_out", "norm2_w", "w_up", "conv_w", "conv_b", "w_down", "final_norm_w"]


def _pack_small(vals):
    flat, offs, o = [], [], 0
    for a in vals:
        n = a.size
        npad = -n % 128
        flat.append(jnp.pad(a.reshape(-1), (0, npad)))
        offs.append((o, n))
        o += n + npad
    tail = -o % 1024
    if tail:
        flat.append(jnp.zeros((tail,), f32))
    return jnp.concatenate(flat).reshape(-1, 128), offs


def _unpack_small(packed, offs, shapes):
    flat = packed.reshape(-1)
    return [flat[o:o + n].reshape(s) for (o, n), s in zip(offs, shapes)]


def _rope_tables(L, nctx_rows):
    t = np.arange(L)
    inv = (ROPE_THETA ** (-np.arange(DH // 4, dtype=np.float64) / (DH // 4))).astype(np.float32)
    ang = np.concatenate([(t // GRID_W).astype(np.float32)[:, None] * inv, (t % GRID_W).astype(np.float32)[:, None] * inv], axis=-1)
    cos = np.repeat(np.cos(ang).astype(np.float32), 2, axis=1)
    sin = np.repeat(np.sin(ang).astype(np.float32), 2, axis=1) * np.tile(np.array([-1.0, 1.0], np.float32), DH // 2)
    cosf = np.concatenate([np.ones((nctx_rows, DH), np.float32), cos], axis=0)
    sins = np.concatenate([np.zeros((nctx_rows, DH), np.float32), sin], axis=0)
    return jnp.asarray(cosf), jnp.asarray(sins)


def kernel(x, c, ctx, c_ctx, w_mod, b_mod, norm1_w, w_in, s5_lambda_re_f, s5_lambda_im_f, s5_log_step_f, s5_lambda_re_b, s5_lambda_im_b, s5_log_step_b, s5_b_re, s5_b_im, s5_c_re, s5_c_im, s5_d, s5_w_glu, s5_b_glu, ret_log_decay_f, ret_log_decay_b, w_out, norm2_w, w_up, conv_w, conv_b, w_down, final_norm_w, loss_target, m_c_ctx, m_w_mod, m_b_mod, m_norm1_w, m_w_in, m_s5_lambda_re_f, m_s5_lambda_im_f, m_s5_log_step_f, m_s5_lambda_re_b, m_s5_lambda_im_b, m_s5_log_step_b, m_s5_b_re, m_s5_b_im, m_s5_c_re, m_s5_c_im, m_s5_d, m_s5_w_glu, m_s5_b_glu, m_ret_log_decay_f, m_ret_log_decay_b, m_w_out, m_norm2_w, m_w_up, m_conv_w, m_conv_b, m_w_down, m_final_norm_w, v_c_ctx, v_w_mod, v_b_mod, v_norm1_w, v_w_in, v_s5_lambda_re_f, v_s5_lambda_im_f, v_s5_log_step_f, v_s5_lambda_re_b, v_s5_lambda_im_b, v_s5_log_step_b, v_s5_b_re, v_s5_b_im, v_s5_c_re, v_s5_c_im, v_s5_d, v_s5_w_glu, v_s5_b_glu, v_ret_log_decay_f, v_ret_log_decay_b, v_w_out, v_norm2_w, v_w_up, v_conv_w, v_conv_b, v_w_down, v_final_norm_w):
    args = dict(locals())
    W = {n: args[n] for n in WEIGHTS}
    M = {n: args["m_" + n] for n in WEIGHTS}
    V = {n: args["v_" + n] for n in WEIGHTS}
    me = _me()
    x2, ctx2, tgt = x[0], ctx[0], loss_target[0]
    L, Lc = x2.shape[0], ctx2.shape[0]
    assert Lc == R and L % R == 0 and L % GRID_W == 0
    nctx = Lc // T

    c_all = _all_gather_small(jnp.pad(c, ((0, 7), (0, 0))), "gather_c")[:, 0, :]
    c9 = jnp.concatenate([c_all, c_ctx[None], jnp.zeros((7, D), f32)], axis=0)
    w_mod_l = w_mod[0]
    ncol = w_mod_l.shape[1]
    m_part = _ada_fwd(c9, w_mod_l, "ada_fwd")
    m_all = _all_gather_small(m_part, "gather_mod").transpose(1, 0, 2).reshape(16, 6, D)
    modx, modc = _mod_select(m_all, b_mod.reshape(6, D), "mod_select")

    w_in_tl, w_up_tl = w_in[0].T.astype(bf16), w_up[0].T.astype(bf16)
    w_out_l, w_down_l, w_glu_l = w_out[0].astype(bf16), w_down[0].astype(bf16), s5_w_glu[0].astype(bf16)
    half_up = w_up_tl.shape[0] // 2
    (w_in_g,) = _exchange([w_in_tl], False, "gather_w_in")
    w_in_t = w_in_g.reshape(INC, D)
    per_cv = conv_w.shape[2]
    conv_pad = jnp.pad(conv_w[0], ((0, 5), (0, 128 * 3 - per_cv)))
    conv_f = _all_gather_small(conv_pad, "gather_conv")[:, :3, :per_cv].transpose(1, 0, 2).reshape(3, DFF)

    pair = lambda a, b: jnp.concatenate([a, b], axis=-1)
    bre_g, bim_g = s5_b_re[0].transpose(0, 2, 1), s5_b_im[0].transpose(0, 2, 1)
    cre_g, cim_g = s5_c_re[0], s5_c_im[0]
    shared = (pair(bre_g, bim_g), pair(bim_g, bre_g), pair(cre_g, cim_g), pair(cim_g, cre_g))
    s5p = {}
    for tag, lre, lim, ls in (("f", s5_lambda_re_f, s5_lambda_im_f, s5_log_step_f), ("b", s5_lambda_re_b, s5_lambda_im_b, s5_log_step_b)):
        s5p[tag] = (pair(lre[0], lre[0])[:, None, :], pair(lim[0], lim[0])[:, None, :], ls[0].reshape(S5G, 1, 1)) + shared
    m_f, mb_f, mc_f, a1_f, a2_f = _s5_build(s5p["f"], False, "s5_build_f")
    m_b, mb_b, mc_b, a1_b, a2_b = _s5_build(s5p["b"], True, "s5_build_b")
    a1_f, a2_f, a1_b, a2_b = (a.reshape(S5G, SB) for a in (a1_f, a2_f, a1_b, a2_b))

    nw1, nw2, fnw = norm1_w, norm2_w, final_norm_w[None]
    cosf, sins = _rope_tables(L, Lc)
    p_ext, w_out_g, w_glu_g = _f1_fwd(x2, ctx2, modx, modc, nw1, w_in_t.T, cosf, sins, "f1_fwd", cargo=([w_out_l, w_glu_l], False))
    nctx5 = Lc // TC
    u_g = _to_groups(p_ext[:, :S5W])
    s_f, s_b = _s5_inc(u_g, mb_f, mb_b, "s5_inc")
    hp_f, hp_b = _s5_carry(s_f, s_b, (a1_f, a2_f), (a1_b, a2_b), nctx5, "s5_carry")
    ys = _from_groups(_s5_out(u_g, m_f, m_b, hp_f, hp_b, mc_f, mc_b, "s5_out"))
    ld8 = lambda ld: jnp.pad(jnp.broadcast_to(ld[0][:, None], (RH, 128)), ((0, 8 - RH), (0, 0)))
    ldf8, ldb8 = ld8(ret_log_decay_f), ld8(ret_log_decay_b)
    of, rp_f, w_up_g1 = _ret_fwd(p_ext, ldf8, False, nctx, "ret_fwd_f", cargo=([w_up_tl[:half_up]], False))
    ob, rp_b, w_up_g2 = _ret_fwd(p_ext, ldb8, True, nctx, "ret_fwd_b", cargo=([w_up_tl[half_up:]], False))
    w_out_f, w_glu_f = w_out_g.reshape(D, D), w_glu_g.reshape(S5W, S5W)
    x1, w_down_g = _mix_fwd(x2, ys, of, ob, p_ext, s5_d, s5_b_glu, modx, w_glu_f, w_out_f, "mix_fwd", cargo=([w_down_l], False))
    w_down_f = w_down_g.reshape(DFF, D)
    w_up_t = jnp.concatenate([w_up_g1, w_up_g2], axis=1).reshape(2 * DFF, D)

    (dx2, da, dgc, f_act, dffn, loss_acc, g_fnw, g_gate2, g_cb, g_cw) = _ffn_fwd(
        x1, tgt, nw2, modx, w_up_t[:DFF].T, w_up_t[DFF:].T, conv_f, conv_b, w_down_f, w_down_f.T, fnw, "ffn_fwd")
    dx1, dag, h2, g_nw2, dmx2 = _ffn_bwd(x1, dx2, da, dgc, nw2, modx, w_up_t, conv_f, "ffn_bwd")
    gw_down = _matmul_tn(f_act, dffn, "dw_down").reshape(NDEV, -1, D)
    gw_up_t = _matmul_tn(dag, h2, "dw_up").reshape(NDEV, -1, D)
    (dy_e, dud_e, do_e, dg_e, cat, dmix, s_act, dz, g_d, g_bglu, g_gate1, l_down) = _mix_bwd(
        x2, ys, of, ob, p_ext, s5_d, s5_b_glu, modx, w_glu_f, w_out_f, dx1, "mix_bwd", cargo=([gw_down], True))
    gw_out = _matmul_tn(cat, dmix, "dw_out").reshape(NDEV, -1, D)
    gw_glu = _matmul_tn(s_act, dz, "dw_glu").reshape(NDEV, -1, S5W)
    dq_f, dk_f, dv_f, gld_f, l_up1 = _ret_bwd(p_ext, ldf8, rp_f, do_e, False, nctx, "ret_bwd_f",
                                              cargo=([gw_up_t[:, :half_up]], True))
    dq_b, dk_b, dv_b, gld_b, l_out, l_glu, l_up2 = _ret_bwd(p_ext, ldb8, rp_b, do_e, True, nctx, "ret_bwd_b",
                                                            cargo=([gw_out, gw_glu, gw_up_t[:, half_up:]], True))

    du1, g_m, dhp_f, dhp_b, dmc_f, dmc_b = _s5_out_bwd(_to_groups(dy_e), u_g, m_f, m_b, hp_f, hp_b, mc_f, mc_b, "s5_out_bwd")
    ds_f, da1_f, da2_f = _s5_carry_bwd(dhp_f, hp_f, a1_f, a2_f, False, nctx5, "s5_carry_bwd_f")
    ds_b, da1_b, da2_b = _s5_carry_bwd(dhp_b, hp_b, a1_b, a2_b, True, nctx5, "s5_carry_bwd_b")
    du_g, dmb_f, dmb_b = _s5_inc_bwd(du1, u_g, ds_f, ds_b, mb_f, mb_b, "s5_inc_bwd")
    zero_p = jnp.zeros((S5G, S5P, SB), f32)
    gf = _s5_build_bwd(s5p["f"], (g_m, dmb_f, dmc_f, da1_f[:, None, :], da2_f[:, None, :]), (zero_p, zero_p), False, "s5_build_bwd_f")
    gb = _s5_build_bwd(s5p["b"], (g_m, dmb_b, dmc_b, da1_b[:, None, :], da2_b[:, None, :]), (gf[3], gf[4]), True, "s5_build_bwd_b")
    g_bre, g_bim = gb[3][:, :, :S5N].transpose(0, 2, 1), gb[3][:, :, S5N:].transpose(0, 2, 1)
    g_cre, g_cim = gb[4][:, :, :S5N], gb[4][:, :, S5N:]

    early = {
        "conv_w": g_cw, "s5_lambda_re_f": gf[0][:, 0, :S5N], "s5_lambda_im_f": gf[1][:, 0, :S5N],
        "s5_log_step_f": gf[2], "s5_lambda_re_b": gb[0][:, 0, :S5N], "s5_lambda_im_b": gb[1][:, 0, :S5N], "s5_log_step_b": gb[2],
        "s5_b_re": g_bre, "s5_b_im": g_bim, "s5_c_re": g_cre, "s5_c_im": g_cim, "s5_d": g_d, "s5_b_glu": g_bglu,
        "ret_log_decay_f": gld_f[:RH, 0], "ret_log_decay_b": gld_b[:RH, 0], "norm2_w": g_nw2, "conv_b": g_cb, "final_norm_w": g_fnw,
    }
    e_names = [n for n in SMALL if n in early]
    packed_e, eoffs = _pack_small([early[n].astype(f32) for n in e_names])
    grad_x, dp_ext, h1, g_nw1, dmx1, dmc1, land_e = _f1_bwd(
        x2, ctx2, modx, modc, nw1, w_in_t, cosf, sins, dx1, (_from_groups(du_g), dud_e, dq_f, dq_b, dk_f, dk_b, dv_f, dv_b, dg_e), "f1_bwd",
        cargo=([packed_e], False))
    gw_in_t = _matmul_tn(dp_ext, h1, "dw_in").reshape(NDEV, -1, D)
    (l_in,) = _exchange([gw_in_t], True, "scatter_dw_in")

    dmx = dmx1 + dmx2
    dmx = dmx.at[2].set(g_gate1[0]).at[5].set(g_gate2[0])
    dm_me = jnp.stack([dmx.reshape(-1), dmc1.reshape(-1)], axis=0)
    dm_all = _all_gather_small(dm_me.reshape(8, -1), "gather_dmod").reshape(NDEV, 2, 6 * D)
    dmx_all, dmc_all = dm_all[:, 0, :], dm_all[:, 1, :]
    my_cols = lambda a: lax.dynamic_slice(a, (0, me * ncol), (NDEV, ncol))
    gw_mod, g_bmod, dc9 = _ada_bwd(c9, dmx_all, dmc_all, my_cols(dmx_all), my_cols(dmc_all), w_mod_l, "ada_bwd")

    sshape = lambda n: (3, DFF) if n == "conv_w" else W[n].shape
    G = dict(zip(e_names, _unpack_small(_sum8(land_e, "reduce_early"), eoffs, [sshape(n) for n in e_names])))
    late = {"c_ctx": dc9[8], "norm1_w": g_nw1}
    packed_l, loffs = _pack_small([late[n].astype(f32) for n in late])
    G.update(zip(late, _unpack_small(_all_reduce_small(packed_l, "reduce_late"), loffs, [W[n].shape for n in late])))
    G["conv_w"] = lax.dynamic_slice(G["conv_w"], (0, me * per_cv), (3, per_cv))[None]
    G["b_mod"] = g_bmod.reshape(b_mod.shape)
    G["w_mod"] = gw_mod[None]
    G["w_in"] = _sum8(l_in, "sum_dw_in").T[None]
    G["w_up"] = jnp.concatenate([_sum8(l_up1, "sum_dw_up1"), _sum8(l_up2, "sum_dw_up2")], axis=0).T[None]
    G["w_out"] = _sum8(l_out, "sum_dw_out")[None]
    G["w_down"] = _sum8(l_down, "sum_dw_down")[None]
    G["s5_w_glu"] = _sum8(l_glu, "sum_dw_glu")[None]

    delta, new_m, new_v = {}, {}, {}
    sm_names = SMALL[1:] + ["b_mod"]
    rows = lambda a: a.reshape(-1, a.shape[-1])
    outs = _adamw_many(*[[rows(d[n]) for n in sm_names] for d in (W, G, M, V)], "adamw_small")
    for dst, src in zip((delta, new_m, new_v), outs):
        dst.update({n: a.reshape(W[n].shape) for n, a in zip(sm_names, src)})
    for n in ["w_mod", "w_in", "w_out", "w_up", "w_down", "s5_w_glu", "conv_w"]:
        d, nm, nv = _adamw(W[n][0], G[n][0], M[n][0], V[n][0], "adamw_" + n)
        delta[n], new_m[n], new_v[n] = d[None], nm[None], nv[None]

    loss = lax.psum(loss_acc[0, 0], ("x", "y", "c"))
    return (loss, grad_x[None], *[G[n] for n in WEIGHTS], *[delta[n] for n in WEIGHTS], *[new_m[n] for n in WEIGHTS],
            *[new_v[n] for n in WEIGHTS])
```

```python
import functools

import numpy as np
import jax
import jax.numpy as jnp
from jax import lax
from jax.experimental import pallas as pl
from jax.experimental.pallas import tpu as pltpu

f32, bf16 = jnp.float32, jnp.bfloat16

D = 1024
S5W, S5G, S5P, S5N = 512, 32, 16, 64
TC = 16
TCP = TC * S5P
SB = 2 * S5N
GBK = 8
CARRY_UNROLL = 8
RH, DH = 4, 128
RW = RH * DH
INC = S5W + 4 * RW
DFF = 2816
T = 128
R = 256
RF = 128
HALO = 8
EPS = 1e-6
ROPE_THETA = 10000.0
GRID_W = 64
NDEV = 8
LR, B1, B2, AEPS, WD, STEP = 0.001, 0.9, 0.999, 1e-08, 0.01, 10
VMEM_LIMIT = 60 * 1024 * 1024
ACC_TILE_BYTES = 6 * 1024 * 1024
MESH = pl.DeviceIdType.MESH

_CP = functools.partial(pltpu.CompilerParams, vmem_limit_bytes=VMEM_LIMIT)
_ARB = ("arbitrary",)
_ANY = pl.BlockSpec(memory_space=pl.ANY)


def _dg(a, b, dims):
    return lax.dot_general(a.astype(bf16), b.astype(bf16), (dims, ((), ())), preferred_element_type=f32)


@jax.custom_vjp
def dnn(a, b):
    return _dg(a, b, ((1,), (0,)))


@jax.custom_vjp
def dnt(a, b):
    return _dg(a, b, ((1,), (1,)))


@jax.custom_vjp
def dtn(a, b):
    return _dg(a, b, ((0,), (0,)))


dnn.defvjp(lambda a, b: (dnn(a, b), (a, b)), lambda r, g: (dnt(g, r[1]).astype(r[0].dtype), dtn(r[0], g).astype(r[1].dtype)))
dnt.defvjp(lambda a, b: (dnt(a, b), (a, b)), lambda r, g: (dnn(g, r[1]).astype(r[0].dtype), dtn(g, r[0]).astype(r[1].dtype)))
dtn.defvjp(lambda a, b: (dtn(a, b), (a, b)), lambda r, g: (dnt(r[1], g).astype(r[0].dtype), dnn(r[0], g).astype(r[1].dtype)))


@jax.custom_vjp
def _dnn_const(a, w, wt):
    return dnn(a, w)


_dnn_const.defvjp(lambda a, w, wt: (dnn(a, w), wt), lambda wt, g: (dnn(g, wt), None, None))


def _rms(t, w):
    return t * lax.rsqrt(jnp.mean(t * t, axis=-1, keepdims=True) + EPS) * w


def _mod(h, shift, scale):
    return h * (1.0 + scale) + shift


def _const_spec(shape):
    n = len(shape)
    return pl.BlockSpec(shape, lambda i, _n=n: (0,) * _n, pipeline_mode=pl.Buffered(1))


def _acc_spec(shape):
    n = len(shape)
    return pl.BlockSpec(shape, lambda i, _n=n: (0,) * _n)


def _me():
    return 4 * lax.axis_index("x") + 2 * lax.axis_index("y") + lax.axis_index("c")


def _peer(r):
    x, y, c = lax.axis_index("x"), lax.axis_index("y"), lax.axis_index("c")
    px = 1 - x if (r >> 2) & 1 else x
    py = 1 - y if (r >> 1) & 1 else y
    pc = 1 - c if r & 1 else c
    return (px, py, pc), 4 * px + 2 * py + pc


def _all_gather_small(v, name):
    r, c = v.shape

    def body(v_ref, out_ref, send_sems, recv_sems):
        me = _me()
        out_ref[me] = v_ref[...]
        sends = []
        for k in range(1, NDEV):
            peer, _ = _peer(k)
            cp = pltpu.make_async_remote_copy(src_ref=v_ref, dst_ref=out_ref.at[me], send_sem=send_sems.at[k - 1],
                                              recv_sem=recv_sems.at[k - 1], device_id=peer, device_id_type=MESH)
            cp.start()
            sends.append(cp)
        for k in range(1, NDEV):
            peer, pidx = _peer(k)
            pltpu.make_async_remote_copy(src_ref=v_ref, dst_ref=out_ref.at[pidx], send_sem=send_sems.at[k - 1],
                                         recv_sem=recv_sems.at[k - 1], device_id=peer, device_id_type=MESH).wait_recv()
        for cp in sends:
            cp.wait_send()

    return pl.pallas_call(
        body, name=name, out_shape=jax.ShapeDtypeStruct((NDEV, r, c), v.dtype),
        in_specs=[pl.BlockSpec(memory_space=pltpu.VMEM)], out_specs=pl.BlockSpec(memory_space=pltpu.VMEM),
        scratch_shapes=[pltpu.SemaphoreType.DMA((NDEV - 1,)), pltpu.SemaphoreType.DMA((NDEV - 1,))],
        compiler_params=_CP(),
    )(v)


def _all_reduce_small(v, name):
    r, c = v.shape

    def body(v_ref, out_ref, land, send_sems, recv_sems):
        me = _me()
        land[me] = v_ref[...]
        sends = []
        for k in range(1, NDEV):
            peer, _ = _peer(k)
            cp = pltpu.make_async_remote_copy(src_ref=v_ref, dst_ref=land.at[me], send_sem=send_sems.at[k - 1],
                                              recv_sem=recv_sems.at[k - 1], device_id=peer, device_id_type=MESH)
            cp.start()
            sends.append(cp)
        for k in range(1, NDEV):
            peer, pidx = _peer(k)
            pltpu.make_async_remote_copy(src_ref=v_ref, dst_ref=land.at[pidx], send_sem=send_sems.at[k - 1],
                                         recv_sem=recv_sems.at[k - 1], device_id=peer, device_id_type=MESH).wait_recv()
        for cp in sends:
            cp.wait_send()
        acc = land[0]
        for j in range(1, NDEV):
            acc = acc + land[j]
        out_ref[...] = acc

    return pl.pallas_call(
        body, name=name, out_shape=jax.ShapeDtypeStruct((r, c), v.dtype),
        in_specs=[pl.BlockSpec(memory_space=pltpu.VMEM)], out_specs=pl.BlockSpec(memory_space=pltpu.VMEM),
        scratch_shapes=[pltpu.VMEM((NDEV, r, c), v.dtype), pltpu.SemaphoreType.DMA((NDEV - 1,)),
                        pltpu.SemaphoreType.DMA((NDEV - 1,))],
        compiler_params=_CP(),
    )(v)


class _Exchange:
    def __init__(self, srcs, dsts, send_sems, recv_sems, local_sems, scatter):
        me = _me()
        n = len(srcs)
        self.sends, self.recvs, self.locals = [], [], []
        for a, (s, d) in enumerate(zip(srcs, dsts)):
            self.locals.append(pltpu.make_async_copy(s.at[me] if scatter else s, d.at[me], local_sems.at[a]))
        for k in range(1, NDEV):
            peer, pidx = _peer(k)
            for a, (s, d) in enumerate(zip(srcs, dsts)):
                src = s.at[pidx] if scatter else s
                sem = (k - 1) * n + a
                for dst, out in ((d.at[me], self.sends), (d.at[pidx], self.recvs)):
                    out.append(pltpu.make_async_remote_copy(src_ref=src, dst_ref=dst, send_sem=send_sems.at[sem],
                                                            recv_sem=recv_sems.at[sem], device_id=peer, device_id_type=MESH))

    def start(self):
        for cp in self.locals + self.sends:
            cp.start()

    def wait(self):
        for cp in self.recvs:
            cp.wait_recv()
        for cp in self.sends:
            cp.wait_send()
        for cp in self.locals:
            cp.wait()


def _exchange_shapes(arrays, scatter):
    return [jax.ShapeDtypeStruct(a.shape if scatter else (NDEV,) + a.shape, a.dtype) for a in arrays]


def _exchange_sems(n):
    return [pltpu.SemaphoreType.DMA(((NDEV - 1) * n,)), pltpu.SemaphoreType.DMA(((NDEV - 1) * n,)), pltpu.SemaphoreType.DMA((n,))]


def _exchange(arrays, scatter, name):
    n = len(arrays)

    def body(*refs):
        ex = _Exchange(refs[:n], refs[n:2 * n], *refs[2 * n:], scatter)
        ex.start()
        ex.wait()

    return pl.pallas_call(body, name=name, out_shape=_exchange_shapes(arrays, scatter), in_specs=[_ANY] * n,
                          out_specs=[_ANY] * n, scratch_shapes=_exchange_sems(n), compiler_params=_CP())(*arrays)


class _Cargo:
    def __init__(self, cargo):
        self.arrays, self.scatter = cargo if cargo else ([], False)
        self.n = len(self.arrays)

    def in_specs(self):
        return [_ANY] * self.n

    def out_shapes(self):
        return _exchange_shapes(self.arrays, self.scatter)

    def sems(self):
        return _exchange_sems(self.n) if self.n else []

    def split(self, refs, n_in, n_out, n_scratch):
        n = self.n
        return refs[:n_in], refs[n_in + n:n_in + n + n_out], refs[n_in + 2 * n + n_out:n_in + 2 * n + n_out + n_scratch]

    def ride(self, refs, n_in, n_out, grid):
        if not self.n:
            return
        n = self.n
        ex = _Exchange(refs[n_in:n_in + n], refs[n_in + n + n_out:n_in + 2 * n + n_out], *refs[-3:], self.scatter)
        grid = (grid,) if isinstance(grid, int) else tuple(grid)
        first = functools.reduce(jnp.logical_and, [pl.program_id(a) == 0 for a in range(len(grid))])
        last = functools.reduce(jnp.logical_and, [pl.program_id(a) == g - 1 for a, g in enumerate(grid)])

        @pl.when(first)
        def _():
            ex.start()

        @pl.when(last)
        def _():
            ex.wait()


def _sum8(land, name):
    _, r, c = land.shape
    rb = next((b for b in (256, 64, 32) if r % b == 0), r)

    def body(l_ref, o_ref):
        acc = l_ref[0].astype(f32)
        for j in range(1, NDEV):
            acc = acc + l_ref[j].astype(f32)
        o_ref[...] = acc

    return pl.pallas_call(
        body, name=name, grid=(r // rb,), out_shape=jax.ShapeDtypeStruct((r, c), f32),
        in_specs=[pl.BlockSpec((NDEV, rb, c), lambda i: (0, i, 0))], out_specs=pl.BlockSpec((rb, c), lambda i: (i, 0)),
        compiler_params=_CP(dimension_semantics=("parallel",)),
    )(land)


def _ada_fwd(c9, w_mod_l, name):
    def body(c_ref, w_ref, o_ref):
        o_ref[...] = dnn(jax.nn.silu(c_ref[...]), w_ref[...])

    return pl.pallas_call(body, name=name, out_shape=jax.ShapeDtypeStruct((16, w_mod_l.shape[1]), f32),
                          compiler_params=_CP())(c9, w_mod_l)


def _mod_select(m_all, b_mod6, name):
    def body(m_ref, b_ref, mx_ref, mc_ref):
        me = _me()
        mx_ref[...] = m_ref[me] + b_ref[...]
        mc_ref[...] = m_ref[8] + b_ref[...]

    return pl.pallas_call(body, name=name, out_shape=[jax.ShapeDtypeStruct((6, D), f32)] * 2, compiler_params=_CP())(m_all, b_mod6)


def _ada_bwd(c9, dmx_all, dmc_all, dmx_l, dmc_l, w_mod_l, name):
    ncol = w_mod_l.shape[1]

    def rowsum(r):
        acc = r[0:1]
        for j in range(1, NDEV):
            acc = acc + r[j:j + 1]
        return acc

    def body(c_ref, xa_ref, ca_ref, xl_ref, cl_ref, w_ref, gw_ref, gb_ref, dc_ref):
        s9, vjp = jax.vjp(jax.nn.silu, c_ref[...])
        dm9 = jnp.concatenate([xl_ref[...], rowsum(cl_ref[...]), jnp.zeros((7, ncol), f32)], axis=0)
        gw_ref[...] = dtn(s9, dm9)
        gb_ref[...] = rowsum(xa_ref[...]) + rowsum(ca_ref[...])
        dc_ref[...] = vjp(dnt(dm9, w_ref[...]))[0]

    return pl.pallas_call(
        body, name=name,
        out_shape=[jax.ShapeDtypeStruct((D, ncol), f32), jax.ShapeDtypeStruct((1, 6 * D), f32), jax.ShapeDtypeStruct((16, D), f32)],
        compiler_params=_CP())(c9, dmx_all, dmc_all, dmx_l, dmc_l, w_mod_l)


def _lane_sign(rank):
    shape = (1,) * (rank - 1) + (SB,)
    return jnp.where(lax.broadcasted_iota(jnp.int32, shape, rank - 1) < S5N, -1.0, 1.0)


def _s5_build_fn(lre2, lim2, ls, bn, bs, cn, cs, rev):
    sg = _lane_sign(3)
    s = jnp.exp(ls)
    ar, ai = lre2 * s, lim2 * s
    e = jnp.exp(ar)
    nr, ni = e * jnp.cos(ai) - 1.0, e * jnp.sin(ai)
    den = lre2 * lre2 + lim2 * lim2
    cr, ci = (nr * lre2 + ni * lim2) / den, (ni * lre2 - nr * lim2) / den
    bbn = cr * bn + (ci * sg) * bs
    bbs = cr * bs - (ci * sg) * bn

    def powers(ex):
        m, ang = jnp.exp(ex * ar), ex * ai
        return m * jnp.cos(ang), m * jnp.sin(ang) * sg

    def times(tabs, xn, xs):
        f1, f2 = tabs
        return f1[:, :, None, :] * xn[:, None, :, :] + f2[:, :, None, :] * xs[:, None, :, :]

    t = lax.broadcasted_iota(jnp.int32, (1, TC, 1), 1).astype(f32)
    if rev:
        e_src, e_dst, e_out, e_in = t - (TC - 1.0), (TC - 1.0) - t, t, TC - t
    else:
        e_src, e_dst, e_out, e_in = -t, t, (TC - 1.0) - t, t + 1.0
    g = lre2.shape[0]
    flat = lambda a: a.reshape(g, TCP, SB)
    conj = -_lane_sign(4)
    ll = flat(times(powers(e_src), bbn, bbs))
    rr = flat(times(powers(e_dst), cn, cs) * conj)
    mb = flat(times(powers(e_out), bbn, bbs))
    mct = flat(times(powers(e_in), cn, cs) * conj)
    a1, a2 = powers(float(TC))
    row = lax.broadcasted_iota(jnp.int32, (TCP, TCP), 0) // S5P
    col = lax.broadcasted_iota(jnp.int32, (TCP, TCP), 1) // S5P
    mask = jnp.where((col <= row) if rev else (col >= row), 1.0, 0.0)
    m = jnp.concatenate([dnt(ll[j], rr[j])[None] for j in range(g)], axis=0) * mask
    return m, mb, mct, a1, a2


def _gspec(*tail):
    nt = len(tail)
    return pl.BlockSpec((GBK,) + tail, lambda i, _n=nt: (i,) + (0,) * _n)


def _s5_build(params, rev, name):
    def body(l1, l2, ls, bn, bs, cn, cs, m_ref, mb_ref, mc_ref, a1_ref, a2_ref):
        m, mb, mct, a1, a2 = _s5_build_fn(l1[...], l2[...], ls[...], bn[...], bs[...], cn[...], cs[...], rev)
        m_ref[...], mb_ref[...], mc_ref[...] = m.astype(bf16), mb.astype(bf16), mct.astype(bf16)
        a1_ref[...], a2_ref[...] = a1, a2

    vec, pm = _gspec(1, SB), _gspec(S5P, SB)
    return pl.pallas_call(
        body, name=name, grid=(S5G // GBK,),
        in_specs=[vec, vec, _gspec(1, 1), pm, pm, pm, pm],
        out_specs=[_gspec(TCP, TCP), _gspec(TCP, SB), _gspec(TCP, SB), vec, vec],
        out_shape=[jax.ShapeDtypeStruct((S5G, TCP, TCP), bf16), jax.ShapeDtypeStruct((S5G, TCP, SB), bf16),
                   jax.ShapeDtypeStruct((S5G, TCP, SB), bf16), jax.ShapeDtypeStruct((S5G, 1, SB), f32),
                   jax.ShapeDtypeStruct((S5G, 1, SB), f32)],
        compiler_params=_CP(dimension_semantics=("parallel",)),
    )(*params)


def _s5_build_bwd(params, cots, prev, rev, name):
    def body(l1, l2, ls, bn, bs, cn, cs, dm, dmb, dmc, da1, da2, pb, pc, gl1, gl2, gls, gb, gc):
        prim = (l1[...], l2[...], ls[...], bn[...], bs[...], cn[...], cs[...])
        _, vjp = jax.vjp(functools.partial(_s5_build_fn, rev=rev), *prim)
        d1, d2, dls, dbn, dbs, dcn, dcs = vjp((dm[...], dmb[...], dmc[...], da1[...], da2[...]))
        gl1[...] = d1 + pltpu.roll(d1, S5N, axis=2)
        gl2[...] = d2 + pltpu.roll(d2, S5N, axis=2)
        gls[...] = dls
        gb[...] = dbn + pltpu.roll(dbs, S5N, axis=2) + pb[...]
        gc[...] = dcn + pltpu.roll(dcs, S5N, axis=2) + pc[...]

    vec, pm, big = _gspec(1, SB), _gspec(S5P, SB), _gspec(TCP, SB)
    return pl.pallas_call(
        body, name=name, grid=(S5G // GBK,),
        in_specs=[vec, vec, _gspec(1, 1), pm, pm, pm, pm, _gspec(TCP, TCP), big, big, vec, vec, pm, pm],
        out_specs=[vec, vec, _gspec(1, 1), pm, pm],
        out_shape=[jax.ShapeDtypeStruct((S5G, 1, SB), f32), jax.ShapeDtypeStruct((S5G, 1, SB), f32),
                   jax.ShapeDtypeStruct((S5G, 1, 1), f32), jax.ShapeDtypeStruct((S5G, S5P, SB), f32),
                   jax.ShapeDtypeStruct((S5G, S5P, SB), f32)],
        compiler_params=_CP(dimension_semantics=("parallel",)),
    )(*params, *cots, *prev)


def _s5_inc(u, mb_f, mb_b, name):
    nc = u.shape[1]

    def body(u_ref, mf_ref, mb_ref, sf_ref, sb_ref):
        for j in range(GBK):
            sf_ref[:, j, :] = jnp.dot(u_ref[j], mf_ref[j], preferred_element_type=f32)
            sb_ref[:, j, :] = jnp.dot(u_ref[j], mb_ref[j], preferred_element_type=f32)

    sspec = pl.BlockSpec((nc, GBK, SB), lambda i: (0, i, 0))
    return pl.pallas_call(
        body, name=name, grid=(S5G // GBK,), in_specs=[_gspec(nc, TCP), _gspec(TCP, SB), _gspec(TCP, SB)],
        out_specs=[sspec, sspec], out_shape=[jax.ShapeDtypeStruct((nc, S5G, SB), f32)] * 2,
        compiler_params=_CP(dimension_semantics=("parallel",)),
    )(u, mb_f, mb_b)


def _idx_fwd(nctx, nch):
    return lambda i: i


def _idx_rev(nctx, nch):
    return lambda i: jnp.where(i < nctx, nctx - 1 - i, nch + nctx - 1 - i)


def _carry_loop(nc, step, init):
    def trip(i, c):
        for k in range(CARRY_UNROLL):
            c = step(i * CARRY_UNROLL + k, c)
        return c

    return lax.fori_loop(0, nc // CARRY_UNROLL, trip, init)


def _s5_carry(s_f, s_b, a_f, a_b, nctx, name):
    nc = s_f.shape[0]
    idx_b = _idx_rev(nctx, nc)

    def body(sf_ref, sb_ref, f1_ref, f2_ref, b1_ref, b2_ref, hf_ref, hb_ref):
        f1, f2, b1, b2 = f1_ref[...], f2_ref[...], b1_ref[...], b2_ref[...]

        def step(i, c):
            hf, hfs, hb, hbs = c
            rb = idx_b(i)
            hf_ref[i] = hf
            hb_ref[rb] = hb
            sf, sb = sf_ref[i], sb_ref[rb]
            return (f1 * hf + f2 * hfs + sf, f1 * hfs - f2 * hf + pltpu.roll(sf, S5N, axis=1),
                    b1 * hb + b2 * hbs + sb, b1 * hbs - b2 * hb + pltpu.roll(sb, S5N, axis=1))

        z = jnp.zeros((S5G, SB), f32)
        _carry_loop(nc, step, (z, z, z, z))

    return pl.pallas_call(body, name=name, out_shape=[jax.ShapeDtypeStruct(s_f.shape, f32)] * 2,
                          compiler_params=_CP())(s_f, s_b, *a_f, *a_b)


def _s5_carry_bwd(dhp, hp, a1, a2, rev, nctx, name):
    nc = hp.shape[0]
    idx = (_idx_rev if rev else _idx_fwd)(nctx, nc)

    def body(dhp_ref, hp_ref, a1_ref, a2_ref, ds_ref, d1_ref, d2_ref):
        f1, f2 = a1_ref[...], a2_ref[...]

        def step(k, carry):
            ab, abs_, d1, d2 = carry
            r = idx(nc - 1 - k)
            ds_ref[r] = ab
            h, dh = hp_ref[r], dhp_ref[r]
            return (dh + f1 * ab - f2 * abs_, pltpu.roll(dh, S5N, axis=1) + f1 * abs_ + f2 * ab,
                    d1 + ab * h, d2 + ab * pltpu.roll(h, S5N, axis=1))

        z = jnp.zeros((S5G, SB), f32)
        _, _, d1, d2 = _carry_loop(nc, step, (z, z, z, z))
        d1_ref[...], d2_ref[...] = d1, d2

    return pl.pallas_call(
        body, name=name,
        out_shape=[jax.ShapeDtypeStruct(hp.shape, f32), jax.ShapeDtypeStruct((S5G, SB), f32), jax.ShapeDtypeStruct((S5G, SB), f32)],
        compiler_params=_CP())(dhp, hp, a1, a2)


def _s5_out(u, m_f, m_b, hp_f, hp_b, mc_f, mc_b, name):
    nc = u.shape[1]

    def body(u_ref, mf_ref, mb_ref, hf_ref, hb_ref, cf_ref, cb_ref, y_ref):
        for j in range(GBK):
            uj = u_ref[j]
            y_ref[j] = (jnp.dot(uj, mf_ref[j], preferred_element_type=f32) + jnp.dot(uj, mb_ref[j], preferred_element_type=f32)
                        + dnt(hf_ref[:, j, :], cf_ref[j]) + dnt(hb_ref[:, j, :], cb_ref[j])).astype(bf16)

    sspec = pl.BlockSpec((nc, GBK, SB), lambda i: (0, i, 0))
    return pl.pallas_call(
        body, name=name, grid=(S5G // GBK,),
        in_specs=[_gspec(nc, TCP), _gspec(TCP, TCP), _gspec(TCP, TCP), sspec, sspec, _gspec(TCP, SB), _gspec(TCP, SB)],
        out_specs=_gspec(nc, TCP), out_shape=jax.ShapeDtypeStruct((S5G, nc, TCP), bf16),
        compiler_params=_CP(dimension_semantics=("parallel",)),
    )(u, m_f, m_b, hp_f, hp_b, mc_f, mc_b)


def _s5_out_bwd(dy, u, m_f, m_b, hp_f, hp_b, mc_f, mc_b, name):
    nc = u.shape[1]

    def body(dy_ref, u_ref, mf_ref, mb_ref, hf_ref, hb_ref, cf_ref, cb_ref, du_ref, g_ref, dhf_ref, dhb_ref, dcf_ref, dcb_ref):
        for j in range(GBK):
            dyj = dy_ref[j]
            du_ref[j] = dnt(dyj, mf_ref[j]) + dnt(dyj, mb_ref[j])
            g_ref[j] = dtn(u_ref[j], dyj)
            dhf_ref[:, j, :] = dnn(dyj, cf_ref[j])
            dhb_ref[:, j, :] = dnn(dyj, cb_ref[j])
            dcf_ref[j] = dtn(dyj, hf_ref[:, j, :])
            dcb_ref[j] = dtn(dyj, hb_ref[:, j, :])

    sspec = pl.BlockSpec((nc, GBK, SB), lambda i: (0, i, 0))
    sshape = jax.ShapeDtypeStruct((nc, S5G, SB), f32)
    cshape = jax.ShapeDtypeStruct((S5G, TCP, SB), f32)
    return pl.pallas_call(
        body, name=name, grid=(S5G // GBK,),
        in_specs=[_gspec(nc, TCP), _gspec(nc, TCP), _gspec(TCP, TCP), _gspec(TCP, TCP), sspec, sspec, _gspec(TCP, SB), _gspec(TCP, SB)],
        out_specs=[_gspec(nc, TCP), _gspec(TCP, TCP), sspec, sspec, _gspec(TCP, SB), _gspec(TCP, SB)],
        out_shape=[jax.ShapeDtypeStruct((S5G, nc, TCP), f32), jax.ShapeDtypeStruct((S5G, TCP, TCP), f32), sshape, sshape, cshape, cshape],
        compiler_params=_CP(dimension_semantics=("parallel",)),
    )(dy, u, m_f, m_b, hp_f, hp_b, mc_f, mc_b)


def _s5_inc_bwd(du1, u, ds_f, ds_b, mb_f, mb_b, name):
    nc = u.shape[1]

    def body(du1_ref, u_ref, dsf_ref, dsb_ref, mf_ref, mb_ref, du_ref, dmf_ref, dmb_ref):
        for j in range(GBK):
            dsf, dsb = dsf_ref[:, j, :], dsb_ref[:, j, :]
            du_ref[j] = (du1_ref[j] + dnt(dsf, mf_ref[j]) + dnt(dsb, mb_ref[j])).astype(bf16)
            dmf_ref[j] = dtn(u_ref[j], dsf)
            dmb_ref[j] = dtn(u_ref[j], dsb)

    sspec = pl.BlockSpec((nc, GBK, SB), lambda i: (0, i, 0))
    cshape = jax.ShapeDtypeStruct((S5G, TCP, SB), f32)
    return pl.pallas_call(
        body, name=name, grid=(S5G // GBK,),
        in_specs=[_gspec(nc, TCP), _gspec(nc, TCP), sspec, sspec, _gspec(TCP, SB), _gspec(TCP, SB)],
        out_specs=[_gspec(nc, TCP), _gspec(TCP, SB), _gspec(TCP, SB)],
        out_shape=[jax.ShapeDtypeStruct((S5G, nc, TCP), bf16), cshape, cshape],
        compiler_params=_CP(dimension_semantics=("parallel",)),
    )(du1, u, ds_f, ds_b, mb_f, mb_b)


def _to_groups(a):
    n = a.shape[0]
    return a.reshape(n // TC, TC, S5G, S5P).transpose(2, 0, 1, 3).reshape(S5G, n // TC, TCP)


def _from_groups(a):
    nc = a.shape[1]
    return a.reshape(S5G, nc, TC, S5P).transpose(1, 2, 0, 3).reshape(nc * TC, S5W)


def _swap_pairs(t):
    lane = lax.broadcasted_iota(jnp.int32, t.shape, 1)
    return jnp.where(lane % 2 == 0, pltpu.roll(t, DH - 1, axis=1), pltpu.roll(t, 1, axis=1))


def _rot(t, cosf, sins):
    return t * cosf + _swap_pairs(t) * sins


def _rot_t(d, cosf, sins):
    return d * cosf - _swap_pairs(d) * sins


def _ret_chunk(qr, kr, v, rp, ld, rev):
    pos = lax.broadcasted_iota(jnp.int32, (T, 1), 0).astype(f32)
    diff = pos - lax.broadcasted_iota(jnp.int32, (1, T), 1).astype(f32)
    if rev:
        keep, dist = diff < 0, jnp.maximum(-diff, 0.0)
        xi, zeta = jnp.exp(ld * (T - pos)), jnp.exp(ld * pos)
    else:
        keep, dist = diff >= 0, jnp.maximum(diff, 0.0)
        xi, zeta = jnp.exp(ld * (pos + 1.0)), jnp.exp(ld * (T - 1.0 - pos))
    dm = jnp.where(keep, jnp.exp(ld * dist), 0.0)
    out = dnn(dnt(qr, kr) * dm, v) + dnn(qr * xi, rp)
    rn = jnp.exp(ld * float(T)) * rp + dtn(kr * zeta, v)
    return out, rn


def _ret_fwd(p_ext, ld8, rev, nctx, name, cargo=None):
    n = p_ext.shape[0]
    nch = n // T
    idx = (_idx_rev if rev else _idx_fwd)(nctx, nch)
    cg = _Cargo(cargo)

    def body(*refs):
        (q_ref, k_ref, v_ref, ld_ref), (o_ref, rp_ref), (r_s,) = cg.split(refs, 4, 2, 1)
        cg.ride(refs, 4, 2, nch)

        @pl.when(pl.program_id(0) == 0)
        def _():
            r_s[...] = jnp.zeros_like(r_s)

        for h in range(RH):
            sl = slice(h * DH, (h + 1) * DH)
            rp = r_s[h]
            rp_ref[0, h] = rp
            out, rn = _ret_chunk(q_ref[:, sl].astype(f32), k_ref[:, sl].astype(f32), v_ref[:, sl].astype(f32), rp,
                                 ld_ref[h:h + 1, 0:1], rev)
            r_s[h] = rn
            o_ref[:, sl] = out

    def colspec(cb):
        return pl.BlockSpec((T, RW), lambda i, _c=cb: (idx(i), _c))

    return pl.pallas_call(
        body, name=name, grid=(nch,),
        in_specs=[colspec(1), colspec(2), colspec(3), _const_spec((8, 128))] + cg.in_specs(),
        out_specs=[pl.BlockSpec((T, RW), lambda i: (idx(i), 0)), pl.BlockSpec((1, RH, DH, DH), lambda i: (i, 0, 0, 0))] + cg.in_specs(),
        out_shape=[jax.ShapeDtypeStruct((n, RW), f32), jax.ShapeDtypeStruct((nch, RH, DH, DH), f32)] + cg.out_shapes(),
        scratch_shapes=[pltpu.VMEM((RH, DH, DH), f32)] + cg.sems(),
        compiler_params=_CP(dimension_semantics=_ARB),
    )(p_ext, p_ext, p_ext, ld8, *cg.arrays)


def _ret_bwd(p_ext, ld8, rprev, do_ext, rev, nctx, name, cargo=None):
    n = p_ext.shape[0]
    nch = n // T
    idx0 = (_idx_rev if rev else _idx_fwd)(nctx, nch)
    idx = lambda j: idx0(nch - 1 - j)
    cg = _Cargo(cargo)

    def body(*refs):
        ins, (dq_ref, dk_ref, dv_ref, dld_ref), (dr_s,) = cg.split(refs, 6, 4, 1)
        q_ref, k_ref, v_ref, ld_ref, rp_ref, do_ref = ins
        cg.ride(refs, 6, 4, nch)

        @pl.when(pl.program_id(0) == 0)
        def _():
            dr_s[...] = jnp.zeros_like(dr_s)
            dld_ref[...] = jnp.zeros_like(dld_ref)

        for h in range(RH):
            sl = slice(h * DH, (h + 1) * DH)
            _, vjp = jax.vjp(functools.partial(_ret_chunk, rev=rev), q_ref[:, sl].astype(f32), k_ref[:, sl].astype(f32),
                             v_ref[:, sl].astype(f32), rp_ref[0, h], ld_ref[h:h + 1, 0:1])
            dqr, dkr, dv, drp, dld = vjp((do_ref[:, sl], dr_s[h]))
            dr_s[h] = drp
            dq_ref[:, sl], dk_ref[:, sl], dv_ref[:, sl] = dqr, dkr, dv
            dld_ref[h:h + 1, :] += jnp.broadcast_to(dld, (1, 128))

    def colspec(cb):
        return pl.BlockSpec((T, RW), lambda j, _c=cb: (idx(j), _c))

    ospec = pl.BlockSpec((T, RW), lambda j: (idx(j), 0))
    oshape = jax.ShapeDtypeStruct((n, RW), f32)
    return pl.pallas_call(
        body, name=name, grid=(nch,),
        in_specs=[colspec(1), colspec(2), colspec(3), _const_spec((8, 128)),
                  pl.BlockSpec((1, RH, DH, DH), lambda j: (nch - 1 - j, 0, 0, 0)), ospec] + cg.in_specs(),
        out_specs=[ospec, ospec, ospec, _acc_spec((8, 128))] + cg.in_specs(),
        out_shape=[oshape, oshape, oshape, jax.ShapeDtypeStruct((8, 128), f32)] + cg.out_shapes(),
        scratch_shapes=[pltpu.VMEM((RH, DH, DH), f32)] + cg.sems(),
        compiler_params=_CP(dimension_semantics=_ARB),
    )(p_ext, p_ext, p_ext, ld8, rprev, do_ext, *cg.arrays)


def _qk_heads(p, fn_q, fn_k):
    heads = lambda base, fn: [fn(p[:, base + h * DH:base + (h + 1) * DH]) for h in range(RH)]
    return jnp.concatenate([p[:, :S5W]] + heads(S5W, fn_q) + heads(S5W + RW, fn_k) + [p[:, S5W + 2 * RW:]], axis=1)


def _f1_fwd(x, ctx, modx, modc, nw1, w_in_n, cosf, sins, name, cargo=None):
    L = x.shape[0]
    nb = L // R + 1
    scale = DH ** -0.5
    cg = _Cargo(cargo)

    def body(*refs):
        (x_ref, c_ref, mx_ref, mc_ref, nw_ref, w_ref, cos_ref, sin_ref), (p_ref,), _ = cg.split(refs, 8, 1, 0)
        cg.ride(refs, 8, 1, nb)
        is_ctx = pl.program_id(0) == 0
        xin = jnp.where(is_ctx, c_ref[...], x_ref[...])
        sh = jnp.where(is_ctx, mc_ref[0:1], mx_ref[0:1])
        sc = jnp.where(is_ctx, mc_ref[1:2], mx_ref[1:2])
        cf, ss = cos_ref[...], sin_ref[...]
        p = dnn(_mod(_rms(xin, nw_ref[...]), sh, sc), w_ref[...])
        p_ref[...] = _qk_heads(p, lambda t: _rot(t, cf, ss), lambda t: _rot(t * scale, cf, ss)).astype(bf16)

    return pl.pallas_call(
        body, name=name, grid=(nb,),
        in_specs=[pl.BlockSpec((R, D), lambda i: (jnp.maximum(i - 1, 0), 0)), _const_spec((R, D)), _const_spec((6, D)),
                  _const_spec((6, D)), _const_spec((1, D)), _const_spec((D, INC)), pl.BlockSpec((R, DH), lambda i: (i, 0)),
                  pl.BlockSpec((R, DH), lambda i: (i, 0))] + cg.in_specs(),
        out_specs=[pl.BlockSpec((R, INC), lambda i: (i, 0))] + cg.in_specs(),
        out_shape=[jax.ShapeDtypeStruct((L + R, INC), bf16)] + cg.out_shapes(),
        scratch_shapes=cg.sems(),
        compiler_params=_CP(dimension_semantics=_ARB),
    )(x, ctx, modx, modc, nw1, w_in_n, cosf, sins, *cg.arrays)


def _f1_bwd(x, ctx, modx, modc, nw1, w_in_t, cosf, sins, dx1, parts, name, cargo=None):
    L = x.shape[0]
    nb = L // R + 1
    scale = DH ** -0.5
    cg = _Cargo(cargo)

    def body(*refs):
        ins, (gx_ref, dp_ref, h1_ref, dnw_ref, dmx_ref, dmc_ref), _ = cg.split(refs, 18, 6, 0)
        x_ref, c_ref, mx_ref, mc_ref, nw_ref, w_ref, cos_ref, sin_ref, dx1_ref, du0, du1, dq0, dq1, dk0, dk1, dv0, dv1, dg0 = ins
        cg.ride(refs, 18, 6, nb)
        i = pl.program_id(0)
        is_ctx = i == 0

        @pl.when(is_ctx)
        def _():
            dnw_ref[...] = jnp.zeros_like(dnw_ref)
            dmx_ref[...] = jnp.zeros_like(dmx_ref)
            dmc_ref[...] = jnp.zeros_like(dmc_ref)

        cf, ss = cos_ref[...], sin_ref[...]
        dp = jnp.concatenate([du0[...].astype(f32) + du1[...], dq0[...] + dq1[...], dk0[...] + dk1[...], dv0[...] + dv1[...],
                              dg0[...]], axis=1)
        dp = _qk_heads(dp, lambda t: _rot_t(t, cf, ss), lambda t: _rot_t(t, cf, ss) * scale).astype(bf16)
        dp_ref[...] = dp
        xin = jnp.where(is_ctx, c_ref[...], x_ref[...])
        sh = jnp.where(is_ctx, mc_ref[0:1], mx_ref[0:1])
        sc = jnp.where(is_ctx, mc_ref[1:2], mx_ref[1:2])
        dh = dnn(dp, w_ref[...])
        h, vjp = jax.vjp(lambda a, b, c, d: _mod(_rms(a, b), c, d), xin, nw_ref[...], sh, sc)
        dxin, dnw, dsh, dsc = vjp(dh)
        h1_ref[...] = h.astype(bf16)
        gx_ref[...] = dx1_ref[...] + dxin
        dnw_ref[...] += dnw
        wx = jnp.where(is_ctx, 0.0, 1.0)
        dmx_ref[0:1] += dsh * wx
        dmx_ref[1:2] += dsc * wx
        dmc_ref[0:1] += dsh * (1.0 - wx)
        dmc_ref[1:2] += dsc * (1.0 - wx)

    lat = pl.BlockSpec((R, D), lambda i: (jnp.maximum(i - 1, 0), 0))
    ext = pl.BlockSpec((R, S5W), lambda i: (i, 0))
    return pl.pallas_call(
        body, name=name, grid=(nb,),
        in_specs=[lat, _const_spec((R, D)), _const_spec((6, D)), _const_spec((6, D)), _const_spec((1, D)), _const_spec((INC, D)),
                  pl.BlockSpec((R, DH), lambda i: (i, 0)), pl.BlockSpec((R, DH), lambda i: (i, 0)), lat] + [ext] * 9 + cg.in_specs(),
        out_specs=[lat, pl.BlockSpec((R, INC), lambda i: (i, 0)), pl.BlockSpec((R, D), lambda i: (i, 0)),
                   _acc_spec((1, D)), _acc_spec((6, D)), _acc_spec((6, D))] + cg.in_specs(),
        out_shape=[jax.ShapeDtypeStruct((L, D), f32), jax.ShapeDtypeStruct((L + R, INC), bf16),
                   jax.ShapeDtypeStruct((L + R, D), bf16), jax.ShapeDtypeStruct((1, D), f32),
                   jax.ShapeDtypeStruct((6, D), f32), jax.ShapeDtypeStruct((6, D), f32)] + cg.out_shapes(),
        scratch_shapes=cg.sems(),
        compiler_params=_CP(dimension_semantics=_ARB),
    )(x, ctx, modx, modc, nw1, w_in_t, cosf, sins, dx1, *parts, *cg.arrays)


def _ret_post(yr, g):
    outs = []
    for h in range(RH):
        yh = yr[:, h * DH:(h + 1) * DH]
        mu = jnp.mean(yh, axis=-1, keepdims=True)
        var = jnp.mean((yh - mu) ** 2, axis=-1, keepdims=True)
        outs.append((yh - mu) * lax.rsqrt(var + EPS))
    return jax.nn.silu(g) * jnp.concatenate(outs, axis=1)


def _mix_fn(ys, u, of, ob, g, x, dvec, bglu, gate1, pz, pm, wglu, wout):
    s = jax.nn.gelu(ys + dvec * u)
    z = dnn(s, wglu) + bglu + pz
    cat = jnp.concatenate([s * jax.nn.sigmoid(z), _ret_post(of + ob, g)], axis=1)
    mix = dnn(cat, wout) + pm
    return x + gate1 * mix, (s, cat)


def _mix_fwd(x, ys, of, ob, p_ext, dvec, bglu, modx, wglu, wout, name, cargo=None):
    L = x.shape[0]
    nb = L // R
    cg = _Cargo(cargo)

    def body(*refs):
        ins, (x1_ref,), _ = cg.split(refs, 11, 1, 0)
        x_ref, ys_ref, of_ref, ob_ref, u_ref, g_ref, d_ref, b_ref, mx_ref, wg_ref, wo_ref = ins
        cg.ride(refs, 11, 1, nb)
        x1_ref[...] = _mix_fn(ys_ref[...].astype(f32), u_ref[...].astype(f32), of_ref[...], ob_ref[...], g_ref[...].astype(f32),
                              x_ref[...], d_ref[...], b_ref[...], mx_ref[2:3], 0.0, 0.0, wg_ref[...], wo_ref[...])[0]

    ext = pl.BlockSpec((R, S5W), lambda i: (i + 1, 0))
    return pl.pallas_call(
        body, name=name, grid=(nb,),
        in_specs=[pl.BlockSpec((R, D), lambda i: (i, 0)), ext, ext, ext, ext, pl.BlockSpec((R, RW), lambda i: (i + 1, 4)),
                  _const_spec((1, S5W)), _const_spec((1, S5W)), _const_spec((6, D)), _const_spec((S5W, S5W)), _const_spec((D, D))]
        + cg.in_specs(),
        out_specs=[pl.BlockSpec((R, D), lambda i: (i, 0))] + cg.in_specs(),
        out_shape=[jax.ShapeDtypeStruct((L, D), f32)] + cg.out_shapes(),
        scratch_shapes=cg.sems(),
        compiler_params=_CP(dimension_semantics=_ARB),
    )(x, ys, of, ob, p_ext, p_ext, dvec, bglu, modx, wglu, wout, *cg.arrays)


def _mix_bwd(x, ys, of, ob, p_ext, dvec, bglu, modx, wglu, wout, dx1, name, cargo=None):
    L = x.shape[0]
    nb = L // R + 1
    cg = _Cargo(cargo)

    def body(*refs):
        ins, outs, _ = cg.split(refs, 12, 11, 0)
        x_ref, ys_ref, of_ref, ob_ref, u_ref, g_ref, d_ref, b_ref, mx_ref, wg_ref, wo_ref, dx1_ref = ins
        dy_ref, dud_ref, do_ref, dg_ref, cat_ref, dmix_ref, s_ref, dz_ref, dd_ref, db_ref, dg1_ref = outs
        cg.ride(refs, 12, 11, nb)
        i = pl.program_id(0)

        @pl.when(i == 0)
        def _():
            for r in outs:
                r[...] = jnp.zeros_like(r)

        @pl.when(i > 0)
        def _():
            fn = lambda ys_, u_, of_, g_, d_, b_, g1_, pz_, pm_: _mix_fn(
                ys_, u_, of_, ob_ref[...], g_, x_ref[...], d_, b_, g1_, pz_, pm_, wg_ref[...], wo_ref[...])
            _, vjp, (s, cat) = jax.vjp(fn, ys_ref[...].astype(f32), u_ref[...].astype(f32), of_ref[...], g_ref[...].astype(f32), d_ref[...],
                                       b_ref[...], mx_ref[2:3], jnp.zeros((R, S5W), f32), jnp.zeros((R, D), f32), has_aux=True)
            dy, dud, do, dg, dd, db, dg1, dz, dmix = vjp(dx1_ref[...])
            dy_ref[...], dud_ref[...], do_ref[...], dg_ref[...] = dy.astype(bf16), dud, do, dg
            cat_ref[...], dmix_ref[...] = cat.astype(bf16), dmix.astype(bf16)
            s_ref[...], dz_ref[...] = s.astype(bf16), dz.astype(bf16)
            dd_ref[...] += dd
            db_ref[...] += db
            dg1_ref[...] += dg1

    lat = pl.BlockSpec((R, D), lambda i: (jnp.maximum(i - 1, 0), 0))
    lat5 = pl.BlockSpec((R, S5W), lambda i: (jnp.maximum(i - 1, 0), 0))
    ext = pl.BlockSpec((R, S5W), lambda i: (i, 0))
    eshape = jax.ShapeDtypeStruct((L + R, S5W), f32)
    return pl.pallas_call(
        body, name=name, grid=(nb,),
        in_specs=[lat, ext, ext, ext, ext, pl.BlockSpec((R, RW), lambda i: (i, 4)),
                  _const_spec((1, S5W)), _const_spec((1, S5W)), _const_spec((6, D)), _const_spec((S5W, S5W)), _const_spec((D, D)), lat]
        + cg.in_specs(),
        out_specs=[ext, ext, ext, ext, lat, lat, lat5, lat5, _acc_spec((1, S5W)), _acc_spec((1, S5W)), _acc_spec((1, D))]
        + cg.in_specs(),
        out_shape=[jax.ShapeDtypeStruct((L + R, S5W), bf16), eshape, eshape, eshape, jax.ShapeDtypeStruct((L, D), bf16),
                   jax.ShapeDtypeStruct((L, D), bf16), jax.ShapeDtypeStruct((L, S5W), bf16), jax.ShapeDtypeStruct((L, S5W), bf16),
                   jax.ShapeDtypeStruct((1, S5W), f32), jax.ShapeDtypeStruct((1, S5W), f32), jax.ShapeDtypeStruct((1, D), f32)]
        + cg.out_shapes(),
        scratch_shapes=cg.sems(),
        compiler_params=_CP(dimension_semantics=_ARB),
    )(x, ys, of, ob, p_ext, p_ext, dvec, bglu, modx, wglu, wout, dx1, *cg.arrays)


def _ffn_tail(gc, a, x1, gate2, fnw, pf, wdown, wdown_t, tgt):
    f = jax.nn.gelu(gc) * a
    ffn = _dnn_const(f, wdown, wdown_t) + pf
    y = _rms(x1 + gate2 * ffn, fnw)
    err = y - tgt
    loss = 0.5 * jnp.sum(jnp.mean(err * err, axis=-1, keepdims=True), axis=0, keepdims=True)
    return loss, f


def _ffn_fwd(x1, tgt, nw2, modx, w_a, w_g, cw, cb, wdown, wdown_t, fnw, name):
    L = x1.shape[0]
    nb = L // RF
    per = RF // HALO

    def body(x_ref, xp_ref, xn_ref, t_ref, nw_ref, mx_ref, wa_ref, wg_ref, cw_ref, cb_ref, wd_ref, wdt_ref, fn_ref,
             dx2_ref, da_ref, dgc_ref, f_ref, dffn_ref, loss_ref, dfn_ref, dg2_ref, dcb_ref, dcw_ref):
        i = pl.program_id(0)

        @pl.when(i == 0)
        def _():
            for r in (loss_ref, dfn_ref, dg2_ref, dcb_ref, dcw_ref):
                r[...] = jnp.zeros_like(r)

        nw, sh, sc, gate2 = nw_ref[...], mx_ref[3:4], mx_ref[4:5], mx_ref[5:6]
        x1b = x_ref[...]
        h2 = _mod(_rms(x1b, nw), sh, sc)
        h2e = jnp.concatenate([_mod(_rms(xp_ref[...], nw), sh, sc), h2, _mod(_rms(xn_ref[...], nw), sh, sc)], axis=0)
        a = dnn(h2, wa_ref[...])
        ge = dnn(h2e, wg_ref[...])
        g = ge[HALO:HALO + RF]
        gp = ge[HALO - 1:HALO] * jnp.where(i > 0, 1.0, 0.0)
        gn = ge[HALO + RF:HALO + RF + 1] * jnp.where(i < nb - 1, 1.0, 0.0)
        row = lax.broadcasted_iota(jnp.int32, (RF, 1), 0)
        g_prev = jnp.where(row == 0, gp, pltpu.roll(g, 1, axis=0))
        g_next = jnp.where(row == RF - 1, gn, pltpu.roll(g, RF - 1, axis=0))
        gc = cb_ref[...] + g_prev * cw_ref[0:1] + g * cw_ref[1:2] + g_next * cw_ref[2:3]
        fn = lambda gc_, a_, x_, g2_, fw_, pf_: _ffn_tail(gc_, a_, x_, g2_, fw_, pf_, wd_ref[...], wdt_ref[...], t_ref[...])
        loss, vjp, f = jax.vjp(fn, gc, a, x1b, gate2, fn_ref[...], jnp.zeros((RF, D), f32), has_aux=True)
        dgc, da, dx2, dg2, dfw, dffn = vjp(jnp.ones((1, 1), f32))
        dx2_ref[...] = dx2
        da_ref[...], dgc_ref[...] = da.astype(bf16), dgc
        f_ref[...], dffn_ref[...] = f.astype(bf16), dffn.astype(bf16)
        loss_ref[...] += jnp.broadcast_to(loss, (1, 128))
        dfn_ref[...] += dfw
        dg2_ref[...] += dg2
        dcb_ref[...] += jnp.sum(dgc, axis=0, keepdims=True)
        dcw_ref[0:1] += jnp.sum(dgc * g_prev, axis=0, keepdims=True)
        dcw_ref[1:2] += jnp.sum(dgc * g, axis=0, keepdims=True)
        dcw_ref[2:3] += jnp.sum(dgc * g_next, axis=0, keepdims=True)

    blk = lambda w: pl.BlockSpec((RF, w), lambda i: (i, 0))
    return pl.pallas_call(
        body, name=name, grid=(nb,),
        in_specs=[blk(D), pl.BlockSpec((HALO, D), lambda i: (jnp.maximum(i * per - 1, 0), 0)),
                  pl.BlockSpec((HALO, D), lambda i: (jnp.minimum((i + 1) * per, L // HALO - 1), 0)), blk(D),
                  _const_spec((1, D)), _const_spec((6, D)), _const_spec((D, DFF)), _const_spec((D, DFF)), _const_spec((3, DFF)),
                  _const_spec((1, DFF)), _const_spec((DFF, D)), _const_spec((D, DFF)), _const_spec((1, D))],
        out_specs=[blk(D), blk(DFF), blk(DFF), blk(DFF), blk(D), _acc_spec((1, 128)), _acc_spec((1, D)), _acc_spec((1, D)),
                   _acc_spec((1, DFF)), _acc_spec((3, DFF))],
        out_shape=[jax.ShapeDtypeStruct((L, D), f32), jax.ShapeDtypeStruct((L, DFF), bf16), jax.ShapeDtypeStruct((L, DFF), f32),
                   jax.ShapeDtypeStruct((L, DFF), bf16), jax.ShapeDtypeStruct((L, D), bf16), jax.ShapeDtypeStruct((1, 128), f32),
                   jax.ShapeDtypeStruct((1, D), f32), jax.ShapeDtypeStruct((1, D), f32), jax.ShapeDtypeStruct((1, DFF), f32),
                   jax.ShapeDtypeStruct((3, DFF), f32)],
        compiler_params=_CP(dimension_semantics=_ARB),
    )(x1, x1, x1, tgt, nw2, modx, w_a, w_g, cw, cb, wdown, wdown_t, fnw)


def _ffn_bwd(x1, dx2, da, dgc, nw2, modx, wup_t, cw, name):
    L = x1.shape[0]
    nb = L // RF
    per = RF // HALO

    def body(x_ref, dx2_ref, da_ref, dgc_ref, dgp_ref, dgn_ref, nw_ref, mx_ref, wu_ref, cw_ref,
             dx1_ref, dag_ref, h2_ref, dnw_ref, dmx_ref):
        i = pl.program_id(0)

        @pl.when(i == 0)
        def _():
            dnw_ref[...] = jnp.zeros_like(dnw_ref)
            dmx_ref[...] = jnp.zeros_like(dmx_ref)

        dgc_b = dgc_ref[...]
        before = dgp_ref[HALO - 1:HALO] * jnp.where(i > 0, 1.0, 0.0)
        after = dgn_ref[0:1] * jnp.where(i < nb - 1, 1.0, 0.0)
        row = lax.broadcasted_iota(jnp.int32, (RF, 1), 0)
        d_prev = jnp.where(row == 0, before, pltpu.roll(dgc_b, 1, axis=0))
        d_next = jnp.where(row == RF - 1, after, pltpu.roll(dgc_b, RF - 1, axis=0))
        dg = cw_ref[0:1] * d_next + cw_ref[1:2] * dgc_b + cw_ref[2:3] * d_prev
        dag = jnp.concatenate([da_ref[...], dg.astype(bf16)], axis=1)
        dag_ref[...] = dag
        dh2 = dnn(dag, wu_ref[...])
        h2, vjp = jax.vjp(lambda a, b, c, d: _mod(_rms(a, b), c, d), x_ref[...], nw_ref[...], mx_ref[3:4], mx_ref[4:5])
        dxa, dnw, dsh, dsc = vjp(dh2)
        h2_ref[...] = h2.astype(bf16)
        dx1_ref[...] = dx2_ref[...] + dxa
        dnw_ref[...] += dnw
        dmx_ref[3:4] += dsh
        dmx_ref[4:5] += dsc

    blk = lambda w: pl.BlockSpec((RF, w), lambda i: (i, 0))
    return pl.pallas_call(
        body, name=name, grid=(nb,),
        in_specs=[blk(D), blk(D), blk(DFF), blk(DFF), pl.BlockSpec((HALO, DFF), lambda i: (jnp.maximum(i * per - 1, 0), 0)),
                  pl.BlockSpec((HALO, DFF), lambda i: (jnp.minimum((i + 1) * per, L // HALO - 1), 0)),
                  _const_spec((1, D)), _const_spec((6, D)), _const_spec((2 * DFF, D)), _const_spec((3, DFF))],
        out_specs=[blk(D), blk(2 * DFF), blk(D), _acc_spec((1, D)), _acc_spec((6, D))],
        out_shape=[jax.ShapeDtypeStruct((L, D), f32), jax.ShapeDtypeStruct((L, 2 * DFF), bf16), jax.ShapeDtypeStruct((L, D), bf16),
                   jax.ShapeDtypeStruct((1, D), f32), jax.ShapeDtypeStruct((6, D), f32)],
        compiler_params=_CP(dimension_semantics=_ARB),
    )(x1, dx2, da, dgc, dgc, dgc, nw2, modx, wup_t, cw)


def _matmul_tn(a, b, name, cargo=None):
    k, m = a.shape
    n = b.shape[1]
    divs = lambda d: [c for c in range(d, 0, -128) if d % c == 0]
    _, tm, tn = min((m * (n // cn) + n * (m // cm), cm, cn) for cm in divs(m) for cn in divs(n) if cm * cn * 4 <= ACC_TILE_BYTES)
    tk = next(c for c in (512, 768, 256, 128) if k % c == 0)
    nk = k // tk
    grid = (m // tm, n // tn, nk)
    cg = _Cargo(cargo)

    def body(*refs):
        (a_ref, b_ref), (o_ref,), (acc,) = cg.split(refs, 2, 1, 1)
        cg.ride(refs, 2, 1, grid)
        q = pl.program_id(2)

        @pl.when(q == 0)
        def _():
            acc[...] = jnp.zeros_like(acc)

        acc[...] += dtn(a_ref[...], b_ref[...])

        @pl.when(q == nk - 1)
        def _():
            o_ref[...] = acc[...].astype(bf16)

    out = pl.pallas_call(
        body, name=name, grid=grid,
        in_specs=[pl.BlockSpec((tk, tm), lambda i, j, q: (q, i)), pl.BlockSpec((tk, tn), lambda i, j, q: (q, j))] + cg.in_specs(),
        out_specs=[pl.BlockSpec((tm, tn), lambda i, j, q: (i, j))] + cg.in_specs(),
        out_shape=[jax.ShapeDtypeStruct((m, n), bf16)] + cg.out_shapes(),
        scratch_shapes=[pltpu.VMEM((tm, tn), f32)] + cg.sems(),
        compiler_params=_CP(dimension_semantics=("arbitrary",) * 3 if cg.n else ("parallel", "parallel", "arbitrary")),
    )(a, b, *cg.arrays)
    return out if cg.n else out[0]


def _adamw_refs(w_ref, g_ref, m_ref, v_ref, d_ref, nm_ref, nv_ref):
    c1, c2 = 1.0 - B1 ** STEP, 1.0 - B2 ** STEP
    gg = g_ref[...]
    nm = B1 * m_ref[...] + (1.0 - B1) * gg
    nv = B2 * v_ref[...] + (1.0 - B2) * jnp.square(gg)
    d_ref[...] = -LR * ((nm / c1) / (jnp.sqrt(nv / c2) + AEPS) + WD * w_ref[...])
    nm_ref[...], nv_ref[...] = nm, nv


def _adamw(w, g, m, v, name):
    def body(*refs):
        _adamw_refs(*refs)

    return pl.pallas_call(body, name=name, out_shape=[jax.ShapeDtypeStruct(w.shape, f32)] * 3, compiler_params=_CP())(w, g, m, v)


def _adamw_many(ws, gs, ms, vs, name):
    n = len(ws)

    def body(*refs):
        for k in range(n):
            _adamw_refs(*[refs[j * n + k] for j in range(7)])

    outs = pl.pallas_call(body, name=name, out_shape=[jax.ShapeDtypeStruct(w.shape, f32) for w in ws] * 3,
                          compiler_params=_CP())(*ws, *gs, *ms, *vs)
    return outs[:n], outs[n:2 * n], outs[2 * n:]


SMALL = ["conv_w", "c_ctx", "norm1_w", "s5_lambda_re_f", "s5_lambda_im_f", "s5_log_step_f", "s5_lambda_re_b", "s5_lambda_im_b",
         "s5_log_step_b", "s5_b_re", "s5_b_im", "s5_c_re", "s5_c_im", "s5_d", "s5_b_glu", "ret_log_decay_f", "ret_log_decay_b",
         "norm2_w", "conv_b", "final_norm_w"]
WEIGHTS = ["c_ctx", "w_mod", "b_mod", "norm1_w", "w_in", "s5_lambda_re_f", "s5_lambda_im_f", "s5_log_step_f", "s5_lambda_re_b",
           "s5_lambda_im_b", "s5_log_step_b", "s5_b_re", "s5_b_im", "s5_c_re", "s5_c_im", "s5_d", "s5_w_glu", "s5_b_glu",
           "ret_log_decay_f", "ret_log_decay_b", "w_out", "norm2_w", "w_up", "conv_w", "conv_b", "w_down", "final_norm_w"]


def _pack_small(vals):
    flat, offs, o = [], [], 0
    for a in vals:
        n = a.size
        npad = -n % 128
        flat.append(jnp.pad(a.reshape(-1), (0, npad)))
        offs.append((o, n))
        o += n + npad
    tail = -o % 1024
    if tail:
        flat.append(jnp.zeros((tail,), f32))
    return jnp.concatenate(flat).reshape(-1, 128), offs


def _unpack_small(packed, offs, shapes):
    flat = packed.reshape(-1)
    return [flat[o:o + n].reshape(s) for (o, n), s in zip(offs, shapes)]


def _rope_tables(L, nctx_rows):
    t = np.arange(L)
    inv = (ROPE_THETA ** (-np.arange(DH // 4, dtype=np.float64) / (DH // 4))).astype(np.float32)
    ang = np.concatenate([(t // GRID_W).astype(np.float32)[:, None] * inv, (t % GRID_W).astype(np.float32)[:, None] * inv], axis=-1)
    cos = np.repeat(np.cos(ang).astype(np.float32), 2, axis=1)
    sin = np.repeat(np.sin(ang).astype(np.float32), 2, axis=1) * np.tile(np.array([-1.0, 1.0], np.float32), DH // 2)
    cosf = np.concatenate([np.ones((nctx_rows, DH), np.float32), cos], axis=0)
    sins = np.concatenate([np.zeros((nctx_rows, DH), np.float32), sin], axis=0)
    return jnp.asarray(cosf), jnp.asarray(sins)


def kernel(x, c, ctx, c_ctx, w_mod, b_mod, norm1_w, w_in, s5_lambda_re_f, s5_lambda_im_f, s5_log_step_f, s5_lambda_re_b, s5_lambda_im_b, s5_log_step_b, s5_b_re, s5_b_im, s5_c_re, s5_c_im, s5_d, s5_w_glu, s5_b_glu, ret_log_decay_f, ret_log_decay_b, w_out, norm2_w, w_up, conv_w, conv_b, w_down, final_norm_w, loss_target, m_c_ctx, m_w_mod, m_b_mod, m_norm1_w, m_w_in, m_s5_lambda_re_f, m_s5_lambda_im_f, m_s5_log_step_f, m_s5_lambda_re_b, m_s5_lambda_im_b, m_s5_log_step_b, m_s5_b_re, m_s5_b_im, m_s5_c_re, m_s5_c_im, m_s5_d, m_s5_w_glu, m_s5_b_glu, m_ret_log_decay_f, m_ret_log_decay_b, m_w_out, m_norm2_w, m_w_up, m_conv_w, m_conv_b, m_w_down, m_final_norm_w, v_c_ctx, v_w_mod, v_b_mod, v_norm1_w, v_w_in, v_s5_lambda_re_f, v_s5_lambda_im_f, v_s5_log_step_f, v_s5_lambda_re_b, v_s5_lambda_im_b, v_s5_log_step_b, v_s5_b_re, v_s5_b_im, v_s5_c_re, v_s5_c_im, v_s5_d, v_s5_w_glu, v_s5_b_glu, v_ret_log_decay_f, v_ret_log_decay_b, v_w_out, v_norm2_w, v_w_up, v_conv_w, v_conv_b, v_w_down, v_final_norm_w):
    args = dict(locals())
    W = {n: args[n] for n in WEIGHTS}
    M = {n: args["m_" + n] for n in WEIGHTS}
    V = {n: args["v_" + n] for n in WEIGHTS}
    me = _me()
    x2, ctx2, tgt = x[0], ctx[0], loss_target[0]
    L, Lc = x2.shape[0], ctx2.shape[0]
    assert Lc == R and L % R == 0 and L % GRID_W == 0
    nctx = Lc // T

    w_in_tl, w_up_tl = w_in[0].T.astype(bf16), w_up[0].T.astype(bf16)
    w_out_l, w_down_l, w_glu_l = w_out[0].astype(bf16), w_down[0].astype(bf16), s5_w_glu[0].astype(bf16)
    half_up = w_up_tl.shape[0] // 2
    per_cv = conv_w.shape[2]
    conv_pad = jnp.pad(conv_w[0], ((0, 5), (0, 128 * 3 - per_cv)))
    w_in_g, c_g, conv_g = _exchange([w_in_tl, jnp.pad(c, ((0, 7), (0, 0))), conv_pad], False, "gather_w_in")
    w_in_t = w_in_g.reshape(INC, D)
    conv_f = conv_g[:, :3, :per_cv].transpose(1, 0, 2).reshape(3, DFF)

    c9 = jnp.concatenate([c_g[:, 0, :], c_ctx[None], jnp.zeros((7, D), f32)], axis=0)
    w_mod_l = w_mod[0]
    ncol = w_mod_l.shape[1]
    m_part = _ada_fwd(c9, w_mod_l, "ada_fwd")
    m_all = _all_gather_small(m_part, "gather_mod").transpose(1, 0, 2).reshape(16, 6, D)
    modx, modc = _mod_select(m_all, b_mod.reshape(6, D), "mod_select")

    pair = lambda a, b: jnp.concatenate([a, b], axis=-1)
    bre_g, bim_g = s5_b_re[0].transpose(0, 2, 1), s5_b_im[0].transpose(0, 2, 1)
    cre_g, cim_g = s5_c_re[0], s5_c_im[0]
    shared = (pair(bre_g, bim_g), pair(bim_g, bre_g), pair(cre_g, cim_g), pair(cim_g, cre_g))
    s5p = {}
    for tag, lre, lim, ls in (("f", s5_lambda_re_f, s5_lambda_im_f, s5_log_step_f), ("b", s5_lambda_re_b, s5_lambda_im_b, s5_log_step_b)):
        s5p[tag] = (pair(lre[0], lre[0])[:, None, :], pair(lim[0], lim[0])[:, None, :], ls[0].reshape(S5G, 1, 1)) + shared
    m_f, mb_f, mc_f, a1_f, a2_f = _s5_build(s5p["f"], False, "s5_build_f")
    m_b, mb_b, mc_b, a1_b, a2_b = _s5_build(s5p["b"], True, "s5_build_b")
    a1_f, a2_f, a1_b, a2_b = (a.reshape(S5G, SB) for a in (a1_f, a2_f, a1_b, a2_b))

    nw1, nw2, fnw = norm1_w, norm2_w, final_norm_w[None]
    cosf, sins = _rope_tables(L, Lc)
    p_ext, w_out_g, w_glu_g = _f1_fwd(x2, ctx2, modx, modc, nw1, w_in_t.T, cosf, sins, "f1_fwd", cargo=([w_out_l, w_glu_l], False))
    nctx5 = Lc // TC
    u_g = _to_groups(p_ext[:, :S5W])
    s_f, s_b = _s5_inc(u_g, mb_f, mb_b, "s5_inc")
    hp_f, hp_b = _s5_carry(s_f, s_b, (a1_f, a2_f), (a1_b, a2_b), nctx5, "s5_carry")
    ys = _from_groups(_s5_out(u_g, m_f, m_b, hp_f, hp_b, mc_f, mc_b, "s5_out"))
    ld8 = lambda ld: jnp.pad(jnp.broadcast_to(ld[0][:, None], (RH, 128)), ((0, 8 - RH), (0, 0)))
    ldf8, ldb8 = ld8(ret_log_decay_f), ld8(ret_log_decay_b)
    of, rp_f, w_up_g1 = _ret_fwd(p_ext, ldf8, False, nctx, "ret_fwd_f", cargo=([w_up_tl[:half_up]], False))
    ob, rp_b, w_up_g2 = _ret_fwd(p_ext, ldb8, True, nctx, "ret_fwd_b", cargo=([w_up_tl[half_up:]], False))
    w_out_f, w_glu_f = w_out_g.reshape(D, D), w_glu_g.reshape(S5W, S5W)
    x1, w_down_g = _mix_fwd(x2, ys, of, ob, p_ext, s5_d, s5_b_glu, modx, w_glu_f, w_out_f, "mix_fwd", cargo=([w_down_l], False))
    w_down_f = w_down_g.reshape(DFF, D)
    w_up_t = jnp.concatenate([w_up_g1, w_up_g2], axis=1).reshape(2 * DFF, D)

    (dx2, da, dgc, f_act, dffn, loss_acc, g_fnw, g_gate2, g_cb, g_cw) = _ffn_fwd(
        x1, tgt, nw2, modx, w_up_t[:DFF].T, w_up_t[DFF:].T, conv_f, conv_b, w_down_f, w_down_f.T, fnw, "ffn_fwd")
    dx1, dag, h2, g_nw2, dmx2 = _ffn_bwd(x1, dx2, da, dgc, nw2, modx, w_up_t, conv_f, "ffn_bwd")
    gw_down = _matmul_tn(f_act, dffn, "dw_down").reshape(NDEV, -1, D)
    gw_up_t = _matmul_tn(dag, h2, "dw_up").reshape(NDEV, -1, D)
    (dy_e, dud_e, do_e, dg_e, cat, dmix, s_act, dz, g_d, g_bglu, g_gate1, l_down) = _mix_bwd(
        x2, ys, of, ob, p_ext, s5_d, s5_b_glu, modx, w_glu_f, w_out_f, dx1, "mix_bwd", cargo=([gw_down], True))
    gw_out = _matmul_tn(cat, dmix, "dw_out").reshape(NDEV, -1, D)
    gw_glu = _matmul_tn(s_act, dz, "dw_glu").reshape(NDEV, -1, S5W)
    dq_f, dk_f, dv_f, gld_f, l_up1 = _ret_bwd(p_ext, ldf8, rp_f, do_e, False, nctx, "ret_bwd_f",
                                              cargo=([gw_up_t[:, :half_up]], True))
    dq_b, dk_b, dv_b, gld_b, l_out, l_glu, l_up2 = _ret_bwd(p_ext, ldb8, rp_b, do_e, True, nctx, "ret_bwd_b",
                                                            cargo=([gw_out, gw_glu, gw_up_t[:, half_up:]], True))

    du1, g_m, dhp_f, dhp_b, dmc_f, dmc_b = _s5_out_bwd(_to_groups(dy_e), u_g, m_f, m_b, hp_f, hp_b, mc_f, mc_b, "s5_out_bwd")
    ds_f, da1_f, da2_f = _s5_carry_bwd(dhp_f, hp_f, a1_f, a2_f, False, nctx5, "s5_carry_bwd_f")
    ds_b, da1_b, da2_b = _s5_carry_bwd(dhp_b, hp_b, a1_b, a2_b, True, nctx5, "s5_carry_bwd_b")
    du_g, dmb_f, dmb_b = _s5_inc_bwd(du1, u_g, ds_f, ds_b, mb_f, mb_b, "s5_inc_bwd")
    zero_p = jnp.zeros((S5G, S5P, SB), f32)
    gf = _s5_build_bwd(s5p["f"], (g_m, dmb_f, dmc_f, da1_f[:, None, :], da2_f[:, None, :]), (zero_p, zero_p), False, "s5_build_bwd_f")
    gb = _s5_build_bwd(s5p["b"], (g_m, dmb_b, dmc_b, da1_b[:, None, :], da2_b[:, None, :]), (gf[3], gf[4]), True, "s5_build_bwd_b")
    g_bre, g_bim = gb[3][:, :, :S5N].transpose(0, 2, 1), gb[3][:, :, S5N:].transpose(0, 2, 1)
    g_cre, g_cim = gb[4][:, :, :S5N], gb[4][:, :, S5N:]

    early = {
        "conv_w": g_cw, "s5_lambda_re_f": gf[0][:, 0, :S5N], "s5_lambda_im_f": gf[1][:, 0, :S5N],
        "s5_log_step_f": gf[2], "s5_lambda_re_b": gb[0][:, 0, :S5N], "s5_lambda_im_b": gb[1][:, 0, :S5N], "s5_log_step_b": gb[2],
        "s5_b_re": g_bre, "s5_b_im": g_bim, "s5_c_re": g_cre, "s5_c_im": g_cim, "s5_d": g_d, "s5_b_glu": g_bglu,
        "ret_log_decay_f": gld_f[:RH, 0], "ret_log_decay_b": gld_b[:RH, 0], "norm2_w": g_nw2, "conv_b": g_cb, "final_norm_w": g_fnw,
    }
    e_names = [n for n in SMALL if n in early]
    packed_e, eoffs = _pack_small([early[n].astype(f32) for n in e_names])
    grad_x, dp_ext, h1, g_nw1, dmx1, dmc1 = _f1_bwd(
        x2, ctx2, modx, modc, nw1, w_in_t, cosf, sins, dx1, (_from_groups(du_g), dud_e, dq_f, dq_b, dk_f, dk_b, dv_f, dv_b, dg_e), "f1_bwd")
    gw_in_t, land_e = _matmul_tn(dp_ext, h1, "dw_in", cargo=([packed_e], False))
    (l_in,) = _exchange([gw_in_t.reshape(NDEV, -1, D)], True, "scatter_dw_in")

    dmx = dmx1 + dmx2
    dmx = dmx.at[2].set(g_gate1[0]).at[5].set(g_gate2[0])
    dm_me = jnp.stack([dmx.reshape(-1), dmc1.reshape(-1)], axis=0)
    dm_all = _all_gather_small(dm_me.reshape(8, -1), "gather_dmod").reshape(NDEV, 2, 6 * D)
    dmx_all, dmc_all = dm_all[:, 0, :], dm_all[:, 1, :]
    my_cols = lambda a: lax.dynamic_slice(a, (0, me * ncol), (NDEV, ncol))
    gw_mod, g_bmod, dc9 = _ada_bwd(c9, dmx_all, dmc_all, my_cols(dmx_all), my_cols(dmc_all), w_mod_l, "ada_bwd")

    sshape = lambda n: (3, DFF) if n == "conv_w" else W[n].shape
    G = dict(zip(e_names, _unpack_small(_sum8(land_e, "reduce_early"), eoffs, [sshape(n) for n in e_names])))
    late = {"c_ctx": dc9[8], "norm1_w": g_nw1}
    packed_l, loffs = _pack_small([late[n].astype(f32) for n in late])
    G.update(zip(late, _unpack_small(_all_reduce_small(packed_l, "reduce_late"), loffs, [W[n].shape for n in late])))
    G["conv_w"] = lax.dynamic_slice(G["conv_w"], (0, me * per_cv), (3, per_cv))[None]
    G["b_mod"] = g_bmod.reshape(b_mod.shape)
    G["w_mod"] = gw_mod[None]
    G["w_in"] = _sum8(l_in, "sum_dw_in").T[None]
    G["w_up"] = jnp.concatenate([_sum8(l_up1, "sum_dw_up1"), _sum8(l_up2, "sum_dw_up2")], axis=0).T[None]
    G["w_out"] = _sum8(l_out, "sum_dw_out")[None]
    G["w_down"] = _sum8(l_down, "sum_dw_down")[None]
    G["s5_w_glu"] = _sum8(l_glu, "sum_dw_glu")[None]

    delta, new_m, new_v = {}, {}, {}
    sm_names = SMALL[1:] + ["b_mod"]
    rows = lambda a: a.reshape(-1, a.shape[-1])
    outs = _adamw_many(*[[rows(d[n]) for n in sm_names] for d in (W, G, M, V)], "adamw_small")
    for dst, src in zip((delta, new_m, new_v), outs):
        dst.update({n: a.reshape(W[n].shape) for n, a in zip(sm_names, src)})
    for n in ["w_mod", "w_in", "w_out", "w_up", "w_down", "s5_w_glu", "conv_w"]:
        d, nm, nv = _adamw(W[n][0], G[n][0], M[n][0], V[n][0], "adamw_" + n)
        delta[n], new_m[n], new_v[n] = d[None], nm[None], nv[None]

    loss = lax.psum(loss_acc[0, 0], ("x", "y", "c"))
    return (loss, grad_x[None], *[G[n] for n in WEIGHTS], *[delta[n] for n in WEIGHTS], *[new_m[n] for n in WEIGHTS],
            *[new_v[n] for n in WEIGHTS])
```

```python
import functools

import numpy as np
import jax
import jax.numpy as jnp
from jax import lax
from jax.experimental import pallas as pl
from jax.experimental.pallas import tpu as pltpu

f32, bf16 = jnp.float32, jnp.bfloat16

D = 1024
S5W, S5G, S5P, S5N = 512, 32, 16, 64
TC = 16
TCP = TC * S5P
SB = 2 * S5N
GBK = 8
CARRY_UNROLL = 8
RH, DH = 4, 128
RW = RH * DH
INC = S5W + 4 * RW
DFF = 2816
T = 128
R = 256
RF = 128
HALO = 8
EPS = 1e-6
ROPE_THETA = 10000.0
GRID_W = 64
NDEV = 8
LR, B1, B2, AEPS, WD, STEP = 0.001, 0.9, 0.999, 1e-08, 0.01, 10
VMEM_LIMIT = 60 * 1024 * 1024
ACC_TILE_BYTES = 6 * 1024 * 1024
MESH = pl.DeviceIdType.MESH

_CP = functools.partial(pltpu.CompilerParams, vmem_limit_bytes=VMEM_LIMIT)
_ARB = ("arbitrary",)
_ANY = pl.BlockSpec(memory_space=pl.ANY)


def _dg(a, b, dims):
    return lax.dot_general(a.astype(bf16), b.astype(bf16), (dims, ((), ())), preferred_element_type=f32)


@jax.custom_vjp
def dnn(a, b):
    return _dg(a, b, ((1,), (0,)))


@jax.custom_vjp
def dnt(a, b):
    return _dg(a, b, ((1,), (1,)))


@jax.custom_vjp
def dtn(a, b):
    return _dg(a, b, ((0,), (0,)))


dnn.defvjp(lambda a, b: (dnn(a, b), (a, b)), lambda r, g: (dnt(g, r[1]).astype(r[0].dtype), dtn(r[0], g).astype(r[1].dtype)))
dnt.defvjp(lambda a, b: (dnt(a, b), (a, b)), lambda r, g: (dnn(g, r[1]).astype(r[0].dtype), dtn(g, r[0]).astype(r[1].dtype)))
dtn.defvjp(lambda a, b: (dtn(a, b), (a, b)), lambda r, g: (dnt(r[1], g).astype(r[0].dtype), dnn(r[0], g).astype(r[1].dtype)))


@jax.custom_vjp
def _dnn_const(a, w, wt):
    return dnn(a, w)


_dnn_const.defvjp(lambda a, w, wt: (dnn(a, w), wt), lambda wt, g: (dnn(g, wt), None, None))


def _rms(t, w):
    return t * lax.rsqrt(jnp.mean(t * t, axis=-1, keepdims=True) + EPS) * w


def _mod(h, shift, scale):
    return h * (1.0 + scale) + shift


def _const_spec(shape):
    n = len(shape)
    return pl.BlockSpec(shape, lambda i, _n=n: (0,) * _n, pipeline_mode=pl.Buffered(1))


def _acc_spec(shape):
    n = len(shape)
    return pl.BlockSpec(shape, lambda i, _n=n: (0,) * _n)


def _me():
    return 4 * lax.axis_index("x") + 2 * lax.axis_index("y") + lax.axis_index("c")


def _peer(r):
    x, y, c = lax.axis_index("x"), lax.axis_index("y"), lax.axis_index("c")
    px = 1 - x if (r >> 2) & 1 else x
    py = 1 - y if (r >> 1) & 1 else y
    pc = 1 - c if r & 1 else c
    return (px, py, pc), 4 * px + 2 * py + pc


def _all_gather_small(v, name):
    r, c = v.shape

    def body(v_ref, out_ref, send_sems, recv_sems):
        me = _me()
        out_ref[me] = v_ref[...]
        sends = []
        for k in range(1, NDEV):
            peer, _ = _peer(k)
            cp = pltpu.make_async_remote_copy(src_ref=v_ref, dst_ref=out_ref.at[me], send_sem=send_sems.at[k - 1],
                                              recv_sem=recv_sems.at[k - 1], device_id=peer, device_id_type=MESH)
            cp.start()
            sends.append(cp)
        for k in range(1, NDEV):
            peer, pidx = _peer(k)
            pltpu.make_async_remote_copy(src_ref=v_ref, dst_ref=out_ref.at[pidx], send_sem=send_sems.at[k - 1],
                                         recv_sem=recv_sems.at[k - 1], device_id=peer, device_id_type=MESH).wait_recv()
        for cp in sends:
            cp.wait_send()

    return pl.pallas_call(
        body, name=name, out_shape=jax.ShapeDtypeStruct((NDEV, r, c), v.dtype),
        in_specs=[pl.BlockSpec(memory_space=pltpu.VMEM)], out_specs=pl.BlockSpec(memory_space=pltpu.VMEM),
        scratch_shapes=[pltpu.SemaphoreType.DMA((NDEV - 1,)), pltpu.SemaphoreType.DMA((NDEV - 1,))],
        compiler_params=_CP(),
    )(v)


def _all_reduce_small(v, name):
    r, c = v.shape

    def body(v_ref, out_ref, land, send_sems, recv_sems):
        me = _me()
        land[me] = v_ref[...]
        sends = []
        for k in range(1, NDEV):
            peer, _ = _peer(k)
            cp = pltpu.make_async_remote_copy(src_ref=v_ref, dst_ref=land.at[me], send_sem=send_sems.at[k - 1],
                                              recv_sem=recv_sems.at[k - 1], device_id=peer, device_id_type=MESH)
            cp.start()
            sends.append(cp)
        for k in range(1, NDEV):
            peer, pidx = _peer(k)
            pltpu.make_async_remote_copy(src_ref=v_ref, dst_ref=land.at[pidx], send_sem=send_sems.at[k - 1],
                                         recv_sem=recv_sems.at[k - 1], device_id=peer, device_id_type=MESH).wait_recv()
        for cp in sends:
            cp.wait_send()
        acc = land[0]
        for j in range(1, NDEV):
            acc = acc + land[j]
        out_ref[...] = acc

    return pl.pallas_call(
        body, name=name, out_shape=jax.ShapeDtypeStruct((r, c), v.dtype),
        in_specs=[pl.BlockSpec(memory_space=pltpu.VMEM)], out_specs=pl.BlockSpec(memory_space=pltpu.VMEM),
        scratch_shapes=[pltpu.VMEM((NDEV, r, c), v.dtype), pltpu.SemaphoreType.DMA((NDEV - 1,)),
                        pltpu.SemaphoreType.DMA((NDEV - 1,))],
        compiler_params=_CP(),
    )(v)


class _Exchange:
    def __init__(self, srcs, dsts, send_sems, recv_sems, local_sems, scatter):
        me = _me()
        n = len(srcs)
        self.sends, self.recvs, self.locals = [], [], []
        for a, (s, d) in enumerate(zip(srcs, dsts)):
            self.locals.append(pltpu.make_async_copy(s.at[me] if scatter else s, d.at[me], local_sems.at[a]))
        for k in range(1, NDEV):
            peer, pidx = _peer(k)
            for a, (s, d) in enumerate(zip(srcs, dsts)):
                src = s.at[pidx] if scatter else s
                sem = (k - 1) * n + a
                for dst, out in ((d.at[me], self.sends), (d.at[pidx], self.recvs)):
                    out.append(pltpu.make_async_remote_copy(src_ref=src, dst_ref=dst, send_sem=send_sems.at[sem],
                                                            recv_sem=recv_sems.at[sem], device_id=peer, device_id_type=MESH))

    def start(self):
        for cp in self.locals + self.sends:
            cp.start()

    def wait(self):
        for cp in self.recvs:
            cp.wait_recv()
        for cp in self.sends:
            cp.wait_send()
        for cp in self.locals:
            cp.wait()


def _exchange_shapes(arrays, scatter):
    return [jax.ShapeDtypeStruct(a.shape if scatter else (NDEV,) + a.shape, a.dtype) for a in arrays]


def _exchange_sems(n):
    return [pltpu.SemaphoreType.DMA(((NDEV - 1) * n,)), pltpu.SemaphoreType.DMA(((NDEV - 1) * n,)), pltpu.SemaphoreType.DMA((n,))]


def _exchange(arrays, scatter, name):
    n = len(arrays)

    def body(*refs):
        ex = _Exchange(refs[:n], refs[n:2 * n], *refs[2 * n:], scatter)
        ex.start()
        ex.wait()

    return pl.pallas_call(body, name=name, out_shape=_exchange_shapes(arrays, scatter), in_specs=[_ANY] * n,
                          out_specs=[_ANY] * n, scratch_shapes=_exchange_sems(n), compiler_params=_CP())(*arrays)


def _chips():
    x, y, c = lax.axis_index("x"), lax.axis_index("y"), lax.axis_index("c")
    return (x, y, c), (x, y, 1 - c), [(1 - x, y), (x, 1 - y), (1 - x, 1 - y)]


def _gather_two_level(arrays, name):
    n = len(arrays)

    def body(*refs):
        srcs, outs = refs[:n], refs[n:2 * n]
        send_sems, recv_sems = refs[2 * n:]
        me, sibling, chips = _chips()
        c = me[2]
        idx = lambda p: 4 * p[0] + 2 * p[1] + p[2]

        def copy(a, k, block, to, src=None):
            return pltpu.make_async_remote_copy(
                src_ref=outs[a].at[idx(block)] if src is None else src, dst_ref=outs[a].at[idx(block)],
                send_sem=send_sems.at[7 * a + k], recv_sem=recv_sems.at[7 * a + k], device_id=to, device_id_type=MESH)

        first, passed = [], []
        for a in range(n):
            outs[a][idx(me)] = srcs[a][...]
            first += [copy(a, 0, me, sibling, src=srcs[a])]
            first += [copy(a, 1 + j, me, (*chip, c), src=srcs[a]) for j, chip in enumerate(chips)]
        for cp in first:
            cp.start()
        for a in range(n):
            for j, chip in enumerate(chips):
                copy(a, 1 + j, (*chip, c), me).wait_recv()
                cp = copy(a, 4 + j, (*chip, c), sibling)
                cp.start()
                passed.append(cp)
        for a in range(n):
            copy(a, 0, sibling, me).wait_recv()
            for j, chip in enumerate(chips):
                copy(a, 4 + j, (*chip, 1 - c), me).wait_recv()
        for cp in first + passed:
            cp.wait_send()

    vm = pl.BlockSpec(memory_space=pltpu.VMEM)
    return pl.pallas_call(
        body, name=name, out_shape=[jax.ShapeDtypeStruct((NDEV,) + a.shape, a.dtype) for a in arrays],
        in_specs=[vm] * n, out_specs=[vm] * n,
        scratch_shapes=[pltpu.SemaphoreType.DMA((7 * n,)), pltpu.SemaphoreType.DMA((7 * n,))],
        compiler_params=_CP(),
    )(*arrays)


def _reduce_scatter_two_level(g, name):
    _, r, c = g.shape
    nchip = NDEV // 2

    def body(g_ref, o_ref, stage, part, land, d_send, d_recv, i_send, i_recv):
        me, sibling, chips = _chips()
        x, y, cc = me
        mine = 2 * x + y

        def blk(k, core):
            return 2 * k + core

        swaps = [pltpu.make_async_remote_copy(src_ref=g_ref.at[blk(k, 1 - cc)], dst_ref=stage.at[k], send_sem=d_send.at[k],
                                              recv_sem=d_recv.at[k], device_id=sibling, device_id_type=MESH) for k in range(nchip)]
        for cp in swaps:
            cp.start()
        for cp in swaps:
            cp.wait_recv()
        for k in range(nchip):
            part[k] = (g_ref[blk(k, cc)].astype(f32) + stage[k].astype(f32)).astype(bf16)
        sends = []
        for j, chip in enumerate(chips):
            kd = 2 * chip[0] + chip[1]
            cp = pltpu.make_async_remote_copy(src_ref=part.at[kd], dst_ref=land.at[mine], send_sem=i_send.at[j],
                                              recv_sem=i_recv.at[j], device_id=(*chip, cc), device_id_type=MESH)
            cp.start()
            sends.append(cp)
        land[mine] = part[mine]
        for j, chip in enumerate(chips):
            ks = 2 * chip[0] + chip[1]
            pltpu.make_async_remote_copy(src_ref=part.at[ks], dst_ref=land.at[ks], send_sem=i_send.at[j], recv_sem=i_recv.at[j],
                                         device_id=(*chip, cc), device_id_type=MESH).wait_recv()
        for cp in swaps + sends:
            cp.wait_send()
        acc = land[0].astype(f32)
        for k in range(1, nchip):
            acc = acc + land[k].astype(f32)
        o_ref[...] = acc

    vm = pl.BlockSpec(memory_space=pltpu.VMEM)
    return pl.pallas_call(
        body, name=name, out_shape=jax.ShapeDtypeStruct((r, c), f32), in_specs=[vm], out_specs=vm,
        scratch_shapes=[pltpu.VMEM((nchip, r, c), g.dtype)] * 3 + [pltpu.SemaphoreType.DMA((nchip,)), pltpu.SemaphoreType.DMA((nchip,)),
                                                                   pltpu.SemaphoreType.DMA((3,)), pltpu.SemaphoreType.DMA((3,))],
        compiler_params=_CP(),
    )(g)


class _Cargo:
    def __init__(self, cargo):
        self.arrays, self.scatter = cargo if cargo else ([], False)
        self.n = len(self.arrays)

    def in_specs(self):
        return [_ANY] * self.n

    def out_shapes(self):
        return _exchange_shapes(self.arrays, self.scatter)

    def sems(self):
        return _exchange_sems(self.n) if self.n else []

    def split(self, refs, n_in, n_out, n_scratch):
        n = self.n
        return refs[:n_in], refs[n_in + n:n_in + n + n_out], refs[n_in + 2 * n + n_out:n_in + 2 * n + n_out + n_scratch]

    def ride(self, refs, n_in, n_out, grid):
        if not self.n:
            return
        n = self.n
        ex = _Exchange(refs[n_in:n_in + n], refs[n_in + n + n_out:n_in + 2 * n + n_out], *refs[-3:], self.scatter)
        grid = (grid,) if isinstance(grid, int) else tuple(grid)
        first = functools.reduce(jnp.logical_and, [pl.program_id(a) == 0 for a in range(len(grid))])
        last = functools.reduce(jnp.logical_and, [pl.program_id(a) == g - 1 for a, g in enumerate(grid)])

        @pl.when(first)
        def _():
            ex.start()

        @pl.when(last)
        def _():
            ex.wait()


def _sum8(land, name):
    _, r, c = land.shape
    rb = next((b for b in (256, 64, 32) if r % b == 0), r)

    def body(l_ref, o_ref):
        acc = l_ref[0].astype(f32)
        for j in range(1, NDEV):
            acc = acc + l_ref[j].astype(f32)
        o_ref[...] = acc

    return pl.pallas_call(
        body, name=name, grid=(r // rb,), out_shape=jax.ShapeDtypeStruct((r, c), f32),
        in_specs=[pl.BlockSpec((NDEV, rb, c), lambda i: (0, i, 0))], out_specs=pl.BlockSpec((rb, c), lambda i: (i, 0)),
        compiler_params=_CP(dimension_semantics=("parallel",)),
    )(land)


def _ada_fwd(c9, w_mod_l, name):
    def body(c_ref, w_ref, o_ref):
        o_ref[...] = dnn(jax.nn.silu(c_ref[...]), w_ref[...])

    return pl.pallas_call(body, name=name, out_shape=jax.ShapeDtypeStruct((16, w_mod_l.shape[1]), f32),
                          compiler_params=_CP())(c9, w_mod_l)


def _mod_select(m_all, b_mod6, name):
    def body(m_ref, b_ref, mx_ref, mc_ref):
        me = _me()
        mx_ref[...] = m_ref[me] + b_ref[...]
        mc_ref[...] = m_ref[8] + b_ref[...]

    return pl.pallas_call(body, name=name, out_shape=[jax.ShapeDtypeStruct((6, D), f32)] * 2, compiler_params=_CP())(m_all, b_mod6)


def _ada_bwd(c9, dmx_all, dmc_all, dmx_l, dmc_l, w_mod_l, name):
    ncol = w_mod_l.shape[1]

    def rowsum(r):
        acc = r[0:1]
        for j in range(1, NDEV):
            acc = acc + r[j:j + 1]
        return acc

    def body(c_ref, xa_ref, ca_ref, xl_ref, cl_ref, w_ref, gw_ref, gb_ref, dc_ref):
        s9, vjp = jax.vjp(jax.nn.silu, c_ref[...])
        dm9 = jnp.concatenate([xl_ref[...], rowsum(cl_ref[...]), jnp.zeros((7, ncol), f32)], axis=0)
        gw_ref[...] = dtn(s9, dm9)
        gb_ref[...] = rowsum(xa_ref[...]) + rowsum(ca_ref[...])
        dc_ref[...] = vjp(dnt(dm9, w_ref[...]))[0]

    return pl.pallas_call(
        body, name=name,
        out_shape=[jax.ShapeDtypeStruct((D, ncol), f32), jax.ShapeDtypeStruct((1, 6 * D), f32), jax.ShapeDtypeStruct((16, D), f32)],
        compiler_params=_CP())(c9, dmx_all, dmc_all, dmx_l, dmc_l, w_mod_l)


def _lane_sign(rank):
    shape = (1,) * (rank - 1) + (SB,)
    return jnp.where(lax.broadcasted_iota(jnp.int32, shape, rank - 1) < S5N, -1.0, 1.0)


def _s5_build_fn(lre2, lim2, ls, bn, bs, cn, cs, rev):
    sg = _lane_sign(3)
    s = jnp.exp(ls)
    ar, ai = lre2 * s, lim2 * s
    e = jnp.exp(ar)
    nr, ni = e * jnp.cos(ai) - 1.0, e * jnp.sin(ai)
    den = lre2 * lre2 + lim2 * lim2
    cr, ci = (nr * lre2 + ni * lim2) / den, (ni * lre2 - nr * lim2) / den
    bbn = cr * bn + (ci * sg) * bs
    bbs = cr * bs - (ci * sg) * bn

    def powers(ex):
        m, ang = jnp.exp(ex * ar), ex * ai
        return m * jnp.cos(ang), m * jnp.sin(ang) * sg

    def times(tabs, xn, xs):
        f1, f2 = tabs
        return f1[:, :, None, :] * xn[:, None, :, :] + f2[:, :, None, :] * xs[:, None, :, :]

    t = lax.broadcasted_iota(jnp.int32, (1, TC, 1), 1).astype(f32)
    if rev:
        e_src, e_dst, e_out, e_in = t - (TC - 1.0), (TC - 1.0) - t, t, TC - t
    else:
        e_src, e_dst, e_out, e_in = -t, t, (TC - 1.0) - t, t + 1.0
    g = lre2.shape[0]
    flat = lambda a: a.reshape(g, TCP, SB)
    conj = -_lane_sign(4)
    ll = flat(times(powers(e_src), bbn, bbs))
    rr = flat(times(powers(e_dst), cn, cs) * conj)
    mb = flat(times(powers(e_out), bbn, bbs))
    mct = flat(times(powers(e_in), cn, cs) * conj)
    a1, a2 = powers(float(TC))
    row = lax.broadcasted_iota(jnp.int32, (TCP, TCP), 0) // S5P
    col = lax.broadcasted_iota(jnp.int32, (TCP, TCP), 1) // S5P
    mask = jnp.where((col <= row) if rev else (col >= row), 1.0, 0.0)
    m = jnp.concatenate([dnt(ll[j], rr[j])[None] for j in range(g)], axis=0) * mask
    return m, mb, mct, a1, a2


def _gspec(*tail):
    nt = len(tail)
    return pl.BlockSpec((GBK,) + tail, lambda i, _n=nt: (i,) + (0,) * _n)


def _s5_build(params, rev, name):
    def body(l1, l2, ls, bn, bs, cn, cs, m_ref, mb_ref, mc_ref, a1_ref, a2_ref):
        m, mb, mct, a1, a2 = _s5_build_fn(l1[...], l2[...], ls[...], bn[...], bs[...], cn[...], cs[...], rev)
        m_ref[...], mb_ref[...], mc_ref[...] = m.astype(bf16), mb.astype(bf16), mct.astype(bf16)
        a1_ref[...], a2_ref[...] = a1, a2

    vec, pm = _gspec(1, SB), _gspec(S5P, SB)
    return pl.pallas_call(
        body, name=name, grid=(S5G // GBK,),
        in_specs=[vec, vec, _gspec(1, 1), pm, pm, pm, pm],
        out_specs=[_gspec(TCP, TCP), _gspec(TCP, SB), _gspec(TCP, SB), vec, vec],
        out_shape=[jax.ShapeDtypeStruct((S5G, TCP, TCP), bf16), jax.ShapeDtypeStruct((S5G, TCP, SB), bf16),
                   jax.ShapeDtypeStruct((S5G, TCP, SB), bf16), jax.ShapeDtypeStruct((S5G, 1, SB), f32),
                   jax.ShapeDtypeStruct((S5G, 1, SB), f32)],
        compiler_params=_CP(dimension_semantics=("parallel",)),
    )(*params)


def _s5_build_bwd(params, cots, prev, rev, name):
    def body(l1, l2, ls, bn, bs, cn, cs, dm, dmb, dmc, da1, da2, pb, pc, gl1, gl2, gls, gb, gc):
        prim = (l1[...], l2[...], ls[...], bn[...], bs[...], cn[...], cs[...])
        _, vjp = jax.vjp(functools.partial(_s5_build_fn, rev=rev), *prim)
        d1, d2, dls, dbn, dbs, dcn, dcs = vjp((dm[...], dmb[...], dmc[...], da1[...], da2[...]))
        gl1[...] = d1 + pltpu.roll(d1, S5N, axis=2)
        gl2[...] = d2 + pltpu.roll(d2, S5N, axis=2)
        gls[...] = dls
        gb[...] = dbn + pltpu.roll(dbs, S5N, axis=2) + pb[...]
        gc[...] = dcn + pltpu.roll(dcs, S5N, axis=2) + pc[...]

    vec, pm, big = _gspec(1, SB), _gspec(S5P, SB), _gspec(TCP, SB)
    return pl.pallas_call(
        body, name=name, grid=(S5G // GBK,),
        in_specs=[vec, vec, _gspec(1, 1), pm, pm, pm, pm, _gspec(TCP, TCP), big, big, vec, vec, pm, pm],
        out_specs=[vec, vec, _gspec(1, 1), pm, pm],
        out_shape=[jax.ShapeDtypeStruct((S5G, 1, SB), f32), jax.ShapeDtypeStruct((S5G, 1, SB), f32),
                   jax.ShapeDtypeStruct((S5G, 1, 1), f32), jax.ShapeDtypeStruct((S5G, S5P, SB), f32),
                   jax.ShapeDtypeStruct((S5G, S5P, SB), f32)],
        compiler_params=_CP(dimension_semantics=("parallel",)),
    )(*params, *cots, *prev)


def _s5_inc(u, mb_f, mb_b, name):
    nc = u.shape[1]

    def body(u_ref, mf_ref, mb_ref, sf_ref, sb_ref):
        for j in range(GBK):
            sf_ref[:, j, :] = jnp.dot(u_ref[j], mf_ref[j], preferred_element_type=f32)
            sb_ref[:, j, :] = jnp.dot(u_ref[j], mb_ref[j], preferred_element_type=f32)

    sspec = pl.BlockSpec((nc, GBK, SB), lambda i: (0, i, 0))
    return pl.pallas_call(
        body, name=name, grid=(S5G // GBK,), in_specs=[_gspec(nc, TCP), _gspec(TCP, SB), _gspec(TCP, SB)],
        out_specs=[sspec, sspec], out_shape=[jax.ShapeDtypeStruct((nc, S5G, SB), f32)] * 2,
        compiler_params=_CP(dimension_semantics=("parallel",)),
    )(u, mb_f, mb_b)


def _idx_fwd(nctx, nch):
    return lambda i: i


def _idx_rev(nctx, nch):
    return lambda i: jnp.where(i < nctx, nctx - 1 - i, nch + nctx - 1 - i)


def _carry_loop(nc, step, init):
    def trip(i, c):
        for k in range(CARRY_UNROLL):
            c = step(i * CARRY_UNROLL + k, c)
        return c

    return lax.fori_loop(0, nc // CARRY_UNROLL, trip, init)


def _s5_carry(s_f, s_b, a_f, a_b, nctx, name):
    nc = s_f.shape[0]
    idx_b = _idx_rev(nctx, nc)

    def body(sf_ref, sb_ref, f1_ref, f2_ref, b1_ref, b2_ref, hf_ref, hb_ref):
        f1, f2, b1, b2 = f1_ref[...], f2_ref[...], b1_ref[...], b2_ref[...]

        def step(i, c):
            hf, hfs, hb, hbs = c
            rb = idx_b(i)
            hf_ref[i] = hf
            hb_ref[rb] = hb
            sf, sb = sf_ref[i], sb_ref[rb]
            return (f1 * hf + f2 * hfs + sf, f1 * hfs - f2 * hf + pltpu.roll(sf, S5N, axis=1),
                    b1 * hb + b2 * hbs + sb, b1 * hbs - b2 * hb + pltpu.roll(sb, S5N, axis=1))

        z = jnp.zeros((S5G, SB), f32)
        _carry_loop(nc, step, (z, z, z, z))

    return pl.pallas_call(body, name=name, out_shape=[jax.ShapeDtypeStruct(s_f.shape, f32)] * 2,
                          compiler_params=_CP())(s_f, s_b, *a_f, *a_b)


def _s5_carry_bwd(dhp, hp, a1, a2, rev, nctx, name):
    nc = hp.shape[0]
    idx = (_idx_rev if rev else _idx_fwd)(nctx, nc)

    def body(dhp_ref, hp_ref, a1_ref, a2_ref, ds_ref, d1_ref, d2_ref):
        f1, f2 = a1_ref[...], a2_ref[...]

        def step(k, carry):
            ab, abs_, d1, d2 = carry
            r = idx(nc - 1 - k)
            ds_ref[r] = ab
            h, dh = hp_ref[r], dhp_ref[r]
            return (dh + f1 * ab - f2 * abs_, pltpu.roll(dh, S5N, axis=1) + f1 * abs_ + f2 * ab,
                    d1 + ab * h, d2 + ab * pltpu.roll(h, S5N, axis=1))

        z = jnp.zeros((S5G, SB), f32)
        _, _, d1, d2 = _carry_loop(nc, step, (z, z, z, z))
        d1_ref[...], d2_ref[...] = d1, d2

    return pl.pallas_call(
        body, name=name,
        out_shape=[jax.ShapeDtypeStruct(hp.shape, f32), jax.ShapeDtypeStruct((S5G, SB), f32), jax.ShapeDtypeStruct((S5G, SB), f32)],
        compiler_params=_CP())(dhp, hp, a1, a2)


def _s5_out(u, m_f, m_b, hp_f, hp_b, mc_f, mc_b, name):
    nc = u.shape[1]

    def body(u_ref, mf_ref, mb_ref, hf_ref, hb_ref, cf_ref, cb_ref, y_ref):
        for j in range(GBK):
            uj = u_ref[j]
            y_ref[j] = (jnp.dot(uj, mf_ref[j], preferred_element_type=f32) + jnp.dot(uj, mb_ref[j], preferred_element_type=f32)
                        + dnt(hf_ref[:, j, :], cf_ref[j]) + dnt(hb_ref[:, j, :], cb_ref[j])).astype(bf16)

    sspec = pl.BlockSpec((nc, GBK, SB), lambda i: (0, i, 0))
    return pl.pallas_call(
        body, name=name, grid=(S5G // GBK,),
        in_specs=[_gspec(nc, TCP), _gspec(TCP, TCP), _gspec(TCP, TCP), sspec, sspec, _gspec(TCP, SB), _gspec(TCP, SB)],
        out_specs=_gspec(nc, TCP), out_shape=jax.ShapeDtypeStruct((S5G, nc, TCP), bf16),
        compiler_params=_CP(dimension_semantics=("parallel",)),
    )(u, m_f, m_b, hp_f, hp_b, mc_f, mc_b)


def _s5_out_bwd(dy, u, m_f, m_b, hp_f, hp_b, mc_f, mc_b, name):
    nc = u.shape[1]

    def body(dy_ref, u_ref, mf_ref, mb_ref, hf_ref, hb_ref, cf_ref, cb_ref, du_ref, g_ref, dhf_ref, dhb_ref, dcf_ref, dcb_ref):
        for j in range(GBK):
            dyj = dy_ref[j]
            du_ref[j] = dnt(dyj, mf_ref[j]) + dnt(dyj, mb_ref[j])
            g_ref[j] = dtn(u_ref[j], dyj)
            dhf_ref[:, j, :] = dnn(dyj, cf_ref[j])
            dhb_ref[:, j, :] = dnn(dyj, cb_ref[j])
            dcf_ref[j] = dtn(dyj, hf_ref[:, j, :])
            dcb_ref[j] = dtn(dyj, hb_ref[:, j, :])

    sspec = pl.BlockSpec((nc, GBK, SB), lambda i: (0, i, 0))
    sshape = jax.ShapeDtypeStruct((nc, S5G, SB), f32)
    cshape = jax.ShapeDtypeStruct((S5G, TCP, SB), f32)
    return pl.pallas_call(
        body, name=name, grid=(S5G // GBK,),
        in_specs=[_gspec(nc, TCP), _gspec(nc, TCP), _gspec(TCP, TCP), _gspec(TCP, TCP), sspec, sspec, _gspec(TCP, SB), _gspec(TCP, SB)],
        out_specs=[_gspec(nc, TCP), _gspec(TCP, TCP), sspec, sspec, _gspec(TCP, SB), _gspec(TCP, SB)],
        out_shape=[jax.ShapeDtypeStruct((S5G, nc, TCP), f32), jax.ShapeDtypeStruct((S5G, TCP, TCP), f32), sshape, sshape, cshape, cshape],
        compiler_params=_CP(dimension_semantics=("parallel",)),
    )(dy, u, m_f, m_b, hp_f, hp_b, mc_f, mc_b)


def _s5_inc_bwd(du1, u, ds_f, ds_b, mb_f, mb_b, name):
    nc = u.shape[1]

    def body(du1_ref, u_ref, dsf_ref, dsb_ref, mf_ref, mb_ref, du_ref, dmf_ref, dmb_ref):
        for j in range(GBK):
            dsf, dsb = dsf_ref[:, j, :], dsb_ref[:, j, :]
            du_ref[j] = (du1_ref[j] + dnt(dsf, mf_ref[j]) + dnt(dsb, mb_ref[j])).astype(bf16)
            dmf_ref[j] = dtn(u_ref[j], dsf)
            dmb_ref[j] = dtn(u_ref[j], dsb)

    sspec = pl.BlockSpec((nc, GBK, SB), lambda i: (0, i, 0))
    cshape = jax.ShapeDtypeStruct((S5G, TCP, SB), f32)
    return pl.pallas_call(
        body, name=name, grid=(S5G // GBK,),
        in_specs=[_gspec(nc, TCP), _gspec(nc, TCP), sspec, sspec, _gspec(TCP, SB), _gspec(TCP, SB)],
        out_specs=[_gspec(nc, TCP), _gspec(TCP, SB), _gspec(TCP, SB)],
        out_shape=[jax.ShapeDtypeStruct((S5G, nc, TCP), bf16), cshape, cshape],
        compiler_params=_CP(dimension_semantics=("parallel",)),
    )(du1, u, ds_f, ds_b, mb_f, mb_b)


def _to_groups(a):
    n = a.shape[0]
    return a.reshape(n // TC, TC, S5G, S5P).transpose(2, 0, 1, 3).reshape(S5G, n // TC, TCP)


def _from_groups(a):
    nc = a.shape[1]
    return a.reshape(S5G, nc, TC, S5P).transpose(1, 2, 0, 3).reshape(nc * TC, S5W)


def _swap_pairs(t):
    lane = lax.broadcasted_iota(jnp.int32, t.shape, 1)
    return jnp.where(lane % 2 == 0, pltpu.roll(t, DH - 1, axis=1), pltpu.roll(t, 1, axis=1))


def _rot(t, cosf, sins):
    return t * cosf + _swap_pairs(t) * sins


def _rot_t(d, cosf, sins):
    return d * cosf - _swap_pairs(d) * sins


def _ret_chunk(qr, kr, v, rp, ld, rev):
    pos = lax.broadcasted_iota(jnp.int32, (T, 1), 0).astype(f32)
    diff = pos - lax.broadcasted_iota(jnp.int32, (1, T), 1).astype(f32)
    if rev:
        keep, dist = diff < 0, jnp.maximum(-diff, 0.0)
        xi, zeta = jnp.exp(ld * (T - pos)), jnp.exp(ld * pos)
    else:
        keep, dist = diff >= 0, jnp.maximum(diff, 0.0)
        xi, zeta = jnp.exp(ld * (pos + 1.0)), jnp.exp(ld * (T - 1.0 - pos))
    dm = jnp.where(keep, jnp.exp(ld * dist), 0.0)
    out = dnn(dnt(qr, kr) * dm, v) + dnn(qr * xi, rp)
    rn = jnp.exp(ld * float(T)) * rp + dtn(kr * zeta, v)
    return out, rn


def _ret_fwd(p_ext, ld8, rev, nctx, name, cargo=None):
    n = p_ext.shape[0]
    nch = n // T
    idx = (_idx_rev if rev else _idx_fwd)(nctx, nch)
    cg = _Cargo(cargo)

    def body(*refs):
        (q_ref, k_ref, v_ref, ld_ref), (o_ref, rp_ref), (r_s,) = cg.split(refs, 4, 2, 1)
        cg.ride(refs, 4, 2, nch)

        @pl.when(pl.program_id(0) == 0)
        def _():
            r_s[...] = jnp.zeros_like(r_s)

        for h in range(RH):
            sl = slice(h * DH, (h + 1) * DH)
            rp = r_s[h]
            rp_ref[0, h] = rp
            out, rn = _ret_chunk(q_ref[:, sl].astype(f32), k_ref[:, sl].astype(f32), v_ref[:, sl].astype(f32), rp,
                                 ld_ref[h:h + 1, 0:1], rev)
            r_s[h] = rn
            o_ref[:, sl] = out

    def colspec(cb):
        return pl.BlockSpec((T, RW), lambda i, _c=cb: (idx(i), _c))

    return pl.pallas_call(
        body, name=name, grid=(nch,),
        in_specs=[colspec(1), colspec(2), colspec(3), _const_spec((8, 128))] + cg.in_specs(),
        out_specs=[pl.BlockSpec((T, RW), lambda i: (idx(i), 0)), pl.BlockSpec((1, RH, DH, DH), lambda i: (i, 0, 0, 0))] + cg.in_specs(),
        out_shape=[jax.ShapeDtypeStruct((n, RW), f32), jax.ShapeDtypeStruct((nch, RH, DH, DH), f32)] + cg.out_shapes(),
        scratch_shapes=[pltpu.VMEM((RH, DH, DH), f32)] + cg.sems(),
        compiler_params=_CP(dimension_semantics=_ARB),
    )(p_ext, p_ext, p_ext, ld8, *cg.arrays)


def _ret_bwd(p_ext, ld8, rprev, do_ext, rev, nctx, name, cargo=None):
    n = p_ext.shape[0]
    nch = n // T
    idx0 = (_idx_rev if rev else _idx_fwd)(nctx, nch)
    idx = lambda j: idx0(nch - 1 - j)
    cg = _Cargo(cargo)

    def body(*refs):
        ins, (dq_ref, dk_ref, dv_ref, dld_ref), (dr_s,) = cg.split(refs, 6, 4, 1)
        q_ref, k_ref, v_ref, ld_ref, rp_ref, do_ref = ins
        cg.ride(refs, 6, 4, nch)

        @pl.when(pl.program_id(0) == 0)
        def _():
            dr_s[...] = jnp.zeros_like(dr_s)
            dld_ref[...] = jnp.zeros_like(dld_ref)

        for h in range(RH):
            sl = slice(h * DH, (h + 1) * DH)
            _, vjp = jax.vjp(functools.partial(_ret_chunk, rev=rev), q_ref[:, sl].astype(f32), k_ref[:, sl].astype(f32),
                             v_ref[:, sl].astype(f32), rp_ref[0, h], ld_ref[h:h + 1, 0:1])
            dqr, dkr, dv, drp, dld = vjp((do_ref[:, sl], dr_s[h]))
            dr_s[h] = drp
            dq_ref[:, sl], dk_ref[:, sl], dv_ref[:, sl] = dqr, dkr, dv
            dld_ref[h:h + 1, :] += jnp.broadcast_to(dld, (1, 128))

    def colspec(cb):
        return pl.BlockSpec((T, RW), lambda j, _c=cb: (idx(j), _c))

    ospec = pl.BlockSpec((T, RW), lambda j: (idx(j), 0))
    oshape = jax.ShapeDtypeStruct((n, RW), f32)
    return pl.pallas_call(
        body, name=name, grid=(nch,),
        in_specs=[colspec(1), colspec(2), colspec(3), _const_spec((8, 128)),
                  pl.BlockSpec((1, RH, DH, DH), lambda j: (nch - 1 - j, 0, 0, 0)), ospec] + cg.in_specs(),
        out_specs=[ospec, ospec, ospec, _acc_spec((8, 128))] + cg.in_specs(),
        out_shape=[oshape, oshape, oshape, jax.ShapeDtypeStruct((8, 128), f32)] + cg.out_shapes(),
        scratch_shapes=[pltpu.VMEM((RH, DH, DH), f32)] + cg.sems(),
        compiler_params=_CP(dimension_semantics=_ARB),
    )(p_ext, p_ext, p_ext, ld8, rprev, do_ext, *cg.arrays)


def _qk_heads(p, fn_q, fn_k):
    heads = lambda base, fn: [fn(p[:, base + h * DH:base + (h + 1) * DH]) for h in range(RH)]
    return jnp.concatenate([p[:, :S5W]] + heads(S5W, fn_q) + heads(S5W + RW, fn_k) + [p[:, S5W + 2 * RW:]], axis=1)


def _f1_fwd(x, ctx, modx, modc, nw1, w_in_n, cosf, sins, name, cargo=None):
    L = x.shape[0]
    nb = L // R + 1
    scale = DH ** -0.5
    cg = _Cargo(cargo)

    def body(*refs):
        (x_ref, c_ref, mx_ref, mc_ref, nw_ref, w_ref, cos_ref, sin_ref), (p_ref,), _ = cg.split(refs, 8, 1, 0)
        cg.ride(refs, 8, 1, nb)
        is_ctx = pl.program_id(0) == 0
        xin = jnp.where(is_ctx, c_ref[...], x_ref[...])
        sh = jnp.where(is_ctx, mc_ref[0:1], mx_ref[0:1])
        sc = jnp.where(is_ctx, mc_ref[1:2], mx_ref[1:2])
        cf, ss = cos_ref[...], sin_ref[...]
        p = dnn(_mod(_rms(xin, nw_ref[...]), sh, sc), w_ref[...])
        p_ref[...] = _qk_heads(p, lambda t: _rot(t, cf, ss), lambda t: _rot(t * scale, cf, ss)).astype(bf16)

    return pl.pallas_call(
        body, name=name, grid=(nb,),
        in_specs=[pl.BlockSpec((R, D), lambda i: (jnp.maximum(i - 1, 0), 0)), _const_spec((R, D)), _const_spec((6, D)),
                  _const_spec((6, D)), _const_spec((1, D)), _const_spec((D, INC)), pl.BlockSpec((R, DH), lambda i: (i, 0)),
                  pl.BlockSpec((R, DH), lambda i: (i, 0))] + cg.in_specs(),
        out_specs=[pl.BlockSpec((R, INC), lambda i: (i, 0))] + cg.in_specs(),
        out_shape=[jax.ShapeDtypeStruct((L + R, INC), bf16)] + cg.out_shapes(),
        scratch_shapes=cg.sems(),
        compiler_params=_CP(dimension_semantics=_ARB),
    )(x, ctx, modx, modc, nw1, w_in_n, cosf, sins, *cg.arrays)


def _f1_bwd(x, ctx, modx, modc, nw1, w_in_t, cosf, sins, dx1, parts, name, cargo=None):
    L = x.shape[0]
    nb = L // R + 1
    scale = DH ** -0.5
    cg = _Cargo(cargo)

    def body(*refs):
        ins, (gx_ref, dp_ref, h1_ref, dnw_ref, dmx_ref, dmc_ref), _ = cg.split(refs, 18, 6, 0)
        x_ref, c_ref, mx_ref, mc_ref, nw_ref, w_ref, cos_ref, sin_ref, dx1_ref, du0, du1, dq0, dq1, dk0, dk1, dv0, dv1, dg0 = ins
        cg.ride(refs, 18, 6, nb)
        i = pl.program_id(0)
        is_ctx = i == 0

        @pl.when(is_ctx)
        def _():
            dnw_ref[...] = jnp.zeros_like(dnw_ref)
            dmx_ref[...] = jnp.zeros_like(dmx_ref)
            dmc_ref[...] = jnp.zeros_like(dmc_ref)

        cf, ss = cos_ref[...], sin_ref[...]
        dp = jnp.concatenate([du0[...].astype(f32) + du1[...], dq0[...] + dq1[...], dk0[...] + dk1[...], dv0[...] + dv1[...],
                              dg0[...]], axis=1)
        dp = _qk_heads(dp, lambda t: _rot_t(t, cf, ss), lambda t: _rot_t(t, cf, ss) * scale).astype(bf16)
        dp_ref[...] = dp
        xin = jnp.where(is_ctx, c_ref[...], x_ref[...])
        sh = jnp.where(is_ctx, mc_ref[0:1], mx_ref[0:1])
        sc = jnp.where(is_ctx, mc_ref[1:2], mx_ref[1:2])
        dh = dnn(dp, w_ref[...])
        h, vjp = jax.vjp(lambda a, b, c, d: _mod(_rms(a, b), c, d), xin, nw_ref[...], sh, sc)
        dxin, dnw, dsh, dsc = vjp(dh)
        h1_ref[...] = h.astype(bf16)
        gx_ref[...] = dx1_ref[...] + dxin
        dnw_ref[...] += dnw
        wx = jnp.where(is_ctx, 0.0, 1.0)
        dmx_ref[0:1] += dsh * wx
        dmx_ref[1:2] += dsc * wx
        dmc_ref[0:1] += dsh * (1.0 - wx)
        dmc_ref[1:2] += dsc * (1.0 - wx)

    lat = pl.BlockSpec((R, D), lambda i: (jnp.maximum(i - 1, 0), 0))
    ext = pl.BlockSpec((R, S5W), lambda i: (i, 0))
    return pl.pallas_call(
        body, name=name, grid=(nb,),
        in_specs=[lat, _const_spec((R, D)), _const_spec((6, D)), _const_spec((6, D)), _const_spec((1, D)), _const_spec((INC, D)),
                  pl.BlockSpec((R, DH), lambda i: (i, 0)), pl.BlockSpec((R, DH), lambda i: (i, 0)), lat] + [ext] * 9 + cg.in_specs(),
        out_specs=[lat, pl.BlockSpec((R, INC), lambda i: (i, 0)), pl.BlockSpec((R, D), lambda i: (i, 0)),
                   _acc_spec((1, D)), _acc_spec((6, D)), _acc_spec((6, D))] + cg.in_specs(),
        out_shape=[jax.ShapeDtypeStruct((L, D), f32), jax.ShapeDtypeStruct((L + R, INC), bf16),
                   jax.ShapeDtypeStruct((L + R, D), bf16), jax.ShapeDtypeStruct((1, D), f32),
                   jax.ShapeDtypeStruct((6, D), f32), jax.ShapeDtypeStruct((6, D), f32)] + cg.out_shapes(),
        scratch_shapes=cg.sems(),
        compiler_params=_CP(dimension_semantics=_ARB),
    )(x, ctx, modx, modc, nw1, w_in_t, cosf, sins, dx1, *parts, *cg.arrays)


def _ret_post(yr, g):
    outs = []
    for h in range(RH):
        yh = yr[:, h * DH:(h + 1) * DH]
        mu = jnp.mean(yh, axis=-1, keepdims=True)
        var = jnp.mean((yh - mu) ** 2, axis=-1, keepdims=True)
        outs.append((yh - mu) * lax.rsqrt(var + EPS))
    return jax.nn.silu(g) * jnp.concatenate(outs, axis=1)


def _mix_fn(ys, u, of, ob, g, x, dvec, bglu, gate1, pz, pm, wglu, wout):
    s = jax.nn.gelu(ys + dvec * u)
    z = dnn(s, wglu) + bglu + pz
    cat = jnp.concatenate([s * jax.nn.sigmoid(z), _ret_post(of + ob, g)], axis=1)
    mix = dnn(cat, wout) + pm
    return x + gate1 * mix, (s, cat)


def _mix_fwd(x, ys, of, ob, p_ext, dvec, bglu, modx, wglu, wout, name, cargo=None):
    L = x.shape[0]
    nb = L // R
    cg = _Cargo(cargo)

    def body(*refs):
        ins, (x1_ref,), _ = cg.split(refs, 11, 1, 0)
        x_ref, ys_ref, of_ref, ob_ref, u_ref, g_ref, d_ref, b_ref, mx_ref, wg_ref, wo_ref = ins
        cg.ride(refs, 11, 1, nb)
        x1_ref[...] = _mix_fn(ys_ref[...].astype(f32), u_ref[...].astype(f32), of_ref[...], ob_ref[...], g_ref[...].astype(f32),
                              x_ref[...], d_ref[...], b_ref[...], mx_ref[2:3], 0.0, 0.0, wg_ref[...], wo_ref[...])[0]

    ext = pl.BlockSpec((R, S5W), lambda i: (i + 1, 0))
    return pl.pallas_call(
        body, name=name, grid=(nb,),
        in_specs=[pl.BlockSpec((R, D), lambda i: (i, 0)), ext, ext, ext, ext, pl.BlockSpec((R, RW), lambda i: (i + 1, 4)),
                  _const_spec((1, S5W)), _const_spec((1, S5W)), _const_spec((6, D)), _const_spec((S5W, S5W)), _const_spec((D, D))]
        + cg.in_specs(),
        out_specs=[pl.BlockSpec((R, D), lambda i: (i, 0))] + cg.in_specs(),
        out_shape=[jax.ShapeDtypeStruct((L, D), f32)] + cg.out_shapes(),
        scratch_shapes=cg.sems(),
        compiler_params=_CP(dimension_semantics=_ARB),
    )(x, ys, of, ob, p_ext, p_ext, dvec, bglu, modx, wglu, wout, *cg.arrays)


def _mix_bwd(x, ys, of, ob, p_ext, dvec, bglu, modx, wglu, wout, dx1, name, cargo=None):
    L = x.shape[0]
    nb = L // R + 1
    cg = _Cargo(cargo)

    def body(*refs):
        ins, outs, _ = cg.split(refs, 12, 11, 0)
        x_ref, ys_ref, of_ref, ob_ref, u_ref, g_ref, d_ref, b_ref, mx_ref, wg_ref, wo_ref, dx1_ref = ins
        dy_ref, dud_ref, do_ref, dg_ref, cat_ref, dmix_ref, s_ref, dz_ref, dd_ref, db_ref, dg1_ref = outs
        cg.ride(refs, 12, 11, nb)
        i = pl.program_id(0)

        @pl.when(i == 0)
        def _():
            for r in outs:
                r[...] = jnp.zeros_like(r)

        @pl.when(i > 0)
        def _():
            fn = lambda ys_, u_, of_, g_, d_, b_, g1_, pz_, pm_: _mix_fn(
                ys_, u_, of_, ob_ref[...], g_, x_ref[...], d_, b_, g1_, pz_, pm_, wg_ref[...], wo_ref[...])
            _, vjp, (s, cat) = jax.vjp(fn, ys_ref[...].astype(f32), u_ref[...].astype(f32), of_ref[...], g_ref[...].astype(f32), d_ref[...],
                                       b_ref[...], mx_ref[2:3], jnp.zeros((R, S5W), f32), jnp.zeros((R, D), f32), has_aux=True)
            dy, dud, do, dg, dd, db, dg1, dz, dmix = vjp(dx1_ref[...])
            dy_ref[...], dud_ref[...], do_ref[...], dg_ref[...] = dy.astype(bf16), dud, do, dg
            cat_ref[...], dmix_ref[...] = cat.astype(bf16), dmix.astype(bf16)
            s_ref[...], dz_ref[...] = s.astype(bf16), dz.astype(bf16)
            dd_ref[...] += dd
            db_ref[...] += db
            dg1_ref[...] += dg1

    lat = pl.BlockSpec((R, D), lambda i: (jnp.maximum(i - 1, 0), 0))
    lat5 = pl.BlockSpec((R, S5W), lambda i: (jnp.maximum(i - 1, 0), 0))
    ext = pl.BlockSpec((R, S5W), lambda i: (i, 0))
    eshape = jax.ShapeDtypeStruct((L + R, S5W), f32)
    return pl.pallas_call(
        body, name=name, grid=(nb,),
        in_specs=[lat, ext, ext, ext, ext, pl.BlockSpec((R, RW), lambda i: (i, 4)),
                  _const_spec((1, S5W)), _const_spec((1, S5W)), _const_spec((6, D)), _const_spec((S5W, S5W)), _const_spec((D, D)), lat]
        + cg.in_specs(),
        out_specs=[ext, ext, ext, ext, lat, lat, lat5, lat5, _acc_spec((1, S5W)), _acc_spec((1, S5W)), _acc_spec((1, D))]
        + cg.in_specs(),
        out_shape=[jax.ShapeDtypeStruct((L + R, S5W), bf16), eshape, eshape, eshape, jax.ShapeDtypeStruct((L, D), bf16),
                   jax.ShapeDtypeStruct((L, D), bf16), jax.ShapeDtypeStruct((L, S5W), bf16), jax.ShapeDtypeStruct((L, S5W), bf16),
                   jax.ShapeDtypeStruct((1, S5W), f32), jax.ShapeDtypeStruct((1, S5W), f32), jax.ShapeDtypeStruct((1, D), f32)]
        + cg.out_shapes(),
        scratch_shapes=cg.sems(),
        compiler_params=_CP(dimension_semantics=_ARB),
    )(x, ys, of, ob, p_ext, p_ext, dvec, bglu, modx, wglu, wout, dx1, *cg.arrays)


def _ffn_tail(gc, a, x1, gate2, fnw, pf, wdown, wdown_t, tgt):
    f = jax.nn.gelu(gc) * a
    ffn = _dnn_const(f, wdown, wdown_t) + pf
    y = _rms(x1 + gate2 * ffn, fnw)
    err = y - tgt
    loss = 0.5 * jnp.sum(jnp.mean(err * err, axis=-1, keepdims=True), axis=0, keepdims=True)
    return loss, f


def _ffn_fwd(x1, tgt, nw2, modx, w_a, w_g, cw, cb, wdown, wdown_t, fnw, name):
    L = x1.shape[0]
    nb = L // RF
    per = RF // HALO

    def body(x_ref, xp_ref, xn_ref, t_ref, nw_ref, mx_ref, wa_ref, wg_ref, cw_ref, cb_ref, wd_ref, wdt_ref, fn_ref,
             dx2_ref, da_ref, dgc_ref, f_ref, dffn_ref, loss_ref, dfn_ref, dg2_ref, dcb_ref, dcw_ref):
        i = pl.program_id(0)

        @pl.when(i == 0)
        def _():
            for r in (loss_ref, dfn_ref, dg2_ref, dcb_ref, dcw_ref):
                r[...] = jnp.zeros_like(r)

        nw, sh, sc, gate2 = nw_ref[...], mx_ref[3:4], mx_ref[4:5], mx_ref[5:6]
        x1b = x_ref[...]
        h2 = _mod(_rms(x1b, nw), sh, sc)
        h2e = jnp.concatenate([_mod(_rms(xp_ref[...], nw), sh, sc), h2, _mod(_rms(xn_ref[...], nw), sh, sc)], axis=0)
        a = dnn(h2, wa_ref[...])
        ge = dnn(h2e, wg_ref[...])
        g = ge[HALO:HALO + RF]
        gp = ge[HALO - 1:HALO] * jnp.where(i > 0, 1.0, 0.0)
        gn = ge[HALO + RF:HALO + RF + 1] * jnp.where(i < nb - 1, 1.0, 0.0)
        row = lax.broadcasted_iota(jnp.int32, (RF, 1), 0)
        g_prev = jnp.where(row == 0, gp, pltpu.roll(g, 1, axis=0))
        g_next = jnp.where(row == RF - 1, gn, pltpu.roll(g, RF - 1, axis=0))
        gc = cb_ref[...] + g_prev * cw_ref[0:1] + g * cw_ref[1:2] + g_next * cw_ref[2:3]
        fn = lambda gc_, a_, x_, g2_, fw_, pf_: _ffn_tail(gc_, a_, x_, g2_, fw_, pf_, wd_ref[...], wdt_ref[...], t_ref[...])
        loss, vjp, f = jax.vjp(fn, gc, a, x1b, gate2, fn_ref[...], jnp.zeros((RF, D), f32), has_aux=True)
        dgc, da, dx2, dg2, dfw, dffn = vjp(jnp.ones((1, 1), f32))
        dx2_ref[...] = dx2
        da_ref[...], dgc_ref[...] = da.astype(bf16), dgc
        f_ref[...], dffn_ref[...] = f.astype(bf16), dffn.astype(bf16)
        loss_ref[...] += jnp.broadcast_to(loss, (1, 128))
        dfn_ref[...] += dfw
        dg2_ref[...] += dg2
        dcb_ref[...] += jnp.sum(dgc, axis=0, keepdims=True)
        dcw_ref[0:1] += jnp.sum(dgc * g_prev, axis=0, keepdims=True)
        dcw_ref[1:2] += jnp.sum(dgc * g, axis=0, keepdims=True)
        dcw_ref[2:3] += jnp.sum(dgc * g_next, axis=0, keepdims=True)

    blk = lambda w: pl.BlockSpec((RF, w), lambda i: (i, 0))
    return pl.pallas_call(
        body, name=name, grid=(nb,),
        in_specs=[blk(D), pl.BlockSpec((HALO, D), lambda i: (jnp.maximum(i * per - 1, 0), 0)),
                  pl.BlockSpec((HALO, D), lambda i: (jnp.minimum((i + 1) * per, L // HALO - 1), 0)), blk(D),
                  _const_spec((1, D)), _const_spec((6, D)), _const_spec((D, DFF)), _const_spec((D, DFF)), _const_spec((3, DFF)),
                  _const_spec((1, DFF)), _const_spec((DFF, D)), _const_spec((D, DFF)), _const_spec((1, D))],
        out_specs=[blk(D), blk(DFF), blk(DFF), blk(DFF), blk(D), _acc_spec((1, 128)), _acc_spec((1, D)), _acc_spec((1, D)),
                   _acc_spec((1, DFF)), _acc_spec((3, DFF))],
        out_shape=[jax.ShapeDtypeStruct((L, D), f32), jax.ShapeDtypeStruct((L, DFF), bf16), jax.ShapeDtypeStruct((L, DFF), f32),
                   jax.ShapeDtypeStruct((L, DFF), bf16), jax.ShapeDtypeStruct((L, D), bf16), jax.ShapeDtypeStruct((1, 128), f32),
                   jax.ShapeDtypeStruct((1, D), f32), jax.ShapeDtypeStruct((1, D), f32), jax.ShapeDtypeStruct((1, DFF), f32),
                   jax.ShapeDtypeStruct((3, DFF), f32)],
        compiler_params=_CP(dimension_semantics=_ARB),
    )(x1, x1, x1, tgt, nw2, modx, w_a, w_g, cw, cb, wdown, wdown_t, fnw)


def _ffn_bwd(x1, dx2, da, dgc, nw2, modx, wup_t, cw, name):
    L = x1.shape[0]
    nb = L // RF
    per = RF // HALO

    def body(x_ref, dx2_ref, da_ref, dgc_ref, dgp_ref, dgn_ref, nw_ref, mx_ref, wu_ref, cw_ref,
             dx1_ref, dag_ref, h2_ref, dnw_ref, dmx_ref):
        i = pl.program_id(0)

        @pl.when(i == 0)
        def _():
            dnw_ref[...] = jnp.zeros_like(dnw_ref)
            dmx_ref[...] = jnp.zeros_like(dmx_ref)

        dgc_b = dgc_ref[...]
        before = dgp_ref[HALO - 1:HALO] * jnp.where(i > 0, 1.0, 0.0)
        after = dgn_ref[0:1] * jnp.where(i < nb - 1, 1.0, 0.0)
        row = lax.broadcasted_iota(jnp.int32, (RF, 1), 0)
        d_prev = jnp.where(row == 0, before, pltpu.roll(dgc_b, 1, axis=0))
        d_next = jnp.where(row == RF - 1, after, pltpu.roll(dgc_b, RF - 1, axis=0))
        dg = cw_ref[0:1] * d_next + cw_ref[1:2] * dgc_b + cw_ref[2:3] * d_prev
        dag = jnp.concatenate([da_ref[...], dg.astype(bf16)], axis=1)
        dag_ref[...] = dag
        dh2 = dnn(dag, wu_ref[...])
        h2, vjp = jax.vjp(lambda a, b, c, d: _mod(_rms(a, b), c, d), x_ref[...], nw_ref[...], mx_ref[3:4], mx_ref[4:5])
        dxa, dnw, dsh, dsc = vjp(dh2)
        h2_ref[...] = h2.astype(bf16)
        dx1_ref[...] = dx2_ref[...] + dxa
        dnw_ref[...] += dnw
        dmx_ref[3:4] += dsh
        dmx_ref[4:5] += dsc

    blk = lambda w: pl.BlockSpec((RF, w), lambda i: (i, 0))
    return pl.pallas_call(
        body, name=name, grid=(nb,),
        in_specs=[blk(D), blk(D), blk(DFF), blk(DFF), pl.BlockSpec((HALO, DFF), lambda i: (jnp.maximum(i * per - 1, 0), 0)),
                  pl.BlockSpec((HALO, DFF), lambda i: (jnp.minimum((i + 1) * per, L // HALO - 1), 0)),
                  _const_spec((1, D)), _const_spec((6, D)), _const_spec((2 * DFF, D)), _const_spec((3, DFF))],
        out_specs=[blk(D), blk(2 * DFF), blk(D), _acc_spec((1, D)), _acc_spec((6, D))],
        out_shape=[jax.ShapeDtypeStruct((L, D), f32), jax.ShapeDtypeStruct((L, 2 * DFF), bf16), jax.ShapeDtypeStruct((L, D), bf16),
                   jax.ShapeDtypeStruct((1, D), f32), jax.ShapeDtypeStruct((6, D), f32)],
        compiler_params=_CP(dimension_semantics=_ARB),
    )(x1, dx2, da, dgc, dgc, dgc, nw2, modx, wup_t, cw)


def _matmul_tn(a, b, name, cargo=None):
    k, m = a.shape
    n = b.shape[1]
    divs = lambda d: [c for c in range(d, 0, -128) if d % c == 0]
    _, tm, tn = min((m * (n // cn) + n * (m // cm), cm, cn) for cm in divs(m) for cn in divs(n) if cm * cn * 4 <= ACC_TILE_BYTES)
    tk = next(c for c in (512, 768, 256, 128) if k % c == 0)
    nk = k // tk
    grid = (m // tm, n // tn, nk)
    cg = _Cargo(cargo)

    def body(*refs):
        (a_ref, b_ref), (o_ref,), (acc,) = cg.split(refs, 2, 1, 1)
        cg.ride(refs, 2, 1, grid)
        q = pl.program_id(2)

        @pl.when(q == 0)
        def _():
            acc[...] = jnp.zeros_like(acc)

        acc[...] += dtn(a_ref[...], b_ref[...])

        @pl.when(q == nk - 1)
        def _():
            o_ref[...] = acc[...].astype(bf16)

    out = pl.pallas_call(
        body, name=name, grid=grid,
        in_specs=[pl.BlockSpec((tk, tm), lambda i, j, q: (q, i)), pl.BlockSpec((tk, tn), lambda i, j, q: (q, j))] + cg.in_specs(),
        out_specs=[pl.BlockSpec((tm, tn), lambda i, j, q: (i, j))] + cg.in_specs(),
        out_shape=[jax.ShapeDtypeStruct((m, n), bf16)] + cg.out_shapes(),
        scratch_shapes=[pltpu.VMEM((tm, tn), f32)] + cg.sems(),
        compiler_params=_CP(dimension_semantics=("arbitrary",) * 3 if cg.n else ("parallel", "parallel", "arbitrary")),
    )(a, b, *cg.arrays)
    return out if cg.n else out[0]


def _adamw_refs(w_ref, g_ref, m_ref, v_ref, d_ref, nm_ref, nv_ref):
    c1, c2 = 1.0 - B1 ** STEP, 1.0 - B2 ** STEP
    gg = g_ref[...]
    nm = B1 * m_ref[...] + (1.0 - B1) * gg
    nv = B2 * v_ref[...] + (1.0 - B2) * jnp.square(gg)
    d_ref[...] = -LR * ((nm / c1) / (jnp.sqrt(nv / c2) + AEPS) + WD * w_ref[...])
    nm_ref[...], nv_ref[...] = nm, nv


def _adamw(w, g, m, v, name):
    def body(*refs):
        _adamw_refs(*refs)

    return pl.pallas_call(body, name=name, out_shape=[jax.ShapeDtypeStruct(w.shape, f32)] * 3, compiler_params=_CP())(w, g, m, v)


def _adamw_many(ws, gs, ms, vs, name):
    n = len(ws)

    def body(*refs):
        for k in range(n):
            _adamw_refs(*[refs[j * n + k] for j in range(7)])

    outs = pl.pallas_call(body, name=name, out_shape=[jax.ShapeDtypeStruct(w.shape, f32) for w in ws] * 3,
                          compiler_params=_CP())(*ws, *gs, *ms, *vs)
    return outs[:n], outs[n:2 * n], outs[2 * n:]


SMALL = ["conv_w", "c_ctx", "norm1_w", "s5_lambda_re_f", "s5_lambda_im_f", "s5_log_step_f", "s5_lambda_re_b", "s5_lambda_im_b",
         "s5_log_step_b", "s5_b_re", "s5_b_im", "s5_c_re", "s5_c_im", "s5_d", "s5_b_glu", "ret_log_decay_f", "ret_log_decay_b",
         "norm2_w", "conv_b", "final_norm_w"]
WEIGHTS = ["c_ctx", "w_mod", "b_mod", "norm1_w", "w_in", "s5_lambda_re_f", "s5_lambda_im_f", "s5_log_step_f", "s5_lambda_re_b",
           "s5_lambda_im_b", "s5_log_step_b", "s5_b_re", "s5_b_im", "s5_c_re", "s5_c_im", "s5_d", "s5_w_glu", "s5_b_glu",
           "ret_log_decay_f", "ret_log_decay_b", "w_out", "norm2_w", "w_up", "conv_w", "conv_b", "w_down", "final_norm_w"]


def _pack_small(vals):
    flat, offs, o = [], [], 0
    for a in vals:
        n = a.size
        npad = -n % 128
        flat.append(jnp.pad(a.reshape(-1), (0, npad)))
        offs.append((o, n))
        o += n + npad
    tail = -o % 1024
    if tail:
        flat.append(jnp.zeros((tail,), f32))
    return jnp.concatenate(flat).reshape(-1, 128), offs


def _unpack_small(packed, offs, shapes):
    flat = packed.reshape(-1)
    return [flat[o:o + n].reshape(s) for (o, n), s in zip(offs, shapes)]


def _rope_tables(L, nctx_rows):
    t = np.arange(L)
    inv = (ROPE_THETA ** (-np.arange(DH // 4, dtype=np.float64) / (DH // 4))).astype(np.float32)
    ang = np.concatenate([(t // GRID_W).astype(np.float32)[:, None] * inv, (t % GRID_W).astype(np.float32)[:, None] * inv], axis=-1)
    cos = np.repeat(np.cos(ang).astype(np.float32), 2, axis=1)
    sin = np.repeat(np.sin(ang).astype(np.float32), 2, axis=1) * np.tile(np.array([-1.0, 1.0], np.float32), DH // 2)
    cosf = np.concatenate([np.ones((nctx_rows, DH), np.float32), cos], axis=0)
    sins = np.concatenate([np.zeros((nctx_rows, DH), np.float32), sin], axis=0)
    return jnp.asarray(cosf), jnp.asarray(sins)


def kernel(x, c, ctx, c_ctx, w_mod, b_mod, norm1_w, w_in, s5_lambda_re_f, s5_lambda_im_f, s5_log_step_f, s5_lambda_re_b, s5_lambda_im_b, s5_log_step_b, s5_b_re, s5_b_im, s5_c_re, s5_c_im, s5_d, s5_w_glu, s5_b_glu, ret_log_decay_f, ret_log_decay_b, w_out, norm2_w, w_up, conv_w, conv_b, w_down, final_norm_w, loss_target, m_c_ctx, m_w_mod, m_b_mod, m_norm1_w, m_w_in, m_s5_lambda_re_f, m_s5_lambda_im_f, m_s5_log_step_f, m_s5_lambda_re_b, m_s5_lambda_im_b, m_s5_log_step_b, m_s5_b_re, m_s5_b_im, m_s5_c_re, m_s5_c_im, m_s5_d, m_s5_w_glu, m_s5_b_glu, m_ret_log_decay_f, m_ret_log_decay_b, m_w_out, m_norm2_w, m_w_up, m_conv_w, m_conv_b, m_w_down, m_final_norm_w, v_c_ctx, v_w_mod, v_b_mod, v_norm1_w, v_w_in, v_s5_lambda_re_f, v_s5_lambda_im_f, v_s5_log_step_f, v_s5_lambda_re_b, v_s5_lambda_im_b, v_s5_log_step_b, v_s5_b_re, v_s5_b_im, v_s5_c_re, v_s5_c_im, v_s5_d, v_s5_w_glu, v_s5_b_glu, v_ret_log_decay_f, v_ret_log_decay_b, v_w_out, v_norm2_w, v_w_up, v_conv_w, v_conv_b, v_w_down, v_final_norm_w):
    args = dict(locals())
    W = {n: args[n] for n in WEIGHTS}
    M = {n: args["m_" + n] for n in WEIGHTS}
    V = {n: args["v_" + n] for n in WEIGHTS}
    me = _me()
    x2, ctx2, tgt = x[0], ctx[0], loss_target[0]
    L, Lc = x2.shape[0], ctx2.shape[0]
    assert Lc == R and L % R == 0 and L % GRID_W == 0
    nctx = Lc // T

    w_in_tl, w_up_tl = w_in[0].T.astype(bf16), w_up[0].T.astype(bf16)
    w_out_l, w_down_l, w_glu_l = w_out[0].astype(bf16), w_down[0].astype(bf16), s5_w_glu[0].astype(bf16)
    half_up = w_up_tl.shape[0] // 2
    per_cv = conv_w.shape[2]
    conv_pad = jnp.pad(conv_w[0], ((0, 5), (0, 128 * 3 - per_cv)))
    w_in_g, c_g, conv_g = _gather_two_level([w_in_tl, jnp.pad(c, ((0, 7), (0, 0))), conv_pad], "gather_w_in")
    w_in_t = w_in_g.reshape(INC, D)
    conv_f = conv_g[:, :3, :per_cv].transpose(1, 0, 2).reshape(3, DFF)

    c9 = jnp.concatenate([c_g[:, 0, :], c_ctx[None], jnp.zeros((7, D), f32)], axis=0)
    w_mod_l = w_mod[0]
    ncol = w_mod_l.shape[1]
    m_part = _ada_fwd(c9, w_mod_l, "ada_fwd")
    m_all = _all_gather_small(m_part, "gather_mod").transpose(1, 0, 2).reshape(16, 6, D)
    modx, modc = _mod_select(m_all, b_mod.reshape(6, D), "mod_select")

    pair = lambda a, b: jnp.concatenate([a, b], axis=-1)
    bre_g, bim_g = s5_b_re[0].transpose(0, 2, 1), s5_b_im[0].transpose(0, 2, 1)
    cre_g, cim_g = s5_c_re[0], s5_c_im[0]
    shared = (pair(bre_g, bim_g), pair(bim_g, bre_g), pair(cre_g, cim_g), pair(cim_g, cre_g))
    s5p = {}
    for tag, lre, lim, ls in (("f", s5_lambda_re_f, s5_lambda_im_f, s5_log_step_f), ("b", s5_lambda_re_b, s5_lambda_im_b, s5_log_step_b)):
        s5p[tag] = (pair(lre[0], lre[0])[:, None, :], pair(lim[0], lim[0])[:, None, :], ls[0].reshape(S5G, 1, 1)) + shared
    m_f, mb_f, mc_f, a1_f, a2_f = _s5_build(s5p["f"], False, "s5_build_f")
    m_b, mb_b, mc_b, a1_b, a2_b = _s5_build(s5p["b"], True, "s5_build_b")
    a1_f, a2_f, a1_b, a2_b = (a.reshape(S5G, SB) for a in (a1_f, a2_f, a1_b, a2_b))

    nw1, nw2, fnw = norm1_w, norm2_w, final_norm_w[None]
    cosf, sins = _rope_tables(L, Lc)
    p_ext, w_out_g, w_glu_g = _f1_fwd(x2, ctx2, modx, modc, nw1, w_in_t.T, cosf, sins, "f1_fwd", cargo=([w_out_l, w_glu_l], False))
    nctx5 = Lc // TC
    u_g = _to_groups(p_ext[:, :S5W])
    s_f, s_b = _s5_inc(u_g, mb_f, mb_b, "s5_inc")
    hp_f, hp_b = _s5_carry(s_f, s_b, (a1_f, a2_f), (a1_b, a2_b), nctx5, "s5_carry")
    ys = _from_groups(_s5_out(u_g, m_f, m_b, hp_f, hp_b, mc_f, mc_b, "s5_out"))
    ld8 = lambda ld: jnp.pad(jnp.broadcast_to(ld[0][:, None], (RH, 128)), ((0, 8 - RH), (0, 0)))
    ldf8, ldb8 = ld8(ret_log_decay_f), ld8(ret_log_decay_b)
    of, rp_f, w_up_g1 = _ret_fwd(p_ext, ldf8, False, nctx, "ret_fwd_f", cargo=([w_up_tl[:half_up]], False))
    ob, rp_b, w_up_g2 = _ret_fwd(p_ext, ldb8, True, nctx, "ret_fwd_b", cargo=([w_up_tl[half_up:]], False))
    w_out_f, w_glu_f = w_out_g.reshape(D, D), w_glu_g.reshape(S5W, S5W)
    x1, w_down_g = _mix_fwd(x2, ys, of, ob, p_ext, s5_d, s5_b_glu, modx, w_glu_f, w_out_f, "mix_fwd", cargo=([w_down_l], False))
    w_down_f = w_down_g.reshape(DFF, D)
    w_up_t = jnp.concatenate([w_up_g1, w_up_g2], axis=1).reshape(2 * DFF, D)

    (dx2, da, dgc, f_act, dffn, loss_acc, g_fnw, g_gate2, g_cb, g_cw) = _ffn_fwd(
        x1, tgt, nw2, modx, w_up_t[:DFF].T, w_up_t[DFF:].T, conv_f, conv_b, w_down_f, w_down_f.T, fnw, "ffn_fwd")
    dx1, dag, h2, g_nw2, dmx2 = _ffn_bwd(x1, dx2, da, dgc, nw2, modx, w_up_t, conv_f, "ffn_bwd")
    gw_down = _matmul_tn(f_act, dffn, "dw_down").reshape(NDEV, -1, D)
    gw_up_t = _matmul_tn(dag, h2, "dw_up").reshape(NDEV, -1, D)
    (dy_e, dud_e, do_e, dg_e, cat, dmix, s_act, dz, g_d, g_bglu, g_gate1, l_down) = _mix_bwd(
        x2, ys, of, ob, p_ext, s5_d, s5_b_glu, modx, w_glu_f, w_out_f, dx1, "mix_bwd", cargo=([gw_down], True))
    gw_out = _matmul_tn(cat, dmix, "dw_out").reshape(NDEV, -1, D)
    gw_glu = _matmul_tn(s_act, dz, "dw_glu").reshape(NDEV, -1, S5W)
    dq_f, dk_f, dv_f, gld_f, l_up1 = _ret_bwd(p_ext, ldf8, rp_f, do_e, False, nctx, "ret_bwd_f",
                                              cargo=([gw_up_t[:, :half_up]], True))
    dq_b, dk_b, dv_b, gld_b, l_out, l_glu, l_up2 = _ret_bwd(p_ext, ldb8, rp_b, do_e, True, nctx, "ret_bwd_b",
                                                            cargo=([gw_out, gw_glu, gw_up_t[:, half_up:]], True))

    du1, g_m, dhp_f, dhp_b, dmc_f, dmc_b = _s5_out_bwd(_to_groups(dy_e), u_g, m_f, m_b, hp_f, hp_b, mc_f, mc_b, "s5_out_bwd")
    ds_f, da1_f, da2_f = _s5_carry_bwd(dhp_f, hp_f, a1_f, a2_f, False, nctx5, "s5_carry_bwd_f")
    ds_b, da1_b, da2_b = _s5_carry_bwd(dhp_b, hp_b, a1_b, a2_b, True, nctx5, "s5_carry_bwd_b")
    du_g, dmb_f, dmb_b = _s5_inc_bwd(du1, u_g, ds_f, ds_b, mb_f, mb_b, "s5_inc_bwd")
    zero_p = jnp.zeros((S5G, S5P, SB), f32)
    gf = _s5_build_bwd(s5p["f"], (g_m, dmb_f, dmc_f, da1_f[:, None, :], da2_f[:, None, :]), (zero_p, zero_p), False, "s5_build_bwd_f")
    gb = _s5_build_bwd(s5p["b"], (g_m, dmb_b, dmc_b, da1_b[:, None, :], da2_b[:, None, :]), (gf[3], gf[4]), True, "s5_build_bwd_b")
    g_bre, g_bim = gb[3][:, :, :S5N].transpose(0, 2, 1), gb[3][:, :, S5N:].transpose(0, 2, 1)
    g_cre, g_cim = gb[4][:, :, :S5N], gb[4][:, :, S5N:]

    early = {
        "conv_w": g_cw, "s5_lambda_re_f": gf[0][:, 0, :S5N], "s5_lambda_im_f": gf[1][:, 0, :S5N],
        "s5_log_step_f": gf[2], "s5_lambda_re_b": gb[0][:, 0, :S5N], "s5_lambda_im_b": gb[1][:, 0, :S5N], "s5_log_step_b": gb[2],
        "s5_b_re": g_bre, "s5_b_im": g_bim, "s5_c_re": g_cre, "s5_c_im": g_cim, "s5_d": g_d, "s5_b_glu": g_bglu,
        "ret_log_decay_f": gld_f[:RH, 0], "ret_log_decay_b": gld_b[:RH, 0], "norm2_w": g_nw2, "conv_b": g_cb, "final_norm_w": g_fnw,
    }
    e_names = [n for n in SMALL if n in early]
    packed_e, eoffs = _pack_small([early[n].astype(f32) for n in e_names])
    grad_x, dp_ext, h1, g_nw1, dmx1, dmc1 = _f1_bwd(
        x2, ctx2, modx, modc, nw1, w_in_t, cosf, sins, dx1, (_from_groups(du_g), dud_e, dq_f, dq_b, dk_f, dk_b, dv_f, dv_b, dg_e), "f1_bwd")
    gw_in_t, land_e = _matmul_tn(dp_ext, h1, "dw_in", cargo=([packed_e], False))
    g_in_t = _reduce_scatter_two_level(gw_in_t.reshape(NDEV, -1, D), "scatter_dw_in")

    dmx = dmx1 + dmx2
    dmx = dmx.at[2].set(g_gate1[0]).at[5].set(g_gate2[0])
    dm_me = jnp.stack([dmx.reshape(-1), dmc1.reshape(-1)], axis=0)
    dm_all = _all_gather_small(dm_me.reshape(8, -1), "gather_dmod").reshape(NDEV, 2, 6 * D)
    dmx_all, dmc_all = dm_all[:, 0, :], dm_all[:, 1, :]
    my_cols = lambda a: lax.dynamic_slice(a, (0, me * ncol), (NDEV, ncol))
    gw_mod, g_bmod, dc9 = _ada_bwd(c9, dmx_all, dmc_all, my_cols(dmx_all), my_cols(dmc_all), w_mod_l, "ada_bwd")

    sshape = lambda n: (3, DFF) if n == "conv_w" else W[n].shape
    G = dict(zip(e_names, _unpack_small(_sum8(land_e, "reduce_early"), eoffs, [sshape(n) for n in e_names])))
    late = {"c_ctx": dc9[8], "norm1_w": g_nw1}
    packed_l, loffs = _pack_small([late[n].astype(f32) for n in late])
    G.update(zip(late, _unpack_small(_all_reduce_small(packed_l, "reduce_late"), loffs, [W[n].shape for n in late])))
    G["conv_w"] = lax.dynamic_slice(G["conv_w"], (0, me * per_cv), (3, per_cv))[None]
    G["b_mod"] = g_bmod.reshape(b_mod.shape)
    G["w_mod"] = gw_mod[None]
    G["w_in"] = g_in_t.T[None]
    G["w_up"] = jnp.concatenate([_sum8(l_up1, "sum_dw_up1"), _sum8(l_up2, "sum_dw_up2")], axis=0).T[None]
    G["w_out"] = _sum8(l_out, "sum_dw_out")[None]
    G["w_down"] = _sum8(l_down, "sum_dw_down")[None]
    G["s5_w_glu"] = _sum8(l_glu, "sum_dw_glu")[None]

    delta, new_m, new_v = {}, {}, {}
    sm_names = SMALL[1:] + ["b_mod"]
    rows = lambda a: a.reshape(-1, a.shape[-1])
    outs = _adamw_many(*[[rows(d[n]) for n in sm_names] for d in (W, G, M, V)], "adamw_small")
    for dst, src in zip((delta, new_m, new_v), outs):
        dst.update({n: a.reshape(W[n].shape) for n, a in zip(sm_names, src)})
    for n in ["w_mod", "w_in", "w_out", "w_up", "w_down", "s5_w_glu", "conv_w"]:
        d, nm, nv = _adamw(W[n][0], G[n][0], M[n][0], V[n][0], "adamw_" + n)
        delta[n], new_m[n], new_v[n] = d[None], nm[None], nv[None]

    loss = lax.psum(loss_acc[0, 0], ("x", "y", "c"))
    return (loss, grad_x[None], *[G[n] for n in WEIGHTS], *[delta[n] for n in WEIGHTS], *[new_m[n] for n in WEIGHTS],
            *[new_v[n] for n in WEIGHTS])
```

```python
import functools

import numpy as np
import jax
import jax.numpy as jnp
from jax import lax
from jax.experimental import pallas as pl
from jax.experimental.pallas import tpu as pltpu

f32, bf16 = jnp.float32, jnp.bfloat16

D = 1024
S5W, S5G, S5P, S5N = 512, 32, 16, 64
TC = 16
TCP = TC * S5P
SB = 2 * S5N
GBK = 8
CARRY_UNROLL = 8
RH, DH = 4, 128
RW = RH * DH
INC = S5W + 4 * RW
DFF = 2816
T = 128
R = 256
RF = 128
HALO = 8
EPS = 1e-6
ROPE_THETA = 10000.0
GRID_W = 64
NDEV = 8
LR, B1, B2, AEPS, WD, STEP = 0.001, 0.9, 0.999, 1e-08, 0.01, 10
VMEM_LIMIT = 60 * 1024 * 1024
ACC_TILE_BYTES = 6 * 1024 * 1024
MESH = pl.DeviceIdType.MESH

_CP = functools.partial(pltpu.CompilerParams, vmem_limit_bytes=VMEM_LIMIT)
_ARB = ("arbitrary",)
_ANY = pl.BlockSpec(memory_space=pl.ANY)


def _dg(a, b, dims):
    return lax.dot_general(a.astype(bf16), b.astype(bf16), (dims, ((), ())), preferred_element_type=f32)


@jax.custom_vjp
def dnn(a, b):
    return _dg(a, b, ((1,), (0,)))


@jax.custom_vjp
def dnt(a, b):
    return _dg(a, b, ((1,), (1,)))


@jax.custom_vjp
def dtn(a, b):
    return _dg(a, b, ((0,), (0,)))


dnn.defvjp(lambda a, b: (dnn(a, b), (a, b)), lambda r, g: (dnt(g, r[1]).astype(r[0].dtype), dtn(r[0], g).astype(r[1].dtype)))
dnt.defvjp(lambda a, b: (dnt(a, b), (a, b)), lambda r, g: (dnn(g, r[1]).astype(r[0].dtype), dtn(g, r[0]).astype(r[1].dtype)))
dtn.defvjp(lambda a, b: (dtn(a, b), (a, b)), lambda r, g: (dnt(r[1], g).astype(r[0].dtype), dnn(r[0], g).astype(r[1].dtype)))


@jax.custom_vjp
def _dnn_const(a, w, wt):
    return dnn(a, w)


_dnn_const.defvjp(lambda a, w, wt: (dnn(a, w), wt), lambda wt, g: (dnn(g, wt), None, None))


_GELU_C0, _GELU_C1 = float(np.sqrt(2.0 / np.pi)), 0.044715


@jax.custom_vjp
def _gelu(x):
    return _gelu_fwd(x)[0]


def _gelu_fwd(x):
    t = jnp.tanh(_GELU_C0 * (x + _GELU_C1 * (x * x * x)))
    return x * (0.5 * (1.0 + t)), (x, t)


def _gelu_bwd(res, g):
    x, t = res
    return (g * (0.5 * (1.0 + t) + (0.5 * _GELU_C0) * x * (1.0 - t * t) * (1.0 + (3.0 * _GELU_C1) * (x * x))),)


_gelu.defvjp(_gelu_fwd, _gelu_bwd)


def _rms(t, w):
    return t * lax.rsqrt(jnp.mean(t * t, axis=-1, keepdims=True) + EPS) * w


def _mod(h, shift, scale):
    return h * (1.0 + scale) + shift


def _const_spec(shape):
    n = len(shape)
    return pl.BlockSpec(shape, lambda i, _n=n: (0,) * _n, pipeline_mode=pl.Buffered(1))


def _acc_spec(shape):
    n = len(shape)
    return pl.BlockSpec(shape, lambda i, _n=n: (0,) * _n)


def _me():
    return 4 * lax.axis_index("x") + 2 * lax.axis_index("y") + lax.axis_index("c")


def _peer(r):
    x, y, c = lax.axis_index("x"), lax.axis_index("y"), lax.axis_index("c")
    px = 1 - x if (r >> 2) & 1 else x
    py = 1 - y if (r >> 1) & 1 else y
    pc = 1 - c if r & 1 else c
    return (px, py, pc), 4 * px + 2 * py + pc


def _all_gather_small(v, name):
    r, c = v.shape

    def body(v_ref, out_ref, send_sems, recv_sems):
        me = _me()
        out_ref[me] = v_ref[...]
        sends = []
        for k in range(1, NDEV):
            peer, _ = _peer(k)
            cp = pltpu.make_async_remote_copy(src_ref=v_ref, dst_ref=out_ref.at[me], send_sem=send_sems.at[k - 1],
                                              recv_sem=recv_sems.at[k - 1], device_id=peer, device_id_type=MESH)
            cp.start()
            sends.append(cp)
        for k in range(1, NDEV):
            peer, pidx = _peer(k)
            pltpu.make_async_remote_copy(src_ref=v_ref, dst_ref=out_ref.at[pidx], send_sem=send_sems.at[k - 1],
                                         recv_sem=recv_sems.at[k - 1], device_id=peer, device_id_type=MESH).wait_recv()
        for cp in sends:
            cp.wait_send()

    return pl.pallas_call(
        body, name=name, out_shape=jax.ShapeDtypeStruct((NDEV, r, c), v.dtype),
        in_specs=[pl.BlockSpec(memory_space=pltpu.VMEM)], out_specs=pl.BlockSpec(memory_space=pltpu.VMEM),
        scratch_shapes=[pltpu.SemaphoreType.DMA((NDEV - 1,)), pltpu.SemaphoreType.DMA((NDEV - 1,))],
        compiler_params=_CP(),
    )(v)


def _all_reduce_small(v, name):
    r, c = v.shape

    def body(v_ref, out_ref, land, send_sems, recv_sems):
        me = _me()
        land[me] = v_ref[...]
        sends = []
        for k in range(1, NDEV):
            peer, _ = _peer(k)
            cp = pltpu.make_async_remote_copy(src_ref=v_ref, dst_ref=land.at[me], send_sem=send_sems.at[k - 1],
                                              recv_sem=recv_sems.at[k - 1], device_id=peer, device_id_type=MESH)
            cp.start()
            sends.append(cp)
        for k in range(1, NDEV):
            peer, pidx = _peer(k)
            pltpu.make_async_remote_copy(src_ref=v_ref, dst_ref=land.at[pidx], send_sem=send_sems.at[k - 1],
                                         recv_sem=recv_sems.at[k - 1], device_id=peer, device_id_type=MESH).wait_recv()
        for cp in sends:
            cp.wait_send()
        acc = land[0]
        for j in range(1, NDEV):
            acc = acc + land[j]
        out_ref[...] = acc

    return pl.pallas_call(
        body, name=name, out_shape=jax.ShapeDtypeStruct((r, c), v.dtype),
        in_specs=[pl.BlockSpec(memory_space=pltpu.VMEM)], out_specs=pl.BlockSpec(memory_space=pltpu.VMEM),
        scratch_shapes=[pltpu.VMEM((NDEV, r, c), v.dtype), pltpu.SemaphoreType.DMA((NDEV - 1,)),
                        pltpu.SemaphoreType.DMA((NDEV - 1,))],
        compiler_params=_CP(),
    )(v)


class _Exchange:
    def __init__(self, srcs, dsts, send_sems, recv_sems, local_sems, scatter):
        me = _me()
        n = len(srcs)
        self.sends, self.recvs, self.locals = [], [], []
        for a, (s, d) in enumerate(zip(srcs, dsts)):
            self.locals.append(pltpu.make_async_copy(s.at[me] if scatter else s, d.at[me], local_sems.at[a]))
        for k in range(1, NDEV):
            peer, pidx = _peer(k)
            for a, (s, d) in enumerate(zip(srcs, dsts)):
                src = s.at[pidx] if scatter else s
                sem = (k - 1) * n + a
                for dst, out in ((d.at[me], self.sends), (d.at[pidx], self.recvs)):
                    out.append(pltpu.make_async_remote_copy(src_ref=src, dst_ref=dst, send_sem=send_sems.at[sem],
                                                            recv_sem=recv_sems.at[sem], device_id=peer, device_id_type=MESH))

    def start(self):
        for cp in self.locals + self.sends:
            cp.start()

    def wait(self):
        for cp in self.recvs:
            cp.wait_recv()
        for cp in self.sends:
            cp.wait_send()
        for cp in self.locals:
            cp.wait()


def _exchange_shapes(arrays, scatter):
    return [jax.ShapeDtypeStruct(a.shape if scatter else (NDEV,) + a.shape, a.dtype) for a in arrays]


def _exchange_sems(n):
    return [pltpu.SemaphoreType.DMA(((NDEV - 1) * n,)), pltpu.SemaphoreType.DMA(((NDEV - 1) * n,)), pltpu.SemaphoreType.DMA((n,))]


def _exchange(arrays, scatter, name):
    n = len(arrays)

    def body(*refs):
        ex = _Exchange(refs[:n], refs[n:2 * n], *refs[2 * n:], scatter)
        ex.start()
        ex.wait()

    return pl.pallas_call(body, name=name, out_shape=_exchange_shapes(arrays, scatter), in_specs=[_ANY] * n,
                          out_specs=[_ANY] * n, scratch_shapes=_exchange_sems(n), compiler_params=_CP())(*arrays)


def _chips():
    x, y, c = lax.axis_index("x"), lax.axis_index("y"), lax.axis_index("c")
    return (x, y, c), (x, y, 1 - c), [(1 - x, y), (x, 1 - y), (1 - x, 1 - y)]


def _gather_two_level(arrays, name):
    n = len(arrays)

    def body(*refs):
        srcs, outs = refs[:n], refs[n:2 * n]
        send_sems, recv_sems = refs[2 * n:]
        me, sibling, chips = _chips()
        c = me[2]
        idx = lambda p: 4 * p[0] + 2 * p[1] + p[2]

        def copy(a, k, block, to, src=None):
            return pltpu.make_async_remote_copy(
                src_ref=outs[a].at[idx(block)] if src is None else src, dst_ref=outs[a].at[idx(block)],
                send_sem=send_sems.at[7 * a + k], recv_sem=recv_sems.at[7 * a + k], device_id=to, device_id_type=MESH)

        first, passed = [], []
        for a in range(n):
            outs[a][idx(me)] = srcs[a][...]
            first += [copy(a, 0, me, sibling, src=srcs[a])]
            first += [copy(a, 1 + j, me, (*chip, c), src=srcs[a]) for j, chip in enumerate(chips)]
        for cp in first:
            cp.start()
        for a in range(n):
            for j, chip in enumerate(chips):
                copy(a, 1 + j, (*chip, c), me).wait_recv()
                cp = copy(a, 4 + j, (*chip, c), sibling)
                cp.start()
                passed.append(cp)
        for a in range(n):
            copy(a, 0, sibling, me).wait_recv()
            for j, chip in enumerate(chips):
                copy(a, 4 + j, (*chip, 1 - c), me).wait_recv()
        for cp in first + passed:
            cp.wait_send()

    vm = pl.BlockSpec(memory_space=pltpu.VMEM)
    return pl.pallas_call(
        body, name=name, out_shape=[jax.ShapeDtypeStruct((NDEV,) + a.shape, a.dtype) for a in arrays],
        in_specs=[vm] * n, out_specs=[vm] * n,
        scratch_shapes=[pltpu.SemaphoreType.DMA((7 * n,)), pltpu.SemaphoreType.DMA((7 * n,))],
        compiler_params=_CP(),
    )(*arrays)


def _reduce_scatter_two_level(g, name):
    _, r, c = g.shape
    nchip = NDEV // 2

    def body(g_ref, o_ref, stage, part, land, d_send, d_recv, i_send, i_recv):
        me, sibling, chips = _chips()
        x, y, cc = me
        mine = 2 * x + y

        def blk(k, core):
            return 2 * k + core

        swaps = [pltpu.make_async_remote_copy(src_ref=g_ref.at[blk(k, 1 - cc)], dst_ref=stage.at[k], send_sem=d_send.at[k],
                                              recv_sem=d_recv.at[k], device_id=sibling, device_id_type=MESH) for k in range(nchip)]
        for cp in swaps:
            cp.start()
        for cp in swaps:
            cp.wait_recv()
        for k in range(nchip):
            part[k] = (g_ref[blk(k, cc)].astype(f32) + stage[k].astype(f32)).astype(bf16)
        sends = []
        for j, chip in enumerate(chips):
            kd = 2 * chip[0] + chip[1]
            cp = pltpu.make_async_remote_copy(src_ref=part.at[kd], dst_ref=land.at[mine], send_sem=i_send.at[j],
                                              recv_sem=i_recv.at[j], device_id=(*chip, cc), device_id_type=MESH)
            cp.start()
            sends.append(cp)
        land[mine] = part[mine]
        for j, chip in enumerate(chips):
            ks = 2 * chip[0] + chip[1]
            pltpu.make_async_remote_copy(src_ref=part.at[ks], dst_ref=land.at[ks], send_sem=i_send.at[j], recv_sem=i_recv.at[j],
                                         device_id=(*chip, cc), device_id_type=MESH).wait_recv()
        for cp in swaps + sends:
            cp.wait_send()
        acc = land[0].astype(f32)
        for k in range(1, nchip):
            acc = acc + land[k].astype(f32)
        o_ref[...] = acc

    vm = pl.BlockSpec(memory_space=pltpu.VMEM)
    return pl.pallas_call(
        body, name=name, out_shape=jax.ShapeDtypeStruct((r, c), f32), in_specs=[vm], out_specs=vm,
        scratch_shapes=[pltpu.VMEM((nchip, r, c), g.dtype)] * 3 + [pltpu.SemaphoreType.DMA((nchip,)), pltpu.SemaphoreType.DMA((nchip,)),
                                                                   pltpu.SemaphoreType.DMA((3,)), pltpu.SemaphoreType.DMA((3,))],
        compiler_params=_CP(),
    )(g)


class _Cargo:
    def __init__(self, cargo):
        self.arrays, self.scatter = cargo if cargo else ([], False)
        self.n = len(self.arrays)

    def in_specs(self):
        return [_ANY] * self.n

    def out_shapes(self):
        return _exchange_shapes(self.arrays, self.scatter)

    def sems(self):
        return _exchange_sems(self.n) if self.n else []

    def split(self, refs, n_in, n_out, n_scratch):
        n = self.n
        return refs[:n_in], refs[n_in + n:n_in + n + n_out], refs[n_in + 2 * n + n_out:n_in + 2 * n + n_out + n_scratch]

    def ride(self, refs, n_in, n_out, grid):
        if not self.n:
            return
        n = self.n
        ex = _Exchange(refs[n_in:n_in + n], refs[n_in + n + n_out:n_in + 2 * n + n_out], *refs[-3:], self.scatter)
        grid = (grid,) if isinstance(grid, int) else tuple(grid)
        first = functools.reduce(jnp.logical_and, [pl.program_id(a) == 0 for a in range(len(grid))])
        last = functools.reduce(jnp.logical_and, [pl.program_id(a) == g - 1 for a, g in enumerate(grid)])

        @pl.when(first)
        def _():
            ex.start()

        @pl.when(last)
        def _():
            ex.wait()


def _sum8(land, name):
    _, r, c = land.shape
    rb = next((b for b in (256, 64, 32) if r % b == 0), r)

    def body(l_ref, o_ref):
        acc = l_ref[0].astype(f32)
        for j in range(1, NDEV):
            acc = acc + l_ref[j].astype(f32)
        o_ref[...] = acc

    return pl.pallas_call(
        body, name=name, grid=(r // rb,), out_shape=jax.ShapeDtypeStruct((r, c), f32),
        in_specs=[pl.BlockSpec((NDEV, rb, c), lambda i: (0, i, 0))], out_specs=pl.BlockSpec((rb, c), lambda i: (i, 0)),
        compiler_params=_CP(dimension_semantics=("parallel",)),
    )(land)


def _ada_fwd(c9, w_mod_l, name):
    def body(c_ref, w_ref, o_ref):
        o_ref[...] = dnn(jax.nn.silu(c_ref[...]), w_ref[...])

    return pl.pallas_call(body, name=name, out_shape=jax.ShapeDtypeStruct((16, w_mod_l.shape[1]), f32),
                          compiler_params=_CP())(c9, w_mod_l)


def _mod_select(m_all, b_mod6, name):
    def body(m_ref, b_ref, mx_ref, mc_ref):
        me = _me()
        mx_ref[...] = m_ref[me] + b_ref[...]
        mc_ref[...] = m_ref[8] + b_ref[...]

    return pl.pallas_call(body, name=name, out_shape=[jax.ShapeDtypeStruct((6, D), f32)] * 2, compiler_params=_CP())(m_all, b_mod6)


def _ada_bwd(c9, dmx_all, dmc_all, dmx_l, dmc_l, w_mod_l, name):
    ncol = w_mod_l.shape[1]

    def rowsum(r):
        acc = r[0:1]
        for j in range(1, NDEV):
            acc = acc + r[j:j + 1]
        return acc

    def body(c_ref, xa_ref, ca_ref, xl_ref, cl_ref, w_ref, gw_ref, gb_ref, dc_ref):
        s9, vjp = jax.vjp(jax.nn.silu, c_ref[...])
        dm9 = jnp.concatenate([xl_ref[...], rowsum(cl_ref[...]), jnp.zeros((7, ncol), f32)], axis=0)
        gw_ref[...] = dtn(s9, dm9)
        gb_ref[...] = rowsum(xa_ref[...]) + rowsum(ca_ref[...])
        dc_ref[...] = vjp(dnt(dm9, w_ref[...]))[0]

    return pl.pallas_call(
        body, name=name,
        out_shape=[jax.ShapeDtypeStruct((D, ncol), f32), jax.ShapeDtypeStruct((1, 6 * D), f32), jax.ShapeDtypeStruct((16, D), f32)],
        compiler_params=_CP())(c9, dmx_all, dmc_all, dmx_l, dmc_l, w_mod_l)


def _lane_sign(rank):
    shape = (1,) * (rank - 1) + (SB,)
    return jnp.where(lax.broadcasted_iota(jnp.int32, shape, rank - 1) < S5N, -1.0, 1.0)


def _s5_build_fn(lre2, lim2, ls, bn, bs, cn, cs, rev):
    sg = _lane_sign(3)
    s = jnp.exp(ls)
    ar, ai = lre2 * s, lim2 * s
    e = jnp.exp(ar)
    nr, ni = e * jnp.cos(ai) - 1.0, e * jnp.sin(ai)
    den = lre2 * lre2 + lim2 * lim2
    cr, ci = (nr * lre2 + ni * lim2) / den, (ni * lre2 - nr * lim2) / den
    bbn = cr * bn + (ci * sg) * bs
    bbs = cr * bs - (ci * sg) * bn

    def powers(ex):
        m, ang = jnp.exp(ex * ar), ex * ai
        return m * jnp.cos(ang), m * jnp.sin(ang) * sg

    def times(tabs, xn, xs):
        f1, f2 = tabs
        return f1[:, :, None, :] * xn[:, None, :, :] + f2[:, :, None, :] * xs[:, None, :, :]

    t = lax.broadcasted_iota(jnp.int32, (1, TC, 1), 1).astype(f32)
    if rev:
        e_src, e_dst, e_out, e_in = t - (TC - 1.0), (TC - 1.0) - t, t, TC - t
    else:
        e_src, e_dst, e_out, e_in = -t, t, (TC - 1.0) - t, t + 1.0
    g = lre2.shape[0]
    flat = lambda a: a.reshape(g, TCP, SB)
    conj = -_lane_sign(4)
    ll = flat(times(powers(e_src), bbn, bbs))
    rr = flat(times(powers(e_dst), cn, cs) * conj)
    mb = flat(times(powers(e_out), bbn, bbs))
    mct = flat(times(powers(e_in), cn, cs) * conj)
    a1, a2 = powers(float(TC))
    row = lax.broadcasted_iota(jnp.int32, (TCP, TCP), 0) // S5P
    col = lax.broadcasted_iota(jnp.int32, (TCP, TCP), 1) // S5P
    mask = jnp.where((col <= row) if rev else (col >= row), 1.0, 0.0)
    m = jnp.concatenate([dnt(ll[j], rr[j])[None] for j in range(g)], axis=0) * mask
    return m, mb, mct, a1, a2


def _gspec(*tail):
    nt = len(tail)
    return pl.BlockSpec((GBK,) + tail, lambda i, _n=nt: (i,) + (0,) * _n)


def _s5_build(params, rev, name):
    def body(l1, l2, ls, bn, bs, cn, cs, m_ref, mb_ref, mc_ref, a1_ref, a2_ref):
        m, mb, mct, a1, a2 = _s5_build_fn(l1[...], l2[...], ls[...], bn[...], bs[...], cn[...], cs[...], rev)
        m_ref[...], mb_ref[...], mc_ref[...] = m.astype(bf16), mb.astype(bf16), mct.astype(bf16)
        a1_ref[...], a2_ref[...] = a1, a2

    vec, pm = _gspec(1, SB), _gspec(S5P, SB)
    return pl.pallas_call(
        body, name=name, grid=(S5G // GBK,),
        in_specs=[vec, vec, _gspec(1, 1), pm, pm, pm, pm],
        out_specs=[_gspec(TCP, TCP), _gspec(TCP, SB), _gspec(TCP, SB), vec, vec],
        out_shape=[jax.ShapeDtypeStruct((S5G, TCP, TCP), bf16), jax.ShapeDtypeStruct((S5G, TCP, SB), bf16),
                   jax.ShapeDtypeStruct((S5G, TCP, SB), bf16), jax.ShapeDtypeStruct((S5G, 1, SB), f32),
                   jax.ShapeDtypeStruct((S5G, 1, SB), f32)],
        compiler_params=_CP(dimension_semantics=("parallel",)),
    )(*params)


def _s5_build_bwd(params, cots, prev, rev, name):
    def body(l1, l2, ls, bn, bs, cn, cs, dm, dmb, dmc, da1, da2, pb, pc, gl1, gl2, gls, gb, gc):
        prim = (l1[...], l2[...], ls[...], bn[...], bs[...], cn[...], cs[...])
        _, vjp = jax.vjp(functools.partial(_s5_build_fn, rev=rev), *prim)
        d1, d2, dls, dbn, dbs, dcn, dcs = vjp((dm[...], dmb[...], dmc[...], da1[...], da2[...]))
        gl1[...] = d1 + pltpu.roll(d1, S5N, axis=2)
        gl2[...] = d2 + pltpu.roll(d2, S5N, axis=2)
        gls[...] = dls
        gb[...] = dbn + pltpu.roll(dbs, S5N, axis=2) + pb[...]
        gc[...] = dcn + pltpu.roll(dcs, S5N, axis=2) + pc[...]

    vec, pm, big = _gspec(1, SB), _gspec(S5P, SB), _gspec(TCP, SB)
    return pl.pallas_call(
        body, name=name, grid=(S5G // GBK,),
        in_specs=[vec, vec, _gspec(1, 1), pm, pm, pm, pm, _gspec(TCP, TCP), big, big, vec, vec, pm, pm],
        out_specs=[vec, vec, _gspec(1, 1), pm, pm],
        out_shape=[jax.ShapeDtypeStruct((S5G, 1, SB), f32), jax.ShapeDtypeStruct((S5G, 1, SB), f32),
                   jax.ShapeDtypeStruct((S5G, 1, 1), f32), jax.ShapeDtypeStruct((S5G, S5P, SB), f32),
                   jax.ShapeDtypeStruct((S5G, S5P, SB), f32)],
        compiler_params=_CP(dimension_semantics=("parallel",)),
    )(*params, *cots, *prev)


def _s5_inc(u, mb_f, mb_b, name):
    nc = u.shape[1]

    def body(u_ref, mf_ref, mb_ref, sf_ref, sb_ref):
        for j in range(GBK):
            sf_ref[:, j, :] = jnp.dot(u_ref[j], mf_ref[j], preferred_element_type=f32)
            sb_ref[:, j, :] = jnp.dot(u_ref[j], mb_ref[j], preferred_element_type=f32)

    sspec = pl.BlockSpec((nc, GBK, SB), lambda i: (0, i, 0))
    return pl.pallas_call(
        body, name=name, grid=(S5G // GBK,), in_specs=[_gspec(nc, TCP), _gspec(TCP, SB), _gspec(TCP, SB)],
        out_specs=[sspec, sspec], out_shape=[jax.ShapeDtypeStruct((nc, S5G, SB), f32)] * 2,
        compiler_params=_CP(dimension_semantics=("parallel",)),
    )(u, mb_f, mb_b)


def _idx_fwd(nctx, nch):
    return lambda i: i


def _idx_rev(nctx, nch):
    return lambda i: jnp.where(i < nctx, nctx - 1 - i, nch + nctx - 1 - i)


def _carry_loop(nc, step, init):
    def trip(i, c):
        for k in range(CARRY_UNROLL):
            c = step(i * CARRY_UNROLL + k, c)
        return c

    return lax.fori_loop(0, nc // CARRY_UNROLL, trip, init)


def _s5_carry(s_f, s_b, a_f, a_b, nctx, name):
    nc = s_f.shape[0]
    idx_b = _idx_rev(nctx, nc)

    def body(sf_ref, sb_ref, f1_ref, f2_ref, b1_ref, b2_ref, hf_ref, hb_ref):
        f1, f2, b1, b2 = f1_ref[...], f2_ref[...], b1_ref[...], b2_ref[...]

        def step(i, c):
            hf, hfs, hb, hbs = c
            rb = idx_b(i)
            hf_ref[i] = hf
            hb_ref[rb] = hb
            sf, sb = sf_ref[i], sb_ref[rb]
            return (f1 * hf + f2 * hfs + sf, f1 * hfs - f2 * hf + pltpu.roll(sf, S5N, axis=1),
                    b1 * hb + b2 * hbs + sb, b1 * hbs - b2 * hb + pltpu.roll(sb, S5N, axis=1))

        z = jnp.zeros((S5G, SB), f32)
        _carry_loop(nc, step, (z, z, z, z))

    return pl.pallas_call(body, name=name, out_shape=[jax.ShapeDtypeStruct(s_f.shape, f32)] * 2,
                          compiler_params=_CP())(s_f, s_b, *a_f, *a_b)


def _s5_carry_bwd(dhp, hp, a1, a2, rev, nctx, name):
    nc = hp.shape[0]
    idx = (_idx_rev if rev else _idx_fwd)(nctx, nc)

    def body(dhp_ref, hp_ref, a1_ref, a2_ref, ds_ref, d1_ref, d2_ref):
        f1, f2 = a1_ref[...], a2_ref[...]

        def step(k, carry):
            ab, abs_, d1, d2 = carry
            r = idx(nc - 1 - k)
            ds_ref[r] = ab
            h, dh = hp_ref[r], dhp_ref[r]
            return (dh + f1 * ab - f2 * abs_, pltpu.roll(dh, S5N, axis=1) + f1 * abs_ + f2 * ab,
                    d1 + ab * h, d2 + ab * pltpu.roll(h, S5N, axis=1))

        z = jnp.zeros((S5G, SB), f32)
        _, _, d1, d2 = _carry_loop(nc, step, (z, z, z, z))
        d1_ref[...], d2_ref[...] = d1, d2

    return pl.pallas_call(
        body, name=name,
        out_shape=[jax.ShapeDtypeStruct(hp.shape, f32), jax.ShapeDtypeStruct((S5G, SB), f32), jax.ShapeDtypeStruct((S5G, SB), f32)],
        compiler_params=_CP())(dhp, hp, a1, a2)


def _s5_out(u, m_f, m_b, hp_f, hp_b, mc_f, mc_b, name):
    nc = u.shape[1]

    def body(u_ref, mf_ref, mb_ref, hf_ref, hb_ref, cf_ref, cb_ref, y_ref):
        for j in range(GBK):
            uj = u_ref[j]
            y_ref[j] = (jnp.dot(uj, mf_ref[j], preferred_element_type=f32) + jnp.dot(uj, mb_ref[j], preferred_element_type=f32)
                        + dnt(hf_ref[:, j, :], cf_ref[j]) + dnt(hb_ref[:, j, :], cb_ref[j])).astype(bf16)

    sspec = pl.BlockSpec((nc, GBK, SB), lambda i: (0, i, 0))
    return pl.pallas_call(
        body, name=name, grid=(S5G // GBK,),
        in_specs=[_gspec(nc, TCP), _gspec(TCP, TCP), _gspec(TCP, TCP), sspec, sspec, _gspec(TCP, SB), _gspec(TCP, SB)],
        out_specs=_gspec(nc, TCP), out_shape=jax.ShapeDtypeStruct((S5G, nc, TCP), bf16),
        compiler_params=_CP(dimension_semantics=("parallel",)),
    )(u, m_f, m_b, hp_f, hp_b, mc_f, mc_b)


def _s5_out_bwd(dy, u, m_f, m_b, hp_f, hp_b, mc_f, mc_b, name):
    nc = u.shape[1]

    def body(dy_ref, u_ref, mf_ref, mb_ref, hf_ref, hb_ref, cf_ref, cb_ref, du_ref, g_ref, dhf_ref, dhb_ref, dcf_ref, dcb_ref):
        for j in range(GBK):
            dyj = dy_ref[j]
            du_ref[j] = dnt(dyj, mf_ref[j]) + dnt(dyj, mb_ref[j])
            g_ref[j] = dtn(u_ref[j], dyj)
            dhf_ref[:, j, :] = dnn(dyj, cf_ref[j])
            dhb_ref[:, j, :] = dnn(dyj, cb_ref[j])
            dcf_ref[j] = dtn(dyj, hf_ref[:, j, :])
            dcb_ref[j] = dtn(dyj, hb_ref[:, j, :])

    sspec = pl.BlockSpec((nc, GBK, SB), lambda i: (0, i, 0))
    sshape = jax.ShapeDtypeStruct((nc, S5G, SB), f32)
    cshape = jax.ShapeDtypeStruct((S5G, TCP, SB), f32)
    return pl.pallas_call(
        body, name=name, grid=(S5G // GBK,),
        in_specs=[_gspec(nc, TCP), _gspec(nc, TCP), _gspec(TCP, TCP), _gspec(TCP, TCP), sspec, sspec, _gspec(TCP, SB), _gspec(TCP, SB)],
        out_specs=[_gspec(nc, TCP), _gspec(TCP, TCP), sspec, sspec, _gspec(TCP, SB), _gspec(TCP, SB)],
        out_shape=[jax.ShapeDtypeStruct((S5G, nc, TCP), f32), jax.ShapeDtypeStruct((S5G, TCP, TCP), f32), sshape, sshape, cshape, cshape],
        compiler_params=_CP(dimension_semantics=("parallel",)),
    )(dy, u, m_f, m_b, hp_f, hp_b, mc_f, mc_b)


def _s5_inc_bwd(du1, u, ds_f, ds_b, mb_f, mb_b, name):
    nc = u.shape[1]

    def body(du1_ref, u_ref, dsf_ref, dsb_ref, mf_ref, mb_ref, du_ref, dmf_ref, dmb_ref):
        for j in range(GBK):
            dsf, dsb = dsf_ref[:, j, :], dsb_ref[:, j, :]
            du_ref[j] = (du1_ref[j] + dnt(dsf, mf_ref[j]) + dnt(dsb, mb_ref[j])).astype(bf16)
            dmf_ref[j] = dtn(u_ref[j], dsf)
            dmb_ref[j] = dtn(u_ref[j], dsb)

    sspec = pl.BlockSpec((nc, GBK, SB), lambda i: (0, i, 0))
    cshape = jax.ShapeDtypeStruct((S5G, TCP, SB), f32)
    return pl.pallas_call(
        body, name=name, grid=(S5G // GBK,),
        in_specs=[_gspec(nc, TCP), _gspec(nc, TCP), sspec, sspec, _gspec(TCP, SB), _gspec(TCP, SB)],
        out_specs=[_gspec(nc, TCP), _gspec(TCP, SB), _gspec(TCP, SB)],
        out_shape=[jax.ShapeDtypeStruct((S5G, nc, TCP), bf16), cshape, cshape],
        compiler_params=_CP(dimension_semantics=("parallel",)),
    )(du1, u, ds_f, ds_b, mb_f, mb_b)


def _to_groups(a):
    n = a.shape[0]
    return a.reshape(n // TC, TC, S5G, S5P).transpose(2, 0, 1, 3).reshape(S5G, n // TC, TCP)


def _from_groups(a):
    nc = a.shape[1]
    return a.reshape(S5G, nc, TC, S5P).transpose(1, 2, 0, 3).reshape(nc * TC, S5W)


def _swap_pairs(t):
    lane = lax.broadcasted_iota(jnp.int32, t.shape, 1)
    return jnp.where(lane % 2 == 0, pltpu.roll(t, DH - 1, axis=1), pltpu.roll(t, 1, axis=1))


def _rot(t, cosf, sins):
    return t * cosf + _swap_pairs(t) * sins


def _rot_t(d, cosf, sins):
    return d * cosf - _swap_pairs(d) * sins


def _ret_chunk(qr, kr, v, rp, ld, rev):
    pos = lax.broadcasted_iota(jnp.int32, (T, 1), 0).astype(f32)
    diff = pos - lax.broadcasted_iota(jnp.int32, (1, T), 1).astype(f32)
    if rev:
        keep, dist = diff < 0, jnp.maximum(-diff, 0.0)
        xi, zeta = jnp.exp(ld * (T - pos)), jnp.exp(ld * pos)
    else:
        keep, dist = diff >= 0, jnp.maximum(diff, 0.0)
        xi, zeta = jnp.exp(ld * (pos + 1.0)), jnp.exp(ld * (T - 1.0 - pos))
    dm = jnp.where(keep, jnp.exp(ld * dist), 0.0)
    out = dnn(dnt(qr, kr) * dm, v) + dnn(qr * xi, rp)
    rn = jnp.exp(ld * float(T)) * rp + dtn(kr * zeta, v)
    return out, rn


def _ret_fwd(p_ext, ld8, rev, nctx, name, cargo=None):
    n = p_ext.shape[0]
    nch = n // T
    idx = (_idx_rev if rev else _idx_fwd)(nctx, nch)
    cg = _Cargo(cargo)

    def body(*refs):
        (q_ref, k_ref, v_ref, ld_ref), (o_ref, rp_ref), (r_s,) = cg.split(refs, 4, 2, 1)
        cg.ride(refs, 4, 2, nch)

        @pl.when(pl.program_id(0) == 0)
        def _():
            r_s[...] = jnp.zeros_like(r_s)

        for h in range(RH):
            sl = slice(h * DH, (h + 1) * DH)
            rp = r_s[h]
            rp_ref[0, h] = rp
            out, rn = _ret_chunk(q_ref[:, sl].astype(f32), k_ref[:, sl].astype(f32), v_ref[:, sl].astype(f32), rp,
                                 ld_ref[h:h + 1, 0:1], rev)
            r_s[h] = rn
            o_ref[:, sl] = out

    def colspec(cb):
        return pl.BlockSpec((T, RW), lambda i, _c=cb: (idx(i), _c))

    return pl.pallas_call(
        body, name=name, grid=(nch,),
        in_specs=[colspec(1), colspec(2), colspec(3), _const_spec((8, 128))] + cg.in_specs(),
        out_specs=[pl.BlockSpec((T, RW), lambda i: (idx(i), 0)), pl.BlockSpec((1, RH, DH, DH), lambda i: (i, 0, 0, 0))] + cg.in_specs(),
        out_shape=[jax.ShapeDtypeStruct((n, RW), f32), jax.ShapeDtypeStruct((nch, RH, DH, DH), f32)] + cg.out_shapes(),
        scratch_shapes=[pltpu.VMEM((RH, DH, DH), f32)] + cg.sems(),
        compiler_params=_CP(dimension_semantics=_ARB),
    )(p_ext, p_ext, p_ext, ld8, *cg.arrays)


def _ret_bwd(p_ext, ld8, rprev, do_ext, rev, nctx, name, cargo=None):
    n = p_ext.shape[0]
    nch = n // T
    idx0 = (_idx_rev if rev else _idx_fwd)(nctx, nch)
    idx = lambda j: idx0(nch - 1 - j)
    cg = _Cargo(cargo)

    def body(*refs):
        ins, (dq_ref, dk_ref, dv_ref, dld_ref), (dr_s,) = cg.split(refs, 6, 4, 1)
        q_ref, k_ref, v_ref, ld_ref, rp_ref, do_ref = ins
        cg.ride(refs, 6, 4, nch)

        @pl.when(pl.program_id(0) == 0)
        def _():
            dr_s[...] = jnp.zeros_like(dr_s)
            dld_ref[...] = jnp.zeros_like(dld_ref)

        for h in range(RH):
            sl = slice(h * DH, (h + 1) * DH)
            _, vjp = jax.vjp(functools.partial(_ret_chunk, rev=rev), q_ref[:, sl].astype(f32), k_ref[:, sl].astype(f32),
                             v_ref[:, sl].astype(f32), rp_ref[0, h], ld_ref[h:h + 1, 0:1])
            dqr, dkr, dv, drp, dld = vjp((do_ref[:, sl], dr_s[h]))
            dr_s[h] = drp
            dq_ref[:, sl], dk_ref[:, sl], dv_ref[:, sl] = dqr, dkr, dv
            dld_ref[h:h + 1, :] += jnp.broadcast_to(dld, (1, 128))

    def colspec(cb):
        return pl.BlockSpec((T, RW), lambda j, _c=cb: (idx(j), _c))

    ospec = pl.BlockSpec((T, RW), lambda j: (idx(j), 0))
    oshape = jax.ShapeDtypeStruct((n, RW), f32)
    return pl.pallas_call(
        body, name=name, grid=(nch,),
        in_specs=[colspec(1), colspec(2), colspec(3), _const_spec((8, 128)),
                  pl.BlockSpec((1, RH, DH, DH), lambda j: (nch - 1 - j, 0, 0, 0)), ospec] + cg.in_specs(),
        out_specs=[ospec, ospec, ospec, _acc_spec((8, 128))] + cg.in_specs(),
        out_shape=[oshape, oshape, oshape, jax.ShapeDtypeStruct((8, 128), f32)] + cg.out_shapes(),
        scratch_shapes=[pltpu.VMEM((RH, DH, DH), f32)] + cg.sems(),
        compiler_params=_CP(dimension_semantics=_ARB),
    )(p_ext, p_ext, p_ext, ld8, rprev, do_ext, *cg.arrays)


def _qk_heads(p, fn_q, fn_k):
    heads = lambda base, fn: [fn(p[:, base + h * DH:base + (h + 1) * DH]) for h in range(RH)]
    return jnp.concatenate([p[:, :S5W]] + heads(S5W, fn_q) + heads(S5W + RW, fn_k) + [p[:, S5W + 2 * RW:]], axis=1)


def _f1_fwd(x, ctx, modx, modc, nw1, w_in_n, cosf, sins, name, cargo=None):
    L = x.shape[0]
    nb = L // R + 1
    scale = DH ** -0.5
    cg = _Cargo(cargo)

    def body(*refs):
        (x_ref, c_ref, mx_ref, mc_ref, nw_ref, w_ref, cos_ref, sin_ref), (p_ref,), _ = cg.split(refs, 8, 1, 0)
        cg.ride(refs, 8, 1, nb)
        is_ctx = pl.program_id(0) == 0
        xin = jnp.where(is_ctx, c_ref[...], x_ref[...])
        sh = jnp.where(is_ctx, mc_ref[0:1], mx_ref[0:1])
        sc = jnp.where(is_ctx, mc_ref[1:2], mx_ref[1:2])
        cf, ss = cos_ref[...], sin_ref[...]
        p = dnn(_mod(_rms(xin, nw_ref[...]), sh, sc), w_ref[...])
        p_ref[...] = _qk_heads(p, lambda t: _rot(t, cf, ss), lambda t: _rot(t * scale, cf, ss)).astype(bf16)

    return pl.pallas_call(
        body, name=name, grid=(nb,),
        in_specs=[pl.BlockSpec((R, D), lambda i: (jnp.maximum(i - 1, 0), 0)), _const_spec((R, D)), _const_spec((6, D)),
                  _const_spec((6, D)), _const_spec((1, D)), _const_spec((D, INC)), pl.BlockSpec((R, DH), lambda i: (i, 0)),
                  pl.BlockSpec((R, DH), lambda i: (i, 0))] + cg.in_specs(),
        out_specs=[pl.BlockSpec((R, INC), lambda i: (i, 0))] + cg.in_specs(),
        out_shape=[jax.ShapeDtypeStruct((L + R, INC), bf16)] + cg.out_shapes(),
        scratch_shapes=cg.sems(),
        compiler_params=_CP(dimension_semantics=_ARB),
    )(x, ctx, modx, modc, nw1, w_in_n, cosf, sins, *cg.arrays)


def _f1_bwd(x, ctx, modx, modc, nw1, w_in_t, cosf, sins, dx1, parts, name, cargo=None):
    L = x.shape[0]
    nb = L // R + 1
    scale = DH ** -0.5
    cg = _Cargo(cargo)

    def body(*refs):
        ins, (gx_ref, dp_ref, h1_ref, dnw_ref, dmx_ref, dmc_ref), _ = cg.split(refs, 18, 6, 0)
        x_ref, c_ref, mx_ref, mc_ref, nw_ref, w_ref, cos_ref, sin_ref, dx1_ref, du0, du1, dq0, dq1, dk0, dk1, dv0, dv1, dg0 = ins
        cg.ride(refs, 18, 6, nb)
        i = pl.program_id(0)
        is_ctx = i == 0

        @pl.when(is_ctx)
        def _():
            dnw_ref[...] = jnp.zeros_like(dnw_ref)
            dmx_ref[...] = jnp.zeros_like(dmx_ref)
            dmc_ref[...] = jnp.zeros_like(dmc_ref)

        cf, ss = cos_ref[...], sin_ref[...]
        dp = jnp.concatenate([du0[...].astype(f32) + du1[...], dq0[...] + dq1[...], dk0[...] + dk1[...], dv0[...] + dv1[...],
                              dg0[...]], axis=1)
        dp = _qk_heads(dp, lambda t: _rot_t(t, cf, ss), lambda t: _rot_t(t, cf, ss) * scale).astype(bf16)
        dp_ref[...] = dp
        xin = jnp.where(is_ctx, c_ref[...], x_ref[...])
        sh = jnp.where(is_ctx, mc_ref[0:1], mx_ref[0:1])
        sc = jnp.where(is_ctx, mc_ref[1:2], mx_ref[1:2])
        dh = dnn(dp, w_ref[...])
        h, vjp = jax.vjp(lambda a, b, c, d: _mod(_rms(a, b), c, d), xin, nw_ref[...], sh, sc)
        dxin, dnw, dsh, dsc = vjp(dh)
        h1_ref[...] = h.astype(bf16)
        gx_ref[...] = dx1_ref[...] + dxin
        dnw_ref[...] += dnw
        wx = jnp.where(is_ctx, 0.0, 1.0)
        dmx_ref[0:1] += dsh * wx
        dmx_ref[1:2] += dsc * wx
        dmc_ref[0:1] += dsh * (1.0 - wx)
        dmc_ref[1:2] += dsc * (1.0 - wx)

    lat = pl.BlockSpec((R, D), lambda i: (jnp.maximum(i - 1, 0), 0))
    ext = pl.BlockSpec((R, S5W), lambda i: (i, 0))
    return pl.pallas_call(
        body, name=name, grid=(nb,),
        in_specs=[lat, _const_spec((R, D)), _const_spec((6, D)), _const_spec((6, D)), _const_spec((1, D)), _const_spec((INC, D)),
                  pl.BlockSpec((R, DH), lambda i: (i, 0)), pl.BlockSpec((R, DH), lambda i: (i, 0)), lat] + [ext] * 9 + cg.in_specs(),
        out_specs=[lat, pl.BlockSpec((R, INC), lambda i: (i, 0)), pl.BlockSpec((R, D), lambda i: (i, 0)),
                   _acc_spec((1, D)), _acc_spec((6, D)), _acc_spec((6, D))] + cg.in_specs(),
        out_shape=[jax.ShapeDtypeStruct((L, D), f32), jax.ShapeDtypeStruct((L + R, INC), bf16),
                   jax.ShapeDtypeStruct((L + R, D), bf16), jax.ShapeDtypeStruct((1, D), f32),
                   jax.ShapeDtypeStruct((6, D), f32), jax.ShapeDtypeStruct((6, D), f32)] + cg.out_shapes(),
        scratch_shapes=cg.sems(),
        compiler_params=_CP(dimension_semantics=_ARB),
    )(x, ctx, modx, modc, nw1, w_in_t, cosf, sins, dx1, *parts, *cg.arrays)


def _ret_post(yr, g):
    outs = []
    for h in range(RH):
        yh = yr[:, h * DH:(h + 1) * DH]
        mu = jnp.mean(yh, axis=-1, keepdims=True)
        var = jnp.mean((yh - mu) ** 2, axis=-1, keepdims=True)
        outs.append((yh - mu) * lax.rsqrt(var + EPS))
    return jax.nn.silu(g) * jnp.concatenate(outs, axis=1)


def _mix_fn(ys, u, of, ob, g, x, dvec, bglu, gate1, pz, pm, wglu, wout):
    s = _gelu(ys + dvec * u)
    z = dnn(s, wglu) + bglu + pz
    cat = jnp.concatenate([s * jax.nn.sigmoid(z), _ret_post(of + ob, g)], axis=1)
    mix = dnn(cat, wout) + pm
    return x + gate1 * mix, (s, cat)


def _mix_fwd(x, ys, of, ob, p_ext, dvec, bglu, modx, wglu, wout, name, cargo=None):
    L = x.shape[0]
    nb = L // R
    cg = _Cargo(cargo)

    def body(*refs):
        ins, (x1_ref,), _ = cg.split(refs, 11, 1, 0)
        x_ref, ys_ref, of_ref, ob_ref, u_ref, g_ref, d_ref, b_ref, mx_ref, wg_ref, wo_ref = ins
        cg.ride(refs, 11, 1, nb)
        x1_ref[...] = _mix_fn(ys_ref[...].astype(f32), u_ref[...].astype(f32), of_ref[...], ob_ref[...], g_ref[...].astype(f32),
                              x_ref[...], d_ref[...], b_ref[...], mx_ref[2:3], 0.0, 0.0, wg_ref[...], wo_ref[...])[0]

    ext = pl.BlockSpec((R, S5W), lambda i: (i + 1, 0))
    return pl.pallas_call(
        body, name=name, grid=(nb,),
        in_specs=[pl.BlockSpec((R, D), lambda i: (i, 0)), ext, ext, ext, ext, pl.BlockSpec((R, RW), lambda i: (i + 1, 4)),
                  _const_spec((1, S5W)), _const_spec((1, S5W)), _const_spec((6, D)), _const_spec((S5W, S5W)), _const_spec((D, D))]
        + cg.in_specs(),
        out_specs=[pl.BlockSpec((R, D), lambda i: (i, 0))] + cg.in_specs(),
        out_shape=[jax.ShapeDtypeStruct((L, D), f32)] + cg.out_shapes(),
        scratch_shapes=cg.sems(),
        compiler_params=_CP(dimension_semantics=_ARB),
    )(x, ys, of, ob, p_ext, p_ext, dvec, bglu, modx, wglu, wout, *cg.arrays)


def _mix_bwd(x, ys, of, ob, p_ext, dvec, bglu, modx, wglu, wout, dx1, name, cargo=None):
    L = x.shape[0]
    nb = L // R + 1
    cg = _Cargo(cargo)

    def body(*refs):
        ins, outs, _ = cg.split(refs, 12, 11, 0)
        x_ref, ys_ref, of_ref, ob_ref, u_ref, g_ref, d_ref, b_ref, mx_ref, wg_ref, wo_ref, dx1_ref = ins
        dy_ref, dud_ref, do_ref, dg_ref, cat_ref, dmix_ref, s_ref, dz_ref, dd_ref, db_ref, dg1_ref = outs
        cg.ride(refs, 12, 11, nb)
        i = pl.program_id(0)

        @pl.when(i == 0)
        def _():
            for r in outs:
                r[...] = jnp.zeros_like(r)

        @pl.when(i > 0)
        def _():
            fn = lambda ys_, u_, of_, g_, d_, b_, g1_, pz_, pm_: _mix_fn(
                ys_, u_, of_, ob_ref[...], g_, x_ref[...], d_, b_, g1_, pz_, pm_, wg_ref[...], wo_ref[...])
            _, vjp, (s, cat) = jax.vjp(fn, ys_ref[...].astype(f32), u_ref[...].astype(f32), of_ref[...], g_ref[...].astype(f32), d_ref[...],
                                       b_ref[...], mx_ref[2:3], jnp.zeros((R, S5W), f32), jnp.zeros((R, D), f32), has_aux=True)
            dy, dud, do, dg, dd, db, dg1, dz, dmix = vjp(dx1_ref[...])
            dy_ref[...], dud_ref[...], do_ref[...], dg_ref[...] = dy.astype(bf16), dud, do, dg
            cat_ref[...], dmix_ref[...] = cat.astype(bf16), dmix.astype(bf16)
            s_ref[...], dz_ref[...] = s.astype(bf16), dz.astype(bf16)
            dd_ref[...] += dd
            db_ref[...] += db
            dg1_ref[...] += dg1

    lat = pl.BlockSpec((R, D), lambda i: (jnp.maximum(i - 1, 0), 0))
    lat5 = pl.BlockSpec((R, S5W), lambda i: (jnp.maximum(i - 1, 0), 0))
    ext = pl.BlockSpec((R, S5W), lambda i: (i, 0))
    eshape = jax.ShapeDtypeStruct((L + R, S5W), f32)
    return pl.pallas_call(
        body, name=name, grid=(nb,),
        in_specs=[lat, ext, ext, ext, ext, pl.BlockSpec((R, RW), lambda i: (i, 4)),
                  _const_spec((1, S5W)), _const_spec((1, S5W)), _const_spec((6, D)), _const_spec((S5W, S5W)), _const_spec((D, D)), lat]
        + cg.in_specs(),
        out_specs=[ext, ext, ext, ext, lat, lat, lat5, lat5, _acc_spec((1, S5W)), _acc_spec((1, S5W)), _acc_spec((1, D))]
        + cg.in_specs(),
        out_shape=[jax.ShapeDtypeStruct((L + R, S5W), bf16), eshape, eshape, eshape, jax.ShapeDtypeStruct((L, D), bf16),
                   jax.ShapeDtypeStruct((L, D), bf16), jax.ShapeDtypeStruct((L, S5W), bf16), jax.ShapeDtypeStruct((L, S5W), bf16),
                   jax.ShapeDtypeStruct((1, S5W), f32), jax.ShapeDtypeStruct((1, S5W), f32), jax.ShapeDtypeStruct((1, D), f32)]
        + cg.out_shapes(),
        scratch_shapes=cg.sems(),
        compiler_params=_CP(dimension_semantics=_ARB),
    )(x, ys, of, ob, p_ext, p_ext, dvec, bglu, modx, wglu, wout, dx1, *cg.arrays)


def _ffn_tail(gc, a, x1, gate2, fnw, pf, wdown, wdown_t, tgt):
    f = _gelu(gc) * a
    ffn = _dnn_const(f, wdown, wdown_t) + pf
    y = _rms(x1 + gate2 * ffn, fnw)
    err = y - tgt
    loss = 0.5 * jnp.sum(jnp.mean(err * err, axis=-1, keepdims=True), axis=0, keepdims=True)
    return loss, f


def _ffn_fwd(x1, tgt, nw2, modx, w_a, w_g, cw, cb, wdown, wdown_t, fnw, name):
    L = x1.shape[0]
    nb = L // RF
    per = RF // HALO

    def body(x_ref, xp_ref, xn_ref, t_ref, nw_ref, mx_ref, wa_ref, wg_ref, cw_ref, cb_ref, wd_ref, wdt_ref, fn_ref,
             dx2_ref, da_ref, dgc_ref, f_ref, dffn_ref, loss_ref, dfn_ref, dg2_ref, dcb_ref, dcw_ref):
        i = pl.program_id(0)

        @pl.when(i == 0)
        def _():
            for r in (loss_ref, dfn_ref, dg2_ref, dcb_ref, dcw_ref):
                r[...] = jnp.zeros_like(r)

        nw, sh, sc, gate2 = nw_ref[...], mx_ref[3:4], mx_ref[4:5], mx_ref[5:6]
        x1b = x_ref[...]
        h2 = _mod(_rms(x1b, nw), sh, sc)
        h2e = jnp.concatenate([_mod(_rms(xp_ref[...], nw), sh, sc), h2, _mod(_rms(xn_ref[...], nw), sh, sc)], axis=0)
        a = dnn(h2, wa_ref[...])
        ge = dnn(h2e, wg_ref[...])
        g = ge[HALO:HALO + RF]
        gp = ge[HALO - 1:HALO] * jnp.where(i > 0, 1.0, 0.0)
        gn = ge[HALO + RF:HALO + RF + 1] * jnp.where(i < nb - 1, 1.0, 0.0)
        row = lax.broadcasted_iota(jnp.int32, (RF, 1), 0)
        g_prev = jnp.where(row == 0, gp, pltpu.roll(g, 1, axis=0))
        g_next = jnp.where(row == RF - 1, gn, pltpu.roll(g, RF - 1, axis=0))
        gc = cb_ref[...] + g_prev * cw_ref[0:1] + g * cw_ref[1:2] + g_next * cw_ref[2:3]
        fn = lambda gc_, a_, x_, g2_, fw_, pf_: _ffn_tail(gc_, a_, x_, g2_, fw_, pf_, wd_ref[...], wdt_ref[...], t_ref[...])
        loss, vjp, f = jax.vjp(fn, gc, a, x1b, gate2, fn_ref[...], jnp.zeros((RF, D), f32), has_aux=True)
        dgc, da, dx2, dg2, dfw, dffn = vjp(jnp.ones((1, 1), f32))
        dx2_ref[...] = dx2
        da_ref[...], dgc_ref[...] = da.astype(bf16), dgc
        f_ref[...], dffn_ref[...] = f.astype(bf16), dffn.astype(bf16)
        loss_ref[...] += jnp.broadcast_to(loss, (1, 128))
        dfn_ref[...] += dfw
        dg2_ref[...] += dg2
        dcb_ref[...] += jnp.sum(dgc, axis=0, keepdims=True)
        dcw_ref[0:1] += jnp.sum(dgc * g_prev, axis=0, keepdims=True)
        dcw_ref[1:2] += jnp.sum(dgc * g, axis=0, keepdims=True)
        dcw_ref[2:3] += jnp.sum(dgc * g_next, axis=0, keepdims=True)

    blk = lambda w: pl.BlockSpec((RF, w), lambda i: (i, 0))
    return pl.pallas_call(
        body, name=name, grid=(nb,),
        in_specs=[blk(D), pl.BlockSpec((HALO, D), lambda i: (jnp.maximum(i * per - 1, 0), 0)),
                  pl.BlockSpec((HALO, D), lambda i: (jnp.minimum((i + 1) * per, L // HALO - 1), 0)), blk(D),
                  _const_spec((1, D)), _const_spec((6, D)), _const_spec((D, DFF)), _const_spec((D, DFF)), _const_spec((3, DFF)),
                  _const_spec((1, DFF)), _const_spec((DFF, D)), _const_spec((D, DFF)), _const_spec((1, D))],
        out_specs=[blk(D), blk(DFF), blk(DFF), blk(DFF), blk(D), _acc_spec((1, 128)), _acc_spec((1, D)), _acc_spec((1, D)),
                   _acc_spec((1, DFF)), _acc_spec((3, DFF))],
        out_shape=[jax.ShapeDtypeStruct((L, D), f32), jax.ShapeDtypeStruct((L, DFF), bf16), jax.ShapeDtypeStruct((L, DFF), f32),
                   jax.ShapeDtypeStruct((L, DFF), bf16), jax.ShapeDtypeStruct((L, D), bf16), jax.ShapeDtypeStruct((1, 128), f32),
                   jax.ShapeDtypeStruct((1, D), f32), jax.ShapeDtypeStruct((1, D), f32), jax.ShapeDtypeStruct((1, DFF), f32),
                   jax.ShapeDtypeStruct((3, DFF), f32)],
        compiler_params=_CP(dimension_semantics=_ARB),
    )(x1, x1, x1, tgt, nw2, modx, w_a, w_g, cw, cb, wdown, wdown_t, fnw)


def _ffn_bwd(x1, dx2, da, dgc, nw2, modx, wup_t, cw, name):
    L = x1.shape[0]
    nb = L // RF
    per = RF // HALO

    def body(x_ref, dx2_ref, da_ref, dgc_ref, dgp_ref, dgn_ref, nw_ref, mx_ref, wu_ref, cw_ref,
             dx1_ref, dag_ref, h2_ref, dnw_ref, dmx_ref):
        i = pl.program_id(0)

        @pl.when(i == 0)
        def _():
            dnw_ref[...] = jnp.zeros_like(dnw_ref)
            dmx_ref[...] = jnp.zeros_like(dmx_ref)

        dgc_b = dgc_ref[...]
        before = dgp_ref[HALO - 1:HALO] * jnp.where(i > 0, 1.0, 0.0)
        after = dgn_ref[0:1] * jnp.where(i < nb - 1, 1.0, 0.0)
        row = lax.broadcasted_iota(jnp.int32, (RF, 1), 0)
        d_prev = jnp.where(row == 0, before, pltpu.roll(dgc_b, 1, axis=0))
        d_next = jnp.where(row == RF - 1, after, pltpu.roll(dgc_b, RF - 1, axis=0))
        dg = cw_ref[0:1] * d_next + cw_ref[1:2] * dgc_b + cw_ref[2:3] * d_prev
        dag = jnp.concatenate([da_ref[...], dg.astype(bf16)], axis=1)
        dag_ref[...] = dag
        dh2 = dnn(dag, wu_ref[...])
        h2, vjp = jax.vjp(lambda a, b, c, d: _mod(_rms(a, b), c, d), x_ref[...], nw_ref[...], mx_ref[3:4], mx_ref[4:5])
        dxa, dnw, dsh, dsc = vjp(dh2)
        h2_ref[...] = h2.astype(bf16)
        dx1_ref[...] = dx2_ref[...] + dxa
        dnw_ref[...] += dnw
        dmx_ref[3:4] += dsh
        dmx_ref[4:5] += dsc

    blk = lambda w: pl.BlockSpec((RF, w), lambda i: (i, 0))
    return pl.pallas_call(
        body, name=name, grid=(nb,),
        in_specs=[blk(D), blk(D), blk(DFF), blk(DFF), pl.BlockSpec((HALO, DFF), lambda i: (jnp.maximum(i * per - 1, 0), 0)),
                  pl.BlockSpec((HALO, DFF), lambda i: (jnp.minimum((i + 1) * per, L // HALO - 1), 0)),
                  _const_spec((1, D)), _const_spec((6, D)), _const_spec((2 * DFF, D)), _const_spec((3, DFF))],
        out_specs=[blk(D), blk(2 * DFF), blk(D), _acc_spec((1, D)), _acc_spec((6, D))],
        out_shape=[jax.ShapeDtypeStruct((L, D), f32), jax.ShapeDtypeStruct((L, 2 * DFF), bf16), jax.ShapeDtypeStruct((L, D), bf16),
                   jax.ShapeDtypeStruct((1, D), f32), jax.ShapeDtypeStruct((6, D), f32)],
        compiler_params=_CP(dimension_semantics=_ARB),
    )(x1, dx2, da, dgc, dgc, dgc, nw2, modx, wup_t, cw)


def _matmul_tn(a, b, name, cargo=None):
    k, m = a.shape
    n = b.shape[1]
    divs = lambda d: [c for c in range(d, 0, -128) if d % c == 0]
    _, tm, tn = min((m * (n // cn) + n * (m // cm), cm, cn) for cm in divs(m) for cn in divs(n) if cm * cn * 4 <= ACC_TILE_BYTES)
    tk = next(c for c in (512, 768, 256, 128) if k % c == 0)
    nk = k // tk
    grid = (m // tm, n // tn, nk)
    cg = _Cargo(cargo)

    def body(*refs):
        (a_ref, b_ref), (o_ref,), (acc,) = cg.split(refs, 2, 1, 1)
        cg.ride(refs, 2, 1, grid)
        q = pl.program_id(2)

        @pl.when(q == 0)
        def _():
            acc[...] = jnp.zeros_like(acc)

        acc[...] += dtn(a_ref[...], b_ref[...])

        @pl.when(q == nk - 1)
        def _():
            o_ref[...] = acc[...].astype(bf16)

    out = pl.pallas_call(
        body, name=name, grid=grid,
        in_specs=[pl.BlockSpec((tk, tm), lambda i, j, q: (q, i)), pl.BlockSpec((tk, tn), lambda i, j, q: (q, j))] + cg.in_specs(),
        out_specs=[pl.BlockSpec((tm, tn), lambda i, j, q: (i, j))] + cg.in_specs(),
        out_shape=[jax.ShapeDtypeStruct((m, n), bf16)] + cg.out_shapes(),
        scratch_shapes=[pltpu.VMEM((tm, tn), f32)] + cg.sems(),
        compiler_params=_CP(dimension_semantics=("arbitrary",) * 3 if cg.n else ("parallel", "parallel", "arbitrary")),
    )(a, b, *cg.arrays)
    return out if cg.n else out[0]


def _adamw_refs(w_ref, g_ref, m_ref, v_ref, d_ref, nm_ref, nv_ref):
    c1, c2 = 1.0 - B1 ** STEP, 1.0 - B2 ** STEP
    gg = g_ref[...]
    nm = B1 * m_ref[...] + (1.0 - B1) * gg
    nv = B2 * v_ref[...] + (1.0 - B2) * jnp.square(gg)
    d_ref[...] = -LR * ((nm / c1) / (jnp.sqrt(nv / c2) + AEPS) + WD * w_ref[...])
    nm_ref[...], nv_ref[...] = nm, nv


def _adamw(w, g, m, v, name):
    def body(*refs):
        _adamw_refs(*refs)

    return pl.pallas_call(body, name=name, out_shape=[jax.ShapeDtypeStruct(w.shape, f32)] * 3, compiler_params=_CP())(w, g, m, v)


def _adamw_landed(land, w, m, v, name):
    def body(l_ref, w_ref, m_ref, v_ref, g_ref, d_ref, nm_ref, nv_ref):
        acc = l_ref[0].astype(f32)
        for j in range(1, NDEV):
            acc = acc + l_ref[j].astype(f32)
        g_ref[...] = acc
        _adamw_refs(w_ref, g_ref, m_ref, v_ref, d_ref, nm_ref, nv_ref)

    return pl.pallas_call(body, name=name, out_shape=[jax.ShapeDtypeStruct(w.shape, f32)] * 4, compiler_params=_CP())(land, w, m, v)


def _adamw_many(ws, gs, ms, vs, name):
    n = len(ws)

    def body(*refs):
        for k in range(n):
            _adamw_refs(*[refs[j * n + k] for j in range(7)])

    outs = pl.pallas_call(body, name=name, out_shape=[jax.ShapeDtypeStruct(w.shape, f32) for w in ws] * 3,
                          compiler_params=_CP())(*ws, *gs, *ms, *vs)
    return outs[:n], outs[n:2 * n], outs[2 * n:]


SMALL = ["conv_w", "c_ctx", "norm1_w", "s5_lambda_re_f", "s5_lambda_im_f", "s5_log_step_f", "s5_lambda_re_b", "s5_lambda_im_b",
         "s5_log_step_b", "s5_b_re", "s5_b_im", "s5_c_re", "s5_c_im", "s5_d", "s5_b_glu", "ret_log_decay_f", "ret_log_decay_b",
         "norm2_w", "conv_b", "final_norm_w"]
WEIGHTS = ["c_ctx", "w_mod", "b_mod", "norm1_w", "w_in", "s5_lambda_re_f", "s5_lambda_im_f", "s5_log_step_f", "s5_lambda_re_b",
           "s5_lambda_im_b", "s5_log_step_b", "s5_b_re", "s5_b_im", "s5_c_re", "s5_c_im", "s5_d", "s5_w_glu", "s5_b_glu",
           "ret_log_decay_f", "ret_log_decay_b", "w_out", "norm2_w", "w_up", "conv_w", "conv_b", "w_down", "final_norm_w"]


def _pack_small(vals):
    flat, offs, o = [], [], 0
    for a in vals:
        n = a.size
        npad = -n % 128
        flat.append(jnp.pad(a.reshape(-1), (0, npad)))
        offs.append((o, n))
        o += n + npad
    tail = -o % 1024
    if tail:
        flat.append(jnp.zeros((tail,), f32))
    return jnp.concatenate(flat).reshape(-1, 128), offs


def _unpack_small(packed, offs, shapes):
    flat = packed.reshape(-1)
    return [flat[o:o + n].reshape(s) for (o, n), s in zip(offs, shapes)]


def _rope_tables(L, nctx_rows):
    t = np.arange(L)
    inv = (ROPE_THETA ** (-np.arange(DH // 4, dtype=np.float64) / (DH // 4))).astype(np.float32)
    ang = np.concatenate([(t // GRID_W).astype(np.float32)[:, None] * inv, (t % GRID_W).astype(np.float32)[:, None] * inv], axis=-1)
    cos = np.repeat(np.cos(ang).astype(np.float32), 2, axis=1)
    sin = np.repeat(np.sin(ang).astype(np.float32), 2, axis=1) * np.tile(np.array([-1.0, 1.0], np.float32), DH // 2)
    cosf = np.concatenate([np.ones((nctx_rows, DH), np.float32), cos], axis=0)
    sins = np.concatenate([np.zeros((nctx_rows, DH), np.float32), sin], axis=0)
    return jnp.asarray(cosf), jnp.asarray(sins)


def kernel(x, c, ctx, c_ctx, w_mod, b_mod, norm1_w, w_in, s5_lambda_re_f, s5_lambda_im_f, s5_log_step_f, s5_lambda_re_b, s5_lambda_im_b, s5_log_step_b, s5_b_re, s5_b_im, s5_c_re, s5_c_im, s5_d, s5_w_glu, s5_b_glu, ret_log_decay_f, ret_log_decay_b, w_out, norm2_w, w_up, conv_w, conv_b, w_down, final_norm_w, loss_target, m_c_ctx, m_w_mod, m_b_mod, m_norm1_w, m_w_in, m_s5_lambda_re_f, m_s5_lambda_im_f, m_s5_log_step_f, m_s5_lambda_re_b, m_s5_lambda_im_b, m_s5_log_step_b, m_s5_b_re, m_s5_b_im, m_s5_c_re, m_s5_c_im, m_s5_d, m_s5_w_glu, m_s5_b_glu, m_ret_log_decay_f, m_ret_log_decay_b, m_w_out, m_norm2_w, m_w_up, m_conv_w, m_conv_b, m_w_down, m_final_norm_w, v_c_ctx, v_w_mod, v_b_mod, v_norm1_w, v_w_in, v_s5_lambda_re_f, v_s5_lambda_im_f, v_s5_log_step_f, v_s5_lambda_re_b, v_s5_lambda_im_b, v_s5_log_step_b, v_s5_b_re, v_s5_b_im, v_s5_c_re, v_s5_c_im, v_s5_d, v_s5_w_glu, v_s5_b_glu, v_ret_log_decay_f, v_ret_log_decay_b, v_w_out, v_norm2_w, v_w_up, v_conv_w, v_conv_b, v_w_down, v_final_norm_w):
    args = dict(locals())
    W = {n: args[n] for n in WEIGHTS}
    M = {n: args["m_" + n] for n in WEIGHTS}
    V = {n: args["v_" + n] for n in WEIGHTS}
    me = _me()
    x2, ctx2, tgt = x[0], ctx[0], loss_target[0]
    L, Lc = x2.shape[0], ctx2.shape[0]
    assert Lc == R and L % R == 0 and L % GRID_W == 0
    nctx = Lc // T

    w_in_tl, w_up_tl = w_in[0].T.astype(bf16), w_up[0].T.astype(bf16)
    w_out_l, w_down_l, w_glu_l = w_out[0].astype(bf16), w_down[0].astype(bf16), s5_w_glu[0].astype(bf16)
    half_up = w_up_tl.shape[0] // 2
    per_cv = conv_w.shape[2]
    conv_pad = jnp.pad(conv_w[0], ((0, 5), (0, 128 * 3 - per_cv)))
    w_in_g, c_g, conv_g = _gather_two_level([w_in_tl, jnp.pad(c, ((0, 7), (0, 0))), conv_pad], "gather_w_in")
    w_in_t = w_in_g.reshape(INC, D)
    conv_f = conv_g[:, :3, :per_cv].transpose(1, 0, 2).reshape(3, DFF)

    c9 = jnp.concatenate([c_g[:, 0, :], c_ctx[None], jnp.zeros((7, D), f32)], axis=0)
    w_mod_l = w_mod[0]
    ncol = w_mod_l.shape[1]
    m_part = _ada_fwd(c9, w_mod_l, "ada_fwd")
    m_all = _all_gather_small(m_part, "gather_mod").transpose(1, 0, 2).reshape(16, 6, D)
    modx, modc = _mod_select(m_all, b_mod.reshape(6, D), "mod_select")

    pair = lambda a, b: jnp.concatenate([a, b], axis=-1)
    bre_g, bim_g = s5_b_re[0].transpose(0, 2, 1), s5_b_im[0].transpose(0, 2, 1)
    cre_g, cim_g = s5_c_re[0], s5_c_im[0]
    shared = (pair(bre_g, bim_g), pair(bim_g, bre_g), pair(cre_g, cim_g), pair(cim_g, cre_g))
    s5p = {}
    for tag, lre, lim, ls in (("f", s5_lambda_re_f, s5_lambda_im_f, s5_log_step_f), ("b", s5_lambda_re_b, s5_lambda_im_b, s5_log_step_b)):
        s5p[tag] = (pair(lre[0], lre[0])[:, None, :], pair(lim[0], lim[0])[:, None, :], ls[0].reshape(S5G, 1, 1)) + shared
    m_f, mb_f, mc_f, a1_f, a2_f = _s5_build(s5p["f"], False, "s5_build_f")
    m_b, mb_b, mc_b, a1_b, a2_b = _s5_build(s5p["b"], True, "s5_build_b")
    a1_f, a2_f, a1_b, a2_b = (a.reshape(S5G, SB) for a in (a1_f, a2_f, a1_b, a2_b))

    nw1, nw2, fnw = norm1_w, norm2_w, final_norm_w[None]
    cosf, sins = _rope_tables(L, Lc)
    p_ext, w_out_g, w_glu_g = _f1_fwd(x2, ctx2, modx, modc, nw1, w_in_t.T, cosf, sins, "f1_fwd", cargo=([w_out_l, w_glu_l], False))
    nctx5 = Lc // TC
    u_g = _to_groups(p_ext[:, :S5W])
    s_f, s_b = _s5_inc(u_g, mb_f, mb_b, "s5_inc")
    hp_f, hp_b = _s5_carry(s_f, s_b, (a1_f, a2_f), (a1_b, a2_b), nctx5, "s5_carry")
    ys = _from_groups(_s5_out(u_g, m_f, m_b, hp_f, hp_b, mc_f, mc_b, "s5_out"))
    ld8 = lambda ld: jnp.pad(jnp.broadcast_to(ld[0][:, None], (RH, 128)), ((0, 8 - RH), (0, 0)))
    ldf8, ldb8 = ld8(ret_log_decay_f), ld8(ret_log_decay_b)
    of, rp_f, w_up_g1 = _ret_fwd(p_ext, ldf8, False, nctx, "ret_fwd_f", cargo=([w_up_tl[:half_up]], False))
    ob, rp_b, w_up_g2 = _ret_fwd(p_ext, ldb8, True, nctx, "ret_fwd_b", cargo=([w_up_tl[half_up:]], False))
    w_out_f, w_glu_f = w_out_g.reshape(D, D), w_glu_g.reshape(S5W, S5W)
    x1, w_down_g = _mix_fwd(x2, ys, of, ob, p_ext, s5_d, s5_b_glu, modx, w_glu_f, w_out_f, "mix_fwd", cargo=([w_down_l], False))
    w_down_f = w_down_g.reshape(DFF, D)
    w_up_t = jnp.concatenate([w_up_g1, w_up_g2], axis=1).reshape(2 * DFF, D)

    (dx2, da, dgc, f_act, dffn, loss_acc, g_fnw, g_gate2, g_cb, g_cw) = _ffn_fwd(
        x1, tgt, nw2, modx, w_up_t[:DFF].T, w_up_t[DFF:].T, conv_f, conv_b, w_down_f, w_down_f.T, fnw, "ffn_fwd")
    dx1, dag, h2, g_nw2, dmx2 = _ffn_bwd(x1, dx2, da, dgc, nw2, modx, w_up_t, conv_f, "ffn_bwd")
    gw_down = _matmul_tn(f_act, dffn, "dw_down").reshape(NDEV, -1, D)
    gw_up_t = _matmul_tn(dag, h2, "dw_up").reshape(NDEV, -1, D)
    (dy_e, dud_e, do_e, dg_e, cat, dmix, s_act, dz, g_d, g_bglu, g_gate1, l_down) = _mix_bwd(
        x2, ys, of, ob, p_ext, s5_d, s5_b_glu, modx, w_glu_f, w_out_f, dx1, "mix_bwd", cargo=([gw_down], True))
    gw_out = _matmul_tn(cat, dmix, "dw_out").reshape(NDEV, -1, D)
    gw_glu = _matmul_tn(s_act, dz, "dw_glu").reshape(NDEV, -1, S5W)
    dq_f, dk_f, dv_f, gld_f, l_up1 = _ret_bwd(p_ext, ldf8, rp_f, do_e, False, nctx, "ret_bwd_f",
                                              cargo=([gw_up_t[:, :half_up]], True))
    dq_b, dk_b, dv_b, gld_b, l_out, l_glu, l_up2 = _ret_bwd(p_ext, ldb8, rp_b, do_e, True, nctx, "ret_bwd_b",
                                                            cargo=([gw_out, gw_glu, gw_up_t[:, half_up:]], True))

    du1, g_m, dhp_f, dhp_b, dmc_f, dmc_b = _s5_out_bwd(_to_groups(dy_e), u_g, m_f, m_b, hp_f, hp_b, mc_f, mc_b, "s5_out_bwd")
    ds_f, da1_f, da2_f = _s5_carry_bwd(dhp_f, hp_f, a1_f, a2_f, False, nctx5, "s5_carry_bwd_f")
    ds_b, da1_b, da2_b = _s5_carry_bwd(dhp_b, hp_b, a1_b, a2_b, True, nctx5, "s5_carry_bwd_b")
    du_g, dmb_f, dmb_b = _s5_inc_bwd(du1, u_g, ds_f, ds_b, mb_f, mb_b, "s5_inc_bwd")
    zero_p = jnp.zeros((S5G, S5P, SB), f32)
    gf = _s5_build_bwd(s5p["f"], (g_m, dmb_f, dmc_f, da1_f[:, None, :], da2_f[:, None, :]), (zero_p, zero_p), False, "s5_build_bwd_f")
    gb = _s5_build_bwd(s5p["b"], (g_m, dmb_b, dmc_b, da1_b[:, None, :], da2_b[:, None, :]), (gf[3], gf[4]), True, "s5_build_bwd_b")
    g_bre, g_bim = gb[3][:, :, :S5N].transpose(0, 2, 1), gb[3][:, :, S5N:].transpose(0, 2, 1)
    g_cre, g_cim = gb[4][:, :, :S5N], gb[4][:, :, S5N:]

    early = {
        "conv_w": g_cw, "s5_lambda_re_f": gf[0][:, 0, :S5N], "s5_lambda_im_f": gf[1][:, 0, :S5N],
        "s5_log_step_f": gf[2], "s5_lambda_re_b": gb[0][:, 0, :S5N], "s5_lambda_im_b": gb[1][:, 0, :S5N], "s5_log_step_b": gb[2],
        "s5_b_re": g_bre, "s5_b_im": g_bim, "s5_c_re": g_cre, "s5_c_im": g_cim, "s5_d": g_d, "s5_b_glu": g_bglu,
        "ret_log_decay_f": gld_f[:RH, 0], "ret_log_decay_b": gld_b[:RH, 0], "norm2_w": g_nw2, "conv_b": g_cb, "final_norm_w": g_fnw,
    }
    e_names = [n for n in SMALL if n in early]
    packed_e, eoffs = _pack_small([early[n].astype(f32) for n in e_names])
    grad_x, dp_ext, h1, g_nw1, dmx1, dmc1 = _f1_bwd(
        x2, ctx2, modx, modc, nw1, w_in_t, cosf, sins, dx1, (_from_groups(du_g), dud_e, dq_f, dq_b, dk_f, dk_b, dv_f, dv_b, dg_e), "f1_bwd")
    gw_in_t, land_e = _matmul_tn(dp_ext, h1, "dw_in", cargo=([packed_e], False))
    g_in_t = _reduce_scatter_two_level(gw_in_t.reshape(NDEV, -1, D), "scatter_dw_in")

    dmx = dmx1 + dmx2
    dmx = dmx.at[2].set(g_gate1[0]).at[5].set(g_gate2[0])
    dm_me = jnp.stack([dmx.reshape(-1), dmc1.reshape(-1)], axis=0)
    dm_all = _all_gather_small(dm_me.reshape(8, -1), "gather_dmod").reshape(NDEV, 2, 6 * D)
    dmx_all, dmc_all = dm_all[:, 0, :], dm_all[:, 1, :]
    my_cols = lambda a: lax.dynamic_slice(a, (0, me * ncol), (NDEV, ncol))
    gw_mod, g_bmod, dc9 = _ada_bwd(c9, dmx_all, dmc_all, my_cols(dmx_all), my_cols(dmc_all), w_mod_l, "ada_bwd")

    sshape = lambda n: (3, DFF) if n == "conv_w" else W[n].shape
    G = dict(zip(e_names, _unpack_small(_sum8(land_e, "reduce_early"), eoffs, [sshape(n) for n in e_names])))
    late = {"c_ctx": dc9[8], "norm1_w": g_nw1}
    packed_l, loffs = _pack_small([late[n].astype(f32) for n in late])
    G.update(zip(late, _unpack_small(_all_reduce_small(packed_l, "reduce_late"), loffs, [W[n].shape for n in late])))
    G["conv_w"] = lax.dynamic_slice(G["conv_w"], (0, me * per_cv), (3, per_cv))[None]
    G["b_mod"] = g_bmod.reshape(b_mod.shape)
    G["w_mod"] = gw_mod[None]
    G["w_in"] = g_in_t.T[None]
    G["w_up"] = jnp.concatenate([_sum8(l_up1, "sum_dw_up1"), _sum8(l_up2, "sum_dw_up2")], axis=0).T[None]

    delta, new_m, new_v = {}, {}, {}
    sm_names = SMALL[1:] + ["b_mod"]
    rows = lambda a: a.reshape(-1, a.shape[-1])
    outs = _adamw_many(*[[rows(d[n]) for n in sm_names] for d in (W, G, M, V)], "adamw_small")
    for dst, src in zip((delta, new_m, new_v), outs):
        dst.update({n: a.reshape(W[n].shape) for n, a in zip(sm_names, src)})
    for n in ["w_mod", "w_in", "w_up", "conv_w"]:
        d, nm, nv = _adamw(W[n][0], G[n][0], M[n][0], V[n][0], "adamw_" + n)
        delta[n], new_m[n], new_v[n] = d[None], nm[None], nv[None]
    for n, land in (("w_out", l_out), ("w_down", l_down), ("s5_w_glu", l_glu)):
        g, d, nm, nv = _adamw_landed(land, W[n][0], M[n][0], V[n][0], "adamw_" + n)
        G[n], delta[n], new_m[n], new_v[n] = g[None], d[None], nm[None], nv[None]

    loss = lax.psum(loss_acc[0, 0], ("x", "y", "c"))
    return (loss, grad_x[None], *[G[n] for n in WEIGHTS], *[delta[n] for n in WEIGHTS], *[new_m[n] for n in WEIGHTS],
            *[new_v[n] for n in WEIGHTS])
```

```python
import functools

import numpy as np
import jax
import jax.numpy as jnp
from jax import lax
from jax.experimental import pallas as pl
from jax.experimental.pallas import tpu as pltpu

f32, bf16 = jnp.float32, jnp.bfloat16

D = 1024
S5W, S5G, S5P, S5N = 512, 32, 16, 64
TC = 16
TCP = TC * S5P
SB = 2 * S5N
GBK = 8
CARRY_UNROLL = 8
RH, DH = 4, 128
RW = RH * DH
INC = S5W + 4 * RW
DFF = 2816
T = 128
R = 256
RF = 128
HALO = 8
EPS = 1e-6
ROPE_THETA = 10000.0
GRID_W = 64
NDEV = 8
LR, B1, B2, AEPS, WD, STEP = 0.001, 0.9, 0.999, 1e-08, 0.01, 10
VMEM_LIMIT = 60 * 1024 * 1024
ACC_TILE_BYTES = 6 * 1024 * 1024
MESH = pl.DeviceIdType.MESH

_CP = functools.partial(pltpu.CompilerParams, vmem_limit_bytes=VMEM_LIMIT)
_ARB = ("arbitrary",)
_ANY = pl.BlockSpec(memory_space=pl.ANY)


def _dg(a, b, dims):
    return lax.dot_general(a.astype(bf16), b.astype(bf16), (dims, ((), ())), preferred_element_type=f32)


@jax.custom_vjp
def dnn(a, b):
    return _dg(a, b, ((1,), (0,)))


@jax.custom_vjp
def dnt(a, b):
    return _dg(a, b, ((1,), (1,)))


@jax.custom_vjp
def dtn(a, b):
    return _dg(a, b, ((0,), (0,)))


dnn.defvjp(lambda a, b: (dnn(a, b), (a, b)), lambda r, g: (dnt(g, r[1]).astype(r[0].dtype), dtn(r[0], g).astype(r[1].dtype)))
dnt.defvjp(lambda a, b: (dnt(a, b), (a, b)), lambda r, g: (dnn(g, r[1]).astype(r[0].dtype), dtn(g, r[0]).astype(r[1].dtype)))
dtn.defvjp(lambda a, b: (dtn(a, b), (a, b)), lambda r, g: (dnt(r[1], g).astype(r[0].dtype), dnn(r[0], g).astype(r[1].dtype)))


@jax.custom_vjp
def _dnn_const(a, w, wt):
    return dnn(a, w)


_dnn_const.defvjp(lambda a, w, wt: (dnn(a, w), wt), lambda wt, g: (dnn(g, wt), None, None))


_GELU_C0, _GELU_C1 = float(np.sqrt(2.0 / np.pi)), 0.044715


@jax.custom_vjp
def _gelu(x):
    return _gelu_fwd(x)[0]


def _gelu_fwd(x):
    t = jnp.tanh(_GELU_C0 * (x + _GELU_C1 * (x * x * x)))
    return x * (0.5 * (1.0 + t)), (x, t)


def _gelu_bwd(res, g):
    x, t = res
    return (g * (0.5 * (1.0 + t) + (0.5 * _GELU_C0) * x * (1.0 - t * t) * (1.0 + (3.0 * _GELU_C1) * (x * x))),)


_gelu.defvjp(_gelu_fwd, _gelu_bwd)


def _rms(t, w):
    return t * lax.rsqrt(jnp.mean(t * t, axis=-1, keepdims=True) + EPS) * w


def _mod(h, shift, scale):
    return h * (1.0 + scale) + shift


def _const_spec(shape):
    n = len(shape)
    return pl.BlockSpec(shape, lambda i, _n=n: (0,) * _n, pipeline_mode=pl.Buffered(1))


def _acc_spec(shape):
    n = len(shape)
    return pl.BlockSpec(shape, lambda i, _n=n: (0,) * _n)


def _me():
    return 4 * lax.axis_index("x") + 2 * lax.axis_index("y") + lax.axis_index("c")


def _peer(r):
    x, y, c = lax.axis_index("x"), lax.axis_index("y"), lax.axis_index("c")
    px = 1 - x if (r >> 2) & 1 else x
    py = 1 - y if (r >> 1) & 1 else y
    pc = 1 - c if r & 1 else c
    return (px, py, pc), 4 * px + 2 * py + pc


def _all_gather_small(v, name):
    r, c = v.shape

    def body(v_ref, out_ref, send_sems, recv_sems):
        me = _me()
        out_ref[me] = v_ref[...]
        sends = []
        for k in range(1, NDEV):
            peer, _ = _peer(k)
            cp = pltpu.make_async_remote_copy(src_ref=v_ref, dst_ref=out_ref.at[me], send_sem=send_sems.at[k - 1],
                                              recv_sem=recv_sems.at[k - 1], device_id=peer, device_id_type=MESH)
            cp.start()
            sends.append(cp)
        for k in range(1, NDEV):
            peer, pidx = _peer(k)
            pltpu.make_async_remote_copy(src_ref=v_ref, dst_ref=out_ref.at[pidx], send_sem=send_sems.at[k - 1],
                                         recv_sem=recv_sems.at[k - 1], device_id=peer, device_id_type=MESH).wait_recv()
        for cp in sends:
            cp.wait_send()

    return pl.pallas_call(
        body, name=name, out_shape=jax.ShapeDtypeStruct((NDEV, r, c), v.dtype),
        in_specs=[pl.BlockSpec(memory_space=pltpu.VMEM)], out_specs=pl.BlockSpec(memory_space=pltpu.VMEM),
        scratch_shapes=[pltpu.SemaphoreType.DMA((NDEV - 1,)), pltpu.SemaphoreType.DMA((NDEV - 1,))],
        compiler_params=_CP(),
    )(v)


def _all_reduce_small(v, name):
    r, c = v.shape

    def body(v_ref, out_ref, land, send_sems, recv_sems):
        me = _me()
        land[me] = v_ref[...]
        sends = []
        for k in range(1, NDEV):
            peer, _ = _peer(k)
            cp = pltpu.make_async_remote_copy(src_ref=v_ref, dst_ref=land.at[me], send_sem=send_sems.at[k - 1],
                                              recv_sem=recv_sems.at[k - 1], device_id=peer, device_id_type=MESH)
            cp.start()
            sends.append(cp)
        for k in range(1, NDEV):
            peer, pidx = _peer(k)
            pltpu.make_async_remote_copy(src_ref=v_ref, dst_ref=land.at[pidx], send_sem=send_sems.at[k - 1],
                                         recv_sem=recv_sems.at[k - 1], device_id=peer, device_id_type=MESH).wait_recv()
        for cp in sends:
            cp.wait_send()
        acc = land[0]
        for j in range(1, NDEV):
            acc = acc + land[j]
        out_ref[...] = acc

    return pl.pallas_call(
        body, name=name, out_shape=jax.ShapeDtypeStruct((r, c), v.dtype),
        in_specs=[pl.BlockSpec(memory_space=pltpu.VMEM)], out_specs=pl.BlockSpec(memory_space=pltpu.VMEM),
        scratch_shapes=[pltpu.VMEM((NDEV, r, c), v.dtype), pltpu.SemaphoreType.DMA((NDEV - 1,)),
                        pltpu.SemaphoreType.DMA((NDEV - 1,))],
        compiler_params=_CP(),
    )(v)


class _Exchange:
    def __init__(self, srcs, dsts, send_sems, recv_sems, local_sems, scatter):
        me = _me()
        n = len(srcs)
        self.sends, self.recvs, self.locals = [], [], []
        for a, (s, d) in enumerate(zip(srcs, dsts)):
            self.locals.append(pltpu.make_async_copy(s.at[me] if scatter else s, d.at[me], local_sems.at[a]))
        for k in range(1, NDEV):
            peer, pidx = _peer(k)
            for a, (s, d) in enumerate(zip(srcs, dsts)):
                src = s.at[pidx] if scatter else s
                sem = (k - 1) * n + a
                for dst, out in ((d.at[me], self.sends), (d.at[pidx], self.recvs)):
                    out.append(pltpu.make_async_remote_copy(src_ref=src, dst_ref=dst, send_sem=send_sems.at[sem],
                                                            recv_sem=recv_sems.at[sem], device_id=peer, device_id_type=MESH))

    def start(self):
        for cp in self.locals + self.sends:
            cp.start()

    def wait(self):
        for cp in self.recvs:
            cp.wait_recv()
        for cp in self.sends:
            cp.wait_send()
        for cp in self.locals:
            cp.wait()


def _exchange_shapes(arrays, scatter):
    return [jax.ShapeDtypeStruct(a.shape if scatter else (NDEV,) + a.shape, a.dtype) for a in arrays]


def _exchange_sems(n):
    return [pltpu.SemaphoreType.DMA(((NDEV - 1) * n,)), pltpu.SemaphoreType.DMA(((NDEV - 1) * n,)), pltpu.SemaphoreType.DMA((n,))]


def _exchange(arrays, scatter, name):
    n = len(arrays)

    def body(*refs):
        ex = _Exchange(refs[:n], refs[n:2 * n], *refs[2 * n:], scatter)
        ex.start()
        ex.wait()

    return pl.pallas_call(body, name=name, out_shape=_exchange_shapes(arrays, scatter), in_specs=[_ANY] * n,
                          out_specs=[_ANY] * n, scratch_shapes=_exchange_sems(n), compiler_params=_CP())(*arrays)


def _chips():
    x, y, c = lax.axis_index("x"), lax.axis_index("y"), lax.axis_index("c")
    return (x, y, c), (x, y, 1 - c), [(1 - x, y), (x, 1 - y), (1 - x, 1 - y)]


def _gather_two_level(arrays, name):
    n = len(arrays)

    def body(*refs):
        srcs, outs = refs[:n], refs[n:2 * n]
        send_sems, recv_sems = refs[2 * n:]
        me, sibling, chips = _chips()
        c = me[2]
        idx = lambda p: 4 * p[0] + 2 * p[1] + p[2]

        def copy(a, k, block, to, src=None):
            return pltpu.make_async_remote_copy(
                src_ref=outs[a].at[idx(block)] if src is None else src, dst_ref=outs[a].at[idx(block)],
                send_sem=send_sems.at[7 * a + k], recv_sem=recv_sems.at[7 * a + k], device_id=to, device_id_type=MESH)

        first, passed = [], []
        for a in range(n):
            outs[a][idx(me)] = srcs[a][...]
            first += [copy(a, 0, me, sibling, src=srcs[a])]
            first += [copy(a, 1 + j, me, (*chip, c), src=srcs[a]) for j, chip in enumerate(chips)]
        for cp in first:
            cp.start()
        for a in range(n):
            for j, chip in enumerate(chips):
                copy(a, 1 + j, (*chip, c), me).wait_recv()
                cp = copy(a, 4 + j, (*chip, c), sibling)
                cp.start()
                passed.append(cp)
        for a in range(n):
            copy(a, 0, sibling, me).wait_recv()
            for j, chip in enumerate(chips):
                copy(a, 4 + j, (*chip, 1 - c), me).wait_recv()
        for cp in first + passed:
            cp.wait_send()

    vm = pl.BlockSpec(memory_space=pltpu.VMEM)
    return pl.pallas_call(
        body, name=name, out_shape=[jax.ShapeDtypeStruct((NDEV,) + a.shape, a.dtype) for a in arrays],
        in_specs=[vm] * n, out_specs=[vm] * n,
        scratch_shapes=[pltpu.SemaphoreType.DMA((7 * n,)), pltpu.SemaphoreType.DMA((7 * n,))],
        compiler_params=_CP(),
    )(*arrays)


def _reduce_scatter_two_level(g, name):
    _, r, c = g.shape
    nchip = NDEV // 2

    def body(g_ref, o_ref, stage, part, land, d_send, d_recv, i_send, i_recv):
        me, sibling, chips = _chips()
        x, y, cc = me
        mine = 2 * x + y

        def blk(k, core):
            return 2 * k + core

        swaps = [pltpu.make_async_remote_copy(src_ref=g_ref.at[blk(k, 1 - cc)], dst_ref=stage.at[k], send_sem=d_send.at[k],
                                              recv_sem=d_recv.at[k], device_id=sibling, device_id_type=MESH) for k in range(nchip)]
        for cp in swaps:
            cp.start()
        for cp in swaps:
            cp.wait_recv()
        for k in range(nchip):
            part[k] = (g_ref[blk(k, cc)].astype(f32) + stage[k].astype(f32)).astype(bf16)
        sends = []
        for j, chip in enumerate(chips):
            kd = 2 * chip[0] + chip[1]
            cp = pltpu.make_async_remote_copy(src_ref=part.at[kd], dst_ref=land.at[mine], send_sem=i_send.at[j],
                                              recv_sem=i_recv.at[j], device_id=(*chip, cc), device_id_type=MESH)
            cp.start()
            sends.append(cp)
        land[mine] = part[mine]
        for j, chip in enumerate(chips):
            ks = 2 * chip[0] + chip[1]
            pltpu.make_async_remote_copy(src_ref=part.at[ks], dst_ref=land.at[ks], send_sem=i_send.at[j], recv_sem=i_recv.at[j],
                                         device_id=(*chip, cc), device_id_type=MESH).wait_recv()
        for cp in swaps + sends:
            cp.wait_send()
        acc = land[0].astype(f32)
        for k in range(1, nchip):
            acc = acc + land[k].astype(f32)
        o_ref[...] = acc

    vm = pl.BlockSpec(memory_space=pltpu.VMEM)
    return pl.pallas_call(
        body, name=name, out_shape=jax.ShapeDtypeStruct((r, c), f32), in_specs=[vm], out_specs=vm,
        scratch_shapes=[pltpu.VMEM((nchip, r, c), g.dtype)] * 3 + [pltpu.SemaphoreType.DMA((nchip,)), pltpu.SemaphoreType.DMA((nchip,)),
                                                                   pltpu.SemaphoreType.DMA((3,)), pltpu.SemaphoreType.DMA((3,))],
        compiler_params=_CP(),
    )(g)


class _Cargo:
    def __init__(self, cargo):
        self.arrays, self.scatter = cargo if cargo else ([], False)
        self.n = len(self.arrays)

    def in_specs(self):
        return [_ANY] * self.n

    def out_shapes(self):
        return _exchange_shapes(self.arrays, self.scatter)

    def sems(self):
        return _exchange_sems(self.n) if self.n else []

    def split(self, refs, n_in, n_out, n_scratch):
        n = self.n
        return refs[:n_in], refs[n_in + n:n_in + n + n_out], refs[n_in + 2 * n + n_out:n_in + 2 * n + n_out + n_scratch]

    def ride(self, refs, n_in, n_out, grid):
        if not self.n:
            return
        n = self.n
        ex = _Exchange(refs[n_in:n_in + n], refs[n_in + n + n_out:n_in + 2 * n + n_out], *refs[-3:], self.scatter)
        grid = (grid,) if isinstance(grid, int) else tuple(grid)
        first = functools.reduce(jnp.logical_and, [pl.program_id(a) == 0 for a in range(len(grid))])
        last = functools.reduce(jnp.logical_and, [pl.program_id(a) == g - 1 for a, g in enumerate(grid)])

        @pl.when(first)
        def _():
            ex.start()

        @pl.when(last)
        def _():
            ex.wait()


def _sum8(land, name):
    _, r, c = land.shape
    rb = next((b for b in (256, 64, 32) if r % b == 0), r)

    def body(l_ref, o_ref):
        acc = l_ref[0].astype(f32)
        for j in range(1, NDEV):
            acc = acc + l_ref[j].astype(f32)
        o_ref[...] = acc

    return pl.pallas_call(
        body, name=name, grid=(r // rb,), out_shape=jax.ShapeDtypeStruct((r, c), f32),
        in_specs=[pl.BlockSpec((NDEV, rb, c), lambda i: (0, i, 0))], out_specs=pl.BlockSpec((rb, c), lambda i: (i, 0)),
        compiler_params=_CP(dimension_semantics=("parallel",)),
    )(land)


def _ada_fwd(c9, w_mod_l, name):
    def body(c_ref, w_ref, o_ref):
        o_ref[...] = dnn(jax.nn.silu(c_ref[...]), w_ref[...])

    return pl.pallas_call(body, name=name, out_shape=jax.ShapeDtypeStruct((16, w_mod_l.shape[1]), f32),
                          compiler_params=_CP())(c9, w_mod_l)


def _mod_select(m_all, b_mod6, name):
    def body(m_ref, b_ref, mx_ref, mc_ref):
        me = _me()
        mx_ref[...] = m_ref[me] + b_ref[...]
        mc_ref[...] = m_ref[8] + b_ref[...]

    return pl.pallas_call(body, name=name, out_shape=[jax.ShapeDtypeStruct((6, D), f32)] * 2, compiler_params=_CP())(m_all, b_mod6)


def _ada_bwd(c9, dmx_all, dmc_all, dmx_l, dmc_l, w_mod_l, name):
    ncol = w_mod_l.shape[1]

    def rowsum(r):
        acc = r[0:1]
        for j in range(1, NDEV):
            acc = acc + r[j:j + 1]
        return acc

    def body(c_ref, xa_ref, ca_ref, xl_ref, cl_ref, w_ref, gw_ref, gb_ref, dc_ref):
        s9, vjp = jax.vjp(jax.nn.silu, c_ref[...])
        dm9 = jnp.concatenate([xl_ref[...], rowsum(cl_ref[...]), jnp.zeros((7, ncol), f32)], axis=0)
        gw_ref[...] = dtn(s9, dm9)
        gb_ref[...] = rowsum(xa_ref[...]) + rowsum(ca_ref[...])
        dc_ref[...] = vjp(dnt(dm9, w_ref[...]))[0]

    return pl.pallas_call(
        body, name=name,
        out_shape=[jax.ShapeDtypeStruct((D, ncol), f32), jax.ShapeDtypeStruct((1, 6 * D), f32), jax.ShapeDtypeStruct((16, D), f32)],
        compiler_params=_CP())(c9, dmx_all, dmc_all, dmx_l, dmc_l, w_mod_l)


def _lane_sign(rank):
    shape = (1,) * (rank - 1) + (SB,)
    return jnp.where(lax.broadcasted_iota(jnp.int32, shape, rank - 1) < S5N, -1.0, 1.0)


def _s5_build_fn(lre2, lim2, ls, bn, bs, cn, cs, rev):
    sg = _lane_sign(3)
    s = jnp.exp(ls)
    ar, ai = lre2 * s, lim2 * s
    e = jnp.exp(ar)
    nr, ni = e * jnp.cos(ai) - 1.0, e * jnp.sin(ai)
    den = lre2 * lre2 + lim2 * lim2
    cr, ci = (nr * lre2 + ni * lim2) / den, (ni * lre2 - nr * lim2) / den
    bbn = cr * bn + (ci * sg) * bs
    bbs = cr * bs - (ci * sg) * bn

    def powers(ex):
        m, ang = jnp.exp(ex * ar), ex * ai
        return m * jnp.cos(ang), m * jnp.sin(ang) * sg

    def times(tabs, xn, xs):
        f1, f2 = tabs
        return f1[:, :, None, :] * xn[:, None, :, :] + f2[:, :, None, :] * xs[:, None, :, :]

    t = lax.broadcasted_iota(jnp.int32, (1, TC, 1), 1).astype(f32)
    if rev:
        e_src, e_dst, e_out, e_in = t - (TC - 1.0), (TC - 1.0) - t, t, TC - t
    else:
        e_src, e_dst, e_out, e_in = -t, t, (TC - 1.0) - t, t + 1.0
    g = lre2.shape[0]
    flat = lambda a: a.reshape(g, TCP, SB)
    conj = -_lane_sign(4)
    ll = flat(times(powers(e_src), bbn, bbs))
    rr = flat(times(powers(e_dst), cn, cs) * conj)
    mb = flat(times(powers(e_out), bbn, bbs))
    mct = flat(times(powers(e_in), cn, cs) * conj)
    a1, a2 = powers(float(TC))
    row = lax.broadcasted_iota(jnp.int32, (TCP, TCP), 0) // S5P
    col = lax.broadcasted_iota(jnp.int32, (TCP, TCP), 1) // S5P
    mask = jnp.where((col <= row) if rev else (col >= row), 1.0, 0.0)
    m = jnp.concatenate([dnt(ll[j], rr[j])[None] for j in range(g)], axis=0) * mask
    return m, mb, mct, a1, a2


def _gspec(*tail):
    nt = len(tail)
    return pl.BlockSpec((GBK,) + tail, lambda i, _n=nt: (i,) + (0,) * _n)


def _s5_build(params, rev, name):
    def body(l1, l2, ls, bn, bs, cn, cs, m_ref, mb_ref, mc_ref, a1_ref, a2_ref):
        m, mb, mct, a1, a2 = _s5_build_fn(l1[...], l2[...], ls[...], bn[...], bs[...], cn[...], cs[...], rev)
        m_ref[...], mb_ref[...], mc_ref[...] = m.astype(bf16), mb.astype(bf16), mct.astype(bf16)
        a1_ref[...], a2_ref[...] = a1, a2

    vec, pm = _gspec(1, SB), _gspec(S5P, SB)
    return pl.pallas_call(
        body, name=name, grid=(S5G // GBK,),
        in_specs=[vec, vec, _gspec(1, 1), pm, pm, pm, pm],
        out_specs=[_gspec(TCP, TCP), _gspec(TCP, SB), _gspec(TCP, SB), vec, vec],
        out_shape=[jax.ShapeDtypeStruct((S5G, TCP, TCP), bf16), jax.ShapeDtypeStruct((S5G, TCP, SB), bf16),
                   jax.ShapeDtypeStruct((S5G, TCP, SB), bf16), jax.ShapeDtypeStruct((S5G, 1, SB), f32),
                   jax.ShapeDtypeStruct((S5G, 1, SB), f32)],
        compiler_params=_CP(dimension_semantics=("parallel",)),
    )(*params)


def _s5_build_bwd(params, cots, prev, rev, name):
    def body(l1, l2, ls, bn, bs, cn, cs, dm, dmb, dmc, da1, da2, pb, pc, gl1, gl2, gls, gb, gc):
        prim = (l1[...], l2[...], ls[...], bn[...], bs[...], cn[...], cs[...])
        _, vjp = jax.vjp(functools.partial(_s5_build_fn, rev=rev), *prim)
        d1, d2, dls, dbn, dbs, dcn, dcs = vjp((dm[...], dmb[...], dmc[...], da1[...], da2[...]))
        gl1[...] = d1 + pltpu.roll(d1, S5N, axis=2)
        gl2[...] = d2 + pltpu.roll(d2, S5N, axis=2)
        gls[...] = dls
        gb[...] = dbn + pltpu.roll(dbs, S5N, axis=2) + pb[...]
        gc[...] = dcn + pltpu.roll(dcs, S5N, axis=2) + pc[...]

    vec, pm, big = _gspec(1, SB), _gspec(S5P, SB), _gspec(TCP, SB)
    return pl.pallas_call(
        body, name=name, grid=(S5G // GBK,),
        in_specs=[vec, vec, _gspec(1, 1), pm, pm, pm, pm, _gspec(TCP, TCP), big, big, vec, vec, pm, pm],
        out_specs=[vec, vec, _gspec(1, 1), pm, pm],
        out_shape=[jax.ShapeDtypeStruct((S5G, 1, SB), f32), jax.ShapeDtypeStruct((S5G, 1, SB), f32),
                   jax.ShapeDtypeStruct((S5G, 1, 1), f32), jax.ShapeDtypeStruct((S5G, S5P, SB), f32),
                   jax.ShapeDtypeStruct((S5G, S5P, SB), f32)],
        compiler_params=_CP(dimension_semantics=("parallel",)),
    )(*params, *cots, *prev)


def _s5_inc(u, mb_f, mb_b, name):
    nc = u.shape[1]

    def body(u_ref, mf_ref, mb_ref, sf_ref, sb_ref):
        for j in range(GBK):
            sf_ref[:, j, :] = jnp.dot(u_ref[j], mf_ref[j], preferred_element_type=f32)
            sb_ref[:, j, :] = jnp.dot(u_ref[j], mb_ref[j], preferred_element_type=f32)

    sspec = pl.BlockSpec((nc, GBK, SB), lambda i: (0, i, 0))
    return pl.pallas_call(
        body, name=name, grid=(S5G // GBK,), in_specs=[_gspec(nc, TCP), _gspec(TCP, SB), _gspec(TCP, SB)],
        out_specs=[sspec, sspec], out_shape=[jax.ShapeDtypeStruct((nc, S5G, SB), f32)] * 2,
        compiler_params=_CP(dimension_semantics=("parallel",)),
    )(u, mb_f, mb_b)


def _idx_fwd(nctx, nch):
    return lambda i: i


def _idx_rev(nctx, nch):
    return lambda i: jnp.where(i < nctx, nctx - 1 - i, nch + nctx - 1 - i)


def _carry_loop(nc, step, init):
    def trip(i, c):
        for k in range(CARRY_UNROLL):
            c = step(i * CARRY_UNROLL + k, c)
        return c

    return lax.fori_loop(0, nc // CARRY_UNROLL, trip, init)


def _s5_carry(s_f, s_b, a_f, a_b, nctx, name):
    nc = s_f.shape[0]
    idx_b = _idx_rev(nctx, nc)

    def body(sf_ref, sb_ref, f1_ref, f2_ref, b1_ref, b2_ref, hf_ref, hb_ref):
        f1, f2, b1, b2 = f1_ref[...], f2_ref[...], b1_ref[...], b2_ref[...]

        def step(i, c):
            hf, hfs, hb, hbs = c
            rb = idx_b(i)
            hf_ref[i] = hf
            hb_ref[rb] = hb
            sf, sb = sf_ref[i], sb_ref[rb]
            return (f1 * hf + f2 * hfs + sf, f1 * hfs - f2 * hf + pltpu.roll(sf, S5N, axis=1),
                    b1 * hb + b2 * hbs + sb, b1 * hbs - b2 * hb + pltpu.roll(sb, S5N, axis=1))

        z = jnp.zeros((S5G, SB), f32)
        _carry_loop(nc, step, (z, z, z, z))

    return pl.pallas_call(body, name=name, out_shape=[jax.ShapeDtypeStruct(s_f.shape, f32)] * 2,
                          compiler_params=_CP())(s_f, s_b, *a_f, *a_b)


def _s5_carry_bwd(dhp, hp, a1, a2, rev, nctx, name):
    nc = hp.shape[0]
    idx = (_idx_rev if rev else _idx_fwd)(nctx, nc)

    def body(dhp_ref, hp_ref, a1_ref, a2_ref, ds_ref, d1_ref, d2_ref):
        f1, f2 = a1_ref[...], a2_ref[...]

        def step(k, carry):
            ab, abs_, d1, d2 = carry
            r = idx(nc - 1 - k)
            ds_ref[r] = ab
            h, dh = hp_ref[r], dhp_ref[r]
            return (dh + f1 * ab - f2 * abs_, pltpu.roll(dh, S5N, axis=1) + f1 * abs_ + f2 * ab,
                    d1 + ab * h, d2 + ab * pltpu.roll(h, S5N, axis=1))

        z = jnp.zeros((S5G, SB), f32)
        _, _, d1, d2 = _carry_loop(nc, step, (z, z, z, z))
        d1_ref[...], d2_ref[...] = d1, d2

    return pl.pallas_call(
        body, name=name,
        out_shape=[jax.ShapeDtypeStruct(hp.shape, f32), jax.ShapeDtypeStruct((S5G, SB), f32), jax.ShapeDtypeStruct((S5G, SB), f32)],
        compiler_params=_CP())(dhp, hp, a1, a2)


def _s5_out(u, m_f, m_b, hp_f, hp_b, mc_f, mc_b, name):
    nc = u.shape[1]

    def body(u_ref, mf_ref, mb_ref, hf_ref, hb_ref, cf_ref, cb_ref, y_ref):
        for j in range(GBK):
            uj = u_ref[j]
            y_ref[j] = (jnp.dot(uj, mf_ref[j], preferred_element_type=f32) + jnp.dot(uj, mb_ref[j], preferred_element_type=f32)
                        + dnt(hf_ref[:, j, :], cf_ref[j]) + dnt(hb_ref[:, j, :], cb_ref[j])).astype(bf16)

    sspec = pl.BlockSpec((nc, GBK, SB), lambda i: (0, i, 0))
    return pl.pallas_call(
        body, name=name, grid=(S5G // GBK,),
        in_specs=[_gspec(nc, TCP), _gspec(TCP, TCP), _gspec(TCP, TCP), sspec, sspec, _gspec(TCP, SB), _gspec(TCP, SB)],
        out_specs=_gspec(nc, TCP), out_shape=jax.ShapeDtypeStruct((S5G, nc, TCP), bf16),
        compiler_params=_CP(dimension_semantics=("parallel",)),
    )(u, m_f, m_b, hp_f, hp_b, mc_f, mc_b)


def _s5_out_bwd(dy, u, m_f, m_b, hp_f, hp_b, mc_f, mc_b, name):
    nc = u.shape[1]

    def body(dy_ref, u_ref, mf_ref, mb_ref, hf_ref, hb_ref, cf_ref, cb_ref, du_ref, g_ref, dhf_ref, dhb_ref, dcf_ref, dcb_ref):
        for j in range(GBK):
            dyj = dy_ref[j]
            du_ref[j] = dnt(dyj, mf_ref[j]) + dnt(dyj, mb_ref[j])
            g_ref[j] = dtn(u_ref[j], dyj)
            dhf_ref[:, j, :] = dnn(dyj, cf_ref[j])
            dhb_ref[:, j, :] = dnn(dyj, cb_ref[j])
            dcf_ref[j] = dtn(dyj, hf_ref[:, j, :])
            dcb_ref[j] = dtn(dyj, hb_ref[:, j, :])

    sspec = pl.BlockSpec((nc, GBK, SB), lambda i: (0, i, 0))
    sshape = jax.ShapeDtypeStruct((nc, S5G, SB), f32)
    cshape = jax.ShapeDtypeStruct((S5G, TCP, SB), f32)
    return pl.pallas_call(
        body, name=name, grid=(S5G // GBK,),
        in_specs=[_gspec(nc, TCP), _gspec(nc, TCP), _gspec(TCP, TCP), _gspec(TCP, TCP), sspec, sspec, _gspec(TCP, SB), _gspec(TCP, SB)],
        out_specs=[_gspec(nc, TCP), _gspec(TCP, TCP), sspec, sspec, _gspec(TCP, SB), _gspec(TCP, SB)],
        out_shape=[jax.ShapeDtypeStruct((S5G, nc, TCP), f32), jax.ShapeDtypeStruct((S5G, TCP, TCP), f32), sshape, sshape, cshape, cshape],
        compiler_params=_CP(dimension_semantics=("parallel",)),
    )(dy, u, m_f, m_b, hp_f, hp_b, mc_f, mc_b)


def _s5_inc_bwd(du1, u, ds_f, ds_b, mb_f, mb_b, name):
    nc = u.shape[1]

    def body(du1_ref, u_ref, dsf_ref, dsb_ref, mf_ref, mb_ref, du_ref, dmf_ref, dmb_ref):
        for j in range(GBK):
            dsf, dsb = dsf_ref[:, j, :], dsb_ref[:, j, :]
            du_ref[j] = (du1_ref[j] + dnt(dsf, mf_ref[j]) + dnt(dsb, mb_ref[j])).astype(bf16)
            dmf_ref[j] = dtn(u_ref[j], dsf)
            dmb_ref[j] = dtn(u_ref[j], dsb)

    sspec = pl.BlockSpec((nc, GBK, SB), lambda i: (0, i, 0))
    cshape = jax.ShapeDtypeStruct((S5G, TCP, SB), f32)
    return pl.pallas_call(
        body, name=name, grid=(S5G // GBK,),
        in_specs=[_gspec(nc, TCP), _gspec(nc, TCP), sspec, sspec, _gspec(TCP, SB), _gspec(TCP, SB)],
        out_specs=[_gspec(nc, TCP), _gspec(TCP, SB), _gspec(TCP, SB)],
        out_shape=[jax.ShapeDtypeStruct((S5G, nc, TCP), bf16), cshape, cshape],
        compiler_params=_CP(dimension_semantics=("parallel",)),
    )(du1, u, ds_f, ds_b, mb_f, mb_b)


def _to_groups(a):
    n = a.shape[0]
    return a.reshape(n // TC, TC, S5G, S5P).transpose(2, 0, 1, 3).reshape(S5G, n // TC, TCP)


def _from_groups(a):
    nc = a.shape[1]
    return a.reshape(S5G, nc, TC, S5P).transpose(1, 2, 0, 3).reshape(nc * TC, S5W)


def _swap_pairs(t):
    lane = lax.broadcasted_iota(jnp.int32, t.shape, 1)
    return jnp.where(lane % 2 == 0, pltpu.roll(t, DH - 1, axis=1), pltpu.roll(t, 1, axis=1))


def _rot(t, cosf, sins):
    return t * cosf + _swap_pairs(t) * sins


def _rot_t(d, cosf, sins):
    return d * cosf - _swap_pairs(d) * sins


def _ret_chunk(qr, kr, v, rp, ld, rev):
    pos = lax.broadcasted_iota(jnp.int32, (T, 1), 0).astype(f32)
    diff = pos - lax.broadcasted_iota(jnp.int32, (1, T), 1).astype(f32)
    if rev:
        keep, dist = diff < 0, jnp.maximum(-diff, 0.0)
        xi, zeta = jnp.exp(ld * (T - pos)), jnp.exp(ld * pos)
    else:
        keep, dist = diff >= 0, jnp.maximum(diff, 0.0)
        xi, zeta = jnp.exp(ld * (pos + 1.0)), jnp.exp(ld * (T - 1.0 - pos))
    dm = jnp.where(keep, jnp.exp(ld * dist), 0.0)
    out = dnn(dnt(qr, kr) * dm, v) + dnn(qr * xi, rp)
    rn = jnp.exp(ld * float(T)) * rp + dtn(kr * zeta, v)
    return out, rn


def _ret_fwd(p_ext, ld8_f, ld8_b, nctx, name, cargo=None):
    n = p_ext.shape[0]
    nch = n // T
    idx_b = _idx_rev(nctx, nch)
    cg = _Cargo(cargo)

    def body(*refs):
        ins, (of_ref, ob_ref, rpf_ref, rpb_ref), (rf_s, rb_s) = cg.split(refs, 8, 4, 2)
        qf, kf, vf, qb, kb, vb, ldf_ref, ldb_ref = ins
        cg.ride(refs, 8, 4, nch)

        @pl.when(pl.program_id(0) == 0)
        def _():
            rf_s[...] = jnp.zeros_like(rf_s)
            rb_s[...] = jnp.zeros_like(rb_s)

        for h in range(RH):
            sl = slice(h * DH, (h + 1) * DH)
            for q_ref, k_ref, v_ref, ld_ref, o_ref, rp_ref, r_s, rev in ((qf, kf, vf, ldf_ref, of_ref, rpf_ref, rf_s, False),
                                                                         (qb, kb, vb, ldb_ref, ob_ref, rpb_ref, rb_s, True)):
                rp = r_s[h]
                rp_ref[0, h] = rp
                out, rn = _ret_chunk(q_ref[:, sl].astype(f32), k_ref[:, sl].astype(f32), v_ref[:, sl].astype(f32), rp,
                                     ld_ref[h:h + 1, 0:1], rev)
                r_s[h] = rn
                o_ref[:, sl] = out

    fcol = lambda cb: pl.BlockSpec((T, RW), lambda i, _c=cb: (i, _c))
    bcol = lambda cb: pl.BlockSpec((T, RW), lambda i, _c=cb: (idx_b(i), _c))
    rspec = pl.BlockSpec((1, RH, DH, DH), lambda i: (i, 0, 0, 0))
    oshape, rshape = jax.ShapeDtypeStruct((n, RW), f32), jax.ShapeDtypeStruct((nch, RH, DH, DH), f32)
    return pl.pallas_call(
        body, name=name, grid=(nch,),
        in_specs=[fcol(1), fcol(2), fcol(3), bcol(1), bcol(2), bcol(3), _const_spec((8, 128)), _const_spec((8, 128))] + cg.in_specs(),
        out_specs=[fcol(0), bcol(0), rspec, rspec] + cg.in_specs(),
        out_shape=[oshape, oshape, rshape, rshape] + cg.out_shapes(),
        scratch_shapes=[pltpu.VMEM((RH, DH, DH), f32)] * 2 + cg.sems(),
        compiler_params=_CP(dimension_semantics=_ARB),
    )(p_ext, p_ext, p_ext, p_ext, p_ext, p_ext, ld8_f, ld8_b, *cg.arrays)


def _ret_bwd(p_ext, ld8_f, ld8_b, rp_f, rp_b, do_ext, nctx, name, cargo=None):
    n = p_ext.shape[0]
    nch = n // T
    idx_rev = _idx_rev(nctx, nch)
    idf = lambda j: nch - 1 - j
    idb = lambda j: idx_rev(nch - 1 - j)
    cg = _Cargo(cargo)

    def body(*refs):
        ins, outs, (drf_s, drb_s) = cg.split(refs, 12, 8, 2)
        qf, kf, vf, qb, kb, vb, ldf_ref, ldb_ref, rpf_ref, rpb_ref, dof_ref, dob_ref = ins
        dqf, dkf, dvf, dqb, dkb, dvb, dldf_ref, dldb_ref = outs
        cg.ride(refs, 12, 8, nch)

        @pl.when(pl.program_id(0) == 0)
        def _():
            for r in (drf_s, drb_s, dldf_ref, dldb_ref):
                r[...] = jnp.zeros_like(r)

        for h in range(RH):
            sl = slice(h * DH, (h + 1) * DH)
            for q_ref, k_ref, v_ref, ld_ref, rp_ref, do_ref, dq_ref, dk_ref, dv_ref, dld_ref, dr_s, rev in (
                    (qf, kf, vf, ldf_ref, rpf_ref, dof_ref, dqf, dkf, dvf, dldf_ref, drf_s, False),
                    (qb, kb, vb, ldb_ref, rpb_ref, dob_ref, dqb, dkb, dvb, dldb_ref, drb_s, True)):
                _, vjp = jax.vjp(functools.partial(_ret_chunk, rev=rev), q_ref[:, sl].astype(f32), k_ref[:, sl].astype(f32),
                                 v_ref[:, sl].astype(f32), rp_ref[0, h], ld_ref[h:h + 1, 0:1])
                dqr, dkr, dv, drp, dld = vjp((do_ref[:, sl], dr_s[h]))
                dr_s[h] = drp
                dq_ref[:, sl], dk_ref[:, sl], dv_ref[:, sl] = dqr, dkr, dv
                dld_ref[h:h + 1, :] += jnp.broadcast_to(dld, (1, 128))

    fcol = lambda cb: pl.BlockSpec((T, RW), lambda j, _c=cb: (idf(j), _c))
    bcol = lambda cb: pl.BlockSpec((T, RW), lambda j, _c=cb: (idb(j), _c))
    rspec = pl.BlockSpec((1, RH, DH, DH), lambda j: (nch - 1 - j, 0, 0, 0))
    oshape = jax.ShapeDtypeStruct((n, RW), f32)
    return pl.pallas_call(
        body, name=name, grid=(nch,),
        in_specs=[fcol(1), fcol(2), fcol(3), bcol(1), bcol(2), bcol(3), _const_spec((8, 128)), _const_spec((8, 128)), rspec, rspec,
                  fcol(0), bcol(0)] + cg.in_specs(),
        out_specs=[fcol(0), fcol(0), fcol(0), bcol(0), bcol(0), bcol(0), _acc_spec((8, 128)), _acc_spec((8, 128))] + cg.in_specs(),
        out_shape=[oshape] * 6 + [jax.ShapeDtypeStruct((8, 128), f32)] * 2 + cg.out_shapes(),
        scratch_shapes=[pltpu.VMEM((RH, DH, DH), f32)] * 2 + cg.sems(),
        compiler_params=_CP(dimension_semantics=_ARB),
    )(p_ext, p_ext, p_ext, p_ext, p_ext, p_ext, ld8_f, ld8_b, rp_f, rp_b, do_ext, do_ext, *cg.arrays)


def _qk_heads(p, fn_q, fn_k):
    heads = lambda base, fn: [fn(p[:, base + h * DH:base + (h + 1) * DH]) for h in range(RH)]
    return jnp.concatenate([p[:, :S5W]] + heads(S5W, fn_q) + heads(S5W + RW, fn_k) + [p[:, S5W + 2 * RW:]], axis=1)


def _f1_fwd(x, ctx, modx, modc, nw1, w_in_n, cosf, sins, name, cargo=None):
    L = x.shape[0]
    nb = L // R + 1
    scale = DH ** -0.5
    cg = _Cargo(cargo)

    def body(*refs):
        (x_ref, c_ref, mx_ref, mc_ref, nw_ref, w_ref, cos_ref, sin_ref), (p_ref,), _ = cg.split(refs, 8, 1, 0)
        cg.ride(refs, 8, 1, nb)
        is_ctx = pl.program_id(0) == 0
        xin = jnp.where(is_ctx, c_ref[...], x_ref[...])
        sh = jnp.where(is_ctx, mc_ref[0:1], mx_ref[0:1])
        sc = jnp.where(is_ctx, mc_ref[1:2], mx_ref[1:2])
        cf, ss = cos_ref[...], sin_ref[...]
        p = dnn(_mod(_rms(xin, nw_ref[...]), sh, sc), w_ref[...])
        p_ref[...] = _qk_heads(p, lambda t: _rot(t, cf, ss), lambda t: _rot(t * scale, cf, ss)).astype(bf16)

    return pl.pallas_call(
        body, name=name, grid=(nb,),
        in_specs=[pl.BlockSpec((R, D), lambda i: (jnp.maximum(i - 1, 0), 0)), _const_spec((R, D)), _const_spec((6, D)),
                  _const_spec((6, D)), _const_spec((1, D)), _const_spec((D, INC)), pl.BlockSpec((R, DH), lambda i: (i, 0)),
                  pl.BlockSpec((R, DH), lambda i: (i, 0))] + cg.in_specs(),
        out_specs=[pl.BlockSpec((R, INC), lambda i: (i, 0))] + cg.in_specs(),
        out_shape=[jax.ShapeDtypeStruct((L + R, INC), bf16)] + cg.out_shapes(),
        scratch_shapes=cg.sems(),
        compiler_params=_CP(dimension_semantics=_ARB),
    )(x, ctx, modx, modc, nw1, w_in_n, cosf, sins, *cg.arrays)


def _f1_bwd(x, ctx, modx, modc, nw1, w_in_t, cosf, sins, dx1, parts, name, cargo=None):
    L = x.shape[0]
    nb = L // R + 1
    scale = DH ** -0.5
    cg = _Cargo(cargo)

    def body(*refs):
        ins, (gx_ref, dp_ref, h1_ref, dnw_ref, dmx_ref, dmc_ref), _ = cg.split(refs, 18, 6, 0)
        x_ref, c_ref, mx_ref, mc_ref, nw_ref, w_ref, cos_ref, sin_ref, dx1_ref, du0, du1, dq0, dq1, dk0, dk1, dv0, dv1, dg0 = ins
        cg.ride(refs, 18, 6, nb)
        i = pl.program_id(0)
        is_ctx = i == 0

        @pl.when(is_ctx)
        def _():
            dnw_ref[...] = jnp.zeros_like(dnw_ref)
            dmx_ref[...] = jnp.zeros_like(dmx_ref)
            dmc_ref[...] = jnp.zeros_like(dmc_ref)

        cf, ss = cos_ref[...], sin_ref[...]
        dp = jnp.concatenate([du0[...].astype(f32) + du1[...], dq0[...] + dq1[...], dk0[...] + dk1[...], dv0[...] + dv1[...],
                              dg0[...]], axis=1)
        dp = _qk_heads(dp, lambda t: _rot_t(t, cf, ss), lambda t: _rot_t(t, cf, ss) * scale).astype(bf16)
        dp_ref[...] = dp
        xin = jnp.where(is_ctx, c_ref[...], x_ref[...])
        sh = jnp.where(is_ctx, mc_ref[0:1], mx_ref[0:1])
        sc = jnp.where(is_ctx, mc_ref[1:2], mx_ref[1:2])
        dh = dnn(dp, w_ref[...])
        h, vjp = jax.vjp(lambda a, b, c, d: _mod(_rms(a, b), c, d), xin, nw_ref[...], sh, sc)
        dxin, dnw, dsh, dsc = vjp(dh)
        h1_ref[...] = h.astype(bf16)
        gx_ref[...] = dx1_ref[...] + dxin
        dnw_ref[...] += dnw
        wx = jnp.where(is_ctx, 0.0, 1.0)
        dmx_ref[0:1] += dsh * wx
        dmx_ref[1:2] += dsc * wx
        dmc_ref[0:1] += dsh * (1.0 - wx)
        dmc_ref[1:2] += dsc * (1.0 - wx)

    lat = pl.BlockSpec((R, D), lambda i: (jnp.maximum(i - 1, 0), 0))
    ext = pl.BlockSpec((R, S5W), lambda i: (i, 0))
    return pl.pallas_call(
        body, name=name, grid=(nb,),
        in_specs=[lat, _const_spec((R, D)), _const_spec((6, D)), _const_spec((6, D)), _const_spec((1, D)), _const_spec((INC, D)),
                  pl.BlockSpec((R, DH), lambda i: (i, 0)), pl.BlockSpec((R, DH), lambda i: (i, 0)), lat] + [ext] * 9 + cg.in_specs(),
        out_specs=[lat, pl.BlockSpec((R, INC), lambda i: (i, 0)), pl.BlockSpec((R, D), lambda i: (i, 0)),
                   _acc_spec((1, D)), _acc_spec((6, D)), _acc_spec((6, D))] + cg.in_specs(),
        out_shape=[jax.ShapeDtypeStruct((L, D), f32), jax.ShapeDtypeStruct((L + R, INC), bf16),
                   jax.ShapeDtypeStruct((L + R, D), bf16), jax.ShapeDtypeStruct((1, D), f32),
                   jax.ShapeDtypeStruct((6, D), f32), jax.ShapeDtypeStruct((6, D), f32)] + cg.out_shapes(),
        scratch_shapes=cg.sems(),
        compiler_params=_CP(dimension_semantics=_ARB),
    )(x, ctx, modx, modc, nw1, w_in_t, cosf, sins, dx1, *parts, *cg.arrays)


def _ret_post(yr, g):
    outs = []
    for h in range(RH):
        yh = yr[:, h * DH:(h + 1) * DH]
        mu = jnp.mean(yh, axis=-1, keepdims=True)
        var = jnp.mean((yh - mu) ** 2, axis=-1, keepdims=True)
        outs.append((yh - mu) * lax.rsqrt(var + EPS))
    return jax.nn.silu(g) * jnp.concatenate(outs, axis=1)


def _mix_fn(ys, u, of, ob, g, x, dvec, bglu, gate1, pz, pm, wglu, wout):
    s = _gelu(ys + dvec * u)
    z = dnn(s, wglu) + bglu + pz
    cat = jnp.concatenate([s * jax.nn.sigmoid(z), _ret_post(of + ob, g)], axis=1)
    mix = dnn(cat, wout) + pm
    return x + gate1 * mix, (s, cat)


def _mix_fwd(x, ys, of, ob, p_ext, dvec, bglu, modx, wglu, wout, name, cargo=None):
    L = x.shape[0]
    nb = L // R
    cg = _Cargo(cargo)

    def body(*refs):
        ins, (x1_ref,), _ = cg.split(refs, 11, 1, 0)
        x_ref, ys_ref, of_ref, ob_ref, u_ref, g_ref, d_ref, b_ref, mx_ref, wg_ref, wo_ref = ins
        cg.ride(refs, 11, 1, nb)
        x1_ref[...] = _mix_fn(ys_ref[...].astype(f32), u_ref[...].astype(f32), of_ref[...], ob_ref[...], g_ref[...].astype(f32),
                              x_ref[...], d_ref[...], b_ref[...], mx_ref[2:3], 0.0, 0.0, wg_ref[...], wo_ref[...])[0]

    ext = pl.BlockSpec((R, S5W), lambda i: (i + 1, 0))
    return pl.pallas_call(
        body, name=name, grid=(nb,),
        in_specs=[pl.BlockSpec((R, D), lambda i: (i, 0)), ext, ext, ext, ext, pl.BlockSpec((R, RW), lambda i: (i + 1, 4)),
                  _const_spec((1, S5W)), _const_spec((1, S5W)), _const_spec((6, D)), _const_spec((S5W, S5W)), _const_spec((D, D))]
        + cg.in_specs(),
        out_specs=[pl.BlockSpec((R, D), lambda i: (i, 0))] + cg.in_specs(),
        out_shape=[jax.ShapeDtypeStruct((L, D), f32)] + cg.out_shapes(),
        scratch_shapes=cg.sems(),
        compiler_params=_CP(dimension_semantics=_ARB),
    )(x, ys, of, ob, p_ext, p_ext, dvec, bglu, modx, wglu, wout, *cg.arrays)


def _mix_bwd(x, ys, of, ob, p_ext, dvec, bglu, modx, wglu, wout, dx1, name, cargo=None):
    L = x.shape[0]
    nb = L // R + 1
    cg = _Cargo(cargo)

    def body(*refs):
        ins, outs, _ = cg.split(refs, 12, 11, 0)
        x_ref, ys_ref, of_ref, ob_ref, u_ref, g_ref, d_ref, b_ref, mx_ref, wg_ref, wo_ref, dx1_ref = ins
        dy_ref, dud_ref, do_ref, dg_ref, cat_ref, dmix_ref, s_ref, dz_ref, dd_ref, db_ref, dg1_ref = outs
        cg.ride(refs, 12, 11, nb)
        i = pl.program_id(0)

        @pl.when(i == 0)
        def _():
            for r in outs:
                r[...] = jnp.zeros_like(r)

        @pl.when(i > 0)
        def _():
            fn = lambda ys_, u_, of_, g_, d_, b_, g1_, pz_, pm_: _mix_fn(
                ys_, u_, of_, ob_ref[...], g_, x_ref[...], d_, b_, g1_, pz_, pm_, wg_ref[...], wo_ref[...])
            _, vjp, (s, cat) = jax.vjp(fn, ys_ref[...].astype(f32), u_ref[...].astype(f32), of_ref[...], g_ref[...].astype(f32), d_ref[...],
                                       b_ref[...], mx_ref[2:3], jnp.zeros((R, S5W), f32), jnp.zeros((R, D), f32), has_aux=True)
            dy, dud, do, dg, dd, db, dg1, dz, dmix = vjp(dx1_ref[...])
            dy_ref[...], dud_ref[...], do_ref[...], dg_ref[...] = dy.astype(bf16), dud, do, dg
            cat_ref[...], dmix_ref[...] = cat.astype(bf16), dmix.astype(bf16)
            s_ref[...], dz_ref[...] = s.astype(bf16), dz.astype(bf16)
            dd_ref[...] += dd
            db_ref[...] += db
            dg1_ref[...] += dg1

    lat = pl.BlockSpec((R, D), lambda i: (jnp.maximum(i - 1, 0), 0))
    lat5 = pl.BlockSpec((R, S5W), lambda i: (jnp.maximum(i - 1, 0), 0))
    ext = pl.BlockSpec((R, S5W), lambda i: (i, 0))
    eshape = jax.ShapeDtypeStruct((L + R, S5W), f32)
    return pl.pallas_call(
        body, name=name, grid=(nb,),
        in_specs=[lat, ext, ext, ext, ext, pl.BlockSpec((R, RW), lambda i: (i, 4)),
                  _const_spec((1, S5W)), _const_spec((1, S5W)), _const_spec((6, D)), _const_spec((S5W, S5W)), _const_spec((D, D)), lat]
        + cg.in_specs(),
        out_specs=[ext, ext, ext, ext, lat, lat, lat5, lat5, _acc_spec((1, S5W)), _acc_spec((1, S5W)), _acc_spec((1, D))]
        + cg.in_specs(),
        out_shape=[jax.ShapeDtypeStruct((L + R, S5W), bf16), eshape, eshape, eshape, jax.ShapeDtypeStruct((L, D), bf16),
                   jax.ShapeDtypeStruct((L, D), bf16), jax.ShapeDtypeStruct((L, S5W), bf16), jax.ShapeDtypeStruct((L, S5W), bf16),
                   jax.ShapeDtypeStruct((1, S5W), f32), jax.ShapeDtypeStruct((1, S5W), f32), jax.ShapeDtypeStruct((1, D), f32)]
        + cg.out_shapes(),
        scratch_shapes=cg.sems(),
        compiler_params=_CP(dimension_semantics=_ARB),
    )(x, ys, of, ob, p_ext, p_ext, dvec, bglu, modx, wglu, wout, dx1, *cg.arrays)


def _ffn_tail(gc, a, x1, gate2, fnw, pf, wdown, wdown_t, tgt):
    f = _gelu(gc) * a
    ffn = _dnn_const(f, wdown, wdown_t) + pf
    y = _rms(x1 + gate2 * ffn, fnw)
    err = y - tgt
    loss = 0.5 * jnp.sum(jnp.mean(err * err, axis=-1, keepdims=True), axis=0, keepdims=True)
    return loss, f


def _ffn_fwd(x1, tgt, nw2, modx, w_a, w_g, cw, cb, wdown, wdown_t, fnw, name):
    L = x1.shape[0]
    nb = L // RF
    per = RF // HALO

    def body(x_ref, xp_ref, xn_ref, t_ref, nw_ref, mx_ref, wa_ref, wg_ref, cw_ref, cb_ref, wd_ref, wdt_ref, fn_ref,
             dx2_ref, da_ref, dgc_ref, f_ref, dffn_ref, loss_ref, dfn_ref, dg2_ref, dcb_ref, dcw_ref):
        i = pl.program_id(0)

        @pl.when(i == 0)
        def _():
            for r in (loss_ref, dfn_ref, dg2_ref, dcb_ref, dcw_ref):
                r[...] = jnp.zeros_like(r)

        nw, sh, sc, gate2 = nw_ref[...], mx_ref[3:4], mx_ref[4:5], mx_ref[5:6]
        x1b = x_ref[...]
        h2 = _mod(_rms(x1b, nw), sh, sc)
        h2e = jnp.concatenate([_mod(_rms(xp_ref[...], nw), sh, sc), h2, _mod(_rms(xn_ref[...], nw), sh, sc)], axis=0)
        a = dnn(h2, wa_ref[...])
        ge = dnn(h2e, wg_ref[...])
        g = ge[HALO:HALO + RF]
        gp = ge[HALO - 1:HALO] * jnp.where(i > 0, 1.0, 0.0)
        gn = ge[HALO + RF:HALO + RF + 1] * jnp.where(i < nb - 1, 1.0, 0.0)
        row = lax.broadcasted_iota(jnp.int32, (RF, 1), 0)
        g_prev = jnp.where(row == 0, gp, pltpu.roll(g, 1, axis=0))
        g_next = jnp.where(row == RF - 1, gn, pltpu.roll(g, RF - 1, axis=0))
        gc = cb_ref[...] + g_prev * cw_ref[0:1] + g * cw_ref[1:2] + g_next * cw_ref[2:3]
        fn = lambda gc_, a_, x_, g2_, fw_, pf_: _ffn_tail(gc_, a_, x_, g2_, fw_, pf_, wd_ref[...], wdt_ref[...], t_ref[...])
        loss, vjp, f = jax.vjp(fn, gc, a, x1b, gate2, fn_ref[...], jnp.zeros((RF, D), f32), has_aux=True)
        dgc, da, dx2, dg2, dfw, dffn = vjp(jnp.ones((1, 1), f32))
        dx2_ref[...] = dx2
        da_ref[...], dgc_ref[...] = da.astype(bf16), dgc
        f_ref[...], dffn_ref[...] = f.astype(bf16), dffn.astype(bf16)
        loss_ref[...] += jnp.broadcast_to(loss, (1, 128))
        dfn_ref[...] += dfw
        dg2_ref[...] += dg2
        dcb_ref[...] += jnp.sum(dgc, axis=0, keepdims=True)
        dcw_ref[0:1] += jnp.sum(dgc * g_prev, axis=0, keepdims=True)
        dcw_ref[1:2] += jnp.sum(dgc * g, axis=0, keepdims=True)
        dcw_ref[2:3] += jnp.sum(dgc * g_next, axis=0, keepdims=True)

    blk = lambda w: pl.BlockSpec((RF, w), lambda i: (i, 0))
    return pl.pallas_call(
        body, name=name, grid=(nb,),
        in_specs=[blk(D), pl.BlockSpec((HALO, D), lambda i: (jnp.maximum(i * per - 1, 0), 0)),
                  pl.BlockSpec((HALO, D), lambda i: (jnp.minimum((i + 1) * per, L // HALO - 1), 0)), blk(D),
                  _const_spec((1, D)), _const_spec((6, D)), _const_spec((D, DFF)), _const_spec((D, DFF)), _const_spec((3, DFF)),
                  _const_spec((1, DFF)), _const_spec((DFF, D)), _const_spec((D, DFF)), _const_spec((1, D))],
        out_specs=[blk(D), blk(DFF), blk(DFF), blk(DFF), blk(D), _acc_spec((1, 128)), _acc_spec((1, D)), _acc_spec((1, D)),
                   _acc_spec((1, DFF)), _acc_spec((3, DFF))],
        out_shape=[jax.ShapeDtypeStruct((L, D), f32), jax.ShapeDtypeStruct((L, DFF), bf16), jax.ShapeDtypeStruct((L, DFF), f32),
                   jax.ShapeDtypeStruct((L, DFF), bf16), jax.ShapeDtypeStruct((L, D), bf16), jax.ShapeDtypeStruct((1, 128), f32),
                   jax.ShapeDtypeStruct((1, D), f32), jax.ShapeDtypeStruct((1, D), f32), jax.ShapeDtypeStruct((1, DFF), f32),
                   jax.ShapeDtypeStruct((3, DFF), f32)],
        compiler_params=_CP(dimension_semantics=_ARB),
    )(x1, x1, x1, tgt, nw2, modx, w_a, w_g, cw, cb, wdown, wdown_t, fnw)


def _ffn_bwd(x1, dx2, da, dgc, nw2, modx, wup_t, cw, name):
    L = x1.shape[0]
    nb = L // RF
    per = RF // HALO

    def body(x_ref, dx2_ref, da_ref, dgc_ref, dgp_ref, dgn_ref, nw_ref, mx_ref, wu_ref, cw_ref,
             dx1_ref, dag_ref, h2_ref, dnw_ref, dmx_ref):
        i = pl.program_id(0)

        @pl.when(i == 0)
        def _():
            dnw_ref[...] = jnp.zeros_like(dnw_ref)
            dmx_ref[...] = jnp.zeros_like(dmx_ref)

        dgc_b = dgc_ref[...]
        before = dgp_ref[HALO - 1:HALO] * jnp.where(i > 0, 1.0, 0.0)
        after = dgn_ref[0:1] * jnp.where(i < nb - 1, 1.0, 0.0)
        row = lax.broadcasted_iota(jnp.int32, (RF, 1), 0)
        d_prev = jnp.where(row == 0, before, pltpu.roll(dgc_b, 1, axis=0))
        d_next = jnp.where(row == RF - 1, after, pltpu.roll(dgc_b, RF - 1, axis=0))
        dg = cw_ref[0:1] * d_next + cw_ref[1:2] * dgc_b + cw_ref[2:3] * d_prev
        dag = jnp.concatenate([da_ref[...], dg.astype(bf16)], axis=1)
        dag_ref[...] = dag
        dh2 = dnn(dag, wu_ref[...])
        h2, vjp = jax.vjp(lambda a, b, c, d: _mod(_rms(a, b), c, d), x_ref[...], nw_ref[...], mx_ref[3:4], mx_ref[4:5])
        dxa, dnw, dsh, dsc = vjp(dh2)
        h2_ref[...] = h2.astype(bf16)
        dx1_ref[...] = dx2_ref[...] + dxa
        dnw_ref[...] += dnw
        dmx_ref[3:4] += dsh
        dmx_ref[4:5] += dsc

    blk = lambda w: pl.BlockSpec((RF, w), lambda i: (i, 0))
    return pl.pallas_call(
        body, name=name, grid=(nb,),
        in_specs=[blk(D), blk(D), blk(DFF), blk(DFF), pl.BlockSpec((HALO, DFF), lambda i: (jnp.maximum(i * per - 1, 0), 0)),
                  pl.BlockSpec((HALO, DFF), lambda i: (jnp.minimum((i + 1) * per, L // HALO - 1), 0)),
                  _const_spec((1, D)), _const_spec((6, D)), _const_spec((2 * DFF, D)), _const_spec((3, DFF))],
        out_specs=[blk(D), blk(2 * DFF), blk(D), _acc_spec((1, D)), _acc_spec((6, D))],
        out_shape=[jax.ShapeDtypeStruct((L, D), f32), jax.ShapeDtypeStruct((L, 2 * DFF), bf16), jax.ShapeDtypeStruct((L, D), bf16),
                   jax.ShapeDtypeStruct((1, D), f32), jax.ShapeDtypeStruct((6, D), f32)],
        compiler_params=_CP(dimension_semantics=_ARB),
    )(x1, dx2, da, dgc, dgc, dgc, nw2, modx, wup_t, cw)


def _matmul_tn(a, b, name, cargo=None):
    k, m = a.shape
    n = b.shape[1]
    divs = lambda d: [c for c in range(d, 0, -128) if d % c == 0]
    _, tm, tn = min((m * (n // cn) + n * (m // cm), cm, cn) for cm in divs(m) for cn in divs(n) if cm * cn * 4 <= ACC_TILE_BYTES)
    tk = next(c for c in (512, 768, 256, 128) if k % c == 0)
    nk = k // tk
    grid = (m // tm, n // tn, nk)
    cg = _Cargo(cargo)

    def body(*refs):
        (a_ref, b_ref), (o_ref,), (acc,) = cg.split(refs, 2, 1, 1)
        cg.ride(refs, 2, 1, grid)
        q = pl.program_id(2)

        @pl.when(q == 0)
        def _():
            acc[...] = jnp.zeros_like(acc)

        acc[...] += dtn(a_ref[...], b_ref[...])

        @pl.when(q == nk - 1)
        def _():
            o_ref[...] = acc[...].astype(bf16)

    out = pl.pallas_call(
        body, name=name, grid=grid,
        in_specs=[pl.BlockSpec((tk, tm), lambda i, j, q: (q, i)), pl.BlockSpec((tk, tn), lambda i, j, q: (q, j))] + cg.in_specs(),
        out_specs=[pl.BlockSpec((tm, tn), lambda i, j, q: (i, j))] + cg.in_specs(),
        out_shape=[jax.ShapeDtypeStruct((m, n), bf16)] + cg.out_shapes(),
        scratch_shapes=[pltpu.VMEM((tm, tn), f32)] + cg.sems(),
        compiler_params=_CP(dimension_semantics=("arbitrary",) * 3 if cg.n else ("parallel", "parallel", "arbitrary")),
    )(a, b, *cg.arrays)
    return out if cg.n else out[0]


def _adamw_refs(w_ref, g_ref, m_ref, v_ref, d_ref, nm_ref, nv_ref):
    c1, c2 = 1.0 - B1 ** STEP, 1.0 - B2 ** STEP
    gg = g_ref[...]
    nm = B1 * m_ref[...] + (1.0 - B1) * gg
    nv = B2 * v_ref[...] + (1.0 - B2) * jnp.square(gg)
    d_ref[...] = -LR * ((nm / c1) / (jnp.sqrt(nv / c2) + AEPS) + WD * w_ref[...])
    nm_ref[...], nv_ref[...] = nm, nv


def _adamw(w, g, m, v, name):
    def body(*refs):
        _adamw_refs(*refs)

    return pl.pallas_call(body, name=name, out_shape=[jax.ShapeDtypeStruct(w.shape, f32)] * 3, compiler_params=_CP())(w, g, m, v)


def _adamw_landed(land, w, m, v, name):
    def body(l_ref, w_ref, m_ref, v_ref, g_ref, d_ref, nm_ref, nv_ref):
        acc = l_ref[0].astype(f32)
        for j in range(1, NDEV):
            acc = acc + l_ref[j].astype(f32)
        g_ref[...] = acc
        _adamw_refs(w_ref, g_ref, m_ref, v_ref, d_ref, nm_ref, nv_ref)

    return pl.pallas_call(body, name=name, out_shape=[jax.ShapeDtypeStruct(w.shape, f32)] * 4, compiler_params=_CP())(land, w, m, v)


def _adamw_many(ws, gs, ms, vs, name):
    n = len(ws)

    def body(*refs):
        for k in range(n):
            _adamw_refs(*[refs[j * n + k] for j in range(7)])

    outs = pl.pallas_call(body, name=name, out_shape=[jax.ShapeDtypeStruct(w.shape, f32) for w in ws] * 3,
                          compiler_params=_CP())(*ws, *gs, *ms, *vs)
    return outs[:n], outs[n:2 * n], outs[2 * n:]


SMALL = ["conv_w", "c_ctx", "norm1_w", "s5_lambda_re_f", "s5_lambda_im_f", "s5_log_step_f", "s5_lambda_re_b", "s5_lambda_im_b",
         "s5_log_step_b", "s5_b_re", "s5_b_im", "s5_c_re", "s5_c_im", "s5_d", "s5_b_glu", "ret_log_decay_f", "ret_log_decay_b",
         "norm2_w", "conv_b", "final_norm_w"]
WEIGHTS = ["c_ctx", "w_mod", "b_mod", "norm1_w", "w_in", "s5_lambda_re_f", "s5_lambda_im_f", "s5_log_step_f", "s5_lambda_re_b",
           "s5_lambda_im_b", "s5_log_step_b", "s5_b_re", "s5_b_im", "s5_c_re", "s5_c_im", "s5_d", "s5_w_glu", "s5_b_glu",
           "ret_log_decay_f", "ret_log_decay_b", "w_out", "norm2_w", "w_up", "conv_w", "conv_b", "w_down", "final_norm_w"]


def _pack_small(vals):
    flat, offs, o = [], [], 0
    for a in vals:
        n = a.size
        npad = -n % 128
        flat.append(jnp.pad(a.reshape(-1), (0, npad)))
        offs.append((o, n))
        o += n + npad
    tail = -o % 1024
    if tail:
        flat.append(jnp.zeros((tail,), f32))
    return jnp.concatenate(flat).reshape(-1, 128), offs


def _unpack_small(packed, offs, shapes):
    flat = packed.reshape(-1)
    return [flat[o:o + n].reshape(s) for (o, n), s in zip(offs, shapes)]


def _rope_tables(L, nctx_rows):
    t = np.arange(L)
    inv = (ROPE_THETA ** (-np.arange(DH // 4, dtype=np.float64) / (DH // 4))).astype(np.float32)
    ang = np.concatenate([(t // GRID_W).astype(np.float32)[:, None] * inv, (t % GRID_W).astype(np.float32)[:, None] * inv], axis=-1)
    cos = np.repeat(np.cos(ang).astype(np.float32), 2, axis=1)
    sin = np.repeat(np.sin(ang).astype(np.float32), 2, axis=1) * np.tile(np.array([-1.0, 1.0], np.float32), DH // 2)
    cosf = np.concatenate([np.ones((nctx_rows, DH), np.float32), cos], axis=0)
    sins = np.concatenate([np.zeros((nctx_rows, DH), np.float32), sin], axis=0)
    return jnp.asarray(cosf), jnp.asarray(sins)


def kernel(x, c, ctx, c_ctx, w_mod, b_mod, norm1_w, w_in, s5_lambda_re_f, s5_lambda_im_f, s5_log_step_f, s5_lambda_re_b, s5_lambda_im_b, s5_log_step_b, s5_b_re, s5_b_im, s5_c_re, s5_c_im, s5_d, s5_w_glu, s5_b_glu, ret_log_decay_f, ret_log_decay_b, w_out, norm2_w, w_up, conv_w, conv_b, w_down, final_norm_w, loss_target, m_c_ctx, m_w_mod, m_b_mod, m_norm1_w, m_w_in, m_s5_lambda_re_f, m_s5_lambda_im_f, m_s5_log_step_f, m_s5_lambda_re_b, m_s5_lambda_im_b, m_s5_log_step_b, m_s5_b_re, m_s5_b_im, m_s5_c_re, m_s5_c_im, m_s5_d, m_s5_w_glu, m_s5_b_glu, m_ret_log_decay_f, m_ret_log_decay_b, m_w_out, m_norm2_w, m_w_up, m_conv_w, m_conv_b, m_w_down, m_final_norm_w, v_c_ctx, v_w_mod, v_b_mod, v_norm1_w, v_w_in, v_s5_lambda_re_f, v_s5_lambda_im_f, v_s5_log_step_f, v_s5_lambda_re_b, v_s5_lambda_im_b, v_s5_log_step_b, v_s5_b_re, v_s5_b_im, v_s5_c_re, v_s5_c_im, v_s5_d, v_s5_w_glu, v_s5_b_glu, v_ret_log_decay_f, v_ret_log_decay_b, v_w_out, v_norm2_w, v_w_up, v_conv_w, v_conv_b, v_w_down, v_final_norm_w):
    args = dict(locals())
    W = {n: args[n] for n in WEIGHTS}
    M = {n: args["m_" + n] for n in WEIGHTS}
    V = {n: args["v_" + n] for n in WEIGHTS}
    me = _me()
    x2, ctx2, tgt = x[0], ctx[0], loss_target[0]
    L, Lc = x2.shape[0], ctx2.shape[0]
    assert Lc == R and L % R == 0 and L % GRID_W == 0
    nctx = Lc // T

    w_in_tl, w_up_tl = w_in[0].T.astype(bf16), w_up[0].T.astype(bf16)
    w_out_l, w_down_l, w_glu_l = w_out[0].astype(bf16), w_down[0].astype(bf16), s5_w_glu[0].astype(bf16)
    per_cv = conv_w.shape[2]
    conv_pad = jnp.pad(conv_w[0], ((0, 5), (0, 128 * 3 - per_cv)))
    w_in_g, c_g, conv_g = _gather_two_level([w_in_tl, jnp.pad(c, ((0, 7), (0, 0))), conv_pad], "gather_w_in")
    w_in_t = w_in_g.reshape(INC, D)
    conv_f = conv_g[:, :3, :per_cv].transpose(1, 0, 2).reshape(3, DFF)

    c9 = jnp.concatenate([c_g[:, 0, :], c_ctx[None], jnp.zeros((7, D), f32)], axis=0)
    w_mod_l = w_mod[0]
    ncol = w_mod_l.shape[1]
    m_part = _ada_fwd(c9, w_mod_l, "ada_fwd")
    m_all = _all_gather_small(m_part, "gather_mod").transpose(1, 0, 2).reshape(16, 6, D)
    modx, modc = _mod_select(m_all, b_mod.reshape(6, D), "mod_select")

    pair = lambda a, b: jnp.concatenate([a, b], axis=-1)
    bre_g, bim_g = s5_b_re[0].transpose(0, 2, 1), s5_b_im[0].transpose(0, 2, 1)
    cre_g, cim_g = s5_c_re[0], s5_c_im[0]
    shared = (pair(bre_g, bim_g), pair(bim_g, bre_g), pair(cre_g, cim_g), pair(cim_g, cre_g))
    s5p = {}
    for tag, lre, lim, ls in (("f", s5_lambda_re_f, s5_lambda_im_f, s5_log_step_f), ("b", s5_lambda_re_b, s5_lambda_im_b, s5_log_step_b)):
        s5p[tag] = (pair(lre[0], lre[0])[:, None, :], pair(lim[0], lim[0])[:, None, :], ls[0].reshape(S5G, 1, 1)) + shared
    m_f, mb_f, mc_f, a1_f, a2_f = _s5_build(s5p["f"], False, "s5_build_f")
    m_b, mb_b, mc_b, a1_b, a2_b = _s5_build(s5p["b"], True, "s5_build_b")
    a1_f, a2_f, a1_b, a2_b = (a.reshape(S5G, SB) for a in (a1_f, a2_f, a1_b, a2_b))

    nw1, nw2, fnw = norm1_w, norm2_w, final_norm_w[None]
    cosf, sins = _rope_tables(L, Lc)
    p_ext, w_out_g, w_glu_g = _f1_fwd(x2, ctx2, modx, modc, nw1, w_in_t.T, cosf, sins, "f1_fwd", cargo=([w_out_l, w_glu_l], False))
    nctx5 = Lc // TC
    u_g = _to_groups(p_ext[:, :S5W])
    s_f, s_b = _s5_inc(u_g, mb_f, mb_b, "s5_inc")
    hp_f, hp_b = _s5_carry(s_f, s_b, (a1_f, a2_f), (a1_b, a2_b), nctx5, "s5_carry")
    ys = _from_groups(_s5_out(u_g, m_f, m_b, hp_f, hp_b, mc_f, mc_b, "s5_out"))
    ld8 = lambda ld: jnp.pad(jnp.broadcast_to(ld[0][:, None], (RH, 128)), ((0, 8 - RH), (0, 0)))
    ldf8, ldb8 = ld8(ret_log_decay_f), ld8(ret_log_decay_b)
    of, ob, rp_f, rp_b, w_up_g = _ret_fwd(p_ext, ldf8, ldb8, nctx, "ret_fwd", cargo=([w_up_tl], False))
    w_out_f, w_glu_f = w_out_g.reshape(D, D), w_glu_g.reshape(S5W, S5W)
    x1, w_down_g = _mix_fwd(x2, ys, of, ob, p_ext, s5_d, s5_b_glu, modx, w_glu_f, w_out_f, "mix_fwd", cargo=([w_down_l], False))
    w_down_f = w_down_g.reshape(DFF, D)
    w_up_t = w_up_g.reshape(2 * DFF, D)

    (dx2, da, dgc, f_act, dffn, loss_acc, g_fnw, g_gate2, g_cb, g_cw) = _ffn_fwd(
        x1, tgt, nw2, modx, w_up_t[:DFF].T, w_up_t[DFF:].T, conv_f, conv_b, w_down_f, w_down_f.T, fnw, "ffn_fwd")
    dx1, dag, h2, g_nw2, dmx2 = _ffn_bwd(x1, dx2, da, dgc, nw2, modx, w_up_t, conv_f, "ffn_bwd")
    gw_down = _matmul_tn(f_act, dffn, "dw_down").reshape(NDEV, -1, D)
    gw_up_t = _matmul_tn(dag, h2, "dw_up").reshape(NDEV, -1, D)
    (dy_e, dud_e, do_e, dg_e, cat, dmix, s_act, dz, g_d, g_bglu, g_gate1, l_down) = _mix_bwd(
        x2, ys, of, ob, p_ext, s5_d, s5_b_glu, modx, w_glu_f, w_out_f, dx1, "mix_bwd", cargo=([gw_down], True))
    gw_out = _matmul_tn(cat, dmix, "dw_out").reshape(NDEV, -1, D)
    gw_glu = _matmul_tn(s_act, dz, "dw_glu").reshape(NDEV, -1, S5W)
    dq_f, dk_f, dv_f, dq_b, dk_b, dv_b, gld_f, gld_b, l_up, l_out, l_glu = _ret_bwd(
        p_ext, ldf8, ldb8, rp_f, rp_b, do_e, nctx, "ret_bwd", cargo=([gw_up_t, gw_out, gw_glu], True))

    du1, g_m, dhp_f, dhp_b, dmc_f, dmc_b = _s5_out_bwd(_to_groups(dy_e), u_g, m_f, m_b, hp_f, hp_b, mc_f, mc_b, "s5_out_bwd")
    ds_f, da1_f, da2_f = _s5_carry_bwd(dhp_f, hp_f, a1_f, a2_f, False, nctx5, "s5_carry_bwd_f")
    ds_b, da1_b, da2_b = _s5_carry_bwd(dhp_b, hp_b, a1_b, a2_b, True, nctx5, "s5_carry_bwd_b")
    du_g, dmb_f, dmb_b = _s5_inc_bwd(du1, u_g, ds_f, ds_b, mb_f, mb_b, "s5_inc_bwd")
    zero_p = jnp.zeros((S5G, S5P, SB), f32)
    gf = _s5_build_bwd(s5p["f"], (g_m, dmb_f, dmc_f, da1_f[:, None, :], da2_f[:, None, :]), (zero_p, zero_p), False, "s5_build_bwd_f")
    gb = _s5_build_bwd(s5p["b"], (g_m, dmb_b, dmc_b, da1_b[:, None, :], da2_b[:, None, :]), (gf[3], gf[4]), True, "s5_build_bwd_b")
    g_bre, g_bim = gb[3][:, :, :S5N].transpose(0, 2, 1), gb[3][:, :, S5N:].transpose(0, 2, 1)
    g_cre, g_cim = gb[4][:, :, :S5N], gb[4][:, :, S5N:]

    early = {
        "conv_w": g_cw, "s5_lambda_re_f": gf[0][:, 0, :S5N], "s5_lambda_im_f": gf[1][:, 0, :S5N],
        "s5_log_step_f": gf[2], "s5_lambda_re_b": gb[0][:, 0, :S5N], "s5_lambda_im_b": gb[1][:, 0, :S5N], "s5_log_step_b": gb[2],
        "s5_b_re": g_bre, "s5_b_im": g_bim, "s5_c_re": g_cre, "s5_c_im": g_cim, "s5_d": g_d, "s5_b_glu": g_bglu,
        "ret_log_decay_f": gld_f[:RH, 0], "ret_log_decay_b": gld_b[:RH, 0], "norm2_w": g_nw2, "conv_b": g_cb, "final_norm_w": g_fnw,
    }
    e_names = [n for n in SMALL if n in early]
    packed_e, eoffs = _pack_small([early[n].astype(f32) for n in e_names])
    grad_x, dp_ext, h1, g_nw1, dmx1, dmc1 = _f1_bwd(
        x2, ctx2, modx, modc, nw1, w_in_t, cosf, sins, dx1, (_from_groups(du_g), dud_e, dq_f, dq_b, dk_f, dk_b, dv_f, dv_b, dg_e), "f1_bwd")
    gw_in_t, land_e = _matmul_tn(dp_ext, h1, "dw_in", cargo=([packed_e], False))
    g_in_t = _reduce_scatter_two_level(gw_in_t.reshape(NDEV, -1, D), "scatter_dw_in")

    dmx = dmx1 + dmx2
    dmx = dmx.at[2].set(g_gate1[0]).at[5].set(g_gate2[0])
    dm_me = jnp.stack([dmx.reshape(-1), dmc1.reshape(-1)], axis=0)
    dm_all = _all_gather_small(dm_me.reshape(8, -1), "gather_dmod").reshape(NDEV, 2, 6 * D)
    dmx_all, dmc_all = dm_all[:, 0, :], dm_all[:, 1, :]
    my_cols = lambda a: lax.dynamic_slice(a, (0, me * ncol), (NDEV, ncol))
    gw_mod, g_bmod, dc9 = _ada_bwd(c9, dmx_all, dmc_all, my_cols(dmx_all), my_cols(dmc_all), w_mod_l, "ada_bwd")

    sshape = lambda n: (3, DFF) if n == "conv_w" else W[n].shape
    G = dict(zip(e_names, _unpack_small(_sum8(land_e, "reduce_early"), eoffs, [sshape(n) for n in e_names])))
    late = {"c_ctx": dc9[8], "norm1_w": g_nw1}
    packed_l, loffs = _pack_small([late[n].astype(f32) for n in late])
    G.update(zip(late, _unpack_small(_all_reduce_small(packed_l, "reduce_late"), loffs, [W[n].shape for n in late])))
    G["conv_w"] = lax.dynamic_slice(G["conv_w"], (0, me * per_cv), (3, per_cv))[None]
    G["b_mod"] = g_bmod.reshape(b_mod.shape)
    G["w_mod"] = gw_mod[None]
    G["w_in"] = g_in_t.T[None]
    G["w_up"] = _sum8(l_up, "sum_dw_up").T[None]

    delta, new_m, new_v = {}, {}, {}
    sm_names = SMALL[1:] + ["b_mod"]
    rows = lambda a: a.reshape(-1, a.shape[-1])
    outs = _adamw_many(*[[rows(d[n]) for n in sm_names] for d in (W, G, M, V)], "adamw_small")
    for dst, src in zip((delta, new_m, new_v), outs):
        dst.update({n: a.reshape(W[n].shape) for n, a in zip(sm_names, src)})
    for n in ["w_mod", "w_in", "w_up", "conv_w"]:
        d, nm, nv = _adamw(W[n][0], G[n][0], M[n][0], V[n][0], "adamw_" + n)
        delta[n], new_m[n], new_v[n] = d[None], nm[None], nv[None]
    for n, land in (("w_out", l_out), ("w_down", l_down), ("s5_w_glu", l_glu)):
        g, d, nm, nv = _adamw_landed(land, W[n][0], M[n][0], V[n][0], "adamw_" + n)
        G[n], delta[n], new_m[n], new_v[n] = g[None], d[None], nm[None], nv[None]

    loss = lax.psum(loss_acc[0, 0], ("x", "y", "c"))
    return (loss, grad_x[None], *[G[n] for n in WEIGHTS], *[delta[n] for n in WEIGHTS], *[new_m[n] for n in WEIGHTS],
            *[new_v[n] for n in WEIGHTS])
```

```python
import functools

import numpy as np
import jax
import jax.numpy as jnp
from jax import lax
from jax.experimental import pallas as pl
from jax.experimental.pallas import tpu as pltpu

f32, bf16 = jnp.float32, jnp.bfloat16

D = 1024
S5W, S5G, S5P, S5N = 512, 32, 16, 64
TC = 16
TCP = TC * S5P
SB = 2 * S5N
GBK = 8
CARRY_UNROLL = 8
RH, DH = 4, 128
RW = RH * DH
INC = S5W + 4 * RW
DFF = 2816
T = 128
R = 256
RF = 128
HALO = 8
EPS = 1e-6
ROPE_THETA = 10000.0
GRID_W = 64
NDEV = 8
LR, B1, B2, AEPS, WD, STEP = 0.001, 0.9, 0.999, 1e-08, 0.01, 10
VMEM_LIMIT = 60 * 1024 * 1024
ACC_TILE_BYTES = 6 * 1024 * 1024
MESH = pl.DeviceIdType.MESH

_CP = functools.partial(pltpu.CompilerParams, vmem_limit_bytes=VMEM_LIMIT)
_ARB = ("arbitrary",)
_ANY = pl.BlockSpec(memory_space=pl.ANY)


def _dg(a, b, dims):
    return lax.dot_general(a.astype(bf16), b.astype(bf16), (dims, ((), ())), preferred_element_type=f32)


@jax.custom_vjp
def dnn(a, b):
    return _dg(a, b, ((1,), (0,)))


@jax.custom_vjp
def dnt(a, b):
    return _dg(a, b, ((1,), (1,)))


@jax.custom_vjp
def dtn(a, b):
    return _dg(a, b, ((0,), (0,)))


dnn.defvjp(lambda a, b: (dnn(a, b), (a, b)), lambda r, g: (dnt(g, r[1]).astype(r[0].dtype), dtn(r[0], g).astype(r[1].dtype)))
dnt.defvjp(lambda a, b: (dnt(a, b), (a, b)), lambda r, g: (dnn(g, r[1]).astype(r[0].dtype), dtn(g, r[0]).astype(r[1].dtype)))
dtn.defvjp(lambda a, b: (dtn(a, b), (a, b)), lambda r, g: (dnt(r[1], g).astype(r[0].dtype), dnn(r[0], g).astype(r[1].dtype)))


@jax.custom_vjp
def _dnn_const(a, w, wt):
    return dnn(a, w)


_dnn_const.defvjp(lambda a, w, wt: (dnn(a, w), wt), lambda wt, g: (dnn(g, wt), None, None))


_GELU_C0, _GELU_C1 = float(np.sqrt(2.0 / np.pi)), 0.044715


@jax.custom_vjp
def _gelu(x):
    return _gelu_fwd(x)[0]


def _gelu_fwd(x):
    t = jnp.tanh(_GELU_C0 * (x + _GELU_C1 * (x * x * x)))
    return x * (0.5 * (1.0 + t)), (x, t)


def _gelu_bwd(res, g):
    x, t = res
    return (g * (0.5 * (1.0 + t) + (0.5 * _GELU_C0) * x * (1.0 - t * t) * (1.0 + (3.0 * _GELU_C1) * (x * x))),)


_gelu.defvjp(_gelu_fwd, _gelu_bwd)


def _rms(t, w):
    return t * lax.rsqrt(jnp.mean(t * t, axis=-1, keepdims=True) + EPS) * w


def _mod(h, shift, scale):
    return h * (1.0 + scale) + shift


def _const_spec(shape):
    n = len(shape)
    return pl.BlockSpec(shape, lambda i, _n=n: (0,) * _n, pipeline_mode=pl.Buffered(1))


def _acc_spec(shape):
    n = len(shape)
    return pl.BlockSpec(shape, lambda i, _n=n: (0,) * _n)


def _me():
    return 4 * lax.axis_index("x") + 2 * lax.axis_index("y") + lax.axis_index("c")


def _peer(r):
    x, y, c = lax.axis_index("x"), lax.axis_index("y"), lax.axis_index("c")
    px = 1 - x if (r >> 2) & 1 else x
    py = 1 - y if (r >> 1) & 1 else y
    pc = 1 - c if r & 1 else c
    return (px, py, pc), 4 * px + 2 * py + pc


def _all_gather_small(v, name):
    r, c = v.shape

    def body(v_ref, out_ref, send_sems, recv_sems):
        me = _me()
        out_ref[me] = v_ref[...]
        sends = []
        for k in range(1, NDEV):
            peer, _ = _peer(k)
            cp = pltpu.make_async_remote_copy(src_ref=v_ref, dst_ref=out_ref.at[me], send_sem=send_sems.at[k - 1],
                                              recv_sem=recv_sems.at[k - 1], device_id=peer, device_id_type=MESH)
            cp.start()
            sends.append(cp)
        for k in range(1, NDEV):
            peer, pidx = _peer(k)
            pltpu.make_async_remote_copy(src_ref=v_ref, dst_ref=out_ref.at[pidx], send_sem=send_sems.at[k - 1],
                                         recv_sem=recv_sems.at[k - 1], device_id=peer, device_id_type=MESH).wait_recv()
        for cp in sends:
            cp.wait_send()

    return pl.pallas_call(
        body, name=name, out_shape=jax.ShapeDtypeStruct((NDEV, r, c), v.dtype),
        in_specs=[pl.BlockSpec(memory_space=pltpu.VMEM)], out_specs=pl.BlockSpec(memory_space=pltpu.VMEM),
        scratch_shapes=[pltpu.SemaphoreType.DMA((NDEV - 1,)), pltpu.SemaphoreType.DMA((NDEV - 1,))],
        compiler_params=_CP(),
    )(v)


def _all_reduce_small(v, name):
    r, c = v.shape

    def body(v_ref, out_ref, land, send_sems, recv_sems):
        me = _me()
        land[me] = v_ref[...]
        sends = []
        for k in range(1, NDEV):
            peer, _ = _peer(k)
            cp = pltpu.make_async_remote_copy(src_ref=v_ref, dst_ref=land.at[me], send_sem=send_sems.at[k - 1],
                                              recv_sem=recv_sems.at[k - 1], device_id=peer, device_id_type=MESH)
            cp.start()
            sends.append(cp)
        for k in range(1, NDEV):
            peer, pidx = _peer(k)
            pltpu.make_async_remote_copy(src_ref=v_ref, dst_ref=land.at[pidx], send_sem=send_sems.at[k - 1],
                                         recv_sem=recv_sems.at[k - 1], device_id=peer, device_id_type=MESH).wait_recv()
        for cp in sends:
            cp.wait_send()
        acc = land[0]
        for j in range(1, NDEV):
            acc = acc + land[j]
        out_ref[...] = acc

    return pl.pallas_call(
        body, name=name, out_shape=jax.ShapeDtypeStruct((r, c), v.dtype),
        in_specs=[pl.BlockSpec(memory_space=pltpu.VMEM)], out_specs=pl.BlockSpec(memory_space=pltpu.VMEM),
        scratch_shapes=[pltpu.VMEM((NDEV, r, c), v.dtype), pltpu.SemaphoreType.DMA((NDEV - 1,)),
                        pltpu.SemaphoreType.DMA((NDEV - 1,))],
        compiler_params=_CP(),
    )(v)


class _Exchange:
    def __init__(self, srcs, dsts, send_sems, recv_sems, local_sems, scatter):
        me = _me()
        n = len(srcs)
        self.sends, self.recvs, self.locals = [], [], []
        for a, (s, d) in enumerate(zip(srcs, dsts)):
            self.locals.append(pltpu.make_async_copy(s.at[me] if scatter else s, d.at[me], local_sems.at[a]))
        for k in range(1, NDEV):
            peer, pidx = _peer(k)
            for a, (s, d) in enumerate(zip(srcs, dsts)):
                src = s.at[pidx] if scatter else s
                sem = (k - 1) * n + a
                for dst, out in ((d.at[me], self.sends), (d.at[pidx], self.recvs)):
                    out.append(pltpu.make_async_remote_copy(src_ref=src, dst_ref=dst, send_sem=send_sems.at[sem],
                                                            recv_sem=recv_sems.at[sem], device_id=peer, device_id_type=MESH))

    def start(self):
        for cp in self.locals + self.sends:
            cp.start()

    def wait(self):
        for cp in self.recvs:
            cp.wait_recv()
        for cp in self.sends:
            cp.wait_send()
        for cp in self.locals:
            cp.wait()


def _exchange_shapes(arrays, scatter):
    return [jax.ShapeDtypeStruct(a.shape if scatter else (NDEV,) + a.shape, a.dtype) for a in arrays]


def _exchange_sems(n):
    return [pltpu.SemaphoreType.DMA(((NDEV - 1) * n,)), pltpu.SemaphoreType.DMA(((NDEV - 1) * n,)), pltpu.SemaphoreType.DMA((n,))]


def _exchange(arrays, scatter, name):
    n = len(arrays)

    def body(*refs):
        ex = _Exchange(refs[:n], refs[n:2 * n], *refs[2 * n:], scatter)
        ex.start()
        ex.wait()

    return pl.pallas_call(body, name=name, out_shape=_exchange_shapes(arrays, scatter), in_specs=[_ANY] * n,
                          out_specs=[_ANY] * n, scratch_shapes=_exchange_sems(n), compiler_params=_CP())(*arrays)


def _chips():
    x, y, c = lax.axis_index("x"), lax.axis_index("y"), lax.axis_index("c")
    return (x, y, c), (x, y, 1 - c), [(1 - x, y), (x, 1 - y), (1 - x, 1 - y)]


def _gather_two_level(arrays, name):
    n = len(arrays)

    def body(*refs):
        srcs, outs = refs[:n], refs[n:2 * n]
        send_sems, recv_sems = refs[2 * n:]
        me, sibling, chips = _chips()
        c = me[2]
        idx = lambda p: 4 * p[0] + 2 * p[1] + p[2]

        def copy(a, k, block, to, src=None):
            return pltpu.make_async_remote_copy(
                src_ref=outs[a].at[idx(block)] if src is None else src, dst_ref=outs[a].at[idx(block)],
                send_sem=send_sems.at[7 * a + k], recv_sem=recv_sems.at[7 * a + k], device_id=to, device_id_type=MESH)

        first, passed = [], []
        for a in range(n):
            outs[a][idx(me)] = srcs[a][...]
            first += [copy(a, 0, me, sibling, src=srcs[a])]
            first += [copy(a, 1 + j, me, (*chip, c), src=srcs[a]) for j, chip in enumerate(chips)]
        for cp in first:
            cp.start()
        for a in range(n):
            for j, chip in enumerate(chips):
                copy(a, 1 + j, (*chip, c), me).wait_recv()
                cp = copy(a, 4 + j, (*chip, c), sibling)
                cp.start()
                passed.append(cp)
        for a in range(n):
            copy(a, 0, sibling, me).wait_recv()
            for j, chip in enumerate(chips):
                copy(a, 4 + j, (*chip, 1 - c), me).wait_recv()
        for cp in first + passed:
            cp.wait_send()

    vm = pl.BlockSpec(memory_space=pltpu.VMEM)
    return pl.pallas_call(
        body, name=name, out_shape=[jax.ShapeDtypeStruct((NDEV,) + a.shape, a.dtype) for a in arrays],
        in_specs=[vm] * n, out_specs=[vm] * n,
        scratch_shapes=[pltpu.SemaphoreType.DMA((7 * n,)), pltpu.SemaphoreType.DMA((7 * n,))],
        compiler_params=_CP(),
    )(*arrays)


def _reduce_scatter_two_level(g, name):
    _, r, c = g.shape
    nchip = NDEV // 2

    def body(g_ref, o_ref, stage, part, land, d_send, d_recv, i_send, i_recv):
        me, sibling, chips = _chips()
        x, y, cc = me
        mine = 2 * x + y

        def blk(k, core):
            return 2 * k + core

        swaps = [pltpu.make_async_remote_copy(src_ref=g_ref.at[blk(k, 1 - cc)], dst_ref=stage.at[k], send_sem=d_send.at[k],
                                              recv_sem=d_recv.at[k], device_id=sibling, device_id_type=MESH) for k in range(nchip)]
        for cp in swaps:
            cp.start()
        for cp in swaps:
            cp.wait_recv()
        for k in range(nchip):
            part[k] = (g_ref[blk(k, cc)].astype(f32) + stage[k].astype(f32)).astype(bf16)
        sends = []
        for j, chip in enumerate(chips):
            kd = 2 * chip[0] + chip[1]
            cp = pltpu.make_async_remote_copy(src_ref=part.at[kd], dst_ref=land.at[mine], send_sem=i_send.at[j],
                                              recv_sem=i_recv.at[j], device_id=(*chip, cc), device_id_type=MESH)
            cp.start()
            sends.append(cp)
        land[mine] = part[mine]
        for j, chip in enumerate(chips):
            ks = 2 * chip[0] + chip[1]
            pltpu.make_async_remote_copy(src_ref=part.at[ks], dst_ref=land.at[ks], send_sem=i_send.at[j], recv_sem=i_recv.at[j],
                                         device_id=(*chip, cc), device_id_type=MESH).wait_recv()
        for cp in swaps + sends:
            cp.wait_send()
        acc = land[0].astype(f32)
        for k in range(1, nchip):
            acc = acc + land[k].astype(f32)
        o_ref[...] = acc

    vm = pl.BlockSpec(memory_space=pltpu.VMEM)
    return pl.pallas_call(
        body, name=name, out_shape=jax.ShapeDtypeStruct((r, c), f32), in_specs=[vm], out_specs=vm,
        scratch_shapes=[pltpu.VMEM((nchip, r, c), g.dtype)] * 3 + [pltpu.SemaphoreType.DMA((nchip,)), pltpu.SemaphoreType.DMA((nchip,)),
                                                                   pltpu.SemaphoreType.DMA((3,)), pltpu.SemaphoreType.DMA((3,))],
        compiler_params=_CP(),
    )(g)


class _Cargo:
    def __init__(self, cargo):
        self.arrays, self.scatter = cargo if cargo else ([], False)
        self.n = len(self.arrays)

    def in_specs(self):
        return [_ANY] * self.n

    def out_shapes(self):
        return _exchange_shapes(self.arrays, self.scatter)

    def sems(self):
        return _exchange_sems(self.n) if self.n else []

    def split(self, refs, n_in, n_out, n_scratch):
        n = self.n
        return refs[:n_in], refs[n_in + n:n_in + n + n_out], refs[n_in + 2 * n + n_out:n_in + 2 * n + n_out + n_scratch]

    def ride(self, refs, n_in, n_out, grid):
        if not self.n:
            return
        n = self.n
        ex = _Exchange(refs[n_in:n_in + n], refs[n_in + n + n_out:n_in + 2 * n + n_out], *refs[-3:], self.scatter)
        grid = (grid,) if isinstance(grid, int) else tuple(grid)
        first = functools.reduce(jnp.logical_and, [pl.program_id(a) == 0 for a in range(len(grid))])
        last = functools.reduce(jnp.logical_and, [pl.program_id(a) == g - 1 for a, g in enumerate(grid)])

        @pl.when(first)
        def _():
            ex.start()

        @pl.when(last)
        def _():
            ex.wait()


def _sum8(land, name):
    _, r, c = land.shape
    rb = next((b for b in (256, 64, 32) if r % b == 0), r)

    def body(l_ref, o_ref):
        acc = l_ref[0].astype(f32)
        for j in range(1, NDEV):
            acc = acc + l_ref[j].astype(f32)
        o_ref[...] = acc

    return pl.pallas_call(
        body, name=name, grid=(r // rb,), out_shape=jax.ShapeDtypeStruct((r, c), f32),
        in_specs=[pl.BlockSpec((NDEV, rb, c), lambda i: (0, i, 0))], out_specs=pl.BlockSpec((rb, c), lambda i: (i, 0)),
        compiler_params=_CP(dimension_semantics=("parallel",)),
    )(land)


def _ada_fwd(c9, w_mod_l, name):
    def body(c_ref, w_ref, o_ref):
        o_ref[...] = dnn(jax.nn.silu(c_ref[...]), w_ref[...])

    return pl.pallas_call(body, name=name, out_shape=jax.ShapeDtypeStruct((16, w_mod_l.shape[1]), f32),
                          compiler_params=_CP())(c9, w_mod_l)


def _mod_select(m_all, b_mod6, name):
    def body(m_ref, b_ref, mx_ref, mc_ref):
        me = _me()
        mx_ref[...] = m_ref[me] + b_ref[...]
        mc_ref[...] = m_ref[8] + b_ref[...]

    return pl.pallas_call(body, name=name, out_shape=[jax.ShapeDtypeStruct((6, D), f32)] * 2, compiler_params=_CP())(m_all, b_mod6)


def _ada_bwd(c9, dmx_all, dmc_all, dmx_l, dmc_l, w_mod_l, name):
    ncol = w_mod_l.shape[1]

    def rowsum(r):
        acc = r[0:1]
        for j in range(1, NDEV):
            acc = acc + r[j:j + 1]
        return acc

    def body(c_ref, xa_ref, ca_ref, xl_ref, cl_ref, w_ref, gw_ref, gb_ref, dc_ref):
        s9, vjp = jax.vjp(jax.nn.silu, c_ref[...])
        dm9 = jnp.concatenate([xl_ref[...], rowsum(cl_ref[...]), jnp.zeros((7, ncol), f32)], axis=0)
        gw_ref[...] = dtn(s9, dm9)
        gb_ref[...] = rowsum(xa_ref[...]) + rowsum(ca_ref[...])
        dc_ref[...] = vjp(dnt(dm9, w_ref[...]))[0]

    return pl.pallas_call(
        body, name=name,
        out_shape=[jax.ShapeDtypeStruct((D, ncol), f32), jax.ShapeDtypeStruct((1, 6 * D), f32), jax.ShapeDtypeStruct((16, D), f32)],
        compiler_params=_CP())(c9, dmx_all, dmc_all, dmx_l, dmc_l, w_mod_l)


def _lane_sign(rank):
    shape = (1,) * (rank - 1) + (SB,)
    return jnp.where(lax.broadcasted_iota(jnp.int32, shape, rank - 1) < S5N, -1.0, 1.0)


def _s5_build_fn(lre2, lim2, ls, bn, bs, cn, cs, rev):
    sg = _lane_sign(3)
    s = jnp.exp(ls)
    ar, ai = lre2 * s, lim2 * s
    e = jnp.exp(ar)
    nr, ni = e * jnp.cos(ai) - 1.0, e * jnp.sin(ai)
    den = lre2 * lre2 + lim2 * lim2
    cr, ci = (nr * lre2 + ni * lim2) / den, (ni * lre2 - nr * lim2) / den
    bbn = cr * bn + (ci * sg) * bs
    bbs = cr * bs - (ci * sg) * bn

    def powers(ex):
        m, ang = jnp.exp(ex * ar), ex * ai
        return m * jnp.cos(ang), m * jnp.sin(ang) * sg

    def times(tabs, xn, xs):
        f1, f2 = tabs
        return f1[:, :, None, :] * xn[:, None, :, :] + f2[:, :, None, :] * xs[:, None, :, :]

    t = lax.broadcasted_iota(jnp.int32, (1, TC, 1), 1).astype(f32)
    if rev:
        e_src, e_dst, e_out, e_in = t - (TC - 1.0), (TC - 1.0) - t, t, TC - t
    else:
        e_src, e_dst, e_out, e_in = -t, t, (TC - 1.0) - t, t + 1.0
    g = lre2.shape[0]
    flat = lambda a: a.reshape(g, TCP, SB)
    conj = -_lane_sign(4)
    ll = flat(times(powers(e_src), bbn, bbs))
    rr = flat(times(powers(e_dst), cn, cs) * conj)
    mb = flat(times(powers(e_out), bbn, bbs))
    mct = flat(times(powers(e_in), cn, cs) * conj)
    a1, a2 = powers(float(TC))
    row = lax.broadcasted_iota(jnp.int32, (TCP, TCP), 0) // S5P
    col = lax.broadcasted_iota(jnp.int32, (TCP, TCP), 1) // S5P
    mask = jnp.where((col <= row) if rev else (col >= row), 1.0, 0.0)
    m = jnp.concatenate([dnt(ll[j], rr[j])[None] for j in range(g)], axis=0) * mask
    return m, mb, mct, a1, a2


def _gspec(*tail):
    nt = len(tail)
    return pl.BlockSpec((GBK,) + tail, lambda i, _n=nt: (i,) + (0,) * _n)


def _s5_build(params, rev, name):
    def body(l1, l2, ls, bn, bs, cn, cs, m_ref, mb_ref, mc_ref, a1_ref, a2_ref):
        m, mb, mct, a1, a2 = _s5_build_fn(l1[...], l2[...], ls[...], bn[...], bs[...], cn[...], cs[...], rev)
        m_ref[...], mb_ref[...], mc_ref[...] = m.astype(bf16), mb.astype(bf16), mct.astype(bf16)
        a1_ref[...], a2_ref[...] = a1, a2

    vec, pm = _gspec(1, SB), _gspec(S5P, SB)
    return pl.pallas_call(
        body, name=name, grid=(S5G // GBK,),
        in_specs=[vec, vec, _gspec(1, 1), pm, pm, pm, pm],
        out_specs=[_gspec(TCP, TCP), _gspec(TCP, SB), _gspec(TCP, SB), vec, vec],
        out_shape=[jax.ShapeDtypeStruct((S5G, TCP, TCP), bf16), jax.ShapeDtypeStruct((S5G, TCP, SB), bf16),
                   jax.ShapeDtypeStruct((S5G, TCP, SB), bf16), jax.ShapeDtypeStruct((S5G, 1, SB), f32),
                   jax.ShapeDtypeStruct((S5G, 1, SB), f32)],
        compiler_params=_CP(dimension_semantics=("parallel",)),
    )(*params)


def _s5_build_bwd(params, cots, prev, rev, name):
    def body(l1, l2, ls, bn, bs, cn, cs, dm, dmb, dmc, da1, da2, pb, pc, gl1, gl2, gls, gb, gc):
        prim = (l1[...], l2[...], ls[...], bn[...], bs[...], cn[...], cs[...])
        _, vjp = jax.vjp(functools.partial(_s5_build_fn, rev=rev), *prim)
        d1, d2, dls, dbn, dbs, dcn, dcs = vjp((dm[...], dmb[...], dmc[...], da1[...], da2[...]))
        gl1[...] = d1 + pltpu.roll(d1, S5N, axis=2)
        gl2[...] = d2 + pltpu.roll(d2, S5N, axis=2)
        gls[...] = dls
        gb[...] = dbn + pltpu.roll(dbs, S5N, axis=2) + pb[...]
        gc[...] = dcn + pltpu.roll(dcs, S5N, axis=2) + pc[...]

    vec, pm, big = _gspec(1, SB), _gspec(S5P, SB), _gspec(TCP, SB)
    return pl.pallas_call(
        body, name=name, grid=(S5G // GBK,),
        in_specs=[vec, vec, _gspec(1, 1), pm, pm, pm, pm, _gspec(TCP, TCP), big, big, vec, vec, pm, pm],
        out_specs=[vec, vec, _gspec(1, 1), pm, pm],
        out_shape=[jax.ShapeDtypeStruct((S5G, 1, SB), f32), jax.ShapeDtypeStruct((S5G, 1, SB), f32),
                   jax.ShapeDtypeStruct((S5G, 1, 1), f32), jax.ShapeDtypeStruct((S5G, S5P, SB), f32),
                   jax.ShapeDtypeStruct((S5G, S5P, SB), f32)],
        compiler_params=_CP(dimension_semantics=("parallel",)),
    )(*params, *cots, *prev)


def _s5_inc(u, mb_f, mb_b, name):
    nc = u.shape[1]

    def body(u_ref, mf_ref, mb_ref, sf_ref, sb_ref):
        for j in range(GBK):
            sf_ref[:, j, :] = jnp.dot(u_ref[j], mf_ref[j], preferred_element_type=f32)
            sb_ref[:, j, :] = jnp.dot(u_ref[j], mb_ref[j], preferred_element_type=f32)

    sspec = pl.BlockSpec((nc, GBK, SB), lambda i: (0, i, 0))
    return pl.pallas_call(
        body, name=name, grid=(S5G // GBK,), in_specs=[_gspec(nc, TCP), _gspec(TCP, SB), _gspec(TCP, SB)],
        out_specs=[sspec, sspec], out_shape=[jax.ShapeDtypeStruct((nc, S5G, SB), f32)] * 2,
        compiler_params=_CP(dimension_semantics=("parallel",)),
    )(u, mb_f, mb_b)


def _idx_fwd(nctx, nch):
    return lambda i: i


def _idx_rev(nctx, nch):
    return lambda i: jnp.where(i < nctx, nctx - 1 - i, nch + nctx - 1 - i)


def _carry_loop(nc, step, init):
    def trip(i, c):
        for k in range(CARRY_UNROLL):
            c = step(i * CARRY_UNROLL + k, c)
        return c

    return lax.fori_loop(0, nc // CARRY_UNROLL, trip, init)


def _s5_carry(s_f, s_b, a_f, a_b, nctx, name):
    nc = s_f.shape[0]
    idx_b = _idx_rev(nctx, nc)

    def body(sf_ref, sb_ref, f1_ref, f2_ref, b1_ref, b2_ref, hf_ref, hb_ref):
        f1, f2, b1, b2 = f1_ref[...], f2_ref[...], b1_ref[...], b2_ref[...]

        def step(i, c):
            hf, hfs, hb, hbs = c
            rb = idx_b(i)
            hf_ref[i] = hf
            hb_ref[rb] = hb
            sf, sb = sf_ref[i], sb_ref[rb]
            return (f1 * hf + f2 * hfs + sf, f1 * hfs - f2 * hf + pltpu.roll(sf, S5N, axis=1),
                    b1 * hb + b2 * hbs + sb, b1 * hbs - b2 * hb + pltpu.roll(sb, S5N, axis=1))

        z = jnp.zeros((S5G, SB), f32)
        _carry_loop(nc, step, (z, z, z, z))

    return pl.pallas_call(body, name=name, out_shape=[jax.ShapeDtypeStruct(s_f.shape, f32)] * 2,
                          compiler_params=_CP())(s_f, s_b, *a_f, *a_b)


def _s5_carry_bwd(dhp, hp, a1, a2, rev, nctx, name):
    nc = hp.shape[0]
    idx = (_idx_rev if rev else _idx_fwd)(nctx, nc)

    def body(dhp_ref, hp_ref, a1_ref, a2_ref, ds_ref, d1_ref, d2_ref):
        f1, f2 = a1_ref[...], a2_ref[...]

        def step(k, carry):
            ab, abs_, d1, d2 = carry
            r = idx(nc - 1 - k)
            ds_ref[r] = ab
            h, dh = hp_ref[r], dhp_ref[r]
            return (dh + f1 * ab - f2 * abs_, pltpu.roll(dh, S5N, axis=1) + f1 * abs_ + f2 * ab,
                    d1 + ab * h, d2 + ab * pltpu.roll(h, S5N, axis=1))

        z = jnp.zeros((S5G, SB), f32)
        _, _, d1, d2 = _carry_loop(nc, step, (z, z, z, z))
        d1_ref[...], d2_ref[...] = d1, d2

    return pl.pallas_call(
        body, name=name,
        out_shape=[jax.ShapeDtypeStruct(hp.shape, f32), jax.ShapeDtypeStruct((S5G, SB), f32), jax.ShapeDtypeStruct((S5G, SB), f32)],
        compiler_params=_CP())(dhp, hp, a1, a2)


def _s5_out(u, m_f, m_b, hp_f, hp_b, mc_f, mc_b, name):
    nc = u.shape[1]

    def body(u_ref, mf_ref, mb_ref, hf_ref, hb_ref, cf_ref, cb_ref, y_ref):
        for j in range(GBK):
            uj = u_ref[j]
            y_ref[j] = (jnp.dot(uj, mf_ref[j], preferred_element_type=f32) + jnp.dot(uj, mb_ref[j], preferred_element_type=f32)
                        + dnt(hf_ref[:, j, :], cf_ref[j]) + dnt(hb_ref[:, j, :], cb_ref[j])).astype(bf16)

    sspec = pl.BlockSpec((nc, GBK, SB), lambda i: (0, i, 0))
    return pl.pallas_call(
        body, name=name, grid=(S5G // GBK,),
        in_specs=[_gspec(nc, TCP), _gspec(TCP, TCP), _gspec(TCP, TCP), sspec, sspec, _gspec(TCP, SB), _gspec(TCP, SB)],
        out_specs=_gspec(nc, TCP), out_shape=jax.ShapeDtypeStruct((S5G, nc, TCP), bf16),
        compiler_params=_CP(dimension_semantics=("parallel",)),
    )(u, m_f, m_b, hp_f, hp_b, mc_f, mc_b)


def _s5_out_bwd(dy, u, m_f, m_b, hp_f, hp_b, mc_f, mc_b, name):
    nc = u.shape[1]

    def body(dy_ref, u_ref, mf_ref, mb_ref, hf_ref, hb_ref, cf_ref, cb_ref, du_ref, g_ref, dhf_ref, dhb_ref, dcf_ref, dcb_ref):
        for j in range(GBK):
            dyj = dy_ref[j]
            du_ref[j] = dnt(dyj, mf_ref[j]) + dnt(dyj, mb_ref[j])
            g_ref[j] = dtn(u_ref[j], dyj)
            dhf_ref[:, j, :] = dnn(dyj, cf_ref[j])
            dhb_ref[:, j, :] = dnn(dyj, cb_ref[j])
            dcf_ref[j] = dtn(dyj, hf_ref[:, j, :])
            dcb_ref[j] = dtn(dyj, hb_ref[:, j, :])

    sspec = pl.BlockSpec((nc, GBK, SB), lambda i: (0, i, 0))
    sshape = jax.ShapeDtypeStruct((nc, S5G, SB), f32)
    cshape = jax.ShapeDtypeStruct((S5G, TCP, SB), f32)
    return pl.pallas_call(
        body, name=name, grid=(S5G // GBK,),
        in_specs=[_gspec(nc, TCP), _gspec(nc, TCP), _gspec(TCP, TCP), _gspec(TCP, TCP), sspec, sspec, _gspec(TCP, SB), _gspec(TCP, SB)],
        out_specs=[_gspec(nc, TCP), _gspec(TCP, TCP), sspec, sspec, _gspec(TCP, SB), _gspec(TCP, SB)],
        out_shape=[jax.ShapeDtypeStruct((S5G, nc, TCP), f32), jax.ShapeDtypeStruct((S5G, TCP, TCP), f32), sshape, sshape, cshape, cshape],
        compiler_params=_CP(dimension_semantics=("parallel",)),
    )(dy, u, m_f, m_b, hp_f, hp_b, mc_f, mc_b)


def _s5_inc_bwd(du1, u, ds_f, ds_b, mb_f, mb_b, name):
    nc = u.shape[1]

    def body(du1_ref, u_ref, dsf_ref, dsb_ref, mf_ref, mb_ref, du_ref, dmf_ref, dmb_ref):
        for j in range(GBK):
            dsf, dsb = dsf_ref[:, j, :], dsb_ref[:, j, :]
            du_ref[j] = (du1_ref[j] + dnt(dsf, mf_ref[j]) + dnt(dsb, mb_ref[j])).astype(bf16)
            dmf_ref[j] = dtn(u_ref[j], dsf)
            dmb_ref[j] = dtn(u_ref[j], dsb)

    sspec = pl.BlockSpec((nc, GBK, SB), lambda i: (0, i, 0))
    cshape = jax.ShapeDtypeStruct((S5G, TCP, SB), f32)
    return pl.pallas_call(
        body, name=name, grid=(S5G // GBK,),
        in_specs=[_gspec(nc, TCP), _gspec(nc, TCP), sspec, sspec, _gspec(TCP, SB), _gspec(TCP, SB)],
        out_specs=[_gspec(nc, TCP), _gspec(TCP, SB), _gspec(TCP, SB)],
        out_shape=[jax.ShapeDtypeStruct((S5G, nc, TCP), bf16), cshape, cshape],
        compiler_params=_CP(dimension_semantics=("parallel",)),
    )(du1, u, ds_f, ds_b, mb_f, mb_b)


def _to_groups(a):
    n = a.shape[0]
    return a.reshape(n // TC, TC, S5G, S5P).transpose(2, 0, 1, 3).reshape(S5G, n // TC, TCP)


def _from_groups(a):
    nc = a.shape[1]
    return a.reshape(S5G, nc, TC, S5P).transpose(1, 2, 0, 3).reshape(nc * TC, S5W)


def _swap_pairs(t):
    lane = lax.broadcasted_iota(jnp.int32, t.shape, 1)
    return jnp.where(lane % 2 == 0, pltpu.roll(t, DH - 1, axis=1), pltpu.roll(t, 1, axis=1))


def _rot(t, cosf, sins):
    return t * cosf + _swap_pairs(t) * sins


def _rot_t(d, cosf, sins):
    return d * cosf - _swap_pairs(d) * sins


def _ret_tables(ld, rev):
    pos = lax.broadcasted_iota(jnp.int32, (T, 1), 0).astype(f32)
    diff = pos - lax.broadcasted_iota(jnp.int32, (1, T), 1).astype(f32)
    if rev:
        keep, dist = diff < 0, jnp.maximum(-diff, 0.0)
        xi, zeta = jnp.exp(ld * (T - pos)), jnp.exp(ld * pos)
    else:
        keep, dist = diff >= 0, jnp.maximum(diff, 0.0)
        xi, zeta = jnp.exp(ld * (pos + 1.0)), jnp.exp(ld * (T - 1.0 - pos))
    return jnp.where(keep, jnp.exp(ld * dist), 0.0), xi, zeta, jnp.exp(ld * float(T))


def _ret_apply(qr, kr, v, rp, dm, xi, zeta, cdec):
    out = dnn(dnt(qr, kr) * dm, v) + dnn(qr * xi, rp)
    return out, cdec * rp + dtn(kr * zeta, v)


def _ret_chunk(qr, kr, v, rp, ld, rev):
    return _ret_apply(qr, kr, v, rp, *_ret_tables(ld, rev))


def _ret_fwd(p_ext, ld8_f, ld8_b, nctx, name, cargo=None):
    n = p_ext.shape[0]
    nch = n // T
    idx_b = _idx_rev(nctx, nch)
    cg = _Cargo(cargo)

    def body(*refs):
        ins, (of_ref, ob_ref, rpf_ref, rpb_ref), (rf_s, rb_s, dm_s, xz_s) = cg.split(refs, 8, 4, 4)
        qf, kf, vf, qb, kb, vb, ldf_ref, ldb_ref = ins
        cg.ride(refs, 8, 4, nch)

        @pl.when(pl.program_id(0) == 0)
        def _():
            rf_s[...] = jnp.zeros_like(rf_s)
            rb_s[...] = jnp.zeros_like(rb_s)
            for d, ld_ref in enumerate((ldf_ref, ldb_ref)):
                for h in range(RH):
                    dm, xi, zeta, cdec = _ret_tables(ld_ref[h:h + 1, 0:1], bool(d))
                    dm_s[d, h] = dm
                    xz_s[d, h, 0] = jnp.broadcast_to(xi, (T, DH))
                    xz_s[d, h, 1] = jnp.broadcast_to(zeta, (T, DH))
                    xz_s[d, h, 2] = jnp.broadcast_to(cdec, (T, DH))

        for h in range(RH):
            sl = slice(h * DH, (h + 1) * DH)
            for d, (q_ref, k_ref, v_ref, o_ref, rp_ref, r_s) in enumerate(((qf, kf, vf, of_ref, rpf_ref, rf_s),
                                                                            (qb, kb, vb, ob_ref, rpb_ref, rb_s))):
                rp = r_s[h]
                rp_ref[0, h] = rp
                out, rn = _ret_apply(q_ref[:, sl].astype(f32), k_ref[:, sl].astype(f32), v_ref[:, sl].astype(f32), rp,
                                     dm_s[d, h], xz_s[d, h, 0], xz_s[d, h, 1], xz_s[d, h, 2])
                r_s[h] = rn
                o_ref[:, sl] = out

    fcol = lambda cb: pl.BlockSpec((T, RW), lambda i, _c=cb: (i, _c))
    bcol = lambda cb: pl.BlockSpec((T, RW), lambda i, _c=cb: (idx_b(i), _c))
    rspec = pl.BlockSpec((1, RH, DH, DH), lambda i: (i, 0, 0, 0))
    oshape, rshape = jax.ShapeDtypeStruct((n, RW), f32), jax.ShapeDtypeStruct((nch, RH, DH, DH), f32)
    return pl.pallas_call(
        body, name=name, grid=(nch,),
        in_specs=[fcol(1), fcol(2), fcol(3), bcol(1), bcol(2), bcol(3), _const_spec((8, 128)), _const_spec((8, 128))] + cg.in_specs(),
        out_specs=[fcol(0), bcol(0), rspec, rspec] + cg.in_specs(),
        out_shape=[oshape, oshape, rshape, rshape] + cg.out_shapes(),
        scratch_shapes=[pltpu.VMEM((RH, DH, DH), f32)] * 2 + [pltpu.VMEM((2, RH, T, T), f32), pltpu.VMEM((2, RH, 3, T, DH), f32)] + cg.sems(),
        compiler_params=_CP(dimension_semantics=_ARB),
    )(p_ext, p_ext, p_ext, p_ext, p_ext, p_ext, ld8_f, ld8_b, *cg.arrays)


def _ret_bwd(p_ext, ld8_f, ld8_b, rp_f, rp_b, do_ext, nctx, name, cargo=None):
    n = p_ext.shape[0]
    nch = n // T
    idx_rev = _idx_rev(nctx, nch)
    idf = lambda j: nch - 1 - j
    idb = lambda j: idx_rev(nch - 1 - j)
    cg = _Cargo(cargo)

    def body(*refs):
        ins, outs, (drf_s, drb_s) = cg.split(refs, 12, 8, 2)
        qf, kf, vf, qb, kb, vb, ldf_ref, ldb_ref, rpf_ref, rpb_ref, dof_ref, dob_ref = ins
        dqf, dkf, dvf, dqb, dkb, dvb, dldf_ref, dldb_ref = outs
        cg.ride(refs, 12, 8, nch)

        @pl.when(pl.program_id(0) == 0)
        def _():
            for r in (drf_s, drb_s, dldf_ref, dldb_ref):
                r[...] = jnp.zeros_like(r)

        for h in range(RH):
            sl = slice(h * DH, (h + 1) * DH)
            for q_ref, k_ref, v_ref, ld_ref, rp_ref, do_ref, dq_ref, dk_ref, dv_ref, dld_ref, dr_s, rev in (
                    (qf, kf, vf, ldf_ref, rpf_ref, dof_ref, dqf, dkf, dvf, dldf_ref, drf_s, False),
                    (qb, kb, vb, ldb_ref, rpb_ref, dob_ref, dqb, dkb, dvb, dldb_ref, drb_s, True)):
                _, vjp = jax.vjp(functools.partial(_ret_chunk, rev=rev), q_ref[:, sl].astype(f32), k_ref[:, sl].astype(f32),
                                 v_ref[:, sl].astype(f32), rp_ref[0, h], ld_ref[h:h + 1, 0:1])
                dqr, dkr, dv, drp, dld = vjp((do_ref[:, sl], dr_s[h]))
                dr_s[h] = drp
                dq_ref[:, sl], dk_ref[:, sl], dv_ref[:, sl] = dqr, dkr, dv
                dld_ref[h:h + 1, :] += jnp.broadcast_to(dld, (1, 128))

    fcol = lambda cb: pl.BlockSpec((T, RW), lambda j, _c=cb: (idf(j), _c))
    bcol = lambda cb: pl.BlockSpec((T, RW), lambda j, _c=cb: (idb(j), _c))
    rspec = pl.BlockSpec((1, RH, DH, DH), lambda j: (nch - 1 - j, 0, 0, 0))
    oshape = jax.ShapeDtypeStruct((n, RW), f32)
    return pl.pallas_call(
        body, name=name, grid=(nch,),
        in_specs=[fcol(1), fcol(2), fcol(3), bcol(1), bcol(2), bcol(3), _const_spec((8, 128)), _const_spec((8, 128)), rspec, rspec,
                  fcol(0), bcol(0)] + cg.in_specs(),
        out_specs=[fcol(0), fcol(0), fcol(0), bcol(0), bcol(0), bcol(0), _acc_spec((8, 128)), _acc_spec((8, 128))] + cg.in_specs(),
        out_shape=[oshape] * 6 + [jax.ShapeDtypeStruct((8, 128), f32)] * 2 + cg.out_shapes(),
        scratch_shapes=[pltpu.VMEM((RH, DH, DH), f32)] * 2 + cg.sems(),
        compiler_params=_CP(dimension_semantics=_ARB),
    )(p_ext, p_ext, p_ext, p_ext, p_ext, p_ext, ld8_f, ld8_b, rp_f, rp_b, do_ext, do_ext, *cg.arrays)


def _qk_heads(p, fn_q, fn_k):
    heads = lambda base, fn: [fn(p[:, base + h * DH:base + (h + 1) * DH]) for h in range(RH)]
    return jnp.concatenate([p[:, :S5W]] + heads(S5W, fn_q) + heads(S5W + RW, fn_k) + [p[:, S5W + 2 * RW:]], axis=1)


def _f1_fwd(x, ctx, modx, modc, nw1, w_in_n, cosf, sins, name, cargo=None):
    L = x.shape[0]
    nb = L // R + 1
    scale = DH ** -0.5
    cg = _Cargo(cargo)

    def body(*refs):
        (x_ref, c_ref, mx_ref, mc_ref, nw_ref, w_ref, cos_ref, sin_ref), (p_ref,), _ = cg.split(refs, 8, 1, 0)
        cg.ride(refs, 8, 1, nb)
        is_ctx = pl.program_id(0) == 0
        xin = jnp.where(is_ctx, c_ref[...], x_ref[...])
        sh = jnp.where(is_ctx, mc_ref[0:1], mx_ref[0:1])
        sc = jnp.where(is_ctx, mc_ref[1:2], mx_ref[1:2])
        cf, ss = cos_ref[...], sin_ref[...]
        p = dnn(_mod(_rms(xin, nw_ref[...]), sh, sc), w_ref[...])
        p_ref[...] = _qk_heads(p, lambda t: _rot(t, cf, ss), lambda t: _rot(t * scale, cf, ss)).astype(bf16)

    return pl.pallas_call(
        body, name=name, grid=(nb,),
        in_specs=[pl.BlockSpec((R, D), lambda i: (jnp.maximum(i - 1, 0), 0)), _const_spec((R, D)), _const_spec((6, D)),
                  _const_spec((6, D)), _const_spec((1, D)), _const_spec((D, INC)), pl.BlockSpec((R, DH), lambda i: (i, 0)),
                  pl.BlockSpec((R, DH), lambda i: (i, 0))] + cg.in_specs(),
        out_specs=[pl.BlockSpec((R, INC), lambda i: (i, 0))] + cg.in_specs(),
        out_shape=[jax.ShapeDtypeStruct((L + R, INC), bf16)] + cg.out_shapes(),
        scratch_shapes=cg.sems(),
        compiler_params=_CP(dimension_semantics=_ARB),
    )(x, ctx, modx, modc, nw1, w_in_n, cosf, sins, *cg.arrays)


def _f1_bwd(x, ctx, modx, modc, nw1, w_in_t, cosf, sins, dx1, parts, name, cargo=None):
    L = x.shape[0]
    nb = L // R + 1
    scale = DH ** -0.5
    cg = _Cargo(cargo)

    def body(*refs):
        ins, (gx_ref, dp_ref, h1_ref, dnw_ref, dmx_ref, dmc_ref), _ = cg.split(refs, 18, 6, 0)
        x_ref, c_ref, mx_ref, mc_ref, nw_ref, w_ref, cos_ref, sin_ref, dx1_ref, du0, du1, dq0, dq1, dk0, dk1, dv0, dv1, dg0 = ins
        cg.ride(refs, 18, 6, nb)
        i = pl.program_id(0)
        is_ctx = i == 0

        @pl.when(is_ctx)
        def _():
            dnw_ref[...] = jnp.zeros_like(dnw_ref)
            dmx_ref[...] = jnp.zeros_like(dmx_ref)
            dmc_ref[...] = jnp.zeros_like(dmc_ref)

        cf, ss = cos_ref[...], sin_ref[...]
        dp = jnp.concatenate([du0[...].astype(f32) + du1[...], dq0[...] + dq1[...], dk0[...] + dk1[...], dv0[...] + dv1[...],
                              dg0[...]], axis=1)
        dp = _qk_heads(dp, lambda t: _rot_t(t, cf, ss), lambda t: _rot_t(t, cf, ss) * scale).astype(bf16)
        dp_ref[...] = dp
        xin = jnp.where(is_ctx, c_ref[...], x_ref[...])
        sh = jnp.where(is_ctx, mc_ref[0:1], mx_ref[0:1])
        sc = jnp.where(is_ctx, mc_ref[1:2], mx_ref[1:2])
        dh = dnn(dp, w_ref[...])
        h, vjp = jax.vjp(lambda a, b, c, d: _mod(_rms(a, b), c, d), xin, nw_ref[...], sh, sc)
        dxin, dnw, dsh, dsc = vjp(dh)
        h1_ref[...] = h.astype(bf16)
        gx_ref[...] = dx1_ref[...] + dxin
        dnw_ref[...] += dnw
        wx = jnp.where(is_ctx, 0.0, 1.0)
        dmx_ref[0:1] += dsh * wx
        dmx_ref[1:2] += dsc * wx
        dmc_ref[0:1] += dsh * (1.0 - wx)
        dmc_ref[1:2] += dsc * (1.0 - wx)

    lat = pl.BlockSpec((R, D), lambda i: (jnp.maximum(i - 1, 0), 0))
    ext = pl.BlockSpec((R, S5W), lambda i: (i, 0))
    return pl.pallas_call(
        body, name=name, grid=(nb,),
        in_specs=[lat, _const_spec((R, D)), _const_spec((6, D)), _const_spec((6, D)), _const_spec((1, D)), _const_spec((INC, D)),
                  pl.BlockSpec((R, DH), lambda i: (i, 0)), pl.BlockSpec((R, DH), lambda i: (i, 0)), lat] + [ext] * 9 + cg.in_specs(),
        out_specs=[lat, pl.BlockSpec((R, INC), lambda i: (i, 0)), pl.BlockSpec((R, D), lambda i: (i, 0)),
                   _acc_spec((1, D)), _acc_spec((6, D)), _acc_spec((6, D))] + cg.in_specs(),
        out_shape=[jax.ShapeDtypeStruct((L, D), f32), jax.ShapeDtypeStruct((L + R, INC), bf16),
                   jax.ShapeDtypeStruct((L + R, D), bf16), jax.ShapeDtypeStruct((1, D), f32),
                   jax.ShapeDtypeStruct((6, D), f32), jax.ShapeDtypeStruct((6, D), f32)] + cg.out_shapes(),
        scratch_shapes=cg.sems(),
        compiler_params=_CP(dimension_semantics=_ARB),
    )(x, ctx, modx, modc, nw1, w_in_t, cosf, sins, dx1, *parts, *cg.arrays)


def _ret_post(yr, g):
    outs = []
    for h in range(RH):
        yh = yr[:, h * DH:(h + 1) * DH]
        mu = jnp.mean(yh, axis=-1, keepdims=True)
        var = jnp.mean((yh - mu) ** 2, axis=-1, keepdims=True)
        outs.append((yh - mu) * lax.rsqrt(var + EPS))
    return jax.nn.silu(g) * jnp.concatenate(outs, axis=1)


def _mix_fn(ys, u, of, ob, g, x, dvec, bglu, gate1, pz, pm, wglu, wout):
    s = _gelu(ys + dvec * u)
    z = dnn(s, wglu) + bglu + pz
    cat = jnp.concatenate([s * jax.nn.sigmoid(z), _ret_post(of + ob, g)], axis=1)
    mix = dnn(cat, wout) + pm
    return x + gate1 * mix, (s, cat)


def _mix_fwd(x, ys, of, ob, p_ext, dvec, bglu, modx, wglu, wout, name, cargo=None):
    L = x.shape[0]
    nb = L // R
    cg = _Cargo(cargo)

    def body(*refs):
        ins, (x1_ref,), _ = cg.split(refs, 11, 1, 0)
        x_ref, ys_ref, of_ref, ob_ref, u_ref, g_ref, d_ref, b_ref, mx_ref, wg_ref, wo_ref = ins
        cg.ride(refs, 11, 1, nb)
        x1_ref[...] = _mix_fn(ys_ref[...].astype(f32), u_ref[...].astype(f32), of_ref[...], ob_ref[...], g_ref[...].astype(f32),
                              x_ref[...], d_ref[...], b_ref[...], mx_ref[2:3], 0.0, 0.0, wg_ref[...], wo_ref[...])[0]

    ext = pl.BlockSpec((R, S5W), lambda i: (i + 1, 0))
    return pl.pallas_call(
        body, name=name, grid=(nb,),
        in_specs=[pl.BlockSpec((R, D), lambda i: (i, 0)), ext, ext, ext, ext, pl.BlockSpec((R, RW), lambda i: (i + 1, 4)),
                  _const_spec((1, S5W)), _const_spec((1, S5W)), _const_spec((6, D)), _const_spec((S5W, S5W)), _const_spec((D, D))]
        + cg.in_specs(),
        out_specs=[pl.BlockSpec((R, D), lambda i: (i, 0))] + cg.in_specs(),
        out_shape=[jax.ShapeDtypeStruct((L, D), f32)] + cg.out_shapes(),
        scratch_shapes=cg.sems(),
        compiler_params=_CP(dimension_semantics=_ARB),
    )(x, ys, of, ob, p_ext, p_ext, dvec, bglu, modx, wglu, wout, *cg.arrays)


def _mix_bwd(x, ys, of, ob, p_ext, dvec, bglu, modx, wglu, wout, dx1, name, cargo=None):
    L = x.shape[0]
    nb = L // R + 1
    cg = _Cargo(cargo)

    def body(*refs):
        ins, outs, _ = cg.split(refs, 12, 11, 0)
        x_ref, ys_ref, of_ref, ob_ref, u_ref, g_ref, d_ref, b_ref, mx_ref, wg_ref, wo_ref, dx1_ref = ins
        dy_ref, dud_ref, do_ref, dg_ref, cat_ref, dmix_ref, s_ref, dz_ref, dd_ref, db_ref, dg1_ref = outs
        cg.ride(refs, 12, 11, nb)
        i = pl.program_id(0)

        @pl.when(i == 0)
        def _():
            for r in outs:
                r[...] = jnp.zeros_like(r)

        @pl.when(i > 0)
        def _():
            fn = lambda ys_, u_, of_, g_, d_, b_, g1_, pz_, pm_: _mix_fn(
                ys_, u_, of_, ob_ref[...], g_, x_ref[...], d_, b_, g1_, pz_, pm_, wg_ref[...], wo_ref[...])
            _, vjp, (s, cat) = jax.vjp(fn, ys_ref[...].astype(f32), u_ref[...].astype(f32), of_ref[...], g_ref[...].astype(f32), d_ref[...],
                                       b_ref[...], mx_ref[2:3], jnp.zeros((R, S5W), f32), jnp.zeros((R, D), f32), has_aux=True)
            dy, dud, do, dg, dd, db, dg1, dz, dmix = vjp(dx1_ref[...])
            dy_ref[...], dud_ref[...], do_ref[...], dg_ref[...] = dy.astype(bf16), dud, do, dg
            cat_ref[...], dmix_ref[...] = cat.astype(bf16), dmix.astype(bf16)
            s_ref[...], dz_ref[...] = s.astype(bf16), dz.astype(bf16)
            dd_ref[...] += dd
            db_ref[...] += db
            dg1_ref[...] += dg1

    lat = pl.BlockSpec((R, D), lambda i: (jnp.maximum(i - 1, 0), 0))
    lat5 = pl.BlockSpec((R, S5W), lambda i: (jnp.maximum(i - 1, 0), 0))
    ext = pl.BlockSpec((R, S5W), lambda i: (i, 0))
    eshape = jax.ShapeDtypeStruct((L + R, S5W), f32)
    return pl.pallas_call(
        body, name=name, grid=(nb,),
        in_specs=[lat, ext, ext, ext, ext, pl.BlockSpec((R, RW), lambda i: (i, 4)),
                  _const_spec((1, S5W)), _const_spec((1, S5W)), _const_spec((6, D)), _const_spec((S5W, S5W)), _const_spec((D, D)), lat]
        + cg.in_specs(),
        out_specs=[ext, ext, ext, ext, lat, lat, lat5, lat5, _acc_spec((1, S5W)), _acc_spec((1, S5W)), _acc_spec((1, D))]
        + cg.in_specs(),
        out_shape=[jax.ShapeDtypeStruct((L + R, S5W), bf16), eshape, eshape, eshape, jax.ShapeDtypeStruct((L, D), bf16),
                   jax.ShapeDtypeStruct((L, D), bf16), jax.ShapeDtypeStruct((L, S5W), bf16), jax.ShapeDtypeStruct((L, S5W), bf16),
                   jax.ShapeDtypeStruct((1, S5W), f32), jax.ShapeDtypeStruct((1, S5W), f32), jax.ShapeDtypeStruct((1, D), f32)]
        + cg.out_shapes(),
        scratch_shapes=cg.sems(),
        compiler_params=_CP(dimension_semantics=_ARB),
    )(x, ys, of, ob, p_ext, p_ext, dvec, bglu, modx, wglu, wout, dx1, *cg.arrays)


def _ffn_tail(gc, a, x1, gate2, fnw, pf, wdown, wdown_t, tgt):
    f = _gelu(gc) * a
    ffn = _dnn_const(f, wdown, wdown_t) + pf
    y = _rms(x1 + gate2 * ffn, fnw)
    err = y - tgt
    loss = 0.5 * jnp.sum(jnp.mean(err * err, axis=-1, keepdims=True), axis=0, keepdims=True)
    return loss, f


def _ffn_fwd(x1, tgt, nw2, modx, w_a, w_g, cw, cb, wdown, wdown_t, fnw, name):
    L = x1.shape[0]
    nb = L // RF
    per = RF // HALO

    def body(x_ref, xp_ref, xn_ref, t_ref, nw_ref, mx_ref, wa_ref, wg_ref, cw_ref, cb_ref, wd_ref, wdt_ref, fn_ref,
             dx2_ref, da_ref, dgc_ref, f_ref, dffn_ref, loss_ref, dfn_ref, dg2_ref, dcb_ref, dcw_ref):
        i = pl.program_id(0)

        @pl.when(i == 0)
        def _():
            for r in (loss_ref, dfn_ref, dg2_ref, dcb_ref, dcw_ref):
                r[...] = jnp.zeros_like(r)

        nw, sh, sc, gate2 = nw_ref[...], mx_ref[3:4], mx_ref[4:5], mx_ref[5:6]
        x1b = x_ref[...]
        h2 = _mod(_rms(x1b, nw), sh, sc)
        h2e = jnp.concatenate([_mod(_rms(xp_ref[...], nw), sh, sc), h2, _mod(_rms(xn_ref[...], nw), sh, sc)], axis=0)
        a = dnn(h2, wa_ref[...])
        ge = dnn(h2e, wg_ref[...])
        g = ge[HALO:HALO + RF]
        gp = ge[HALO - 1:HALO] * jnp.where(i > 0, 1.0, 0.0)
        gn = ge[HALO + RF:HALO + RF + 1] * jnp.where(i < nb - 1, 1.0, 0.0)
        row = lax.broadcasted_iota(jnp.int32, (RF, 1), 0)
        g_prev = jnp.where(row == 0, gp, pltpu.roll(g, 1, axis=0))
        g_next = jnp.where(row == RF - 1, gn, pltpu.roll(g, RF - 1, axis=0))
        gc = cb_ref[...] + g_prev * cw_ref[0:1] + g * cw_ref[1:2] + g_next * cw_ref[2:3]
        fn = lambda gc_, a_, x_, g2_, fw_, pf_: _ffn_tail(gc_, a_, x_, g2_, fw_, pf_, wd_ref[...], wdt_ref[...], t_ref[...])
        loss, vjp, f = jax.vjp(fn, gc, a, x1b, gate2, fn_ref[...], jnp.zeros((RF, D), f32), has_aux=True)
        dgc, da, dx2, dg2, dfw, dffn = vjp(jnp.ones((1, 1), f32))
        dx2_ref[...] = dx2
        da_ref[...], dgc_ref[...] = da.astype(bf16), dgc
        f_ref[...], dffn_ref[...] = f.astype(bf16), dffn.astype(bf16)
        loss_ref[...] += jnp.broadcast_to(loss, (1, 128))
        dfn_ref[...] += dfw
        dg2_ref[...] += dg2
        dcb_ref[...] += jnp.sum(dgc, axis=0, keepdims=True)
        dcw_ref[0:1] += jnp.sum(dgc * g_prev, axis=0, keepdims=True)
        dcw_ref[1:2] += jnp.sum(dgc * g, axis=0, keepdims=True)
        dcw_ref[2:3] += jnp.sum(dgc * g_next, axis=0, keepdims=True)

    blk = lambda w: pl.BlockSpec((RF, w), lambda i: (i, 0))
    return pl.pallas_call(
        body, name=name, grid=(nb,),
        in_specs=[blk(D), pl.BlockSpec((HALO, D), lambda i: (jnp.maximum(i * per - 1, 0), 0)),
                  pl.BlockSpec((HALO, D), lambda i: (jnp.minimum((i + 1) * per, L // HALO - 1), 0)), blk(D),
                  _const_spec((1, D)), _const_spec((6, D)), _const_spec((D, DFF)), _const_spec((D, DFF)), _const_spec((3, DFF)),
                  _const_spec((1, DFF)), _const_spec((DFF, D)), _const_spec((D, DFF)), _const_spec((1, D))],
        out_specs=[blk(D), blk(DFF), blk(DFF), blk(DFF), blk(D), _acc_spec((1, 128)), _acc_spec((1, D)), _acc_spec((1, D)),
                   _acc_spec((1, DFF)), _acc_spec((3, DFF))],
        out_shape=[jax.ShapeDtypeStruct((L, D), f32), jax.ShapeDtypeStruct((L, DFF), bf16), jax.ShapeDtypeStruct((L, DFF), f32),
                   jax.ShapeDtypeStruct((L, DFF), bf16), jax.ShapeDtypeStruct((L, D), bf16), jax.ShapeDtypeStruct((1, 128), f32),
                   jax.ShapeDtypeStruct((1, D), f32), jax.ShapeDtypeStruct((1, D), f32), jax.ShapeDtypeStruct((1, DFF), f32),
                   jax.ShapeDtypeStruct((3, DFF), f32)],
        compiler_params=_CP(dimension_semantics=_ARB),
    )(x1, x1, x1, tgt, nw2, modx, w_a, w_g, cw, cb, wdown, wdown_t, fnw)


def _ffn_bwd(x1, dx2, da, dgc, nw2, modx, wup_t, cw, name):
    L = x1.shape[0]
    nb = L // RF
    per = RF // HALO

    def body(x_ref, dx2_ref, da_ref, dgc_ref, dgp_ref, dgn_ref, nw_ref, mx_ref, wu_ref, cw_ref,
             dx1_ref, dag_ref, h2_ref, dnw_ref, dmx_ref):
        i = pl.program_id(0)

        @pl.when(i == 0)
        def _():
            dnw_ref[...] = jnp.zeros_like(dnw_ref)
            dmx_ref[...] = jnp.zeros_like(dmx_ref)

        dgc_b = dgc_ref[...]
        before = dgp_ref[HALO - 1:HALO] * jnp.where(i > 0, 1.0, 0.0)
        after = dgn_ref[0:1] * jnp.where(i < nb - 1, 1.0, 0.0)
        row = lax.broadcasted_iota(jnp.int32, (RF, 1), 0)
        d_prev = jnp.where(row == 0, before, pltpu.roll(dgc_b, 1, axis=0))
        d_next = jnp.where(row == RF - 1, after, pltpu.roll(dgc_b, RF - 1, axis=0))
        dg = cw_ref[0:1] * d_next + cw_ref[1:2] * dgc_b + cw_ref[2:3] * d_prev
        dag = jnp.concatenate([da_ref[...], dg.astype(bf16)], axis=1)
        dag_ref[...] = dag
        dh2 = dnn(dag, wu_ref[...])
        h2, vjp = jax.vjp(lambda a, b, c, d: _mod(_rms(a, b), c, d), x_ref[...], nw_ref[...], mx_ref[3:4], mx_ref[4:5])
        dxa, dnw, dsh, dsc = vjp(dh2)
        h2_ref[...] = h2.astype(bf16)
        dx1_ref[...] = dx2_ref[...] + dxa
        dnw_ref[...] += dnw
        dmx_ref[3:4] += dsh
        dmx_ref[4:5] += dsc

    blk = lambda w: pl.BlockSpec((RF, w), lambda i: (i, 0))
    return pl.pallas_call(
        body, name=name, grid=(nb,),
        in_specs=[blk(D), blk(D), blk(DFF), blk(DFF), pl.BlockSpec((HALO, DFF), lambda i: (jnp.maximum(i * per - 1, 0), 0)),
                  pl.BlockSpec((HALO, DFF), lambda i: (jnp.minimum((i + 1) * per, L // HALO - 1), 0)),
                  _const_spec((1, D)), _const_spec((6, D)), _const_spec((2 * DFF, D)), _const_spec((3, DFF))],
        out_specs=[blk(D), blk(2 * DFF), blk(D), _acc_spec((1, D)), _acc_spec((6, D))],
        out_shape=[jax.ShapeDtypeStruct((L, D), f32), jax.ShapeDtypeStruct((L, 2 * DFF), bf16), jax.ShapeDtypeStruct((L, D), bf16),
                   jax.ShapeDtypeStruct((1, D), f32), jax.ShapeDtypeStruct((6, D), f32)],
        compiler_params=_CP(dimension_semantics=_ARB),
    )(x1, dx2, da, dgc, dgc, dgc, nw2, modx, wup_t, cw)


def _matmul_tn(a, b, name, cargo=None):
    k, m = a.shape
    n = b.shape[1]
    divs = lambda d: [c for c in range(d, 0, -128) if d % c == 0]
    _, tm, tn = min((m * (n // cn) + n * (m // cm), cm, cn) for cm in divs(m) for cn in divs(n) if cm * cn * 4 <= ACC_TILE_BYTES)
    tk = next(c for c in (512, 768, 256, 128) if k % c == 0)
    nk = k // tk
    grid = (m // tm, n // tn, nk)
    cg = _Cargo(cargo)

    def body(*refs):
        (a_ref, b_ref), (o_ref,), (acc,) = cg.split(refs, 2, 1, 1)
        cg.ride(refs, 2, 1, grid)
        q = pl.program_id(2)

        @pl.when(q == 0)
        def _():
            acc[...] = jnp.zeros_like(acc)

        acc[...] += dtn(a_ref[...], b_ref[...])

        @pl.when(q == nk - 1)
        def _():
            o_ref[...] = acc[...].astype(bf16)

    out = pl.pallas_call(
        body, name=name, grid=grid,
        in_specs=[pl.BlockSpec((tk, tm), lambda i, j, q: (q, i)), pl.BlockSpec((tk, tn), lambda i, j, q: (q, j))] + cg.in_specs(),
        out_specs=[pl.BlockSpec((tm, tn), lambda i, j, q: (i, j))] + cg.in_specs(),
        out_shape=[jax.ShapeDtypeStruct((m, n), bf16)] + cg.out_shapes(),
        scratch_shapes=[pltpu.VMEM((tm, tn), f32)] + cg.sems(),
        compiler_params=_CP(dimension_semantics=("arbitrary",) * 3 if cg.n else ("parallel", "parallel", "arbitrary")),
    )(a, b, *cg.arrays)
    return out if cg.n else out[0]


def _adamw_refs(w_ref, g_ref, m_ref, v_ref, d_ref, nm_ref, nv_ref):
    c1, c2 = 1.0 - B1 ** STEP, 1.0 - B2 ** STEP
    gg = g_ref[...]
    nm = B1 * m_ref[...] + (1.0 - B1) * gg
    nv = B2 * v_ref[...] + (1.0 - B2) * jnp.square(gg)
    d_ref[...] = -LR * ((nm / c1) / (jnp.sqrt(nv / c2) + AEPS) + WD * w_ref[...])
    nm_ref[...], nv_ref[...] = nm, nv


def _adamw(w, g, m, v, name):
    def body(*refs):
        _adamw_refs(*refs)

    return pl.pallas_call(body, name=name, out_shape=[jax.ShapeDtypeStruct(w.shape, f32)] * 3, compiler_params=_CP())(w, g, m, v)


def _adamw_landed(land, w, m, v, name):
    def body(l_ref, w_ref, m_ref, v_ref, g_ref, d_ref, nm_ref, nv_ref):
        acc = l_ref[0].astype(f32)
        for j in range(1, NDEV):
            acc = acc + l_ref[j].astype(f32)
        g_ref[...] = acc
        _adamw_refs(w_ref, g_ref, m_ref, v_ref, d_ref, nm_ref, nv_ref)

    return pl.pallas_call(body, name=name, out_shape=[jax.ShapeDtypeStruct(w.shape, f32)] * 4, compiler_params=_CP())(land, w, m, v)


def _adamw_many(ws, gs, ms, vs, name):
    n = len(ws)

    def body(*refs):
        for k in range(n):
            _adamw_refs(*[refs[j * n + k] for j in range(7)])

    outs = pl.pallas_call(body, name=name, out_shape=[jax.ShapeDtypeStruct(w.shape, f32) for w in ws] * 3,
                          compiler_params=_CP())(*ws, *gs, *ms, *vs)
    return outs[:n], outs[n:2 * n], outs[2 * n:]


SMALL = ["conv_w", "c_ctx", "norm1_w", "s5_lambda_re_f", "s5_lambda_im_f", "s5_log_step_f", "s5_lambda_re_b", "s5_lambda_im_b",
         "s5_log_step_b", "s5_b_re", "s5_b_im", "s5_c_re", "s5_c_im", "s5_d", "s5_b_glu", "ret_log_decay_f", "ret_log_decay_b",
         "norm2_w", "conv_b", "final_norm_w"]
WEIGHTS = ["c_ctx", "w_mod", "b_mod", "norm1_w", "w_in", "s5_lambda_re_f", "s5_lambda_im_f", "s5_log_step_f", "s5_lambda_re_b",
           "s5_lambda_im_b", "s5_log_step_b", "s5_b_re", "s5_b_im", "s5_c_re", "s5_c_im", "s5_d", "s5_w_glu", "s5_b_glu",
           "ret_log_decay_f", "ret_log_decay_b", "w_out", "norm2_w", "w_up", "conv_w", "conv_b", "w_down", "final_norm_w"]


def _pack_small(vals):
    flat, offs, o = [], [], 0
    for a in vals:
        n = a.size
        npad = -n % 128
        flat.append(jnp.pad(a.reshape(-1), (0, npad)))
        offs.append((o, n))
        o += n + npad
    tail = -o % 1024
    if tail:
        flat.append(jnp.zeros((tail,), f32))
    return jnp.concatenate(flat).reshape(-1, 128), offs


def _unpack_small(packed, offs, shapes):
    flat = packed.reshape(-1)
    return [flat[o:o + n].reshape(s) for (o, n), s in zip(offs, shapes)]


def _rope_tables(L, nctx_rows):
    t = np.arange(L)
    inv = (ROPE_THETA ** (-np.arange(DH // 4, dtype=np.float64) / (DH // 4))).astype(np.float32)
    ang = np.concatenate([(t // GRID_W).astype(np.float32)[:, None] * inv, (t % GRID_W).astype(np.float32)[:, None] * inv], axis=-1)
    cos = np.repeat(np.cos(ang).astype(np.float32), 2, axis=1)
    sin = np.repeat(np.sin(ang).astype(np.float32), 2, axis=1) * np.tile(np.array([-1.0, 1.0], np.float32), DH // 2)
    cosf = np.concatenate([np.ones((nctx_rows, DH), np.float32), cos], axis=0)
    sins = np.concatenate([np.zeros((nctx_rows, DH), np.float32), sin], axis=0)
    return jnp.asarray(cosf), jnp.asarray(sins)


def kernel(x, c, ctx, c_ctx, w_mod, b_mod, norm1_w, w_in, s5_lambda_re_f, s5_lambda_im_f, s5_log_step_f, s5_lambda_re_b, s5_lambda_im_b, s5_log_step_b, s5_b_re, s5_b_im, s5_c_re, s5_c_im, s5_d, s5_w_glu, s5_b_glu, ret_log_decay_f, ret_log_decay_b, w_out, norm2_w, w_up, conv_w, conv_b, w_down, final_norm_w, loss_target, m_c_ctx, m_w_mod, m_b_mod, m_norm1_w, m_w_in, m_s5_lambda_re_f, m_s5_lambda_im_f, m_s5_log_step_f, m_s5_lambda_re_b, m_s5_lambda_im_b, m_s5_log_step_b, m_s5_b_re, m_s5_b_im, m_s5_c_re, m_s5_c_im, m_s5_d, m_s5_w_glu, m_s5_b_glu, m_ret_log_decay_f, m_ret_log_decay_b, m_w_out, m_norm2_w, m_w_up, m_conv_w, m_conv_b, m_w_down, m_final_norm_w, v_c_ctx, v_w_mod, v_b_mod, v_norm1_w, v_w_in, v_s5_lambda_re_f, v_s5_lambda_im_f, v_s5_log_step_f, v_s5_lambda_re_b, v_s5_lambda_im_b, v_s5_log_step_b, v_s5_b_re, v_s5_b_im, v_s5_c_re, v_s5_c_im, v_s5_d, v_s5_w_glu, v_s5_b_glu, v_ret_log_decay_f, v_ret_log_decay_b, v_w_out, v_norm2_w, v_w_up, v_conv_w, v_conv_b, v_w_down, v_final_norm_w):
    args = dict(locals())
    W = {n: args[n] for n in WEIGHTS}
    M = {n: args["m_" + n] for n in WEIGHTS}
    V = {n: args["v_" + n] for n in WEIGHTS}
    me = _me()
    x2, ctx2, tgt = x[0], ctx[0], loss_target[0]
    L, Lc = x2.shape[0], ctx2.shape[0]
    assert Lc == R and L % R == 0 and L % GRID_W == 0
    nctx = Lc // T

    w_in_tl, w_up_tl = w_in[0].T.astype(bf16), w_up[0].T.astype(bf16)
    w_out_l, w_down_l, w_glu_l = w_out[0].astype(bf16), w_down[0].astype(bf16), s5_w_glu[0].astype(bf16)
    per_cv = conv_w.shape[2]
    conv_pad = jnp.pad(conv_w[0], ((0, 5), (0, 128 * 3 - per_cv)))
    w_in_g, c_g, conv_g = _gather_two_level([w_in_tl, jnp.pad(c, ((0, 7), (0, 0))), conv_pad], "gather_w_in")
    w_in_t = w_in_g.reshape(INC, D)
    conv_f = conv_g[:, :3, :per_cv].transpose(1, 0, 2).reshape(3, DFF)

    c9 = jnp.concatenate([c_g[:, 0, :], c_ctx[None], jnp.zeros((7, D), f32)], axis=0)
    w_mod_l = w_mod[0]
    ncol = w_mod_l.shape[1]
    m_part = _ada_fwd(c9, w_mod_l, "ada_fwd")
    m_all = _all_gather_small(m_part, "gather_mod").transpose(1, 0, 2).reshape(16, 6, D)
    modx, modc = _mod_select(m_all, b_mod.reshape(6, D), "mod_select")

    pair = lambda a, b: jnp.concatenate([a, b], axis=-1)
    bre_g, bim_g = s5_b_re[0].transpose(0, 2, 1), s5_b_im[0].transpose(0, 2, 1)
    cre_g, cim_g = s5_c_re[0], s5_c_im[0]
    shared = (pair(bre_g, bim_g), pair(bim_g, bre_g), pair(cre_g, cim_g), pair(cim_g, cre_g))
    s5p = {}
    for tag, lre, lim, ls in (("f", s5_lambda_re_f, s5_lambda_im_f, s5_log_step_f), ("b", s5_lambda_re_b, s5_lambda_im_b, s5_log_step_b)):
        s5p[tag] = (pair(lre[0], lre[0])[:, None, :], pair(lim[0], lim[0])[:, None, :], ls[0].reshape(S5G, 1, 1)) + shared
    m_f, mb_f, mc_f, a1_f, a2_f = _s5_build(s5p["f"], False, "s5_build_f")
    m_b, mb_b, mc_b, a1_b, a2_b = _s5_build(s5p["b"], True, "s5_build_b")
    a1_f, a2_f, a1_b, a2_b = (a.reshape(S5G, SB) for a in (a1_f, a2_f, a1_b, a2_b))

    nw1, nw2, fnw = norm1_w, norm2_w, final_norm_w[None]
    cosf, sins = _rope_tables(L, Lc)
    p_ext, w_out_g, w_glu_g = _f1_fwd(x2, ctx2, modx, modc, nw1, w_in_t.T, cosf, sins, "f1_fwd", cargo=([w_out_l, w_glu_l], False))
    nctx5 = Lc // TC
    u_g = _to_groups(p_ext[:, :S5W])
    s_f, s_b = _s5_inc(u_g, mb_f, mb_b, "s5_inc")
    hp_f, hp_b = _s5_carry(s_f, s_b, (a1_f, a2_f), (a1_b, a2_b), nctx5, "s5_carry")
    ys = _from_groups(_s5_out(u_g, m_f, m_b, hp_f, hp_b, mc_f, mc_b, "s5_out"))
    ld8 = lambda ld: jnp.pad(jnp.broadcast_to(ld[0][:, None], (RH, 128)), ((0, 8 - RH), (0, 0)))
    ldf8, ldb8 = ld8(ret_log_decay_f), ld8(ret_log_decay_b)
    of, ob, rp_f, rp_b, w_up_g = _ret_fwd(p_ext, ldf8, ldb8, nctx, "ret_fwd", cargo=([w_up_tl], False))
    w_out_f, w_glu_f = w_out_g.reshape(D, D), w_glu_g.reshape(S5W, S5W)
    x1, w_down_g = _mix_fwd(x2, ys, of, ob, p_ext, s5_d, s5_b_glu, modx, w_glu_f, w_out_f, "mix_fwd", cargo=([w_down_l], False))
    w_down_f = w_down_g.reshape(DFF, D)
    w_up_t = w_up_g.reshape(2 * DFF, D)

    (dx2, da, dgc, f_act, dffn, loss_acc, g_fnw, g_gate2, g_cb, g_cw) = _ffn_fwd(
        x1, tgt, nw2, modx, w_up_t[:DFF].T, w_up_t[DFF:].T, conv_f, conv_b, w_down_f, w_down_f.T, fnw, "ffn_fwd")
    dx1, dag, h2, g_nw2, dmx2 = _ffn_bwd(x1, dx2, da, dgc, nw2, modx, w_up_t, conv_f, "ffn_bwd")
    gw_down = _matmul_tn(f_act, dffn, "dw_down").reshape(NDEV, -1, D)
    gw_up_t = _matmul_tn(dag, h2, "dw_up").reshape(NDEV, -1, D)
    (dy_e, dud_e, do_e, dg_e, cat, dmix, s_act, dz, g_d, g_bglu, g_gate1, l_down) = _mix_bwd(
        x2, ys, of, ob, p_ext, s5_d, s5_b_glu, modx, w_glu_f, w_out_f, dx1, "mix_bwd", cargo=([gw_down], True))
    gw_out = _matmul_tn(cat, dmix, "dw_out").reshape(NDEV, -1, D)
    gw_glu = _matmul_tn(s_act, dz, "dw_glu").reshape(NDEV, -1, S5W)
    dq_f, dk_f, dv_f, dq_b, dk_b, dv_b, gld_f, gld_b, l_up, l_out, l_glu = _ret_bwd(
        p_ext, ldf8, ldb8, rp_f, rp_b, do_e, nctx, "ret_bwd", cargo=([gw_up_t, gw_out, gw_glu], True))

    du1, g_m, dhp_f, dhp_b, dmc_f, dmc_b = _s5_out_bwd(_to_groups(dy_e), u_g, m_f, m_b, hp_f, hp_b, mc_f, mc_b, "s5_out_bwd")
    ds_f, da1_f, da2_f = _s5_carry_bwd(dhp_f, hp_f, a1_f, a2_f, False, nctx5, "s5_carry_bwd_f")
    ds_b, da1_b, da2_b = _s5_carry_bwd(dhp_b, hp_b, a1_b, a2_b, True, nctx5, "s5_carry_bwd_b")
    du_g, dmb_f, dmb_b = _s5_inc_bwd(du1, u_g, ds_f, ds_b, mb_f, mb_b, "s5_inc_bwd")
    zero_p = jnp.zeros((S5G, S5P, SB), f32)
    gf = _s5_build_bwd(s5p["f"], (g_m, dmb_f, dmc_f, da1_f[:, None, :], da2_f[:, None, :]), (zero_p, zero_p), False, "s5_build_bwd_f")
    gb = _s5_build_bwd(s5p["b"], (g_m, dmb_b, dmc_b, da1_b[:, None, :], da2_b[:, None, :]), (gf[3], gf[4]), True, "s5_build_bwd_b")
    g_bre, g_bim = gb[3][:, :, :S5N].transpose(0, 2, 1), gb[3][:, :, S5N:].transpose(0, 2, 1)
    g_cre, g_cim = gb[4][:, :, :S5N], gb[4][:, :, S5N:]

    early = {
        "conv_w": g_cw, "s5_lambda_re_f": gf[0][:, 0, :S5N], "s5_lambda_im_f": gf[1][:, 0, :S5N],
        "s5_log_step_f": gf[2], "s5_lambda_re_b": gb[0][:, 0, :S5N], "s5_lambda_im_b": gb[1][:, 0, :S5N], "s5_log_step_b": gb[2],
        "s5_b_re": g_bre, "s5_b_im": g_bim, "s5_c_re": g_cre, "s5_c_im": g_cim, "s5_d": g_d, "s5_b_glu": g_bglu,
        "ret_log_decay_f": gld_f[:RH, 0], "ret_log_decay_b": gld_b[:RH, 0], "norm2_w": g_nw2, "conv_b": g_cb, "final_norm_w": g_fnw,
    }
    e_names = [n for n in SMALL if n in early]
    packed_e, eoffs = _pack_small([early[n].astype(f32) for n in e_names])
    grad_x, dp_ext, h1, g_nw1, dmx1, dmc1 = _f1_bwd(
        x2, ctx2, modx, modc, nw1, w_in_t, cosf, sins, dx1, (_from_groups(du_g), dud_e, dq_f, dq_b, dk_f, dk_b, dv_f, dv_b, dg_e), "f1_bwd")
    gw_in_t, land_e = _matmul_tn(dp_ext, h1, "dw_in", cargo=([packed_e], False))
    g_in_t = _reduce_scatter_two_level(gw_in_t.reshape(NDEV, -1, D), "scatter_dw_in")

    dmx = dmx1 + dmx2
    dmx = dmx.at[2].set(g_gate1[0]).at[5].set(g_gate2[0])
    dm_me = jnp.stack([dmx.reshape(-1), dmc1.reshape(-1)], axis=0)
    dm_all = _all_gather_small(dm_me.reshape(8, -1), "gather_dmod").reshape(NDEV, 2, 6 * D)
    dmx_all, dmc_all = dm_all[:, 0, :], dm_all[:, 1, :]
    my_cols = lambda a: lax.dynamic_slice(a, (0, me * ncol), (NDEV, ncol))
    gw_mod, g_bmod, dc9 = _ada_bwd(c9, dmx_all, dmc_all, my_cols(dmx_all), my_cols(dmc_all), w_mod_l, "ada_bwd")

    sshape = lambda n: (3, DFF) if n == "conv_w" else W[n].shape
    G = dict(zip(e_names, _unpack_small(_sum8(land_e, "reduce_early"), eoffs, [sshape(n) for n in e_names])))
    late = {"c_ctx": dc9[8], "norm1_w": g_nw1}
    packed_l, loffs = _pack_small([late[n].astype(f32) for n in late])
    G.update(zip(late, _unpack_small(_all_reduce_small(packed_l, "reduce_late"), loffs, [W[n].shape for n in late])))
    G["conv_w"] = lax.dynamic_slice(G["conv_w"], (0, me * per_cv), (3, per_cv))[None]
    G["b_mod"] = g_bmod.reshape(b_mod.shape)
    G["w_mod"] = gw_mod[None]
    G["w_in"] = g_in_t.T[None]
    G["w_up"] = _sum8(l_up, "sum_dw_up").T[None]

    delta, new_m, new_v = {}, {}, {}
    sm_names = SMALL[1:] + ["b_mod"]
    rows = lambda a: a.reshape(-1, a.shape[-1])
    outs = _adamw_many(*[[rows(d[n]) for n in sm_names] for d in (W, G, M, V)], "adamw_small")
    for dst, src in zip((delta, new_m, new_v), outs):
        dst.update({n: a.reshape(W[n].shape) for n, a in zip(sm_names, src)})
    for n in ["w_mod", "w_in", "w_up", "conv_w"]:
        d, nm, nv = _adamw(W[n][0], G[n][0], M[n][0], V[n][0], "adamw_" + n)
        delta[n], new_m[n], new_v[n] = d[None], nm[None], nv[None]
    for n, land in (("w_out", l_out), ("w_down", l_down), ("s5_w_glu", l_glu)):
        g, d, nm, nv = _adamw_landed(land, W[n][0], M[n][0], V[n][0], "adamw_" + n)
        G[n], delta[n], new_m[n], new_v[n] = g[None], d[None], nm[None], nv[None]

    loss = lax.psum(loss_acc[0, 0], ("x", "y", "c"))
    return (loss, grad_x[None], *[G[n] for n in WEIGHTS], *[delta[n] for n in WEIGHTS], *[new_m[n] for n in WEIGHTS],
            *[new_v[n] for n in WEIGHTS])
```

```python
import functools

import numpy as np
import jax
import jax.numpy as jnp
from jax import lax
from jax.experimental import pallas as pl
from jax.experimental.pallas import tpu as pltpu

f32, bf16 = jnp.float32, jnp.bfloat16

D = 1024
S5W, S5G, S5P, S5N = 512, 32, 16, 64
TC = 16
TCP = TC * S5P
SB = 2 * S5N
GBK = 8
UP_HEAD = 192
CARRY_UNROLL = 8
RH, DH = 4, 128
RW = RH * DH
INC = S5W + 4 * RW
DFF = 2816
T = 128
R = 256
RF = 128
HALO = 8
EPS = 1e-6
ROPE_THETA = 10000.0
GRID_W = 64
NDEV = 8
LR, B1, B2, AEPS, WD, STEP = 0.001, 0.9, 0.999, 1e-08, 0.01, 10
VMEM_LIMIT = 60 * 1024 * 1024
ACC_TILE_BYTES = 6 * 1024 * 1024
MESH = pl.DeviceIdType.MESH

_CP = functools.partial(pltpu.CompilerParams, vmem_limit_bytes=VMEM_LIMIT)
_ARB = ("arbitrary",)
_ANY = pl.BlockSpec(memory_space=pl.ANY)


def _dg(a, b, dims):
    return lax.dot_general(a.astype(bf16), b.astype(bf16), (dims, ((), ())), preferred_element_type=f32)


@jax.custom_vjp
def dnn(a, b):
    return _dg(a, b, ((1,), (0,)))


@jax.custom_vjp
def dnt(a, b):
    return _dg(a, b, ((1,), (1,)))


@jax.custom_vjp
def dtn(a, b):
    return _dg(a, b, ((0,), (0,)))


dnn.defvjp(lambda a, b: (dnn(a, b), (a, b)), lambda r, g: (dnt(g, r[1]).astype(r[0].dtype), dtn(r[0], g).astype(r[1].dtype)))
dnt.defvjp(lambda a, b: (dnt(a, b), (a, b)), lambda r, g: (dnn(g, r[1]).astype(r[0].dtype), dtn(g, r[0]).astype(r[1].dtype)))
dtn.defvjp(lambda a, b: (dtn(a, b), (a, b)), lambda r, g: (dnt(r[1], g).astype(r[0].dtype), dnn(r[0], g).astype(r[1].dtype)))


@jax.custom_vjp
def _dnn_const(a, w, wt):
    return dnn(a, w)


_dnn_const.defvjp(lambda a, w, wt: (dnn(a, w), wt), lambda wt, g: (dnn(g, wt), None, None))


_GELU_C0, _GELU_C1 = float(np.sqrt(2.0 / np.pi)), 0.044715


@jax.custom_vjp
def _gelu(x):
    return _gelu_fwd(x)[0]


def _gelu_fwd(x):
    t = jnp.tanh(_GELU_C0 * (x + _GELU_C1 * (x * x * x)))
    return x * (0.5 * (1.0 + t)), (x, t)


def _gelu_bwd(res, g):
    x, t = res
    return (g * (0.5 * (1.0 + t) + (0.5 * _GELU_C0) * x * (1.0 - t * t) * (1.0 + (3.0 * _GELU_C1) * (x * x))),)


_gelu.defvjp(_gelu_fwd, _gelu_bwd)


def _rms(t, w):
    return t * lax.rsqrt(jnp.mean(t * t, axis=-1, keepdims=True) + EPS) * w


def _mod(h, shift, scale):
    return h * (1.0 + scale) + shift


def _const_spec(shape):
    n = len(shape)
    return pl.BlockSpec(shape, lambda i, _n=n: (0,) * _n, pipeline_mode=pl.Buffered(1))


def _acc_spec(shape):
    n = len(shape)
    return pl.BlockSpec(shape, lambda i, _n=n: (0,) * _n)


def _me():
    return 4 * lax.axis_index("x") + 2 * lax.axis_index("y") + lax.axis_index("c")


def _peer(r):
    x, y, c = lax.axis_index("x"), lax.axis_index("y"), lax.axis_index("c")
    px = 1 - x if (r >> 2) & 1 else x
    py = 1 - y if (r >> 1) & 1 else y
    pc = 1 - c if r & 1 else c
    return (px, py, pc), 4 * px + 2 * py + pc


def _all_gather_small(v, name):
    r, c = v.shape

    def body(v_ref, out_ref, send_sems, recv_sems):
        me = _me()
        out_ref[me] = v_ref[...]
        sends = []
        for k in range(1, NDEV):
            peer, _ = _peer(k)
            cp = pltpu.make_async_remote_copy(src_ref=v_ref, dst_ref=out_ref.at[me], send_sem=send_sems.at[k - 1],
                                              recv_sem=recv_sems.at[k - 1], device_id=peer, device_id_type=MESH)
            cp.start()
            sends.append(cp)
        for k in range(1, NDEV):
            peer, pidx = _peer(k)
            pltpu.make_async_remote_copy(src_ref=v_ref, dst_ref=out_ref.at[pidx], send_sem=send_sems.at[k - 1],
                                         recv_sem=recv_sems.at[k - 1], device_id=peer, device_id_type=MESH).wait_recv()
        for cp in sends:
            cp.wait_send()

    return pl.pallas_call(
        body, name=name, out_shape=jax.ShapeDtypeStruct((NDEV, r, c), v.dtype),
        in_specs=[pl.BlockSpec(memory_space=pltpu.VMEM)], out_specs=pl.BlockSpec(memory_space=pltpu.VMEM),
        scratch_shapes=[pltpu.SemaphoreType.DMA((NDEV - 1,)), pltpu.SemaphoreType.DMA((NDEV - 1,))],
        compiler_params=_CP(),
    )(v)


def _all_reduce_small(v, name):
    r, c = v.shape

    def body(v_ref, out_ref, land, send_sems, recv_sems):
        me = _me()
        land[me] = v_ref[...]
        sends = []
        for k in range(1, NDEV):
            peer, _ = _peer(k)
            cp = pltpu.make_async_remote_copy(src_ref=v_ref, dst_ref=land.at[me], send_sem=send_sems.at[k - 1],
                                              recv_sem=recv_sems.at[k - 1], device_id=peer, device_id_type=MESH)
            cp.start()
            sends.append(cp)
        for k in range(1, NDEV):
            peer, pidx = _peer(k)
            pltpu.make_async_remote_copy(src_ref=v_ref, dst_ref=land.at[pidx], send_sem=send_sems.at[k - 1],
                                         recv_sem=recv_sems.at[k - 1], device_id=peer, device_id_type=MESH).wait_recv()
        for cp in sends:
            cp.wait_send()
        acc = land[0]
        for j in range(1, NDEV):
            acc = acc + land[j]
        out_ref[...] = acc

    return pl.pallas_call(
        body, name=name, out_shape=jax.ShapeDtypeStruct((r, c), v.dtype),
        in_specs=[pl.BlockSpec(memory_space=pltpu.VMEM)], out_specs=pl.BlockSpec(memory_space=pltpu.VMEM),
        scratch_shapes=[pltpu.VMEM((NDEV, r, c), v.dtype), pltpu.SemaphoreType.DMA((NDEV - 1,)),
                        pltpu.SemaphoreType.DMA((NDEV - 1,))],
        compiler_params=_CP(),
    )(v)


class _Exchange:
    def __init__(self, srcs, dsts, send_sems, recv_sems, local_sems, scatter):
        me = _me()
        n = len(srcs)
        self.sends, self.recvs, self.locals = [], [], []
        for a, (s, d) in enumerate(zip(srcs, dsts)):
            self.locals.append(pltpu.make_async_copy(s.at[me] if scatter else s, d.at[me], local_sems.at[a]))
        for k in range(1, NDEV):
            peer, pidx = _peer(k)
            for a, (s, d) in enumerate(zip(srcs, dsts)):
                src = s.at[pidx] if scatter else s
                sem = (k - 1) * n + a
                for dst, out in ((d.at[me], self.sends), (d.at[pidx], self.recvs)):
                    out.append(pltpu.make_async_remote_copy(src_ref=src, dst_ref=dst, send_sem=send_sems.at[sem],
                                                            recv_sem=recv_sems.at[sem], device_id=peer, device_id_type=MESH))

    def start(self):
        for cp in self.locals + self.sends:
            cp.start()

    def wait(self):
        for cp in self.recvs:
            cp.wait_recv()
        for cp in self.sends:
            cp.wait_send()
        for cp in self.locals:
            cp.wait()


def _exchange_shapes(arrays, scatter):
    return [jax.ShapeDtypeStruct(a.shape if scatter else (NDEV,) + a.shape, a.dtype) for a in arrays]


def _exchange_sems(n):
    return [pltpu.SemaphoreType.DMA(((NDEV - 1) * n,)), pltpu.SemaphoreType.DMA(((NDEV - 1) * n,)), pltpu.SemaphoreType.DMA((n,))]


def _exchange(arrays, scatter, name):
    n = len(arrays)

    def body(*refs):
        ex = _Exchange(refs[:n], refs[n:2 * n], *refs[2 * n:], scatter)
        ex.start()
        ex.wait()

    return pl.pallas_call(body, name=name, out_shape=_exchange_shapes(arrays, scatter), in_specs=[_ANY] * n,
                          out_specs=[_ANY] * n, scratch_shapes=_exchange_sems(n), compiler_params=_CP())(*arrays)


def _chips():
    x, y, c = lax.axis_index("x"), lax.axis_index("y"), lax.axis_index("c")
    return (x, y, c), (x, y, 1 - c), [(1 - x, y), (x, 1 - y), (1 - x, 1 - y)]


def _gather_two_level(arrays, name):
    n = len(arrays)

    def body(*refs):
        srcs, outs = refs[:n], refs[n:2 * n]
        send_sems, recv_sems = refs[2 * n:]
        me, sibling, chips = _chips()
        c = me[2]
        idx = lambda p: 4 * p[0] + 2 * p[1] + p[2]

        def copy(a, k, block, to, src=None):
            return pltpu.make_async_remote_copy(
                src_ref=outs[a].at[idx(block)] if src is None else src, dst_ref=outs[a].at[idx(block)],
                send_sem=send_sems.at[7 * a + k], recv_sem=recv_sems.at[7 * a + k], device_id=to, device_id_type=MESH)

        first, passed = [], []
        for a in range(n):
            outs[a][idx(me)] = srcs[a][...]
            first += [copy(a, 0, me, sibling, src=srcs[a])]
            first += [copy(a, 1 + j, me, (*chip, c), src=srcs[a]) for j, chip in enumerate(chips)]
        for cp in first:
            cp.start()
        for a in range(n):
            for j, chip in enumerate(chips):
                copy(a, 1 + j, (*chip, c), me).wait_recv()
                cp = copy(a, 4 + j, (*chip, c), sibling)
                cp.start()
                passed.append(cp)
        for a in range(n):
            copy(a, 0, sibling, me).wait_recv()
            for j, chip in enumerate(chips):
                copy(a, 4 + j, (*chip, 1 - c), me).wait_recv()
        for cp in first + passed:
            cp.wait_send()

    vm = pl.BlockSpec(memory_space=pltpu.VMEM)
    return pl.pallas_call(
        body, name=name, out_shape=[jax.ShapeDtypeStruct((NDEV,) + a.shape, a.dtype) for a in arrays],
        in_specs=[vm] * n, out_specs=[vm] * n,
        scratch_shapes=[pltpu.SemaphoreType.DMA((7 * n,)), pltpu.SemaphoreType.DMA((7 * n,))],
        compiler_params=_CP(),
    )(*arrays)


def _reduce_scatter_two_level(g, name):
    _, r, c = g.shape
    nchip = NDEV // 2

    def body(g_ref, o_ref, stage, part, land, d_send, d_recv, i_send, i_recv):
        me, sibling, chips = _chips()
        x, y, cc = me
        mine = 2 * x + y

        def blk(k, core):
            return 2 * k + core

        swaps = [pltpu.make_async_remote_copy(src_ref=g_ref.at[blk(k, 1 - cc)], dst_ref=stage.at[k], send_sem=d_send.at[k],
                                              recv_sem=d_recv.at[k], device_id=sibling, device_id_type=MESH) for k in range(nchip)]
        for cp in swaps:
            cp.start()
        for cp in swaps:
            cp.wait_recv()
        for k in range(nchip):
            part[k] = (g_ref[blk(k, cc)].astype(f32) + stage[k].astype(f32)).astype(bf16)
        sends = []
        for j, chip in enumerate(chips):
            kd = 2 * chip[0] + chip[1]
            cp = pltpu.make_async_remote_copy(src_ref=part.at[kd], dst_ref=land.at[mine], send_sem=i_send.at[j],
                                              recv_sem=i_recv.at[j], device_id=(*chip, cc), device_id_type=MESH)
            cp.start()
            sends.append(cp)
        land[mine] = part[mine]
        for j, chip in enumerate(chips):
            ks = 2 * chip[0] + chip[1]
            pltpu.make_async_remote_copy(src_ref=part.at[ks], dst_ref=land.at[ks], send_sem=i_send.at[j], recv_sem=i_recv.at[j],
                                         device_id=(*chip, cc), device_id_type=MESH).wait_recv()
        for cp in swaps + sends:
            cp.wait_send()
        acc = land[0].astype(f32)
        for k in range(1, nchip):
            acc = acc + land[k].astype(f32)
        o_ref[...] = acc

    vm = pl.BlockSpec(memory_space=pltpu.VMEM)
    return pl.pallas_call(
        body, name=name, out_shape=jax.ShapeDtypeStruct((r, c), f32), in_specs=[vm], out_specs=vm,
        scratch_shapes=[pltpu.VMEM((nchip, r, c), g.dtype)] * 3 + [pltpu.SemaphoreType.DMA((nchip,)), pltpu.SemaphoreType.DMA((nchip,)),
                                                                   pltpu.SemaphoreType.DMA((3,)), pltpu.SemaphoreType.DMA((3,))],
        compiler_params=_CP(),
    )(g)


class _Cargo:
    def __init__(self, cargo):
        self.arrays, self.scatter = cargo if cargo else ([], False)
        self.n = len(self.arrays)

    def in_specs(self):
        return [_ANY] * self.n

    def out_shapes(self):
        return _exchange_shapes(self.arrays, self.scatter)

    def sems(self):
        return _exchange_sems(self.n) if self.n else []

    def split(self, refs, n_in, n_out, n_scratch):
        n = self.n
        return refs[:n_in], refs[n_in + n:n_in + n + n_out], refs[n_in + 2 * n + n_out:n_in + 2 * n + n_out + n_scratch]

    def ride(self, refs, n_in, n_out, grid):
        if not self.n:
            return
        n = self.n
        ex = _Exchange(refs[n_in:n_in + n], refs[n_in + n + n_out:n_in + 2 * n + n_out], *refs[-3:], self.scatter)
        grid = (grid,) if isinstance(grid, int) else tuple(grid)
        first = functools.reduce(jnp.logical_and, [pl.program_id(a) == 0 for a in range(len(grid))])
        last = functools.reduce(jnp.logical_and, [pl.program_id(a) == g - 1 for a, g in enumerate(grid)])

        @pl.when(first)
        def _():
            ex.start()

        @pl.when(last)
        def _():
            ex.wait()


def _sum8(land, name):
    _, r, c = land.shape
    rb = next((b for b in (256, 64, 32) if r % b == 0), r)

    def body(l_ref, o_ref):
        acc = l_ref[0].astype(f32)
        for j in range(1, NDEV):
            acc = acc + l_ref[j].astype(f32)
        o_ref[...] = acc

    return pl.pallas_call(
        body, name=name, grid=(r // rb,), out_shape=jax.ShapeDtypeStruct((r, c), f32),
        in_specs=[pl.BlockSpec((NDEV, rb, c), lambda i: (0, i, 0))], out_specs=pl.BlockSpec((rb, c), lambda i: (i, 0)),
        compiler_params=_CP(dimension_semantics=("parallel",)),
    )(land)


def _ada_fwd(c9, w_mod_l, name):
    def body(c_ref, w_ref, o_ref):
        o_ref[...] = dnn(jax.nn.silu(c_ref[...]), w_ref[...])

    return pl.pallas_call(body, name=name, out_shape=jax.ShapeDtypeStruct((16, w_mod_l.shape[1]), f32),
                          compiler_params=_CP())(c9, w_mod_l)


def _mod_select(m_all, b_mod6, name):
    def body(m_ref, b_ref, mx_ref, mc_ref):
        me = _me()
        mx_ref[...] = m_ref[me] + b_ref[...]
        mc_ref[...] = m_ref[8] + b_ref[...]

    return pl.pallas_call(body, name=name, out_shape=[jax.ShapeDtypeStruct((6, D), f32)] * 2, compiler_params=_CP())(m_all, b_mod6)


def _ada_bwd(c9, dmx_all, dmc_all, dmx_l, dmc_l, w_mod_l, name):
    ncol = w_mod_l.shape[1]

    def rowsum(r):
        acc = r[0:1]
        for j in range(1, NDEV):
            acc = acc + r[j:j + 1]
        return acc

    def body(c_ref, xa_ref, ca_ref, xl_ref, cl_ref, w_ref, gw_ref, gb_ref, dc_ref):
        s9, vjp = jax.vjp(jax.nn.silu, c_ref[...])
        dm9 = jnp.concatenate([xl_ref[...], rowsum(cl_ref[...]), jnp.zeros((7, ncol), f32)], axis=0)
        gw_ref[...] = dtn(s9, dm9)
        gb_ref[...] = rowsum(xa_ref[...]) + rowsum(ca_ref[...])
        dc_ref[...] = vjp(dnt(dm9, w_ref[...]))[0]

    return pl.pallas_call(
        body, name=name,
        out_shape=[jax.ShapeDtypeStruct((D, ncol), f32), jax.ShapeDtypeStruct((1, 6 * D), f32), jax.ShapeDtypeStruct((16, D), f32)],
        compiler_params=_CP())(c9, dmx_all, dmc_all, dmx_l, dmc_l, w_mod_l)


def _lane_sign(rank):
    shape = (1,) * (rank - 1) + (SB,)
    return jnp.where(lax.broadcasted_iota(jnp.int32, shape, rank - 1) < S5N, -1.0, 1.0)


def _s5_build_fn(lre2, lim2, ls, bn, bs, cn, cs, rev):
    sg = _lane_sign(3)
    s = jnp.exp(ls)
    ar, ai = lre2 * s, lim2 * s
    e = jnp.exp(ar)
    nr, ni = e * jnp.cos(ai) - 1.0, e * jnp.sin(ai)
    den = lre2 * lre2 + lim2 * lim2
    cr, ci = (nr * lre2 + ni * lim2) / den, (ni * lre2 - nr * lim2) / den
    bbn = cr * bn + (ci * sg) * bs
    bbs = cr * bs - (ci * sg) * bn

    def powers(ex):
        m, ang = jnp.exp(ex * ar), ex * ai
        return m * jnp.cos(ang), m * jnp.sin(ang) * sg

    def times(tabs, xn, xs):
        f1, f2 = tabs
        return f1[:, :, None, :] * xn[:, None, :, :] + f2[:, :, None, :] * xs[:, None, :, :]

    t = lax.broadcasted_iota(jnp.int32, (1, TC, 1), 1).astype(f32)
    if rev:
        e_src, e_dst, e_out, e_in = t - (TC - 1.0), (TC - 1.0) - t, t, TC - t
    else:
        e_src, e_dst, e_out, e_in = -t, t, (TC - 1.0) - t, t + 1.0
    g = lre2.shape[0]
    flat = lambda a: a.reshape(g, TCP, SB)
    conj = -_lane_sign(4)
    ll = flat(times(powers(e_src), bbn, bbs))
    rr = flat(times(powers(e_dst), cn, cs) * conj)
    mb = flat(times(powers(e_out), bbn, bbs))
    mct = flat(times(powers(e_in), cn, cs) * conj)
    a1, a2 = powers(float(TC))
    row = lax.broadcasted_iota(jnp.int32, (TCP, TCP), 0) // S5P
    col = lax.broadcasted_iota(jnp.int32, (TCP, TCP), 1) // S5P
    mask = jnp.where((col <= row) if rev else (col >= row), 1.0, 0.0)
    m = jnp.concatenate([dnt(ll[j], rr[j])[None] for j in range(g)], axis=0) * mask
    return m, mb, mct, a1, a2


def _gspec(*tail):
    nt = len(tail)
    return pl.BlockSpec((GBK,) + tail, lambda i, _n=nt: (i,) + (0,) * _n)


def _s5_build(params, rev, name):
    def body(l1, l2, ls, bn, bs, cn, cs, m_ref, mb_ref, mc_ref, a1_ref, a2_ref):
        m, mb, mct, a1, a2 = _s5_build_fn(l1[...], l2[...], ls[...], bn[...], bs[...], cn[...], cs[...], rev)
        m_ref[...], mb_ref[...], mc_ref[...] = m.astype(bf16), mb.astype(bf16), mct.astype(bf16)
        a1_ref[...], a2_ref[...] = a1, a2

    vec, pm = _gspec(1, SB), _gspec(S5P, SB)
    return pl.pallas_call(
        body, name=name, grid=(S5G // GBK,),
        in_specs=[vec, vec, _gspec(1, 1), pm, pm, pm, pm],
        out_specs=[_gspec(TCP, TCP), _gspec(TCP, SB), _gspec(TCP, SB), vec, vec],
        out_shape=[jax.ShapeDtypeStruct((S5G, TCP, TCP), bf16), jax.ShapeDtypeStruct((S5G, TCP, SB), bf16),
                   jax.ShapeDtypeStruct((S5G, TCP, SB), bf16), jax.ShapeDtypeStruct((S5G, 1, SB), f32),
                   jax.ShapeDtypeStruct((S5G, 1, SB), f32)],
        compiler_params=_CP(dimension_semantics=("parallel",)),
    )(*params)


def _s5_build_bwd(params, cots, prev, rev, name):
    def body(l1, l2, ls, bn, bs, cn, cs, dm, dmb, dmc, da1, da2, pb, pc, gl1, gl2, gls, gb, gc):
        prim = (l1[...], l2[...], ls[...], bn[...], bs[...], cn[...], cs[...])
        _, vjp = jax.vjp(functools.partial(_s5_build_fn, rev=rev), *prim)
        d1, d2, dls, dbn, dbs, dcn, dcs = vjp((dm[...], dmb[...], dmc[...], da1[...], da2[...]))
        gl1[...] = d1 + pltpu.roll(d1, S5N, axis=2)
        gl2[...] = d2 + pltpu.roll(d2, S5N, axis=2)
        gls[...] = dls
        gb[...] = dbn + pltpu.roll(dbs, S5N, axis=2) + pb[...]
        gc[...] = dcn + pltpu.roll(dcs, S5N, axis=2) + pc[...]

    vec, pm, big = _gspec(1, SB), _gspec(S5P, SB), _gspec(TCP, SB)
    return pl.pallas_call(
        body, name=name, grid=(S5G // GBK,),
        in_specs=[vec, vec, _gspec(1, 1), pm, pm, pm, pm, _gspec(TCP, TCP), big, big, vec, vec, pm, pm],
        out_specs=[vec, vec, _gspec(1, 1), pm, pm],
        out_shape=[jax.ShapeDtypeStruct((S5G, 1, SB), f32), jax.ShapeDtypeStruct((S5G, 1, SB), f32),
                   jax.ShapeDtypeStruct((S5G, 1, 1), f32), jax.ShapeDtypeStruct((S5G, S5P, SB), f32),
                   jax.ShapeDtypeStruct((S5G, S5P, SB), f32)],
        compiler_params=_CP(dimension_semantics=("parallel",)),
    )(*params, *cots, *prev)


def _s5_inc(u, mb_f, mb_b, name):
    nc = u.shape[1]

    def body(u_ref, mf_ref, mb_ref, sf_ref, sb_ref):
        for j in range(GBK):
            sf_ref[:, j, :] = jnp.dot(u_ref[j], mf_ref[j], preferred_element_type=f32)
            sb_ref[:, j, :] = jnp.dot(u_ref[j], mb_ref[j], preferred_element_type=f32)

    sspec = pl.BlockSpec((nc, GBK, SB), lambda i: (0, i, 0))
    return pl.pallas_call(
        body, name=name, grid=(S5G // GBK,), in_specs=[_gspec(nc, TCP), _gspec(TCP, SB), _gspec(TCP, SB)],
        out_specs=[sspec, sspec], out_shape=[jax.ShapeDtypeStruct((nc, S5G, SB), f32)] * 2,
        compiler_params=_CP(dimension_semantics=("parallel",)),
    )(u, mb_f, mb_b)


def _idx_fwd(nctx, nch):
    return lambda i: i


def _idx_rev(nctx, nch):
    return lambda i: jnp.where(i < nctx, nctx - 1 - i, nch + nctx - 1 - i)


def _carry_loop(nc, step, init):
    def trip(i, c):
        for k in range(CARRY_UNROLL):
            c = step(i * CARRY_UNROLL + k, c)
        return c

    return lax.fori_loop(0, nc // CARRY_UNROLL, trip, init)


def _s5_carry(s_f, s_b, a_f, a_b, nctx, name):
    nc = s_f.shape[0]
    idx_b = _idx_rev(nctx, nc)

    def body(sf_ref, sb_ref, f1_ref, f2_ref, b1_ref, b2_ref, hf_ref, hb_ref):
        f1, f2, b1, b2 = f1_ref[...], f2_ref[...], b1_ref[...], b2_ref[...]

        def step(i, c):
            hf, hfs, hb, hbs = c
            rb = idx_b(i)
            hf_ref[i] = hf
            hb_ref[rb] = hb
            sf, sb = sf_ref[i], sb_ref[rb]
            return (f1 * hf + f2 * hfs + sf, f1 * hfs - f2 * hf + pltpu.roll(sf, S5N, axis=1),
                    b1 * hb + b2 * hbs + sb, b1 * hbs - b2 * hb + pltpu.roll(sb, S5N, axis=1))

        z = jnp.zeros((S5G, SB), f32)
        _carry_loop(nc, step, (z, z, z, z))

    return pl.pallas_call(body, name=name, out_shape=[jax.ShapeDtypeStruct(s_f.shape, f32)] * 2,
                          compiler_params=_CP())(s_f, s_b, *a_f, *a_b)


def _s5_carry_bwd(dhp, hp, a1, a2, rev, nctx, name):
    nc = hp.shape[0]
    idx = (_idx_rev if rev else _idx_fwd)(nctx, nc)

    def body(dhp_ref, hp_ref, a1_ref, a2_ref, ds_ref, d1_ref, d2_ref):
        f1, f2 = a1_ref[...], a2_ref[...]

        def step(k, carry):
            ab, abs_, d1, d2 = carry
            r = idx(nc - 1 - k)
            ds_ref[r] = ab
            h, dh = hp_ref[r], dhp_ref[r]
            return (dh + f1 * ab - f2 * abs_, pltpu.roll(dh, S5N, axis=1) + f1 * abs_ + f2 * ab,
                    d1 + ab * h, d2 + ab * pltpu.roll(h, S5N, axis=1))

        z = jnp.zeros((S5G, SB), f32)
        _, _, d1, d2 = _carry_loop(nc, step, (z, z, z, z))
        d1_ref[...], d2_ref[...] = d1, d2

    return pl.pallas_call(
        body, name=name,
        out_shape=[jax.ShapeDtypeStruct(hp.shape, f32), jax.ShapeDtypeStruct((S5G, SB), f32), jax.ShapeDtypeStruct((S5G, SB), f32)],
        compiler_params=_CP())(dhp, hp, a1, a2)


def _s5_out(u, m_f, m_b, hp_f, hp_b, mc_f, mc_b, name):
    nc = u.shape[1]

    def body(u_ref, mf_ref, mb_ref, hf_ref, hb_ref, cf_ref, cb_ref, y_ref):
        for j in range(GBK):
            uj = u_ref[j]
            y_ref[j] = (jnp.dot(uj, mf_ref[j], preferred_element_type=f32) + jnp.dot(uj, mb_ref[j], preferred_element_type=f32)
                        + dnt(hf_ref[:, j, :], cf_ref[j]) + dnt(hb_ref[:, j, :], cb_ref[j])).astype(bf16)

    sspec = pl.BlockSpec((nc, GBK, SB), lambda i: (0, i, 0))
    return pl.pallas_call(
        body, name=name, grid=(S5G // GBK,),
        in_specs=[_gspec(nc, TCP), _gspec(TCP, TCP), _gspec(TCP, TCP), sspec, sspec, _gspec(TCP, SB), _gspec(TCP, SB)],
        out_specs=_gspec(nc, TCP), out_shape=jax.ShapeDtypeStruct((S5G, nc, TCP), bf16),
        compiler_params=_CP(dimension_semantics=("parallel",)),
    )(u, m_f, m_b, hp_f, hp_b, mc_f, mc_b)


def _s5_out_bwd(dy, u, m_f, m_b, hp_f, hp_b, mc_f, mc_b, name):
    nc = u.shape[1]

    def body(dy_ref, u_ref, mf_ref, mb_ref, hf_ref, hb_ref, cf_ref, cb_ref, du_ref, g_ref, dhf_ref, dhb_ref, dcf_ref, dcb_ref):
        for j in range(GBK):
            dyj = dy_ref[j]
            du_ref[j] = dnt(dyj, mf_ref[j]) + dnt(dyj, mb_ref[j])
            g_ref[j] = dtn(u_ref[j], dyj)
            dhf_ref[:, j, :] = dnn(dyj, cf_ref[j])
            dhb_ref[:, j, :] = dnn(dyj, cb_ref[j])
            dcf_ref[j] = dtn(dyj, hf_ref[:, j, :])
            dcb_ref[j] = dtn(dyj, hb_ref[:, j, :])

    sspec = pl.BlockSpec((nc, GBK, SB), lambda i: (0, i, 0))
    sshape = jax.ShapeDtypeStruct((nc, S5G, SB), f32)
    cshape = jax.ShapeDtypeStruct((S5G, TCP, SB), f32)
    return pl.pallas_call(
        body, name=name, grid=(S5G // GBK,),
        in_specs=[_gspec(nc, TCP), _gspec(nc, TCP), _gspec(TCP, TCP), _gspec(TCP, TCP), sspec, sspec, _gspec(TCP, SB), _gspec(TCP, SB)],
        out_specs=[_gspec(nc, TCP), _gspec(TCP, TCP), sspec, sspec, _gspec(TCP, SB), _gspec(TCP, SB)],
        out_shape=[jax.ShapeDtypeStruct((S5G, nc, TCP), f32), jax.ShapeDtypeStruct((S5G, TCP, TCP), f32), sshape, sshape, cshape, cshape],
        compiler_params=_CP(dimension_semantics=("parallel",)),
    )(dy, u, m_f, m_b, hp_f, hp_b, mc_f, mc_b)


def _s5_inc_bwd(du1, u, ds_f, ds_b, mb_f, mb_b, name):
    nc = u.shape[1]

    def body(du1_ref, u_ref, dsf_ref, dsb_ref, mf_ref, mb_ref, du_ref, dmf_ref, dmb_ref):
        for j in range(GBK):
            dsf, dsb = dsf_ref[:, j, :], dsb_ref[:, j, :]
            du_ref[j] = (du1_ref[j] + dnt(dsf, mf_ref[j]) + dnt(dsb, mb_ref[j])).astype(bf16)
            dmf_ref[j] = dtn(u_ref[j], dsf)
            dmb_ref[j] = dtn(u_ref[j], dsb)

    sspec = pl.BlockSpec((nc, GBK, SB), lambda i: (0, i, 0))
    cshape = jax.ShapeDtypeStruct((S5G, TCP, SB), f32)
    return pl.pallas_call(
        body, name=name, grid=(S5G // GBK,),
        in_specs=[_gspec(nc, TCP), _gspec(nc, TCP), sspec, sspec, _gspec(TCP, SB), _gspec(TCP, SB)],
        out_specs=[_gspec(nc, TCP), _gspec(TCP, SB), _gspec(TCP, SB)],
        out_shape=[jax.ShapeDtypeStruct((S5G, nc, TCP), bf16), cshape, cshape],
        compiler_params=_CP(dimension_semantics=("parallel",)),
    )(du1, u, ds_f, ds_b, mb_f, mb_b)


def _to_groups(a):
    n = a.shape[0]
    return a.reshape(n // TC, TC, S5G, S5P).transpose(2, 0, 1, 3).reshape(S5G, n // TC, TCP)


def _from_groups(a):
    nc = a.shape[1]
    return a.reshape(S5G, nc, TC, S5P).transpose(1, 2, 0, 3).reshape(nc * TC, S5W)


def _swap_pairs(t):
    lane = lax.broadcasted_iota(jnp.int32, t.shape, 1)
    return jnp.where(lane % 2 == 0, pltpu.roll(t, DH - 1, axis=1), pltpu.roll(t, 1, axis=1))


def _rot(t, cosf, sins):
    return t * cosf + _swap_pairs(t) * sins


def _rot_t(d, cosf, sins):
    return d * cosf - _swap_pairs(d) * sins


def _ret_tables(ld, rev):
    pos = lax.broadcasted_iota(jnp.int32, (T, 1), 0).astype(f32)
    diff = pos - lax.broadcasted_iota(jnp.int32, (1, T), 1).astype(f32)
    if rev:
        keep, dist = diff < 0, jnp.maximum(-diff, 0.0)
        xi, zeta = jnp.exp(ld * (T - pos)), jnp.exp(ld * pos)
    else:
        keep, dist = diff >= 0, jnp.maximum(diff, 0.0)
        xi, zeta = jnp.exp(ld * (pos + 1.0)), jnp.exp(ld * (T - 1.0 - pos))
    return jnp.where(keep, jnp.exp(ld * dist), 0.0), xi, zeta, jnp.exp(ld * float(T))


def _ret_apply(qr, kr, v, rp, dm, xi, zeta, cdec):
    out = dnn(dnt(qr, kr) * dm, v) + dnn(qr * xi, rp)
    return out, cdec * rp + dtn(kr * zeta, v)


def _ret_chunk(qr, kr, v, rp, ld, rev):
    return _ret_apply(qr, kr, v, rp, *_ret_tables(ld, rev))


def _ret_fwd(p_ext, ld8_f, ld8_b, nctx, name, cargo=None):
    n = p_ext.shape[0]
    nch = n // T
    idx_b = _idx_rev(nctx, nch)
    cg = _Cargo(cargo)

    def body(*refs):
        ins, (of_ref, ob_ref, rpf_ref, rpb_ref), (rf_s, rb_s, dm_s, xz_s) = cg.split(refs, 8, 4, 4)
        qf, kf, vf, qb, kb, vb, ldf_ref, ldb_ref = ins
        cg.ride(refs, 8, 4, nch)

        @pl.when(pl.program_id(0) == 0)
        def _():
            rf_s[...] = jnp.zeros_like(rf_s)
            rb_s[...] = jnp.zeros_like(rb_s)
            for d, ld_ref in enumerate((ldf_ref, ldb_ref)):
                for h in range(RH):
                    dm, xi, zeta, cdec = _ret_tables(ld_ref[h:h + 1, 0:1], bool(d))
                    dm_s[d, h] = dm
                    xz_s[d, h, 0] = jnp.broadcast_to(xi, (T, DH))
                    xz_s[d, h, 1] = jnp.broadcast_to(zeta, (T, DH))
                    xz_s[d, h, 2] = jnp.broadcast_to(cdec, (T, DH))

        for h in range(RH):
            sl = slice(h * DH, (h + 1) * DH)
            for d, (q_ref, k_ref, v_ref, o_ref, rp_ref, r_s) in enumerate(((qf, kf, vf, of_ref, rpf_ref, rf_s),
                                                                            (qb, kb, vb, ob_ref, rpb_ref, rb_s))):
                rp = r_s[h]
                rp_ref[0, h] = rp
                out, rn = _ret_apply(q_ref[:, sl].astype(f32), k_ref[:, sl].astype(f32), v_ref[:, sl].astype(f32), rp,
                                     dm_s[d, h], xz_s[d, h, 0], xz_s[d, h, 1], xz_s[d, h, 2])
                r_s[h] = rn
                o_ref[:, sl] = out

    fcol = lambda cb: pl.BlockSpec((T, RW), lambda i, _c=cb: (i, _c))
    bcol = lambda cb: pl.BlockSpec((T, RW), lambda i, _c=cb: (idx_b(i), _c))
    rspec = pl.BlockSpec((1, RH, DH, DH), lambda i: (i, 0, 0, 0))
    oshape, rshape = jax.ShapeDtypeStruct((n, RW), f32), jax.ShapeDtypeStruct((nch, RH, DH, DH), f32)
    return pl.pallas_call(
        body, name=name, grid=(nch,),
        in_specs=[fcol(1), fcol(2), fcol(3), bcol(1), bcol(2), bcol(3), _const_spec((8, 128)), _const_spec((8, 128))] + cg.in_specs(),
        out_specs=[fcol(0), bcol(0), rspec, rspec] + cg.in_specs(),
        out_shape=[oshape, oshape, rshape, rshape] + cg.out_shapes(),
        scratch_shapes=[pltpu.VMEM((RH, DH, DH), f32)] * 2 + [pltpu.VMEM((2, RH, T, T), f32), pltpu.VMEM((2, RH, 3, T, DH), f32)] + cg.sems(),
        compiler_params=_CP(dimension_semantics=_ARB),
    )(p_ext, p_ext, p_ext, p_ext, p_ext, p_ext, ld8_f, ld8_b, *cg.arrays)


def _ret_bwd(p_ext, ld8_f, ld8_b, rp_f, rp_b, do_ext, nctx, name, cargo=None):
    n = p_ext.shape[0]
    nch = n // T
    idx_rev = _idx_rev(nctx, nch)
    idf = lambda j: nch - 1 - j
    idb = lambda j: idx_rev(nch - 1 - j)
    cg = _Cargo(cargo)

    def body(*refs):
        ins, outs, (drf_s, drb_s) = cg.split(refs, 12, 8, 2)
        qf, kf, vf, qb, kb, vb, ldf_ref, ldb_ref, rpf_ref, rpb_ref, dof_ref, dob_ref = ins
        dqf, dkf, dvf, dqb, dkb, dvb, dldf_ref, dldb_ref = outs
        cg.ride(refs, 12, 8, nch)

        @pl.when(pl.program_id(0) == 0)
        def _():
            for r in (drf_s, drb_s, dldf_ref, dldb_ref):
                r[...] = jnp.zeros_like(r)

        for h in range(RH):
            sl = slice(h * DH, (h + 1) * DH)
            for q_ref, k_ref, v_ref, ld_ref, rp_ref, do_ref, dq_ref, dk_ref, dv_ref, dld_ref, dr_s, rev in (
                    (qf, kf, vf, ldf_ref, rpf_ref, dof_ref, dqf, dkf, dvf, dldf_ref, drf_s, False),
                    (qb, kb, vb, ldb_ref, rpb_ref, dob_ref, dqb, dkb, dvb, dldb_ref, drb_s, True)):
                _, vjp = jax.vjp(functools.partial(_ret_chunk, rev=rev), q_ref[:, sl].astype(f32), k_ref[:, sl].astype(f32),
                                 v_ref[:, sl].astype(f32), rp_ref[0, h], ld_ref[h:h + 1, 0:1])
                dqr, dkr, dv, drp, dld = vjp((do_ref[:, sl], dr_s[h]))
                dr_s[h] = drp
                dq_ref[:, sl], dk_ref[:, sl], dv_ref[:, sl] = dqr, dkr, dv
                dld_ref[h:h + 1, :] += jnp.broadcast_to(dld, (1, 128))

    fcol = lambda cb: pl.BlockSpec((T, RW), lambda j, _c=cb: (idf(j), _c))
    bcol = lambda cb: pl.BlockSpec((T, RW), lambda j, _c=cb: (idb(j), _c))
    rspec = pl.BlockSpec((1, RH, DH, DH), lambda j: (nch - 1 - j, 0, 0, 0))
    oshape = jax.ShapeDtypeStruct((n, RW), f32)
    return pl.pallas_call(
        body, name=name, grid=(nch,),
        in_specs=[fcol(1), fcol(2), fcol(3), bcol(1), bcol(2), bcol(3), _const_spec((8, 128)), _const_spec((8, 128)), rspec, rspec,
                  fcol(0), bcol(0)] + cg.in_specs(),
        out_specs=[fcol(0), fcol(0), fcol(0), bcol(0), bcol(0), bcol(0), _acc_spec((8, 128)), _acc_spec((8, 128))] + cg.in_specs(),
        out_shape=[oshape] * 6 + [jax.ShapeDtypeStruct((8, 128), f32)] * 2 + cg.out_shapes(),
        scratch_shapes=[pltpu.VMEM((RH, DH, DH), f32)] * 2 + cg.sems(),
        compiler_params=_CP(dimension_semantics=_ARB),
    )(p_ext, p_ext, p_ext, p_ext, p_ext, p_ext, ld8_f, ld8_b, rp_f, rp_b, do_ext, do_ext, *cg.arrays)


def _qk_heads(p, fn_q, fn_k):
    heads = lambda base, fn: [fn(p[:, base + h * DH:base + (h + 1) * DH]) for h in range(RH)]
    return jnp.concatenate([p[:, :S5W]] + heads(S5W, fn_q) + heads(S5W + RW, fn_k) + [p[:, S5W + 2 * RW:]], axis=1)


def _f1_fwd(x, ctx, modx, modc, nw1, w_in_n, cosf, sins, name, cargo=None):
    L = x.shape[0]
    nb = L // R + 1
    scale = DH ** -0.5
    cg = _Cargo(cargo)

    def body(*refs):
        (x_ref, c_ref, mx_ref, mc_ref, nw_ref, w_ref, cos_ref, sin_ref), (p_ref,), _ = cg.split(refs, 8, 1, 0)
        cg.ride(refs, 8, 1, nb)
        is_ctx = pl.program_id(0) == 0
        xin = jnp.where(is_ctx, c_ref[...], x_ref[...])
        sh = jnp.where(is_ctx, mc_ref[0:1], mx_ref[0:1])
        sc = jnp.where(is_ctx, mc_ref[1:2], mx_ref[1:2])
        cf, ss = cos_ref[...], sin_ref[...]
        p = dnn(_mod(_rms(xin, nw_ref[...]), sh, sc), w_ref[...])
        p_ref[...] = _qk_heads(p, lambda t: _rot(t, cf, ss), lambda t: _rot(t * scale, cf, ss)).astype(bf16)

    return pl.pallas_call(
        body, name=name, grid=(nb,),
        in_specs=[pl.BlockSpec((R, D), lambda i: (jnp.maximum(i - 1, 0), 0)), _const_spec((R, D)), _const_spec((6, D)),
                  _const_spec((6, D)), _const_spec((1, D)), _const_spec((D, INC)), pl.BlockSpec((R, DH), lambda i: (i, 0)),
                  pl.BlockSpec((R, DH), lambda i: (i, 0))] + cg.in_specs(),
        out_specs=[pl.BlockSpec((R, INC), lambda i: (i, 0))] + cg.in_specs(),
        out_shape=[jax.ShapeDtypeStruct((L + R, INC), bf16)] + cg.out_shapes(),
        scratch_shapes=cg.sems(),
        compiler_params=_CP(dimension_semantics=_ARB),
    )(x, ctx, modx, modc, nw1, w_in_n, cosf, sins, *cg.arrays)


def _f1_bwd(x, ctx, modx, modc, nw1, w_in_t, cosf, sins, dx1, parts, name, cargo=None):
    L = x.shape[0]
    nb = L // R + 1
    scale = DH ** -0.5
    cg = _Cargo(cargo)

    def body(*refs):
        ins, (gx_ref, dp_ref, h1_ref, dnw_ref, dmx_ref, dmc_ref), _ = cg.split(refs, 18, 6, 0)
        x_ref, c_ref, mx_ref, mc_ref, nw_ref, w_ref, cos_ref, sin_ref, dx1_ref, du0, du1, dq0, dq1, dk0, dk1, dv0, dv1, dg0 = ins
        cg.ride(refs, 18, 6, nb)
        i = pl.program_id(0)
        is_ctx = i == 0

        @pl.when(is_ctx)
        def _():
            dnw_ref[...] = jnp.zeros_like(dnw_ref)
            dmx_ref[...] = jnp.zeros_like(dmx_ref)
            dmc_ref[...] = jnp.zeros_like(dmc_ref)

        cf, ss = cos_ref[...], sin_ref[...]
        dp = jnp.concatenate([du0[...].astype(f32) + du1[...], dq0[...] + dq1[...], dk0[...] + dk1[...], dv0[...] + dv1[...],
                              dg0[...]], axis=1)
        dp = _qk_heads(dp, lambda t: _rot_t(t, cf, ss), lambda t: _rot_t(t, cf, ss) * scale).astype(bf16)
        dp_ref[...] = dp
        xin = jnp.where(is_ctx, c_ref[...], x_ref[...])
        sh = jnp.where(is_ctx, mc_ref[0:1], mx_ref[0:1])
        sc = jnp.where(is_ctx, mc_ref[1:2], mx_ref[1:2])
        dh = dnn(dp, w_ref[...])
        h, vjp = jax.vjp(lambda a, b, c, d: _mod(_rms(a, b), c, d), xin, nw_ref[...], sh, sc)
        dxin, dnw, dsh, dsc = vjp(dh)
        h1_ref[...] = h.astype(bf16)
        gx_ref[...] = dx1_ref[...] + dxin
        dnw_ref[...] += dnw
        wx = jnp.where(is_ctx, 0.0, 1.0)
        dmx_ref[0:1] += dsh * wx
        dmx_ref[1:2] += dsc * wx
        dmc_ref[0:1] += dsh * (1.0 - wx)
        dmc_ref[1:2] += dsc * (1.0 - wx)

    lat = pl.BlockSpec((R, D), lambda i: (jnp.maximum(i - 1, 0), 0))
    ext = pl.BlockSpec((R, S5W), lambda i: (i, 0))
    return pl.pallas_call(
        body, name=name, grid=(nb,),
        in_specs=[lat, _const_spec((R, D)), _const_spec((6, D)), _const_spec((6, D)), _const_spec((1, D)), _const_spec((INC, D)),
                  pl.BlockSpec((R, DH), lambda i: (i, 0)), pl.BlockSpec((R, DH), lambda i: (i, 0)), lat] + [ext] * 9 + cg.in_specs(),
        out_specs=[lat, pl.BlockSpec((R, INC), lambda i: (i, 0)), pl.BlockSpec((R, D), lambda i: (i, 0)),
                   _acc_spec((1, D)), _acc_spec((6, D)), _acc_spec((6, D))] + cg.in_specs(),
        out_shape=[jax.ShapeDtypeStruct((L, D), f32), jax.ShapeDtypeStruct((L + R, INC), bf16),
                   jax.ShapeDtypeStruct((L + R, D), bf16), jax.ShapeDtypeStruct((1, D), f32),
                   jax.ShapeDtypeStruct((6, D), f32), jax.ShapeDtypeStruct((6, D), f32)] + cg.out_shapes(),
        scratch_shapes=cg.sems(),
        compiler_params=_CP(dimension_semantics=_ARB),
    )(x, ctx, modx, modc, nw1, w_in_t, cosf, sins, dx1, *parts, *cg.arrays)


def _ret_post(yr, g):
    outs = []
    for h in range(RH):
        yh = yr[:, h * DH:(h + 1) * DH]
        mu = jnp.mean(yh, axis=-1, keepdims=True)
        var = jnp.mean((yh - mu) ** 2, axis=-1, keepdims=True)
        outs.append((yh - mu) * lax.rsqrt(var + EPS))
    return jax.nn.silu(g) * jnp.concatenate(outs, axis=1)


def _mix_fn(ys, u, of, ob, g, x, dvec, bglu, gate1, pz, pm, wglu, wout):
    s = _gelu(ys + dvec * u)
    z = dnn(s, wglu) + bglu + pz
    cat = jnp.concatenate([s * jax.nn.sigmoid(z), _ret_post(of + ob, g)], axis=1)
    mix = dnn(cat, wout) + pm
    return x + gate1 * mix, (s, cat)


def _mix_fwd(x, ys, of, ob, p_ext, dvec, bglu, modx, wglu, wout, name, cargo=None):
    L = x.shape[0]
    nb = L // R
    cg = _Cargo(cargo)

    def body(*refs):
        ins, (x1_ref,), _ = cg.split(refs, 11, 1, 0)
        x_ref, ys_ref, of_ref, ob_ref, u_ref, g_ref, d_ref, b_ref, mx_ref, wg_ref, wo_ref = ins
        cg.ride(refs, 11, 1, nb)
        x1_ref[...] = _mix_fn(ys_ref[...].astype(f32), u_ref[...].astype(f32), of_ref[...], ob_ref[...], g_ref[...].astype(f32),
                              x_ref[...], d_ref[...], b_ref[...], mx_ref[2:3], 0.0, 0.0, wg_ref[...], wo_ref[...])[0]

    ext = pl.BlockSpec((R, S5W), lambda i: (i + 1, 0))
    return pl.pallas_call(
        body, name=name, grid=(nb,),
        in_specs=[pl.BlockSpec((R, D), lambda i: (i, 0)), ext, ext, ext, ext, pl.BlockSpec((R, RW), lambda i: (i + 1, 4)),
                  _const_spec((1, S5W)), _const_spec((1, S5W)), _const_spec((6, D)), _const_spec((S5W, S5W)), _const_spec((D, D))]
        + cg.in_specs(),
        out_specs=[pl.BlockSpec((R, D), lambda i: (i, 0))] + cg.in_specs(),
        out_shape=[jax.ShapeDtypeStruct((L, D), f32)] + cg.out_shapes(),
        scratch_shapes=cg.sems(),
        compiler_params=_CP(dimension_semantics=_ARB),
    )(x, ys, of, ob, p_ext, p_ext, dvec, bglu, modx, wglu, wout, *cg.arrays)


def _mix_bwd(x, ys, of, ob, p_ext, dvec, bglu, modx, wglu, wout, dx1, name, cargo=None):
    L = x.shape[0]
    nb = L // R + 1
    cg = _Cargo(cargo)

    def body(*refs):
        ins, outs, _ = cg.split(refs, 12, 11, 0)
        x_ref, ys_ref, of_ref, ob_ref, u_ref, g_ref, d_ref, b_ref, mx_ref, wg_ref, wo_ref, dx1_ref = ins
        dy_ref, dud_ref, do_ref, dg_ref, cat_ref, dmix_ref, s_ref, dz_ref, dd_ref, db_ref, dg1_ref = outs
        cg.ride(refs, 12, 11, nb)
        i = pl.program_id(0)

        @pl.when(i == 0)
        def _():
            for r in outs:
                r[...] = jnp.zeros_like(r)

        @pl.when(i > 0)
        def _():
            fn = lambda ys_, u_, of_, g_, d_, b_, g1_, pz_, pm_: _mix_fn(
                ys_, u_, of_, ob_ref[...], g_, x_ref[...], d_, b_, g1_, pz_, pm_, wg_ref[...], wo_ref[...])
            _, vjp, (s, cat) = jax.vjp(fn, ys_ref[...].astype(f32), u_ref[...].astype(f32), of_ref[...], g_ref[...].astype(f32), d_ref[...],
                                       b_ref[...], mx_ref[2:3], jnp.zeros((R, S5W), f32), jnp.zeros((R, D), f32), has_aux=True)
            dy, dud, do, dg, dd, db, dg1, dz, dmix = vjp(dx1_ref[...])
            dy_ref[...], dud_ref[...], do_ref[...], dg_ref[...] = dy.astype(bf16), dud, do, dg
            cat_ref[...], dmix_ref[...] = cat.astype(bf16), dmix.astype(bf16)
            s_ref[...], dz_ref[...] = s.astype(bf16), dz.astype(bf16)
            dd_ref[...] += dd
            db_ref[...] += db
            dg1_ref[...] += dg1

    lat = pl.BlockSpec((R, D), lambda i: (jnp.maximum(i - 1, 0), 0))
    lat5 = pl.BlockSpec((R, S5W), lambda i: (jnp.maximum(i - 1, 0), 0))
    ext = pl.BlockSpec((R, S5W), lambda i: (i, 0))
    eshape = jax.ShapeDtypeStruct((L + R, S5W), f32)
    return pl.pallas_call(
        body, name=name, grid=(nb,),
        in_specs=[lat, ext, ext, ext, ext, pl.BlockSpec((R, RW), lambda i: (i, 4)),
                  _const_spec((1, S5W)), _const_spec((1, S5W)), _const_spec((6, D)), _const_spec((S5W, S5W)), _const_spec((D, D)), lat]
        + cg.in_specs(),
        out_specs=[ext, ext, ext, ext, lat, lat, lat5, lat5, _acc_spec((1, S5W)), _acc_spec((1, S5W)), _acc_spec((1, D))]
        + cg.in_specs(),
        out_shape=[jax.ShapeDtypeStruct((L + R, S5W), bf16), eshape, eshape, eshape, jax.ShapeDtypeStruct((L, D), bf16),
                   jax.ShapeDtypeStruct((L, D), bf16), jax.ShapeDtypeStruct((L, S5W), bf16), jax.ShapeDtypeStruct((L, S5W), bf16),
                   jax.ShapeDtypeStruct((1, S5W), f32), jax.ShapeDtypeStruct((1, S5W), f32), jax.ShapeDtypeStruct((1, D), f32)]
        + cg.out_shapes(),
        scratch_shapes=cg.sems(),
        compiler_params=_CP(dimension_semantics=_ARB),
    )(x, ys, of, ob, p_ext, p_ext, dvec, bglu, modx, wglu, wout, dx1, *cg.arrays)


def _ffn_tail(gc, a, x1, gate2, fnw, pf, wdown, wdown_t, tgt):
    f = _gelu(gc) * a
    ffn = _dnn_const(f, wdown, wdown_t) + pf
    y = _rms(x1 + gate2 * ffn, fnw)
    err = y - tgt
    loss = 0.5 * jnp.sum(jnp.mean(err * err, axis=-1, keepdims=True), axis=0, keepdims=True)
    return loss, f


def _ffn_fwd(x1, tgt, nw2, modx, w_a, w_g, cw, cb, wdown, wdown_t, fnw, name):
    L = x1.shape[0]
    nb = L // RF
    per = RF // HALO

    def body(x_ref, xp_ref, xn_ref, t_ref, nw_ref, mx_ref, wa_ref, wg_ref, cw_ref, cb_ref, wd_ref, wdt_ref, fn_ref,
             dx2_ref, da_ref, dgc_ref, f_ref, dffn_ref, loss_ref, dfn_ref, dg2_ref, dcb_ref, dcw_ref):
        i = pl.program_id(0)

        @pl.when(i == 0)
        def _():
            for r in (loss_ref, dfn_ref, dg2_ref, dcb_ref, dcw_ref):
                r[...] = jnp.zeros_like(r)

        nw, sh, sc, gate2 = nw_ref[...], mx_ref[3:4], mx_ref[4:5], mx_ref[5:6]
        x1b = x_ref[...]
        h2 = _mod(_rms(x1b, nw), sh, sc)
        h2e = jnp.concatenate([_mod(_rms(xp_ref[...], nw), sh, sc), h2, _mod(_rms(xn_ref[...], nw), sh, sc)], axis=0)
        a = dnn(h2, wa_ref[...])
        ge = dnn(h2e, wg_ref[...])
        g = ge[HALO:HALO + RF]
        gp = ge[HALO - 1:HALO] * jnp.where(i > 0, 1.0, 0.0)
        gn = ge[HALO + RF:HALO + RF + 1] * jnp.where(i < nb - 1, 1.0, 0.0)
        row = lax.broadcasted_iota(jnp.int32, (RF, 1), 0)
        g_prev = jnp.where(row == 0, gp, pltpu.roll(g, 1, axis=0))
        g_next = jnp.where(row == RF - 1, gn, pltpu.roll(g, RF - 1, axis=0))
        gc = cb_ref[...] + g_prev * cw_ref[0:1] + g * cw_ref[1:2] + g_next * cw_ref[2:3]
        fn = lambda gc_, a_, x_, g2_, fw_, pf_: _ffn_tail(gc_, a_, x_, g2_, fw_, pf_, wd_ref[...], wdt_ref[...], t_ref[...])
        loss, vjp, f = jax.vjp(fn, gc, a, x1b, gate2, fn_ref[...], jnp.zeros((RF, D), f32), has_aux=True)
        dgc, da, dx2, dg2, dfw, dffn = vjp(jnp.ones((1, 1), f32))
        dx2_ref[...] = dx2
        da_ref[...], dgc_ref[...] = da.astype(bf16), dgc
        f_ref[...], dffn_ref[...] = f.astype(bf16), dffn.astype(bf16)
        loss_ref[...] += jnp.broadcast_to(loss, (1, 128))
        dfn_ref[...] += dfw
        dg2_ref[...] += dg2
        dcb_ref[...] += jnp.sum(dgc, axis=0, keepdims=True)
        dcw_ref[0:1] += jnp.sum(dgc * g_prev, axis=0, keepdims=True)
        dcw_ref[1:2] += jnp.sum(dgc * g, axis=0, keepdims=True)
        dcw_ref[2:3] += jnp.sum(dgc * g_next, axis=0, keepdims=True)

    blk = lambda w: pl.BlockSpec((RF, w), lambda i: (i, 0))
    return pl.pallas_call(
        body, name=name, grid=(nb,),
        in_specs=[blk(D), pl.BlockSpec((HALO, D), lambda i: (jnp.maximum(i * per - 1, 0), 0)),
                  pl.BlockSpec((HALO, D), lambda i: (jnp.minimum((i + 1) * per, L // HALO - 1), 0)), blk(D),
                  _const_spec((1, D)), _const_spec((6, D)), _const_spec((D, DFF)), _const_spec((D, DFF)), _const_spec((3, DFF)),
                  _const_spec((1, DFF)), _const_spec((DFF, D)), _const_spec((D, DFF)), _const_spec((1, D))],
        out_specs=[blk(D), blk(DFF), blk(DFF), blk(DFF), blk(D), _acc_spec((1, 128)), _acc_spec((1, D)), _acc_spec((1, D)),
                   _acc_spec((1, DFF)), _acc_spec((3, DFF))],
        out_shape=[jax.ShapeDtypeStruct((L, D), f32), jax.ShapeDtypeStruct((L, DFF), bf16), jax.ShapeDtypeStruct((L, DFF), f32),
                   jax.ShapeDtypeStruct((L, DFF), bf16), jax.ShapeDtypeStruct((L, D), bf16), jax.ShapeDtypeStruct((1, 128), f32),
                   jax.ShapeDtypeStruct((1, D), f32), jax.ShapeDtypeStruct((1, D), f32), jax.ShapeDtypeStruct((1, DFF), f32),
                   jax.ShapeDtypeStruct((3, DFF), f32)],
        compiler_params=_CP(dimension_semantics=_ARB),
    )(x1, x1, x1, tgt, nw2, modx, w_a, w_g, cw, cb, wdown, wdown_t, fnw)


def _ffn_bwd(x1, dx2, da, dgc, nw2, modx, wup_t, cw, name):
    L = x1.shape[0]
    nb = L // RF
    per = RF // HALO

    def body(x_ref, dx2_ref, da_ref, dgc_ref, dgp_ref, dgn_ref, nw_ref, mx_ref, wu_ref, cw_ref,
             dx1_ref, dag_ref, h2_ref, dnw_ref, dmx_ref):
        i = pl.program_id(0)

        @pl.when(i == 0)
        def _():
            dnw_ref[...] = jnp.zeros_like(dnw_ref)
            dmx_ref[...] = jnp.zeros_like(dmx_ref)

        dgc_b = dgc_ref[...]
        before = dgp_ref[HALO - 1:HALO] * jnp.where(i > 0, 1.0, 0.0)
        after = dgn_ref[0:1] * jnp.where(i < nb - 1, 1.0, 0.0)
        row = lax.broadcasted_iota(jnp.int32, (RF, 1), 0)
        d_prev = jnp.where(row == 0, before, pltpu.roll(dgc_b, 1, axis=0))
        d_next = jnp.where(row == RF - 1, after, pltpu.roll(dgc_b, RF - 1, axis=0))
        dg = cw_ref[0:1] * d_next + cw_ref[1:2] * dgc_b + cw_ref[2:3] * d_prev
        dag = jnp.concatenate([da_ref[...], dg.astype(bf16)], axis=1)
        dag_ref[...] = dag
        dh2 = dnn(dag, wu_ref[...])
        h2, vjp = jax.vjp(lambda a, b, c, d: _mod(_rms(a, b), c, d), x_ref[...], nw_ref[...], mx_ref[3:4], mx_ref[4:5])
        dxa, dnw, dsh, dsc = vjp(dh2)
        h2_ref[...] = h2.astype(bf16)
        dx1_ref[...] = dx2_ref[...] + dxa
        dnw_ref[...] += dnw
        dmx_ref[3:4] += dsh
        dmx_ref[4:5] += dsc

    blk = lambda w: pl.BlockSpec((RF, w), lambda i: (i, 0))
    return pl.pallas_call(
        body, name=name, grid=(nb,),
        in_specs=[blk(D), blk(D), blk(DFF), blk(DFF), pl.BlockSpec((HALO, DFF), lambda i: (jnp.maximum(i * per - 1, 0), 0)),
                  pl.BlockSpec((HALO, DFF), lambda i: (jnp.minimum((i + 1) * per, L // HALO - 1), 0)),
                  _const_spec((1, D)), _const_spec((6, D)), _const_spec((2 * DFF, D)), _const_spec((3, DFF))],
        out_specs=[blk(D), blk(2 * DFF), blk(D), _acc_spec((1, D)), _acc_spec((6, D))],
        out_shape=[jax.ShapeDtypeStruct((L, D), f32), jax.ShapeDtypeStruct((L, 2 * DFF), bf16), jax.ShapeDtypeStruct((L, D), bf16),
                   jax.ShapeDtypeStruct((1, D), f32), jax.ShapeDtypeStruct((6, D), f32)],
        compiler_params=_CP(dimension_semantics=_ARB),
    )(x1, dx2, da, dgc, dgc, dgc, nw2, modx, wup_t, cw)


def _matmul_tn(a, b, name, cargo=None):
    k, m = a.shape
    n = b.shape[1]
    divs = lambda d: [c for c in range(d, 0, -128) if d % c == 0]
    _, tm, tn = min((m * (n // cn) + n * (m // cm), cm, cn) for cm in divs(m) for cn in divs(n) if cm * cn * 4 <= ACC_TILE_BYTES)
    tk = next(c for c in (512, 768, 256, 128) if k % c == 0)
    nk = k // tk
    grid = (m // tm, n // tn, nk)
    cg = _Cargo(cargo)

    def body(*refs):
        (a_ref, b_ref), (o_ref,), (acc,) = cg.split(refs, 2, 1, 1)
        cg.ride(refs, 2, 1, grid)
        q = pl.program_id(2)

        @pl.when(q == 0)
        def _():
            acc[...] = jnp.zeros_like(acc)

        acc[...] += dtn(a_ref[...], b_ref[...])

        @pl.when(q == nk - 1)
        def _():
            o_ref[...] = acc[...].astype(bf16)

    out = pl.pallas_call(
        body, name=name, grid=grid,
        in_specs=[pl.BlockSpec((tk, tm), lambda i, j, q: (q, i)), pl.BlockSpec((tk, tn), lambda i, j, q: (q, j))] + cg.in_specs(),
        out_specs=[pl.BlockSpec((tm, tn), lambda i, j, q: (i, j))] + cg.in_specs(),
        out_shape=[jax.ShapeDtypeStruct((m, n), bf16)] + cg.out_shapes(),
        scratch_shapes=[pltpu.VMEM((tm, tn), f32)] + cg.sems(),
        compiler_params=_CP(dimension_semantics=("arbitrary",) * 3 if cg.n else ("parallel", "parallel", "arbitrary")),
    )(a, b, *cg.arrays)
    return out if cg.n else out[0]


def _adamw_refs(w_ref, g_ref, m_ref, v_ref, d_ref, nm_ref, nv_ref):
    c1, c2 = 1.0 - B1 ** STEP, 1.0 - B2 ** STEP
    gg = g_ref[...]
    nm = B1 * m_ref[...] + (1.0 - B1) * gg
    nv = B2 * v_ref[...] + (1.0 - B2) * jnp.square(gg)
    d_ref[...] = -LR * ((nm / c1) / (jnp.sqrt(nv / c2) + AEPS) + WD * w_ref[...])
    nm_ref[...], nv_ref[...] = nm, nv


def _adamw(w, g, m, v, name):
    def body(*refs):
        _adamw_refs(*refs)

    return pl.pallas_call(body, name=name, out_shape=[jax.ShapeDtypeStruct(w.shape, f32)] * 3, compiler_params=_CP())(w, g, m, v)


def _adamw_landed(land, w, m, v, name):
    def body(l_ref, w_ref, m_ref, v_ref, g_ref, d_ref, nm_ref, nv_ref):
        acc = l_ref[0].astype(f32)
        for j in range(1, NDEV):
            acc = acc + l_ref[j].astype(f32)
        g_ref[...] = acc
        _adamw_refs(w_ref, g_ref, m_ref, v_ref, d_ref, nm_ref, nv_ref)

    return pl.pallas_call(body, name=name, out_shape=[jax.ShapeDtypeStruct(w.shape, f32)] * 4, compiler_params=_CP())(land, w, m, v)


def _adamw_many(ws, gs, ms, vs, name):
    n = len(ws)

    def body(*refs):
        for k in range(n):
            _adamw_refs(*[refs[j * n + k] for j in range(7)])

    outs = pl.pallas_call(body, name=name, out_shape=[jax.ShapeDtypeStruct(w.shape, f32) for w in ws] * 3,
                          compiler_params=_CP())(*ws, *gs, *ms, *vs)
    return outs[:n], outs[n:2 * n], outs[2 * n:]


SMALL = ["conv_w", "c_ctx", "norm1_w", "s5_lambda_re_f", "s5_lambda_im_f", "s5_log_step_f", "s5_lambda_re_b", "s5_lambda_im_b",
         "s5_log_step_b", "s5_b_re", "s5_b_im", "s5_c_re", "s5_c_im", "s5_d", "s5_b_glu", "ret_log_decay_f", "ret_log_decay_b",
         "norm2_w", "conv_b", "final_norm_w"]
WEIGHTS = ["c_ctx", "w_mod", "b_mod", "norm1_w", "w_in", "s5_lambda_re_f", "s5_lambda_im_f", "s5_log_step_f", "s5_lambda_re_b",
           "s5_lambda_im_b", "s5_log_step_b", "s5_b_re", "s5_b_im", "s5_c_re", "s5_c_im", "s5_d", "s5_w_glu", "s5_b_glu",
           "ret_log_decay_f", "ret_log_decay_b", "w_out", "norm2_w", "w_up", "conv_w", "conv_b", "w_down", "final_norm_w"]


def _pack_small(vals):
    flat, offs, o = [], [], 0
    for a in vals:
        n = a.size
        npad = -n % 128
        flat.append(jnp.pad(a.reshape(-1), (0, npad)))
        offs.append((o, n))
        o += n + npad
    tail = -o % 1024
    if tail:
        flat.append(jnp.zeros((tail,), f32))
    return jnp.concatenate(flat).reshape(-1, 128), offs


def _unpack_small(packed, offs, shapes):
    flat = packed.reshape(-1)
    return [flat[o:o + n].reshape(s) for (o, n), s in zip(offs, shapes)]


def _rope_tables(L, nctx_rows):
    t = np.arange(L)
    inv = (ROPE_THETA ** (-np.arange(DH // 4, dtype=np.float64) / (DH // 4))).astype(np.float32)
    ang = np.concatenate([(t // GRID_W).astype(np.float32)[:, None] * inv, (t % GRID_W).astype(np.float32)[:, None] * inv], axis=-1)
    cos = np.repeat(np.cos(ang).astype(np.float32), 2, axis=1)
    sin = np.repeat(np.sin(ang).astype(np.float32), 2, axis=1) * np.tile(np.array([-1.0, 1.0], np.float32), DH // 2)
    cosf = np.concatenate([np.ones((nctx_rows, DH), np.float32), cos], axis=0)
    sins = np.concatenate([np.zeros((nctx_rows, DH), np.float32), sin], axis=0)
    return jnp.asarray(cosf), jnp.asarray(sins)


def kernel(x, c, ctx, c_ctx, w_mod, b_mod, norm1_w, w_in, s5_lambda_re_f, s5_lambda_im_f, s5_log_step_f, s5_lambda_re_b, s5_lambda_im_b, s5_log_step_b, s5_b_re, s5_b_im, s5_c_re, s5_c_im, s5_d, s5_w_glu, s5_b_glu, ret_log_decay_f, ret_log_decay_b, w_out, norm2_w, w_up, conv_w, conv_b, w_down, final_norm_w, loss_target, m_c_ctx, m_w_mod, m_b_mod, m_norm1_w, m_w_in, m_s5_lambda_re_f, m_s5_lambda_im_f, m_s5_log_step_f, m_s5_lambda_re_b, m_s5_lambda_im_b, m_s5_log_step_b, m_s5_b_re, m_s5_b_im, m_s5_c_re, m_s5_c_im, m_s5_d, m_s5_w_glu, m_s5_b_glu, m_ret_log_decay_f, m_ret_log_decay_b, m_w_out, m_norm2_w, m_w_up, m_conv_w, m_conv_b, m_w_down, m_final_norm_w, v_c_ctx, v_w_mod, v_b_mod, v_norm1_w, v_w_in, v_s5_lambda_re_f, v_s5_lambda_im_f, v_s5_log_step_f, v_s5_lambda_re_b, v_s5_lambda_im_b, v_s5_log_step_b, v_s5_b_re, v_s5_b_im, v_s5_c_re, v_s5_c_im, v_s5_d, v_s5_w_glu, v_s5_b_glu, v_ret_log_decay_f, v_ret_log_decay_b, v_w_out, v_norm2_w, v_w_up, v_conv_w, v_conv_b, v_w_down, v_final_norm_w):
    args = dict(locals())
    W = {n: args[n] for n in WEIGHTS}
    M = {n: args["m_" + n] for n in WEIGHTS}
    V = {n: args["v_" + n] for n in WEIGHTS}
    me = _me()
    x2, ctx2, tgt = x[0], ctx[0], loss_target[0]
    L, Lc = x2.shape[0], ctx2.shape[0]
    assert Lc == R and L % R == 0 and L % GRID_W == 0
    nctx = Lc // T

    w_in_tl, w_up_tl = w_in[0].T.astype(bf16), w_up[0].T.astype(bf16)
    w_out_l, w_down_l, w_glu_l = w_out[0].astype(bf16), w_down[0].astype(bf16), s5_w_glu[0].astype(bf16)
    per_cv = conv_w.shape[2]
    conv_pad = jnp.pad(conv_w[0], ((0, 5), (0, 128 * 3 - per_cv)))
    w_in_g, c_g, conv_g = _gather_two_level([w_in_tl, jnp.pad(c, ((0, 7), (0, 0))), conv_pad], "gather_w_in")
    w_in_t = w_in_g.reshape(INC, D)
    conv_f = conv_g[:, :3, :per_cv].transpose(1, 0, 2).reshape(3, DFF)

    c9 = jnp.concatenate([c_g[:, 0, :], c_ctx[None], jnp.zeros((7, D), f32)], axis=0)
    w_mod_l = w_mod[0]
    ncol = w_mod_l.shape[1]
    m_part = _ada_fwd(c9, w_mod_l, "ada_fwd")
    m_all = _all_gather_small(m_part, "gather_mod").transpose(1, 0, 2).reshape(16, 6, D)
    modx, modc = _mod_select(m_all, b_mod.reshape(6, D), "mod_select")

    pair = lambda a, b: jnp.concatenate([a, b], axis=-1)
    bre_g, bim_g = s5_b_re[0].transpose(0, 2, 1), s5_b_im[0].transpose(0, 2, 1)
    cre_g, cim_g = s5_c_re[0], s5_c_im[0]
    shared = (pair(bre_g, bim_g), pair(bim_g, bre_g), pair(cre_g, cim_g), pair(cim_g, cre_g))
    s5p = {}
    for tag, lre, lim, ls in (("f", s5_lambda_re_f, s5_lambda_im_f, s5_log_step_f), ("b", s5_lambda_re_b, s5_lambda_im_b, s5_log_step_b)):
        s5p[tag] = (pair(lre[0], lre[0])[:, None, :], pair(lim[0], lim[0])[:, None, :], ls[0].reshape(S5G, 1, 1)) + shared
    m_f, mb_f, mc_f, a1_f, a2_f = _s5_build(s5p["f"], False, "s5_build_f")
    m_b, mb_b, mc_b, a1_b, a2_b = _s5_build(s5p["b"], True, "s5_build_b")
    a1_f, a2_f, a1_b, a2_b = (a.reshape(S5G, SB) for a in (a1_f, a2_f, a1_b, a2_b))

    nw1, nw2, fnw = norm1_w, norm2_w, final_norm_w[None]
    cosf, sins = _rope_tables(L, Lc)
    p_ext, w_out_g, w_glu_g, w_up_g1 = _f1_fwd(x2, ctx2, modx, modc, nw1, w_in_t.T, cosf, sins, "f1_fwd",
                                               cargo=([w_out_l, w_glu_l, w_up_tl[:UP_HEAD]], False))
    nctx5 = Lc // TC
    u_g = _to_groups(p_ext[:, :S5W])
    s_f, s_b = _s5_inc(u_g, mb_f, mb_b, "s5_inc")
    hp_f, hp_b = _s5_carry(s_f, s_b, (a1_f, a2_f), (a1_b, a2_b), nctx5, "s5_carry")
    ys = _from_groups(_s5_out(u_g, m_f, m_b, hp_f, hp_b, mc_f, mc_b, "s5_out"))
    ld8 = lambda ld: jnp.pad(jnp.broadcast_to(ld[0][:, None], (RH, 128)), ((0, 8 - RH), (0, 0)))
    ldf8, ldb8 = ld8(ret_log_decay_f), ld8(ret_log_decay_b)
    of, ob, rp_f, rp_b, w_up_g2 = _ret_fwd(p_ext, ldf8, ldb8, nctx, "ret_fwd", cargo=([w_up_tl[UP_HEAD:]], False))
    w_out_f, w_glu_f = w_out_g.reshape(D, D), w_glu_g.reshape(S5W, S5W)
    x1, w_down_g = _mix_fwd(x2, ys, of, ob, p_ext, s5_d, s5_b_glu, modx, w_glu_f, w_out_f, "mix_fwd", cargo=([w_down_l], False))
    w_down_f = w_down_g.reshape(DFF, D)
    w_up_t = jnp.concatenate([w_up_g1, w_up_g2], axis=1).reshape(2 * DFF, D)

    (dx2, da, dgc, f_act, dffn, loss_acc, g_fnw, g_gate2, g_cb, g_cw) = _ffn_fwd(
        x1, tgt, nw2, modx, w_up_t[:DFF].T, w_up_t[DFF:].T, conv_f, conv_b, w_down_f, w_down_f.T, fnw, "ffn_fwd")
    dx1, dag, h2, g_nw2, dmx2 = _ffn_bwd(x1, dx2, da, dgc, nw2, modx, w_up_t, conv_f, "ffn_bwd")
    gw_down = _matmul_tn(f_act, dffn, "dw_down").reshape(NDEV, -1, D)
    gw_up_t = _matmul_tn(dag, h2, "dw_up").reshape(NDEV, -1, D)
    (dy_e, dud_e, do_e, dg_e, cat, dmix, s_act, dz, g_d, g_bglu, g_gate1, l_down) = _mix_bwd(
        x2, ys, of, ob, p_ext, s5_d, s5_b_glu, modx, w_glu_f, w_out_f, dx1, "mix_bwd", cargo=([gw_down], True))
    gw_out = _matmul_tn(cat, dmix, "dw_out").reshape(NDEV, -1, D)
    gw_glu = _matmul_tn(s_act, dz, "dw_glu").reshape(NDEV, -1, S5W)
    dq_f, dk_f, dv_f, dq_b, dk_b, dv_b, gld_f, gld_b, l_up, l_out, l_glu = _ret_bwd(
        p_ext, ldf8, ldb8, rp_f, rp_b, do_e, nctx, "ret_bwd", cargo=([gw_up_t, gw_out, gw_glu], True))

    du1, g_m, dhp_f, dhp_b, dmc_f, dmc_b = _s5_out_bwd(_to_groups(dy_e), u_g, m_f, m_b, hp_f, hp_b, mc_f, mc_b, "s5_out_bwd")
    ds_f, da1_f, da2_f = _s5_carry_bwd(dhp_f, hp_f, a1_f, a2_f, False, nctx5, "s5_carry_bwd_f")
    ds_b, da1_b, da2_b = _s5_carry_bwd(dhp_b, hp_b, a1_b, a2_b, True, nctx5, "s5_carry_bwd_b")
    du_g, dmb_f, dmb_b = _s5_inc_bwd(du1, u_g, ds_f, ds_b, mb_f, mb_b, "s5_inc_bwd")
    zero_p = jnp.zeros((S5G, S5P, SB), f32)
    gf = _s5_build_bwd(s5p["f"], (g_m, dmb_f, dmc_f, da1_f[:, None, :], da2_f[:, None, :]), (zero_p, zero_p), False, "s5_build_bwd_f")
    gb = _s5_build_bwd(s5p["b"], (g_m, dmb_b, dmc_b, da1_b[:, None, :], da2_b[:, None, :]), (gf[3], gf[4]), True, "s5_build_bwd_b")
    g_bre, g_bim = gb[3][:, :, :S5N].transpose(0, 2, 1), gb[3][:, :, S5N:].transpose(0, 2, 1)
    g_cre, g_cim = gb[4][:, :, :S5N], gb[4][:, :, S5N:]

    early = {
        "conv_w": g_cw, "s5_lambda_re_f": gf[0][:, 0, :S5N], "s5_lambda_im_f": gf[1][:, 0, :S5N],
        "s5_log_step_f": gf[2], "s5_lambda_re_b": gb[0][:, 0, :S5N], "s5_lambda_im_b": gb[1][:, 0, :S5N], "s5_log_step_b": gb[2],
        "s5_b_re": g_bre, "s5_b_im": g_bim, "s5_c_re": g_cre, "s5_c_im": g_cim, "s5_d": g_d, "s5_b_glu": g_bglu,
        "ret_log_decay_f": gld_f[:RH, 0], "ret_log_decay_b": gld_b[:RH, 0], "norm2_w": g_nw2, "conv_b": g_cb, "final_norm_w": g_fnw,
    }
    e_names = [n for n in SMALL if n in early]
    packed_e, eoffs = _pack_small([early[n].astype(f32) for n in e_names])
    grad_x, dp_ext, h1, g_nw1, dmx1, dmc1 = _f1_bwd(
        x2, ctx2, modx, modc, nw1, w_in_t, cosf, sins, dx1, (_from_groups(du_g), dud_e, dq_f, dq_b, dk_f, dk_b, dv_f, dv_b, dg_e), "f1_bwd")
    gw_in_t, land_e = _matmul_tn(dp_ext, h1, "dw_in", cargo=([packed_e], False))
    g_in_t = _reduce_scatter_two_level(gw_in_t.reshape(NDEV, -1, D), "scatter_dw_in")

    dmx = dmx1 + dmx2
    dmx = dmx.at[2].set(g_gate1[0]).at[5].set(g_gate2[0])
    dm_me = jnp.stack([dmx.reshape(-1), dmc1.reshape(-1)], axis=0)
    dm_all = _all_gather_small(dm_me.reshape(8, -1), "gather_dmod").reshape(NDEV, 2, 6 * D)
    dmx_all, dmc_all = dm_all[:, 0, :], dm_all[:, 1, :]
    my_cols = lambda a: lax.dynamic_slice(a, (0, me * ncol), (NDEV, ncol))
    gw_mod, g_bmod, dc9 = _ada_bwd(c9, dmx_all, dmc_all, my_cols(dmx_all), my_cols(dmc_all), w_mod_l, "ada_bwd")

    sshape = lambda n: (3, DFF) if n == "conv_w" else W[n].shape
    G = dict(zip(e_names, _unpack_small(_sum8(land_e, "reduce_early"), eoffs, [sshape(n) for n in e_names])))
    late = {"c_ctx": dc9[8], "norm1_w": g_nw1}
    packed_l, loffs = _pack_small([late[n].astype(f32) for n in late])
    G.update(zip(late, _unpack_small(_all_reduce_small(packed_l, "reduce_late"), loffs, [W[n].shape for n in late])))
    G["conv_w"] = lax.dynamic_slice(G["conv_w"], (0, me * per_cv), (3, per_cv))[None]
    G["b_mod"] = g_bmod.reshape(b_mod.shape)
    G["w_mod"] = gw_mod[None]
    G["w_in"] = g_in_t.T[None]
    G["w_up"] = _sum8(l_up, "sum_dw_up").T[None]

    delta, new_m, new_v = {}, {}, {}
    sm_names = SMALL[1:] + ["b_mod"]
    rows = lambda a: a.reshape(-1, a.shape[-1])
    outs = _adamw_many(*[[rows(d[n]) for n in sm_names] for d in (W, G, M, V)], "adamw_small")
    for dst, src in zip((delta, new_m, new_v), outs):
        dst.update({n: a.reshape(W[n].shape) for n, a in zip(sm_names, src)})
    for n in ["w_mod", "w_in", "w_up", "conv_w"]:
        d, nm, nv = _adamw(W[n][0], G[n][0], M[n][0], V[n][0], "adamw_" + n)
        delta[n], new_m[n], new_v[n] = d[None], nm[None], nv[None]
    for n, land in (("w_out", l_out), ("w_down", l_down), ("s5_w_glu", l_glu)):
        g, d, nm, nv = _adamw_landed(land, W[n][0], M[n][0], V[n][0], "adamw_" + n)
        G[n], delta[n], new_m[n], new_v[n] = g[None], d[None], nm[None], nv[None]

    loss = lax.psum(loss_acc[0, 0], ("x", "y", "c"))
    return (loss, grad_x[None], *[G[n] for n in WEIGHTS], *[delta[n] for n in WEIGHTS], *[new_m[n] for n in WEIGHTS],
            *[new_v[n] for n in WEIGHTS])
```

```python
import functools

import numpy as np
import jax
import jax.numpy as jnp
from jax import lax
from jax.experimental import pallas as pl
from jax.experimental.pallas import tpu as pltpu

f32, bf16 = jnp.float32, jnp.bfloat16

D = 1024
S5W, S5G, S5P, S5N = 512, 32, 16, 64
TC = 16
TCP = TC * S5P
SB = 2 * S5N
GBK = 8
UP_HEAD = 192
CARRY_UNROLL = 8
RH, DH = 4, 128
RW = RH * DH
INC = S5W + 4 * RW
DFF = 2816
T = 128
R = 256
RF = 128
HALO = 8
EPS = 1e-6
ROPE_THETA = 10000.0
GRID_W = 64
NDEV = 8
LR, B1, B2, AEPS, WD, STEP = 0.001, 0.9, 0.999, 1e-08, 0.01, 10
VMEM_LIMIT = 60 * 1024 * 1024
ACC_TILE_BYTES = 6 * 1024 * 1024
MESH = pl.DeviceIdType.MESH

_CP = functools.partial(pltpu.CompilerParams, vmem_limit_bytes=VMEM_LIMIT)
_ARB = ("arbitrary",)
_ANY = pl.BlockSpec(memory_space=pl.ANY)


def _dg(a, b, dims):
    return lax.dot_general(a.astype(bf16), b.astype(bf16), (dims, ((), ())), preferred_element_type=f32)


@jax.custom_vjp
def dnn(a, b):
    return _dg(a, b, ((1,), (0,)))


@jax.custom_vjp
def dnt(a, b):
    return _dg(a, b, ((1,), (1,)))


@jax.custom_vjp
def dtn(a, b):
    return _dg(a, b, ((0,), (0,)))


dnn.defvjp(lambda a, b: (dnn(a, b), (a, b)), lambda r, g: (dnt(g, r[1]).astype(r[0].dtype), dtn(r[0], g).astype(r[1].dtype)))
dnt.defvjp(lambda a, b: (dnt(a, b), (a, b)), lambda r, g: (dnn(g, r[1]).astype(r[0].dtype), dtn(g, r[0]).astype(r[1].dtype)))
dtn.defvjp(lambda a, b: (dtn(a, b), (a, b)), lambda r, g: (dnt(r[1], g).astype(r[0].dtype), dnn(r[0], g).astype(r[1].dtype)))


@jax.custom_vjp
def _dnn_const(a, w, wt):
    return dnn(a, w)


_dnn_const.defvjp(lambda a, w, wt: (dnn(a, w), wt), lambda wt, g: (dnn(g, wt), None, None))


_GELU_C0, _GELU_C1 = float(np.sqrt(2.0 / np.pi)), 0.044715


@jax.custom_vjp
def _gelu(x):
    return _gelu_fwd(x)[0]


def _gelu_fwd(x):
    t = jnp.tanh(_GELU_C0 * (x + _GELU_C1 * (x * x * x)))
    return x * (0.5 * (1.0 + t)), (x, t)


def _gelu_bwd(res, g):
    x, t = res
    return (g * (0.5 * (1.0 + t) + (0.5 * _GELU_C0) * x * (1.0 - t * t) * (1.0 + (3.0 * _GELU_C1) * (x * x))),)


_gelu.defvjp(_gelu_fwd, _gelu_bwd)


def _rms(t, w):
    return t * lax.rsqrt(jnp.mean(t * t, axis=-1, keepdims=True) + EPS) * w


def _mod(h, shift, scale):
    return h * (1.0 + scale) + shift


def _const_spec(shape):
    n = len(shape)
    return pl.BlockSpec(shape, lambda i, _n=n: (0,) * _n, pipeline_mode=pl.Buffered(1))


def _acc_spec(shape):
    n = len(shape)
    return pl.BlockSpec(shape, lambda i, _n=n: (0,) * _n)


def _me():
    return 4 * lax.axis_index("x") + 2 * lax.axis_index("y") + lax.axis_index("c")


def _peer(r):
    x, y, c = lax.axis_index("x"), lax.axis_index("y"), lax.axis_index("c")
    px = 1 - x if (r >> 2) & 1 else x
    py = 1 - y if (r >> 1) & 1 else y
    pc = 1 - c if r & 1 else c
    return (px, py, pc), 4 * px + 2 * py + pc


def _all_gather_small(v, name):
    r, c = v.shape

    def body(v_ref, out_ref, send_sems, recv_sems):
        me = _me()
        out_ref[me] = v_ref[...]
        sends = []
        for k in range(1, NDEV):
            peer, _ = _peer(k)
            cp = pltpu.make_async_remote_copy(src_ref=v_ref, dst_ref=out_ref.at[me], send_sem=send_sems.at[k - 1],
                                              recv_sem=recv_sems.at[k - 1], device_id=peer, device_id_type=MESH)
            cp.start()
            sends.append(cp)
        for k in range(1, NDEV):
            peer, pidx = _peer(k)
            pltpu.make_async_remote_copy(src_ref=v_ref, dst_ref=out_ref.at[pidx], send_sem=send_sems.at[k - 1],
                                         recv_sem=recv_sems.at[k - 1], device_id=peer, device_id_type=MESH).wait_recv()
        for cp in sends:
            cp.wait_send()

    return pl.pallas_call(
        body, name=name, out_shape=jax.ShapeDtypeStruct((NDEV, r, c), v.dtype),
        in_specs=[pl.BlockSpec(memory_space=pltpu.VMEM)], out_specs=pl.BlockSpec(memory_space=pltpu.VMEM),
        scratch_shapes=[pltpu.SemaphoreType.DMA((NDEV - 1,)), pltpu.SemaphoreType.DMA((NDEV - 1,))],
        compiler_params=_CP(),
    )(v)


def _all_reduce_small(v, name):
    r, c = v.shape

    def body(v_ref, out_ref, land, send_sems, recv_sems):
        me = _me()
        land[me] = v_ref[...]
        sends = []
        for k in range(1, NDEV):
            peer, _ = _peer(k)
            cp = pltpu.make_async_remote_copy(src_ref=v_ref, dst_ref=land.at[me], send_sem=send_sems.at[k - 1],
                                              recv_sem=recv_sems.at[k - 1], device_id=peer, device_id_type=MESH)
            cp.start()
            sends.append(cp)
        for k in range(1, NDEV):
            peer, pidx = _peer(k)
            pltpu.make_async_remote_copy(src_ref=v_ref, dst_ref=land.at[pidx], send_sem=send_sems.at[k - 1],
                                         recv_sem=recv_sems.at[k - 1], device_id=peer, device_id_type=MESH).wait_recv()
        for cp in sends:
            cp.wait_send()
        acc = land[0]
        for j in range(1, NDEV):
            acc = acc + land[j]
        out_ref[...] = acc

    return pl.pallas_call(
        body, name=name, out_shape=jax.ShapeDtypeStruct((r, c), v.dtype),
        in_specs=[pl.BlockSpec(memory_space=pltpu.VMEM)], out_specs=pl.BlockSpec(memory_space=pltpu.VMEM),
        scratch_shapes=[pltpu.VMEM((NDEV, r, c), v.dtype), pltpu.SemaphoreType.DMA((NDEV - 1,)),
                        pltpu.SemaphoreType.DMA((NDEV - 1,))],
        compiler_params=_CP(),
    )(v)


class _Exchange:
    def __init__(self, srcs, dsts, send_sems, recv_sems, local_sems, scatter):
        me = _me()
        n = len(srcs)
        self.sends, self.recvs, self.locals = [], [], []
        for a, (s, d) in enumerate(zip(srcs, dsts)):
            self.locals.append(pltpu.make_async_copy(s.at[me] if scatter else s, d.at[me], local_sems.at[a]))
        for k in range(1, NDEV):
            peer, pidx = _peer(k)
            for a, (s, d) in enumerate(zip(srcs, dsts)):
                src = s.at[pidx] if scatter else s
                sem = (k - 1) * n + a
                for dst, out in ((d.at[me], self.sends), (d.at[pidx], self.recvs)):
                    out.append(pltpu.make_async_remote_copy(src_ref=src, dst_ref=dst, send_sem=send_sems.at[sem],
                                                            recv_sem=recv_sems.at[sem], device_id=peer, device_id_type=MESH))

    def start(self):
        for cp in self.locals + self.sends:
            cp.start()

    def wait(self):
        for cp in self.recvs:
            cp.wait_recv()
        for cp in self.sends:
            cp.wait_send()
        for cp in self.locals:
            cp.wait()


def _exchange_shapes(arrays, scatter):
    return [jax.ShapeDtypeStruct(a.shape if scatter else (NDEV,) + a.shape, a.dtype) for a in arrays]


def _exchange_sems(n):
    return [pltpu.SemaphoreType.DMA(((NDEV - 1) * n,)), pltpu.SemaphoreType.DMA(((NDEV - 1) * n,)), pltpu.SemaphoreType.DMA((n,))]


def _exchange(arrays, scatter, name):
    n = len(arrays)

    def body(*refs):
        ex = _Exchange(refs[:n], refs[n:2 * n], *refs[2 * n:], scatter)
        ex.start()
        ex.wait()

    return pl.pallas_call(body, name=name, out_shape=_exchange_shapes(arrays, scatter), in_specs=[_ANY] * n,
                          out_specs=[_ANY] * n, scratch_shapes=_exchange_sems(n), compiler_params=_CP())(*arrays)


def _chips():
    x, y, c = lax.axis_index("x"), lax.axis_index("y"), lax.axis_index("c")
    return (x, y, c), (x, y, 1 - c), [(1 - x, y), (x, 1 - y), (1 - x, 1 - y)]


def _gather_two_level(arrays, name):
    n = len(arrays)

    def body(*refs):
        srcs, outs = refs[:n], refs[n:2 * n]
        send_sems, recv_sems = refs[2 * n:]
        me, sibling, chips = _chips()
        c = me[2]
        idx = lambda p: 4 * p[0] + 2 * p[1] + p[2]

        def copy(a, k, block, to, src=None):
            return pltpu.make_async_remote_copy(
                src_ref=outs[a].at[idx(block)] if src is None else src, dst_ref=outs[a].at[idx(block)],
                send_sem=send_sems.at[7 * a + k], recv_sem=recv_sems.at[7 * a + k], device_id=to, device_id_type=MESH)

        first, passed = [], []
        for a in range(n):
            outs[a][idx(me)] = srcs[a][...]
            first += [copy(a, 0, me, sibling, src=srcs[a])]
            first += [copy(a, 1 + j, me, (*chip, c), src=srcs[a]) for j, chip in enumerate(chips)]
        for cp in first:
            cp.start()
        for a in range(n):
            for j, chip in enumerate(chips):
                copy(a, 1 + j, (*chip, c), me).wait_recv()
                cp = copy(a, 4 + j, (*chip, c), sibling)
                cp.start()
                passed.append(cp)
        for a in range(n):
            copy(a, 0, sibling, me).wait_recv()
            for j, chip in enumerate(chips):
                copy(a, 4 + j, (*chip, 1 - c), me).wait_recv()
        for cp in first + passed:
            cp.wait_send()

    vm = pl.BlockSpec(memory_space=pltpu.VMEM)
    return pl.pallas_call(
        body, name=name, out_shape=[jax.ShapeDtypeStruct((NDEV,) + a.shape, a.dtype) for a in arrays],
        in_specs=[vm] * n, out_specs=[vm] * n,
        scratch_shapes=[pltpu.SemaphoreType.DMA((7 * n,)), pltpu.SemaphoreType.DMA((7 * n,))],
        compiler_params=_CP(),
    )(*arrays)


def _reduce_scatter_two_level(g, name):
    _, r, c = g.shape
    nchip = NDEV // 2

    def body(g_ref, o_ref, stage, part, land, d_send, d_recv, i_send, i_recv):
        me, sibling, chips = _chips()
        x, y, cc = me
        mine = 2 * x + y

        def blk(k, core):
            return 2 * k + core

        swaps = [pltpu.make_async_remote_copy(src_ref=g_ref.at[blk(k, 1 - cc)], dst_ref=stage.at[k], send_sem=d_send.at[k],
                                              recv_sem=d_recv.at[k], device_id=sibling, device_id_type=MESH) for k in range(nchip)]
        for cp in swaps:
            cp.start()
        for cp in swaps:
            cp.wait_recv()
        for k in range(nchip):
            part[k] = (g_ref[blk(k, cc)].astype(f32) + stage[k].astype(f32)).astype(bf16)
        sends = []
        for j, chip in enumerate(chips):
            kd = 2 * chip[0] + chip[1]
            cp = pltpu.make_async_remote_copy(src_ref=part.at[kd], dst_ref=land.at[mine], send_sem=i_send.at[j],
                                              recv_sem=i_recv.at[j], device_id=(*chip, cc), device_id_type=MESH)
            cp.start()
            sends.append(cp)
        land[mine] = part[mine]
        for j, chip in enumerate(chips):
            ks = 2 * chip[0] + chip[1]
            pltpu.make_async_remote_copy(src_ref=part.at[ks], dst_ref=land.at[ks], send_sem=i_send.at[j], recv_sem=i_recv.at[j],
                                         device_id=(*chip, cc), device_id_type=MESH).wait_recv()
        for cp in swaps + sends:
            cp.wait_send()
        acc = land[0].astype(f32)
        for k in range(1, nchip):
            acc = acc + land[k].astype(f32)
        o_ref[...] = acc

    vm = pl.BlockSpec(memory_space=pltpu.VMEM)
    return pl.pallas_call(
        body, name=name, out_shape=jax.ShapeDtypeStruct((r, c), f32), in_specs=[vm], out_specs=vm,
        scratch_shapes=[pltpu.VMEM((nchip, r, c), g.dtype)] * 3 + [pltpu.SemaphoreType.DMA((nchip,)), pltpu.SemaphoreType.DMA((nchip,)),
                                                                   pltpu.SemaphoreType.DMA((3,)), pltpu.SemaphoreType.DMA((3,))],
        compiler_params=_CP(),
    )(g)


class _Cargo:
    def __init__(self, cargo):
        self.arrays, self.scatter = cargo if cargo else ([], False)
        self.n = len(self.arrays)

    def in_specs(self):
        return [_ANY] * self.n

    def out_shapes(self):
        return _exchange_shapes(self.arrays, self.scatter)

    def sems(self):
        return _exchange_sems(self.n) if self.n else []

    def split(self, refs, n_in, n_out, n_scratch):
        n = self.n
        return refs[:n_in], refs[n_in + n:n_in + n + n_out], refs[n_in + 2 * n + n_out:n_in + 2 * n + n_out + n_scratch]

    def ride(self, refs, n_in, n_out, grid):
        if not self.n:
            return
        n = self.n
        ex = _Exchange(refs[n_in:n_in + n], refs[n_in + n + n_out:n_in + 2 * n + n_out], *refs[-3:], self.scatter)
        grid = (grid,) if isinstance(grid, int) else tuple(grid)
        first = functools.reduce(jnp.logical_and, [pl.program_id(a) == 0 for a in range(len(grid))])
        last = functools.reduce(jnp.logical_and, [pl.program_id(a) == g - 1 for a, g in enumerate(grid)])

        @pl.when(first)
        def _():
            ex.start()

        @pl.when(last)
        def _():
            ex.wait()


def _sum8(land, name):
    _, r, c = land.shape
    rb = next((b for b in (256, 64, 32) if r % b == 0), r)

    def body(l_ref, o_ref):
        acc = l_ref[0].astype(f32)
        for j in range(1, NDEV):
            acc = acc + l_ref[j].astype(f32)
        o_ref[...] = acc

    return pl.pallas_call(
        body, name=name, grid=(r // rb,), out_shape=jax.ShapeDtypeStruct((r, c), f32),
        in_specs=[pl.BlockSpec((NDEV, rb, c), lambda i: (0, i, 0))], out_specs=pl.BlockSpec((rb, c), lambda i: (i, 0)),
        compiler_params=_CP(dimension_semantics=("parallel",)),
    )(land)


def _ada_fwd(c9, w_mod_l, name):
    def body(c_ref, w_ref, o_ref):
        o_ref[...] = dnn(jax.nn.silu(c_ref[...]), w_ref[...])

    return pl.pallas_call(body, name=name, out_shape=jax.ShapeDtypeStruct((16, w_mod_l.shape[1]), f32),
                          compiler_params=_CP())(c9, w_mod_l)


def _mod_select(m_all, b_mod6, name):
    def body(m_ref, b_ref, mx_ref, mc_ref):
        me = _me()
        mx_ref[...] = m_ref[me] + b_ref[...]
        mc_ref[...] = m_ref[8] + b_ref[...]

    return pl.pallas_call(body, name=name, out_shape=[jax.ShapeDtypeStruct((6, D), f32)] * 2, compiler_params=_CP())(m_all, b_mod6)


def _ada_bwd(c9, dmx_all, dmc_all, dmx_l, dmc_l, w_mod_l, name):
    ncol = w_mod_l.shape[1]

    def rowsum(r):
        acc = r[0:1]
        for j in range(1, NDEV):
            acc = acc + r[j:j + 1]
        return acc

    def body(c_ref, xa_ref, ca_ref, xl_ref, cl_ref, w_ref, gw_ref, gb_ref, dc_ref):
        s9, vjp = jax.vjp(jax.nn.silu, c_ref[...])
        dm9 = jnp.concatenate([xl_ref[...], rowsum(cl_ref[...]), jnp.zeros((7, ncol), f32)], axis=0)
        gw_ref[...] = dtn(s9, dm9)
        gb_ref[...] = rowsum(xa_ref[...]) + rowsum(ca_ref[...])
        dc_ref[...] = vjp(dnt(dm9, w_ref[...]))[0]

    return pl.pallas_call(
        body, name=name,
        out_shape=[jax.ShapeDtypeStruct((D, ncol), f32), jax.ShapeDtypeStruct((1, 6 * D), f32), jax.ShapeDtypeStruct((16, D), f32)],
        compiler_params=_CP())(c9, dmx_all, dmc_all, dmx_l, dmc_l, w_mod_l)


def _lane_sign(rank):
    shape = (1,) * (rank - 1) + (SB,)
    return jnp.where(lax.broadcasted_iota(jnp.int32, shape, rank - 1) < S5N, -1.0, 1.0)


def _s5_build_fn(lre2, lim2, ls, bn, bs, cn, cs, rev):
    sg = _lane_sign(3)
    s = jnp.exp(ls)
    ar, ai = lre2 * s, lim2 * s
    e = jnp.exp(ar)
    nr, ni = e * jnp.cos(ai) - 1.0, e * jnp.sin(ai)
    den = lre2 * lre2 + lim2 * lim2
    cr, ci = (nr * lre2 + ni * lim2) / den, (ni * lre2 - nr * lim2) / den
    bbn = cr * bn + (ci * sg) * bs
    bbs = cr * bs - (ci * sg) * bn

    def powers(ex):
        m, ang = jnp.exp(ex * ar), ex * ai
        return m * jnp.cos(ang), m * jnp.sin(ang) * sg

    def times(tabs, xn, xs):
        f1, f2 = tabs
        return f1[:, :, None, :] * xn[:, None, :, :] + f2[:, :, None, :] * xs[:, None, :, :]

    t = lax.broadcasted_iota(jnp.int32, (1, TC, 1), 1).astype(f32)
    if rev:
        e_src, e_dst, e_out, e_in = t - (TC - 1.0), (TC - 1.0) - t, t, TC - t
    else:
        e_src, e_dst, e_out, e_in = -t, t, (TC - 1.0) - t, t + 1.0
    g = lre2.shape[0]
    flat = lambda a: a.reshape(g, TCP, SB)
    conj = -_lane_sign(4)
    ll = flat(times(powers(e_src), bbn, bbs))
    rr = flat(times(powers(e_dst), cn, cs) * conj)
    mb = flat(times(powers(e_out), bbn, bbs))
    mct = flat(times(powers(e_in), cn, cs) * conj)
    a1, a2 = powers(float(TC))
    row = lax.broadcasted_iota(jnp.int32, (TCP, TCP), 0) // S5P
    col = lax.broadcasted_iota(jnp.int32, (TCP, TCP), 1) // S5P
    mask = jnp.where((col <= row) if rev else (col >= row), 1.0, 0.0)
    m = jnp.concatenate([dnt(ll[j], rr[j])[None] for j in range(g)], axis=0) * mask
    return m, mb, mct, a1, a2


def _gspec(*tail):
    nt = len(tail)
    return pl.BlockSpec((GBK,) + tail, lambda i, _n=nt: (i,) + (0,) * _n)


def _s5_build(params, rev, name):
    def body(l1, l2, ls, bn, bs, cn, cs, m_ref, mb_ref, mc_ref, a1_ref, a2_ref):
        m, mb, mct, a1, a2 = _s5_build_fn(l1[...], l2[...], ls[...], bn[...], bs[...], cn[...], cs[...], rev)
        m_ref[...], mb_ref[...], mc_ref[...] = m.astype(bf16), mb.astype(bf16), mct.astype(bf16)
        a1_ref[...], a2_ref[...] = a1, a2

    vec, pm = _gspec(1, SB), _gspec(S5P, SB)
    return pl.pallas_call(
        body, name=name, grid=(S5G // GBK,),
        in_specs=[vec, vec, _gspec(1, 1), pm, pm, pm, pm],
        out_specs=[_gspec(TCP, TCP), _gspec(TCP, SB), _gspec(TCP, SB), vec, vec],
        out_shape=[jax.ShapeDtypeStruct((S5G, TCP, TCP), bf16), jax.ShapeDtypeStruct((S5G, TCP, SB), bf16),
                   jax.ShapeDtypeStruct((S5G, TCP, SB), bf16), jax.ShapeDtypeStruct((S5G, 1, SB), f32),
                   jax.ShapeDtypeStruct((S5G, 1, SB), f32)],
        compiler_params=_CP(dimension_semantics=("parallel",)),
    )(*params)


def _s5_build_bwd(params, cots, prev, rev, name):
    def body(l1, l2, ls, bn, bs, cn, cs, dm, dmb, dmc, da1, da2, pb, pc, gl1, gl2, gls, gb, gc):
        prim = (l1[...], l2[...], ls[...], bn[...], bs[...], cn[...], cs[...])
        _, vjp = jax.vjp(functools.partial(_s5_build_fn, rev=rev), *prim)
        d1, d2, dls, dbn, dbs, dcn, dcs = vjp((dm[...], dmb[...], dmc[...], da1[...], da2[...]))
        gl1[...] = d1 + pltpu.roll(d1, S5N, axis=2)
        gl2[...] = d2 + pltpu.roll(d2, S5N, axis=2)
        gls[...] = dls
        gb[...] = dbn + pltpu.roll(dbs, S5N, axis=2) + pb[...]
        gc[...] = dcn + pltpu.roll(dcs, S5N, axis=2) + pc[...]

    vec, pm, big = _gspec(1, SB), _gspec(S5P, SB), _gspec(TCP, SB)
    return pl.pallas_call(
        body, name=name, grid=(S5G // GBK,),
        in_specs=[vec, vec, _gspec(1, 1), pm, pm, pm, pm, _gspec(TCP, TCP), big, big, vec, vec, pm, pm],
        out_specs=[vec, vec, _gspec(1, 1), pm, pm],
        out_shape=[jax.ShapeDtypeStruct((S5G, 1, SB), f32), jax.ShapeDtypeStruct((S5G, 1, SB), f32),
                   jax.ShapeDtypeStruct((S5G, 1, 1), f32), jax.ShapeDtypeStruct((S5G, S5P, SB), f32),
                   jax.ShapeDtypeStruct((S5G, S5P, SB), f32)],
        compiler_params=_CP(dimension_semantics=("parallel",)),
    )(*params, *cots, *prev)


def _s5_inc(u, mb_f, mb_b, name):
    nc = u.shape[1]

    def body(u_ref, mf_ref, mb_ref, sf_ref, sb_ref):
        for j in range(GBK):
            sf_ref[:, j, :] = jnp.dot(u_ref[j], mf_ref[j], preferred_element_type=f32)
            sb_ref[:, j, :] = jnp.dot(u_ref[j], mb_ref[j], preferred_element_type=f32)

    sspec = pl.BlockSpec((nc, GBK, SB), lambda i: (0, i, 0))
    return pl.pallas_call(
        body, name=name, grid=(S5G // GBK,), in_specs=[_gspec(nc, TCP), _gspec(TCP, SB), _gspec(TCP, SB)],
        out_specs=[sspec, sspec], out_shape=[jax.ShapeDtypeStruct((nc, S5G, SB), f32)] * 2,
        compiler_params=_CP(dimension_semantics=("parallel",)),
    )(u, mb_f, mb_b)


def _idx_fwd(nctx, nch):
    return lambda i: i


def _idx_rev(nctx, nch):
    return lambda i: jnp.where(i < nctx, nctx - 1 - i, nch + nctx - 1 - i)


def _carry_loop(nc, step, init):
    def trip(i, c):
        for k in range(CARRY_UNROLL):
            c = step(i * CARRY_UNROLL + k, c)
        return c

    return lax.fori_loop(0, nc // CARRY_UNROLL, trip, init)


def _s5_carry(s_f, s_b, a_f, a_b, nctx, name):
    nc = s_f.shape[0]
    idx_b = _idx_rev(nctx, nc)

    def body(sf_ref, sb_ref, f1_ref, f2_ref, b1_ref, b2_ref, hf_ref, hb_ref):
        f1, f2, b1, b2 = f1_ref[...], f2_ref[...], b1_ref[...], b2_ref[...]

        def step(i, c):
            hf, hfs, hb, hbs = c
            rb = idx_b(i)
            hf_ref[i] = hf
            hb_ref[rb] = hb
            sf, sb = sf_ref[i], sb_ref[rb]
            return (f1 * hf + f2 * hfs + sf, f1 * hfs - f2 * hf + pltpu.roll(sf, S5N, axis=1),
                    b1 * hb + b2 * hbs + sb, b1 * hbs - b2 * hb + pltpu.roll(sb, S5N, axis=1))

        z = jnp.zeros((S5G, SB), f32)
        _carry_loop(nc, step, (z, z, z, z))

    return pl.pallas_call(body, name=name, out_shape=[jax.ShapeDtypeStruct(s_f.shape, f32)] * 2,
                          compiler_params=_CP())(s_f, s_b, *a_f, *a_b)


def _s5_carry_bwd(dhp, hp, a1, a2, rev, nctx, name):
    nc = hp.shape[0]
    idx = (_idx_rev if rev else _idx_fwd)(nctx, nc)

    def body(dhp_ref, hp_ref, a1_ref, a2_ref, ds_ref, d1_ref, d2_ref):
        f1, f2 = a1_ref[...], a2_ref[...]

        def step(k, carry):
            ab, abs_, d1, d2 = carry
            r = idx(nc - 1 - k)
            ds_ref[r] = ab
            h, dh = hp_ref[r], dhp_ref[r]
            return (dh + f1 * ab - f2 * abs_, pltpu.roll(dh, S5N, axis=1) + f1 * abs_ + f2 * ab,
                    d1 + ab * h, d2 + ab * pltpu.roll(h, S5N, axis=1))

        z = jnp.zeros((S5G, SB), f32)
        _, _, d1, d2 = _carry_loop(nc, step, (z, z, z, z))
        d1_ref[...], d2_ref[...] = d1, d2

    return pl.pallas_call(
        body, name=name,
        out_shape=[jax.ShapeDtypeStruct(hp.shape, f32), jax.ShapeDtypeStruct((S5G, SB), f32), jax.ShapeDtypeStruct((S5G, SB), f32)],
        compiler_params=_CP())(dhp, hp, a1, a2)


def _s5_out(u, m_f, m_b, hp_f, hp_b, mc_f, mc_b, name):
    nc = u.shape[1]

    def body(u_ref, mf_ref, mb_ref, hf_ref, hb_ref, cf_ref, cb_ref, y_ref):
        for j in range(GBK):
            uj = u_ref[j]
            y_ref[j] = (jnp.dot(uj, mf_ref[j], preferred_element_type=f32) + jnp.dot(uj, mb_ref[j], preferred_element_type=f32)
                        + dnt(hf_ref[:, j, :], cf_ref[j]) + dnt(hb_ref[:, j, :], cb_ref[j])).astype(bf16)

    sspec = pl.BlockSpec((nc, GBK, SB), lambda i: (0, i, 0))
    return pl.pallas_call(
        body, name=name, grid=(S5G // GBK,),
        in_specs=[_gspec(nc, TCP), _gspec(TCP, TCP), _gspec(TCP, TCP), sspec, sspec, _gspec(TCP, SB), _gspec(TCP, SB)],
        out_specs=_gspec(nc, TCP), out_shape=jax.ShapeDtypeStruct((S5G, nc, TCP), bf16),
        compiler_params=_CP(dimension_semantics=("parallel",)),
    )(u, m_f, m_b, hp_f, hp_b, mc_f, mc_b)


def _s5_out_bwd(dy, u, m_f, m_b, hp_f, hp_b, mc_f, mc_b, name):
    nc = u.shape[1]

    def body(dy_ref, u_ref, mf_ref, mb_ref, hf_ref, hb_ref, cf_ref, cb_ref, du_ref, g_ref, dhf_ref, dhb_ref, dcf_ref, dcb_ref):
        for j in range(GBK):
            dyj = dy_ref[j]
            du_ref[j] = dnt(dyj, mf_ref[j]) + dnt(dyj, mb_ref[j])
            g_ref[j] = dtn(u_ref[j], dyj)
            dhf_ref[:, j, :] = dnn(dyj, cf_ref[j])
            dhb_ref[:, j, :] = dnn(dyj, cb_ref[j])
            dcf_ref[j] = dtn(dyj, hf_ref[:, j, :])
            dcb_ref[j] = dtn(dyj, hb_ref[:, j, :])

    sspec = pl.BlockSpec((nc, GBK, SB), lambda i: (0, i, 0))
    sshape = jax.ShapeDtypeStruct((nc, S5G, SB), f32)
    cshape = jax.ShapeDtypeStruct((S5G, TCP, SB), f32)
    return pl.pallas_call(
        body, name=name, grid=(S5G // GBK,),
        in_specs=[_gspec(nc, TCP), _gspec(nc, TCP), _gspec(TCP, TCP), _gspec(TCP, TCP), sspec, sspec, _gspec(TCP, SB), _gspec(TCP, SB)],
        out_specs=[_gspec(nc, TCP), _gspec(TCP, TCP), sspec, sspec, _gspec(TCP, SB), _gspec(TCP, SB)],
        out_shape=[jax.ShapeDtypeStruct((S5G, nc, TCP), f32), jax.ShapeDtypeStruct((S5G, TCP, TCP), f32), sshape, sshape, cshape, cshape],
        compiler_params=_CP(dimension_semantics=("parallel",)),
    )(dy, u, m_f, m_b, hp_f, hp_b, mc_f, mc_b)


def _s5_inc_bwd(du1, u, ds_f, ds_b, mb_f, mb_b, name):
    nc = u.shape[1]

    def body(du1_ref, u_ref, dsf_ref, dsb_ref, mf_ref, mb_ref, du_ref, dmf_ref, dmb_ref):
        for j in range(GBK):
            dsf, dsb = dsf_ref[:, j, :], dsb_ref[:, j, :]
            du_ref[j] = (du1_ref[j] + dnt(dsf, mf_ref[j]) + dnt(dsb, mb_ref[j])).astype(bf16)
            dmf_ref[j] = dtn(u_ref[j], dsf)
            dmb_ref[j] = dtn(u_ref[j], dsb)

    sspec = pl.BlockSpec((nc, GBK, SB), lambda i: (0, i, 0))
    cshape = jax.ShapeDtypeStruct((S5G, TCP, SB), f32)
    return pl.pallas_call(
        body, name=name, grid=(S5G // GBK,),
        in_specs=[_gspec(nc, TCP), _gspec(nc, TCP), sspec, sspec, _gspec(TCP, SB), _gspec(TCP, SB)],
        out_specs=[_gspec(nc, TCP), _gspec(TCP, SB), _gspec(TCP, SB)],
        out_shape=[jax.ShapeDtypeStruct((S5G, nc, TCP), bf16), cshape, cshape],
        compiler_params=_CP(dimension_semantics=("parallel",)),
    )(du1, u, ds_f, ds_b, mb_f, mb_b)


def _to_groups(a):
    n = a.shape[0]
    return a.reshape(n // TC, TC, S5G, S5P).transpose(2, 0, 1, 3).reshape(S5G, n // TC, TCP)


def _from_groups(a):
    nc = a.shape[1]
    return a.reshape(S5G, nc, TC, S5P).transpose(1, 2, 0, 3).reshape(nc * TC, S5W)


def _swap_pairs(t):
    lane = lax.broadcasted_iota(jnp.int32, t.shape, 1)
    return jnp.where(lane % 2 == 0, pltpu.roll(t, DH - 1, axis=1), pltpu.roll(t, 1, axis=1))


def _rot(t, cosf, sins):
    return t * cosf + _swap_pairs(t) * sins


def _rot_t(d, cosf, sins):
    return d * cosf - _swap_pairs(d) * sins


def _ret_tables(ld, rev):
    pos = lax.broadcasted_iota(jnp.int32, (T, 1), 0).astype(f32)
    diff = pos - lax.broadcasted_iota(jnp.int32, (1, T), 1).astype(f32)
    if rev:
        keep, dist = diff < 0, jnp.maximum(-diff, 0.0)
        xi, zeta = jnp.exp(ld * (T - pos)), jnp.exp(ld * pos)
    else:
        keep, dist = diff >= 0, jnp.maximum(diff, 0.0)
        xi, zeta = jnp.exp(ld * (pos + 1.0)), jnp.exp(ld * (T - 1.0 - pos))
    return jnp.where(keep, jnp.exp(ld * dist), 0.0), xi, zeta, jnp.exp(ld * float(T))


def _ret_apply(qr, kr, v, rp, dm, xi, zeta, cdec):
    out = dnn(dnt(qr, kr) * dm, v) + dnn(qr * xi, rp)
    return out, cdec * rp + dtn(kr * zeta, v)


def _ret_chunk(qr, kr, v, rp, ld, rev):
    return _ret_apply(qr, kr, v, rp, *_ret_tables(ld, rev))


def _ret_fwd(p_ext, ld8_f, ld8_b, nctx, name, cargo=None):
    n = p_ext.shape[0]
    nch = n // T
    idx_b = _idx_rev(nctx, nch)
    cg = _Cargo(cargo)

    def body(*refs):
        ins, (of_ref, ob_ref, rpf_ref, rpb_ref), (rf_s, rb_s, dm_s, xz_s) = cg.split(refs, 8, 4, 4)
        qf, kf, vf, qb, kb, vb, ldf_ref, ldb_ref = ins
        cg.ride(refs, 8, 4, nch)

        @pl.when(pl.program_id(0) == 0)
        def _():
            rf_s[...] = jnp.zeros_like(rf_s)
            rb_s[...] = jnp.zeros_like(rb_s)
            for d, ld_ref in enumerate((ldf_ref, ldb_ref)):
                for h in range(RH):
                    dm, xi, zeta, cdec = _ret_tables(ld_ref[h:h + 1, 0:1], bool(d))
                    dm_s[d, h] = dm
                    xz_s[d, h, 0] = jnp.broadcast_to(xi, (T, DH))
                    xz_s[d, h, 1] = jnp.broadcast_to(zeta, (T, DH))
                    xz_s[d, h, 2] = jnp.broadcast_to(cdec, (T, DH))

        for h in range(RH):
            sl = slice(h * DH, (h + 1) * DH)
            for d, (q_ref, k_ref, v_ref, o_ref, rp_ref, r_s) in enumerate(((qf, kf, vf, of_ref, rpf_ref, rf_s),
                                                                            (qb, kb, vb, ob_ref, rpb_ref, rb_s))):
                rp = r_s[h]
                rp_ref[0, h] = rp
                out, rn = _ret_apply(q_ref[:, sl].astype(f32), k_ref[:, sl].astype(f32), v_ref[:, sl].astype(f32), rp,
                                     dm_s[d, h], xz_s[d, h, 0], xz_s[d, h, 1], xz_s[d, h, 2])
                r_s[h] = rn
                o_ref[:, sl] = out

    fcol = lambda cb: pl.BlockSpec((T, RW), lambda i, _c=cb: (i, _c))
    bcol = lambda cb: pl.BlockSpec((T, RW), lambda i, _c=cb: (idx_b(i), _c))
    rspec = pl.BlockSpec((1, RH, DH, DH), lambda i: (i, 0, 0, 0))
    oshape, rshape = jax.ShapeDtypeStruct((n, RW), f32), jax.ShapeDtypeStruct((nch, RH, DH, DH), f32)
    return pl.pallas_call(
        body, name=name, grid=(nch,),
        in_specs=[fcol(1), fcol(2), fcol(3), bcol(1), bcol(2), bcol(3), _const_spec((8, 128)), _const_spec((8, 128))] + cg.in_specs(),
        out_specs=[fcol(0), bcol(0), rspec, rspec] + cg.in_specs(),
        out_shape=[oshape, oshape, rshape, rshape] + cg.out_shapes(),
        scratch_shapes=[pltpu.VMEM((RH, DH, DH), f32)] * 2 + [pltpu.VMEM((2, RH, T, T), f32), pltpu.VMEM((2, RH, 3, T, DH), f32)] + cg.sems(),
        compiler_params=_CP(dimension_semantics=_ARB),
    )(p_ext, p_ext, p_ext, p_ext, p_ext, p_ext, ld8_f, ld8_b, *cg.arrays)


def _ret_bwd(p_ext, ld8_f, ld8_b, rp_f, rp_b, do_ext, nctx, name, cargo=None):
    n = p_ext.shape[0]
    nch = n // T
    idx_rev = _idx_rev(nctx, nch)
    idf = lambda j: nch - 1 - j
    idb = lambda j: idx_rev(nch - 1 - j)
    cg = _Cargo(cargo)

    def body(*refs):
        ins, outs, (drf_s, drb_s, dm_s, xz_s, gdm_s, gxz_s) = cg.split(refs, 12, 8, 6)
        qf, kf, vf, qb, kb, vb, ldf_ref, ldb_ref, rpf_ref, rpb_ref, dof_ref, dob_ref = ins
        dqf, dkf, dvf, dqb, dkb, dvb, dldf_ref, dldb_ref = outs
        cg.ride(refs, 12, 8, nch)
        lds = (ldf_ref, ldb_ref)

        @pl.when(pl.program_id(0) == 0)
        def _():
            for r in (drf_s, drb_s, gdm_s, gxz_s):
                r[...] = jnp.zeros_like(r)
            for d in range(2):
                for h in range(RH):
                    dm, xi, zeta, cdec = _ret_tables(lds[d][h:h + 1, 0:1], bool(d))
                    dm_s[d, h] = dm
                    for k, tab in enumerate((xi, zeta, cdec)):
                        xz_s[d, h, k] = jnp.broadcast_to(tab, (T, DH))

        for h in range(RH):
            sl = slice(h * DH, (h + 1) * DH)
            for d, (q_ref, k_ref, v_ref, rp_ref, do_ref, dq_ref, dk_ref, dv_ref, dr_s) in enumerate((
                    (qf, kf, vf, rpf_ref, dof_ref, dqf, dkf, dvf, drf_s), (qb, kb, vb, rpb_ref, dob_ref, dqb, dkb, dvb, drb_s))):
                _, vjp = jax.vjp(_ret_apply, q_ref[:, sl].astype(f32), k_ref[:, sl].astype(f32), v_ref[:, sl].astype(f32),
                                 rp_ref[0, h], dm_s[d, h], xz_s[d, h, 0], xz_s[d, h, 1], xz_s[d, h, 2])
                dqr, dkr, dv, drp, gdm, gxi, gzeta, gcdec = vjp((do_ref[:, sl], dr_s[h]))
                dr_s[h] = drp
                dq_ref[:, sl], dk_ref[:, sl], dv_ref[:, sl] = dqr, dkr, dv
                gdm_s[d, h] += gdm
                for k, g in enumerate((gxi, gzeta, gcdec)):
                    gxz_s[d, h, k] += g

        @pl.when(pl.program_id(0) == nch - 1)
        def _():
            for d, dld_ref in enumerate((dldf_ref, dldb_ref)):
                dld_ref[...] = jnp.zeros_like(dld_ref)
                for h in range(RH):
                    _, vjp = jax.vjp(functools.partial(_ret_tables, rev=bool(d)), lds[d][h:h + 1, 0:1])
                    lanes = lambda a: jnp.sum(a, axis=1, keepdims=True)
                    (dld,) = vjp((gdm_s[d, h], lanes(gxz_s[d, h, 0]), lanes(gxz_s[d, h, 1]),
                                  jnp.sum(lanes(gxz_s[d, h, 2]), axis=0, keepdims=True)))
                    dld_ref[h:h + 1, :] = jnp.broadcast_to(dld, (1, 128))

    fcol = lambda cb: pl.BlockSpec((T, RW), lambda j, _c=cb: (idf(j), _c))
    bcol = lambda cb: pl.BlockSpec((T, RW), lambda j, _c=cb: (idb(j), _c))
    rspec = pl.BlockSpec((1, RH, DH, DH), lambda j: (nch - 1 - j, 0, 0, 0))
    oshape = jax.ShapeDtypeStruct((n, RW), f32)
    return pl.pallas_call(
        body, name=name, grid=(nch,),
        in_specs=[fcol(1), fcol(2), fcol(3), bcol(1), bcol(2), bcol(3), _const_spec((8, 128)), _const_spec((8, 128)), rspec, rspec,
                  fcol(0), bcol(0)] + cg.in_specs(),
        out_specs=[fcol(0), fcol(0), fcol(0), bcol(0), bcol(0), bcol(0), _acc_spec((8, 128)), _acc_spec((8, 128))] + cg.in_specs(),
        out_shape=[oshape] * 6 + [jax.ShapeDtypeStruct((8, 128), f32)] * 2 + cg.out_shapes(),
        scratch_shapes=[pltpu.VMEM((RH, DH, DH), f32)] * 2 + [pltpu.VMEM((2, RH, T, T), f32), pltpu.VMEM((2, RH, 3, T, DH), f32)] * 2
        + cg.sems(),
        compiler_params=_CP(dimension_semantics=_ARB),
    )(p_ext, p_ext, p_ext, p_ext, p_ext, p_ext, ld8_f, ld8_b, rp_f, rp_b, do_ext, do_ext, *cg.arrays)


def _qk_heads(p, fn_q, fn_k):
    heads = lambda base, fn: [fn(p[:, base + h * DH:base + (h + 1) * DH]) for h in range(RH)]
    return jnp.concatenate([p[:, :S5W]] + heads(S5W, fn_q) + heads(S5W + RW, fn_k) + [p[:, S5W + 2 * RW:]], axis=1)


def _f1_fwd(x, ctx, modx, modc, nw1, w_in_n, cosf, sins, name, cargo=None):
    L = x.shape[0]
    nb = L // R + 1
    scale = DH ** -0.5
    cg = _Cargo(cargo)

    def body(*refs):
        (x_ref, c_ref, mx_ref, mc_ref, nw_ref, w_ref, cos_ref, sin_ref), (p_ref,), _ = cg.split(refs, 8, 1, 0)
        cg.ride(refs, 8, 1, nb)
        is_ctx = pl.program_id(0) == 0
        xin = jnp.where(is_ctx, c_ref[...], x_ref[...])
        sh = jnp.where(is_ctx, mc_ref[0:1], mx_ref[0:1])
        sc = jnp.where(is_ctx, mc_ref[1:2], mx_ref[1:2])
        cf, ss = cos_ref[...], sin_ref[...]
        p = dnn(_mod(_rms(xin, nw_ref[...]), sh, sc), w_ref[...])
        p_ref[...] = _qk_heads(p, lambda t: _rot(t, cf, ss), lambda t: _rot(t * scale, cf, ss)).astype(bf16)

    return pl.pallas_call(
        body, name=name, grid=(nb,),
        in_specs=[pl.BlockSpec((R, D), lambda i: (jnp.maximum(i - 1, 0), 0)), _const_spec((R, D)), _const_spec((6, D)),
                  _const_spec((6, D)), _const_spec((1, D)), _const_spec((D, INC)), pl.BlockSpec((R, DH), lambda i: (i, 0)),
                  pl.BlockSpec((R, DH), lambda i: (i, 0))] + cg.in_specs(),
        out_specs=[pl.BlockSpec((R, INC), lambda i: (i, 0))] + cg.in_specs(),
        out_shape=[jax.ShapeDtypeStruct((L + R, INC), bf16)] + cg.out_shapes(),
        scratch_shapes=cg.sems(),
        compiler_params=_CP(dimension_semantics=_ARB),
    )(x, ctx, modx, modc, nw1, w_in_n, cosf, sins, *cg.arrays)


def _f1_bwd(x, ctx, modx, modc, nw1, w_in_t, cosf, sins, dx1, parts, name, cargo=None):
    L = x.shape[0]
    nb = L // R + 1
    scale = DH ** -0.5
    cg = _Cargo(cargo)

    def body(*refs):
        ins, (gx_ref, dp_ref, h1_ref, dnw_ref, dmx_ref, dmc_ref), _ = cg.split(refs, 18, 6, 0)
        x_ref, c_ref, mx_ref, mc_ref, nw_ref, w_ref, cos_ref, sin_ref, dx1_ref, du0, du1, dq0, dq1, dk0, dk1, dv0, dv1, dg0 = ins
        cg.ride(refs, 18, 6, nb)
        i = pl.program_id(0)
        is_ctx = i == 0

        @pl.when(is_ctx)
        def _():
            dnw_ref[...] = jnp.zeros_like(dnw_ref)
            dmx_ref[...] = jnp.zeros_like(dmx_ref)
            dmc_ref[...] = jnp.zeros_like(dmc_ref)

        cf, ss = cos_ref[...], sin_ref[...]
        dp = jnp.concatenate([du0[...].astype(f32) + du1[...], dq0[...] + dq1[...], dk0[...] + dk1[...], dv0[...] + dv1[...],
                              dg0[...]], axis=1)
        dp = _qk_heads(dp, lambda t: _rot_t(t, cf, ss), lambda t: _rot_t(t, cf, ss) * scale).astype(bf16)
        dp_ref[...] = dp
        xin = jnp.where(is_ctx, c_ref[...], x_ref[...])
        sh = jnp.where(is_ctx, mc_ref[0:1], mx_ref[0:1])
        sc = jnp.where(is_ctx, mc_ref[1:2], mx_ref[1:2])
        dh = dnn(dp, w_ref[...])
        h, vjp = jax.vjp(lambda a, b, c, d: _mod(_rms(a, b), c, d), xin, nw_ref[...], sh, sc)
        dxin, dnw, dsh, dsc = vjp(dh)
        h1_ref[...] = h.astype(bf16)
        gx_ref[...] = dx1_ref[...] + dxin
        dnw_ref[...] += dnw
        wx = jnp.where(is_ctx, 0.0, 1.0)
        dmx_ref[0:1] += dsh * wx
        dmx_ref[1:2] += dsc * wx
        dmc_ref[0:1] += dsh * (1.0 - wx)
        dmc_ref[1:2] += dsc * (1.0 - wx)

    lat = pl.BlockSpec((R, D), lambda i: (jnp.maximum(i - 1, 0), 0))
    ext = pl.BlockSpec((R, S5W), lambda i: (i, 0))
    return pl.pallas_call(
        body, name=name, grid=(nb,),
        in_specs=[lat, _const_spec((R, D)), _const_spec((6, D)), _const_spec((6, D)), _const_spec((1, D)), _const_spec((INC, D)),
                  pl.BlockSpec((R, DH), lambda i: (i, 0)), pl.BlockSpec((R, DH), lambda i: (i, 0)), lat] + [ext] * 9 + cg.in_specs(),
        out_specs=[lat, pl.BlockSpec((R, INC), lambda i: (i, 0)), pl.BlockSpec((R, D), lambda i: (i, 0)),
                   _acc_spec((1, D)), _acc_spec((6, D)), _acc_spec((6, D))] + cg.in_specs(),
        out_shape=[jax.ShapeDtypeStruct((L, D), f32), jax.ShapeDtypeStruct((L + R, INC), bf16),
                   jax.ShapeDtypeStruct((L + R, D), bf16), jax.ShapeDtypeStruct((1, D), f32),
                   jax.ShapeDtypeStruct((6, D), f32), jax.ShapeDtypeStruct((6, D), f32)] + cg.out_shapes(),
        scratch_shapes=cg.sems(),
        compiler_params=_CP(dimension_semantics=_ARB),
    )(x, ctx, modx, modc, nw1, w_in_t, cosf, sins, dx1, *parts, *cg.arrays)


def _ret_post(yr, g):
    outs = []
    for h in range(RH):
        yh = yr[:, h * DH:(h + 1) * DH]
        mu = jnp.mean(yh, axis=-1, keepdims=True)
        var = jnp.mean((yh - mu) ** 2, axis=-1, keepdims=True)
        outs.append((yh - mu) * lax.rsqrt(var + EPS))
    return jax.nn.silu(g) * jnp.concatenate(outs, axis=1)


def _mix_fn(ys, u, of, ob, g, x, dvec, bglu, gate1, pz, pm, wglu, wout):
    s = _gelu(ys + dvec * u)
    z = dnn(s, wglu) + bglu + pz
    cat = jnp.concatenate([s * jax.nn.sigmoid(z), _ret_post(of + ob, g)], axis=1)
    mix = dnn(cat, wout) + pm
    return x + gate1 * mix, (s, cat)


def _mix_fwd(x, ys, of, ob, p_ext, dvec, bglu, modx, wglu, wout, name, cargo=None):
    L = x.shape[0]
    nb = L // R
    cg = _Cargo(cargo)

    def body(*refs):
        ins, (x1_ref,), _ = cg.split(refs, 11, 1, 0)
        x_ref, ys_ref, of_ref, ob_ref, u_ref, g_ref, d_ref, b_ref, mx_ref, wg_ref, wo_ref = ins
        cg.ride(refs, 11, 1, nb)
        x1_ref[...] = _mix_fn(ys_ref[...].astype(f32), u_ref[...].astype(f32), of_ref[...], ob_ref[...], g_ref[...].astype(f32),
                              x_ref[...], d_ref[...], b_ref[...], mx_ref[2:3], 0.0, 0.0, wg_ref[...], wo_ref[...])[0]

    ext = pl.BlockSpec((R, S5W), lambda i: (i + 1, 0))
    return pl.pallas_call(
        body, name=name, grid=(nb,),
        in_specs=[pl.BlockSpec((R, D), lambda i: (i, 0)), ext, ext, ext, ext, pl.BlockSpec((R, RW), lambda i: (i + 1, 4)),
                  _const_spec((1, S5W)), _const_spec((1, S5W)), _const_spec((6, D)), _const_spec((S5W, S5W)), _const_spec((D, D))]
        + cg.in_specs(),
        out_specs=[pl.BlockSpec((R, D), lambda i: (i, 0))] + cg.in_specs(),
        out_shape=[jax.ShapeDtypeStruct((L, D), f32)] + cg.out_shapes(),
        scratch_shapes=cg.sems(),
        compiler_params=_CP(dimension_semantics=_ARB),
    )(x, ys, of, ob, p_ext, p_ext, dvec, bglu, modx, wglu, wout, *cg.arrays)


def _mix_bwd(x, ys, of, ob, p_ext, dvec, bglu, modx, wglu, wout, dx1, name, cargo=None):
    L = x.shape[0]
    nb = L // R + 1
    cg = _Cargo(cargo)

    def body(*refs):
        ins, outs, _ = cg.split(refs, 12, 11, 0)
        x_ref, ys_ref, of_ref, ob_ref, u_ref, g_ref, d_ref, b_ref, mx_ref, wg_ref, wo_ref, dx1_ref = ins
        dy_ref, dud_ref, do_ref, dg_ref, cat_ref, dmix_ref, s_ref, dz_ref, dd_ref, db_ref, dg1_ref = outs
        cg.ride(refs, 12, 11, nb)
        i = pl.program_id(0)

        @pl.when(i == 0)
        def _():
            for r in outs:
                r[...] = jnp.zeros_like(r)

        @pl.when(i > 0)
        def _():
            fn = lambda ys_, u_, of_, g_, d_, b_, g1_, pz_, pm_: _mix_fn(
                ys_, u_, of_, ob_ref[...], g_, x_ref[...], d_, b_, g1_, pz_, pm_, wg_ref[...], wo_ref[...])
            _, vjp, (s, cat) = jax.vjp(fn, ys_ref[...].astype(f32), u_ref[...].astype(f32), of_ref[...], g_ref[...].astype(f32), d_ref[...],
                                       b_ref[...], mx_ref[2:3], jnp.zeros((R, S5W), f32), jnp.zeros((R, D), f32), has_aux=True)
            dy, dud, do, dg, dd, db, dg1, dz, dmix = vjp(dx1_ref[...])
            dy_ref[...], dud_ref[...], do_ref[...], dg_ref[...] = dy.astype(bf16), dud, do, dg
            cat_ref[...], dmix_ref[...] = cat.astype(bf16), dmix.astype(bf16)
            s_ref[...], dz_ref[...] = s.astype(bf16), dz.astype(bf16)
            dd_ref[...] += dd
            db_ref[...] += db
            dg1_ref[...] += dg1

    lat = pl.BlockSpec((R, D), lambda i: (jnp.maximum(i - 1, 0), 0))
    lat5 = pl.BlockSpec((R, S5W), lambda i: (jnp.maximum(i - 1, 0), 0))
    ext = pl.BlockSpec((R, S5W), lambda i: (i, 0))
    eshape = jax.ShapeDtypeStruct((L + R, S5W), f32)
    return pl.pallas_call(
        body, name=name, grid=(nb,),
        in_specs=[lat, ext, ext, ext, ext, pl.BlockSpec((R, RW), lambda i: (i, 4)),
                  _const_spec((1, S5W)), _const_spec((1, S5W)), _const_spec((6, D)), _const_spec((S5W, S5W)), _const_spec((D, D)), lat]
        + cg.in_specs(),
        out_specs=[ext, ext, ext, ext, lat, lat, lat5, lat5, _acc_spec((1, S5W)), _acc_spec((1, S5W)), _acc_spec((1, D))]
        + cg.in_specs(),
        out_shape=[jax.ShapeDtypeStruct((L + R, S5W), bf16), eshape, eshape, eshape, jax.ShapeDtypeStruct((L, D), bf16),
                   jax.ShapeDtypeStruct((L, D), bf16), jax.ShapeDtypeStruct((L, S5W), bf16), jax.ShapeDtypeStruct((L, S5W), bf16),
                   jax.ShapeDtypeStruct((1, S5W), f32), jax.ShapeDtypeStruct((1, S5W), f32), jax.ShapeDtypeStruct((1, D), f32)]
        + cg.out_shapes(),
        scratch_shapes=cg.sems(),
        compiler_params=_CP(dimension_semantics=_ARB),
    )(x, ys, of, ob, p_ext, p_ext, dvec, bglu, modx, wglu, wout, dx1, *cg.arrays)


def _ffn_tail(gc, a, x1, gate2, fnw, pf, wdown, wdown_t, tgt):
    f = _gelu(gc) * a
    ffn = _dnn_const(f, wdown, wdown_t) + pf
    y = _rms(x1 + gate2 * ffn, fnw)
    err = y - tgt
    loss = 0.5 * jnp.sum(jnp.mean(err * err, axis=-1, keepdims=True), axis=0, keepdims=True)
    return loss, f


def _ffn_fwd(x1, tgt, nw2, modx, w_a, w_g, cw, cb, wdown, wdown_t, fnw, name):
    L = x1.shape[0]
    nb = L // RF
    per = RF // HALO

    def body(x_ref, xp_ref, xn_ref, t_ref, nw_ref, mx_ref, wa_ref, wg_ref, cw_ref, cb_ref, wd_ref, wdt_ref, fn_ref,
             dx2_ref, da_ref, dgc_ref, f_ref, dffn_ref, loss_ref, dfn_ref, dg2_ref, dcb_ref, dcw_ref):
        i = pl.program_id(0)

        @pl.when(i == 0)
        def _():
            for r in (loss_ref, dfn_ref, dg2_ref, dcb_ref, dcw_ref):
                r[...] = jnp.zeros_like(r)

        nw, sh, sc, gate2 = nw_ref[...], mx_ref[3:4], mx_ref[4:5], mx_ref[5:6]
        x1b = x_ref[...]
        h2 = _mod(_rms(x1b, nw), sh, sc)
        h2e = jnp.concatenate([_mod(_rms(xp_ref[...], nw), sh, sc), h2, _mod(_rms(xn_ref[...], nw), sh, sc)], axis=0)
        a = dnn(h2, wa_ref[...])
        ge = dnn(h2e, wg_ref[...])
        g = ge[HALO:HALO + RF]
        gp = ge[HALO - 1:HALO] * jnp.where(i > 0, 1.0, 0.0)
        gn = ge[HALO + RF:HALO + RF + 1] * jnp.where(i < nb - 1, 1.0, 0.0)
        row = lax.broadcasted_iota(jnp.int32, (RF, 1), 0)
        g_prev = jnp.where(row == 0, gp, pltpu.roll(g, 1, axis=0))
        g_next = jnp.where(row == RF - 1, gn, pltpu.roll(g, RF - 1, axis=0))
        gc = cb_ref[...] + g_prev * cw_ref[0:1] + g * cw_ref[1:2] + g_next * cw_ref[2:3]
        fn = lambda gc_, a_, x_, g2_, fw_, pf_: _ffn_tail(gc_, a_, x_, g2_, fw_, pf_, wd_ref[...], wdt_ref[...], t_ref[...])
        loss, vjp, f = jax.vjp(fn, gc, a, x1b, gate2, fn_ref[...], jnp.zeros((RF, D), f32), has_aux=True)
        dgc, da, dx2, dg2, dfw, dffn = vjp(jnp.ones((1, 1), f32))
        dx2_ref[...] = dx2
        da_ref[...], dgc_ref[...] = da.astype(bf16), dgc
        f_ref[...], dffn_ref[...] = f.astype(bf16), dffn.astype(bf16)
        loss_ref[...] += jnp.broadcast_to(loss, (1, 128))
        dfn_ref[...] += dfw
        dg2_ref[...] += dg2
        dcb_ref[...] += jnp.sum(dgc, axis=0, keepdims=True)
        dcw_ref[0:1] += jnp.sum(dgc * g_prev, axis=0, keepdims=True)
        dcw_ref[1:2] += jnp.sum(dgc * g, axis=0, keepdims=True)
        dcw_ref[2:3] += jnp.sum(dgc * g_next, axis=0, keepdims=True)

    blk = lambda w: pl.BlockSpec((RF, w), lambda i: (i, 0))
    return pl.pallas_call(
        body, name=name, grid=(nb,),
        in_specs=[blk(D), pl.BlockSpec((HALO, D), lambda i: (jnp.maximum(i * per - 1, 0), 0)),
                  pl.BlockSpec((HALO, D), lambda i: (jnp.minimum((i + 1) * per, L // HALO - 1), 0)), blk(D),
                  _const_spec((1, D)), _const_spec((6, D)), _const_spec((D, DFF)), _const_spec((D, DFF)), _const_spec((3, DFF)),
                  _const_spec((1, DFF)), _const_spec((DFF, D)), _const_spec((D, DFF)), _const_spec((1, D))],
        out_specs=[blk(D), blk(DFF), blk(DFF), blk(DFF), blk(D), _acc_spec((1, 128)), _acc_spec((1, D)), _acc_spec((1, D)),
                   _acc_spec((1, DFF)), _acc_spec((3, DFF))],
        out_shape=[jax.ShapeDtypeStruct((L, D), f32), jax.ShapeDtypeStruct((L, DFF), bf16), jax.ShapeDtypeStruct((L, DFF), f32),
                   jax.ShapeDtypeStruct((L, DFF), bf16), jax.ShapeDtypeStruct((L, D), bf16), jax.ShapeDtypeStruct((1, 128), f32),
                   jax.ShapeDtypeStruct((1, D), f32), jax.ShapeDtypeStruct((1, D), f32), jax.ShapeDtypeStruct((1, DFF), f32),
                   jax.ShapeDtypeStruct((3, DFF), f32)],
        compiler_params=_CP(dimension_semantics=_ARB),
    )(x1, x1, x1, tgt, nw2, modx, w_a, w_g, cw, cb, wdown, wdown_t, fnw)


def _ffn_bwd(x1, dx2, da, dgc, nw2, modx, wup_t, cw, name):
    L = x1.shape[0]
    nb = L // RF
    per = RF // HALO

    def body(x_ref, dx2_ref, da_ref, dgc_ref, dgp_ref, dgn_ref, nw_ref, mx_ref, wu_ref, cw_ref,
             dx1_ref, dag_ref, h2_ref, dnw_ref, dmx_ref):
        i = pl.program_id(0)

        @pl.when(i == 0)
        def _():
            dnw_ref[...] = jnp.zeros_like(dnw_ref)
            dmx_ref[...] = jnp.zeros_like(dmx_ref)

        dgc_b = dgc_ref[...]
        before = dgp_ref[HALO - 1:HALO] * jnp.where(i > 0, 1.0, 0.0)
        after = dgn_ref[0:1] * jnp.where(i < nb - 1, 1.0, 0.0)
        row = lax.broadcasted_iota(jnp.int32, (RF, 1), 0)
        d_prev = jnp.where(row == 0, before, pltpu.roll(dgc_b, 1, axis=0))
        d_next = jnp.where(row == RF - 1, after, pltpu.roll(dgc_b, RF - 1, axis=0))
        dg = cw_ref[0:1] * d_next + cw_ref[1:2] * dgc_b + cw_ref[2:3] * d_prev
        dag = jnp.concatenate([da_ref[...], dg.astype(bf16)], axis=1)
        dag_ref[...] = dag
        dh2 = dnn(dag, wu_ref[...])
        h2, vjp = jax.vjp(lambda a, b, c, d: _mod(_rms(a, b), c, d), x_ref[...], nw_ref[...], mx_ref[3:4], mx_ref[4:5])
        dxa, dnw, dsh, dsc = vjp(dh2)
        h2_ref[...] = h2.astype(bf16)
        dx1_ref[...] = dx2_ref[...] + dxa
        dnw_ref[...] += dnw
        dmx_ref[3:4] += dsh
        dmx_ref[4:5] += dsc

    blk = lambda w: pl.BlockSpec((RF, w), lambda i: (i, 0))
    return pl.pallas_call(
        body, name=name, grid=(nb,),
        in_specs=[blk(D), blk(D), blk(DFF), blk(DFF), pl.BlockSpec((HALO, DFF), lambda i: (jnp.maximum(i * per - 1, 0), 0)),
                  pl.BlockSpec((HALO, DFF), lambda i: (jnp.minimum((i + 1) * per, L // HALO - 1), 0)),
                  _const_spec((1, D)), _const_spec((6, D)), _const_spec((2 * DFF, D)), _const_spec((3, DFF))],
        out_specs=[blk(D), blk(2 * DFF), blk(D), _acc_spec((1, D)), _acc_spec((6, D))],
        out_shape=[jax.ShapeDtypeStruct((L, D), f32), jax.ShapeDtypeStruct((L, 2 * DFF), bf16), jax.ShapeDtypeStruct((L, D), bf16),
                   jax.ShapeDtypeStruct((1, D), f32), jax.ShapeDtypeStruct((6, D), f32)],
        compiler_params=_CP(dimension_semantics=_ARB),
    )(x1, dx2, da, dgc, dgc, dgc, nw2, modx, wup_t, cw)


def _matmul_tn(a, b, name, cargo=None):
    k, m = a.shape
    n = b.shape[1]
    divs = lambda d: [c for c in range(d, 0, -128) if d % c == 0]
    _, tm, tn = min((m * (n // cn) + n * (m // cm), cm, cn) for cm in divs(m) for cn in divs(n) if cm * cn * 4 <= ACC_TILE_BYTES)
    tk = next(c for c in (512, 768, 256, 128) if k % c == 0)
    nk = k // tk
    grid = (m // tm, n // tn, nk)
    cg = _Cargo(cargo)

    def body(*refs):
        (a_ref, b_ref), (o_ref,), (acc,) = cg.split(refs, 2, 1, 1)
        cg.ride(refs, 2, 1, grid)
        q = pl.program_id(2)

        @pl.when(q == 0)
        def _():
            acc[...] = jnp.zeros_like(acc)

        acc[...] += dtn(a_ref[...], b_ref[...])

        @pl.when(q == nk - 1)
        def _():
            o_ref[...] = acc[...].astype(bf16)

    out = pl.pallas_call(
        body, name=name, grid=grid,
        in_specs=[pl.BlockSpec((tk, tm), lambda i, j, q: (q, i)), pl.BlockSpec((tk, tn), lambda i, j, q: (q, j))] + cg.in_specs(),
        out_specs=[pl.BlockSpec((tm, tn), lambda i, j, q: (i, j))] + cg.in_specs(),
        out_shape=[jax.ShapeDtypeStruct((m, n), bf16)] + cg.out_shapes(),
        scratch_shapes=[pltpu.VMEM((tm, tn), f32)] + cg.sems(),
        compiler_params=_CP(dimension_semantics=("arbitrary",) * 3 if cg.n else ("parallel", "parallel", "arbitrary")),
    )(a, b, *cg.arrays)
    return out if cg.n else out[0]


def _adamw_refs(w_ref, g_ref, m_ref, v_ref, d_ref, nm_ref, nv_ref):
    c1, c2 = 1.0 - B1 ** STEP, 1.0 - B2 ** STEP
    gg = g_ref[...]
    nm = B1 * m_ref[...] + (1.0 - B1) * gg
    nv = B2 * v_ref[...] + (1.0 - B2) * jnp.square(gg)
    d_ref[...] = -LR * ((nm / c1) / (jnp.sqrt(nv / c2) + AEPS) + WD * w_ref[...])
    nm_ref[...], nv_ref[...] = nm, nv


def _adamw(w, g, m, v, name):
    def body(*refs):
        _adamw_refs(*refs)

    return pl.pallas_call(body, name=name, out_shape=[jax.ShapeDtypeStruct(w.shape, f32)] * 3, compiler_params=_CP())(w, g, m, v)


def _adamw_landed(land, w, m, v, name):
    def body(l_ref, w_ref, m_ref, v_ref, g_ref, d_ref, nm_ref, nv_ref):
        acc = l_ref[0].astype(f32)
        for j in range(1, NDEV):
            acc = acc + l_ref[j].astype(f32)
        g_ref[...] = acc
        _adamw_refs(w_ref, g_ref, m_ref, v_ref, d_ref, nm_ref, nv_ref)

    return pl.pallas_call(body, name=name, out_shape=[jax.ShapeDtypeStruct(w.shape, f32)] * 4, compiler_params=_CP())(land, w, m, v)


def _adamw_many(ws, gs, ms, vs, name):
    n = len(ws)

    def body(*refs):
        for k in range(n):
            _adamw_refs(*[refs[j * n + k] for j in range(7)])

    outs = pl.pallas_call(body, name=name, out_shape=[jax.ShapeDtypeStruct(w.shape, f32) for w in ws] * 3,
                          compiler_params=_CP())(*ws, *gs, *ms, *vs)
    return outs[:n], outs[n:2 * n], outs[2 * n:]


SMALL = ["conv_w", "c_ctx", "norm1_w", "s5_lambda_re_f", "s5_lambda_im_f", "s5_log_step_f", "s5_lambda_re_b", "s5_lambda_im_b",
         "s5_log_step_b", "s5_b_re", "s5_b_im", "s5_c_re", "s5_c_im", "s5_d", "s5_b_glu", "ret_log_decay_f", "ret_log_decay_b",
         "norm2_w", "conv_b", "final_norm_w"]
WEIGHTS = ["c_ctx", "w_mod", "b_mod", "norm1_w", "w_in", "s5_lambda_re_f", "s5_lambda_im_f", "s5_log_step_f", "s5_lambda_re_b",
           "s5_lambda_im_b", "s5_log_step_b", "s5_b_re", "s5_b_im", "s5_c_re", "s5_c_im", "s5_d", "s5_w_glu", "s5_b_glu",
           "ret_log_decay_f", "ret_log_decay_b", "w_out", "norm2_w", "w_up", "conv_w", "conv_b", "w_down", "final_norm_w"]


def _pack_small(vals):
    flat, offs, o = [], [], 0
    for a in vals:
        n = a.size
        npad = -n % 128
        flat.append(jnp.pad(a.reshape(-1), (0, npad)))
        offs.append((o, n))
        o += n + npad
    tail = -o % 1024
    if tail:
        flat.append(jnp.zeros((tail,), f32))
    return jnp.concatenate(flat).reshape(-1, 128), offs


def _unpack_small(packed, offs, shapes):
    flat = packed.reshape(-1)
    return [flat[o:o + n].reshape(s) for (o, n), s in zip(offs, shapes)]


def _rope_tables(L, nctx_rows):
    t = np.arange(L)
    inv = (ROPE_THETA ** (-np.arange(DH // 4, dtype=np.float64) / (DH // 4))).astype(np.float32)
    ang = np.concatenate([(t // GRID_W).astype(np.float32)[:, None] * inv, (t % GRID_W).astype(np.float32)[:, None] * inv], axis=-1)
    cos = np.repeat(np.cos(ang).astype(np.float32), 2, axis=1)
    sin = np.repeat(np.sin(ang).astype(np.float32), 2, axis=1) * np.tile(np.array([-1.0, 1.0], np.float32), DH // 2)
    cosf = np.concatenate([np.ones((nctx_rows, DH), np.float32), cos], axis=0)
    sins = np.concatenate([np.zeros((nctx_rows, DH), np.float32), sin], axis=0)
    return jnp.asarray(cosf), jnp.asarray(sins)


def kernel(x, c, ctx, c_ctx, w_mod, b_mod, norm1_w, w_in, s5_lambda_re_f, s5_lambda_im_f, s5_log_step_f, s5_lambda_re_b, s5_lambda_im_b, s5_log_step_b, s5_b_re, s5_b_im, s5_c_re, s5_c_im, s5_d, s5_w_glu, s5_b_glu, ret_log_decay_f, ret_log_decay_b, w_out, norm2_w, w_up, conv_w, conv_b, w_down, final_norm_w, loss_target, m_c_ctx, m_w_mod, m_b_mod, m_norm1_w, m_w_in, m_s5_lambda_re_f, m_s5_lambda_im_f, m_s5_log_step_f, m_s5_lambda_re_b, m_s5_lambda_im_b, m_s5_log_step_b, m_s5_b_re, m_s5_b_im, m_s5_c_re, m_s5_c_im, m_s5_d, m_s5_w_glu, m_s5_b_glu, m_ret_log_decay_f, m_ret_log_decay_b, m_w_out, m_norm2_w, m_w_up, m_conv_w, m_conv_b, m_w_down, m_final_norm_w, v_c_ctx, v_w_mod, v_b_mod, v_norm1_w, v_w_in, v_s5_lambda_re_f, v_s5_lambda_im_f, v_s5_log_step_f, v_s5_lambda_re_b, v_s5_lambda_im_b, v_s5_log_step_b, v_s5_b_re, v_s5_b_im, v_s5_c_re, v_s5_c_im, v_s5_d, v_s5_w_glu, v_s5_b_glu, v_ret_log_decay_f, v_ret_log_decay_b, v_w_out, v_norm2_w, v_w_up, v_conv_w, v_conv_b, v_w_down, v_final_norm_w):
    args = dict(locals())
    W = {n: args[n] for n in WEIGHTS}
    M = {n: args["m_" + n] for n in WEIGHTS}
    V = {n: args["v_" + n] for n in WEIGHTS}
    me = _me()
    x2, ctx2, tgt = x[0], ctx[0], loss_target[0]
    L, Lc = x2.shape[0], ctx2.shape[0]
    assert Lc == R and L % R == 0 and L % GRID_W == 0
    nctx = Lc // T

    w_in_tl, w_up_tl = w_in[0].T.astype(bf16), w_up[0].T.astype(bf16)
    w_out_l, w_down_l, w_glu_l = w_out[0].astype(bf16), w_down[0].astype(bf16), s5_w_glu[0].astype(bf16)
    per_cv = conv_w.shape[2]
    conv_pad = jnp.pad(conv_w[0], ((0, 5), (0, 128 * 3 - per_cv)))
    w_in_g, c_g, conv_g = _gather_two_level([w_in_tl, jnp.pad(c, ((0, 7), (0, 0))), conv_pad], "gather_w_in")
    w_in_t = w_in_g.reshape(INC, D)
    conv_f = conv_g[:, :3, :per_cv].transpose(1, 0, 2).reshape(3, DFF)

    c9 = jnp.concatenate([c_g[:, 0, :], c_ctx[None], jnp.zeros((7, D), f32)], axis=0)
    w_mod_l = w_mod[0]
    ncol = w_mod_l.shape[1]
    m_part = _ada_fwd(c9, w_mod_l, "ada_fwd")
    m_all = _all_gather_small(m_part, "gather_mod").transpose(1, 0, 2).reshape(16, 6, D)
    modx, modc = _mod_select(m_all, b_mod.reshape(6, D), "mod_select")

    pair = lambda a, b: jnp.concatenate([a, b], axis=-1)
    bre_g, bim_g = s5_b_re[0].transpose(0, 2, 1), s5_b_im[0].transpose(0, 2, 1)
    cre_g, cim_g = s5_c_re[0], s5_c_im[0]
    shared = (pair(bre_g, bim_g), pair(bim_g, bre_g), pair(cre_g, cim_g), pair(cim_g, cre_g))
    s5p = {}
    for tag, lre, lim, ls in (("f", s5_lambda_re_f, s5_lambda_im_f, s5_log_step_f), ("b", s5_lambda_re_b, s5_lambda_im_b, s5_log_step_b)):
        s5p[tag] = (pair(lre[0], lre[0])[:, None, :], pair(lim[0], lim[0])[:, None, :], ls[0].reshape(S5G, 1, 1)) + shared
    m_f, mb_f, mc_f, a1_f, a2_f = _s5_build(s5p["f"], False, "s5_build_f")
    m_b, mb_b, mc_b, a1_b, a2_b = _s5_build(s5p["b"], True, "s5_build_b")
    a1_f, a2_f, a1_b, a2_b = (a.reshape(S5G, SB) for a in (a1_f, a2_f, a1_b, a2_b))

    nw1, nw2, fnw = norm1_w, norm2_w, final_norm_w[None]
    cosf, sins = _rope_tables(L, Lc)
    p_ext, w_out_g, w_glu_g, w_up_g1 = _f1_fwd(x2, ctx2, modx, modc, nw1, w_in_t.T, cosf, sins, "f1_fwd",
                                               cargo=([w_out_l, w_glu_l, w_up_tl[:UP_HEAD]], False))
    nctx5 = Lc // TC
    u_g = _to_groups(p_ext[:, :S5W])
    s_f, s_b = _s5_inc(u_g, mb_f, mb_b, "s5_inc")
    hp_f, hp_b = _s5_carry(s_f, s_b, (a1_f, a2_f), (a1_b, a2_b), nctx5, "s5_carry")
    ys = _from_groups(_s5_out(u_g, m_f, m_b, hp_f, hp_b, mc_f, mc_b, "s5_out"))
    ld8 = lambda ld: jnp.pad(jnp.broadcast_to(ld[0][:, None], (RH, 128)), ((0, 8 - RH), (0, 0)))
    ldf8, ldb8 = ld8(ret_log_decay_f), ld8(ret_log_decay_b)
    of, ob, rp_f, rp_b, w_up_g2 = _ret_fwd(p_ext, ldf8, ldb8, nctx, "ret_fwd", cargo=([w_up_tl[UP_HEAD:]], False))
    w_out_f, w_glu_f = w_out_g.reshape(D, D), w_glu_g.reshape(S5W, S5W)
    x1, w_down_g = _mix_fwd(x2, ys, of, ob, p_ext, s5_d, s5_b_glu, modx, w_glu_f, w_out_f, "mix_fwd", cargo=([w_down_l], False))
    w_down_f = w_down_g.reshape(DFF, D)
    w_up_t = jnp.concatenate([w_up_g1, w_up_g2], axis=1).reshape(2 * DFF, D)

    (dx2, da, dgc, f_act, dffn, loss_acc, g_fnw, g_gate2, g_cb, g_cw) = _ffn_fwd(
        x1, tgt, nw2, modx, w_up_t[:DFF].T, w_up_t[DFF:].T, conv_f, conv_b, w_down_f, w_down_f.T, fnw, "ffn_fwd")
    dx1, dag, h2, g_nw2, dmx2 = _ffn_bwd(x1, dx2, da, dgc, nw2, modx, w_up_t, conv_f, "ffn_bwd")
    gw_down = _matmul_tn(f_act, dffn, "dw_down").reshape(NDEV, -1, D)
    gw_up_t = _matmul_tn(dag, h2, "dw_up").reshape(NDEV, -1, D)
    (dy_e, dud_e, do_e, dg_e, cat, dmix, s_act, dz, g_d, g_bglu, g_gate1, l_down) = _mix_bwd(
        x2, ys, of, ob, p_ext, s5_d, s5_b_glu, modx, w_glu_f, w_out_f, dx1, "mix_bwd", cargo=([gw_down], True))
    gw_out = _matmul_tn(cat, dmix, "dw_out").reshape(NDEV, -1, D)
    gw_glu = _matmul_tn(s_act, dz, "dw_glu").reshape(NDEV, -1, S5W)
    dq_f, dk_f, dv_f, dq_b, dk_b, dv_b, gld_f, gld_b, l_up, l_out, l_glu = _ret_bwd(
        p_ext, ldf8, ldb8, rp_f, rp_b, do_e, nctx, "ret_bwd", cargo=([gw_up_t, gw_out, gw_glu], True))

    du1, g_m, dhp_f, dhp_b, dmc_f, dmc_b = _s5_out_bwd(_to_groups(dy_e), u_g, m_f, m_b, hp_f, hp_b, mc_f, mc_b, "s5_out_bwd")
    ds_f, da1_f, da2_f = _s5_carry_bwd(dhp_f, hp_f, a1_f, a2_f, False, nctx5, "s5_carry_bwd_f")
    ds_b, da1_b, da2_b = _s5_carry_bwd(dhp_b, hp_b, a1_b, a2_b, True, nctx5, "s5_carry_bwd_b")
    du_g, dmb_f, dmb_b = _s5_inc_bwd(du1, u_g, ds_f, ds_b, mb_f, mb_b, "s5_inc_bwd")
    zero_p = jnp.zeros((S5G, S5P, SB), f32)
    gf = _s5_build_bwd(s5p["f"], (g_m, dmb_f, dmc_f, da1_f[:, None, :], da2_f[:, None, :]), (zero_p, zero_p), False, "s5_build_bwd_f")
    gb = _s5_build_bwd(s5p["b"], (g_m, dmb_b, dmc_b, da1_b[:, None, :], da2_b[:, None, :]), (gf[3], gf[4]), True, "s5_build_bwd_b")
    g_bre, g_bim = gb[3][:, :, :S5N].transpose(0, 2, 1), gb[3][:, :, S5N:].transpose(0, 2, 1)
    g_cre, g_cim = gb[4][:, :, :S5N], gb[4][:, :, S5N:]

    early = {
        "conv_w": g_cw, "s5_lambda_re_f": gf[0][:, 0, :S5N], "s5_lambda_im_f": gf[1][:, 0, :S5N],
        "s5_log_step_f": gf[2], "s5_lambda_re_b": gb[0][:, 0, :S5N], "s5_lambda_im_b": gb[1][:, 0, :S5N], "s5_log_step_b": gb[2],
        "s5_b_re": g_bre, "s5_b_im": g_bim, "s5_c_re": g_cre, "s5_c_im": g_cim, "s5_d": g_d, "s5_b_glu": g_bglu,
        "ret_log_decay_f": gld_f[:RH, 0], "ret_log_decay_b": gld_b[:RH, 0], "norm2_w": g_nw2, "conv_b": g_cb, "final_norm_w": g_fnw,
    }
    e_names = [n for n in SMALL if n in early]
    packed_e, eoffs = _pack_small([early[n].astype(f32) for n in e_names])
    grad_x, dp_ext, h1, g_nw1, dmx1, dmc1 = _f1_bwd(
        x2, ctx2, modx, modc, nw1, w_in_t, cosf, sins, dx1, (_from_groups(du_g), dud_e, dq_f, dq_b, dk_f, dk_b, dv_f, dv_b, dg_e), "f1_bwd")
    gw_in_t, land_e = _matmul_tn(dp_ext, h1, "dw_in", cargo=([packed_e], False))
    g_in_t = _reduce_scatter_two_level(gw_in_t.reshape(NDEV, -1, D), "scatter_dw_in")

    dmx = dmx1 + dmx2
    dmx = dmx.at[2].set(g_gate1[0]).at[5].set(g_gate2[0])
    dm_me = jnp.stack([dmx.reshape(-1), dmc1.reshape(-1)], axis=0)
    dm_all = _all_gather_small(dm_me.reshape(8, -1), "gather_dmod").reshape(NDEV, 2, 6 * D)
    dmx_all, dmc_all = dm_all[:, 0, :], dm_all[:, 1, :]
    my_cols = lambda a: lax.dynamic_slice(a, (0, me * ncol), (NDEV, ncol))
    gw_mod, g_bmod, dc9 = _ada_bwd(c9, dmx_all, dmc_all, my_cols(dmx_all), my_cols(dmc_all), w_mod_l, "ada_bwd")

    sshape = lambda n: (3, DFF) if n == "conv_w" else W[n].shape
    G = dict(zip(e_names, _unpack_small(_sum8(land_e, "reduce_early"), eoffs, [sshape(n) for n in e_names])))
    late = {"c_ctx": dc9[8], "norm1_w": g_nw1}
    packed_l, loffs = _pack_small([late[n].astype(f32) for n in late])
    G.update(zip(late, _unpack_small(_all_reduce_small(packed_l, "reduce_late"), loffs, [W[n].shape for n in late])))
    G["conv_w"] = lax.dynamic_slice(G["conv_w"], (0, me * per_cv), (3, per_cv))[None]
    G["b_mod"] = g_bmod.reshape(b_mod.shape)
    G["w_mod"] = gw_mod[None]
    G["w_in"] = g_in_t.T[None]
    G["w_up"] = _sum8(l_up, "sum_dw_up").T[None]

    delta, new_m, new_v = {}, {}, {}
    sm_names = SMALL[1:] + ["b_mod"]
    rows = lambda a: a.reshape(-1, a.shape[-1])
    outs = _adamw_many(*[[rows(d[n]) for n in sm_names] for d in (W, G, M, V)], "adamw_small")
    for dst, src in zip((delta, new_m, new_v), outs):
        dst.update({n: a.reshape(W[n].shape) for n, a in zip(sm_names, src)})
    for n in ["w_mod", "w_in", "w_up", "conv_w"]:
        d, nm, nv = _adamw(W[n][0], G[n][0], M[n][0], V[n][0], "adamw_" + n)
        delta[n], new_m[n], new_v[n] = d[None], nm[None], nv[None]
    for n, land in (("w_out", l_out), ("w_down", l_down), ("s5_w_glu", l_glu)):
        g, d, nm, nv = _adamw_landed(land, W[n][0], M[n][0], V[n][0], "adamw_" + n)
        G[n], delta[n], new_m[n], new_v[n] = g[None], d[None], nm[None], nv[None]

    loss = lax.psum(loss_acc[0, 0], ("x", "y", "c"))
    return (loss, grad_x[None], *[G[n] for n in WEIGHTS], *[delta[n] for n in WEIGHTS], *[new_m[n] for n in WEIGHTS],
            *[new_v[n] for n in WEIGHTS])
```

```python
import functools

import numpy as np
import jax
import jax.numpy as jnp
from jax import lax
from jax.experimental import pallas as pl
from jax.experimental.pallas import tpu as pltpu

f32, bf16 = jnp.float32, jnp.bfloat16

D = 1024
S5W, S5G, S5P, S5N = 512, 32, 16, 64
TC = 16
TCP = TC * S5P
SB = 2 * S5N
GBK = 8
UP_HEAD = 192
CARRY_UNROLL = 8
RH, DH = 4, 128
RW = RH * DH
INC = S5W + 4 * RW
DFF = 2816
T = 128
R = 256
RF = 128
HALO = 8
EPS = 1e-6
ROPE_THETA = 10000.0
GRID_W = 64
NDEV = 8
LR, B1, B2, AEPS, WD, STEP = 0.001, 0.9, 0.999, 1e-08, 0.01, 10
VMEM_LIMIT = 60 * 1024 * 1024
ACC_TILE_BYTES = 6 * 1024 * 1024
MESH = pl.DeviceIdType.MESH

_CP = functools.partial(pltpu.CompilerParams, vmem_limit_bytes=VMEM_LIMIT)
_ARB = ("arbitrary",)
_ANY = pl.BlockSpec(memory_space=pl.ANY)


def _dg(a, b, dims):
    return lax.dot_general(a.astype(bf16), b.astype(bf16), (dims, ((), ())), preferred_element_type=f32)


@jax.custom_vjp
def dnn(a, b):
    return _dg(a, b, ((1,), (0,)))


@jax.custom_vjp
def dnt(a, b):
    return _dg(a, b, ((1,), (1,)))


@jax.custom_vjp
def dtn(a, b):
    return _dg(a, b, ((0,), (0,)))


dnn.defvjp(lambda a, b: (dnn(a, b), (a, b)), lambda r, g: (dnt(g, r[1]).astype(r[0].dtype), dtn(r[0], g).astype(r[1].dtype)))
dnt.defvjp(lambda a, b: (dnt(a, b), (a, b)), lambda r, g: (dnn(g, r[1]).astype(r[0].dtype), dtn(g, r[0]).astype(r[1].dtype)))
dtn.defvjp(lambda a, b: (dtn(a, b), (a, b)), lambda r, g: (dnt(r[1], g).astype(r[0].dtype), dnn(r[0], g).astype(r[1].dtype)))


@jax.custom_vjp
def _dnn_const(a, w, wt):
    return dnn(a, w)


_dnn_const.defvjp(lambda a, w, wt: (dnn(a, w), wt), lambda wt, g: (dnn(g, wt), None, None))


_GELU_C0, _GELU_C1 = float(np.sqrt(2.0 / np.pi)), 0.044715


@jax.custom_vjp
def _gelu(x):
    return _gelu_fwd(x)[0]


def _gelu_fwd(x):
    t = jnp.tanh(_GELU_C0 * (x + _GELU_C1 * (x * x * x)))
    return x * (0.5 * (1.0 + t)), (x, t)


def _gelu_bwd(res, g):
    x, t = res
    return (g * (0.5 * (1.0 + t) + (0.5 * _GELU_C0) * x * (1.0 - t * t) * (1.0 + (3.0 * _GELU_C1) * (x * x))),)


_gelu.defvjp(_gelu_fwd, _gelu_bwd)


def _rms(t, w):
    return t * lax.rsqrt(jnp.mean(t * t, axis=-1, keepdims=True) + EPS) * w


@jax.custom_vjp
def _norm_mod(x, w, shift, scale):
    return _norm_mod_fwd(x, w, shift, scale)[0]


def _norm_mod_fwd(x, w, shift, scale):
    r = lax.rsqrt(jnp.mean(x * x, axis=-1, keepdims=True) + EPS)
    n = x * r
    return (n * w) * (1.0 + scale) + shift, (n, r, w, scale)


def _norm_mod_bwd(res, dh):
    n, r, w, scale = res
    col = jnp.sum(dh * n, axis=0, keepdims=True)
    dn = dh * (w * (1.0 + scale))
    dx = r * (dn - n * jnp.mean(dn * n, axis=-1, keepdims=True))
    return dx, col * (1.0 + scale), jnp.sum(dh, axis=0, keepdims=True), col * w


_norm_mod.defvjp(_norm_mod_fwd, _norm_mod_bwd)


def _const_spec(shape):
    n = len(shape)
    return pl.BlockSpec(shape, lambda i, _n=n: (0,) * _n, pipeline_mode=pl.Buffered(1))


def _acc_spec(shape):
    n = len(shape)
    return pl.BlockSpec(shape, lambda i, _n=n: (0,) * _n)


def _me():
    return 4 * lax.axis_index("x") + 2 * lax.axis_index("y") + lax.axis_index("c")


def _peer(r):
    x, y, c = lax.axis_index("x"), lax.axis_index("y"), lax.axis_index("c")
    px = 1 - x if (r >> 2) & 1 else x
    py = 1 - y if (r >> 1) & 1 else y
    pc = 1 - c if r & 1 else c
    return (px, py, pc), 4 * px + 2 * py + pc


def _all_gather_small(v, name):
    r, c = v.shape

    def body(v_ref, out_ref, send_sems, recv_sems):
        me = _me()
        out_ref[me] = v_ref[...]
        sends = []
        for k in range(1, NDEV):
            peer, _ = _peer(k)
            cp = pltpu.make_async_remote_copy(src_ref=v_ref, dst_ref=out_ref.at[me], send_sem=send_sems.at[k - 1],
                                              recv_sem=recv_sems.at[k - 1], device_id=peer, device_id_type=MESH)
            cp.start()
            sends.append(cp)
        for k in range(1, NDEV):
            peer, pidx = _peer(k)
            pltpu.make_async_remote_copy(src_ref=v_ref, dst_ref=out_ref.at[pidx], send_sem=send_sems.at[k - 1],
                                         recv_sem=recv_sems.at[k - 1], device_id=peer, device_id_type=MESH).wait_recv()
        for cp in sends:
            cp.wait_send()

    return pl.pallas_call(
        body, name=name, out_shape=jax.ShapeDtypeStruct((NDEV, r, c), v.dtype),
        in_specs=[pl.BlockSpec(memory_space=pltpu.VMEM)], out_specs=pl.BlockSpec(memory_space=pltpu.VMEM),
        scratch_shapes=[pltpu.SemaphoreType.DMA((NDEV - 1,)), pltpu.SemaphoreType.DMA((NDEV - 1,))],
        compiler_params=_CP(),
    )(v)


def _all_reduce_small(v, name):
    r, c = v.shape

    def body(v_ref, out_ref, land, send_sems, recv_sems):
        me = _me()
        land[me] = v_ref[...]
        sends = []
        for k in range(1, NDEV):
            peer, _ = _peer(k)
            cp = pltpu.make_async_remote_copy(src_ref=v_ref, dst_ref=land.at[me], send_sem=send_sems.at[k - 1],
                                              recv_sem=recv_sems.at[k - 1], device_id=peer, device_id_type=MESH)
            cp.start()
            sends.append(cp)
        for k in range(1, NDEV):
            peer, pidx = _peer(k)
            pltpu.make_async_remote_copy(src_ref=v_ref, dst_ref=land.at[pidx], send_sem=send_sems.at[k - 1],
                                         recv_sem=recv_sems.at[k - 1], device_id=peer, device_id_type=MESH).wait_recv()
        for cp in sends:
            cp.wait_send()
        acc = land[0]
        for j in range(1, NDEV):
            acc = acc + land[j]
        out_ref[...] = acc

    return pl.pallas_call(
        body, name=name, out_shape=jax.ShapeDtypeStruct((r, c), v.dtype),
        in_specs=[pl.BlockSpec(memory_space=pltpu.VMEM)], out_specs=pl.BlockSpec(memory_space=pltpu.VMEM),
        scratch_shapes=[pltpu.VMEM((NDEV, r, c), v.dtype), pltpu.SemaphoreType.DMA((NDEV - 1,)),
                        pltpu.SemaphoreType.DMA((NDEV - 1,))],
        compiler_params=_CP(),
    )(v)


class _Exchange:
    def __init__(self, srcs, dsts, send_sems, recv_sems, local_sems, scatter):
        me = _me()
        n = len(srcs)
        self.sends, self.recvs, self.locals = [], [], []
        for a, (s, d) in enumerate(zip(srcs, dsts)):
            self.locals.append(pltpu.make_async_copy(s.at[me] if scatter else s, d.at[me], local_sems.at[a]))
        for k in range(1, NDEV):
            peer, pidx = _peer(k)
            for a, (s, d) in enumerate(zip(srcs, dsts)):
                src = s.at[pidx] if scatter else s
                sem = (k - 1) * n + a
                for dst, out in ((d.at[me], self.sends), (d.at[pidx], self.recvs)):
                    out.append(pltpu.make_async_remote_copy(src_ref=src, dst_ref=dst, send_sem=send_sems.at[sem],
                                                            recv_sem=recv_sems.at[sem], device_id=peer, device_id_type=MESH))

    def start(self):
        for cp in self.locals + self.sends:
            cp.start()

    def wait(self):
        for cp in self.recvs:
            cp.wait_recv()
        for cp in self.sends:
            cp.wait_send()
        for cp in self.locals:
            cp.wait()


def _exchange_shapes(arrays, scatter):
    return [jax.ShapeDtypeStruct(a.shape if scatter else (NDEV,) + a.shape, a.dtype) for a in arrays]


def _exchange_sems(n):
    return [pltpu.SemaphoreType.DMA(((NDEV - 1) * n,)), pltpu.SemaphoreType.DMA(((NDEV - 1) * n,)), pltpu.SemaphoreType.DMA((n,))]


def _chips():
    x, y, c = lax.axis_index("x"), lax.axis_index("y"), lax.axis_index("c")
    return (x, y, c), (x, y, 1 - c), [(1 - x, y), (x, 1 - y), (1 - x, 1 - y)]


def _gather_two_level(arrays, name):
    n = len(arrays)

    def body(*refs):
        srcs, outs = refs[:n], refs[n:2 * n]
        send_sems, recv_sems = refs[2 * n:]
        me, sibling, chips = _chips()
        c = me[2]
        idx = lambda p: 4 * p[0] + 2 * p[1] + p[2]

        def copy(a, k, block, to, src=None):
            return pltpu.make_async_remote_copy(
                src_ref=outs[a].at[idx(block)] if src is None else src, dst_ref=outs[a].at[idx(block)],
                send_sem=send_sems.at[7 * a + k], recv_sem=recv_sems.at[7 * a + k], device_id=to, device_id_type=MESH)

        first, passed = [], []
        for a in range(n):
            outs[a][idx(me)] = srcs[a][...]
            first += [copy(a, 0, me, sibling, src=srcs[a])]
            first += [copy(a, 1 + j, me, (*chip, c), src=srcs[a]) for j, chip in enumerate(chips)]
        for cp in first:
            cp.start()
        for a in range(n):
            for j, chip in enumerate(chips):
                copy(a, 1 + j, (*chip, c), me).wait_recv()
                cp = copy(a, 4 + j, (*chip, c), sibling)
                cp.start()
                passed.append(cp)
        for a in range(n):
            copy(a, 0, sibling, me).wait_recv()
            for j, chip in enumerate(chips):
                copy(a, 4 + j, (*chip, 1 - c), me).wait_recv()
        for cp in first + passed:
            cp.wait_send()

    vm = pl.BlockSpec(memory_space=pltpu.VMEM)
    return pl.pallas_call(
        body, name=name, out_shape=[jax.ShapeDtypeStruct((NDEV,) + a.shape, a.dtype) for a in arrays],
        in_specs=[vm] * n, out_specs=[vm] * n,
        scratch_shapes=[pltpu.SemaphoreType.DMA((7 * n,)), pltpu.SemaphoreType.DMA((7 * n,))],
        compiler_params=_CP(),
    )(*arrays)


def _reduce_scatter_two_level(g, name):
    _, r, c = g.shape
    nchip = NDEV // 2

    def body(g_ref, o_ref, stage, part, land, d_send, d_recv, i_send, i_recv):
        me, sibling, chips = _chips()
        x, y, cc = me
        mine = 2 * x + y

        def blk(k, core):
            return 2 * k + core

        swaps = [pltpu.make_async_remote_copy(src_ref=g_ref.at[blk(k, 1 - cc)], dst_ref=stage.at[k], send_sem=d_send.at[k],
                                              recv_sem=d_recv.at[k], device_id=sibling, device_id_type=MESH) for k in range(nchip)]
        for cp in swaps:
            cp.start()
        for cp in swaps:
            cp.wait_recv()
        for k in range(nchip):
            part[k] = (g_ref[blk(k, cc)].astype(f32) + stage[k].astype(f32)).astype(bf16)
        sends = []
        for j, chip in enumerate(chips):
            kd = 2 * chip[0] + chip[1]
            cp = pltpu.make_async_remote_copy(src_ref=part.at[kd], dst_ref=land.at[mine], send_sem=i_send.at[j],
                                              recv_sem=i_recv.at[j], device_id=(*chip, cc), device_id_type=MESH)
            cp.start()
            sends.append(cp)
        land[mine] = part[mine]
        for j, chip in enumerate(chips):
            ks = 2 * chip[0] + chip[1]
            pltpu.make_async_remote_copy(src_ref=part.at[ks], dst_ref=land.at[ks], send_sem=i_send.at[j], recv_sem=i_recv.at[j],
                                         device_id=(*chip, cc), device_id_type=MESH).wait_recv()
        for cp in swaps + sends:
            cp.wait_send()
        acc = land[0].astype(f32)
        for k in range(1, nchip):
            acc = acc + land[k].astype(f32)
        o_ref[...] = acc

    vm = pl.BlockSpec(memory_space=pltpu.VMEM)
    return pl.pallas_call(
        body, name=name, out_shape=jax.ShapeDtypeStruct((r, c), f32), in_specs=[vm], out_specs=vm,
        scratch_shapes=[pltpu.VMEM((nchip, r, c), g.dtype)] * 3 + [pltpu.SemaphoreType.DMA((nchip,)), pltpu.SemaphoreType.DMA((nchip,)),
                                                                   pltpu.SemaphoreType.DMA((3,)), pltpu.SemaphoreType.DMA((3,))],
        compiler_params=_CP(),
    )(g)


class _Cargo:
    def __init__(self, cargo):
        self.arrays, self.scatter = cargo if cargo else ([], False)
        self.n = len(self.arrays)

    def in_specs(self):
        return [_ANY] * self.n

    def out_shapes(self):
        return _exchange_shapes(self.arrays, self.scatter)

    def sems(self):
        return _exchange_sems(self.n) if self.n else []

    def split(self, refs, n_in, n_out, n_scratch):
        n = self.n
        return refs[:n_in], refs[n_in + n:n_in + n + n_out], refs[n_in + 2 * n + n_out:n_in + 2 * n + n_out + n_scratch]

    def ride(self, refs, n_in, n_out, grid):
        if not self.n:
            return
        n = self.n
        ex = _Exchange(refs[n_in:n_in + n], refs[n_in + n + n_out:n_in + 2 * n + n_out], *refs[-3:], self.scatter)
        grid = (grid,) if isinstance(grid, int) else tuple(grid)
        first = functools.reduce(jnp.logical_and, [pl.program_id(a) == 0 for a in range(len(grid))])
        last = functools.reduce(jnp.logical_and, [pl.program_id(a) == g - 1 for a, g in enumerate(grid)])

        @pl.when(first)
        def _():
            ex.start()

        @pl.when(last)
        def _():
            ex.wait()


def _sum8(land, name):
    _, r, c = land.shape
    rb = next((b for b in (256, 64, 32) if r % b == 0), r)

    def body(l_ref, o_ref):
        acc = l_ref[0].astype(f32)
        for j in range(1, NDEV):
            acc = acc + l_ref[j].astype(f32)
        o_ref[...] = acc

    return pl.pallas_call(
        body, name=name, grid=(r // rb,), out_shape=jax.ShapeDtypeStruct((r, c), f32),
        in_specs=[pl.BlockSpec((NDEV, rb, c), lambda i: (0, i, 0))], out_specs=pl.BlockSpec((rb, c), lambda i: (i, 0)),
        compiler_params=_CP(dimension_semantics=("parallel",)),
    )(land)


def _ada_fwd(c9, w_mod_l, name):
    def body(c_ref, w_ref, o_ref):
        o_ref[...] = dnn(jax.nn.silu(c_ref[...]), w_ref[...])

    return pl.pallas_call(body, name=name, out_shape=jax.ShapeDtypeStruct((16, w_mod_l.shape[1]), f32),
                          compiler_params=_CP())(c9, w_mod_l)


def _mod_select(m_all, b_mod6, name):
    def body(m_ref, b_ref, mx_ref, mc_ref):
        me = _me()
        mx_ref[...] = m_ref[me] + b_ref[...]
        mc_ref[...] = m_ref[8] + b_ref[...]

    return pl.pallas_call(body, name=name, out_shape=[jax.ShapeDtypeStruct((6, D), f32)] * 2, compiler_params=_CP())(m_all, b_mod6)


def _ada_bwd(c9, dmx_all, dmc_all, dmx_l, dmc_l, w_mod_l, name):
    ncol = w_mod_l.shape[1]

    def rowsum(r):
        acc = r[0:1]
        for j in range(1, NDEV):
            acc = acc + r[j:j + 1]
        return acc

    def body(c_ref, xa_ref, ca_ref, xl_ref, cl_ref, w_ref, gw_ref, gb_ref, dc_ref):
        s9, vjp = jax.vjp(jax.nn.silu, c_ref[...])
        dm9 = jnp.concatenate([xl_ref[...], rowsum(cl_ref[...]), jnp.zeros((7, ncol), f32)], axis=0)
        gw_ref[...] = dtn(s9, dm9)
        gb_ref[...] = rowsum(xa_ref[...]) + rowsum(ca_ref[...])
        dc_ref[...] = vjp(dnt(dm9, w_ref[...]))[0]

    return pl.pallas_call(
        body, name=name,
        out_shape=[jax.ShapeDtypeStruct((D, ncol), f32), jax.ShapeDtypeStruct((1, 6 * D), f32), jax.ShapeDtypeStruct((16, D), f32)],
        compiler_params=_CP())(c9, dmx_all, dmc_all, dmx_l, dmc_l, w_mod_l)


def _lane_sign(rank):
    shape = (1,) * (rank - 1) + (SB,)
    return jnp.where(lax.broadcasted_iota(jnp.int32, shape, rank - 1) < S5N, -1.0, 1.0)


def _s5_build_fn(lre2, lim2, ls, bn, bs, cn, cs, rev):
    sg = _lane_sign(3)
    s = jnp.exp(ls)
    ar, ai = lre2 * s, lim2 * s
    e = jnp.exp(ar)
    nr, ni = e * jnp.cos(ai) - 1.0, e * jnp.sin(ai)
    den = lre2 * lre2 + lim2 * lim2
    cr, ci = (nr * lre2 + ni * lim2) / den, (ni * lre2 - nr * lim2) / den
    bbn = cr * bn + (ci * sg) * bs
    bbs = cr * bs - (ci * sg) * bn

    def powers(ex):
        m, ang = jnp.exp(ex * ar), ex * ai
        return m * jnp.cos(ang), m * jnp.sin(ang) * sg

    def times(tabs, xn, xs):
        f1, f2 = tabs
        return f1[:, :, None, :] * xn[:, None, :, :] + f2[:, :, None, :] * xs[:, None, :, :]

    t = lax.broadcasted_iota(jnp.int32, (1, TC, 1), 1).astype(f32)
    if rev:
        e_src, e_dst, e_out, e_in = t - (TC - 1.0), (TC - 1.0) - t, t, TC - t
    else:
        e_src, e_dst, e_out, e_in = -t, t, (TC - 1.0) - t, t + 1.0
    g = lre2.shape[0]
    flat = lambda a: a.reshape(g, TCP, SB)
    conj = -_lane_sign(4)
    ll = flat(times(powers(e_src), bbn, bbs))
    rr = flat(times(powers(e_dst), cn, cs) * conj)
    mb = flat(times(powers(e_out), bbn, bbs))
    mct = flat(times(powers(e_in), cn, cs) * conj)
    a1, a2 = powers(float(TC))
    row = lax.broadcasted_iota(jnp.int32, (TCP, TCP), 0) // S5P
    col = lax.broadcasted_iota(jnp.int32, (TCP, TCP), 1) // S5P
    mask = jnp.where((col <= row) if rev else (col >= row), 1.0, 0.0)
    m = jnp.concatenate([dnt(ll[j], rr[j])[None] for j in range(g)], axis=0) * mask
    return m, mb, mct, a1, a2


def _gspec(*tail):
    nt = len(tail)
    return pl.BlockSpec((GBK,) + tail, lambda i, _n=nt: (i,) + (0,) * _n)


def _s5_build(params, rev, name):
    def body(l1, l2, ls, bn, bs, cn, cs, m_ref, mb_ref, mc_ref, a1_ref, a2_ref):
        m, mb, mct, a1, a2 = _s5_build_fn(l1[...], l2[...], ls[...], bn[...], bs[...], cn[...], cs[...], rev)
        m_ref[...], mb_ref[...], mc_ref[...] = m.astype(bf16), mb.astype(bf16), mct.astype(bf16)
        a1_ref[...], a2_ref[...] = a1, a2

    vec, pm = _gspec(1, SB), _gspec(S5P, SB)
    return pl.pallas_call(
        body, name=name, grid=(S5G // GBK,),
        in_specs=[vec, vec, _gspec(1, 1), pm, pm, pm, pm],
        out_specs=[_gspec(TCP, TCP), _gspec(TCP, SB), _gspec(TCP, SB), vec, vec],
        out_shape=[jax.ShapeDtypeStruct((S5G, TCP, TCP), bf16), jax.ShapeDtypeStruct((S5G, TCP, SB), bf16),
                   jax.ShapeDtypeStruct((S5G, TCP, SB), bf16), jax.ShapeDtypeStruct((S5G, 1, SB), f32),
                   jax.ShapeDtypeStruct((S5G, 1, SB), f32)],
        compiler_params=_CP(dimension_semantics=("parallel",)),
    )(*params)


def _s5_build_bwd(params, cots, prev, rev, name):
    def body(l1, l2, ls, bn, bs, cn, cs, dm, dmb, dmc, da1, da2, pb, pc, gl1, gl2, gls, gb, gc):
        prim = (l1[...], l2[...], ls[...], bn[...], bs[...], cn[...], cs[...])
        _, vjp = jax.vjp(functools.partial(_s5_build_fn, rev=rev), *prim)
        d1, d2, dls, dbn, dbs, dcn, dcs = vjp((dm[...], dmb[...], dmc[...], da1[...], da2[...]))
        gl1[...] = d1 + pltpu.roll(d1, S5N, axis=2)
        gl2[...] = d2 + pltpu.roll(d2, S5N, axis=2)
        gls[...] = dls
        gb[...] = dbn + pltpu.roll(dbs, S5N, axis=2) + pb[...]
        gc[...] = dcn + pltpu.roll(dcs, S5N, axis=2) + pc[...]

    vec, pm, big = _gspec(1, SB), _gspec(S5P, SB), _gspec(TCP, SB)
    return pl.pallas_call(
        body, name=name, grid=(S5G // GBK,),
        in_specs=[vec, vec, _gspec(1, 1), pm, pm, pm, pm, _gspec(TCP, TCP), big, big, vec, vec, pm, pm],
        out_specs=[vec, vec, _gspec(1, 1), pm, pm],
        out_shape=[jax.ShapeDtypeStruct((S5G, 1, SB), f32), jax.ShapeDtypeStruct((S5G, 1, SB), f32),
                   jax.ShapeDtypeStruct((S5G, 1, 1), f32), jax.ShapeDtypeStruct((S5G, S5P, SB), f32),
                   jax.ShapeDtypeStruct((S5G, S5P, SB), f32)],
        compiler_params=_CP(dimension_semantics=("parallel",)),
    )(*params, *cots, *prev)


def _s5_inc(u, mb_f, mb_b, name):
    nc = u.shape[1]

    def body(u_ref, mf_ref, mb_ref, sf_ref, sb_ref):
        for j in range(GBK):
            sf_ref[:, j, :] = jnp.dot(u_ref[j], mf_ref[j], preferred_element_type=f32)
            sb_ref[:, j, :] = jnp.dot(u_ref[j], mb_ref[j], preferred_element_type=f32)

    sspec = pl.BlockSpec((nc, GBK, SB), lambda i: (0, i, 0))
    return pl.pallas_call(
        body, name=name, grid=(S5G // GBK,), in_specs=[_gspec(nc, TCP), _gspec(TCP, SB), _gspec(TCP, SB)],
        out_specs=[sspec, sspec], out_shape=[jax.ShapeDtypeStruct((nc, S5G, SB), f32)] * 2,
        compiler_params=_CP(dimension_semantics=("parallel",)),
    )(u, mb_f, mb_b)


def _idx_fwd(nctx, nch):
    return lambda i: i


def _idx_rev(nctx, nch):
    return lambda i: jnp.where(i < nctx, nctx - 1 - i, nch + nctx - 1 - i)


def _carry_loop(nc, step, init):
    def trip(i, c):
        for k in range(CARRY_UNROLL):
            c = step(i * CARRY_UNROLL + k, c)
        return c

    return lax.fori_loop(0, nc // CARRY_UNROLL, trip, init)


def _s5_carry(s_f, s_b, a_f, a_b, nctx, name):
    nc = s_f.shape[0]
    idx_b = _idx_rev(nctx, nc)

    def body(sf_ref, sb_ref, f1_ref, f2_ref, b1_ref, b2_ref, hf_ref, hb_ref):
        f1, f2, b1, b2 = f1_ref[...], f2_ref[...], b1_ref[...], b2_ref[...]

        def step(i, c):
            hf, hfs, hb, hbs = c
            rb = idx_b(i)
            hf_ref[i] = hf
            hb_ref[rb] = hb
            sf, sb = sf_ref[i], sb_ref[rb]
            return (f1 * hf + f2 * hfs + sf, f1 * hfs - f2 * hf + pltpu.roll(sf, S5N, axis=1),
                    b1 * hb + b2 * hbs + sb, b1 * hbs - b2 * hb + pltpu.roll(sb, S5N, axis=1))

        z = jnp.zeros((S5G, SB), f32)
        _carry_loop(nc, step, (z, z, z, z))

    return pl.pallas_call(body, name=name, out_shape=[jax.ShapeDtypeStruct(s_f.shape, f32)] * 2,
                          compiler_params=_CP())(s_f, s_b, *a_f, *a_b)


def _s5_carry_bwd(dhp, hp, a1, a2, rev, nctx, name):
    nc = hp.shape[0]
    idx = (_idx_rev if rev else _idx_fwd)(nctx, nc)

    def body(dhp_ref, hp_ref, a1_ref, a2_ref, ds_ref, d1_ref, d2_ref):
        f1, f2 = a1_ref[...], a2_ref[...]

        def step(k, carry):
            ab, abs_, d1, d2 = carry
            r = idx(nc - 1 - k)
            ds_ref[r] = ab
            h, dh = hp_ref[r], dhp_ref[r]
            return (dh + f1 * ab - f2 * abs_, pltpu.roll(dh, S5N, axis=1) + f1 * abs_ + f2 * ab,
                    d1 + ab * h, d2 + ab * pltpu.roll(h, S5N, axis=1))

        z = jnp.zeros((S5G, SB), f32)
        _, _, d1, d2 = _carry_loop(nc, step, (z, z, z, z))
        d1_ref[...], d2_ref[...] = d1, d2

    return pl.pallas_call(
        body, name=name,
        out_shape=[jax.ShapeDtypeStruct(hp.shape, f32), jax.ShapeDtypeStruct((S5G, SB), f32), jax.ShapeDtypeStruct((S5G, SB), f32)],
        compiler_params=_CP())(dhp, hp, a1, a2)


def _s5_out(u, m_f, m_b, hp_f, hp_b, mc_f, mc_b, name):
    nc = u.shape[1]

    def body(u_ref, mf_ref, mb_ref, hf_ref, hb_ref, cf_ref, cb_ref, y_ref):
        for j in range(GBK):
            uj = u_ref[j]
            y_ref[j] = (jnp.dot(uj, mf_ref[j], preferred_element_type=f32) + jnp.dot(uj, mb_ref[j], preferred_element_type=f32)
                        + dnt(hf_ref[:, j, :], cf_ref[j]) + dnt(hb_ref[:, j, :], cb_ref[j])).astype(bf16)

    sspec = pl.BlockSpec((nc, GBK, SB), lambda i: (0, i, 0))
    return pl.pallas_call(
        body, name=name, grid=(S5G // GBK,),
        in_specs=[_gspec(nc, TCP), _gspec(TCP, TCP), _gspec(TCP, TCP), sspec, sspec, _gspec(TCP, SB), _gspec(TCP, SB)],
        out_specs=_gspec(nc, TCP), out_shape=jax.ShapeDtypeStruct((S5G, nc, TCP), bf16),
        compiler_params=_CP(dimension_semantics=("parallel",)),
    )(u, m_f, m_b, hp_f, hp_b, mc_f, mc_b)


def _s5_out_bwd(dy, u, m_f, m_b, hp_f, hp_b, mc_f, mc_b, name):
    nc = u.shape[1]

    def body(dy_ref, u_ref, mf_ref, mb_ref, hf_ref, hb_ref, cf_ref, cb_ref, du_ref, g_ref, dhf_ref, dhb_ref, dcf_ref, dcb_ref):
        for j in range(GBK):
            dyj = dy_ref[j]
            du_ref[j] = dnt(dyj, mf_ref[j]) + dnt(dyj, mb_ref[j])
            g_ref[j] = dtn(u_ref[j], dyj)
            dhf_ref[:, j, :] = dnn(dyj, cf_ref[j])
            dhb_ref[:, j, :] = dnn(dyj, cb_ref[j])
            dcf_ref[j] = dtn(dyj, hf_ref[:, j, :])
            dcb_ref[j] = dtn(dyj, hb_ref[:, j, :])

    sspec = pl.BlockSpec((nc, GBK, SB), lambda i: (0, i, 0))
    sshape = jax.ShapeDtypeStruct((nc, S5G, SB), f32)
    cshape = jax.ShapeDtypeStruct((S5G, TCP, SB), f32)
    return pl.pallas_call(
        body, name=name, grid=(S5G // GBK,),
        in_specs=[_gspec(nc, TCP), _gspec(nc, TCP), _gspec(TCP, TCP), _gspec(TCP, TCP), sspec, sspec, _gspec(TCP, SB), _gspec(TCP, SB)],
        out_specs=[_gspec(nc, TCP), _gspec(TCP, TCP), sspec, sspec, _gspec(TCP, SB), _gspec(TCP, SB)],
        out_shape=[jax.ShapeDtypeStruct((S5G, nc, TCP), f32), jax.ShapeDtypeStruct((S5G, TCP, TCP), f32), sshape, sshape, cshape, cshape],
        compiler_params=_CP(dimension_semantics=("parallel",)),
    )(dy, u, m_f, m_b, hp_f, hp_b, mc_f, mc_b)


def _s5_inc_bwd(du1, u, ds_f, ds_b, mb_f, mb_b, name):
    nc = u.shape[1]

    def body(du1_ref, u_ref, dsf_ref, dsb_ref, mf_ref, mb_ref, du_ref, dmf_ref, dmb_ref):
        for j in range(GBK):
            dsf, dsb = dsf_ref[:, j, :], dsb_ref[:, j, :]
            du_ref[j] = (du1_ref[j] + dnt(dsf, mf_ref[j]) + dnt(dsb, mb_ref[j])).astype(bf16)
            dmf_ref[j] = dtn(u_ref[j], dsf)
            dmb_ref[j] = dtn(u_ref[j], dsb)

    sspec = pl.BlockSpec((nc, GBK, SB), lambda i: (0, i, 0))
    cshape = jax.ShapeDtypeStruct((S5G, TCP, SB), f32)
    return pl.pallas_call(
        body, name=name, grid=(S5G // GBK,),
        in_specs=[_gspec(nc, TCP), _gspec(nc, TCP), sspec, sspec, _gspec(TCP, SB), _gspec(TCP, SB)],
        out_specs=[_gspec(nc, TCP), _gspec(TCP, SB), _gspec(TCP, SB)],
        out_shape=[jax.ShapeDtypeStruct((S5G, nc, TCP), bf16), cshape, cshape],
        compiler_params=_CP(dimension_semantics=("parallel",)),
    )(du1, u, ds_f, ds_b, mb_f, mb_b)


def _to_groups(a):
    n = a.shape[0]
    return a.reshape(n // TC, TC, S5G, S5P).transpose(2, 0, 1, 3).reshape(S5G, n // TC, TCP)


def _from_groups(a):
    nc = a.shape[1]
    return a.reshape(S5G, nc, TC, S5P).transpose(1, 2, 0, 3).reshape(nc * TC, S5W)


def _swap_pairs(t):
    lane = lax.broadcasted_iota(jnp.int32, t.shape, 1)
    return jnp.where(lane % 2 == 0, pltpu.roll(t, DH - 1, axis=1), pltpu.roll(t, 1, axis=1))


def _rot(t, cosf, sins):
    return t * cosf + _swap_pairs(t) * sins


def _rot_t(d, cosf, sins):
    return d * cosf - _swap_pairs(d) * sins


def _ret_tables(ld, rev):
    pos = lax.broadcasted_iota(jnp.int32, (T, 1), 0).astype(f32)
    diff = pos - lax.broadcasted_iota(jnp.int32, (1, T), 1).astype(f32)
    if rev:
        keep, dist = diff < 0, jnp.maximum(-diff, 0.0)
        xi, zeta = jnp.exp(ld * (T - pos)), jnp.exp(ld * pos)
    else:
        keep, dist = diff >= 0, jnp.maximum(diff, 0.0)
        xi, zeta = jnp.exp(ld * (pos + 1.0)), jnp.exp(ld * (T - 1.0 - pos))
    return jnp.where(keep, jnp.exp(ld * dist), 0.0), xi, zeta, jnp.exp(ld * float(T))


def _ret_apply(qr, kr, v, rp, dm, xi, zeta, cdec):
    out = dnn(dnt(qr, kr) * dm, v) + dnn(qr * xi, rp)
    return out, cdec * rp + dtn(kr * zeta, v)


def _ret_fwd(p_ext, ld8_f, ld8_b, nctx, name, cargo=None):
    n = p_ext.shape[0]
    nch = n // T
    idx_b = _idx_rev(nctx, nch)
    cg = _Cargo(cargo)

    def body(*refs):
        ins, (of_ref, ob_ref, rpf_ref, rpb_ref), (rf_s, rb_s, dm_s, xz_s) = cg.split(refs, 8, 4, 4)
        qf, kf, vf, qb, kb, vb, ldf_ref, ldb_ref = ins
        cg.ride(refs, 8, 4, nch)

        @pl.when(pl.program_id(0) == 0)
        def _():
            rf_s[...] = jnp.zeros_like(rf_s)
            rb_s[...] = jnp.zeros_like(rb_s)
            for d, ld_ref in enumerate((ldf_ref, ldb_ref)):
                for h in range(RH):
                    dm, xi, zeta, cdec = _ret_tables(ld_ref[h:h + 1, 0:1], bool(d))
                    dm_s[d, h] = dm
                    xz_s[d, h, 0] = jnp.broadcast_to(xi, (T, DH))
                    xz_s[d, h, 1] = jnp.broadcast_to(zeta, (T, DH))
                    xz_s[d, h, 2] = jnp.broadcast_to(cdec, (T, DH))

        for h in range(RH):
            sl = slice(h * DH, (h + 1) * DH)
            for d, (q_ref, k_ref, v_ref, o_ref, rp_ref, r_s) in enumerate(((qf, kf, vf, of_ref, rpf_ref, rf_s),
                                                                            (qb, kb, vb, ob_ref, rpb_ref, rb_s))):
                rp = r_s[h]
                rp_ref[0, h] = rp
                out, rn = _ret_apply(q_ref[:, sl].astype(f32), k_ref[:, sl].astype(f32), v_ref[:, sl].astype(f32), rp,
                                     dm_s[d, h], xz_s[d, h, 0], xz_s[d, h, 1], xz_s[d, h, 2])
                r_s[h] = rn
                o_ref[:, sl] = out

    fcol = lambda cb: pl.BlockSpec((T, RW), lambda i, _c=cb: (i, _c))
    bcol = lambda cb: pl.BlockSpec((T, RW), lambda i, _c=cb: (idx_b(i), _c))
    rspec = pl.BlockSpec((1, RH, DH, DH), lambda i: (i, 0, 0, 0))
    oshape, rshape = jax.ShapeDtypeStruct((n, RW), f32), jax.ShapeDtypeStruct((nch, RH, DH, DH), f32)
    return pl.pallas_call(
        body, name=name, grid=(nch,),
        in_specs=[fcol(1), fcol(2), fcol(3), bcol(1), bcol(2), bcol(3), _const_spec((8, 128)), _const_spec((8, 128))] + cg.in_specs(),
        out_specs=[fcol(0), bcol(0), rspec, rspec] + cg.in_specs(),
        out_shape=[oshape, oshape, rshape, rshape] + cg.out_shapes(),
        scratch_shapes=[pltpu.VMEM((RH, DH, DH), f32)] * 2 + [pltpu.VMEM((2, RH, T, T), f32), pltpu.VMEM((2, RH, 3, T, DH), f32)] + cg.sems(),
        compiler_params=_CP(dimension_semantics=_ARB),
    )(p_ext, p_ext, p_ext, p_ext, p_ext, p_ext, ld8_f, ld8_b, *cg.arrays)


def _ret_bwd(p_ext, ld8_f, ld8_b, rp_f, rp_b, do_ext, nctx, name, cargo=None):
    n = p_ext.shape[0]
    nch = n // T
    idx_rev = _idx_rev(nctx, nch)
    idf = lambda j: nch - 1 - j
    idb = lambda j: idx_rev(nch - 1 - j)
    cg = _Cargo(cargo)

    def body(*refs):
        ins, outs, (drf_s, drb_s, dm_s, xz_s, gdm_s, gxz_s) = cg.split(refs, 12, 8, 6)
        qf, kf, vf, qb, kb, vb, ldf_ref, ldb_ref, rpf_ref, rpb_ref, dof_ref, dob_ref = ins
        dqf, dkf, dvf, dqb, dkb, dvb, dldf_ref, dldb_ref = outs
        cg.ride(refs, 12, 8, nch)
        lds = (ldf_ref, ldb_ref)

        @pl.when(pl.program_id(0) == 0)
        def _():
            for r in (drf_s, drb_s, gdm_s, gxz_s):
                r[...] = jnp.zeros_like(r)
            for d in range(2):
                for h in range(RH):
                    dm, xi, zeta, cdec = _ret_tables(lds[d][h:h + 1, 0:1], bool(d))
                    dm_s[d, h] = dm
                    for k, tab in enumerate((xi, zeta, cdec)):
                        xz_s[d, h, k] = jnp.broadcast_to(tab, (T, DH))

        for h in range(RH):
            sl = slice(h * DH, (h + 1) * DH)
            for d, (q_ref, k_ref, v_ref, rp_ref, do_ref, dq_ref, dk_ref, dv_ref, dr_s) in enumerate((
                    (qf, kf, vf, rpf_ref, dof_ref, dqf, dkf, dvf, drf_s), (qb, kb, vb, rpb_ref, dob_ref, dqb, dkb, dvb, drb_s))):
                _, vjp = jax.vjp(_ret_apply, q_ref[:, sl].astype(f32), k_ref[:, sl].astype(f32), v_ref[:, sl].astype(f32),
                                 rp_ref[0, h], dm_s[d, h], xz_s[d, h, 0], xz_s[d, h, 1], xz_s[d, h, 2])
                dqr, dkr, dv, drp, gdm, gxi, gzeta, gcdec = vjp((do_ref[:, sl], dr_s[h]))
                dr_s[h] = drp
                dq_ref[:, sl], dk_ref[:, sl], dv_ref[:, sl] = dqr, dkr, dv
                gdm_s[d, h] += gdm
                for k, g in enumerate((gxi, gzeta, gcdec)):
                    gxz_s[d, h, k] += g

        @pl.when(pl.program_id(0) == nch - 1)
        def _():
            for d, dld_ref in enumerate((dldf_ref, dldb_ref)):
                dld_ref[...] = jnp.zeros_like(dld_ref)
                for h in range(RH):
                    _, vjp = jax.vjp(functools.partial(_ret_tables, rev=bool(d)), lds[d][h:h + 1, 0:1])
                    lanes = lambda a: jnp.sum(a, axis=1, keepdims=True)
                    (dld,) = vjp((gdm_s[d, h], lanes(gxz_s[d, h, 0]), lanes(gxz_s[d, h, 1]),
                                  jnp.sum(lanes(gxz_s[d, h, 2]), axis=0, keepdims=True)))
                    dld_ref[h:h + 1, :] = jnp.broadcast_to(dld, (1, 128))

    fcol = lambda cb: pl.BlockSpec((T, RW), lambda j, _c=cb: (idf(j), _c))
    bcol = lambda cb: pl.BlockSpec((T, RW), lambda j, _c=cb: (idb(j), _c))
    rspec = pl.BlockSpec((1, RH, DH, DH), lambda j: (nch - 1 - j, 0, 0, 0))
    oshape = jax.ShapeDtypeStruct((n, RW), f32)
    return pl.pallas_call(
        body, name=name, grid=(nch,),
        in_specs=[fcol(1), fcol(2), fcol(3), bcol(1), bcol(2), bcol(3), _const_spec((8, 128)), _const_spec((8, 128)), rspec, rspec,
                  fcol(0), bcol(0)] + cg.in_specs(),
        out_specs=[fcol(0), fcol(0), fcol(0), bcol(0), bcol(0), bcol(0), _acc_spec((8, 128)), _acc_spec((8, 128))] + cg.in_specs(),
        out_shape=[oshape] * 6 + [jax.ShapeDtypeStruct((8, 128), f32)] * 2 + cg.out_shapes(),
        scratch_shapes=[pltpu.VMEM((RH, DH, DH), f32)] * 2 + [pltpu.VMEM((2, RH, T, T), f32), pltpu.VMEM((2, RH, 3, T, DH), f32)] * 2
        + cg.sems(),
        compiler_params=_CP(dimension_semantics=_ARB),
    )(p_ext, p_ext, p_ext, p_ext, p_ext, p_ext, ld8_f, ld8_b, rp_f, rp_b, do_ext, do_ext, *cg.arrays)


def _qk_heads(p, fn_q, fn_k):
    heads = lambda base, fn: [fn(p[:, base + h * DH:base + (h + 1) * DH]) for h in range(RH)]
    return jnp.concatenate([p[:, :S5W]] + heads(S5W, fn_q) + heads(S5W + RW, fn_k) + [p[:, S5W + 2 * RW:]], axis=1)


def _f1_fwd(x, ctx, modx, modc, nw1, w_in_n, cosf, sins, name, cargo=None):
    L = x.shape[0]
    nb = L // R + 1
    scale = DH ** -0.5
    cg = _Cargo(cargo)

    def body(*refs):
        (x_ref, c_ref, mx_ref, mc_ref, nw_ref, w_ref, cos_ref, sin_ref), (p_ref,), _ = cg.split(refs, 8, 1, 0)
        cg.ride(refs, 8, 1, nb)
        is_ctx = pl.program_id(0) == 0
        xin = jnp.where(is_ctx, c_ref[...], x_ref[...])
        sh = jnp.where(is_ctx, mc_ref[0:1], mx_ref[0:1])
        sc = jnp.where(is_ctx, mc_ref[1:2], mx_ref[1:2])
        cf, ss = cos_ref[...], sin_ref[...]
        p = dnn(_norm_mod(xin, nw_ref[...], sh, sc), w_ref[...])
        p_ref[...] = _qk_heads(p, lambda t: _rot(t, cf, ss), lambda t: _rot(t * scale, cf, ss)).astype(bf16)

    return pl.pallas_call(
        body, name=name, grid=(nb,),
        in_specs=[pl.BlockSpec((R, D), lambda i: (jnp.maximum(i - 1, 0), 0)), _const_spec((R, D)), _const_spec((6, D)),
                  _const_spec((6, D)), _const_spec((1, D)), _const_spec((D, INC)), pl.BlockSpec((R, DH), lambda i: (i, 0)),
                  pl.BlockSpec((R, DH), lambda i: (i, 0))] + cg.in_specs(),
        out_specs=[pl.BlockSpec((R, INC), lambda i: (i, 0))] + cg.in_specs(),
        out_shape=[jax.ShapeDtypeStruct((L + R, INC), bf16)] + cg.out_shapes(),
        scratch_shapes=cg.sems(),
        compiler_params=_CP(dimension_semantics=_ARB),
    )(x, ctx, modx, modc, nw1, w_in_n, cosf, sins, *cg.arrays)


def _f1_bwd(x, ctx, modx, modc, nw1, w_in_t, cosf, sins, dx1, parts, name, cargo=None):
    L = x.shape[0]
    nb = L // R + 1
    scale = DH ** -0.5
    cg = _Cargo(cargo)

    def body(*refs):
        ins, (gx_ref, dp_ref, h1_ref, dnw_ref, dmx_ref, dmc_ref), _ = cg.split(refs, 18, 6, 0)
        x_ref, c_ref, mx_ref, mc_ref, nw_ref, w_ref, cos_ref, sin_ref, dx1_ref, du0, du1, dq0, dq1, dk0, dk1, dv0, dv1, dg0 = ins
        cg.ride(refs, 18, 6, nb)
        i = pl.program_id(0)
        is_ctx = i == 0

        @pl.when(is_ctx)
        def _():
            dnw_ref[...] = jnp.zeros_like(dnw_ref)
            dmx_ref[...] = jnp.zeros_like(dmx_ref)
            dmc_ref[...] = jnp.zeros_like(dmc_ref)

        cf, ss = cos_ref[...], sin_ref[...]
        dp = jnp.concatenate([du0[...].astype(f32) + du1[...], dq0[...] + dq1[...], dk0[...] + dk1[...], dv0[...] + dv1[...],
                              dg0[...]], axis=1)
        dp = _qk_heads(dp, lambda t: _rot_t(t, cf, ss), lambda t: _rot_t(t, cf, ss) * scale).astype(bf16)
        dp_ref[...] = dp
        xin = jnp.where(is_ctx, c_ref[...], x_ref[...])
        sh = jnp.where(is_ctx, mc_ref[0:1], mx_ref[0:1])
        sc = jnp.where(is_ctx, mc_ref[1:2], mx_ref[1:2])
        dh = dnn(dp, w_ref[...])
        h, vjp = jax.vjp(_norm_mod, xin, nw_ref[...], sh, sc)
        dxin, dnw, dsh, dsc = vjp(dh)
        h1_ref[...] = h.astype(bf16)
        gx_ref[...] = dx1_ref[...] + dxin
        dnw_ref[...] += dnw
        wx = jnp.where(is_ctx, 0.0, 1.0)
        dmx_ref[0:1] += dsh * wx
        dmx_ref[1:2] += dsc * wx
        dmc_ref[0:1] += dsh * (1.0 - wx)
        dmc_ref[1:2] += dsc * (1.0 - wx)

    lat = pl.BlockSpec((R, D), lambda i: (jnp.maximum(i - 1, 0), 0))
    ext = pl.BlockSpec((R, S5W), lambda i: (i, 0))
    return pl.pallas_call(
        body, name=name, grid=(nb,),
        in_specs=[lat, _const_spec((R, D)), _const_spec((6, D)), _const_spec((6, D)), _const_spec((1, D)), _const_spec((INC, D)),
                  pl.BlockSpec((R, DH), lambda i: (i, 0)), pl.BlockSpec((R, DH), lambda i: (i, 0)), lat] + [ext] * 9 + cg.in_specs(),
        out_specs=[lat, pl.BlockSpec((R, INC), lambda i: (i, 0)), pl.BlockSpec((R, D), lambda i: (i, 0)),
                   _acc_spec((1, D)), _acc_spec((6, D)), _acc_spec((6, D))] + cg.in_specs(),
        out_shape=[jax.ShapeDtypeStruct((L, D), f32), jax.ShapeDtypeStruct((L + R, INC), bf16),
                   jax.ShapeDtypeStruct((L + R, D), bf16), jax.ShapeDtypeStruct((1, D), f32),
                   jax.ShapeDtypeStruct((6, D), f32), jax.ShapeDtypeStruct((6, D), f32)] + cg.out_shapes(),
        scratch_shapes=cg.sems(),
        compiler_params=_CP(dimension_semantics=_ARB),
    )(x, ctx, modx, modc, nw1, w_in_t, cosf, sins, dx1, *parts, *cg.arrays)


def _ret_post(yr, g):
    outs = []
    for h in range(RH):
        yh = yr[:, h * DH:(h + 1) * DH]
        mu = jnp.mean(yh, axis=-1, keepdims=True)
        var = jnp.mean((yh - mu) ** 2, axis=-1, keepdims=True)
        outs.append((yh - mu) * lax.rsqrt(var + EPS))
    return jax.nn.silu(g) * jnp.concatenate(outs, axis=1)


def _mix_fn(ys, u, of, ob, g, x, dvec, bglu, gate1, pz, pm, wglu, wout):
    s = _gelu(ys + dvec * u)
    z = dnn(s, wglu) + bglu + pz
    cat = jnp.concatenate([s * jax.nn.sigmoid(z), _ret_post(of + ob, g)], axis=1)
    mix = dnn(cat, wout) + pm
    return x + gate1 * mix, (s, cat)


def _mix_fwd(x, ys, of, ob, p_ext, dvec, bglu, modx, wglu, wout, name, cargo=None):
    L = x.shape[0]
    nb = L // R
    cg = _Cargo(cargo)

    def body(*refs):
        ins, (x1_ref,), _ = cg.split(refs, 11, 1, 0)
        x_ref, ys_ref, of_ref, ob_ref, u_ref, g_ref, d_ref, b_ref, mx_ref, wg_ref, wo_ref = ins
        cg.ride(refs, 11, 1, nb)
        x1_ref[...] = _mix_fn(ys_ref[...].astype(f32), u_ref[...].astype(f32), of_ref[...], ob_ref[...], g_ref[...].astype(f32),
                              x_ref[...], d_ref[...], b_ref[...], mx_ref[2:3], 0.0, 0.0, wg_ref[...], wo_ref[...])[0]

    ext = pl.BlockSpec((R, S5W), lambda i: (i + 1, 0))
    return pl.pallas_call(
        body, name=name, grid=(nb,),
        in_specs=[pl.BlockSpec((R, D), lambda i: (i, 0)), ext, ext, ext, ext, pl.BlockSpec((R, RW), lambda i: (i + 1, 4)),
                  _const_spec((1, S5W)), _const_spec((1, S5W)), _const_spec((6, D)), _const_spec((S5W, S5W)), _const_spec((D, D))]
        + cg.in_specs(),
        out_specs=[pl.BlockSpec((R, D), lambda i: (i, 0))] + cg.in_specs(),
        out_shape=[jax.ShapeDtypeStruct((L, D), f32)] + cg.out_shapes(),
        scratch_shapes=cg.sems(),
        compiler_params=_CP(dimension_semantics=_ARB),
    )(x, ys, of, ob, p_ext, p_ext, dvec, bglu, modx, wglu, wout, *cg.arrays)


def _mix_bwd(x, ys, of, ob, p_ext, dvec, bglu, modx, wglu, wout, dx1, name, cargo=None):
    L = x.shape[0]
    nb = L // R + 1
    cg = _Cargo(cargo)

    def body(*refs):
        ins, outs, _ = cg.split(refs, 12, 11, 0)
        x_ref, ys_ref, of_ref, ob_ref, u_ref, g_ref, d_ref, b_ref, mx_ref, wg_ref, wo_ref, dx1_ref = ins
        dy_ref, dud_ref, do_ref, dg_ref, cat_ref, dmix_ref, s_ref, dz_ref, dd_ref, db_ref, dg1_ref = outs
        cg.ride(refs, 12, 11, nb)
        i = pl.program_id(0)

        @pl.when(i == 0)
        def _():
            for r in outs:
                r[...] = jnp.zeros_like(r)

        @pl.when(i > 0)
        def _():
            fn = lambda ys_, u_, of_, g_, d_, b_, g1_, pz_, pm_: _mix_fn(
                ys_, u_, of_, ob_ref[...], g_, x_ref[...], d_, b_, g1_, pz_, pm_, wg_ref[...], wo_ref[...])
            _, vjp, (s, cat) = jax.vjp(fn, ys_ref[...].astype(f32), u_ref[...].astype(f32), of_ref[...], g_ref[...].astype(f32), d_ref[...],
                                       b_ref[...], mx_ref[2:3], jnp.zeros((R, S5W), f32), jnp.zeros((R, D), f32), has_aux=True)
            dy, dud, do, dg, dd, db, dg1, dz, dmix = vjp(dx1_ref[...])
            dy_ref[...], dud_ref[...], do_ref[...], dg_ref[...] = dy.astype(bf16), dud, do, dg
            cat_ref[...], dmix_ref[...] = cat.astype(bf16), dmix.astype(bf16)
            s_ref[...], dz_ref[...] = s.astype(bf16), dz.astype(bf16)
            dd_ref[...] += dd
            db_ref[...] += db
            dg1_ref[...] += dg1

    lat = pl.BlockSpec((R, D), lambda i: (jnp.maximum(i - 1, 0), 0))
    lat5 = pl.BlockSpec((R, S5W), lambda i: (jnp.maximum(i - 1, 0), 0))
    ext = pl.BlockSpec((R, S5W), lambda i: (i, 0))
    eshape = jax.ShapeDtypeStruct((L + R, S5W), f32)
    return pl.pallas_call(
        body, name=name, grid=(nb,),
        in_specs=[lat, ext, ext, ext, ext, pl.BlockSpec((R, RW), lambda i: (i, 4)),
                  _const_spec((1, S5W)), _const_spec((1, S5W)), _const_spec((6, D)), _const_spec((S5W, S5W)), _const_spec((D, D)), lat]
        + cg.in_specs(),
        out_specs=[ext, ext, ext, ext, lat, lat, lat5, lat5, _acc_spec((1, S5W)), _acc_spec((1, S5W)), _acc_spec((1, D))]
        + cg.in_specs(),
        out_shape=[jax.ShapeDtypeStruct((L + R, S5W), bf16), eshape, eshape, eshape, jax.ShapeDtypeStruct((L, D), bf16),
                   jax.ShapeDtypeStruct((L, D), bf16), jax.ShapeDtypeStruct((L, S5W), bf16), jax.ShapeDtypeStruct((L, S5W), bf16),
                   jax.ShapeDtypeStruct((1, S5W), f32), jax.ShapeDtypeStruct((1, S5W), f32), jax.ShapeDtypeStruct((1, D), f32)]
        + cg.out_shapes(),
        scratch_shapes=cg.sems(),
        compiler_params=_CP(dimension_semantics=_ARB),
    )(x, ys, of, ob, p_ext, p_ext, dvec, bglu, modx, wglu, wout, dx1, *cg.arrays)


def _ffn_tail(gc, a, x1, gate2, fnw, pf, wdown, wdown_t, tgt):
    f = _gelu(gc) * a
    ffn = _dnn_const(f, wdown, wdown_t) + pf
    y = _rms(x1 + gate2 * ffn, fnw)
    err = y - tgt
    loss = 0.5 * jnp.sum(jnp.mean(err * err, axis=-1, keepdims=True), axis=0, keepdims=True)
    return loss, f


def _ffn_fwd(x1, tgt, nw2, modx, w_a, w_g, cw, cb, wdown, wdown_t, fnw, name):
    L = x1.shape[0]
    nb = L // RF
    per = RF // HALO

    def body(x_ref, xp_ref, xn_ref, t_ref, nw_ref, mx_ref, wa_ref, wg_ref, cw_ref, cb_ref, wd_ref, wdt_ref, fn_ref,
             dx2_ref, da_ref, dgc_ref, f_ref, dffn_ref, loss_ref, dfn_ref, dg2_ref, dcb_ref, dcw_ref):
        i = pl.program_id(0)

        @pl.when(i == 0)
        def _():
            for r in (loss_ref, dfn_ref, dg2_ref, dcb_ref, dcw_ref):
                r[...] = jnp.zeros_like(r)

        nw, sh, sc, gate2 = nw_ref[...], mx_ref[3:4], mx_ref[4:5], mx_ref[5:6]
        x1b = x_ref[...]
        h2 = _norm_mod(x1b, nw, sh, sc)
        h2e = jnp.concatenate([_norm_mod(xp_ref[...], nw, sh, sc), h2, _norm_mod(xn_ref[...], nw, sh, sc)], axis=0)
        a = dnn(h2, wa_ref[...])
        ge = dnn(h2e, wg_ref[...])
        g = ge[HALO:HALO + RF]
        gp = ge[HALO - 1:HALO] * jnp.where(i > 0, 1.0, 0.0)
        gn = ge[HALO + RF:HALO + RF + 1] * jnp.where(i < nb - 1, 1.0, 0.0)
        row = lax.broadcasted_iota(jnp.int32, (RF, 1), 0)
        g_prev = jnp.where(row == 0, gp, pltpu.roll(g, 1, axis=0))
        g_next = jnp.where(row == RF - 1, gn, pltpu.roll(g, RF - 1, axis=0))
        gc = cb_ref[...] + g_prev * cw_ref[0:1] + g * cw_ref[1:2] + g_next * cw_ref[2:3]
        fn = lambda gc_, a_, x_, g2_, fw_, pf_: _ffn_tail(gc_, a_, x_, g2_, fw_, pf_, wd_ref[...], wdt_ref[...], t_ref[...])
        loss, vjp, f = jax.vjp(fn, gc, a, x1b, gate2, fn_ref[...], jnp.zeros((RF, D), f32), has_aux=True)
        dgc, da, dx2, dg2, dfw, dffn = vjp(jnp.ones((1, 1), f32))
        dx2_ref[...] = dx2
        da_ref[...], dgc_ref[...] = da.astype(bf16), dgc
        f_ref[...], dffn_ref[...] = f.astype(bf16), dffn.astype(bf16)
        loss_ref[...] += jnp.broadcast_to(loss, (1, 128))
        dfn_ref[...] += dfw
        dg2_ref[...] += dg2
        dcb_ref[...] += jnp.sum(dgc, axis=0, keepdims=True)
        dcw_ref[0:1] += jnp.sum(dgc * g_prev, axis=0, keepdims=True)
        dcw_ref[1:2] += jnp.sum(dgc * g, axis=0, keepdims=True)
        dcw_ref[2:3] += jnp.sum(dgc * g_next, axis=0, keepdims=True)

    blk = lambda w: pl.BlockSpec((RF, w), lambda i: (i, 0))
    return pl.pallas_call(
        body, name=name, grid=(nb,),
        in_specs=[blk(D), pl.BlockSpec((HALO, D), lambda i: (jnp.maximum(i * per - 1, 0), 0)),
                  pl.BlockSpec((HALO, D), lambda i: (jnp.minimum((i + 1) * per, L // HALO - 1), 0)), blk(D),
                  _const_spec((1, D)), _const_spec((6, D)), _const_spec((D, DFF)), _const_spec((D, DFF)), _const_spec((3, DFF)),
                  _const_spec((1, DFF)), _const_spec((DFF, D)), _const_spec((D, DFF)), _const_spec((1, D))],
        out_specs=[blk(D), blk(DFF), blk(DFF), blk(DFF), blk(D), _acc_spec((1, 128)), _acc_spec((1, D)), _acc_spec((1, D)),
                   _acc_spec((1, DFF)), _acc_spec((3, DFF))],
        out_shape=[jax.ShapeDtypeStruct((L, D), f32), jax.ShapeDtypeStruct((L, DFF), bf16), jax.ShapeDtypeStruct((L, DFF), f32),
                   jax.ShapeDtypeStruct((L, DFF), bf16), jax.ShapeDtypeStruct((L, D), bf16), jax.ShapeDtypeStruct((1, 128), f32),
                   jax.ShapeDtypeStruct((1, D), f32), jax.ShapeDtypeStruct((1, D), f32), jax.ShapeDtypeStruct((1, DFF), f32),
                   jax.ShapeDtypeStruct((3, DFF), f32)],
        compiler_params=_CP(dimension_semantics=_ARB),
    )(x1, x1, x1, tgt, nw2, modx, w_a, w_g, cw, cb, wdown, wdown_t, fnw)


def _ffn_bwd(x1, dx2, da, dgc, nw2, modx, wup_t, cw, name):
    L = x1.shape[0]
    nb = L // RF
    per = RF // HALO

    def body(x_ref, dx2_ref, da_ref, dgc_ref, dgp_ref, dgn_ref, nw_ref, mx_ref, wu_ref, cw_ref,
             dx1_ref, dag_ref, h2_ref, dnw_ref, dmx_ref):
        i = pl.program_id(0)

        @pl.when(i == 0)
        def _():
            dnw_ref[...] = jnp.zeros_like(dnw_ref)
            dmx_ref[...] = jnp.zeros_like(dmx_ref)

        dgc_b = dgc_ref[...]
        before = dgp_ref[HALO - 1:HALO] * jnp.where(i > 0, 1.0, 0.0)
        after = dgn_ref[0:1] * jnp.where(i < nb - 1, 1.0, 0.0)
        row = lax.broadcasted_iota(jnp.int32, (RF, 1), 0)
        d_prev = jnp.where(row == 0, before, pltpu.roll(dgc_b, 1, axis=0))
        d_next = jnp.where(row == RF - 1, after, pltpu.roll(dgc_b, RF - 1, axis=0))
        dg = cw_ref[0:1] * d_next + cw_ref[1:2] * dgc_b + cw_ref[2:3] * d_prev
        dag = jnp.concatenate([da_ref[...], dg.astype(bf16)], axis=1)
        dag_ref[...] = dag
        dh2 = dnn(dag, wu_ref[...])
        h2, vjp = jax.vjp(_norm_mod, x_ref[...], nw_ref[...], mx_ref[3:4], mx_ref[4:5])
        dxa, dnw, dsh, dsc = vjp(dh2)
        h2_ref[...] = h2.astype(bf16)
        dx1_ref[...] = dx2_ref[...] + dxa
        dnw_ref[...] += dnw
        dmx_ref[3:4] += dsh
        dmx_ref[4:5] += dsc

    blk = lambda w: pl.BlockSpec((RF, w), lambda i: (i, 0))
    return pl.pallas_call(
        body, name=name, grid=(nb,),
        in_specs=[blk(D), blk(D), blk(DFF), blk(DFF), pl.BlockSpec((HALO, DFF), lambda i: (jnp.maximum(i * per - 1, 0), 0)),
                  pl.BlockSpec((HALO, DFF), lambda i: (jnp.minimum((i + 1) * per, L // HALO - 1), 0)),
                  _const_spec((1, D)), _const_spec((6, D)), _const_spec((2 * DFF, D)), _const_spec((3, DFF))],
        out_specs=[blk(D), blk(2 * DFF), blk(D), _acc_spec((1, D)), _acc_spec((6, D))],
        out_shape=[jax.ShapeDtypeStruct((L, D), f32), jax.ShapeDtypeStruct((L, 2 * DFF), bf16), jax.ShapeDtypeStruct((L, D), bf16),
                   jax.ShapeDtypeStruct((1, D), f32), jax.ShapeDtypeStruct((6, D), f32)],
        compiler_params=_CP(dimension_semantics=_ARB),
    )(x1, dx2, da, dgc, dgc, dgc, nw2, modx, wup_t, cw)


def _matmul_tn(a, b, name, cargo=None):
    k, m = a.shape
    n = b.shape[1]
    divs = lambda d: [c for c in range(d, 0, -128) if d % c == 0]
    _, tm, tn = min((m * (n // cn) + n * (m // cm), cm, cn) for cm in divs(m) for cn in divs(n) if cm * cn * 4 <= ACC_TILE_BYTES)
    tk = next(c for c in (512, 768, 256, 128) if k % c == 0)
    nk = k // tk
    grid = (m // tm, n // tn, nk)
    cg = _Cargo(cargo)

    def body(*refs):
        (a_ref, b_ref), (o_ref,), (acc,) = cg.split(refs, 2, 1, 1)
        cg.ride(refs, 2, 1, grid)
        q = pl.program_id(2)

        @pl.when(q == 0)
        def _():
            acc[...] = jnp.zeros_like(acc)

        acc[...] += dtn(a_ref[...], b_ref[...])

        @pl.when(q == nk - 1)
        def _():
            o_ref[...] = acc[...].astype(bf16)

    out = pl.pallas_call(
        body, name=name, grid=grid,
        in_specs=[pl.BlockSpec((tk, tm), lambda i, j, q: (q, i)), pl.BlockSpec((tk, tn), lambda i, j, q: (q, j))] + cg.in_specs(),
        out_specs=[pl.BlockSpec((tm, tn), lambda i, j, q: (i, j))] + cg.in_specs(),
        out_shape=[jax.ShapeDtypeStruct((m, n), bf16)] + cg.out_shapes(),
        scratch_shapes=[pltpu.VMEM((tm, tn), f32)] + cg.sems(),
        compiler_params=_CP(dimension_semantics=("arbitrary",) * 3 if cg.n else ("parallel", "parallel", "arbitrary")),
    )(a, b, *cg.arrays)
    return out if cg.n else out[0]


def _adamw_refs(w_ref, g_ref, m_ref, v_ref, d_ref, nm_ref, nv_ref):
    c1, c2 = 1.0 - B1 ** STEP, 1.0 - B2 ** STEP
    gg = g_ref[...]
    nm = B1 * m_ref[...] + (1.0 - B1) * gg
    nv = B2 * v_ref[...] + (1.0 - B2) * jnp.square(gg)
    d_ref[...] = -LR * ((nm / c1) / (jnp.sqrt(nv / c2) + AEPS) + WD * w_ref[...])
    nm_ref[...], nv_ref[...] = nm, nv


def _adamw(w, g, m, v, name):
    def body(*refs):
        _adamw_refs(*refs)

    return pl.pallas_call(body, name=name, out_shape=[jax.ShapeDtypeStruct(w.shape, f32)] * 3, compiler_params=_CP())(w, g, m, v)


def _adamw_landed(land, w, m, v, name):
    def body(l_ref, w_ref, m_ref, v_ref, g_ref, d_ref, nm_ref, nv_ref):
        acc = l_ref[0].astype(f32)
        for j in range(1, NDEV):
            acc = acc + l_ref[j].astype(f32)
        g_ref[...] = acc
        _adamw_refs(w_ref, g_ref, m_ref, v_ref, d_ref, nm_ref, nv_ref)

    return pl.pallas_call(body, name=name, out_shape=[jax.ShapeDtypeStruct(w.shape, f32)] * 4, compiler_params=_CP())(land, w, m, v)


def _adamw_many(ws, gs, ms, vs, name):
    n = len(ws)

    def body(*refs):
        for k in range(n):
            _adamw_refs(*[refs[j * n + k] for j in range(7)])

    outs = pl.pallas_call(body, name=name, out_shape=[jax.ShapeDtypeStruct(w.shape, f32) for w in ws] * 3,
                          compiler_params=_CP())(*ws, *gs, *ms, *vs)
    return outs[:n], outs[n:2 * n], outs[2 * n:]


SMALL = ["conv_w", "c_ctx", "norm1_w", "s5_lambda_re_f", "s5_lambda_im_f", "s5_log_step_f", "s5_lambda_re_b", "s5_lambda_im_b",
         "s5_log_step_b", "s5_b_re", "s5_b_im", "s5_c_re", "s5_c_im", "s5_d", "s5_b_glu", "ret_log_decay_f", "ret_log_decay_b",
         "norm2_w", "conv_b", "final_norm_w"]
WEIGHTS = ["c_ctx", "w_mod", "b_mod", "norm1_w", "w_in", "s5_lambda_re_f", "s5_lambda_im_f", "s5_log_step_f", "s5_lambda_re_b",
           "s5_lambda_im_b", "s5_log_step_b", "s5_b_re", "s5_b_im", "s5_c_re", "s5_c_im", "s5_d", "s5_w_glu", "s5_b_glu",
           "ret_log_decay_f", "ret_log_decay_b", "w_out", "norm2_w", "w_up", "conv_w", "conv_b", "w_down", "final_norm_w"]


def _pack_small(vals):
    flat, offs, o = [], [], 0
    for a in vals:
        n = a.size
        npad = -n % 128
        flat.append(jnp.pad(a.reshape(-1), (0, npad)))
        offs.append((o, n))
        o += n + npad
    tail = -o % 1024
    if tail:
        flat.append(jnp.zeros((tail,), f32))
    return jnp.concatenate(flat).reshape(-1, 128), offs


def _unpack_small(packed, offs, shapes):
    flat = packed.reshape(-1)
    return [flat[o:o + n].reshape(s) for (o, n), s in zip(offs, shapes)]


def _rope_tables(L, nctx_rows):
    t = np.arange(L)
    inv = (ROPE_THETA ** (-np.arange(DH // 4, dtype=np.float64) / (DH // 4))).astype(np.float32)
    ang = np.concatenate([(t // GRID_W).astype(np.float32)[:, None] * inv, (t % GRID_W).astype(np.float32)[:, None] * inv], axis=-1)
    cos = np.repeat(np.cos(ang).astype(np.float32), 2, axis=1)
    sin = np.repeat(np.sin(ang).astype(np.float32), 2, axis=1) * np.tile(np.array([-1.0, 1.0], np.float32), DH // 2)
    cosf = np.concatenate([np.ones((nctx_rows, DH), np.float32), cos], axis=0)
    sins = np.concatenate([np.zeros((nctx_rows, DH), np.float32), sin], axis=0)
    return jnp.asarray(cosf), jnp.asarray(sins)


def kernel(x, c, ctx, c_ctx, w_mod, b_mod, norm1_w, w_in, s5_lambda_re_f, s5_lambda_im_f, s5_log_step_f, s5_lambda_re_b, s5_lambda_im_b, s5_log_step_b, s5_b_re, s5_b_im, s5_c_re, s5_c_im, s5_d, s5_w_glu, s5_b_glu, ret_log_decay_f, ret_log_decay_b, w_out, norm2_w, w_up, conv_w, conv_b, w_down, final_norm_w, loss_target, m_c_ctx, m_w_mod, m_b_mod, m_norm1_w, m_w_in, m_s5_lambda_re_f, m_s5_lambda_im_f, m_s5_log_step_f, m_s5_lambda_re_b, m_s5_lambda_im_b, m_s5_log_step_b, m_s5_b_re, m_s5_b_im, m_s5_c_re, m_s5_c_im, m_s5_d, m_s5_w_glu, m_s5_b_glu, m_ret_log_decay_f, m_ret_log_decay_b, m_w_out, m_norm2_w, m_w_up, m_conv_w, m_conv_b, m_w_down, m_final_norm_w, v_c_ctx, v_w_mod, v_b_mod, v_norm1_w, v_w_in, v_s5_lambda_re_f, v_s5_lambda_im_f, v_s5_log_step_f, v_s5_lambda_re_b, v_s5_lambda_im_b, v_s5_log_step_b, v_s5_b_re, v_s5_b_im, v_s5_c_re, v_s5_c_im, v_s5_d, v_s5_w_glu, v_s5_b_glu, v_ret_log_decay_f, v_ret_log_decay_b, v_w_out, v_norm2_w, v_w_up, v_conv_w, v_conv_b, v_w_down, v_final_norm_w):
    args = dict(locals())
    W = {n: args[n] for n in WEIGHTS}
    M = {n: args["m_" + n] for n in WEIGHTS}
    V = {n: args["v_" + n] for n in WEIGHTS}
    me = _me()
    x2, ctx2, tgt = x[0], ctx[0], loss_target[0]
    L, Lc = x2.shape[0], ctx2.shape[0]
    assert Lc == R and L % R == 0 and L % GRID_W == 0
    nctx = Lc // T

    w_in_tl, w_up_tl = w_in[0].T.astype(bf16), w_up[0].T.astype(bf16)
    w_out_l, w_down_l, w_glu_l = w_out[0].astype(bf16), w_down[0].astype(bf16), s5_w_glu[0].astype(bf16)
    per_cv = conv_w.shape[2]
    conv_pad = jnp.pad(conv_w[0], ((0, 5), (0, 128 * 3 - per_cv)))
    w_in_g, c_g, conv_g = _gather_two_level([w_in_tl, jnp.pad(c, ((0, 7), (0, 0))), conv_pad], "gather_w_in")
    w_in_t = w_in_g.reshape(INC, D)
    conv_f = conv_g[:, :3, :per_cv].transpose(1, 0, 2).reshape(3, DFF)

    c9 = jnp.concatenate([c_g[:, 0, :], c_ctx[None], jnp.zeros((7, D), f32)], axis=0)
    w_mod_l = w_mod[0]
    ncol = w_mod_l.shape[1]
    m_part = _ada_fwd(c9, w_mod_l, "ada_fwd")
    m_all = _all_gather_small(m_part, "gather_mod").transpose(1, 0, 2).reshape(16, 6, D)
    modx, modc = _mod_select(m_all, b_mod.reshape(6, D), "mod_select")

    pair = lambda a, b: jnp.concatenate([a, b], axis=-1)
    bre_g, bim_g = s5_b_re[0].transpose(0, 2, 1), s5_b_im[0].transpose(0, 2, 1)
    cre_g, cim_g = s5_c_re[0], s5_c_im[0]
    shared = (pair(bre_g, bim_g), pair(bim_g, bre_g), pair(cre_g, cim_g), pair(cim_g, cre_g))
    s5p = {}
    for tag, lre, lim, ls in (("f", s5_lambda_re_f, s5_lambda_im_f, s5_log_step_f), ("b", s5_lambda_re_b, s5_lambda_im_b, s5_log_step_b)):
        s5p[tag] = (pair(lre[0], lre[0])[:, None, :], pair(lim[0], lim[0])[:, None, :], ls[0].reshape(S5G, 1, 1)) + shared
    m_f, mb_f, mc_f, a1_f, a2_f = _s5_build(s5p["f"], False, "s5_build_f")
    m_b, mb_b, mc_b, a1_b, a2_b = _s5_build(s5p["b"], True, "s5_build_b")
    a1_f, a2_f, a1_b, a2_b = (a.reshape(S5G, SB) for a in (a1_f, a2_f, a1_b, a2_b))

    nw1, nw2, fnw = norm1_w, norm2_w, final_norm_w[None]
    cosf, sins = _rope_tables(L, Lc)
    p_ext, w_out_g, w_glu_g, w_up_g1 = _f1_fwd(x2, ctx2, modx, modc, nw1, w_in_t.T, cosf, sins, "f1_fwd",
                                               cargo=([w_out_l, w_glu_l, w_up_tl[:UP_HEAD]], False))
    nctx5 = Lc // TC
    u_g = _to_groups(p_ext[:, :S5W])
    s_f, s_b = _s5_inc(u_g, mb_f, mb_b, "s5_inc")
    hp_f, hp_b = _s5_carry(s_f, s_b, (a1_f, a2_f), (a1_b, a2_b), nctx5, "s5_carry")
    ys = _from_groups(_s5_out(u_g, m_f, m_b, hp_f, hp_b, mc_f, mc_b, "s5_out"))
    ld8 = lambda ld: jnp.pad(jnp.broadcast_to(ld[0][:, None], (RH, 128)), ((0, 8 - RH), (0, 0)))
    ldf8, ldb8 = ld8(ret_log_decay_f), ld8(ret_log_decay_b)
    of, ob, rp_f, rp_b, w_up_g2 = _ret_fwd(p_ext, ldf8, ldb8, nctx, "ret_fwd", cargo=([w_up_tl[UP_HEAD:]], False))
    w_out_f, w_glu_f = w_out_g.reshape(D, D), w_glu_g.reshape(S5W, S5W)
    x1, w_down_g = _mix_fwd(x2, ys, of, ob, p_ext, s5_d, s5_b_glu, modx, w_glu_f, w_out_f, "mix_fwd", cargo=([w_down_l], False))
    w_down_f = w_down_g.reshape(DFF, D)
    w_up_t = jnp.concatenate([w_up_g1, w_up_g2], axis=1).reshape(2 * DFF, D)

    (dx2, da, dgc, f_act, dffn, loss_acc, g_fnw, g_gate2, g_cb, g_cw) = _ffn_fwd(
        x1, tgt, nw2, modx, w_up_t[:DFF].T, w_up_t[DFF:].T, conv_f, conv_b, w_down_f, w_down_f.T, fnw, "ffn_fwd")
    dx1, dag, h2, g_nw2, dmx2 = _ffn_bwd(x1, dx2, da, dgc, nw2, modx, w_up_t, conv_f, "ffn_bwd")
    gw_down = _matmul_tn(f_act, dffn, "dw_down").reshape(NDEV, -1, D)
    gw_up_t = _matmul_tn(dag, h2, "dw_up").reshape(NDEV, -1, D)
    (dy_e, dud_e, do_e, dg_e, cat, dmix, s_act, dz, g_d, g_bglu, g_gate1, l_down) = _mix_bwd(
        x2, ys, of, ob, p_ext, s5_d, s5_b_glu, modx, w_glu_f, w_out_f, dx1, "mix_bwd", cargo=([gw_down], True))
    gw_out = _matmul_tn(cat, dmix, "dw_out").reshape(NDEV, -1, D)
    gw_glu = _matmul_tn(s_act, dz, "dw_glu").reshape(NDEV, -1, S5W)
    dq_f, dk_f, dv_f, dq_b, dk_b, dv_b, gld_f, gld_b, l_up, l_out, l_glu = _ret_bwd(
        p_ext, ldf8, ldb8, rp_f, rp_b, do_e, nctx, "ret_bwd", cargo=([gw_up_t, gw_out, gw_glu], True))

    du1, g_m, dhp_f, dhp_b, dmc_f, dmc_b = _s5_out_bwd(_to_groups(dy_e), u_g, m_f, m_b, hp_f, hp_b, mc_f, mc_b, "s5_out_bwd")
    ds_f, da1_f, da2_f = _s5_carry_bwd(dhp_f, hp_f, a1_f, a2_f, False, nctx5, "s5_carry_bwd_f")
    ds_b, da1_b, da2_b = _s5_carry_bwd(dhp_b, hp_b, a1_b, a2_b, True, nctx5, "s5_carry_bwd_b")
    du_g, dmb_f, dmb_b = _s5_inc_bwd(du1, u_g, ds_f, ds_b, mb_f, mb_b, "s5_inc_bwd")
    zero_p = jnp.zeros((S5G, S5P, SB), f32)
    gf = _s5_build_bwd(s5p["f"], (g_m, dmb_f, dmc_f, da1_f[:, None, :], da2_f[:, None, :]), (zero_p, zero_p), False, "s5_build_bwd_f")
    gb = _s5_build_bwd(s5p["b"], (g_m, dmb_b, dmc_b, da1_b[:, None, :], da2_b[:, None, :]), (gf[3], gf[4]), True, "s5_build_bwd_b")
    g_bre, g_bim = gb[3][:, :, :S5N].transpose(0, 2, 1), gb[3][:, :, S5N:].transpose(0, 2, 1)
    g_cre, g_cim = gb[4][:, :, :S5N], gb[4][:, :, S5N:]

    early = {
        "conv_w": g_cw, "s5_lambda_re_f": gf[0][:, 0, :S5N], "s5_lambda_im_f": gf[1][:, 0, :S5N],
        "s5_log_step_f": gf[2], "s5_lambda_re_b": gb[0][:, 0, :S5N], "s5_lambda_im_b": gb[1][:, 0, :S5N], "s5_log_step_b": gb[2],
        "s5_b_re": g_bre, "s5_b_im": g_bim, "s5_c_re": g_cre, "s5_c_im": g_cim, "s5_d": g_d, "s5_b_glu": g_bglu,
        "ret_log_decay_f": gld_f[:RH, 0], "ret_log_decay_b": gld_b[:RH, 0], "norm2_w": g_nw2, "conv_b": g_cb, "final_norm_w": g_fnw,
    }
    e_names = [n for n in SMALL if n in early]
    packed_e, eoffs = _pack_small([early[n].astype(f32) for n in e_names])
    grad_x, dp_ext, h1, g_nw1, dmx1, dmc1 = _f1_bwd(
        x2, ctx2, modx, modc, nw1, w_in_t, cosf, sins, dx1, (_from_groups(du_g), dud_e, dq_f, dq_b, dk_f, dk_b, dv_f, dv_b, dg_e), "f1_bwd")
    gw_in_t, land_e = _matmul_tn(dp_ext, h1, "dw_in", cargo=([packed_e], False))
    g_in_t = _reduce_scatter_two_level(gw_in_t.reshape(NDEV, -1, D), "scatter_dw_in")

    dmx = dmx1 + dmx2
    dmx = dmx.at[2].set(g_gate1[0]).at[5].set(g_gate2[0])
    dm_me = jnp.stack([dmx.reshape(-1), dmc1.reshape(-1)], axis=0)
    dm_all = _all_gather_small(dm_me.reshape(8, -1), "gather_dmod").reshape(NDEV, 2, 6 * D)
    dmx_all, dmc_all = dm_all[:, 0, :], dm_all[:, 1, :]
    my_cols = lambda a: lax.dynamic_slice(a, (0, me * ncol), (NDEV, ncol))
    gw_mod, g_bmod, dc9 = _ada_bwd(c9, dmx_all, dmc_all, my_cols(dmx_all), my_cols(dmc_all), w_mod_l, "ada_bwd")

    sshape = lambda n: (3, DFF) if n == "conv_w" else W[n].shape
    G = dict(zip(e_names, _unpack_small(_sum8(land_e, "reduce_early"), eoffs, [sshape(n) for n in e_names])))
    late = {"c_ctx": dc9[8], "norm1_w": g_nw1}
    packed_l, loffs = _pack_small([late[n].astype(f32) for n in late])
    G.update(zip(late, _unpack_small(_all_reduce_small(packed_l, "reduce_late"), loffs, [W[n].shape for n in late])))
    G["conv_w"] = lax.dynamic_slice(G["conv_w"], (0, me * per_cv), (3, per_cv))[None]
    G["b_mod"] = g_bmod.reshape(b_mod.shape)
    G["w_mod"] = gw_mod[None]
    G["w_in"] = g_in_t.T[None]
    G["w_up"] = _sum8(l_up, "sum_dw_up").T[None]

    delta, new_m, new_v = {}, {}, {}
    sm_names = SMALL[1:] + ["b_mod"]
    rows = lambda a: a.reshape(-1, a.shape[-1])
    outs = _adamw_many(*[[rows(d[n]) for n in sm_names] for d in (W, G, M, V)], "adamw_small")
    for dst, src in zip((delta, new_m, new_v), outs):
        dst.update({n: a.reshape(W[n].shape) for n, a in zip(sm_names, src)})
    for n in ["w_mod", "w_in", "w_up", "conv_w"]:
        d, nm, nv = _adamw(W[n][0], G[n][0], M[n][0], V[n][0], "adamw_" + n)
        delta[n], new_m[n], new_v[n] = d[None], nm[None], nv[None]
    for n, land in (("w_out", l_out), ("w_down", l_down), ("s5_w_glu", l_glu)):
        g, d, nm, nv = _adamw_landed(land, W[n][0], M[n][0], V[n][0], "adamw_" + n)
        G[n], delta[n], new_m[n], new_v[n] = g[None], d[None], nm[None], nv[None]

    loss = lax.psum(loss_acc[0, 0], ("x", "y", "c"))
    return (loss, grad_x[None], *[G[n] for n in WEIGHTS], *[delta[n] for n in WEIGHTS], *[new_m[n] for n in WEIGHTS],
            *[new_v[n] for n in WEIGHTS])
```

```python
import functools

import numpy as np
import jax
import jax.numpy as jnp
from jax import lax
from jax.experimental import pallas as pl
from jax.experimental.pallas import tpu as pltpu

f32, bf16 = jnp.float32, jnp.bfloat16

D = 1024
S5W, S5G, S5P, S5N = 512, 32, 16, 64
TC = 16
TCP = TC * S5P
SB = 2 * S5N
GBK = 8
UP_HEAD = 192
CARRY_UNROLL = 8
RH, DH = 4, 128
RW = RH * DH
INC = S5W + 4 * RW
DFF = 2816
T = 128
R = 256
RF = 128
RFB = 256
HALO = 8
EPS = 1e-6
ROPE_THETA = 10000.0
GRID_W = 64
NDEV = 8
LR, B1, B2, AEPS, WD, STEP = 0.001, 0.9, 0.999, 1e-08, 0.01, 10
VMEM_LIMIT = 60 * 1024 * 1024
ACC_TILE_BYTES = 6 * 1024 * 1024
MESH = pl.DeviceIdType.MESH

_CP = functools.partial(pltpu.CompilerParams, vmem_limit_bytes=VMEM_LIMIT)
_ARB = ("arbitrary",)
_ANY = pl.BlockSpec(memory_space=pl.ANY)


def _dg(a, b, dims):
    return lax.dot_general(a.astype(bf16), b.astype(bf16), (dims, ((), ())), preferred_element_type=f32)


@jax.custom_vjp
def dnn(a, b):
    return _dg(a, b, ((1,), (0,)))


@jax.custom_vjp
def dnt(a, b):
    return _dg(a, b, ((1,), (1,)))


@jax.custom_vjp
def dtn(a, b):
    return _dg(a, b, ((0,), (0,)))


dnn.defvjp(lambda a, b: (dnn(a, b), (a, b)), lambda r, g: (dnt(g, r[1]).astype(r[0].dtype), dtn(r[0], g).astype(r[1].dtype)))
dnt.defvjp(lambda a, b: (dnt(a, b), (a, b)), lambda r, g: (dnn(g, r[1]).astype(r[0].dtype), dtn(g, r[0]).astype(r[1].dtype)))
dtn.defvjp(lambda a, b: (dtn(a, b), (a, b)), lambda r, g: (dnt(r[1], g).astype(r[0].dtype), dnn(r[0], g).astype(r[1].dtype)))


@jax.custom_vjp
def _dnn_const(a, w, wt):
    return dnn(a, w)


_dnn_const.defvjp(lambda a, w, wt: (dnn(a, w), wt), lambda wt, g: (dnn(g, wt), None, None))


_GELU_C0, _GELU_C1 = float(np.sqrt(2.0 / np.pi)), 0.044715


@jax.custom_vjp
def _gelu(x):
    return _gelu_fwd(x)[0]


def _gelu_fwd(x):
    t = jnp.tanh(_GELU_C0 * (x + _GELU_C1 * (x * x * x)))
    return x * (0.5 * (1.0 + t)), (x, t)


def _gelu_bwd(res, g):
    x, t = res
    return (g * (0.5 * (1.0 + t) + (0.5 * _GELU_C0) * x * (1.0 - t * t) * (1.0 + (3.0 * _GELU_C1) * (x * x))),)


_gelu.defvjp(_gelu_fwd, _gelu_bwd)


def _rms(t, w):
    return t * lax.rsqrt(jnp.mean(t * t, axis=-1, keepdims=True) + EPS) * w


@jax.custom_vjp
def _norm_mod(x, w, shift, scale):
    return _norm_mod_fwd(x, w, shift, scale)[0]


def _norm_mod_fwd(x, w, shift, scale):
    r = lax.rsqrt(jnp.mean(x * x, axis=-1, keepdims=True) + EPS)
    n = x * r
    return (n * w) * (1.0 + scale) + shift, (n, r, w, scale)


def _norm_mod_bwd(res, dh):
    n, r, w, scale = res
    col = jnp.sum(dh * n, axis=0, keepdims=True)
    dn = dh * (w * (1.0 + scale))
    dx = r * (dn - n * jnp.mean(dn * n, axis=-1, keepdims=True))
    return dx, col * (1.0 + scale), jnp.sum(dh, axis=0, keepdims=True), col * w


_norm_mod.defvjp(_norm_mod_fwd, _norm_mod_bwd)


def _const_spec(shape):
    n = len(shape)
    return pl.BlockSpec(shape, lambda i, _n=n: (0,) * _n, pipeline_mode=pl.Buffered(1))


def _acc_spec(shape):
    n = len(shape)
    return pl.BlockSpec(shape, lambda i, _n=n: (0,) * _n)


def _me():
    return 4 * lax.axis_index("x") + 2 * lax.axis_index("y") + lax.axis_index("c")


def _peer(r):
    x, y, c = lax.axis_index("x"), lax.axis_index("y"), lax.axis_index("c")
    px = 1 - x if (r >> 2) & 1 else x
    py = 1 - y if (r >> 1) & 1 else y
    pc = 1 - c if r & 1 else c
    return (px, py, pc), 4 * px + 2 * py + pc


def _all_gather_small(v, name):
    r, c = v.shape

    def body(v_ref, out_ref, send_sems, recv_sems):
        me = _me()
        out_ref[me] = v_ref[...]
        sends = []
        for k in range(1, NDEV):
            peer, _ = _peer(k)
            cp = pltpu.make_async_remote_copy(src_ref=v_ref, dst_ref=out_ref.at[me], send_sem=send_sems.at[k - 1],
                                              recv_sem=recv_sems.at[k - 1], device_id=peer, device_id_type=MESH)
            cp.start()
            sends.append(cp)
        for k in range(1, NDEV):
            peer, pidx = _peer(k)
            pltpu.make_async_remote_copy(src_ref=v_ref, dst_ref=out_ref.at[pidx], send_sem=send_sems.at[k - 1],
                                         recv_sem=recv_sems.at[k - 1], device_id=peer, device_id_type=MESH).wait_recv()
        for cp in sends:
            cp.wait_send()

    return pl.pallas_call(
        body, name=name, out_shape=jax.ShapeDtypeStruct((NDEV, r, c), v.dtype),
        in_specs=[pl.BlockSpec(memory_space=pltpu.VMEM)], out_specs=pl.BlockSpec(memory_space=pltpu.VMEM),
        scratch_shapes=[pltpu.SemaphoreType.DMA((NDEV - 1,)), pltpu.SemaphoreType.DMA((NDEV - 1,))],
        compiler_params=_CP(),
    )(v)


def _all_reduce_small(v, name):
    r, c = v.shape

    def body(v_ref, out_ref, land, send_sems, recv_sems):
        me = _me()
        land[me] = v_ref[...]
        sends = []
        for k in range(1, NDEV):
            peer, _ = _peer(k)
            cp = pltpu.make_async_remote_copy(src_ref=v_ref, dst_ref=land.at[me], send_sem=send_sems.at[k - 1],
                                              recv_sem=recv_sems.at[k - 1], device_id=peer, device_id_type=MESH)
            cp.start()
            sends.append(cp)
        for k in range(1, NDEV):
            peer, pidx = _peer(k)
            pltpu.make_async_remote_copy(src_ref=v_ref, dst_ref=land.at[pidx], send_sem=send_sems.at[k - 1],
                                         recv_sem=recv_sems.at[k - 1], device_id=peer, device_id_type=MESH).wait_recv()
        for cp in sends:
            cp.wait_send()
        acc = land[0]
        for j in range(1, NDEV):
            acc = acc + land[j]
        out_ref[...] = acc

    return pl.pallas_call(
        body, name=name, out_shape=jax.ShapeDtypeStruct((r, c), v.dtype),
        in_specs=[pl.BlockSpec(memory_space=pltpu.VMEM)], out_specs=pl.BlockSpec(memory_space=pltpu.VMEM),
        scratch_shapes=[pltpu.VMEM((NDEV, r, c), v.dtype), pltpu.SemaphoreType.DMA((NDEV - 1,)),
                        pltpu.SemaphoreType.DMA((NDEV - 1,))],
        compiler_params=_CP(),
    )(v)


class _Exchange:
    def __init__(self, srcs, dsts, send_sems, recv_sems, local_sems, scatter):
        me = _me()
        n = len(srcs)
        self.sends, self.recvs, self.locals = [], [], []
        for a, (s, d) in enumerate(zip(srcs, dsts)):
            self.locals.append(pltpu.make_async_copy(s.at[me] if scatter else s, d.at[me], local_sems.at[a]))
        for k in range(1, NDEV):
            peer, pidx = _peer(k)
            for a, (s, d) in enumerate(zip(srcs, dsts)):
                src = s.at[pidx] if scatter else s
                sem = (k - 1) * n + a
                for dst, out in ((d.at[me], self.sends), (d.at[pidx], self.recvs)):
                    out.append(pltpu.make_async_remote_copy(src_ref=src, dst_ref=dst, send_sem=send_sems.at[sem],
                                                            recv_sem=recv_sems.at[sem], device_id=peer, device_id_type=MESH))

    def start(self):
        for cp in self.locals + self.sends:
            cp.start()

    def wait(self):
        for cp in self.recvs:
            cp.wait_recv()
        for cp in self.sends:
            cp.wait_send()
        for cp in self.locals:
            cp.wait()


def _exchange_shapes(arrays, scatter):
    return [jax.ShapeDtypeStruct(a.shape if scatter else (NDEV,) + a.shape, a.dtype) for a in arrays]


def _exchange_sems(n):
    return [pltpu.SemaphoreType.DMA(((NDEV - 1) * n,)), pltpu.SemaphoreType.DMA(((NDEV - 1) * n,)), pltpu.SemaphoreType.DMA((n,))]


def _chips():
    x, y, c = lax.axis_index("x"), lax.axis_index("y"), lax.axis_index("c")
    return (x, y, c), (x, y, 1 - c), [(1 - x, y), (x, 1 - y), (1 - x, 1 - y)]


def _gather_two_level(arrays, name):
    n = len(arrays)

    def body(*refs):
        srcs, outs = refs[:n], refs[n:2 * n]
        send_sems, recv_sems = refs[2 * n:]
        me, sibling, chips = _chips()
        c = me[2]
        idx = lambda p: 4 * p[0] + 2 * p[1] + p[2]

        def copy(a, k, block, to, src=None):
            return pltpu.make_async_remote_copy(
                src_ref=outs[a].at[idx(block)] if src is None else src, dst_ref=outs[a].at[idx(block)],
                send_sem=send_sems.at[7 * a + k], recv_sem=recv_sems.at[7 * a + k], device_id=to, device_id_type=MESH)

        first, passed = [], []
        for a in range(n):
            outs[a][idx(me)] = srcs[a][...]
            first += [copy(a, 0, me, sibling, src=srcs[a])]
            first += [copy(a, 1 + j, me, (*chip, c), src=srcs[a]) for j, chip in enumerate(chips)]
        for cp in first:
            cp.start()
        for a in range(n):
            for j, chip in enumerate(chips):
                copy(a, 1 + j, (*chip, c), me).wait_recv()
                cp = copy(a, 4 + j, (*chip, c), sibling)
                cp.start()
                passed.append(cp)
        for a in range(n):
            copy(a, 0, sibling, me).wait_recv()
            for j, chip in enumerate(chips):
                copy(a, 4 + j, (*chip, 1 - c), me).wait_recv()
        for cp in first + passed:
            cp.wait_send()

    vm = pl.BlockSpec(memory_space=pltpu.VMEM)
    return pl.pallas_call(
        body, name=name, out_shape=[jax.ShapeDtypeStruct((NDEV,) + a.shape, a.dtype) for a in arrays],
        in_specs=[vm] * n, out_specs=[vm] * n,
        scratch_shapes=[pltpu.SemaphoreType.DMA((7 * n,)), pltpu.SemaphoreType.DMA((7 * n,))],
        compiler_params=_CP(),
    )(*arrays)


def _reduce_scatter_two_level(g, name):
    _, r, c = g.shape
    nchip = NDEV // 2

    def body(g_ref, o_ref, stage, part, land, d_send, d_recv, i_send, i_recv):
        me, sibling, chips = _chips()
        x, y, cc = me
        mine = 2 * x + y

        def blk(k, core):
            return 2 * k + core

        swaps = [pltpu.make_async_remote_copy(src_ref=g_ref.at[blk(k, 1 - cc)], dst_ref=stage.at[k], send_sem=d_send.at[k],
                                              recv_sem=d_recv.at[k], device_id=sibling, device_id_type=MESH) for k in range(nchip)]
        for cp in swaps:
            cp.start()
        for cp in swaps:
            cp.wait_recv()
        for k in range(nchip):
            part[k] = (g_ref[blk(k, cc)].astype(f32) + stage[k].astype(f32)).astype(bf16)
        sends = []
        for j, chip in enumerate(chips):
            kd = 2 * chip[0] + chip[1]
            cp = pltpu.make_async_remote_copy(src_ref=part.at[kd], dst_ref=land.at[mine], send_sem=i_send.at[j],
                                              recv_sem=i_recv.at[j], device_id=(*chip, cc), device_id_type=MESH)
            cp.start()
            sends.append(cp)
        land[mine] = part[mine]
        for j, chip in enumerate(chips):
            ks = 2 * chip[0] + chip[1]
            pltpu.make_async_remote_copy(src_ref=part.at[ks], dst_ref=land.at[ks], send_sem=i_send.at[j], recv_sem=i_recv.at[j],
                                         device_id=(*chip, cc), device_id_type=MESH).wait_recv()
        for cp in swaps + sends:
            cp.wait_send()
        acc = land[0].astype(f32)
        for k in range(1, nchip):
            acc = acc + land[k].astype(f32)
        o_ref[...] = acc

    vm = pl.BlockSpec(memory_space=pltpu.VMEM)
    return pl.pallas_call(
        body, name=name, out_shape=jax.ShapeDtypeStruct((r, c), f32), in_specs=[vm], out_specs=vm,
        scratch_shapes=[pltpu.VMEM((nchip, r, c), g.dtype)] * 3 + [pltpu.SemaphoreType.DMA((nchip,)), pltpu.SemaphoreType.DMA((nchip,)),
                                                                   pltpu.SemaphoreType.DMA((3,)), pltpu.SemaphoreType.DMA((3,))],
        compiler_params=_CP(),
    )(g)


class _Cargo:
    def __init__(self, cargo):
        self.arrays, self.scatter = cargo if cargo else ([], False)
        self.n = len(self.arrays)

    def in_specs(self):
        return [_ANY] * self.n

    def out_shapes(self):
        return _exchange_shapes(self.arrays, self.scatter)

    def sems(self):
        return _exchange_sems(self.n) if self.n else []

    def split(self, refs, n_in, n_out, n_scratch):
        n = self.n
        return refs[:n_in], refs[n_in + n:n_in + n + n_out], refs[n_in + 2 * n + n_out:n_in + 2 * n + n_out + n_scratch]

    def ride(self, refs, n_in, n_out, grid):
        if not self.n:
            return
        n = self.n
        ex = _Exchange(refs[n_in:n_in + n], refs[n_in + n + n_out:n_in + 2 * n + n_out], *refs[-3:], self.scatter)
        grid = (grid,) if isinstance(grid, int) else tuple(grid)
        first = functools.reduce(jnp.logical_and, [pl.program_id(a) == 0 for a in range(len(grid))])
        last = functools.reduce(jnp.logical_and, [pl.program_id(a) == g - 1 for a, g in enumerate(grid)])

        @pl.when(first)
        def _():
            ex.start()

        @pl.when(last)
        def _():
            ex.wait()


def _sum8(land, name):
    _, r, c = land.shape
    rb = next((b for b in (256, 64, 32) if r % b == 0), r)

    def body(l_ref, o_ref):
        acc = l_ref[0].astype(f32)
        for j in range(1, NDEV):
            acc = acc + l_ref[j].astype(f32)
        o_ref[...] = acc

    return pl.pallas_call(
        body, name=name, grid=(r // rb,), out_shape=jax.ShapeDtypeStruct((r, c), f32),
        in_specs=[pl.BlockSpec((NDEV, rb, c), lambda i: (0, i, 0))], out_specs=pl.BlockSpec((rb, c), lambda i: (i, 0)),
        compiler_params=_CP(dimension_semantics=("parallel",)),
    )(land)


def _ada_fwd(c9, w_mod_l, name):
    def body(c_ref, w_ref, o_ref):
        o_ref[...] = dnn(jax.nn.silu(c_ref[...]), w_ref[...])

    return pl.pallas_call(body, name=name, out_shape=jax.ShapeDtypeStruct((16, w_mod_l.shape[1]), f32),
                          compiler_params=_CP())(c9, w_mod_l)


def _mod_select(m_all, b_mod6, name):
    def body(m_ref, b_ref, mx_ref, mc_ref):
        me = _me()
        mx_ref[...] = m_ref[me] + b_ref[...]
        mc_ref[...] = m_ref[8] + b_ref[...]

    return pl.pallas_call(body, name=name, out_shape=[jax.ShapeDtypeStruct((6, D), f32)] * 2, compiler_params=_CP())(m_all, b_mod6)


def _ada_bwd(c9, dmx_all, dmc_all, dmx_l, dmc_l, w_mod_l, name):
    ncol = w_mod_l.shape[1]

    def rowsum(r):
        acc = r[0:1]
        for j in range(1, NDEV):
            acc = acc + r[j:j + 1]
        return acc

    def body(c_ref, xa_ref, ca_ref, xl_ref, cl_ref, w_ref, gw_ref, gb_ref, dc_ref):
        s9, vjp = jax.vjp(jax.nn.silu, c_ref[...])
        dm9 = jnp.concatenate([xl_ref[...], rowsum(cl_ref[...]), jnp.zeros((7, ncol), f32)], axis=0)
        gw_ref[...] = dtn(s9, dm9)
        gb_ref[...] = rowsum(xa_ref[...]) + rowsum(ca_ref[...])
        dc_ref[...] = vjp(dnt(dm9, w_ref[...]))[0]

    return pl.pallas_call(
        body, name=name,
        out_shape=[jax.ShapeDtypeStruct((D, ncol), f32), jax.ShapeDtypeStruct((1, 6 * D), f32), jax.ShapeDtypeStruct((16, D), f32)],
        compiler_params=_CP())(c9, dmx_all, dmc_all, dmx_l, dmc_l, w_mod_l)


def _lane_sign(rank):
    shape = (1,) * (rank - 1) + (SB,)
    return jnp.where(lax.broadcasted_iota(jnp.int32, shape, rank - 1) < S5N, -1.0, 1.0)


def _s5_build_fn(lre2, lim2, ls, bn, bs, cn, cs, rev):
    sg = _lane_sign(3)
    s = jnp.exp(ls)
    ar, ai = lre2 * s, lim2 * s
    e = jnp.exp(ar)
    nr, ni = e * jnp.cos(ai) - 1.0, e * jnp.sin(ai)
    den = lre2 * lre2 + lim2 * lim2
    cr, ci = (nr * lre2 + ni * lim2) / den, (ni * lre2 - nr * lim2) / den
    bbn = cr * bn + (ci * sg) * bs
    bbs = cr * bs - (ci * sg) * bn

    def powers(ex):
        m, ang = jnp.exp(ex * ar), ex * ai
        return m * jnp.cos(ang), m * jnp.sin(ang) * sg

    def times(tabs, xn, xs):
        f1, f2 = tabs
        return f1[:, :, None, :] * xn[:, None, :, :] + f2[:, :, None, :] * xs[:, None, :, :]

    t = lax.broadcasted_iota(jnp.int32, (1, TC, 1), 1).astype(f32)
    if rev:
        e_src, e_dst, e_out, e_in = t - (TC - 1.0), (TC - 1.0) - t, t, TC - t
    else:
        e_src, e_dst, e_out, e_in = -t, t, (TC - 1.0) - t, t + 1.0
    g = lre2.shape[0]
    flat = lambda a: a.reshape(g, TCP, SB)
    conj = -_lane_sign(4)
    ll = flat(times(powers(e_src), bbn, bbs))
    rr = flat(times(powers(e_dst), cn, cs) * conj)
    mb = flat(times(powers(e_out), bbn, bbs))
    mct = flat(times(powers(e_in), cn, cs) * conj)
    a1, a2 = powers(float(TC))
    row = lax.broadcasted_iota(jnp.int32, (TCP, TCP), 0) // S5P
    col = lax.broadcasted_iota(jnp.int32, (TCP, TCP), 1) // S5P
    mask = jnp.where((col <= row) if rev else (col >= row), 1.0, 0.0)
    m = jnp.concatenate([dnt(ll[j], rr[j])[None] for j in range(g)], axis=0) * mask
    return m, mb, mct, a1, a2


def _gspec(*tail):
    nt = len(tail)
    return pl.BlockSpec((GBK,) + tail, lambda i, _n=nt: (i,) + (0,) * _n)


def _s5_build(params, rev, name):
    def body(l1, l2, ls, bn, bs, cn, cs, m_ref, mb_ref, mc_ref, a1_ref, a2_ref):
        m, mb, mct, a1, a2 = _s5_build_fn(l1[...], l2[...], ls[...], bn[...], bs[...], cn[...], cs[...], rev)
        m_ref[...], mb_ref[...], mc_ref[...] = m.astype(bf16), mb.astype(bf16), mct.astype(bf16)
        a1_ref[...], a2_ref[...] = a1, a2

    vec, pm = _gspec(1, SB), _gspec(S5P, SB)
    return pl.pallas_call(
        body, name=name, grid=(S5G // GBK,),
        in_specs=[vec, vec, _gspec(1, 1), pm, pm, pm, pm],
        out_specs=[_gspec(TCP, TCP), _gspec(TCP, SB), _gspec(TCP, SB), vec, vec],
        out_shape=[jax.ShapeDtypeStruct((S5G, TCP, TCP), bf16), jax.ShapeDtypeStruct((S5G, TCP, SB), bf16),
                   jax.ShapeDtypeStruct((S5G, TCP, SB), bf16), jax.ShapeDtypeStruct((S5G, 1, SB), f32),
                   jax.ShapeDtypeStruct((S5G, 1, SB), f32)],
        compiler_params=_CP(dimension_semantics=("parallel",)),
    )(*params)


def _s5_build_bwd(params, cots, prev, rev, name):
    def body(l1, l2, ls, bn, bs, cn, cs, dm, dmb, dmc, da1, da2, pb, pc, gl1, gl2, gls, gb, gc):
        prim = (l1[...], l2[...], ls[...], bn[...], bs[...], cn[...], cs[...])
        _, vjp = jax.vjp(functools.partial(_s5_build_fn, rev=rev), *prim)
        d1, d2, dls, dbn, dbs, dcn, dcs = vjp((dm[...], dmb[...], dmc[...], da1[...], da2[...]))
        gl1[...] = d1 + pltpu.roll(d1, S5N, axis=2)
        gl2[...] = d2 + pltpu.roll(d2, S5N, axis=2)
        gls[...] = dls
        gb[...] = dbn + pltpu.roll(dbs, S5N, axis=2) + pb[...]
        gc[...] = dcn + pltpu.roll(dcs, S5N, axis=2) + pc[...]

    vec, pm, big = _gspec(1, SB), _gspec(S5P, SB), _gspec(TCP, SB)
    return pl.pallas_call(
        body, name=name, grid=(S5G // GBK,),
        in_specs=[vec, vec, _gspec(1, 1), pm, pm, pm, pm, _gspec(TCP, TCP), big, big, vec, vec, pm, pm],
        out_specs=[vec, vec, _gspec(1, 1), pm, pm],
        out_shape=[jax.ShapeDtypeStruct((S5G, 1, SB), f32), jax.ShapeDtypeStruct((S5G, 1, SB), f32),
                   jax.ShapeDtypeStruct((S5G, 1, 1), f32), jax.ShapeDtypeStruct((S5G, S5P, SB), f32),
                   jax.ShapeDtypeStruct((S5G, S5P, SB), f32)],
        compiler_params=_CP(dimension_semantics=("parallel",)),
    )(*params, *cots, *prev)


def _s5_inc(u, mb_f, mb_b, name):
    nc = u.shape[1]

    def body(u_ref, mf_ref, mb_ref, sf_ref, sb_ref):
        for j in range(GBK):
            sf_ref[:, j, :] = jnp.dot(u_ref[j], mf_ref[j], preferred_element_type=f32)
            sb_ref[:, j, :] = jnp.dot(u_ref[j], mb_ref[j], preferred_element_type=f32)

    sspec = pl.BlockSpec((nc, GBK, SB), lambda i: (0, i, 0))
    return pl.pallas_call(
        body, name=name, grid=(S5G // GBK,), in_specs=[_gspec(nc, TCP), _gspec(TCP, SB), _gspec(TCP, SB)],
        out_specs=[sspec, sspec], out_shape=[jax.ShapeDtypeStruct((nc, S5G, SB), f32)] * 2,
        compiler_params=_CP(dimension_semantics=("parallel",)),
    )(u, mb_f, mb_b)


def _idx_fwd(nctx, nch):
    return lambda i: i


def _idx_rev(nctx, nch):
    return lambda i: jnp.where(i < nctx, nctx - 1 - i, nch + nctx - 1 - i)


def _carry_loop(nc, step, init):
    def trip(i, c):
        for k in range(CARRY_UNROLL):
            c = step(i * CARRY_UNROLL + k, c)
        return c

    return lax.fori_loop(0, nc // CARRY_UNROLL, trip, init)


def _s5_carry(s_f, s_b, a_f, a_b, nctx, name):
    nc = s_f.shape[0]
    idx_b = _idx_rev(nctx, nc)

    def body(sf_ref, sb_ref, f1_ref, f2_ref, b1_ref, b2_ref, hf_ref, hb_ref):
        f1, f2, b1, b2 = f1_ref[...], f2_ref[...], b1_ref[...], b2_ref[...]

        def step(i, c):
            hf, hfs, hb, hbs = c
            rb = idx_b(i)
            hf_ref[i] = hf
            hb_ref[rb] = hb
            sf, sb = sf_ref[i], sb_ref[rb]
            return (f1 * hf + f2 * hfs + sf, f1 * hfs - f2 * hf + pltpu.roll(sf, S5N, axis=1),
                    b1 * hb + b2 * hbs + sb, b1 * hbs - b2 * hb + pltpu.roll(sb, S5N, axis=1))

        z = jnp.zeros((S5G, SB), f32)
        _carry_loop(nc, step, (z, z, z, z))

    return pl.pallas_call(body, name=name, out_shape=[jax.ShapeDtypeStruct(s_f.shape, f32)] * 2,
                          compiler_params=_CP())(s_f, s_b, *a_f, *a_b)


def _s5_carry_bwd(dhp, hp, a1, a2, rev, nctx, name):
    nc = hp.shape[0]
    idx = (_idx_rev if rev else _idx_fwd)(nctx, nc)

    def body(dhp_ref, hp_ref, a1_ref, a2_ref, ds_ref, d1_ref, d2_ref):
        f1, f2 = a1_ref[...], a2_ref[...]

        def step(k, carry):
            ab, abs_, d1, d2 = carry
            r = idx(nc - 1 - k)
            ds_ref[r] = ab
            h, dh = hp_ref[r], dhp_ref[r]
            return (dh + f1 * ab - f2 * abs_, pltpu.roll(dh, S5N, axis=1) + f1 * abs_ + f2 * ab,
                    d1 + ab * h, d2 + ab * pltpu.roll(h, S5N, axis=1))

        z = jnp.zeros((S5G, SB), f32)
        _, _, d1, d2 = _carry_loop(nc, step, (z, z, z, z))
        d1_ref[...], d2_ref[...] = d1, d2

    return pl.pallas_call(
        body, name=name,
        out_shape=[jax.ShapeDtypeStruct(hp.shape, f32), jax.ShapeDtypeStruct((S5G, SB), f32), jax.ShapeDtypeStruct((S5G, SB), f32)],
        compiler_params=_CP())(dhp, hp, a1, a2)


def _s5_out(u, m_f, m_b, hp_f, hp_b, mc_f, mc_b, name):
    nc = u.shape[1]

    def body(u_ref, mf_ref, mb_ref, hf_ref, hb_ref, cf_ref, cb_ref, y_ref):
        for j in range(GBK):
            uj = u_ref[j]
            y_ref[j] = (jnp.dot(uj, mf_ref[j], preferred_element_type=f32) + jnp.dot(uj, mb_ref[j], preferred_element_type=f32)
                        + dnt(hf_ref[:, j, :], cf_ref[j]) + dnt(hb_ref[:, j, :], cb_ref[j])).astype(bf16)

    sspec = pl.BlockSpec((nc, GBK, SB), lambda i: (0, i, 0))
    return pl.pallas_call(
        body, name=name, grid=(S5G // GBK,),
        in_specs=[_gspec(nc, TCP), _gspec(TCP, TCP), _gspec(TCP, TCP), sspec, sspec, _gspec(TCP, SB), _gspec(TCP, SB)],
        out_specs=_gspec(nc, TCP), out_shape=jax.ShapeDtypeStruct((S5G, nc, TCP), bf16),
        compiler_params=_CP(dimension_semantics=("parallel",)),
    )(u, m_f, m_b, hp_f, hp_b, mc_f, mc_b)


def _s5_out_bwd(dy, u, m_f, m_b, hp_f, hp_b, mc_f, mc_b, name):
    nc = u.shape[1]

    def body(dy_ref, u_ref, mf_ref, mb_ref, hf_ref, hb_ref, cf_ref, cb_ref, du_ref, g_ref, dhf_ref, dhb_ref, dcf_ref, dcb_ref):
        for j in range(GBK):
            dyj = dy_ref[j]
            du_ref[j] = dnt(dyj, mf_ref[j]) + dnt(dyj, mb_ref[j])
            g_ref[j] = dtn(u_ref[j], dyj)
            dhf_ref[:, j, :] = dnn(dyj, cf_ref[j])
            dhb_ref[:, j, :] = dnn(dyj, cb_ref[j])
            dcf_ref[j] = dtn(dyj, hf_ref[:, j, :])
            dcb_ref[j] = dtn(dyj, hb_ref[:, j, :])

    sspec = pl.BlockSpec((nc, GBK, SB), lambda i: (0, i, 0))
    sshape = jax.ShapeDtypeStruct((nc, S5G, SB), f32)
    cshape = jax.ShapeDtypeStruct((S5G, TCP, SB), f32)
    return pl.pallas_call(
        body, name=name, grid=(S5G // GBK,),
        in_specs=[_gspec(nc, TCP), _gspec(nc, TCP), _gspec(TCP, TCP), _gspec(TCP, TCP), sspec, sspec, _gspec(TCP, SB), _gspec(TCP, SB)],
        out_specs=[_gspec(nc, TCP), _gspec(TCP, TCP), sspec, sspec, _gspec(TCP, SB), _gspec(TCP, SB)],
        out_shape=[jax.ShapeDtypeStruct((S5G, nc, TCP), f32), jax.ShapeDtypeStruct((S5G, TCP, TCP), f32), sshape, sshape, cshape, cshape],
        compiler_params=_CP(dimension_semantics=("parallel",)),
    )(dy, u, m_f, m_b, hp_f, hp_b, mc_f, mc_b)


def _s5_inc_bwd(du1, u, ds_f, ds_b, mb_f, mb_b, name):
    nc = u.shape[1]

    def body(du1_ref, u_ref, dsf_ref, dsb_ref, mf_ref, mb_ref, du_ref, dmf_ref, dmb_ref):
        for j in range(GBK):
            dsf, dsb = dsf_ref[:, j, :], dsb_ref[:, j, :]
            du_ref[j] = (du1_ref[j] + dnt(dsf, mf_ref[j]) + dnt(dsb, mb_ref[j])).astype(bf16)
            dmf_ref[j] = dtn(u_ref[j], dsf)
            dmb_ref[j] = dtn(u_ref[j], dsb)

    sspec = pl.BlockSpec((nc, GBK, SB), lambda i: (0, i, 0))
    cshape = jax.ShapeDtypeStruct((S5G, TCP, SB), f32)
    return pl.pallas_call(
        body, name=name, grid=(S5G // GBK,),
        in_specs=[_gspec(nc, TCP), _gspec(nc, TCP), sspec, sspec, _gspec(TCP, SB), _gspec(TCP, SB)],
        out_specs=[_gspec(nc, TCP), _gspec(TCP, SB), _gspec(TCP, SB)],
        out_shape=[jax.ShapeDtypeStruct((S5G, nc, TCP), bf16), cshape, cshape],
        compiler_params=_CP(dimension_semantics=("parallel",)),
    )(du1, u, ds_f, ds_b, mb_f, mb_b)


def _to_groups(a):
    n = a.shape[0]
    return a.reshape(n // TC, TC, S5G, S5P).transpose(2, 0, 1, 3).reshape(S5G, n // TC, TCP)


def _from_groups(a):
    nc = a.shape[1]
    return a.reshape(S5G, nc, TC, S5P).transpose(1, 2, 0, 3).reshape(nc * TC, S5W)


def _swap_pairs(t):
    lane = lax.broadcasted_iota(jnp.int32, t.shape, 1)
    return jnp.where(lane % 2 == 0, pltpu.roll(t, DH - 1, axis=1), pltpu.roll(t, 1, axis=1))


def _rot(t, cosf, sins):
    return t * cosf + _swap_pairs(t) * sins


def _rot_t(d, cosf, sins):
    return d * cosf - _swap_pairs(d) * sins


def _ret_tables(ld, rev):
    pos = lax.broadcasted_iota(jnp.int32, (T, 1), 0).astype(f32)
    diff = pos - lax.broadcasted_iota(jnp.int32, (1, T), 1).astype(f32)
    if rev:
        keep, dist = diff < 0, jnp.maximum(-diff, 0.0)
        xi, zeta = jnp.exp(ld * (T - pos)), jnp.exp(ld * pos)
    else:
        keep, dist = diff >= 0, jnp.maximum(diff, 0.0)
        xi, zeta = jnp.exp(ld * (pos + 1.0)), jnp.exp(ld * (T - 1.0 - pos))
    return jnp.where(keep, jnp.exp(ld * dist), 0.0), xi, zeta, jnp.exp(ld * float(T))


def _ret_apply(qr, kr, v, rp, dm, xi, zeta, cdec):
    out = dnn(dnt(qr, kr) * dm, v) + dnn(qr * xi, rp)
    return out, cdec * rp + dtn(kr * zeta, v)


def _ret_fwd(p_ext, ld8_f, ld8_b, nctx, name, cargo=None):
    n = p_ext.shape[0]
    nch = n // T
    idx_b = _idx_rev(nctx, nch)
    cg = _Cargo(cargo)

    def body(*refs):
        ins, (of_ref, ob_ref, rpf_ref, rpb_ref), (rf_s, rb_s, dm_s, xz_s) = cg.split(refs, 8, 4, 4)
        qf, kf, vf, qb, kb, vb, ldf_ref, ldb_ref = ins
        cg.ride(refs, 8, 4, nch)

        @pl.when(pl.program_id(0) == 0)
        def _():
            rf_s[...] = jnp.zeros_like(rf_s)
            rb_s[...] = jnp.zeros_like(rb_s)
            for d, ld_ref in enumerate((ldf_ref, ldb_ref)):
                for h in range(RH):
                    dm, xi, zeta, cdec = _ret_tables(ld_ref[h:h + 1, 0:1], bool(d))
                    dm_s[d, h] = dm
                    xz_s[d, h, 0] = jnp.broadcast_to(xi, (T, DH))
                    xz_s[d, h, 1] = jnp.broadcast_to(zeta, (T, DH))
                    xz_s[d, h, 2] = jnp.broadcast_to(cdec, (T, DH))

        for h in range(RH):
            sl = slice(h * DH, (h + 1) * DH)
            for d, (q_ref, k_ref, v_ref, o_ref, rp_ref, r_s) in enumerate(((qf, kf, vf, of_ref, rpf_ref, rf_s),
                                                                            (qb, kb, vb, ob_ref, rpb_ref, rb_s))):
                rp = r_s[h]
                rp_ref[0, h] = rp
                out, rn = _ret_apply(q_ref[:, sl].astype(f32), k_ref[:, sl].astype(f32), v_ref[:, sl].astype(f32), rp,
                                     dm_s[d, h], xz_s[d, h, 0], xz_s[d, h, 1], xz_s[d, h, 2])
                r_s[h] = rn
                o_ref[:, sl] = out

    fcol = lambda cb: pl.BlockSpec((T, RW), lambda i, _c=cb: (i, _c))
    bcol = lambda cb: pl.BlockSpec((T, RW), lambda i, _c=cb: (idx_b(i), _c))
    rspec = pl.BlockSpec((1, RH, DH, DH), lambda i: (i, 0, 0, 0))
    oshape, rshape = jax.ShapeDtypeStruct((n, RW), f32), jax.ShapeDtypeStruct((nch, RH, DH, DH), f32)
    return pl.pallas_call(
        body, name=name, grid=(nch,),
        in_specs=[fcol(1), fcol(2), fcol(3), bcol(1), bcol(2), bcol(3), _const_spec((8, 128)), _const_spec((8, 128))] + cg.in_specs(),
        out_specs=[fcol(0), bcol(0), rspec, rspec] + cg.in_specs(),
        out_shape=[oshape, oshape, rshape, rshape] + cg.out_shapes(),
        scratch_shapes=[pltpu.VMEM((RH, DH, DH), f32)] * 2 + [pltpu.VMEM((2, RH, T, T), f32), pltpu.VMEM((2, RH, 3, T, DH), f32)] + cg.sems(),
        compiler_params=_CP(dimension_semantics=_ARB),
    )(p_ext, p_ext, p_ext, p_ext, p_ext, p_ext, ld8_f, ld8_b, *cg.arrays)


def _ret_bwd(p_ext, ld8_f, ld8_b, rp_f, rp_b, do_ext, nctx, name, cargo=None):
    n = p_ext.shape[0]
    nch = n // T
    idx_rev = _idx_rev(nctx, nch)
    idf = lambda j: nch - 1 - j
    idb = lambda j: idx_rev(nch - 1 - j)
    cg = _Cargo(cargo)

    def body(*refs):
        ins, outs, (drf_s, drb_s, dm_s, xz_s, gdm_s, gxz_s) = cg.split(refs, 12, 8, 6)
        qf, kf, vf, qb, kb, vb, ldf_ref, ldb_ref, rpf_ref, rpb_ref, dof_ref, dob_ref = ins
        dqf, dkf, dvf, dqb, dkb, dvb, dldf_ref, dldb_ref = outs
        cg.ride(refs, 12, 8, nch)
        lds = (ldf_ref, ldb_ref)

        @pl.when(pl.program_id(0) == 0)
        def _():
            for r in (drf_s, drb_s, gdm_s, gxz_s):
                r[...] = jnp.zeros_like(r)
            for d in range(2):
                for h in range(RH):
                    dm, xi, zeta, cdec = _ret_tables(lds[d][h:h + 1, 0:1], bool(d))
                    dm_s[d, h] = dm
                    for k, tab in enumerate((xi, zeta, cdec)):
                        xz_s[d, h, k] = jnp.broadcast_to(tab, (T, DH))

        for h in range(RH):
            sl = slice(h * DH, (h + 1) * DH)
            for d, (q_ref, k_ref, v_ref, rp_ref, do_ref, dq_ref, dk_ref, dv_ref, dr_s) in enumerate((
                    (qf, kf, vf, rpf_ref, dof_ref, dqf, dkf, dvf, drf_s), (qb, kb, vb, rpb_ref, dob_ref, dqb, dkb, dvb, drb_s))):
                _, vjp = jax.vjp(_ret_apply, q_ref[:, sl].astype(f32), k_ref[:, sl].astype(f32), v_ref[:, sl].astype(f32),
                                 rp_ref[0, h], dm_s[d, h], xz_s[d, h, 0], xz_s[d, h, 1], xz_s[d, h, 2])
                dqr, dkr, dv, drp, gdm, gxi, gzeta, gcdec = vjp((do_ref[:, sl], dr_s[h]))
                dr_s[h] = drp
                dq_ref[:, sl], dk_ref[:, sl], dv_ref[:, sl] = dqr, dkr, dv
                gdm_s[d, h] += gdm
                for k, g in enumerate((gxi, gzeta, gcdec)):
                    gxz_s[d, h, k] += g

        @pl.when(pl.program_id(0) == nch - 1)
        def _():
            for d, dld_ref in enumerate((dldf_ref, dldb_ref)):
                dld_ref[...] = jnp.zeros_like(dld_ref)
                for h in range(RH):
                    _, vjp = jax.vjp(functools.partial(_ret_tables, rev=bool(d)), lds[d][h:h + 1, 0:1])
                    lanes = lambda a: jnp.sum(a, axis=1, keepdims=True)
                    (dld,) = vjp((gdm_s[d, h], lanes(gxz_s[d, h, 0]), lanes(gxz_s[d, h, 1]),
                                  jnp.sum(lanes(gxz_s[d, h, 2]), axis=0, keepdims=True)))
                    dld_ref[h:h + 1, :] = jnp.broadcast_to(dld, (1, 128))

    fcol = lambda cb: pl.BlockSpec((T, RW), lambda j, _c=cb: (idf(j), _c))
    bcol = lambda cb: pl.BlockSpec((T, RW), lambda j, _c=cb: (idb(j), _c))
    rspec = pl.BlockSpec((1, RH, DH, DH), lambda j: (nch - 1 - j, 0, 0, 0))
    oshape = jax.ShapeDtypeStruct((n, RW), f32)
    return pl.pallas_call(
        body, name=name, grid=(nch,),
        in_specs=[fcol(1), fcol(2), fcol(3), bcol(1), bcol(2), bcol(3), _const_spec((8, 128)), _const_spec((8, 128)), rspec, rspec,
                  fcol(0), bcol(0)] + cg.in_specs(),
        out_specs=[fcol(0), fcol(0), fcol(0), bcol(0), bcol(0), bcol(0), _acc_spec((8, 128)), _acc_spec((8, 128))] + cg.in_specs(),
        out_shape=[oshape] * 6 + [jax.ShapeDtypeStruct((8, 128), f32)] * 2 + cg.out_shapes(),
        scratch_shapes=[pltpu.VMEM((RH, DH, DH), f32)] * 2 + [pltpu.VMEM((2, RH, T, T), f32), pltpu.VMEM((2, RH, 3, T, DH), f32)] * 2
        + cg.sems(),
        compiler_params=_CP(dimension_semantics=_ARB),
    )(p_ext, p_ext, p_ext, p_ext, p_ext, p_ext, ld8_f, ld8_b, rp_f, rp_b, do_ext, do_ext, *cg.arrays)


def _qk_heads(p, fn_q, fn_k):
    heads = lambda base, fn: [fn(p[:, base + h * DH:base + (h + 1) * DH]) for h in range(RH)]
    return jnp.concatenate([p[:, :S5W]] + heads(S5W, fn_q) + heads(S5W + RW, fn_k) + [p[:, S5W + 2 * RW:]], axis=1)


def _f1_fwd(x, ctx, modx, modc, nw1, w_in_n, cosf, sins, name, cargo=None):
    L = x.shape[0]
    nb = L // R + 1
    scale = DH ** -0.5
    cg = _Cargo(cargo)

    def body(*refs):
        (x_ref, c_ref, mx_ref, mc_ref, nw_ref, w_ref, cos_ref, sin_ref), (p_ref,), _ = cg.split(refs, 8, 1, 0)
        cg.ride(refs, 8, 1, nb)
        is_ctx = pl.program_id(0) == 0
        xin = jnp.where(is_ctx, c_ref[...], x_ref[...])
        sh = jnp.where(is_ctx, mc_ref[0:1], mx_ref[0:1])
        sc = jnp.where(is_ctx, mc_ref[1:2], mx_ref[1:2])
        cf, ss = cos_ref[...], sin_ref[...]
        p = dnn(_norm_mod(xin, nw_ref[...], sh, sc), w_ref[...])
        p_ref[...] = _qk_heads(p, lambda t: _rot(t, cf, ss), lambda t: _rot(t * scale, cf, ss)).astype(bf16)

    return pl.pallas_call(
        body, name=name, grid=(nb,),
        in_specs=[pl.BlockSpec((R, D), lambda i: (jnp.maximum(i - 1, 0), 0)), _const_spec((R, D)), _const_spec((6, D)),
                  _const_spec((6, D)), _const_spec((1, D)), _const_spec((D, INC)), pl.BlockSpec((R, DH), lambda i: (i, 0)),
                  pl.BlockSpec((R, DH), lambda i: (i, 0))] + cg.in_specs(),
        out_specs=[pl.BlockSpec((R, INC), lambda i: (i, 0))] + cg.in_specs(),
        out_shape=[jax.ShapeDtypeStruct((L + R, INC), bf16)] + cg.out_shapes(),
        scratch_shapes=cg.sems(),
        compiler_params=_CP(dimension_semantics=_ARB),
    )(x, ctx, modx, modc, nw1, w_in_n, cosf, sins, *cg.arrays)


def _f1_bwd(x, ctx, modx, modc, nw1, w_in_t, cosf, sins, dx1, parts, name, cargo=None):
    L = x.shape[0]
    nb = L // R + 1
    scale = DH ** -0.5
    cg = _Cargo(cargo)

    def body(*refs):
        ins, (gx_ref, dp_ref, h1_ref, dnw_ref, dmx_ref, dmc_ref), _ = cg.split(refs, 18, 6, 0)
        x_ref, c_ref, mx_ref, mc_ref, nw_ref, w_ref, cos_ref, sin_ref, dx1_ref, du0, du1, dq0, dq1, dk0, dk1, dv0, dv1, dg0 = ins
        cg.ride(refs, 18, 6, nb)
        i = pl.program_id(0)
        is_ctx = i == 0

        @pl.when(is_ctx)
        def _():
            dnw_ref[...] = jnp.zeros_like(dnw_ref)
            dmx_ref[...] = jnp.zeros_like(dmx_ref)
            dmc_ref[...] = jnp.zeros_like(dmc_ref)

        cf, ss = cos_ref[...], sin_ref[...]
        dp = jnp.concatenate([du0[...].astype(f32) + du1[...], dq0[...] + dq1[...], dk0[...] + dk1[...], dv0[...] + dv1[...],
                              dg0[...]], axis=1)
        dp = _qk_heads(dp, lambda t: _rot_t(t, cf, ss), lambda t: _rot_t(t, cf, ss) * scale).astype(bf16)
        dp_ref[...] = dp
        xin = jnp.where(is_ctx, c_ref[...], x_ref[...])
        sh = jnp.where(is_ctx, mc_ref[0:1], mx_ref[0:1])
        sc = jnp.where(is_ctx, mc_ref[1:2], mx_ref[1:2])
        dh = dnn(dp, w_ref[...])
        h, vjp = jax.vjp(_norm_mod, xin, nw_ref[...], sh, sc)
        dxin, dnw, dsh, dsc = vjp(dh)
        h1_ref[...] = h.astype(bf16)
        gx_ref[...] = dx1_ref[...] + dxin
        dnw_ref[...] += dnw
        wx = jnp.where(is_ctx, 0.0, 1.0)
        dmx_ref[0:1] += dsh * wx
        dmx_ref[1:2] += dsc * wx
        dmc_ref[0:1] += dsh * (1.0 - wx)
        dmc_ref[1:2] += dsc * (1.0 - wx)

    lat = pl.BlockSpec((R, D), lambda i: (jnp.maximum(i - 1, 0), 0))
    ext = pl.BlockSpec((R, S5W), lambda i: (i, 0))
    return pl.pallas_call(
        body, name=name, grid=(nb,),
        in_specs=[lat, _const_spec((R, D)), _const_spec((6, D)), _const_spec((6, D)), _const_spec((1, D)), _const_spec((INC, D)),
                  pl.BlockSpec((R, DH), lambda i: (i, 0)), pl.BlockSpec((R, DH), lambda i: (i, 0)), lat] + [ext] * 9 + cg.in_specs(),
        out_specs=[lat, pl.BlockSpec((R, INC), lambda i: (i, 0)), pl.BlockSpec((R, D), lambda i: (i, 0)),
                   _acc_spec((1, D)), _acc_spec((6, D)), _acc_spec((6, D))] + cg.in_specs(),
        out_shape=[jax.ShapeDtypeStruct((L, D), f32), jax.ShapeDtypeStruct((L + R, INC), bf16),
                   jax.ShapeDtypeStruct((L + R, D), bf16), jax.ShapeDtypeStruct((1, D), f32),
                   jax.ShapeDtypeStruct((6, D), f32), jax.ShapeDtypeStruct((6, D), f32)] + cg.out_shapes(),
        scratch_shapes=cg.sems(),
        compiler_params=_CP(dimension_semantics=_ARB),
    )(x, ctx, modx, modc, nw1, w_in_t, cosf, sins, dx1, *parts, *cg.arrays)


def _ret_post(yr, g):
    outs = []
    for h in range(RH):
        yh = yr[:, h * DH:(h + 1) * DH]
        mu = jnp.mean(yh, axis=-1, keepdims=True)
        var = jnp.mean((yh - mu) ** 2, axis=-1, keepdims=True)
        outs.append((yh - mu) * lax.rsqrt(var + EPS))
    return jax.nn.silu(g) * jnp.concatenate(outs, axis=1)


def _mix_fn(ys, u, of, ob, g, x, dvec, bglu, gate1, pz, pm, wglu, wout):
    s = _gelu(ys + dvec * u)
    z = dnn(s, wglu) + bglu + pz
    cat = jnp.concatenate([s * jax.nn.sigmoid(z), _ret_post(of + ob, g)], axis=1)
    mix = dnn(cat, wout) + pm
    return x + gate1 * mix, (s, cat)


def _mix_fwd(x, ys, of, ob, p_ext, dvec, bglu, modx, wglu, wout, name, cargo=None):
    L = x.shape[0]
    nb = L // R
    cg = _Cargo(cargo)

    def body(*refs):
        ins, (x1_ref,), _ = cg.split(refs, 11, 1, 0)
        x_ref, ys_ref, of_ref, ob_ref, u_ref, g_ref, d_ref, b_ref, mx_ref, wg_ref, wo_ref = ins
        cg.ride(refs, 11, 1, nb)
        x1_ref[...] = _mix_fn(ys_ref[...].astype(f32), u_ref[...].astype(f32), of_ref[...], ob_ref[...], g_ref[...].astype(f32),
                              x_ref[...], d_ref[...], b_ref[...], mx_ref[2:3], 0.0, 0.0, wg_ref[...], wo_ref[...])[0]

    ext = pl.BlockSpec((R, S5W), lambda i: (i + 1, 0))
    return pl.pallas_call(
        body, name=name, grid=(nb,),
        in_specs=[pl.BlockSpec((R, D), lambda i: (i, 0)), ext, ext, ext, ext, pl.BlockSpec((R, RW), lambda i: (i + 1, 4)),
                  _const_spec((1, S5W)), _const_spec((1, S5W)), _const_spec((6, D)), _const_spec((S5W, S5W)), _const_spec((D, D))]
        + cg.in_specs(),
        out_specs=[pl.BlockSpec((R, D), lambda i: (i, 0))] + cg.in_specs(),
        out_shape=[jax.ShapeDtypeStruct((L, D), f32)] + cg.out_shapes(),
        scratch_shapes=cg.sems(),
        compiler_params=_CP(dimension_semantics=_ARB),
    )(x, ys, of, ob, p_ext, p_ext, dvec, bglu, modx, wglu, wout, *cg.arrays)


def _mix_bwd(x, ys, of, ob, p_ext, dvec, bglu, modx, wglu, wout, dx1, name, cargo=None):
    L = x.shape[0]
    nb = L // R + 1
    cg = _Cargo(cargo)

    def body(*refs):
        ins, outs, _ = cg.split(refs, 12, 11, 0)
        x_ref, ys_ref, of_ref, ob_ref, u_ref, g_ref, d_ref, b_ref, mx_ref, wg_ref, wo_ref, dx1_ref = ins
        dy_ref, dud_ref, do_ref, dg_ref, cat_ref, dmix_ref, s_ref, dz_ref, dd_ref, db_ref, dg1_ref = outs
        cg.ride(refs, 12, 11, nb)
        i = pl.program_id(0)

        @pl.when(i == 0)
        def _():
            for r in outs:
                r[...] = jnp.zeros_like(r)

        @pl.when(i > 0)
        def _():
            fn = lambda ys_, u_, of_, g_, d_, b_, g1_, pz_, pm_: _mix_fn(
                ys_, u_, of_, ob_ref[...], g_, x_ref[...], d_, b_, g1_, pz_, pm_, wg_ref[...], wo_ref[...])
            _, vjp, (s, cat) = jax.vjp(fn, ys_ref[...].astype(f32), u_ref[...].astype(f32), of_ref[...], g_ref[...].astype(f32), d_ref[...],
                                       b_ref[...], mx_ref[2:3], jnp.zeros((R, S5W), f32), jnp.zeros((R, D), f32), has_aux=True)
            dy, dud, do, dg, dd, db, dg1, dz, dmix = vjp(dx1_ref[...])
            dy_ref[...], dud_ref[...], do_ref[...], dg_ref[...] = dy.astype(bf16), dud, do, dg
            cat_ref[...], dmix_ref[...] = cat.astype(bf16), dmix.astype(bf16)
            s_ref[...], dz_ref[...] = s.astype(bf16), dz.astype(bf16)
            dd_ref[...] += dd
            db_ref[...] += db
            dg1_ref[...] += dg1

    lat = pl.BlockSpec((R, D), lambda i: (jnp.maximum(i - 1, 0), 0))
    lat5 = pl.BlockSpec((R, S5W), lambda i: (jnp.maximum(i - 1, 0), 0))
    ext = pl.BlockSpec((R, S5W), lambda i: (i, 0))
    eshape = jax.ShapeDtypeStruct((L + R, S5W), f32)
    return pl.pallas_call(
        body, name=name, grid=(nb,),
        in_specs=[lat, ext, ext, ext, ext, pl.BlockSpec((R, RW), lambda i: (i, 4)),
                  _const_spec((1, S5W)), _const_spec((1, S5W)), _const_spec((6, D)), _const_spec((S5W, S5W)), _const_spec((D, D)), lat]
        + cg.in_specs(),
        out_specs=[ext, ext, ext, ext, lat, lat, lat5, lat5, _acc_spec((1, S5W)), _acc_spec((1, S5W)), _acc_spec((1, D))]
        + cg.in_specs(),
        out_shape=[jax.ShapeDtypeStruct((L + R, S5W), bf16), eshape, eshape, eshape, jax.ShapeDtypeStruct((L, D), bf16),
                   jax.ShapeDtypeStruct((L, D), bf16), jax.ShapeDtypeStruct((L, S5W), bf16), jax.ShapeDtypeStruct((L, S5W), bf16),
                   jax.ShapeDtypeStruct((1, S5W), f32), jax.ShapeDtypeStruct((1, S5W), f32), jax.ShapeDtypeStruct((1, D), f32)]
        + cg.out_shapes(),
        scratch_shapes=cg.sems(),
        compiler_params=_CP(dimension_semantics=_ARB),
    )(x, ys, of, ob, p_ext, p_ext, dvec, bglu, modx, wglu, wout, dx1, *cg.arrays)


def _ffn_tail(gc, a, x1, gate2, fnw, pf, wdown, wdown_t, tgt):
    f = _gelu(gc) * a
    ffn = _dnn_const(f, wdown, wdown_t) + pf
    y = _rms(x1 + gate2 * ffn, fnw)
    err = y - tgt
    loss = 0.5 * jnp.sum(jnp.mean(err * err, axis=-1, keepdims=True), axis=0, keepdims=True)
    return loss, f


def _ffn_fwd(x1, tgt, nw2, modx, w_a, w_g, cw, cb, wdown, wdown_t, fnw, name):
    L = x1.shape[0]
    nb = L // RF
    per = RF // HALO

    def body(x_ref, xp_ref, xn_ref, t_ref, nw_ref, mx_ref, wa_ref, wg_ref, cw_ref, cb_ref, wd_ref, wdt_ref, fn_ref,
             dx2_ref, da_ref, dgc_ref, f_ref, dffn_ref, loss_ref, dfn_ref, dg2_ref, dcb_ref, dcw_ref):
        i = pl.program_id(0)

        @pl.when(i == 0)
        def _():
            for r in (loss_ref, dfn_ref, dg2_ref, dcb_ref, dcw_ref):
                r[...] = jnp.zeros_like(r)

        nw, sh, sc, gate2 = nw_ref[...], mx_ref[3:4], mx_ref[4:5], mx_ref[5:6]
        x1b = x_ref[...]
        h2 = _norm_mod(x1b, nw, sh, sc)
        h2e = jnp.concatenate([_norm_mod(xp_ref[...], nw, sh, sc), h2, _norm_mod(xn_ref[...], nw, sh, sc)], axis=0)
        a = dnn(h2, wa_ref[...])
        ge = dnn(h2e, wg_ref[...])
        g = ge[HALO:HALO + RF]
        gp = ge[HALO - 1:HALO] * jnp.where(i > 0, 1.0, 0.0)
        gn = ge[HALO + RF:HALO + RF + 1] * jnp.where(i < nb - 1, 1.0, 0.0)
        row = lax.broadcasted_iota(jnp.int32, (RF, 1), 0)
        g_prev = jnp.where(row == 0, gp, pltpu.roll(g, 1, axis=0))
        g_next = jnp.where(row == RF - 1, gn, pltpu.roll(g, RF - 1, axis=0))
        gc = cb_ref[...] + g_prev * cw_ref[0:1] + g * cw_ref[1:2] + g_next * cw_ref[2:3]
        fn = lambda gc_, a_, x_, g2_, fw_, pf_: _ffn_tail(gc_, a_, x_, g2_, fw_, pf_, wd_ref[...], wdt_ref[...], t_ref[...])
        loss, vjp, f = jax.vjp(fn, gc, a, x1b, gate2, fn_ref[...], jnp.zeros((RF, D), f32), has_aux=True)
        dgc, da, dx2, dg2, dfw, dffn = vjp(jnp.ones((1, 1), f32))
        dx2_ref[...] = dx2
        da_ref[...], dgc_ref[...] = da.astype(bf16), dgc
        f_ref[...], dffn_ref[...] = f.astype(bf16), dffn.astype(bf16)
        loss_ref[...] += jnp.broadcast_to(loss, (1, 128))
        dfn_ref[...] += dfw
        dg2_ref[...] += dg2
        dcb_ref[...] += jnp.sum(dgc, axis=0, keepdims=True)
        dcw_ref[0:1] += jnp.sum(dgc * g_prev, axis=0, keepdims=True)
        dcw_ref[1:2] += jnp.sum(dgc * g, axis=0, keepdims=True)
        dcw_ref[2:3] += jnp.sum(dgc * g_next, axis=0, keepdims=True)

    blk = lambda w: pl.BlockSpec((RF, w), lambda i: (i, 0))
    return pl.pallas_call(
        body, name=name, grid=(nb,),
        in_specs=[blk(D), pl.BlockSpec((HALO, D), lambda i: (jnp.maximum(i * per - 1, 0), 0)),
                  pl.BlockSpec((HALO, D), lambda i: (jnp.minimum((i + 1) * per, L // HALO - 1), 0)), blk(D),
                  _const_spec((1, D)), _const_spec((6, D)), _const_spec((D, DFF)), _const_spec((D, DFF)), _const_spec((3, DFF)),
                  _const_spec((1, DFF)), _const_spec((DFF, D)), _const_spec((D, DFF)), _const_spec((1, D))],
        out_specs=[blk(D), blk(DFF), blk(DFF), blk(DFF), blk(D), _acc_spec((1, 128)), _acc_spec((1, D)), _acc_spec((1, D)),
                   _acc_spec((1, DFF)), _acc_spec((3, DFF))],
        out_shape=[jax.ShapeDtypeStruct((L, D), f32), jax.ShapeDtypeStruct((L, DFF), bf16), jax.ShapeDtypeStruct((L, DFF), f32),
                   jax.ShapeDtypeStruct((L, DFF), bf16), jax.ShapeDtypeStruct((L, D), bf16), jax.ShapeDtypeStruct((1, 128), f32),
                   jax.ShapeDtypeStruct((1, D), f32), jax.ShapeDtypeStruct((1, D), f32), jax.ShapeDtypeStruct((1, DFF), f32),
                   jax.ShapeDtypeStruct((3, DFF), f32)],
        compiler_params=_CP(dimension_semantics=_ARB),
    )(x1, x1, x1, tgt, nw2, modx, w_a, w_g, cw, cb, wdown, wdown_t, fnw)


def _ffn_bwd(x1, dx2, da, dgc, nw2, modx, wup_t, cw, name):
    L = x1.shape[0]
    rf = RFB
    nb = L // rf
    per = rf // HALO

    def body(x_ref, dx2_ref, da_ref, dgc_ref, dgp_ref, dgn_ref, nw_ref, mx_ref, wu_ref, cw_ref,
             dx1_ref, dag_ref, h2_ref, dnw_ref, dmx_ref):
        i = pl.program_id(0)

        @pl.when(i == 0)
        def _():
            dnw_ref[...] = jnp.zeros_like(dnw_ref)
            dmx_ref[...] = jnp.zeros_like(dmx_ref)

        dgc_b = dgc_ref[...]
        before = dgp_ref[HALO - 1:HALO] * jnp.where(i > 0, 1.0, 0.0)
        after = dgn_ref[0:1] * jnp.where(i < nb - 1, 1.0, 0.0)
        row = lax.broadcasted_iota(jnp.int32, (rf, 1), 0)
        d_prev = jnp.where(row == 0, before, pltpu.roll(dgc_b, 1, axis=0))
        d_next = jnp.where(row == rf - 1, after, pltpu.roll(dgc_b, rf - 1, axis=0))
        dg = cw_ref[0:1] * d_next + cw_ref[1:2] * dgc_b + cw_ref[2:3] * d_prev
        dag = jnp.concatenate([da_ref[...], dg.astype(bf16)], axis=1)
        dag_ref[...] = dag
        dh2 = dnn(dag, wu_ref[...])
        h2, vjp = jax.vjp(_norm_mod, x_ref[...], nw_ref[...], mx_ref[3:4], mx_ref[4:5])
        dxa, dnw, dsh, dsc = vjp(dh2)
        h2_ref[...] = h2.astype(bf16)
        dx1_ref[...] = dx2_ref[...] + dxa
        dnw_ref[...] += dnw
        dmx_ref[3:4] += dsh
        dmx_ref[4:5] += dsc

    blk = lambda w: pl.BlockSpec((rf, w), lambda i: (i, 0))
    return pl.pallas_call(
        body, name=name, grid=(nb,),
        in_specs=[blk(D), blk(D), blk(DFF), blk(DFF), pl.BlockSpec((HALO, DFF), lambda i: (jnp.maximum(i * per - 1, 0), 0)),
                  pl.BlockSpec((HALO, DFF), lambda i: (jnp.minimum((i + 1) * per, L // HALO - 1), 0)),
                  _const_spec((1, D)), _const_spec((6, D)), _const_spec((2 * DFF, D)), _const_spec((3, DFF))],
        out_specs=[blk(D), blk(2 * DFF), blk(D), _acc_spec((1, D)), _acc_spec((6, D))],
        out_shape=[jax.ShapeDtypeStruct((L, D), f32), jax.ShapeDtypeStruct((L, 2 * DFF), bf16), jax.ShapeDtypeStruct((L, D), bf16),
                   jax.ShapeDtypeStruct((1, D), f32), jax.ShapeDtypeStruct((6, D), f32)],
        compiler_params=_CP(dimension_semantics=_ARB),
    )(x1, dx2, da, dgc, dgc, dgc, nw2, modx, wup_t, cw)


def _matmul_tn(a, b, name, cargo=None):
    k, m = a.shape
    n = b.shape[1]
    divs = lambda d: [c for c in range(d, 0, -128) if d % c == 0]
    _, tm, tn = min((m * (n // cn) + n * (m // cm), cm, cn) for cm in divs(m) for cn in divs(n) if cm * cn * 4 <= ACC_TILE_BYTES)
    tk = next(c for c in (512, 768, 256, 128) if k % c == 0)
    nk = k // tk
    grid = (m // tm, n // tn, nk)
    cg = _Cargo(cargo)

    def body(*refs):
        (a_ref, b_ref), (o_ref,), (acc,) = cg.split(refs, 2, 1, 1)
        cg.ride(refs, 2, 1, grid)
        q = pl.program_id(2)

        @pl.when(q == 0)
        def _():
            acc[...] = jnp.zeros_like(acc)

        acc[...] += dtn(a_ref[...], b_ref[...])

        @pl.when(q == nk - 1)
        def _():
            o_ref[...] = acc[...].astype(bf16)

    out = pl.pallas_call(
        body, name=name, grid=grid,
        in_specs=[pl.BlockSpec((tk, tm), lambda i, j, q: (q, i)), pl.BlockSpec((tk, tn), lambda i, j, q: (q, j))] + cg.in_specs(),
        out_specs=[pl.BlockSpec((tm, tn), lambda i, j, q: (i, j))] + cg.in_specs(),
        out_shape=[jax.ShapeDtypeStruct((m, n), bf16)] + cg.out_shapes(),
        scratch_shapes=[pltpu.VMEM((tm, tn), f32)] + cg.sems(),
        compiler_params=_CP(dimension_semantics=("arbitrary",) * 3 if cg.n else ("parallel", "parallel", "arbitrary")),
    )(a, b, *cg.arrays)
    return out if cg.n else out[0]


def _adamw_refs(w_ref, g_ref, m_ref, v_ref, d_ref, nm_ref, nv_ref):
    c1, c2 = 1.0 - B1 ** STEP, 1.0 - B2 ** STEP
    gg = g_ref[...]
    nm = B1 * m_ref[...] + (1.0 - B1) * gg
    nv = B2 * v_ref[...] + (1.0 - B2) * jnp.square(gg)
    d_ref[...] = -LR * ((nm / c1) / (jnp.sqrt(nv / c2) + AEPS) + WD * w_ref[...])
    nm_ref[...], nv_ref[...] = nm, nv


def _adamw(w, g, m, v, name):
    def body(*refs):
        _adamw_refs(*refs)

    return pl.pallas_call(body, name=name, out_shape=[jax.ShapeDtypeStruct(w.shape, f32)] * 3, compiler_params=_CP())(w, g, m, v)


def _adamw_landed(land, w, m, v, name):
    def body(l_ref, w_ref, m_ref, v_ref, g_ref, d_ref, nm_ref, nv_ref):
        acc = l_ref[0].astype(f32)
        for j in range(1, NDEV):
            acc = acc + l_ref[j].astype(f32)
        g_ref[...] = acc
        _adamw_refs(w_ref, g_ref, m_ref, v_ref, d_ref, nm_ref, nv_ref)

    return pl.pallas_call(body, name=name, out_shape=[jax.ShapeDtypeStruct(w.shape, f32)] * 4, compiler_params=_CP())(land, w, m, v)


def _adamw_many(ws, gs, ms, vs, name):
    n = len(ws)

    def body(*refs):
        for k in range(n):
            _adamw_refs(*[refs[j * n + k] for j in range(7)])

    outs = pl.pallas_call(body, name=name, out_shape=[jax.ShapeDtypeStruct(w.shape, f32) for w in ws] * 3,
                          compiler_params=_CP())(*ws, *gs, *ms, *vs)
    return outs[:n], outs[n:2 * n], outs[2 * n:]


SMALL = ["conv_w", "c_ctx", "norm1_w", "s5_lambda_re_f", "s5_lambda_im_f", "s5_log_step_f", "s5_lambda_re_b", "s5_lambda_im_b",
         "s5_log_step_b", "s5_b_re", "s5_b_im", "s5_c_re", "s5_c_im", "s5_d", "s5_b_glu", "ret_log_decay_f", "ret_log_decay_b",
         "norm2_w", "conv_b", "final_norm_w"]
WEIGHTS = ["c_ctx", "w_mod", "b_mod", "norm1_w", "w_in", "s5_lambda_re_f", "s5_lambda_im_f", "s5_log_step_f", "s5_lambda_re_b",
           "s5_lambda_im_b", "s5_log_step_b", "s5_b_re", "s5_b_im", "s5_c_re", "s5_c_im", "s5_d", "s5_w_glu", "s5_b_glu",
           "ret_log_decay_f", "ret_log_decay_b", "w_out", "norm2_w", "w_up", "conv_w", "conv_b", "w_down", "final_norm_w"]


def _pack_small(vals):
    flat, offs, o = [], [], 0
    for a in vals:
        n = a.size
        npad = -n % 128
        flat.append(jnp.pad(a.reshape(-1), (0, npad)))
        offs.append((o, n))
        o += n + npad
    tail = -o % 1024
    if tail:
        flat.append(jnp.zeros((tail,), f32))
    return jnp.concatenate(flat).reshape(-1, 128), offs


def _unpack_small(packed, offs, shapes):
    flat = packed.reshape(-1)
    return [flat[o:o + n].reshape(s) for (o, n), s in zip(offs, shapes)]


def _rope_tables(L, nctx_rows):
    t = np.arange(L)
    inv = (ROPE_THETA ** (-np.arange(DH // 4, dtype=np.float64) / (DH // 4))).astype(np.float32)
    ang = np.concatenate([(t // GRID_W).astype(np.float32)[:, None] * inv, (t % GRID_W).astype(np.float32)[:, None] * inv], axis=-1)
    cos = np.repeat(np.cos(ang).astype(np.float32), 2, axis=1)
    sin = np.repeat(np.sin(ang).astype(np.float32), 2, axis=1) * np.tile(np.array([-1.0, 1.0], np.float32), DH // 2)
    cosf = np.concatenate([np.ones((nctx_rows, DH), np.float32), cos], axis=0)
    sins = np.concatenate([np.zeros((nctx_rows, DH), np.float32), sin], axis=0)
    return jnp.asarray(cosf), jnp.asarray(sins)


def kernel(x, c, ctx, c_ctx, w_mod, b_mod, norm1_w, w_in, s5_lambda_re_f, s5_lambda_im_f, s5_log_step_f, s5_lambda_re_b, s5_lambda_im_b, s5_log_step_b, s5_b_re, s5_b_im, s5_c_re, s5_c_im, s5_d, s5_w_glu, s5_b_glu, ret_log_decay_f, ret_log_decay_b, w_out, norm2_w, w_up, conv_w, conv_b, w_down, final_norm_w, loss_target, m_c_ctx, m_w_mod, m_b_mod, m_norm1_w, m_w_in, m_s5_lambda_re_f, m_s5_lambda_im_f, m_s5_log_step_f, m_s5_lambda_re_b, m_s5_lambda_im_b, m_s5_log_step_b, m_s5_b_re, m_s5_b_im, m_s5_c_re, m_s5_c_im, m_s5_d, m_s5_w_glu, m_s5_b_glu, m_ret_log_decay_f, m_ret_log_decay_b, m_w_out, m_norm2_w, m_w_up, m_conv_w, m_conv_b, m_w_down, m_final_norm_w, v_c_ctx, v_w_mod, v_b_mod, v_norm1_w, v_w_in, v_s5_lambda_re_f, v_s5_lambda_im_f, v_s5_log_step_f, v_s5_lambda_re_b, v_s5_lambda_im_b, v_s5_log_step_b, v_s5_b_re, v_s5_b_im, v_s5_c_re, v_s5_c_im, v_s5_d, v_s5_w_glu, v_s5_b_glu, v_ret_log_decay_f, v_ret_log_decay_b, v_w_out, v_norm2_w, v_w_up, v_conv_w, v_conv_b, v_w_down, v_final_norm_w):
    args = dict(locals())
    W = {n: args[n] for n in WEIGHTS}
    M = {n: args["m_" + n] for n in WEIGHTS}
    V = {n: args["v_" + n] for n in WEIGHTS}
    me = _me()
    x2, ctx2, tgt = x[0], ctx[0], loss_target[0]
    L, Lc = x2.shape[0], ctx2.shape[0]
    assert Lc == R and L % R == 0 and L % GRID_W == 0
    nctx = Lc // T

    w_in_tl, w_up_tl = w_in[0].T.astype(bf16), w_up[0].T.astype(bf16)
    w_out_l, w_down_l, w_glu_l = w_out[0].astype(bf16), w_down[0].astype(bf16), s5_w_glu[0].astype(bf16)
    per_cv = conv_w.shape[2]
    conv_pad = jnp.pad(conv_w[0], ((0, 5), (0, 128 * 3 - per_cv)))
    w_in_g, c_g, conv_g = _gather_two_level([w_in_tl, jnp.pad(c, ((0, 7), (0, 0))), conv_pad], "gather_w_in")
    w_in_t = w_in_g.reshape(INC, D)
    conv_f = conv_g[:, :3, :per_cv].transpose(1, 0, 2).reshape(3, DFF)

    c9 = jnp.concatenate([c_g[:, 0, :], c_ctx[None], jnp.zeros((7, D), f32)], axis=0)
    w_mod_l = w_mod[0]
    ncol = w_mod_l.shape[1]
    m_part = _ada_fwd(c9, w_mod_l, "ada_fwd")
    m_all = _all_gather_small(m_part, "gather_mod").transpose(1, 0, 2).reshape(16, 6, D)
    modx, modc = _mod_select(m_all, b_mod.reshape(6, D), "mod_select")

    pair = lambda a, b: jnp.concatenate([a, b], axis=-1)
    bre_g, bim_g = s5_b_re[0].transpose(0, 2, 1), s5_b_im[0].transpose(0, 2, 1)
    cre_g, cim_g = s5_c_re[0], s5_c_im[0]
    shared = (pair(bre_g, bim_g), pair(bim_g, bre_g), pair(cre_g, cim_g), pair(cim_g, cre_g))
    s5p = {}
    for tag, lre, lim, ls in (("f", s5_lambda_re_f, s5_lambda_im_f, s5_log_step_f), ("b", s5_lambda_re_b, s5_lambda_im_b, s5_log_step_b)):
        s5p[tag] = (pair(lre[0], lre[0])[:, None, :], pair(lim[0], lim[0])[:, None, :], ls[0].reshape(S5G, 1, 1)) + shared
    m_f, mb_f, mc_f, a1_f, a2_f = _s5_build(s5p["f"], False, "s5_build_f")
    m_b, mb_b, mc_b, a1_b, a2_b = _s5_build(s5p["b"], True, "s5_build_b")
    a1_f, a2_f, a1_b, a2_b = (a.reshape(S5G, SB) for a in (a1_f, a2_f, a1_b, a2_b))

    nw1, nw2, fnw = norm1_w, norm2_w, final_norm_w[None]
    cosf, sins = _rope_tables(L, Lc)
    p_ext, w_out_g, w_glu_g, w_up_g1 = _f1_fwd(x2, ctx2, modx, modc, nw1, w_in_t.T, cosf, sins, "f1_fwd",
                                               cargo=([w_out_l, w_glu_l, w_up_tl[:UP_HEAD]], False))
    nctx5 = Lc // TC
    u_g = _to_groups(p_ext[:, :S5W])
    s_f, s_b = _s5_inc(u_g, mb_f, mb_b, "s5_inc")
    hp_f, hp_b = _s5_carry(s_f, s_b, (a1_f, a2_f), (a1_b, a2_b), nctx5, "s5_carry")
    ys = _from_groups(_s5_out(u_g, m_f, m_b, hp_f, hp_b, mc_f, mc_b, "s5_out"))
    ld8 = lambda ld: jnp.pad(jnp.broadcast_to(ld[0][:, None], (RH, 128)), ((0, 8 - RH), (0, 0)))
    ldf8, ldb8 = ld8(ret_log_decay_f), ld8(ret_log_decay_b)
    of, ob, rp_f, rp_b, w_up_g2 = _ret_fwd(p_ext, ldf8, ldb8, nctx, "ret_fwd", cargo=([w_up_tl[UP_HEAD:]], False))
    w_out_f, w_glu_f = w_out_g.reshape(D, D), w_glu_g.reshape(S5W, S5W)
    x1, w_down_g = _mix_fwd(x2, ys, of, ob, p_ext, s5_d, s5_b_glu, modx, w_glu_f, w_out_f, "mix_fwd", cargo=([w_down_l], False))
    w_down_f = w_down_g.reshape(DFF, D)
    w_up_t = jnp.concatenate([w_up_g1, w_up_g2], axis=1).reshape(2 * DFF, D)

    (dx2, da, dgc, f_act, dffn, loss_acc, g_fnw, g_gate2, g_cb, g_cw) = _ffn_fwd(
        x1, tgt, nw2, modx, w_up_t[:DFF].T, w_up_t[DFF:].T, conv_f, conv_b, w_down_f, w_down_f.T, fnw, "ffn_fwd")
    dx1, dag, h2, g_nw2, dmx2 = _ffn_bwd(x1, dx2, da, dgc, nw2, modx, w_up_t, conv_f, "ffn_bwd")
    gw_down = _matmul_tn(f_act, dffn, "dw_down").reshape(NDEV, -1, D)
    gw_up_t = _matmul_tn(dag, h2, "dw_up").reshape(NDEV, -1, D)
    (dy_e, dud_e, do_e, dg_e, cat, dmix, s_act, dz, g_d, g_bglu, g_gate1, l_down) = _mix_bwd(
        x2, ys, of, ob, p_ext, s5_d, s5_b_glu, modx, w_glu_f, w_out_f, dx1, "mix_bwd", cargo=([gw_down], True))
    gw_out = _matmul_tn(cat, dmix, "dw_out").reshape(NDEV, -1, D)
    gw_glu = _matmul_tn(s_act, dz, "dw_glu").reshape(NDEV, -1, S5W)
    dq_f, dk_f, dv_f, dq_b, dk_b, dv_b, gld_f, gld_b, l_up, l_out, l_glu = _ret_bwd(
        p_ext, ldf8, ldb8, rp_f, rp_b, do_e, nctx, "ret_bwd", cargo=([gw_up_t, gw_out, gw_glu], True))

    du1, g_m, dhp_f, dhp_b, dmc_f, dmc_b = _s5_out_bwd(_to_groups(dy_e), u_g, m_f, m_b, hp_f, hp_b, mc_f, mc_b, "s5_out_bwd")
    ds_f, da1_f, da2_f = _s5_carry_bwd(dhp_f, hp_f, a1_f, a2_f, False, nctx5, "s5_carry_bwd_f")
    ds_b, da1_b, da2_b = _s5_carry_bwd(dhp_b, hp_b, a1_b, a2_b, True, nctx5, "s5_carry_bwd_b")
    du_g, dmb_f, dmb_b = _s5_inc_bwd(du1, u_g, ds_f, ds_b, mb_f, mb_b, "s5_inc_bwd")
    zero_p = jnp.zeros((S5G, S5P, SB), f32)
    gf = _s5_build_bwd(s5p["f"], (g_m, dmb_f, dmc_f, da1_f[:, None, :], da2_f[:, None, :]), (zero_p, zero_p), False, "s5_build_bwd_f")
    gb = _s5_build_bwd(s5p["b"], (g_m, dmb_b, dmc_b, da1_b[:, None, :], da2_b[:, None, :]), (gf[3], gf[4]), True, "s5_build_bwd_b")
    g_bre, g_bim = gb[3][:, :, :S5N].transpose(0, 2, 1), gb[3][:, :, S5N:].transpose(0, 2, 1)
    g_cre, g_cim = gb[4][:, :, :S5N], gb[4][:, :, S5N:]

    early = {
        "conv_w": g_cw, "s5_lambda_re_f": gf[0][:, 0, :S5N], "s5_lambda_im_f": gf[1][:, 0, :S5N],
        "s5_log_step_f": gf[2], "s5_lambda_re_b": gb[0][:, 0, :S5N], "s5_lambda_im_b": gb[1][:, 0, :S5N], "s5_log_step_b": gb[2],
        "s5_b_re": g_bre, "s5_b_im": g_bim, "s5_c_re": g_cre, "s5_c_im": g_cim, "s5_d": g_d, "s5_b_glu": g_bglu,
        "ret_log_decay_f": gld_f[:RH, 0], "ret_log_decay_b": gld_b[:RH, 0], "norm2_w": g_nw2, "conv_b": g_cb, "final_norm_w": g_fnw,
    }
    e_names = [n for n in SMALL if n in early]
    packed_e, eoffs = _pack_small([early[n].astype(f32) for n in e_names])
    grad_x, dp_ext, h1, g_nw1, dmx1, dmc1 = _f1_bwd(
        x2, ctx2, modx, modc, nw1, w_in_t, cosf, sins, dx1, (_from_groups(du_g), dud_e, dq_f, dq_b, dk_f, dk_b, dv_f, dv_b, dg_e), "f1_bwd")
    gw_in_t, land_e = _matmul_tn(dp_ext, h1, "dw_in", cargo=([packed_e], False))
    g_in_t = _reduce_scatter_two_level(gw_in_t.reshape(NDEV, -1, D), "scatter_dw_in")

    dmx = dmx1 + dmx2
    dmx = dmx.at[2].set(g_gate1[0]).at[5].set(g_gate2[0])
    dm_me = jnp.stack([dmx.reshape(-1), dmc1.reshape(-1)], axis=0)
    dm_all = _all_gather_small(dm_me.reshape(8, -1), "gather_dmod").reshape(NDEV, 2, 6 * D)
    dmx_all, dmc_all = dm_all[:, 0, :], dm_all[:, 1, :]
    my_cols = lambda a: lax.dynamic_slice(a, (0, me * ncol), (NDEV, ncol))
    gw_mod, g_bmod, dc9 = _ada_bwd(c9, dmx_all, dmc_all, my_cols(dmx_all), my_cols(dmc_all), w_mod_l, "ada_bwd")

    sshape = lambda n: (3, DFF) if n == "conv_w" else W[n].shape
    G = dict(zip(e_names, _unpack_small(_sum8(land_e, "reduce_early"), eoffs, [sshape(n) for n in e_names])))
    late = {"c_ctx": dc9[8], "norm1_w": g_nw1}
    packed_l, loffs = _pack_small([late[n].astype(f32) for n in late])
    G.update(zip(late, _unpack_small(_all_reduce_small(packed_l, "reduce_late"), loffs, [W[n].shape for n in late])))
    G["conv_w"] = lax.dynamic_slice(G["conv_w"], (0, me * per_cv), (3, per_cv))[None]
    G["b_mod"] = g_bmod.reshape(b_mod.shape)
    G["w_mod"] = gw_mod[None]
    G["w_in"] = g_in_t.T[None]
    G["w_up"] = _sum8(l_up, "sum_dw_up").T[None]

    delta, new_m, new_v = {}, {}, {}
    sm_names = SMALL[1:] + ["b_mod"]
    rows = lambda a: a.reshape(-1, a.shape[-1])
    outs = _adamw_many(*[[rows(d[n]) for n in sm_names] for d in (W, G, M, V)], "adamw_small")
    for dst, src in zip((delta, new_m, new_v), outs):
        dst.update({n: a.reshape(W[n].shape) for n, a in zip(sm_names, src)})
    for n in ["w_mod", "w_in", "w_up", "conv_w"]:
        d, nm, nv = _adamw(W[n][0], G[n][0], M[n][0], V[n][0], "adamw_" + n)
        delta[n], new_m[n], new_v[n] = d[None], nm[None], nv[None]
    for n, land in (("w_out", l_out), ("w_down", l_down), ("s5_w_glu", l_glu)):
        g, d, nm, nv = _adamw_landed(land, W[n][0], M[n][0], V[n][0], "adamw_" + n)
        G[n], delta[n], new_m[n], new_v[n] = g[None], d[None], nm[None], nv[None]

    loss = lax.psum(loss_acc[0, 0], ("x", "y", "c"))
    return (loss, grad_x[None], *[G[n] for n in WEIGHTS], *[delta[n] for n in WEIGHTS], *[new_m[n] for n in WEIGHTS],
            *[new_v[n] for n in WEIGHTS])
```

```python
import functools

import numpy as np
import jax
import jax.numpy as jnp
from jax import lax
from jax.experimental import pallas as pl
from jax.experimental.pallas import tpu as pltpu

f32, bf16 = jnp.float32, jnp.bfloat16

D = 1024
S5W, S5G, S5P, S5N = 512, 32, 16, 64
TC = 16
TCP = TC * S5P
SB = 2 * S5N
GBK = 8
UP_HEAD = 192
CARRY_UNROLL = 8
RH, DH = 4, 128
RW = RH * DH
INC = S5W + 4 * RW
DFF = 2816
T = 128
R = 256
RF = 128
RFB = 256
HALO = 8
EPS = 1e-6
ROPE_THETA = 10000.0
GRID_W = 64
NDEV = 8
LR, B1, B2, AEPS, WD, STEP = 0.001, 0.9, 0.999, 1e-08, 0.01, 10
VMEM_LIMIT = 60 * 1024 * 1024
ACC_TILE_BYTES = 6 * 1024 * 1024
MESH = pl.DeviceIdType.MESH

_CP = functools.partial(pltpu.CompilerParams, vmem_limit_bytes=VMEM_LIMIT)
_ARB = ("arbitrary",)
_ANY = pl.BlockSpec(memory_space=pl.ANY)


def _dg(a, b, dims):
    return lax.dot_general(a.astype(bf16), b.astype(bf16), (dims, ((), ())), preferred_element_type=f32)


@jax.custom_vjp
def dnn(a, b):
    return _dg(a, b, ((1,), (0,)))


@jax.custom_vjp
def dnt(a, b):
    return _dg(a, b, ((1,), (1,)))


@jax.custom_vjp
def dtn(a, b):
    return _dg(a, b, ((0,), (0,)))


dnn.defvjp(lambda a, b: (dnn(a, b), (a, b)), lambda r, g: (dnt(g, r[1]).astype(r[0].dtype), dtn(r[0], g).astype(r[1].dtype)))
dnt.defvjp(lambda a, b: (dnt(a, b), (a, b)), lambda r, g: (dnn(g, r[1]).astype(r[0].dtype), dtn(g, r[0]).astype(r[1].dtype)))
dtn.defvjp(lambda a, b: (dtn(a, b), (a, b)), lambda r, g: (dnt(r[1], g).astype(r[0].dtype), dnn(r[0], g).astype(r[1].dtype)))


@jax.custom_vjp
def _dnn_const(a, w, wt):
    return dnn(a, w)


_dnn_const.defvjp(lambda a, w, wt: (dnn(a, w), wt), lambda wt, g: (dnn(g, wt), None, None))


_GELU_C0, _GELU_C1 = float(np.sqrt(2.0 / np.pi)), 0.044715


@jax.custom_vjp
def _gelu(x):
    return _gelu_fwd(x)[0]


def _gelu_fwd(x):
    t = jnp.tanh(_GELU_C0 * (x + _GELU_C1 * (x * x * x)))
    return x * (0.5 * (1.0 + t)), (x, t)


def _gelu_bwd(res, g):
    x, t = res
    return (g * (0.5 * (1.0 + t) + (0.5 * _GELU_C0) * x * (1.0 - t * t) * (1.0 + (3.0 * _GELU_C1) * (x * x))),)


_gelu.defvjp(_gelu_fwd, _gelu_bwd)


def _rms(t, w):
    return t * lax.rsqrt(jnp.mean(t * t, axis=-1, keepdims=True) + EPS) * w


@jax.custom_vjp
def _norm_mod(x, w, shift, scale):
    return _norm_mod_fwd(x, w, shift, scale)[0]


def _norm_mod_fwd(x, w, shift, scale):
    r = lax.rsqrt(jnp.mean(x * x, axis=-1, keepdims=True) + EPS)
    n = x * r
    return (n * w) * (1.0 + scale) + shift, (n, r, w, scale)


def _norm_mod_bwd(res, dh):
    n, r, w, scale = res
    col = jnp.sum(dh * n, axis=0, keepdims=True)
    dn = dh * (w * (1.0 + scale))
    dx = r * (dn - n * jnp.mean(dn * n, axis=-1, keepdims=True))
    return dx, col * (1.0 + scale), jnp.sum(dh, axis=0, keepdims=True), col * w


_norm_mod.defvjp(_norm_mod_fwd, _norm_mod_bwd)


def _const_spec(shape):
    n = len(shape)
    return pl.BlockSpec(shape, lambda i, _n=n: (0,) * _n, pipeline_mode=pl.Buffered(1))


def _acc_spec(shape):
    n = len(shape)
    return pl.BlockSpec(shape, lambda i, _n=n: (0,) * _n)


def _me():
    return 4 * lax.axis_index("x") + 2 * lax.axis_index("y") + lax.axis_index("c")


def _peer(r):
    x, y, c = lax.axis_index("x"), lax.axis_index("y"), lax.axis_index("c")
    px = 1 - x if (r >> 2) & 1 else x
    py = 1 - y if (r >> 1) & 1 else y
    pc = 1 - c if r & 1 else c
    return (px, py, pc), 4 * px + 2 * py + pc


def _all_gather_small(v, name):
    r, c = v.shape

    def body(v_ref, out_ref, send_sems, recv_sems):
        me = _me()
        out_ref[me] = v_ref[...]
        sends = []
        for k in range(1, NDEV):
            peer, _ = _peer(k)
            cp = pltpu.make_async_remote_copy(src_ref=v_ref, dst_ref=out_ref.at[me], send_sem=send_sems.at[k - 1],
                                              recv_sem=recv_sems.at[k - 1], device_id=peer, device_id_type=MESH)
            cp.start()
            sends.append(cp)
        for k in range(1, NDEV):
            peer, pidx = _peer(k)
            pltpu.make_async_remote_copy(src_ref=v_ref, dst_ref=out_ref.at[pidx], send_sem=send_sems.at[k - 1],
                                         recv_sem=recv_sems.at[k - 1], device_id=peer, device_id_type=MESH).wait_recv()
        for cp in sends:
            cp.wait_send()

    return pl.pallas_call(
        body, name=name, out_shape=jax.ShapeDtypeStruct((NDEV, r, c), v.dtype),
        in_specs=[pl.BlockSpec(memory_space=pltpu.VMEM)], out_specs=pl.BlockSpec(memory_space=pltpu.VMEM),
        scratch_shapes=[pltpu.SemaphoreType.DMA((NDEV - 1,)), pltpu.SemaphoreType.DMA((NDEV - 1,))],
        compiler_params=_CP(),
    )(v)


def _all_reduce_small(v, name):
    r, c = v.shape

    def body(v_ref, out_ref, land, send_sems, recv_sems):
        me = _me()
        land[me] = v_ref[...]
        sends = []
        for k in range(1, NDEV):
            peer, _ = _peer(k)
            cp = pltpu.make_async_remote_copy(src_ref=v_ref, dst_ref=land.at[me], send_sem=send_sems.at[k - 1],
                                              recv_sem=recv_sems.at[k - 1], device_id=peer, device_id_type=MESH)
            cp.start()
            sends.append(cp)
        for k in range(1, NDEV):
            peer, pidx = _peer(k)
            pltpu.make_async_remote_copy(src_ref=v_ref, dst_ref=land.at[pidx], send_sem=send_sems.at[k - 1],
                                         recv_sem=recv_sems.at[k - 1], device_id=peer, device_id_type=MESH).wait_recv()
        for cp in sends:
            cp.wait_send()
        acc = land[0]
        for j in range(1, NDEV):
            acc = acc + land[j]
        out_ref[...] = acc

    return pl.pallas_call(
        body, name=name, out_shape=jax.ShapeDtypeStruct((r, c), v.dtype),
        in_specs=[pl.BlockSpec(memory_space=pltpu.VMEM)], out_specs=pl.BlockSpec(memory_space=pltpu.VMEM),
        scratch_shapes=[pltpu.VMEM((NDEV, r, c), v.dtype), pltpu.SemaphoreType.DMA((NDEV - 1,)),
                        pltpu.SemaphoreType.DMA((NDEV - 1,))],
        compiler_params=_CP(),
    )(v)


class _Exchange:
    def __init__(self, srcs, dsts, send_sems, recv_sems, local_sems, scatter):
        me = _me()
        n = len(srcs)
        self.sends, self.recvs, self.locals = [], [], []
        for a, (s, d) in enumerate(zip(srcs, dsts)):
            self.locals.append(pltpu.make_async_copy(s.at[me] if scatter else s, d.at[me], local_sems.at[a]))
        for k in range(1, NDEV):
            peer, pidx = _peer(k)
            for a, (s, d) in enumerate(zip(srcs, dsts)):
                src = s.at[pidx] if scatter else s
                sem = (k - 1) * n + a
                for dst, out in ((d.at[me], self.sends), (d.at[pidx], self.recvs)):
                    out.append(pltpu.make_async_remote_copy(src_ref=src, dst_ref=dst, send_sem=send_sems.at[sem],
                                                            recv_sem=recv_sems.at[sem], device_id=peer, device_id_type=MESH))

    def start(self):
        for cp in self.locals + self.sends:
            cp.start()

    def wait(self):
        for cp in self.recvs:
            cp.wait_recv()
        for cp in self.sends:
            cp.wait_send()
        for cp in self.locals:
            cp.wait()


def _exchange_shapes(arrays, scatter):
    return [jax.ShapeDtypeStruct(a.shape if scatter else (NDEV,) + a.shape, a.dtype) for a in arrays]


def _exchange_sems(n):
    return [pltpu.SemaphoreType.DMA(((NDEV - 1) * n,)), pltpu.SemaphoreType.DMA(((NDEV - 1) * n,)), pltpu.SemaphoreType.DMA((n,))]


def _chips():
    x, y, c = lax.axis_index("x"), lax.axis_index("y"), lax.axis_index("c")
    return (x, y, c), (x, y, 1 - c), [(1 - x, y), (x, 1 - y), (1 - x, 1 - y)]


def _gather_two_level(arrays, name):
    n = len(arrays)

    def body(*refs):
        srcs, outs = refs[:n], refs[n:2 * n]
        send_sems, recv_sems = refs[2 * n:]
        me, sibling, chips = _chips()
        c = me[2]
        idx = lambda p: 4 * p[0] + 2 * p[1] + p[2]

        def copy(a, k, block, to, src=None):
            return pltpu.make_async_remote_copy(
                src_ref=outs[a].at[idx(block)] if src is None else src, dst_ref=outs[a].at[idx(block)],
                send_sem=send_sems.at[7 * a + k], recv_sem=recv_sems.at[7 * a + k], device_id=to, device_id_type=MESH)

        first, passed = [], []
        for a in range(n):
            outs[a][idx(me)] = srcs[a][...]
            first += [copy(a, 0, me, sibling, src=srcs[a])]
            first += [copy(a, 1 + j, me, (*chip, c), src=srcs[a]) for j, chip in enumerate(chips)]
        for cp in first:
            cp.start()
        for a in range(n):
            for j, chip in enumerate(chips):
                copy(a, 1 + j, (*chip, c), me).wait_recv()
                cp = copy(a, 4 + j, (*chip, c), sibling)
                cp.start()
                passed.append(cp)
        for a in range(n):
            copy(a, 0, sibling, me).wait_recv()
            for j, chip in enumerate(chips):
                copy(a, 4 + j, (*chip, 1 - c), me).wait_recv()
        for cp in first + passed:
            cp.wait_send()

    vm = pl.BlockSpec(memory_space=pltpu.VMEM)
    return pl.pallas_call(
        body, name=name, out_shape=[jax.ShapeDtypeStruct((NDEV,) + a.shape, a.dtype) for a in arrays],
        in_specs=[vm] * n, out_specs=[vm] * n,
        scratch_shapes=[pltpu.SemaphoreType.DMA((7 * n,)), pltpu.SemaphoreType.DMA((7 * n,))],
        compiler_params=_CP(),
    )(*arrays)


def _reduce_scatter_two_level(g, name):
    _, r, c = g.shape
    nchip = NDEV // 2

    def body(g_ref, o_ref, stage, part, land, d_send, d_recv, i_send, i_recv):
        me, sibling, chips = _chips()
        x, y, cc = me
        mine = 2 * x + y

        def blk(k, core):
            return 2 * k + core

        swaps = [pltpu.make_async_remote_copy(src_ref=g_ref.at[blk(k, 1 - cc)], dst_ref=stage.at[k], send_sem=d_send.at[k],
                                              recv_sem=d_recv.at[k], device_id=sibling, device_id_type=MESH) for k in range(nchip)]
        for cp in swaps:
            cp.start()
        for cp in swaps:
            cp.wait_recv()
        for k in range(nchip):
            part[k] = (g_ref[blk(k, cc)].astype(f32) + stage[k].astype(f32)).astype(bf16)
        sends = []
        for j, chip in enumerate(chips):
            kd = 2 * chip[0] + chip[1]
            cp = pltpu.make_async_remote_copy(src_ref=part.at[kd], dst_ref=land.at[mine], send_sem=i_send.at[j],
                                              recv_sem=i_recv.at[j], device_id=(*chip, cc), device_id_type=MESH)
            cp.start()
            sends.append(cp)
        land[mine] = part[mine]
        for j, chip in enumerate(chips):
            ks = 2 * chip[0] + chip[1]
            pltpu.make_async_remote_copy(src_ref=part.at[ks], dst_ref=land.at[ks], send_sem=i_send.at[j], recv_sem=i_recv.at[j],
                                         device_id=(*chip, cc), device_id_type=MESH).wait_recv()
        for cp in swaps + sends:
            cp.wait_send()
        acc = land[0].astype(f32)
        for k in range(1, nchip):
            acc = acc + land[k].astype(f32)
        o_ref[...] = acc

    vm = pl.BlockSpec(memory_space=pltpu.VMEM)
    return pl.pallas_call(
        body, name=name, out_shape=jax.ShapeDtypeStruct((r, c), f32), in_specs=[vm], out_specs=vm,
        scratch_shapes=[pltpu.VMEM((nchip, r, c), g.dtype)] * 3 + [pltpu.SemaphoreType.DMA((nchip,)), pltpu.SemaphoreType.DMA((nchip,)),
                                                                   pltpu.SemaphoreType.DMA((3,)), pltpu.SemaphoreType.DMA((3,))],
        compiler_params=_CP(),
    )(g)


class _Cargo:
    def __init__(self, cargo):
        self.arrays, self.scatter = cargo if cargo else ([], False)
        self.n = len(self.arrays)

    def in_specs(self):
        return [_ANY] * self.n

    def out_shapes(self):
        return _exchange_shapes(self.arrays, self.scatter)

    def sems(self):
        return _exchange_sems(self.n) if self.n else []

    def split(self, refs, n_in, n_out, n_scratch):
        n = self.n
        return refs[:n_in], refs[n_in + n:n_in + n + n_out], refs[n_in + 2 * n + n_out:n_in + 2 * n + n_out + n_scratch]

    def ride(self, refs, n_in, n_out, grid):
        if not self.n:
            return
        n = self.n
        ex = _Exchange(refs[n_in:n_in + n], refs[n_in + n + n_out:n_in + 2 * n + n_out], *refs[-3:], self.scatter)
        grid = (grid,) if isinstance(grid, int) else tuple(grid)
        first = functools.reduce(jnp.logical_and, [pl.program_id(a) == 0 for a in range(len(grid))])
        last = functools.reduce(jnp.logical_and, [pl.program_id(a) == g - 1 for a, g in enumerate(grid)])

        @pl.when(first)
        def _():
            ex.start()

        @pl.when(last)
        def _():
            ex.wait()


def _sum8(land, name):
    _, r, c = land.shape
    rb = next((b for b in (256, 64, 32) if r % b == 0), r)

    def body(l_ref, o_ref):
        acc = l_ref[0].astype(f32)
        for j in range(1, NDEV):
            acc = acc + l_ref[j].astype(f32)
        o_ref[...] = acc

    return pl.pallas_call(
        body, name=name, grid=(r // rb,), out_shape=jax.ShapeDtypeStruct((r, c), f32),
        in_specs=[pl.BlockSpec((NDEV, rb, c), lambda i: (0, i, 0))], out_specs=pl.BlockSpec((rb, c), lambda i: (i, 0)),
        compiler_params=_CP(dimension_semantics=("parallel",)),
    )(land)


def _ada_fwd(c9, w_mod_l, name):
    def body(c_ref, w_ref, o_ref):
        o_ref[...] = dnn(jax.nn.silu(c_ref[...]), w_ref[...])

    return pl.pallas_call(body, name=name, out_shape=jax.ShapeDtypeStruct((16, w_mod_l.shape[1]), f32),
                          compiler_params=_CP())(c9, w_mod_l)


def _mod_select(m_all, b_mod6, name):
    def body(m_ref, b_ref, mx_ref, mc_ref):
        me = _me()
        mx_ref[...] = m_ref[me] + b_ref[...]
        mc_ref[...] = m_ref[8] + b_ref[...]

    return pl.pallas_call(body, name=name, out_shape=[jax.ShapeDtypeStruct((6, D), f32)] * 2, compiler_params=_CP())(m_all, b_mod6)


def _ada_bwd(c9, dmx_all, dmc_all, dmx_l, dmc_l, w_mod_l, name):
    ncol = w_mod_l.shape[1]

    def rowsum(r):
        acc = r[0:1]
        for j in range(1, NDEV):
            acc = acc + r[j:j + 1]
        return acc

    def body(c_ref, xa_ref, ca_ref, xl_ref, cl_ref, w_ref, gw_ref, gb_ref, dc_ref):
        s9, vjp = jax.vjp(jax.nn.silu, c_ref[...])
        dm9 = jnp.concatenate([xl_ref[...], rowsum(cl_ref[...]), jnp.zeros((7, ncol), f32)], axis=0)
        gw_ref[...] = dtn(s9, dm9)
        gb_ref[...] = rowsum(xa_ref[...]) + rowsum(ca_ref[...])
        dc_ref[...] = vjp(dnt(dm9, w_ref[...]))[0]

    return pl.pallas_call(
        body, name=name,
        out_shape=[jax.ShapeDtypeStruct((D, ncol), f32), jax.ShapeDtypeStruct((1, 6 * D), f32), jax.ShapeDtypeStruct((16, D), f32)],
        compiler_params=_CP())(c9, dmx_all, dmc_all, dmx_l, dmc_l, w_mod_l)


def _lane_sign(rank):
    shape = (1,) * (rank - 1) + (SB,)
    return jnp.where(lax.broadcasted_iota(jnp.int32, shape, rank - 1) < S5N, -1.0, 1.0)


def _s5_build_fn(lre2, lim2, ls, bn, bs, cn, cs, rev):
    sg = _lane_sign(3)
    s = jnp.exp(ls)
    ar, ai = lre2 * s, lim2 * s
    e = jnp.exp(ar)
    nr, ni = e * jnp.cos(ai) - 1.0, e * jnp.sin(ai)
    den = lre2 * lre2 + lim2 * lim2
    cr, ci = (nr * lre2 + ni * lim2) / den, (ni * lre2 - nr * lim2) / den
    bbn = cr * bn + (ci * sg) * bs
    bbs = cr * bs - (ci * sg) * bn

    def powers(ex):
        m, ang = jnp.exp(ex * ar), ex * ai
        return m * jnp.cos(ang), m * jnp.sin(ang) * sg

    def times(tabs, xn, xs):
        f1, f2 = tabs
        return f1[:, :, None, :] * xn[:, None, :, :] + f2[:, :, None, :] * xs[:, None, :, :]

    t = lax.broadcasted_iota(jnp.int32, (1, TC, 1), 1).astype(f32)
    if rev:
        e_src, e_dst, e_out, e_in = t - (TC - 1.0), (TC - 1.0) - t, t, TC - t
    else:
        e_src, e_dst, e_out, e_in = -t, t, (TC - 1.0) - t, t + 1.0
    g = lre2.shape[0]
    flat = lambda a: a.reshape(g, TCP, SB)
    conj = -_lane_sign(4)
    ll = flat(times(powers(e_src), bbn, bbs))
    rr = flat(times(powers(e_dst), cn, cs) * conj)
    mb = flat(times(powers(e_out), bbn, bbs))
    mct = flat(times(powers(e_in), cn, cs) * conj)
    a1, a2 = powers(float(TC))
    row = lax.broadcasted_iota(jnp.int32, (TCP, TCP), 0) // S5P
    col = lax.broadcasted_iota(jnp.int32, (TCP, TCP), 1) // S5P
    mask = jnp.where((col <= row) if rev else (col >= row), 1.0, 0.0)
    m = jnp.concatenate([dnt(ll[j], rr[j])[None] for j in range(g)], axis=0) * mask
    return m, mb, mct, a1, a2


def _gspec(*tail):
    nt = len(tail)
    return pl.BlockSpec((GBK,) + tail, lambda i, _n=nt: (i,) + (0,) * _n)


def _s5_build(params, rev, name):
    def body(l1, l2, ls, bn, bs, cn, cs, m_ref, mb_ref, mc_ref, a1_ref, a2_ref):
        m, mb, mct, a1, a2 = _s5_build_fn(l1[...], l2[...], ls[...], bn[...], bs[...], cn[...], cs[...], rev)
        m_ref[...], mb_ref[...], mc_ref[...] = m.astype(bf16), mb.astype(bf16), mct.astype(bf16)
        a1_ref[...], a2_ref[...] = a1, a2

    vec, pm = _gspec(1, SB), _gspec(S5P, SB)
    return pl.pallas_call(
        body, name=name, grid=(S5G // GBK,),
        in_specs=[vec, vec, _gspec(1, 1), pm, pm, pm, pm],
        out_specs=[_gspec(TCP, TCP), _gspec(TCP, SB), _gspec(TCP, SB), vec, vec],
        out_shape=[jax.ShapeDtypeStruct((S5G, TCP, TCP), bf16), jax.ShapeDtypeStruct((S5G, TCP, SB), bf16),
                   jax.ShapeDtypeStruct((S5G, TCP, SB), bf16), jax.ShapeDtypeStruct((S5G, 1, SB), f32),
                   jax.ShapeDtypeStruct((S5G, 1, SB), f32)],
        compiler_params=_CP(dimension_semantics=("parallel",)),
    )(*params)


def _s5_build_bwd(params, cots, prev, rev, name):
    def body(l1, l2, ls, bn, bs, cn, cs, dm, dmb, dmc, da1, da2, pb, pc, gl1, gl2, gls, gb, gc):
        prim = (l1[...], l2[...], ls[...], bn[...], bs[...], cn[...], cs[...])
        _, vjp = jax.vjp(functools.partial(_s5_build_fn, rev=rev), *prim)
        d1, d2, dls, dbn, dbs, dcn, dcs = vjp((dm[...], dmb[...], dmc[...], da1[...], da2[...]))
        gl1[...] = d1 + pltpu.roll(d1, S5N, axis=2)
        gl2[...] = d2 + pltpu.roll(d2, S5N, axis=2)
        gls[...] = dls
        gb[...] = dbn + pltpu.roll(dbs, S5N, axis=2) + pb[...]
        gc[...] = dcn + pltpu.roll(dcs, S5N, axis=2) + pc[...]

    vec, pm, big = _gspec(1, SB), _gspec(S5P, SB), _gspec(TCP, SB)
    return pl.pallas_call(
        body, name=name, grid=(S5G // GBK,),
        in_specs=[vec, vec, _gspec(1, 1), pm, pm, pm, pm, _gspec(TCP, TCP), big, big, vec, vec, pm, pm],
        out_specs=[vec, vec, _gspec(1, 1), pm, pm],
        out_shape=[jax.ShapeDtypeStruct((S5G, 1, SB), f32), jax.ShapeDtypeStruct((S5G, 1, SB), f32),
                   jax.ShapeDtypeStruct((S5G, 1, 1), f32), jax.ShapeDtypeStruct((S5G, S5P, SB), f32),
                   jax.ShapeDtypeStruct((S5G, S5P, SB), f32)],
        compiler_params=_CP(dimension_semantics=("parallel",)),
    )(*params, *cots, *prev)


def _s5_inc(u, mb_f, mb_b, name):
    nc = u.shape[1]

    def body(u_ref, mf_ref, mb_ref, sf_ref, sb_ref):
        for j in range(GBK):
            sf_ref[:, j, :] = jnp.dot(u_ref[j], mf_ref[j], preferred_element_type=f32)
            sb_ref[:, j, :] = jnp.dot(u_ref[j], mb_ref[j], preferred_element_type=f32)

    sspec = pl.BlockSpec((nc, GBK, SB), lambda i: (0, i, 0))
    return pl.pallas_call(
        body, name=name, grid=(S5G // GBK,), in_specs=[_gspec(nc, TCP), _gspec(TCP, SB), _gspec(TCP, SB)],
        out_specs=[sspec, sspec], out_shape=[jax.ShapeDtypeStruct((nc, S5G, SB), f32)] * 2,
        compiler_params=_CP(dimension_semantics=("parallel",)),
    )(u, mb_f, mb_b)


def _idx_fwd(nctx, nch):
    return lambda i: i


def _idx_rev(nctx, nch):
    return lambda i: jnp.where(i < nctx, nctx - 1 - i, nch + nctx - 1 - i)


def _carry_loop(nc, step, init):
    def trip(i, c):
        for k in range(CARRY_UNROLL):
            c = step(i * CARRY_UNROLL + k, c)
        return c

    return lax.fori_loop(0, nc // CARRY_UNROLL, trip, init)


def _s5_carry(s_f, s_b, a_f, a_b, nctx, name):
    nc = s_f.shape[0]
    idx_b = _idx_rev(nctx, nc)

    def body(sf_ref, sb_ref, f1_ref, f2_ref, b1_ref, b2_ref, hf_ref, hb_ref):
        f1, f2, b1, b2 = f1_ref[...], f2_ref[...], b1_ref[...], b2_ref[...]

        def step(i, c):
            hf, hfs, hb, hbs = c
            rb = idx_b(i)
            hf_ref[i] = hf
            hb_ref[rb] = hb
            sf, sb = sf_ref[i], sb_ref[rb]
            return (f1 * hf + f2 * hfs + sf, f1 * hfs - f2 * hf + pltpu.roll(sf, S5N, axis=1),
                    b1 * hb + b2 * hbs + sb, b1 * hbs - b2 * hb + pltpu.roll(sb, S5N, axis=1))

        z = jnp.zeros((S5G, SB), f32)
        _carry_loop(nc, step, (z, z, z, z))

    return pl.pallas_call(body, name=name, out_shape=[jax.ShapeDtypeStruct(s_f.shape, f32)] * 2,
                          compiler_params=_CP())(s_f, s_b, *a_f, *a_b)


def _s5_carry_bwd(dhp, hp, a1, a2, rev, nctx, name):
    nc = hp.shape[0]
    idx = (_idx_rev if rev else _idx_fwd)(nctx, nc)

    def body(dhp_ref, hp_ref, a1_ref, a2_ref, ds_ref, d1_ref, d2_ref):
        f1, f2 = a1_ref[...], a2_ref[...]

        def step(k, carry):
            ab, abs_, d1, d2 = carry
            r = idx(nc - 1 - k)
            ds_ref[r] = ab
            h, dh = hp_ref[r], dhp_ref[r]
            return (dh + f1 * ab - f2 * abs_, pltpu.roll(dh, S5N, axis=1) + f1 * abs_ + f2 * ab,
                    d1 + ab * h, d2 + ab * pltpu.roll(h, S5N, axis=1))

        z = jnp.zeros((S5G, SB), f32)
        _, _, d1, d2 = _carry_loop(nc, step, (z, z, z, z))
        d1_ref[...], d2_ref[...] = d1, d2

    return pl.pallas_call(
        body, name=name,
        out_shape=[jax.ShapeDtypeStruct(hp.shape, f32), jax.ShapeDtypeStruct((S5G, SB), f32), jax.ShapeDtypeStruct((S5G, SB), f32)],
        compiler_params=_CP())(dhp, hp, a1, a2)


def _s5_out(u, m_f, m_b, hp_f, hp_b, mc_f, mc_b, name):
    nc = u.shape[1]

    def body(u_ref, mf_ref, mb_ref, hf_ref, hb_ref, cf_ref, cb_ref, y_ref):
        for j in range(GBK):
            uj = u_ref[j]
            y_ref[j] = (jnp.dot(uj, mf_ref[j], preferred_element_type=f32) + jnp.dot(uj, mb_ref[j], preferred_element_type=f32)
                        + dnt(hf_ref[:, j, :], cf_ref[j]) + dnt(hb_ref[:, j, :], cb_ref[j])).astype(bf16)

    sspec = pl.BlockSpec((nc, GBK, SB), lambda i: (0, i, 0))
    return pl.pallas_call(
        body, name=name, grid=(S5G // GBK,),
        in_specs=[_gspec(nc, TCP), _gspec(TCP, TCP), _gspec(TCP, TCP), sspec, sspec, _gspec(TCP, SB), _gspec(TCP, SB)],
        out_specs=_gspec(nc, TCP), out_shape=jax.ShapeDtypeStruct((S5G, nc, TCP), bf16),
        compiler_params=_CP(dimension_semantics=("parallel",)),
    )(u, m_f, m_b, hp_f, hp_b, mc_f, mc_b)


def _s5_out_bwd(dy, u, m_f, m_b, hp_f, hp_b, mc_f, mc_b, name):
    nc = u.shape[1]

    def body(dy_ref, u_ref, mf_ref, mb_ref, hf_ref, hb_ref, cf_ref, cb_ref, du_ref, g_ref, dhf_ref, dhb_ref, dcf_ref, dcb_ref):
        for j in range(GBK):
            dyj = dy_ref[j]
            du_ref[j] = dnt(dyj, mf_ref[j]) + dnt(dyj, mb_ref[j])
            g_ref[j] = dtn(u_ref[j], dyj)
            dhf_ref[:, j, :] = dnn(dyj, cf_ref[j])
            dhb_ref[:, j, :] = dnn(dyj, cb_ref[j])
            dcf_ref[j] = dtn(dyj, hf_ref[:, j, :])
            dcb_ref[j] = dtn(dyj, hb_ref[:, j, :])

    sspec = pl.BlockSpec((nc, GBK, SB), lambda i: (0, i, 0))
    sshape = jax.ShapeDtypeStruct((nc, S5G, SB), f32)
    cshape = jax.ShapeDtypeStruct((S5G, TCP, SB), f32)
    return pl.pallas_call(
        body, name=name, grid=(S5G // GBK,),
        in_specs=[_gspec(nc, TCP), _gspec(nc, TCP), _gspec(TCP, TCP), _gspec(TCP, TCP), sspec, sspec, _gspec(TCP, SB), _gspec(TCP, SB)],
        out_specs=[_gspec(nc, TCP), _gspec(TCP, TCP), sspec, sspec, _gspec(TCP, SB), _gspec(TCP, SB)],
        out_shape=[jax.ShapeDtypeStruct((S5G, nc, TCP), f32), jax.ShapeDtypeStruct((S5G, TCP, TCP), f32), sshape, sshape, cshape, cshape],
        compiler_params=_CP(dimension_semantics=("parallel",)),
    )(dy, u, m_f, m_b, hp_f, hp_b, mc_f, mc_b)


def _s5_inc_bwd(du1, u, ds_f, ds_b, mb_f, mb_b, name):
    nc = u.shape[1]

    def body(du1_ref, u_ref, dsf_ref, dsb_ref, mf_ref, mb_ref, du_ref, dmf_ref, dmb_ref):
        for j in range(GBK):
            dsf, dsb = dsf_ref[:, j, :], dsb_ref[:, j, :]
            du_ref[j] = (du1_ref[j] + dnt(dsf, mf_ref[j]) + dnt(dsb, mb_ref[j])).astype(bf16)
            dmf_ref[j] = dtn(u_ref[j], dsf)
            dmb_ref[j] = dtn(u_ref[j], dsb)

    sspec = pl.BlockSpec((nc, GBK, SB), lambda i: (0, i, 0))
    cshape = jax.ShapeDtypeStruct((S5G, TCP, SB), f32)
    return pl.pallas_call(
        body, name=name, grid=(S5G // GBK,),
        in_specs=[_gspec(nc, TCP), _gspec(nc, TCP), sspec, sspec, _gspec(TCP, SB), _gspec(TCP, SB)],
        out_specs=[_gspec(nc, TCP), _gspec(TCP, SB), _gspec(TCP, SB)],
        out_shape=[jax.ShapeDtypeStruct((S5G, nc, TCP), bf16), cshape, cshape],
        compiler_params=_CP(dimension_semantics=("parallel",)),
    )(du1, u, ds_f, ds_b, mb_f, mb_b)


def _to_groups(a):
    n = a.shape[0]
    return a.reshape(n // TC, TC, S5G, S5P).transpose(2, 0, 1, 3).reshape(S5G, n // TC, TCP)


def _from_groups(a):
    nc = a.shape[1]
    return a.reshape(S5G, nc, TC, S5P).transpose(1, 2, 0, 3).reshape(nc * TC, S5W)


def _swap_pairs(t):
    lane = lax.broadcasted_iota(jnp.int32, t.shape, 1)
    return jnp.where(lane % 2 == 0, pltpu.roll(t, DH - 1, axis=1), pltpu.roll(t, 1, axis=1))


def _rot(t, cosf, sins):
    return t * cosf + _swap_pairs(t) * sins


def _rot_t(d, cosf, sins):
    return d * cosf - _swap_pairs(d) * sins


def _ret_tables(ld, rev):
    pos = lax.broadcasted_iota(jnp.int32, (T, 1), 0).astype(f32)
    diff = pos - lax.broadcasted_iota(jnp.int32, (1, T), 1).astype(f32)
    if rev:
        keep, dist = diff < 0, jnp.maximum(-diff, 0.0)
        xi, zeta = jnp.exp(ld * (T - pos)), jnp.exp(ld * pos)
    else:
        keep, dist = diff >= 0, jnp.maximum(diff, 0.0)
        xi, zeta = jnp.exp(ld * (pos + 1.0)), jnp.exp(ld * (T - 1.0 - pos))
    return jnp.where(keep, jnp.exp(ld * dist), 0.0), xi, zeta, jnp.exp(ld * float(T))


def _ret_apply(qr, kr, v, rp, dm, xi, zeta, cdec):
    out = dnn(dnt(qr, kr) * dm, v) + dnn(qr * xi, rp)
    return out, cdec * rp + dtn(kr * zeta, v)


def _ret_fwd(p_ext, ld8_f, ld8_b, nctx, name, cargo=None):
    n = p_ext.shape[0]
    nch = n // T
    idx_b = _idx_rev(nctx, nch)
    cg = _Cargo(cargo)

    def body(*refs):
        ins, (of_ref, ob_ref, rpf_ref, rpb_ref), (rf_s, rb_s, dm_s, xz_s) = cg.split(refs, 8, 4, 4)
        qf, kf, vf, qb, kb, vb, ldf_ref, ldb_ref = ins
        cg.ride(refs, 8, 4, nch)

        @pl.when(pl.program_id(0) == 0)
        def _():
            rf_s[...] = jnp.zeros_like(rf_s)
            rb_s[...] = jnp.zeros_like(rb_s)
            for d, ld_ref in enumerate((ldf_ref, ldb_ref)):
                for h in range(RH):
                    dm, xi, zeta, cdec = _ret_tables(ld_ref[h:h + 1, 0:1], bool(d))
                    dm_s[d, h] = dm
                    xz_s[d, h, 0] = jnp.broadcast_to(xi, (T, DH))
                    xz_s[d, h, 1] = jnp.broadcast_to(zeta, (T, DH))
                    xz_s[d, h, 2] = jnp.broadcast_to(cdec, (T, DH))

        for h in range(RH):
            sl = slice(h * DH, (h + 1) * DH)
            for d, (q_ref, k_ref, v_ref, o_ref, rp_ref, r_s) in enumerate(((qf, kf, vf, of_ref, rpf_ref, rf_s),
                                                                            (qb, kb, vb, ob_ref, rpb_ref, rb_s))):
                rp = r_s[h]
                rp_ref[0, h] = rp
                out, rn = _ret_apply(q_ref[:, sl].astype(f32), k_ref[:, sl].astype(f32), v_ref[:, sl].astype(f32), rp,
                                     dm_s[d, h], xz_s[d, h, 0], xz_s[d, h, 1], xz_s[d, h, 2])
                r_s[h] = rn
                o_ref[:, sl] = out

    fcol = lambda cb: pl.BlockSpec((T, RW), lambda i, _c=cb: (i, _c))
    bcol = lambda cb: pl.BlockSpec((T, RW), lambda i, _c=cb: (idx_b(i), _c))
    rspec = pl.BlockSpec((1, RH, DH, DH), lambda i: (i, 0, 0, 0))
    oshape, rshape = jax.ShapeDtypeStruct((n, RW), f32), jax.ShapeDtypeStruct((nch, RH, DH, DH), f32)
    return pl.pallas_call(
        body, name=name, grid=(nch,),
        in_specs=[fcol(1), fcol(2), fcol(3), bcol(1), bcol(2), bcol(3), _const_spec((8, 128)), _const_spec((8, 128))] + cg.in_specs(),
        out_specs=[fcol(0), bcol(0), rspec, rspec] + cg.in_specs(),
        out_shape=[oshape, oshape, rshape, rshape] + cg.out_shapes(),
        scratch_shapes=[pltpu.VMEM((RH, DH, DH), f32)] * 2 + [pltpu.VMEM((2, RH, T, T), f32), pltpu.VMEM((2, RH, 3, T, DH), f32)] + cg.sems(),
        compiler_params=_CP(dimension_semantics=_ARB),
    )(p_ext, p_ext, p_ext, p_ext, p_ext, p_ext, ld8_f, ld8_b, *cg.arrays)


def _ret_bwd(p_ext, ld8_f, ld8_b, rp_f, rp_b, do_ext, nctx, name, cargo=None):
    n = p_ext.shape[0]
    nch = n // T
    idx_rev = _idx_rev(nctx, nch)
    idf = lambda j: nch - 1 - j
    idb = lambda j: idx_rev(nch - 1 - j)
    cg = _Cargo(cargo)

    def body(*refs):
        ins, outs, (drf_s, drb_s, dm_s, xz_s, gdm_s, gxz_s) = cg.split(refs, 12, 8, 6)
        qf, kf, vf, qb, kb, vb, ldf_ref, ldb_ref, rpf_ref, rpb_ref, dof_ref, dob_ref = ins
        dqf, dkf, dvf, dqb, dkb, dvb, dldf_ref, dldb_ref = outs
        cg.ride(refs, 12, 8, nch)
        lds = (ldf_ref, ldb_ref)

        @pl.when(pl.program_id(0) == 0)
        def _():
            for r in (drf_s, drb_s, gdm_s, gxz_s):
                r[...] = jnp.zeros_like(r)
            for d in range(2):
                for h in range(RH):
                    dm, xi, zeta, cdec = _ret_tables(lds[d][h:h + 1, 0:1], bool(d))
                    dm_s[d, h] = dm
                    for k, tab in enumerate((xi, zeta, cdec)):
                        xz_s[d, h, k] = jnp.broadcast_to(tab, (T, DH))

        for h in range(RH):
            sl = slice(h * DH, (h + 1) * DH)
            for d, (q_ref, k_ref, v_ref, rp_ref, do_ref, dq_ref, dk_ref, dv_ref, dr_s) in enumerate((
                    (qf, kf, vf, rpf_ref, dof_ref, dqf, dkf, dvf, drf_s), (qb, kb, vb, rpb_ref, dob_ref, dqb, dkb, dvb, drb_s))):
                _, vjp = jax.vjp(_ret_apply, q_ref[:, sl].astype(f32), k_ref[:, sl].astype(f32), v_ref[:, sl].astype(f32),
                                 rp_ref[0, h], dm_s[d, h], xz_s[d, h, 0], xz_s[d, h, 1], xz_s[d, h, 2])
                dqr, dkr, dv, drp, gdm, gxi, gzeta, gcdec = vjp((do_ref[:, sl], dr_s[h]))
                dr_s[h] = drp
                dq_ref[:, sl], dk_ref[:, sl], dv_ref[:, sl] = dqr, dkr, dv
                gdm_s[d, h] += gdm
                for k, g in enumerate((gxi, gzeta, gcdec)):
                    gxz_s[d, h, k] += g

        @pl.when(pl.program_id(0) == nch - 1)
        def _():
            for d, dld_ref in enumerate((dldf_ref, dldb_ref)):
                dld_ref[...] = jnp.zeros_like(dld_ref)
                for h in range(RH):
                    _, vjp = jax.vjp(functools.partial(_ret_tables, rev=bool(d)), lds[d][h:h + 1, 0:1])
                    lanes = lambda a: jnp.sum(a, axis=1, keepdims=True)
                    (dld,) = vjp((gdm_s[d, h], lanes(gxz_s[d, h, 0]), lanes(gxz_s[d, h, 1]),
                                  jnp.sum(lanes(gxz_s[d, h, 2]), axis=0, keepdims=True)))
                    dld_ref[h:h + 1, :] = jnp.broadcast_to(dld, (1, 128))

    fcol = lambda cb: pl.BlockSpec((T, RW), lambda j, _c=cb: (idf(j), _c))
    bcol = lambda cb: pl.BlockSpec((T, RW), lambda j, _c=cb: (idb(j), _c))
    rspec = pl.BlockSpec((1, RH, DH, DH), lambda j: (nch - 1 - j, 0, 0, 0))
    oshape = jax.ShapeDtypeStruct((n, RW), f32)
    return pl.pallas_call(
        body, name=name, grid=(nch,),
        in_specs=[fcol(1), fcol(2), fcol(3), bcol(1), bcol(2), bcol(3), _const_spec((8, 128)), _const_spec((8, 128)), rspec, rspec,
                  fcol(0), bcol(0)] + cg.in_specs(),
        out_specs=[fcol(0), fcol(0), fcol(0), bcol(0), bcol(0), bcol(0), _acc_spec((8, 128)), _acc_spec((8, 128))] + cg.in_specs(),
        out_shape=[oshape] * 6 + [jax.ShapeDtypeStruct((8, 128), f32)] * 2 + cg.out_shapes(),
        scratch_shapes=[pltpu.VMEM((RH, DH, DH), f32)] * 2 + [pltpu.VMEM((2, RH, T, T), f32), pltpu.VMEM((2, RH, 3, T, DH), f32)] * 2
        + cg.sems(),
        compiler_params=_CP(dimension_semantics=_ARB),
    )(p_ext, p_ext, p_ext, p_ext, p_ext, p_ext, ld8_f, ld8_b, rp_f, rp_b, do_ext, do_ext, *cg.arrays)


def _qk_heads(p, fn_q, fn_k):
    heads = lambda base, fn: [fn(p[:, base + h * DH:base + (h + 1) * DH]) for h in range(RH)]
    return jnp.concatenate([p[:, :S5W]] + heads(S5W, fn_q) + heads(S5W + RW, fn_k) + [p[:, S5W + 2 * RW:]], axis=1)


def _f1_fwd(x, ctx, modx, modc, nw1, w_in_n, cosf, sins, name, cargo=None):
    L = x.shape[0]
    nb = L // R + 1
    scale = DH ** -0.5
    cg = _Cargo(cargo)

    def body(*refs):
        (x_ref, c_ref, mx_ref, mc_ref, nw_ref, w_ref, cos_ref, sin_ref), (p_ref,), _ = cg.split(refs, 8, 1, 0)
        cg.ride(refs, 8, 1, nb)
        is_ctx = pl.program_id(0) == 0
        xin = jnp.where(is_ctx, c_ref[...], x_ref[...])
        sh = jnp.where(is_ctx, mc_ref[0:1], mx_ref[0:1])
        sc = jnp.where(is_ctx, mc_ref[1:2], mx_ref[1:2])
        cf, ss = cos_ref[...], sin_ref[...]
        p = dnn(_norm_mod(xin, nw_ref[...], sh, sc), w_ref[...])
        p_ref[...] = _qk_heads(p, lambda t: _rot(t, cf, ss), lambda t: _rot(t * scale, cf, ss)).astype(bf16)

    return pl.pallas_call(
        body, name=name, grid=(nb,),
        in_specs=[pl.BlockSpec((R, D), lambda i: (jnp.maximum(i - 1, 0), 0)), _const_spec((R, D)), _const_spec((6, D)),
                  _const_spec((6, D)), _const_spec((1, D)), _const_spec((D, INC)), pl.BlockSpec((R, DH), lambda i: (i, 0)),
                  pl.BlockSpec((R, DH), lambda i: (i, 0))] + cg.in_specs(),
        out_specs=[pl.BlockSpec((R, INC), lambda i: (i, 0))] + cg.in_specs(),
        out_shape=[jax.ShapeDtypeStruct((L + R, INC), bf16)] + cg.out_shapes(),
        scratch_shapes=cg.sems(),
        compiler_params=_CP(dimension_semantics=_ARB),
    )(x, ctx, modx, modc, nw1, w_in_n, cosf, sins, *cg.arrays)


def _f1_bwd(x, ctx, modx, modc, nw1, w_in_t, cosf, sins, dx1, parts, name, cargo=None):
    L = x.shape[0]
    nb = L // R + 1
    scale = DH ** -0.5
    cg = _Cargo(cargo)

    def body(*refs):
        ins, (gx_ref, dp_ref, h1_ref, dnw_ref, dmx_ref, dmc_ref), _ = cg.split(refs, 18, 6, 0)
        x_ref, c_ref, mx_ref, mc_ref, nw_ref, w_ref, cos_ref, sin_ref, dx1_ref, du0, du1, dq0, dq1, dk0, dk1, dv0, dv1, dg0 = ins
        cg.ride(refs, 18, 6, nb)
        i = pl.program_id(0)
        is_ctx = i == 0

        @pl.when(is_ctx)
        def _():
            dnw_ref[...] = jnp.zeros_like(dnw_ref)
            dmx_ref[...] = jnp.zeros_like(dmx_ref)
            dmc_ref[...] = jnp.zeros_like(dmc_ref)

        cf, ss = cos_ref[...], sin_ref[...]
        dp = jnp.concatenate([du0[...].astype(f32) + du1[...], dq0[...] + dq1[...], dk0[...] + dk1[...], dv0[...] + dv1[...],
                              dg0[...]], axis=1)
        dp = _qk_heads(dp, lambda t: _rot_t(t, cf, ss), lambda t: _rot_t(t, cf, ss) * scale).astype(bf16)
        dp_ref[...] = dp
        xin = jnp.where(is_ctx, c_ref[...], x_ref[...])
        sh = jnp.where(is_ctx, mc_ref[0:1], mx_ref[0:1])
        sc = jnp.where(is_ctx, mc_ref[1:2], mx_ref[1:2])
        dh = dnn(dp, w_ref[...])
        h, vjp = jax.vjp(_norm_mod, xin, nw_ref[...], sh, sc)
        dxin, dnw, dsh, dsc = vjp(dh)
        h1_ref[...] = h.astype(bf16)
        gx_ref[...] = dx1_ref[...] + dxin
        dnw_ref[...] += dnw
        wx = jnp.where(is_ctx, 0.0, 1.0)
        dmx_ref[0:1] += dsh * wx
        dmx_ref[1:2] += dsc * wx
        dmc_ref[0:1] += dsh * (1.0 - wx)
        dmc_ref[1:2] += dsc * (1.0 - wx)

    lat = pl.BlockSpec((R, D), lambda i: (jnp.maximum(i - 1, 0), 0))
    ext = pl.BlockSpec((R, S5W), lambda i: (i, 0))
    return pl.pallas_call(
        body, name=name, grid=(nb,),
        in_specs=[lat, _const_spec((R, D)), _const_spec((6, D)), _const_spec((6, D)), _const_spec((1, D)), _const_spec((INC, D)),
                  pl.BlockSpec((R, DH), lambda i: (i, 0)), pl.BlockSpec((R, DH), lambda i: (i, 0)), lat] + [ext] * 9 + cg.in_specs(),
        out_specs=[lat, pl.BlockSpec((R, INC), lambda i: (i, 0)), pl.BlockSpec((R, D), lambda i: (i, 0)),
                   _acc_spec((1, D)), _acc_spec((6, D)), _acc_spec((6, D))] + cg.in_specs(),
        out_shape=[jax.ShapeDtypeStruct((L, D), f32), jax.ShapeDtypeStruct((L + R, INC), bf16),
                   jax.ShapeDtypeStruct((L + R, D), bf16), jax.ShapeDtypeStruct((1, D), f32),
                   jax.ShapeDtypeStruct((6, D), f32), jax.ShapeDtypeStruct((6, D), f32)] + cg.out_shapes(),
        scratch_shapes=cg.sems(),
        compiler_params=_CP(dimension_semantics=_ARB),
    )(x, ctx, modx, modc, nw1, w_in_t, cosf, sins, dx1, *parts, *cg.arrays)


def _ret_post(yr, g):
    outs = []
    for h in range(RH):
        yh = yr[:, h * DH:(h + 1) * DH]
        mu = jnp.mean(yh, axis=-1, keepdims=True)
        var = jnp.mean((yh - mu) ** 2, axis=-1, keepdims=True)
        outs.append((yh - mu) * lax.rsqrt(var + EPS))
    return jax.nn.silu(g) * jnp.concatenate(outs, axis=1)


def _mix_fn(ys, u, of, ob, g, x, dvec, bglu, gate1, pz, pm, wglu, wout):
    s = _gelu(ys + dvec * u)
    z = dnn(s, wglu) + bglu + pz
    cat = jnp.concatenate([s * jax.nn.sigmoid(z), _ret_post(of + ob, g)], axis=1)
    mix = dnn(cat, wout) + pm
    return x + gate1 * mix, (s, cat)


def _mix_fwd(x, ys, of, ob, p_ext, dvec, bglu, modx, wglu, wout, name, cargo=None):
    L = x.shape[0]
    nb = L // R
    cg = _Cargo(cargo)

    def body(*refs):
        ins, (x1_ref,), _ = cg.split(refs, 11, 1, 0)
        x_ref, ys_ref, of_ref, ob_ref, u_ref, g_ref, d_ref, b_ref, mx_ref, wg_ref, wo_ref = ins
        cg.ride(refs, 11, 1, nb)
        x1_ref[...] = _mix_fn(ys_ref[...].astype(f32), u_ref[...].astype(f32), of_ref[...], ob_ref[...], g_ref[...].astype(f32),
                              x_ref[...], d_ref[...], b_ref[...], mx_ref[2:3], 0.0, 0.0, wg_ref[...], wo_ref[...])[0]

    ext = pl.BlockSpec((R, S5W), lambda i: (i + 1, 0))
    return pl.pallas_call(
        body, name=name, grid=(nb,),
        in_specs=[pl.BlockSpec((R, D), lambda i: (i, 0)), ext, ext, ext, ext, pl.BlockSpec((R, RW), lambda i: (i + 1, 4)),
                  _const_spec((1, S5W)), _const_spec((1, S5W)), _const_spec((6, D)), _const_spec((S5W, S5W)), _const_spec((D, D))]
        + cg.in_specs(),
        out_specs=[pl.BlockSpec((R, D), lambda i: (i, 0))] + cg.in_specs(),
        out_shape=[jax.ShapeDtypeStruct((L, D), f32)] + cg.out_shapes(),
        scratch_shapes=cg.sems(),
        compiler_params=_CP(dimension_semantics=_ARB),
    )(x, ys, of, ob, p_ext, p_ext, dvec, bglu, modx, wglu, wout, *cg.arrays)


def _mix_bwd(x, ys, of, ob, p_ext, dvec, bglu, modx, wglu, wout, dx1, name, cargo=None):
    L = x.shape[0]
    nb = L // R + 1
    cg = _Cargo(cargo)

    def body(*refs):
        ins, outs, _ = cg.split(refs, 12, 11, 0)
        x_ref, ys_ref, of_ref, ob_ref, u_ref, g_ref, d_ref, b_ref, mx_ref, wg_ref, wo_ref, dx1_ref = ins
        dy_ref, dud_ref, do_ref, dg_ref, cat_ref, dmix_ref, s_ref, dz_ref, dd_ref, db_ref, dg1_ref = outs
        cg.ride(refs, 12, 11, nb)
        i = pl.program_id(0)

        @pl.when(i == 0)
        def _():
            for r in outs:
                r[...] = jnp.zeros_like(r)

        @pl.when(i > 0)
        def _():
            fn = lambda ys_, u_, of_, g_, d_, b_, g1_, pz_, pm_: _mix_fn(
                ys_, u_, of_, ob_ref[...], g_, x_ref[...], d_, b_, g1_, pz_, pm_, wg_ref[...], wo_ref[...])
            _, vjp, (s, cat) = jax.vjp(fn, ys_ref[...].astype(f32), u_ref[...].astype(f32), of_ref[...], g_ref[...].astype(f32), d_ref[...],
                                       b_ref[...], mx_ref[2:3], jnp.zeros((R, S5W), f32), jnp.zeros((R, D), f32), has_aux=True)
            dy, dud, do, dg, dd, db, dg1, dz, dmix = vjp(dx1_ref[...])
            dy_ref[...], dud_ref[...], do_ref[...], dg_ref[...] = dy.astype(bf16), dud, do, dg
            cat_ref[...], dmix_ref[...] = cat.astype(bf16), dmix.astype(bf16)
            s_ref[...], dz_ref[...] = s.astype(bf16), dz.astype(bf16)
            dd_ref[...] += dd
            db_ref[...] += db
            dg1_ref[...] += dg1

    lat = pl.BlockSpec((R, D), lambda i: (jnp.maximum(i - 1, 0), 0))
    lat5 = pl.BlockSpec((R, S5W), lambda i: (jnp.maximum(i - 1, 0), 0))
    ext = pl.BlockSpec((R, S5W), lambda i: (i, 0))
    eshape = jax.ShapeDtypeStruct((L + R, S5W), f32)
    return pl.pallas_call(
        body, name=name, grid=(nb,),
        in_specs=[lat, ext, ext, ext, ext, pl.BlockSpec((R, RW), lambda i: (i, 4)),
                  _const_spec((1, S5W)), _const_spec((1, S5W)), _const_spec((6, D)), _const_spec((S5W, S5W)), _const_spec((D, D)), lat]
        + cg.in_specs(),
        out_specs=[ext, ext, ext, ext, lat, lat, lat5, lat5, _acc_spec((1, S5W)), _acc_spec((1, S5W)), _acc_spec((1, D))]
        + cg.in_specs(),
        out_shape=[jax.ShapeDtypeStruct((L + R, S5W), bf16), eshape, eshape, eshape, jax.ShapeDtypeStruct((L, D), bf16),
                   jax.ShapeDtypeStruct((L, D), bf16), jax.ShapeDtypeStruct((L, S5W), bf16), jax.ShapeDtypeStruct((L, S5W), bf16),
                   jax.ShapeDtypeStruct((1, S5W), f32), jax.ShapeDtypeStruct((1, S5W), f32), jax.ShapeDtypeStruct((1, D), f32)]
        + cg.out_shapes(),
        scratch_shapes=cg.sems(),
        compiler_params=_CP(dimension_semantics=_ARB),
    )(x, ys, of, ob, p_ext, p_ext, dvec, bglu, modx, wglu, wout, dx1, *cg.arrays)


def _ffn_tail(gc, a, x1, gate2, fnw, pf, wdown, wdown_t, tgt):
    f = _gelu(gc) * a
    ffn = _dnn_const(f, wdown, wdown_t) + pf
    y = _rms(x1 + gate2 * ffn, fnw)
    err = y - tgt
    loss = 0.5 * jnp.sum(jnp.mean(err * err, axis=-1, keepdims=True), axis=0, keepdims=True)
    return loss, f


def _ffn_fwd(x1, tgt, nw2, modx, w_a, w_g, cw, cb, wdown, wdown_t, fnw, name):
    L = x1.shape[0]
    nb = L // RF
    per = RF // HALO

    def body(x_ref, xp_ref, xn_ref, t_ref, nw_ref, mx_ref, wa_ref, wg_ref, cw_ref, cb_ref, wd_ref, wdt_ref, fn_ref,
             dx2_ref, da_ref, dgc_ref, f_ref, dffn_ref, loss_ref, dfn_ref, dg2_ref, dcb_ref, dcw_ref):
        i = pl.program_id(0)

        @pl.when(i == 0)
        def _():
            for r in (loss_ref, dfn_ref, dg2_ref, dcb_ref, dcw_ref):
                r[...] = jnp.zeros_like(r)

        nw, sh, sc, gate2 = nw_ref[...], mx_ref[3:4], mx_ref[4:5], mx_ref[5:6]
        x1b = x_ref[...]
        h2 = _norm_mod(x1b, nw, sh, sc)
        h2e = jnp.concatenate([_norm_mod(xp_ref[...], nw, sh, sc), h2, _norm_mod(xn_ref[...], nw, sh, sc)], axis=0)
        a = dnn(h2, wa_ref[...])
        ge = dnn(h2e, wg_ref[...])
        g = ge[HALO:HALO + RF]
        gp = ge[HALO - 1:HALO] * jnp.where(i > 0, 1.0, 0.0)
        gn = ge[HALO + RF:HALO + RF + 1] * jnp.where(i < nb - 1, 1.0, 0.0)
        row = lax.broadcasted_iota(jnp.int32, (RF, 1), 0)
        g_prev = jnp.where(row == 0, gp, pltpu.roll(g, 1, axis=0))
        g_next = jnp.where(row == RF - 1, gn, pltpu.roll(g, RF - 1, axis=0))
        gc = cb_ref[...] + g_prev * cw_ref[0:1] + g * cw_ref[1:2] + g_next * cw_ref[2:3]
        fn = lambda gc_, a_, x_, g2_, fw_, pf_: _ffn_tail(gc_, a_, x_, g2_, fw_, pf_, wd_ref[...], wdt_ref[...], t_ref[...])
        loss, vjp, f = jax.vjp(fn, gc, a, x1b, gate2, fn_ref[...], jnp.zeros((RF, D), f32), has_aux=True)
        dgc, da, dx2, dg2, dfw, dffn = vjp(jnp.ones((1, 1), f32))
        dx2_ref[...] = dx2
        da_ref[...], dgc_ref[...] = da.astype(bf16), dgc
        f_ref[...], dffn_ref[...] = f.astype(bf16), dffn.astype(bf16)
        loss_ref[...] += jnp.broadcast_to(loss, (1, 128))
        dfn_ref[...] += dfw
        dg2_ref[...] += dg2
        dcb_ref[...] += jnp.sum(dgc, axis=0, keepdims=True)
        dcw_ref[0:1] += jnp.sum(dgc * g_prev, axis=0, keepdims=True)
        dcw_ref[1:2] += jnp.sum(dgc * g, axis=0, keepdims=True)
        dcw_ref[2:3] += jnp.sum(dgc * g_next, axis=0, keepdims=True)

    blk = lambda w: pl.BlockSpec((RF, w), lambda i: (i, 0))
    return pl.pallas_call(
        body, name=name, grid=(nb,),
        in_specs=[blk(D), pl.BlockSpec((HALO, D), lambda i: (jnp.maximum(i * per - 1, 0), 0)),
                  pl.BlockSpec((HALO, D), lambda i: (jnp.minimum((i + 1) * per, L // HALO - 1), 0)), blk(D),
                  _const_spec((1, D)), _const_spec((6, D)), _const_spec((D, DFF)), _const_spec((D, DFF)), _const_spec((3, DFF)),
                  _const_spec((1, DFF)), _const_spec((DFF, D)), _const_spec((D, DFF)), _const_spec((1, D))],
        out_specs=[blk(D), blk(DFF), blk(DFF), blk(DFF), blk(D), _acc_spec((1, 128)), _acc_spec((1, D)), _acc_spec((1, D)),
                   _acc_spec((1, DFF)), _acc_spec((3, DFF))],
        out_shape=[jax.ShapeDtypeStruct((L, D), f32), jax.ShapeDtypeStruct((L, DFF), bf16), jax.ShapeDtypeStruct((L, DFF), f32),
                   jax.ShapeDtypeStruct((L, DFF), bf16), jax.ShapeDtypeStruct((L, D), bf16), jax.ShapeDtypeStruct((1, 128), f32),
                   jax.ShapeDtypeStruct((1, D), f32), jax.ShapeDtypeStruct((1, D), f32), jax.ShapeDtypeStruct((1, DFF), f32),
                   jax.ShapeDtypeStruct((3, DFF), f32)],
        compiler_params=_CP(dimension_semantics=_ARB),
    )(x1, x1, x1, tgt, nw2, modx, w_a, w_g, cw, cb, wdown, wdown_t, fnw)


def _ffn_bwd(x1, dx2, da, dgc, nw2, modx, wup_t, cw, name):
    L = x1.shape[0]
    rf = RFB
    nb = L // rf
    per = rf // HALO

    def body(x_ref, dx2_ref, da_ref, dgc_ref, dgp_ref, dgn_ref, nw_ref, mx_ref, wu_ref, cw_ref,
             dx1_ref, dag_ref, h2_ref, dnw_ref, dmx_ref):
        i = pl.program_id(0)

        @pl.when(i == 0)
        def _():
            dnw_ref[...] = jnp.zeros_like(dnw_ref)
            dmx_ref[...] = jnp.zeros_like(dmx_ref)

        dgc_b = dgc_ref[...]
        before = dgp_ref[HALO - 1:HALO] * jnp.where(i > 0, 1.0, 0.0)
        after = dgn_ref[0:1] * jnp.where(i < nb - 1, 1.0, 0.0)
        row = lax.broadcasted_iota(jnp.int32, (rf, 1), 0)
        d_prev = jnp.where(row == 0, before, pltpu.roll(dgc_b, 1, axis=0))
        d_next = jnp.where(row == rf - 1, after, pltpu.roll(dgc_b, rf - 1, axis=0))
        dg = cw_ref[0:1] * d_next + cw_ref[1:2] * dgc_b + cw_ref[2:3] * d_prev
        dag = jnp.concatenate([da_ref[...], dg.astype(bf16)], axis=1)
        dag_ref[...] = dag
        dh2 = dnn(dag, wu_ref[...])
        h2, vjp = jax.vjp(_norm_mod, x_ref[...], nw_ref[...], mx_ref[3:4], mx_ref[4:5])
        dxa, dnw, dsh, dsc = vjp(dh2)
        h2_ref[...] = h2.astype(bf16)
        dx1_ref[...] = dx2_ref[...] + dxa
        dnw_ref[...] += dnw
        dmx_ref[3:4] += dsh
        dmx_ref[4:5] += dsc

    blk = lambda w: pl.BlockSpec((rf, w), lambda i: (i, 0))
    return pl.pallas_call(
        body, name=name, grid=(nb,),
        in_specs=[blk(D), blk(D), blk(DFF), blk(DFF), pl.BlockSpec((HALO, DFF), lambda i: (jnp.maximum(i * per - 1, 0), 0)),
                  pl.BlockSpec((HALO, DFF), lambda i: (jnp.minimum((i + 1) * per, L // HALO - 1), 0)),
                  _const_spec((1, D)), _const_spec((6, D)), _const_spec((2 * DFF, D)), _const_spec((3, DFF))],
        out_specs=[blk(D), blk(2 * DFF), blk(D), _acc_spec((1, D)), _acc_spec((6, D))],
        out_shape=[jax.ShapeDtypeStruct((L, D), f32), jax.ShapeDtypeStruct((L, 2 * DFF), bf16), jax.ShapeDtypeStruct((L, D), bf16),
                   jax.ShapeDtypeStruct((1, D), f32), jax.ShapeDtypeStruct((6, D), f32)],
        compiler_params=_CP(dimension_semantics=_ARB),
    )(x1, dx2, da, dgc, dgc, dgc, nw2, modx, wup_t, cw)


def _matmul_tn(a, b, name, cargo=None):
    k, m = a.shape
    n = b.shape[1]
    divs = lambda d: [c for c in range(d, 0, -128) if d % c == 0]
    _, tm, tn = min((m * (n // cn) + n * (m // cm), cm, cn) for cm in divs(m) for cn in divs(n) if cm * cn * 4 <= ACC_TILE_BYTES)
    tk = next(c for c in (1024, 768, 512, 256, 128) if k % c == 0)
    nk = k // tk
    grid = (m // tm, n // tn, nk)
    cg = _Cargo(cargo)

    def body(*refs):
        (a_ref, b_ref), (o_ref,), (acc,) = cg.split(refs, 2, 1, 1)
        cg.ride(refs, 2, 1, grid)
        q = pl.program_id(2)

        @pl.when(q == 0)
        def _():
            acc[...] = jnp.zeros_like(acc)

        acc[...] += dtn(a_ref[...], b_ref[...])

        @pl.when(q == nk - 1)
        def _():
            o_ref[...] = acc[...].astype(bf16)

    out = pl.pallas_call(
        body, name=name, grid=grid,
        in_specs=[pl.BlockSpec((tk, tm), lambda i, j, q: (q, i)), pl.BlockSpec((tk, tn), lambda i, j, q: (q, j))] + cg.in_specs(),
        out_specs=[pl.BlockSpec((tm, tn), lambda i, j, q: (i, j))] + cg.in_specs(),
        out_shape=[jax.ShapeDtypeStruct((m, n), bf16)] + cg.out_shapes(),
        scratch_shapes=[pltpu.VMEM((tm, tn), f32)] + cg.sems(),
        compiler_params=_CP(dimension_semantics=("arbitrary",) * 3 if cg.n else ("parallel", "parallel", "arbitrary")),
    )(a, b, *cg.arrays)
    return out if cg.n else out[0]


def _adamw_refs(w_ref, g_ref, m_ref, v_ref, d_ref, nm_ref, nv_ref):
    c1, c2 = 1.0 - B1 ** STEP, 1.0 - B2 ** STEP
    gg = g_ref[...]
    nm = B1 * m_ref[...] + (1.0 - B1) * gg
    nv = B2 * v_ref[...] + (1.0 - B2) * jnp.square(gg)
    d_ref[...] = -LR * ((nm / c1) / (jnp.sqrt(nv / c2) + AEPS) + WD * w_ref[...])
    nm_ref[...], nv_ref[...] = nm, nv


def _adamw(w, g, m, v, name):
    def body(*refs):
        _adamw_refs(*refs)

    return pl.pallas_call(body, name=name, out_shape=[jax.ShapeDtypeStruct(w.shape, f32)] * 3, compiler_params=_CP())(w, g, m, v)


def _adamw_landed(land, w, m, v, name):
    def body(l_ref, w_ref, m_ref, v_ref, g_ref, d_ref, nm_ref, nv_ref):
        acc = l_ref[0].astype(f32)
        for j in range(1, NDEV):
            acc = acc + l_ref[j].astype(f32)
        g_ref[...] = acc
        _adamw_refs(w_ref, g_ref, m_ref, v_ref, d_ref, nm_ref, nv_ref)

    return pl.pallas_call(body, name=name, out_shape=[jax.ShapeDtypeStruct(w.shape, f32)] * 4, compiler_params=_CP())(land, w, m, v)


def _adamw_many(ws, gs, ms, vs, name):
    n = len(ws)

    def body(*refs):
        for k in range(n):
            _adamw_refs(*[refs[j * n + k] for j in range(7)])

    outs = pl.pallas_call(body, name=name, out_shape=[jax.ShapeDtypeStruct(w.shape, f32) for w in ws] * 3,
                          compiler_params=_CP())(*ws, *gs, *ms, *vs)
    return outs[:n], outs[n:2 * n], outs[2 * n:]


SMALL = ["conv_w", "c_ctx", "norm1_w", "s5_lambda_re_f", "s5_lambda_im_f", "s5_log_step_f", "s5_lambda_re_b", "s5_lambda_im_b",
         "s5_log_step_b", "s5_b_re", "s5_b_im", "s5_c_re", "s5_c_im", "s5_d", "s5_b_glu", "ret_log_decay_f", "ret_log_decay_b",
         "norm2_w", "conv_b", "final_norm_w"]
WEIGHTS = ["c_ctx", "w_mod", "b_mod", "norm1_w", "w_in", "s5_lambda_re_f", "s5_lambda_im_f", "s5_log_step_f", "s5_lambda_re_b",
           "s5_lambda_im_b", "s5_log_step_b", "s5_b_re", "s5_b_im", "s5_c_re", "s5_c_im", "s5_d", "s5_w_glu", "s5_b_glu",
           "ret_log_decay_f", "ret_log_decay_b", "w_out", "norm2_w", "w_up", "conv_w", "conv_b", "w_down", "final_norm_w"]


def _pack_small(vals):
    flat, offs, o = [], [], 0
    for a in vals:
        n = a.size
        npad = -n % 128
        flat.append(jnp.pad(a.reshape(-1), (0, npad)))
        offs.append((o, n))
        o += n + npad
    tail = -o % 1024
    if tail:
        flat.append(jnp.zeros((tail,), f32))
    return jnp.concatenate(flat).reshape(-1, 128), offs


def _unpack_small(packed, offs, shapes):
    flat = packed.reshape(-1)
    return [flat[o:o + n].reshape(s) for (o, n), s in zip(offs, shapes)]


def _rope_tables(L, nctx_rows):
    t = np.arange(L)
    inv = (ROPE_THETA ** (-np.arange(DH // 4, dtype=np.float64) / (DH // 4))).astype(np.float32)
    ang = np.concatenate([(t // GRID_W).astype(np.float32)[:, None] * inv, (t % GRID_W).astype(np.float32)[:, None] * inv], axis=-1)
    cos = np.repeat(np.cos(ang).astype(np.float32), 2, axis=1)
    sin = np.repeat(np.sin(ang).astype(np.float32), 2, axis=1) * np.tile(np.array([-1.0, 1.0], np.float32), DH // 2)
    cosf = np.concatenate([np.ones((nctx_rows, DH), np.float32), cos], axis=0)
    sins = np.concatenate([np.zeros((nctx_rows, DH), np.float32), sin], axis=0)
    return jnp.asarray(cosf), jnp.asarray(sins)


def kernel(x, c, ctx, c_ctx, w_mod, b_mod, norm1_w, w_in, s5_lambda_re_f, s5_lambda_im_f, s5_log_step_f, s5_lambda_re_b, s5_lambda_im_b, s5_log_step_b, s5_b_re, s5_b_im, s5_c_re, s5_c_im, s5_d, s5_w_glu, s5_b_glu, ret_log_decay_f, ret_log_decay_b, w_out, norm2_w, w_up, conv_w, conv_b, w_down, final_norm_w, loss_target, m_c_ctx, m_w_mod, m_b_mod, m_norm1_w, m_w_in, m_s5_lambda_re_f, m_s5_lambda_im_f, m_s5_log_step_f, m_s5_lambda_re_b, m_s5_lambda_im_b, m_s5_log_step_b, m_s5_b_re, m_s5_b_im, m_s5_c_re, m_s5_c_im, m_s5_d, m_s5_w_glu, m_s5_b_glu, m_ret_log_decay_f, m_ret_log_decay_b, m_w_out, m_norm2_w, m_w_up, m_conv_w, m_conv_b, m_w_down, m_final_norm_w, v_c_ctx, v_w_mod, v_b_mod, v_norm1_w, v_w_in, v_s5_lambda_re_f, v_s5_lambda_im_f, v_s5_log_step_f, v_s5_lambda_re_b, v_s5_lambda_im_b, v_s5_log_step_b, v_s5_b_re, v_s5_b_im, v_s5_c_re, v_s5_c_im, v_s5_d, v_s5_w_glu, v_s5_b_glu, v_ret_log_decay_f, v_ret_log_decay_b, v_w_out, v_norm2_w, v_w_up, v_conv_w, v_conv_b, v_w_down, v_final_norm_w):
    args = dict(locals())
    W = {n: args[n] for n in WEIGHTS}
    M = {n: args["m_" + n] for n in WEIGHTS}
    V = {n: args["v_" + n] for n in WEIGHTS}
    me = _me()
    x2, ctx2, tgt = x[0], ctx[0], loss_target[0]
    L, Lc = x2.shape[0], ctx2.shape[0]
    assert Lc == R and L % R == 0 and L % GRID_W == 0
    nctx = Lc // T

    w_in_tl, w_up_tl = w_in[0].T.astype(bf16), w_up[0].T.astype(bf16)
    w_out_l, w_down_l, w_glu_l = w_out[0].astype(bf16), w_down[0].astype(bf16), s5_w_glu[0].astype(bf16)
    per_cv = conv_w.shape[2]
    conv_pad = jnp.pad(conv_w[0], ((0, 5), (0, 128 * 3 - per_cv)))
    w_in_g, c_g, conv_g = _gather_two_level([w_in_tl, jnp.pad(c, ((0, 7), (0, 0))), conv_pad], "gather_w_in")
    w_in_t = w_in_g.reshape(INC, D)
    conv_f = conv_g[:, :3, :per_cv].transpose(1, 0, 2).reshape(3, DFF)

    c9 = jnp.concatenate([c_g[:, 0, :], c_ctx[None], jnp.zeros((7, D), f32)], axis=0)
    w_mod_l = w_mod[0]
    ncol = w_mod_l.shape[1]
    m_part = _ada_fwd(c9, w_mod_l, "ada_fwd")
    m_all = _all_gather_small(m_part, "gather_mod").transpose(1, 0, 2).reshape(16, 6, D)
    modx, modc = _mod_select(m_all, b_mod.reshape(6, D), "mod_select")

    pair = lambda a, b: jnp.concatenate([a, b], axis=-1)
    bre_g, bim_g = s5_b_re[0].transpose(0, 2, 1), s5_b_im[0].transpose(0, 2, 1)
    cre_g, cim_g = s5_c_re[0], s5_c_im[0]
    shared = (pair(bre_g, bim_g), pair(bim_g, bre_g), pair(cre_g, cim_g), pair(cim_g, cre_g))
    s5p = {}
    for tag, lre, lim, ls in (("f", s5_lambda_re_f, s5_lambda_im_f, s5_log_step_f), ("b", s5_lambda_re_b, s5_lambda_im_b, s5_log_step_b)):
        s5p[tag] = (pair(lre[0], lre[0])[:, None, :], pair(lim[0], lim[0])[:, None, :], ls[0].reshape(S5G, 1, 1)) + shared
    m_f, mb_f, mc_f, a1_f, a2_f = _s5_build(s5p["f"], False, "s5_build_f")
    m_b, mb_b, mc_b, a1_b, a2_b = _s5_build(s5p["b"], True, "s5_build_b")
    a1_f, a2_f, a1_b, a2_b = (a.reshape(S5G, SB) for a in (a1_f, a2_f, a1_b, a2_b))

    nw1, nw2, fnw = norm1_w, norm2_w, final_norm_w[None]
    cosf, sins = _rope_tables(L, Lc)
    p_ext, w_out_g, w_glu_g, w_up_g1 = _f1_fwd(x2, ctx2, modx, modc, nw1, w_in_t.T, cosf, sins, "f1_fwd",
                                               cargo=([w_out_l, w_glu_l, w_up_tl[:UP_HEAD]], False))
    nctx5 = Lc // TC
    u_g = _to_groups(p_ext[:, :S5W])
    s_f, s_b = _s5_inc(u_g, mb_f, mb_b, "s5_inc")
    hp_f, hp_b = _s5_carry(s_f, s_b, (a1_f, a2_f), (a1_b, a2_b), nctx5, "s5_carry")
    ys = _from_groups(_s5_out(u_g, m_f, m_b, hp_f, hp_b, mc_f, mc_b, "s5_out"))
    ld8 = lambda ld: jnp.pad(jnp.broadcast_to(ld[0][:, None], (RH, 128)), ((0, 8 - RH), (0, 0)))
    ldf8, ldb8 = ld8(ret_log_decay_f), ld8(ret_log_decay_b)
    of, ob, rp_f, rp_b, w_up_g2 = _ret_fwd(p_ext, ldf8, ldb8, nctx, "ret_fwd", cargo=([w_up_tl[UP_HEAD:]], False))
    w_out_f, w_glu_f = w_out_g.reshape(D, D), w_glu_g.reshape(S5W, S5W)
    x1, w_down_g = _mix_fwd(x2, ys, of, ob, p_ext, s5_d, s5_b_glu, modx, w_glu_f, w_out_f, "mix_fwd", cargo=([w_down_l], False))
    w_down_f = w_down_g.reshape(DFF, D)
    w_up_t = jnp.concatenate([w_up_g1, w_up_g2], axis=1).reshape(2 * DFF, D)

    (dx2, da, dgc, f_act, dffn, loss_acc, g_fnw, g_gate2, g_cb, g_cw) = _ffn_fwd(
        x1, tgt, nw2, modx, w_up_t[:DFF].T, w_up_t[DFF:].T, conv_f, conv_b, w_down_f, w_down_f.T, fnw, "ffn_fwd")
    dx1, dag, h2, g_nw2, dmx2 = _ffn_bwd(x1, dx2, da, dgc, nw2, modx, w_up_t, conv_f, "ffn_bwd")
    gw_down = _matmul_tn(f_act, dffn, "dw_down").reshape(NDEV, -1, D)
    gw_up_t = _matmul_tn(dag, h2, "dw_up").reshape(NDEV, -1, D)
    (dy_e, dud_e, do_e, dg_e, cat, dmix, s_act, dz, g_d, g_bglu, g_gate1, l_down) = _mix_bwd(
        x2, ys, of, ob, p_ext, s5_d, s5_b_glu, modx, w_glu_f, w_out_f, dx1, "mix_bwd", cargo=([gw_down], True))
    gw_out = _matmul_tn(cat, dmix, "dw_out").reshape(NDEV, -1, D)
    gw_glu = _matmul_tn(s_act, dz, "dw_glu").reshape(NDEV, -1, S5W)
    dq_f, dk_f, dv_f, dq_b, dk_b, dv_b, gld_f, gld_b, l_up, l_out, l_glu = _ret_bwd(
        p_ext, ldf8, ldb8, rp_f, rp_b, do_e, nctx, "ret_bwd", cargo=([gw_up_t, gw_out, gw_glu], True))

    du1, g_m, dhp_f, dhp_b, dmc_f, dmc_b = _s5_out_bwd(_to_groups(dy_e), u_g, m_f, m_b, hp_f, hp_b, mc_f, mc_b, "s5_out_bwd")
    ds_f, da1_f, da2_f = _s5_carry_bwd(dhp_f, hp_f, a1_f, a2_f, False, nctx5, "s5_carry_bwd_f")
    ds_b, da1_b, da2_b = _s5_carry_bwd(dhp_b, hp_b, a1_b, a2_b, True, nctx5, "s5_carry_bwd_b")
    du_g, dmb_f, dmb_b = _s5_inc_bwd(du1, u_g, ds_f, ds_b, mb_f, mb_b, "s5_inc_bwd")
    zero_p = jnp.zeros((S5G, S5P, SB), f32)
    gf = _s5_build_bwd(s5p["f"], (g_m, dmb_f, dmc_f, da1_f[:, None, :], da2_f[:, None, :]), (zero_p, zero_p), False, "s5_build_bwd_f")
    gb = _s5_build_bwd(s5p["b"], (g_m, dmb_b, dmc_b, da1_b[:, None, :], da2_b[:, None, :]), (gf[3], gf[4]), True, "s5_build_bwd_b")
    g_bre, g_bim = gb[3][:, :, :S5N].transpose(0, 2, 1), gb[3][:, :, S5N:].transpose(0, 2, 1)
    g_cre, g_cim = gb[4][:, :, :S5N], gb[4][:, :, S5N:]

    early = {
        "conv_w": g_cw, "s5_lambda_re_f": gf[0][:, 0, :S5N], "s5_lambda_im_f": gf[1][:, 0, :S5N],
        "s5_log_step_f": gf[2], "s5_lambda_re_b": gb[0][:, 0, :S5N], "s5_lambda_im_b": gb[1][:, 0, :S5N], "s5_log_step_b": gb[2],
        "s5_b_re": g_bre, "s5_b_im": g_bim, "s5_c_re": g_cre, "s5_c_im": g_cim, "s5_d": g_d, "s5_b_glu": g_bglu,
        "ret_log_decay_f": gld_f[:RH, 0], "ret_log_decay_b": gld_b[:RH, 0], "norm2_w": g_nw2, "conv_b": g_cb, "final_norm_w": g_fnw,
    }
    e_names = [n for n in SMALL if n in early]
    packed_e, eoffs = _pack_small([early[n].astype(f32) for n in e_names])
    grad_x, dp_ext, h1, g_nw1, dmx1, dmc1 = _f1_bwd(
        x2, ctx2, modx, modc, nw1, w_in_t, cosf, sins, dx1, (_from_groups(du_g), dud_e, dq_f, dq_b, dk_f, dk_b, dv_f, dv_b, dg_e), "f1_bwd")
    gw_in_t, land_e = _matmul_tn(dp_ext, h1, "dw_in", cargo=([packed_e], False))
    g_in_t = _reduce_scatter_two_level(gw_in_t.reshape(NDEV, -1, D), "scatter_dw_in")

    dmx = dmx1 + dmx2
    dmx = dmx.at[2].set(g_gate1[0]).at[5].set(g_gate2[0])
    dm_me = jnp.stack([dmx.reshape(-1), dmc1.reshape(-1)], axis=0)
    dm_all = _all_gather_small(dm_me.reshape(8, -1), "gather_dmod").reshape(NDEV, 2, 6 * D)
    dmx_all, dmc_all = dm_all[:, 0, :], dm_all[:, 1, :]
    my_cols = lambda a: lax.dynamic_slice(a, (0, me * ncol), (NDEV, ncol))
    gw_mod, g_bmod, dc9 = _ada_bwd(c9, dmx_all, dmc_all, my_cols(dmx_all), my_cols(dmc_all), w_mod_l, "ada_bwd")

    sshape = lambda n: (3, DFF) if n == "conv_w" else W[n].shape
    G = dict(zip(e_names, _unpack_small(_sum8(land_e, "reduce_early"), eoffs, [sshape(n) for n in e_names])))
    late = {"c_ctx": dc9[8], "norm1_w": g_nw1}
    packed_l, loffs = _pack_small([late[n].astype(f32) for n in late])
    G.update(zip(late, _unpack_small(_all_reduce_small(packed_l, "reduce_late"), loffs, [W[n].shape for n in late])))
    G["conv_w"] = lax.dynamic_slice(G["conv_w"], (0, me * per_cv), (3, per_cv))[None]
    G["b_mod"] = g_bmod.reshape(b_mod.shape)
    G["w_mod"] = gw_mod[None]
    G["w_in"] = g_in_t.T[None]
    G["w_up"] = _sum8(l_up, "sum_dw_up").T[None]

    delta, new_m, new_v = {}, {}, {}
    sm_names = SMALL[1:] + ["b_mod"]
    rows = lambda a: a.reshape(-1, a.shape[-1])
    outs = _adamw_many(*[[rows(d[n]) for n in sm_names] for d in (W, G, M, V)], "adamw_small")
    for dst, src in zip((delta, new_m, new_v), outs):
        dst.update({n: a.reshape(W[n].shape) for n, a in zip(sm_names, src)})
    for n in ["w_mod", "w_in", "w_up", "conv_w"]:
        d, nm, nv = _adamw(W[n][0], G[n][0], M[n][0], V[n][0], "adamw_" + n)
        delta[n], new_m[n], new_v[n] = d[None], nm[None], nv[None]
    for n, land in (("w_out", l_out), ("w_down", l_down), ("s5_w_glu", l_glu)):
        g, d, nm, nv = _adamw_landed(land, W[n][0], M[n][0], V[n][0], "adamw_" + n)
        G[n], delta[n], new_m[n], new_v[n] = g[None], d[None], nm[None], nv[None]

    loss = lax.psum(loss_acc[0, 0], ("x", "y", "c"))
    return (loss, grad_x[None], *[G[n] for n in WEIGHTS], *[delta[n] for n in WEIGHTS], *[new_m[n] for n in WEIGHTS],
            *[new_v[n] for n in WEIGHTS])
```

```python
import functools

import numpy as np
import jax
import jax.numpy as jnp
from jax import lax
from jax.experimental import pallas as pl
from jax.experimental.pallas import tpu as pltpu

f32, bf16 = jnp.float32, jnp.bfloat16

D = 1024
S5W, S5G, S5P, S5N = 512, 32, 16, 64
TC = 16
TCP = TC * S5P
SB = 2 * S5N
GBK = 8
UP_HEAD = 192
CARRY_UNROLL = 16
RH, DH = 4, 128
RW = RH * DH
INC = S5W + 4 * RW
DFF = 2816
T = 128
R = 256
RF = 128
RFB = 256
HALO = 8
EPS = 1e-6
ROPE_THETA = 10000.0
GRID_W = 64
NDEV = 8
LR, B1, B2, AEPS, WD, STEP = 0.001, 0.9, 0.999, 1e-08, 0.01, 10
VMEM_LIMIT = 60 * 1024 * 1024
ACC_TILE_BYTES = 6 * 1024 * 1024
MESH = pl.DeviceIdType.MESH

_CP = functools.partial(pltpu.CompilerParams, vmem_limit_bytes=VMEM_LIMIT)
_ARB = ("arbitrary",)
_ANY = pl.BlockSpec(memory_space=pl.ANY)


def _dg(a, b, dims):
    return lax.dot_general(a.astype(bf16), b.astype(bf16), (dims, ((), ())), preferred_element_type=f32)


@jax.custom_vjp
def dnn(a, b):
    return _dg(a, b, ((1,), (0,)))


@jax.custom_vjp
def dnt(a, b):
    return _dg(a, b, ((1,), (1,)))


@jax.custom_vjp
def dtn(a, b):
    return _dg(a, b, ((0,), (0,)))


dnn.defvjp(lambda a, b: (dnn(a, b), (a, b)), lambda r, g: (dnt(g, r[1]).astype(r[0].dtype), dtn(r[0], g).astype(r[1].dtype)))
dnt.defvjp(lambda a, b: (dnt(a, b), (a, b)), lambda r, g: (dnn(g, r[1]).astype(r[0].dtype), dtn(g, r[0]).astype(r[1].dtype)))
dtn.defvjp(lambda a, b: (dtn(a, b), (a, b)), lambda r, g: (dnt(r[1], g).astype(r[0].dtype), dnn(r[0], g).astype(r[1].dtype)))


@jax.custom_vjp
def _dnn_const(a, w, wt):
    return dnn(a, w)


_dnn_const.defvjp(lambda a, w, wt: (dnn(a, w), wt), lambda wt, g: (dnn(g, wt), None, None))


_GELU_C0, _GELU_C1 = float(np.sqrt(2.0 / np.pi)), 0.044715


@jax.custom_vjp
def _gelu(x):
    return _gelu_fwd(x)[0]


def _gelu_fwd(x):
    t = jnp.tanh(_GELU_C0 * (x + _GELU_C1 * (x * x * x)))
    return x * (0.5 * (1.0 + t)), (x, t)


def _gelu_bwd(res, g):
    x, t = res
    return (g * (0.5 * (1.0 + t) + (0.5 * _GELU_C0) * x * (1.0 - t * t) * (1.0 + (3.0 * _GELU_C1) * (x * x))),)


_gelu.defvjp(_gelu_fwd, _gelu_bwd)


def _rms(t, w):
    return t * lax.rsqrt(jnp.mean(t * t, axis=-1, keepdims=True) + EPS) * w


@jax.custom_vjp
def _norm_mod(x, w, shift, scale):
    return _norm_mod_fwd(x, w, shift, scale)[0]


def _norm_mod_fwd(x, w, shift, scale):
    r = lax.rsqrt(jnp.mean(x * x, axis=-1, keepdims=True) + EPS)
    n = x * r
    return (n * w) * (1.0 + scale) + shift, (n, r, w, scale)


def _norm_mod_bwd(res, dh):
    n, r, w, scale = res
    col = jnp.sum(dh * n, axis=0, keepdims=True)
    dn = dh * (w * (1.0 + scale))
    dx = r * (dn - n * jnp.mean(dn * n, axis=-1, keepdims=True))
    return dx, col * (1.0 + scale), jnp.sum(dh, axis=0, keepdims=True), col * w


_norm_mod.defvjp(_norm_mod_fwd, _norm_mod_bwd)


def _const_spec(shape):
    n = len(shape)
    return pl.BlockSpec(shape, lambda i, _n=n: (0,) * _n, pipeline_mode=pl.Buffered(1))


def _acc_spec(shape):
    n = len(shape)
    return pl.BlockSpec(shape, lambda i, _n=n: (0,) * _n)


def _me():
    return 4 * lax.axis_index("x") + 2 * lax.axis_index("y") + lax.axis_index("c")


def _peer(r):
    x, y, c = lax.axis_index("x"), lax.axis_index("y"), lax.axis_index("c")
    px = 1 - x if (r >> 2) & 1 else x
    py = 1 - y if (r >> 1) & 1 else y
    pc = 1 - c if r & 1 else c
    return (px, py, pc), 4 * px + 2 * py + pc


def _all_gather_small(v, name):
    r, c = v.shape

    def body(v_ref, out_ref, send_sems, recv_sems):
        me = _me()
        out_ref[me] = v_ref[...]
        sends = []
        for k in range(1, NDEV):
            peer, _ = _peer(k)
            cp = pltpu.make_async_remote_copy(src_ref=v_ref, dst_ref=out_ref.at[me], send_sem=send_sems.at[k - 1],
                                              recv_sem=recv_sems.at[k - 1], device_id=peer, device_id_type=MESH)
            cp.start()
            sends.append(cp)
        for k in range(1, NDEV):
            peer, pidx = _peer(k)
            pltpu.make_async_remote_copy(src_ref=v_ref, dst_ref=out_ref.at[pidx], send_sem=send_sems.at[k - 1],
                                         recv_sem=recv_sems.at[k - 1], device_id=peer, device_id_type=MESH).wait_recv()
        for cp in sends:
            cp.wait_send()

    return pl.pallas_call(
        body, name=name, out_shape=jax.ShapeDtypeStruct((NDEV, r, c), v.dtype),
        in_specs=[pl.BlockSpec(memory_space=pltpu.VMEM)], out_specs=pl.BlockSpec(memory_space=pltpu.VMEM),
        scratch_shapes=[pltpu.SemaphoreType.DMA((NDEV - 1,)), pltpu.SemaphoreType.DMA((NDEV - 1,))],
        compiler_params=_CP(),
    )(v)


def _all_reduce_small(v, name):
    r, c = v.shape

    def body(v_ref, out_ref, land, send_sems, recv_sems):
        me = _me()
        land[me] = v_ref[...]
        sends = []
        for k in range(1, NDEV):
            peer, _ = _peer(k)
            cp = pltpu.make_async_remote_copy(src_ref=v_ref, dst_ref=land.at[me], send_sem=send_sems.at[k - 1],
                                              recv_sem=recv_sems.at[k - 1], device_id=peer, device_id_type=MESH)
            cp.start()
            sends.append(cp)
        for k in range(1, NDEV):
            peer, pidx = _peer(k)
            pltpu.make_async_remote_copy(src_ref=v_ref, dst_ref=land.at[pidx], send_sem=send_sems.at[k - 1],
                                         recv_sem=recv_sems.at[k - 1], device_id=peer, device_id_type=MESH).wait_recv()
        for cp in sends:
            cp.wait_send()
        acc = land[0]
        for j in range(1, NDEV):
            acc = acc + land[j]
        out_ref[...] = acc

    return pl.pallas_call(
        body, name=name, out_shape=jax.ShapeDtypeStruct((r, c), v.dtype),
        in_specs=[pl.BlockSpec(memory_space=pltpu.VMEM)], out_specs=pl.BlockSpec(memory_space=pltpu.VMEM),
        scratch_shapes=[pltpu.VMEM((NDEV, r, c), v.dtype), pltpu.SemaphoreType.DMA((NDEV - 1,)),
                        pltpu.SemaphoreType.DMA((NDEV - 1,))],
        compiler_params=_CP(),
    )(v)


class _Exchange:
    def __init__(self, srcs, dsts, send_sems, recv_sems, local_sems, scatter):
        me = _me()
        n = len(srcs)
        self.sends, self.recvs, self.locals = [], [], []
        for a, (s, d) in enumerate(zip(srcs, dsts)):
            self.locals.append(pltpu.make_async_copy(s.at[me] if scatter else s, d.at[me], local_sems.at[a]))
        for k in range(1, NDEV):
            peer, pidx = _peer(k)
            for a, (s, d) in enumerate(zip(srcs, dsts)):
                src = s.at[pidx] if scatter else s
                sem = (k - 1) * n + a
                for dst, out in ((d.at[me], self.sends), (d.at[pidx], self.recvs)):
                    out.append(pltpu.make_async_remote_copy(src_ref=src, dst_ref=dst, send_sem=send_sems.at[sem],
                                                            recv_sem=recv_sems.at[sem], device_id=peer, device_id_type=MESH))

    def start(self):
        for cp in self.locals + self.sends:
            cp.start()

    def wait(self):
        for cp in self.recvs:
            cp.wait_recv()
        for cp in self.sends:
            cp.wait_send()
        for cp in self.locals:
            cp.wait()


def _exchange_shapes(arrays, scatter):
    return [jax.ShapeDtypeStruct(a.shape if scatter else (NDEV,) + a.shape, a.dtype) for a in arrays]


def _exchange_sems(n):
    return [pltpu.SemaphoreType.DMA(((NDEV - 1) * n,)), pltpu.SemaphoreType.DMA(((NDEV - 1) * n,)), pltpu.SemaphoreType.DMA((n,))]


def _chips():
    x, y, c = lax.axis_index("x"), lax.axis_index("y"), lax.axis_index("c")
    return (x, y, c), (x, y, 1 - c), [(1 - x, y), (x, 1 - y), (1 - x, 1 - y)]


def _gather_two_level(arrays, name):
    n = len(arrays)

    def body(*refs):
        srcs, outs = refs[:n], refs[n:2 * n]
        send_sems, recv_sems = refs[2 * n:]
        me, sibling, chips = _chips()
        c = me[2]
        idx = lambda p: 4 * p[0] + 2 * p[1] + p[2]

        def copy(a, k, block, to, src=None):
            return pltpu.make_async_remote_copy(
                src_ref=outs[a].at[idx(block)] if src is None else src, dst_ref=outs[a].at[idx(block)],
                send_sem=send_sems.at[7 * a + k], recv_sem=recv_sems.at[7 * a + k], device_id=to, device_id_type=MESH)

        first, passed = [], []
        for a in range(n):
            outs[a][idx(me)] = srcs[a][...]
            first += [copy(a, 0, me, sibling, src=srcs[a])]
            first += [copy(a, 1 + j, me, (*chip, c), src=srcs[a]) for j, chip in enumerate(chips)]
        for cp in first:
            cp.start()
        for a in range(n):
            for j, chip in enumerate(chips):
                copy(a, 1 + j, (*chip, c), me).wait_recv()
                cp = copy(a, 4 + j, (*chip, c), sibling)
                cp.start()
                passed.append(cp)
        for a in range(n):
            copy(a, 0, sibling, me).wait_recv()
            for j, chip in enumerate(chips):
                copy(a, 4 + j, (*chip, 1 - c), me).wait_recv()
        for cp in first + passed:
            cp.wait_send()

    vm = pl.BlockSpec(memory_space=pltpu.VMEM)
    return pl.pallas_call(
        body, name=name, out_shape=[jax.ShapeDtypeStruct((NDEV,) + a.shape, a.dtype) for a in arrays],
        in_specs=[vm] * n, out_specs=[vm] * n,
        scratch_shapes=[pltpu.SemaphoreType.DMA((7 * n,)), pltpu.SemaphoreType.DMA((7 * n,))],
        compiler_params=_CP(),
    )(*arrays)


def _reduce_scatter_two_level(g, name):
    _, r, c = g.shape
    nchip = NDEV // 2

    def body(g_ref, o_ref, stage, part, land, d_send, d_recv, i_send, i_recv):
        me, sibling, chips = _chips()
        x, y, cc = me
        mine = 2 * x + y

        def blk(k, core):
            return 2 * k + core

        swaps = [pltpu.make_async_remote_copy(src_ref=g_ref.at[blk(k, 1 - cc)], dst_ref=stage.at[k], send_sem=d_send.at[k],
                                              recv_sem=d_recv.at[k], device_id=sibling, device_id_type=MESH) for k in range(nchip)]
        for cp in swaps:
            cp.start()
        for cp in swaps:
            cp.wait_recv()
        for k in range(nchip):
            part[k] = (g_ref[blk(k, cc)].astype(f32) + stage[k].astype(f32)).astype(bf16)
        sends = []
        for j, chip in enumerate(chips):
            kd = 2 * chip[0] + chip[1]
            cp = pltpu.make_async_remote_copy(src_ref=part.at[kd], dst_ref=land.at[mine], send_sem=i_send.at[j],
                                              recv_sem=i_recv.at[j], device_id=(*chip, cc), device_id_type=MESH)
            cp.start()
            sends.append(cp)
        land[mine] = part[mine]
        for j, chip in enumerate(chips):
            ks = 2 * chip[0] + chip[1]
            pltpu.make_async_remote_copy(src_ref=part.at[ks], dst_ref=land.at[ks], send_sem=i_send.at[j], recv_sem=i_recv.at[j],
                                         device_id=(*chip, cc), device_id_type=MESH).wait_recv()
        for cp in swaps + sends:
            cp.wait_send()
        acc = land[0].astype(f32)
        for k in range(1, nchip):
            acc = acc + land[k].astype(f32)
        o_ref[...] = acc

    vm = pl.BlockSpec(memory_space=pltpu.VMEM)
    return pl.pallas_call(
        body, name=name, out_shape=jax.ShapeDtypeStruct((r, c), f32), in_specs=[vm], out_specs=vm,
        scratch_shapes=[pltpu.VMEM((nchip, r, c), g.dtype)] * 3 + [pltpu.SemaphoreType.DMA((nchip,)), pltpu.SemaphoreType.DMA((nchip,)),
                                                                   pltpu.SemaphoreType.DMA((3,)), pltpu.SemaphoreType.DMA((3,))],
        compiler_params=_CP(),
    )(g)


class _Cargo:
    def __init__(self, cargo):
        self.arrays, self.scatter = cargo if cargo else ([], False)
        self.n = len(self.arrays)

    def in_specs(self):
        return [_ANY] * self.n

    def out_shapes(self):
        return _exchange_shapes(self.arrays, self.scatter)

    def sems(self):
        return _exchange_sems(self.n) if self.n else []

    def split(self, refs, n_in, n_out, n_scratch):
        n = self.n
        return refs[:n_in], refs[n_in + n:n_in + n + n_out], refs[n_in + 2 * n + n_out:n_in + 2 * n + n_out + n_scratch]

    def ride(self, refs, n_in, n_out, grid):
        if not self.n:
            return
        n = self.n
        ex = _Exchange(refs[n_in:n_in + n], refs[n_in + n + n_out:n_in + 2 * n + n_out], *refs[-3:], self.scatter)
        grid = (grid,) if isinstance(grid, int) else tuple(grid)
        first = functools.reduce(jnp.logical_and, [pl.program_id(a) == 0 for a in range(len(grid))])
        last = functools.reduce(jnp.logical_and, [pl.program_id(a) == g - 1 for a, g in enumerate(grid)])

        @pl.when(first)
        def _():
            ex.start()

        @pl.when(last)
        def _():
            ex.wait()


def _sum8(land, name):
    _, r, c = land.shape
    rb = next((b for b in (256, 64, 32) if r % b == 0), r)

    def body(l_ref, o_ref):
        acc = l_ref[0].astype(f32)
        for j in range(1, NDEV):
            acc = acc + l_ref[j].astype(f32)
        o_ref[...] = acc

    return pl.pallas_call(
        body, name=name, grid=(r // rb,), out_shape=jax.ShapeDtypeStruct((r, c), f32),
        in_specs=[pl.BlockSpec((NDEV, rb, c), lambda i: (0, i, 0))], out_specs=pl.BlockSpec((rb, c), lambda i: (i, 0)),
        compiler_params=_CP(dimension_semantics=("parallel",)),
    )(land)


def _ada_fwd(c9, w_mod_l, name):
    def body(c_ref, w_ref, o_ref):
        o_ref[...] = dnn(jax.nn.silu(c_ref[...]), w_ref[...])

    return pl.pallas_call(body, name=name, out_shape=jax.ShapeDtypeStruct((16, w_mod_l.shape[1]), f32),
                          compiler_params=_CP())(c9, w_mod_l)


def _mod_select(m_all, b_mod6, name):
    def body(m_ref, b_ref, mx_ref, mc_ref):
        me = _me()
        mx_ref[...] = m_ref[me] + b_ref[...]
        mc_ref[...] = m_ref[8] + b_ref[...]

    return pl.pallas_call(body, name=name, out_shape=[jax.ShapeDtypeStruct((6, D), f32)] * 2, compiler_params=_CP())(m_all, b_mod6)


def _ada_bwd(c9, dmx_all, dmc_all, dmx_l, dmc_l, w_mod_l, name):
    ncol = w_mod_l.shape[1]

    def rowsum(r):
        acc = r[0:1]
        for j in range(1, NDEV):
            acc = acc + r[j:j + 1]
        return acc

    def body(c_ref, xa_ref, ca_ref, xl_ref, cl_ref, w_ref, gw_ref, gb_ref, dc_ref):
        s9, vjp = jax.vjp(jax.nn.silu, c_ref[...])
        dm9 = jnp.concatenate([xl_ref[...], rowsum(cl_ref[...]), jnp.zeros((7, ncol), f32)], axis=0)
        gw_ref[...] = dtn(s9, dm9)
        gb_ref[...] = rowsum(xa_ref[...]) + rowsum(ca_ref[...])
        dc_ref[...] = vjp(dnt(dm9, w_ref[...]))[0]

    return pl.pallas_call(
        body, name=name,
        out_shape=[jax.ShapeDtypeStruct((D, ncol), f32), jax.ShapeDtypeStruct((1, 6 * D), f32), jax.ShapeDtypeStruct((16, D), f32)],
        compiler_params=_CP())(c9, dmx_all, dmc_all, dmx_l, dmc_l, w_mod_l)


def _lane_sign(rank):
    shape = (1,) * (rank - 1) + (SB,)
    return jnp.where(lax.broadcasted_iota(jnp.int32, shape, rank - 1) < S5N, -1.0, 1.0)


def _s5_build_fn(lre2, lim2, ls, bn, bs, cn, cs, rev):
    sg = _lane_sign(3)
    s = jnp.exp(ls)
    ar, ai = lre2 * s, lim2 * s
    e = jnp.exp(ar)
    nr, ni = e * jnp.cos(ai) - 1.0, e * jnp.sin(ai)
    den = lre2 * lre2 + lim2 * lim2
    cr, ci = (nr * lre2 + ni * lim2) / den, (ni * lre2 - nr * lim2) / den
    bbn = cr * bn + (ci * sg) * bs
    bbs = cr * bs - (ci * sg) * bn

    def powers(ex):
        m, ang = jnp.exp(ex * ar), ex * ai
        return m * jnp.cos(ang), m * jnp.sin(ang) * sg

    def times(tabs, xn, xs):
        f1, f2 = tabs
        return f1[:, :, None, :] * xn[:, None, :, :] + f2[:, :, None, :] * xs[:, None, :, :]

    t = lax.broadcasted_iota(jnp.int32, (1, TC, 1), 1).astype(f32)
    if rev:
        e_src, e_dst, e_out, e_in = t - (TC - 1.0), (TC - 1.0) - t, t, TC - t
    else:
        e_src, e_dst, e_out, e_in = -t, t, (TC - 1.0) - t, t + 1.0
    g = lre2.shape[0]
    flat = lambda a: a.reshape(g, TCP, SB)
    conj = -_lane_sign(4)
    ll = flat(times(powers(e_src), bbn, bbs))
    rr = flat(times(powers(e_dst), cn, cs) * conj)
    mb = flat(times(powers(e_out), bbn, bbs))
    mct = flat(times(powers(e_in), cn, cs) * conj)
    a1, a2 = powers(float(TC))
    row = lax.broadcasted_iota(jnp.int32, (TCP, TCP), 0) // S5P
    col = lax.broadcasted_iota(jnp.int32, (TCP, TCP), 1) // S5P
    mask = jnp.where((col <= row) if rev else (col >= row), 1.0, 0.0)
    m = jnp.concatenate([dnt(ll[j], rr[j])[None] for j in range(g)], axis=0) * mask
    return m, mb, mct, a1, a2


def _gspec(*tail):
    nt = len(tail)
    return pl.BlockSpec((GBK,) + tail, lambda i, _n=nt: (i,) + (0,) * _n)


def _s5_build(params, rev, name):
    def body(l1, l2, ls, bn, bs, cn, cs, m_ref, mb_ref, mc_ref, a1_ref, a2_ref):
        m, mb, mct, a1, a2 = _s5_build_fn(l1[...], l2[...], ls[...], bn[...], bs[...], cn[...], cs[...], rev)
        m_ref[...], mb_ref[...], mc_ref[...] = m.astype(bf16), mb.astype(bf16), mct.astype(bf16)
        a1_ref[...], a2_ref[...] = a1, a2

    vec, pm = _gspec(1, SB), _gspec(S5P, SB)
    return pl.pallas_call(
        body, name=name, grid=(S5G // GBK,),
        in_specs=[vec, vec, _gspec(1, 1), pm, pm, pm, pm],
        out_specs=[_gspec(TCP, TCP), _gspec(TCP, SB), _gspec(TCP, SB), vec, vec],
        out_shape=[jax.ShapeDtypeStruct((S5G, TCP, TCP), bf16), jax.ShapeDtypeStruct((S5G, TCP, SB), bf16),
                   jax.ShapeDtypeStruct((S5G, TCP, SB), bf16), jax.ShapeDtypeStruct((S5G, 1, SB), f32),
                   jax.ShapeDtypeStruct((S5G, 1, SB), f32)],
        compiler_params=_CP(dimension_semantics=("parallel",)),
    )(*params)


def _s5_build_bwd(params, cots, prev, rev, name):
    def body(l1, l2, ls, bn, bs, cn, cs, dm, dmb, dmc, da1, da2, pb, pc, gl1, gl2, gls, gb, gc):
        prim = (l1[...], l2[...], ls[...], bn[...], bs[...], cn[...], cs[...])
        _, vjp = jax.vjp(functools.partial(_s5_build_fn, rev=rev), *prim)
        d1, d2, dls, dbn, dbs, dcn, dcs = vjp((dm[...], dmb[...], dmc[...], da1[...], da2[...]))
        gl1[...] = d1 + pltpu.roll(d1, S5N, axis=2)
        gl2[...] = d2 + pltpu.roll(d2, S5N, axis=2)
        gls[...] = dls
        gb[...] = dbn + pltpu.roll(dbs, S5N, axis=2) + pb[...]
        gc[...] = dcn + pltpu.roll(dcs, S5N, axis=2) + pc[...]

    vec, pm, big = _gspec(1, SB), _gspec(S5P, SB), _gspec(TCP, SB)
    return pl.pallas_call(
        body, name=name, grid=(S5G // GBK,),
        in_specs=[vec, vec, _gspec(1, 1), pm, pm, pm, pm, _gspec(TCP, TCP), big, big, vec, vec, pm, pm],
        out_specs=[vec, vec, _gspec(1, 1), pm, pm],
        out_shape=[jax.ShapeDtypeStruct((S5G, 1, SB), f32), jax.ShapeDtypeStruct((S5G, 1, SB), f32),
                   jax.ShapeDtypeStruct((S5G, 1, 1), f32), jax.ShapeDtypeStruct((S5G, S5P, SB), f32),
                   jax.ShapeDtypeStruct((S5G, S5P, SB), f32)],
        compiler_params=_CP(dimension_semantics=("parallel",)),
    )(*params, *cots, *prev)


def _s5_inc(u, mb_f, mb_b, name):
    nc = u.shape[1]

    def body(u_ref, mf_ref, mb_ref, sf_ref, sb_ref):
        for j in range(GBK):
            sf_ref[:, j, :] = jnp.dot(u_ref[j], mf_ref[j], preferred_element_type=f32)
            sb_ref[:, j, :] = jnp.dot(u_ref[j], mb_ref[j], preferred_element_type=f32)

    sspec = pl.BlockSpec((nc, GBK, SB), lambda i: (0, i, 0))
    return pl.pallas_call(
        body, name=name, grid=(S5G // GBK,), in_specs=[_gspec(nc, TCP), _gspec(TCP, SB), _gspec(TCP, SB)],
        out_specs=[sspec, sspec], out_shape=[jax.ShapeDtypeStruct((nc, S5G, SB), f32)] * 2,
        compiler_params=_CP(dimension_semantics=("parallel",)),
    )(u, mb_f, mb_b)


def _idx_fwd(nctx, nch):
    return lambda i: i


def _idx_rev(nctx, nch):
    return lambda i: jnp.where(i < nctx, nctx - 1 - i, nch + nctx - 1 - i)


def _carry_loop(nc, step, init):
    def trip(i, c):
        for k in range(CARRY_UNROLL):
            c = step(i * CARRY_UNROLL + k, c)
        return c

    return lax.fori_loop(0, nc // CARRY_UNROLL, trip, init)


def _s5_carry(s_f, s_b, a_f, a_b, nctx, name):
    nc = s_f.shape[0]
    idx_b = _idx_rev(nctx, nc)

    def body(sf_ref, sb_ref, f1_ref, f2_ref, b1_ref, b2_ref, hf_ref, hb_ref):
        f1, f2, b1, b2 = f1_ref[...], f2_ref[...], b1_ref[...], b2_ref[...]

        def step(i, c):
            hf, hfs, hb, hbs = c
            rb = idx_b(i)
            hf_ref[i] = hf
            hb_ref[rb] = hb
            sf, sb = sf_ref[i], sb_ref[rb]
            return (f1 * hf + f2 * hfs + sf, f1 * hfs - f2 * hf + pltpu.roll(sf, S5N, axis=1),
                    b1 * hb + b2 * hbs + sb, b1 * hbs - b2 * hb + pltpu.roll(sb, S5N, axis=1))

        z = jnp.zeros((S5G, SB), f32)
        _carry_loop(nc, step, (z, z, z, z))

    return pl.pallas_call(body, name=name, out_shape=[jax.ShapeDtypeStruct(s_f.shape, f32)] * 2,
                          compiler_params=_CP())(s_f, s_b, *a_f, *a_b)


def _s5_carry_bwd(dhp, hp, a1, a2, rev, nctx, name):
    nc = hp.shape[0]
    idx = (_idx_rev if rev else _idx_fwd)(nctx, nc)

    def body(dhp_ref, hp_ref, a1_ref, a2_ref, ds_ref, d1_ref, d2_ref):
        f1, f2 = a1_ref[...], a2_ref[...]

        def step(k, carry):
            ab, abs_, d1, d2 = carry
            r = idx(nc - 1 - k)
            ds_ref[r] = ab
            h, dh = hp_ref[r], dhp_ref[r]
            return (dh + f1 * ab - f2 * abs_, pltpu.roll(dh, S5N, axis=1) + f1 * abs_ + f2 * ab,
                    d1 + ab * h, d2 + ab * pltpu.roll(h, S5N, axis=1))

        z = jnp.zeros((S5G, SB), f32)
        _, _, d1, d2 = _carry_loop(nc, step, (z, z, z, z))
        d1_ref[...], d2_ref[...] = d1, d2

    return pl.pallas_call(
        body, name=name,
        out_shape=[jax.ShapeDtypeStruct(hp.shape, f32), jax.ShapeDtypeStruct((S5G, SB), f32), jax.ShapeDtypeStruct((S5G, SB), f32)],
        compiler_params=_CP())(dhp, hp, a1, a2)


def _s5_out(u, m_f, m_b, hp_f, hp_b, mc_f, mc_b, name):
    nc = u.shape[1]

    def body(u_ref, mf_ref, mb_ref, hf_ref, hb_ref, cf_ref, cb_ref, y_ref):
        for j in range(GBK):
            uj = u_ref[j]
            y_ref[j] = (jnp.dot(uj, mf_ref[j], preferred_element_type=f32) + jnp.dot(uj, mb_ref[j], preferred_element_type=f32)
                        + dnt(hf_ref[:, j, :], cf_ref[j]) + dnt(hb_ref[:, j, :], cb_ref[j])).astype(bf16)

    sspec = pl.BlockSpec((nc, GBK, SB), lambda i: (0, i, 0))
    return pl.pallas_call(
        body, name=name, grid=(S5G // GBK,),
        in_specs=[_gspec(nc, TCP), _gspec(TCP, TCP), _gspec(TCP, TCP), sspec, sspec, _gspec(TCP, SB), _gspec(TCP, SB)],
        out_specs=_gspec(nc, TCP), out_shape=jax.ShapeDtypeStruct((S5G, nc, TCP), bf16),
        compiler_params=_CP(dimension_semantics=("parallel",)),
    )(u, m_f, m_b, hp_f, hp_b, mc_f, mc_b)


def _s5_out_bwd(dy, u, m_f, m_b, hp_f, hp_b, mc_f, mc_b, name):
    nc = u.shape[1]

    def body(dy_ref, u_ref, mf_ref, mb_ref, hf_ref, hb_ref, cf_ref, cb_ref, du_ref, g_ref, dhf_ref, dhb_ref, dcf_ref, dcb_ref):
        for j in range(GBK):
            dyj = dy_ref[j]
            du_ref[j] = dnt(dyj, mf_ref[j]) + dnt(dyj, mb_ref[j])
            g_ref[j] = dtn(u_ref[j], dyj)
            dhf_ref[:, j, :] = dnn(dyj, cf_ref[j])
            dhb_ref[:, j, :] = dnn(dyj, cb_ref[j])
            dcf_ref[j] = dtn(dyj, hf_ref[:, j, :])
            dcb_ref[j] = dtn(dyj, hb_ref[:, j, :])

    sspec = pl.BlockSpec((nc, GBK, SB), lambda i: (0, i, 0))
    sshape = jax.ShapeDtypeStruct((nc, S5G, SB), f32)
    cshape = jax.ShapeDtypeStruct((S5G, TCP, SB), f32)
    return pl.pallas_call(
        body, name=name, grid=(S5G // GBK,),
        in_specs=[_gspec(nc, TCP), _gspec(nc, TCP), _gspec(TCP, TCP), _gspec(TCP, TCP), sspec, sspec, _gspec(TCP, SB), _gspec(TCP, SB)],
        out_specs=[_gspec(nc, TCP), _gspec(TCP, TCP), sspec, sspec, _gspec(TCP, SB), _gspec(TCP, SB)],
        out_shape=[jax.ShapeDtypeStruct((S5G, nc, TCP), f32), jax.ShapeDtypeStruct((S5G, TCP, TCP), f32), sshape, sshape, cshape, cshape],
        compiler_params=_CP(dimension_semantics=("parallel",)),
    )(dy, u, m_f, m_b, hp_f, hp_b, mc_f, mc_b)


def _s5_inc_bwd(du1, u, ds_f, ds_b, mb_f, mb_b, name):
    nc = u.shape[1]

    def body(du1_ref, u_ref, dsf_ref, dsb_ref, mf_ref, mb_ref, du_ref, dmf_ref, dmb_ref):
        for j in range(GBK):
            dsf, dsb = dsf_ref[:, j, :], dsb_ref[:, j, :]
            du_ref[j] = (du1_ref[j] + dnt(dsf, mf_ref[j]) + dnt(dsb, mb_ref[j])).astype(bf16)
            dmf_ref[j] = dtn(u_ref[j], dsf)
            dmb_ref[j] = dtn(u_ref[j], dsb)

    sspec = pl.BlockSpec((nc, GBK, SB), lambda i: (0, i, 0))
    cshape = jax.ShapeDtypeStruct((S5G, TCP, SB), f32)
    return pl.pallas_call(
        body, name=name, grid=(S5G // GBK,),
        in_specs=[_gspec(nc, TCP), _gspec(nc, TCP), sspec, sspec, _gspec(TCP, SB), _gspec(TCP, SB)],
        out_specs=[_gspec(nc, TCP), _gspec(TCP, SB), _gspec(TCP, SB)],
        out_shape=[jax.ShapeDtypeStruct((S5G, nc, TCP), bf16), cshape, cshape],
        compiler_params=_CP(dimension_semantics=("parallel",)),
    )(du1, u, ds_f, ds_b, mb_f, mb_b)


def _to_groups(a):
    n = a.shape[0]
    return a.reshape(n // TC, TC, S5G, S5P).transpose(2, 0, 1, 3).reshape(S5G, n // TC, TCP)


def _from_groups(a):
    nc = a.shape[1]
    return a.reshape(S5G, nc, TC, S5P).transpose(1, 2, 0, 3).reshape(nc * TC, S5W)


def _swap_pairs(t):
    lane = lax.broadcasted_iota(jnp.int32, t.shape, 1)
    return jnp.where(lane % 2 == 0, pltpu.roll(t, DH - 1, axis=1), pltpu.roll(t, 1, axis=1))


def _rot(t, cosf, sins):
    return t * cosf + _swap_pairs(t) * sins


def _rot_t(d, cosf, sins):
    return d * cosf - _swap_pairs(d) * sins


def _ret_tables(ld, rev):
    pos = lax.broadcasted_iota(jnp.int32, (T, 1), 0).astype(f32)
    diff = pos - lax.broadcasted_iota(jnp.int32, (1, T), 1).astype(f32)
    if rev:
        keep, dist = diff < 0, jnp.maximum(-diff, 0.0)
        xi, zeta = jnp.exp(ld * (T - pos)), jnp.exp(ld * pos)
    else:
        keep, dist = diff >= 0, jnp.maximum(diff, 0.0)
        xi, zeta = jnp.exp(ld * (pos + 1.0)), jnp.exp(ld * (T - 1.0 - pos))
    return jnp.where(keep, jnp.exp(ld * dist), 0.0), xi, zeta, jnp.exp(ld * float(T))


def _ret_apply(qr, kr, v, rp, dm, xi, zeta, cdec):
    out = dnn(dnt(qr, kr) * dm, v) + dnn(qr * xi, rp)
    return out, cdec * rp + dtn(kr * zeta, v)


def _ret_fwd(p_ext, ld8_f, ld8_b, nctx, name, cargo=None):
    n = p_ext.shape[0]
    nch = n // T
    idx_b = _idx_rev(nctx, nch)
    cg = _Cargo(cargo)

    def body(*refs):
        ins, (of_ref, ob_ref, rpf_ref, rpb_ref), (rf_s, rb_s, dm_s, xz_s) = cg.split(refs, 8, 4, 4)
        qf, kf, vf, qb, kb, vb, ldf_ref, ldb_ref = ins
        cg.ride(refs, 8, 4, nch)

        @pl.when(pl.program_id(0) == 0)
        def _():
            rf_s[...] = jnp.zeros_like(rf_s)
            rb_s[...] = jnp.zeros_like(rb_s)
            for d, ld_ref in enumerate((ldf_ref, ldb_ref)):
                for h in range(RH):
                    dm, xi, zeta, cdec = _ret_tables(ld_ref[h:h + 1, 0:1], bool(d))
                    dm_s[d, h] = dm
                    xz_s[d, h, 0] = jnp.broadcast_to(xi, (T, DH))
                    xz_s[d, h, 1] = jnp.broadcast_to(zeta, (T, DH))
                    xz_s[d, h, 2] = jnp.broadcast_to(cdec, (T, DH))

        for h in range(RH):
            sl = slice(h * DH, (h + 1) * DH)
            for d, (q_ref, k_ref, v_ref, o_ref, rp_ref, r_s) in enumerate(((qf, kf, vf, of_ref, rpf_ref, rf_s),
                                                                            (qb, kb, vb, ob_ref, rpb_ref, rb_s))):
                rp = r_s[h]
                rp_ref[0, h] = rp
                out, rn = _ret_apply(q_ref[:, sl].astype(f32), k_ref[:, sl].astype(f32), v_ref[:, sl].astype(f32), rp,
                                     dm_s[d, h], xz_s[d, h, 0], xz_s[d, h, 1], xz_s[d, h, 2])
                r_s[h] = rn
                o_ref[:, sl] = out

    fcol = lambda cb: pl.BlockSpec((T, RW), lambda i, _c=cb: (i, _c))
    bcol = lambda cb: pl.BlockSpec((T, RW), lambda i, _c=cb: (idx_b(i), _c))
    rspec = pl.BlockSpec((1, RH, DH, DH), lambda i: (i, 0, 0, 0))
    oshape, rshape = jax.ShapeDtypeStruct((n, RW), f32), jax.ShapeDtypeStruct((nch, RH, DH, DH), f32)
    return pl.pallas_call(
        body, name=name, grid=(nch,),
        in_specs=[fcol(1), fcol(2), fcol(3), bcol(1), bcol(2), bcol(3), _const_spec((8, 128)), _const_spec((8, 128))] + cg.in_specs(),
        out_specs=[fcol(0), bcol(0), rspec, rspec] + cg.in_specs(),
        out_shape=[oshape, oshape, rshape, rshape] + cg.out_shapes(),
        scratch_shapes=[pltpu.VMEM((RH, DH, DH), f32)] * 2 + [pltpu.VMEM((2, RH, T, T), f32), pltpu.VMEM((2, RH, 3, T, DH), f32)] + cg.sems(),
        compiler_params=_CP(dimension_semantics=_ARB),
    )(p_ext, p_ext, p_ext, p_ext, p_ext, p_ext, ld8_f, ld8_b, *cg.arrays)


def _ret_bwd(p_ext, ld8_f, ld8_b, rp_f, rp_b, do_ext, nctx, name, cargo=None):
    n = p_ext.shape[0]
    nch = n // T
    idx_rev = _idx_rev(nctx, nch)
    idf = lambda j: nch - 1 - j
    idb = lambda j: idx_rev(nch - 1 - j)
    cg = _Cargo(cargo)

    def body(*refs):
        ins, outs, (drf_s, drb_s, dm_s, xz_s, gdm_s, gxz_s) = cg.split(refs, 12, 8, 6)
        qf, kf, vf, qb, kb, vb, ldf_ref, ldb_ref, rpf_ref, rpb_ref, dof_ref, dob_ref = ins
        dqf, dkf, dvf, dqb, dkb, dvb, dldf_ref, dldb_ref = outs
        cg.ride(refs, 12, 8, nch)
        lds = (ldf_ref, ldb_ref)

        @pl.when(pl.program_id(0) == 0)
        def _():
            for r in (drf_s, drb_s, gdm_s, gxz_s):
                r[...] = jnp.zeros_like(r)
            for d in range(2):
                for h in range(RH):
                    dm, xi, zeta, cdec = _ret_tables(lds[d][h:h + 1, 0:1], bool(d))
                    dm_s[d, h] = dm
                    for k, tab in enumerate((xi, zeta, cdec)):
                        xz_s[d, h, k] = jnp.broadcast_to(tab, (T, DH))

        for h in range(RH):
            sl = slice(h * DH, (h + 1) * DH)
            for d, (q_ref, k_ref, v_ref, rp_ref, do_ref, dq_ref, dk_ref, dv_ref, dr_s) in enumerate((
                    (qf, kf, vf, rpf_ref, dof_ref, dqf, dkf, dvf, drf_s), (qb, kb, vb, rpb_ref, dob_ref, dqb, dkb, dvb, drb_s))):
                _, vjp = jax.vjp(_ret_apply, q_ref[:, sl].astype(f32), k_ref[:, sl].astype(f32), v_ref[:, sl].astype(f32),
                                 rp_ref[0, h], dm_s[d, h], xz_s[d, h, 0], xz_s[d, h, 1], xz_s[d, h, 2])
                dqr, dkr, dv, drp, gdm, gxi, gzeta, gcdec = vjp((do_ref[:, sl], dr_s[h]))
                dr_s[h] = drp
                dq_ref[:, sl], dk_ref[:, sl], dv_ref[:, sl] = dqr, dkr, dv
                gdm_s[d, h] += gdm
                for k, g in enumerate((gxi, gzeta, gcdec)):
                    gxz_s[d, h, k] += g

        @pl.when(pl.program_id(0) == nch - 1)
        def _():
            for d, dld_ref in enumerate((dldf_ref, dldb_ref)):
                dld_ref[...] = jnp.zeros_like(dld_ref)
                for h in range(RH):
                    _, vjp = jax.vjp(functools.partial(_ret_tables, rev=bool(d)), lds[d][h:h + 1, 0:1])
                    lanes = lambda a: jnp.sum(a, axis=1, keepdims=True)
                    (dld,) = vjp((gdm_s[d, h], lanes(gxz_s[d, h, 0]), lanes(gxz_s[d, h, 1]),
                                  jnp.sum(lanes(gxz_s[d, h, 2]), axis=0, keepdims=True)))
                    dld_ref[h:h + 1, :] = jnp.broadcast_to(dld, (1, 128))

    fcol = lambda cb: pl.BlockSpec((T, RW), lambda j, _c=cb: (idf(j), _c))
    bcol = lambda cb: pl.BlockSpec((T, RW), lambda j, _c=cb: (idb(j), _c))
    rspec = pl.BlockSpec((1, RH, DH, DH), lambda j: (nch - 1 - j, 0, 0, 0))
    oshape = jax.ShapeDtypeStruct((n, RW), f32)
    return pl.pallas_call(
        body, name=name, grid=(nch,),
        in_specs=[fcol(1), fcol(2), fcol(3), bcol(1), bcol(2), bcol(3), _const_spec((8, 128)), _const_spec((8, 128)), rspec, rspec,
                  fcol(0), bcol(0)] + cg.in_specs(),
        out_specs=[fcol(0), fcol(0), fcol(0), bcol(0), bcol(0), bcol(0), _acc_spec((8, 128)), _acc_spec((8, 128))] + cg.in_specs(),
        out_shape=[oshape] * 6 + [jax.ShapeDtypeStruct((8, 128), f32)] * 2 + cg.out_shapes(),
        scratch_shapes=[pltpu.VMEM((RH, DH, DH), f32)] * 2 + [pltpu.VMEM((2, RH, T, T), f32), pltpu.VMEM((2, RH, 3, T, DH), f32)] * 2
        + cg.sems(),
        compiler_params=_CP(dimension_semantics=_ARB),
    )(p_ext, p_ext, p_ext, p_ext, p_ext, p_ext, ld8_f, ld8_b, rp_f, rp_b, do_ext, do_ext, *cg.arrays)


def _qk_heads(p, fn_q, fn_k):
    heads = lambda base, fn: [fn(p[:, base + h * DH:base + (h + 1) * DH]) for h in range(RH)]
    return jnp.concatenate([p[:, :S5W]] + heads(S5W, fn_q) + heads(S5W + RW, fn_k) + [p[:, S5W + 2 * RW:]], axis=1)


def _f1_fwd(x, ctx, modx, modc, nw1, w_in_n, cosf, sins, name, cargo=None):
    L = x.shape[0]
    nb = L // R + 1
    scale = DH ** -0.5
    cg = _Cargo(cargo)

    def body(*refs):
        (x_ref, c_ref, mx_ref, mc_ref, nw_ref, w_ref, cos_ref, sin_ref), (p_ref,), _ = cg.split(refs, 8, 1, 0)
        cg.ride(refs, 8, 1, nb)
        is_ctx = pl.program_id(0) == 0
        xin = jnp.where(is_ctx, c_ref[...], x_ref[...])
        sh = jnp.where(is_ctx, mc_ref[0:1], mx_ref[0:1])
        sc = jnp.where(is_ctx, mc_ref[1:2], mx_ref[1:2])
        cf, ss = cos_ref[...], sin_ref[...]
        p = dnn(_norm_mod(xin, nw_ref[...], sh, sc), w_ref[...])
        p_ref[...] = _qk_heads(p, lambda t: _rot(t, cf, ss), lambda t: _rot(t * scale, cf, ss)).astype(bf16)

    return pl.pallas_call(
        body, name=name, grid=(nb,),
        in_specs=[pl.BlockSpec((R, D), lambda i: (jnp.maximum(i - 1, 0), 0)), _const_spec((R, D)), _const_spec((6, D)),
                  _const_spec((6, D)), _const_spec((1, D)), _const_spec((D, INC)), pl.BlockSpec((R, DH), lambda i: (i, 0)),
                  pl.BlockSpec((R, DH), lambda i: (i, 0))] + cg.in_specs(),
        out_specs=[pl.BlockSpec((R, INC), lambda i: (i, 0))] + cg.in_specs(),
        out_shape=[jax.ShapeDtypeStruct((L + R, INC), bf16)] + cg.out_shapes(),
        scratch_shapes=cg.sems(),
        compiler_params=_CP(dimension_semantics=_ARB),
    )(x, ctx, modx, modc, nw1, w_in_n, cosf, sins, *cg.arrays)


def _f1_bwd(x, ctx, modx, modc, nw1, w_in_t, cosf, sins, dx1, parts, name, cargo=None):
    L = x.shape[0]
    nb = L // R + 1
    scale = DH ** -0.5
    cg = _Cargo(cargo)

    def body(*refs):
        ins, (gx_ref, dp_ref, h1_ref, dnw_ref, dmx_ref, dmc_ref), _ = cg.split(refs, 18, 6, 0)
        x_ref, c_ref, mx_ref, mc_ref, nw_ref, w_ref, cos_ref, sin_ref, dx1_ref, du0, du1, dq0, dq1, dk0, dk1, dv0, dv1, dg0 = ins
        cg.ride(refs, 18, 6, nb)
        i = pl.program_id(0)
        is_ctx = i == 0

        @pl.when(is_ctx)
        def _():
            dnw_ref[...] = jnp.zeros_like(dnw_ref)
            dmx_ref[...] = jnp.zeros_like(dmx_ref)
            dmc_ref[...] = jnp.zeros_like(dmc_ref)

        cf, ss = cos_ref[...], sin_ref[...]
        dp = jnp.concatenate([du0[...].astype(f32) + du1[...], dq0[...] + dq1[...], dk0[...] + dk1[...], dv0[...] + dv1[...],
                              dg0[...]], axis=1)
        dp = _qk_heads(dp, lambda t: _rot_t(t, cf, ss), lambda t: _rot_t(t, cf, ss) * scale).astype(bf16)
        dp_ref[...] = dp
        xin = jnp.where(is_ctx, c_ref[...], x_ref[...])
        sh = jnp.where(is_ctx, mc_ref[0:1], mx_ref[0:1])
        sc = jnp.where(is_ctx, mc_ref[1:2], mx_ref[1:2])
        dh = dnn(dp, w_ref[...])
        h, vjp = jax.vjp(_norm_mod, xin, nw_ref[...], sh, sc)
        dxin, dnw, dsh, dsc = vjp(dh)
        h1_ref[...] = h.astype(bf16)
        gx_ref[...] = dx1_ref[...] + dxin
        dnw_ref[...] += dnw
        wx = jnp.where(is_ctx, 0.0, 1.0)
        dmx_ref[0:1] += dsh * wx
        dmx_ref[1:2] += dsc * wx
        dmc_ref[0:1] += dsh * (1.0 - wx)
        dmc_ref[1:2] += dsc * (1.0 - wx)

    lat = pl.BlockSpec((R, D), lambda i: (jnp.maximum(i - 1, 0), 0))
    ext = pl.BlockSpec((R, S5W), lambda i: (i, 0))
    return pl.pallas_call(
        body, name=name, grid=(nb,),
        in_specs=[lat, _const_spec((R, D)), _const_spec((6, D)), _const_spec((6, D)), _const_spec((1, D)), _const_spec((INC, D)),
                  pl.BlockSpec((R, DH), lambda i: (i, 0)), pl.BlockSpec((R, DH), lambda i: (i, 0)), lat] + [ext] * 9 + cg.in_specs(),
        out_specs=[lat, pl.BlockSpec((R, INC), lambda i: (i, 0)), pl.BlockSpec((R, D), lambda i: (i, 0)),
                   _acc_spec((1, D)), _acc_spec((6, D)), _acc_spec((6, D))] + cg.in_specs(),
        out_shape=[jax.ShapeDtypeStruct((L, D), f32), jax.ShapeDtypeStruct((L + R, INC), bf16),
                   jax.ShapeDtypeStruct((L + R, D), bf16), jax.ShapeDtypeStruct((1, D), f32),
                   jax.ShapeDtypeStruct((6, D), f32), jax.ShapeDtypeStruct((6, D), f32)] + cg.out_shapes(),
        scratch_shapes=cg.sems(),
        compiler_params=_CP(dimension_semantics=_ARB),
    )(x, ctx, modx, modc, nw1, w_in_t, cosf, sins, dx1, *parts, *cg.arrays)


def _ret_post(yr, g):
    outs = []
    for h in range(RH):
        yh = yr[:, h * DH:(h + 1) * DH]
        mu = jnp.mean(yh, axis=-1, keepdims=True)
        var = jnp.mean((yh - mu) ** 2, axis=-1, keepdims=True)
        outs.append((yh - mu) * lax.rsqrt(var + EPS))
    return jax.nn.silu(g) * jnp.concatenate(outs, axis=1)


def _mix_fn(ys, u, of, ob, g, x, dvec, bglu, gate1, pz, pm, wglu, wout):
    s = _gelu(ys + dvec * u)
    z = dnn(s, wglu) + bglu + pz
    cat = jnp.concatenate([s * jax.nn.sigmoid(z), _ret_post(of + ob, g)], axis=1)
    mix = dnn(cat, wout) + pm
    return x + gate1 * mix, (s, cat)


def _mix_fwd(x, ys, of, ob, p_ext, dvec, bglu, modx, wglu, wout, name, cargo=None):
    L = x.shape[0]
    nb = L // R
    cg = _Cargo(cargo)

    def body(*refs):
        ins, (x1_ref,), _ = cg.split(refs, 11, 1, 0)
        x_ref, ys_ref, of_ref, ob_ref, u_ref, g_ref, d_ref, b_ref, mx_ref, wg_ref, wo_ref = ins
        cg.ride(refs, 11, 1, nb)
        x1_ref[...] = _mix_fn(ys_ref[...].astype(f32), u_ref[...].astype(f32), of_ref[...], ob_ref[...], g_ref[...].astype(f32),
                              x_ref[...], d_ref[...], b_ref[...], mx_ref[2:3], 0.0, 0.0, wg_ref[...], wo_ref[...])[0]

    ext = pl.BlockSpec((R, S5W), lambda i: (i + 1, 0))
    return pl.pallas_call(
        body, name=name, grid=(nb,),
        in_specs=[pl.BlockSpec((R, D), lambda i: (i, 0)), ext, ext, ext, ext, pl.BlockSpec((R, RW), lambda i: (i + 1, 4)),
                  _const_spec((1, S5W)), _const_spec((1, S5W)), _const_spec((6, D)), _const_spec((S5W, S5W)), _const_spec((D, D))]
        + cg.in_specs(),
        out_specs=[pl.BlockSpec((R, D), lambda i: (i, 0))] + cg.in_specs(),
        out_shape=[jax.ShapeDtypeStruct((L, D), f32)] + cg.out_shapes(),
        scratch_shapes=cg.sems(),
        compiler_params=_CP(dimension_semantics=_ARB),
    )(x, ys, of, ob, p_ext, p_ext, dvec, bglu, modx, wglu, wout, *cg.arrays)


def _mix_bwd(x, ys, of, ob, p_ext, dvec, bglu, modx, wglu, wout, dx1, name, cargo=None):
    L = x.shape[0]
    nb = L // R + 1
    cg = _Cargo(cargo)

    def body(*refs):
        ins, outs, _ = cg.split(refs, 12, 11, 0)
        x_ref, ys_ref, of_ref, ob_ref, u_ref, g_ref, d_ref, b_ref, mx_ref, wg_ref, wo_ref, dx1_ref = ins
        dy_ref, dud_ref, do_ref, dg_ref, cat_ref, dmix_ref, s_ref, dz_ref, dd_ref, db_ref, dg1_ref = outs
        cg.ride(refs, 12, 11, nb)
        i = pl.program_id(0)

        @pl.when(i == 0)
        def _():
            for r in outs:
                r[...] = jnp.zeros_like(r)

        @pl.when(i > 0)
        def _():
            fn = lambda ys_, u_, of_, g_, d_, b_, g1_, pz_, pm_: _mix_fn(
                ys_, u_, of_, ob_ref[...], g_, x_ref[...], d_, b_, g1_, pz_, pm_, wg_ref[...], wo_ref[...])
            _, vjp, (s, cat) = jax.vjp(fn, ys_ref[...].astype(f32), u_ref[...].astype(f32), of_ref[...], g_ref[...].astype(f32), d_ref[...],
                                       b_ref[...], mx_ref[2:3], jnp.zeros((R, S5W), f32), jnp.zeros((R, D), f32), has_aux=True)
            dy, dud, do, dg, dd, db, dg1, dz, dmix = vjp(dx1_ref[...])
            dy_ref[...], dud_ref[...], do_ref[...], dg_ref[...] = dy.astype(bf16), dud, do, dg
            cat_ref[...], dmix_ref[...] = cat.astype(bf16), dmix.astype(bf16)
            s_ref[...], dz_ref[...] = s.astype(bf16), dz.astype(bf16)
            dd_ref[...] += dd
            db_ref[...] += db
            dg1_ref[...] += dg1

    lat = pl.BlockSpec((R, D), lambda i: (jnp.maximum(i - 1, 0), 0))
    lat5 = pl.BlockSpec((R, S5W), lambda i: (jnp.maximum(i - 1, 0), 0))
    ext = pl.BlockSpec((R, S5W), lambda i: (i, 0))
    eshape = jax.ShapeDtypeStruct((L + R, S5W), f32)
    return pl.pallas_call(
        body, name=name, grid=(nb,),
        in_specs=[lat, ext, ext, ext, ext, pl.BlockSpec((R, RW), lambda i: (i, 4)),
                  _const_spec((1, S5W)), _const_spec((1, S5W)), _const_spec((6, D)), _const_spec((S5W, S5W)), _const_spec((D, D)), lat]
        + cg.in_specs(),
        out_specs=[ext, ext, ext, ext, lat, lat, lat5, lat5, _acc_spec((1, S5W)), _acc_spec((1, S5W)), _acc_spec((1, D))]
        + cg.in_specs(),
        out_shape=[jax.ShapeDtypeStruct((L + R, S5W), bf16), eshape, eshape, eshape, jax.ShapeDtypeStruct((L, D), bf16),
                   jax.ShapeDtypeStruct((L, D), bf16), jax.ShapeDtypeStruct((L, S5W), bf16), jax.ShapeDtypeStruct((L, S5W), bf16),
                   jax.ShapeDtypeStruct((1, S5W), f32), jax.ShapeDtypeStruct((1, S5W), f32), jax.ShapeDtypeStruct((1, D), f32)]
        + cg.out_shapes(),
        scratch_shapes=cg.sems(),
        compiler_params=_CP(dimension_semantics=_ARB),
    )(x, ys, of, ob, p_ext, p_ext, dvec, bglu, modx, wglu, wout, dx1, *cg.arrays)


def _ffn_tail(gc, a, x1, gate2, fnw, pf, wdown, wdown_t, tgt):
    f = _gelu(gc) * a
    ffn = _dnn_const(f, wdown, wdown_t) + pf
    y = _rms(x1 + gate2 * ffn, fnw)
    err = y - tgt
    loss = 0.5 * jnp.sum(jnp.mean(err * err, axis=-1, keepdims=True), axis=0, keepdims=True)
    return loss, f


def _ffn_fwd(x1, tgt, nw2, modx, w_a, w_g, cw, cb, wdown, wdown_t, fnw, name):
    L = x1.shape[0]
    nb = L // RF
    per = RF // HALO

    def body(x_ref, xp_ref, xn_ref, t_ref, nw_ref, mx_ref, wa_ref, wg_ref, cw_ref, cb_ref, wd_ref, wdt_ref, fn_ref,
             dx2_ref, da_ref, dgc_ref, f_ref, dffn_ref, loss_ref, dfn_ref, dg2_ref, dcb_ref, dcw_ref):
        i = pl.program_id(0)

        @pl.when(i == 0)
        def _():
            for r in (loss_ref, dfn_ref, dg2_ref, dcb_ref, dcw_ref):
                r[...] = jnp.zeros_like(r)

        nw, sh, sc, gate2 = nw_ref[...], mx_ref[3:4], mx_ref[4:5], mx_ref[5:6]
        x1b = x_ref[...]
        h2 = _norm_mod(x1b, nw, sh, sc)
        h2e = jnp.concatenate([_norm_mod(xp_ref[...], nw, sh, sc), h2, _norm_mod(xn_ref[...], nw, sh, sc)], axis=0)
        a = dnn(h2, wa_ref[...])
        ge = dnn(h2e, wg_ref[...])
        g = ge[HALO:HALO + RF]
        gp = ge[HALO - 1:HALO] * jnp.where(i > 0, 1.0, 0.0)
        gn = ge[HALO + RF:HALO + RF + 1] * jnp.where(i < nb - 1, 1.0, 0.0)
        row = lax.broadcasted_iota(jnp.int32, (RF, 1), 0)
        g_prev = jnp.where(row == 0, gp, pltpu.roll(g, 1, axis=0))
        g_next = jnp.where(row == RF - 1, gn, pltpu.roll(g, RF - 1, axis=0))
        gc = cb_ref[...] + g_prev * cw_ref[0:1] + g * cw_ref[1:2] + g_next * cw_ref[2:3]
        fn = lambda gc_, a_, x_, g2_, fw_, pf_: _ffn_tail(gc_, a_, x_, g2_, fw_, pf_, wd_ref[...], wdt_ref[...], t_ref[...])
        loss, vjp, f = jax.vjp(fn, gc, a, x1b, gate2, fn_ref[...], jnp.zeros((RF, D), f32), has_aux=True)
        dgc, da, dx2, dg2, dfw, dffn = vjp(jnp.ones((1, 1), f32))
        dx2_ref[...] = dx2
        da_ref[...], dgc_ref[...] = da.astype(bf16), dgc
        f_ref[...], dffn_ref[...] = f.astype(bf16), dffn.astype(bf16)
        loss_ref[...] += jnp.broadcast_to(loss, (1, 128))
        dfn_ref[...] += dfw
        dg2_ref[...] += dg2
        dcb_ref[...] += jnp.sum(dgc, axis=0, keepdims=True)
        dcw_ref[0:1] += jnp.sum(dgc * g_prev, axis=0, keepdims=True)
        dcw_ref[1:2] += jnp.sum(dgc * g, axis=0, keepdims=True)
        dcw_ref[2:3] += jnp.sum(dgc * g_next, axis=0, keepdims=True)

    blk = lambda w: pl.BlockSpec((RF, w), lambda i: (i, 0))
    return pl.pallas_call(
        body, name=name, grid=(nb,),
        in_specs=[blk(D), pl.BlockSpec((HALO, D), lambda i: (jnp.maximum(i * per - 1, 0), 0)),
                  pl.BlockSpec((HALO, D), lambda i: (jnp.minimum((i + 1) * per, L // HALO - 1), 0)), blk(D),
                  _const_spec((1, D)), _const_spec((6, D)), _const_spec((D, DFF)), _const_spec((D, DFF)), _const_spec((3, DFF)),
                  _const_spec((1, DFF)), _const_spec((DFF, D)), _const_spec((D, DFF)), _const_spec((1, D))],
        out_specs=[blk(D), blk(DFF), blk(DFF), blk(DFF), blk(D), _acc_spec((1, 128)), _acc_spec((1, D)), _acc_spec((1, D)),
                   _acc_spec((1, DFF)), _acc_spec((3, DFF))],
        out_shape=[jax.ShapeDtypeStruct((L, D), f32), jax.ShapeDtypeStruct((L, DFF), bf16), jax.ShapeDtypeStruct((L, DFF), f32),
                   jax.ShapeDtypeStruct((L, DFF), bf16), jax.ShapeDtypeStruct((L, D), bf16), jax.ShapeDtypeStruct((1, 128), f32),
                   jax.ShapeDtypeStruct((1, D), f32), jax.ShapeDtypeStruct((1, D), f32), jax.ShapeDtypeStruct((1, DFF), f32),
                   jax.ShapeDtypeStruct((3, DFF), f32)],
        compiler_params=_CP(dimension_semantics=_ARB),
    )(x1, x1, x1, tgt, nw2, modx, w_a, w_g, cw, cb, wdown, wdown_t, fnw)


def _ffn_bwd(x1, dx2, da, dgc, nw2, modx, wup_t, cw, name):
    L = x1.shape[0]
    rf = RFB
    nb = L // rf
    per = rf // HALO

    def body(x_ref, dx2_ref, da_ref, dgc_ref, dgp_ref, dgn_ref, nw_ref, mx_ref, wu_ref, cw_ref,
             dx1_ref, dag_ref, h2_ref, dnw_ref, dmx_ref):
        i = pl.program_id(0)

        @pl.when(i == 0)
        def _():
            dnw_ref[...] = jnp.zeros_like(dnw_ref)
            dmx_ref[...] = jnp.zeros_like(dmx_ref)

        dgc_b = dgc_ref[...]
        before = dgp_ref[HALO - 1:HALO] * jnp.where(i > 0, 1.0, 0.0)
        after = dgn_ref[0:1] * jnp.where(i < nb - 1, 1.0, 0.0)
        row = lax.broadcasted_iota(jnp.int32, (rf, 1), 0)
        d_prev = jnp.where(row == 0, before, pltpu.roll(dgc_b, 1, axis=0))
        d_next = jnp.where(row == rf - 1, after, pltpu.roll(dgc_b, rf - 1, axis=0))
        dg = cw_ref[0:1] * d_next + cw_ref[1:2] * dgc_b + cw_ref[2:3] * d_prev
        dag = jnp.concatenate([da_ref[...], dg.astype(bf16)], axis=1)
        dag_ref[...] = dag
        dh2 = dnn(dag, wu_ref[...])
        h2, vjp = jax.vjp(_norm_mod, x_ref[...], nw_ref[...], mx_ref[3:4], mx_ref[4:5])
        dxa, dnw, dsh, dsc = vjp(dh2)
        h2_ref[...] = h2.astype(bf16)
        dx1_ref[...] = dx2_ref[...] + dxa
        dnw_ref[...] += dnw
        dmx_ref[3:4] += dsh
        dmx_ref[4:5] += dsc

    blk = lambda w: pl.BlockSpec((rf, w), lambda i: (i, 0))
    return pl.pallas_call(
        body, name=name, grid=(nb,),
        in_specs=[blk(D), blk(D), blk(DFF), blk(DFF), pl.BlockSpec((HALO, DFF), lambda i: (jnp.maximum(i * per - 1, 0), 0)),
                  pl.BlockSpec((HALO, DFF), lambda i: (jnp.minimum((i + 1) * per, L // HALO - 1), 0)),
                  _const_spec((1, D)), _const_spec((6, D)), _const_spec((2 * DFF, D)), _const_spec((3, DFF))],
        out_specs=[blk(D), blk(2 * DFF), blk(D), _acc_spec((1, D)), _acc_spec((6, D))],
        out_shape=[jax.ShapeDtypeStruct((L, D), f32), jax.ShapeDtypeStruct((L, 2 * DFF), bf16), jax.ShapeDtypeStruct((L, D), bf16),
                   jax.ShapeDtypeStruct((1, D), f32), jax.ShapeDtypeStruct((6, D), f32)],
        compiler_params=_CP(dimension_semantics=_ARB),
    )(x1, dx2, da, dgc, dgc, dgc, nw2, modx, wup_t, cw)


def _matmul_tn(a, b, name, cargo=None):
    k, m = a.shape
    n = b.shape[1]
    divs = lambda d: [c for c in range(d, 0, -128) if d % c == 0]
    _, tm, tn = min((m * (n // cn) + n * (m // cm), cm, cn) for cm in divs(m) for cn in divs(n) if cm * cn * 4 <= ACC_TILE_BYTES)
    tk = next(c for c in (1024, 768, 512, 256, 128) if k % c == 0)
    nk = k // tk
    grid = (m // tm, n // tn, nk)
    cg = _Cargo(cargo)

    def body(*refs):
        (a_ref, b_ref), (o_ref,), (acc,) = cg.split(refs, 2, 1, 1)
        cg.ride(refs, 2, 1, grid)
        q = pl.program_id(2)

        @pl.when(q == 0)
        def _():
            acc[...] = jnp.zeros_like(acc)

        acc[...] += dtn(a_ref[...], b_ref[...])

        @pl.when(q == nk - 1)
        def _():
            o_ref[...] = acc[...].astype(bf16)

    out = pl.pallas_call(
        body, name=name, grid=grid,
        in_specs=[pl.BlockSpec((tk, tm), lambda i, j, q: (q, i)), pl.BlockSpec((tk, tn), lambda i, j, q: (q, j))] + cg.in_specs(),
        out_specs=[pl.BlockSpec((tm, tn), lambda i, j, q: (i, j))] + cg.in_specs(),
        out_shape=[jax.ShapeDtypeStruct((m, n), bf16)] + cg.out_shapes(),
        scratch_shapes=[pltpu.VMEM((tm, tn), f32)] + cg.sems(),
        compiler_params=_CP(dimension_semantics=("arbitrary",) * 3 if cg.n else ("parallel", "parallel", "arbitrary")),
    )(a, b, *cg.arrays)
    return out if cg.n else out[0]


def _adamw_refs(w_ref, g_ref, m_ref, v_ref, d_ref, nm_ref, nv_ref):
    c1, c2 = 1.0 - B1 ** STEP, 1.0 - B2 ** STEP
    gg = g_ref[...]
    nm = B1 * m_ref[...] + (1.0 - B1) * gg
    nv = B2 * v_ref[...] + (1.0 - B2) * jnp.square(gg)
    d_ref[...] = -LR * ((nm / c1) / (jnp.sqrt(nv / c2) + AEPS) + WD * w_ref[...])
    nm_ref[...], nv_ref[...] = nm, nv


def _adamw(w, g, m, v, name):
    def body(*refs):
        _adamw_refs(*refs)

    return pl.pallas_call(body, name=name, out_shape=[jax.ShapeDtypeStruct(w.shape, f32)] * 3, compiler_params=_CP())(w, g, m, v)


def _adamw_landed(land, w, m, v, name):
    def body(l_ref, w_ref, m_ref, v_ref, g_ref, d_ref, nm_ref, nv_ref):
        acc = l_ref[0].astype(f32)
        for j in range(1, NDEV):
            acc = acc + l_ref[j].astype(f32)
        g_ref[...] = acc
        _adamw_refs(w_ref, g_ref, m_ref, v_ref, d_ref, nm_ref, nv_ref)

    return pl.pallas_call(body, name=name, out_shape=[jax.ShapeDtypeStruct(w.shape, f32)] * 4, compiler_params=_CP())(land, w, m, v)


def _adamw_many(ws, gs, ms, vs, name):
    n = len(ws)

    def body(*refs):
        for k in range(n):
            _adamw_refs(*[refs[j * n + k] for j in range(7)])

    outs = pl.pallas_call(body, name=name, out_shape=[jax.ShapeDtypeStruct(w.shape, f32) for w in ws] * 3,
                          compiler_params=_CP())(*ws, *gs, *ms, *vs)
    return outs[:n], outs[n:2 * n], outs[2 * n:]


SMALL = ["conv_w", "c_ctx", "norm1_w", "s5_lambda_re_f", "s5_lambda_im_f", "s5_log_step_f", "s5_lambda_re_b", "s5_lambda_im_b",
         "s5_log_step_b", "s5_b_re", "s5_b_im", "s5_c_re", "s5_c_im", "s5_d", "s5_b_glu", "ret_log_decay_f", "ret_log_decay_b",
         "norm2_w", "conv_b", "final_norm_w"]
WEIGHTS = ["c_ctx", "w_mod", "b_mod", "norm1_w", "w_in", "s5_lambda_re_f", "s5_lambda_im_f", "s5_log_step_f", "s5_lambda_re_b",
           "s5_lambda_im_b", "s5_log_step_b", "s5_b_re", "s5_b_im", "s5_c_re", "s5_c_im", "s5_d", "s5_w_glu", "s5_b_glu",
           "ret_log_decay_f", "ret_log_decay_b", "w_out", "norm2_w", "w_up", "conv_w", "conv_b", "w_down", "final_norm_w"]


def _pack_small(vals):
    flat, offs, o = [], [], 0
    for a in vals:
        n = a.size
        npad = -n % 128
        flat.append(jnp.pad(a.reshape(-1), (0, npad)))
        offs.append((o, n))
        o += n + npad
    tail = -o % 1024
    if tail:
        flat.append(jnp.zeros((tail,), f32))
    return jnp.concatenate(flat).reshape(-1, 128), offs


def _unpack_small(packed, offs, shapes):
    flat = packed.reshape(-1)
    return [flat[o:o + n].reshape(s) for (o, n), s in zip(offs, shapes)]


def _rope_tables(L, nctx_rows):
    t = np.arange(L)
    inv = (ROPE_THETA ** (-np.arange(DH // 4, dtype=np.float64) / (DH // 4))).astype(np.float32)
    ang = np.concatenate([(t // GRID_W).astype(np.float32)[:, None] * inv, (t % GRID_W).astype(np.float32)[:, None] * inv], axis=-1)
    cos = np.repeat(np.cos(ang).astype(np.float32), 2, axis=1)
    sin = np.repeat(np.sin(ang).astype(np.float32), 2, axis=1) * np.tile(np.array([-1.0, 1.0], np.float32), DH // 2)
    cosf = np.concatenate([np.ones((nctx_rows, DH), np.float32), cos], axis=0)
    sins = np.concatenate([np.zeros((nctx_rows, DH), np.float32), sin], axis=0)
    return jnp.asarray(cosf), jnp.asarray(sins)


def kernel(x, c, ctx, c_ctx, w_mod, b_mod, norm1_w, w_in, s5_lambda_re_f, s5_lambda_im_f, s5_log_step_f, s5_lambda_re_b, s5_lambda_im_b, s5_log_step_b, s5_b_re, s5_b_im, s5_c_re, s5_c_im, s5_d, s5_w_glu, s5_b_glu, ret_log_decay_f, ret_log_decay_b, w_out, norm2_w, w_up, conv_w, conv_b, w_down, final_norm_w, loss_target, m_c_ctx, m_w_mod, m_b_mod, m_norm1_w, m_w_in, m_s5_lambda_re_f, m_s5_lambda_im_f, m_s5_log_step_f, m_s5_lambda_re_b, m_s5_lambda_im_b, m_s5_log_step_b, m_s5_b_re, m_s5_b_im, m_s5_c_re, m_s5_c_im, m_s5_d, m_s5_w_glu, m_s5_b_glu, m_ret_log_decay_f, m_ret_log_decay_b, m_w_out, m_norm2_w, m_w_up, m_conv_w, m_conv_b, m_w_down, m_final_norm_w, v_c_ctx, v_w_mod, v_b_mod, v_norm1_w, v_w_in, v_s5_lambda_re_f, v_s5_lambda_im_f, v_s5_log_step_f, v_s5_lambda_re_b, v_s5_lambda_im_b, v_s5_log_step_b, v_s5_b_re, v_s5_b_im, v_s5_c_re, v_s5_c_im, v_s5_d, v_s5_w_glu, v_s5_b_glu, v_ret_log_decay_f, v_ret_log_decay_b, v_w_out, v_norm2_w, v_w_up, v_conv_w, v_conv_b, v_w_down, v_final_norm_w):
    args = dict(locals())
    W = {n: args[n] for n in WEIGHTS}
    M = {n: args["m_" + n] for n in WEIGHTS}
    V = {n: args["v_" + n] for n in WEIGHTS}
    me = _me()
    x2, ctx2, tgt = x[0], ctx[0], loss_target[0]
    L, Lc = x2.shape[0], ctx2.shape[0]
    assert Lc == R and L % R == 0 and L % GRID_W == 0
    nctx = Lc // T

    w_in_tl, w_up_tl = w_in[0].T.astype(bf16), w_up[0].T.astype(bf16)
    w_out_l, w_down_l, w_glu_l = w_out[0].astype(bf16), w_down[0].astype(bf16), s5_w_glu[0].astype(bf16)
    per_cv = conv_w.shape[2]
    conv_pad = jnp.pad(conv_w[0], ((0, 5), (0, 128 * 3 - per_cv)))
    w_in_g, c_g, conv_g = _gather_two_level([w_in_tl, jnp.pad(c, ((0, 7), (0, 0))), conv_pad], "gather_w_in")
    w_in_t = w_in_g.reshape(INC, D)
    conv_f = conv_g[:, :3, :per_cv].transpose(1, 0, 2).reshape(3, DFF)

    c9 = jnp.concatenate([c_g[:, 0, :], c_ctx[None], jnp.zeros((7, D), f32)], axis=0)
    w_mod_l = w_mod[0]
    ncol = w_mod_l.shape[1]
    m_part = _ada_fwd(c9, w_mod_l, "ada_fwd")
    m_all = _all_gather_small(m_part, "gather_mod").transpose(1, 0, 2).reshape(16, 6, D)
    modx, modc = _mod_select(m_all, b_mod.reshape(6, D), "mod_select")

    pair = lambda a, b: jnp.concatenate([a, b], axis=-1)
    bre_g, bim_g = s5_b_re[0].transpose(0, 2, 1), s5_b_im[0].transpose(0, 2, 1)
    cre_g, cim_g = s5_c_re[0], s5_c_im[0]
    shared = (pair(bre_g, bim_g), pair(bim_g, bre_g), pair(cre_g, cim_g), pair(cim_g, cre_g))
    s5p = {}
    for tag, lre, lim, ls in (("f", s5_lambda_re_f, s5_lambda_im_f, s5_log_step_f), ("b", s5_lambda_re_b, s5_lambda_im_b, s5_log_step_b)):
        s5p[tag] = (pair(lre[0], lre[0])[:, None, :], pair(lim[0], lim[0])[:, None, :], ls[0].reshape(S5G, 1, 1)) + shared
    m_f, mb_f, mc_f, a1_f, a2_f = _s5_build(s5p["f"], False, "s5_build_f")
    m_b, mb_b, mc_b, a1_b, a2_b = _s5_build(s5p["b"], True, "s5_build_b")
    a1_f, a2_f, a1_b, a2_b = (a.reshape(S5G, SB) for a in (a1_f, a2_f, a1_b, a2_b))

    nw1, nw2, fnw = norm1_w, norm2_w, final_norm_w[None]
    cosf, sins = _rope_tables(L, Lc)
    p_ext, w_out_g, w_glu_g, w_up_g1 = _f1_fwd(x2, ctx2, modx, modc, nw1, w_in_t.T, cosf, sins, "f1_fwd",
                                               cargo=([w_out_l, w_glu_l, w_up_tl[:UP_HEAD]], False))
    nctx5 = Lc // TC
    u_g = _to_groups(p_ext[:, :S5W])
    s_f, s_b = _s5_inc(u_g, mb_f, mb_b, "s5_inc")
    hp_f, hp_b = _s5_carry(s_f, s_b, (a1_f, a2_f), (a1_b, a2_b), nctx5, "s5_carry")
    ys = _from_groups(_s5_out(u_g, m_f, m_b, hp_f, hp_b, mc_f, mc_b, "s5_out"))
    ld8 = lambda ld: jnp.pad(jnp.broadcast_to(ld[0][:, None], (RH, 128)), ((0, 8 - RH), (0, 0)))
    ldf8, ldb8 = ld8(ret_log_decay_f), ld8(ret_log_decay_b)
    of, ob, rp_f, rp_b, w_up_g2 = _ret_fwd(p_ext, ldf8, ldb8, nctx, "ret_fwd", cargo=([w_up_tl[UP_HEAD:]], False))
    w_out_f, w_glu_f = w_out_g.reshape(D, D), w_glu_g.reshape(S5W, S5W)
    x1, w_down_g = _mix_fwd(x2, ys, of, ob, p_ext, s5_d, s5_b_glu, modx, w_glu_f, w_out_f, "mix_fwd", cargo=([w_down_l], False))
    w_down_f = w_down_g.reshape(DFF, D)
    w_up_t = jnp.concatenate([w_up_g1, w_up_g2], axis=1).reshape(2 * DFF, D)

    (dx2, da, dgc, f_act, dffn, loss_acc, g_fnw, g_gate2, g_cb, g_cw) = _ffn_fwd(
        x1, tgt, nw2, modx, w_up_t[:DFF].T, w_up_t[DFF:].T, conv_f, conv_b, w_down_f, w_down_f.T, fnw, "ffn_fwd")
    dx1, dag, h2, g_nw2, dmx2 = _ffn_bwd(x1, dx2, da, dgc, nw2, modx, w_up_t, conv_f, "ffn_bwd")
    gw_down = _matmul_tn(f_act, dffn, "dw_down").reshape(NDEV, -1, D)
    gw_up_t = _matmul_tn(dag, h2, "dw_up").reshape(NDEV, -1, D)
    (dy_e, dud_e, do_e, dg_e, cat, dmix, s_act, dz, g_d, g_bglu, g_gate1, l_down) = _mix_bwd(
        x2, ys, of, ob, p_ext, s5_d, s5_b_glu, modx, w_glu_f, w_out_f, dx1, "mix_bwd", cargo=([gw_down], True))
    gw_out = _matmul_tn(cat, dmix, "dw_out").reshape(NDEV, -1, D)
    gw_glu = _matmul_tn(s_act, dz, "dw_glu").reshape(NDEV, -1, S5W)
    dq_f, dk_f, dv_f, dq_b, dk_b, dv_b, gld_f, gld_b, l_up, l_out, l_glu = _ret_bwd(
        p_ext, ldf8, ldb8, rp_f, rp_b, do_e, nctx, "ret_bwd", cargo=([gw_up_t, gw_out, gw_glu], True))

    du1, g_m, dhp_f, dhp_b, dmc_f, dmc_b = _s5_out_bwd(_to_groups(dy_e), u_g, m_f, m_b, hp_f, hp_b, mc_f, mc_b, "s5_out_bwd")
    ds_f, da1_f, da2_f = _s5_carry_bwd(dhp_f, hp_f, a1_f, a2_f, False, nctx5, "s5_carry_bwd_f")
    ds_b, da1_b, da2_b = _s5_carry_bwd(dhp_b, hp_b, a1_b, a2_b, True, nctx5, "s5_carry_bwd_b")
    du_g, dmb_f, dmb_b = _s5_inc_bwd(du1, u_g, ds_f, ds_b, mb_f, mb_b, "s5_inc_bwd")
    zero_p = jnp.zeros((S5G, S5P, SB), f32)
    gf = _s5_build_bwd(s5p["f"], (g_m, dmb_f, dmc_f, da1_f[:, None, :], da2_f[:, None, :]), (zero_p, zero_p), False, "s5_build_bwd_f")
    gb = _s5_build_bwd(s5p["b"], (g_m, dmb_b, dmc_b, da1_b[:, None, :], da2_b[:, None, :]), (gf[3], gf[4]), True, "s5_build_bwd_b")
    g_bre, g_bim = gb[3][:, :, :S5N].transpose(0, 2, 1), gb[3][:, :, S5N:].transpose(0, 2, 1)
    g_cre, g_cim = gb[4][:, :, :S5N], gb[4][:, :, S5N:]

    early = {
        "conv_w": g_cw, "s5_lambda_re_f": gf[0][:, 0, :S5N], "s5_lambda_im_f": gf[1][:, 0, :S5N],
        "s5_log_step_f": gf[2], "s5_lambda_re_b": gb[0][:, 0, :S5N], "s5_lambda_im_b": gb[1][:, 0, :S5N], "s5_log_step_b": gb[2],
        "s5_b_re": g_bre, "s5_b_im": g_bim, "s5_c_re": g_cre, "s5_c_im": g_cim, "s5_d": g_d, "s5_b_glu": g_bglu,
        "ret_log_decay_f": gld_f[:RH, 0], "ret_log_decay_b": gld_b[:RH, 0], "norm2_w": g_nw2, "conv_b": g_cb, "final_norm_w": g_fnw,
    }
    e_names = [n for n in SMALL if n in early]
    packed_e, eoffs = _pack_small([early[n].astype(f32) for n in e_names])
    grad_x, dp_ext, h1, g_nw1, dmx1, dmc1 = _f1_bwd(
        x2, ctx2, modx, modc, nw1, w_in_t, cosf, sins, dx1, (_from_groups(du_g), dud_e, dq_f, dq_b, dk_f, dk_b, dv_f, dv_b, dg_e), "f1_bwd")
    gw_in_t, land_e = _matmul_tn(dp_ext, h1, "dw_in", cargo=([packed_e], False))
    g_in_t = _reduce_scatter_two_level(gw_in_t.reshape(NDEV, -1, D), "scatter_dw_in")

    dmx = dmx1 + dmx2
    dmx = dmx.at[2].set(g_gate1[0]).at[5].set(g_gate2[0])
    dm_me = jnp.stack([dmx.reshape(-1), dmc1.reshape(-1)], axis=0)
    dm_all = _all_gather_small(dm_me.reshape(8, -1), "gather_dmod").reshape(NDEV, 2, 6 * D)
    dmx_all, dmc_all = dm_all[:, 0, :], dm_all[:, 1, :]
    my_cols = lambda a: lax.dynamic_slice(a, (0, me * ncol), (NDEV, ncol))
    gw_mod, g_bmod, dc9 = _ada_bwd(c9, dmx_all, dmc_all, my_cols(dmx_all), my_cols(dmc_all), w_mod_l, "ada_bwd")

    sshape = lambda n: (3, DFF) if n == "conv_w" else W[n].shape
    G = dict(zip(e_names, _unpack_small(_sum8(land_e, "reduce_early"), eoffs, [sshape(n) for n in e_names])))
    late = {"c_ctx": dc9[8], "norm1_w": g_nw1}
    packed_l, loffs = _pack_small([late[n].astype(f32) for n in late])
    G.update(zip(late, _unpack_small(_all_reduce_small(packed_l, "reduce_late"), loffs, [W[n].shape for n in late])))
    G["conv_w"] = lax.dynamic_slice(G["conv_w"], (0, me * per_cv), (3, per_cv))[None]
    G["b_mod"] = g_bmod.reshape(b_mod.shape)
    G["w_mod"] = gw_mod[None]
    G["w_in"] = g_in_t.T[None]
    G["w_up"] = _sum8(l_up, "sum_dw_up").T[None]

    delta, new_m, new_v = {}, {}, {}
    sm_names = SMALL[1:] + ["b_mod"]
    rows = lambda a: a.reshape(-1, a.shape[-1])
    outs = _adamw_many(*[[rows(d[n]) for n in sm_names] for d in (W, G, M, V)], "adamw_small")
    for dst, src in zip((delta, new_m, new_v), outs):
        dst.update({n: a.reshape(W[n].shape) for n, a in zip(sm_names, src)})
    for n in ["w_mod", "w_in", "w_up", "conv_w"]:
        d, nm, nv = _adamw(W[n][0], G[n][0], M[n][0], V[n][0], "adamw_" + n)
        delta[n], new_m[n], new_v[n] = d[None], nm[None], nv[None]
    for n, land in (("w_out", l_out), ("w_down", l_down), ("s5_w_glu", l_glu)):
        g, d, nm, nv = _adamw_landed(land, W[n][0], M[n][0], V[n][0], "adamw_" + n)
        G[n], delta[n], new_m[n], new_v[n] = g[None], d[None], nm[None], nv[None]

    loss = lax.psum(loss_acc[0, 0], ("x", "y", "c"))
    return (loss, grad_x[None], *[G[n] for n in WEIGHTS], *[delta[n] for n in WEIGHTS], *[new_m[n] for n in WEIGHTS],
            *[new_v[n] for n in WEIGHTS])
```
